```python
import jax, jax.numpy as jnp
from jax import lax
import numpy as np

D_MODEL = 2048
BATCH = 8
SEQ = 2048
DEPTH = 2

N_EVEN = (DEPTH + 1) // 2
N_ODD = DEPTH // 2
MLA_HEADS = 8
Q_LORA = 512
KV_LORA = 512
QK_NOPE = 128
QK_ROPE = 64
V_HEAD = 128
ROPE_BASE = 10000.0
Q_BLOCK = 128
SGU_GROUPS = 8
SGU_CH = 128
CHUNK = 128
CONV_DIM = D_MODEL
CONV_WIDTH = 3
D_FF = 4 * D_MODEL
EPS = 1e-6

MLA_OUT = MLA_HEADS * V_HEAD
SGU_OUT = SGU_GROUPS * SGU_CH
MIX_WIDTH = MLA_OUT + SGU_OUT
EVEN_IN = Q_LORA + KV_LORA + QK_ROPE + 2 * SGU_OUT

kernel_name = "hybrid_mla_sgu_shortconv_block"


def rms_norm(x, g):
    xf = x.astype(jnp.float32)
    y = xf * lax.rsqrt(jnp.mean(xf * xf, axis=-1, keepdims=True) + EPS)
    return (y * g.astype(jnp.float32)).astype(x.dtype)


def group_layer_norm(v, g):
    vf = v.astype(jnp.float32)
    mu = jnp.mean(vf, axis=-1, keepdims=True)
    var = jnp.mean(jnp.square(vf - mu), axis=-1, keepdims=True)
    return ((vf - mu) * lax.rsqrt(var + EPS) * g.astype(jnp.float32)).astype(v.dtype)


def rope_tables(positions):
    inv_freq = ROPE_BASE ** (-jnp.arange(0, QK_ROPE, 2, dtype=jnp.float32) / QK_ROPE)
    ang = positions.astype(jnp.float32)[..., None] * inv_freq
    return jnp.cos(ang), jnp.sin(ang)


def apply_rope(x, cos, sin):
    xf = x.astype(jnp.float32)
    x1, x2 = xf[..., : QK_ROPE // 2], xf[..., QK_ROPE // 2:]
    return jnp.concatenate([x1 * cos - x2 * sin, x2 * cos + x1 * sin], axis=-1).astype(x.dtype)


def mla_mixer(c_q, c_kv, k_rope_raw, cos, sin, q_norm, w_uq, kv_norm, w_ukv):
    B, S, _ = c_q.shape
    q = (rms_norm(c_q, q_norm) @ w_uq).reshape(B, S, MLA_HEADS, QK_NOPE + QK_ROPE)
    q_nope = q[..., :QK_NOPE]
    q_rope = apply_rope(q[..., QK_NOPE:], cos[:, :, None], sin[:, :, None])
    kv = (rms_norm(c_kv, kv_norm) @ w_ukv).reshape(B, S, MLA_HEADS, QK_NOPE + V_HEAD)
    k_nope, v = kv[..., :QK_NOPE], kv[..., QK_NOPE:]
    k_rope = apply_rope(k_rope_raw, cos, sin)
    scale = (QK_NOPE + QK_ROPE) ** -0.5
    outs = []
    for i in range(S // Q_BLOCK):
        q0, kend = i * Q_BLOCK, (i + 1) * Q_BLOCK
        s = (jnp.einsum('bqhd,bkhd->bhqk', q_nope[:, q0:kend], k_nope[:, :kend])
             + jnp.einsum('bqhr,bkr->bhqk', q_rope[:, q0:kend], k_rope[:, :kend]))
        s = s.astype(jnp.float32) * scale
        q_idx = q0 + jnp.arange(Q_BLOCK)
        mask = jnp.arange(kend)[None, :] <= q_idx[:, None]
        p = jax.nn.softmax(jnp.where(mask, s, -jnp.inf), axis=-1).astype(v.dtype)
        outs.append(jnp.einsum('bhqk,bkhd->bqhd', p, v[:, :kend]))
    o = jnp.concatenate(outs, axis=1)
    return o.reshape(B, S, MLA_OUT)


def sgu_mixer(uv, v_norm, w_s, b_s):
    B, S, _ = uv.shape
    uv = jax.nn.gelu(uv)
    u, v = uv[..., :SGU_OUT], uv[..., SGU_OUT:]
    v = group_layer_norm(v.reshape(B, S, SGU_GROUPS, SGU_CH), v_norm)
    v = v.reshape(B, S // CHUNK, CHUNK, SGU_GROUPS, SGU_CH)
    w = jnp.tril(w_s)
    y = jnp.einsum('gts,bcsgd->bctgd', w, v) + b_s.T[None, None, :, :, None]
    return u * y.reshape(B, S, SGU_OUT)


def short_conv_mixer(h, w_in, conv_w, w_out):
    proj = h @ w_in
    b_gate = proj[..., :CONV_DIM]
    c_gate = proj[..., CONV_DIM:2 * CONV_DIM]
    xin = proj[..., 2 * CONV_DIM:]
    z = c_gate * xin
    z = lax.conv_general_dilated(
        z, conv_w[:, None, :].astype(z.dtype), window_strides=(1,),
        padding=[(CONV_WIDTH - 1, 0)], dimension_numbers=('NWC', 'WIO', 'NWC'),
        feature_group_count=CONV_DIM)
    return (b_gate * z) @ w_out


def sqrelu_mlp(h, w1, w2):
    a = jax.nn.relu(h @ w1)
    return (a * a) @ w2


def _fwd_setup_inputs(seed: int = 0) -> dict:
    key = jax.random.key(seed)
    ks = jax.random.split(key, 24)

    def nrm(k, shape, scale):
        return jax.random.normal(k, shape, jnp.float32) * scale

    def gain(k, shape):
        return 1.0 + 0.02 * jax.random.normal(k, shape, jnp.float32)

    x = jax.random.normal(ks[0], (BATCH, SEQ, D_MODEL), jnp.float32)
    offset = jax.random.randint(ks[1], (BATCH, 1), 0, 4096, dtype=jnp.int32)
    positions = offset + jnp.arange(SEQ, dtype=jnp.int32)[None, :]
    return {
        "x": x,
        "positions": positions,
        "e_norm_mix": gain(ks[2], (N_EVEN, D_MODEL)),
        "e_w_in": nrm(ks[3], (N_EVEN, D_MODEL, EVEN_IN), D_MODEL ** -0.5),
        "e_q_norm": gain(ks[4], (N_EVEN, Q_LORA)),
        "e_w_uq": nrm(ks[5], (N_EVEN, Q_LORA, MLA_HEADS * (QK_NOPE + QK_ROPE)), Q_LORA ** -0.5),
        "e_kv_norm": gain(ks[6], (N_EVEN, KV_LORA)),
        "e_w_ukv": nrm(ks[7], (N_EVEN, KV_LORA, MLA_HEADS * (QK_NOPE + V_HEAD)), KV_LORA ** -0.5),
        "e_v_norm": gain(ks[8], (N_EVEN, SGU_GROUPS, SGU_CH)),
        "e_sgu_w": nrm(ks[9], (N_EVEN, SGU_GROUPS, CHUNK, CHUNK), CHUNK ** -0.5),
        "e_sgu_b": 1.0 + nrm(ks[10], (N_EVEN, SGU_GROUPS, CHUNK), 0.1),
        "e_mla_out_norm": gain(ks[11], (N_EVEN, MLA_OUT)),
        "e_sgu_out_norm": gain(ks[12], (N_EVEN, SGU_OUT)),
        "e_w_out": nrm(ks[13], (N_EVEN, MIX_WIDTH, D_MODEL), MIX_WIDTH ** -0.5),
        "o_norm_mix": gain(ks[14], (N_ODD, D_MODEL)),
        "o_w_in": nrm(ks[15], (N_ODD, D_MODEL, 3 * CONV_DIM), D_MODEL ** -0.5),
        "o_conv_w": nrm(ks[16], (N_ODD, CONV_WIDTH, CONV_DIM), CONV_WIDTH ** -0.5),
        "o_w_out": nrm(ks[17], (N_ODD, CONV_DIM, D_MODEL), CONV_DIM ** -0.5),
        "mlp_norm": gain(ks[18], (DEPTH, D_MODEL)),
        "mlp_w1": nrm(ks[19], (DEPTH, D_MODEL, D_FF), D_MODEL ** -0.5),
        "mlp_w2": nrm(ks[20], (DEPTH, D_FF, D_MODEL), 0.5 * D_FF ** -0.5),
        "final_norm": gain(ks[21], (D_MODEL,)),
    }


def _fwd_reference(x, positions, e_norm_mix, e_w_in, e_q_norm, e_w_uq, e_kv_norm, e_w_ukv,
              e_v_norm, e_sgu_w, e_sgu_b, e_mla_out_norm, e_sgu_out_norm, e_w_out,
              o_norm_mix, o_w_in, o_conv_w, o_w_out, mlp_norm, mlp_w1, mlp_w2, final_norm):
    cos, sin = rope_tables(positions)
    c1 = Q_LORA
    c2 = c1 + KV_LORA
    c3 = c2 + QK_ROPE
    for layer in range(DEPTH):
        i = layer // 2
        if layer % 2 == 0:
            h = rms_norm(x, e_norm_mix[i])
            proj = h @ e_w_in[i]
            a = mla_mixer(proj[..., :c1], proj[..., c1:c2], proj[..., c2:c3], cos, sin,
                          e_q_norm[i], e_w_uq[i], e_kv_norm[i], e_w_ukv[i])
            s = sgu_mixer(proj[..., c3:], e_v_norm[i], e_sgu_w[i], e_sgu_b[i])
            mixed = jnp.concatenate([rms_norm(a, e_mla_out_norm[i]),
                                     rms_norm(s, e_sgu_out_norm[i])], axis=-1)
            x = x + mixed @ e_w_out[i]
        else:
            x = x + short_conv_mixer(rms_norm(x, o_norm_mix[i]), o_w_in[i], o_conv_w[i], o_w_out[i])
        x = x + sqrelu_mlp(rms_norm(x, mlp_norm[layer]), mlp_w1[layer], mlp_w2[layer])
    return rms_norm(x, final_norm)


import jax as _jax
import jax.numpy as _jnp

TWIN_FORMAT = 'train_step'
FWD_PARAMS = ['x', 'positions', 'e_norm_mix', 'e_w_in', 'e_q_norm', 'e_w_uq', 'e_kv_norm', 'e_w_ukv', 'e_v_norm', 'e_sgu_w', 'e_sgu_b', 'e_mla_out_norm', 'e_sgu_out_norm', 'e_w_out', 'o_norm_mix', 'o_w_in', 'o_conv_w', 'o_w_out', 'mlp_norm', 'mlp_w1', 'mlp_w2', 'final_norm']
TWIN_WEIGHTS = ['e_norm_mix', 'e_w_in', 'e_q_norm', 'e_w_uq', 'e_kv_norm', 'e_w_ukv', 'e_v_norm', 'e_sgu_w', 'e_sgu_b', 'e_mla_out_norm', 'e_sgu_out_norm', 'e_w_out', 'o_norm_mix', 'o_w_in', 'o_conv_w', 'o_w_out', 'mlp_norm', 'mlp_w1', 'mlp_w2', 'final_norm']
TWIN_DIFF_INPUT = 'x'
TWIN_INPUTS = ['x', 'positions', 'e_norm_mix', 'e_w_in', 'e_q_norm', 'e_w_uq', 'e_kv_norm', 'e_w_ukv', 'e_v_norm', 'e_sgu_w', 'e_sgu_b', 'e_mla_out_norm', 'e_sgu_out_norm', 'e_w_out', 'o_norm_mix', 'o_w_in', 'o_conv_w', 'o_w_out', 'mlp_norm', 'mlp_w1', 'mlp_w2', 'final_norm', 'loss_target', 'm_e_norm_mix', 'm_e_w_in', 'm_e_q_norm', 'm_e_w_uq', 'm_e_kv_norm', 'm_e_w_ukv', 'm_e_v_norm', 'm_e_sgu_w', 'm_e_sgu_b', 'm_e_mla_out_norm', 'm_e_sgu_out_norm', 'm_e_w_out', 'm_o_norm_mix', 'm_o_w_in', 'm_o_conv_w', 'm_o_w_out', 'm_mlp_norm', 'm_mlp_w1', 'm_mlp_w2', 'm_final_norm', 'v_e_norm_mix', 'v_e_w_in', 'v_e_q_norm', 'v_e_w_uq', 'v_e_kv_norm', 'v_e_w_ukv', 'v_e_v_norm', 'v_e_sgu_w', 'v_e_sgu_b', 'v_e_mla_out_norm', 'v_e_sgu_out_norm', 'v_e_w_out', 'v_o_norm_mix', 'v_o_w_in', 'v_o_conv_w', 'v_o_w_out', 'v_mlp_norm', 'v_mlp_w1', 'v_mlp_w2', 'v_final_norm']
TWIN_OUTPUTS = ['loss', 'grad_x', 'grad_e_norm_mix', 'grad_e_w_in', 'grad_e_q_norm', 'grad_e_w_uq', 'grad_e_kv_norm', 'grad_e_w_ukv', 'grad_e_v_norm', 'grad_e_sgu_w', 'grad_e_sgu_b', 'grad_e_mla_out_norm', 'grad_e_sgu_out_norm', 'grad_e_w_out', 'grad_o_norm_mix', 'grad_o_w_in', 'grad_o_conv_w', 'grad_o_w_out', 'grad_mlp_norm', 'grad_mlp_w1', 'grad_mlp_w2', 'grad_final_norm', 'delta_e_norm_mix', 'delta_e_w_in', 'delta_e_q_norm', 'delta_e_w_uq', 'delta_e_kv_norm', 'delta_e_w_ukv', 'delta_e_v_norm', 'delta_e_sgu_w', 'delta_e_sgu_b', 'delta_e_mla_out_norm', 'delta_e_sgu_out_norm', 'delta_e_w_out', 'delta_o_norm_mix', 'delta_o_w_in', 'delta_o_conv_w', 'delta_o_w_out', 'delta_mlp_norm', 'delta_mlp_w1', 'delta_mlp_w2', 'delta_final_norm', 'new_m_e_norm_mix', 'new_m_e_w_in', 'new_m_e_q_norm', 'new_m_e_w_uq', 'new_m_e_kv_norm', 'new_m_e_w_ukv', 'new_m_e_v_norm', 'new_m_e_sgu_w', 'new_m_e_sgu_b', 'new_m_e_mla_out_norm', 'new_m_e_sgu_out_norm', 'new_m_e_w_out', 'new_m_o_norm_mix', 'new_m_o_w_in', 'new_m_o_conv_w', 'new_m_o_w_out', 'new_m_mlp_norm', 'new_m_mlp_w1', 'new_m_mlp_w2', 'new_m_final_norm', 'new_v_e_norm_mix', 'new_v_e_w_in', 'new_v_e_q_norm', 'new_v_e_w_uq', 'new_v_e_kv_norm', 'new_v_e_w_ukv', 'new_v_e_v_norm', 'new_v_e_sgu_w', 'new_v_e_sgu_b', 'new_v_e_mla_out_norm', 'new_v_e_sgu_out_norm', 'new_v_e_w_out', 'new_v_o_norm_mix', 'new_v_o_w_in', 'new_v_o_conv_w', 'new_v_o_w_out', 'new_v_mlp_norm', 'new_v_mlp_w1', 'new_v_mlp_w2', 'new_v_final_norm']
TWIN_LEAF_KINDS = {'loss': 'loss', 'grad_x': 'grad_x', 'grad_e_norm_mix': 'grad_w', 'grad_e_w_in': 'grad_w', 'grad_e_q_norm': 'grad_w', 'grad_e_w_uq': 'grad_w', 'grad_e_kv_norm': 'grad_w', 'grad_e_w_ukv': 'grad_w', 'grad_e_v_norm': 'grad_w', 'grad_e_sgu_w': 'grad_w', 'grad_e_sgu_b': 'grad_w', 'grad_e_mla_out_norm': 'grad_w', 'grad_e_sgu_out_norm': 'grad_w', 'grad_e_w_out': 'grad_w', 'grad_o_norm_mix': 'grad_w', 'grad_o_w_in': 'grad_w', 'grad_o_conv_w': 'grad_w', 'grad_o_w_out': 'grad_w', 'grad_mlp_norm': 'grad_w', 'grad_mlp_w1': 'grad_w', 'grad_mlp_w2': 'grad_w', 'grad_final_norm': 'grad_w', 'delta_e_norm_mix': 'delta_w', 'delta_e_w_in': 'delta_w', 'delta_e_q_norm': 'delta_w', 'delta_e_w_uq': 'delta_w', 'delta_e_kv_norm': 'delta_w', 'delta_e_w_ukv': 'delta_w', 'delta_e_v_norm': 'delta_w', 'delta_e_sgu_w': 'delta_w', 'delta_e_sgu_b': 'delta_w', 'delta_e_mla_out_norm': 'delta_w', 'delta_e_sgu_out_norm': 'delta_w', 'delta_e_w_out': 'delta_w', 'delta_o_norm_mix': 'delta_w', 'delta_o_w_in': 'delta_w', 'delta_o_conv_w': 'delta_w', 'delta_o_w_out': 'delta_w', 'delta_mlp_norm': 'delta_w', 'delta_mlp_w1': 'delta_w', 'delta_mlp_w2': 'delta_w', 'delta_final_norm': 'delta_w', 'new_m_e_norm_mix': 'new_m', 'new_m_e_w_in': 'new_m', 'new_m_e_q_norm': 'new_m', 'new_m_e_w_uq': 'new_m', 'new_m_e_kv_norm': 'new_m', 'new_m_e_w_ukv': 'new_m', 'new_m_e_v_norm': 'new_m', 'new_m_e_sgu_w': 'new_m', 'new_m_e_sgu_b': 'new_m', 'new_m_e_mla_out_norm': 'new_m', 'new_m_e_sgu_out_norm': 'new_m', 'new_m_e_w_out': 'new_m', 'new_m_o_norm_mix': 'new_m', 'new_m_o_w_in': 'new_m', 'new_m_o_conv_w': 'new_m', 'new_m_o_w_out': 'new_m', 'new_m_mlp_norm': 'new_m', 'new_m_mlp_w1': 'new_m', 'new_m_mlp_w2': 'new_m', 'new_m_final_norm': 'new_m', 'new_v_e_norm_mix': 'new_v', 'new_v_e_w_in': 'new_v', 'new_v_e_q_norm': 'new_v', 'new_v_e_w_uq': 'new_v', 'new_v_e_kv_norm': 'new_v', 'new_v_e_w_ukv': 'new_v', 'new_v_e_v_norm': 'new_v', 'new_v_e_sgu_w': 'new_v', 'new_v_e_sgu_b': 'new_v', 'new_v_e_mla_out_norm': 'new_v', 'new_v_e_sgu_out_norm': 'new_v', 'new_v_e_w_out': 'new_v', 'new_v_o_norm_mix': 'new_v', 'new_v_o_w_in': 'new_v', 'new_v_o_conv_w': 'new_v', 'new_v_o_w_out': 'new_v', 'new_v_mlp_norm': 'new_v', 'new_v_mlp_w1': 'new_v', 'new_v_mlp_w2': 'new_v', 'new_v_final_norm': 'new_v'}


def _forward(args):
    return _fwd_reference(*[args[k] for k in FWD_PARAMS])


def _output_shape():
    out = _jax.eval_shape(lambda: _forward(_fwd_setup_inputs(0)))
    return out.shape, out.dtype

N_MICROBATCH = 1
ADAM_LR = 0.001
ADAM_B1 = 0.9
ADAM_B2 = 0.999
ADAM_EPS = 1e-08
ADAM_WD = 0.01
ADAM_STEP = 10
PER_EXAMPLE_BATCH_AXIS = {'x': 0, 'positions': 0, 'loss_target': 0}
SHARED_INPUTS = []
_WEIGHT_DTYPES = {'e_norm_mix': _jnp.float32, 'e_w_in': _jnp.float32, 'e_q_norm': _jnp.float32, 'e_w_uq': _jnp.float32, 'e_kv_norm': _jnp.float32, 'e_w_ukv': _jnp.float32, 'e_v_norm': _jnp.float32, 'e_sgu_w': _jnp.float32, 'e_sgu_b': _jnp.float32, 'e_mla_out_norm': _jnp.float32, 'e_sgu_out_norm': _jnp.float32, 'e_w_out': _jnp.float32, 'o_norm_mix': _jnp.float32, 'o_w_in': _jnp.float32, 'o_conv_w': _jnp.float32, 'o_w_out': _jnp.float32, 'mlp_norm': _jnp.float32, 'mlp_w1': _jnp.float32, 'mlp_w2': _jnp.float32, 'final_norm': _jnp.float32}
MOMENT_SCALE = {'e_norm_mix': 8.546360e-02, 'e_w_in': 6.764362e-02, 'e_q_norm': 8.779531e-02, 'e_w_uq': 4.810703e-02, 'e_kv_norm': 1.346580e-01, 'e_w_ukv': 5.577304e-02, 'e_v_norm': 3.431560e-02, 'e_sgu_w': 3.229649e-02, 'e_sgu_b': 4.793788e-02, 'e_mla_out_norm': 6.653960e-02, 'e_sgu_out_norm': 6.219267e-02, 'e_w_out': 5.919092e-02, 'o_norm_mix': 6.247315e-02, 'o_w_in': 3.454311e-02, 'o_conv_w': 3.559180e-02, 'o_w_out': 3.475877e-02, 'mlp_norm': 3.022548e-02, 'mlp_w1': 1.534039e-02, 'mlp_w2': 5.522466e-02, 'final_norm': 8.035139e+00}


def _to_microbatches(a, axis):
    t = _jnp.moveaxis(a, axis, 0)
    t = t.reshape((N_MICROBATCH, t.shape[0] // N_MICROBATCH) + t.shape[1:])
    return _jnp.moveaxis(t, 1, axis + 1)


def setup_inputs(seed: int = 0) -> dict:
    inp = _fwd_setup_inputs(seed)
    key = _jax.random.fold_in(_jax.random.key(seed), 7919)
    shape, _ = _output_shape()
    out = dict(inp)
    out["loss_target"] = _jax.random.normal(_jax.random.fold_in(key, 0), shape, _jnp.float32)
    for i, name in enumerate(TWIN_WEIGHTS):
        w = inp[name].astype(_jnp.float32)
        if MOMENT_SCALE is None:
            s = _jnp.sqrt(_jnp.mean(_jnp.square(w)) + 1e-30)
        else:
            s = MOMENT_SCALE[name]
        km, kv = _jax.random.split(_jax.random.fold_in(key, i + 1))
        out[name] = w
        out["m_" + name] = s * _jax.random.normal(km, w.shape, _jnp.float32)
        out["v_" + name] = (s * s) * _jax.random.uniform(kv, w.shape, _jnp.float32, 0.5, 1.5)
    if N_MICROBATCH > 1:
        for name, axis in PER_EXAMPLE_BATCH_AXIS.items():
            out[name] = _to_microbatches(out[name], axis)
    return {'x': out['x'], 'positions': out['positions'], 'e_norm_mix': out['e_norm_mix'], 'e_w_in': out['e_w_in'], 'e_q_norm': out['e_q_norm'], 'e_w_uq': out['e_w_uq'], 'e_kv_norm': out['e_kv_norm'], 'e_w_ukv': out['e_w_ukv'], 'e_v_norm': out['e_v_norm'], 'e_sgu_w': out['e_sgu_w'], 'e_sgu_b': out['e_sgu_b'], 'e_mla_out_norm': out['e_mla_out_norm'], 'e_sgu_out_norm': out['e_sgu_out_norm'], 'e_w_out': out['e_w_out'], 'o_norm_mix': out['o_norm_mix'], 'o_w_in': out['o_w_in'], 'o_conv_w': out['o_conv_w'], 'o_w_out': out['o_w_out'], 'mlp_norm': out['mlp_norm'], 'mlp_w1': out['mlp_w1'], 'mlp_w2': out['mlp_w2'], 'final_norm': out['final_norm'], 'loss_target': out['loss_target'], 'm_e_norm_mix': out['m_e_norm_mix'], 'm_e_w_in': out['m_e_w_in'], 'm_e_q_norm': out['m_e_q_norm'], 'm_e_w_uq': out['m_e_w_uq'], 'm_e_kv_norm': out['m_e_kv_norm'], 'm_e_w_ukv': out['m_e_w_ukv'], 'm_e_v_norm': out['m_e_v_norm'], 'm_e_sgu_w': out['m_e_sgu_w'], 'm_e_sgu_b': out['m_e_sgu_b'], 'm_e_mla_out_norm': out['m_e_mla_out_norm'], 'm_e_sgu_out_norm': out['m_e_sgu_out_norm'], 'm_e_w_out': out['m_e_w_out'], 'm_o_norm_mix': out['m_o_norm_mix'], 'm_o_w_in': out['m_o_w_in'], 'm_o_conv_w': out['m_o_conv_w'], 'm_o_w_out': out['m_o_w_out'], 'm_mlp_norm': out['m_mlp_norm'], 'm_mlp_w1': out['m_mlp_w1'], 'm_mlp_w2': out['m_mlp_w2'], 'm_final_norm': out['m_final_norm'], 'v_e_norm_mix': out['v_e_norm_mix'], 'v_e_w_in': out['v_e_w_in'], 'v_e_q_norm': out['v_e_q_norm'], 'v_e_w_uq': out['v_e_w_uq'], 'v_e_kv_norm': out['v_e_kv_norm'], 'v_e_w_ukv': out['v_e_w_ukv'], 'v_e_v_norm': out['v_e_v_norm'], 'v_e_sgu_w': out['v_e_sgu_w'], 'v_e_sgu_b': out['v_e_sgu_b'], 'v_e_mla_out_norm': out['v_e_mla_out_norm'], 'v_e_sgu_out_norm': out['v_e_sgu_out_norm'], 'v_e_w_out': out['v_e_w_out'], 'v_o_norm_mix': out['v_o_norm_mix'], 'v_o_w_in': out['v_o_w_in'], 'v_o_conv_w': out['v_o_conv_w'], 'v_o_w_out': out['v_o_w_out'], 'v_mlp_norm': out['v_mlp_norm'], 'v_mlp_w1': out['v_mlp_w1'], 'v_mlp_w2': out['v_mlp_w2'], 'v_final_norm': out['v_final_norm']}


def _loss(weights, diff, rest, loss_target):
    with _jax.named_scope("forward"):
        args = {**rest, TWIN_DIFF_INPUT: diff, **{k: w.astype(_WEIGHT_DTYPES[k]) for k, w in weights.items()}}
        y = _forward(args)
    with _jax.named_scope("loss_head"):
        err = _jnp.square(y.astype(_jnp.float32) - loss_target)
        return 0.5 * _jnp.sum(_jnp.mean(err, axis=-1)) if err.ndim else 0.5 * err


def _adamw(w, g, m, v):
    m = ADAM_B1 * m + (1.0 - ADAM_B1) * g
    v = ADAM_B2 * v + (1.0 - ADAM_B2) * _jnp.square(g)
    m_hat = m / (1.0 - ADAM_B1 ** ADAM_STEP)
    v_hat = v / (1.0 - ADAM_B2 ** ADAM_STEP)
    delta = -ADAM_LR * (m_hat / (_jnp.sqrt(v_hat) + ADAM_EPS) + ADAM_WD * w)
    return delta, m, v


def reference(x, positions, e_norm_mix, e_w_in, e_q_norm, e_w_uq, e_kv_norm, e_w_ukv, e_v_norm, e_sgu_w, e_sgu_b, e_mla_out_norm, e_sgu_out_norm, e_w_out, o_norm_mix, o_w_in, o_conv_w, o_w_out, mlp_norm, mlp_w1, mlp_w2, final_norm, loss_target, m_e_norm_mix, m_e_w_in, m_e_q_norm, m_e_w_uq, m_e_kv_norm, m_e_w_ukv, m_e_v_norm, m_e_sgu_w, m_e_sgu_b, m_e_mla_out_norm, m_e_sgu_out_norm, m_e_w_out, m_o_norm_mix, m_o_w_in, m_o_conv_w, m_o_w_out, m_mlp_norm, m_mlp_w1, m_mlp_w2, m_final_norm, v_e_norm_mix, v_e_w_in, v_e_q_norm, v_e_w_uq, v_e_kv_norm, v_e_w_ukv, v_e_v_norm, v_e_sgu_w, v_e_sgu_b, v_e_mla_out_norm, v_e_sgu_out_norm, v_e_w_out, v_o_norm_mix, v_o_w_in, v_o_conv_w, v_o_w_out, v_mlp_norm, v_mlp_w1, v_mlp_w2, v_final_norm):
    given = dict(x=x, positions=positions, e_norm_mix=e_norm_mix, e_w_in=e_w_in, e_q_norm=e_q_norm, e_w_uq=e_w_uq, e_kv_norm=e_kv_norm, e_w_ukv=e_w_ukv, e_v_norm=e_v_norm, e_sgu_w=e_sgu_w, e_sgu_b=e_sgu_b, e_mla_out_norm=e_mla_out_norm, e_sgu_out_norm=e_sgu_out_norm, e_w_out=e_w_out, o_norm_mix=o_norm_mix, o_w_in=o_w_in, o_conv_w=o_conv_w, o_w_out=o_w_out, mlp_norm=mlp_norm, mlp_w1=mlp_w1, mlp_w2=mlp_w2, final_norm=final_norm, loss_target=loss_target, m_e_norm_mix=m_e_norm_mix, m_e_w_in=m_e_w_in, m_e_q_norm=m_e_q_norm, m_e_w_uq=m_e_w_uq, m_e_kv_norm=m_e_kv_norm, m_e_w_ukv=m_e_w_ukv, m_e_v_norm=m_e_v_norm, m_e_sgu_w=m_e_sgu_w, m_e_sgu_b=m_e_sgu_b, m_e_mla_out_norm=m_e_mla_out_norm, m_e_sgu_out_norm=m_e_sgu_out_norm, m_e_w_out=m_e_w_out, m_o_norm_mix=m_o_norm_mix, m_o_w_in=m_o_w_in, m_o_conv_w=m_o_conv_w, m_o_w_out=m_o_w_out, m_mlp_norm=m_mlp_norm, m_mlp_w1=m_mlp_w1, m_mlp_w2=m_mlp_w2, m_final_norm=m_final_norm, v_e_norm_mix=v_e_norm_mix, v_e_w_in=v_e_w_in, v_e_q_norm=v_e_q_norm, v_e_w_uq=v_e_w_uq, v_e_kv_norm=v_e_kv_norm, v_e_w_ukv=v_e_w_ukv, v_e_v_norm=v_e_v_norm, v_e_sgu_w=v_e_sgu_w, v_e_sgu_b=v_e_sgu_b, v_e_mla_out_norm=v_e_mla_out_norm, v_e_sgu_out_norm=v_e_sgu_out_norm, v_e_w_out=v_e_w_out, v_o_norm_mix=v_o_norm_mix, v_o_w_in=v_o_w_in, v_o_conv_w=v_o_conv_w, v_o_w_out=v_o_w_out, v_mlp_norm=v_mlp_norm, v_mlp_w1=v_mlp_w1, v_mlp_w2=v_mlp_w2, v_final_norm=v_final_norm)
    weights = {n: given[n] for n in TWIN_WEIGHTS}
    shared = {n: given[n] for n in SHARED_INPUTS}
    per_example = {n: given[n] for n in ['x', 'positions']}
    grad_fn = _jax.value_and_grad(_loss, argnums=(0, 1))

    def one_microbatch(ex, loss_target):
        ex = dict(ex)
        diff = ex.pop(TWIN_DIFF_INPUT)
        return grad_fn(weights, diff, {**shared, **ex}, loss_target)

    if N_MICROBATCH == 1:
        loss, (grad_w, grad_x) = one_microbatch(per_example, given["loss_target"])
    else:
        def body(carry, xs):
            loss_sum, grad_sum = carry
            l_k, (gw_k, gx_k) = one_microbatch(xs[0], xs[1])
            with _jax.named_scope("update"):
                return (loss_sum + l_k, _jax.tree.map(_jnp.add, grad_sum, gw_k)), gx_k

        init = (_jnp.zeros((), _jnp.float32), _jax.tree.map(_jnp.zeros_like, weights))
        (loss, grad_w), grad_x = _jax.lax.scan(body, init, (per_example, given["loss_target"]))
    with _jax.named_scope("update"):
        delta_w, new_m, new_v = {}, {}, {}
        for n in TWIN_WEIGHTS:
            delta_w[n], new_m[n], new_v[n] = _adamw(weights[n], grad_w[n], given["m_" + n], given["v_" + n])
    return (loss, grad_x, *[grad_w[n] for n in TWIN_WEIGHTS], *[delta_w[n] for n in TWIN_WEIGHTS],
            *[new_m[n] for n in TWIN_WEIGHTS], *[new_v[n] for n in TWIN_WEIGHTS])
```

```python
import functools

import jax
import jax.numpy as jnp
from jax import lax
from jax.experimental import pallas as pl
from jax.experimental.pallas import tpu as pltpu

F32 = jnp.float32
BF16 = jnp.bfloat16
MESH = pl.DeviceIdType.MESH

LANES = 128
ROPE = 64
ROPE_HALF = ROPE // 2
ROPE_BASE = 10000.0
EPS = 1e-6
N_CHIPS = 4
VMEM_LIMIT = 48 * 1024 * 1024
NEG = -1e30

ADAM_LR = 0.001
ADAM_B1 = 0.9
ADAM_B2 = 0.999
ADAM_EPS = 1e-08
ADAM_WD = 0.01
ADAM_STEP = 10


def _pick(n, target, step=LANES):
    best = None
    for t in range(step, min(n, target) + 1, step):
        if n % t == 0:
            best = t
    return best if best is not None else n


def _params(sem, vmem=VMEM_LIMIT):
    return pltpu.CompilerParams(dimension_semantics=sem, vmem_limit_bytes=vmem)


class Mat:
    def __init__(self, arr, rows, cols, kind="plain", lead=(), col_off=0, shape=None, dtype=None):
        self.arr, self.rows, self.cols, self.kind, self.lead, self.col_off = arr, rows, cols, kind, tuple(lead), col_off
        self.shape = tuple(arr.shape) if arr is not None else tuple(shape)
        self.dtype = arr.dtype if arr is not None else dtype

    def sds(self):
        return jax.ShapeDtypeStruct(self.shape, self.dtype)

    def spec(self, br, bc, gridmap):
        lead, nl = self.lead, len(self.lead)
        if self.kind == "plain":
            assert self.col_off % bc == 0 and self.rows % br == 0 and self.cols % bc == 0, (self.shape, br, bc)
            off = self.col_off // bc
            block = (None,) * nl + (br, bc)

            def phys(rb, cb):
                return lead + (rb, cb + off)
        elif self.kind == "colstack":
            cs = self.shape[-1]
            assert cs % bc == 0 and self.rows % br == 0, (self.shape, br, bc)
            q = cs // bc
            block = (None,) * (nl + 1) + (br, bc)

            def phys(rb, cb):
                return (cb // q,) + lead + (rb, cb % q)
        else:
            rs = self.shape[-2]
            assert rs % br == 0 and self.cols % bc == 0, (self.shape, br, bc)
            q = rs // br
            block = (None,) * (nl + 1) + (br, bc)

            def phys(rb, cb):
                return (rb // q,) + lead + (rb % q, cb)

        return pl.BlockSpec(block, lambda *g: phys(*gridmap(*g)))


def _matmul(name, a, b, mode, outs, tm, tn, tk, epilogue=None, extras=(), aliases=None, alias_in=()):
    if mode == "nn":
        m, k, n = a.rows, a.cols, b.cols
        a_spec = a.spec(tm, tk, lambda i, j, kk: (i, kk))
        b_spec = b.spec(tk, tn, lambda i, j, kk: (kk, j))
        dims = (((1,), (0,)), ((), ()))
    elif mode == "nt":
        m, k, n = a.rows, a.cols, b.rows
        a_spec = a.spec(tm, tk, lambda i, j, kk: (i, kk))
        b_spec = b.spec(tn, tk, lambda i, j, kk: (j, kk))
        dims = (((1,), (1,)), ((), ()))
    else:
        k, m, n = a.rows, a.cols, b.cols
        a_spec = a.spec(tk, tm, lambda i, j, kk: (kk, i))
        b_spec = b.spec(tk, tn, lambda i, j, kk: (kk, j))
        dims = (((0,), (0,)), ((), ()))
    assert m % tm == 0 and n % tn == 0 and k % tk == 0, (name, m, n, k, tm, tn, tk)
    grid = (m // tm, n // tn, k // tk)
    nk = grid[2]
    n_ex, n_out, n_al = len(extras), len(outs), len(alias_in)
    tile = lambda i, j, kk: (i, j)

    def body(a_ref, b_ref, *rest):
        ex = rest[:n_ex]
        out_refs = rest[n_ex + n_al:n_ex + n_al + n_out]
        acc = rest[-1]
        kk = pl.program_id(2)

        @pl.when(kk == 0)
        def _():
            acc[...] = jnp.zeros_like(acc)

        acc[...] += lax.dot_general(a_ref[...], b_ref[...], dims, preferred_element_type=F32)

        @pl.when(kk == nk - 1)
        def _():
            vals = epilogue(acc[...], *[e[...] for e in ex]) if epilogue is not None else (acc[...],)
            for o, v in zip(out_refs, vals):
                o[...] = v.astype(o.dtype)

    res = pl.pallas_call(
        body, name=name, grid=grid,
        in_specs=[a_spec, b_spec] + [e.spec(tm, tn, tile) for e in extras]
        + [pl.BlockSpec(memory_space=pl.ANY) for _ in alias_in],
        out_specs=[o.spec(tm, tn, tile) for o in outs],
        out_shape=[o.sds() for o in outs],
        scratch_shapes=[pltpu.VMEM((tm, tn), F32)],
        input_output_aliases=aliases or {},
        compiler_params=_params(("parallel", "parallel", "arbitrary")),
    )(a.arr, b.arr, *[e.arr for e in extras], *alias_in)
    return res


def _out(rows, cols, dtype, kind="plain", lead=(), shape=None):
    return Mat(None, rows, cols, kind, lead, shape=shape if shape is not None else (rows, cols), dtype=dtype)


def _rt(arr, tr, width=None, cb=0):
    width = arr.shape[1] if width is None else width
    return arr, pl.BlockSpec((tr, width), lambda i: (i, cb))


def _whole(arr):
    nd = arr.ndim
    return arr, pl.BlockSpec(arr.shape, lambda i: (0,) * nd)


def _rowwise(name, fn, n_steps, ins, outs, accs=()):
    n_in, n_out, n_acc = len(ins), len(outs), len(accs)

    def body(*refs):
        vals = fn(*[r[...] for r in refs[:n_in]])
        if not isinstance(vals, (tuple, list)):
            vals = (vals,)
        for ref, v in zip(refs[n_in:n_in + n_out], vals[:n_out]):
            ref[...] = v.astype(ref.dtype)
        if n_acc:
            acc_refs = refs[n_in + n_out:]

            @pl.when(pl.program_id(0) == 0)
            def _():
                for ref in acc_refs:
                    ref[...] = jnp.zeros_like(ref)

            for ref, v in zip(acc_refs, vals[n_out:]):
                ref[...] += v

    acc_specs = [pl.BlockSpec(s.shape, lambda i, nd=len(s.shape): (0,) * nd) for s in accs]
    res = pl.pallas_call(
        body, name=name, grid=(n_steps,),
        in_specs=[s for _, s in ins],
        out_specs=[s for _, s in outs] + acc_specs,
        out_shape=[o for o, _ in outs] + list(accs),
        compiler_params=_params(("arbitrary",) if n_acc else ("parallel",)),
    )(*[a for a, _ in ins])
    return res


def _rt_out(t, width, dtype, tr):
    return jax.ShapeDtypeStruct((t, width), dtype), pl.BlockSpec((tr, width), lambda i: (i, 0))


def _rms(x, g):
    r = lax.rsqrt(jnp.mean(x * x, axis=-1, keepdims=True) + EPS)
    return x * r * g


def _rms_bwd(dy, x, g):
    r = lax.rsqrt(jnp.mean(x * x, axis=-1, keepdims=True) + EPS)
    xh = x * r
    dxh = dy * g
    dx = r * (dxh - xh * jnp.mean(dxh * xh, axis=-1, keepdims=True))
    dg = jnp.sum(dy * xh, axis=0, keepdims=True)
    return dx, dg


def _gelu(x):
    k = 0.7978845608028654
    th = jnp.tanh(k * (x + 0.044715 * (x * x * x)))
    return x * (0.5 * (1.0 + th))


def _gelu_grad(x):
    k = 0.7978845608028654
    x2 = x * x
    th = jnp.tanh(k * (x + 0.044715 * (x2 * x)))
    return 0.5 * (1.0 + th) + 0.5 * x * (1.0 - th * th) * (k * (1.0 + 3.0 * 0.044715 * x2))


def _norm_fwd(name, x, g, tr):
    t, d = x.shape
    return _rowwise(name, lambda xv, gv: _rms(xv, gv), t // tr, [_rt(x, tr), _whole(g)], [_rt_out(t, d, BF16, tr)])[0]


def _norm_bwd(name, dh, x, g, dres, tr):
    t, d = x.shape

    def fn(dhv, xv, gv, drv):
        dx, dg = _rms_bwd(dhv, xv, gv)
        dx = dx + drv
        return dx, dx, dg

    return _rowwise(name, fn, t // tr, [_rt(dh, tr), _rt(x, tr), _whole(g), _rt(dres, tr)],
                    [_rt_out(t, d, F32, tr), _rt_out(t, d, BF16, tr)], [jax.ShapeDtypeStruct((1, d), F32)])


def _rope_tables(posf, invf, cmask, smask, tr):
    t = posf.shape[0]

    def fn(p, f, cm, sm):
        ang = p * f
        return jnp.cos(ang) * cm, jnp.sin(ang) * sm

    return _rowwise("rope_tables", fn, t // tr, [_rt(posf, tr), _whole(invf), _whole(cmask), _whole(smask)],
                    [_rt_out(t, LANES, F32, tr), _rt_out(t, LANES, F32, tr)])


def _rot(v, c, s):
    return v * c + pltpu.roll(v, ROPE, axis=1) * s


def _rot_bwd(dv, c, s):
    return dv * c + pltpu.roll(dv * s, ROPE, axis=1)


def _rope_fwd(qfull, proj, kr_cb, ctab, stab, heads, tr):
    t = qfull.shape[0]
    hw = heads * LANES

    def fn(q, kr, c, s):
        parts = [q[:, :hw]] + [_rot(q[:, hw + h * LANES: hw + (h + 1) * LANES], c, s) for h in range(heads)]
        return jnp.concatenate(parts, axis=1), _rot(kr, c, s)

    return _rowwise("rope_fwd", fn, t // tr, [_rt(qfull, tr), _rt(proj, tr, LANES, kr_cb), _rt(ctab, tr), _rt(stab, tr)],
                    [_rt_out(t, 2 * hw, BF16, tr), _rt_out(t, LANES, BF16, tr)])


def _rope_bwd(dq1, dq2, dkr_h, ctab, stab, heads, tr):
    t = dq1.shape[0]
    hw = heads * LANES

    def fn(a, b, dk, c, s):
        parts = [a] + [_rot_bwd(b[:, h * LANES:(h + 1) * LANES], c, s) for h in range(heads)]
        dks = dk[0]
        for h in range(1, heads):
            dks = dks + dk[h]
        return jnp.concatenate(parts, axis=1), _rot_bwd(dks, c, s)

    dk_spec = pl.BlockSpec((heads, tr, LANES), lambda i: (0, i, 0))
    return _rowwise("rope_bwd", fn, t // tr, [_rt(dq1, tr), _rt(dq2, tr), (dkr_h, dk_spec), _rt(ctab, tr), _rt(stab, tr)],
                    [_rt_out(t, 2 * hw, BF16, tr), _rt_out(t, LANES, BF16, tr)])


def _dot_nt(a, b):
    return lax.dot_general(a, b, (((1,), (1,)), ((), ())), preferred_element_type=F32)


def _dot_tn(a, b):
    return lax.dot_general(a, b, (((0,), (0,)), ((), ())), preferred_element_type=F32)


def _dot(a, b):
    return jnp.dot(a, b, preferred_element_type=F32)


def _causal(i, j, tq, tk):
    rows = i * tq + lax.broadcasted_iota(jnp.int32, (tq, tk), 0)
    cols = j * tk + lax.broadcasted_iota(jnp.int32, (tq, tk), 1)
    return cols <= rows


def _attn_fwd(qall, kvall, kr, heads, scale, tq):
    t = qall.shape[0]
    nq = t // tq
    tk = tq

    def body(qn_ref, qr_ref, kn_ref, v_ref, kr_ref, o_ref, lse_ref, m_ref, l_ref, acc_ref):
        i = pl.program_id(1)
        qn, qr = qn_ref[...], qr_ref[...]
        m_ref[...] = jnp.full_like(m_ref, NEG)
        l_ref[...] = jnp.zeros_like(l_ref)
        acc_ref[...] = jnp.zeros_like(acc_ref)

        def step(j, carry):
            ks = pl.multiple_of(j * tk, tk)
            s = (_dot_nt(qn, kn_ref[pl.ds(ks, tk), :]) + _dot_nt(qr, kr_ref[pl.ds(ks, tk), :])) * scale
            s = jnp.where(_causal(i, j, tq, tk), s, NEG)
            m_prev = m_ref[...]
            m_new = jnp.maximum(m_prev, jnp.max(s, axis=-1, keepdims=True))
            p = jnp.exp(s - m_new[:, :1])
            alpha = jnp.exp(m_prev - m_new)
            l_ref[...] = alpha * l_ref[...] + jnp.sum(p, axis=-1, keepdims=True)
            acc_ref[...] = alpha * acc_ref[...] + _dot(p.astype(BF16), v_ref[pl.ds(ks, tk), :])
            m_ref[...] = m_new
            return carry

        lax.fori_loop(0, i + 1, step, 0)
        o_ref[...] = acc_ref[...] / l_ref[...]
        lse_ref[...] = m_ref[...] + jnp.log(l_ref[...])

    return pl.pallas_call(
        body, name="attn_fwd", grid=(heads, nq),
        in_specs=[pl.BlockSpec((tq, LANES), lambda h, i: (i, h)),
                  pl.BlockSpec((tq, LANES), lambda h, i: (i, heads + h)),
                  pl.BlockSpec((t, LANES), lambda h, i: (0, h)),
                  pl.BlockSpec((t, LANES), lambda h, i: (0, heads + h)),
                  pl.BlockSpec((t, LANES), lambda h, i: (0, 0))],
        out_specs=[pl.BlockSpec((tq, LANES), lambda h, i: (i, h)),
                   pl.BlockSpec((None, tq, LANES), lambda h, i: (h, i, 0))],
        out_shape=[jax.ShapeDtypeStruct((t, heads * LANES), F32), jax.ShapeDtypeStruct((heads, t, LANES), F32)],
        scratch_shapes=[pltpu.VMEM((tq, LANES), F32)] * 3,
        compiler_params=_params(("parallel", "arbitrary")),
    )(qall, qall, kvall, kvall, kr)


def _attn_dq(qall, kvall, kr, do, lse, delta, heads, scale, tq):
    t = qall.shape[0]
    nq = t // tq
    tk = tq

    def body(qn_ref, qr_ref, kn_ref, v_ref, kr_ref, do_ref, lse_ref, dl_ref, dq1_ref, dq2_ref, a1_ref, a2_ref):
        i = pl.program_id(1)
        qn, qr, do_v = qn_ref[...], qr_ref[...], do_ref[...]
        lse_v, dl_v = lse_ref[...][:, :1], dl_ref[...][:, :1]
        a1_ref[...] = jnp.zeros_like(a1_ref)
        a2_ref[...] = jnp.zeros_like(a2_ref)

        def step(j, carry):
            ks = pl.multiple_of(j * tk, tk)
            k1, k2 = kn_ref[pl.ds(ks, tk), :], kr_ref[pl.ds(ks, tk), :]
            s = (_dot_nt(qn, k1) + _dot_nt(qr, k2)) * scale
            p = jnp.where(_causal(i, j, tq, tk), jnp.exp(s - lse_v), 0.0)
            dp = _dot_nt(do_v, v_ref[pl.ds(ks, tk), :])
            ds = (p * (dp - dl_v) * scale).astype(BF16)
            a1_ref[...] += _dot(ds, k1)
            a2_ref[...] += _dot(ds, k2)
            return carry

        lax.fori_loop(0, i + 1, step, 0)
        dq1_ref[...] = a1_ref[...]
        dq2_ref[...] = a2_ref[...]

    qblk = lambda off: pl.BlockSpec((tq, LANES), lambda h, i: (i, off + h))
    full = lambda off: pl.BlockSpec((t, LANES), lambda h, i: (0, off + h))
    stat = pl.BlockSpec((None, tq, LANES), lambda h, i: (h, i, 0))
    return pl.pallas_call(
        body, name="attn_dq", grid=(heads, nq),
        in_specs=[qblk(0), qblk(heads), full(0), full(heads), pl.BlockSpec((t, LANES), lambda h, i: (0, 0)),
                  qblk(0), stat, stat],
        out_specs=[qblk(0), qblk(0)],
        out_shape=[jax.ShapeDtypeStruct((t, heads * LANES), F32)] * 2,
        scratch_shapes=[pltpu.VMEM((tq, LANES), F32)] * 2,
        compiler_params=_params(("parallel", "arbitrary")),
    )(qall, qall, kvall, kvall, kr, do, lse, delta)


def _attn_dkv(qall, kvall, kr, do, lse, delta, heads, scale, tq):
    t = qall.shape[0]
    nq = t // tq
    tk = tq

    def body(qn_ref, qr_ref, kn_ref, v_ref, kr_ref, do_ref, lse_ref, dl_ref, dk_ref, dv_ref, dkr_ref, ak, av, akr):
        j = pl.program_id(1)
        k1, k2, vv = kn_ref[...], kr_ref[...], v_ref[...]
        ak[...] = jnp.zeros_like(ak)
        av[...] = jnp.zeros_like(av)
        akr[...] = jnp.zeros_like(akr)

        def step(i, carry):
            qs = pl.multiple_of(i * tq, tq)
            qn, qr, do_v = qn_ref[pl.ds(qs, tq), :], qr_ref[pl.ds(qs, tq), :], do_ref[pl.ds(qs, tq), :]
            lse_v, dl_v = lse_ref[pl.ds(qs, tq), :][:, :1], dl_ref[pl.ds(qs, tq), :][:, :1]
            s = (_dot_nt(qn, k1) + _dot_nt(qr, k2)) * scale
            p = jnp.where(_causal(i, j, tq, tk), jnp.exp(s - lse_v), 0.0)
            dp = _dot_nt(do_v, vv)
            ds = (p * (dp - dl_v) * scale).astype(BF16)
            av[...] += _dot_tn(p.astype(BF16), do_v)
            ak[...] += _dot_tn(ds, qn)
            akr[...] += _dot_tn(ds, qr)
            return carry

        lax.fori_loop(j, nq, step, 0)
        dk_ref[...] = ak[...].astype(dk_ref.dtype)
        dv_ref[...] = av[...].astype(dv_ref.dtype)
        dkr_ref[...] = akr[...]

    kblk = lambda off: pl.BlockSpec((tk, LANES), lambda h, j: (j, off + h))
    full = lambda off: pl.BlockSpec((t, LANES), lambda h, j: (0, off + h))
    stat = pl.BlockSpec((None, t, LANES), lambda h, j: (h, 0, 0))
    return pl.pallas_call(
        body, name="attn_dkv", grid=(heads, nq),
        in_specs=[full(0), full(heads), kblk(0), kblk(heads), pl.BlockSpec((tk, LANES), lambda h, j: (j, 0)),
                  full(0), stat, stat],
        out_specs=[kblk(0), kblk(0), pl.BlockSpec((None, tk, LANES), lambda h, j: (h, j, 0))],
        out_shape=[jax.ShapeDtypeStruct((t, heads * LANES), BF16)] * 2 + [jax.ShapeDtypeStruct((heads, t, LANES), F32)],
        scratch_shapes=[pltpu.VMEM((tk, LANES), F32)] * 3,
        compiler_params=_params(("parallel", "arbitrary")),
    )(qall, qall, kvall, kvall, kr, do, lse, delta)


def _tril():
    return lax.broadcasted_iota(jnp.int32, (LANES, LANES), 0) >= lax.broadcasted_iota(jnp.int32, (LANES, LANES), 1)


def _group_norm(vg):
    mu = jnp.mean(vg, axis=-1, keepdims=True)
    vc = vg - mu
    rs = lax.rsqrt(jnp.mean(vc * vc, axis=-1, keepdims=True) + EPS)
    return vc * rs, rs


def _sgu_fwd(proj, gain, w, bias, groups, rb):
    t = proj.shape[0]
    gw = groups * LANES
    cpb = rb // LANES

    def body(u_ref, v_ref, gain_ref, w_ref, b_ref, s_ref):
        tril = _tril()
        for g in range(groups):
            wt = jnp.where(tril, w_ref[g], 0.0).astype(BF16)
            cols = slice(g * LANES, (g + 1) * LANES)
            for ci in range(cpb):
                rows = slice(ci * LANES, (ci + 1) * LANES)
                ug = _gelu(u_ref[rows, cols])
                vh, _ = _group_norm(_gelu(v_ref[rows, cols]))
                vn = vh * gain_ref[:, cols]
                y = _dot(wt, vn.astype(BF16)) + b_ref[g]
                s_ref[rows, cols] = ug * y

    return pl.pallas_call(
        body, name="sgu_fwd", grid=(t // rb,),
        in_specs=[pl.BlockSpec((rb, gw), lambda i: (i, 0)), pl.BlockSpec((rb, gw), lambda i: (i, 1)),
                  pl.BlockSpec((1, gw), lambda i: (0, 0)),
                  pl.BlockSpec((groups, LANES, LANES), lambda i: (0, 0, 0)),
                  pl.BlockSpec((groups, LANES, LANES), lambda i: (0, 0, 0))],
        out_specs=pl.BlockSpec((rb, gw), lambda i: (i, 0)),
        out_shape=jax.ShapeDtypeStruct((t, gw), F32),
        compiler_params=_params(("parallel",)),
    )(proj, proj, gain, w, bias)


def _sgu_bwd(proj, ds, gain, w, bias, groups, rb):
    t = proj.shape[0]
    gw = groups * LANES
    cpb = rb // LANES
    n_steps = t // rb

    def body(u_ref, v_ref, ds_ref, gain_ref, w_ref, b_ref, du_ref, dv_ref, dw_ref, db_ref, dg_ref, dy_acc):
        step = pl.program_id(0)

        @pl.when(step == 0)
        def _():
            dw_ref[...] = jnp.zeros_like(dw_ref)
            dy_acc[...] = jnp.zeros_like(dy_acc)
            dg_ref[...] = jnp.zeros_like(dg_ref)

        tril = _tril()
        for g in range(groups):
            wt = jnp.where(tril, w_ref[g], 0.0).astype(BF16)
            cols = slice(g * LANES, (g + 1) * LANES)
            gain_g = gain_ref[:, cols]
            for ci in range(cpb):
                rows = slice(ci * LANES, (ci + 1) * LANES)
                u_raw, v_raw, ds_v = u_ref[rows, cols], v_ref[rows, cols], ds_ref[rows, cols]
                ug = _gelu(u_raw)
                vh, rs = _group_norm(_gelu(v_raw))
                vn = (vh * gain_g).astype(BF16)
                y = _dot(wt, vn) + b_ref[g]
                dy = ds_v * ug
                dyb = dy.astype(BF16)
                du_ref[rows, cols] = (ds_v * y * _gelu_grad(u_raw)).astype(du_ref.dtype)
                dy_acc[g] += dy
                dw_ref[g] += _dot_nt(dyb, vn)
                dvn = _dot_tn(wt, dyb)
                dg_ref[:, cols] += jnp.sum(dvn * vh, axis=0, keepdims=True)
                dvh = dvn * gain_g
                dvg = rs * (dvh - jnp.mean(dvh, axis=-1, keepdims=True)
                            - vh * jnp.mean(dvh * vh, axis=-1, keepdims=True))
                dv_ref[rows, cols] = (dvg * _gelu_grad(v_raw)).astype(dv_ref.dtype)

        @pl.when(step == n_steps - 1)
        def _():
            ones = jnp.ones((8, LANES), F32)
            for g in range(groups):
                dw_ref[g] = jnp.where(tril, dw_ref[g], 0.0)
                db_ref[g] = lax.dot_general(ones, dy_acc[g], (((1,), (1,)), ((), ())),
                                            precision=lax.Precision.HIGHEST, preferred_element_type=F32)

    blk = lambda cb: pl.BlockSpec((rb, gw), lambda i: (i, cb))
    whole3 = pl.BlockSpec((groups, LANES, LANES), lambda i: (0, 0, 0))
    return pl.pallas_call(
        body, name="sgu_bwd", grid=(n_steps,),
        in_specs=[blk(0), blk(1), blk(0), pl.BlockSpec((1, gw), lambda i: (0, 0)), whole3, whole3],
        out_specs=[blk(0), blk(0), whole3, pl.BlockSpec((groups, 8, LANES), lambda i: (0, 0, 0)),
                   pl.BlockSpec((1, gw), lambda i: (0, 0))],
        out_shape=[jax.ShapeDtypeStruct((t, gw), BF16), jax.ShapeDtypeStruct((t, gw), BF16),
                   jax.ShapeDtypeStruct((groups, LANES, LANES), F32), jax.ShapeDtypeStruct((groups, 8, LANES), F32),
                   jax.ShapeDtypeStruct((1, gw), F32)],
        scratch_shapes=[pltpu.VMEM((groups, LANES, LANES), F32)],
        compiler_params=_params(("arbitrary",)),
    )(proj, proj, ds, gain, w, bias)


def _shift_down(z, s):
    rows = lax.broadcasted_iota(jnp.int32, z.shape, 0)
    return jnp.where(rows >= s, pltpu.roll(z, s, axis=0), 0.0)


def _shift_up(z, s):
    n = z.shape[0]
    rows = lax.broadcasted_iota(jnp.int32, z.shape, 0)
    return jnp.where(rows < n - s, pltpu.roll(z, n - s, axis=0), 0.0)


def _conv_fwd(proj3, cw, tc):
    _, t, cd = proj3.shape

    def body(p_ref, w_ref, o_ref):
        z = p_ref[1] * p_ref[2]
        w = w_ref[...]
        zc = w[2:3] * z + w[1:2] * _shift_down(z, 1) + w[0:1] * _shift_down(z, 2)
        o_ref[...] = (p_ref[0] * zc).astype(o_ref.dtype)

    return pl.pallas_call(
        body, name="conv_fwd", grid=(cd // tc,),
        in_specs=[pl.BlockSpec((3, t, tc), lambda j: (0, 0, j)), pl.BlockSpec((8, tc), lambda j: (0, j))],
        out_specs=pl.BlockSpec((t, tc), lambda j: (0, j)),
        out_shape=jax.ShapeDtypeStruct((t, cd), BF16),
        compiler_params=_params(("parallel",)),
    )(proj3, cw)


def _conv_bwd(proj3, cw, dbz, tc):
    _, t, cd = proj3.shape

    def body(p_ref, w_ref, d_ref, o_ref, dw_ref):
        b, c, xin = p_ref[0], p_ref[1], p_ref[2]
        w = w_ref[...]
        z = c * xin
        z1, z2 = _shift_down(z, 1), _shift_down(z, 2)
        zc = w[2:3] * z + w[1:2] * z1 + w[0:1] * z2
        d = d_ref[...]
        dzc = d * b
        dz = w[2:3] * dzc + w[1:2] * _shift_up(dzc, 1) + w[0:1] * _shift_up(dzc, 2)
        o_ref[0] = (d * zc).astype(o_ref.dtype)
        o_ref[1] = (dz * xin).astype(o_ref.dtype)
        o_ref[2] = (dz * c).astype(o_ref.dtype)
        row = lax.broadcasted_iota(jnp.int32, (8, tc), 0)
        dw0 = jnp.sum(dzc * z2, axis=0, keepdims=True)
        dw1 = jnp.sum(dzc * z1, axis=0, keepdims=True)
        dw2 = jnp.sum(dzc * z, axis=0, keepdims=True)
        dw_ref[...] = jnp.where(row == 0, dw0, 0.0) + jnp.where(row == 1, dw1, 0.0) + jnp.where(row == 2, dw2, 0.0)

    return pl.pallas_call(
        body, name="conv_bwd", grid=(cd // tc,),
        in_specs=[pl.BlockSpec((3, t, tc), lambda j: (0, 0, j)), pl.BlockSpec((8, tc), lambda j: (0, j)),
                  pl.BlockSpec((t, tc), lambda j: (0, j))],
        out_specs=[pl.BlockSpec((3, t, tc), lambda j: (0, 0, j)), pl.BlockSpec((8, tc), lambda j: (0, j))],
        out_shape=[jax.ShapeDtypeStruct((3, t, cd), BF16), jax.ShapeDtypeStruct((8, cd), F32)],
        compiler_params=_params(("parallel",)),
    )(proj3, cw, dbz)


def _place():
    x, y, c = lax.axis_index("x"), lax.axis_index("y"), lax.axis_index("c")
    chips = [(1 - x, y), (x, 1 - y), (1 - x, 1 - y)]
    return x, y, c, chips


def _any_specs(n):
    return [pl.BlockSpec(memory_space=pl.ANY) for _ in range(n)]


def _all_gather(entries):
    n = len(entries)

    def body(*refs):
        ins, outs = refs[:n], refs[n:2 * n]
        ici_send, ici_recv, d2d_send, d2d_recv, loc_sem = refs[2 * n:]
        x, y, c, chips = _place()
        me = 2 * x + y
        sib = (x, y, 1 - c)
        local = [pltpu.make_async_copy(ins[i], outs[i].at[me], loc_sem.at[i]) for i in range(n)]
        for cp in local:
            cp.start()

        def ici(i, k, src, slot, half, dev):
            return pltpu.make_async_remote_copy(src_ref=src, dst_ref=outs[i].at[slot, half],
                                                send_sem=ici_send.at[3 * i + k], recv_sem=ici_recv.at[3 * i + k],
                                                device_id=dev, device_id_type=MESH)

        def d2d(i, k, slot, half, dev):
            blk = outs[i].at[slot, half]
            return pltpu.make_async_remote_copy(src_ref=blk, dst_ref=blk,
                                                send_sem=d2d_send.at[3 * i + k], recv_sem=d2d_recv.at[3 * i + k],
                                                device_id=dev, device_id_type=MESH)

        sends = [ici(i, k, ins[i].at[c], me, c, (*chip, c)) for i in range(n) for k, chip in enumerate(chips)]
        for cp in sends:
            cp.start()
        passed = []
        for i in range(n):
            for k, (px, py) in enumerate(chips):
                slot = 2 * px + py
                ici(i, k, ins[i].at[c], slot, c, (px, py, c)).wait_recv()
                fwd = d2d(i, k, slot, c, sib)
                fwd.start()
                passed.append(fwd)
        for i in range(n):
            for k, (px, py) in enumerate(chips):
                d2d(i, k, 2 * px + py, 1 - c, sib).wait_recv()
        for cp in sends + passed:
            cp.wait_send()
        for cp in local:
            cp.wait()

    return pl.pallas_call(
        body, name="all_gather_weights",
        in_specs=_any_specs(n), out_specs=_any_specs(n),
        out_shape=[jax.ShapeDtypeStruct((N_CHIPS,) + e.shape, e.dtype) for e in entries],
        scratch_shapes=[pltpu.SemaphoreType.DMA((3 * n,))] * 4 + [pltpu.SemaphoreType.DMA((n,))],
        compiler_params=pltpu.CompilerParams(has_side_effects=True),
    )(*entries)


def _pair_exchange(entries):
    n = len(entries)

    def body(*refs):
        ins, outs = refs[:n], refs[n:2 * n]
        send, recv = refs[2 * n:]
        x, y, c, _ = _place()
        sib = (x, y, 1 - c)

        def cp(i, j):
            return pltpu.make_async_remote_copy(src_ref=ins[i].at[j, 1 - c], dst_ref=outs[i].at[j],
                                                send_sem=send.at[N_CHIPS * i + j], recv_sem=recv.at[N_CHIPS * i + j],
                                                device_id=sib, device_id_type=MESH)

        cps = [cp(i, j) for i in range(n) for j in range(N_CHIPS)]
        for d in cps:
            d.start()
        for d in cps:
            d.wait_recv()
        for d in cps:
            d.wait_send()

    return pl.pallas_call(
        body, name="grad_pair_exchange",
        in_specs=_any_specs(n), out_specs=_any_specs(n),
        out_shape=[jax.ShapeDtypeStruct((N_CHIPS,) + e.shape[2:], e.dtype) for e in entries],
        scratch_shapes=[pltpu.SemaphoreType.DMA((N_CHIPS * n,))] * 2,
        compiler_params=pltpu.CompilerParams(has_side_effects=True),
    )(*entries)


def _chip_exchange(entries):
    n = len(entries)

    def body(*refs):
        ins, outs = refs[:n], refs[n:2 * n]
        send, recv = refs[2 * n:]
        x, y, c, chips = _place()

        def cp(i, k, px, py):
            return pltpu.make_async_remote_copy(src_ref=ins[i].at[2 * px + py], dst_ref=outs[i].at[k],
                                                send_sem=send.at[3 * i + k], recv_sem=recv.at[3 * i + k],
                                                device_id=(px, py, c), device_id_type=MESH)

        cps = [cp(i, k, px, py) for i in range(n) for k, (px, py) in enumerate(chips)]
        for d in cps:
            d.start()
        for d in cps:
            d.wait_recv()
        for d in cps:
            d.wait_send()

    return pl.pallas_call(
        body, name="grad_chip_exchange",
        in_specs=_any_specs(n), out_specs=_any_specs(n),
        out_shape=[jax.ShapeDtypeStruct((3,) + e.shape[1:], e.dtype) for e in entries],
        scratch_shapes=[pltpu.SemaphoreType.DMA((3 * n,))] * 2,
        compiler_params=pltpu.CompilerParams(has_side_effects=True),
    )(*entries)


def _pair_share(entries):
    n = len(entries)

    def body(*refs):
        ins, outs = refs[:n], refs[n:2 * n]
        send, recv, loc = refs[2 * n:]
        x, y, c, _ = _place()
        sib = (x, y, 1 - c)
        local = [pltpu.make_async_copy(ins[i], outs[i].at[c], loc.at[i]) for i in range(n)]
        for d in local:
            d.start()

        def cp(i, half):
            return pltpu.make_async_remote_copy(src_ref=ins[i], dst_ref=outs[i].at[half],
                                                send_sem=send.at[i], recv_sem=recv.at[i],
                                                device_id=sib, device_id_type=MESH)

        cps = [cp(i, c) for i in range(n)]
        for d in cps:
            d.start()
        for i in range(n):
            cp(i, 1 - c).wait_recv()
        for d in cps:
            d.wait_send()
        for d in local:
            d.wait()

    return pl.pallas_call(
        body, name="grad_pair_share",
        in_specs=_any_specs(n), out_specs=_any_specs(n),
        out_shape=[jax.ShapeDtypeStruct((2,) + e.shape, e.dtype) for e in entries],
        scratch_shapes=[pltpu.SemaphoreType.DMA((n,))] * 3,
        compiler_params=pltpu.CompilerParams(has_side_effects=True),
    )(*entries)


def _gather_all_devices(v):
    def body(v_ref, o_ref, send, recv, loc):
        x, y, c, _ = _place()
        me = 4 * x + 2 * y + c
        own = pltpu.make_async_copy(v_ref, o_ref.at[me], loc)
        own.start()
        rels = [(fx, fy, fc) for fx in (0, 1) for fy in (0, 1) for fc in (0, 1)][1:]

        def peer(fx, fy, fc):
            return (x + fx - 2 * x * fx, y + fy - 2 * y * fy, c + fc - 2 * c * fc)

        def cp(r, slot, dev):
            return pltpu.make_async_remote_copy(src_ref=v_ref, dst_ref=o_ref.at[slot], send_sem=send.at[r],
                                                recv_sem=recv.at[r], device_id=dev, device_id_type=MESH)

        cps = [cp(r, me, peer(*f)) for r, f in enumerate(rels)]
        for d in cps:
            d.start()
        for r, f in enumerate(rels):
            px, py, pc = peer(*f)
            cp(r, 4 * px + 2 * py + pc, (px, py, pc)).wait_recv()
        for d in cps:
            d.wait_send()
        own.wait()

    return pl.pallas_call(
        body, name="gather_small_grads",
        in_specs=_any_specs(1), out_specs=_any_specs(1)[0],
        out_shape=jax.ShapeDtypeStruct((8,) + v.shape, v.dtype),
        scratch_shapes=[pltpu.SemaphoreType.DMA((7,)), pltpu.SemaphoreType.DMA((7,)), pltpu.SemaphoreType.DMA],
        compiler_params=pltpu.CompilerParams(has_side_effects=True),
    )(v)


def _row_tile(rows, cols, itemsize=4, budget=2 * 1024 * 1024):
    best = None
    for t in range(8, rows + 1, 8):
        if rows % t == 0 and t * cols * itemsize <= budget:
            best = t
    return best if best is not None else rows


def _pair_sum(g5, gsib, cidx):
    _, _, rh, cols = g5.shape
    tr = _row_tile(rh, cols)

    def body(c_ref, a_ref, b_ref, o_ref):
        o_ref[...] = (a_ref[...] + b_ref[...]).astype(o_ref.dtype)

    grid_spec = pltpu.PrefetchScalarGridSpec(
        num_scalar_prefetch=1, grid=(N_CHIPS, rh // tr),
        in_specs=[pl.BlockSpec((None, None, tr, cols), lambda j, r, c_ref: (j, c_ref[0], r, 0)),
                  pl.BlockSpec((None, tr, cols), lambda j, r, c_ref: (j, r, 0))],
        out_specs=pl.BlockSpec((None, tr, cols), lambda j, r, c_ref: (j, r, 0)))
    return pl.pallas_call(body, name="grad_pair_sum", grid_spec=grid_spec,
                          out_shape=jax.ShapeDtypeStruct((N_CHIPS, rh, cols), BF16),
                          compiler_params=_params(("parallel", "parallel")))(cidx, g5, gsib)


def _chip_sum(part, recv, me):
    _, rh, cols = part.shape
    tr = _row_tile(rh, cols)

    def body(m_ref, a_ref, b_ref, o_ref):
        acc = a_ref[...].astype(F32)
        for k in range(3):
            acc = acc + b_ref[k].astype(F32)
        o_ref[...] = acc

    grid_spec = pltpu.PrefetchScalarGridSpec(
        num_scalar_prefetch=1, grid=(rh // tr,),
        in_specs=[pl.BlockSpec((None, tr, cols), lambda r, m_ref: (m_ref[0], r, 0)),
                  pl.BlockSpec((3, tr, cols), lambda r, m_ref: (0, r, 0))],
        out_specs=pl.BlockSpec((tr, cols), lambda r, m_ref: (r, 0)))
    return pl.pallas_call(body, name="grad_chip_sum", grid_spec=grid_spec,
                          out_shape=jax.ShapeDtypeStruct((rh, cols), F32),
                          compiler_params=_params(("parallel",)))(me, part, recv)


def _sum_devices(g):
    _, rows, cols = g.shape
    tr = _row_tile(rows, cols, budget=256 * 1024)

    def body(g_ref, o_ref):
        acc = g_ref[0]
        for d in range(1, 8):
            acc = acc + g_ref[d]
        o_ref[...] = acc

    return pl.pallas_call(body, name="sum_small_grads", grid=(rows // tr,),
                          in_specs=[pl.BlockSpec((8, tr, cols), lambda r: (0, r, 0))],
                          out_specs=pl.BlockSpec((tr, cols), lambda r: (r, 0)),
                          out_shape=jax.ShapeDtypeStruct((rows, cols), F32),
                          compiler_params=_params(("parallel",)))(g)


def _cast_bf16(w):
    rows, cols = w.shape
    tr = _row_tile(rows, cols)
    def body(i_ref, o_ref):
        o_ref[...] = i_ref[...].astype(BF16)

    return pl.pallas_call(body, name="cast_weights",
                          grid=(rows // tr,), in_specs=[pl.BlockSpec((tr, cols), lambda r: (r, 0))],
                          out_specs=pl.BlockSpec((tr, cols), lambda r: (r, 0)),
                          out_shape=jax.ShapeDtypeStruct((rows, cols), BF16),
                          compiler_params=_params(("parallel",)))(w)


def _adamw(w, g, m, v):
    rows, cols = w.shape
    tr = _row_tile(rows, cols, budget=1024 * 1024)

    def body(w_ref, g_ref, m_ref, v_ref, go_ref, d_ref, mo_ref, vo_ref):
        gv = g_ref[...]
        mn = ADAM_B1 * m_ref[...] + (1.0 - ADAM_B1) * gv
        vn = ADAM_B2 * v_ref[...] + (1.0 - ADAM_B2) * jnp.square(gv)
        m_hat = mn / (1.0 - ADAM_B1 ** ADAM_STEP)
        v_hat = vn / (1.0 - ADAM_B2 ** ADAM_STEP)
        d_ref[...] = -ADAM_LR * (m_hat / (jnp.sqrt(v_hat) + ADAM_EPS) + ADAM_WD * w_ref[...])
        go_ref[...] = gv
        mo_ref[...] = mn
        vo_ref[...] = vn

    spec = pl.BlockSpec((tr, cols), lambda r: (r, 0))
    return pl.pallas_call(body, name="adamw", grid=(rows // tr,), in_specs=[spec] * 4, out_specs=[spec] * 4,
                          out_shape=[jax.ShapeDtypeStruct((rows, cols), F32)] * 4,
                          compiler_params=_params(("parallel",)))(w, g, m, v)


def _pad_rope(w):
    z = jnp.zeros(w.shape[:-1] + (ROPE_HALF,), w.dtype)
    return jnp.concatenate([w[..., :ROPE_HALF], z, w[..., ROPE_HALF:], z], axis=-1)


def _unpad_rope(g):
    return jnp.concatenate([g[..., :ROPE_HALF], g[..., ROPE:ROPE + ROPE_HALF]], axis=-1)


def _unstack_cols(s):
    n, r, cs = s.shape
    return jnp.transpose(s, (1, 0, 2)).reshape(r, n * cs)


def _stack_cols(f):
    r, cfull = f.shape
    return jnp.transpose(f.reshape(r, N_CHIPS, cfull // N_CHIPS), (1, 0, 2))


def _small_shard(norm, conv):
    return jnp.concatenate([jnp.pad(norm, ((0, 15), (0, 0))), jnp.pad(conv, ((0, 13), (0, 0)))], axis=0)


def _flat_rows(a):
    return a.reshape(-1, LANES)


def _pack_small(arrs):
    return jnp.concatenate([_flat_rows(a.astype(F32)) for a in arrs], axis=0)


def _unpack_small(flat, like):
    out, r = [], 0
    for a in like:
        n = a.size // LANES
        out.append(flat[r:r + n].reshape(a.shape))
        r += n
    return out


def kernel(x, positions, e_norm_mix, e_w_in, e_q_norm, e_w_uq, e_kv_norm, e_w_ukv, e_v_norm, e_sgu_w, e_sgu_b, e_mla_out_norm, e_sgu_out_norm, e_w_out, o_norm_mix, o_w_in, o_conv_w, o_w_out, mlp_norm, mlp_w1, mlp_w2, final_norm, loss_target, m_e_norm_mix, m_e_w_in, m_e_q_norm, m_e_w_uq, m_e_kv_norm, m_e_w_ukv, m_e_v_norm, m_e_sgu_w, m_e_sgu_b, m_e_mla_out_norm, m_e_sgu_out_norm, m_e_w_out, m_o_norm_mix, m_o_w_in, m_o_conv_w, m_o_w_out, m_mlp_norm, m_mlp_w1, m_mlp_w2, m_final_norm, v_e_norm_mix, v_e_w_in, v_e_q_norm, v_e_w_uq, v_e_kv_norm, v_e_w_ukv, v_e_v_norm, v_e_sgu_w, v_e_sgu_b, v_e_mla_out_norm, v_e_sgu_out_norm, v_e_w_out, v_o_norm_mix, v_o_w_in, v_o_conv_w, v_o_w_out, v_mlp_norm, v_mlp_w1, v_mlp_w2, v_final_norm):
    t, d = x.shape[1], x.shape[2]
    ql, kvl = e_q_norm.shape[1], e_kv_norm.shape[1]
    groups = e_v_norm.shape[1]
    gw = groups * LANES
    heads = N_CHIPS * e_w_uq.shape[2] // (LANES + ROPE)
    hw = heads * LANES
    mix = hw + gw
    ei = N_CHIPS * e_w_in.shape[2]
    cd = N_CHIPS * o_conv_w.shape[2]
    ff = N_CHIPS * mlp_w1.shape[2]
    ffs = ff // N_CHIPS
    pi = 2 * gw + ql + kvl + LANES
    assert e_norm_mix.shape[0] == 1 and o_norm_mix.shape[0] == 1 and mlp_norm.shape[0] == 2
    assert ei == ql + kvl + ROPE + 2 * gw and cd == d and e_sgu_w.shape[2] == LANES
    assert (2 * gw) % ql == 0 and (2 * gw + ql) % kvl == 0 and t % LANES == 0
    scale = (LANES + ROPE) ** -0.5

    tr = min(256, t)
    tm = _pick(t, 1024, 8)
    xs = x.reshape(t, d)
    tgt = loss_target.reshape(t, d)
    cidx = lax.axis_index("c").astype(jnp.int32).reshape(1)
    chip = (2 * lax.axis_index("x") + lax.axis_index("y")).astype(jnp.int32).reshape(1)

    small_shard = _small_shard(o_norm_mix, o_conv_w[0])
    shards = [_cast_bf16(e_w_in[0]), _cast_bf16(e_w_uq[0]), _cast_bf16(e_w_ukv[0]), _cast_bf16(e_w_out[0]),
              _cast_bf16(o_w_in[0]), _cast_bf16(o_w_out[0]),
              _cast_bf16(mlp_w1.reshape(2 * d, ffs)), _cast_bf16(mlp_w2.reshape(2 * ffs, d)), small_shard]
    gathered = _all_gather([s.reshape(2, s.shape[0] // 2, s.shape[1]) for s in shards])
    gathered = [g.reshape(N_CHIPS, 2 * g.shape[2], g.shape[3]) for g in gathered]
    w_in_g, w_uq_g, w_ukv_g, w_eout_g, w_oin_g, w_oout_g, w1_g, w2_g, small_g = gathered
    w1_g = w1_g.reshape(N_CHIPS, 2, d, ffs)
    w2_g = w2_g.reshape(N_CHIPS, 2, ffs, d)

    full = _unstack_cols(w_in_g)
    c2, c3 = ql + kvl, ql + kvl + ROPE
    w_in_all = jnp.concatenate([full[:, c3:], full[:, :c2], _pad_rope(full[:, c2:c3])], axis=1)
    full = _unstack_cols(w_uq_g).reshape(ql, heads, LANES + ROPE)
    w_q_all = jnp.concatenate([full[:, :, :LANES].reshape(ql, hw), _pad_rope(full[:, :, LANES:]).reshape(ql, hw)], axis=1)
    full = _unstack_cols(w_ukv_g).reshape(kvl, heads, 2 * LANES)
    w_kv_all = jnp.concatenate([full[:, :, :LANES].reshape(kvl, hw), full[:, :, LANES:].reshape(kvl, hw)], axis=1)
    w_eout = w_eout_g.reshape(mix, d)
    w_oout = w_oout_g.reshape(cd, d)
    g_o = small_g[:, 0].reshape(1, d)
    conv_w = jnp.pad(jnp.transpose(small_g[:, 16:19], (1, 0, 2)).reshape(3, cd), ((0, 5), (0, 0)))

    g_e, g_q, g_kv = e_norm_mix, e_q_norm, e_kv_norm
    g_vn = e_v_norm.reshape(1, gw)
    sgu_w = e_sgu_w[0]
    sgu_b = jnp.broadcast_to(e_sgu_b[0][:, :, None], (groups, LANES, LANES))
    g_mla, g_sgu = e_mla_out_norm, e_sgu_out_norm
    g_m0, g_m1 = mlp_norm[0:1], mlp_norm[1:2]
    g_f = final_norm.reshape(1, d)

    inv_freq = ROPE_BASE ** (-jnp.arange(0, ROPE, 2, dtype=F32) / ROPE)
    zeros32 = jnp.zeros((ROPE_HALF,), F32)
    ones32 = jnp.ones((ROPE_HALF,), F32)
    invf = jnp.concatenate([inv_freq, zeros32, inv_freq, zeros32]).reshape(1, LANES)
    cmask = jnp.concatenate([ones32, zeros32, ones32, zeros32]).reshape(1, LANES)
    smask = jnp.concatenate([-ones32, zeros32, ones32, zeros32]).reshape(1, LANES)
    ctab, stab = _rope_tables(positions.reshape(t, 1).astype(F32), invf, cmask, smask, tr)

    def mlp_fwd(tag, xin, g, layer):
        hm = _norm_fwd("mlp_norm_" + tag, xin, g, tr)
        tn = _pick(ffs, 1024)
        a, act = _matmul("mlp_up_" + tag, Mat(hm, t, d), Mat(w1_g, d, ff, "colstack", (layer,)), "nn",
                         [_out(t, ff, BF16), _out(t, ff, BF16)], tm, tn, _pick(d, 1024),
                         epilogue=lambda z: (jnp.maximum(z, 0.0), jnp.square(jnp.maximum(z, 0.0))))
        xo, = _matmul("mlp_down_" + tag, Mat(act, t, ff), Mat(w2_g, ff, d, "rowstack", (layer,)), "nn",
                      [_out(t, d, F32)], tm, _pick(d, 1024), _pick(ffs, 1024),
                      epilogue=lambda z, r: (z + r,), extras=[Mat(xin, t, d)])
        return xo, hm, a, act

    def mlp_bwd(tag, dx, dxb, xin, g, layer, hm, a, act, dw1_prev, dw2_prev):
        tn = _pick(ffs, 1024)
        dz, = _matmul("mlp_dact_" + tag, Mat(dxb, t, d), Mat(w2_g, ff, d, "rowstack", (layer,)), "nt",
                      [_out(t, ff, BF16)], tm, tn, _pick(d, 1024),
                      epilogue=lambda z, av: (z * (2.0 * av.astype(F32)),), extras=[Mat(a, t, ff)])
        al = {} if dw1_prev is None else {2: 0}
        prev2 = () if dw2_prev is None else (dw2_prev,)
        prev1 = () if dw1_prev is None else (dw1_prev,)
        dw2, = _matmul("mlp_dw2_" + tag, Mat(act, t, ff), Mat(dxb, t, d), "tn",
                       [_out(ff, d, F32, "rowstack", (layer,), (N_CHIPS, 2, ffs, d))], tn, _pick(d, 1024), _pick(t, 512, 8),
                       aliases=al, alias_in=prev2)
        dw1, = _matmul("mlp_dw1_" + tag, Mat(hm, t, d), Mat(dz, t, ff), "tn",
                       [_out(d, ff, F32, "colstack", (layer,), (N_CHIPS, 2, d, ffs))], _pick(d, 1024), tn, _pick(t, 512, 8),
                       aliases=al, alias_in=prev1)
        dhm, = _matmul("mlp_dh_" + tag, Mat(dz, t, ff), Mat(w1_g, d, ff, "colstack", (layer,)), "nt",
                       [_out(t, d, F32)], tm, _pick(d, 1024), tn)
        dxo, dxob, dg = _norm_bwd("mlp_norm_bwd_" + tag, dhm, xin, g, dx, tr)
        return dxo, dxob, dg, dw1, dw2

    h0 = _norm_fwd("e_norm", xs, g_e, tr)
    proj, = _matmul("e_proj", Mat(h0, t, d), Mat(w_in_all, d, pi), "nn", [_out(t, pi, F32)], tm, _pick(pi, 1024), _pick(d, 1024))
    cq_cb, ckv_cb, kr_cb = 2 * gw // ql, (2 * gw + ql) // kvl, (2 * gw + ql + kvl) // LANES
    qn, kvn = _rowwise("qkv_norm", lambda a, b, ga, gb: (_rms(a, ga), _rms(b, gb)), t // tr,
                       [_rt(proj, tr, ql, cq_cb), _rt(proj, tr, kvl, ckv_cb), _whole(g_q), _whole(g_kv)],
                       [_rt_out(t, ql, BF16, tr), _rt_out(t, kvl, BF16, tr)])
    qfull, = _matmul("q_up", Mat(qn, t, ql), Mat(w_q_all, ql, 2 * hw), "nn", [_out(t, 2 * hw, F32)], tm, _pick(2 * hw, 1024), ql)
    kvall, = _matmul("kv_up", Mat(kvn, t, kvl), Mat(w_kv_all, kvl, 2 * hw), "nn", [_out(t, 2 * hw, BF16)], tm, _pick(2 * hw, 1024), kvl)
    qall, kr = _rope_fwd(qfull, proj, kr_cb, ctab, stab, heads, tr)
    att, lse = _attn_fwd(qall, kvall, kr, heads, scale, tr)
    rb = min(2 * LANES, t)
    sgu = _sgu_fwd(proj, g_vn, sgu_w, sgu_b, groups, rb)
    mixed = _rowwise("mix_norm", lambda a, s, ga, gs: jnp.concatenate([_rms(a, ga), _rms(s, gs)], axis=1), t // tr,
                     [_rt(att, tr), _rt(sgu, tr), _whole(g_mla), _whole(g_sgu)], [_rt_out(t, mix, BF16, tr)])[0]
    x1, = _matmul("e_out", Mat(mixed, t, mix), Mat(w_eout, mix, d), "nn", [_out(t, d, F32)], tm, _pick(d, 1024), _pick(mix, 1024),
                  epilogue=lambda z, r: (z + r,), extras=[Mat(xs, t, d)])
    x2, hm0, a0, act0 = mlp_fwd("0", x1, g_m0, 0)

    h1 = _norm_fwd("o_norm", x2, g_o, tr)
    oin = Mat(w_oin_g, d, 3 * cd, "colstack")
    tn_o = _pick(_gcd(3 * cd // N_CHIPS, cd), 512)
    proj3, = _matmul("o_proj", Mat(h1, t, d), oin, "nn", [_out(t, 3 * cd, F32, "colstack", (), (3, t, cd))], tm, tn_o, _pick(d, 1024))
    tc = _pick(cd, 256)
    bz = _conv_fwd(proj3, conv_w, tc)
    x3, = _matmul("o_out", Mat(bz, t, cd), Mat(w_oout, cd, d), "nn", [_out(t, d, F32)], tm, _pick(d, 1024), _pick(cd, 1024),
                  epilogue=lambda z, r: (z + r,), extras=[Mat(x2, t, d)])
    x4, hm1, a1, act1 = mlp_fwd("1", x3, g_m1, 1)

    def final_fn(xv, gv, tv):
        r = lax.rsqrt(jnp.mean(xv * xv, axis=-1, keepdims=True) + EPS)
        xh = xv * r
        err = xh * gv - tv
        dy = err * (1.0 / d)
        dxh = dy * gv
        dx = r * (dxh - xh * jnp.mean(dxh * xh, axis=-1, keepdims=True))
        sq = jnp.sum(err * err, axis=0, keepdims=True)
        part = sq[:, :LANES]
        for k in range(1, d // LANES):
            part = part + sq[:, k * LANES:(k + 1) * LANES]
        return dx, dx, part, jnp.sum(dy * xh, axis=0, keepdims=True)

    dx4, dx4b, loss_vec, dg_f = _rowwise("loss_final_norm", final_fn, t // tr, [_rt(x4, tr), _whole(g_f), _rt(tgt, tr)],
                                         [_rt_out(t, d, F32, tr), _rt_out(t, d, BF16, tr)],
                                         [jax.ShapeDtypeStruct((1, LANES), F32), jax.ShapeDtypeStruct((1, d), F32)])
    loss = lax.psum(0.5 * jnp.sum(loss_vec) / d, ("x", "y", "c"))

    dx3, dx3b, dg_m1, dw1, dw2 = mlp_bwd("1", dx4, dx4b, x3, g_m1, 1, hm1, a1, act1, None, None)

    dbz, = _matmul("o_out_dx", Mat(dx3b, t, d), Mat(w_oout, cd, d), "nt", [_out(t, cd, F32)], tm, _pick(cd, 1024), _pick(d, 1024))
    dw_oout, = _matmul("o_out_dw", Mat(bz, t, cd), Mat(dx3b, t, d), "tn", [_out(cd, d, F32)], _pick(cd, 1024), _pick(d, 1024), _pick(t, 512, 8))
    dproj3, dconv = _conv_bwd(proj3, conv_w, dbz, tc)
    dp3 = Mat(dproj3, t, 3 * cd, "colstack")
    dw_oin, = _matmul("o_proj_dw", Mat(h1, t, d), dp3, "tn", [_out(d, 3 * cd, F32, "colstack", (), (N_CHIPS, d, 3 * cd // N_CHIPS))],
                      _pick(d, 1024), tn_o, _pick(t, 512, 8))
    dh1, = _matmul("o_proj_dx", dp3, oin, "nt", [_out(t, d, F32)], tm, _pick(d, 1024), tn_o)
    dx2, dx2b, dg_o = _norm_bwd("o_norm_bwd", dh1, x2, g_o, dx3, tr)

    dx1, dx1b, dg_m0, dw1, dw2 = mlp_bwd("0", dx2, dx2b, x1, g_m0, 0, hm0, a0, act0, dw1, dw2)

    dmixed, = _matmul("e_out_dx", Mat(dx1b, t, d), Mat(w_eout, mix, d), "nt", [_out(t, mix, F32)], tm, _pick(mix, 1024), _pick(d, 1024))
    dw_eout, = _matmul("e_out_dw", Mat(mixed, t, mix), Mat(dx1b, t, d), "tn", [_out(mix, d, F32)], _pick(mix, 1024), _pick(d, 1024), _pick(t, 512, 8))

    def mixb_fn(dm, a, s, ga, gs):
        da, dga = _rms_bwd(dm[:, :hw], a, ga)
        dsg, dgs = _rms_bwd(dm[:, hw:], s, gs)
        prod = da * a
        delta = jnp.stack([jnp.broadcast_to(jnp.sum(prod[:, h * LANES:(h + 1) * LANES], axis=-1, keepdims=True), (tr, LANES))
                           for h in range(heads)], axis=0)
        return da, dsg, delta, dga, dgs

    da_b, dsgu, delta, dg_mla, dg_sgu = _rowwise(
        "mix_norm_bwd", mixb_fn, t // tr, [_rt(dmixed, tr), _rt(att, tr), _rt(sgu, tr), _whole(g_mla), _whole(g_sgu)],
        [_rt_out(t, hw, BF16, tr), _rt_out(t, gw, F32, tr),
         (jax.ShapeDtypeStruct((heads, t, LANES), F32), pl.BlockSpec((heads, tr, LANES), lambda i: (0, i, 0)))],
        [jax.ShapeDtypeStruct((1, hw), F32), jax.ShapeDtypeStruct((1, gw), F32)])

    du, dv, dsgu_w, dsgu_b8, dg_vn = _sgu_bwd(proj, dsgu, g_vn, sgu_w, sgu_b, groups, rb)
    dq1, dq2 = _attn_dq(qall, kvall, kr, da_b, lse, delta, heads, scale, tr)
    dk1, dvv, dkr_h = _attn_dkv(qall, kvall, kr, da_b, lse, delta, heads, scale, tr)
    dqfull, dkr = _rope_bwd(dq1, dq2, dkr_h, ctab, stab, heads, tr)
    dkvall = jnp.concatenate([dk1, dvv], axis=1)
    dw_q, = _matmul("q_up_dw", Mat(qn, t, ql), Mat(dqfull, t, 2 * hw), "tn", [_out(ql, 2 * hw, F32)], ql, _pick(2 * hw, 1024), _pick(t, 512, 8))
    dqn, = _matmul("q_up_dx", Mat(dqfull, t, 2 * hw), Mat(w_q_all, ql, 2 * hw), "nt", [_out(t, ql, F32)], tm, ql, _pick(2 * hw, 1024))
    dw_kv, = _matmul("kv_up_dw", Mat(kvn, t, kvl), Mat(dkvall, t, 2 * hw), "tn", [_out(kvl, 2 * hw, F32)], kvl, _pick(2 * hw, 1024), _pick(t, 512, 8))
    dkvn, = _matmul("kv_up_dx", Mat(dkvall, t, 2 * hw), Mat(w_kv_all, kvl, 2 * hw), "nt", [_out(t, kvl, F32)], tm, kvl, _pick(2 * hw, 1024))

    def qkvb_fn(da, db, a, b, ga, gb):
        dxa, dga = _rms_bwd(da, a, ga)
        dxb, dgb = _rms_bwd(db, b, gb)
        return dxa, dxb, dga, dgb

    dcq, dckv, dg_q, dg_kv = _rowwise(
        "qkv_norm_bwd", qkvb_fn, t // tr,
        [_rt(dqn, tr), _rt(dkvn, tr), _rt(proj, tr, ql, cq_cb), _rt(proj, tr, kvl, ckv_cb), _whole(g_q), _whole(g_kv)],
        [_rt_out(t, ql, BF16, tr), _rt_out(t, kvl, BF16, tr)],
        [jax.ShapeDtypeStruct((1, ql), F32), jax.ShapeDtypeStruct((1, kvl), F32)])
    dproj = jnp.concatenate([du, dv, dcq, dckv, dkr], axis=1)
    dw_in, = _matmul("e_proj_dw", Mat(h0, t, d), Mat(dproj, t, pi), "tn", [_out(d, pi, F32)], _pick(d, 1024), _pick(pi, 1024), _pick(t, 512, 8))
    dh0, = _matmul("e_proj_dx", Mat(dproj, t, pi), Mat(w_in_all, d, pi), "nt", [_out(t, d, F32)], tm, _pick(d, 1024), _pick(pi, 1024))
    dx0, _, dg_e = _norm_bwd("e_norm_bwd", dh0, xs, g_e, dx1, tr)

    gfull = jnp.concatenate([dw_in[:, 2 * gw:2 * gw + c2], _unpad_rope(dw_in[:, 2 * gw + c2:]), dw_in[:, :2 * gw]], axis=1)
    gw_in = _stack_cols(gfull)
    gq = jnp.concatenate([dw_q[:, :hw].reshape(ql, heads, LANES), _unpad_rope(dw_q[:, hw:].reshape(ql, heads, LANES))], axis=-1)
    gw_uq = _stack_cols(gq.reshape(ql, heads * (LANES + ROPE)))
    gkv = jnp.concatenate([dw_kv[:, :hw].reshape(kvl, heads, LANES), dw_kv[:, hw:].reshape(kvl, heads, LANES)], axis=-1)
    gw_ukv = _stack_cols(gkv.reshape(kvl, heads * 2 * LANES))
    dconv_s = jnp.transpose(dconv[:3].reshape(3, N_CHIPS, cd // N_CHIPS), (1, 0, 2))
    gsmall = jnp.concatenate([jnp.pad(dg_o.reshape(N_CHIPS, 1, d // N_CHIPS), ((0, 0), (0, 15), (0, 0))),
                              jnp.pad(dconv_s, ((0, 0), (0, 13), (0, 0)))], axis=1)
    stacked = [gw_in, gw_uq, gw_ukv, dw_eout.reshape(N_CHIPS, mix // N_CHIPS, d), dw_oin,
               dw_oout.reshape(N_CHIPS, cd // N_CHIPS, d), dw1.reshape(N_CHIPS, 2 * d, ffs), dw2.reshape(N_CHIPS, 2 * ffs, d), gsmall]
    g5 = [g.reshape(N_CHIPS, 2, g.shape[1] // 2, g.shape[2]) for g in stacked]
    from_sib = _pair_exchange(g5)
    part = [_pair_sum(a, b, cidx) for a, b in zip(g5, from_sib)]
    from_chips = _chip_exchange(part)
    half = [_chip_sum(p, r, chip) for p, r in zip(part, from_chips)]
    reduced = [r.reshape(2 * r.shape[1], r.shape[2]) for r in _pair_share(half)]
    r_in, r_uq, r_ukv, r_eout, r_oin, r_oout, r_w1, r_w2, r_small = reduced

    small_like = [e_norm_mix, e_q_norm, e_kv_norm, e_v_norm, e_sgu_w, e_sgu_b, e_mla_out_norm, e_sgu_out_norm, mlp_norm, final_norm]
    small_grads = [dg_e, dg_q, dg_kv, dg_vn, dsgu_w, dsgu_b8[:, 0, :], dg_mla, dg_sgu, jnp.concatenate([dg_m0, dg_m1], axis=0), dg_f]
    sflat = _pack_small(small_grads)
    pad = (-sflat.shape[0]) % 8
    sflat = jnp.pad(sflat, ((0, pad), (0, 0)))
    g_small = _sum_devices(_gather_all_devices(sflat))

    def padded(arrs):
        return jnp.pad(_pack_small(arrs), ((0, pad), (0, 0)))

    s_m = [m_e_norm_mix, m_e_q_norm, m_e_kv_norm, m_e_v_norm, m_e_sgu_w, m_e_sgu_b, m_e_mla_out_norm, m_e_sgu_out_norm, m_mlp_norm, m_final_norm]
    s_v = [v_e_norm_mix, v_e_q_norm, v_e_kv_norm, v_e_v_norm, v_e_sgu_w, v_e_sgu_b, v_e_mla_out_norm, v_e_sgu_out_norm, v_mlp_norm, v_final_norm]
    s_out = [_unpack_small(o, small_like) for o in _adamw(padded(small_like), g_small, padded(s_m), padded(s_v))]

    big = {
        "e_w_in": _adamw(e_w_in[0], r_in, m_e_w_in[0], v_e_w_in[0]),
        "e_w_uq": _adamw(e_w_uq[0], r_uq, m_e_w_uq[0], v_e_w_uq[0]),
        "e_w_ukv": _adamw(e_w_ukv[0], r_ukv, m_e_w_ukv[0], v_e_w_ukv[0]),
        "e_w_out": _adamw(e_w_out[0], r_eout, m_e_w_out[0], v_e_w_out[0]),
        "o_w_in": _adamw(o_w_in[0], r_oin, m_o_w_in[0], v_o_w_in[0]),
        "o_w_out": _adamw(o_w_out[0], r_oout, m_o_w_out[0], v_o_w_out[0]),
        "mlp_w1": _adamw(mlp_w1.reshape(2 * d, ffs), r_w1, m_mlp_w1.reshape(2 * d, ffs), v_mlp_w1.reshape(2 * d, ffs)),
        "mlp_w2": _adamw(mlp_w2.reshape(2 * ffs, d), r_w2, m_mlp_w2.reshape(2 * ffs, d), v_mlp_w2.reshape(2 * ffs, d)),
    }
    sm = _adamw(small_shard, r_small, _small_shard(m_o_norm_mix, m_o_conv_w[0]), _small_shard(v_o_norm_mix, v_o_conv_w[0]))

    names = ["e_norm_mix", "e_w_in", "e_q_norm", "e_w_uq", "e_kv_norm", "e_w_ukv", "e_v_norm", "e_sgu_w", "e_sgu_b",
             "e_mla_out_norm", "e_sgu_out_norm", "e_w_out", "o_norm_mix", "o_w_in", "o_conv_w", "o_w_out",
             "mlp_norm", "mlp_w1", "mlp_w2", "final_norm"]
    shapes = {"e_w_in": e_w_in.shape, "e_w_uq": e_w_uq.shape, "e_w_ukv": e_w_ukv.shape, "e_w_out": e_w_out.shape,
              "o_w_in": o_w_in.shape, "o_w_out": o_w_out.shape, "mlp_w1": mlp_w1.shape, "mlp_w2": mlp_w2.shape}
    small_names = ["e_norm_mix", "e_q_norm", "e_kv_norm", "e_v_norm", "e_sgu_w", "e_sgu_b", "e_mla_out_norm",
                   "e_sgu_out_norm", "mlp_norm", "final_norm"]

    def leaf(kind, name):
        if name in big:
            return big[name][kind].reshape(shapes[name])
        if name == "o_norm_mix":
            return sm[kind][0:1]
        if name == "o_conv_w":
            return sm[kind][16:19].reshape(o_conv_w.shape)
        return s_out[kind][small_names.index(name)]

    outs = [loss, dx0.reshape(x.shape)]
    for kind in range(4):
        outs += [leaf(kind, nm) for nm in names]
    return tuple(outs)


def _gcd(a, b):
    while b:
        a, b = b, a % b
    return a
```

```python
import functools

import jax
import jax.numpy as jnp
from jax import lax
from jax.experimental import pallas as pl
from jax.experimental.pallas import tpu as pltpu

F32 = jnp.float32
BF16 = jnp.bfloat16
MESH = pl.DeviceIdType.MESH

LANES = 128
ROPE = 64
ROPE_HALF = ROPE // 2
ROPE_BASE = 10000.0
EPS = 1e-6
N_CHIPS = 4
VMEM_LIMIT = 48 * 1024 * 1024
NEG = -1e30

ADAM_LR = 0.001
ADAM_B1 = 0.9
ADAM_B2 = 0.999
ADAM_EPS = 1e-08
ADAM_WD = 0.01
ADAM_STEP = 10


def _pick(n, target, step=LANES):
    best = None
    for t in range(step, min(n, target) + 1, step):
        if n % t == 0:
            best = t
    return best if best is not None else n


def _params(sem, vmem=VMEM_LIMIT):
    return pltpu.CompilerParams(dimension_semantics=sem, vmem_limit_bytes=vmem)


class Mat:
    def __init__(self, arr, rows, cols, kind="plain", lead=(), col_off=0, shape=None, dtype=None):
        self.arr, self.rows, self.cols, self.kind, self.lead, self.col_off = arr, rows, cols, kind, tuple(lead), col_off
        self.shape = tuple(arr.shape) if arr is not None else tuple(shape)
        self.dtype = arr.dtype if arr is not None else dtype

    def sds(self):
        return jax.ShapeDtypeStruct(self.shape, self.dtype)

    def spec(self, br, bc, gridmap):
        lead, nl = self.lead, len(self.lead)
        if self.kind == "plain":
            assert self.col_off % bc == 0 and self.rows % br == 0 and self.cols % bc == 0, (self.shape, br, bc)
            off = self.col_off // bc
            block = (None,) * nl + (br, bc)

            def phys(rb, cb):
                return lead + (rb, cb + off)
        elif self.kind == "colstack":
            cs = self.shape[-1]
            assert cs % bc == 0 and self.rows % br == 0, (self.shape, br, bc)
            q = cs // bc
            block = (None,) * (nl + 1) + (br, bc)

            def phys(rb, cb):
                return (cb // q,) + lead + (rb, cb % q)
        else:
            rs = self.shape[-2]
            assert rs % br == 0 and self.cols % bc == 0, (self.shape, br, bc)
            q = rs // br
            block = (None,) * (nl + 1) + (br, bc)

            def phys(rb, cb):
                return (rb // q,) + lead + (rb % q, cb)

        return pl.BlockSpec(block, lambda *g: phys(*gridmap(*g)))


def _matmul(name, a, b, mode, outs, tm, tn, tk, epilogue=None, extras=(), aliases=None, alias_in=()):
    if mode == "nn":
        m, k, n = a.rows, a.cols, b.cols
        a_spec = a.spec(tm, tk, lambda i, j, kk: (i, kk))
        b_spec = b.spec(tk, tn, lambda i, j, kk: (kk, j))
        dims = (((1,), (0,)), ((), ()))
    elif mode == "nt":
        m, k, n = a.rows, a.cols, b.rows
        a_spec = a.spec(tm, tk, lambda i, j, kk: (i, kk))
        b_spec = b.spec(tn, tk, lambda i, j, kk: (j, kk))
        dims = (((1,), (1,)), ((), ()))
    else:
        k, m, n = a.rows, a.cols, b.cols
        a_spec = a.spec(tk, tm, lambda i, j, kk: (kk, i))
        b_spec = b.spec(tk, tn, lambda i, j, kk: (kk, j))
        dims = (((0,), (0,)), ((), ()))
    assert m % tm == 0 and n % tn == 0 and k % tk == 0, (name, m, n, k, tm, tn, tk)
    grid = (m // tm, n // tn, k // tk)
    nk = grid[2]
    n_ex, n_out, n_al = len(extras), len(outs), len(alias_in)
    tile = lambda i, j, kk: (i, j)

    def body(a_ref, b_ref, *rest):
        ex = rest[:n_ex]
        out_refs = rest[n_ex + n_al:n_ex + n_al + n_out]
        acc = rest[-1]
        kk = pl.program_id(2)

        @pl.when(kk == 0)
        def _():
            acc[...] = jnp.zeros_like(acc)

        acc[...] += lax.dot_general(a_ref[...], b_ref[...], dims, preferred_element_type=F32)

        @pl.when(kk == nk - 1)
        def _():
            vals = epilogue(acc[...], *[e[...] for e in ex]) if epilogue is not None else (acc[...],)
            for o, v in zip(out_refs, vals):
                o[...] = v.astype(o.dtype)

    res = pl.pallas_call(
        body, name=name, grid=grid,
        in_specs=[a_spec, b_spec] + [e.spec(tm, tn, tile) for e in extras]
        + [pl.BlockSpec(memory_space=pl.ANY) for _ in alias_in],
        out_specs=[o.spec(tm, tn, tile) for o in outs],
        out_shape=[o.sds() for o in outs],
        scratch_shapes=[pltpu.VMEM((tm, tn), F32)],
        input_output_aliases=aliases or {},
        compiler_params=_params(("parallel", "parallel", "arbitrary")),
    )(a.arr, b.arr, *[e.arr for e in extras], *alias_in)
    return res


def _out(rows, cols, dtype, kind="plain", lead=(), shape=None):
    return Mat(None, rows, cols, kind, lead, shape=shape if shape is not None else (rows, cols), dtype=dtype)


def _rt(arr, tr, width=None, cb=0):
    width = arr.shape[1] if width is None else width
    return arr, pl.BlockSpec((tr, width), lambda i: (i, cb))


def _whole(arr):
    nd = arr.ndim
    return arr, pl.BlockSpec(arr.shape, lambda i: (0,) * nd)


def _rowwise(name, fn, n_steps, ins, outs, accs=()):
    n_in, n_out, n_acc = len(ins), len(outs), len(accs)

    def body(*refs):
        vals = fn(*[r[...] for r in refs[:n_in]])
        if not isinstance(vals, (tuple, list)):
            vals = (vals,)
        for ref, v in zip(refs[n_in:n_in + n_out], vals[:n_out]):
            ref[...] = v.astype(ref.dtype)
        if n_acc:
            acc_refs = refs[n_in + n_out:]

            @pl.when(pl.program_id(0) == 0)
            def _():
                for ref in acc_refs:
                    ref[...] = jnp.zeros_like(ref)

            for ref, v in zip(acc_refs, vals[n_out:]):
                ref[...] += v

    acc_specs = [pl.BlockSpec(s.shape, lambda i, nd=len(s.shape): (0,) * nd) for s in accs]
    res = pl.pallas_call(
        body, name=name, grid=(n_steps,),
        in_specs=[s for _, s in ins],
        out_specs=[s for _, s in outs] + acc_specs,
        out_shape=[o for o, _ in outs] + list(accs),
        compiler_params=_params(("arbitrary",) if n_acc else ("parallel",)),
    )(*[a for a, _ in ins])
    return res


def _rt_out(t, width, dtype, tr):
    return jax.ShapeDtypeStruct((t, width), dtype), pl.BlockSpec((tr, width), lambda i: (i, 0))


def _rms(x, g):
    r = lax.rsqrt(jnp.mean(x * x, axis=-1, keepdims=True) + EPS)
    return x * r * g


def _rms_bwd(dy, x, g):
    r = lax.rsqrt(jnp.mean(x * x, axis=-1, keepdims=True) + EPS)
    xh = x * r
    dxh = dy * g
    dx = r * (dxh - xh * jnp.mean(dxh * xh, axis=-1, keepdims=True))
    dg = jnp.sum(dy * xh, axis=0, keepdims=True)
    return dx, dg


def _gelu(x):
    k = 0.7978845608028654
    th = jnp.tanh(k * (x + 0.044715 * (x * x * x)))
    return x * (0.5 * (1.0 + th))


def _gelu_grad(x):
    k = 0.7978845608028654
    x2 = x * x
    th = jnp.tanh(k * (x + 0.044715 * (x2 * x)))
    return 0.5 * (1.0 + th) + 0.5 * x * (1.0 - th * th) * (k * (1.0 + 3.0 * 0.044715 * x2))


def _norm_fwd(name, x, g, tr):
    t, d = x.shape
    return _rowwise(name, lambda xv, gv: _rms(xv, gv), t // tr, [_rt(x, tr), _whole(g)], [_rt_out(t, d, BF16, tr)])[0]


def _norm_bwd(name, dh, x, g, dres, tr):
    t, d = x.shape

    def fn(dhv, xv, gv, drv):
        dx, dg = _rms_bwd(dhv, xv, gv)
        dx = dx + drv
        return dx, dx, dg

    return _rowwise(name, fn, t // tr, [_rt(dh, tr), _rt(x, tr), _whole(g), _rt(dres, tr)],
                    [_rt_out(t, d, F32, tr), _rt_out(t, d, BF16, tr)], [jax.ShapeDtypeStruct((1, d), F32)])


def _rope_tables(posf, invf, cmask, smask, tr):
    t = posf.shape[0]

    def fn(p, f, cm, sm):
        ang = p * f
        return jnp.cos(ang) * cm, jnp.sin(ang) * sm

    return _rowwise("rope_tables", fn, t // tr, [_rt(posf, tr), _whole(invf), _whole(cmask), _whole(smask)],
                    [_rt_out(t, LANES, F32, tr), _rt_out(t, LANES, F32, tr)])


def _rot(v, c, s):
    return v * c + pltpu.roll(v, ROPE, axis=1) * s


def _rot_bwd(dv, c, s):
    return dv * c + pltpu.roll(dv * s, ROPE, axis=1)


def _rope_fwd(qfull, proj, kr_cb, ctab, stab, heads, tr):
    t = qfull.shape[0]
    hw = heads * LANES

    def fn(q, kr, c, s):
        parts = [q[:, :hw]] + [_rot(q[:, hw + h * LANES: hw + (h + 1) * LANES], c, s) for h in range(heads)]
        return jnp.concatenate(parts, axis=1), _rot(kr, c, s)

    return _rowwise("rope_fwd", fn, t // tr, [_rt(qfull, tr), _rt(proj, tr, LANES, kr_cb), _rt(ctab, tr), _rt(stab, tr)],
                    [_rt_out(t, 2 * hw, BF16, tr), _rt_out(t, LANES, BF16, tr)])


def _rope_bwd(dq1, dq2, dkr_h, ctab, stab, heads, tr):
    t = dq1.shape[0]
    hw = heads * LANES

    def fn(a, b, dk, c, s):
        parts = [a] + [_rot_bwd(b[:, h * LANES:(h + 1) * LANES], c, s) for h in range(heads)]
        dks = dk[0]
        for h in range(1, heads):
            dks = dks + dk[h]
        return jnp.concatenate(parts, axis=1), _rot_bwd(dks, c, s)

    dk_spec = pl.BlockSpec((heads, tr, LANES), lambda i: (0, i, 0))
    return _rowwise("rope_bwd", fn, t // tr, [_rt(dq1, tr), _rt(dq2, tr), (dkr_h, dk_spec), _rt(ctab, tr), _rt(stab, tr)],
                    [_rt_out(t, 2 * hw, BF16, tr), _rt_out(t, LANES, BF16, tr)])


def _dot_nt(a, b):
    return lax.dot_general(a, b, (((1,), (1,)), ((), ())), preferred_element_type=F32)


def _dot_tn(a, b):
    return lax.dot_general(a, b, (((0,), (0,)), ((), ())), preferred_element_type=F32)


def _dot(a, b):
    return jnp.dot(a, b, preferred_element_type=F32)


def _causal(i, j, tq, tk):
    rows = i * tq + lax.broadcasted_iota(jnp.int32, (tq, tk), 0)
    cols = j * tk + lax.broadcasted_iota(jnp.int32, (tq, tk), 1)
    return cols <= rows


def _attn_fwd(qall, kvall, kr, heads, scale, tq):
    t = qall.shape[0]
    nq = t // tq
    tk = tq

    def body(qn_ref, qr_ref, kn_ref, v_ref, kr_ref, o_ref, lse_ref, m_ref, l_ref, acc_ref):
        i = pl.program_id(1)
        qn, qr = qn_ref[...], qr_ref[...]
        m_ref[...] = jnp.full_like(m_ref, NEG)
        l_ref[...] = jnp.zeros_like(l_ref)
        acc_ref[...] = jnp.zeros_like(acc_ref)

        def step(j, carry):
            ks = pl.multiple_of(j * tk, tk)
            s = (_dot_nt(qn, kn_ref[pl.ds(ks, tk), :]) + _dot_nt(qr, kr_ref[pl.ds(ks, tk), :])) * scale
            s = jnp.where(_causal(i, j, tq, tk), s, NEG)
            m_prev = m_ref[...]
            m_new = jnp.maximum(m_prev, jnp.max(s, axis=-1, keepdims=True))
            p = jnp.exp(s - m_new[:, :1])
            alpha = jnp.exp(m_prev - m_new)
            l_ref[...] = alpha * l_ref[...] + jnp.sum(p, axis=-1, keepdims=True)
            acc_ref[...] = alpha * acc_ref[...] + _dot(p.astype(BF16), v_ref[pl.ds(ks, tk), :])
            m_ref[...] = m_new
            return carry

        lax.fori_loop(0, i + 1, step, 0)
        o_ref[...] = acc_ref[...] / l_ref[...]
        lse_ref[...] = m_ref[...] + jnp.log(l_ref[...])

    return pl.pallas_call(
        body, name="attn_fwd", grid=(heads, nq),
        in_specs=[pl.BlockSpec((tq, LANES), lambda h, i: (i, h)),
                  pl.BlockSpec((tq, LANES), lambda h, i: (i, heads + h)),
                  pl.BlockSpec((t, LANES), lambda h, i: (0, h)),
                  pl.BlockSpec((t, LANES), lambda h, i: (0, heads + h)),
                  pl.BlockSpec((t, LANES), lambda h, i: (0, 0))],
        out_specs=[pl.BlockSpec((tq, LANES), lambda h, i: (i, h)),
                   pl.BlockSpec((None, tq, LANES), lambda h, i: (h, i, 0))],
        out_shape=[jax.ShapeDtypeStruct((t, heads * LANES), F32), jax.ShapeDtypeStruct((heads, t, LANES), F32)],
        scratch_shapes=[pltpu.VMEM((tq, LANES), F32)] * 3,
        compiler_params=_params(("parallel", "arbitrary")),
    )(qall, qall, kvall, kvall, kr)


def _attn_dq(qall, kvall, kr, do, lse, delta, heads, scale, tq):
    t = qall.shape[0]
    nq = t // tq
    tk = tq

    def body(qn_ref, qr_ref, kn_ref, v_ref, kr_ref, do_ref, lse_ref, dl_ref, dq1_ref, dq2_ref, a1_ref, a2_ref):
        i = pl.program_id(1)
        qn, qr, do_v = qn_ref[...], qr_ref[...], do_ref[...]
        lse_v, dl_v = lse_ref[...][:, :1], dl_ref[...][:, :1]
        a1_ref[...] = jnp.zeros_like(a1_ref)
        a2_ref[...] = jnp.zeros_like(a2_ref)

        def step(j, carry):
            ks = pl.multiple_of(j * tk, tk)
            k1, k2 = kn_ref[pl.ds(ks, tk), :], kr_ref[pl.ds(ks, tk), :]
            s = (_dot_nt(qn, k1) + _dot_nt(qr, k2)) * scale
            p = jnp.where(_causal(i, j, tq, tk), jnp.exp(s - lse_v), 0.0)
            dp = _dot_nt(do_v, v_ref[pl.ds(ks, tk), :])
            ds = (p * (dp - dl_v) * scale).astype(BF16)
            a1_ref[...] += _dot(ds, k1)
            a2_ref[...] += _dot(ds, k2)
            return carry

        lax.fori_loop(0, i + 1, step, 0)
        dq1_ref[...] = a1_ref[...]
        dq2_ref[...] = a2_ref[...]

    qblk = lambda off: pl.BlockSpec((tq, LANES), lambda h, i: (i, off + h))
    full = lambda off: pl.BlockSpec((t, LANES), lambda h, i: (0, off + h))
    stat = pl.BlockSpec((None, tq, LANES), lambda h, i: (h, i, 0))
    return pl.pallas_call(
        body, name="attn_dq", grid=(heads, nq),
        in_specs=[qblk(0), qblk(heads), full(0), full(heads), pl.BlockSpec((t, LANES), lambda h, i: (0, 0)),
                  qblk(0), stat, stat],
        out_specs=[qblk(0), qblk(0)],
        out_shape=[jax.ShapeDtypeStruct((t, heads * LANES), F32)] * 2,
        scratch_shapes=[pltpu.VMEM((tq, LANES), F32)] * 2,
        compiler_params=_params(("parallel", "arbitrary")),
    )(qall, qall, kvall, kvall, kr, do, lse, delta)


def _attn_dkv(qall, kvall, kr, do, lse, delta, heads, scale, tq):
    t = qall.shape[0]
    nq = t // tq
    tk = tq

    def body(qn_ref, qr_ref, kn_ref, v_ref, kr_ref, do_ref, lse_ref, dl_ref, dk_ref, dv_ref, dkr_ref, ak, av, akr):
        j = pl.program_id(1)
        k1, k2, vv = kn_ref[...], kr_ref[...], v_ref[...]
        ak[...] = jnp.zeros_like(ak)
        av[...] = jnp.zeros_like(av)
        akr[...] = jnp.zeros_like(akr)

        def step(i, carry):
            qs = pl.multiple_of(i * tq, tq)
            qn, qr, do_v = qn_ref[pl.ds(qs, tq), :], qr_ref[pl.ds(qs, tq), :], do_ref[pl.ds(qs, tq), :]
            lse_v, dl_v = lse_ref[pl.ds(qs, tq), :][:, :1], dl_ref[pl.ds(qs, tq), :][:, :1]
            s = (_dot_nt(qn, k1) + _dot_nt(qr, k2)) * scale
            p = jnp.where(_causal(i, j, tq, tk), jnp.exp(s - lse_v), 0.0)
            dp = _dot_nt(do_v, vv)
            ds = (p * (dp - dl_v) * scale).astype(BF16)
            av[...] += _dot_tn(p.astype(BF16), do_v)
            ak[...] += _dot_tn(ds, qn)
            akr[...] += _dot_tn(ds, qr)
            return carry

        lax.fori_loop(j, nq, step, 0)
        dk_ref[...] = ak[...].astype(dk_ref.dtype)
        dv_ref[...] = av[...].astype(dv_ref.dtype)
        dkr_ref[...] = akr[...]

    kblk = lambda off: pl.BlockSpec((tk, LANES), lambda h, j: (j, off + h))
    full = lambda off: pl.BlockSpec((t, LANES), lambda h, j: (0, off + h))
    stat = pl.BlockSpec((None, t, LANES), lambda h, j: (h, 0, 0))
    return pl.pallas_call(
        body, name="attn_dkv", grid=(heads, nq),
        in_specs=[full(0), full(heads), kblk(0), kblk(heads), pl.BlockSpec((tk, LANES), lambda h, j: (j, 0)),
                  full(0), stat, stat],
        out_specs=[kblk(0), kblk(0), pl.BlockSpec((None, tk, LANES), lambda h, j: (h, j, 0))],
        out_shape=[jax.ShapeDtypeStruct((t, heads * LANES), BF16)] * 2 + [jax.ShapeDtypeStruct((heads, t, LANES), F32)],
        scratch_shapes=[pltpu.VMEM((tk, LANES), F32)] * 3,
        compiler_params=_params(("parallel", "arbitrary")),
    )(qall, qall, kvall, kvall, kr, do, lse, delta)


def _tril():
    return lax.broadcasted_iota(jnp.int32, (LANES, LANES), 0) >= lax.broadcasted_iota(jnp.int32, (LANES, LANES), 1)


def _group_norm(vg):
    mu = jnp.mean(vg, axis=-1, keepdims=True)
    vc = vg - mu
    rs = lax.rsqrt(jnp.mean(vc * vc, axis=-1, keepdims=True) + EPS)
    return vc * rs, rs


def _sgu_fwd(proj, gain, w, bias, groups, rb):
    t = proj.shape[0]
    gw = groups * LANES
    cpb = rb // LANES

    def body(u_ref, v_ref, gain_ref, w_ref, b_ref, s_ref):
        tril = _tril()
        for g in range(groups):
            wt = jnp.where(tril, w_ref[g], 0.0).astype(BF16)
            cols = slice(g * LANES, (g + 1) * LANES)
            for ci in range(cpb):
                rows = slice(ci * LANES, (ci + 1) * LANES)
                ug = _gelu(u_ref[rows, cols])
                vh, _ = _group_norm(_gelu(v_ref[rows, cols]))
                vn = vh * gain_ref[:, cols]
                y = _dot(wt, vn.astype(BF16)) + b_ref[g]
                s_ref[rows, cols] = ug * y

    return pl.pallas_call(
        body, name="sgu_fwd", grid=(t // rb,),
        in_specs=[pl.BlockSpec((rb, gw), lambda i: (i, 0)), pl.BlockSpec((rb, gw), lambda i: (i, 1)),
                  pl.BlockSpec((1, gw), lambda i: (0, 0)),
                  pl.BlockSpec((groups, LANES, LANES), lambda i: (0, 0, 0)),
                  pl.BlockSpec((groups, LANES, LANES), lambda i: (0, 0, 0))],
        out_specs=pl.BlockSpec((rb, gw), lambda i: (i, 0)),
        out_shape=jax.ShapeDtypeStruct((t, gw), F32),
        compiler_params=_params(("parallel",)),
    )(proj, proj, gain, w, bias)


def _sgu_bwd(proj, ds, gain, w, bias, groups, rb):
    t = proj.shape[0]
    gw = groups * LANES
    cpb = rb // LANES
    n_steps = t // rb

    def body(u_ref, v_ref, ds_ref, gain_ref, w_ref, b_ref, du_ref, dv_ref, dw_ref, db_ref, dg_ref, dy_acc):
        step = pl.program_id(0)

        @pl.when(step == 0)
        def _():
            dw_ref[...] = jnp.zeros_like(dw_ref)
            dy_acc[...] = jnp.zeros_like(dy_acc)
            dg_ref[...] = jnp.zeros_like(dg_ref)

        tril = _tril()
        for g in range(groups):
            wt = jnp.where(tril, w_ref[g], 0.0).astype(BF16)
            cols = slice(g * LANES, (g + 1) * LANES)
            gain_g = gain_ref[:, cols]
            for ci in range(cpb):
                rows = slice(ci * LANES, (ci + 1) * LANES)
                u_raw, v_raw, ds_v = u_ref[rows, cols], v_ref[rows, cols], ds_ref[rows, cols]
                ug = _gelu(u_raw)
                vh, rs = _group_norm(_gelu(v_raw))
                vn = (vh * gain_g).astype(BF16)
                y = _dot(wt, vn) + b_ref[g]
                dy = ds_v * ug
                dyb = dy.astype(BF16)
                du_ref[rows, cols] = (ds_v * y * _gelu_grad(u_raw)).astype(du_ref.dtype)
                dy_acc[g] += dy
                dw_ref[g] += _dot_nt(dyb, vn)
                dvn = _dot_tn(wt, dyb)
                dg_ref[:, cols] += jnp.sum(dvn * vh, axis=0, keepdims=True)
                dvh = dvn * gain_g
                dvg = rs * (dvh - jnp.mean(dvh, axis=-1, keepdims=True)
                            - vh * jnp.mean(dvh * vh, axis=-1, keepdims=True))
                dv_ref[rows, cols] = (dvg * _gelu_grad(v_raw)).astype(dv_ref.dtype)

        @pl.when(step == n_steps - 1)
        def _():
            ones = jnp.ones((8, LANES), F32)
            for g in range(groups):
                dw_ref[g] = jnp.where(tril, dw_ref[g], 0.0)
                db_ref[g] = lax.dot_general(ones, dy_acc[g], (((1,), (1,)), ((), ())),
                                            precision=lax.Precision.HIGHEST, preferred_element_type=F32)

    blk = lambda cb: pl.BlockSpec((rb, gw), lambda i: (i, cb))
    whole3 = pl.BlockSpec((groups, LANES, LANES), lambda i: (0, 0, 0))
    return pl.pallas_call(
        body, name="sgu_bwd", grid=(n_steps,),
        in_specs=[blk(0), blk(1), blk(0), pl.BlockSpec((1, gw), lambda i: (0, 0)), whole3, whole3],
        out_specs=[blk(0), blk(0), whole3, pl.BlockSpec((groups, 8, LANES), lambda i: (0, 0, 0)),
                   pl.BlockSpec((1, gw), lambda i: (0, 0))],
        out_shape=[jax.ShapeDtypeStruct((t, gw), BF16), jax.ShapeDtypeStruct((t, gw), BF16),
                   jax.ShapeDtypeStruct((groups, LANES, LANES), F32), jax.ShapeDtypeStruct((groups, 8, LANES), F32),
                   jax.ShapeDtypeStruct((1, gw), F32)],
        scratch_shapes=[pltpu.VMEM((groups, LANES, LANES), F32)],
        compiler_params=_params(("arbitrary",)),
    )(proj, proj, ds, gain, w, bias)


def _shift_down(z, s):
    rows = lax.broadcasted_iota(jnp.int32, z.shape, 0)
    return jnp.where(rows >= s, pltpu.roll(z, s, axis=0), 0.0)


def _shift_up(z, s):
    n = z.shape[0]
    rows = lax.broadcasted_iota(jnp.int32, z.shape, 0)
    return jnp.where(rows < n - s, pltpu.roll(z, n - s, axis=0), 0.0)


def _conv_fwd(proj3, cw, tc):
    _, t, cd = proj3.shape

    def body(p_ref, w_ref, o_ref):
        z = p_ref[1] * p_ref[2]
        w = w_ref[...]
        zc = w[2:3] * z + w[1:2] * _shift_down(z, 1) + w[0:1] * _shift_down(z, 2)
        o_ref[...] = (p_ref[0] * zc).astype(o_ref.dtype)

    return pl.pallas_call(
        body, name="conv_fwd", grid=(cd // tc,),
        in_specs=[pl.BlockSpec((3, t, tc), lambda j: (0, 0, j)), pl.BlockSpec((8, tc), lambda j: (0, j))],
        out_specs=pl.BlockSpec((t, tc), lambda j: (0, j)),
        out_shape=jax.ShapeDtypeStruct((t, cd), BF16),
        compiler_params=_params(("parallel",)),
    )(proj3, cw)


def _conv_bwd(proj3, cw, dbz, tc):
    _, t, cd = proj3.shape

    def body(p_ref, w_ref, d_ref, o_ref, dw_ref):
        b, c, xin = p_ref[0], p_ref[1], p_ref[2]
        w = w_ref[...]
        z = c * xin
        z1, z2 = _shift_down(z, 1), _shift_down(z, 2)
        zc = w[2:3] * z + w[1:2] * z1 + w[0:1] * z2
        d = d_ref[...]
        dzc = d * b
        dz = w[2:3] * dzc + w[1:2] * _shift_up(dzc, 1) + w[0:1] * _shift_up(dzc, 2)
        o_ref[0] = (d * zc).astype(o_ref.dtype)
        o_ref[1] = (dz * xin).astype(o_ref.dtype)
        o_ref[2] = (dz * c).astype(o_ref.dtype)
        row = lax.broadcasted_iota(jnp.int32, (8, tc), 0)
        dw0 = jnp.sum(dzc * z2, axis=0, keepdims=True)
        dw1 = jnp.sum(dzc * z1, axis=0, keepdims=True)
        dw2 = jnp.sum(dzc * z, axis=0, keepdims=True)
        dw_ref[...] = jnp.where(row == 0, dw0, 0.0) + jnp.where(row == 1, dw1, 0.0) + jnp.where(row == 2, dw2, 0.0)

    return pl.pallas_call(
        body, name="conv_bwd", grid=(cd // tc,),
        in_specs=[pl.BlockSpec((3, t, tc), lambda j: (0, 0, j)), pl.BlockSpec((8, tc), lambda j: (0, j)),
                  pl.BlockSpec((t, tc), lambda j: (0, j))],
        out_specs=[pl.BlockSpec((3, t, tc), lambda j: (0, 0, j)), pl.BlockSpec((8, tc), lambda j: (0, j))],
        out_shape=[jax.ShapeDtypeStruct((3, t, cd), BF16), jax.ShapeDtypeStruct((8, cd), F32)],
        compiler_params=_params(("parallel",)),
    )(proj3, cw, dbz)


def _place():
    x, y, c = lax.axis_index("x"), lax.axis_index("y"), lax.axis_index("c")
    chips = [(1 - x, y), (x, 1 - y), (1 - x, 1 - y)]
    return x, y, c, chips


def _any_specs(n):
    return [pl.BlockSpec(memory_space=pl.ANY) for _ in range(n)]


def _all_gather(bufs):
    n = len(bufs)

    def body(*refs):
        ins, outs = refs[:n], refs[n:2 * n]
        ici_send, ici_recv, d2d_send, d2d_recv = refs[2 * n:]
        x, y, c, chips = _place()
        me = 2 * x + y
        sib = (x, y, 1 - c)

        def ici(i, k, slot, dev):
            return pltpu.make_async_remote_copy(src_ref=ins[i].at[slot, c], dst_ref=outs[i].at[slot, c],
                                                send_sem=ici_send.at[3 * i + k], recv_sem=ici_recv.at[3 * i + k],
                                                device_id=dev, device_id_type=MESH)

        def d2d(i, k, slot, half):
            return pltpu.make_async_remote_copy(src_ref=ins[i].at[slot, half], dst_ref=outs[i].at[slot, half],
                                                send_sem=d2d_send.at[3 * i + k], recv_sem=d2d_recv.at[3 * i + k],
                                                device_id=sib, device_id_type=MESH)

        sends = [ici(i, k, me, (*chip, c)) for i in range(n) for k, chip in enumerate(chips)]
        for cp in sends:
            cp.start()
        passed = []
        for i in range(n):
            for k, (px, py) in enumerate(chips):
                ici(i, k, 2 * px + py, (px, py, c)).wait_recv()
                fwd = d2d(i, k, 2 * px + py, c)
                fwd.start()
                passed.append(fwd)
        for i in range(n):
            for k, (px, py) in enumerate(chips):
                d2d(i, k, 2 * px + py, 1 - c).wait_recv()
        for cp in sends + passed:
            cp.wait_send()

    return pl.pallas_call(
        body, name="all_gather_weights",
        in_specs=_any_specs(n), out_specs=_any_specs(n),
        out_shape=[jax.ShapeDtypeStruct(b.shape, b.dtype) for b in bufs],
        scratch_shapes=[pltpu.SemaphoreType.DMA((3 * n,))] * 4,
        input_output_aliases={i: i for i in range(n)},
        compiler_params=pltpu.CompilerParams(has_side_effects=True),
    )(*bufs)


def _pair_exchange(entries):
    n = len(entries)

    def body(*refs):
        ins, outs = refs[:n], refs[n:2 * n]
        send, recv = refs[2 * n:]
        x, y, c, _ = _place()
        sib = (x, y, 1 - c)

        def cp(i, j):
            return pltpu.make_async_remote_copy(src_ref=ins[i].at[j, 1 - c], dst_ref=outs[i].at[j],
                                                send_sem=send.at[N_CHIPS * i + j], recv_sem=recv.at[N_CHIPS * i + j],
                                                device_id=sib, device_id_type=MESH)

        cps = [cp(i, j) for i in range(n) for j in range(N_CHIPS)]
        for d in cps:
            d.start()
        for d in cps:
            d.wait_recv()
        for d in cps:
            d.wait_send()

    return pl.pallas_call(
        body, name="grad_pair_exchange",
        in_specs=_any_specs(n), out_specs=_any_specs(n),
        out_shape=[jax.ShapeDtypeStruct((N_CHIPS,) + e.shape[2:], e.dtype) for e in entries],
        scratch_shapes=[pltpu.SemaphoreType.DMA((N_CHIPS * n,))] * 2,
        compiler_params=pltpu.CompilerParams(has_side_effects=True),
    )(*entries)


def _chip_exchange(entries):
    n = len(entries)

    def body(*refs):
        ins, outs = refs[:n], refs[n:2 * n]
        send, recv = refs[2 * n:]
        x, y, c, chips = _place()

        def cp(i, k, px, py):
            return pltpu.make_async_remote_copy(src_ref=ins[i].at[2 * px + py], dst_ref=outs[i].at[k],
                                                send_sem=send.at[3 * i + k], recv_sem=recv.at[3 * i + k],
                                                device_id=(px, py, c), device_id_type=MESH)

        cps = [cp(i, k, px, py) for i in range(n) for k, (px, py) in enumerate(chips)]
        for d in cps:
            d.start()
        for d in cps:
            d.wait_recv()
        for d in cps:
            d.wait_send()

    return pl.pallas_call(
        body, name="grad_chip_exchange",
        in_specs=_any_specs(n), out_specs=_any_specs(n),
        out_shape=[jax.ShapeDtypeStruct((3,) + e.shape[1:], e.dtype) for e in entries],
        scratch_shapes=[pltpu.SemaphoreType.DMA((3 * n,))] * 2,
        compiler_params=pltpu.CompilerParams(has_side_effects=True),
    )(*entries)


def _pair_share(bufs):
    n = len(bufs)

    def body(*refs):
        ins, outs = refs[:n], refs[n:2 * n]
        send, recv = refs[2 * n:]
        x, y, c, _ = _place()
        sib = (x, y, 1 - c)

        def cp(i, half):
            return pltpu.make_async_remote_copy(src_ref=ins[i].at[half], dst_ref=outs[i].at[half],
                                                send_sem=send.at[i], recv_sem=recv.at[i],
                                                device_id=sib, device_id_type=MESH)

        cps = [cp(i, c) for i in range(n)]
        for d in cps:
            d.start()
        for i in range(n):
            cp(i, 1 - c).wait_recv()
        for d in cps:
            d.wait_send()

    return pl.pallas_call(
        body, name="grad_pair_share",
        in_specs=_any_specs(n), out_specs=_any_specs(n),
        out_shape=[jax.ShapeDtypeStruct(b.shape, b.dtype) for b in bufs],
        scratch_shapes=[pltpu.SemaphoreType.DMA((n,))] * 2,
        input_output_aliases={i: i for i in range(n)},
        compiler_params=pltpu.CompilerParams(has_side_effects=True),
    )(*bufs)


def _gather_all_devices(v):
    def body(v_ref, o_ref, send, recv, loc):
        x, y, c, _ = _place()
        me = 4 * x + 2 * y + c
        own = pltpu.make_async_copy(v_ref, o_ref.at[me], loc)
        own.start()
        rels = [(fx, fy, fc) for fx in (0, 1) for fy in (0, 1) for fc in (0, 1)][1:]

        def peer(fx, fy, fc):
            return (x + fx - 2 * x * fx, y + fy - 2 * y * fy, c + fc - 2 * c * fc)

        def cp(r, slot, dev):
            return pltpu.make_async_remote_copy(src_ref=v_ref, dst_ref=o_ref.at[slot], send_sem=send.at[r],
                                                recv_sem=recv.at[r], device_id=dev, device_id_type=MESH)

        cps = [cp(r, me, peer(*f)) for r, f in enumerate(rels)]
        for d in cps:
            d.start()
        for r, f in enumerate(rels):
            px, py, pc = peer(*f)
            cp(r, 4 * px + 2 * py + pc, (px, py, pc)).wait_recv()
        for d in cps:
            d.wait_send()
        own.wait()

    return pl.pallas_call(
        body, name="gather_small_grads",
        in_specs=_any_specs(1), out_specs=_any_specs(1)[0],
        out_shape=jax.ShapeDtypeStruct((8,) + v.shape, v.dtype),
        scratch_shapes=[pltpu.SemaphoreType.DMA((7,)), pltpu.SemaphoreType.DMA((7,)), pltpu.SemaphoreType.DMA],
        compiler_params=pltpu.CompilerParams(has_side_effects=True),
    )(v)


def _row_tile(rows, cols, itemsize=4, budget=2 * 1024 * 1024):
    best = None
    for t in range(8, rows + 1, 8):
        if rows % t == 0 and t * cols * itemsize <= budget:
            best = t
    return best if best is not None else rows


def _pair_sum(g5, gsib, cidx):
    _, _, rh, cols = g5.shape
    tr = _row_tile(rh, cols)

    def body(c_ref, a_ref, b_ref, o_ref):
        o_ref[...] = (a_ref[...] + b_ref[...]).astype(o_ref.dtype)

    grid_spec = pltpu.PrefetchScalarGridSpec(
        num_scalar_prefetch=1, grid=(N_CHIPS, rh // tr),
        in_specs=[pl.BlockSpec((None, None, tr, cols), lambda j, r, c_ref: (j, c_ref[0], r, 0)),
                  pl.BlockSpec((None, tr, cols), lambda j, r, c_ref: (j, r, 0))],
        out_specs=pl.BlockSpec((None, tr, cols), lambda j, r, c_ref: (j, r, 0)))
    return pl.pallas_call(body, name="grad_pair_sum", grid_spec=grid_spec,
                          out_shape=jax.ShapeDtypeStruct((N_CHIPS, rh, cols), BF16),
                          compiler_params=_params(("parallel", "parallel")))(cidx, g5, gsib)


def _chip_sum(part, recv, place):
    _, rh, cols = part.shape
    tr = _row_tile(rh, cols)

    def body(p_ref, a_ref, b_ref, o_ref):
        acc = a_ref[...].astype(F32)
        for k in range(3):
            acc = acc + b_ref[k].astype(F32)
        o_ref[...] = acc

    grid_spec = pltpu.PrefetchScalarGridSpec(
        num_scalar_prefetch=1, grid=(rh // tr,),
        in_specs=[pl.BlockSpec((None, tr, cols), lambda r, p_ref: (p_ref[0], r, 0)),
                  pl.BlockSpec((3, tr, cols), lambda r, p_ref: (0, r, 0))],
        out_specs=pl.BlockSpec((None, tr, cols), lambda r, p_ref: (p_ref[1], r, 0)))
    return pl.pallas_call(body, name="grad_chip_sum", grid_spec=grid_spec,
                          out_shape=jax.ShapeDtypeStruct((2, rh, cols), F32),
                          compiler_params=_params(("parallel",)))(place, part, recv)


def _sum_devices(g):
    _, rows, cols = g.shape
    tr = _row_tile(rows, cols, budget=256 * 1024)

    def body(g_ref, o_ref):
        acc = g_ref[0]
        for d in range(1, 8):
            acc = acc + g_ref[d]
        o_ref[...] = acc

    return pl.pallas_call(body, name="sum_small_grads", grid=(rows // tr,),
                          in_specs=[pl.BlockSpec((8, tr, cols), lambda r: (0, r, 0))],
                          out_specs=pl.BlockSpec((tr, cols), lambda r: (r, 0)),
                          out_shape=jax.ShapeDtypeStruct((rows, cols), F32),
                          compiler_params=_params(("parallel",)))(g)


def _place_shard(w, chip, dtype):
    rows, cols = w.shape
    tr = _row_tile(rows, cols)

    def body(p_ref, i_ref, o_ref):
        o_ref[...] = i_ref[...].astype(o_ref.dtype)

    grid_spec = pltpu.PrefetchScalarGridSpec(
        num_scalar_prefetch=1, grid=(rows // tr,),
        in_specs=[pl.BlockSpec((tr, cols), lambda r, p_ref: (r, 0))],
        out_specs=pl.BlockSpec((None, tr, cols), lambda r, p_ref: (p_ref[0], r, 0)))
    return pl.pallas_call(body, name="place_shard", grid_spec=grid_spec,
                          out_shape=jax.ShapeDtypeStruct((N_CHIPS, rows, cols), dtype),
                          compiler_params=_params(("parallel",)))(chip, w)


def _adamw(w, g, m, v):
    rows, cols = w.shape
    tr = _row_tile(rows, cols, budget=1024 * 1024)

    def body(w_ref, g_ref, m_ref, v_ref, go_ref, d_ref, mo_ref, vo_ref):
        gv = g_ref[...]
        mn = ADAM_B1 * m_ref[...] + (1.0 - ADAM_B1) * gv
        vn = ADAM_B2 * v_ref[...] + (1.0 - ADAM_B2) * jnp.square(gv)
        m_hat = mn / (1.0 - ADAM_B1 ** ADAM_STEP)
        v_hat = vn / (1.0 - ADAM_B2 ** ADAM_STEP)
        d_ref[...] = -ADAM_LR * (m_hat / (jnp.sqrt(v_hat) + ADAM_EPS) + ADAM_WD * w_ref[...])
        go_ref[...] = gv
        mo_ref[...] = mn
        vo_ref[...] = vn

    spec = pl.BlockSpec((tr, cols), lambda r: (r, 0))
    return pl.pallas_call(body, name="adamw", grid=(rows // tr,), in_specs=[spec] * 4, out_specs=[spec] * 4,
                          out_shape=[jax.ShapeDtypeStruct((rows, cols), F32)] * 4,
                          compiler_params=_params(("parallel",)))(w, g, m, v)


def _pad_rope(w):
    z = jnp.zeros(w.shape[:-1] + (ROPE_HALF,), w.dtype)
    return jnp.concatenate([w[..., :ROPE_HALF], z, w[..., ROPE_HALF:], z], axis=-1)


def _unpad_rope(g):
    return jnp.concatenate([g[..., :ROPE_HALF], g[..., ROPE:ROPE + ROPE_HALF]], axis=-1)


def _unstack_cols(s):
    n, r, cs = s.shape
    return jnp.transpose(s, (1, 0, 2)).reshape(r, n * cs)


def _stack_cols(f):
    r, cfull = f.shape
    return jnp.transpose(f.reshape(r, N_CHIPS, cfull // N_CHIPS), (1, 0, 2))


def _small_shard(norm, conv):
    return jnp.concatenate([jnp.pad(norm, ((0, 15), (0, 0))), jnp.pad(conv, ((0, 13), (0, 0)))], axis=0)


def _flat_rows(a):
    return a.reshape(-1, LANES)


def _pack_small(arrs):
    return jnp.concatenate([_flat_rows(a.astype(F32)) for a in arrs], axis=0)


def _unpack_small(flat, like):
    out, r = [], 0
    for a in like:
        n = a.size // LANES
        out.append(flat[r:r + n].reshape(a.shape))
        r += n
    return out


def kernel(x, positions, e_norm_mix, e_w_in, e_q_norm, e_w_uq, e_kv_norm, e_w_ukv, e_v_norm, e_sgu_w, e_sgu_b, e_mla_out_norm, e_sgu_out_norm, e_w_out, o_norm_mix, o_w_in, o_conv_w, o_w_out, mlp_norm, mlp_w1, mlp_w2, final_norm, loss_target, m_e_norm_mix, m_e_w_in, m_e_q_norm, m_e_w_uq, m_e_kv_norm, m_e_w_ukv, m_e_v_norm, m_e_sgu_w, m_e_sgu_b, m_e_mla_out_norm, m_e_sgu_out_norm, m_e_w_out, m_o_norm_mix, m_o_w_in, m_o_conv_w, m_o_w_out, m_mlp_norm, m_mlp_w1, m_mlp_w2, m_final_norm, v_e_norm_mix, v_e_w_in, v_e_q_norm, v_e_w_uq, v_e_kv_norm, v_e_w_ukv, v_e_v_norm, v_e_sgu_w, v_e_sgu_b, v_e_mla_out_norm, v_e_sgu_out_norm, v_e_w_out, v_o_norm_mix, v_o_w_in, v_o_conv_w, v_o_w_out, v_mlp_norm, v_mlp_w1, v_mlp_w2, v_final_norm):
    t, d = x.shape[1], x.shape[2]
    ql, kvl = e_q_norm.shape[1], e_kv_norm.shape[1]
    groups = e_v_norm.shape[1]
    gw = groups * LANES
    heads = N_CHIPS * e_w_uq.shape[2] // (LANES + ROPE)
    hw = heads * LANES
    mix = hw + gw
    ei = N_CHIPS * e_w_in.shape[2]
    cd = N_CHIPS * o_conv_w.shape[2]
    ff = N_CHIPS * mlp_w1.shape[2]
    ffs = ff // N_CHIPS
    pi = 2 * gw + ql + kvl + LANES
    assert e_norm_mix.shape[0] == 1 and o_norm_mix.shape[0] == 1 and mlp_norm.shape[0] == 2
    assert ei == ql + kvl + ROPE + 2 * gw and cd == d and e_sgu_w.shape[2] == LANES
    assert (2 * gw) % ql == 0 and (2 * gw + ql) % kvl == 0 and t % LANES == 0
    scale = (LANES + ROPE) ** -0.5

    tr = min(256, t)
    tm = _pick(t, 1024, 8)
    xs = x.reshape(t, d)
    tgt = loss_target.reshape(t, d)
    cidx = lax.axis_index("c").astype(jnp.int32).reshape(1)
    chip = (2 * lax.axis_index("x") + lax.axis_index("y")).astype(jnp.int32).reshape(1)

    small_shard = _small_shard(o_norm_mix, o_conv_w[0])
    shards = [_place_shard(w, chip, BF16) for w in (e_w_in[0], e_w_uq[0], e_w_ukv[0], e_w_out[0], o_w_in[0], o_w_out[0],
                                                    mlp_w1.reshape(2 * d, ffs), mlp_w2.reshape(2 * ffs, d))]
    shards.append(_place_shard(small_shard, chip, F32))
    gathered = _all_gather([s.reshape(N_CHIPS, 2, s.shape[1] // 2, s.shape[2]) for s in shards])
    gathered = [g.reshape(N_CHIPS, 2 * g.shape[2], g.shape[3]) for g in gathered]
    w_in_g, w_uq_g, w_ukv_g, w_eout_g, w_oin_g, w_oout_g, w1_g, w2_g, small_g = gathered
    w1_g = w1_g.reshape(N_CHIPS, 2, d, ffs)
    w2_g = w2_g.reshape(N_CHIPS, 2, ffs, d)

    full = _unstack_cols(w_in_g)
    c2, c3 = ql + kvl, ql + kvl + ROPE
    w_in_all = jnp.concatenate([full[:, c3:], full[:, :c2], _pad_rope(full[:, c2:c3])], axis=1)
    full = _unstack_cols(w_uq_g).reshape(ql, heads, LANES + ROPE)
    w_q_all = jnp.concatenate([full[:, :, :LANES].reshape(ql, hw), _pad_rope(full[:, :, LANES:]).reshape(ql, hw)], axis=1)
    full = _unstack_cols(w_ukv_g).reshape(kvl, heads, 2 * LANES)
    w_kv_all = jnp.concatenate([full[:, :, :LANES].reshape(kvl, hw), full[:, :, LANES:].reshape(kvl, hw)], axis=1)
    w_eout = w_eout_g.reshape(mix, d)
    w_oout = w_oout_g.reshape(cd, d)
    g_o = small_g[:, 0].reshape(1, d)
    conv_w = jnp.pad(jnp.transpose(small_g[:, 16:19], (1, 0, 2)).reshape(3, cd), ((0, 5), (0, 0)))

    g_e, g_q, g_kv = e_norm_mix, e_q_norm, e_kv_norm
    g_vn = e_v_norm.reshape(1, gw)
    sgu_w = e_sgu_w[0]
    sgu_b = jnp.broadcast_to(e_sgu_b[0][:, :, None], (groups, LANES, LANES))
    g_mla, g_sgu = e_mla_out_norm, e_sgu_out_norm
    g_m0, g_m1 = mlp_norm[0:1], mlp_norm[1:2]
    g_f = final_norm.reshape(1, d)

    inv_freq = ROPE_BASE ** (-jnp.arange(0, ROPE, 2, dtype=F32) / ROPE)
    zeros32 = jnp.zeros((ROPE_HALF,), F32)
    ones32 = jnp.ones((ROPE_HALF,), F32)
    invf = jnp.concatenate([inv_freq, zeros32, inv_freq, zeros32]).reshape(1, LANES)
    cmask = jnp.concatenate([ones32, zeros32, ones32, zeros32]).reshape(1, LANES)
    smask = jnp.concatenate([-ones32, zeros32, ones32, zeros32]).reshape(1, LANES)
    ctab, stab = _rope_tables(positions.reshape(t, 1).astype(F32), invf, cmask, smask, tr)

    def mlp_fwd(tag, xin, g, layer):
        hm = _norm_fwd("mlp_norm_" + tag, xin, g, tr)
        tn = _pick(ffs, 1024)
        a, act = _matmul("mlp_up_" + tag, Mat(hm, t, d), Mat(w1_g, d, ff, "colstack", (layer,)), "nn",
                         [_out(t, ff, BF16), _out(t, ff, BF16)], tm, tn, _pick(d, 1024),
                         epilogue=lambda z: (jnp.maximum(z, 0.0), jnp.square(jnp.maximum(z, 0.0))))
        xo, = _matmul("mlp_down_" + tag, Mat(act, t, ff), Mat(w2_g, ff, d, "rowstack", (layer,)), "nn",
                      [_out(t, d, F32)], tm, _pick(d, 1024), _pick(ffs, 1024),
                      epilogue=lambda z, r: (z + r,), extras=[Mat(xin, t, d)])
        return xo, hm, a, act

    def mlp_bwd(tag, dx, dxb, xin, g, layer, hm, a, act, dw1_prev, dw2_prev):
        tn = _pick(ffs, 1024)
        dz, = _matmul("mlp_dact_" + tag, Mat(dxb, t, d), Mat(w2_g, ff, d, "rowstack", (layer,)), "nt",
                      [_out(t, ff, BF16)], tm, tn, _pick(d, 1024),
                      epilogue=lambda z, av: (z * (2.0 * av.astype(F32)),), extras=[Mat(a, t, ff)])
        al = {} if dw1_prev is None else {2: 0}
        prev2 = () if dw2_prev is None else (dw2_prev,)
        prev1 = () if dw1_prev is None else (dw1_prev,)
        dw2, = _matmul("mlp_dw2_" + tag, Mat(act, t, ff), Mat(dxb, t, d), "tn",
                       [_out(ff, d, F32, "rowstack", (layer,), (N_CHIPS, 2, ffs, d))], tn, _pick(d, 1024), _pick(t, 512, 8),
                       aliases=al, alias_in=prev2)
        dw1, = _matmul("mlp_dw1_" + tag, Mat(hm, t, d), Mat(dz, t, ff), "tn",
                       [_out(d, ff, F32, "colstack", (layer,), (N_CHIPS, 2, d, ffs))], _pick(d, 1024), tn, _pick(t, 512, 8),
                       aliases=al, alias_in=prev1)
        dhm, = _matmul("mlp_dh_" + tag, Mat(dz, t, ff), Mat(w1_g, d, ff, "colstack", (layer,)), "nt",
                       [_out(t, d, F32)], tm, _pick(d, 1024), tn)
        dxo, dxob, dg = _norm_bwd("mlp_norm_bwd_" + tag, dhm, xin, g, dx, tr)
        return dxo, dxob, dg, dw1, dw2

    h0 = _norm_fwd("e_norm", xs, g_e, tr)
    proj, = _matmul("e_proj", Mat(h0, t, d), Mat(w_in_all, d, pi), "nn", [_out(t, pi, F32)], tm, _pick(pi, 1024), _pick(d, 1024))
    cq_cb, ckv_cb, kr_cb = 2 * gw // ql, (2 * gw + ql) // kvl, (2 * gw + ql + kvl) // LANES
    qn, kvn = _rowwise("qkv_norm", lambda a, b, ga, gb: (_rms(a, ga), _rms(b, gb)), t // tr,
                       [_rt(proj, tr, ql, cq_cb), _rt(proj, tr, kvl, ckv_cb), _whole(g_q), _whole(g_kv)],
                       [_rt_out(t, ql, BF16, tr), _rt_out(t, kvl, BF16, tr)])
    qfull, = _matmul("q_up", Mat(qn, t, ql), Mat(w_q_all, ql, 2 * hw), "nn", [_out(t, 2 * hw, F32)], tm, _pick(2 * hw, 1024), ql)
    kvall, = _matmul("kv_up", Mat(kvn, t, kvl), Mat(w_kv_all, kvl, 2 * hw), "nn", [_out(t, 2 * hw, BF16)], tm, _pick(2 * hw, 1024), kvl)
    qall, kr = _rope_fwd(qfull, proj, kr_cb, ctab, stab, heads, tr)
    att, lse = _attn_fwd(qall, kvall, kr, heads, scale, tr)
    rb = min(2 * LANES, t)
    sgu = _sgu_fwd(proj, g_vn, sgu_w, sgu_b, groups, rb)
    mixed = _rowwise("mix_norm", lambda a, s, ga, gs: jnp.concatenate([_rms(a, ga), _rms(s, gs)], axis=1), t // tr,
                     [_rt(att, tr), _rt(sgu, tr), _whole(g_mla), _whole(g_sgu)], [_rt_out(t, mix, BF16, tr)])[0]
    x1, = _matmul("e_out", Mat(mixed, t, mix), Mat(w_eout, mix, d), "nn", [_out(t, d, F32)], tm, _pick(d, 1024), _pick(mix, 1024),
                  epilogue=lambda z, r: (z + r,), extras=[Mat(xs, t, d)])
    x2, hm0, a0, act0 = mlp_fwd("0", x1, g_m0, 0)

    h1 = _norm_fwd("o_norm", x2, g_o, tr)
    oin = Mat(w_oin_g, d, 3 * cd, "colstack")
    tn_o = _pick(_gcd(3 * cd // N_CHIPS, cd), 512)
    proj3, = _matmul("o_proj", Mat(h1, t, d), oin, "nn", [_out(t, 3 * cd, F32, "colstack", (), (3, t, cd))], tm, tn_o, _pick(d, 1024))
    tc = _pick(cd, 256)
    bz = _conv_fwd(proj3, conv_w, tc)
    x3, = _matmul("o_out", Mat(bz, t, cd), Mat(w_oout, cd, d), "nn", [_out(t, d, F32)], tm, _pick(d, 1024), _pick(cd, 1024),
                  epilogue=lambda z, r: (z + r,), extras=[Mat(x2, t, d)])
    x4, hm1, a1, act1 = mlp_fwd("1", x3, g_m1, 1)

    def final_fn(xv, gv, tv):
        r = lax.rsqrt(jnp.mean(xv * xv, axis=-1, keepdims=True) + EPS)
        xh = xv * r
        err = xh * gv - tv
        dy = err * (1.0 / d)
        dxh = dy * gv
        dx = r * (dxh - xh * jnp.mean(dxh * xh, axis=-1, keepdims=True))
        sq = jnp.sum(err * err, axis=0, keepdims=True)
        part = sq[:, :LANES]
        for k in range(1, d // LANES):
            part = part + sq[:, k * LANES:(k + 1) * LANES]
        return dx, dx, part, jnp.sum(dy * xh, axis=0, keepdims=True)

    dx4, dx4b, loss_vec, dg_f = _rowwise("loss_final_norm", final_fn, t // tr, [_rt(x4, tr), _whole(g_f), _rt(tgt, tr)],
                                         [_rt_out(t, d, F32, tr), _rt_out(t, d, BF16, tr)],
                                         [jax.ShapeDtypeStruct((1, LANES), F32), jax.ShapeDtypeStruct((1, d), F32)])
    loss = lax.psum(0.5 * jnp.sum(loss_vec) / d, ("x", "y", "c"))

    dx3, dx3b, dg_m1, dw1, dw2 = mlp_bwd("1", dx4, dx4b, x3, g_m1, 1, hm1, a1, act1, None, None)

    dbz, = _matmul("o_out_dx", Mat(dx3b, t, d), Mat(w_oout, cd, d), "nt", [_out(t, cd, F32)], tm, _pick(cd, 1024), _pick(d, 1024))
    dw_oout, = _matmul("o_out_dw", Mat(bz, t, cd), Mat(dx3b, t, d), "tn", [_out(cd, d, F32)], _pick(cd, 1024), _pick(d, 1024), _pick(t, 512, 8))
    dproj3, dconv = _conv_bwd(proj3, conv_w, dbz, tc)
    dp3 = Mat(dproj3, t, 3 * cd, "colstack")
    dw_oin, = _matmul("o_proj_dw", Mat(h1, t, d), dp3, "tn", [_out(d, 3 * cd, F32, "colstack", (), (N_CHIPS, d, 3 * cd // N_CHIPS))],
                      _pick(d, 1024), tn_o, _pick(t, 512, 8))
    dh1, = _matmul("o_proj_dx", dp3, oin, "nt", [_out(t, d, F32)], tm, _pick(d, 1024), tn_o)
    dx2, dx2b, dg_o = _norm_bwd("o_norm_bwd", dh1, x2, g_o, dx3, tr)

    dx1, dx1b, dg_m0, dw1, dw2 = mlp_bwd("0", dx2, dx2b, x1, g_m0, 0, hm0, a0, act0, dw1, dw2)

    dmixed, = _matmul("e_out_dx", Mat(dx1b, t, d), Mat(w_eout, mix, d), "nt", [_out(t, mix, F32)], tm, _pick(mix, 1024), _pick(d, 1024))
    dw_eout, = _matmul("e_out_dw", Mat(mixed, t, mix), Mat(dx1b, t, d), "tn", [_out(mix, d, F32)], _pick(mix, 1024), _pick(d, 1024), _pick(t, 512, 8))

    def mixb_fn(dm, a, s, ga, gs):
        da, dga = _rms_bwd(dm[:, :hw], a, ga)
        dsg, dgs = _rms_bwd(dm[:, hw:], s, gs)
        prod = da * a
        delta = jnp.stack([jnp.broadcast_to(jnp.sum(prod[:, h * LANES:(h + 1) * LANES], axis=-1, keepdims=True), (tr, LANES))
                           for h in range(heads)], axis=0)
        return da, dsg, delta, dga, dgs

    da_b, dsgu, delta, dg_mla, dg_sgu = _rowwise(
        "mix_norm_bwd", mixb_fn, t // tr, [_rt(dmixed, tr), _rt(att, tr), _rt(sgu, tr), _whole(g_mla), _whole(g_sgu)],
        [_rt_out(t, hw, BF16, tr), _rt_out(t, gw, F32, tr),
         (jax.ShapeDtypeStruct((heads, t, LANES), F32), pl.BlockSpec((heads, tr, LANES), lambda i: (0, i, 0)))],
        [jax.ShapeDtypeStruct((1, hw), F32), jax.ShapeDtypeStruct((1, gw), F32)])

    du, dv, dsgu_w, dsgu_b8, dg_vn = _sgu_bwd(proj, dsgu, g_vn, sgu_w, sgu_b, groups, rb)
    dq1, dq2 = _attn_dq(qall, kvall, kr, da_b, lse, delta, heads, scale, tr)
    dk1, dvv, dkr_h = _attn_dkv(qall, kvall, kr, da_b, lse, delta, heads, scale, tr)
    dqfull, dkr = _rope_bwd(dq1, dq2, dkr_h, ctab, stab, heads, tr)
    dkvall = jnp.concatenate([dk1, dvv], axis=1)
    dw_q, = _matmul("q_up_dw", Mat(qn, t, ql), Mat(dqfull, t, 2 * hw), "tn", [_out(ql, 2 * hw, F32)], ql, _pick(2 * hw, 1024), _pick(t, 512, 8))
    dqn, = _matmul("q_up_dx", Mat(dqfull, t, 2 * hw), Mat(w_q_all, ql, 2 * hw), "nt", [_out(t, ql, F32)], tm, ql, _pick(2 * hw, 1024))
    dw_kv, = _matmul("kv_up_dw", Mat(kvn, t, kvl), Mat(dkvall, t, 2 * hw), "tn", [_out(kvl, 2 * hw, F32)], kvl, _pick(2 * hw, 1024), _pick(t, 512, 8))
    dkvn, = _matmul("kv_up_dx", Mat(dkvall, t, 2 * hw), Mat(w_kv_all, kvl, 2 * hw), "nt", [_out(t, kvl, F32)], tm, kvl, _pick(2 * hw, 1024))

    def qkvb_fn(da, db, a, b, ga, gb):
        dxa, dga = _rms_bwd(da, a, ga)
        dxb, dgb = _rms_bwd(db, b, gb)
        return dxa, dxb, dga, dgb

    dcq, dckv, dg_q, dg_kv = _rowwise(
        "qkv_norm_bwd", qkvb_fn, t // tr,
        [_rt(dqn, tr), _rt(dkvn, tr), _rt(proj, tr, ql, cq_cb), _rt(proj, tr, kvl, ckv_cb), _whole(g_q), _whole(g_kv)],
        [_rt_out(t, ql, BF16, tr), _rt_out(t, kvl, BF16, tr)],
        [jax.ShapeDtypeStruct((1, ql), F32), jax.ShapeDtypeStruct((1, kvl), F32)])
    dproj = jnp.concatenate([du, dv, dcq, dckv, dkr], axis=1)
    dw_in, = _matmul("e_proj_dw", Mat(h0, t, d), Mat(dproj, t, pi), "tn", [_out(d, pi, F32)], _pick(d, 1024), _pick(pi, 1024), _pick(t, 512, 8))
    dh0, = _matmul("e_proj_dx", Mat(dproj, t, pi), Mat(w_in_all, d, pi), "nt", [_out(t, d, F32)], tm, _pick(d, 1024), _pick(pi, 1024))
    dx0, _, dg_e = _norm_bwd("e_norm_bwd", dh0, xs, g_e, dx1, tr)

    gfull = jnp.concatenate([dw_in[:, 2 * gw:2 * gw + c2], _unpad_rope(dw_in[:, 2 * gw + c2:]), dw_in[:, :2 * gw]], axis=1)
    gw_in = _stack_cols(gfull)
    gq = jnp.concatenate([dw_q[:, :hw].reshape(ql, heads, LANES), _unpad_rope(dw_q[:, hw:].reshape(ql, heads, LANES))], axis=-1)
    gw_uq = _stack_cols(gq.reshape(ql, heads * (LANES + ROPE)))
    gkv = jnp.concatenate([dw_kv[:, :hw].reshape(kvl, heads, LANES), dw_kv[:, hw:].reshape(kvl, heads, LANES)], axis=-1)
    gw_ukv = _stack_cols(gkv.reshape(kvl, heads * 2 * LANES))
    dconv_s = jnp.transpose(dconv[:3].reshape(3, N_CHIPS, cd // N_CHIPS), (1, 0, 2))
    gsmall = jnp.concatenate([jnp.pad(dg_o.reshape(N_CHIPS, 1, d // N_CHIPS), ((0, 0), (0, 15), (0, 0))),
                              jnp.pad(dconv_s, ((0, 0), (0, 13), (0, 0)))], axis=1)
    stacked = [gw_in, gw_uq, gw_ukv, dw_eout.reshape(N_CHIPS, mix // N_CHIPS, d), dw_oin,
               dw_oout.reshape(N_CHIPS, cd // N_CHIPS, d), dw1.reshape(N_CHIPS, 2 * d, ffs), dw2.reshape(N_CHIPS, 2 * ffs, d), gsmall]
    g5 = [g.reshape(N_CHIPS, 2, g.shape[1] // 2, g.shape[2]) for g in stacked]
    from_sib = _pair_exchange(g5)
    part = [_pair_sum(a, b, cidx) for a, b in zip(g5, from_sib)]
    from_chips = _chip_exchange(part)
    half = [_chip_sum(p, r, jnp.concatenate([chip, cidx])) for p, r in zip(part, from_chips)]
    reduced = [r.reshape(2 * r.shape[1], r.shape[2]) for r in _pair_share(half)]
    r_in, r_uq, r_ukv, r_eout, r_oin, r_oout, r_w1, r_w2, r_small = reduced

    small_like = [e_norm_mix, e_q_norm, e_kv_norm, e_v_norm, e_sgu_w, e_sgu_b, e_mla_out_norm, e_sgu_out_norm, mlp_norm, final_norm]
    small_grads = [dg_e, dg_q, dg_kv, dg_vn, dsgu_w, dsgu_b8[:, 0, :], dg_mla, dg_sgu, jnp.concatenate([dg_m0, dg_m1], axis=0), dg_f]
    sflat = _pack_small(small_grads)
    pad = (-sflat.shape[0]) % 8
    sflat = jnp.pad(sflat, ((0, pad), (0, 0)))
    g_small = _sum_devices(_gather_all_devices(sflat))

    def padded(arrs):
        return jnp.pad(_pack_small(arrs), ((0, pad), (0, 0)))

    s_m = [m_e_norm_mix, m_e_q_norm, m_e_kv_norm, m_e_v_norm, m_e_sgu_w, m_e_sgu_b, m_e_mla_out_norm, m_e_sgu_out_norm, m_mlp_norm, m_final_norm]
    s_v = [v_e_norm_mix, v_e_q_norm, v_e_kv_norm, v_e_v_norm, v_e_sgu_w, v_e_sgu_b, v_e_mla_out_norm, v_e_sgu_out_norm, v_mlp_norm, v_final_norm]
    s_out = [_unpack_small(o, small_like) for o in _adamw(padded(small_like), g_small, padded(s_m), padded(s_v))]

    big = {
        "e_w_in": _adamw(e_w_in[0], r_in, m_e_w_in[0], v_e_w_in[0]),
        "e_w_uq": _adamw(e_w_uq[0], r_uq, m_e_w_uq[0], v_e_w_uq[0]),
        "e_w_ukv": _adamw(e_w_ukv[0], r_ukv, m_e_w_ukv[0], v_e_w_ukv[0]),
        "e_w_out": _adamw(e_w_out[0], r_eout, m_e_w_out[0], v_e_w_out[0]),
        "o_w_in": _adamw(o_w_in[0], r_oin, m_o_w_in[0], v_o_w_in[0]),
        "o_w_out": _adamw(o_w_out[0], r_oout, m_o_w_out[0], v_o_w_out[0]),
        "mlp_w1": _adamw(mlp_w1.reshape(2 * d, ffs), r_w1, m_mlp_w1.reshape(2 * d, ffs), v_mlp_w1.reshape(2 * d, ffs)),
        "mlp_w2": _adamw(mlp_w2.reshape(2 * ffs, d), r_w2, m_mlp_w2.reshape(2 * ffs, d), v_mlp_w2.reshape(2 * ffs, d)),
    }
    sm = _adamw(small_shard, r_small, _small_shard(m_o_norm_mix, m_o_conv_w[0]), _small_shard(v_o_norm_mix, v_o_conv_w[0]))

    names = ["e_norm_mix", "e_w_in", "e_q_norm", "e_w_uq", "e_kv_norm", "e_w_ukv", "e_v_norm", "e_sgu_w", "e_sgu_b",
             "e_mla_out_norm", "e_sgu_out_norm", "e_w_out", "o_norm_mix", "o_w_in", "o_conv_w", "o_w_out",
             "mlp_norm", "mlp_w1", "mlp_w2", "final_norm"]
    shapes = {"e_w_in": e_w_in.shape, "e_w_uq": e_w_uq.shape, "e_w_ukv": e_w_ukv.shape, "e_w_out": e_w_out.shape,
              "o_w_in": o_w_in.shape, "o_w_out": o_w_out.shape, "mlp_w1": mlp_w1.shape, "mlp_w2": mlp_w2.shape}
    small_names = ["e_norm_mix", "e_q_norm", "e_kv_norm", "e_v_norm", "e_sgu_w", "e_sgu_b", "e_mla_out_norm",
                   "e_sgu_out_norm", "mlp_norm", "final_norm"]

    def leaf(kind, name):
        if name in big:
            return big[name][kind].reshape(shapes[name])
        if name == "o_norm_mix":
            return sm[kind][0:1]
        if name == "o_conv_w":
            return sm[kind][16:19].reshape(o_conv_w.shape)
        return s_out[kind][small_names.index(name)]

    outs = [loss, dx0.reshape(x.shape)]
    for kind in range(4):
        outs += [leaf(kind, nm) for nm in names]
    return tuple(outs)


def _gcd(a, b):
    while b:
        a, b = b, a % b
    return a
```

```python
import functools

import jax
import jax.numpy as jnp
from jax import lax
from jax.experimental import pallas as pl
from jax.experimental.pallas import tpu as pltpu

F32 = jnp.float32
BF16 = jnp.bfloat16
MESH = pl.DeviceIdType.MESH

LANES = 128
ROPE = 64
ROPE_HALF = ROPE // 2
ROPE_BASE = 10000.0
EPS = 1e-6
N_CHIPS = 4
VMEM_LIMIT = 48 * 1024 * 1024
NEG = -1e30

ADAM_LR = 0.001
ADAM_B1 = 0.9
ADAM_B2 = 0.999
ADAM_EPS = 1e-08
ADAM_WD = 0.01
ADAM_STEP = 10


def _pick(n, target, step=LANES):
    best = None
    for t in range(step, min(n, target) + 1, step):
        if n % t == 0:
            best = t
    return best if best is not None else n


def _params(sem, vmem=VMEM_LIMIT):
    return pltpu.CompilerParams(dimension_semantics=sem, vmem_limit_bytes=vmem)


class Mat:
    def __init__(self, arr, rows, cols, kind="plain", lead=(), col_off=0, shape=None, dtype=None):
        self.arr, self.rows, self.cols, self.kind, self.lead, self.col_off = arr, rows, cols, kind, tuple(lead), col_off
        self.shape = tuple(arr.shape) if arr is not None else tuple(shape)
        self.dtype = arr.dtype if arr is not None else dtype

    def sds(self):
        return jax.ShapeDtypeStruct(self.shape, self.dtype)

    def spec(self, br, bc, gridmap):
        lead, nl = self.lead, len(self.lead)
        if self.kind == "plain":
            assert self.col_off % bc == 0 and self.rows % br == 0 and self.cols % bc == 0, (self.shape, br, bc)
            off = self.col_off // bc
            block = (None,) * nl + (br, bc)

            def phys(rb, cb):
                return lead + (rb, cb + off)
        elif self.kind == "colstack":
            cs = self.shape[-1]
            assert cs % bc == 0 and self.rows % br == 0, (self.shape, br, bc)
            q = cs // bc
            block = (None,) * (nl + 1) + (br, bc)

            def phys(rb, cb):
                return (cb // q,) + lead + (rb, cb % q)
        else:
            rs = self.shape[-2]
            assert rs % br == 0 and self.cols % bc == 0, (self.shape, br, bc)
            q = rs // br
            block = (None,) * (nl + 1) + (br, bc)

            def phys(rb, cb):
                return (rb // q,) + lead + (rb % q, cb)

        return pl.BlockSpec(block, lambda *g: phys(*gridmap(*g)))


def _matmul(name, a, b, mode, outs, tm, tn, tk, epilogue=None, extras=(), deps=()):
    if mode == "nn":
        m, k, n = a.rows, a.cols, b.cols
        a_spec = a.spec(tm, tk, lambda i, j, kk: (i, kk))
        b_spec = b.spec(tk, tn, lambda i, j, kk: (kk, j))
        dims = (((1,), (0,)), ((), ()))
    elif mode == "nt":
        m, k, n = a.rows, a.cols, b.rows
        a_spec = a.spec(tm, tk, lambda i, j, kk: (i, kk))
        b_spec = b.spec(tn, tk, lambda i, j, kk: (j, kk))
        dims = (((1,), (1,)), ((), ()))
    else:
        k, m, n = a.rows, a.cols, b.cols
        a_spec = a.spec(tk, tm, lambda i, j, kk: (kk, i))
        b_spec = b.spec(tk, tn, lambda i, j, kk: (kk, j))
        dims = (((0,), (0,)), ((), ()))
    assert m % tm == 0 and n % tn == 0 and k % tk == 0, (name, m, n, k, tm, tn, tk)
    grid = (m // tm, n // tn, k // tk)
    nk = grid[2]
    n_ex, n_out, n_dep = len(extras), len(outs), len(deps)
    tile = lambda i, j, kk: (i, j)

    def body(a_ref, b_ref, *rest):
        ex = rest[:n_ex]
        out_refs = rest[n_ex + n_dep:n_ex + n_dep + n_out]
        acc = rest[-1]
        kk = pl.program_id(2)

        @pl.when(kk == 0)
        def _():
            acc[...] = jnp.zeros_like(acc)

        acc[...] += lax.dot_general(a_ref[...], b_ref[...], dims, preferred_element_type=F32)

        @pl.when(kk == nk - 1)
        def _():
            vals = epilogue(acc[...], *[e[...] for e in ex]) if epilogue is not None else (acc[...],)
            for o, v in zip(out_refs, vals):
                o[...] = v.astype(o.dtype)

    res = pl.pallas_call(
        body, name=name, grid=grid,
        in_specs=[a_spec, b_spec] + [e.spec(tm, tn, tile) for e in extras]
        + [pl.BlockSpec(memory_space=pl.ANY) for _ in deps],
        out_specs=[o.spec(tm, tn, tile) for o in outs],
        out_shape=[o.sds() for o in outs],
        scratch_shapes=[pltpu.VMEM((tm, tn), F32)],
        compiler_params=_params(("parallel", "parallel", "arbitrary")),
    )(a.arr, b.arr, *[e.arr for e in extras], *deps)
    return res


def _out(rows, cols, dtype, kind="plain", lead=(), shape=None):
    return Mat(None, rows, cols, kind, lead, shape=shape if shape is not None else (rows, cols), dtype=dtype)


def _rt(arr, tr, width=None, cb=0):
    width = arr.shape[1] if width is None else width
    return arr, pl.BlockSpec((tr, width), lambda i: (i, cb))


def _whole(arr):
    nd = arr.ndim
    return arr, pl.BlockSpec(arr.shape, lambda i: (0,) * nd)


def _rowwise(name, fn, n_steps, ins, outs, accs=(), deps=()):
    n_in, n_out, n_acc, n_dep = len(ins), len(outs), len(accs), len(deps)

    def body(*refs):
        vals = fn(*[r[...] for r in refs[:n_in]])
        if not isinstance(vals, (tuple, list)):
            vals = (vals,)
        for ref, v in zip(refs[n_in + n_dep:n_in + n_dep + n_out], vals[:n_out]):
            ref[...] = v.astype(ref.dtype)
        if n_acc:
            acc_refs = refs[n_in + n_dep + n_out:]

            @pl.when(pl.program_id(0) == 0)
            def _():
                for ref in acc_refs:
                    ref[...] = jnp.zeros_like(ref)

            for ref, v in zip(acc_refs, vals[n_out:]):
                ref[...] += v

    acc_specs = [pl.BlockSpec(s.shape, lambda i, nd=len(s.shape): (0,) * nd) for s in accs]
    res = pl.pallas_call(
        body, name=name, grid=(n_steps,),
        in_specs=[s for _, s in ins] + [pl.BlockSpec(memory_space=pl.ANY) for _ in deps],
        out_specs=[s for _, s in outs] + acc_specs,
        out_shape=[o for o, _ in outs] + list(accs),
        compiler_params=_params(("arbitrary",) if n_acc else ("parallel",)),
    )(*[a for a, _ in ins], *deps)
    return res


def _rt_out(t, width, dtype, tr):
    return jax.ShapeDtypeStruct((t, width), dtype), pl.BlockSpec((tr, width), lambda i: (i, 0))


def _rms(x, g):
    r = lax.rsqrt(jnp.mean(x * x, axis=-1, keepdims=True) + EPS)
    return x * r * g


def _rms_bwd(dy, x, g):
    r = lax.rsqrt(jnp.mean(x * x, axis=-1, keepdims=True) + EPS)
    xh = x * r
    dxh = dy * g
    dx = r * (dxh - xh * jnp.mean(dxh * xh, axis=-1, keepdims=True))
    dg = jnp.sum(dy * xh, axis=0, keepdims=True)
    return dx, dg


def _gelu(x):
    k = 0.7978845608028654
    th = jnp.tanh(k * (x + 0.044715 * (x * x * x)))
    return x * (0.5 * (1.0 + th))


def _gelu_grad(x):
    k = 0.7978845608028654
    x2 = x * x
    th = jnp.tanh(k * (x + 0.044715 * (x2 * x)))
    return 0.5 * (1.0 + th) + 0.5 * x * (1.0 - th * th) * (k * (1.0 + 3.0 * 0.044715 * x2))


def _norm_fwd(name, x, g, tr):
    t, d = x.shape
    return _rowwise(name, lambda xv, gv: _rms(xv, gv), t // tr, [_rt(x, tr), _whole(g)], [_rt_out(t, d, BF16, tr)])[0]


def _norm_bwd(name, dh, x, g, dres, tr):
    t, d = x.shape

    def fn(dhv, xv, gv, drv):
        dx, dg = _rms_bwd(dhv, xv, gv)
        dx = dx + drv
        return dx, dx, dg

    return _rowwise(name, fn, t // tr, [_rt(dh, tr), _rt(x, tr), _whole(g), _rt(dres, tr)],
                    [_rt_out(t, d, F32, tr), _rt_out(t, d, BF16, tr)], [jax.ShapeDtypeStruct((1, d), F32)])


def _rope_tables(posf, invf, cmask, smask, tr):
    t = posf.shape[0]

    def fn(p, f, cm, sm):
        ang = p * f
        return jnp.cos(ang) * cm, jnp.sin(ang) * sm

    return _rowwise("rope_tables", fn, t // tr, [_rt(posf, tr), _whole(invf), _whole(cmask), _whole(smask)],
                    [_rt_out(t, LANES, F32, tr), _rt_out(t, LANES, F32, tr)])


def _rot(v, c, s):
    return v * c + pltpu.roll(v, ROPE, axis=1) * s


def _rot_bwd(dv, c, s):
    return dv * c + pltpu.roll(dv * s, ROPE, axis=1)


def _rope_fwd(qfull, proj, kr_cb, ctab, stab, heads, tr):
    t = qfull.shape[0]
    hw = heads * LANES

    def fn(q, kr, c, s):
        parts = [q[:, :hw]] + [_rot(q[:, hw + h * LANES: hw + (h + 1) * LANES], c, s) for h in range(heads)]
        return jnp.concatenate(parts, axis=1), _rot(kr, c, s)

    return _rowwise("rope_fwd", fn, t // tr, [_rt(qfull, tr), _rt(proj, tr, LANES, kr_cb), _rt(ctab, tr), _rt(stab, tr)],
                    [_rt_out(t, 2 * hw, BF16, tr), _rt_out(t, LANES, BF16, tr)])


def _rope_bwd(dq1, dq2, dkr_h, ctab, stab, heads, tr):
    t = dq1.shape[0]
    hw = heads * LANES

    def fn(a, b, dk, c, s):
        parts = [a] + [_rot_bwd(b[:, h * LANES:(h + 1) * LANES], c, s) for h in range(heads)]
        dks = dk[0]
        for h in range(1, heads):
            dks = dks + dk[h]
        return jnp.concatenate(parts, axis=1), _rot_bwd(dks, c, s)

    dk_spec = pl.BlockSpec((heads, tr, LANES), lambda i: (0, i, 0))
    return _rowwise("rope_bwd", fn, t // tr, [_rt(dq1, tr), _rt(dq2, tr), (dkr_h, dk_spec), _rt(ctab, tr), _rt(stab, tr)],
                    [_rt_out(t, 2 * hw, BF16, tr), _rt_out(t, LANES, BF16, tr)])


def _dot_nt(a, b):
    return lax.dot_general(a, b, (((1,), (1,)), ((), ())), preferred_element_type=F32)


def _dot_tn(a, b):
    return lax.dot_general(a, b, (((0,), (0,)), ((), ())), preferred_element_type=F32)


def _dot(a, b):
    return jnp.dot(a, b, preferred_element_type=F32)


def _causal(i, j, tq, tk):
    rows = i * tq + lax.broadcasted_iota(jnp.int32, (tq, tk), 0)
    cols = j * tk + lax.broadcasted_iota(jnp.int32, (tq, tk), 1)
    return cols <= rows


def _attn_fwd(qall, kvall, kr, heads, scale, tq):
    t = qall.shape[0]
    nq = t // tq
    tk = tq

    def body(qn_ref, qr_ref, kn_ref, v_ref, kr_ref, o_ref, lse_ref, m_ref, l_ref, acc_ref):
        i = pl.program_id(1)
        qn, qr = qn_ref[...], qr_ref[...]
        m_ref[...] = jnp.full_like(m_ref, NEG)
        l_ref[...] = jnp.zeros_like(l_ref)
        acc_ref[...] = jnp.zeros_like(acc_ref)

        def step(j, carry):
            ks = pl.multiple_of(j * tk, tk)
            s = (_dot_nt(qn, kn_ref[pl.ds(ks, tk), :]) + _dot_nt(qr, kr_ref[pl.ds(ks, tk), :])) * scale
            s = jnp.where(_causal(i, j, tq, tk), s, NEG)
            m_prev = m_ref[...]
            m_new = jnp.maximum(m_prev, jnp.max(s, axis=-1, keepdims=True))
            p = jnp.exp(s - m_new[:, :1])
            alpha = jnp.exp(m_prev - m_new)
            l_ref[...] = alpha * l_ref[...] + jnp.sum(p, axis=-1, keepdims=True)
            acc_ref[...] = alpha * acc_ref[...] + _dot(p.astype(BF16), v_ref[pl.ds(ks, tk), :])
            m_ref[...] = m_new
            return carry

        lax.fori_loop(0, i + 1, step, 0)
        o_ref[...] = acc_ref[...] / l_ref[...]
        lse_ref[...] = m_ref[...] + jnp.log(l_ref[...])

    return pl.pallas_call(
        body, name="attn_fwd", grid=(heads, nq),
        in_specs=[pl.BlockSpec((tq, LANES), lambda h, i: (i, h)),
                  pl.BlockSpec((tq, LANES), lambda h, i: (i, heads + h)),
                  pl.BlockSpec((t, LANES), lambda h, i: (0, h)),
                  pl.BlockSpec((t, LANES), lambda h, i: (0, heads + h)),
                  pl.BlockSpec((t, LANES), lambda h, i: (0, 0))],
        out_specs=[pl.BlockSpec((tq, LANES), lambda h, i: (i, h)),
                   pl.BlockSpec((None, tq, LANES), lambda h, i: (h, i, 0))],
        out_shape=[jax.ShapeDtypeStruct((t, heads * LANES), F32), jax.ShapeDtypeStruct((heads, t, LANES), F32)],
        scratch_shapes=[pltpu.VMEM((tq, LANES), F32)] * 3,
        compiler_params=_params(("parallel", "arbitrary")),
    )(qall, qall, kvall, kvall, kr)


def _attn_dq(qall, kvall, kr, do, lse, delta, heads, scale, tq):
    t = qall.shape[0]
    nq = t // tq
    tk = tq

    def body(qn_ref, qr_ref, kn_ref, v_ref, kr_ref, do_ref, lse_ref, dl_ref, dq1_ref, dq2_ref, a1_ref, a2_ref):
        i = pl.program_id(1)
        qn, qr, do_v = qn_ref[...], qr_ref[...], do_ref[...]
        lse_v, dl_v = lse_ref[...][:, :1], dl_ref[...][:, :1]
        a1_ref[...] = jnp.zeros_like(a1_ref)
        a2_ref[...] = jnp.zeros_like(a2_ref)

        def step(j, carry):
            ks = pl.multiple_of(j * tk, tk)
            k1, k2 = kn_ref[pl.ds(ks, tk), :], kr_ref[pl.ds(ks, tk), :]
            s = (_dot_nt(qn, k1) + _dot_nt(qr, k2)) * scale
            p = jnp.where(_causal(i, j, tq, tk), jnp.exp(s - lse_v), 0.0)
            dp = _dot_nt(do_v, v_ref[pl.ds(ks, tk), :])
            ds = (p * (dp - dl_v) * scale).astype(BF16)
            a1_ref[...] += _dot(ds, k1)
            a2_ref[...] += _dot(ds, k2)
            return carry

        lax.fori_loop(0, i + 1, step, 0)
        dq1_ref[...] = a1_ref[...]
        dq2_ref[...] = a2_ref[...]

    qblk = lambda off: pl.BlockSpec((tq, LANES), lambda h, i: (i, off + h))
    full = lambda off: pl.BlockSpec((t, LANES), lambda h, i: (0, off + h))
    stat = pl.BlockSpec((None, tq, LANES), lambda h, i: (h, i, 0))
    return pl.pallas_call(
        body, name="attn_dq", grid=(heads, nq),
        in_specs=[qblk(0), qblk(heads), full(0), full(heads), pl.BlockSpec((t, LANES), lambda h, i: (0, 0)),
                  qblk(0), stat, stat],
        out_specs=[qblk(0), qblk(0)],
        out_shape=[jax.ShapeDtypeStruct((t, heads * LANES), F32)] * 2,
        scratch_shapes=[pltpu.VMEM((tq, LANES), F32)] * 2,
        compiler_params=_params(("parallel", "arbitrary")),
    )(qall, qall, kvall, kvall, kr, do, lse, delta)


def _attn_dkv(qall, kvall, kr, do, lse, delta, heads, scale, tq):
    t = qall.shape[0]
    nq = t // tq
    tk = tq

    def body(qn_ref, qr_ref, kn_ref, v_ref, kr_ref, do_ref, lse_ref, dl_ref, dk_ref, dv_ref, dkr_ref, ak, av, akr):
        j = pl.program_id(1)
        k1, k2, vv = kn_ref[...], kr_ref[...], v_ref[...]
        ak[...] = jnp.zeros_like(ak)
        av[...] = jnp.zeros_like(av)
        akr[...] = jnp.zeros_like(akr)

        def step(i, carry):
            qs = pl.multiple_of(i * tq, tq)
            qn, qr, do_v = qn_ref[pl.ds(qs, tq), :], qr_ref[pl.ds(qs, tq), :], do_ref[pl.ds(qs, tq), :]
            lse_v, dl_v = lse_ref[pl.ds(qs, tq), :][:, :1], dl_ref[pl.ds(qs, tq), :][:, :1]
            s = (_dot_nt(qn, k1) + _dot_nt(qr, k2)) * scale
            p = jnp.where(_causal(i, j, tq, tk), jnp.exp(s - lse_v), 0.0)
            dp = _dot_nt(do_v, vv)
            ds = (p * (dp - dl_v) * scale).astype(BF16)
            av[...] += _dot_tn(p.astype(BF16), do_v)
            ak[...] += _dot_tn(ds, qn)
            akr[...] += _dot_tn(ds, qr)
            return carry

        lax.fori_loop(j, nq, step, 0)
        dk_ref[...] = ak[...].astype(dk_ref.dtype)
        dv_ref[...] = av[...].astype(dv_ref.dtype)
        dkr_ref[...] = akr[...]

    kblk = lambda off: pl.BlockSpec((tk, LANES), lambda h, j: (j, off + h))
    full = lambda off: pl.BlockSpec((t, LANES), lambda h, j: (0, off + h))
    stat = pl.BlockSpec((None, t, LANES), lambda h, j: (h, 0, 0))
    return pl.pallas_call(
        body, name="attn_dkv", grid=(heads, nq),
        in_specs=[full(0), full(heads), kblk(0), kblk(heads), pl.BlockSpec((tk, LANES), lambda h, j: (j, 0)),
                  full(0), stat, stat],
        out_specs=[kblk(0), kblk(0), pl.BlockSpec((None, tk, LANES), lambda h, j: (h, j, 0))],
        out_shape=[jax.ShapeDtypeStruct((t, heads * LANES), BF16)] * 2 + [jax.ShapeDtypeStruct((heads, t, LANES), F32)],
        scratch_shapes=[pltpu.VMEM((tk, LANES), F32)] * 3,
        compiler_params=_params(("parallel", "arbitrary")),
    )(qall, qall, kvall, kvall, kr, do, lse, delta)


def _tril():
    return lax.broadcasted_iota(jnp.int32, (LANES, LANES), 0) >= lax.broadcasted_iota(jnp.int32, (LANES, LANES), 1)


def _group_norm(vg):
    mu = jnp.mean(vg, axis=-1, keepdims=True)
    vc = vg - mu
    rs = lax.rsqrt(jnp.mean(vc * vc, axis=-1, keepdims=True) + EPS)
    return vc * rs, rs


def _sgu_fwd(proj, gain, w, bias, groups, rb):
    t = proj.shape[0]
    gw = groups * LANES
    cpb = rb // LANES

    def body(u_ref, v_ref, gain_ref, w_ref, b_ref, s_ref):
        tril = _tril()
        for g in range(groups):
            wt = jnp.where(tril, w_ref[g], 0.0).astype(BF16)
            cols = slice(g * LANES, (g + 1) * LANES)
            for ci in range(cpb):
                rows = slice(ci * LANES, (ci + 1) * LANES)
                ug = _gelu(u_ref[rows, cols])
                vh, _ = _group_norm(_gelu(v_ref[rows, cols]))
                vn = vh * gain_ref[:, cols]
                y = _dot(wt, vn.astype(BF16)) + b_ref[g]
                s_ref[rows, cols] = ug * y

    return pl.pallas_call(
        body, name="sgu_fwd", grid=(t // rb,),
        in_specs=[pl.BlockSpec((rb, gw), lambda i: (i, 0)), pl.BlockSpec((rb, gw), lambda i: (i, 1)),
                  pl.BlockSpec((1, gw), lambda i: (0, 0)),
                  pl.BlockSpec((groups, LANES, LANES), lambda i: (0, 0, 0)),
                  pl.BlockSpec((groups, LANES, LANES), lambda i: (0, 0, 0))],
        out_specs=pl.BlockSpec((rb, gw), lambda i: (i, 0)),
        out_shape=jax.ShapeDtypeStruct((t, gw), F32),
        compiler_params=_params(("parallel",)),
    )(proj, proj, gain, w, bias)


def _sgu_bwd(proj, ds, gain, w, bias, groups, rb):
    t = proj.shape[0]
    gw = groups * LANES
    cpb = rb // LANES
    n_steps = t // rb

    def body(u_ref, v_ref, ds_ref, gain_ref, w_ref, b_ref, du_ref, dv_ref, dw_ref, db_ref, dg_ref, dy_acc):
        step = pl.program_id(0)

        @pl.when(step == 0)
        def _():
            dw_ref[...] = jnp.zeros_like(dw_ref)
            dy_acc[...] = jnp.zeros_like(dy_acc)
            dg_ref[...] = jnp.zeros_like(dg_ref)

        tril = _tril()
        for g in range(groups):
            wt = jnp.where(tril, w_ref[g], 0.0).astype(BF16)
            cols = slice(g * LANES, (g + 1) * LANES)
            gain_g = gain_ref[:, cols]
            for ci in range(cpb):
                rows = slice(ci * LANES, (ci + 1) * LANES)
                u_raw, v_raw, ds_v = u_ref[rows, cols], v_ref[rows, cols], ds_ref[rows, cols]
                ug = _gelu(u_raw)
                vh, rs = _group_norm(_gelu(v_raw))
                vn = (vh * gain_g).astype(BF16)
                y = _dot(wt, vn) + b_ref[g]
                dy = ds_v * ug
                dyb = dy.astype(BF16)
                du_ref[rows, cols] = (ds_v * y * _gelu_grad(u_raw)).astype(du_ref.dtype)
                dy_acc[g] += dy
                dw_ref[g] += _dot_nt(dyb, vn)
                dvn = _dot_tn(wt, dyb)
                dg_ref[:, cols] += jnp.sum(dvn * vh, axis=0, keepdims=True)
                dvh = dvn * gain_g
                dvg = rs * (dvh - jnp.mean(dvh, axis=-1, keepdims=True)
                            - vh * jnp.mean(dvh * vh, axis=-1, keepdims=True))
                dv_ref[rows, cols] = (dvg * _gelu_grad(v_raw)).astype(dv_ref.dtype)

        @pl.when(step == n_steps - 1)
        def _():
            ones = jnp.ones((8, LANES), F32)
            for g in range(groups):
                dw_ref[g] = jnp.where(tril, dw_ref[g], 0.0)
                db_ref[g] = lax.dot_general(ones, dy_acc[g], (((1,), (1,)), ((), ())),
                                            precision=lax.Precision.HIGHEST, preferred_element_type=F32)

    blk = lambda cb: pl.BlockSpec((rb, gw), lambda i: (i, cb))
    whole3 = pl.BlockSpec((groups, LANES, LANES), lambda i: (0, 0, 0))
    return pl.pallas_call(
        body, name="sgu_bwd", grid=(n_steps,),
        in_specs=[blk(0), blk(1), blk(0), pl.BlockSpec((1, gw), lambda i: (0, 0)), whole3, whole3],
        out_specs=[blk(0), blk(0), whole3, pl.BlockSpec((groups, 8, LANES), lambda i: (0, 0, 0)),
                   pl.BlockSpec((1, gw), lambda i: (0, 0))],
        out_shape=[jax.ShapeDtypeStruct((t, gw), BF16), jax.ShapeDtypeStruct((t, gw), BF16),
                   jax.ShapeDtypeStruct((groups, LANES, LANES), F32), jax.ShapeDtypeStruct((groups, 8, LANES), F32),
                   jax.ShapeDtypeStruct((1, gw), F32)],
        scratch_shapes=[pltpu.VMEM((groups, LANES, LANES), F32)],
        compiler_params=_params(("arbitrary",)),
    )(proj, proj, ds, gain, w, bias)


def _shift_down(z, s):
    rows = lax.broadcasted_iota(jnp.int32, z.shape, 0)
    return jnp.where(rows >= s, pltpu.roll(z, s, axis=0), 0.0)


def _shift_up(z, s):
    n = z.shape[0]
    rows = lax.broadcasted_iota(jnp.int32, z.shape, 0)
    return jnp.where(rows < n - s, pltpu.roll(z, n - s, axis=0), 0.0)


def _conv_fwd(proj3, cw, tc):
    _, t, cd = proj3.shape

    def body(p_ref, w_ref, o_ref):
        z = p_ref[1] * p_ref[2]
        w = w_ref[...]
        zc = w[2:3] * z + w[1:2] * _shift_down(z, 1) + w[0:1] * _shift_down(z, 2)
        o_ref[...] = (p_ref[0] * zc).astype(o_ref.dtype)

    return pl.pallas_call(
        body, name="conv_fwd", grid=(cd // tc,),
        in_specs=[pl.BlockSpec((3, t, tc), lambda j: (0, 0, j)), pl.BlockSpec((8, tc), lambda j: (0, j))],
        out_specs=pl.BlockSpec((t, tc), lambda j: (0, j)),
        out_shape=jax.ShapeDtypeStruct((t, cd), BF16),
        compiler_params=_params(("parallel",)),
    )(proj3, cw)


def _conv_bwd(proj3, cw, dbz, tc):
    _, t, cd = proj3.shape

    def body(p_ref, w_ref, d_ref, o_ref, dw_ref):
        b, c, xin = p_ref[0], p_ref[1], p_ref[2]
        w = w_ref[...]
        z = c * xin
        z1, z2 = _shift_down(z, 1), _shift_down(z, 2)
        zc = w[2:3] * z + w[1:2] * z1 + w[0:1] * z2
        d = d_ref[...]
        dzc = d * b
        dz = w[2:3] * dzc + w[1:2] * _shift_up(dzc, 1) + w[0:1] * _shift_up(dzc, 2)
        o_ref[0] = (d * zc).astype(o_ref.dtype)
        o_ref[1] = (dz * xin).astype(o_ref.dtype)
        o_ref[2] = (dz * c).astype(o_ref.dtype)
        row = lax.broadcasted_iota(jnp.int32, (8, tc), 0)
        dw0 = jnp.sum(dzc * z2, axis=0, keepdims=True)
        dw1 = jnp.sum(dzc * z1, axis=0, keepdims=True)
        dw2 = jnp.sum(dzc * z, axis=0, keepdims=True)
        dw_ref[...] = jnp.where(row == 0, dw0, 0.0) + jnp.where(row == 1, dw1, 0.0) + jnp.where(row == 2, dw2, 0.0)

    return pl.pallas_call(
        body, name="conv_bwd", grid=(cd // tc,),
        in_specs=[pl.BlockSpec((3, t, tc), lambda j: (0, 0, j)), pl.BlockSpec((8, tc), lambda j: (0, j)),
                  pl.BlockSpec((t, tc), lambda j: (0, j))],
        out_specs=[pl.BlockSpec((3, t, tc), lambda j: (0, 0, j)), pl.BlockSpec((8, tc), lambda j: (0, j))],
        out_shape=[jax.ShapeDtypeStruct((3, t, cd), BF16), jax.ShapeDtypeStruct((8, cd), F32)],
        compiler_params=_params(("parallel",)),
    )(proj3, cw, dbz)


def _place():
    x, y, c = lax.axis_index("x"), lax.axis_index("y"), lax.axis_index("c")
    chips = [(1 - x, y), (x, 1 - y), (1 - x, 1 - y)]
    return x, y, c, chips


def _any_specs(n):
    return [pl.BlockSpec(memory_space=pl.ANY) for _ in range(n)]


HBM_SPEC = pl.BlockSpec(memory_space=pltpu.HBM)
SEM_SPEC = pl.BlockSpec(memory_space=pltpu.SEMAPHORE)
ORDERED_EFFECT = pltpu.SideEffectType.DATAFLOW_SIDE_EFFECTING


def _in_hbm(a):
    return pltpu.with_memory_space_constraint(a, pltpu.HBM)


def _token():
    return jax.ShapeDtypeStruct((8, LANES), F32), pl.BlockSpec(memory_space=pltpu.VMEM)


def _gather_start(groups):
    sizes = [len(g) for g in groups]
    flat = [b for g in groups for b in g]
    n, ng = len(flat), len(groups)

    def body(*refs):
        ins, sems, token = refs[:n], refs[n:n + 2 * ng], refs[-1]
        x, y, c, chips = _place()
        me = 2 * x + y
        i = 0
        for gi, size in enumerate(sizes):
            for j in range(size):
                blk = ins[i].at[me, c]
                for k, chip in enumerate(chips):
                    pltpu.make_async_remote_copy(src_ref=blk, dst_ref=blk, send_sem=sems[2 * gi].at[3 * j + k],
                                                 recv_sem=sems[2 * gi + 1].at[3 * j + k],
                                                 device_id=(*chip, c), device_id_type=MESH).start()
                i += 1
        token[...] = jnp.zeros_like(token)

    tok_shape, tok_spec = _token()
    res = pl.pallas_call(
        body, name="gather_start",
        in_specs=[HBM_SPEC] * n,
        out_specs=[SEM_SPEC] * (2 * ng) + [HBM_SPEC] * n + [tok_spec],
        out_shape=[pltpu.SemaphoreType.DMA((3 * size,)) for size in sizes for _ in (0, 1)]
        + [pltpu.HBM(b.shape, b.dtype) for b in flat] + [tok_shape],
        input_output_aliases={i: 2 * ng + i for i in range(n)},
        compiler_params=pltpu.CompilerParams(has_side_effects=ORDERED_EFFECT),
    )(*[_in_hbm(b) for b in flat])
    out, i = [], 2 * ng
    for gi, size in enumerate(sizes):
        out.append((res[2 * gi], res[2 * gi + 1], list(res[i:i + size])))
        i += size
    return out, res[-1]


def _gather_wait(tag, send, recv, bufs, after):
    n = len(bufs)

    def body(*refs):
        ins, send_ref, recv_ref = refs[:n], refs[n], refs[n + 1]
        x, y, c, chips = _place()
        me = 2 * x + y
        for j in range(n):
            for k, (px, py) in enumerate(chips):
                cp = pltpu.make_async_remote_copy(src_ref=ins[j].at[me, c], dst_ref=ins[j].at[2 * px + py, c],
                                                  send_sem=send_ref.at[3 * j + k], recv_sem=recv_ref.at[3 * j + k],
                                                  device_id=(px, py, c), device_id_type=MESH)
                cp.wait_send()
                cp.wait_recv()

    return pl.pallas_call(
        body, name="gather_wait_" + tag,
        in_specs=[HBM_SPEC] * n + [SEM_SPEC, SEM_SPEC, pl.BlockSpec(memory_space=pl.ANY)],
        out_specs=[HBM_SPEC] * n,
        out_shape=[pltpu.HBM(b.shape, b.dtype) for b in bufs],
        input_output_aliases={i: i for i in range(n)},
        compiler_params=pltpu.CompilerParams(has_side_effects=ORDERED_EFFECT),
    )(*bufs, send, recv, after)


def _gather_forward(tag, bufs):
    n = len(bufs)

    def body(*refs):
        ins, outs = refs[:n], refs[n:2 * n]
        send, recv = refs[2 * n:]
        x, y, c, chips = _place()
        sib = (x, y, 1 - c)

        def cp(i, k, slot, half):
            return pltpu.make_async_remote_copy(src_ref=ins[i].at[slot, half], dst_ref=outs[i].at[slot, half],
                                                send_sem=send.at[3 * i + k], recv_sem=recv.at[3 * i + k],
                                                device_id=sib, device_id_type=MESH)

        cps = [cp(i, k, 2 * px + py, c) for i in range(n) for k, (px, py) in enumerate(chips)]
        for d in cps:
            d.start()
        for i in range(n):
            for k, (px, py) in enumerate(chips):
                cp(i, k, 2 * px + py, 1 - c).wait_recv()
        for d in cps:
            d.wait_send()

    return pl.pallas_call(
        body, name="gather_forward_" + tag,
        in_specs=_any_specs(n), out_specs=_any_specs(n),
        out_shape=[jax.ShapeDtypeStruct(b.shape, b.dtype) for b in bufs],
        scratch_shapes=[pltpu.SemaphoreType.DMA((3 * n,))] * 2,
        input_output_aliases={i: i for i in range(n)},
        compiler_params=pltpu.CompilerParams(has_side_effects=True),
    )(*bufs)


def _pair_exchange(tag, entries):
    n = len(entries)

    def body(*refs):
        ins, outs = refs[:n], refs[n:2 * n]
        send, recv = refs[2 * n:]
        x, y, c, _ = _place()
        sib = (x, y, 1 - c)

        def cp(i, j):
            return pltpu.make_async_remote_copy(src_ref=ins[i].at[j, 1 - c], dst_ref=outs[i].at[j],
                                                send_sem=send.at[N_CHIPS * i + j], recv_sem=recv.at[N_CHIPS * i + j],
                                                device_id=sib, device_id_type=MESH)

        cps = [cp(i, j) for i in range(n) for j in range(N_CHIPS)]
        for d in cps:
            d.start()
        for d in cps:
            d.wait_recv()
        for d in cps:
            d.wait_send()

    return pl.pallas_call(
        body, name="grad_pair_exchange_" + tag,
        in_specs=_any_specs(n), out_specs=_any_specs(n),
        out_shape=[jax.ShapeDtypeStruct((N_CHIPS,) + e.shape[2:], e.dtype) for e in entries],
        scratch_shapes=[pltpu.SemaphoreType.DMA((N_CHIPS * n,))] * 2,
        compiler_params=pltpu.CompilerParams(has_side_effects=True),
    )(*entries)


def _scatter_start(tag, parts):
    n = len(parts)
    lands = [lax.empty((3,) + p.shape[1:], p.dtype) for p in parts]

    def body(*refs):
        ins, zones, send, recv, token = refs[:n], refs[n:2 * n], refs[2 * n], refs[2 * n + 1], refs[-1]
        x, y, c, chips = _place()
        for i in range(n):
            for k, (px, py) in enumerate(chips):
                pltpu.make_async_remote_copy(src_ref=ins[i].at[2 * px + py], dst_ref=zones[i].at[k],
                                             send_sem=send.at[3 * i + k], recv_sem=recv.at[3 * i + k],
                                             device_id=(px, py, c), device_id_type=MESH).start()
        token[...] = jnp.zeros_like(token)

    tok_shape, tok_spec = _token()
    res = pl.pallas_call(
        body, name="scatter_start_" + tag,
        in_specs=[HBM_SPEC] * (2 * n),
        out_specs=[SEM_SPEC, SEM_SPEC] + [HBM_SPEC] * (2 * n) + [tok_spec],
        out_shape=[pltpu.SemaphoreType.DMA((3 * n,))] * 2 + [pltpu.HBM(a.shape, a.dtype) for a in parts + lands] + [tok_shape],
        input_output_aliases={i: 2 + i for i in range(2 * n)},
        compiler_params=pltpu.CompilerParams(has_side_effects=ORDERED_EFFECT),
    )(*[_in_hbm(a) for a in parts + lands])
    return (res[0], res[1], list(res[2:2 + n]), list(res[2 + n:2 + 2 * n])), res[-1]


def _scatter_wait(tag, send, recv, parts, lands, after):
    n = len(parts)

    def body(*refs):
        ins, zones, send_ref, recv_ref = refs[:n], refs[n:2 * n], refs[2 * n], refs[2 * n + 1]
        x, y, c, chips = _place()
        for i in range(n):
            for k, (px, py) in enumerate(chips):
                cp = pltpu.make_async_remote_copy(src_ref=ins[i].at[2 * px + py], dst_ref=zones[i].at[k],
                                                  send_sem=send_ref.at[3 * i + k], recv_sem=recv_ref.at[3 * i + k],
                                                  device_id=(px, py, c), device_id_type=MESH)
                cp.wait_send()
                cp.wait_recv()

    res = pl.pallas_call(
        body, name="scatter_wait_" + tag,
        in_specs=[HBM_SPEC] * (2 * n) + [SEM_SPEC, SEM_SPEC, pl.BlockSpec(memory_space=pl.ANY)],
        out_specs=[HBM_SPEC] * (2 * n),
        out_shape=[pltpu.HBM(a.shape, a.dtype) for a in parts + lands],
        input_output_aliases={i: i for i in range(2 * n)},
        compiler_params=pltpu.CompilerParams(has_side_effects=ORDERED_EFFECT),
    )(*parts, *lands, send, recv, after)
    return list(res[:n]), list(res[n:])


def _pair_share(bufs):
    n = len(bufs)

    def body(*refs):
        ins, outs = refs[:n], refs[n:2 * n]
        send, recv = refs[2 * n:]
        x, y, c, _ = _place()
        sib = (x, y, 1 - c)

        def cp(i, half):
            return pltpu.make_async_remote_copy(src_ref=ins[i].at[half], dst_ref=outs[i].at[half],
                                                send_sem=send.at[i], recv_sem=recv.at[i],
                                                device_id=sib, device_id_type=MESH)

        cps = [cp(i, c) for i in range(n)]
        for d in cps:
            d.start()
        for i in range(n):
            cp(i, 1 - c).wait_recv()
        for d in cps:
            d.wait_send()

    return pl.pallas_call(
        body, name="grad_pair_share",
        in_specs=_any_specs(n), out_specs=_any_specs(n),
        out_shape=[jax.ShapeDtypeStruct(b.shape, b.dtype) for b in bufs],
        scratch_shapes=[pltpu.SemaphoreType.DMA((n,))] * 2,
        input_output_aliases={i: i for i in range(n)},
        compiler_params=pltpu.CompilerParams(has_side_effects=True),
    )(*bufs)


def _gather_all_devices(v):
    def body(v_ref, o_ref, send, recv, loc):
        x, y, c, _ = _place()
        me = 4 * x + 2 * y + c
        own = pltpu.make_async_copy(v_ref, o_ref.at[me], loc)
        own.start()
        rels = [(fx, fy, fc) for fx in (0, 1) for fy in (0, 1) for fc in (0, 1)][1:]

        def peer(fx, fy, fc):
            return (x + fx - 2 * x * fx, y + fy - 2 * y * fy, c + fc - 2 * c * fc)

        def cp(r, slot, dev):
            return pltpu.make_async_remote_copy(src_ref=v_ref, dst_ref=o_ref.at[slot], send_sem=send.at[r],
                                                recv_sem=recv.at[r], device_id=dev, device_id_type=MESH)

        cps = [cp(r, me, peer(*f)) for r, f in enumerate(rels)]
        for d in cps:
            d.start()
        for r, f in enumerate(rels):
            px, py, pc = peer(*f)
            cp(r, 4 * px + 2 * py + pc, (px, py, pc)).wait_recv()
        for d in cps:
            d.wait_send()
        own.wait()

    return pl.pallas_call(
        body, name="gather_small_grads",
        in_specs=_any_specs(1), out_specs=_any_specs(1)[0],
        out_shape=jax.ShapeDtypeStruct((8,) + v.shape, v.dtype),
        scratch_shapes=[pltpu.SemaphoreType.DMA((7,)), pltpu.SemaphoreType.DMA((7,)), pltpu.SemaphoreType.DMA],
        compiler_params=pltpu.CompilerParams(has_side_effects=True),
    )(v)


def _row_tile(rows, cols, itemsize=4, budget=2 * 1024 * 1024):
    best = None
    for t in range(8, rows + 1, 8):
        if rows % t == 0 and t * cols * itemsize <= budget:
            best = t
    return best if best is not None else rows


def _my_chip():
    return 2 * lax.axis_index("x") + lax.axis_index("y")


def _pair_sum(g5, gsib):
    _, _, rh, cols = g5.shape
    tr = _row_tile(rh, cols)

    def body(a_ref, b_ref, o_ref):
        o_ref[...] = (a_ref[...] + b_ref[...]).astype(o_ref.dtype)

    return pl.pallas_call(body, name="grad_pair_sum", grid=(N_CHIPS, rh // tr),
                          in_specs=[pl.BlockSpec((None, None, tr, cols), lambda j, r: (j, lax.axis_index("c"), r, 0)),
                                    pl.BlockSpec((None, tr, cols), lambda j, r: (j, r, 0))],
                          out_specs=pl.BlockSpec((None, tr, cols), lambda j, r: (j, r, 0)),
                          out_shape=jax.ShapeDtypeStruct((N_CHIPS, rh, cols), BF16),
                          compiler_params=_params(("parallel", "parallel")))(g5, gsib)


def _chip_sum(part, recv):
    _, rh, cols = part.shape
    tr = _row_tile(rh, cols)

    def body(a_ref, b_ref, o_ref):
        acc = a_ref[...].astype(F32)
        for k in range(3):
            acc = acc + b_ref[k].astype(F32)
        o_ref[...] = acc

    return pl.pallas_call(body, name="grad_chip_sum", grid=(rh // tr,),
                          in_specs=[pl.BlockSpec((None, tr, cols), lambda r: (_my_chip(), r, 0)),
                                    pl.BlockSpec((3, tr, cols), lambda r: (0, r, 0))],
                          out_specs=pl.BlockSpec((None, tr, cols), lambda r: (lax.axis_index("c"), r, 0)),
                          out_shape=jax.ShapeDtypeStruct((2, rh, cols), F32),
                          compiler_params=_params(("parallel",)))(part, recv)


def _sum_devices(g):
    _, rows, cols = g.shape
    tr = _row_tile(rows, cols, budget=256 * 1024)

    def body(g_ref, o_ref):
        acc = g_ref[0]
        for d in range(1, 8):
            acc = acc + g_ref[d]
        o_ref[...] = acc

    return pl.pallas_call(body, name="sum_small_grads", grid=(rows // tr,),
                          in_specs=[pl.BlockSpec((8, tr, cols), lambda r: (0, r, 0))],
                          out_specs=pl.BlockSpec((tr, cols), lambda r: (r, 0)),
                          out_shape=jax.ShapeDtypeStruct((rows, cols), F32),
                          compiler_params=_params(("parallel",)))(g)


def _place_shard(w, layer, dtype):
    _, rows, cols = w.shape
    tr = _row_tile(rows, cols)

    def body(i_ref, o_ref):
        o_ref[...] = i_ref[...].astype(o_ref.dtype)

    out = pl.pallas_call(body, name="place_shard", grid=(rows // tr,),
                         in_specs=[pl.BlockSpec((None, tr, cols), lambda r: (layer, r, 0))],
                         out_specs=pl.BlockSpec((None, tr, cols), lambda r: (_my_chip(), r, 0)),
                         out_shape=jax.ShapeDtypeStruct((N_CHIPS, rows, cols), dtype),
                         compiler_params=_params(("parallel",)))(w)
    return out.reshape(N_CHIPS, 2, rows // 2, cols)


def _adamw(w, gs, m, v):
    n_layers, rows, cols = w.shape
    tr = _row_tile(rows, cols, budget=1024 * 1024)

    def body(w_ref, m_ref, v_ref, *rest):
        g_refs = rest[:n_layers]
        go_ref, d_ref, mo_ref, vo_ref = rest[n_layers:]
        gv = g_refs[0][...]
        for layer in range(1, n_layers):
            gv = jnp.where(pl.program_id(0) == layer, g_refs[layer][...], gv)
        mn = ADAM_B1 * m_ref[...] + (1.0 - ADAM_B1) * gv
        vn = ADAM_B2 * v_ref[...] + (1.0 - ADAM_B2) * jnp.square(gv)
        m_hat = mn / (1.0 - ADAM_B1 ** ADAM_STEP)
        v_hat = vn / (1.0 - ADAM_B2 ** ADAM_STEP)
        d_ref[...] = -ADAM_LR * (m_hat / (jnp.sqrt(v_hat) + ADAM_EPS) + ADAM_WD * w_ref[...])
        go_ref[...] = gv
        mo_ref[...] = mn
        vo_ref[...] = vn

    spec = pl.BlockSpec((None, tr, cols), lambda layer, r: (layer, r, 0))
    g_specs = [pl.BlockSpec((tr, cols), lambda layer, r, own=own: (jnp.where(layer == own, r, 0), 0))
               for own in range(n_layers)]
    return pl.pallas_call(body, name="adamw", grid=(n_layers, rows // tr), in_specs=[spec] * 3 + g_specs,
                          out_specs=[spec] * 4, out_shape=[jax.ShapeDtypeStruct((n_layers, rows, cols), F32)] * 4,
                          compiler_params=_params(("parallel", "parallel")))(w, m, v, *gs)


def _pad_rope(w):
    z = jnp.zeros(w.shape[:-1] + (ROPE_HALF,), w.dtype)
    return jnp.concatenate([w[..., :ROPE_HALF], z, w[..., ROPE_HALF:], z], axis=-1)


def _unpad_rope(g):
    return jnp.concatenate([g[..., :ROPE_HALF], g[..., ROPE:ROPE + ROPE_HALF]], axis=-1)


def _unstack_cols(s):
    n, r, cs = s.shape
    return jnp.transpose(s, (1, 0, 2)).reshape(r, n * cs)


def _stack_cols(f):
    r, cfull = f.shape
    return jnp.transpose(f.reshape(r, N_CHIPS, cfull // N_CHIPS), (1, 0, 2))


def _small_shard(norm, conv):
    return jnp.concatenate([jnp.pad(norm, ((0, 15), (0, 0))), jnp.pad(conv, ((0, 13), (0, 0)))], axis=0)


def _flat_rows(a):
    return a.reshape(-1, LANES)


def _pack_small(arrs):
    return jnp.concatenate([_flat_rows(a.astype(F32)) for a in arrs], axis=0)


def _unpack_small(flat, like):
    out, r = [], 0
    for a in like:
        n = a.size // LANES
        out.append(flat[r:r + n].reshape(a.shape))
        r += n
    return out


def kernel(x, positions, e_norm_mix, e_w_in, e_q_norm, e_w_uq, e_kv_norm, e_w_ukv, e_v_norm, e_sgu_w, e_sgu_b, e_mla_out_norm, e_sgu_out_norm, e_w_out, o_norm_mix, o_w_in, o_conv_w, o_w_out, mlp_norm, mlp_w1, mlp_w2, final_norm, loss_target, m_e_norm_mix, m_e_w_in, m_e_q_norm, m_e_w_uq, m_e_kv_norm, m_e_w_ukv, m_e_v_norm, m_e_sgu_w, m_e_sgu_b, m_e_mla_out_norm, m_e_sgu_out_norm, m_e_w_out, m_o_norm_mix, m_o_w_in, m_o_conv_w, m_o_w_out, m_mlp_norm, m_mlp_w1, m_mlp_w2, m_final_norm, v_e_norm_mix, v_e_w_in, v_e_q_norm, v_e_w_uq, v_e_kv_norm, v_e_w_ukv, v_e_v_norm, v_e_sgu_w, v_e_sgu_b, v_e_mla_out_norm, v_e_sgu_out_norm, v_e_w_out, v_o_norm_mix, v_o_w_in, v_o_conv_w, v_o_w_out, v_mlp_norm, v_mlp_w1, v_mlp_w2, v_final_norm):
    t, d = x.shape[1], x.shape[2]
    ql, kvl = e_q_norm.shape[1], e_kv_norm.shape[1]
    groups = e_v_norm.shape[1]
    gw = groups * LANES
    heads = N_CHIPS * e_w_uq.shape[2] // (LANES + ROPE)
    hw = heads * LANES
    mix = hw + gw
    ei = N_CHIPS * e_w_in.shape[2]
    cd = N_CHIPS * o_conv_w.shape[2]
    ff = N_CHIPS * mlp_w1.shape[2]
    ffs = ff // N_CHIPS
    pi = 2 * gw + ql + kvl + LANES
    assert e_norm_mix.shape[0] == 1 and o_norm_mix.shape[0] == 1 and mlp_norm.shape[0] == 2
    assert ei == ql + kvl + ROPE + 2 * gw and cd == d and e_sgu_w.shape[2] == LANES
    assert (2 * gw) % ql == 0 and (2 * gw + ql) % kvl == 0 and t % LANES == 0
    scale = (LANES + ROPE) ** -0.5

    tr = min(256, t)
    tm = _pick(t, 1024, 8)
    xs = x.reshape(t, d)
    tgt = loss_target.reshape(t, d)

    small_shard = _small_shard(o_norm_mix, o_conv_w[0])
    layer_groups = [
        [_place_shard(e_w_in, 0, BF16), _place_shard(e_w_uq, 0, BF16), _place_shard(e_w_ukv, 0, BF16),
         _place_shard(e_w_out, 0, BF16), _place_shard(small_shard[None], 0, F32)],
        [_place_shard(mlp_w1, 0, BF16), _place_shard(mlp_w2, 0, BF16)],
        [_place_shard(o_w_in, 0, BF16), _place_shard(o_w_out, 0, BF16)],
        [_place_shard(mlp_w1, 1, BF16), _place_shard(mlp_w2, 1, BF16)]]
    started, gather_token = _gather_start(layer_groups)

    def gathered(gi, tag, after):
        send, recv, bufs = started[gi]
        bufs = _gather_forward(tag, _gather_wait(tag, send, recv, bufs, after))
        return [b.reshape(N_CHIPS, 2 * b.shape[2], b.shape[3]) for b in bufs]

    w_in_g, w_uq_g, w_ukv_g, w_eout_g, small_g = gathered(0, "e", gather_token)

    full = _unstack_cols(w_in_g)
    c2, c3 = ql + kvl, ql + kvl + ROPE
    w_in_all = jnp.concatenate([full[:, c3:], full[:, :c2], _pad_rope(full[:, c2:c3])], axis=1)
    full = _unstack_cols(w_uq_g).reshape(ql, heads, LANES + ROPE)
    w_q_all = jnp.concatenate([full[:, :, :LANES].reshape(ql, hw), _pad_rope(full[:, :, LANES:]).reshape(ql, hw)], axis=1)
    full = _unstack_cols(w_ukv_g).reshape(kvl, heads, 2 * LANES)
    w_kv_all = jnp.concatenate([full[:, :, :LANES].reshape(kvl, hw), full[:, :, LANES:].reshape(kvl, hw)], axis=1)
    w_eout = w_eout_g.reshape(mix, d)
    g_o = small_g[:, 0].reshape(1, d)
    conv_w = jnp.pad(jnp.transpose(small_g[:, 16:19], (1, 0, 2)).reshape(3, cd), ((0, 5), (0, 0)))

    g_e, g_q, g_kv = e_norm_mix, e_q_norm, e_kv_norm
    g_vn = e_v_norm.reshape(1, gw)
    sgu_w = e_sgu_w[0]
    sgu_b = jnp.broadcast_to(e_sgu_b[0][:, :, None], (groups, LANES, LANES))
    g_mla, g_sgu = e_mla_out_norm, e_sgu_out_norm
    g_m0, g_m1 = mlp_norm[0:1], mlp_norm[1:2]
    g_f = final_norm.reshape(1, d)

    inv_freq = ROPE_BASE ** (-jnp.arange(0, ROPE, 2, dtype=F32) / ROPE)
    zeros32 = jnp.zeros((ROPE_HALF,), F32)
    ones32 = jnp.ones((ROPE_HALF,), F32)
    invf = jnp.concatenate([inv_freq, zeros32, inv_freq, zeros32]).reshape(1, LANES)
    cmask = jnp.concatenate([ones32, zeros32, ones32, zeros32]).reshape(1, LANES)
    smask = jnp.concatenate([-ones32, zeros32, ones32, zeros32]).reshape(1, LANES)
    ctab, stab = _rope_tables(positions.reshape(t, 1).astype(F32), invf, cmask, smask, tr)

    def mlp_fwd(tag, xin, g, w1, w2):
        hm = _norm_fwd("mlp_norm_" + tag, xin, g, tr)
        tn = _pick(ffs, 1024)
        a, act = _matmul("mlp_up_" + tag, Mat(hm, t, d), w1, "nn",
                         [_out(t, ff, BF16), _out(t, ff, BF16)], tm, tn, _pick(d, 1024),
                         epilogue=lambda z: (jnp.maximum(z, 0.0), jnp.square(jnp.maximum(z, 0.0))))
        xo, = _matmul("mlp_down_" + tag, Mat(act, t, ff), w2, "nn",
                      [_out(t, d, F32)], tm, _pick(d, 1024), _pick(ffs, 1024),
                      epilogue=lambda z, r: (z + r,), extras=[Mat(xin, t, d)])
        return xo, hm, a, act

    def mlp_bwd(tag, dx, dxb, xin, g, w1, w2, hm, a, act, deps):
        tn = _pick(ffs, 1024)
        dz, = _matmul("mlp_dact_" + tag, Mat(dxb, t, d), w2, "nt",
                      [_out(t, ff, BF16)], tm, tn, _pick(d, 1024),
                      epilogue=lambda z, av: (z * (2.0 * av.astype(F32)),), extras=[Mat(a, t, ff)], deps=deps)
        dw2, = _matmul("mlp_dw2_" + tag, Mat(act, t, ff), Mat(dxb, t, d), "tn",
                       [_out(ff, d, F32)], tn, _pick(d, 1024), _pick(t, 512, 8))
        dw1, = _matmul("mlp_dw1_" + tag, Mat(hm, t, d), Mat(dz, t, ff), "tn",
                       [_out(d, ff, F32, "colstack", (), (N_CHIPS, d, ffs))], _pick(d, 1024), tn, _pick(t, 512, 8))
        dhm, = _matmul("mlp_dh_" + tag, Mat(dz, t, ff), w1, "nt",
                       [_out(t, d, F32)], tm, _pick(d, 1024), tn)
        dxo, dxob, dg = _norm_bwd("mlp_norm_bwd_" + tag, dhm, xin, g, dx, tr)
        return dxo, dxob, dg, dw1, dw2.reshape(N_CHIPS, ffs, d)

    def scatter(tag, stacked):
        g5 = [g.reshape(N_CHIPS, 2, g.shape[1] // 2, g.shape[2]) for g in stacked]
        part = [_pair_sum(a, b) for a, b in zip(g5, _pair_exchange(tag, g5))]
        return _scatter_start(tag, part)

    h0 = _norm_fwd("e_norm", xs, g_e, tr)
    proj, = _matmul("e_proj", Mat(h0, t, d), Mat(w_in_all, d, pi), "nn", [_out(t, pi, F32)], tm, _pick(pi, 1024), _pick(d, 1024))
    cq_cb, ckv_cb, kr_cb = 2 * gw // ql, (2 * gw + ql) // kvl, (2 * gw + ql + kvl) // LANES
    qn, kvn = _rowwise("qkv_norm", lambda a, b, ga, gb: (_rms(a, ga), _rms(b, gb)), t // tr,
                       [_rt(proj, tr, ql, cq_cb), _rt(proj, tr, kvl, ckv_cb), _whole(g_q), _whole(g_kv)],
                       [_rt_out(t, ql, BF16, tr), _rt_out(t, kvl, BF16, tr)])
    qfull, = _matmul("q_up", Mat(qn, t, ql), Mat(w_q_all, ql, 2 * hw), "nn", [_out(t, 2 * hw, F32)], tm, _pick(2 * hw, 1024), ql)
    kvall, = _matmul("kv_up", Mat(kvn, t, kvl), Mat(w_kv_all, kvl, 2 * hw), "nn", [_out(t, 2 * hw, BF16)], tm, _pick(2 * hw, 1024), kvl)
    qall, kr = _rope_fwd(qfull, proj, kr_cb, ctab, stab, heads, tr)
    att, lse = _attn_fwd(qall, kvall, kr, heads, scale, tr)
    rb = min(2 * LANES, t)
    sgu = _sgu_fwd(proj, g_vn, sgu_w, sgu_b, groups, rb)
    mixed = _rowwise("mix_norm", lambda a, s, ga, gs: jnp.concatenate([_rms(a, ga), _rms(s, gs)], axis=1), t // tr,
                     [_rt(att, tr), _rt(sgu, tr), _whole(g_mla), _whole(g_sgu)], [_rt_out(t, mix, BF16, tr)])[0]
    x1, = _matmul("e_out", Mat(mixed, t, mix), Mat(w_eout, mix, d), "nn", [_out(t, d, F32)], tm, _pick(d, 1024), _pick(mix, 1024),
                  epilogue=lambda z, r: (z + r,), extras=[Mat(xs, t, d)])
    w1_g, w2_g = gathered(1, "m0", x1)
    w1_0, w2_0 = Mat(w1_g, d, ff, "colstack"), Mat(w2_g.reshape(ff, d), ff, d)
    x2, hm0, a0, act0 = mlp_fwd("0", x1, g_m0, w1_0, w2_0)

    w_oin_g, w_oout_g = gathered(2, "o", x2)
    w_oout = w_oout_g.reshape(cd, d)
    h1 = _norm_fwd("o_norm", x2, g_o, tr)
    oin = Mat(w_oin_g, d, 3 * cd, "colstack")
    tn_o = _pick(_gcd(3 * cd // N_CHIPS, cd), 512)
    proj3, = _matmul("o_proj", Mat(h1, t, d), oin, "nn", [_out(t, 3 * cd, F32, "colstack", (), (3, t, cd))], tm, tn_o, _pick(d, 1024))
    tc = _pick(cd, 256)
    bz = _conv_fwd(proj3, conv_w, tc)
    x3, = _matmul("o_out", Mat(bz, t, cd), Mat(w_oout, cd, d), "nn", [_out(t, d, F32)], tm, _pick(d, 1024), _pick(cd, 1024),
                  epilogue=lambda z, r: (z + r,), extras=[Mat(x2, t, d)])
    w1_g, w2_g = gathered(3, "m1", x3)
    w1_1, w2_1 = Mat(w1_g, d, ff, "colstack"), Mat(w2_g.reshape(ff, d), ff, d)
    x4, hm1, a1, act1 = mlp_fwd("1", x3, g_m1, w1_1, w2_1)

    def final_fn(xv, gv, tv):
        r = lax.rsqrt(jnp.mean(xv * xv, axis=-1, keepdims=True) + EPS)
        xh = xv * r
        err = xh * gv - tv
        dy = err * (1.0 / d)
        dxh = dy * gv
        dx = r * (dxh - xh * jnp.mean(dxh * xh, axis=-1, keepdims=True))
        sq = jnp.sum(err * err, axis=0, keepdims=True)
        part = sq[:, :LANES]
        for k in range(1, d // LANES):
            part = part + sq[:, k * LANES:(k + 1) * LANES]
        return dx, dx, part, jnp.sum(dy * xh, axis=0, keepdims=True)

    dx4, dx4b, loss_vec, dg_f = _rowwise("loss_final_norm", final_fn, t // tr, [_rt(x4, tr), _whole(g_f), _rt(tgt, tr)],
                                         [_rt_out(t, d, F32, tr), _rt_out(t, d, BF16, tr)],
                                         [jax.ShapeDtypeStruct((1, LANES), F32), jax.ShapeDtypeStruct((1, d), F32)])
    loss = lax.psum(0.5 * jnp.sum(loss_vec) / d, ("x", "y", "c"))

    dx3, dx3b, dg_m1, dw1, dw2 = mlp_bwd("1", dx4, dx4b, x3, g_m1, w1_1, w2_1, hm1, a1, act1, ())
    sc_m1, tok = scatter("m1", [dw1, dw2])

    dbz, = _matmul("o_out_dx", Mat(dx3b, t, d), Mat(w_oout, cd, d), "nt", [_out(t, cd, F32)], tm, _pick(cd, 1024), _pick(d, 1024),
                   deps=(tok,))
    dw_oout, = _matmul("o_out_dw", Mat(bz, t, cd), Mat(dx3b, t, d), "tn", [_out(cd, d, F32)], _pick(cd, 1024), _pick(d, 1024), _pick(t, 512, 8))
    dproj3, dconv = _conv_bwd(proj3, conv_w, dbz, tc)
    dp3 = Mat(dproj3, t, 3 * cd, "colstack")
    dw_oin, = _matmul("o_proj_dw", Mat(h1, t, d), dp3, "tn", [_out(d, 3 * cd, F32, "colstack", (), (N_CHIPS, d, 3 * cd // N_CHIPS))],
                      _pick(d, 1024), tn_o, _pick(t, 512, 8))
    dh1, = _matmul("o_proj_dx", dp3, oin, "nt", [_out(t, d, F32)], tm, _pick(d, 1024), tn_o)
    dx2, dx2b, dg_o = _norm_bwd("o_norm_bwd", dh1, x2, g_o, dx3, tr)

    dconv_s = jnp.transpose(dconv[:3].reshape(3, N_CHIPS, cd // N_CHIPS), (1, 0, 2))
    gsmall = jnp.concatenate([jnp.pad(dg_o.reshape(N_CHIPS, 1, d // N_CHIPS), ((0, 0), (0, 15), (0, 0))),
                              jnp.pad(dconv_s, ((0, 0), (0, 13), (0, 0)))], axis=1)
    sc_o, tok = scatter("o", [dw_oin, dw_oout.reshape(N_CHIPS, cd // N_CHIPS, d), gsmall])

    dx1, dx1b, dg_m0, dw1, dw2 = mlp_bwd("0", dx2, dx2b, x1, g_m0, w1_0, w2_0, hm0, a0, act0, (tok,))
    sc_m0, tok = scatter("m0", [dw1, dw2])

    dmixed, = _matmul("e_out_dx", Mat(dx1b, t, d), Mat(w_eout, mix, d), "nt", [_out(t, mix, F32)], tm, _pick(mix, 1024), _pick(d, 1024),
                      deps=(tok,))
    dw_eout, = _matmul("e_out_dw", Mat(mixed, t, mix), Mat(dx1b, t, d), "tn", [_out(mix, d, F32)], _pick(mix, 1024), _pick(d, 1024), _pick(t, 512, 8))

    def mixb_fn(dm, a, s, ga, gs):
        da, dga = _rms_bwd(dm[:, :hw], a, ga)
        dsg, dgs = _rms_bwd(dm[:, hw:], s, gs)
        prod = da * a
        delta = jnp.stack([jnp.broadcast_to(jnp.sum(prod[:, h * LANES:(h + 1) * LANES], axis=-1, keepdims=True), (tr, LANES))
                           for h in range(heads)], axis=0)
        return da, dsg, delta, dga, dgs

    da_b, dsgu, delta, dg_mla, dg_sgu = _rowwise(
        "mix_norm_bwd", mixb_fn, t // tr, [_rt(dmixed, tr), _rt(att, tr), _rt(sgu, tr), _whole(g_mla), _whole(g_sgu)],
        [_rt_out(t, hw, BF16, tr), _rt_out(t, gw, F32, tr),
         (jax.ShapeDtypeStruct((heads, t, LANES), F32), pl.BlockSpec((heads, tr, LANES), lambda i: (0, i, 0)))],
        [jax.ShapeDtypeStruct((1, hw), F32), jax.ShapeDtypeStruct((1, gw), F32)])

    du, dv, dsgu_w, dsgu_b8, dg_vn = _sgu_bwd(proj, dsgu, g_vn, sgu_w, sgu_b, groups, rb)
    dq1, dq2 = _attn_dq(qall, kvall, kr, da_b, lse, delta, heads, scale, tr)
    dk1, dvv, dkr_h = _attn_dkv(qall, kvall, kr, da_b, lse, delta, heads, scale, tr)
    dqfull, dkr = _rope_bwd(dq1, dq2, dkr_h, ctab, stab, heads, tr)
    dkvall = jnp.concatenate([dk1, dvv], axis=1)
    dw_q, = _matmul("q_up_dw", Mat(qn, t, ql), Mat(dqfull, t, 2 * hw), "tn", [_out(ql, 2 * hw, F32)], ql, _pick(2 * hw, 1024), _pick(t, 512, 8))
    dqn, = _matmul("q_up_dx", Mat(dqfull, t, 2 * hw), Mat(w_q_all, ql, 2 * hw), "nt", [_out(t, ql, F32)], tm, ql, _pick(2 * hw, 1024))
    dw_kv, = _matmul("kv_up_dw", Mat(kvn, t, kvl), Mat(dkvall, t, 2 * hw), "tn", [_out(kvl, 2 * hw, F32)], kvl, _pick(2 * hw, 1024), _pick(t, 512, 8))
    dkvn, = _matmul("kv_up_dx", Mat(dkvall, t, 2 * hw), Mat(w_kv_all, kvl, 2 * hw), "nt", [_out(t, kvl, F32)], tm, kvl, _pick(2 * hw, 1024))

    def qkvb_fn(da, db, a, b, ga, gb):
        dxa, dga = _rms_bwd(da, a, ga)
        dxb, dgb = _rms_bwd(db, b, gb)
        return dxa, dxb, dga, dgb

    dcq, dckv, dg_q, dg_kv = _rowwise(
        "qkv_norm_bwd", qkvb_fn, t // tr,
        [_rt(dqn, tr), _rt(dkvn, tr), _rt(proj, tr, ql, cq_cb), _rt(proj, tr, kvl, ckv_cb), _whole(g_q), _whole(g_kv)],
        [_rt_out(t, ql, BF16, tr), _rt_out(t, kvl, BF16, tr)],
        [jax.ShapeDtypeStruct((1, ql), F32), jax.ShapeDtypeStruct((1, kvl), F32)])
    dproj = jnp.concatenate([du, dv, dcq, dckv, dkr], axis=1)
    dw_in, = _matmul("e_proj_dw", Mat(h0, t, d), Mat(dproj, t, pi), "tn", [_out(d, pi, F32)], _pick(d, 1024), _pick(pi, 1024), _pick(t, 512, 8))
    dh0, = _matmul("e_proj_dx", Mat(dproj, t, pi), Mat(w_in_all, d, pi), "nt", [_out(t, d, F32)], tm, _pick(d, 1024), _pick(pi, 1024))
    dx0, _, dg_e = _norm_bwd("e_norm_bwd", dh0, xs, g_e, dx1, tr)

    gfull = jnp.concatenate([dw_in[:, 2 * gw:2 * gw + c2], _unpad_rope(dw_in[:, 2 * gw + c2:]), dw_in[:, :2 * gw]], axis=1)
    gw_in = _stack_cols(gfull)
    gq = jnp.concatenate([dw_q[:, :hw].reshape(ql, heads, LANES), _unpad_rope(dw_q[:, hw:].reshape(ql, heads, LANES))], axis=-1)
    gw_uq = _stack_cols(gq.reshape(ql, heads * (LANES + ROPE)))
    gkv = jnp.concatenate([dw_kv[:, :hw].reshape(kvl, heads, LANES), dw_kv[:, hw:].reshape(kvl, heads, LANES)], axis=-1)
    gw_ukv = _stack_cols(gkv.reshape(kvl, heads * 2 * LANES))
    sc_e, tok = scatter("e", [gw_in, gw_uq, gw_ukv, dw_eout.reshape(N_CHIPS, mix // N_CHIPS, d)])

    half, after = [], dx0
    for tag, (send, recv, part, lands) in (("m1", sc_m1), ("o", sc_o), ("m0", sc_m0), ("e", sc_e)):
        part, lands = _scatter_wait(tag, send, recv, part, lands, after)
        half += [_chip_sum(p, r) for p, r in zip(part, lands)]
        after = half[-1]
    reduced = [r.reshape(2 * r.shape[1], r.shape[2]) for r in _pair_share(half)]
    r_w1_1, r_w2_1, r_oin, r_oout, r_small, r_w1_0, r_w2_0, r_in, r_uq, r_ukv, r_eout = reduced

    small_like = [e_norm_mix, e_q_norm, e_kv_norm, e_v_norm, e_sgu_w, e_sgu_b, e_mla_out_norm, e_sgu_out_norm, mlp_norm, final_norm]
    small_grads = [dg_e, dg_q, dg_kv, dg_vn, dsgu_w, dsgu_b8[:, 0, :], dg_mla, dg_sgu, jnp.concatenate([dg_m0, dg_m1], axis=0), dg_f]
    sflat = _pack_small(small_grads)
    pad = (-sflat.shape[0]) % 8
    sflat = jnp.pad(sflat, ((0, pad), (0, 0)))
    g_small = _sum_devices(_gather_all_devices(sflat))

    def padded(arrs):
        return jnp.pad(_pack_small(arrs), ((0, pad), (0, 0)))

    s_m = [m_e_norm_mix, m_e_q_norm, m_e_kv_norm, m_e_v_norm, m_e_sgu_w, m_e_sgu_b, m_e_mla_out_norm, m_e_sgu_out_norm, m_mlp_norm, m_final_norm]
    s_v = [v_e_norm_mix, v_e_q_norm, v_e_kv_norm, v_e_v_norm, v_e_sgu_w, v_e_sgu_b, v_e_mla_out_norm, v_e_sgu_out_norm, v_mlp_norm, v_final_norm]
    s_out = [_unpack_small(o[0], small_like)
             for o in _adamw(padded(small_like)[None], [g_small], padded(s_m)[None], padded(s_v)[None])]

    big = {
        "e_w_in": _adamw(e_w_in, [r_in], m_e_w_in, v_e_w_in),
        "e_w_uq": _adamw(e_w_uq, [r_uq], m_e_w_uq, v_e_w_uq),
        "e_w_ukv": _adamw(e_w_ukv, [r_ukv], m_e_w_ukv, v_e_w_ukv),
        "e_w_out": _adamw(e_w_out, [r_eout], m_e_w_out, v_e_w_out),
        "o_w_in": _adamw(o_w_in, [r_oin], m_o_w_in, v_o_w_in),
        "o_w_out": _adamw(o_w_out, [r_oout], m_o_w_out, v_o_w_out),
        "mlp_w1": _adamw(mlp_w1, [r_w1_0, r_w1_1], m_mlp_w1, v_mlp_w1),
        "mlp_w2": _adamw(mlp_w2, [r_w2_0, r_w2_1], m_mlp_w2, v_mlp_w2),
    }
    sm = [o[0] for o in _adamw(small_shard[None], [r_small], _small_shard(m_o_norm_mix, m_o_conv_w[0])[None],
                               _small_shard(v_o_norm_mix, v_o_conv_w[0])[None])]

    names = ["e_norm_mix", "e_w_in", "e_q_norm", "e_w_uq", "e_kv_norm", "e_w_ukv", "e_v_norm", "e_sgu_w", "e_sgu_b",
             "e_mla_out_norm", "e_sgu_out_norm", "e_w_out", "o_norm_mix", "o_w_in", "o_conv_w", "o_w_out",
             "mlp_norm", "mlp_w1", "mlp_w2", "final_norm"]
    shapes = {"e_w_in": e_w_in.shape, "e_w_uq": e_w_uq.shape, "e_w_ukv": e_w_ukv.shape, "e_w_out": e_w_out.shape,
              "o_w_in": o_w_in.shape, "o_w_out": o_w_out.shape, "mlp_w1": mlp_w1.shape, "mlp_w2": mlp_w2.shape}
    small_names = ["e_norm_mix", "e_q_norm", "e_kv_norm", "e_v_norm", "e_sgu_w", "e_sgu_b", "e_mla_out_norm",
                   "e_sgu_out_norm", "mlp_norm", "final_norm"]

    def leaf(kind, name):
        if name in big:
            return big[name][kind].reshape(shapes[name])
        if name == "o_norm_mix":
            return sm[kind][0:1]
        if name == "o_conv_w":
            return sm[kind][16:19].reshape(o_conv_w.shape)
        return s_out[kind][small_names.index(name)]

    outs = [loss, dx0.reshape(x.shape)]
    for kind in range(4):
        outs += [leaf(kind, nm) for nm in names]
    return tuple(outs)


def _gcd(a, b):
    while b:
        a, b = b, a % b
    return a
```

```python
import functools

import jax
import jax.numpy as jnp
from jax import lax
from jax.experimental import pallas as pl
from jax.experimental.pallas import tpu as pltpu

F32 = jnp.float32
BF16 = jnp.bfloat16
MESH = pl.DeviceIdType.MESH

LANES = 128
ROPE = 64
ROPE_HALF = ROPE // 2
ROPE_BASE = 10000.0
EPS = 1e-6
N_CHIPS = 4
VMEM_LIMIT = 48 * 1024 * 1024
NEG = -1e30

ADAM_LR = 0.001
ADAM_B1 = 0.9
ADAM_B2 = 0.999
ADAM_EPS = 1e-08
ADAM_WD = 0.01
ADAM_STEP = 10


def _pick(n, target, step=LANES):
    best = None
    for t in range(step, min(n, target) + 1, step):
        if n % t == 0:
            best = t
    return best if best is not None else n


def _params(sem, vmem=VMEM_LIMIT):
    return pltpu.CompilerParams(dimension_semantics=sem, vmem_limit_bytes=vmem)


class Mat:
    def __init__(self, arr, rows, cols, kind="plain", lead=(), col_off=0, shape=None, dtype=None):
        self.arr, self.rows, self.cols, self.kind, self.lead, self.col_off = arr, rows, cols, kind, tuple(lead), col_off
        self.shape = tuple(arr.shape) if arr is not None else tuple(shape)
        self.dtype = arr.dtype if arr is not None else dtype

    def sds(self):
        return jax.ShapeDtypeStruct(self.shape, self.dtype)

    def spec(self, br, bc, gridmap):
        lead, nl = self.lead, len(self.lead)
        if self.kind == "plain":
            assert self.col_off % bc == 0 and self.rows % br == 0 and self.cols % bc == 0, (self.shape, br, bc)
            off = self.col_off // bc
            block = (None,) * nl + (br, bc)

            def phys(rb, cb):
                return lead + (rb, cb + off)
        elif self.kind == "colstack":
            cs = self.shape[-1]
            assert cs % bc == 0 and self.rows % br == 0, (self.shape, br, bc)
            q = cs // bc
            block = (None,) * (nl + 1) + (br, bc)

            def phys(rb, cb):
                return (cb // q,) + lead + (rb, cb % q)
        else:
            rs = self.shape[-2]
            assert rs % br == 0 and self.cols % bc == 0, (self.shape, br, bc)
            q = rs // br
            block = (None,) * (nl + 1) + (br, bc)

            def phys(rb, cb):
                return (rb // q,) + lead + (rb % q, cb)

        return pl.BlockSpec(block, lambda *g: phys(*gridmap(*g)))


def _matmul(name, a, b, mode, outs, tm, tn, tk, epilogue=None, extras=(), deps=()):
    if mode == "nn":
        m, k, n = a.rows, a.cols, b.cols
        a_spec = a.spec(tm, tk, lambda i, j, kk: (i, kk))
        b_spec = b.spec(tk, tn, lambda i, j, kk: (kk, j))
        dims = (((1,), (0,)), ((), ()))
    elif mode == "nt":
        m, k, n = a.rows, a.cols, b.rows
        a_spec = a.spec(tm, tk, lambda i, j, kk: (i, kk))
        b_spec = b.spec(tn, tk, lambda i, j, kk: (j, kk))
        dims = (((1,), (1,)), ((), ()))
    else:
        k, m, n = a.rows, a.cols, b.cols
        a_spec = a.spec(tk, tm, lambda i, j, kk: (kk, i))
        b_spec = b.spec(tk, tn, lambda i, j, kk: (kk, j))
        dims = (((0,), (0,)), ((), ()))
    assert m % tm == 0 and n % tn == 0 and k % tk == 0, (name, m, n, k, tm, tn, tk)
    grid = (m // tm, n // tn, k // tk)
    nk = grid[2]
    n_ex, n_out, n_dep = len(extras), len(outs), len(deps)
    tile = lambda i, j, kk: (i, j)

    def finish(z, ex, out_refs):
        vals = epilogue(z, *[e[...] for e in ex]) if epilogue is not None else (z,)
        for o, v in zip(out_refs, vals):
            o[...] = v.astype(o.dtype)

    def body_single(a_ref, b_ref, *rest):
        finish(lax.dot_general(a_ref[...], b_ref[...], dims, preferred_element_type=F32),
               rest[:n_ex], rest[n_ex + n_dep:n_ex + n_dep + n_out])

    def body_acc(a_ref, b_ref, *rest):
        acc = rest[-1]
        kk = pl.program_id(2)

        @pl.when(kk == 0)
        def _():
            acc[...] = jnp.zeros_like(acc)

        acc[...] += lax.dot_general(a_ref[...], b_ref[...], dims, preferred_element_type=F32)

        @pl.when(kk == nk - 1)
        def _():
            finish(acc[...], rest[:n_ex], rest[n_ex + n_dep:n_ex + n_dep + n_out])

    res = pl.pallas_call(
        body_single if nk == 1 else body_acc, name=name, grid=grid,
        in_specs=[a_spec, b_spec] + [e.spec(tm, tn, tile) for e in extras]
        + [pl.BlockSpec(memory_space=pl.ANY) for _ in deps],
        out_specs=[o.spec(tm, tn, tile) for o in outs],
        out_shape=[o.sds() for o in outs],
        scratch_shapes=[] if nk == 1 else [pltpu.VMEM((tm, tn), F32)],
        compiler_params=_params(("parallel", "parallel", "arbitrary")),
    )(a.arr, b.arr, *[e.arr for e in extras], *deps)
    return res


def _out(rows, cols, dtype, kind="plain", lead=(), shape=None):
    return Mat(None, rows, cols, kind, lead, shape=shape if shape is not None else (rows, cols), dtype=dtype)


def _rt(arr, tr, width=None, cb=0):
    width = arr.shape[1] if width is None else width
    return arr, pl.BlockSpec((tr, width), lambda i: (i, cb))


def _whole(arr):
    nd = arr.ndim
    return arr, pl.BlockSpec(arr.shape, lambda i: (0,) * nd)


def _rowwise(name, fn, n_steps, ins, outs, accs=(), deps=()):
    n_in, n_out, n_acc, n_dep = len(ins), len(outs), len(accs), len(deps)

    def body(*refs):
        vals = fn(*[r[...] for r in refs[:n_in]])
        if not isinstance(vals, (tuple, list)):
            vals = (vals,)
        for ref, v in zip(refs[n_in + n_dep:n_in + n_dep + n_out], vals[:n_out]):
            ref[...] = v.astype(ref.dtype)
        if n_acc:
            acc_refs = refs[n_in + n_dep + n_out:]

            @pl.when(pl.program_id(0) == 0)
            def _():
                for ref in acc_refs:
                    ref[...] = jnp.zeros_like(ref)

            for ref, v in zip(acc_refs, vals[n_out:]):
                ref[...] += v

    acc_specs = [pl.BlockSpec(s.shape, lambda i, nd=len(s.shape): (0,) * nd) for s in accs]
    res = pl.pallas_call(
        body, name=name, grid=(n_steps,),
        in_specs=[s for _, s in ins] + [pl.BlockSpec(memory_space=pl.ANY) for _ in deps],
        out_specs=[s for _, s in outs] + acc_specs,
        out_shape=[o for o, _ in outs] + list(accs),
        compiler_params=_params(("arbitrary",) if n_acc else ("parallel",)),
    )(*[a for a, _ in ins], *deps)
    return res


def _rt_out(t, width, dtype, tr):
    return jax.ShapeDtypeStruct((t, width), dtype), pl.BlockSpec((tr, width), lambda i: (i, 0))


def _rms(x, g):
    r = lax.rsqrt(jnp.mean(x * x, axis=-1, keepdims=True) + EPS)
    return x * r * g


def _rms_bwd(dy, x, g):
    r = lax.rsqrt(jnp.mean(x * x, axis=-1, keepdims=True) + EPS)
    xh = x * r
    dxh = dy * g
    dx = r * (dxh - xh * jnp.mean(dxh * xh, axis=-1, keepdims=True))
    dg = jnp.sum(dy * xh, axis=0, keepdims=True)
    return dx, dg


def _gelu(x):
    k = 0.7978845608028654
    th = jnp.tanh(k * (x + 0.044715 * (x * x * x)))
    return x * (0.5 * (1.0 + th))


def _gelu_grad(x):
    k = 0.7978845608028654
    x2 = x * x
    th = jnp.tanh(k * (x + 0.044715 * (x2 * x)))
    return 0.5 * (1.0 + th) + 0.5 * x * (1.0 - th * th) * (k * (1.0 + 3.0 * 0.044715 * x2))


def _norm_fwd(name, x, g, tr):
    t, d = x.shape
    return _rowwise(name, lambda xv, gv: _rms(xv, gv), t // tr, [_rt(x, tr), _whole(g)], [_rt_out(t, d, BF16, tr)])[0]


def _norm_bwd(name, dh, x, g, dres, tr):
    t, d = x.shape

    def fn(dhv, xv, gv, drv):
        dx, dg = _rms_bwd(dhv, xv, gv)
        dx = dx + drv
        return dx, dx, dg

    return _rowwise(name, fn, t // tr, [_rt(dh, tr), _rt(x, tr), _whole(g), _rt(dres, tr)],
                    [_rt_out(t, d, F32, tr), _rt_out(t, d, BF16, tr)], [jax.ShapeDtypeStruct((1, d), F32)])


def _rope_tables(posf, invf, cmask, smask, tr):
    t = posf.shape[0]

    def fn(p, f, cm, sm):
        ang = p * f
        return jnp.cos(ang) * cm, jnp.sin(ang) * sm

    return _rowwise("rope_tables", fn, t // tr, [_rt(posf, tr), _whole(invf), _whole(cmask), _whole(smask)],
                    [_rt_out(t, LANES, F32, tr), _rt_out(t, LANES, F32, tr)])


def _rot(v, c, s):
    return v * c + pltpu.roll(v, ROPE, axis=1) * s


def _rot_bwd(dv, c, s):
    return dv * c + pltpu.roll(dv * s, ROPE, axis=1)


def _rope_fwd(qfull, proj, kr_cb, ctab, stab, heads, tr):
    t = qfull.shape[0]
    hw = heads * LANES

    def fn(q, kr, c, s):
        parts = [q[:, :hw]] + [_rot(q[:, hw + h * LANES: hw + (h + 1) * LANES], c, s) for h in range(heads)]
        return jnp.concatenate(parts, axis=1), _rot(kr, c, s)

    return _rowwise("rope_fwd", fn, t // tr, [_rt(qfull, tr), _rt(proj, tr, LANES, kr_cb), _rt(ctab, tr), _rt(stab, tr)],
                    [_rt_out(t, 2 * hw, BF16, tr), _rt_out(t, LANES, BF16, tr)])


def _rope_bwd(dq1, dq2, dkr_h, ctab, stab, heads, tr):
    t = dq1.shape[0]
    hw = heads * LANES

    def fn(a, b, dk, c, s):
        parts = [a] + [_rot_bwd(b[:, h * LANES:(h + 1) * LANES], c, s) for h in range(heads)]
        dks = dk[0]
        for h in range(1, heads):
            dks = dks + dk[h]
        return jnp.concatenate(parts, axis=1), _rot_bwd(dks, c, s)

    dk_spec = pl.BlockSpec((heads, tr, LANES), lambda i: (0, i, 0))
    return _rowwise("rope_bwd", fn, t // tr, [_rt(dq1, tr), _rt(dq2, tr), (dkr_h, dk_spec), _rt(ctab, tr), _rt(stab, tr)],
                    [_rt_out(t, 2 * hw, BF16, tr), _rt_out(t, LANES, BF16, tr)])


def _dot_nt(a, b):
    return lax.dot_general(a, b, (((1,), (1,)), ((), ())), preferred_element_type=F32)


def _dot_tn(a, b):
    return lax.dot_general(a, b, (((0,), (0,)), ((), ())), preferred_element_type=F32)


def _dot(a, b):
    return jnp.dot(a, b, preferred_element_type=F32)


def _causal(i, j, tq, tk):
    rows = i * tq + lax.broadcasted_iota(jnp.int32, (tq, tk), 0)
    cols = j * tk + lax.broadcasted_iota(jnp.int32, (tq, tk), 1)
    return cols <= rows


def _attn_fwd(qall, kvall, kr, heads, scale, tq):
    t = qall.shape[0]
    nq = t // tq
    tk = tq

    def body(qn_ref, qr_ref, kn_ref, v_ref, kr_ref, o_ref, lse_ref, m_ref, l_ref, acc_ref):
        i = pl.program_id(1)
        qn, qr = qn_ref[...], qr_ref[...]
        m_ref[...] = jnp.full_like(m_ref, NEG)
        l_ref[...] = jnp.zeros_like(l_ref)
        acc_ref[...] = jnp.zeros_like(acc_ref)

        def step(j, carry):
            ks = pl.multiple_of(j * tk, tk)
            s = (_dot_nt(qn, kn_ref[pl.ds(ks, tk), :]) + _dot_nt(qr, kr_ref[pl.ds(ks, tk), :])) * scale
            s = jnp.where(_causal(i, j, tq, tk), s, NEG)
            m_prev = m_ref[...]
            m_new = jnp.maximum(m_prev, jnp.max(s, axis=-1, keepdims=True))
            p = jnp.exp(s - m_new[:, :1])
            alpha = jnp.exp(m_prev - m_new)
            l_ref[...] = alpha * l_ref[...] + jnp.sum(p, axis=-1, keepdims=True)
            acc_ref[...] = alpha * acc_ref[...] + _dot(p.astype(BF16), v_ref[pl.ds(ks, tk), :])
            m_ref[...] = m_new
            return carry

        lax.fori_loop(0, i + 1, step, 0)
        o_ref[...] = acc_ref[...] / l_ref[...]
        lse_ref[...] = m_ref[...] + jnp.log(l_ref[...])

    return pl.pallas_call(
        body, name="attn_fwd", grid=(heads, nq),
        in_specs=[pl.BlockSpec((tq, LANES), lambda h, i: (i, h)),
                  pl.BlockSpec((tq, LANES), lambda h, i: (i, heads + h)),
                  pl.BlockSpec((t, LANES), lambda h, i: (0, h)),
                  pl.BlockSpec((t, LANES), lambda h, i: (0, heads + h)),
                  pl.BlockSpec((t, LANES), lambda h, i: (0, 0))],
        out_specs=[pl.BlockSpec((tq, LANES), lambda h, i: (i, h)),
                   pl.BlockSpec((None, tq, LANES), lambda h, i: (h, i, 0))],
        out_shape=[jax.ShapeDtypeStruct((t, heads * LANES), F32), jax.ShapeDtypeStruct((heads, t, LANES), F32)],
        scratch_shapes=[pltpu.VMEM((tq, LANES), F32)] * 3,
        compiler_params=_params(("parallel", "arbitrary")),
    )(qall, qall, kvall, kvall, kr)


def _attn_dq(qall, kvall, kr, do, lse, delta, heads, scale, tq):
    t = qall.shape[0]
    nq = t // tq
    tk = tq

    def body(qn_ref, qr_ref, kn_ref, v_ref, kr_ref, do_ref, lse_ref, dl_ref, dq1_ref, dq2_ref, a1_ref, a2_ref):
        i = pl.program_id(1)
        qn, qr, do_v = qn_ref[...], qr_ref[...], do_ref[...]
        lse_v, dl_v = lse_ref[...][:, :1], dl_ref[...][:, :1]
        a1_ref[...] = jnp.zeros_like(a1_ref)
        a2_ref[...] = jnp.zeros_like(a2_ref)

        def step(j, carry):
            ks = pl.multiple_of(j * tk, tk)
            k1, k2 = kn_ref[pl.ds(ks, tk), :], kr_ref[pl.ds(ks, tk), :]
            s = (_dot_nt(qn, k1) + _dot_nt(qr, k2)) * scale
            p = jnp.where(_causal(i, j, tq, tk), jnp.exp(s - lse_v), 0.0)
            dp = _dot_nt(do_v, v_ref[pl.ds(ks, tk), :])
            ds = (p * (dp - dl_v) * scale).astype(BF16)
            a1_ref[...] += _dot(ds, k1)
            a2_ref[...] += _dot(ds, k2)
            return carry

        lax.fori_loop(0, i + 1, step, 0)
        dq1_ref[...] = a1_ref[...]
        dq2_ref[...] = a2_ref[...]

    qblk = lambda off: pl.BlockSpec((tq, LANES), lambda h, i: (i, off + h))
    full = lambda off: pl.BlockSpec((t, LANES), lambda h, i: (0, off + h))
    stat = pl.BlockSpec((None, tq, LANES), lambda h, i: (h, i, 0))
    return pl.pallas_call(
        body, name="attn_dq", grid=(heads, nq),
        in_specs=[qblk(0), qblk(heads), full(0), full(heads), pl.BlockSpec((t, LANES), lambda h, i: (0, 0)),
                  qblk(0), stat, stat],
        out_specs=[qblk(0), qblk(0)],
        out_shape=[jax.ShapeDtypeStruct((t, heads * LANES), F32)] * 2,
        scratch_shapes=[pltpu.VMEM((tq, LANES), F32)] * 2,
        compiler_params=_params(("parallel", "arbitrary")),
    )(qall, qall, kvall, kvall, kr, do, lse, delta)


def _attn_dkv(qall, kvall, kr, do, lse, delta, heads, scale, tq):
    t = qall.shape[0]
    nq = t // tq
    tk = tq

    def body(qn_ref, qr_ref, kn_ref, v_ref, kr_ref, do_ref, lse_ref, dl_ref, dk_ref, dv_ref, dkr_ref, ak, av, akr):
        j = pl.program_id(1)
        k1, k2, vv = kn_ref[...], kr_ref[...], v_ref[...]
        ak[...] = jnp.zeros_like(ak)
        av[...] = jnp.zeros_like(av)
        akr[...] = jnp.zeros_like(akr)

        def step(i, carry):
            qs = pl.multiple_of(i * tq, tq)
            qn, qr, do_v = qn_ref[pl.ds(qs, tq), :], qr_ref[pl.ds(qs, tq), :], do_ref[pl.ds(qs, tq), :]
            lse_v, dl_v = lse_ref[pl.ds(qs, tq), :][:, :1], dl_ref[pl.ds(qs, tq), :][:, :1]
            s = (_dot_nt(qn, k1) + _dot_nt(qr, k2)) * scale
            p = jnp.where(_causal(i, j, tq, tk), jnp.exp(s - lse_v), 0.0)
            dp = _dot_nt(do_v, vv)
            ds = (p * (dp - dl_v) * scale).astype(BF16)
            av[...] += _dot_tn(p.astype(BF16), do_v)
            ak[...] += _dot_tn(ds, qn)
            akr[...] += _dot_tn(ds, qr)
            return carry

        lax.fori_loop(j, nq, step, 0)
        dk_ref[...] = ak[...].astype(dk_ref.dtype)
        dv_ref[...] = av[...].astype(dv_ref.dtype)
        dkr_ref[...] = akr[...]

    kblk = lambda off: pl.BlockSpec((tk, LANES), lambda h, j: (j, off + h))
    full = lambda off: pl.BlockSpec((t, LANES), lambda h, j: (0, off + h))
    stat = pl.BlockSpec((None, t, LANES), lambda h, j: (h, 0, 0))
    return pl.pallas_call(
        body, name="attn_dkv", grid=(heads, nq),
        in_specs=[full(0), full(heads), kblk(0), kblk(heads), pl.BlockSpec((tk, LANES), lambda h, j: (j, 0)),
                  full(0), stat, stat],
        out_specs=[kblk(0), kblk(0), pl.BlockSpec((None, tk, LANES), lambda h, j: (h, j, 0))],
        out_shape=[jax.ShapeDtypeStruct((t, heads * LANES), BF16)] * 2 + [jax.ShapeDtypeStruct((heads, t, LANES), F32)],
        scratch_shapes=[pltpu.VMEM((tk, LANES), F32)] * 3,
        compiler_params=_params(("parallel", "arbitrary")),
    )(qall, qall, kvall, kvall, kr, do, lse, delta)


def _tril():
    return lax.broadcasted_iota(jnp.int32, (LANES, LANES), 0) >= lax.broadcasted_iota(jnp.int32, (LANES, LANES), 1)


def _group_norm(vg):
    mu = jnp.mean(vg, axis=-1, keepdims=True)
    vc = vg - mu
    rs = lax.rsqrt(jnp.mean(vc * vc, axis=-1, keepdims=True) + EPS)
    return vc * rs, rs


def _sgu_fwd(proj, gain, w, bias, groups, rb):
    t = proj.shape[0]
    gw = groups * LANES
    cpb = rb // LANES

    def body(u_ref, v_ref, gain_ref, w_ref, b_ref, s_ref):
        tril = _tril()
        for g in range(groups):
            wt = jnp.where(tril, w_ref[g], 0.0).astype(BF16)
            cols = slice(g * LANES, (g + 1) * LANES)
            for ci in range(cpb):
                rows = slice(ci * LANES, (ci + 1) * LANES)
                ug = _gelu(u_ref[rows, cols])
                vh, _ = _group_norm(_gelu(v_ref[rows, cols]))
                vn = vh * gain_ref[:, cols]
                y = _dot(wt, vn.astype(BF16)) + b_ref[g]
                s_ref[rows, cols] = ug * y

    return pl.pallas_call(
        body, name="sgu_fwd", grid=(t // rb,),
        in_specs=[pl.BlockSpec((rb, gw), lambda i: (i, 0)), pl.BlockSpec((rb, gw), lambda i: (i, 1)),
                  pl.BlockSpec((1, gw), lambda i: (0, 0)),
                  pl.BlockSpec((groups, LANES, LANES), lambda i: (0, 0, 0)),
                  pl.BlockSpec((groups, LANES, LANES), lambda i: (0, 0, 0))],
        out_specs=pl.BlockSpec((rb, gw), lambda i: (i, 0)),
        out_shape=jax.ShapeDtypeStruct((t, gw), F32),
        compiler_params=_params(("parallel",)),
    )(proj, proj, gain, w, bias)


def _sgu_bwd(proj, ds, gain, w, bias, groups, rb):
    t = proj.shape[0]
    gw = groups * LANES
    cpb = rb // LANES
    n_steps = t // rb

    def body(u_ref, v_ref, ds_ref, gain_ref, w_ref, b_ref, du_ref, dv_ref, dw_ref, db_ref, dg_ref, dy_acc):
        step = pl.program_id(0)

        @pl.when(step == 0)
        def _():
            dw_ref[...] = jnp.zeros_like(dw_ref)
            dy_acc[...] = jnp.zeros_like(dy_acc)
            dg_ref[...] = jnp.zeros_like(dg_ref)

        tril = _tril()
        for g in range(groups):
            wt = jnp.where(tril, w_ref[g], 0.0).astype(BF16)
            cols = slice(g * LANES, (g + 1) * LANES)
            gain_g = gain_ref[:, cols]
            for ci in range(cpb):
                rows = slice(ci * LANES, (ci + 1) * LANES)
                u_raw, v_raw, ds_v = u_ref[rows, cols], v_ref[rows, cols], ds_ref[rows, cols]
                ug = _gelu(u_raw)
                vh, rs = _group_norm(_gelu(v_raw))
                vn = (vh * gain_g).astype(BF16)
                y = _dot(wt, vn) + b_ref[g]
                dy = ds_v * ug
                dyb = dy.astype(BF16)
                du_ref[rows, cols] = (ds_v * y * _gelu_grad(u_raw)).astype(du_ref.dtype)
                dy_acc[g] += dy
                dw_ref[g] += _dot_nt(dyb, vn)
                dvn = _dot_tn(wt, dyb)
                dg_ref[:, cols] += jnp.sum(dvn * vh, axis=0, keepdims=True)
                dvh = dvn * gain_g
                dvg = rs * (dvh - jnp.mean(dvh, axis=-1, keepdims=True)
                            - vh * jnp.mean(dvh * vh, axis=-1, keepdims=True))
                dv_ref[rows, cols] = (dvg * _gelu_grad(v_raw)).astype(dv_ref.dtype)

        @pl.when(step == n_steps - 1)
        def _():
            ones = jnp.ones((8, LANES), F32)
            for g in range(groups):
                dw_ref[g] = jnp.where(tril, dw_ref[g], 0.0)
                db_ref[g] = lax.dot_general(ones, dy_acc[g], (((1,), (1,)), ((), ())),
                                            precision=lax.Precision.HIGHEST, preferred_element_type=F32)

    blk = lambda cb: pl.BlockSpec((rb, gw), lambda i: (i, cb))
    whole3 = pl.BlockSpec((groups, LANES, LANES), lambda i: (0, 0, 0))
    return pl.pallas_call(
        body, name="sgu_bwd", grid=(n_steps,),
        in_specs=[blk(0), blk(1), blk(0), pl.BlockSpec((1, gw), lambda i: (0, 0)), whole3, whole3],
        out_specs=[blk(0), blk(0), whole3, pl.BlockSpec((groups, 8, LANES), lambda i: (0, 0, 0)),
                   pl.BlockSpec((1, gw), lambda i: (0, 0))],
        out_shape=[jax.ShapeDtypeStruct((t, gw), BF16), jax.ShapeDtypeStruct((t, gw), BF16),
                   jax.ShapeDtypeStruct((groups, LANES, LANES), F32), jax.ShapeDtypeStruct((groups, 8, LANES), F32),
                   jax.ShapeDtypeStruct((1, gw), F32)],
        scratch_shapes=[pltpu.VMEM((groups, LANES, LANES), F32)],
        compiler_params=_params(("arbitrary",)),
    )(proj, proj, ds, gain, w, bias)


def _shift_down(z, s):
    rows = lax.broadcasted_iota(jnp.int32, z.shape, 0)
    return jnp.where(rows >= s, pltpu.roll(z, s, axis=0), 0.0)


def _shift_up(z, s):
    n = z.shape[0]
    rows = lax.broadcasted_iota(jnp.int32, z.shape, 0)
    return jnp.where(rows < n - s, pltpu.roll(z, n - s, axis=0), 0.0)


def _conv_fwd(proj3, cw, tc):
    _, t, cd = proj3.shape

    def body(p_ref, w_ref, o_ref):
        z = p_ref[1] * p_ref[2]
        w = w_ref[...]
        zc = w[2:3] * z + w[1:2] * _shift_down(z, 1) + w[0:1] * _shift_down(z, 2)
        o_ref[...] = (p_ref[0] * zc).astype(o_ref.dtype)

    return pl.pallas_call(
        body, name="conv_fwd", grid=(cd // tc,),
        in_specs=[pl.BlockSpec((3, t, tc), lambda j: (0, 0, j)), pl.BlockSpec((8, tc), lambda j: (0, j))],
        out_specs=pl.BlockSpec((t, tc), lambda j: (0, j)),
        out_shape=jax.ShapeDtypeStruct((t, cd), BF16),
        compiler_params=_params(("parallel",)),
    )(proj3, cw)


def _conv_bwd(proj3, cw, dbz, tc):
    _, t, cd = proj3.shape

    def body(p_ref, w_ref, d_ref, o_ref, dw_ref):
        b, c, xin = p_ref[0], p_ref[1], p_ref[2]
        w = w_ref[...]
        z = c * xin
        z1, z2 = _shift_down(z, 1), _shift_down(z, 2)
        zc = w[2:3] * z + w[1:2] * z1 + w[0:1] * z2
        d = d_ref[...]
        dzc = d * b
        dz = w[2:3] * dzc + w[1:2] * _shift_up(dzc, 1) + w[0:1] * _shift_up(dzc, 2)
        o_ref[0] = (d * zc).astype(o_ref.dtype)
        o_ref[1] = (dz * xin).astype(o_ref.dtype)
        o_ref[2] = (dz * c).astype(o_ref.dtype)
        row = lax.broadcasted_iota(jnp.int32, (8, tc), 0)
        dw0 = jnp.sum(dzc * z2, axis=0, keepdims=True)
        dw1 = jnp.sum(dzc * z1, axis=0, keepdims=True)
        dw2 = jnp.sum(dzc * z, axis=0, keepdims=True)
        dw_ref[...] = jnp.where(row == 0, dw0, 0.0) + jnp.where(row == 1, dw1, 0.0) + jnp.where(row == 2, dw2, 0.0)

    return pl.pallas_call(
        body, name="conv_bwd", grid=(cd // tc,),
        in_specs=[pl.BlockSpec((3, t, tc), lambda j: (0, 0, j)), pl.BlockSpec((8, tc), lambda j: (0, j)),
                  pl.BlockSpec((t, tc), lambda j: (0, j))],
        out_specs=[pl.BlockSpec((3, t, tc), lambda j: (0, 0, j)), pl.BlockSpec((8, tc), lambda j: (0, j))],
        out_shape=[jax.ShapeDtypeStruct((3, t, cd), BF16), jax.ShapeDtypeStruct((8, cd), F32)],
        compiler_params=_params(("parallel",)),
    )(proj3, cw, dbz)


def _place():
    x, y, c = lax.axis_index("x"), lax.axis_index("y"), lax.axis_index("c")
    chips = [(1 - x, y), (x, 1 - y), (1 - x, 1 - y)]
    return x, y, c, chips


def _any_specs(n):
    return [pl.BlockSpec(memory_space=pl.ANY) for _ in range(n)]


HBM_SPEC = pl.BlockSpec(memory_space=pltpu.HBM)
SEM_SPEC = pl.BlockSpec(memory_space=pltpu.SEMAPHORE)
ORDERED_EFFECT = pltpu.SideEffectType.DATAFLOW_SIDE_EFFECTING


def _in_hbm(a):
    return pltpu.with_memory_space_constraint(a, pltpu.HBM)


def _token():
    return jax.ShapeDtypeStruct((8, LANES), F32), pl.BlockSpec(memory_space=pltpu.VMEM)


def _gather_start(groups):
    sizes = [len(g) for g in groups]
    flat = [b for g in groups for b in g]
    n, ng = len(flat), len(groups)

    def body(*refs):
        ins, sems, token = refs[:n], refs[n:n + 2 * ng], refs[-1]
        x, y, c, chips = _place()
        me = 2 * x + y
        i = 0
        for gi, size in enumerate(sizes):
            for j in range(size):
                blk = ins[i].at[me, c]
                for k, chip in enumerate(chips):
                    pltpu.make_async_remote_copy(src_ref=blk, dst_ref=blk, send_sem=sems[2 * gi].at[3 * j + k],
                                                 recv_sem=sems[2 * gi + 1].at[3 * j + k],
                                                 device_id=(*chip, c), device_id_type=MESH).start()
                i += 1
        token[...] = jnp.zeros_like(token)

    tok_shape, tok_spec = _token()
    res = pl.pallas_call(
        body, name="gather_start",
        in_specs=[HBM_SPEC] * n,
        out_specs=[SEM_SPEC] * (2 * ng) + [HBM_SPEC] * n + [tok_spec],
        out_shape=[pltpu.SemaphoreType.DMA((3 * size,)) for size in sizes for _ in (0, 1)]
        + [pltpu.HBM(b.shape, b.dtype) for b in flat] + [tok_shape],
        input_output_aliases={i: 2 * ng + i for i in range(n)},
        compiler_params=pltpu.CompilerParams(has_side_effects=ORDERED_EFFECT),
    )(*[_in_hbm(b) for b in flat])
    out, i = [], 2 * ng
    for gi, size in enumerate(sizes):
        out.append((res[2 * gi], res[2 * gi + 1], list(res[i:i + size])))
        i += size
    return out, res[-1]


def _gather_wait(tag, send, recv, bufs, after):
    n = len(bufs)

    def body(*refs):
        ins, send_ref, recv_ref = refs[:n], refs[n], refs[n + 1]
        x, y, c, chips = _place()
        me = 2 * x + y
        for j in range(n):
            for k, (px, py) in enumerate(chips):
                cp = pltpu.make_async_remote_copy(src_ref=ins[j].at[me, c], dst_ref=ins[j].at[2 * px + py, c],
                                                  send_sem=send_ref.at[3 * j + k], recv_sem=recv_ref.at[3 * j + k],
                                                  device_id=(px, py, c), device_id_type=MESH)
                cp.wait_send()
                cp.wait_recv()

    return pl.pallas_call(
        body, name="gather_wait_" + tag,
        in_specs=[HBM_SPEC] * n + [SEM_SPEC, SEM_SPEC, pl.BlockSpec(memory_space=pl.ANY)],
        out_specs=[HBM_SPEC] * n,
        out_shape=[pltpu.HBM(b.shape, b.dtype) for b in bufs],
        input_output_aliases={i: i for i in range(n)},
        compiler_params=pltpu.CompilerParams(has_side_effects=ORDERED_EFFECT),
    )(*bufs, send, recv, after)


def _gather_forward(tag, bufs):
    n = len(bufs)

    def body(*refs):
        ins, outs = refs[:n], refs[n:2 * n]
        send, recv = refs[2 * n:]
        x, y, c, chips = _place()
        sib = (x, y, 1 - c)

        def cp(i, k, slot, half):
            return pltpu.make_async_remote_copy(src_ref=ins[i].at[slot, half], dst_ref=outs[i].at[slot, half],
                                                send_sem=send.at[3 * i + k], recv_sem=recv.at[3 * i + k],
                                                device_id=sib, device_id_type=MESH)

        cps = [cp(i, k, 2 * px + py, c) for i in range(n) for k, (px, py) in enumerate(chips)]
        for d in cps:
            d.start()
        for i in range(n):
            for k, (px, py) in enumerate(chips):
                cp(i, k, 2 * px + py, 1 - c).wait_recv()
        for d in cps:
            d.wait_send()

    return pl.pallas_call(
        body, name="gather_forward_" + tag,
        in_specs=_any_specs(n), out_specs=_any_specs(n),
        out_shape=[jax.ShapeDtypeStruct(b.shape, b.dtype) for b in bufs],
        scratch_shapes=[pltpu.SemaphoreType.DMA((3 * n,))] * 2,
        input_output_aliases={i: i for i in range(n)},
        compiler_params=pltpu.CompilerParams(has_side_effects=True),
    )(*bufs)


def _pair_exchange(tag, entries):
    n = len(entries)

    def body(*refs):
        ins, outs = refs[:n], refs[n:2 * n]
        send, recv = refs[2 * n:]
        x, y, c, _ = _place()
        sib = (x, y, 1 - c)

        def cp(i, j):
            return pltpu.make_async_remote_copy(src_ref=ins[i].at[j, 1 - c], dst_ref=outs[i].at[j],
                                                send_sem=send.at[N_CHIPS * i + j], recv_sem=recv.at[N_CHIPS * i + j],
                                                device_id=sib, device_id_type=MESH)

        cps = [cp(i, j) for i in range(n) for j in range(N_CHIPS)]
        for d in cps:
            d.start()
        for d in cps:
            d.wait_recv()
        for d in cps:
            d.wait_send()

    return pl.pallas_call(
        body, name="grad_pair_exchange_" + tag,
        in_specs=_any_specs(n), out_specs=_any_specs(n),
        out_shape=[jax.ShapeDtypeStruct((N_CHIPS,) + e.shape[2:], e.dtype) for e in entries],
        scratch_shapes=[pltpu.SemaphoreType.DMA((N_CHIPS * n,))] * 2,
        compiler_params=pltpu.CompilerParams(has_side_effects=True),
    )(*entries)


def _scatter_start(tag, parts):
    n = len(parts)
    lands = [lax.empty((3,) + p.shape[1:], p.dtype) for p in parts]

    def body(*refs):
        ins, zones, send, recv, token = refs[:n], refs[n:2 * n], refs[2 * n], refs[2 * n + 1], refs[-1]
        x, y, c, chips = _place()
        for i in range(n):
            for k, (px, py) in enumerate(chips):
                pltpu.make_async_remote_copy(src_ref=ins[i].at[2 * px + py], dst_ref=zones[i].at[k],
                                             send_sem=send.at[3 * i + k], recv_sem=recv.at[3 * i + k],
                                             device_id=(px, py, c), device_id_type=MESH).start()
        token[...] = jnp.zeros_like(token)

    tok_shape, tok_spec = _token()
    res = pl.pallas_call(
        body, name="scatter_start_" + tag,
        in_specs=[HBM_SPEC] * (2 * n),
        out_specs=[SEM_SPEC, SEM_SPEC] + [HBM_SPEC] * (2 * n) + [tok_spec],
        out_shape=[pltpu.SemaphoreType.DMA((3 * n,))] * 2 + [pltpu.HBM(a.shape, a.dtype) for a in parts + lands] + [tok_shape],
        input_output_aliases={i: 2 + i for i in range(2 * n)},
        compiler_params=pltpu.CompilerParams(has_side_effects=ORDERED_EFFECT),
    )(*[_in_hbm(a) for a in parts + lands])
    return (res[0], res[1], list(res[2:2 + n]), list(res[2 + n:2 + 2 * n])), res[-1]


def _scatter_wait(tag, send, recv, parts, lands, after):
    n = len(parts)

    def body(*refs):
        ins, zones, send_ref, recv_ref = refs[:n], refs[n:2 * n], refs[2 * n], refs[2 * n + 1]
        x, y, c, chips = _place()
        for i in range(n):
            for k, (px, py) in enumerate(chips):
                cp = pltpu.make_async_remote_copy(src_ref=ins[i].at[2 * px + py], dst_ref=zones[i].at[k],
                                                  send_sem=send_ref.at[3 * i + k], recv_sem=recv_ref.at[3 * i + k],
                                                  device_id=(px, py, c), device_id_type=MESH)
                cp.wait_send()
                cp.wait_recv()

    res = pl.pallas_call(
        body, name="scatter_wait_" + tag,
        in_specs=[HBM_SPEC] * (2 * n) + [SEM_SPEC, SEM_SPEC, pl.BlockSpec(memory_space=pl.ANY)],
        out_specs=[HBM_SPEC] * (2 * n),
        out_shape=[pltpu.HBM(a.shape, a.dtype) for a in parts + lands],
        input_output_aliases={i: i for i in range(2 * n)},
        compiler_params=pltpu.CompilerParams(has_side_effects=ORDERED_EFFECT),
    )(*parts, *lands, send, recv, after)
    return list(res[:n]), list(res[n:])


def _pair_share(tag, bufs):
    n = len(bufs)

    def body(*refs):
        ins, outs = refs[:n], refs[n:2 * n]
        send, recv = refs[2 * n:]
        x, y, c, _ = _place()
        sib = (x, y, 1 - c)

        def cp(i, half):
            return pltpu.make_async_remote_copy(src_ref=ins[i].at[half], dst_ref=outs[i].at[half],
                                                send_sem=send.at[i], recv_sem=recv.at[i],
                                                device_id=sib, device_id_type=MESH)

        cps = [cp(i, c) for i in range(n)]
        for d in cps:
            d.start()
        for i in range(n):
            cp(i, 1 - c).wait_recv()
        for d in cps:
            d.wait_send()

    return pl.pallas_call(
        body, name="grad_pair_share_" + tag,
        in_specs=_any_specs(n), out_specs=_any_specs(n),
        out_shape=[jax.ShapeDtypeStruct(b.shape, b.dtype) for b in bufs],
        scratch_shapes=[pltpu.SemaphoreType.DMA((n,))] * 2,
        input_output_aliases={i: i for i in range(n)},
        compiler_params=pltpu.CompilerParams(has_side_effects=True),
    )(*bufs)


def _gather_all_devices(v):
    def body(v_ref, o_ref, send, recv, loc):
        x, y, c, _ = _place()
        me = 4 * x + 2 * y + c
        own = pltpu.make_async_copy(v_ref, o_ref.at[me], loc)
        own.start()
        rels = [(fx, fy, fc) for fx in (0, 1) for fy in (0, 1) for fc in (0, 1)][1:]

        def peer(fx, fy, fc):
            return (x + fx - 2 * x * fx, y + fy - 2 * y * fy, c + fc - 2 * c * fc)

        def cp(r, slot, dev):
            return pltpu.make_async_remote_copy(src_ref=v_ref, dst_ref=o_ref.at[slot], send_sem=send.at[r],
                                                recv_sem=recv.at[r], device_id=dev, device_id_type=MESH)

        cps = [cp(r, me, peer(*f)) for r, f in enumerate(rels)]
        for d in cps:
            d.start()
        for r, f in enumerate(rels):
            px, py, pc = peer(*f)
            cp(r, 4 * px + 2 * py + pc, (px, py, pc)).wait_recv()
        for d in cps:
            d.wait_send()
        own.wait()

    return pl.pallas_call(
        body, name="gather_small_grads",
        in_specs=_any_specs(1), out_specs=_any_specs(1)[0],
        out_shape=jax.ShapeDtypeStruct((8,) + v.shape, v.dtype),
        scratch_shapes=[pltpu.SemaphoreType.DMA((7,)), pltpu.SemaphoreType.DMA((7,)), pltpu.SemaphoreType.DMA],
        compiler_params=pltpu.CompilerParams(has_side_effects=True),
    )(v)


def _row_tile(rows, cols, itemsize=4, budget=2 * 1024 * 1024):
    best = None
    for t in range(8, rows + 1, 8):
        if rows % t == 0 and t * cols * itemsize <= budget:
            best = t
    return best if best is not None else rows


def _my_chip():
    return 2 * lax.axis_index("x") + lax.axis_index("y")


def _pair_sum(g5, gsib):
    _, _, rh, cols = g5.shape
    tr = _row_tile(rh, cols)

    def body(a_ref, b_ref, o_ref):
        o_ref[...] = (a_ref[...].astype(F32) + b_ref[...].astype(F32)).astype(o_ref.dtype)

    return pl.pallas_call(body, name="grad_pair_sum", grid=(N_CHIPS, rh // tr),
                          in_specs=[pl.BlockSpec((None, None, tr, cols), lambda j, r: (j, lax.axis_index("c"), r, 0)),
                                    pl.BlockSpec((None, tr, cols), lambda j, r: (j, r, 0))],
                          out_specs=pl.BlockSpec((None, tr, cols), lambda j, r: (j, r, 0)),
                          out_shape=jax.ShapeDtypeStruct((N_CHIPS, rh, cols), BF16),
                          compiler_params=_params(("parallel", "parallel")))(g5, gsib)


def _chip_sum(part, recv):
    _, rh, cols = part.shape
    tr = _row_tile(rh, cols)

    def body(a_ref, b_ref, o_ref):
        acc = a_ref[...].astype(F32)
        for k in range(3):
            acc = acc + b_ref[k].astype(F32)
        o_ref[...] = acc

    return pl.pallas_call(body, name="grad_chip_sum", grid=(rh // tr,),
                          in_specs=[pl.BlockSpec((None, tr, cols), lambda r: (_my_chip(), r, 0)),
                                    pl.BlockSpec((3, tr, cols), lambda r: (0, r, 0))],
                          out_specs=pl.BlockSpec((None, tr, cols), lambda r: (lax.axis_index("c"), r, 0)),
                          out_shape=jax.ShapeDtypeStruct((2, rh, cols), F32),
                          compiler_params=_params(("parallel",)))(part, recv)


def _sum_devices(g):
    _, rows, cols = g.shape
    tr = _row_tile(rows, cols, budget=256 * 1024)

    def body(g_ref, o_ref):
        acc = g_ref[0]
        for d in range(1, 8):
            acc = acc + g_ref[d]
        o_ref[...] = acc

    return pl.pallas_call(body, name="sum_small_grads", grid=(rows // tr,),
                          in_specs=[pl.BlockSpec((8, tr, cols), lambda r: (0, r, 0))],
                          out_specs=pl.BlockSpec((tr, cols), lambda r: (r, 0)),
                          out_shape=jax.ShapeDtypeStruct((rows, cols), F32),
                          compiler_params=_params(("parallel",)))(g)


def _place_shard(w, layer, dtype):
    _, rows, cols = w.shape
    tr = _row_tile(rows, cols)

    def body(i_ref, o_ref):
        o_ref[...] = i_ref[...].astype(o_ref.dtype)

    out = pl.pallas_call(body, name="place_shard", grid=(rows // tr,),
                         in_specs=[pl.BlockSpec((None, tr, cols), lambda r: (layer, r, 0))],
                         out_specs=pl.BlockSpec((None, tr, cols), lambda r: (_my_chip(), r, 0)),
                         out_shape=jax.ShapeDtypeStruct((N_CHIPS, rows, cols), dtype),
                         compiler_params=_params(("parallel",)))(w)
    return out.reshape(N_CHIPS, 2, rows // 2, cols)


def _adamw(w, gs, m, v):
    n_layers, rows, cols = w.shape
    tr = _row_tile(rows, cols, budget=1024 * 1024)

    def body(w_ref, m_ref, v_ref, *rest):
        g_refs = rest[:n_layers]
        go_ref, d_ref, mo_ref, vo_ref = rest[n_layers:]
        gv = g_refs[0][...]
        for layer in range(1, n_layers):
            gv = jnp.where(pl.program_id(0) == layer, g_refs[layer][...], gv)
        mn = ADAM_B1 * m_ref[...] + (1.0 - ADAM_B1) * gv
        vn = ADAM_B2 * v_ref[...] + (1.0 - ADAM_B2) * jnp.square(gv)
        m_hat = mn / (1.0 - ADAM_B1 ** ADAM_STEP)
        v_hat = vn / (1.0 - ADAM_B2 ** ADAM_STEP)
        d_ref[...] = -ADAM_LR * (m_hat / (jnp.sqrt(v_hat) + ADAM_EPS) + ADAM_WD * w_ref[...])
        go_ref[...] = gv
        mo_ref[...] = mn
        vo_ref[...] = vn

    spec = pl.BlockSpec((None, tr, cols), lambda layer, r: (layer, r, 0))
    g_specs = [pl.BlockSpec((tr, cols), lambda layer, r, own=own: (jnp.where(layer == own, r, 0), 0))
               for own in range(n_layers)]
    return pl.pallas_call(body, name="adamw", grid=(n_layers, rows // tr), in_specs=[spec] * 3 + g_specs,
                          out_specs=[spec] * 4, out_shape=[jax.ShapeDtypeStruct((n_layers, rows, cols), F32)] * 4,
                          compiler_params=_params(("parallel", "parallel")))(w, m, v, *gs)


def _pad_rope(w):
    z = jnp.zeros(w.shape[:-1] + (ROPE_HALF,), w.dtype)
    return jnp.concatenate([w[..., :ROPE_HALF], z, w[..., ROPE_HALF:], z], axis=-1)


def _unpad_rope(g):
    return jnp.concatenate([g[..., :ROPE_HALF], g[..., ROPE:ROPE + ROPE_HALF]], axis=-1)


def _unstack_cols(s):
    n, r, cs = s.shape
    return jnp.transpose(s, (1, 0, 2)).reshape(r, n * cs)


def _stack_cols(f):
    r, cfull = f.shape
    return jnp.transpose(f.reshape(r, N_CHIPS, cfull // N_CHIPS), (1, 0, 2))


def _small_shard(norm, conv):
    return jnp.concatenate([jnp.pad(norm, ((0, 15), (0, 0))), jnp.pad(conv, ((0, 13), (0, 0)))], axis=0)


def _flat_rows(a):
    return a.reshape(-1, LANES)


def _pack_small(arrs):
    return jnp.concatenate([_flat_rows(a.astype(F32)) for a in arrs], axis=0)


def _unpack_small(flat, like):
    out, r = [], 0
    for a in like:
        n = a.size // LANES
        out.append(flat[r:r + n].reshape(a.shape))
        r += n
    return out


def kernel(x, positions, e_norm_mix, e_w_in, e_q_norm, e_w_uq, e_kv_norm, e_w_ukv, e_v_norm, e_sgu_w, e_sgu_b, e_mla_out_norm, e_sgu_out_norm, e_w_out, o_norm_mix, o_w_in, o_conv_w, o_w_out, mlp_norm, mlp_w1, mlp_w2, final_norm, loss_target, m_e_norm_mix, m_e_w_in, m_e_q_norm, m_e_w_uq, m_e_kv_norm, m_e_w_ukv, m_e_v_norm, m_e_sgu_w, m_e_sgu_b, m_e_mla_out_norm, m_e_sgu_out_norm, m_e_w_out, m_o_norm_mix, m_o_w_in, m_o_conv_w, m_o_w_out, m_mlp_norm, m_mlp_w1, m_mlp_w2, m_final_norm, v_e_norm_mix, v_e_w_in, v_e_q_norm, v_e_w_uq, v_e_kv_norm, v_e_w_ukv, v_e_v_norm, v_e_sgu_w, v_e_sgu_b, v_e_mla_out_norm, v_e_sgu_out_norm, v_e_w_out, v_o_norm_mix, v_o_w_in, v_o_conv_w, v_o_w_out, v_mlp_norm, v_mlp_w1, v_mlp_w2, v_final_norm):
    t, d = x.shape[1], x.shape[2]
    ql, kvl = e_q_norm.shape[1], e_kv_norm.shape[1]
    groups = e_v_norm.shape[1]
    gw = groups * LANES
    heads = N_CHIPS * e_w_uq.shape[2] // (LANES + ROPE)
    hw = heads * LANES
    mix = hw + gw
    ei = N_CHIPS * e_w_in.shape[2]
    cd = N_CHIPS * o_conv_w.shape[2]
    ff = N_CHIPS * mlp_w1.shape[2]
    ffs = ff // N_CHIPS
    pi = 2 * gw + ql + kvl + LANES
    assert e_norm_mix.shape[0] == 1 and o_norm_mix.shape[0] == 1 and mlp_norm.shape[0] == 2
    assert ei == ql + kvl + ROPE + 2 * gw and cd == d and e_sgu_w.shape[2] == LANES
    assert (2 * gw) % ql == 0 and (2 * gw + ql) % kvl == 0 and t % LANES == 0
    scale = (LANES + ROPE) ** -0.5

    tr = min(256, t)
    tm = _pick(t, 1024, 8)
    kt, kd = _pick(t, 2048, 8), _pick(d, 2048)
    xs = x.reshape(t, d)
    tgt = loss_target.reshape(t, d)

    small_shard = _small_shard(o_norm_mix, o_conv_w[0])
    layer_groups = [
        [_place_shard(e_w_in, 0, BF16)],
        [_place_shard(e_w_uq, 0, BF16), _place_shard(e_w_ukv, 0, BF16), _place_shard(e_w_out, 0, BF16),
         _place_shard(small_shard[None], 0, F32)],
        [_place_shard(mlp_w1, 0, BF16), _place_shard(mlp_w2, 0, BF16)],
        [_place_shard(o_w_in, 0, BF16), _place_shard(o_w_out, 0, BF16)],
        [_place_shard(mlp_w1, 1, BF16), _place_shard(mlp_w2, 1, BF16)]]
    started, gather_token = _gather_start(layer_groups)

    def gathered(gi, tag, after):
        send, recv, bufs = started[gi]
        bufs = _gather_forward(tag, _gather_wait(tag, send, recv, bufs, after))
        return [b.reshape(N_CHIPS, 2 * b.shape[2], b.shape[3]) for b in bufs]

    w_in_g, = gathered(0, "e_in", gather_token)
    full = _unstack_cols(w_in_g)
    c2, c3 = ql + kvl, ql + kvl + ROPE
    w_in_all = jnp.concatenate([full[:, c3:], full[:, :c2], _pad_rope(full[:, c2:c3])], axis=1)

    g_e = e_norm_mix
    h0 = _norm_fwd("e_norm", xs, g_e, tr)
    proj, = _matmul("e_proj", Mat(h0, t, d), Mat(w_in_all, d, pi), "nn", [_out(t, pi, F32)], tm, _pick(pi, 1024), kd)

    w_uq_g, w_ukv_g, w_eout_g, small_g = gathered(1, "e", proj)
    full = _unstack_cols(w_uq_g).reshape(ql, heads, LANES + ROPE)
    w_q_all = jnp.concatenate([full[:, :, :LANES].reshape(ql, hw), _pad_rope(full[:, :, LANES:]).reshape(ql, hw)], axis=1)
    full = _unstack_cols(w_ukv_g).reshape(kvl, heads, 2 * LANES)
    w_kv_all = jnp.concatenate([full[:, :, :LANES].reshape(kvl, hw), full[:, :, LANES:].reshape(kvl, hw)], axis=1)
    w_eout = w_eout_g.reshape(mix, d)
    g_o = small_g[:, 0].reshape(1, d)
    conv_w = jnp.pad(jnp.transpose(small_g[:, 16:19], (1, 0, 2)).reshape(3, cd), ((0, 5), (0, 0)))

    g_q, g_kv = e_q_norm, e_kv_norm
    g_vn = e_v_norm.reshape(1, gw)
    sgu_w = e_sgu_w[0]
    sgu_b = jnp.broadcast_to(e_sgu_b[0][:, :, None], (groups, LANES, LANES))
    g_mla, g_sgu = e_mla_out_norm, e_sgu_out_norm
    g_m0, g_m1 = mlp_norm[0:1], mlp_norm[1:2]
    g_f = final_norm.reshape(1, d)

    inv_freq = ROPE_BASE ** (-jnp.arange(0, ROPE, 2, dtype=F32) / ROPE)
    zeros32 = jnp.zeros((ROPE_HALF,), F32)
    ones32 = jnp.ones((ROPE_HALF,), F32)
    invf = jnp.concatenate([inv_freq, zeros32, inv_freq, zeros32]).reshape(1, LANES)
    cmask = jnp.concatenate([ones32, zeros32, ones32, zeros32]).reshape(1, LANES)
    smask = jnp.concatenate([-ones32, zeros32, ones32, zeros32]).reshape(1, LANES)
    ctab, stab = _rope_tables(positions.reshape(t, 1).astype(F32), invf, cmask, smask, tr)

    def mlp_fwd(tag, xin, g, w1, w2):
        hm = _norm_fwd("mlp_norm_" + tag, xin, g, tr)
        tn = _pick(ffs, 1024)
        a, act = _matmul("mlp_up_" + tag, Mat(hm, t, d), w1, "nn",
                         [_out(t, ff, BF16), _out(t, ff, BF16)], tm, tn, kd,
                         epilogue=lambda z: (jnp.maximum(z, 0.0), jnp.square(jnp.maximum(z, 0.0))))
        xo, = _matmul("mlp_down_" + tag, Mat(act, t, ff), w2, "nn",
                      [_out(t, d, F32)], tm, _pick(d, 1024), _pick(ffs, 2048),
                      epilogue=lambda z, r: (z + r,), extras=[Mat(xin, t, d)])
        return xo, hm, a, act

    def mlp_bwd(tag, dx, dxb, xin, g, w1, w2, hm, a, act, deps):
        tn = _pick(ffs, 1024)
        dz, = _matmul("mlp_dact_" + tag, Mat(dxb, t, d), w2, "nt",
                      [_out(t, ff, BF16)], tm, tn, kd,
                      epilogue=lambda z, av: (z * (2.0 * av.astype(F32)),), extras=[Mat(a, t, ff)], deps=deps)
        dw2, = _matmul("mlp_dw2_" + tag, Mat(act, t, ff), Mat(dxb, t, d), "tn",
                       [_out(ff, d, BF16)], tn, _pick(d, 1024), kt)
        dw1, = _matmul("mlp_dw1_" + tag, Mat(hm, t, d), Mat(dz, t, ff), "tn",
                       [_out(d, ff, BF16, "colstack", (), (N_CHIPS, d, ffs))], _pick(d, 1024), tn, kt)
        dhm, = _matmul("mlp_dh_" + tag, Mat(dz, t, ff), w1, "nt",
                       [_out(t, d, F32)], tm, _pick(d, 1024), _pick(ffs, 2048))
        dxo, dxob, dg = _norm_bwd("mlp_norm_bwd_" + tag, dhm, xin, g, dx, tr)
        return dxo, dxob, dg, dw1, dw2.reshape(N_CHIPS, ffs, d)

    def scatter(tag, stacked):
        g5 = [g.reshape(N_CHIPS, 2, g.shape[1] // 2, g.shape[2]) for g in stacked]
        part = [_pair_sum(a, b) for a, b in zip(g5, _pair_exchange(tag, g5))]
        return _scatter_start(tag, part)

    cq_cb, ckv_cb, kr_cb = 2 * gw // ql, (2 * gw + ql) // kvl, (2 * gw + ql + kvl) // LANES
    qn, kvn = _rowwise("qkv_norm", lambda a, b, ga, gb: (_rms(a, ga), _rms(b, gb)), t // tr,
                       [_rt(proj, tr, ql, cq_cb), _rt(proj, tr, kvl, ckv_cb), _whole(g_q), _whole(g_kv)],
                       [_rt_out(t, ql, BF16, tr), _rt_out(t, kvl, BF16, tr)])
    qfull, = _matmul("q_up", Mat(qn, t, ql), Mat(w_q_all, ql, 2 * hw), "nn", [_out(t, 2 * hw, F32)], tm, _pick(2 * hw, 1024), ql)
    kvall, = _matmul("kv_up", Mat(kvn, t, kvl), Mat(w_kv_all, kvl, 2 * hw), "nn", [_out(t, 2 * hw, BF16)], tm, _pick(2 * hw, 1024), kvl)
    qall, kr = _rope_fwd(qfull, proj, kr_cb, ctab, stab, heads, tr)
    att, lse = _attn_fwd(qall, kvall, kr, heads, scale, tr)
    rb = min(2 * LANES, t)
    sgu = _sgu_fwd(proj, g_vn, sgu_w, sgu_b, groups, rb)
    mixed = _rowwise("mix_norm", lambda a, s, ga, gs: jnp.concatenate([_rms(a, ga), _rms(s, gs)], axis=1), t // tr,
                     [_rt(att, tr), _rt(sgu, tr), _whole(g_mla), _whole(g_sgu)], [_rt_out(t, mix, BF16, tr)])[0]
    x1, = _matmul("e_out", Mat(mixed, t, mix), Mat(w_eout, mix, d), "nn", [_out(t, d, F32)], tm, _pick(d, 1024), _pick(mix, 2048),
                  epilogue=lambda z, r: (z + r,), extras=[Mat(xs, t, d)])
    w1_g, w2_g = gathered(2, "m0", x1)
    w1_0, w2_0 = Mat(w1_g, d, ff, "colstack"), Mat(w2_g.reshape(ff, d), ff, d)
    x2, hm0, a0, act0 = mlp_fwd("0", x1, g_m0, w1_0, w2_0)

    w_oin_g, w_oout_g = gathered(3, "o", x2)
    w_oout = w_oout_g.reshape(cd, d)
    h1 = _norm_fwd("o_norm", x2, g_o, tr)
    oin = Mat(w_oin_g, d, 3 * cd, "colstack")
    tn_o = _pick(_gcd(3 * cd // N_CHIPS, cd), 512)
    proj3, = _matmul("o_proj", Mat(h1, t, d), oin, "nn", [_out(t, 3 * cd, F32, "colstack", (), (3, t, cd))], tm, tn_o, kd)
    tc = _pick(cd, 256)
    bz = _conv_fwd(proj3, conv_w, tc)
    x3, = _matmul("o_out", Mat(bz, t, cd), Mat(w_oout, cd, d), "nn", [_out(t, d, F32)], tm, _pick(d, 1024), _pick(cd, 2048),
                  epilogue=lambda z, r: (z + r,), extras=[Mat(x2, t, d)])
    w1_g, w2_g = gathered(4, "m1", x3)
    w1_1, w2_1 = Mat(w1_g, d, ff, "colstack"), Mat(w2_g.reshape(ff, d), ff, d)
    x4, hm1, a1, act1 = mlp_fwd("1", x3, g_m1, w1_1, w2_1)

    def final_fn(xv, gv, tv):
        r = lax.rsqrt(jnp.mean(xv * xv, axis=-1, keepdims=True) + EPS)
        xh = xv * r
        err = xh * gv - tv
        dy = err * (1.0 / d)
        dxh = dy * gv
        dx = r * (dxh - xh * jnp.mean(dxh * xh, axis=-1, keepdims=True))
        sq = jnp.sum(err * err, axis=0, keepdims=True)
        part = sq[:, :LANES]
        for k in range(1, d // LANES):
            part = part + sq[:, k * LANES:(k + 1) * LANES]
        return dx, dx, part, jnp.sum(dy * xh, axis=0, keepdims=True)

    dx4, dx4b, loss_vec, dg_f = _rowwise("loss_final_norm", final_fn, t // tr, [_rt(x4, tr), _whole(g_f), _rt(tgt, tr)],
                                         [_rt_out(t, d, F32, tr), _rt_out(t, d, BF16, tr)],
                                         [jax.ShapeDtypeStruct((1, LANES), F32), jax.ShapeDtypeStruct((1, d), F32)])
    loss = lax.psum(0.5 * jnp.sum(loss_vec) / d, ("x", "y", "c"))

    dx3, dx3b, dg_m1, dw1, dw2 = mlp_bwd("1", dx4, dx4b, x3, g_m1, w1_1, w2_1, hm1, a1, act1, ())
    sc_m1, tok = scatter("m1", [dw1, dw2])

    dbz, = _matmul("o_out_dx", Mat(dx3b, t, d), Mat(w_oout, cd, d), "nt", [_out(t, cd, F32)], tm, _pick(cd, 1024), kd,
                   deps=(tok,))
    dw_oout, = _matmul("o_out_dw", Mat(bz, t, cd), Mat(dx3b, t, d), "tn", [_out(cd, d, BF16)], _pick(cd, 1024), _pick(d, 1024), kt)
    dproj3, dconv = _conv_bwd(proj3, conv_w, dbz, tc)
    dp3 = Mat(dproj3, t, 3 * cd, "colstack")
    dw_oin, = _matmul("o_proj_dw", Mat(h1, t, d), dp3, "tn", [_out(d, 3 * cd, BF16, "colstack", (), (N_CHIPS, d, 3 * cd // N_CHIPS))],
                      _pick(d, 1024), tn_o, kt)
    dh1, = _matmul("o_proj_dx", dp3, oin, "nt", [_out(t, d, F32)], tm, _pick(d, 1024), tn_o)
    dx2, dx2b, dg_o = _norm_bwd("o_norm_bwd", dh1, x2, g_o, dx3, tr)

    dconv_s = jnp.transpose(dconv[:3].reshape(3, N_CHIPS, cd // N_CHIPS), (1, 0, 2))
    gsmall = jnp.concatenate([jnp.pad(dg_o.reshape(N_CHIPS, 1, d // N_CHIPS), ((0, 0), (0, 15), (0, 0))),
                              jnp.pad(dconv_s, ((0, 0), (0, 13), (0, 0)))], axis=1)
    sc_o, tok = scatter("o", [dw_oin, dw_oout.reshape(N_CHIPS, cd // N_CHIPS, d), gsmall])

    dx1, dx1b, dg_m0, dw1, dw2 = mlp_bwd("0", dx2, dx2b, x1, g_m0, w1_0, w2_0, hm0, a0, act0, (tok,))
    sc_m0, tok = scatter("m0", [dw1, dw2])

    dmixed, = _matmul("e_out_dx", Mat(dx1b, t, d), Mat(w_eout, mix, d), "nt", [_out(t, mix, F32)], tm, _pick(mix, 1024), kd,
                      deps=(tok,))
    dw_eout, = _matmul("e_out_dw", Mat(mixed, t, mix), Mat(dx1b, t, d), "tn", [_out(mix, d, BF16)], _pick(mix, 1024), _pick(d, 1024), kt)

    def mixb_fn(dm, a, s, ga, gs):
        da, dga = _rms_bwd(dm[:, :hw], a, ga)
        dsg, dgs = _rms_bwd(dm[:, hw:], s, gs)
        prod = da * a
        delta = jnp.stack([jnp.broadcast_to(jnp.sum(prod[:, h * LANES:(h + 1) * LANES], axis=-1, keepdims=True), (tr, LANES))
                           for h in range(heads)], axis=0)
        return da, dsg, delta, dga, dgs

    da_b, dsgu, delta, dg_mla, dg_sgu = _rowwise(
        "mix_norm_bwd", mixb_fn, t // tr, [_rt(dmixed, tr), _rt(att, tr), _rt(sgu, tr), _whole(g_mla), _whole(g_sgu)],
        [_rt_out(t, hw, BF16, tr), _rt_out(t, gw, F32, tr),
         (jax.ShapeDtypeStruct((heads, t, LANES), F32), pl.BlockSpec((heads, tr, LANES), lambda i: (0, i, 0)))],
        [jax.ShapeDtypeStruct((1, hw), F32), jax.ShapeDtypeStruct((1, gw), F32)])

    du, dv, dsgu_w, dsgu_b8, dg_vn = _sgu_bwd(proj, dsgu, g_vn, sgu_w, sgu_b, groups, rb)
    dq1, dq2 = _attn_dq(qall, kvall, kr, da_b, lse, delta, heads, scale, tr)
    dk1, dvv, dkr_h = _attn_dkv(qall, kvall, kr, da_b, lse, delta, heads, scale, tr)
    dqfull, dkr = _rope_bwd(dq1, dq2, dkr_h, ctab, stab, heads, tr)
    dkvall = jnp.concatenate([dk1, dvv], axis=1)
    dw_q, = _matmul("q_up_dw", Mat(qn, t, ql), Mat(dqfull, t, 2 * hw), "tn", [_out(ql, 2 * hw, BF16)], ql, _pick(2 * hw, 1024), kt)
    dqn, = _matmul("q_up_dx", Mat(dqfull, t, 2 * hw), Mat(w_q_all, ql, 2 * hw), "nt", [_out(t, ql, F32)], tm, ql, _pick(2 * hw, 2048))
    dw_kv, = _matmul("kv_up_dw", Mat(kvn, t, kvl), Mat(dkvall, t, 2 * hw), "tn", [_out(kvl, 2 * hw, BF16)], kvl, _pick(2 * hw, 1024), kt)
    dkvn, = _matmul("kv_up_dx", Mat(dkvall, t, 2 * hw), Mat(w_kv_all, kvl, 2 * hw), "nt", [_out(t, kvl, F32)], tm, kvl, _pick(2 * hw, 2048))

    def qkvb_fn(da, db, a, b, ga, gb):
        dxa, dga = _rms_bwd(da, a, ga)
        dxb, dgb = _rms_bwd(db, b, gb)
        return dxa, dxb, dga, dgb

    dcq, dckv, dg_q, dg_kv = _rowwise(
        "qkv_norm_bwd", qkvb_fn, t // tr,
        [_rt(dqn, tr), _rt(dkvn, tr), _rt(proj, tr, ql, cq_cb), _rt(proj, tr, kvl, ckv_cb), _whole(g_q), _whole(g_kv)],
        [_rt_out(t, ql, BF16, tr), _rt_out(t, kvl, BF16, tr)],
        [jax.ShapeDtypeStruct((1, ql), F32), jax.ShapeDtypeStruct((1, kvl), F32)])
    dproj = jnp.concatenate([du, dv, dcq, dckv, dkr], axis=1)
    dw_in, = _matmul("e_proj_dw", Mat(h0, t, d), Mat(dproj, t, pi), "tn", [_out(d, pi, BF16)], _pick(d, 1024), _pick(pi, 1024), kt)
    dh0, = _matmul("e_proj_dx", Mat(dproj, t, pi), Mat(w_in_all, d, pi), "nt", [_out(t, d, F32)], tm, _pick(d, 1024), _pick(pi, 4096))
    dx0, _, dg_e = _norm_bwd("e_norm_bwd", dh0, xs, g_e, dx1, tr)

    gfull = jnp.concatenate([dw_in[:, 2 * gw:2 * gw + c2], _unpad_rope(dw_in[:, 2 * gw + c2:]), dw_in[:, :2 * gw]], axis=1)
    gw_in = _stack_cols(gfull)
    gq = jnp.concatenate([dw_q[:, :hw].reshape(ql, heads, LANES), _unpad_rope(dw_q[:, hw:].reshape(ql, heads, LANES))], axis=-1)
    gw_uq = _stack_cols(gq.reshape(ql, heads * (LANES + ROPE)))
    gkv = jnp.concatenate([dw_kv[:, :hw].reshape(kvl, heads, LANES), dw_kv[:, hw:].reshape(kvl, heads, LANES)], axis=-1)
    gw_ukv = _stack_cols(gkv.reshape(kvl, heads * 2 * LANES))
    sc_e, tok = scatter("e", [gw_in, gw_uq, gw_ukv, dw_eout.reshape(N_CHIPS, mix // N_CHIPS, d)])

    def reduced(tag, sc, after):
        send, recv, part, lands = sc
        part, lands = _scatter_wait(tag, send, recv, part, lands, after)
        half = [_chip_sum(p, r) for p, r in zip(part, lands)]
        return [r.reshape(2 * r.shape[1], r.shape[2]) for r in _pair_share(tag, half)]

    r_w1_1, r_w2_1 = reduced("m1", sc_m1, tok)
    r_oin, r_oout, r_small = reduced("o", sc_o, r_w2_1)
    r_w1_0, r_w2_0 = reduced("m0", sc_m0, r_small)
    late = {
        "o_w_in": _adamw(o_w_in, [r_oin], m_o_w_in, v_o_w_in),
        "o_w_out": _adamw(o_w_out, [r_oout], m_o_w_out, v_o_w_out),
        "mlp_w1": _adamw(mlp_w1, [r_w1_0, r_w1_1], m_mlp_w1, v_mlp_w1),
        "mlp_w2": _adamw(mlp_w2, [r_w2_0, r_w2_1], m_mlp_w2, v_mlp_w2),
    }

    small_like = [e_norm_mix, e_q_norm, e_kv_norm, e_v_norm, e_sgu_w, e_sgu_b, e_mla_out_norm, e_sgu_out_norm, mlp_norm, final_norm]
    small_grads = [dg_e, dg_q, dg_kv, dg_vn, dsgu_w, dsgu_b8[:, 0, :], dg_mla, dg_sgu, jnp.concatenate([dg_m0, dg_m1], axis=0), dg_f]
    sflat = _pack_small(small_grads)
    pad = (-sflat.shape[0]) % 8
    sflat = jnp.pad(sflat, ((0, pad), (0, 0)))
    g_small = _sum_devices(_gather_all_devices(sflat))

    def padded(arrs):
        return jnp.pad(_pack_small(arrs), ((0, pad), (0, 0)))

    s_m = [m_e_norm_mix, m_e_q_norm, m_e_kv_norm, m_e_v_norm, m_e_sgu_w, m_e_sgu_b, m_e_mla_out_norm, m_e_sgu_out_norm, m_mlp_norm, m_final_norm]
    s_v = [v_e_norm_mix, v_e_q_norm, v_e_kv_norm, v_e_v_norm, v_e_sgu_w, v_e_sgu_b, v_e_mla_out_norm, v_e_sgu_out_norm, v_mlp_norm, v_final_norm]
    s_out = [_unpack_small(o[0], small_like)
             for o in _adamw(padded(small_like)[None], [g_small], padded(s_m)[None], padded(s_v)[None])]

    sm = [o[0] for o in _adamw(small_shard[None], [r_small], _small_shard(m_o_norm_mix, m_o_conv_w[0])[None],
                               _small_shard(v_o_norm_mix, v_o_conv_w[0])[None])]

    r_in, r_uq, r_ukv, r_eout = reduced("e", sc_e, late["mlp_w2"][1])
    big = dict(late)
    big.update({
        "e_w_in": _adamw(e_w_in, [r_in], m_e_w_in, v_e_w_in),
        "e_w_uq": _adamw(e_w_uq, [r_uq], m_e_w_uq, v_e_w_uq),
        "e_w_ukv": _adamw(e_w_ukv, [r_ukv], m_e_w_ukv, v_e_w_ukv),
        "e_w_out": _adamw(e_w_out, [r_eout], m_e_w_out, v_e_w_out),
    })

    names = ["e_norm_mix", "e_w_in", "e_q_norm", "e_w_uq", "e_kv_norm", "e_w_ukv", "e_v_norm", "e_sgu_w", "e_sgu_b",
             "e_mla_out_norm", "e_sgu_out_norm", "e_w_out", "o_norm_mix", "o_w_in", "o_conv_w", "o_w_out",
             "mlp_norm", "mlp_w1", "mlp_w2", "final_norm"]
    shapes = {"e_w_in": e_w_in.shape, "e_w_uq": e_w_uq.shape, "e_w_ukv": e_w_ukv.shape, "e_w_out": e_w_out.shape,
              "o_w_in": o_w_in.shape, "o_w_out": o_w_out.shape, "mlp_w1": mlp_w1.shape, "mlp_w2": mlp_w2.shape}
    small_names = ["e_norm_mix", "e_q_norm", "e_kv_norm", "e_v_norm", "e_sgu_w", "e_sgu_b", "e_mla_out_norm",
                   "e_sgu_out_norm", "mlp_norm", "final_norm"]

    def leaf(kind, name):
        if name in big:
            return big[name][kind].reshape(shapes[name])
        if name == "o_norm_mix":
            return sm[kind][0:1]
        if name == "o_conv_w":
            return sm[kind][16:19].reshape(o_conv_w.shape)
        return s_out[kind][small_names.index(name)]

    outs = [loss, dx0.reshape(x.shape)]
    for kind in range(4):
        outs += [leaf(kind, nm) for nm in names]
    return tuple(outs)


def _gcd(a, b):
    while b:
        a, b = b, a % b
    return a
```

```python
import functools

import jax
import jax.numpy as jnp
from jax import lax
from jax.experimental import pallas as pl
from jax.experimental.pallas import tpu as pltpu

F32 = jnp.float32
BF16 = jnp.bfloat16
MESH = pl.DeviceIdType.MESH

LANES = 128
ROPE = 64
ROPE_HALF = ROPE // 2
ROPE_BASE = 10000.0
EPS = 1e-6
N_CHIPS = 4
VMEM_LIMIT = 48 * 1024 * 1024
NEG = -1e30

ADAM_LR = 0.001
ADAM_B1 = 0.9
ADAM_B2 = 0.999
ADAM_EPS = 1e-08
ADAM_WD = 0.01
ADAM_STEP = 10


def _pick(n, target, step=LANES):
    best = None
    for t in range(step, min(n, target) + 1, step):
        if n % t == 0:
            best = t
    return best if best is not None else n


def _params(sem, vmem=VMEM_LIMIT):
    return pltpu.CompilerParams(dimension_semantics=sem, vmem_limit_bytes=vmem)


class Mat:
    def __init__(self, arr, rows, cols, kind="plain", lead=(), col_off=0, shape=None, dtype=None):
        self.arr, self.rows, self.cols, self.kind, self.lead, self.col_off = arr, rows, cols, kind, tuple(lead), col_off
        self.shape = tuple(arr.shape) if arr is not None else tuple(shape)
        self.dtype = arr.dtype if arr is not None else dtype

    def sds(self):
        return jax.ShapeDtypeStruct(self.shape, self.dtype)

    def spec(self, br, bc, gridmap):
        lead, nl = self.lead, len(self.lead)
        if self.kind == "plain":
            assert self.col_off % bc == 0 and self.rows % br == 0 and self.cols % bc == 0, (self.shape, br, bc)
            off = self.col_off // bc
            block = (None,) * nl + (br, bc)

            def phys(rb, cb):
                return lead + (rb, cb + off)
        elif self.kind == "colstack":
            cs = self.shape[-1]
            assert cs % bc == 0 and self.rows % br == 0, (self.shape, br, bc)
            q = cs // bc
            block = (None,) * (nl + 1) + (br, bc)

            def phys(rb, cb):
                return (cb // q,) + lead + (rb, cb % q)
        else:
            rs = self.shape[-2]
            assert rs % br == 0 and self.cols % bc == 0, (self.shape, br, bc)
            q = rs // br
            block = (None,) * (nl + 1) + (br, bc)

            def phys(rb, cb):
                return (rb // q,) + lead + (rb % q, cb)

        return pl.BlockSpec(block, lambda *g: phys(*gridmap(*g)))


def _matmul(name, a, b, mode, outs, tm, tn, tk, epilogue=None, extras=(), deps=()):
    if mode == "nn":
        m, k, n = a.rows, a.cols, b.cols
        a_spec = a.spec(tm, tk, lambda i, j, kk: (i, kk))
        b_spec = b.spec(tk, tn, lambda i, j, kk: (kk, j))
        dims = (((1,), (0,)), ((), ()))
    elif mode == "nt":
        m, k, n = a.rows, a.cols, b.rows
        a_spec = a.spec(tm, tk, lambda i, j, kk: (i, kk))
        b_spec = b.spec(tn, tk, lambda i, j, kk: (j, kk))
        dims = (((1,), (1,)), ((), ()))
    else:
        k, m, n = a.rows, a.cols, b.cols
        a_spec = a.spec(tk, tm, lambda i, j, kk: (kk, i))
        b_spec = b.spec(tk, tn, lambda i, j, kk: (kk, j))
        dims = (((0,), (0,)), ((), ()))
    assert m % tm == 0 and n % tn == 0 and k % tk == 0, (name, m, n, k, tm, tn, tk)
    grid = (m // tm, n // tn, k // tk)
    nk = grid[2]
    n_ex, n_out, n_dep = len(extras), len(outs), len(deps)
    tile = lambda i, j, kk: (i, j)

    def finish(z, ex, out_refs):
        vals = epilogue(z, *[e[...] for e in ex]) if epilogue is not None else (z,)
        for o, v in zip(out_refs, vals):
            o[...] = v.astype(o.dtype)

    def body_single(a_ref, b_ref, *rest):
        finish(lax.dot_general(a_ref[...], b_ref[...], dims, preferred_element_type=F32),
               rest[:n_ex], rest[n_ex + n_dep:n_ex + n_dep + n_out])

    def body_acc(a_ref, b_ref, *rest):
        acc = rest[-1]
        kk = pl.program_id(2)

        @pl.when(kk == 0)
        def _():
            acc[...] = jnp.zeros_like(acc)

        acc[...] += lax.dot_general(a_ref[...], b_ref[...], dims, preferred_element_type=F32)

        @pl.when(kk == nk - 1)
        def _():
            finish(acc[...], rest[:n_ex], rest[n_ex + n_dep:n_ex + n_dep + n_out])

    res = pl.pallas_call(
        body_single if nk == 1 else body_acc, name=name, grid=grid,
        in_specs=[a_spec, b_spec] + [e.spec(tm, tn, tile) for e in extras]
        + [pl.BlockSpec(memory_space=pl.ANY) for _ in deps],
        out_specs=[o.spec(tm, tn, tile) for o in outs],
        out_shape=[o.sds() for o in outs],
        scratch_shapes=[] if nk == 1 else [pltpu.VMEM((tm, tn), F32)],
        compiler_params=_params(("parallel", "parallel", "arbitrary")),
    )(a.arr, b.arr, *[e.arr for e in extras], *deps)
    return res


def _out(rows, cols, dtype, kind="plain", lead=(), shape=None):
    return Mat(None, rows, cols, kind, lead, shape=shape if shape is not None else (rows, cols), dtype=dtype)


def _rt(arr, tr, width=None, cb=0):
    width = arr.shape[1] if width is None else width
    return arr, pl.BlockSpec((tr, width), lambda i: (i, cb))


def _whole(arr):
    nd = arr.ndim
    return arr, pl.BlockSpec(arr.shape, lambda i: (0,) * nd)


def _rowwise(name, fn, n_steps, ins, outs, accs=(), deps=()):
    n_in, n_out, n_acc, n_dep = len(ins), len(outs), len(accs), len(deps)

    def body(*refs):
        vals = fn(*[r[...] for r in refs[:n_in]])
        if not isinstance(vals, (tuple, list)):
            vals = (vals,)
        for ref, v in zip(refs[n_in + n_dep:n_in + n_dep + n_out], vals[:n_out]):
            ref[...] = v.astype(ref.dtype)
        if n_acc:
            acc_refs = refs[n_in + n_dep + n_out:]

            @pl.when(pl.program_id(0) == 0)
            def _():
                for ref in acc_refs:
                    ref[...] = jnp.zeros_like(ref)

            for ref, v in zip(acc_refs, vals[n_out:]):
                ref[...] += v

    acc_specs = [pl.BlockSpec(s.shape, lambda i, nd=len(s.shape): (0,) * nd) for s in accs]
    res = pl.pallas_call(
        body, name=name, grid=(n_steps,),
        in_specs=[s for _, s in ins] + [pl.BlockSpec(memory_space=pl.ANY) for _ in deps],
        out_specs=[s for _, s in outs] + acc_specs,
        out_shape=[o for o, _ in outs] + list(accs),
        compiler_params=_params(("arbitrary",) if n_acc else ("parallel",)),
    )(*[a for a, _ in ins], *deps)
    return res


def _rt_out(t, width, dtype, tr):
    return jax.ShapeDtypeStruct((t, width), dtype), pl.BlockSpec((tr, width), lambda i: (i, 0))


def _rms(x, g):
    r = lax.rsqrt(jnp.mean(x * x, axis=-1, keepdims=True) + EPS)
    return x * r * g


def _rms_bwd(dy, x, g):
    r = lax.rsqrt(jnp.mean(x * x, axis=-1, keepdims=True) + EPS)
    xh = x * r
    dxh = dy * g
    dx = r * (dxh - xh * jnp.mean(dxh * xh, axis=-1, keepdims=True))
    dg = jnp.sum(dy * xh, axis=0, keepdims=True)
    return dx, dg


def _gelu(x):
    k = 0.7978845608028654
    th = jnp.tanh(k * (x + 0.044715 * (x * x * x)))
    return x * (0.5 * (1.0 + th))


def _gelu_grad(x):
    k = 0.7978845608028654
    x2 = x * x
    th = jnp.tanh(k * (x + 0.044715 * (x2 * x)))
    return 0.5 * (1.0 + th) + 0.5 * x * (1.0 - th * th) * (k * (1.0 + 3.0 * 0.044715 * x2))


def _norm_fwd(name, x, g, tr):
    t, d = x.shape
    return _rowwise(name, lambda xv, gv: _rms(xv, gv), t // tr, [_rt(x, tr), _whole(g)], [_rt_out(t, d, BF16, tr)])[0]


def _norm_bwd(name, dh, x, g, dres, tr):
    t, d = x.shape

    def fn(dhv, xv, gv, drv):
        dx, dg = _rms_bwd(dhv, xv, gv)
        dx = dx + drv
        return dx, dx, dg

    return _rowwise(name, fn, t // tr, [_rt(dh, tr), _rt(x, tr), _whole(g), _rt(dres, tr)],
                    [_rt_out(t, d, F32, tr), _rt_out(t, d, BF16, tr)], [jax.ShapeDtypeStruct((1, d), F32)])


def _rope_tables(posf, invf, cmask, smask, tr):
    t = posf.shape[0]

    def fn(p, f, cm, sm):
        ang = p * f
        return jnp.cos(ang) * cm, jnp.sin(ang) * sm

    return _rowwise("rope_tables", fn, t // tr, [_rt(posf, tr), _whole(invf), _whole(cmask), _whole(smask)],
                    [_rt_out(t, LANES, F32, tr), _rt_out(t, LANES, F32, tr)])


def _rot(v, c, s):
    return v * c + pltpu.roll(v, ROPE, axis=1) * s


def _rot_bwd(dv, c, s):
    return dv * c + pltpu.roll(dv * s, ROPE, axis=1)


def _rope_fwd(qfull, proj, kr_cb, ctab, stab, heads, tr):
    t = qfull.shape[0]
    hw = heads * LANES

    def fn(q, kr, c, s):
        parts = [q[:, :hw]] + [_rot(q[:, hw + h * LANES: hw + (h + 1) * LANES], c, s) for h in range(heads)]
        return jnp.concatenate(parts, axis=1), _rot(kr, c, s)

    return _rowwise("rope_fwd", fn, t // tr, [_rt(qfull, tr), _rt(proj, tr, LANES, kr_cb), _rt(ctab, tr), _rt(stab, tr)],
                    [_rt_out(t, 2 * hw, BF16, tr), _rt_out(t, LANES, BF16, tr)])


def _rope_bwd(dq1, dq2, dkr_h, ctab, stab, heads, tr):
    t = dq1.shape[0]
    hw = heads * LANES

    def fn(a, b, dk, c, s):
        parts = [a] + [_rot_bwd(b[:, h * LANES:(h + 1) * LANES], c, s) for h in range(heads)]
        dks = dk[0]
        for h in range(1, heads):
            dks = dks + dk[h]
        return jnp.concatenate(parts, axis=1), _rot_bwd(dks, c, s)

    dk_spec = pl.BlockSpec((heads, tr, LANES), lambda i: (0, i, 0))
    return _rowwise("rope_bwd", fn, t // tr, [_rt(dq1, tr), _rt(dq2, tr), (dkr_h, dk_spec), _rt(ctab, tr), _rt(stab, tr)],
                    [_rt_out(t, 2 * hw, BF16, tr), _rt_out(t, LANES, BF16, tr)])


def _dot_nt(a, b):
    return lax.dot_general(a, b, (((1,), (1,)), ((), ())), preferred_element_type=F32)


def _dot_tn(a, b):
    return lax.dot_general(a, b, (((0,), (0,)), ((), ())), preferred_element_type=F32)


def _dot(a, b):
    return jnp.dot(a, b, preferred_element_type=F32)


def _ranges(n_blocks):
    n_var = min(4, n_blocks)
    assert n_blocks % n_var == 0
    return n_var, n_blocks // n_var


def _row_of(col):
    return col.T[:8, :]


def _attn_fwd(qall, kvall, kr, heads, scale, tq):
    t = qall.shape[0]
    nq = t // tq
    n_var, per = _ranges(nq)

    def body(qn_ref, qr_ref, kn_ref, v_ref, kr_ref, o_ref, lse_ref, lser_ref):
        i = pl.program_id(1)
        for var in range(n_var):
            kv = (var + 1) * per * tq

            @pl.when(jnp.logical_and(i >= var * per, i < (var + 1) * per))
            def _(kv=kv):
                s = (_dot_nt(qn_ref[...], kn_ref[:kv, :]) + _dot_nt(qr_ref[...], kr_ref[:kv, :])) * scale
                rows = i * tq + lax.broadcasted_iota(jnp.int32, (tq, kv), 0)
                cols = lax.broadcasted_iota(jnp.int32, (tq, kv), 1)
                s = jnp.where(cols <= rows, s, NEG)
                m = jnp.max(s, axis=-1, keepdims=True)
                p = jnp.exp(s - m)
                l = jnp.sum(p, axis=-1, keepdims=True)
                o_ref[...] = _dot(p.astype(BF16), v_ref[:kv, :]) / l
                lse = jnp.broadcast_to(m + jnp.log(l), (tq, LANES))
                lse_ref[...] = lse
                lser_ref[...] = _row_of(lse)

    return pl.pallas_call(
        body, name="attn_fwd", grid=(heads, nq),
        in_specs=[pl.BlockSpec((tq, LANES), lambda h, i: (i, h)),
                  pl.BlockSpec((tq, LANES), lambda h, i: (i, heads + h)),
                  pl.BlockSpec((t, LANES), lambda h, i: (0, h)),
                  pl.BlockSpec((t, LANES), lambda h, i: (0, heads + h)),
                  pl.BlockSpec((t, LANES), lambda h, i: (0, 0))],
        out_specs=[pl.BlockSpec((tq, LANES), lambda h, i: (i, h)),
                   pl.BlockSpec((None, tq, LANES), lambda h, i: (h, i, 0)),
                   pl.BlockSpec((None, 8, tq), lambda h, i: (h, 0, i))],
        out_shape=[jax.ShapeDtypeStruct((t, heads * LANES), F32), jax.ShapeDtypeStruct((heads, t, LANES), F32),
                   jax.ShapeDtypeStruct((heads, 8, t), F32)],
        compiler_params=_params(("parallel", "parallel")),
    )(qall, qall, kvall, kvall, kr)


def _attn_dq(qall, kvall, kr, do, lse, delta, heads, scale, tq):
    t = qall.shape[0]
    nq = t // tq
    n_var, per = _ranges(nq)

    def body(qn_ref, qr_ref, kn_ref, v_ref, kr_ref, do_ref, lse_ref, dl_ref, dq1_ref, dq2_ref):
        i = pl.program_id(1)
        for var in range(n_var):
            kv = (var + 1) * per * tq

            @pl.when(jnp.logical_and(i >= var * per, i < (var + 1) * per))
            def _(kv=kv):
                k1, k2 = kn_ref[:kv, :], kr_ref[:kv, :]
                s = (_dot_nt(qn_ref[...], k1) + _dot_nt(qr_ref[...], k2)) * scale
                rows = i * tq + lax.broadcasted_iota(jnp.int32, (tq, kv), 0)
                cols = lax.broadcasted_iota(jnp.int32, (tq, kv), 1)
                p = jnp.where(cols <= rows, jnp.exp(s - lse_ref[...][:, :1]), 0.0)
                dp = _dot_nt(do_ref[...], v_ref[:kv, :])
                ds = (p * (dp - dl_ref[...][:, :1]) * scale).astype(BF16)
                dq1_ref[...] = _dot(ds, k1)
                dq2_ref[...] = _dot(ds, k2)

    qblk = lambda off: pl.BlockSpec((tq, LANES), lambda h, i: (i, off + h))
    full = lambda off: pl.BlockSpec((t, LANES), lambda h, i: (0, off + h))
    stat = pl.BlockSpec((None, tq, LANES), lambda h, i: (h, i, 0))
    return pl.pallas_call(
        body, name="attn_dq", grid=(heads, nq),
        in_specs=[qblk(0), qblk(heads), full(0), full(heads), pl.BlockSpec((t, LANES), lambda h, i: (0, 0)),
                  qblk(0), stat, stat],
        out_specs=[qblk(0), qblk(0)],
        out_shape=[jax.ShapeDtypeStruct((t, heads * LANES), F32)] * 2,
        compiler_params=_params(("parallel", "parallel")),
    )(qall, qall, kvall, kvall, kr, do, lse, delta)


def _attn_dkv(qall, kvall, kr, do, lse_row, delta_row, heads, scale, tk):
    t = qall.shape[0]
    nk = t // tk
    n_var, per = _ranges(nk)

    def body(qn_ref, qr_ref, kn_ref, v_ref, kr_ref, do_ref, lse_ref, dl_ref, dk_ref, dv_ref, dkr_ref):
        j = pl.program_id(1)
        for var in range(n_var):
            q0 = var * per * tk
            nq = t - q0

            @pl.when(jnp.logical_and(j >= var * per, j < (var + 1) * per))
            def _(q0=q0, nq=nq):
                qn, qr, do_v = qn_ref[q0:, :], qr_ref[q0:, :], do_ref[q0:, :]
                st = (_dot_nt(kn_ref[...], qn) + _dot_nt(kr_ref[...], qr)) * scale
                keys = j * tk + lax.broadcasted_iota(jnp.int32, (tk, nq), 0)
                queries = q0 + lax.broadcasted_iota(jnp.int32, (tk, nq), 1)
                pt = jnp.where(keys <= queries, jnp.exp(st - lse_ref[0:1, q0:]), 0.0)
                dpt = _dot_nt(v_ref[...], do_v)
                dst = (pt * (dpt - dl_ref[0:1, q0:]) * scale).astype(BF16)
                dv_ref[...] = _dot(pt.astype(BF16), do_v).astype(dv_ref.dtype)
                dk_ref[...] = _dot(dst, qn).astype(dk_ref.dtype)
                dkr_ref[...] = _dot(dst, qr)

    kblk = lambda off: pl.BlockSpec((tk, LANES), lambda h, j: (j, off + h))
    full = lambda off: pl.BlockSpec((t, LANES), lambda h, j: (0, off + h))
    stat = pl.BlockSpec((None, 8, t), lambda h, j: (h, 0, 0))
    return pl.pallas_call(
        body, name="attn_dkv", grid=(heads, nk),
        in_specs=[full(0), full(heads), kblk(0), kblk(heads), pl.BlockSpec((tk, LANES), lambda h, j: (j, 0)),
                  full(0), stat, stat],
        out_specs=[kblk(0), kblk(0), pl.BlockSpec((None, tk, LANES), lambda h, j: (h, j, 0))],
        out_shape=[jax.ShapeDtypeStruct((t, heads * LANES), BF16)] * 2 + [jax.ShapeDtypeStruct((heads, t, LANES), F32)],
        compiler_params=_params(("parallel", "parallel")),
    )(qall, qall, kvall, kvall, kr, do, lse_row, delta_row)


def _tril():
    return lax.broadcasted_iota(jnp.int32, (LANES, LANES), 0) >= lax.broadcasted_iota(jnp.int32, (LANES, LANES), 1)


def _group_norm(vg):
    mu = jnp.mean(vg, axis=-1, keepdims=True)
    vc = vg - mu
    rs = lax.rsqrt(jnp.mean(vc * vc, axis=-1, keepdims=True) + EPS)
    return vc * rs, rs


def _sgu_fwd(proj, gain, w, bias, groups, rb):
    t = proj.shape[0]
    gw = groups * LANES
    cpb = rb // LANES

    def body(u_ref, v_ref, gain_ref, w_ref, b_ref, s_ref):
        tril = _tril()
        for g in range(groups):
            wt = jnp.where(tril, w_ref[g], 0.0).astype(BF16)
            cols = slice(g * LANES, (g + 1) * LANES)
            for ci in range(cpb):
                rows = slice(ci * LANES, (ci + 1) * LANES)
                ug = _gelu(u_ref[rows, cols])
                vh, _ = _group_norm(_gelu(v_ref[rows, cols]))
                vn = vh * gain_ref[:, cols]
                y = _dot(wt, vn.astype(BF16)) + b_ref[g]
                s_ref[rows, cols] = ug * y

    return pl.pallas_call(
        body, name="sgu_fwd", grid=(t // rb,),
        in_specs=[pl.BlockSpec((rb, gw), lambda i: (i, 0)), pl.BlockSpec((rb, gw), lambda i: (i, 1)),
                  pl.BlockSpec((1, gw), lambda i: (0, 0)),
                  pl.BlockSpec((groups, LANES, LANES), lambda i: (0, 0, 0)),
                  pl.BlockSpec((groups, LANES, LANES), lambda i: (0, 0, 0))],
        out_specs=pl.BlockSpec((rb, gw), lambda i: (i, 0)),
        out_shape=jax.ShapeDtypeStruct((t, gw), F32),
        compiler_params=_params(("parallel",)),
    )(proj, proj, gain, w, bias)


def _sgu_bwd(proj, ds, gain, w, bias, groups, rb):
    t = proj.shape[0]
    gw = groups * LANES
    cpb = rb // LANES
    n_steps = t // rb

    def body(u_ref, v_ref, ds_ref, gain_ref, w_ref, b_ref, du_ref, dv_ref, dw_ref, db_ref, dg_ref, dy_acc):
        step = pl.program_id(0)

        @pl.when(step == 0)
        def _():
            dw_ref[...] = jnp.zeros_like(dw_ref)
            dy_acc[...] = jnp.zeros_like(dy_acc)
            dg_ref[...] = jnp.zeros_like(dg_ref)

        tril = _tril()
        for g in range(groups):
            wt = jnp.where(tril, w_ref[g], 0.0).astype(BF16)
            cols = slice(g * LANES, (g + 1) * LANES)
            gain_g = gain_ref[:, cols]
            for ci in range(cpb):
                rows = slice(ci * LANES, (ci + 1) * LANES)
                u_raw, v_raw, ds_v = u_ref[rows, cols], v_ref[rows, cols], ds_ref[rows, cols]
                ug = _gelu(u_raw)
                vh, rs = _group_norm(_gelu(v_raw))
                vn = (vh * gain_g).astype(BF16)
                y = _dot(wt, vn) + b_ref[g]
                dy = ds_v * ug
                dyb = dy.astype(BF16)
                du_ref[rows, cols] = (ds_v * y * _gelu_grad(u_raw)).astype(du_ref.dtype)
                dy_acc[g] += dy
                dw_ref[g] += _dot_nt(dyb, vn)
                dvn = _dot_tn(wt, dyb)
                dg_ref[:, cols] += jnp.sum(dvn * vh, axis=0, keepdims=True)
                dvh = dvn * gain_g
                dvg = rs * (dvh - jnp.mean(dvh, axis=-1, keepdims=True)
                            - vh * jnp.mean(dvh * vh, axis=-1, keepdims=True))
                dv_ref[rows, cols] = (dvg * _gelu_grad(v_raw)).astype(dv_ref.dtype)

        @pl.when(step == n_steps - 1)
        def _():
            ones = jnp.ones((8, LANES), F32)
            for g in range(groups):
                dw_ref[g] = jnp.where(tril, dw_ref[g], 0.0)
                db_ref[g] = lax.dot_general(ones, dy_acc[g], (((1,), (1,)), ((), ())),
                                            precision=lax.Precision.HIGHEST, preferred_element_type=F32)

    blk = lambda cb: pl.BlockSpec((rb, gw), lambda i: (i, cb))
    whole3 = pl.BlockSpec((groups, LANES, LANES), lambda i: (0, 0, 0))
    return pl.pallas_call(
        body, name="sgu_bwd", grid=(n_steps,),
        in_specs=[blk(0), blk(1), blk(0), pl.BlockSpec((1, gw), lambda i: (0, 0)), whole3, whole3],
        out_specs=[blk(0), blk(0), whole3, pl.BlockSpec((groups, 8, LANES), lambda i: (0, 0, 0)),
                   pl.BlockSpec((1, gw), lambda i: (0, 0))],
        out_shape=[jax.ShapeDtypeStruct((t, gw), BF16), jax.ShapeDtypeStruct((t, gw), BF16),
                   jax.ShapeDtypeStruct((groups, LANES, LANES), F32), jax.ShapeDtypeStruct((groups, 8, LANES), F32),
                   jax.ShapeDtypeStruct((1, gw), F32)],
        scratch_shapes=[pltpu.VMEM((groups, LANES, LANES), F32)],
        compiler_params=_params(("arbitrary",)),
    )(proj, proj, ds, gain, w, bias)


def _shift_down(z, s):
    rows = lax.broadcasted_iota(jnp.int32, z.shape, 0)
    return jnp.where(rows >= s, pltpu.roll(z, s, axis=0), 0.0)


def _shift_up(z, s):
    n = z.shape[0]
    rows = lax.broadcasted_iota(jnp.int32, z.shape, 0)
    return jnp.where(rows < n - s, pltpu.roll(z, n - s, axis=0), 0.0)


def _conv_fwd(proj3, cw, tc):
    _, t, cd = proj3.shape

    def body(p_ref, w_ref, o_ref):
        z = p_ref[1] * p_ref[2]
        w = w_ref[...]
        zc = w[2:3] * z + w[1:2] * _shift_down(z, 1) + w[0:1] * _shift_down(z, 2)
        o_ref[...] = (p_ref[0] * zc).astype(o_ref.dtype)

    return pl.pallas_call(
        body, name="conv_fwd", grid=(cd // tc,),
        in_specs=[pl.BlockSpec((3, t, tc), lambda j: (0, 0, j)), pl.BlockSpec((8, tc), lambda j: (0, j))],
        out_specs=pl.BlockSpec((t, tc), lambda j: (0, j)),
        out_shape=jax.ShapeDtypeStruct((t, cd), BF16),
        compiler_params=_params(("parallel",)),
    )(proj3, cw)


def _conv_bwd(proj3, cw, dbz, tc):
    _, t, cd = proj3.shape

    def body(p_ref, w_ref, d_ref, o_ref, dw_ref):
        b, c, xin = p_ref[0], p_ref[1], p_ref[2]
        w = w_ref[...]
        z = c * xin
        z1, z2 = _shift_down(z, 1), _shift_down(z, 2)
        zc = w[2:3] * z + w[1:2] * z1 + w[0:1] * z2
        d = d_ref[...]
        dzc = d * b
        dz = w[2:3] * dzc + w[1:2] * _shift_up(dzc, 1) + w[0:1] * _shift_up(dzc, 2)
        o_ref[0] = (d * zc).astype(o_ref.dtype)
        o_ref[1] = (dz * xin).astype(o_ref.dtype)
        o_ref[2] = (dz * c).astype(o_ref.dtype)
        row = lax.broadcasted_iota(jnp.int32, (8, tc), 0)
        dw0 = jnp.sum(dzc * z2, axis=0, keepdims=True)
        dw1 = jnp.sum(dzc * z1, axis=0, keepdims=True)
        dw2 = jnp.sum(dzc * z, axis=0, keepdims=True)
        dw_ref[...] = jnp.where(row == 0, dw0, 0.0) + jnp.where(row == 1, dw1, 0.0) + jnp.where(row == 2, dw2, 0.0)

    return pl.pallas_call(
        body, name="conv_bwd", grid=(cd // tc,),
        in_specs=[pl.BlockSpec((3, t, tc), lambda j: (0, 0, j)), pl.BlockSpec((8, tc), lambda j: (0, j)),
                  pl.BlockSpec((t, tc), lambda j: (0, j))],
        out_specs=[pl.BlockSpec((3, t, tc), lambda j: (0, 0, j)), pl.BlockSpec((8, tc), lambda j: (0, j))],
        out_shape=[jax.ShapeDtypeStruct((3, t, cd), BF16), jax.ShapeDtypeStruct((8, cd), F32)],
        compiler_params=_params(("parallel",)),
    )(proj3, cw, dbz)


def _place():
    x, y, c = lax.axis_index("x"), lax.axis_index("y"), lax.axis_index("c")
    chips = [(1 - x, y), (x, 1 - y), (1 - x, 1 - y)]
    return x, y, c, chips


def _any_specs(n):
    return [pl.BlockSpec(memory_space=pl.ANY) for _ in range(n)]


HBM_SPEC = pl.BlockSpec(memory_space=pltpu.HBM)
SEM_SPEC = pl.BlockSpec(memory_space=pltpu.SEMAPHORE)
ORDERED_EFFECT = pltpu.SideEffectType.DATAFLOW_SIDE_EFFECTING


def _in_hbm(a):
    return pltpu.with_memory_space_constraint(a, pltpu.HBM)


def _token():
    return jax.ShapeDtypeStruct((8, LANES), F32), pl.BlockSpec(memory_space=pltpu.VMEM)


def _gather_start(groups):
    sizes = [len(g) for g in groups]
    flat = [b for g in groups for b in g]
    n, ng = len(flat), len(groups)

    def body(*refs):
        ins, sems, token = refs[:n], refs[n:n + 2 * ng], refs[-1]
        x, y, c, chips = _place()
        me = 2 * x + y
        i = 0
        for gi, size in enumerate(sizes):
            for j in range(size):
                blk = ins[i].at[me, c]
                for k, chip in enumerate(chips):
                    pltpu.make_async_remote_copy(src_ref=blk, dst_ref=blk, send_sem=sems[2 * gi].at[3 * j + k],
                                                 recv_sem=sems[2 * gi + 1].at[3 * j + k],
                                                 device_id=(*chip, c), device_id_type=MESH).start()
                i += 1
        token[...] = jnp.zeros_like(token)

    tok_shape, tok_spec = _token()
    res = pl.pallas_call(
        body, name="gather_start",
        in_specs=[HBM_SPEC] * n,
        out_specs=[SEM_SPEC] * (2 * ng) + [HBM_SPEC] * n + [tok_spec],
        out_shape=[pltpu.SemaphoreType.DMA((3 * size,)) for size in sizes for _ in (0, 1)]
        + [pltpu.HBM(b.shape, b.dtype) for b in flat] + [tok_shape],
        input_output_aliases={i: 2 * ng + i for i in range(n)},
        compiler_params=pltpu.CompilerParams(has_side_effects=ORDERED_EFFECT),
    )(*[_in_hbm(b) for b in flat])
    out, i = [], 2 * ng
    for gi, size in enumerate(sizes):
        out.append((res[2 * gi], res[2 * gi + 1], list(res[i:i + size])))
        i += size
    return out, res[-1]


def _gather_wait(tag, send, recv, bufs, after):
    n = len(bufs)

    def body(*refs):
        ins, send_ref, recv_ref = refs[:n], refs[n], refs[n + 1]
        x, y, c, chips = _place()
        me = 2 * x + y
        for j in range(n):
            for k, (px, py) in enumerate(chips):
                cp = pltpu.make_async_remote_copy(src_ref=ins[j].at[me, c], dst_ref=ins[j].at[2 * px + py, c],
                                                  send_sem=send_ref.at[3 * j + k], recv_sem=recv_ref.at[3 * j + k],
                                                  device_id=(px, py, c), device_id_type=MESH)
                cp.wait_send()
                cp.wait_recv()

    return pl.pallas_call(
        body, name="gather_wait_" + tag,
        in_specs=[HBM_SPEC] * n + [SEM_SPEC, SEM_SPEC, pl.BlockSpec(memory_space=pl.ANY)],
        out_specs=[HBM_SPEC] * n,
        out_shape=[pltpu.HBM(b.shape, b.dtype) for b in bufs],
        input_output_aliases={i: i for i in range(n)},
        compiler_params=pltpu.CompilerParams(has_side_effects=ORDERED_EFFECT),
    )(*bufs, send, recv, after)


def _gather_forward(tag, bufs):
    n = len(bufs)

    def body(*refs):
        ins, outs = refs[:n], refs[n:2 * n]
        send, recv = refs[2 * n:]
        x, y, c, chips = _place()
        sib = (x, y, 1 - c)

        def cp(i, k, slot, half):
            return pltpu.make_async_remote_copy(src_ref=ins[i].at[slot, half], dst_ref=outs[i].at[slot, half],
                                                send_sem=send.at[3 * i + k], recv_sem=recv.at[3 * i + k],
                                                device_id=sib, device_id_type=MESH)

        cps = [cp(i, k, 2 * px + py, c) for i in range(n) for k, (px, py) in enumerate(chips)]
        for d in cps:
            d.start()
        for i in range(n):
            for k, (px, py) in enumerate(chips):
                cp(i, k, 2 * px + py, 1 - c).wait_recv()
        for d in cps:
            d.wait_send()

    return pl.pallas_call(
        body, name="gather_forward_" + tag,
        in_specs=_any_specs(n), out_specs=_any_specs(n),
        out_shape=[jax.ShapeDtypeStruct(b.shape, b.dtype) for b in bufs],
        scratch_shapes=[pltpu.SemaphoreType.DMA((3 * n,))] * 2,
        input_output_aliases={i: i for i in range(n)},
        compiler_params=pltpu.CompilerParams(has_side_effects=True),
    )(*bufs)


def _pair_exchange(tag, entries):
    n = len(entries)

    def body(*refs):
        ins, outs = refs[:n], refs[n:2 * n]
        send, recv = refs[2 * n:]
        x, y, c, _ = _place()
        sib = (x, y, 1 - c)

        def cp(i, j):
            return pltpu.make_async_remote_copy(src_ref=ins[i].at[j, 1 - c], dst_ref=outs[i].at[j],
                                                send_sem=send.at[N_CHIPS * i + j], recv_sem=recv.at[N_CHIPS * i + j],
                                                device_id=sib, device_id_type=MESH)

        cps = [cp(i, j) for i in range(n) for j in range(N_CHIPS)]
        for d in cps:
            d.start()
        for d in cps:
            d.wait_recv()
        for d in cps:
            d.wait_send()

    return pl.pallas_call(
        body, name="grad_pair_exchange_" + tag,
        in_specs=_any_specs(n), out_specs=_any_specs(n),
        out_shape=[jax.ShapeDtypeStruct((N_CHIPS,) + e.shape[2:], e.dtype) for e in entries],
        scratch_shapes=[pltpu.SemaphoreType.DMA((N_CHIPS * n,))] * 2,
        compiler_params=pltpu.CompilerParams(has_side_effects=True),
    )(*entries)


def _scatter_start(tag, parts):
    n = len(parts)
    lands = [lax.empty((3,) + p.shape[1:], p.dtype) for p in parts]

    def body(*refs):
        ins, zones, send, recv, token = refs[:n], refs[n:2 * n], refs[2 * n], refs[2 * n + 1], refs[-1]
        x, y, c, chips = _place()
        for i in range(n):
            for k, (px, py) in enumerate(chips):
                pltpu.make_async_remote_copy(src_ref=ins[i].at[2 * px + py], dst_ref=zones[i].at[k],
                                             send_sem=send.at[3 * i + k], recv_sem=recv.at[3 * i + k],
                                             device_id=(px, py, c), device_id_type=MESH).start()
        token[...] = jnp.zeros_like(token)

    tok_shape, tok_spec = _token()
    res = pl.pallas_call(
        body, name="scatter_start_" + tag,
        in_specs=[HBM_SPEC] * (2 * n),
        out_specs=[SEM_SPEC, SEM_SPEC] + [HBM_SPEC] * (2 * n) + [tok_spec],
        out_shape=[pltpu.SemaphoreType.DMA((3 * n,))] * 2 + [pltpu.HBM(a.shape, a.dtype) for a in parts + lands] + [tok_shape],
        input_output_aliases={i: 2 + i for i in range(2 * n)},
        compiler_params=pltpu.CompilerParams(has_side_effects=ORDERED_EFFECT),
    )(*[_in_hbm(a) for a in parts + lands])
    return (res[0], res[1], list(res[2:2 + n]), list(res[2 + n:2 + 2 * n])), res[-1]


def _scatter_wait(tag, send, recv, parts, lands, after):
    n = len(parts)

    def body(*refs):
        ins, zones, send_ref, recv_ref = refs[:n], refs[n:2 * n], refs[2 * n], refs[2 * n + 1]
        x, y, c, chips = _place()
        for i in range(n):
            for k, (px, py) in enumerate(chips):
                cp = pltpu.make_async_remote_copy(src_ref=ins[i].at[2 * px + py], dst_ref=zones[i].at[k],
                                                  send_sem=send_ref.at[3 * i + k], recv_sem=recv_ref.at[3 * i + k],
                                                  device_id=(px, py, c), device_id_type=MESH)
                cp.wait_send()
                cp.wait_recv()

    res = pl.pallas_call(
        body, name="scatter_wait_" + tag,
        in_specs=[HBM_SPEC] * (2 * n) + [SEM_SPEC, SEM_SPEC, pl.BlockSpec(memory_space=pl.ANY)],
        out_specs=[HBM_SPEC] * (2 * n),
        out_shape=[pltpu.HBM(a.shape, a.dtype) for a in parts + lands],
        input_output_aliases={i: i for i in range(2 * n)},
        compiler_params=pltpu.CompilerParams(has_side_effects=ORDERED_EFFECT),
    )(*parts, *lands, send, recv, after)
    return list(res[:n]), list(res[n:])


def _pair_share(tag, bufs):
    n = len(bufs)

    def body(*refs):
        ins, outs = refs[:n], refs[n:2 * n]
        send, recv = refs[2 * n:]
        x, y, c, _ = _place()
        sib = (x, y, 1 - c)

        def cp(i, half):
            return pltpu.make_async_remote_copy(src_ref=ins[i].at[half], dst_ref=outs[i].at[half],
                                                send_sem=send.at[i], recv_sem=recv.at[i],
                                                device_id=sib, device_id_type=MESH)

        cps = [cp(i, c) for i in range(n)]
        for d in cps:
            d.start()
        for i in range(n):
            cp(i, 1 - c).wait_recv()
        for d in cps:
            d.wait_send()

    return pl.pallas_call(
        body, name="grad_pair_share_" + tag,
        in_specs=_any_specs(n), out_specs=_any_specs(n),
        out_shape=[jax.ShapeDtypeStruct(b.shape, b.dtype) for b in bufs],
        scratch_shapes=[pltpu.SemaphoreType.DMA((n,))] * 2,
        input_output_aliases={i: i for i in range(n)},
        compiler_params=pltpu.CompilerParams(has_side_effects=True),
    )(*bufs)


def _gather_all_devices(v):
    def body(v_ref, o_ref, send, recv, loc):
        x, y, c, _ = _place()
        me = 4 * x + 2 * y + c
        own = pltpu.make_async_copy(v_ref, o_ref.at[me], loc)
        own.start()
        rels = [(fx, fy, fc) for fx in (0, 1) for fy in (0, 1) for fc in (0, 1)][1:]

        def peer(fx, fy, fc):
            return (x + fx - 2 * x * fx, y + fy - 2 * y * fy, c + fc - 2 * c * fc)

        def cp(r, slot, dev):
            return pltpu.make_async_remote_copy(src_ref=v_ref, dst_ref=o_ref.at[slot], send_sem=send.at[r],
                                                recv_sem=recv.at[r], device_id=dev, device_id_type=MESH)

        cps = [cp(r, me, peer(*f)) for r, f in enumerate(rels)]
        for d in cps:
            d.start()
        for r, f in enumerate(rels):
            px, py, pc = peer(*f)
            cp(r, 4 * px + 2 * py + pc, (px, py, pc)).wait_recv()
        for d in cps:
            d.wait_send()
        own.wait()

    return pl.pallas_call(
        body, name="gather_small_grads",
        in_specs=_any_specs(1), out_specs=_any_specs(1)[0],
        out_shape=jax.ShapeDtypeStruct((8,) + v.shape, v.dtype),
        scratch_shapes=[pltpu.SemaphoreType.DMA((7,)), pltpu.SemaphoreType.DMA((7,)), pltpu.SemaphoreType.DMA],
        compiler_params=pltpu.CompilerParams(has_side_effects=True),
    )(v)


def _row_tile(rows, cols, itemsize=4, budget=2 * 1024 * 1024):
    best = None
    for t in range(8, rows + 1, 8):
        if rows % t == 0 and t * cols * itemsize <= budget:
            best = t
    return best if best is not None else rows


def _my_chip():
    return 2 * lax.axis_index("x") + lax.axis_index("y")


def _pair_sum(g5, gsib):
    _, _, rh, cols = g5.shape
    tr = _row_tile(rh, cols)

    def body(a_ref, b_ref, o_ref):
        o_ref[...] = (a_ref[...].astype(F32) + b_ref[...].astype(F32)).astype(o_ref.dtype)

    return pl.pallas_call(body, name="grad_pair_sum", grid=(N_CHIPS, rh // tr),
                          in_specs=[pl.BlockSpec((None, None, tr, cols), lambda j, r: (j, lax.axis_index("c"), r, 0)),
                                    pl.BlockSpec((None, tr, cols), lambda j, r: (j, r, 0))],
                          out_specs=pl.BlockSpec((None, tr, cols), lambda j, r: (j, r, 0)),
                          out_shape=jax.ShapeDtypeStruct((N_CHIPS, rh, cols), BF16),
                          compiler_params=_params(("parallel", "parallel")))(g5, gsib)


def _chip_sum(part, recv):
    _, rh, cols = part.shape
    tr = _row_tile(rh, cols)

    def body(a_ref, b_ref, o_ref):
        acc = a_ref[...].astype(F32)
        for k in range(3):
            acc = acc + b_ref[k].astype(F32)
        o_ref[...] = acc

    return pl.pallas_call(body, name="grad_chip_sum", grid=(rh // tr,),
                          in_specs=[pl.BlockSpec((None, tr, cols), lambda r: (_my_chip(), r, 0)),
                                    pl.BlockSpec((3, tr, cols), lambda r: (0, r, 0))],
                          out_specs=pl.BlockSpec((None, tr, cols), lambda r: (lax.axis_index("c"), r, 0)),
                          out_shape=jax.ShapeDtypeStruct((2, rh, cols), F32),
                          compiler_params=_params(("parallel",)))(part, recv)


def _sum_devices(g):
    _, rows, cols = g.shape
    tr = _row_tile(rows, cols, budget=256 * 1024)

    def body(g_ref, o_ref):
        acc = g_ref[0]
        for d in range(1, 8):
            acc = acc + g_ref[d]
        o_ref[...] = acc

    return pl.pallas_call(body, name="sum_small_grads", grid=(rows // tr,),
                          in_specs=[pl.BlockSpec((8, tr, cols), lambda r: (0, r, 0))],
                          out_specs=pl.BlockSpec((tr, cols), lambda r: (r, 0)),
                          out_shape=jax.ShapeDtypeStruct((rows, cols), F32),
                          compiler_params=_params(("parallel",)))(g)


def _place_shard(w, layer, dtype):
    _, rows, cols = w.shape
    tr = _row_tile(rows, cols)

    def body(i_ref, o_ref):
        o_ref[...] = i_ref[...].astype(o_ref.dtype)

    out = pl.pallas_call(body, name="place_shard", grid=(rows // tr,),
                         in_specs=[pl.BlockSpec((None, tr, cols), lambda r: (layer, r, 0))],
                         out_specs=pl.BlockSpec((None, tr, cols), lambda r: (_my_chip(), r, 0)),
                         out_shape=jax.ShapeDtypeStruct((N_CHIPS, rows, cols), dtype),
                         compiler_params=_params(("parallel",)))(w)
    return out.reshape(N_CHIPS, 2, rows // 2, cols)


def _adamw(w, gs, m, v):
    n_layers, rows, cols = w.shape
    tr = _row_tile(rows, cols, budget=1024 * 1024)

    def body(w_ref, m_ref, v_ref, *rest):
        g_refs = rest[:n_layers]
        go_ref, d_ref, mo_ref, vo_ref = rest[n_layers:]
        gv = g_refs[0][...]
        for layer in range(1, n_layers):
            gv = jnp.where(pl.program_id(0) == layer, g_refs[layer][...], gv)
        mn = ADAM_B1 * m_ref[...] + (1.0 - ADAM_B1) * gv
        vn = ADAM_B2 * v_ref[...] + (1.0 - ADAM_B2) * jnp.square(gv)
        m_hat = mn / (1.0 - ADAM_B1 ** ADAM_STEP)
        v_hat = vn / (1.0 - ADAM_B2 ** ADAM_STEP)
        d_ref[...] = -ADAM_LR * (m_hat / (jnp.sqrt(v_hat) + ADAM_EPS) + ADAM_WD * w_ref[...])
        go_ref[...] = gv
        mo_ref[...] = mn
        vo_ref[...] = vn

    spec = pl.BlockSpec((None, tr, cols), lambda layer, r: (layer, r, 0))
    g_specs = [pl.BlockSpec((tr, cols), lambda layer, r, own=own: (jnp.where(layer == own, r, 0), 0))
               for own in range(n_layers)]
    return pl.pallas_call(body, name="adamw", grid=(n_layers, rows // tr), in_specs=[spec] * 3 + g_specs,
                          out_specs=[spec] * 4, out_shape=[jax.ShapeDtypeStruct((n_layers, rows, cols), F32)] * 4,
                          compiler_params=_params(("parallel", "parallel")))(w, m, v, *gs)


def _pad_rope(w):
    z = jnp.zeros(w.shape[:-1] + (ROPE_HALF,), w.dtype)
    return jnp.concatenate([w[..., :ROPE_HALF], z, w[..., ROPE_HALF:], z], axis=-1)


def _unpad_rope(g):
    return jnp.concatenate([g[..., :ROPE_HALF], g[..., ROPE:ROPE + ROPE_HALF]], axis=-1)


def _unstack_cols(s):
    n, r, cs = s.shape
    return jnp.transpose(s, (1, 0, 2)).reshape(r, n * cs)


def _stack_cols(f):
    r, cfull = f.shape
    return jnp.transpose(f.reshape(r, N_CHIPS, cfull // N_CHIPS), (1, 0, 2))


def _small_shard(norm, conv):
    return jnp.concatenate([jnp.pad(norm, ((0, 15), (0, 0))), jnp.pad(conv, ((0, 13), (0, 0)))], axis=0)


def _flat_rows(a):
    return a.reshape(-1, LANES)


def _pack_small(arrs):
    return jnp.concatenate([_flat_rows(a.astype(F32)) for a in arrs], axis=0)


def _unpack_small(flat, like):
    out, r = [], 0
    for a in like:
        n = a.size // LANES
        out.append(flat[r:r + n].reshape(a.shape))
        r += n
    return out


def kernel(x, positions, e_norm_mix, e_w_in, e_q_norm, e_w_uq, e_kv_norm, e_w_ukv, e_v_norm, e_sgu_w, e_sgu_b, e_mla_out_norm, e_sgu_out_norm, e_w_out, o_norm_mix, o_w_in, o_conv_w, o_w_out, mlp_norm, mlp_w1, mlp_w2, final_norm, loss_target, m_e_norm_mix, m_e_w_in, m_e_q_norm, m_e_w_uq, m_e_kv_norm, m_e_w_ukv, m_e_v_norm, m_e_sgu_w, m_e_sgu_b, m_e_mla_out_norm, m_e_sgu_out_norm, m_e_w_out, m_o_norm_mix, m_o_w_in, m_o_conv_w, m_o_w_out, m_mlp_norm, m_mlp_w1, m_mlp_w2, m_final_norm, v_e_norm_mix, v_e_w_in, v_e_q_norm, v_e_w_uq, v_e_kv_norm, v_e_w_ukv, v_e_v_norm, v_e_sgu_w, v_e_sgu_b, v_e_mla_out_norm, v_e_sgu_out_norm, v_e_w_out, v_o_norm_mix, v_o_w_in, v_o_conv_w, v_o_w_out, v_mlp_norm, v_mlp_w1, v_mlp_w2, v_final_norm):
    t, d = x.shape[1], x.shape[2]
    ql, kvl = e_q_norm.shape[1], e_kv_norm.shape[1]
    groups = e_v_norm.shape[1]
    gw = groups * LANES
    heads = N_CHIPS * e_w_uq.shape[2] // (LANES + ROPE)
    hw = heads * LANES
    mix = hw + gw
    ei = N_CHIPS * e_w_in.shape[2]
    cd = N_CHIPS * o_conv_w.shape[2]
    ff = N_CHIPS * mlp_w1.shape[2]
    ffs = ff // N_CHIPS
    pi = 2 * gw + ql + kvl + LANES
    assert e_norm_mix.shape[0] == 1 and o_norm_mix.shape[0] == 1 and mlp_norm.shape[0] == 2
    assert ei == ql + kvl + ROPE + 2 * gw and cd == d and e_sgu_w.shape[2] == LANES
    assert (2 * gw) % ql == 0 and (2 * gw + ql) % kvl == 0 and t % LANES == 0
    scale = (LANES + ROPE) ** -0.5

    tr = min(256, t)
    tm = _pick(t, 1024, 8)
    kt, kd = _pick(t, 2048, 8), _pick(d, 2048)
    xs = x.reshape(t, d)
    tgt = loss_target.reshape(t, d)

    small_shard = _small_shard(o_norm_mix, o_conv_w[0])
    layer_groups = [
        [_place_shard(e_w_in, 0, BF16)],
        [_place_shard(e_w_uq, 0, BF16), _place_shard(e_w_ukv, 0, BF16), _place_shard(e_w_out, 0, BF16),
         _place_shard(small_shard[None], 0, F32)],
        [_place_shard(mlp_w1, 0, BF16), _place_shard(mlp_w2, 0, BF16)],
        [_place_shard(o_w_in, 0, BF16), _place_shard(o_w_out, 0, BF16)],
        [_place_shard(mlp_w1, 1, BF16), _place_shard(mlp_w2, 1, BF16)]]
    started, gather_token = _gather_start(layer_groups)

    def gathered(gi, tag, after):
        send, recv, bufs = started[gi]
        bufs = _gather_forward(tag, _gather_wait(tag, send, recv, bufs, after))
        return [b.reshape(N_CHIPS, 2 * b.shape[2], b.shape[3]) for b in bufs]

    w_in_g, = gathered(0, "e_in", gather_token)
    full = _unstack_cols(w_in_g)
    c2, c3 = ql + kvl, ql + kvl + ROPE
    w_in_all = jnp.concatenate([full[:, c3:], full[:, :c2], _pad_rope(full[:, c2:c3])], axis=1)

    g_e = e_norm_mix
    h0 = _norm_fwd("e_norm", xs, g_e, tr)
    proj, = _matmul("e_proj", Mat(h0, t, d), Mat(w_in_all, d, pi), "nn", [_out(t, pi, F32)], tm, _pick(pi, 1024), kd)

    w_uq_g, w_ukv_g, w_eout_g, small_g = gathered(1, "e", proj)
    full = _unstack_cols(w_uq_g).reshape(ql, heads, LANES + ROPE)
    w_q_all = jnp.concatenate([full[:, :, :LANES].reshape(ql, hw), _pad_rope(full[:, :, LANES:]).reshape(ql, hw)], axis=1)
    full = _unstack_cols(w_ukv_g).reshape(kvl, heads, 2 * LANES)
    w_kv_all = jnp.concatenate([full[:, :, :LANES].reshape(kvl, hw), full[:, :, LANES:].reshape(kvl, hw)], axis=1)
    w_eout = w_eout_g.reshape(mix, d)
    g_o = small_g[:, 0].reshape(1, d)
    conv_w = jnp.pad(jnp.transpose(small_g[:, 16:19], (1, 0, 2)).reshape(3, cd), ((0, 5), (0, 0)))

    g_q, g_kv = e_q_norm, e_kv_norm
    g_vn = e_v_norm.reshape(1, gw)
    sgu_w = e_sgu_w[0]
    sgu_b = jnp.broadcast_to(e_sgu_b[0][:, :, None], (groups, LANES, LANES))
    g_mla, g_sgu = e_mla_out_norm, e_sgu_out_norm
    g_m0, g_m1 = mlp_norm[0:1], mlp_norm[1:2]
    g_f = final_norm.reshape(1, d)

    inv_freq = ROPE_BASE ** (-jnp.arange(0, ROPE, 2, dtype=F32) / ROPE)
    zeros32 = jnp.zeros((ROPE_HALF,), F32)
    ones32 = jnp.ones((ROPE_HALF,), F32)
    invf = jnp.concatenate([inv_freq, zeros32, inv_freq, zeros32]).reshape(1, LANES)
    cmask = jnp.concatenate([ones32, zeros32, ones32, zeros32]).reshape(1, LANES)
    smask = jnp.concatenate([-ones32, zeros32, ones32, zeros32]).reshape(1, LANES)
    ctab, stab = _rope_tables(positions.reshape(t, 1).astype(F32), invf, cmask, smask, tr)

    def mlp_fwd(tag, xin, g, w1, w2):
        hm = _norm_fwd("mlp_norm_" + tag, xin, g, tr)
        tn = _pick(ffs, 1024)
        a, act = _matmul("mlp_up_" + tag, Mat(hm, t, d), w1, "nn",
                         [_out(t, ff, BF16), _out(t, ff, BF16)], tm, tn, kd,
                         epilogue=lambda z: (jnp.maximum(z, 0.0), jnp.square(jnp.maximum(z, 0.0))))
        xo, = _matmul("mlp_down_" + tag, Mat(act, t, ff), w2, "nn",
                      [_out(t, d, F32)], tm, _pick(d, 1024), _pick(ffs, 2048),
                      epilogue=lambda z, r: (z + r,), extras=[Mat(xin, t, d)])
        return xo, hm, a, act

    def mlp_bwd(tag, dx, dxb, xin, g, w1, w2, hm, a, act, deps):
        tn = _pick(ffs, 1024)
        dz, = _matmul("mlp_dact_" + tag, Mat(dxb, t, d), w2, "nt",
                      [_out(t, ff, BF16)], tm, tn, kd,
                      epilogue=lambda z, av: (z * (2.0 * av.astype(F32)),), extras=[Mat(a, t, ff)], deps=deps)
        dw2, = _matmul("mlp_dw2_" + tag, Mat(act, t, ff), Mat(dxb, t, d), "tn",
                       [_out(ff, d, BF16)], tn, _pick(d, 1024), kt)
        dw1, = _matmul("mlp_dw1_" + tag, Mat(hm, t, d), Mat(dz, t, ff), "tn",
                       [_out(d, ff, BF16, "colstack", (), (N_CHIPS, d, ffs))], _pick(d, 1024), tn, kt)
        dhm, = _matmul("mlp_dh_" + tag, Mat(dz, t, ff), w1, "nt",
                       [_out(t, d, F32)], tm, _pick(d, 1024), _pick(ffs, 2048))
        dxo, dxob, dg = _norm_bwd("mlp_norm_bwd_" + tag, dhm, xin, g, dx, tr)
        return dxo, dxob, dg, dw1, dw2.reshape(N_CHIPS, ffs, d)

    def scatter(tag, stacked):
        g5 = [g.reshape(N_CHIPS, 2, g.shape[1] // 2, g.shape[2]) for g in stacked]
        part = [_pair_sum(a, b) for a, b in zip(g5, _pair_exchange(tag, g5))]
        return _scatter_start(tag, part)

    cq_cb, ckv_cb, kr_cb = 2 * gw // ql, (2 * gw + ql) // kvl, (2 * gw + ql + kvl) // LANES
    qn, kvn = _rowwise("qkv_norm", lambda a, b, ga, gb: (_rms(a, ga), _rms(b, gb)), t // tr,
                       [_rt(proj, tr, ql, cq_cb), _rt(proj, tr, kvl, ckv_cb), _whole(g_q), _whole(g_kv)],
                       [_rt_out(t, ql, BF16, tr), _rt_out(t, kvl, BF16, tr)])
    qfull, = _matmul("q_up", Mat(qn, t, ql), Mat(w_q_all, ql, 2 * hw), "nn", [_out(t, 2 * hw, F32)], tm, _pick(2 * hw, 1024), ql)
    kvall, = _matmul("kv_up", Mat(kvn, t, kvl), Mat(w_kv_all, kvl, 2 * hw), "nn", [_out(t, 2 * hw, BF16)], tm, _pick(2 * hw, 1024), kvl)
    qall, kr = _rope_fwd(qfull, proj, kr_cb, ctab, stab, heads, tr)
    att, lse, lse_row = _attn_fwd(qall, kvall, kr, heads, scale, tr)
    rb = min(2 * LANES, t)
    sgu = _sgu_fwd(proj, g_vn, sgu_w, sgu_b, groups, rb)
    mixed = _rowwise("mix_norm", lambda a, s, ga, gs: jnp.concatenate([_rms(a, ga), _rms(s, gs)], axis=1), t // tr,
                     [_rt(att, tr), _rt(sgu, tr), _whole(g_mla), _whole(g_sgu)], [_rt_out(t, mix, BF16, tr)])[0]
    x1, = _matmul("e_out", Mat(mixed, t, mix), Mat(w_eout, mix, d), "nn", [_out(t, d, F32)], tm, _pick(d, 1024), _pick(mix, 2048),
                  epilogue=lambda z, r: (z + r,), extras=[Mat(xs, t, d)])
    w1_g, w2_g = gathered(2, "m0", x1)
    w1_0, w2_0 = Mat(w1_g, d, ff, "colstack"), Mat(w2_g.reshape(ff, d), ff, d)
    x2, hm0, a0, act0 = mlp_fwd("0", x1, g_m0, w1_0, w2_0)

    w_oin_g, w_oout_g = gathered(3, "o", x2)
    w_oout = w_oout_g.reshape(cd, d)
    h1 = _norm_fwd("o_norm", x2, g_o, tr)
    oin = Mat(w_oin_g, d, 3 * cd, "colstack")
    tn_o = _pick(_gcd(3 * cd // N_CHIPS, cd), 512)
    proj3, = _matmul("o_proj", Mat(h1, t, d), oin, "nn", [_out(t, 3 * cd, F32, "colstack", (), (3, t, cd))], tm, tn_o, kd)
    tc = _pick(cd, 256)
    bz = _conv_fwd(proj3, conv_w, tc)
    x3, = _matmul("o_out", Mat(bz, t, cd), Mat(w_oout, cd, d), "nn", [_out(t, d, F32)], tm, _pick(d, 1024), _pick(cd, 2048),
                  epilogue=lambda z, r: (z + r,), extras=[Mat(x2, t, d)])
    w1_g, w2_g = gathered(4, "m1", x3)
    w1_1, w2_1 = Mat(w1_g, d, ff, "colstack"), Mat(w2_g.reshape(ff, d), ff, d)
    x4, hm1, a1, act1 = mlp_fwd("1", x3, g_m1, w1_1, w2_1)

    def final_fn(xv, gv, tv):
        r = lax.rsqrt(jnp.mean(xv * xv, axis=-1, keepdims=True) + EPS)
        xh = xv * r
        err = xh * gv - tv
        dy = err * (1.0 / d)
        dxh = dy * gv
        dx = r * (dxh - xh * jnp.mean(dxh * xh, axis=-1, keepdims=True))
        sq = jnp.sum(err * err, axis=0, keepdims=True)
        part = sq[:, :LANES]
        for k in range(1, d // LANES):
            part = part + sq[:, k * LANES:(k + 1) * LANES]
        return dx, dx, part, jnp.sum(dy * xh, axis=0, keepdims=True)

    dx4, dx4b, loss_vec, dg_f = _rowwise("loss_final_norm", final_fn, t // tr, [_rt(x4, tr), _whole(g_f), _rt(tgt, tr)],
                                         [_rt_out(t, d, F32, tr), _rt_out(t, d, BF16, tr)],
                                         [jax.ShapeDtypeStruct((1, LANES), F32), jax.ShapeDtypeStruct((1, d), F32)])
    loss = lax.psum(0.5 * jnp.sum(loss_vec) / d, ("x", "y", "c"))

    dx3, dx3b, dg_m1, dw1, dw2 = mlp_bwd("1", dx4, dx4b, x3, g_m1, w1_1, w2_1, hm1, a1, act1, ())
    sc_m1, tok = scatter("m1", [dw1, dw2])

    dbz, = _matmul("o_out_dx", Mat(dx3b, t, d), Mat(w_oout, cd, d), "nt", [_out(t, cd, F32)], tm, _pick(cd, 1024), kd,
                   deps=(tok,))
    dw_oout, = _matmul("o_out_dw", Mat(bz, t, cd), Mat(dx3b, t, d), "tn", [_out(cd, d, BF16)], _pick(cd, 1024), _pick(d, 1024), kt)
    dproj3, dconv = _conv_bwd(proj3, conv_w, dbz, tc)
    dp3 = Mat(dproj3, t, 3 * cd, "colstack")
    dw_oin, = _matmul("o_proj_dw", Mat(h1, t, d), dp3, "tn", [_out(d, 3 * cd, BF16, "colstack", (), (N_CHIPS, d, 3 * cd // N_CHIPS))],
                      _pick(d, 1024), tn_o, kt)
    dh1, = _matmul("o_proj_dx", dp3, oin, "nt", [_out(t, d, F32)], tm, _pick(d, 1024), tn_o)
    dx2, dx2b, dg_o = _norm_bwd("o_norm_bwd", dh1, x2, g_o, dx3, tr)

    dconv_s = jnp.transpose(dconv[:3].reshape(3, N_CHIPS, cd // N_CHIPS), (1, 0, 2))
    gsmall = jnp.concatenate([jnp.pad(dg_o.reshape(N_CHIPS, 1, d // N_CHIPS), ((0, 0), (0, 15), (0, 0))),
                              jnp.pad(dconv_s, ((0, 0), (0, 13), (0, 0)))], axis=1)
    sc_o, tok = scatter("o", [dw_oin, dw_oout.reshape(N_CHIPS, cd // N_CHIPS, d), gsmall])

    dx1, dx1b, dg_m0, dw1, dw2 = mlp_bwd("0", dx2, dx2b, x1, g_m0, w1_0, w2_0, hm0, a0, act0, (tok,))
    sc_m0, tok = scatter("m0", [dw1, dw2])

    dmixed, = _matmul("e_out_dx", Mat(dx1b, t, d), Mat(w_eout, mix, d), "nt", [_out(t, mix, F32)], tm, _pick(mix, 1024), kd,
                      deps=(tok,))
    dw_eout, = _matmul("e_out_dw", Mat(mixed, t, mix), Mat(dx1b, t, d), "tn", [_out(mix, d, BF16)], _pick(mix, 1024), _pick(d, 1024), kt)

    def mixb_fn(dm, a, s, ga, gs):
        da, dga = _rms_bwd(dm[:, :hw], a, ga)
        dsg, dgs = _rms_bwd(dm[:, hw:], s, gs)
        prod = da * a
        cols = [jnp.broadcast_to(jnp.sum(prod[:, h * LANES:(h + 1) * LANES], axis=-1, keepdims=True), (tr, LANES))
                for h in range(heads)]
        return da, dsg, jnp.stack(cols, axis=0), jnp.stack([_row_of(c) for c in cols], axis=0), dga, dgs

    da_b, dsgu, delta, delta_row, dg_mla, dg_sgu = _rowwise(
        "mix_norm_bwd", mixb_fn, t // tr, [_rt(dmixed, tr), _rt(att, tr), _rt(sgu, tr), _whole(g_mla), _whole(g_sgu)],
        [_rt_out(t, hw, BF16, tr), _rt_out(t, gw, F32, tr),
         (jax.ShapeDtypeStruct((heads, t, LANES), F32), pl.BlockSpec((heads, tr, LANES), lambda i: (0, i, 0))),
         (jax.ShapeDtypeStruct((heads, 8, t), F32), pl.BlockSpec((heads, 8, tr), lambda i: (0, 0, i)))],
        [jax.ShapeDtypeStruct((1, hw), F32), jax.ShapeDtypeStruct((1, gw), F32)])

    du, dv, dsgu_w, dsgu_b8, dg_vn = _sgu_bwd(proj, dsgu, g_vn, sgu_w, sgu_b, groups, rb)
    dq1, dq2 = _attn_dq(qall, kvall, kr, da_b, lse, delta, heads, scale, tr)
    dk1, dvv, dkr_h = _attn_dkv(qall, kvall, kr, da_b, lse_row, delta_row, heads, scale, tr)
    dqfull, dkr = _rope_bwd(dq1, dq2, dkr_h, ctab, stab, heads, tr)
    dkvall = jnp.concatenate([dk1, dvv], axis=1)
    dw_q, = _matmul("q_up_dw", Mat(qn, t, ql), Mat(dqfull, t, 2 * hw), "tn", [_out(ql, 2 * hw, BF16)], ql, _pick(2 * hw, 1024), kt)
    dqn, = _matmul("q_up_dx", Mat(dqfull, t, 2 * hw), Mat(w_q_all, ql, 2 * hw), "nt", [_out(t, ql, F32)], tm, ql, _pick(2 * hw, 2048))
    dw_kv, = _matmul("kv_up_dw", Mat(kvn, t, kvl), Mat(dkvall, t, 2 * hw), "tn", [_out(kvl, 2 * hw, BF16)], kvl, _pick(2 * hw, 1024), kt)
    dkvn, = _matmul("kv_up_dx", Mat(dkvall, t, 2 * hw), Mat(w_kv_all, kvl, 2 * hw), "nt", [_out(t, kvl, F32)], tm, kvl, _pick(2 * hw, 2048))

    def qkvb_fn(da, db, a, b, ga, gb):
        dxa, dga = _rms_bwd(da, a, ga)
        dxb, dgb = _rms_bwd(db, b, gb)
        return dxa, dxb, dga, dgb

    dcq, dckv, dg_q, dg_kv = _rowwise(
        "qkv_norm_bwd", qkvb_fn, t // tr,
        [_rt(dqn, tr), _rt(dkvn, tr), _rt(proj, tr, ql, cq_cb), _rt(proj, tr, kvl, ckv_cb), _whole(g_q), _whole(g_kv)],
        [_rt_out(t, ql, BF16, tr), _rt_out(t, kvl, BF16, tr)],
        [jax.ShapeDtypeStruct((1, ql), F32), jax.ShapeDtypeStruct((1, kvl), F32)])
    dproj = jnp.concatenate([du, dv, dcq, dckv, dkr], axis=1)
    dw_in, = _matmul("e_proj_dw", Mat(h0, t, d), Mat(dproj, t, pi), "tn", [_out(d, pi, BF16)], _pick(d, 1024), _pick(pi, 1024), kt)
    dh0, = _matmul("e_proj_dx", Mat(dproj, t, pi), Mat(w_in_all, d, pi), "nt", [_out(t, d, F32)], tm, _pick(d, 1024), _pick(pi, 4096))
    dx0, _, dg_e = _norm_bwd("e_norm_bwd", dh0, xs, g_e, dx1, tr)

    gfull = jnp.concatenate([dw_in[:, 2 * gw:2 * gw + c2], _unpad_rope(dw_in[:, 2 * gw + c2:]), dw_in[:, :2 * gw]], axis=1)
    gw_in = _stack_cols(gfull)
    gq = jnp.concatenate([dw_q[:, :hw].reshape(ql, heads, LANES), _unpad_rope(dw_q[:, hw:].reshape(ql, heads, LANES))], axis=-1)
    gw_uq = _stack_cols(gq.reshape(ql, heads * (LANES + ROPE)))
    gkv = jnp.concatenate([dw_kv[:, :hw].reshape(kvl, heads, LANES), dw_kv[:, hw:].reshape(kvl, heads, LANES)], axis=-1)
    gw_ukv = _stack_cols(gkv.reshape(kvl, heads * 2 * LANES))
    sc_e, tok = scatter("e", [gw_in, gw_uq, gw_ukv, dw_eout.reshape(N_CHIPS, mix // N_CHIPS, d)])

    def reduced(tag, sc, after):
        send, recv, part, lands = sc
        part, lands = _scatter_wait(tag, send, recv, part, lands, after)
        half = [_chip_sum(p, r) for p, r in zip(part, lands)]
        return [r.reshape(2 * r.shape[1], r.shape[2]) for r in _pair_share(tag, half)]

    r_w1_1, r_w2_1 = reduced("m1", sc_m1, tok)
    r_oin, r_oout, r_small = reduced("o", sc_o, r_w2_1)
    r_w1_0, r_w2_0 = reduced("m0", sc_m0, r_small)
    late = {
        "o_w_in": _adamw(o_w_in, [r_oin], m_o_w_in, v_o_w_in),
        "o_w_out": _adamw(o_w_out, [r_oout], m_o_w_out, v_o_w_out),
        "mlp_w1": _adamw(mlp_w1, [r_w1_0, r_w1_1], m_mlp_w1, v_mlp_w1),
        "mlp_w2": _adamw(mlp_w2, [r_w2_0, r_w2_1], m_mlp_w2, v_mlp_w2),
    }

    small_like = [e_norm_mix, e_q_norm, e_kv_norm, e_v_norm, e_sgu_w, e_sgu_b, e_mla_out_norm, e_sgu_out_norm, mlp_norm, final_norm]
    small_grads = [dg_e, dg_q, dg_kv, dg_vn, dsgu_w, dsgu_b8[:, 0, :], dg_mla, dg_sgu, jnp.concatenate([dg_m0, dg_m1], axis=0), dg_f]
    sflat = _pack_small(small_grads)
    pad = (-sflat.shape[0]) % 8
    sflat = jnp.pad(sflat, ((0, pad), (0, 0)))
    g_small = _sum_devices(_gather_all_devices(sflat))

    def padded(arrs):
        return jnp.pad(_pack_small(arrs), ((0, pad), (0, 0)))

    s_m = [m_e_norm_mix, m_e_q_norm, m_e_kv_norm, m_e_v_norm, m_e_sgu_w, m_e_sgu_b, m_e_mla_out_norm, m_e_sgu_out_norm, m_mlp_norm, m_final_norm]
    s_v = [v_e_norm_mix, v_e_q_norm, v_e_kv_norm, v_e_v_norm, v_e_sgu_w, v_e_sgu_b, v_e_mla_out_norm, v_e_sgu_out_norm, v_mlp_norm, v_final_norm]
    s_out = [_unpack_small(o[0], small_like)
             for o in _adamw(padded(small_like)[None], [g_small], padded(s_m)[None], padded(s_v)[None])]

    sm = [o[0] for o in _adamw(small_shard[None], [r_small], _small_shard(m_o_norm_mix, m_o_conv_w[0])[None],
                               _small_shard(v_o_norm_mix, v_o_conv_w[0])[None])]

    r_in, r_uq, r_ukv, r_eout = reduced("e", sc_e, late["mlp_w2"][1])
    big = dict(late)
    big.update({
        "e_w_in": _adamw(e_w_in, [r_in], m_e_w_in, v_e_w_in),
        "e_w_uq": _adamw(e_w_uq, [r_uq], m_e_w_uq, v_e_w_uq),
        "e_w_ukv": _adamw(e_w_ukv, [r_ukv], m_e_w_ukv, v_e_w_ukv),
        "e_w_out": _adamw(e_w_out, [r_eout], m_e_w_out, v_e_w_out),
    })

    names = ["e_norm_mix", "e_w_in", "e_q_norm", "e_w_uq", "e_kv_norm", "e_w_ukv", "e_v_norm", "e_sgu_w", "e_sgu_b",
             "e_mla_out_norm", "e_sgu_out_norm", "e_w_out", "o_norm_mix", "o_w_in", "o_conv_w", "o_w_out",
             "mlp_norm", "mlp_w1", "mlp_w2", "final_norm"]
    shapes = {"e_w_in": e_w_in.shape, "e_w_uq": e_w_uq.shape, "e_w_ukv": e_w_ukv.shape, "e_w_out": e_w_out.shape,
              "o_w_in": o_w_in.shape, "o_w_out": o_w_out.shape, "mlp_w1": mlp_w1.shape, "mlp_w2": mlp_w2.shape}
    small_names = ["e_norm_mix", "e_q_norm", "e_kv_norm", "e_v_norm", "e_sgu_w", "e_sgu_b", "e_mla_out_norm",
                   "e_sgu_out_norm", "mlp_norm", "final_norm"]

    def leaf(kind, name):
        if name in big:
            return big[name][kind].reshape(shapes[name])
        if name == "o_norm_mix":
            return sm[kind][0:1]
        if name == "o_conv_w":
            return sm[kind][16:19].reshape(o_conv_w.shape)
        return s_out[kind][small_names.index(name)]

    outs = [loss, dx0.reshape(x.shape)]
    for kind in range(4):
        outs += [leaf(kind, nm) for nm in names]
    return tuple(outs)


def _gcd(a, b):
    while b:
        a, b = b, a % b
    return a
```

```python
import functools

import jax
import jax.numpy as jnp
from jax import lax
from jax.experimental import pallas as pl
from jax.experimental.pallas import tpu as pltpu

F32 = jnp.float32
BF16 = jnp.bfloat16
MESH = pl.DeviceIdType.MESH

LANES = 128
ROPE = 64
ROPE_HALF = ROPE // 2
ROPE_BASE = 10000.0
EPS = 1e-6
N_CHIPS = 4
VMEM_LIMIT = 48 * 1024 * 1024
NEG = -1e30

ADAM_LR = 0.001
ADAM_B1 = 0.9
ADAM_B2 = 0.999
ADAM_EPS = 1e-08
ADAM_WD = 0.01
ADAM_STEP = 10


def _pick(n, target, step=LANES):
    best = None
    for t in range(step, min(n, target) + 1, step):
        if n % t == 0:
            best = t
    return best if best is not None else n


def _params(sem, vmem=VMEM_LIMIT):
    return pltpu.CompilerParams(dimension_semantics=sem, vmem_limit_bytes=vmem)


class Mat:
    def __init__(self, arr, rows, cols, kind="plain", lead=(), col_off=0, shape=None, dtype=None):
        self.arr, self.rows, self.cols, self.kind, self.lead, self.col_off = arr, rows, cols, kind, tuple(lead), col_off
        self.shape = tuple(arr.shape) if arr is not None else tuple(shape)
        self.dtype = arr.dtype if arr is not None else dtype

    def sds(self):
        return jax.ShapeDtypeStruct(self.shape, self.dtype)

    def spec(self, br, bc, gridmap):
        lead, nl = self.lead, len(self.lead)
        if self.kind == "plain":
            assert self.col_off % bc == 0 and self.rows % br == 0 and self.cols % bc == 0, (self.shape, br, bc)
            off = self.col_off // bc
            block = (None,) * nl + (br, bc)

            def phys(rb, cb):
                return lead + (rb, cb + off)
        elif self.kind == "colstack":
            cs = self.shape[-1]
            assert cs % bc == 0 and self.rows % br == 0, (self.shape, br, bc)
            q = cs // bc
            block = (None,) * (nl + 1) + (br, bc)

            def phys(rb, cb):
                return (cb // q,) + lead + (rb, cb % q)
        else:
            rs = self.shape[-2]
            assert rs % br == 0 and self.cols % bc == 0, (self.shape, br, bc)
            q = rs // br
            block = (None,) * (nl + 1) + (br, bc)

            def phys(rb, cb):
                return (rb // q,) + lead + (rb % q, cb)

        return pl.BlockSpec(block, lambda *g: phys(*gridmap(*g)))


def _matmul(name, a, b, mode, outs, tm, tn, tk, epilogue=None, extras=(), deps=()):
    if mode == "nn":
        m, k, n = a.rows, a.cols, b.cols
        a_spec = a.spec(tm, tk, lambda i, j, kk: (i, kk))
        b_spec = b.spec(tk, tn, lambda i, j, kk: (kk, j))
        dims = (((1,), (0,)), ((), ()))
    elif mode == "nt":
        m, k, n = a.rows, a.cols, b.rows
        a_spec = a.spec(tm, tk, lambda i, j, kk: (i, kk))
        b_spec = b.spec(tn, tk, lambda i, j, kk: (j, kk))
        dims = (((1,), (1,)), ((), ()))
    else:
        k, m, n = a.rows, a.cols, b.cols
        a_spec = a.spec(tk, tm, lambda i, j, kk: (kk, i))
        b_spec = b.spec(tk, tn, lambda i, j, kk: (kk, j))
        dims = (((0,), (0,)), ((), ()))
    assert m % tm == 0 and n % tn == 0 and k % tk == 0, (name, m, n, k, tm, tn, tk)
    grid = (m // tm, n // tn, k // tk)
    nk = grid[2]
    n_ex, n_out, n_dep = len(extras), len(outs), len(deps)
    tile = lambda i, j, kk: (i, j)

    def finish(z, ex, out_refs):
        vals = epilogue(z, *[e[...] for e in ex]) if epilogue is not None else (z,)
        for o, v in zip(out_refs, vals):
            o[...] = v.astype(o.dtype)

    def body_single(a_ref, b_ref, *rest):
        finish(lax.dot_general(a_ref[...], b_ref[...], dims, preferred_element_type=F32),
               rest[:n_ex], rest[n_ex + n_dep:n_ex + n_dep + n_out])

    def body_acc(a_ref, b_ref, *rest):
        acc = rest[-1]
        kk = pl.program_id(2)

        @pl.when(kk == 0)
        def _():
            acc[...] = jnp.zeros_like(acc)

        acc[...] += lax.dot_general(a_ref[...], b_ref[...], dims, preferred_element_type=F32)

        @pl.when(kk == nk - 1)
        def _():
            finish(acc[...], rest[:n_ex], rest[n_ex + n_dep:n_ex + n_dep + n_out])

    res = pl.pallas_call(
        body_single if nk == 1 else body_acc, name=name, grid=grid,
        in_specs=[a_spec, b_spec] + [e.spec(tm, tn, tile) for e in extras]
        + [pl.BlockSpec(memory_space=pl.ANY) for _ in deps],
        out_specs=[o.spec(tm, tn, tile) for o in outs],
        out_shape=[o.sds() for o in outs],
        scratch_shapes=[] if nk == 1 else [pltpu.VMEM((tm, tn), F32)],
        compiler_params=_params(("parallel", "parallel", "arbitrary")),
    )(a.arr, b.arr, *[e.arr for e in extras], *deps)
    return res


def _out(rows, cols, dtype, kind="plain", lead=(), shape=None):
    return Mat(None, rows, cols, kind, lead, shape=shape if shape is not None else (rows, cols), dtype=dtype)


def _rt(arr, tr, width=None, cb=0):
    width = arr.shape[1] if width is None else width
    return arr, pl.BlockSpec((tr, width), lambda i: (i, cb))


def _whole(arr):
    nd = arr.ndim
    return arr, pl.BlockSpec(arr.shape, lambda i: (0,) * nd)


def _rowwise(name, fn, n_steps, ins, outs, accs=(), deps=()):
    n_in, n_out, n_acc, n_dep = len(ins), len(outs), len(accs), len(deps)

    def body(*refs):
        vals = fn(*[r[...] for r in refs[:n_in]])
        if not isinstance(vals, (tuple, list)):
            vals = (vals,)
        for ref, v in zip(refs[n_in + n_dep:n_in + n_dep + n_out], vals[:n_out]):
            ref[...] = v.astype(ref.dtype)
        if n_acc:
            acc_refs = refs[n_in + n_dep + n_out:]

            @pl.when(pl.program_id(0) == 0)
            def _():
                for ref in acc_refs:
                    ref[...] = jnp.zeros_like(ref)

            for ref, v in zip(acc_refs, vals[n_out:]):
                ref[...] += v

    acc_specs = [pl.BlockSpec(s.shape, lambda i, nd=len(s.shape): (0,) * nd) for s in accs]
    res = pl.pallas_call(
        body, name=name, grid=(n_steps,),
        in_specs=[s for _, s in ins] + [pl.BlockSpec(memory_space=pl.ANY) for _ in deps],
        out_specs=[s for _, s in outs] + acc_specs,
        out_shape=[o for o, _ in outs] + list(accs),
        compiler_params=_params(("arbitrary",) if n_acc else ("parallel",)),
    )(*[a for a, _ in ins], *deps)
    return res


def _rt_out(t, width, dtype, tr):
    return jax.ShapeDtypeStruct((t, width), dtype), pl.BlockSpec((tr, width), lambda i: (i, 0))


def _rms(x, g):
    r = lax.rsqrt(jnp.mean(x * x, axis=-1, keepdims=True) + EPS)
    return x * r * g


def _rms_bwd(dy, x, g):
    r = lax.rsqrt(jnp.mean(x * x, axis=-1, keepdims=True) + EPS)
    xh = x * r
    dxh = dy * g
    dx = r * (dxh - xh * jnp.mean(dxh * xh, axis=-1, keepdims=True))
    dg = jnp.sum(dy * xh, axis=0, keepdims=True)
    return dx, dg


def _gelu(x):
    k = 0.7978845608028654
    th = jnp.tanh(k * (x + 0.044715 * (x * x * x)))
    return x * (0.5 * (1.0 + th))


def _gelu_grad(x):
    k = 0.7978845608028654
    x2 = x * x
    th = jnp.tanh(k * (x + 0.044715 * (x2 * x)))
    return 0.5 * (1.0 + th) + 0.5 * x * (1.0 - th * th) * (k * (1.0 + 3.0 * 0.044715 * x2))


def _norm_fwd(name, x, g, tr):
    t, d = x.shape
    return _rowwise(name, lambda xv, gv: _rms(xv, gv), t // tr, [_rt(x, tr), _whole(g)], [_rt_out(t, d, BF16, tr)])[0]


def _norm_bwd(name, dh, x, g, dres, tr):
    t, d = x.shape

    def fn(dhv, xv, gv, drv):
        dx, dg = _rms_bwd(dhv, xv, gv)
        dx = dx + drv
        return dx, dx, dg

    return _rowwise(name, fn, t // tr, [_rt(dh, tr), _rt(x, tr), _whole(g), _rt(dres, tr)],
                    [_rt_out(t, d, F32, tr), _rt_out(t, d, BF16, tr)], [jax.ShapeDtypeStruct((1, d), F32)])


def _rope_tables(posf, invf, cmask, smask, tr):
    t = posf.shape[0]

    def fn(p, f, cm, sm):
        ang = p * f
        return jnp.cos(ang) * cm, jnp.sin(ang) * sm

    return _rowwise("rope_tables", fn, t // tr, [_rt(posf, tr), _whole(invf), _whole(cmask), _whole(smask)],
                    [_rt_out(t, LANES, F32, tr), _rt_out(t, LANES, F32, tr)])


def _rot(v, c, s):
    return v * c + pltpu.roll(v, ROPE, axis=1) * s


def _rot_bwd(dv, c, s):
    return dv * c + pltpu.roll(dv * s, ROPE, axis=1)


def _rope_fwd(qfull, proj, kr_cb, ctab, stab, heads, tr):
    t = qfull.shape[0]
    hw = heads * LANES

    def fn(q, kr, c, s):
        parts = [q[:, :hw]] + [_rot(q[:, hw + h * LANES: hw + (h + 1) * LANES], c, s) for h in range(heads)]
        return jnp.concatenate(parts, axis=1), _rot(kr, c, s)

    return _rowwise("rope_fwd", fn, t // tr, [_rt(qfull, tr), _rt(proj, tr, LANES, kr_cb), _rt(ctab, tr), _rt(stab, tr)],
                    [_rt_out(t, 2 * hw, BF16, tr), _rt_out(t, LANES, BF16, tr)])


def _rope_bwd(dq1, dq2, dkr_h, ctab, stab, heads, tr):
    t = dq1.shape[0]
    hw = heads * LANES

    def fn(a, b, dk, c, s):
        parts = [a] + [_rot_bwd(b[:, h * LANES:(h + 1) * LANES], c, s) for h in range(heads)]
        dks = dk[0]
        for h in range(1, heads):
            dks = dks + dk[h]
        return jnp.concatenate(parts, axis=1), _rot_bwd(dks, c, s)

    dk_spec = pl.BlockSpec((heads, tr, LANES), lambda i: (0, i, 0))
    return _rowwise("rope_bwd", fn, t // tr, [_rt(dq1, tr), _rt(dq2, tr), (dkr_h, dk_spec), _rt(ctab, tr), _rt(stab, tr)],
                    [_rt_out(t, 2 * hw, BF16, tr), _rt_out(t, LANES, BF16, tr)])


def _dot_nt(a, b):
    return lax.dot_general(a, b, (((1,), (1,)), ((), ())), preferred_element_type=F32)


def _dot_tn(a, b):
    return lax.dot_general(a, b, (((0,), (0,)), ((), ())), preferred_element_type=F32)


def _dot(a, b):
    return jnp.dot(a, b, preferred_element_type=F32)


def _ranges(n_blocks):
    n_var = min(4, n_blocks)
    assert n_blocks % n_var == 0
    return n_var, n_blocks // n_var


def _row_of(col):
    return col.T[:8, :]


def _attn_fwd(qall, kvall, kr, heads, scale, tq):
    t = qall.shape[0]
    nq = t // tq
    n_var, per = _ranges(nq)

    def body(qn_ref, qr_ref, kn_ref, v_ref, kr_ref, o_ref, lse_ref, lser_ref):
        i = pl.program_id(1)
        for var in range(n_var):
            kv = (var + 1) * per * tq

            @pl.when(jnp.logical_and(i >= var * per, i < (var + 1) * per))
            def _(kv=kv):
                s = (_dot_nt(qn_ref[...], kn_ref[:kv, :]) + _dot_nt(qr_ref[...], kr_ref[:kv, :])) * scale
                rows = i * tq + lax.broadcasted_iota(jnp.int32, (tq, kv), 0)
                cols = lax.broadcasted_iota(jnp.int32, (tq, kv), 1)
                s = jnp.where(cols <= rows, s, NEG)
                m = jnp.max(s, axis=-1, keepdims=True)
                p = jnp.exp(s - m)
                l = jnp.sum(p, axis=-1, keepdims=True)
                o_ref[...] = _dot(p.astype(BF16), v_ref[:kv, :]) / l
                lse = jnp.broadcast_to(m + jnp.log(l), (tq, LANES))
                lse_ref[...] = lse
                lser_ref[...] = _row_of(lse)

    return pl.pallas_call(
        body, name="attn_fwd", grid=(heads, nq),
        in_specs=[pl.BlockSpec((tq, LANES), lambda h, i: (i, h)),
                  pl.BlockSpec((tq, LANES), lambda h, i: (i, heads + h)),
                  pl.BlockSpec((t, LANES), lambda h, i: (0, h)),
                  pl.BlockSpec((t, LANES), lambda h, i: (0, heads + h)),
                  pl.BlockSpec((t, LANES), lambda h, i: (0, 0))],
        out_specs=[pl.BlockSpec((tq, LANES), lambda h, i: (i, h)),
                   pl.BlockSpec((None, tq, LANES), lambda h, i: (h, i, 0)),
                   pl.BlockSpec((None, 8, tq), lambda h, i: (h, 0, i))],
        out_shape=[jax.ShapeDtypeStruct((t, heads * LANES), F32), jax.ShapeDtypeStruct((heads, t, LANES), F32),
                   jax.ShapeDtypeStruct((heads, 8, t), F32)],
        compiler_params=_params(("parallel", "parallel")),
    )(qall, qall, kvall, kvall, kr)


def _attn_dq(qall, kvall, kr, do, lse, delta, heads, scale, tq):
    t = qall.shape[0]
    nq = t // tq
    n_var, per = _ranges(nq)

    def body(qn_ref, qr_ref, kn_ref, v_ref, kr_ref, do_ref, lse_ref, dl_ref, dq1_ref, dq2_ref):
        i = pl.program_id(1)
        for var in range(n_var):
            kv = (var + 1) * per * tq

            @pl.when(jnp.logical_and(i >= var * per, i < (var + 1) * per))
            def _(kv=kv):
                k1, k2 = kn_ref[:kv, :], kr_ref[:kv, :]
                s = (_dot_nt(qn_ref[...], k1) + _dot_nt(qr_ref[...], k2)) * scale
                rows = i * tq + lax.broadcasted_iota(jnp.int32, (tq, kv), 0)
                cols = lax.broadcasted_iota(jnp.int32, (tq, kv), 1)
                p = jnp.where(cols <= rows, jnp.exp(s - lse_ref[...][:, :1]), 0.0)
                dp = _dot_nt(do_ref[...], v_ref[:kv, :])
                ds = (p * (dp - dl_ref[...][:, :1]) * scale).astype(BF16)
                dq1_ref[...] = _dot(ds, k1)
                dq2_ref[...] = _dot(ds, k2)

    qblk = lambda off: pl.BlockSpec((tq, LANES), lambda h, i: (i, off + h))
    full = lambda off: pl.BlockSpec((t, LANES), lambda h, i: (0, off + h))
    stat = pl.BlockSpec((None, tq, LANES), lambda h, i: (h, i, 0))
    return pl.pallas_call(
        body, name="attn_dq", grid=(heads, nq),
        in_specs=[qblk(0), qblk(heads), full(0), full(heads), pl.BlockSpec((t, LANES), lambda h, i: (0, 0)),
                  qblk(0), stat, stat],
        out_specs=[qblk(0), qblk(0)],
        out_shape=[jax.ShapeDtypeStruct((t, heads * LANES), F32)] * 2,
        compiler_params=_params(("parallel", "parallel")),
    )(qall, qall, kvall, kvall, kr, do, lse, delta)


def _attn_dkv(qall, kvall, kr, do, lse_row, delta_row, heads, scale, tk):
    t = qall.shape[0]
    nk = t // tk
    n_var, per = _ranges(nk)

    def body(qn_ref, qr_ref, kn_ref, v_ref, kr_ref, do_ref, lse_ref, dl_ref, dk_ref, dv_ref, dkr_ref):
        j = pl.program_id(1)
        for var in range(n_var):
            q0 = var * per * tk
            nq = t - q0

            @pl.when(jnp.logical_and(j >= var * per, j < (var + 1) * per))
            def _(q0=q0, nq=nq):
                qn, qr, do_v = qn_ref[q0:, :], qr_ref[q0:, :], do_ref[q0:, :]
                st = (_dot_nt(kn_ref[...], qn) + _dot_nt(kr_ref[...], qr)) * scale
                keys = j * tk + lax.broadcasted_iota(jnp.int32, (tk, nq), 0)
                queries = q0 + lax.broadcasted_iota(jnp.int32, (tk, nq), 1)
                pt = jnp.where(keys <= queries, jnp.exp(st - lse_ref[0:1, q0:]), 0.0)
                dpt = _dot_nt(v_ref[...], do_v)
                dst = (pt * (dpt - dl_ref[0:1, q0:]) * scale).astype(BF16)
                dv_ref[...] = _dot(pt.astype(BF16), do_v).astype(dv_ref.dtype)
                dk_ref[...] = _dot(dst, qn).astype(dk_ref.dtype)
                dkr_ref[...] = _dot(dst, qr)

    kblk = lambda off: pl.BlockSpec((tk, LANES), lambda h, j: (j, off + h))
    full = lambda off: pl.BlockSpec((t, LANES), lambda h, j: (0, off + h))
    stat = pl.BlockSpec((None, 8, t), lambda h, j: (h, 0, 0))
    return pl.pallas_call(
        body, name="attn_dkv", grid=(heads, nk),
        in_specs=[full(0), full(heads), kblk(0), kblk(heads), pl.BlockSpec((tk, LANES), lambda h, j: (j, 0)),
                  full(0), stat, stat],
        out_specs=[kblk(0), kblk(0), pl.BlockSpec((None, tk, LANES), lambda h, j: (h, j, 0))],
        out_shape=[jax.ShapeDtypeStruct((t, heads * LANES), BF16)] * 2 + [jax.ShapeDtypeStruct((heads, t, LANES), F32)],
        compiler_params=_params(("parallel", "parallel")),
    )(qall, qall, kvall, kvall, kr, do, lse_row, delta_row)


def _tril():
    return lax.broadcasted_iota(jnp.int32, (LANES, LANES), 0) >= lax.broadcasted_iota(jnp.int32, (LANES, LANES), 1)


def _group_norm(vg):
    mu = jnp.mean(vg, axis=-1, keepdims=True)
    vc = vg - mu
    rs = lax.rsqrt(jnp.mean(vc * vc, axis=-1, keepdims=True) + EPS)
    return vc * rs, rs


def _sgu_fwd(proj, gain, w, bias, groups, rb):
    t = proj.shape[0]
    gw = groups * LANES
    cpb = rb // LANES

    def body(u_ref, v_ref, gain_ref, w_ref, b_ref, s_ref):
        tril = _tril()
        for g in range(groups):
            wt = jnp.where(tril, w_ref[g], 0.0).astype(BF16)
            cols = slice(g * LANES, (g + 1) * LANES)
            for ci in range(cpb):
                rows = slice(ci * LANES, (ci + 1) * LANES)
                ug = _gelu(u_ref[rows, cols])
                vh, _ = _group_norm(_gelu(v_ref[rows, cols]))
                vn = vh * gain_ref[:, cols]
                y = _dot(wt, vn.astype(BF16)) + b_ref[g]
                s_ref[rows, cols] = ug * y

    return pl.pallas_call(
        body, name="sgu_fwd", grid=(t // rb,),
        in_specs=[pl.BlockSpec((rb, gw), lambda i: (i, 0)), pl.BlockSpec((rb, gw), lambda i: (i, 1)),
                  pl.BlockSpec((1, gw), lambda i: (0, 0)),
                  pl.BlockSpec((groups, LANES, LANES), lambda i: (0, 0, 0)),
                  pl.BlockSpec((groups, LANES, LANES), lambda i: (0, 0, 0))],
        out_specs=pl.BlockSpec((rb, gw), lambda i: (i, 0)),
        out_shape=jax.ShapeDtypeStruct((t, gw), F32),
        compiler_params=_params(("parallel",)),
    )(proj, proj, gain, w, bias)


def _sgu_bwd(proj, ds, gain, w, bias, groups, rb):
    t = proj.shape[0]
    gw = groups * LANES
    cpb = rb // LANES
    n_steps = t // rb

    def body(u_ref, v_ref, ds_ref, gain_ref, w_ref, b_ref, du_ref, dv_ref, dw_ref, db_ref, dg_ref, dy_acc):
        step = pl.program_id(0)

        @pl.when(step == 0)
        def _():
            dw_ref[...] = jnp.zeros_like(dw_ref)
            dy_acc[...] = jnp.zeros_like(dy_acc)
            dg_ref[...] = jnp.zeros_like(dg_ref)

        tril = _tril()
        for g in range(groups):
            wt = jnp.where(tril, w_ref[g], 0.0).astype(BF16)
            cols = slice(g * LANES, (g + 1) * LANES)
            gain_g = gain_ref[:, cols]
            for ci in range(cpb):
                rows = slice(ci * LANES, (ci + 1) * LANES)
                u_raw, v_raw, ds_v = u_ref[rows, cols], v_ref[rows, cols], ds_ref[rows, cols]
                ug = _gelu(u_raw)
                vh, rs = _group_norm(_gelu(v_raw))
                vn = (vh * gain_g).astype(BF16)
                y = _dot(wt, vn) + b_ref[g]
                dy = ds_v * ug
                dyb = dy.astype(BF16)
                du_ref[rows, cols] = (ds_v * y * _gelu_grad(u_raw)).astype(du_ref.dtype)
                dy_acc[g] += dy
                dw_ref[g] += _dot_nt(dyb, vn)
                dvn = _dot_tn(wt, dyb)
                dg_ref[:, cols] += jnp.sum(dvn * vh, axis=0, keepdims=True)
                dvh = dvn * gain_g
                dvg = rs * (dvh - jnp.mean(dvh, axis=-1, keepdims=True)
                            - vh * jnp.mean(dvh * vh, axis=-1, keepdims=True))
                dv_ref[rows, cols] = (dvg * _gelu_grad(v_raw)).astype(dv_ref.dtype)

        @pl.when(step == n_steps - 1)
        def _():
            ones = jnp.ones((8, LANES), F32)
            for g in range(groups):
                dw_ref[g] = jnp.where(tril, dw_ref[g], 0.0)
                db_ref[g] = lax.dot_general(ones, dy_acc[g], (((1,), (1,)), ((), ())),
                                            precision=lax.Precision.HIGHEST, preferred_element_type=F32)

    blk = lambda cb: pl.BlockSpec((rb, gw), lambda i: (i, cb))
    whole3 = pl.BlockSpec((groups, LANES, LANES), lambda i: (0, 0, 0))
    return pl.pallas_call(
        body, name="sgu_bwd", grid=(n_steps,),
        in_specs=[blk(0), blk(1), blk(0), pl.BlockSpec((1, gw), lambda i: (0, 0)), whole3, whole3],
        out_specs=[blk(0), blk(0), whole3, pl.BlockSpec((groups, 8, LANES), lambda i: (0, 0, 0)),
                   pl.BlockSpec((1, gw), lambda i: (0, 0))],
        out_shape=[jax.ShapeDtypeStruct((t, gw), BF16), jax.ShapeDtypeStruct((t, gw), BF16),
                   jax.ShapeDtypeStruct((groups, LANES, LANES), F32), jax.ShapeDtypeStruct((groups, 8, LANES), F32),
                   jax.ShapeDtypeStruct((1, gw), F32)],
        scratch_shapes=[pltpu.VMEM((groups, LANES, LANES), F32)],
        compiler_params=_params(("arbitrary",)),
    )(proj, proj, ds, gain, w, bias)


def _shift_down(z, s):
    rows = lax.broadcasted_iota(jnp.int32, z.shape, 0)
    return jnp.where(rows >= s, pltpu.roll(z, s, axis=0), 0.0)


def _shift_up(z, s):
    n = z.shape[0]
    rows = lax.broadcasted_iota(jnp.int32, z.shape, 0)
    return jnp.where(rows < n - s, pltpu.roll(z, n - s, axis=0), 0.0)


def _conv_fwd(proj3, cw, tc):
    _, t, cd = proj3.shape

    def body(p_ref, w_ref, o_ref):
        z = p_ref[1] * p_ref[2]
        w = w_ref[...]
        zc = w[2:3] * z + w[1:2] * _shift_down(z, 1) + w[0:1] * _shift_down(z, 2)
        o_ref[...] = (p_ref[0] * zc).astype(o_ref.dtype)

    return pl.pallas_call(
        body, name="conv_fwd", grid=(cd // tc,),
        in_specs=[pl.BlockSpec((3, t, tc), lambda j: (0, 0, j)), pl.BlockSpec((8, tc), lambda j: (0, j))],
        out_specs=pl.BlockSpec((t, tc), lambda j: (0, j)),
        out_shape=jax.ShapeDtypeStruct((t, cd), BF16),
        compiler_params=_params(("parallel",)),
    )(proj3, cw)


def _conv_bwd(proj3, cw, dbz, tc):
    _, t, cd = proj3.shape

    def body(p_ref, w_ref, d_ref, o_ref, dw_ref):
        b, c, xin = p_ref[0], p_ref[1], p_ref[2]
        w = w_ref[...]
        z = c * xin
        z1, z2 = _shift_down(z, 1), _shift_down(z, 2)
        zc = w[2:3] * z + w[1:2] * z1 + w[0:1] * z2
        d = d_ref[...]
        dzc = d * b
        dz = w[2:3] * dzc + w[1:2] * _shift_up(dzc, 1) + w[0:1] * _shift_up(dzc, 2)
        o_ref[0] = (d * zc).astype(o_ref.dtype)
        o_ref[1] = (dz * xin).astype(o_ref.dtype)
        o_ref[2] = (dz * c).astype(o_ref.dtype)
        row = lax.broadcasted_iota(jnp.int32, (8, tc), 0)
        dw0 = jnp.sum(dzc * z2, axis=0, keepdims=True)
        dw1 = jnp.sum(dzc * z1, axis=0, keepdims=True)
        dw2 = jnp.sum(dzc * z, axis=0, keepdims=True)
        dw_ref[...] = jnp.where(row == 0, dw0, 0.0) + jnp.where(row == 1, dw1, 0.0) + jnp.where(row == 2, dw2, 0.0)

    return pl.pallas_call(
        body, name="conv_bwd", grid=(cd // tc,),
        in_specs=[pl.BlockSpec((3, t, tc), lambda j: (0, 0, j)), pl.BlockSpec((8, tc), lambda j: (0, j)),
                  pl.BlockSpec((t, tc), lambda j: (0, j))],
        out_specs=[pl.BlockSpec((3, t, tc), lambda j: (0, 0, j)), pl.BlockSpec((8, tc), lambda j: (0, j))],
        out_shape=[jax.ShapeDtypeStruct((3, t, cd), BF16), jax.ShapeDtypeStruct((8, cd), F32)],
        compiler_params=_params(("parallel",)),
    )(proj3, cw, dbz)


def _place():
    x, y, c = lax.axis_index("x"), lax.axis_index("y"), lax.axis_index("c")
    chips = [(1 - x, y), (x, 1 - y), (1 - x, 1 - y)]
    return x, y, c, chips


def _any_specs(n):
    return [pl.BlockSpec(memory_space=pl.ANY) for _ in range(n)]


HBM_SPEC = pl.BlockSpec(memory_space=pltpu.HBM)
SEM_SPEC = pl.BlockSpec(memory_space=pltpu.SEMAPHORE)
ORDERED_EFFECT = pltpu.SideEffectType.DATAFLOW_SIDE_EFFECTING


def _in_hbm(a):
    return pltpu.with_memory_space_constraint(a, pltpu.HBM)


def _token():
    return jax.ShapeDtypeStruct((8, LANES), F32), pl.BlockSpec(memory_space=pltpu.VMEM)


def _gather_start(groups):
    sizes = [len(g) for g in groups]
    flat = [b for g in groups for b in g]
    n, ng = len(flat), len(groups)

    def body(*refs):
        ins, sems, token = refs[:n], refs[n:n + 2 * ng], refs[-1]
        x, y, c, chips = _place()
        me = 2 * x + y
        i = 0
        for gi, size in enumerate(sizes):
            for j in range(size):
                blk = ins[i].at[me, c]
                for k, chip in enumerate(chips):
                    pltpu.make_async_remote_copy(src_ref=blk, dst_ref=blk, send_sem=sems[2 * gi].at[3 * j + k],
                                                 recv_sem=sems[2 * gi + 1].at[3 * j + k],
                                                 device_id=(*chip, c), device_id_type=MESH).start()
                i += 1
        token[...] = jnp.zeros_like(token)

    tok_shape, tok_spec = _token()
    res = pl.pallas_call(
        body, name="gather_start",
        in_specs=[HBM_SPEC] * n,
        out_specs=[SEM_SPEC] * (2 * ng) + [HBM_SPEC] * n + [tok_spec],
        out_shape=[pltpu.SemaphoreType.DMA((3 * size,)) for size in sizes for _ in (0, 1)]
        + [pltpu.HBM(b.shape, b.dtype) for b in flat] + [tok_shape],
        input_output_aliases={i: 2 * ng + i for i in range(n)},
        compiler_params=pltpu.CompilerParams(has_side_effects=ORDERED_EFFECT),
    )(*[_in_hbm(b) for b in flat])
    out, i = [], 2 * ng
    for gi, size in enumerate(sizes):
        out.append((res[2 * gi], res[2 * gi + 1], list(res[i:i + size])))
        i += size
    return out, res[-1]


def _gather_wait(tag, send, recv, bufs, after):
    n = len(bufs)

    def body(*refs):
        ins, send_ref, recv_ref = refs[:n], refs[n], refs[n + 1]
        x, y, c, chips = _place()
        me = 2 * x + y
        for j in range(n):
            for k, (px, py) in enumerate(chips):
                cp = pltpu.make_async_remote_copy(src_ref=ins[j].at[me, c], dst_ref=ins[j].at[2 * px + py, c],
                                                  send_sem=send_ref.at[3 * j + k], recv_sem=recv_ref.at[3 * j + k],
                                                  device_id=(px, py, c), device_id_type=MESH)
                cp.wait_send()
                cp.wait_recv()

    return pl.pallas_call(
        body, name="gather_wait_" + tag,
        in_specs=[HBM_SPEC] * n + [SEM_SPEC, SEM_SPEC, pl.BlockSpec(memory_space=pl.ANY)],
        out_specs=[HBM_SPEC] * n,
        out_shape=[pltpu.HBM(b.shape, b.dtype) for b in bufs],
        input_output_aliases={i: i for i in range(n)},
        compiler_params=pltpu.CompilerParams(has_side_effects=ORDERED_EFFECT),
    )(*bufs, send, recv, after)


def _gather_forward(tag, bufs):
    n = len(bufs)

    def body(*refs):
        ins, outs = refs[:n], refs[n:2 * n]
        send, recv = refs[2 * n:]
        x, y, c, chips = _place()
        sib = (x, y, 1 - c)

        def cp(i, k, slot, half):
            return pltpu.make_async_remote_copy(src_ref=ins[i].at[slot, half], dst_ref=outs[i].at[slot, half],
                                                send_sem=send.at[3 * i + k], recv_sem=recv.at[3 * i + k],
                                                device_id=sib, device_id_type=MESH)

        cps = [cp(i, k, 2 * px + py, c) for i in range(n) for k, (px, py) in enumerate(chips)]
        for d in cps:
            d.start()
        for i in range(n):
            for k, (px, py) in enumerate(chips):
                cp(i, k, 2 * px + py, 1 - c).wait_recv()
        for d in cps:
            d.wait_send()

    return pl.pallas_call(
        body, name="gather_forward_" + tag,
        in_specs=_any_specs(n), out_specs=_any_specs(n),
        out_shape=[jax.ShapeDtypeStruct(b.shape, b.dtype) for b in bufs],
        scratch_shapes=[pltpu.SemaphoreType.DMA((3 * n,))] * 2,
        input_output_aliases={i: i for i in range(n)},
        compiler_params=pltpu.CompilerParams(has_side_effects=True),
    )(*bufs)


def _pair_exchange(tag, entries):
    n = len(entries)

    def body(*refs):
        ins, outs = refs[:n], refs[n:2 * n]
        send, recv = refs[2 * n:]
        x, y, c, _ = _place()
        sib = (x, y, 1 - c)

        def cp(i, j):
            return pltpu.make_async_remote_copy(src_ref=ins[i].at[j, 1 - c], dst_ref=outs[i].at[j],
                                                send_sem=send.at[N_CHIPS * i + j], recv_sem=recv.at[N_CHIPS * i + j],
                                                device_id=sib, device_id_type=MESH)

        cps = [cp(i, j) for i in range(n) for j in range(N_CHIPS)]
        for d in cps:
            d.start()
        for d in cps:
            d.wait_recv()
        for d in cps:
            d.wait_send()

    return pl.pallas_call(
        body, name="grad_pair_exchange_" + tag,
        in_specs=_any_specs(n), out_specs=_any_specs(n),
        out_shape=[jax.ShapeDtypeStruct((N_CHIPS,) + e.shape[2:], e.dtype) for e in entries],
        scratch_shapes=[pltpu.SemaphoreType.DMA((N_CHIPS * n,))] * 2,
        compiler_params=pltpu.CompilerParams(has_side_effects=True),
    )(*entries)


def _pair_route(srcs, zones):
    x, y, c, _ = _place()
    return [(srcs[i].at[j, 1 - c], zones[i].at[j], (x, y, 1 - c)) for i in range(len(srcs)) for j in range(N_CHIPS)]


def _chip_route(srcs, zones):
    x, y, c, chips = _place()
    return [(srcs[i].at[2 * px + py], zones[i].at[k], (px, py, c)) for i in range(len(srcs)) for k, (px, py) in enumerate(chips)]


def _exchange_start(name, route, per_entry, srcs, zone_shapes):
    n = len(srcs)
    lands = [lax.empty(s, a.dtype) for s, a in zip(zone_shapes, srcs)]

    def body(*refs):
        ins, zones, send, recv, token = refs[:n], refs[n:2 * n], refs[2 * n], refs[2 * n + 1], refs[-1]
        for k, (src, dst, dev) in enumerate(route(ins, zones)):
            pltpu.make_async_remote_copy(src_ref=src, dst_ref=dst, send_sem=send.at[k], recv_sem=recv.at[k],
                                         device_id=dev, device_id_type=MESH).start()
        token[...] = jnp.zeros_like(token)

    tok_shape, tok_spec = _token()
    res = pl.pallas_call(
        body, name=name,
        in_specs=[HBM_SPEC] * (2 * n),
        out_specs=[SEM_SPEC, SEM_SPEC] + [HBM_SPEC] * (2 * n) + [tok_spec],
        out_shape=[pltpu.SemaphoreType.DMA((per_entry * n,))] * 2 + [pltpu.HBM(a.shape, a.dtype) for a in srcs + lands]
        + [tok_shape],
        input_output_aliases={i: 2 + i for i in range(2 * n)},
        compiler_params=pltpu.CompilerParams(has_side_effects=ORDERED_EFFECT),
    )(*[_in_hbm(a) for a in srcs + lands])
    return (res[0], res[1], list(res[2:2 + n]), list(res[2 + n:2 + 2 * n])), res[-1]


def _exchange_wait(name, route, started, after):
    send, recv, srcs, lands = started
    n = len(srcs)

    def body(*refs):
        ins, zones, send_ref, recv_ref = refs[:n], refs[n:2 * n], refs[2 * n], refs[2 * n + 1]
        for k, (src, dst, dev) in enumerate(route(ins, zones)):
            cp = pltpu.make_async_remote_copy(src_ref=src, dst_ref=dst, send_sem=send_ref.at[k], recv_sem=recv_ref.at[k],
                                              device_id=dev, device_id_type=MESH)
            cp.wait_send()
            cp.wait_recv()

    res = pl.pallas_call(
        body, name=name,
        in_specs=[HBM_SPEC] * (2 * n) + [SEM_SPEC, SEM_SPEC, pl.BlockSpec(memory_space=pl.ANY)],
        out_specs=[HBM_SPEC] * (2 * n),
        out_shape=[pltpu.HBM(a.shape, a.dtype) for a in srcs + lands],
        input_output_aliases={i: i for i in range(2 * n)},
        compiler_params=pltpu.CompilerParams(has_side_effects=ORDERED_EFFECT),
    )(*srcs, *lands, send, recv, after)
    return list(res[:n]), list(res[n:])


def _pair_share(tag, bufs):
    n = len(bufs)

    def body(*refs):
        ins, outs = refs[:n], refs[n:2 * n]
        send, recv = refs[2 * n:]
        x, y, c, _ = _place()
        sib = (x, y, 1 - c)

        def cp(i, half):
            return pltpu.make_async_remote_copy(src_ref=ins[i].at[half], dst_ref=outs[i].at[half],
                                                send_sem=send.at[i], recv_sem=recv.at[i],
                                                device_id=sib, device_id_type=MESH)

        cps = [cp(i, c) for i in range(n)]
        for d in cps:
            d.start()
        for i in range(n):
            cp(i, 1 - c).wait_recv()
        for d in cps:
            d.wait_send()

    return pl.pallas_call(
        body, name="grad_pair_share_" + tag,
        in_specs=_any_specs(n), out_specs=_any_specs(n),
        out_shape=[jax.ShapeDtypeStruct(b.shape, b.dtype) for b in bufs],
        scratch_shapes=[pltpu.SemaphoreType.DMA((n,))] * 2,
        input_output_aliases={i: i for i in range(n)},
        compiler_params=pltpu.CompilerParams(has_side_effects=True),
    )(*bufs)


def _gather_all_devices(v):
    def body(v_ref, o_ref, send, recv, loc):
        x, y, c, _ = _place()
        me = 4 * x + 2 * y + c
        own = pltpu.make_async_copy(v_ref, o_ref.at[me], loc)
        own.start()
        rels = [(fx, fy, fc) for fx in (0, 1) for fy in (0, 1) for fc in (0, 1)][1:]

        def peer(fx, fy, fc):
            return (x + fx - 2 * x * fx, y + fy - 2 * y * fy, c + fc - 2 * c * fc)

        def cp(r, slot, dev):
            return pltpu.make_async_remote_copy(src_ref=v_ref, dst_ref=o_ref.at[slot], send_sem=send.at[r],
                                                recv_sem=recv.at[r], device_id=dev, device_id_type=MESH)

        cps = [cp(r, me, peer(*f)) for r, f in enumerate(rels)]
        for d in cps:
            d.start()
        for r, f in enumerate(rels):
            px, py, pc = peer(*f)
            cp(r, 4 * px + 2 * py + pc, (px, py, pc)).wait_recv()
        for d in cps:
            d.wait_send()
        own.wait()

    return pl.pallas_call(
        body, name="gather_small_grads",
        in_specs=_any_specs(1), out_specs=_any_specs(1)[0],
        out_shape=jax.ShapeDtypeStruct((8,) + v.shape, v.dtype),
        scratch_shapes=[pltpu.SemaphoreType.DMA((7,)), pltpu.SemaphoreType.DMA((7,)), pltpu.SemaphoreType.DMA],
        compiler_params=pltpu.CompilerParams(has_side_effects=True),
    )(v)


def _row_tile(rows, cols, itemsize=4, budget=2 * 1024 * 1024):
    best = None
    for t in range(8, rows + 1, 8):
        if rows % t == 0 and t * cols * itemsize <= budget:
            best = t
    return best if best is not None else rows


def _my_chip():
    return 2 * lax.axis_index("x") + lax.axis_index("y")


def _pair_sum(g5, gsib):
    _, _, rh, cols = g5.shape
    tr = _row_tile(rh, cols)

    def body(a_ref, b_ref, o_ref):
        o_ref[...] = (a_ref[...].astype(F32) + b_ref[...].astype(F32)).astype(o_ref.dtype)

    return pl.pallas_call(body, name="grad_pair_sum", grid=(N_CHIPS, rh // tr),
                          in_specs=[pl.BlockSpec((None, None, tr, cols), lambda j, r: (j, lax.axis_index("c"), r, 0)),
                                    pl.BlockSpec((None, tr, cols), lambda j, r: (j, r, 0))],
                          out_specs=pl.BlockSpec((None, tr, cols), lambda j, r: (j, r, 0)),
                          out_shape=jax.ShapeDtypeStruct((N_CHIPS, rh, cols), BF16),
                          compiler_params=_params(("parallel", "parallel")))(g5, gsib)


def _chip_sum(part, recv):
    _, rh, cols = part.shape
    tr = _row_tile(rh, cols)

    def body(a_ref, b_ref, o_ref):
        acc = a_ref[...].astype(F32)
        for k in range(3):
            acc = acc + b_ref[k].astype(F32)
        o_ref[...] = acc

    return pl.pallas_call(body, name="grad_chip_sum", grid=(rh // tr,),
                          in_specs=[pl.BlockSpec((None, tr, cols), lambda r: (_my_chip(), r, 0)),
                                    pl.BlockSpec((3, tr, cols), lambda r: (0, r, 0))],
                          out_specs=pl.BlockSpec((None, tr, cols), lambda r: (lax.axis_index("c"), r, 0)),
                          out_shape=jax.ShapeDtypeStruct((2, rh, cols), F32),
                          compiler_params=_params(("parallel",)))(part, recv)


def _sum_devices(g):
    _, rows, cols = g.shape
    tr = _row_tile(rows, cols, budget=256 * 1024)

    def body(g_ref, o_ref):
        acc = g_ref[0]
        for d in range(1, 8):
            acc = acc + g_ref[d]
        o_ref[...] = acc

    return pl.pallas_call(body, name="sum_small_grads", grid=(rows // tr,),
                          in_specs=[pl.BlockSpec((8, tr, cols), lambda r: (0, r, 0))],
                          out_specs=pl.BlockSpec((tr, cols), lambda r: (r, 0)),
                          out_shape=jax.ShapeDtypeStruct((rows, cols), F32),
                          compiler_params=_params(("parallel",)))(g)


def _place_shard(w, layer, dtype):
    _, rows, cols = w.shape
    tr = _row_tile(rows, cols)

    def body(i_ref, o_ref):
        o_ref[...] = i_ref[...].astype(o_ref.dtype)

    out = pl.pallas_call(body, name="place_shard", grid=(rows // tr,),
                         in_specs=[pl.BlockSpec((None, tr, cols), lambda r: (layer, r, 0))],
                         out_specs=pl.BlockSpec((None, tr, cols), lambda r: (_my_chip(), r, 0)),
                         out_shape=jax.ShapeDtypeStruct((N_CHIPS, rows, cols), dtype),
                         compiler_params=_params(("parallel",)))(w)
    return out.reshape(N_CHIPS, 2, rows // 2, cols)


def _adamw(w, gs, m, v):
    n_layers, rows, cols = w.shape
    tr = _row_tile(rows, cols, budget=1024 * 1024)

    def body(w_ref, m_ref, v_ref, *rest):
        g_refs = rest[:n_layers]
        go_ref, d_ref, mo_ref, vo_ref = rest[n_layers:]
        gv = g_refs[0][...]
        for layer in range(1, n_layers):
            gv = jnp.where(pl.program_id(0) == layer, g_refs[layer][...], gv)
        mn = ADAM_B1 * m_ref[...] + (1.0 - ADAM_B1) * gv
        vn = ADAM_B2 * v_ref[...] + (1.0 - ADAM_B2) * jnp.square(gv)
        m_hat = mn / (1.0 - ADAM_B1 ** ADAM_STEP)
        v_hat = vn / (1.0 - ADAM_B2 ** ADAM_STEP)
        d_ref[...] = -ADAM_LR * (m_hat / (jnp.sqrt(v_hat) + ADAM_EPS) + ADAM_WD * w_ref[...])
        go_ref[...] = gv
        mo_ref[...] = mn
        vo_ref[...] = vn

    spec = pl.BlockSpec((None, tr, cols), lambda layer, r: (layer, r, 0))
    g_specs = [pl.BlockSpec((tr, cols), lambda layer, r, own=own: (jnp.where(layer == own, r, 0), 0))
               for own in range(n_layers)]
    return pl.pallas_call(body, name="adamw", grid=(n_layers, rows // tr), in_specs=[spec] * 3 + g_specs,
                          out_specs=[spec] * 4, out_shape=[jax.ShapeDtypeStruct((n_layers, rows, cols), F32)] * 4,
                          compiler_params=_params(("parallel", "parallel")))(w, m, v, *gs)


def _pad_rope(w):
    z = jnp.zeros(w.shape[:-1] + (ROPE_HALF,), w.dtype)
    return jnp.concatenate([w[..., :ROPE_HALF], z, w[..., ROPE_HALF:], z], axis=-1)


def _unpad_rope(g):
    return jnp.concatenate([g[..., :ROPE_HALF], g[..., ROPE:ROPE + ROPE_HALF]], axis=-1)


def _unstack_cols(s):
    n, r, cs = s.shape
    return jnp.transpose(s, (1, 0, 2)).reshape(r, n * cs)


def _stack_cols(f):
    r, cfull = f.shape
    return jnp.transpose(f.reshape(r, N_CHIPS, cfull // N_CHIPS), (1, 0, 2))


def _small_shard(norm, conv):
    return jnp.concatenate([jnp.pad(norm, ((0, 15), (0, 0))), jnp.pad(conv, ((0, 13), (0, 0)))], axis=0)


def _flat_rows(a):
    return a.reshape(-1, LANES)


def _pack_small(arrs):
    return jnp.concatenate([_flat_rows(a.astype(F32)) for a in arrs], axis=0)


def _unpack_small(flat, like):
    out, r = [], 0
    for a in like:
        n = a.size // LANES
        out.append(flat[r:r + n].reshape(a.shape))
        r += n
    return out


def kernel(x, positions, e_norm_mix, e_w_in, e_q_norm, e_w_uq, e_kv_norm, e_w_ukv, e_v_norm, e_sgu_w, e_sgu_b, e_mla_out_norm, e_sgu_out_norm, e_w_out, o_norm_mix, o_w_in, o_conv_w, o_w_out, mlp_norm, mlp_w1, mlp_w2, final_norm, loss_target, m_e_norm_mix, m_e_w_in, m_e_q_norm, m_e_w_uq, m_e_kv_norm, m_e_w_ukv, m_e_v_norm, m_e_sgu_w, m_e_sgu_b, m_e_mla_out_norm, m_e_sgu_out_norm, m_e_w_out, m_o_norm_mix, m_o_w_in, m_o_conv_w, m_o_w_out, m_mlp_norm, m_mlp_w1, m_mlp_w2, m_final_norm, v_e_norm_mix, v_e_w_in, v_e_q_norm, v_e_w_uq, v_e_kv_norm, v_e_w_ukv, v_e_v_norm, v_e_sgu_w, v_e_sgu_b, v_e_mla_out_norm, v_e_sgu_out_norm, v_e_w_out, v_o_norm_mix, v_o_w_in, v_o_conv_w, v_o_w_out, v_mlp_norm, v_mlp_w1, v_mlp_w2, v_final_norm):
    t, d = x.shape[1], x.shape[2]
    ql, kvl = e_q_norm.shape[1], e_kv_norm.shape[1]
    groups = e_v_norm.shape[1]
    gw = groups * LANES
    heads = N_CHIPS * e_w_uq.shape[2] // (LANES + ROPE)
    hw = heads * LANES
    mix = hw + gw
    ei = N_CHIPS * e_w_in.shape[2]
    cd = N_CHIPS * o_conv_w.shape[2]
    ff = N_CHIPS * mlp_w1.shape[2]
    ffs = ff // N_CHIPS
    pi = 2 * gw + ql + kvl + LANES
    assert e_norm_mix.shape[0] == 1 and o_norm_mix.shape[0] == 1 and mlp_norm.shape[0] == 2
    assert ei == ql + kvl + ROPE + 2 * gw and cd == d and e_sgu_w.shape[2] == LANES
    assert (2 * gw) % ql == 0 and (2 * gw + ql) % kvl == 0 and t % LANES == 0
    scale = (LANES + ROPE) ** -0.5

    tr = min(256, t)
    tm = _pick(t, 1024, 8)
    kt, kd = _pick(t, 2048, 8), _pick(d, 2048)
    xs = x.reshape(t, d)
    tgt = loss_target.reshape(t, d)

    small_shard = _small_shard(o_norm_mix, o_conv_w[0])
    layer_groups = [
        [_place_shard(e_w_in, 0, BF16)],
        [_place_shard(e_w_uq, 0, BF16), _place_shard(e_w_ukv, 0, BF16), _place_shard(e_w_out, 0, BF16),
         _place_shard(small_shard[None], 0, F32)],
        [_place_shard(mlp_w1, 0, BF16)], [_place_shard(mlp_w2, 0, BF16)],
        [_place_shard(o_w_in, 0, BF16), _place_shard(o_w_out, 0, BF16)],
        [_place_shard(mlp_w1, 1, BF16)], [_place_shard(mlp_w2, 1, BF16)]]
    started, gather_token = _gather_start(layer_groups)

    def gathered(gi, tag, after):
        send, recv, bufs = started[gi]
        bufs = _gather_forward(tag, _gather_wait(tag, send, recv, bufs, after))
        return [b.reshape(N_CHIPS, 2 * b.shape[2], b.shape[3]) for b in bufs]

    w_in_g, = gathered(0, "e_in", gather_token)
    full = _unstack_cols(w_in_g)
    c2, c3 = ql + kvl, ql + kvl + ROPE
    w_in_all = jnp.concatenate([full[:, c3:], full[:, :c2], _pad_rope(full[:, c2:c3])], axis=1)

    g_e = e_norm_mix
    h0 = _norm_fwd("e_norm", xs, g_e, tr)
    proj, = _matmul("e_proj", Mat(h0, t, d), Mat(w_in_all, d, pi), "nn", [_out(t, pi, F32)], tm, _pick(pi, 1024), kd)

    w_uq_g, w_ukv_g, w_eout_g, small_g = gathered(1, "e", proj)
    full = _unstack_cols(w_uq_g).reshape(ql, heads, LANES + ROPE)
    w_q_all = jnp.concatenate([full[:, :, :LANES].reshape(ql, hw), _pad_rope(full[:, :, LANES:]).reshape(ql, hw)], axis=1)
    full = _unstack_cols(w_ukv_g).reshape(kvl, heads, 2 * LANES)
    w_kv_all = jnp.concatenate([full[:, :, :LANES].reshape(kvl, hw), full[:, :, LANES:].reshape(kvl, hw)], axis=1)
    w_eout = w_eout_g.reshape(mix, d)
    g_o = small_g[:, 0].reshape(1, d)
    conv_w = jnp.pad(jnp.transpose(small_g[:, 16:19], (1, 0, 2)).reshape(3, cd), ((0, 5), (0, 0)))

    g_q, g_kv = e_q_norm, e_kv_norm
    g_vn = e_v_norm.reshape(1, gw)
    sgu_w = e_sgu_w[0]
    sgu_b = jnp.broadcast_to(e_sgu_b[0][:, :, None], (groups, LANES, LANES))
    g_mla, g_sgu = e_mla_out_norm, e_sgu_out_norm
    g_m0, g_m1 = mlp_norm[0:1], mlp_norm[1:2]
    g_f = final_norm.reshape(1, d)

    inv_freq = ROPE_BASE ** (-jnp.arange(0, ROPE, 2, dtype=F32) / ROPE)
    zeros32 = jnp.zeros((ROPE_HALF,), F32)
    ones32 = jnp.ones((ROPE_HALF,), F32)
    invf = jnp.concatenate([inv_freq, zeros32, inv_freq, zeros32]).reshape(1, LANES)
    cmask = jnp.concatenate([ones32, zeros32, ones32, zeros32]).reshape(1, LANES)
    smask = jnp.concatenate([-ones32, zeros32, ones32, zeros32]).reshape(1, LANES)
    ctab, stab = _rope_tables(positions.reshape(t, 1).astype(F32), invf, cmask, smask, tr)

    def mlp_fwd(tag, xin, g, gi):
        hm = _norm_fwd("mlp_norm_" + tag, xin, g, tr)
        tn = _pick(ffs, 1024)
        w1 = Mat(gathered(gi, "w1_" + tag, hm)[0], d, ff, "colstack")
        a, act = _matmul("mlp_up_" + tag, Mat(hm, t, d), w1, "nn",
                         [_out(t, ff, BF16), _out(t, ff, BF16)], tm, tn, kd,
                         epilogue=lambda z: (jnp.maximum(z, 0.0), jnp.square(jnp.maximum(z, 0.0))))
        w2 = Mat(gathered(gi + 1, "w2_" + tag, act)[0].reshape(ff, d), ff, d)
        xo, = _matmul("mlp_down_" + tag, Mat(act, t, ff), w2, "nn",
                      [_out(t, d, F32)], tm, _pick(d, 1024), _pick(ffs, 2048),
                      epilogue=lambda z, r: (z + r,), extras=[Mat(xin, t, d)])
        return xo, hm, a, act, w1, w2

    def chip_start(tag, part):
        return _exchange_start("scatter_start_" + tag, _chip_route, 3, part, [(3,) + p.shape[1:] for p in part])

    def pair_start(tag, stacked):
        g5 = [g.reshape(N_CHIPS, 2, g.shape[1] // 2, g.shape[2]) for g in stacked]
        return _exchange_start("pair_start_" + tag, _pair_route, N_CHIPS, g5, [(N_CHIPS,) + g.shape[2:] for g in g5])

    def pair_finish(tag, started, after):
        g5, from_sib = _exchange_wait("pair_wait_" + tag, _pair_route, started, after)
        return chip_start(tag, [_pair_sum(a, b) for a, b in zip(g5, from_sib)])

    def mlp_bwd(tag, dx, dxb, xin, g, w1, w2, hm, a, act, deps, extra_grads=()):
        tn = _pick(ffs, 1024)
        dz, = _matmul("mlp_dact_" + tag, Mat(dxb, t, d), w2, "nt",
                      [_out(t, ff, BF16)], tm, tn, kd,
                      epilogue=lambda z, av: (z * (2.0 * av.astype(F32)),), extras=[Mat(a, t, ff)], deps=deps)
        dw2, = _matmul("mlp_dw2_" + tag, Mat(act, t, ff), Mat(dxb, t, d), "tn",
                       [_out(ff, d, BF16)], tn, _pick(d, 1024), kt)
        dw1, = _matmul("mlp_dw1_" + tag, Mat(hm, t, d), Mat(dz, t, ff), "tn",
                       [_out(d, ff, BF16, "colstack", (), (N_CHIPS, d, ffs))], _pick(d, 1024), tn, kt)
        started, tok = pair_start("m" + tag, [dw1, dw2.reshape(N_CHIPS, ffs, d), *extra_grads])
        dhm, = _matmul("mlp_dh_" + tag, Mat(dz, t, ff), w1, "nt",
                       [_out(t, d, F32)], tm, _pick(d, 1024), _pick(ffs, 2048), deps=(tok,))
        dxo, dxob, dg = _norm_bwd("mlp_norm_bwd_" + tag, dhm, xin, g, dx, tr)
        sc, tok = pair_finish("m" + tag, started, dxo)
        return dxo, dxob, dg, sc, tok

    cq_cb, ckv_cb, kr_cb = 2 * gw // ql, (2 * gw + ql) // kvl, (2 * gw + ql + kvl) // LANES
    qn, kvn = _rowwise("qkv_norm", lambda a, b, ga, gb: (_rms(a, ga), _rms(b, gb)), t // tr,
                       [_rt(proj, tr, ql, cq_cb), _rt(proj, tr, kvl, ckv_cb), _whole(g_q), _whole(g_kv)],
                       [_rt_out(t, ql, BF16, tr), _rt_out(t, kvl, BF16, tr)])
    qfull, = _matmul("q_up", Mat(qn, t, ql), Mat(w_q_all, ql, 2 * hw), "nn", [_out(t, 2 * hw, F32)], tm, _pick(2 * hw, 1024), ql)
    kvall, = _matmul("kv_up", Mat(kvn, t, kvl), Mat(w_kv_all, kvl, 2 * hw), "nn", [_out(t, 2 * hw, BF16)], tm, _pick(2 * hw, 1024), kvl)
    qall, kr = _rope_fwd(qfull, proj, kr_cb, ctab, stab, heads, tr)
    att, lse, lse_row = _attn_fwd(qall, kvall, kr, heads, scale, tr)
    rb = min(2 * LANES, t)
    sgu = _sgu_fwd(proj, g_vn, sgu_w, sgu_b, groups, rb)
    mixed = _rowwise("mix_norm", lambda a, s, ga, gs: jnp.concatenate([_rms(a, ga), _rms(s, gs)], axis=1), t // tr,
                     [_rt(att, tr), _rt(sgu, tr), _whole(g_mla), _whole(g_sgu)], [_rt_out(t, mix, BF16, tr)])[0]
    x1, = _matmul("e_out", Mat(mixed, t, mix), Mat(w_eout, mix, d), "nn", [_out(t, d, F32)], tm, _pick(d, 1024), _pick(mix, 2048),
                  epilogue=lambda z, r: (z + r,), extras=[Mat(xs, t, d)])
    x2, hm0, a0, act0, w1_0, w2_0 = mlp_fwd("0", x1, g_m0, 2)

    w_oin_g, w_oout_g = gathered(4, "o", x2)
    w_oout = w_oout_g.reshape(cd, d)
    h1 = _norm_fwd("o_norm", x2, g_o, tr)
    oin = Mat(w_oin_g, d, 3 * cd, "colstack")
    tn_o = _pick(_gcd(3 * cd // N_CHIPS, cd), 512)
    proj3, = _matmul("o_proj", Mat(h1, t, d), oin, "nn", [_out(t, 3 * cd, F32, "colstack", (), (3, t, cd))], tm, tn_o, kd)
    tc = _pick(cd, 256)
    bz = _conv_fwd(proj3, conv_w, tc)
    x3, = _matmul("o_out", Mat(bz, t, cd), Mat(w_oout, cd, d), "nn", [_out(t, d, F32)], tm, _pick(d, 1024), _pick(cd, 2048),
                  epilogue=lambda z, r: (z + r,), extras=[Mat(x2, t, d)])
    x4, hm1, a1, act1, w1_1, w2_1 = mlp_fwd("1", x3, g_m1, 5)

    def final_fn(xv, gv, tv):
        r = lax.rsqrt(jnp.mean(xv * xv, axis=-1, keepdims=True) + EPS)
        xh = xv * r
        err = xh * gv - tv
        dy = err * (1.0 / d)
        dxh = dy * gv
        dx = r * (dxh - xh * jnp.mean(dxh * xh, axis=-1, keepdims=True))
        sq = jnp.sum(err * err, axis=0, keepdims=True)
        part = sq[:, :LANES]
        for k in range(1, d // LANES):
            part = part + sq[:, k * LANES:(k + 1) * LANES]
        return dx, dx, part, jnp.sum(dy * xh, axis=0, keepdims=True)

    dx4, dx4b, loss_vec, dg_f = _rowwise("loss_final_norm", final_fn, t // tr, [_rt(x4, tr), _whole(g_f), _rt(tgt, tr)],
                                         [_rt_out(t, d, F32, tr), _rt_out(t, d, BF16, tr)],
                                         [jax.ShapeDtypeStruct((1, LANES), F32), jax.ShapeDtypeStruct((1, d), F32)])
    loss = lax.psum(0.5 * jnp.sum(loss_vec) / d, ("x", "y", "c"))

    dx3, dx3b, dg_m1, sc_m1, tok = mlp_bwd("1", dx4, dx4b, x3, g_m1, w1_1, w2_1, hm1, a1, act1, ())

    dbz, = _matmul("o_out_dx", Mat(dx3b, t, d), Mat(w_oout, cd, d), "nt", [_out(t, cd, F32)], tm, _pick(cd, 1024), kd,
                   deps=(tok,))
    dw_oout, = _matmul("o_out_dw", Mat(bz, t, cd), Mat(dx3b, t, d), "tn", [_out(cd, d, BF16)], _pick(cd, 1024), _pick(d, 1024), kt)
    dproj3, dconv = _conv_bwd(proj3, conv_w, dbz, tc)
    dp3 = Mat(dproj3, t, 3 * cd, "colstack")
    dw_oin, = _matmul("o_proj_dw", Mat(h1, t, d), dp3, "tn", [_out(d, 3 * cd, BF16, "colstack", (), (N_CHIPS, d, 3 * cd // N_CHIPS))],
                      _pick(d, 1024), tn_o, kt)
    started_o, tok = pair_start("o", [dw_oin, dw_oout.reshape(N_CHIPS, cd // N_CHIPS, d)])
    dh1, = _matmul("o_proj_dx", dp3, oin, "nt", [_out(t, d, F32)], tm, _pick(d, 1024), tn_o, deps=(tok,))
    dx2, dx2b, dg_o = _norm_bwd("o_norm_bwd", dh1, x2, g_o, dx3, tr)
    sc_o, tok = pair_finish("o", started_o, dx2)

    dconv_s = jnp.transpose(dconv[:3].reshape(3, N_CHIPS, cd // N_CHIPS), (1, 0, 2))
    gsmall = jnp.concatenate([jnp.pad(dg_o.reshape(N_CHIPS, 1, d // N_CHIPS), ((0, 0), (0, 15), (0, 0))),
                              jnp.pad(dconv_s, ((0, 0), (0, 13), (0, 0)))], axis=1)
    dx1, dx1b, dg_m0, sc_m0, tok = mlp_bwd("0", dx2, dx2b, x1, g_m0, w1_0, w2_0, hm0, a0, act0, (tok,), (gsmall,))

    dmixed, = _matmul("e_out_dx", Mat(dx1b, t, d), Mat(w_eout, mix, d), "nt", [_out(t, mix, F32)], tm, _pick(mix, 1024), kd,
                      deps=(tok,))
    dw_eout, = _matmul("e_out_dw", Mat(mixed, t, mix), Mat(dx1b, t, d), "tn", [_out(mix, d, BF16)], _pick(mix, 1024), _pick(d, 1024), kt)

    def mixb_fn(dm, a, s, ga, gs):
        da, dga = _rms_bwd(dm[:, :hw], a, ga)
        dsg, dgs = _rms_bwd(dm[:, hw:], s, gs)
        prod = da * a
        cols = [jnp.broadcast_to(jnp.sum(prod[:, h * LANES:(h + 1) * LANES], axis=-1, keepdims=True), (tr, LANES))
                for h in range(heads)]
        return da, dsg, jnp.stack(cols, axis=0), jnp.stack([_row_of(c) for c in cols], axis=0), dga, dgs

    da_b, dsgu, delta, delta_row, dg_mla, dg_sgu = _rowwise(
        "mix_norm_bwd", mixb_fn, t // tr, [_rt(dmixed, tr), _rt(att, tr), _rt(sgu, tr), _whole(g_mla), _whole(g_sgu)],
        [_rt_out(t, hw, BF16, tr), _rt_out(t, gw, F32, tr),
         (jax.ShapeDtypeStruct((heads, t, LANES), F32), pl.BlockSpec((heads, tr, LANES), lambda i: (0, i, 0))),
         (jax.ShapeDtypeStruct((heads, 8, t), F32), pl.BlockSpec((heads, 8, tr), lambda i: (0, 0, i)))],
        [jax.ShapeDtypeStruct((1, hw), F32), jax.ShapeDtypeStruct((1, gw), F32)])

    du, dv, dsgu_w, dsgu_b8, dg_vn = _sgu_bwd(proj, dsgu, g_vn, sgu_w, sgu_b, groups, rb)
    dq1, dq2 = _attn_dq(qall, kvall, kr, da_b, lse, delta, heads, scale, tr)
    dk1, dvv, dkr_h = _attn_dkv(qall, kvall, kr, da_b, lse_row, delta_row, heads, scale, tr)
    dqfull, dkr = _rope_bwd(dq1, dq2, dkr_h, ctab, stab, heads, tr)
    dkvall = jnp.concatenate([dk1, dvv], axis=1)
    dw_q, = _matmul("q_up_dw", Mat(qn, t, ql), Mat(dqfull, t, 2 * hw), "tn", [_out(ql, 2 * hw, BF16)], ql, _pick(2 * hw, 1024), kt)
    dqn, = _matmul("q_up_dx", Mat(dqfull, t, 2 * hw), Mat(w_q_all, ql, 2 * hw), "nt", [_out(t, ql, F32)], tm, ql, _pick(2 * hw, 2048))
    dw_kv, = _matmul("kv_up_dw", Mat(kvn, t, kvl), Mat(dkvall, t, 2 * hw), "tn", [_out(kvl, 2 * hw, BF16)], kvl, _pick(2 * hw, 1024), kt)
    dkvn, = _matmul("kv_up_dx", Mat(dkvall, t, 2 * hw), Mat(w_kv_all, kvl, 2 * hw), "nt", [_out(t, kvl, F32)], tm, kvl, _pick(2 * hw, 2048))

    def qkvb_fn(da, db, a, b, ga, gb):
        dxa, dga = _rms_bwd(da, a, ga)
        dxb, dgb = _rms_bwd(db, b, gb)
        return dxa, dxb, dga, dgb

    dcq, dckv, dg_q, dg_kv = _rowwise(
        "qkv_norm_bwd", qkvb_fn, t // tr,
        [_rt(dqn, tr), _rt(dkvn, tr), _rt(proj, tr, ql, cq_cb), _rt(proj, tr, kvl, ckv_cb), _whole(g_q), _whole(g_kv)],
        [_rt_out(t, ql, BF16, tr), _rt_out(t, kvl, BF16, tr)],
        [jax.ShapeDtypeStruct((1, ql), F32), jax.ShapeDtypeStruct((1, kvl), F32)])
    dproj = jnp.concatenate([du, dv, dcq, dckv, dkr], axis=1)
    dw_in, = _matmul("e_proj_dw", Mat(h0, t, d), Mat(dproj, t, pi), "tn", [_out(d, pi, BF16)], _pick(d, 1024), _pick(pi, 1024), kt)
    dh0, = _matmul("e_proj_dx", Mat(dproj, t, pi), Mat(w_in_all, d, pi), "nt", [_out(t, d, F32)], tm, _pick(d, 1024), _pick(pi, 4096))
    dx0, _, dg_e = _norm_bwd("e_norm_bwd", dh0, xs, g_e, dx1, tr)

    gfull = jnp.concatenate([dw_in[:, 2 * gw:2 * gw + c2], _unpad_rope(dw_in[:, 2 * gw + c2:]), dw_in[:, :2 * gw]], axis=1)
    gw_in = _stack_cols(gfull)
    gq = jnp.concatenate([dw_q[:, :hw].reshape(ql, heads, LANES), _unpad_rope(dw_q[:, hw:].reshape(ql, heads, LANES))], axis=-1)
    gw_uq = _stack_cols(gq.reshape(ql, heads * (LANES + ROPE)))
    gkv = jnp.concatenate([dw_kv[:, :hw].reshape(kvl, heads, LANES), dw_kv[:, hw:].reshape(kvl, heads, LANES)], axis=-1)
    gw_ukv = _stack_cols(gkv.reshape(kvl, heads * 2 * LANES))
    g5_e = [g.reshape(N_CHIPS, 2, g.shape[1] // 2, g.shape[2])
            for g in (gw_in, gw_uq, gw_ukv, dw_eout.reshape(N_CHIPS, mix // N_CHIPS, d))]
    sc_e, tok = chip_start("e", [_pair_sum(a, b) for a, b in zip(g5_e, _pair_exchange("e", g5_e))])

    def reduced(tag, sc, after):
        part, lands = _exchange_wait("scatter_wait_" + tag, _chip_route, sc, after)
        half = [_chip_sum(p, r) for p, r in zip(part, lands)]
        return [r.reshape(2 * r.shape[1], r.shape[2]) for r in _pair_share(tag, half)]

    r_w1_1, r_w2_1 = reduced("m1", sc_m1, tok)
    r_oin, r_oout = reduced("o", sc_o, r_w2_1)
    r_w1_0, r_w2_0, r_small = reduced("m0", sc_m0, r_oout)
    late = {
        "o_w_in": _adamw(o_w_in, [r_oin], m_o_w_in, v_o_w_in),
        "o_w_out": _adamw(o_w_out, [r_oout], m_o_w_out, v_o_w_out),
        "mlp_w1": _adamw(mlp_w1, [r_w1_0, r_w1_1], m_mlp_w1, v_mlp_w1),
        "mlp_w2": _adamw(mlp_w2, [r_w2_0, r_w2_1], m_mlp_w2, v_mlp_w2),
    }

    small_like = [e_norm_mix, e_q_norm, e_kv_norm, e_v_norm, e_sgu_w, e_sgu_b, e_mla_out_norm, e_sgu_out_norm, mlp_norm, final_norm]
    small_grads = [dg_e, dg_q, dg_kv, dg_vn, dsgu_w, dsgu_b8[:, 0, :], dg_mla, dg_sgu, jnp.concatenate([dg_m0, dg_m1], axis=0), dg_f]
    sflat = _pack_small(small_grads)
    pad = (-sflat.shape[0]) % 8
    sflat = jnp.pad(sflat, ((0, pad), (0, 0)))
    g_small = _sum_devices(_gather_all_devices(sflat))

    def padded(arrs):
        return jnp.pad(_pack_small(arrs), ((0, pad), (0, 0)))

    s_m = [m_e_norm_mix, m_e_q_norm, m_e_kv_norm, m_e_v_norm, m_e_sgu_w, m_e_sgu_b, m_e_mla_out_norm, m_e_sgu_out_norm, m_mlp_norm, m_final_norm]
    s_v = [v_e_norm_mix, v_e_q_norm, v_e_kv_norm, v_e_v_norm, v_e_sgu_w, v_e_sgu_b, v_e_mla_out_norm, v_e_sgu_out_norm, v_mlp_norm, v_final_norm]
    s_out = [_unpack_small(o[0], small_like)
             for o in _adamw(padded(small_like)[None], [g_small], padded(s_m)[None], padded(s_v)[None])]

    sm = [o[0] for o in _adamw(small_shard[None], [r_small], _small_shard(m_o_norm_mix, m_o_conv_w[0])[None],
                               _small_shard(v_o_norm_mix, v_o_conv_w[0])[None])]

    r_in, r_uq, r_ukv, r_eout = reduced("e", sc_e, late["mlp_w2"][1])
    big = dict(late)
    big.update({
        "e_w_in": _adamw(e_w_in, [r_in], m_e_w_in, v_e_w_in),
        "e_w_uq": _adamw(e_w_uq, [r_uq], m_e_w_uq, v_e_w_uq),
        "e_w_ukv": _adamw(e_w_ukv, [r_ukv], m_e_w_ukv, v_e_w_ukv),
        "e_w_out": _adamw(e_w_out, [r_eout], m_e_w_out, v_e_w_out),
    })

    names = ["e_norm_mix", "e_w_in", "e_q_norm", "e_w_uq", "e_kv_norm", "e_w_ukv", "e_v_norm", "e_sgu_w", "e_sgu_b",
             "e_mla_out_norm", "e_sgu_out_norm", "e_w_out", "o_norm_mix", "o_w_in", "o_conv_w", "o_w_out",
             "mlp_norm", "mlp_w1", "mlp_w2", "final_norm"]
    shapes = {"e_w_in": e_w_in.shape, "e_w_uq": e_w_uq.shape, "e_w_ukv": e_w_ukv.shape, "e_w_out": e_w_out.shape,
              "o_w_in": o_w_in.shape, "o_w_out": o_w_out.shape, "mlp_w1": mlp_w1.shape, "mlp_w2": mlp_w2.shape}
    small_names = ["e_norm_mix", "e_q_norm", "e_kv_norm", "e_v_norm", "e_sgu_w", "e_sgu_b", "e_mla_out_norm",
                   "e_sgu_out_norm", "mlp_norm", "final_norm"]

    def leaf(kind, name):
        if name in big:
            return big[name][kind].reshape(shapes[name])
        if name == "o_norm_mix":
            return sm[kind][0:1]
        if name == "o_conv_w":
            return sm[kind][16:19].reshape(o_conv_w.shape)
        return s_out[kind][small_names.index(name)]

    outs = [loss, dx0.reshape(x.shape)]
    for kind in range(4):
        outs += [leaf(kind, nm) for nm in names]
    return tuple(outs)


def _gcd(a, b):
    while b:
        a, b = b, a % b
    return a
```

```python
import functools

import jax
import jax.numpy as jnp
from jax import lax
from jax.experimental import pallas as pl
from jax.experimental.pallas import tpu as pltpu

F32 = jnp.float32
BF16 = jnp.bfloat16
MESH = pl.DeviceIdType.MESH

LANES = 128
ROPE = 64
ROPE_HALF = ROPE // 2
ROPE_BASE = 10000.0
EPS = 1e-6
N_CHIPS = 4
VMEM_LIMIT = 48 * 1024 * 1024
NEG = -1e30

ADAM_LR = 0.001
ADAM_B1 = 0.9
ADAM_B2 = 0.999
ADAM_EPS = 1e-08
ADAM_WD = 0.01
ADAM_STEP = 10


def _pick(n, target, step=LANES):
    best = None
    for t in range(step, min(n, target) + 1, step):
        if n % t == 0:
            best = t
    return best if best is not None else n


def _params(sem, vmem=VMEM_LIMIT):
    return pltpu.CompilerParams(dimension_semantics=sem, vmem_limit_bytes=vmem)


class Mat:
    def __init__(self, arr, rows, cols, kind="plain", lead=(), col_off=0, shape=None, dtype=None):
        self.arr, self.rows, self.cols, self.kind, self.lead, self.col_off = arr, rows, cols, kind, tuple(lead), col_off
        self.shape = tuple(arr.shape) if arr is not None else tuple(shape)
        self.dtype = arr.dtype if arr is not None else dtype

    def sds(self):
        return jax.ShapeDtypeStruct(self.shape, self.dtype)

    def spec(self, br, bc, gridmap):
        lead, nl = self.lead, len(self.lead)
        if self.kind == "plain":
            assert self.col_off % bc == 0 and self.rows % br == 0 and self.cols % bc == 0, (self.shape, br, bc)
            off = self.col_off // bc
            block = (None,) * nl + (br, bc)

            def phys(rb, cb):
                return lead + (rb, cb + off)
        elif self.kind == "colstack":
            cs = self.shape[-1]
            assert cs % bc == 0 and self.rows % br == 0, (self.shape, br, bc)
            q = cs // bc
            block = (None,) * (nl + 1) + (br, bc)

            def phys(rb, cb):
                return (cb // q,) + lead + (rb, cb % q)
        else:
            rs = self.shape[-2]
            assert rs % br == 0 and self.cols % bc == 0, (self.shape, br, bc)
            q = rs // br
            block = (None,) * (nl + 1) + (br, bc)

            def phys(rb, cb):
                return (rb // q,) + lead + (rb % q, cb)

        return pl.BlockSpec(block, lambda *g: phys(*gridmap(*g)))


def _matmul(name, a, b, mode, outs, tm, tn, tk, epilogue=None, extras=(), deps=()):
    if mode == "nn":
        m, k, n = a.rows, a.cols, b.cols
        a_spec = a.spec(tm, tk, lambda i, j, kk: (i, kk))
        b_spec = b.spec(tk, tn, lambda i, j, kk: (kk, j))
        dims = (((1,), (0,)), ((), ()))
    elif mode == "nt":
        m, k, n = a.rows, a.cols, b.rows
        a_spec = a.spec(tm, tk, lambda i, j, kk: (i, kk))
        b_spec = b.spec(tn, tk, lambda i, j, kk: (j, kk))
        dims = (((1,), (1,)), ((), ()))
    else:
        k, m, n = a.rows, a.cols, b.cols
        a_spec = a.spec(tk, tm, lambda i, j, kk: (kk, i))
        b_spec = b.spec(tk, tn, lambda i, j, kk: (kk, j))
        dims = (((0,), (0,)), ((), ()))
    assert m % tm == 0 and n % tn == 0 and k % tk == 0, (name, m, n, k, tm, tn, tk)
    grid = (m // tm, n // tn, k // tk)
    nk = grid[2]
    n_ex, n_out, n_dep = len(extras), len(outs), len(deps)
    tile = lambda i, j, kk: (i, j)

    def finish(z, ex, out_refs):
        vals = epilogue(z, *[e[...] for e in ex]) if epilogue is not None else (z,)
        for o, v in zip(out_refs, vals):
            o[...] = v.astype(o.dtype)

    def body_single(a_ref, b_ref, *rest):
        finish(lax.dot_general(a_ref[...], b_ref[...], dims, preferred_element_type=F32),
               rest[:n_ex], rest[n_ex + n_dep:n_ex + n_dep + n_out])

    def body_acc(a_ref, b_ref, *rest):
        acc = rest[-1]
        kk = pl.program_id(2)

        @pl.when(kk == 0)
        def _():
            acc[...] = jnp.zeros_like(acc)

        acc[...] += lax.dot_general(a_ref[...], b_ref[...], dims, preferred_element_type=F32)

        @pl.when(kk == nk - 1)
        def _():
            finish(acc[...], rest[:n_ex], rest[n_ex + n_dep:n_ex + n_dep + n_out])

    res = pl.pallas_call(
        body_single if nk == 1 else body_acc, name=name, grid=grid,
        in_specs=[a_spec, b_spec] + [e.spec(tm, tn, tile) for e in extras]
        + [pl.BlockSpec(memory_space=pl.ANY) for _ in deps],
        out_specs=[o.spec(tm, tn, tile) for o in outs],
        out_shape=[o.sds() for o in outs],
        scratch_shapes=[] if nk == 1 else [pltpu.VMEM((tm, tn), F32)],
        compiler_params=_params(("parallel", "parallel", "arbitrary")),
    )(a.arr, b.arr, *[e.arr for e in extras], *deps)
    return res


def _out(rows, cols, dtype, kind="plain", lead=(), shape=None):
    return Mat(None, rows, cols, kind, lead, shape=shape if shape is not None else (rows, cols), dtype=dtype)


def _rt(arr, tr, width=None, cb=0):
    width = arr.shape[1] if width is None else width
    return arr, pl.BlockSpec((tr, width), lambda i: (i, cb))


def _whole(arr):
    nd = arr.ndim
    return arr, pl.BlockSpec(arr.shape, lambda i: (0,) * nd)


def _rowwise(name, fn, n_steps, ins, outs, accs=(), deps=()):
    n_in, n_out, n_acc, n_dep = len(ins), len(outs), len(accs), len(deps)

    def body(*refs):
        vals = fn(*[r[...] for r in refs[:n_in]])
        if not isinstance(vals, (tuple, list)):
            vals = (vals,)
        for ref, v in zip(refs[n_in + n_dep:n_in + n_dep + n_out], vals[:n_out]):
            ref[...] = v.astype(ref.dtype)
        if n_acc:
            acc_refs = refs[n_in + n_dep + n_out:]

            @pl.when(pl.program_id(0) == 0)
            def _():
                for ref in acc_refs:
                    ref[...] = jnp.zeros_like(ref)

            for ref, v in zip(acc_refs, vals[n_out:]):
                ref[...] += v

    acc_specs = [pl.BlockSpec(s.shape, lambda i, nd=len(s.shape): (0,) * nd) for s in accs]
    res = pl.pallas_call(
        body, name=name, grid=(n_steps,),
        in_specs=[s for _, s in ins] + [pl.BlockSpec(memory_space=pl.ANY) for _ in deps],
        out_specs=[s for _, s in outs] + acc_specs,
        out_shape=[o for o, _ in outs] + list(accs),
        compiler_params=_params(("arbitrary",) if n_acc else ("parallel",)),
    )(*[a for a, _ in ins], *deps)
    return res


def _rt_out(t, width, dtype, tr):
    return jax.ShapeDtypeStruct((t, width), dtype), pl.BlockSpec((tr, width), lambda i: (i, 0))


def _rms(x, g):
    r = lax.rsqrt(jnp.mean(x * x, axis=-1, keepdims=True) + EPS)
    return x * r * g


def _rms_bwd(dy, x, g):
    r = lax.rsqrt(jnp.mean(x * x, axis=-1, keepdims=True) + EPS)
    xh = x * r
    dxh = dy * g
    dx = r * (dxh - xh * jnp.mean(dxh * xh, axis=-1, keepdims=True))
    dg = jnp.sum(dy * xh, axis=0, keepdims=True)
    return dx, dg


def _gelu(x):
    k = 0.7978845608028654
    th = jnp.tanh(k * (x + 0.044715 * (x * x * x)))
    return x * (0.5 * (1.0 + th))


def _gelu_grad(x):
    k = 0.7978845608028654
    x2 = x * x
    th = jnp.tanh(k * (x + 0.044715 * (x2 * x)))
    return 0.5 * (1.0 + th) + 0.5 * x * (1.0 - th * th) * (k * (1.0 + 3.0 * 0.044715 * x2))


def _norm_fwd(name, x, g, tr):
    t, d = x.shape
    return _rowwise(name, lambda xv, gv: _rms(xv, gv), t // tr, [_rt(x, tr), _whole(g)], [_rt_out(t, d, BF16, tr)])[0]


def _norm_bwd(name, dh, x, g, dres, tr):
    t, d = x.shape

    def fn(dhv, xv, gv, drv):
        dx, dg = _rms_bwd(dhv, xv, gv)
        dx = dx + drv
        return dx, dx, dg

    return _rowwise(name, fn, t // tr, [_rt(dh, tr), _rt(x, tr), _whole(g), _rt(dres, tr)],
                    [_rt_out(t, d, F32, tr), _rt_out(t, d, BF16, tr)], [jax.ShapeDtypeStruct((1, d), F32)])


def _rope_tables(posf, invf, cmask, smask, tr):
    t = posf.shape[0]

    def fn(p, f, cm, sm):
        ang = p * f
        return jnp.cos(ang) * cm, jnp.sin(ang) * sm

    return _rowwise("rope_tables", fn, t // tr, [_rt(posf, tr), _whole(invf), _whole(cmask), _whole(smask)],
                    [_rt_out(t, LANES, F32, tr), _rt_out(t, LANES, F32, tr)])


def _rot(v, c, s):
    return v * c + pltpu.roll(v, ROPE, axis=1) * s


def _rot_bwd(dv, c, s):
    return dv * c + pltpu.roll(dv * s, ROPE, axis=1)


def _rope_fwd(qfull, proj, kr_cb, ctab, stab, heads, tr):
    t = qfull.shape[0]
    hw = heads * LANES

    def fn(q, kr, c, s):
        parts = [q[:, :hw]] + [_rot(q[:, hw + h * LANES: hw + (h + 1) * LANES], c, s) for h in range(heads)]
        return jnp.concatenate(parts, axis=1), _rot(kr, c, s)

    return _rowwise("rope_fwd", fn, t // tr, [_rt(qfull, tr), _rt(proj, tr, LANES, kr_cb), _rt(ctab, tr), _rt(stab, tr)],
                    [_rt_out(t, 2 * hw, BF16, tr), _rt_out(t, LANES, BF16, tr)])


def _rope_bwd(dq1, dq2, dkr_h, ctab, stab, heads, tr):
    t = dq1.shape[0]
    hw = heads * LANES

    def fn(a, b, dk, c, s):
        parts = [a] + [_rot_bwd(b[:, h * LANES:(h + 1) * LANES], c, s) for h in range(heads)]
        dks = dk[0]
        for h in range(1, heads):
            dks = dks + dk[h]
        return jnp.concatenate(parts, axis=1), _rot_bwd(dks, c, s)

    dk_spec = pl.BlockSpec((heads, tr, LANES), lambda i: (0, i, 0))
    return _rowwise("rope_bwd", fn, t // tr, [_rt(dq1, tr), _rt(dq2, tr), (dkr_h, dk_spec), _rt(ctab, tr), _rt(stab, tr)],
                    [_rt_out(t, 2 * hw, BF16, tr), _rt_out(t, LANES, BF16, tr)])


def _dot_nt(a, b):
    return lax.dot_general(a, b, (((1,), (1,)), ((), ())), preferred_element_type=F32)


def _dot_tn(a, b):
    return lax.dot_general(a, b, (((0,), (0,)), ((), ())), preferred_element_type=F32)


def _dot(a, b):
    return jnp.dot(a, b, preferred_element_type=F32)


def _ranges(n_blocks):
    n_var = min(4, n_blocks)
    assert n_blocks % n_var == 0
    return n_var, n_blocks // n_var


def _row_of(col):
    return col.T[:8, :]


def _attn_fwd(qall, kvall, kr, heads, scale, tq):
    t = qall.shape[0]
    nq = t // tq
    n_var, per = _ranges(nq)

    def body(qn_ref, qr_ref, kn_ref, v_ref, kr_ref, o_ref, lse_ref, lser_ref):
        i = pl.program_id(1)
        for var in range(n_var):
            kv = (var + 1) * per * tq

            @pl.when(jnp.logical_and(i >= var * per, i < (var + 1) * per))
            def _(kv=kv):
                s = (_dot_nt(qn_ref[...], kn_ref[:kv, :]) + _dot_nt(qr_ref[...], kr_ref[:kv, :])) * scale
                rows = i * tq + lax.broadcasted_iota(jnp.int32, (tq, kv), 0)
                cols = lax.broadcasted_iota(jnp.int32, (tq, kv), 1)
                s = jnp.where(cols <= rows, s, NEG)
                m = jnp.max(s, axis=-1, keepdims=True)
                p = jnp.exp(s - m)
                l = jnp.sum(p, axis=-1, keepdims=True)
                o_ref[...] = _dot(p.astype(BF16), v_ref[:kv, :]) / l
                lse = jnp.broadcast_to(m + jnp.log(l), (tq, LANES))
                lse_ref[...] = lse
                lser_ref[...] = _row_of(lse)

    return pl.pallas_call(
        body, name="attn_fwd", grid=(heads, nq),
        in_specs=[pl.BlockSpec((tq, LANES), lambda h, i: (i, h)),
                  pl.BlockSpec((tq, LANES), lambda h, i: (i, heads + h)),
                  pl.BlockSpec((t, LANES), lambda h, i: (0, h)),
                  pl.BlockSpec((t, LANES), lambda h, i: (0, heads + h)),
                  pl.BlockSpec((t, LANES), lambda h, i: (0, 0))],
        out_specs=[pl.BlockSpec((tq, LANES), lambda h, i: (i, h)),
                   pl.BlockSpec((None, tq, LANES), lambda h, i: (h, i, 0)),
                   pl.BlockSpec((None, 8, tq), lambda h, i: (h, 0, i))],
        out_shape=[jax.ShapeDtypeStruct((t, heads * LANES), F32), jax.ShapeDtypeStruct((heads, t, LANES), F32),
                   jax.ShapeDtypeStruct((heads, 8, t), F32)],
        compiler_params=_params(("parallel", "parallel")),
    )(qall, qall, kvall, kvall, kr)


def _attn_dq(qall, kvall, kr, do, lse, delta, heads, scale, tq):
    t = qall.shape[0]
    nq = t // tq
    n_var, per = _ranges(nq)

    def body(qn_ref, qr_ref, kn_ref, v_ref, kr_ref, do_ref, lse_ref, dl_ref, dq1_ref, dq2_ref):
        i = pl.program_id(1)
        for var in range(n_var):
            kv = (var + 1) * per * tq

            @pl.when(jnp.logical_and(i >= var * per, i < (var + 1) * per))
            def _(kv=kv):
                k1, k2 = kn_ref[:kv, :], kr_ref[:kv, :]
                s = (_dot_nt(qn_ref[...], k1) + _dot_nt(qr_ref[...], k2)) * scale
                rows = i * tq + lax.broadcasted_iota(jnp.int32, (tq, kv), 0)
                cols = lax.broadcasted_iota(jnp.int32, (tq, kv), 1)
                p = jnp.where(cols <= rows, jnp.exp(s - lse_ref[...][:, :1]), 0.0)
                dp = _dot_nt(do_ref[...], v_ref[:kv, :])
                ds = (p * (dp - dl_ref[...][:, :1]) * scale).astype(BF16)
                dq1_ref[...] = _dot(ds, k1)
                dq2_ref[...] = _dot(ds, k2)

    qblk = lambda off: pl.BlockSpec((tq, LANES), lambda h, i: (i, off + h))
    full = lambda off: pl.BlockSpec((t, LANES), lambda h, i: (0, off + h))
    stat = pl.BlockSpec((None, tq, LANES), lambda h, i: (h, i, 0))
    return pl.pallas_call(
        body, name="attn_dq", grid=(heads, nq),
        in_specs=[qblk(0), qblk(heads), full(0), full(heads), pl.BlockSpec((t, LANES), lambda h, i: (0, 0)),
                  qblk(0), stat, stat],
        out_specs=[qblk(0), qblk(0)],
        out_shape=[jax.ShapeDtypeStruct((t, heads * LANES), F32)] * 2,
        compiler_params=_params(("parallel", "parallel")),
    )(qall, qall, kvall, kvall, kr, do, lse, delta)


def _attn_dkv(qall, kvall, kr, do, lse_row, delta_row, heads, scale, tk):
    t = qall.shape[0]
    nk = t // tk
    n_var, per = _ranges(nk)

    def body(qn_ref, qr_ref, kn_ref, v_ref, kr_ref, do_ref, lse_ref, dl_ref, dk_ref, dv_ref, dkr_ref):
        j = pl.program_id(1)
        for var in range(n_var):
            q0 = var * per * tk
            nq = t - q0

            @pl.when(jnp.logical_and(j >= var * per, j < (var + 1) * per))
            def _(q0=q0, nq=nq):
                qn, qr, do_v = qn_ref[q0:, :], qr_ref[q0:, :], do_ref[q0:, :]
                st = (_dot_nt(kn_ref[...], qn) + _dot_nt(kr_ref[...], qr)) * scale
                keys = j * tk + lax.broadcasted_iota(jnp.int32, (tk, nq), 0)
                queries = q0 + lax.broadcasted_iota(jnp.int32, (tk, nq), 1)
                pt = jnp.where(keys <= queries, jnp.exp(st - lse_ref[0:1, q0:]), 0.0)
                dpt = _dot_nt(v_ref[...], do_v)
                dst = (pt * (dpt - dl_ref[0:1, q0:]) * scale).astype(BF16)
                dv_ref[...] = _dot(pt.astype(BF16), do_v).astype(dv_ref.dtype)
                dk_ref[...] = _dot(dst, qn).astype(dk_ref.dtype)
                dkr_ref[...] = _dot(dst, qr)

    kblk = lambda off: pl.BlockSpec((tk, LANES), lambda h, j: (j, off + h))
    full = lambda off: pl.BlockSpec((t, LANES), lambda h, j: (0, off + h))
    stat = pl.BlockSpec((None, 8, t), lambda h, j: (h, 0, 0))
    return pl.pallas_call(
        body, name="attn_dkv", grid=(heads, nk),
        in_specs=[full(0), full(heads), kblk(0), kblk(heads), pl.BlockSpec((tk, LANES), lambda h, j: (j, 0)),
                  full(0), stat, stat],
        out_specs=[kblk(0), kblk(0), pl.BlockSpec((None, tk, LANES), lambda h, j: (h, j, 0))],
        out_shape=[jax.ShapeDtypeStruct((t, heads * LANES), BF16)] * 2 + [jax.ShapeDtypeStruct((heads, t, LANES), F32)],
        compiler_params=_params(("parallel", "parallel")),
    )(qall, qall, kvall, kvall, kr, do, lse_row, delta_row)


def _tril():
    return lax.broadcasted_iota(jnp.int32, (LANES, LANES), 0) >= lax.broadcasted_iota(jnp.int32, (LANES, LANES), 1)


def _group_norm(vg):
    mu = jnp.mean(vg, axis=-1, keepdims=True)
    vc = vg - mu
    rs = lax.rsqrt(jnp.mean(vc * vc, axis=-1, keepdims=True) + EPS)
    return vc * rs, rs


def _sgu_fwd(proj, gain, w, bias, groups, rb):
    t = proj.shape[0]
    gw = groups * LANES
    cpb = rb // LANES

    def body(u_ref, v_ref, gain_ref, w_ref, b_ref, s_ref):
        tril = _tril()
        for g in range(groups):
            wt = jnp.where(tril, w_ref[g], 0.0).astype(BF16)
            cols = slice(g * LANES, (g + 1) * LANES)
            for ci in range(cpb):
                rows = slice(ci * LANES, (ci + 1) * LANES)
                ug = _gelu(u_ref[rows, cols])
                vh, _ = _group_norm(_gelu(v_ref[rows, cols]))
                vn = vh * gain_ref[:, cols]
                y = _dot(wt, vn.astype(BF16)) + b_ref[g]
                s_ref[rows, cols] = ug * y

    return pl.pallas_call(
        body, name="sgu_fwd", grid=(t // rb,),
        in_specs=[pl.BlockSpec((rb, gw), lambda i: (i, 0)), pl.BlockSpec((rb, gw), lambda i: (i, 1)),
                  pl.BlockSpec((1, gw), lambda i: (0, 0)),
                  pl.BlockSpec((groups, LANES, LANES), lambda i: (0, 0, 0)),
                  pl.BlockSpec((groups, LANES, LANES), lambda i: (0, 0, 0))],
        out_specs=pl.BlockSpec((rb, gw), lambda i: (i, 0)),
        out_shape=jax.ShapeDtypeStruct((t, gw), F32),
        compiler_params=_params(("parallel",)),
    )(proj, proj, gain, w, bias)


def _sgu_bwd(proj, ds, gain, w, bias, groups, rb):
    t = proj.shape[0]
    gw = groups * LANES
    cpb = rb // LANES
    n_steps = t // rb

    def body(u_ref, v_ref, ds_ref, gain_ref, w_ref, b_ref, du_ref, dv_ref, dw_ref, db_ref, dg_ref, dy_acc):
        step = pl.program_id(0)

        @pl.when(step == 0)
        def _():
            dw_ref[...] = jnp.zeros_like(dw_ref)
            dy_acc[...] = jnp.zeros_like(dy_acc)
            dg_ref[...] = jnp.zeros_like(dg_ref)

        tril = _tril()
        for g in range(groups):
            wt = jnp.where(tril, w_ref[g], 0.0).astype(BF16)
            cols = slice(g * LANES, (g + 1) * LANES)
            gain_g = gain_ref[:, cols]
            for ci in range(cpb):
                rows = slice(ci * LANES, (ci + 1) * LANES)
                u_raw, v_raw, ds_v = u_ref[rows, cols], v_ref[rows, cols], ds_ref[rows, cols]
                ug = _gelu(u_raw)
                vh, rs = _group_norm(_gelu(v_raw))
                vn = (vh * gain_g).astype(BF16)
                y = _dot(wt, vn) + b_ref[g]
                dy = ds_v * ug
                dyb = dy.astype(BF16)
                du_ref[rows, cols] = (ds_v * y * _gelu_grad(u_raw)).astype(du_ref.dtype)
                dy_acc[g] += dy
                dw_ref[g] += _dot_nt(dyb, vn)
                dvn = _dot_tn(wt, dyb)
                dg_ref[:, cols] += jnp.sum(dvn * vh, axis=0, keepdims=True)
                dvh = dvn * gain_g
                dvg = rs * (dvh - jnp.mean(dvh, axis=-1, keepdims=True)
                            - vh * jnp.mean(dvh * vh, axis=-1, keepdims=True))
                dv_ref[rows, cols] = (dvg * _gelu_grad(v_raw)).astype(dv_ref.dtype)

        @pl.when(step == n_steps - 1)
        def _():
            ones = jnp.ones((8, LANES), F32)
            for g in range(groups):
                dw_ref[g] = jnp.where(tril, dw_ref[g], 0.0)
                db_ref[g] = lax.dot_general(ones, dy_acc[g], (((1,), (1,)), ((), ())),
                                            precision=lax.Precision.HIGHEST, preferred_element_type=F32)

    blk = lambda cb: pl.BlockSpec((rb, gw), lambda i: (i, cb))
    whole3 = pl.BlockSpec((groups, LANES, LANES), lambda i: (0, 0, 0))
    return pl.pallas_call(
        body, name="sgu_bwd", grid=(n_steps,),
        in_specs=[blk(0), blk(1), blk(0), pl.BlockSpec((1, gw), lambda i: (0, 0)), whole3, whole3],
        out_specs=[blk(0), blk(0), whole3, pl.BlockSpec((groups, 8, LANES), lambda i: (0, 0, 0)),
                   pl.BlockSpec((1, gw), lambda i: (0, 0))],
        out_shape=[jax.ShapeDtypeStruct((t, gw), BF16), jax.ShapeDtypeStruct((t, gw), BF16),
                   jax.ShapeDtypeStruct((groups, LANES, LANES), F32), jax.ShapeDtypeStruct((groups, 8, LANES), F32),
                   jax.ShapeDtypeStruct((1, gw), F32)],
        scratch_shapes=[pltpu.VMEM((groups, LANES, LANES), F32)],
        compiler_params=_params(("arbitrary",)),
    )(proj, proj, ds, gain, w, bias)


def _shift_down(z, s):
    rows = lax.broadcasted_iota(jnp.int32, z.shape, 0)
    return jnp.where(rows >= s, pltpu.roll(z, s, axis=0), 0.0)


def _shift_up(z, s):
    n = z.shape[0]
    rows = lax.broadcasted_iota(jnp.int32, z.shape, 0)
    return jnp.where(rows < n - s, pltpu.roll(z, n - s, axis=0), 0.0)


def _conv_fwd(proj3, cw, tc):
    _, t, cd = proj3.shape

    def body(p_ref, w_ref, o_ref):
        z = p_ref[1] * p_ref[2]
        w = w_ref[...]
        zc = w[2:3] * z + w[1:2] * _shift_down(z, 1) + w[0:1] * _shift_down(z, 2)
        o_ref[...] = (p_ref[0] * zc).astype(o_ref.dtype)

    return pl.pallas_call(
        body, name="conv_fwd", grid=(cd // tc,),
        in_specs=[pl.BlockSpec((3, t, tc), lambda j: (0, 0, j)), pl.BlockSpec((8, tc), lambda j: (0, j))],
        out_specs=pl.BlockSpec((t, tc), lambda j: (0, j)),
        out_shape=jax.ShapeDtypeStruct((t, cd), BF16),
        compiler_params=_params(("parallel",)),
    )(proj3, cw)


def _conv_bwd(proj3, cw, dbz, tc):
    _, t, cd = proj3.shape

    def body(p_ref, w_ref, d_ref, o_ref, dw_ref):
        b, c, xin = p_ref[0], p_ref[1], p_ref[2]
        w = w_ref[...]
        z = c * xin
        z1, z2 = _shift_down(z, 1), _shift_down(z, 2)
        zc = w[2:3] * z + w[1:2] * z1 + w[0:1] * z2
        d = d_ref[...]
        dzc = d * b
        dz = w[2:3] * dzc + w[1:2] * _shift_up(dzc, 1) + w[0:1] * _shift_up(dzc, 2)
        o_ref[0] = (d * zc).astype(o_ref.dtype)
        o_ref[1] = (dz * xin).astype(o_ref.dtype)
        o_ref[2] = (dz * c).astype(o_ref.dtype)
        row = lax.broadcasted_iota(jnp.int32, (8, tc), 0)
        dw0 = jnp.sum(dzc * z2, axis=0, keepdims=True)
        dw1 = jnp.sum(dzc * z1, axis=0, keepdims=True)
        dw2 = jnp.sum(dzc * z, axis=0, keepdims=True)
        dw_ref[...] = jnp.where(row == 0, dw0, 0.0) + jnp.where(row == 1, dw1, 0.0) + jnp.where(row == 2, dw2, 0.0)

    return pl.pallas_call(
        body, name="conv_bwd", grid=(cd // tc,),
        in_specs=[pl.BlockSpec((3, t, tc), lambda j: (0, 0, j)), pl.BlockSpec((8, tc), lambda j: (0, j)),
                  pl.BlockSpec((t, tc), lambda j: (0, j))],
        out_specs=[pl.BlockSpec((3, t, tc), lambda j: (0, 0, j)), pl.BlockSpec((8, tc), lambda j: (0, j))],
        out_shape=[jax.ShapeDtypeStruct((3, t, cd), BF16), jax.ShapeDtypeStruct((8, cd), F32)],
        compiler_params=_params(("parallel",)),
    )(proj3, cw, dbz)


def _place():
    x, y, c = lax.axis_index("x"), lax.axis_index("y"), lax.axis_index("c")
    chips = [(1 - x, y), (x, 1 - y), (1 - x, 1 - y)]
    return x, y, c, chips


def _any_specs(n):
    return [pl.BlockSpec(memory_space=pl.ANY) for _ in range(n)]


HBM_SPEC = pl.BlockSpec(memory_space=pltpu.HBM)
SEM_SPEC = pl.BlockSpec(memory_space=pltpu.SEMAPHORE)
ORDERED_EFFECT = pltpu.SideEffectType.DATAFLOW_SIDE_EFFECTING


def _in_hbm(a):
    return pltpu.with_memory_space_constraint(a, pltpu.HBM)


def _token():
    return jax.ShapeDtypeStruct((8, LANES), F32), pl.BlockSpec(memory_space=pltpu.VMEM)


def _gather_start(name, groups):
    sizes = [len(g) for g in groups]
    flat = [b for g in groups for b in g]
    n, ng = len(flat), len(groups)

    def body(*refs):
        ins, sems, token = refs[:n], refs[n:n + 2 * ng], refs[-1]
        x, y, c, chips = _place()
        me = 2 * x + y
        i = 0
        for gi, size in enumerate(sizes):
            for j in range(size):
                blk = ins[i].at[me, c]
                for k, chip in enumerate(chips):
                    pltpu.make_async_remote_copy(src_ref=blk, dst_ref=blk, send_sem=sems[2 * gi].at[3 * j + k],
                                                 recv_sem=sems[2 * gi + 1].at[3 * j + k],
                                                 device_id=(*chip, c), device_id_type=MESH).start()
                i += 1
        token[...] = jnp.zeros_like(token)

    tok_shape, tok_spec = _token()
    res = pl.pallas_call(
        body, name=name,
        in_specs=[HBM_SPEC] * n,
        out_specs=[SEM_SPEC] * (2 * ng) + [HBM_SPEC] * n + [tok_spec],
        out_shape=[pltpu.SemaphoreType.DMA((3 * size,)) for size in sizes for _ in (0, 1)]
        + [pltpu.HBM(b.shape, b.dtype) for b in flat] + [tok_shape],
        input_output_aliases={i: 2 * ng + i for i in range(n)},
        compiler_params=pltpu.CompilerParams(has_side_effects=ORDERED_EFFECT),
    )(*[_in_hbm(b) for b in flat])
    out, i = [], 2 * ng
    for gi, size in enumerate(sizes):
        out.append((res[2 * gi], res[2 * gi + 1], list(res[i:i + size])))
        i += size
    return out, res[-1]


def _gather_wait(tag, send, recv, bufs, after):
    n = len(bufs)
    after = tuple(after) if isinstance(after, (tuple, list)) else (after,)

    def body(*refs):
        ins, send_ref, recv_ref = refs[:n], refs[n], refs[n + 1]
        x, y, c, chips = _place()
        me = 2 * x + y
        for j in range(n):
            for k, (px, py) in enumerate(chips):
                cp = pltpu.make_async_remote_copy(src_ref=ins[j].at[me, c], dst_ref=ins[j].at[2 * px + py, c],
                                                  send_sem=send_ref.at[3 * j + k], recv_sem=recv_ref.at[3 * j + k],
                                                  device_id=(px, py, c), device_id_type=MESH)
                cp.wait_send()
                cp.wait_recv()

    return pl.pallas_call(
        body, name="gather_wait_" + tag,
        in_specs=[HBM_SPEC] * n + [SEM_SPEC, SEM_SPEC] + _any_specs(len(after)),
        out_specs=[HBM_SPEC] * n,
        out_shape=[pltpu.HBM(b.shape, b.dtype) for b in bufs],
        input_output_aliases={i: i for i in range(n)},
        compiler_params=pltpu.CompilerParams(has_side_effects=ORDERED_EFFECT),
    )(*bufs, send, recv, *after)


def _gather_forward(tag, bufs):
    n = len(bufs)

    def body(*refs):
        ins, outs = refs[:n], refs[n:2 * n]
        send, recv = refs[2 * n:]
        x, y, c, chips = _place()
        sib = (x, y, 1 - c)

        def cp(i, k, slot, half):
            return pltpu.make_async_remote_copy(src_ref=ins[i].at[slot, half], dst_ref=outs[i].at[slot, half],
                                                send_sem=send.at[3 * i + k], recv_sem=recv.at[3 * i + k],
                                                device_id=sib, device_id_type=MESH)

        cps = [cp(i, k, 2 * px + py, c) for i in range(n) for k, (px, py) in enumerate(chips)]
        for d in cps:
            d.start()
        for i in range(n):
            for k, (px, py) in enumerate(chips):
                cp(i, k, 2 * px + py, 1 - c).wait_recv()
        for d in cps:
            d.wait_send()

    return pl.pallas_call(
        body, name="gather_forward_" + tag,
        in_specs=_any_specs(n), out_specs=_any_specs(n),
        out_shape=[jax.ShapeDtypeStruct(b.shape, b.dtype) for b in bufs],
        scratch_shapes=[pltpu.SemaphoreType.DMA((3 * n,))] * 2,
        input_output_aliases={i: i for i in range(n)},
        compiler_params=pltpu.CompilerParams(has_side_effects=True),
    )(*bufs)


def _pair_exchange(tag, entries):
    n = len(entries)

    def body(*refs):
        ins, outs = refs[:n], refs[n:2 * n]
        send, recv = refs[2 * n:]
        x, y, c, _ = _place()
        sib = (x, y, 1 - c)

        def cp(i, j):
            return pltpu.make_async_remote_copy(src_ref=ins[i].at[j, 1 - c], dst_ref=outs[i].at[j],
                                                send_sem=send.at[N_CHIPS * i + j], recv_sem=recv.at[N_CHIPS * i + j],
                                                device_id=sib, device_id_type=MESH)

        cps = [cp(i, j) for i in range(n) for j in range(N_CHIPS)]
        for d in cps:
            d.start()
        for d in cps:
            d.wait_recv()
        for d in cps:
            d.wait_send()

    return pl.pallas_call(
        body, name="grad_pair_exchange_" + tag,
        in_specs=_any_specs(n), out_specs=_any_specs(n),
        out_shape=[jax.ShapeDtypeStruct((N_CHIPS,) + e.shape[2:], e.dtype) for e in entries],
        scratch_shapes=[pltpu.SemaphoreType.DMA((N_CHIPS * n,))] * 2,
        compiler_params=pltpu.CompilerParams(has_side_effects=True),
    )(*entries)


def _pair_route(srcs, zones):
    x, y, c, _ = _place()
    return [(srcs[i].at[j, 1 - c], zones[i].at[j], (x, y, 1 - c)) for i in range(len(srcs)) for j in range(N_CHIPS)]


def _chip_route(srcs, zones):
    x, y, c, chips = _place()
    return [(srcs[i].at[2 * px + py], zones[i].at[k], (px, py, c)) for i in range(len(srcs)) for k, (px, py) in enumerate(chips)]


def _all_route(srcs, zones):
    x, y, c, _ = _place()
    flips = [(fx, fy, fc) for fx in (0, 1) for fy in (0, 1) for fc in (0, 1)][1:]
    return [(srcs[0], zones[0].at[4 * x + 2 * y + c], (x + fx - 2 * x * fx, y + fy - 2 * y * fy, c + fc - 2 * c * fc))
            for fx, fy, fc in flips]


def _exchange_start(name, route, per_entry, srcs, zones):
    n = len(srcs)
    lands = [lax.empty(z, a.dtype) if isinstance(z, tuple) else z for z, a in zip(zones, srcs)]

    def body(*refs):
        ins, zones, send, recv, token = refs[:n], refs[n:2 * n], refs[2 * n], refs[2 * n + 1], refs[-1]
        for k, (src, dst, dev) in enumerate(route(ins, zones)):
            pltpu.make_async_remote_copy(src_ref=src, dst_ref=dst, send_sem=send.at[k], recv_sem=recv.at[k],
                                         device_id=dev, device_id_type=MESH).start()
        token[...] = jnp.zeros_like(token)

    tok_shape, tok_spec = _token()
    res = pl.pallas_call(
        body, name=name,
        in_specs=[HBM_SPEC] * (2 * n),
        out_specs=[SEM_SPEC, SEM_SPEC] + [HBM_SPEC] * (2 * n) + [tok_spec],
        out_shape=[pltpu.SemaphoreType.DMA((per_entry * n,))] * 2 + [pltpu.HBM(a.shape, a.dtype) for a in srcs + lands]
        + [tok_shape],
        input_output_aliases={i: 2 + i for i in range(2 * n)},
        compiler_params=pltpu.CompilerParams(has_side_effects=ORDERED_EFFECT),
    )(*[_in_hbm(a) for a in srcs + lands])
    return (res[0], res[1], list(res[2:2 + n]), list(res[2 + n:2 + 2 * n])), res[-1]


def _exchange_wait(name, route, started, after):
    send, recv, srcs, lands = started
    n = len(srcs)

    def body(*refs):
        ins, zones, send_ref, recv_ref = refs[:n], refs[n:2 * n], refs[2 * n], refs[2 * n + 1]
        for k, (src, dst, dev) in enumerate(route(ins, zones)):
            cp = pltpu.make_async_remote_copy(src_ref=src, dst_ref=dst, send_sem=send_ref.at[k], recv_sem=recv_ref.at[k],
                                              device_id=dev, device_id_type=MESH)
            cp.wait_send()
            cp.wait_recv()

    res = pl.pallas_call(
        body, name=name,
        in_specs=[HBM_SPEC] * (2 * n) + [SEM_SPEC, SEM_SPEC, pl.BlockSpec(memory_space=pl.ANY)],
        out_specs=[HBM_SPEC] * (2 * n),
        out_shape=[pltpu.HBM(a.shape, a.dtype) for a in srcs + lands],
        input_output_aliases={i: i for i in range(2 * n)},
        compiler_params=pltpu.CompilerParams(has_side_effects=ORDERED_EFFECT),
    )(*srcs, *lands, send, recv, after)
    return list(res[:n]), list(res[n:])


def _pair_share(tag, bufs):
    n = len(bufs)

    def body(*refs):
        ins, outs = refs[:n], refs[n:2 * n]
        send, recv = refs[2 * n:]
        x, y, c, _ = _place()
        sib = (x, y, 1 - c)

        def cp(i, half):
            return pltpu.make_async_remote_copy(src_ref=ins[i].at[half], dst_ref=outs[i].at[half],
                                                send_sem=send.at[i], recv_sem=recv.at[i],
                                                device_id=sib, device_id_type=MESH)

        cps = [cp(i, c) for i in range(n)]
        for d in cps:
            d.start()
        for i in range(n):
            cp(i, 1 - c).wait_recv()
        for d in cps:
            d.wait_send()

    return pl.pallas_call(
        body, name="grad_pair_share_" + tag,
        in_specs=_any_specs(n), out_specs=_any_specs(n),
        out_shape=[jax.ShapeDtypeStruct(b.shape, b.dtype) for b in bufs],
        scratch_shapes=[pltpu.SemaphoreType.DMA((n,))] * 2,
        input_output_aliases={i: i for i in range(n)},
        compiler_params=pltpu.CompilerParams(has_side_effects=True),
    )(*bufs)


def _spread(v):
    rows, cols = v.shape
    tr = _row_tile(rows, cols, budget=256 * 1024)

    def body(v_ref, o_ref):
        o_ref[...] = jnp.broadcast_to(v_ref[...][None], o_ref.shape)

    return pl.pallas_call(body, name="spread_small_grads", grid=(rows // tr,),
                          in_specs=[pl.BlockSpec((tr, cols), lambda r: (r, 0))],
                          out_specs=pl.BlockSpec((8, tr, cols), lambda r: (0, r, 0)),
                          out_shape=jax.ShapeDtypeStruct((8, rows, cols), v.dtype),
                          compiler_params=_params(("parallel",)))(v)


def _row_tile(rows, cols, itemsize=4, budget=2 * 1024 * 1024):
    best = None
    for t in range(8, rows + 1, 8):
        if rows % t == 0 and t * cols * itemsize <= budget:
            best = t
    return best if best is not None else rows


def _my_chip():
    return 2 * lax.axis_index("x") + lax.axis_index("y")


def _pair_sum(g5, gsib):
    _, _, rh, cols = g5.shape
    tr = _row_tile(rh, cols)

    def body(a_ref, b_ref, o_ref):
        o_ref[...] = (a_ref[...].astype(F32) + b_ref[...].astype(F32)).astype(o_ref.dtype)

    return pl.pallas_call(body, name="grad_pair_sum", grid=(N_CHIPS, rh // tr),
                          in_specs=[pl.BlockSpec((None, None, tr, cols), lambda j, r: (j, lax.axis_index("c"), r, 0)),
                                    pl.BlockSpec((None, tr, cols), lambda j, r: (j, r, 0))],
                          out_specs=pl.BlockSpec((None, tr, cols), lambda j, r: (j, r, 0)),
                          out_shape=jax.ShapeDtypeStruct((N_CHIPS, rh, cols), BF16),
                          compiler_params=_params(("parallel", "parallel")))(g5, gsib)


def _chip_sum(part, recv):
    _, rh, cols = part.shape
    tr = _row_tile(rh, cols)

    def body(a_ref, b_ref, o_ref):
        acc = a_ref[...].astype(F32)
        for k in range(3):
            acc = acc + b_ref[k].astype(F32)
        o_ref[...] = acc

    return pl.pallas_call(body, name="grad_chip_sum", grid=(rh // tr,),
                          in_specs=[pl.BlockSpec((None, tr, cols), lambda r: (_my_chip(), r, 0)),
                                    pl.BlockSpec((3, tr, cols), lambda r: (0, r, 0))],
                          out_specs=pl.BlockSpec((None, tr, cols), lambda r: (lax.axis_index("c"), r, 0)),
                          out_shape=jax.ShapeDtypeStruct((2, rh, cols), F32),
                          compiler_params=_params(("parallel",)))(part, recv)


def _sum_devices(g):
    _, rows, cols = g.shape
    tr = _row_tile(rows, cols, budget=256 * 1024)

    def body(g_ref, o_ref):
        acc = g_ref[0]
        for d in range(1, 8):
            acc = acc + g_ref[d]
        o_ref[...] = acc

    return pl.pallas_call(body, name="sum_small_grads", grid=(rows // tr,),
                          in_specs=[pl.BlockSpec((8, tr, cols), lambda r: (0, r, 0))],
                          out_specs=pl.BlockSpec((tr, cols), lambda r: (r, 0)),
                          out_shape=jax.ShapeDtypeStruct((rows, cols), F32),
                          compiler_params=_params(("parallel",)))(g)


def _place_shard(w, layer, dtype, deps=()):
    _, rows, cols = w.shape
    tr = _row_tile(rows, cols)

    def body(i_ref, *rest):
        o_ref = rest[-1]
        o_ref[...] = i_ref[...].astype(o_ref.dtype)

    out = pl.pallas_call(body, name="place_shard", grid=(rows // tr,),
                         in_specs=[pl.BlockSpec((None, tr, cols), lambda r: (layer, r, 0))] + _any_specs(len(deps)),
                         out_specs=pl.BlockSpec((None, tr, cols), lambda r: (_my_chip(), r, 0)),
                         out_shape=jax.ShapeDtypeStruct((N_CHIPS, rows, cols), dtype),
                         compiler_params=_params(("parallel",)))(w, *deps)
    return out.reshape(N_CHIPS, 2, rows // 2, cols)


def _adamw(w, gs, m, v):
    n_layers, rows, cols = w.shape
    tr = _row_tile(rows, cols, budget=1024 * 1024)

    def body(w_ref, m_ref, v_ref, *rest):
        g_refs = rest[:n_layers]
        go_ref, d_ref, mo_ref, vo_ref = rest[n_layers:]
        gv = g_refs[0][...]
        for layer in range(1, n_layers):
            gv = jnp.where(pl.program_id(0) == layer, g_refs[layer][...], gv)
        mn = ADAM_B1 * m_ref[...] + (1.0 - ADAM_B1) * gv
        vn = ADAM_B2 * v_ref[...] + (1.0 - ADAM_B2) * jnp.square(gv)
        m_hat = mn / (1.0 - ADAM_B1 ** ADAM_STEP)
        v_hat = vn / (1.0 - ADAM_B2 ** ADAM_STEP)
        d_ref[...] = -ADAM_LR * (m_hat / (jnp.sqrt(v_hat) + ADAM_EPS) + ADAM_WD * w_ref[...])
        go_ref[...] = gv
        mo_ref[...] = mn
        vo_ref[...] = vn

    spec = pl.BlockSpec((None, tr, cols), lambda layer, r: (layer, r, 0))
    g_specs = [pl.BlockSpec((tr, cols), lambda layer, r, own=own: (jnp.where(layer == own, r, 0), 0))
               for own in range(n_layers)]
    return pl.pallas_call(body, name="adamw", grid=(n_layers, rows // tr), in_specs=[spec] * 3 + g_specs,
                          out_specs=[spec] * 4, out_shape=[jax.ShapeDtypeStruct((n_layers, rows, cols), F32)] * 4,
                          compiler_params=_params(("parallel", "parallel")))(w, m, v, *gs)


def _pad_rope(w):
    z = jnp.zeros(w.shape[:-1] + (ROPE_HALF,), w.dtype)
    return jnp.concatenate([w[..., :ROPE_HALF], z, w[..., ROPE_HALF:], z], axis=-1)


def _unpad_rope(g):
    return jnp.concatenate([g[..., :ROPE_HALF], g[..., ROPE:ROPE + ROPE_HALF]], axis=-1)


def _unstack_cols(s):
    n, r, cs = s.shape
    return jnp.transpose(s, (1, 0, 2)).reshape(r, n * cs)


def _stack_cols(f):
    r, cfull = f.shape
    return jnp.transpose(f.reshape(r, N_CHIPS, cfull // N_CHIPS), (1, 0, 2))


def _small_shard(norm, conv):
    return jnp.concatenate([jnp.pad(norm, ((0, 15), (0, 0))), jnp.pad(conv, ((0, 13), (0, 0)))], axis=0)


def _flat_rows(a):
    return a.reshape(-1, LANES)


def _pack_small(arrs):
    return jnp.concatenate([_flat_rows(a.astype(F32)) for a in arrs], axis=0)


def _unpack_small(flat, like):
    out, r = [], 0
    for a in like:
        n = a.size // LANES
        out.append(flat[r:r + n].reshape(a.shape))
        r += n
    return out


def kernel(x, positions, e_norm_mix, e_w_in, e_q_norm, e_w_uq, e_kv_norm, e_w_ukv, e_v_norm, e_sgu_w, e_sgu_b, e_mla_out_norm, e_sgu_out_norm, e_w_out, o_norm_mix, o_w_in, o_conv_w, o_w_out, mlp_norm, mlp_w1, mlp_w2, final_norm, loss_target, m_e_norm_mix, m_e_w_in, m_e_q_norm, m_e_w_uq, m_e_kv_norm, m_e_w_ukv, m_e_v_norm, m_e_sgu_w, m_e_sgu_b, m_e_mla_out_norm, m_e_sgu_out_norm, m_e_w_out, m_o_norm_mix, m_o_w_in, m_o_conv_w, m_o_w_out, m_mlp_norm, m_mlp_w1, m_mlp_w2, m_final_norm, v_e_norm_mix, v_e_w_in, v_e_q_norm, v_e_w_uq, v_e_kv_norm, v_e_w_ukv, v_e_v_norm, v_e_sgu_w, v_e_sgu_b, v_e_mla_out_norm, v_e_sgu_out_norm, v_e_w_out, v_o_norm_mix, v_o_w_in, v_o_conv_w, v_o_w_out, v_mlp_norm, v_mlp_w1, v_mlp_w2, v_final_norm):
    t, d = x.shape[1], x.shape[2]
    ql, kvl = e_q_norm.shape[1], e_kv_norm.shape[1]
    groups = e_v_norm.shape[1]
    gw = groups * LANES
    heads = N_CHIPS * e_w_uq.shape[2] // (LANES + ROPE)
    hw = heads * LANES
    mix = hw + gw
    ei = N_CHIPS * e_w_in.shape[2]
    cd = N_CHIPS * o_conv_w.shape[2]
    ff = N_CHIPS * mlp_w1.shape[2]
    ffs = ff // N_CHIPS
    pi = 2 * gw + ql + kvl + LANES
    assert e_norm_mix.shape[0] == 1 and o_norm_mix.shape[0] == 1 and mlp_norm.shape[0] == 2
    assert ei == ql + kvl + ROPE + 2 * gw and cd == d and e_sgu_w.shape[2] == LANES
    assert (2 * gw) % ql == 0 and (2 * gw + ql) % kvl == 0 and t % LANES == 0
    scale = (LANES + ROPE) ** -0.5

    tr = min(256, t)
    tm = _pick(t, 1024, 8)
    kt, kd = _pick(t, 2048, 8), _pick(d, 2048)
    xs = x.reshape(t, d)
    tgt = loss_target.reshape(t, d)

    small_shard = _small_shard(o_norm_mix, o_conv_w[0])
    first, tok = _gather_start("gather_start_e", [
        [_place_shard(e_w_in, 0, BF16)],
        [_place_shard(e_w_uq, 0, BF16), _place_shard(e_w_ukv, 0, BF16), _place_shard(e_w_out, 0, BF16),
         _place_shard(small_shard[None], 0, F32)]])
    rest, tok = _gather_start("gather_start_rest", [
        [_place_shard(mlp_w1, 0, BF16, (tok,))], [_place_shard(mlp_w2, 0, BF16, (tok,))],
        [_place_shard(o_w_in, 0, BF16, (tok,)), _place_shard(o_w_out, 0, BF16, (tok,))],
        [_place_shard(mlp_w1, 1, BF16, (tok,))], [_place_shard(mlp_w2, 1, BF16, (tok,))]])
    started = first + rest

    def gathered(gi, tag, after):
        send, recv, bufs = started[gi]
        bufs = _gather_forward(tag, _gather_wait(tag, send, recv, bufs, after))
        return [b.reshape(N_CHIPS, 2 * b.shape[2], b.shape[3]) for b in bufs]

    g_e = e_norm_mix
    h0 = _norm_fwd("e_norm", xs, g_e, tr)
    inv_freq = ROPE_BASE ** (-jnp.arange(0, ROPE, 2, dtype=F32) / ROPE)
    zeros32 = jnp.zeros((ROPE_HALF,), F32)
    ones32 = jnp.ones((ROPE_HALF,), F32)
    invf = jnp.concatenate([inv_freq, zeros32, inv_freq, zeros32]).reshape(1, LANES)
    cmask = jnp.concatenate([ones32, zeros32, ones32, zeros32]).reshape(1, LANES)
    smask = jnp.concatenate([-ones32, zeros32, ones32, zeros32]).reshape(1, LANES)
    ctab, stab = _rope_tables(positions.reshape(t, 1).astype(F32), invf, cmask, smask, tr)

    w_in_g, = gathered(0, "e_in", (h0, ctab, tok))
    full = _unstack_cols(w_in_g)
    c2, c3 = ql + kvl, ql + kvl + ROPE
    w_in_all = jnp.concatenate([full[:, c3:], full[:, :c2], _pad_rope(full[:, c2:c3])], axis=1)
    proj, = _matmul("e_proj", Mat(h0, t, d), Mat(w_in_all, d, pi), "nn", [_out(t, pi, F32)], tm, _pick(pi, 1024), kd)

    w_uq_g, w_ukv_g, w_eout_g, small_g = gathered(1, "e", proj)
    full = _unstack_cols(w_uq_g).reshape(ql, heads, LANES + ROPE)
    w_q_all = jnp.concatenate([full[:, :, :LANES].reshape(ql, hw), _pad_rope(full[:, :, LANES:]).reshape(ql, hw)], axis=1)
    full = _unstack_cols(w_ukv_g).reshape(kvl, heads, 2 * LANES)
    w_kv_all = jnp.concatenate([full[:, :, :LANES].reshape(kvl, hw), full[:, :, LANES:].reshape(kvl, hw)], axis=1)
    w_eout = w_eout_g.reshape(mix, d)
    g_o = small_g[:, 0].reshape(1, d)
    conv_w = jnp.pad(jnp.transpose(small_g[:, 16:19], (1, 0, 2)).reshape(3, cd), ((0, 5), (0, 0)))

    g_q, g_kv = e_q_norm, e_kv_norm
    g_vn = e_v_norm.reshape(1, gw)
    sgu_w = e_sgu_w[0]
    sgu_b = jnp.broadcast_to(e_sgu_b[0][:, :, None], (groups, LANES, LANES))
    g_mla, g_sgu = e_mla_out_norm, e_sgu_out_norm
    g_m0, g_m1 = mlp_norm[0:1], mlp_norm[1:2]
    g_f = final_norm.reshape(1, d)

    def mlp_fwd(tag, xin, g, gi):
        hm = _norm_fwd("mlp_norm_" + tag, xin, g, tr)
        tn = _pick(ffs, 1024)
        w1 = Mat(gathered(gi, "w1_" + tag, hm)[0], d, ff, "colstack")
        a, act = _matmul("mlp_up_" + tag, Mat(hm, t, d), w1, "nn",
                         [_out(t, ff, BF16), _out(t, ff, BF16)], tm, tn, kd,
                         epilogue=lambda z: (jnp.maximum(z, 0.0), jnp.square(jnp.maximum(z, 0.0))))
        w2 = Mat(gathered(gi + 1, "w2_" + tag, act)[0].reshape(ff, d), ff, d)
        xo, = _matmul("mlp_down_" + tag, Mat(act, t, ff), w2, "nn",
                      [_out(t, d, F32)], tm, _pick(d, 1024), _pick(ffs, 2048),
                      epilogue=lambda z, r: (z + r,), extras=[Mat(xin, t, d)])
        return xo, hm, a, act, w1, w2

    def chip_start(tag, part):
        return _exchange_start("scatter_start_" + tag, _chip_route, 3, part, [(3,) + p.shape[1:] for p in part])

    def pair_start(tag, stacked):
        g5 = [g.reshape(N_CHIPS, 2, g.shape[1] // 2, g.shape[2]) for g in stacked]
        return _exchange_start("pair_start_" + tag, _pair_route, N_CHIPS, g5, [(N_CHIPS,) + g.shape[2:] for g in g5])

    def pair_finish(tag, started, after):
        g5, from_sib = _exchange_wait("pair_wait_" + tag, _pair_route, started, after)
        return chip_start(tag, [_pair_sum(a, b) for a, b in zip(g5, from_sib)])

    def mlp_bwd(tag, dx, dxb, xin, g, w1, w2, hm, a, act, deps, extra_grads=()):
        tn = _pick(ffs, 1024)
        dz, = _matmul("mlp_dact_" + tag, Mat(dxb, t, d), w2, "nt",
                      [_out(t, ff, BF16)], tm, tn, kd,
                      epilogue=lambda z, av: (z * (2.0 * av.astype(F32)),), extras=[Mat(a, t, ff)], deps=deps)
        dw2, = _matmul("mlp_dw2_" + tag, Mat(act, t, ff), Mat(dxb, t, d), "tn",
                       [_out(ff, d, BF16)], tn, _pick(d, 1024), kt)
        dw1, = _matmul("mlp_dw1_" + tag, Mat(hm, t, d), Mat(dz, t, ff), "tn",
                       [_out(d, ff, BF16, "colstack", (), (N_CHIPS, d, ffs))], _pick(d, 1024), tn, kt)
        started, tok = pair_start("m" + tag, [dw1, dw2.reshape(N_CHIPS, ffs, d), *extra_grads])
        dhm, = _matmul("mlp_dh_" + tag, Mat(dz, t, ff), w1, "nt",
                       [_out(t, d, F32)], tm, _pick(d, 1024), _pick(ffs, 2048), deps=(tok,))
        dxo, dxob, dg = _norm_bwd("mlp_norm_bwd_" + tag, dhm, xin, g, dx, tr)
        sc, tok = pair_finish("m" + tag, started, dxo)
        return dxo, dxob, dg, sc, tok

    cq_cb, ckv_cb, kr_cb = 2 * gw // ql, (2 * gw + ql) // kvl, (2 * gw + ql + kvl) // LANES
    qn, kvn = _rowwise("qkv_norm", lambda a, b, ga, gb: (_rms(a, ga), _rms(b, gb)), t // tr,
                       [_rt(proj, tr, ql, cq_cb), _rt(proj, tr, kvl, ckv_cb), _whole(g_q), _whole(g_kv)],
                       [_rt_out(t, ql, BF16, tr), _rt_out(t, kvl, BF16, tr)])
    qfull, = _matmul("q_up", Mat(qn, t, ql), Mat(w_q_all, ql, 2 * hw), "nn", [_out(t, 2 * hw, F32)], tm, _pick(2 * hw, 1024), ql)
    kvall, = _matmul("kv_up", Mat(kvn, t, kvl), Mat(w_kv_all, kvl, 2 * hw), "nn", [_out(t, 2 * hw, BF16)], tm, _pick(2 * hw, 1024), kvl)
    qall, kr = _rope_fwd(qfull, proj, kr_cb, ctab, stab, heads, tr)
    att, lse, lse_row = _attn_fwd(qall, kvall, kr, heads, scale, tr)
    rb = min(2 * LANES, t)
    sgu = _sgu_fwd(proj, g_vn, sgu_w, sgu_b, groups, rb)
    mixed = _rowwise("mix_norm", lambda a, s, ga, gs: jnp.concatenate([_rms(a, ga), _rms(s, gs)], axis=1), t // tr,
                     [_rt(att, tr), _rt(sgu, tr), _whole(g_mla), _whole(g_sgu)], [_rt_out(t, mix, BF16, tr)])[0]
    x1, = _matmul("e_out", Mat(mixed, t, mix), Mat(w_eout, mix, d), "nn", [_out(t, d, F32)], tm, _pick(d, 1024), _pick(mix, 2048),
                  epilogue=lambda z, r: (z + r,), extras=[Mat(xs, t, d)])
    x2, hm0, a0, act0, w1_0, w2_0 = mlp_fwd("0", x1, g_m0, 2)

    w_oin_g, w_oout_g = gathered(4, "o", x2)
    w_oout = w_oout_g.reshape(cd, d)
    h1 = _norm_fwd("o_norm", x2, g_o, tr)
    oin = Mat(w_oin_g, d, 3 * cd, "colstack")
    tn_o = _pick(_gcd(3 * cd // N_CHIPS, cd), 512)
    proj3, = _matmul("o_proj", Mat(h1, t, d), oin, "nn", [_out(t, 3 * cd, F32, "colstack", (), (3, t, cd))], tm, tn_o, kd)
    tc = _pick(cd, 256)
    bz = _conv_fwd(proj3, conv_w, tc)
    x3, = _matmul("o_out", Mat(bz, t, cd), Mat(w_oout, cd, d), "nn", [_out(t, d, F32)], tm, _pick(d, 1024), _pick(cd, 2048),
                  epilogue=lambda z, r: (z + r,), extras=[Mat(x2, t, d)])
    x4, hm1, a1, act1, w1_1, w2_1 = mlp_fwd("1", x3, g_m1, 5)

    def final_fn(xv, gv, tv):
        r = lax.rsqrt(jnp.mean(xv * xv, axis=-1, keepdims=True) + EPS)
        xh = xv * r
        err = xh * gv - tv
        dy = err * (1.0 / d)
        dxh = dy * gv
        dx = r * (dxh - xh * jnp.mean(dxh * xh, axis=-1, keepdims=True))
        sq = jnp.sum(err * err, axis=0, keepdims=True)
        part = sq[:, :LANES]
        for k in range(1, d // LANES):
            part = part + sq[:, k * LANES:(k + 1) * LANES]
        return dx, dx, part, jnp.sum(dy * xh, axis=0, keepdims=True)

    dx4, dx4b, loss_vec, dg_f = _rowwise("loss_final_norm", final_fn, t // tr, [_rt(x4, tr), _whole(g_f), _rt(tgt, tr)],
                                         [_rt_out(t, d, F32, tr), _rt_out(t, d, BF16, tr)],
                                         [jax.ShapeDtypeStruct((1, LANES), F32), jax.ShapeDtypeStruct((1, d), F32)])
    loss = lax.psum(0.5 * jnp.sum(loss_vec) / d, ("x", "y", "c"))

    dx3, dx3b, dg_m1, sc_m1, tok = mlp_bwd("1", dx4, dx4b, x3, g_m1, w1_1, w2_1, hm1, a1, act1, ())

    dbz, = _matmul("o_out_dx", Mat(dx3b, t, d), Mat(w_oout, cd, d), "nt", [_out(t, cd, F32)], tm, _pick(cd, 1024), kd,
                   deps=(tok,))
    dw_oout, = _matmul("o_out_dw", Mat(bz, t, cd), Mat(dx3b, t, d), "tn", [_out(cd, d, BF16)], _pick(cd, 1024), _pick(d, 1024), kt)
    dproj3, dconv = _conv_bwd(proj3, conv_w, dbz, tc)
    dp3 = Mat(dproj3, t, 3 * cd, "colstack")
    dw_oin, = _matmul("o_proj_dw", Mat(h1, t, d), dp3, "tn", [_out(d, 3 * cd, BF16, "colstack", (), (N_CHIPS, d, 3 * cd // N_CHIPS))],
                      _pick(d, 1024), tn_o, kt)
    started_o, tok = pair_start("o", [dw_oin, dw_oout.reshape(N_CHIPS, cd // N_CHIPS, d)])
    dh1, = _matmul("o_proj_dx", dp3, oin, "nt", [_out(t, d, F32)], tm, _pick(d, 1024), tn_o, deps=(tok,))
    dx2, dx2b, dg_o = _norm_bwd("o_norm_bwd", dh1, x2, g_o, dx3, tr)
    sc_o, tok = pair_finish("o", started_o, dx2)

    dconv_s = jnp.transpose(dconv[:3].reshape(3, N_CHIPS, cd // N_CHIPS), (1, 0, 2))
    gsmall = jnp.concatenate([jnp.pad(dg_o.reshape(N_CHIPS, 1, d // N_CHIPS), ((0, 0), (0, 15), (0, 0))),
                              jnp.pad(dconv_s, ((0, 0), (0, 13), (0, 0)))], axis=1)
    dx1, dx1b, dg_m0, sc_m0, tok = mlp_bwd("0", dx2, dx2b, x1, g_m0, w1_0, w2_0, hm0, a0, act0, (tok,), (gsmall,))

    dmixed, = _matmul("e_out_dx", Mat(dx1b, t, d), Mat(w_eout, mix, d), "nt", [_out(t, mix, F32)], tm, _pick(mix, 1024), kd,
                      deps=(tok,))
    dw_eout, = _matmul("e_out_dw", Mat(mixed, t, mix), Mat(dx1b, t, d), "tn", [_out(mix, d, BF16)], _pick(mix, 1024), _pick(d, 1024), kt)

    def mixb_fn(dm, a, s, ga, gs):
        da, dga = _rms_bwd(dm[:, :hw], a, ga)
        dsg, dgs = _rms_bwd(dm[:, hw:], s, gs)
        prod = da * a
        cols = [jnp.broadcast_to(jnp.sum(prod[:, h * LANES:(h + 1) * LANES], axis=-1, keepdims=True), (tr, LANES))
                for h in range(heads)]
        return da, dsg, jnp.stack(cols, axis=0), jnp.stack([_row_of(c) for c in cols], axis=0), dga, dgs

    da_b, dsgu, delta, delta_row, dg_mla, dg_sgu = _rowwise(
        "mix_norm_bwd", mixb_fn, t // tr, [_rt(dmixed, tr), _rt(att, tr), _rt(sgu, tr), _whole(g_mla), _whole(g_sgu)],
        [_rt_out(t, hw, BF16, tr), _rt_out(t, gw, F32, tr),
         (jax.ShapeDtypeStruct((heads, t, LANES), F32), pl.BlockSpec((heads, tr, LANES), lambda i: (0, i, 0))),
         (jax.ShapeDtypeStruct((heads, 8, t), F32), pl.BlockSpec((heads, 8, tr), lambda i: (0, 0, i)))],
        [jax.ShapeDtypeStruct((1, hw), F32), jax.ShapeDtypeStruct((1, gw), F32)])

    du, dv, dsgu_w, dsgu_b8, dg_vn = _sgu_bwd(proj, dsgu, g_vn, sgu_w, sgu_b, groups, rb)
    dq1, dq2 = _attn_dq(qall, kvall, kr, da_b, lse, delta, heads, scale, tr)
    dk1, dvv, dkr_h = _attn_dkv(qall, kvall, kr, da_b, lse_row, delta_row, heads, scale, tr)
    dqfull, dkr = _rope_bwd(dq1, dq2, dkr_h, ctab, stab, heads, tr)
    dkvall = jnp.concatenate([dk1, dvv], axis=1)
    dw_q, = _matmul("q_up_dw", Mat(qn, t, ql), Mat(dqfull, t, 2 * hw), "tn", [_out(ql, 2 * hw, BF16)], ql, _pick(2 * hw, 1024), kt)
    dqn, = _matmul("q_up_dx", Mat(dqfull, t, 2 * hw), Mat(w_q_all, ql, 2 * hw), "nt", [_out(t, ql, F32)], tm, ql, _pick(2 * hw, 2048))
    dw_kv, = _matmul("kv_up_dw", Mat(kvn, t, kvl), Mat(dkvall, t, 2 * hw), "tn", [_out(kvl, 2 * hw, BF16)], kvl, _pick(2 * hw, 1024), kt)
    dkvn, = _matmul("kv_up_dx", Mat(dkvall, t, 2 * hw), Mat(w_kv_all, kvl, 2 * hw), "nt", [_out(t, kvl, F32)], tm, kvl, _pick(2 * hw, 2048))

    def qkvb_fn(da, db, a, b, ga, gb):
        dxa, dga = _rms_bwd(da, a, ga)
        dxb, dgb = _rms_bwd(db, b, gb)
        return dxa, dxb, dga, dgb

    dcq, dckv, dg_q, dg_kv = _rowwise(
        "qkv_norm_bwd", qkvb_fn, t // tr,
        [_rt(dqn, tr), _rt(dkvn, tr), _rt(proj, tr, ql, cq_cb), _rt(proj, tr, kvl, ckv_cb), _whole(g_q), _whole(g_kv)],
        [_rt_out(t, ql, BF16, tr), _rt_out(t, kvl, BF16, tr)],
        [jax.ShapeDtypeStruct((1, ql), F32), jax.ShapeDtypeStruct((1, kvl), F32)])
    dproj = jnp.concatenate([du, dv, dcq, dckv, dkr], axis=1)
    dw_in, = _matmul("e_proj_dw", Mat(h0, t, d), Mat(dproj, t, pi), "tn", [_out(d, pi, BF16)], _pick(d, 1024), _pick(pi, 1024), kt)
    dh0, = _matmul("e_proj_dx", Mat(dproj, t, pi), Mat(w_in_all, d, pi), "nt", [_out(t, d, F32)], tm, _pick(d, 1024), _pick(pi, 4096))
    dx0, _, dg_e = _norm_bwd("e_norm_bwd", dh0, xs, g_e, dx1, tr)

    gfull = jnp.concatenate([dw_in[:, 2 * gw:2 * gw + c2], _unpad_rope(dw_in[:, 2 * gw + c2:]), dw_in[:, :2 * gw]], axis=1)
    gw_in = _stack_cols(gfull)
    gq = jnp.concatenate([dw_q[:, :hw].reshape(ql, heads, LANES), _unpad_rope(dw_q[:, hw:].reshape(ql, heads, LANES))], axis=-1)
    gw_uq = _stack_cols(gq.reshape(ql, heads * (LANES + ROPE)))
    gkv = jnp.concatenate([dw_kv[:, :hw].reshape(kvl, heads, LANES), dw_kv[:, hw:].reshape(kvl, heads, LANES)], axis=-1)
    gw_ukv = _stack_cols(gkv.reshape(kvl, heads * 2 * LANES))
    g5_e = [g.reshape(N_CHIPS, 2, g.shape[1] // 2, g.shape[2])
            for g in (gw_in, gw_uq, gw_ukv, dw_eout.reshape(N_CHIPS, mix // N_CHIPS, d))]
    sc_e, tok = chip_start("e", [_pair_sum(a, b) for a, b in zip(g5_e, _pair_exchange("e", g5_e))])

    def reduced(tag, sc, after):
        part, lands = _exchange_wait("scatter_wait_" + tag, _chip_route, sc, after)
        half = [_chip_sum(p, r) for p, r in zip(part, lands)]
        return [r.reshape(2 * r.shape[1], r.shape[2]) for r in _pair_share(tag, half)]

    small_like = [e_norm_mix, e_q_norm, e_kv_norm, e_v_norm, e_sgu_w, e_sgu_b, e_mla_out_norm, e_sgu_out_norm, mlp_norm, final_norm]
    small_grads = [dg_e, dg_q, dg_kv, dg_vn, dsgu_w, dsgu_b8[:, 0, :], dg_mla, dg_sgu, jnp.concatenate([dg_m0, dg_m1], axis=0), dg_f]
    sflat = _pack_small(small_grads)
    pad = (-sflat.shape[0]) % 8
    sflat = jnp.pad(sflat, ((0, pad), (0, 0)))
    small_started, tok = _exchange_start("small_start", _all_route, 7, [sflat], [_spread(sflat)])

    r_w1_1, r_w2_1 = reduced("m1", sc_m1, tok)
    r_oin, r_oout = reduced("o", sc_o, r_w2_1)
    r_w1_0, r_w2_0, r_small = reduced("m0", sc_m0, r_oout)
    late = {
        "o_w_in": _adamw(o_w_in, [r_oin], m_o_w_in, v_o_w_in),
        "o_w_out": _adamw(o_w_out, [r_oout], m_o_w_out, v_o_w_out),
        "mlp_w1": _adamw(mlp_w1, [r_w1_0, r_w1_1], m_mlp_w1, v_mlp_w1),
        "mlp_w2": _adamw(mlp_w2, [r_w2_0, r_w2_1], m_mlp_w2, v_mlp_w2),
    }

    _, (all_small,) = _exchange_wait("small_wait", _all_route, small_started, late["mlp_w2"][1])
    g_small = _sum_devices(all_small)

    def padded(arrs):
        return jnp.pad(_pack_small(arrs), ((0, pad), (0, 0)))

    s_m = [m_e_norm_mix, m_e_q_norm, m_e_kv_norm, m_e_v_norm, m_e_sgu_w, m_e_sgu_b, m_e_mla_out_norm, m_e_sgu_out_norm, m_mlp_norm, m_final_norm]
    s_v = [v_e_norm_mix, v_e_q_norm, v_e_kv_norm, v_e_v_norm, v_e_sgu_w, v_e_sgu_b, v_e_mla_out_norm, v_e_sgu_out_norm, v_mlp_norm, v_final_norm]
    s_out = [_unpack_small(o[0], small_like)
             for o in _adamw(padded(small_like)[None], [g_small], padded(s_m)[None], padded(s_v)[None])]

    sm = [o[0] for o in _adamw(small_shard[None], [r_small], _small_shard(m_o_norm_mix, m_o_conv_w[0])[None],
                               _small_shard(v_o_norm_mix, v_o_conv_w[0])[None])]

    r_in, r_uq, r_ukv, r_eout = reduced("e", sc_e, late["mlp_w2"][1])
    big = dict(late)
    big.update({
        "e_w_in": _adamw(e_w_in, [r_in], m_e_w_in, v_e_w_in),
        "e_w_uq": _adamw(e_w_uq, [r_uq], m_e_w_uq, v_e_w_uq),
        "e_w_ukv": _adamw(e_w_ukv, [r_ukv], m_e_w_ukv, v_e_w_ukv),
        "e_w_out": _adamw(e_w_out, [r_eout], m_e_w_out, v_e_w_out),
    })

    names = ["e_norm_mix", "e_w_in", "e_q_norm", "e_w_uq", "e_kv_norm", "e_w_ukv", "e_v_norm", "e_sgu_w", "e_sgu_b",
             "e_mla_out_norm", "e_sgu_out_norm", "e_w_out", "o_norm_mix", "o_w_in", "o_conv_w", "o_w_out",
             "mlp_norm", "mlp_w1", "mlp_w2", "final_norm"]
    shapes = {"e_w_in": e_w_in.shape, "e_w_uq": e_w_uq.shape, "e_w_ukv": e_w_ukv.shape, "e_w_out": e_w_out.shape,
              "o_w_in": o_w_in.shape, "o_w_out": o_w_out.shape, "mlp_w1": mlp_w1.shape, "mlp_w2": mlp_w2.shape}
    small_names = ["e_norm_mix", "e_q_norm", "e_kv_norm", "e_v_norm", "e_sgu_w", "e_sgu_b", "e_mla_out_norm",
                   "e_sgu_out_norm", "mlp_norm", "final_norm"]

    def leaf(kind, name):
        if name in big:
            return big[name][kind].reshape(shapes[name])
        if name == "o_norm_mix":
            return sm[kind][0:1]
        if name == "o_conv_w":
            return sm[kind][16:19].reshape(o_conv_w.shape)
        return s_out[kind][small_names.index(name)]

    outs = [loss, dx0.reshape(x.shape)]
    for kind in range(4):
        outs += [leaf(kind, nm) for nm in names]
    return tuple(outs)


def _gcd(a, b):
    while b:
        a, b = b, a % b
    return a
```

```python
import functools

import jax
import jax.numpy as jnp
from jax import lax
from jax.experimental import pallas as pl
from jax.experimental.pallas import tpu as pltpu

F32 = jnp.float32
BF16 = jnp.bfloat16
MESH = pl.DeviceIdType.MESH

LANES = 128
ROPE = 64
ROPE_HALF = ROPE // 2
ROPE_BASE = 10000.0
EPS = 1e-6
N_CHIPS = 4
VMEM_LIMIT = 48 * 1024 * 1024
NEG = -1e30

ADAM_LR = 0.001
ADAM_B1 = 0.9
ADAM_B2 = 0.999
ADAM_EPS = 1e-08
ADAM_WD = 0.01
ADAM_STEP = 10


def _pick(n, target, step=LANES):
    best = None
    for t in range(step, min(n, target) + 1, step):
        if n % t == 0:
            best = t
    return best if best is not None else n


def _params(sem, vmem=VMEM_LIMIT):
    return pltpu.CompilerParams(dimension_semantics=sem, vmem_limit_bytes=vmem)


class Mat:
    def __init__(self, arr, rows, cols, kind="plain", lead=(), col_off=0, shape=None, dtype=None):
        self.arr, self.rows, self.cols, self.kind, self.lead, self.col_off = arr, rows, cols, kind, tuple(lead), col_off
        self.shape = tuple(arr.shape) if arr is not None else tuple(shape)
        self.dtype = arr.dtype if arr is not None else dtype

    def sds(self):
        return jax.ShapeDtypeStruct(self.shape, self.dtype)

    def spec(self, br, bc, gridmap):
        lead, nl = self.lead, len(self.lead)
        if self.kind == "plain":
            assert self.col_off % bc == 0 and self.rows % br == 0 and self.cols % bc == 0, (self.shape, br, bc)
            off = self.col_off // bc
            block = (None,) * nl + (br, bc)

            def phys(rb, cb):
                return lead + (rb, cb + off)
        elif self.kind == "colstack":
            cs = self.shape[-1]
            assert cs % bc == 0 and self.rows % br == 0, (self.shape, br, bc)
            q = cs // bc
            block = (None,) * (nl + 1) + (br, bc)

            def phys(rb, cb):
                return (cb // q,) + lead + (rb, cb % q)
        else:
            rs = self.shape[-2]
            assert rs % br == 0 and self.cols % bc == 0, (self.shape, br, bc)
            q = rs // br
            block = (None,) * (nl + 1) + (br, bc)

            def phys(rb, cb):
                return (rb // q,) + lead + (rb % q, cb)

        return pl.BlockSpec(block, lambda *g: phys(*gridmap(*g)))


def _matmul(name, a, b, mode, outs, tm, tn, tk, epilogue=None, extras=(), deps=()):
    if mode == "nn":
        m, k, n = a.rows, a.cols, b.cols
        a_spec = a.spec(tm, tk, lambda i, j, kk: (i, kk))
        b_spec = b.spec(tk, tn, lambda i, j, kk: (kk, j))
        dims = (((1,), (0,)), ((), ()))
    elif mode == "nt":
        m, k, n = a.rows, a.cols, b.rows
        a_spec = a.spec(tm, tk, lambda i, j, kk: (i, kk))
        b_spec = b.spec(tn, tk, lambda i, j, kk: (j, kk))
        dims = (((1,), (1,)), ((), ()))
    else:
        k, m, n = a.rows, a.cols, b.cols
        a_spec = a.spec(tk, tm, lambda i, j, kk: (kk, i))
        b_spec = b.spec(tk, tn, lambda i, j, kk: (kk, j))
        dims = (((0,), (0,)), ((), ()))
    assert m % tm == 0 and n % tn == 0 and k % tk == 0, (name, m, n, k, tm, tn, tk)
    grid = (m // tm, n // tn, k // tk)
    nk = grid[2]
    n_ex, n_out, n_dep = len(extras), len(outs), len(deps)
    tile = lambda i, j, kk: (i, j)

    def finish(z, ex, out_refs):
        vals = epilogue(z, *[e[...] for e in ex]) if epilogue is not None else (z,)
        for o, v in zip(out_refs, vals):
            o[...] = v.astype(o.dtype)

    def body_single(a_ref, b_ref, *rest):
        finish(lax.dot_general(a_ref[...], b_ref[...], dims, preferred_element_type=F32),
               rest[:n_ex], rest[n_ex + n_dep:n_ex + n_dep + n_out])

    def body_acc(a_ref, b_ref, *rest):
        acc = rest[-1]
        kk = pl.program_id(2)

        @pl.when(kk == 0)
        def _():
            acc[...] = jnp.zeros_like(acc)

        acc[...] += lax.dot_general(a_ref[...], b_ref[...], dims, preferred_element_type=F32)

        @pl.when(kk == nk - 1)
        def _():
            finish(acc[...], rest[:n_ex], rest[n_ex + n_dep:n_ex + n_dep + n_out])

    res = pl.pallas_call(
        body_single if nk == 1 else body_acc, name=name, grid=grid,
        in_specs=[a_spec, b_spec] + [e.spec(tm, tn, tile) for e in extras]
        + [pl.BlockSpec(memory_space=pl.ANY) for _ in deps],
        out_specs=[o.spec(tm, tn, tile) for o in outs],
        out_shape=[o.sds() for o in outs],
        scratch_shapes=[] if nk == 1 else [pltpu.VMEM((tm, tn), F32)],
        compiler_params=_params(("parallel", "parallel", "arbitrary")),
    )(a.arr, b.arr, *[e.arr for e in extras], *deps)
    return res


def _out(rows, cols, dtype, kind="plain", lead=(), shape=None):
    return Mat(None, rows, cols, kind, lead, shape=shape if shape is not None else (rows, cols), dtype=dtype)


def _rt(arr, tr, width=None, cb=0):
    width = arr.shape[1] if width is None else width
    return arr, pl.BlockSpec((tr, width), lambda i: (i, cb))


def _whole(arr):
    nd = arr.ndim
    return arr, pl.BlockSpec(arr.shape, lambda i: (0,) * nd)


def _rowwise(name, fn, n_steps, ins, outs, accs=(), deps=()):
    n_in, n_out, n_acc, n_dep = len(ins), len(outs), len(accs), len(deps)

    def body(*refs):
        vals = fn(*[r[...] for r in refs[:n_in]])
        if not isinstance(vals, (tuple, list)):
            vals = (vals,)
        for ref, v in zip(refs[n_in + n_dep:n_in + n_dep + n_out], vals[:n_out]):
            ref[...] = v.astype(ref.dtype)
        if n_acc:
            acc_refs = refs[n_in + n_dep + n_out:]

            @pl.when(pl.program_id(0) == 0)
            def _():
                for ref in acc_refs:
                    ref[...] = jnp.zeros_like(ref)

            for ref, v in zip(acc_refs, vals[n_out:]):
                ref[...] += v

    acc_specs = [pl.BlockSpec(s.shape, lambda i, nd=len(s.shape): (0,) * nd) for s in accs]
    res = pl.pallas_call(
        body, name=name, grid=(n_steps,),
        in_specs=[s for _, s in ins] + [pl.BlockSpec(memory_space=pl.ANY) for _ in deps],
        out_specs=[s for _, s in outs] + acc_specs,
        out_shape=[o for o, _ in outs] + list(accs),
        compiler_params=_params(("arbitrary",) if n_acc else ("parallel",)),
    )(*[a for a, _ in ins], *deps)
    return res


def _rt_out(t, width, dtype, tr):
    return jax.ShapeDtypeStruct((t, width), dtype), pl.BlockSpec((tr, width), lambda i: (i, 0))


def _rms(x, g):
    r = lax.rsqrt(jnp.mean(x * x, axis=-1, keepdims=True) + EPS)
    return x * r * g


def _rms_bwd(dy, x, g):
    r = lax.rsqrt(jnp.mean(x * x, axis=-1, keepdims=True) + EPS)
    xh = x * r
    dxh = dy * g
    dx = r * (dxh - xh * jnp.mean(dxh * xh, axis=-1, keepdims=True))
    dg = jnp.sum(dy * xh, axis=0, keepdims=True)
    return dx, dg


def _gelu(x):
    k = 0.7978845608028654
    th = jnp.tanh(k * (x + 0.044715 * (x * x * x)))
    return x * (0.5 * (1.0 + th))


def _gelu_grad(x):
    k = 0.7978845608028654
    x2 = x * x
    th = jnp.tanh(k * (x + 0.044715 * (x2 * x)))
    return 0.5 * (1.0 + th) + 0.5 * x * (1.0 - th * th) * (k * (1.0 + 3.0 * 0.044715 * x2))


def _norm_fwd(name, x, g, tr):
    t, d = x.shape
    return _rowwise(name, lambda xv, gv: _rms(xv, gv), t // tr, [_rt(x, tr), _whole(g)], [_rt_out(t, d, BF16, tr)])[0]


def _norm_bwd(name, dh, x, g, dres, tr):
    t, d = x.shape

    def fn(dhv, xv, gv, drv):
        dx, dg = _rms_bwd(dhv, xv, gv)
        dx = dx + drv
        return dx, dx, dg

    return _rowwise(name, fn, t // tr, [_rt(dh, tr), _rt(x, tr), _whole(g), _rt(dres, tr)],
                    [_rt_out(t, d, F32, tr), _rt_out(t, d, BF16, tr)], [jax.ShapeDtypeStruct((1, d), F32)])


def _rope_tables(posf, invf, cmask, smask, tr):
    t = posf.shape[0]

    def fn(p, f, cm, sm):
        ang = p * f
        return jnp.cos(ang) * cm, jnp.sin(ang) * sm

    return _rowwise("rope_tables", fn, t // tr, [_rt(posf, tr), _whole(invf), _whole(cmask), _whole(smask)],
                    [_rt_out(t, LANES, F32, tr), _rt_out(t, LANES, F32, tr)])


def _rot(v, c, s):
    return v * c + pltpu.roll(v, ROPE, axis=1) * s


def _rot_bwd(dv, c, s):
    return dv * c + pltpu.roll(dv * s, ROPE, axis=1)


def _rope_fwd(qfull, proj, kr_cb, ctab, stab, heads, tr):
    t = qfull.shape[0]
    hw = heads * LANES

    def fn(q, kr, c, s):
        parts = [q[:, :hw]] + [_rot(q[:, hw + h * LANES: hw + (h + 1) * LANES], c, s) for h in range(heads)]
        return jnp.concatenate(parts, axis=1), _rot(kr, c, s)

    return _rowwise("rope_fwd", fn, t // tr, [_rt(qfull, tr), _rt(proj, tr, LANES, kr_cb), _rt(ctab, tr), _rt(stab, tr)],
                    [_rt_out(t, 2 * hw, BF16, tr), _rt_out(t, LANES, BF16, tr)])


def _rope_bwd(dq1, dq2, dkr_h, ctab, stab, heads, tr):
    t = dq1.shape[0]
    hw = heads * LANES

    def fn(a, b, dk, c, s):
        parts = [a] + [_rot_bwd(b[:, h * LANES:(h + 1) * LANES], c, s) for h in range(heads)]
        dks = dk[0]
        for h in range(1, heads):
            dks = dks + dk[h]
        return jnp.concatenate(parts, axis=1), _rot_bwd(dks, c, s)

    dk_spec = pl.BlockSpec((heads, tr, LANES), lambda i: (0, i, 0))
    return _rowwise("rope_bwd", fn, t // tr, [_rt(dq1, tr), _rt(dq2, tr), (dkr_h, dk_spec), _rt(ctab, tr), _rt(stab, tr)],
                    [_rt_out(t, 2 * hw, BF16, tr), _rt_out(t, LANES, BF16, tr)])


def _dot_nt(a, b):
    return lax.dot_general(a, b, (((1,), (1,)), ((), ())), preferred_element_type=F32)


def _dot_tn(a, b):
    return lax.dot_general(a, b, (((0,), (0,)), ((), ())), preferred_element_type=F32)


def _dot(a, b):
    return jnp.dot(a, b, preferred_element_type=F32)


def _ranges(n_blocks):
    n_var = min(4, n_blocks)
    assert n_blocks % n_var == 0
    return n_var, n_blocks // n_var


def _row_of(col):
    return col.T[:8, :]


def _attn_fwd(qall, kvall, kr, heads, scale, tq):
    t = qall.shape[0]
    nq = t // tq
    n_var, per = _ranges(nq)

    def body(qn_ref, qr_ref, kn_ref, v_ref, kr_ref, o_ref, lse_ref, lser_ref):
        i = pl.program_id(1)
        for var in range(n_var):
            kv = (var + 1) * per * tq

            @pl.when(jnp.logical_and(i >= var * per, i < (var + 1) * per))
            def _(kv=kv):
                s = (_dot_nt(qn_ref[...], kn_ref[:kv, :]) + _dot_nt(qr_ref[...], kr_ref[:kv, :])) * scale
                rows = i * tq + lax.broadcasted_iota(jnp.int32, (tq, kv), 0)
                cols = lax.broadcasted_iota(jnp.int32, (tq, kv), 1)
                s = jnp.where(cols <= rows, s, NEG)
                m = jnp.max(s, axis=-1, keepdims=True)
                p = jnp.exp(s - m)
                l = jnp.sum(p, axis=-1, keepdims=True)
                o_ref[...] = _dot(p.astype(BF16), v_ref[:kv, :]) / l
                lse = jnp.broadcast_to(m + jnp.log(l), (tq, LANES))
                lse_ref[...] = lse
                lser_ref[...] = _row_of(lse)

    return pl.pallas_call(
        body, name="attn_fwd", grid=(heads, nq),
        in_specs=[pl.BlockSpec((tq, LANES), lambda h, i: (i, h)),
                  pl.BlockSpec((tq, LANES), lambda h, i: (i, heads + h)),
                  pl.BlockSpec((t, LANES), lambda h, i: (0, h)),
                  pl.BlockSpec((t, LANES), lambda h, i: (0, heads + h)),
                  pl.BlockSpec((t, LANES), lambda h, i: (0, 0))],
        out_specs=[pl.BlockSpec((tq, LANES), lambda h, i: (i, h)),
                   pl.BlockSpec((None, tq, LANES), lambda h, i: (h, i, 0)),
                   pl.BlockSpec((None, 8, tq), lambda h, i: (h, 0, i))],
        out_shape=[jax.ShapeDtypeStruct((t, heads * LANES), F32), jax.ShapeDtypeStruct((heads, t, LANES), F32),
                   jax.ShapeDtypeStruct((heads, 8, t), F32)],
        compiler_params=_params(("parallel", "parallel")),
    )(qall, qall, kvall, kvall, kr)


def _attn_dq(qall, kvall, kr, do, lse, delta, heads, scale, tq):
    t = qall.shape[0]
    nq = t // tq
    n_var, per = _ranges(nq)

    def body(qn_ref, qr_ref, kn_ref, v_ref, kr_ref, do_ref, lse_ref, dl_ref, dq1_ref, dq2_ref):
        i = pl.program_id(1)
        for var in range(n_var):
            kv = (var + 1) * per * tq

            @pl.when(jnp.logical_and(i >= var * per, i < (var + 1) * per))
            def _(kv=kv):
                k1, k2 = kn_ref[:kv, :], kr_ref[:kv, :]
                s = (_dot_nt(qn_ref[...], k1) + _dot_nt(qr_ref[...], k2)) * scale
                rows = i * tq + lax.broadcasted_iota(jnp.int32, (tq, kv), 0)
                cols = lax.broadcasted_iota(jnp.int32, (tq, kv), 1)
                p = jnp.where(cols <= rows, jnp.exp(s - lse_ref[...][:, :1]), 0.0)
                dp = _dot_nt(do_ref[...], v_ref[:kv, :])
                ds = (p * (dp - dl_ref[...][:, :1]) * scale).astype(BF16)
                dq1_ref[...] = _dot(ds, k1)
                dq2_ref[...] = _dot(ds, k2)

    qblk = lambda off: pl.BlockSpec((tq, LANES), lambda h, i: (i, off + h))
    full = lambda off: pl.BlockSpec((t, LANES), lambda h, i: (0, off + h))
    stat = pl.BlockSpec((None, tq, LANES), lambda h, i: (h, i, 0))
    return pl.pallas_call(
        body, name="attn_dq", grid=(heads, nq),
        in_specs=[qblk(0), qblk(heads), full(0), full(heads), pl.BlockSpec((t, LANES), lambda h, i: (0, 0)),
                  qblk(0), stat, stat],
        out_specs=[qblk(0), qblk(0)],
        out_shape=[jax.ShapeDtypeStruct((t, heads * LANES), F32)] * 2,
        compiler_params=_params(("parallel", "parallel")),
    )(qall, qall, kvall, kvall, kr, do, lse, delta)


def _attn_dkv(qall, kvall, kr, do, lse_row, delta_row, heads, scale, tk):
    t = qall.shape[0]
    nk = t // tk
    n_var, per = _ranges(nk)

    def body(qn_ref, qr_ref, kn_ref, v_ref, kr_ref, do_ref, lse_ref, dl_ref, dk_ref, dv_ref, dkr_ref):
        j = pl.program_id(1)
        for var in range(n_var):
            q0 = var * per * tk
            nq = t - q0

            @pl.when(jnp.logical_and(j >= var * per, j < (var + 1) * per))
            def _(q0=q0, nq=nq):
                qn, qr, do_v = qn_ref[q0:, :], qr_ref[q0:, :], do_ref[q0:, :]
                st = (_dot_nt(kn_ref[...], qn) + _dot_nt(kr_ref[...], qr)) * scale
                keys = j * tk + lax.broadcasted_iota(jnp.int32, (tk, nq), 0)
                queries = q0 + lax.broadcasted_iota(jnp.int32, (tk, nq), 1)
                pt = jnp.where(keys <= queries, jnp.exp(st - lse_ref[0:1, q0:]), 0.0)
                dpt = _dot_nt(v_ref[...], do_v)
                dst = (pt * (dpt - dl_ref[0:1, q0:]) * scale).astype(BF16)
                dv_ref[...] = _dot(pt.astype(BF16), do_v).astype(dv_ref.dtype)
                dk_ref[...] = _dot(dst, qn).astype(dk_ref.dtype)
                dkr_ref[...] = _dot(dst, qr)

    kblk = lambda off: pl.BlockSpec((tk, LANES), lambda h, j: (j, off + h))
    full = lambda off: pl.BlockSpec((t, LANES), lambda h, j: (0, off + h))
    stat = pl.BlockSpec((None, 8, t), lambda h, j: (h, 0, 0))
    return pl.pallas_call(
        body, name="attn_dkv", grid=(heads, nk),
        in_specs=[full(0), full(heads), kblk(0), kblk(heads), pl.BlockSpec((tk, LANES), lambda h, j: (j, 0)),
                  full(0), stat, stat],
        out_specs=[kblk(0), kblk(0), pl.BlockSpec((None, tk, LANES), lambda h, j: (h, j, 0))],
        out_shape=[jax.ShapeDtypeStruct((t, heads * LANES), BF16)] * 2 + [jax.ShapeDtypeStruct((heads, t, LANES), F32)],
        compiler_params=_params(("parallel", "parallel")),
    )(qall, qall, kvall, kvall, kr, do, lse_row, delta_row)


def _tril():
    return lax.broadcasted_iota(jnp.int32, (LANES, LANES), 0) >= lax.broadcasted_iota(jnp.int32, (LANES, LANES), 1)


def _group_norm(vg):
    mu = jnp.mean(vg, axis=-1, keepdims=True)
    vc = vg - mu
    rs = lax.rsqrt(jnp.mean(vc * vc, axis=-1, keepdims=True) + EPS)
    return vc * rs, rs


def _sgu_fwd(proj, gain, w, bias, groups, rb):
    t = proj.shape[0]
    gw = groups * LANES
    cpb = rb // LANES

    def body(u_ref, v_ref, gain_ref, w_ref, b_ref, s_ref):
        tril = _tril()
        for g in range(groups):
            wt = jnp.where(tril, w_ref[g], 0.0).astype(BF16)
            cols = slice(g * LANES, (g + 1) * LANES)
            for ci in range(cpb):
                rows = slice(ci * LANES, (ci + 1) * LANES)
                ug = _gelu(u_ref[rows, cols])
                vh, _ = _group_norm(_gelu(v_ref[rows, cols]))
                vn = vh * gain_ref[:, cols]
                y = _dot(wt, vn.astype(BF16)) + b_ref[g]
                s_ref[rows, cols] = ug * y

    return pl.pallas_call(
        body, name="sgu_fwd", grid=(t // rb,),
        in_specs=[pl.BlockSpec((rb, gw), lambda i: (i, 0)), pl.BlockSpec((rb, gw), lambda i: (i, 1)),
                  pl.BlockSpec((1, gw), lambda i: (0, 0)),
                  pl.BlockSpec((groups, LANES, LANES), lambda i: (0, 0, 0)),
                  pl.BlockSpec((groups, LANES, LANES), lambda i: (0, 0, 0))],
        out_specs=pl.BlockSpec((rb, gw), lambda i: (i, 0)),
        out_shape=jax.ShapeDtypeStruct((t, gw), F32),
        compiler_params=_params(("parallel",)),
    )(proj, proj, gain, w, bias)


def _sgu_bwd(proj, ds, gain, w, bias, groups, rb):
    t = proj.shape[0]
    gw = groups * LANES
    cpb = rb // LANES
    n_steps = t // rb

    def body(u_ref, v_ref, ds_ref, gain_ref, w_ref, b_ref, du_ref, dv_ref, dw_ref, db_ref, dg_ref, dy_acc):
        step = pl.program_id(0)

        @pl.when(step == 0)
        def _():
            dw_ref[...] = jnp.zeros_like(dw_ref)
            dy_acc[...] = jnp.zeros_like(dy_acc)
            dg_ref[...] = jnp.zeros_like(dg_ref)

        tril = _tril()
        for g in range(groups):
            wt = jnp.where(tril, w_ref[g], 0.0).astype(BF16)
            cols = slice(g * LANES, (g + 1) * LANES)
            gain_g = gain_ref[:, cols]
            for ci in range(cpb):
                rows = slice(ci * LANES, (ci + 1) * LANES)
                u_raw, v_raw, ds_v = u_ref[rows, cols], v_ref[rows, cols], ds_ref[rows, cols]
                ug = _gelu(u_raw)
                vh, rs = _group_norm(_gelu(v_raw))
                vn = (vh * gain_g).astype(BF16)
                y = _dot(wt, vn) + b_ref[g]
                dy = ds_v * ug
                dyb = dy.astype(BF16)
                du_ref[rows, cols] = (ds_v * y * _gelu_grad(u_raw)).astype(du_ref.dtype)
                dy_acc[g] += dy
                dw_ref[g] += _dot_nt(dyb, vn)
                dvn = _dot_tn(wt, dyb)
                dg_ref[:, cols] += jnp.sum(dvn * vh, axis=0, keepdims=True)
                dvh = dvn * gain_g
                dvg = rs * (dvh - jnp.mean(dvh, axis=-1, keepdims=True)
                            - vh * jnp.mean(dvh * vh, axis=-1, keepdims=True))
                dv_ref[rows, cols] = (dvg * _gelu_grad(v_raw)).astype(dv_ref.dtype)

        @pl.when(step == n_steps - 1)
        def _():
            ones = jnp.ones((8, LANES), F32)
            for g in range(groups):
                dw_ref[g] = jnp.where(tril, dw_ref[g], 0.0)
                db_ref[g] = lax.dot_general(ones, dy_acc[g], (((1,), (1,)), ((), ())),
                                            precision=lax.Precision.HIGHEST, preferred_element_type=F32)

    blk = lambda cb: pl.BlockSpec((rb, gw), lambda i: (i, cb))
    whole3 = pl.BlockSpec((groups, LANES, LANES), lambda i: (0, 0, 0))
    return pl.pallas_call(
        body, name="sgu_bwd", grid=(n_steps,),
        in_specs=[blk(0), blk(1), blk(0), pl.BlockSpec((1, gw), lambda i: (0, 0)), whole3, whole3],
        out_specs=[blk(0), blk(0), whole3, pl.BlockSpec((groups, 8, LANES), lambda i: (0, 0, 0)),
                   pl.BlockSpec((1, gw), lambda i: (0, 0))],
        out_shape=[jax.ShapeDtypeStruct((t, gw), BF16), jax.ShapeDtypeStruct((t, gw), BF16),
                   jax.ShapeDtypeStruct((groups, LANES, LANES), F32), jax.ShapeDtypeStruct((groups, 8, LANES), F32),
                   jax.ShapeDtypeStruct((1, gw), F32)],
        scratch_shapes=[pltpu.VMEM((groups, LANES, LANES), F32)],
        compiler_params=_params(("arbitrary",)),
    )(proj, proj, ds, gain, w, bias)


def _shift_down(z, s):
    rows = lax.broadcasted_iota(jnp.int32, z.shape, 0)
    return jnp.where(rows >= s, pltpu.roll(z, s, axis=0), 0.0)


def _shift_up(z, s):
    n = z.shape[0]
    rows = lax.broadcasted_iota(jnp.int32, z.shape, 0)
    return jnp.where(rows < n - s, pltpu.roll(z, n - s, axis=0), 0.0)


def _conv_fwd(proj3, cw, tc):
    _, t, cd = proj3.shape

    def body(p_ref, w_ref, o_ref):
        z = p_ref[1] * p_ref[2]
        w = w_ref[...]
        zc = w[2:3] * z + w[1:2] * _shift_down(z, 1) + w[0:1] * _shift_down(z, 2)
        o_ref[...] = (p_ref[0] * zc).astype(o_ref.dtype)

    return pl.pallas_call(
        body, name="conv_fwd", grid=(cd // tc,),
        in_specs=[pl.BlockSpec((3, t, tc), lambda j: (0, 0, j)), pl.BlockSpec((8, tc), lambda j: (0, j))],
        out_specs=pl.BlockSpec((t, tc), lambda j: (0, j)),
        out_shape=jax.ShapeDtypeStruct((t, cd), BF16),
        compiler_params=_params(("parallel",)),
    )(proj3, cw)


def _conv_bwd(proj3, cw, dbz, tc):
    _, t, cd = proj3.shape

    def body(p_ref, w_ref, d_ref, o_ref, dw_ref):
        b, c, xin = p_ref[0], p_ref[1], p_ref[2]
        w = w_ref[...]
        z = c * xin
        z1, z2 = _shift_down(z, 1), _shift_down(z, 2)
        zc = w[2:3] * z + w[1:2] * z1 + w[0:1] * z2
        d = d_ref[...]
        dzc = d * b
        dz = w[2:3] * dzc + w[1:2] * _shift_up(dzc, 1) + w[0:1] * _shift_up(dzc, 2)
        o_ref[0] = (d * zc).astype(o_ref.dtype)
        o_ref[1] = (dz * xin).astype(o_ref.dtype)
        o_ref[2] = (dz * c).astype(o_ref.dtype)
        row = lax.broadcasted_iota(jnp.int32, (8, tc), 0)
        dw0 = jnp.sum(dzc * z2, axis=0, keepdims=True)
        dw1 = jnp.sum(dzc * z1, axis=0, keepdims=True)
        dw2 = jnp.sum(dzc * z, axis=0, keepdims=True)
        dw_ref[...] = jnp.where(row == 0, dw0, 0.0) + jnp.where(row == 1, dw1, 0.0) + jnp.where(row == 2, dw2, 0.0)

    return pl.pallas_call(
        body, name="conv_bwd", grid=(cd // tc,),
        in_specs=[pl.BlockSpec((3, t, tc), lambda j: (0, 0, j)), pl.BlockSpec((8, tc), lambda j: (0, j)),
                  pl.BlockSpec((t, tc), lambda j: (0, j))],
        out_specs=[pl.BlockSpec((3, t, tc), lambda j: (0, 0, j)), pl.BlockSpec((8, tc), lambda j: (0, j))],
        out_shape=[jax.ShapeDtypeStruct((3, t, cd), BF16), jax.ShapeDtypeStruct((8, cd), F32)],
        compiler_params=_params(("parallel",)),
    )(proj3, cw, dbz)


def _place():
    x, y, c = lax.axis_index("x"), lax.axis_index("y"), lax.axis_index("c")
    chips = [(1 - x, y), (x, 1 - y), (1 - x, 1 - y)]
    return x, y, c, chips


def _any_specs(n):
    return [pl.BlockSpec(memory_space=pl.ANY) for _ in range(n)]


HBM_SPEC = pl.BlockSpec(memory_space=pltpu.HBM)
SEM_SPEC = pl.BlockSpec(memory_space=pltpu.SEMAPHORE)
ORDERED_EFFECT = pltpu.SideEffectType.DATAFLOW_SIDE_EFFECTING


def _in_hbm(a):
    return pltpu.with_memory_space_constraint(a, pltpu.HBM)


def _token():
    return jax.ShapeDtypeStruct((8, LANES), F32), pl.BlockSpec(memory_space=pltpu.VMEM)


def _gather_start(name, groups):
    sizes = [len(g) for g in groups]
    flat = [b for g in groups for b in g]
    n, ng = len(flat), len(groups)

    def body(*refs):
        ins, sems, token = refs[:n], refs[n:n + 2 * ng], refs[-1]
        x, y, c, chips = _place()
        me = 2 * x + y
        i = 0
        for gi, size in enumerate(sizes):
            for j in range(size):
                blk = ins[i].at[me, c]
                for k, chip in enumerate(chips):
                    pltpu.make_async_remote_copy(src_ref=blk, dst_ref=blk, send_sem=sems[2 * gi].at[3 * j + k],
                                                 recv_sem=sems[2 * gi + 1].at[3 * j + k],
                                                 device_id=(*chip, c), device_id_type=MESH).start()
                i += 1
        token[...] = jnp.zeros_like(token)

    tok_shape, tok_spec = _token()
    res = pl.pallas_call(
        body, name=name,
        in_specs=[HBM_SPEC] * n,
        out_specs=[SEM_SPEC] * (2 * ng) + [HBM_SPEC] * n + [tok_spec],
        out_shape=[pltpu.SemaphoreType.DMA((3 * size,)) for size in sizes for _ in (0, 1)]
        + [pltpu.HBM(b.shape, b.dtype) for b in flat] + [tok_shape],
        input_output_aliases={i: 2 * ng + i for i in range(n)},
        compiler_params=pltpu.CompilerParams(has_side_effects=ORDERED_EFFECT),
    )(*[_in_hbm(b) for b in flat])
    out, i = [], 2 * ng
    for gi, size in enumerate(sizes):
        out.append((res[2 * gi], res[2 * gi + 1], list(res[i:i + size])))
        i += size
    return out, res[-1]


def _gather_wait(tag, send, recv, bufs, after):
    n = len(bufs)
    after = tuple(after) if isinstance(after, (tuple, list)) else (after,)

    def body(*refs):
        ins, send_ref, recv_ref = refs[:n], refs[n], refs[n + 1]
        x, y, c, chips = _place()
        me = 2 * x + y
        for j in range(n):
            for k, (px, py) in enumerate(chips):
                cp = pltpu.make_async_remote_copy(src_ref=ins[j].at[me, c], dst_ref=ins[j].at[2 * px + py, c],
                                                  send_sem=send_ref.at[3 * j + k], recv_sem=recv_ref.at[3 * j + k],
                                                  device_id=(px, py, c), device_id_type=MESH)
                cp.wait_send()
                cp.wait_recv()

    return pl.pallas_call(
        body, name="gather_wait_" + tag,
        in_specs=[HBM_SPEC] * n + [SEM_SPEC, SEM_SPEC] + _any_specs(len(after)),
        out_specs=[HBM_SPEC] * n,
        out_shape=[pltpu.HBM(b.shape, b.dtype) for b in bufs],
        input_output_aliases={i: i for i in range(n)},
        compiler_params=pltpu.CompilerParams(has_side_effects=ORDERED_EFFECT),
    )(*bufs, send, recv, *after)


def _gather_forward(tag, bufs):
    n = len(bufs)

    def body(*refs):
        ins, outs = refs[:n], refs[n:2 * n]
        send, recv = refs[2 * n:]
        x, y, c, chips = _place()
        sib = (x, y, 1 - c)

        def cp(i, k, slot, half):
            return pltpu.make_async_remote_copy(src_ref=ins[i].at[slot, half], dst_ref=outs[i].at[slot, half],
                                                send_sem=send.at[3 * i + k], recv_sem=recv.at[3 * i + k],
                                                device_id=sib, device_id_type=MESH)

        cps = [cp(i, k, 2 * px + py, c) for i in range(n) for k, (px, py) in enumerate(chips)]
        for d in cps:
            d.start()
        for i in range(n):
            for k, (px, py) in enumerate(chips):
                cp(i, k, 2 * px + py, 1 - c).wait_recv()
        for d in cps:
            d.wait_send()

    return pl.pallas_call(
        body, name="gather_forward_" + tag,
        in_specs=_any_specs(n), out_specs=_any_specs(n),
        out_shape=[jax.ShapeDtypeStruct(b.shape, b.dtype) for b in bufs],
        scratch_shapes=[pltpu.SemaphoreType.DMA((3 * n,))] * 2,
        input_output_aliases={i: i for i in range(n)},
        compiler_params=pltpu.CompilerParams(has_side_effects=True),
    )(*bufs)


def _pair_route(srcs, zones):
    x, y, c, _ = _place()
    return [(srcs[i].at[j, 1 - c], zones[i].at[j], (x, y, 1 - c)) for i in range(len(srcs)) for j in range(N_CHIPS)]


def _chip_route(srcs, zones):
    x, y, c, chips = _place()
    return [(srcs[i].at[2 * px + py], zones[i].at[k], (px, py, c)) for i in range(len(srcs)) for k, (px, py) in enumerate(chips)]


def _all_route(srcs, zones):
    x, y, c, _ = _place()
    flips = [(fx, fy, fc) for fx in (0, 1) for fy in (0, 1) for fc in (0, 1)][1:]
    return [(srcs[0], zones[0].at[4 * x + 2 * y + c], (x + fx - 2 * x * fx, y + fy - 2 * y * fy, c + fc - 2 * c * fc))
            for fx, fy, fc in flips]


def _share_route(srcs, zones):
    x, y, c, _ = _place()
    return [(s.at[c], s.at[c], (x, y, 1 - c)) for s in srcs]


def _exchange_start(name, route, n_copies, srcs, zones):
    n, nz = len(srcs), len(zones)
    lands = [lax.empty(z, a.dtype) if isinstance(z, tuple) else z for z, a in zip(zones, srcs)]

    def body(*refs):
        ins, zone_refs, send, recv, token = refs[:n], refs[n:n + nz], refs[n + nz], refs[n + nz + 1], refs[-1]
        for k, (src, dst, dev) in enumerate(route(ins, zone_refs)):
            pltpu.make_async_remote_copy(src_ref=src, dst_ref=dst, send_sem=send.at[k], recv_sem=recv.at[k],
                                         device_id=dev, device_id_type=MESH).start()
        token[...] = jnp.zeros_like(token)

    tok_shape, tok_spec = _token()
    res = pl.pallas_call(
        body, name=name,
        in_specs=[HBM_SPEC] * (n + nz),
        out_specs=[SEM_SPEC, SEM_SPEC] + [HBM_SPEC] * (n + nz) + [tok_spec],
        out_shape=[pltpu.SemaphoreType.DMA((n_copies,))] * 2 + [pltpu.HBM(a.shape, a.dtype) for a in srcs + lands]
        + [tok_shape],
        input_output_aliases={i: 2 + i for i in range(n + nz)},
        compiler_params=pltpu.CompilerParams(has_side_effects=ORDERED_EFFECT),
    )(*[_in_hbm(a) for a in srcs + lands])
    return (res[0], res[1], list(res[2:2 + n]), list(res[2 + n:2 + n + nz])), res[-1]


def _exchange_wait(name, route, started, after):
    send, recv, srcs, lands = started
    n, nz = len(srcs), len(lands)
    after = tuple(after) if isinstance(after, (tuple, list)) else (after,)

    def body(*refs):
        ins, zone_refs, send_ref, recv_ref = refs[:n], refs[n:n + nz], refs[n + nz], refs[n + nz + 1]
        for k, (src, dst, dev) in enumerate(route(ins, zone_refs)):
            cp = pltpu.make_async_remote_copy(src_ref=src, dst_ref=dst, send_sem=send_ref.at[k], recv_sem=recv_ref.at[k],
                                              device_id=dev, device_id_type=MESH)
            cp.wait_send()
            cp.wait_recv()

    res = pl.pallas_call(
        body, name=name,
        in_specs=[HBM_SPEC] * (n + nz) + [SEM_SPEC, SEM_SPEC] + _any_specs(len(after)),
        out_specs=[HBM_SPEC] * (n + nz),
        out_shape=[pltpu.HBM(a.shape, a.dtype) for a in srcs + lands],
        input_output_aliases={i: i for i in range(n + nz)},
        compiler_params=pltpu.CompilerParams(has_side_effects=ORDERED_EFFECT),
    )(*srcs, *lands, send, recv, *after)
    return list(res[:n]), list(res[n:])


def _spread(v):
    rows, cols = v.shape
    tr = _row_tile(rows, cols, budget=256 * 1024)

    def body(v_ref, o_ref):
        o_ref[...] = jnp.broadcast_to(v_ref[...][None], o_ref.shape)

    return pl.pallas_call(body, name="spread_small_grads", grid=(rows // tr,),
                          in_specs=[pl.BlockSpec((tr, cols), lambda r: (r, 0))],
                          out_specs=pl.BlockSpec((8, tr, cols), lambda r: (0, r, 0)),
                          out_shape=jax.ShapeDtypeStruct((8, rows, cols), v.dtype),
                          compiler_params=_params(("parallel",)))(v)


def _row_tile(rows, cols, itemsize=4, budget=2 * 1024 * 1024):
    best = None
    for t in range(8, rows + 1, 8):
        if rows % t == 0 and t * cols * itemsize <= budget:
            best = t
    return best if best is not None else rows


def _my_chip():
    return 2 * lax.axis_index("x") + lax.axis_index("y")


def _pair_sum(g5, gsib):
    _, _, rh, cols = g5.shape
    tr = _row_tile(rh, cols)

    def body(a_ref, b_ref, o_ref):
        o_ref[...] = (a_ref[...].astype(F32) + b_ref[...].astype(F32)).astype(o_ref.dtype)

    return pl.pallas_call(body, name="grad_pair_sum", grid=(N_CHIPS, rh // tr),
                          in_specs=[pl.BlockSpec((None, None, tr, cols), lambda j, r: (j, lax.axis_index("c"), r, 0)),
                                    pl.BlockSpec((None, tr, cols), lambda j, r: (j, r, 0))],
                          out_specs=pl.BlockSpec((None, tr, cols), lambda j, r: (j, r, 0)),
                          out_shape=jax.ShapeDtypeStruct((N_CHIPS, rh, cols), BF16),
                          compiler_params=_params(("parallel", "parallel")))(g5, gsib)


def _chip_sum(part, recv):
    _, rh, cols = part.shape
    tr = _row_tile(rh, cols)

    def body(a_ref, b_ref, o_ref):
        acc = a_ref[...].astype(F32)
        for k in range(3):
            acc = acc + b_ref[k].astype(F32)
        o_ref[...] = acc

    return pl.pallas_call(body, name="grad_chip_sum", grid=(rh // tr,),
                          in_specs=[pl.BlockSpec((None, tr, cols), lambda r: (_my_chip(), r, 0)),
                                    pl.BlockSpec((3, tr, cols), lambda r: (0, r, 0))],
                          out_specs=pl.BlockSpec((None, tr, cols), lambda r: (lax.axis_index("c"), r, 0)),
                          out_shape=jax.ShapeDtypeStruct((2, rh, cols), F32),
                          compiler_params=_params(("parallel",)))(part, recv)


def _sum_devices(g):
    _, rows, cols = g.shape
    tr = _row_tile(rows, cols, budget=256 * 1024)

    def body(g_ref, o_ref):
        acc = g_ref[0]
        for d in range(1, 8):
            acc = acc + g_ref[d]
        o_ref[...] = acc

    return pl.pallas_call(body, name="sum_small_grads", grid=(rows // tr,),
                          in_specs=[pl.BlockSpec((8, tr, cols), lambda r: (0, r, 0))],
                          out_specs=pl.BlockSpec((tr, cols), lambda r: (r, 0)),
                          out_shape=jax.ShapeDtypeStruct((rows, cols), F32),
                          compiler_params=_params(("parallel",)))(g)


def _place_shard(w, layer, dtype, deps=()):
    _, rows, cols = w.shape
    tr = _row_tile(rows, cols)

    def body(i_ref, *rest):
        o_ref = rest[-1]
        o_ref[...] = i_ref[...].astype(o_ref.dtype)

    out = pl.pallas_call(body, name="place_shard", grid=(rows // tr,),
                         in_specs=[pl.BlockSpec((None, tr, cols), lambda r: (layer, r, 0))] + _any_specs(len(deps)),
                         out_specs=pl.BlockSpec((None, tr, cols), lambda r: (_my_chip(), r, 0)),
                         out_shape=jax.ShapeDtypeStruct((N_CHIPS, rows, cols), dtype),
                         compiler_params=_params(("parallel",)))(w, *deps)
    return out.reshape(N_CHIPS, 2, rows // 2, cols)


def _adamw(w, gs, m, v):
    n_layers, rows, cols = w.shape
    tr = _row_tile(rows, cols, budget=1024 * 1024)

    def body(w_ref, m_ref, v_ref, *rest):
        g_refs = rest[:n_layers]
        go_ref, d_ref, mo_ref, vo_ref = rest[n_layers:]
        gv = g_refs[0][...]
        for layer in range(1, n_layers):
            gv = jnp.where(pl.program_id(0) == layer, g_refs[layer][...], gv)
        mn = ADAM_B1 * m_ref[...] + (1.0 - ADAM_B1) * gv
        vn = ADAM_B2 * v_ref[...] + (1.0 - ADAM_B2) * jnp.square(gv)
        m_hat = mn / (1.0 - ADAM_B1 ** ADAM_STEP)
        v_hat = vn / (1.0 - ADAM_B2 ** ADAM_STEP)
        d_ref[...] = -ADAM_LR * (m_hat / (jnp.sqrt(v_hat) + ADAM_EPS) + ADAM_WD * w_ref[...])
        go_ref[...] = gv
        mo_ref[...] = mn
        vo_ref[...] = vn

    spec = pl.BlockSpec((None, tr, cols), lambda layer, r: (layer, r, 0))
    g_specs = [pl.BlockSpec((tr, cols), lambda layer, r, own=own: (jnp.where(layer == own, r, 0), 0))
               for own in range(n_layers)]
    return pl.pallas_call(body, name="adamw", grid=(n_layers, rows // tr), in_specs=[spec] * 3 + g_specs,
                          out_specs=[spec] * 4, out_shape=[jax.ShapeDtypeStruct((n_layers, rows, cols), F32)] * 4,
                          compiler_params=_params(("parallel", "parallel")))(w, m, v, *gs)


def _pad_rope(w):
    z = jnp.zeros(w.shape[:-1] + (ROPE_HALF,), w.dtype)
    return jnp.concatenate([w[..., :ROPE_HALF], z, w[..., ROPE_HALF:], z], axis=-1)


def _unpad_rope(g):
    return jnp.concatenate([g[..., :ROPE_HALF], g[..., ROPE:ROPE + ROPE_HALF]], axis=-1)


def _unstack_cols(s):
    n, r, cs = s.shape
    return jnp.transpose(s, (1, 0, 2)).reshape(r, n * cs)


def _stack_cols(f):
    r, cfull = f.shape
    return jnp.transpose(f.reshape(r, N_CHIPS, cfull // N_CHIPS), (1, 0, 2))


def _small_shard(norm, conv):
    return jnp.concatenate([jnp.pad(norm, ((0, 15), (0, 0))), jnp.pad(conv, ((0, 13), (0, 0)))], axis=0)


def _flat_rows(a):
    return a.reshape(-1, LANES)


def _pack_small(arrs):
    return jnp.concatenate([_flat_rows(a.astype(F32)) for a in arrs], axis=0)


def _unpack_small(flat, like):
    out, r = [], 0
    for a in like:
        n = a.size // LANES
        out.append(flat[r:r + n].reshape(a.shape))
        r += n
    return out


def kernel(x, positions, e_norm_mix, e_w_in, e_q_norm, e_w_uq, e_kv_norm, e_w_ukv, e_v_norm, e_sgu_w, e_sgu_b, e_mla_out_norm, e_sgu_out_norm, e_w_out, o_norm_mix, o_w_in, o_conv_w, o_w_out, mlp_norm, mlp_w1, mlp_w2, final_norm, loss_target, m_e_norm_mix, m_e_w_in, m_e_q_norm, m_e_w_uq, m_e_kv_norm, m_e_w_ukv, m_e_v_norm, m_e_sgu_w, m_e_sgu_b, m_e_mla_out_norm, m_e_sgu_out_norm, m_e_w_out, m_o_norm_mix, m_o_w_in, m_o_conv_w, m_o_w_out, m_mlp_norm, m_mlp_w1, m_mlp_w2, m_final_norm, v_e_norm_mix, v_e_w_in, v_e_q_norm, v_e_w_uq, v_e_kv_norm, v_e_w_ukv, v_e_v_norm, v_e_sgu_w, v_e_sgu_b, v_e_mla_out_norm, v_e_sgu_out_norm, v_e_w_out, v_o_norm_mix, v_o_w_in, v_o_conv_w, v_o_w_out, v_mlp_norm, v_mlp_w1, v_mlp_w2, v_final_norm):
    t, d = x.shape[1], x.shape[2]
    ql, kvl = e_q_norm.shape[1], e_kv_norm.shape[1]
    groups = e_v_norm.shape[1]
    gw = groups * LANES
    heads = N_CHIPS * e_w_uq.shape[2] // (LANES + ROPE)
    hw = heads * LANES
    mix = hw + gw
    ei = N_CHIPS * e_w_in.shape[2]
    cd = N_CHIPS * o_conv_w.shape[2]
    ff = N_CHIPS * mlp_w1.shape[2]
    ffs = ff // N_CHIPS
    pi = 2 * gw + ql + kvl + LANES
    assert e_norm_mix.shape[0] == 1 and o_norm_mix.shape[0] == 1 and mlp_norm.shape[0] == 2
    assert ei == ql + kvl + ROPE + 2 * gw and cd == d and e_sgu_w.shape[2] == LANES
    assert (2 * gw) % ql == 0 and (2 * gw + ql) % kvl == 0 and t % LANES == 0
    scale = (LANES + ROPE) ** -0.5

    tr = min(256, t)
    tm = _pick(t, 1024, 8)
    kt, kd = _pick(t, 2048, 8), _pick(d, 2048)
    xs = x.reshape(t, d)
    tgt = loss_target.reshape(t, d)

    small_shard = _small_shard(o_norm_mix, o_conv_w[0])
    first, tok = _gather_start("gather_start_e", [
        [_place_shard(e_w_in, 0, BF16)],
        [_place_shard(e_w_uq, 0, BF16), _place_shard(e_w_ukv, 0, BF16), _place_shard(e_w_out, 0, BF16),
         _place_shard(small_shard[None], 0, F32)]])
    rest, tok = _gather_start("gather_start_rest", [
        [_place_shard(mlp_w1, 0, BF16, (tok,))], [_place_shard(mlp_w2, 0, BF16, (tok,))],
        [_place_shard(o_w_in, 0, BF16, (tok,)), _place_shard(o_w_out, 0, BF16, (tok,))],
        [_place_shard(mlp_w1, 1, BF16, (tok,))], [_place_shard(mlp_w2, 1, BF16, (tok,))]])
    started = first + rest

    def gathered(gi, tag, after):
        send, recv, bufs = started[gi]
        bufs = _gather_forward(tag, _gather_wait(tag, send, recv, bufs, after))
        return [b.reshape(N_CHIPS, 2 * b.shape[2], b.shape[3]) for b in bufs]

    g_e = e_norm_mix
    h0 = _norm_fwd("e_norm", xs, g_e, tr)
    inv_freq = ROPE_BASE ** (-jnp.arange(0, ROPE, 2, dtype=F32) / ROPE)
    zeros32 = jnp.zeros((ROPE_HALF,), F32)
    ones32 = jnp.ones((ROPE_HALF,), F32)
    invf = jnp.concatenate([inv_freq, zeros32, inv_freq, zeros32]).reshape(1, LANES)
    cmask = jnp.concatenate([ones32, zeros32, ones32, zeros32]).reshape(1, LANES)
    smask = jnp.concatenate([-ones32, zeros32, ones32, zeros32]).reshape(1, LANES)
    ctab, stab = _rope_tables(positions.reshape(t, 1).astype(F32), invf, cmask, smask, tr)

    w_in_g, = gathered(0, "e_in", (h0, ctab, tok))
    full = _unstack_cols(w_in_g)
    c2, c3 = ql + kvl, ql + kvl + ROPE
    w_in_all = jnp.concatenate([full[:, c3:], full[:, :c2], _pad_rope(full[:, c2:c3])], axis=1)
    proj, = _matmul("e_proj", Mat(h0, t, d), Mat(w_in_all, d, pi), "nn", [_out(t, pi, F32)], tm, _pick(pi, 1024), kd)

    w_uq_g, w_ukv_g, w_eout_g, small_g = gathered(1, "e", proj)
    full = _unstack_cols(w_uq_g).reshape(ql, heads, LANES + ROPE)
    w_q_all = jnp.concatenate([full[:, :, :LANES].reshape(ql, hw), _pad_rope(full[:, :, LANES:]).reshape(ql, hw)], axis=1)
    full = _unstack_cols(w_ukv_g).reshape(kvl, heads, 2 * LANES)
    w_kv_all = jnp.concatenate([full[:, :, :LANES].reshape(kvl, hw), full[:, :, LANES:].reshape(kvl, hw)], axis=1)
    w_eout = w_eout_g.reshape(mix, d)
    g_o = small_g[:, 0].reshape(1, d)
    conv_w = jnp.pad(jnp.transpose(small_g[:, 16:19], (1, 0, 2)).reshape(3, cd), ((0, 5), (0, 0)))

    g_q, g_kv = e_q_norm, e_kv_norm
    g_vn = e_v_norm.reshape(1, gw)
    sgu_w = e_sgu_w[0]
    sgu_b = jnp.broadcast_to(e_sgu_b[0][:, :, None], (groups, LANES, LANES))
    g_mla, g_sgu = e_mla_out_norm, e_sgu_out_norm
    g_m0, g_m1 = mlp_norm[0:1], mlp_norm[1:2]
    g_f = final_norm.reshape(1, d)

    def mlp_fwd(tag, xin, g, gi):
        hm = _norm_fwd("mlp_norm_" + tag, xin, g, tr)
        tn = _pick(ffs, 1024)
        w1 = Mat(gathered(gi, "w1_" + tag, hm)[0], d, ff, "colstack")
        a, act = _matmul("mlp_up_" + tag, Mat(hm, t, d), w1, "nn",
                         [_out(t, ff, BF16), _out(t, ff, BF16)], tm, tn, kd,
                         epilogue=lambda z: (jnp.maximum(z, 0.0), jnp.square(jnp.maximum(z, 0.0))))
        w2 = Mat(gathered(gi + 1, "w2_" + tag, act)[0].reshape(ff, d), ff, d)
        xo, = _matmul("mlp_down_" + tag, Mat(act, t, ff), w2, "nn",
                      [_out(t, d, F32)], tm, _pick(d, 1024), _pick(ffs, 2048),
                      epilogue=lambda z, r: (z + r,), extras=[Mat(xin, t, d)])
        return xo, hm, a, act, w1, w2

    def chip_start(tag, part):
        return _exchange_start("scatter_start_" + tag, _chip_route, 3 * len(part), part, [(3,) + p.shape[1:] for p in part])

    def pair_start(tag, stacked):
        g5 = [g.reshape(N_CHIPS, 2, g.shape[1] // 2, g.shape[2]) for g in stacked]
        return _exchange_start("pair_start_" + tag, _pair_route, N_CHIPS * len(g5), g5,
                               [(N_CHIPS,) + g.shape[2:] for g in g5])

    def pair_finish(tag, started, after):
        g5, from_sib = _exchange_wait("pair_wait_" + tag, _pair_route, started, after)
        return chip_start(tag, [_pair_sum(a, b) for a, b in zip(g5, from_sib)])

    def mlp_bwd(tag, dx, dxb, xin, g, w1, w2, hm, a, act, deps, extra_grads=()):
        tn = _pick(ffs, 1024)
        dz, = _matmul("mlp_dact_" + tag, Mat(dxb, t, d), w2, "nt",
                      [_out(t, ff, BF16)], tm, tn, kd,
                      epilogue=lambda z, av: (z * (2.0 * av.astype(F32)),), extras=[Mat(a, t, ff)], deps=deps)
        dw2, = _matmul("mlp_dw2_" + tag, Mat(act, t, ff), Mat(dxb, t, d), "tn",
                       [_out(ff, d, BF16)], tn, _pick(d, 1024), kt)
        dw1, = _matmul("mlp_dw1_" + tag, Mat(hm, t, d), Mat(dz, t, ff), "tn",
                       [_out(d, ff, BF16, "colstack", (), (N_CHIPS, d, ffs))], _pick(d, 1024), tn, kt)
        started, tok = pair_start("m" + tag, [dw1, dw2.reshape(N_CHIPS, ffs, d), *extra_grads])
        dhm, = _matmul("mlp_dh_" + tag, Mat(dz, t, ff), w1, "nt",
                       [_out(t, d, F32)], tm, _pick(d, 1024), _pick(ffs, 2048), deps=(tok,))
        dxo, dxob, dg = _norm_bwd("mlp_norm_bwd_" + tag, dhm, xin, g, dx, tr)
        sc, tok = pair_finish("m" + tag, started, dxo)
        return dxo, dxob, dg, sc, tok

    cq_cb, ckv_cb, kr_cb = 2 * gw // ql, (2 * gw + ql) // kvl, (2 * gw + ql + kvl) // LANES
    qn, kvn = _rowwise("qkv_norm", lambda a, b, ga, gb: (_rms(a, ga), _rms(b, gb)), t // tr,
                       [_rt(proj, tr, ql, cq_cb), _rt(proj, tr, kvl, ckv_cb), _whole(g_q), _whole(g_kv)],
                       [_rt_out(t, ql, BF16, tr), _rt_out(t, kvl, BF16, tr)])
    qfull, = _matmul("q_up", Mat(qn, t, ql), Mat(w_q_all, ql, 2 * hw), "nn", [_out(t, 2 * hw, F32)], tm, _pick(2 * hw, 1024), ql)
    kvall, = _matmul("kv_up", Mat(kvn, t, kvl), Mat(w_kv_all, kvl, 2 * hw), "nn", [_out(t, 2 * hw, BF16)], tm, _pick(2 * hw, 1024), kvl)
    qall, kr = _rope_fwd(qfull, proj, kr_cb, ctab, stab, heads, tr)
    att, lse, lse_row = _attn_fwd(qall, kvall, kr, heads, scale, tr)
    rb = min(2 * LANES, t)
    sgu = _sgu_fwd(proj, g_vn, sgu_w, sgu_b, groups, rb)
    mixed = _rowwise("mix_norm", lambda a, s, ga, gs: jnp.concatenate([_rms(a, ga), _rms(s, gs)], axis=1), t // tr,
                     [_rt(att, tr), _rt(sgu, tr), _whole(g_mla), _whole(g_sgu)], [_rt_out(t, mix, BF16, tr)])[0]
    x1, = _matmul("e_out", Mat(mixed, t, mix), Mat(w_eout, mix, d), "nn", [_out(t, d, F32)], tm, _pick(d, 1024), _pick(mix, 2048),
                  epilogue=lambda z, r: (z + r,), extras=[Mat(xs, t, d)])
    x2, hm0, a0, act0, w1_0, w2_0 = mlp_fwd("0", x1, g_m0, 2)

    w_oin_g, w_oout_g = gathered(4, "o", x2)
    w_oout = w_oout_g.reshape(cd, d)
    h1 = _norm_fwd("o_norm", x2, g_o, tr)
    oin = Mat(w_oin_g, d, 3 * cd, "colstack")
    tn_o = _pick(_gcd(3 * cd // N_CHIPS, cd), 512)
    proj3, = _matmul("o_proj", Mat(h1, t, d), oin, "nn", [_out(t, 3 * cd, F32, "colstack", (), (3, t, cd))], tm, tn_o, kd)
    tc = _pick(cd, 256)
    bz = _conv_fwd(proj3, conv_w, tc)
    x3, = _matmul("o_out", Mat(bz, t, cd), Mat(w_oout, cd, d), "nn", [_out(t, d, F32)], tm, _pick(d, 1024), _pick(cd, 2048),
                  epilogue=lambda z, r: (z + r,), extras=[Mat(x2, t, d)])
    x4, hm1, a1, act1, w1_1, w2_1 = mlp_fwd("1", x3, g_m1, 5)

    def final_fn(xv, gv, tv):
        r = lax.rsqrt(jnp.mean(xv * xv, axis=-1, keepdims=True) + EPS)
        xh = xv * r
        err = xh * gv - tv
        dy = err * (1.0 / d)
        dxh = dy * gv
        dx = r * (dxh - xh * jnp.mean(dxh * xh, axis=-1, keepdims=True))
        sq = jnp.sum(err * err, axis=0, keepdims=True)
        part = sq[:, :LANES]
        for k in range(1, d // LANES):
            part = part + sq[:, k * LANES:(k + 1) * LANES]
        return dx, dx, part, jnp.sum(dy * xh, axis=0, keepdims=True)

    dx4, dx4b, loss_vec, dg_f = _rowwise("loss_final_norm", final_fn, t // tr, [_rt(x4, tr), _whole(g_f), _rt(tgt, tr)],
                                         [_rt_out(t, d, F32, tr), _rt_out(t, d, BF16, tr)],
                                         [jax.ShapeDtypeStruct((1, LANES), F32), jax.ShapeDtypeStruct((1, d), F32)])
    loss = lax.psum(0.5 * jnp.sum(loss_vec) / d, ("x", "y", "c"))

    dx3, dx3b, dg_m1, sc_m1, tok = mlp_bwd("1", dx4, dx4b, x3, g_m1, w1_1, w2_1, hm1, a1, act1, ())

    dbz, = _matmul("o_out_dx", Mat(dx3b, t, d), Mat(w_oout, cd, d), "nt", [_out(t, cd, F32)], tm, _pick(cd, 1024), kd,
                   deps=(tok,))
    dw_oout, = _matmul("o_out_dw", Mat(bz, t, cd), Mat(dx3b, t, d), "tn", [_out(cd, d, BF16)], _pick(cd, 1024), _pick(d, 1024), kt)
    dproj3, dconv = _conv_bwd(proj3, conv_w, dbz, tc)
    dp3 = Mat(dproj3, t, 3 * cd, "colstack")
    dw_oin, = _matmul("o_proj_dw", Mat(h1, t, d), dp3, "tn", [_out(d, 3 * cd, BF16, "colstack", (), (N_CHIPS, d, 3 * cd // N_CHIPS))],
                      _pick(d, 1024), tn_o, kt)
    started_o, tok = pair_start("o", [dw_oin, dw_oout.reshape(N_CHIPS, cd // N_CHIPS, d)])
    dh1, = _matmul("o_proj_dx", dp3, oin, "nt", [_out(t, d, F32)], tm, _pick(d, 1024), tn_o, deps=(tok,))
    dx2, dx2b, dg_o = _norm_bwd("o_norm_bwd", dh1, x2, g_o, dx3, tr)
    sc_o, tok = pair_finish("o", started_o, dx2)

    dconv_s = jnp.transpose(dconv[:3].reshape(3, N_CHIPS, cd // N_CHIPS), (1, 0, 2))
    gsmall = jnp.concatenate([jnp.pad(dg_o.reshape(N_CHIPS, 1, d // N_CHIPS), ((0, 0), (0, 15), (0, 0))),
                              jnp.pad(dconv_s, ((0, 0), (0, 13), (0, 0)))], axis=1)
    dx1, dx1b, dg_m0, sc_m0, tok = mlp_bwd("0", dx2, dx2b, x1, g_m0, w1_0, w2_0, hm0, a0, act0, (tok,), (gsmall,))

    dmixed, = _matmul("e_out_dx", Mat(dx1b, t, d), Mat(w_eout, mix, d), "nt", [_out(t, mix, F32)], tm, _pick(mix, 1024), kd,
                      deps=(tok,))
    dw_eout, = _matmul("e_out_dw", Mat(mixed, t, mix), Mat(dx1b, t, d), "tn", [_out(mix, d, BF16)], _pick(mix, 1024), _pick(d, 1024), kt)

    def mixb_fn(dm, a, s, ga, gs):
        da, dga = _rms_bwd(dm[:, :hw], a, ga)
        dsg, dgs = _rms_bwd(dm[:, hw:], s, gs)
        prod = da * a
        cols = [jnp.broadcast_to(jnp.sum(prod[:, h * LANES:(h + 1) * LANES], axis=-1, keepdims=True), (tr, LANES))
                for h in range(heads)]
        return da, dsg, jnp.stack(cols, axis=0), jnp.stack([_row_of(c) for c in cols], axis=0), dga, dgs

    da_b, dsgu, delta, delta_row, dg_mla, dg_sgu = _rowwise(
        "mix_norm_bwd", mixb_fn, t // tr, [_rt(dmixed, tr), _rt(att, tr), _rt(sgu, tr), _whole(g_mla), _whole(g_sgu)],
        [_rt_out(t, hw, BF16, tr), _rt_out(t, gw, F32, tr),
         (jax.ShapeDtypeStruct((heads, t, LANES), F32), pl.BlockSpec((heads, tr, LANES), lambda i: (0, i, 0))),
         (jax.ShapeDtypeStruct((heads, 8, t), F32), pl.BlockSpec((heads, 8, tr), lambda i: (0, 0, i)))],
        [jax.ShapeDtypeStruct((1, hw), F32), jax.ShapeDtypeStruct((1, gw), F32)])

    du, dv, dsgu_w, dsgu_b8, dg_vn = _sgu_bwd(proj, dsgu, g_vn, sgu_w, sgu_b, groups, rb)
    dq1, dq2 = _attn_dq(qall, kvall, kr, da_b, lse, delta, heads, scale, tr)
    dk1, dvv, dkr_h = _attn_dkv(qall, kvall, kr, da_b, lse_row, delta_row, heads, scale, tr)
    dqfull, dkr = _rope_bwd(dq1, dq2, dkr_h, ctab, stab, heads, tr)
    dkvall = jnp.concatenate([dk1, dvv], axis=1)
    dw_q, = _matmul("q_up_dw", Mat(qn, t, ql), Mat(dqfull, t, 2 * hw), "tn", [_out(ql, 2 * hw, BF16)], ql, _pick(2 * hw, 1024), kt)
    dqn, = _matmul("q_up_dx", Mat(dqfull, t, 2 * hw), Mat(w_q_all, ql, 2 * hw), "nt", [_out(t, ql, F32)], tm, ql, _pick(2 * hw, 2048))
    dw_kv, = _matmul("kv_up_dw", Mat(kvn, t, kvl), Mat(dkvall, t, 2 * hw), "tn", [_out(kvl, 2 * hw, BF16)], kvl, _pick(2 * hw, 1024), kt)
    dkvn, = _matmul("kv_up_dx", Mat(dkvall, t, 2 * hw), Mat(w_kv_all, kvl, 2 * hw), "nt", [_out(t, kvl, F32)], tm, kvl, _pick(2 * hw, 2048))

    def qkvb_fn(da, db, a, b, ga, gb):
        dxa, dga = _rms_bwd(da, a, ga)
        dxb, dgb = _rms_bwd(db, b, gb)
        return dxa, dxb, dga, dgb

    dcq, dckv, dg_q, dg_kv = _rowwise(
        "qkv_norm_bwd", qkvb_fn, t // tr,
        [_rt(dqn, tr), _rt(dkvn, tr), _rt(proj, tr, ql, cq_cb), _rt(proj, tr, kvl, ckv_cb), _whole(g_q), _whole(g_kv)],
        [_rt_out(t, ql, BF16, tr), _rt_out(t, kvl, BF16, tr)],
        [jax.ShapeDtypeStruct((1, ql), F32), jax.ShapeDtypeStruct((1, kvl), F32)])
    dproj = jnp.concatenate([du, dv, dcq, dckv, dkr], axis=1)
    dw_in, = _matmul("e_proj_dw", Mat(h0, t, d), Mat(dproj, t, pi), "tn", [_out(d, pi, BF16)], _pick(d, 1024), _pick(pi, 1024), kt)
    dh0, = _matmul("e_proj_dx", Mat(dproj, t, pi), Mat(w_in_all, d, pi), "nt", [_out(t, d, F32)], tm, _pick(d, 1024), _pick(pi, 4096))
    dx0, _, dg_e = _norm_bwd("e_norm_bwd", dh0, xs, g_e, dx1, tr)

    gfull = jnp.concatenate([dw_in[:, 2 * gw:2 * gw + c2], _unpad_rope(dw_in[:, 2 * gw + c2:]), dw_in[:, :2 * gw]], axis=1)
    gw_in = _stack_cols(gfull)
    gq = jnp.concatenate([dw_q[:, :hw].reshape(ql, heads, LANES), _unpad_rope(dw_q[:, hw:].reshape(ql, heads, LANES))], axis=-1)
    gw_uq = _stack_cols(gq.reshape(ql, heads * (LANES + ROPE)))
    gkv = jnp.concatenate([dw_kv[:, :hw].reshape(kvl, heads, LANES), dw_kv[:, hw:].reshape(kvl, heads, LANES)], axis=-1)
    gw_ukv = _stack_cols(gkv.reshape(kvl, heads * 2 * LANES))
    started_e, tok_pair = pair_start("e", [gw_in, gw_uq, gw_ukv, dw_eout.reshape(N_CHIPS, mix // N_CHIPS, d)])

    small_like = [e_norm_mix, e_q_norm, e_kv_norm, e_v_norm, e_sgu_w, e_sgu_b, e_mla_out_norm, e_sgu_out_norm, mlp_norm, final_norm]
    small_grads = [dg_e, dg_q, dg_kv, dg_vn, dsgu_w, dsgu_b8[:, 0, :], dg_mla, dg_sgu, jnp.concatenate([dg_m0, dg_m1], axis=0), dg_f]
    sflat = _pack_small(small_grads)
    pad = (-sflat.shape[0]) % 8
    sflat = jnp.pad(sflat, ((0, pad), (0, 0)))
    small_started, tok_small = _exchange_start("small_start", _all_route, 7, [sflat], [_spread(sflat)])

    def summed(tag, sc, after):
        part, lands = _exchange_wait("scatter_wait_" + tag, _chip_route, sc, after)
        half = [_chip_sum(p, r) for p, r in zip(part, lands)]
        return _exchange_start("share_start_" + tag, _share_route, len(half), half, [])

    def shared(tag, started, after):
        bufs, _ = _exchange_wait("share_wait_" + tag, _share_route, started, after)
        return [r.reshape(2 * r.shape[1], r.shape[2]) for r in bufs]

    sh_m1, tok = summed("m1", sc_m1, (tok_pair, tok_small))
    sc_e, tok = pair_finish("e", started_e, tok)
    sh_o, tok = summed("o", sc_o, tok)
    sh_m0, tok = summed("m0", sc_m0, tok)
    r_oin, r_oout = shared("o", sh_o, tok)
    late = {"o_w_in": _adamw(o_w_in, [r_oin], m_o_w_in, v_o_w_in),
            "o_w_out": _adamw(o_w_out, [r_oout], m_o_w_out, v_o_w_out)}
    r_w1_1, r_w2_1 = shared("m1", sh_m1, late["o_w_in"][1])
    r_w1_0, r_w2_0, r_small = shared("m0", sh_m0, r_w2_1)
    late["mlp_w1"] = _adamw(mlp_w1, [r_w1_0, r_w1_1], m_mlp_w1, v_mlp_w1)
    late["mlp_w2"] = _adamw(mlp_w2, [r_w2_0, r_w2_1], m_mlp_w2, v_mlp_w2)

    _, (all_small,) = _exchange_wait("small_wait", _all_route, small_started, late["mlp_w2"][1])
    g_small = _sum_devices(all_small)

    def padded(arrs):
        return jnp.pad(_pack_small(arrs), ((0, pad), (0, 0)))

    s_m = [m_e_norm_mix, m_e_q_norm, m_e_kv_norm, m_e_v_norm, m_e_sgu_w, m_e_sgu_b, m_e_mla_out_norm, m_e_sgu_out_norm, m_mlp_norm, m_final_norm]
    s_v = [v_e_norm_mix, v_e_q_norm, v_e_kv_norm, v_e_v_norm, v_e_sgu_w, v_e_sgu_b, v_e_mla_out_norm, v_e_sgu_out_norm, v_mlp_norm, v_final_norm]
    s_out = [_unpack_small(o[0], small_like)
             for o in _adamw(padded(small_like)[None], [g_small], padded(s_m)[None], padded(s_v)[None])]

    sm = [o[0] for o in _adamw(small_shard[None], [r_small], _small_shard(m_o_norm_mix, m_o_conv_w[0])[None],
                               _small_shard(v_o_norm_mix, v_o_conv_w[0])[None])]

    sh_e, tok = summed("e", sc_e, late["mlp_w2"][1])
    r_in, r_uq, r_ukv, r_eout = shared("e", sh_e, tok)
    big = dict(late)
    flip = lambda a: jnp.swapaxes(a, 1, 2)
    big.update({
        "e_w_in": [flip(o) for o in _adamw(flip(e_w_in), [r_in.T], flip(m_e_w_in), flip(v_e_w_in))],
        "e_w_uq": _adamw(e_w_uq, [r_uq], m_e_w_uq, v_e_w_uq),
        "e_w_ukv": _adamw(e_w_ukv, [r_ukv], m_e_w_ukv, v_e_w_ukv),
        "e_w_out": _adamw(e_w_out, [r_eout], m_e_w_out, v_e_w_out),
    })

    names = ["e_norm_mix", "e_w_in", "e_q_norm", "e_w_uq", "e_kv_norm", "e_w_ukv", "e_v_norm", "e_sgu_w", "e_sgu_b",
             "e_mla_out_norm", "e_sgu_out_norm", "e_w_out", "o_norm_mix", "o_w_in", "o_conv_w", "o_w_out",
             "mlp_norm", "mlp_w1", "mlp_w2", "final_norm"]
    shapes = {"e_w_in": e_w_in.shape, "e_w_uq": e_w_uq.shape, "e_w_ukv": e_w_ukv.shape, "e_w_out": e_w_out.shape,
              "o_w_in": o_w_in.shape, "o_w_out": o_w_out.shape, "mlp_w1": mlp_w1.shape, "mlp_w2": mlp_w2.shape}
    small_names = ["e_norm_mix", "e_q_norm", "e_kv_norm", "e_v_norm", "e_sgu_w", "e_sgu_b", "e_mla_out_norm",
                   "e_sgu_out_norm", "mlp_norm", "final_norm"]

    def leaf(kind, name):
        if name in big:
            return big[name][kind].reshape(shapes[name])
        if name == "o_norm_mix":
            return sm[kind][0:1]
        if name == "o_conv_w":
            return sm[kind][16:19].reshape(o_conv_w.shape)
        return s_out[kind][small_names.index(name)]

    outs = [loss, dx0.reshape(x.shape)]
    for kind in range(4):
        outs += [leaf(kind, nm) for nm in names]
    return tuple(outs)


def _gcd(a, b):
    while b:
        a, b = b, a % b
    return a
```

```python
import functools

import jax
import jax.numpy as jnp
from jax import lax
from jax.experimental import pallas as pl
from jax.experimental.pallas import tpu as pltpu

F32 = jnp.float32
BF16 = jnp.bfloat16
MESH = pl.DeviceIdType.MESH

LANES = 128
ROPE = 64
ROPE_HALF = ROPE // 2
ROPE_BASE = 10000.0
EPS = 1e-6
N_CHIPS = 4
VMEM_LIMIT = 48 * 1024 * 1024
NEG = -1e30

ADAM_LR = 0.001
ADAM_B1 = 0.9
ADAM_B2 = 0.999
ADAM_EPS = 1e-08
ADAM_WD = 0.01
ADAM_STEP = 10


def _pick(n, target, step=LANES):
    best = None
    for t in range(step, min(n, target) + 1, step):
        if n % t == 0:
            best = t
    return best if best is not None else n


def _params(sem, vmem=VMEM_LIMIT):
    return pltpu.CompilerParams(dimension_semantics=sem, vmem_limit_bytes=vmem)


class Mat:
    def __init__(self, arr, rows, cols, kind="plain", lead=(), col_off=0, shape=None, dtype=None):
        self.arr, self.rows, self.cols, self.kind, self.lead, self.col_off = arr, rows, cols, kind, tuple(lead), col_off
        self.shape = tuple(arr.shape) if arr is not None else tuple(shape)
        self.dtype = arr.dtype if arr is not None else dtype

    def sds(self):
        return jax.ShapeDtypeStruct(self.shape, self.dtype)

    def spec(self, br, bc, gridmap):
        lead, nl = self.lead, len(self.lead)
        if self.kind == "plain":
            assert self.col_off % bc == 0 and self.rows % br == 0 and self.cols % bc == 0, (self.shape, br, bc)
            off = self.col_off // bc
            block = (None,) * nl + (br, bc)

            def phys(rb, cb):
                return lead + (rb, cb + off)
        elif self.kind == "colstack":
            cs = self.shape[-1]
            assert cs % bc == 0 and self.rows % br == 0, (self.shape, br, bc)
            q = cs // bc
            block = (None,) * (nl + 1) + (br, bc)

            def phys(rb, cb):
                return (cb // q,) + lead + (rb, cb % q)
        else:
            rs = self.shape[-2]
            assert rs % br == 0 and self.cols % bc == 0, (self.shape, br, bc)
            q = rs // br
            block = (None,) * (nl + 1) + (br, bc)

            def phys(rb, cb):
                return (rb // q,) + lead + (rb % q, cb)

        return pl.BlockSpec(block, lambda *g: phys(*gridmap(*g)))


def _matmul(name, a, b, mode, outs, tm, tn, tk, epilogue=None, extras=(), deps=()):
    if mode == "nn":
        m, k, n = a.rows, a.cols, b.cols
        a_spec = a.spec(tm, tk, lambda i, j, kk: (i, kk))
        b_spec = b.spec(tk, tn, lambda i, j, kk: (kk, j))
        dims = (((1,), (0,)), ((), ()))
    elif mode == "nt":
        m, k, n = a.rows, a.cols, b.rows
        a_spec = a.spec(tm, tk, lambda i, j, kk: (i, kk))
        b_spec = b.spec(tn, tk, lambda i, j, kk: (j, kk))
        dims = (((1,), (1,)), ((), ()))
    else:
        k, m, n = a.rows, a.cols, b.cols
        a_spec = a.spec(tk, tm, lambda i, j, kk: (kk, i))
        b_spec = b.spec(tk, tn, lambda i, j, kk: (kk, j))
        dims = (((0,), (0,)), ((), ()))
    assert m % tm == 0 and n % tn == 0 and k % tk == 0, (name, m, n, k, tm, tn, tk)
    grid = (m // tm, n // tn, k // tk)
    nk = grid[2]
    n_ex, n_out, n_dep = len(extras), len(outs), len(deps)
    tile = lambda i, j, kk: (i, j)

    def finish(z, ex, out_refs):
        vals = epilogue(z, *[e[...] for e in ex]) if epilogue is not None else (z,)
        for o, v in zip(out_refs, vals):
            o[...] = v.astype(o.dtype)

    def body_single(a_ref, b_ref, *rest):
        finish(lax.dot_general(a_ref[...], b_ref[...], dims, preferred_element_type=F32),
               rest[:n_ex], rest[n_ex + n_dep:n_ex + n_dep + n_out])

    def body_acc(a_ref, b_ref, *rest):
        acc = rest[-1]
        kk = pl.program_id(2)

        @pl.when(kk == 0)
        def _():
            acc[...] = jnp.zeros_like(acc)

        acc[...] += lax.dot_general(a_ref[...], b_ref[...], dims, preferred_element_type=F32)

        @pl.when(kk == nk - 1)
        def _():
            finish(acc[...], rest[:n_ex], rest[n_ex + n_dep:n_ex + n_dep + n_out])

    res = pl.pallas_call(
        body_single if nk == 1 else body_acc, name=name, grid=grid,
        in_specs=[a_spec, b_spec] + [e.spec(tm, tn, tile) for e in extras]
        + [pl.BlockSpec(memory_space=pl.ANY) for _ in deps],
        out_specs=[o.spec(tm, tn, tile) for o in outs],
        out_shape=[o.sds() for o in outs],
        scratch_shapes=[] if nk == 1 else [pltpu.VMEM((tm, tn), F32)],
        compiler_params=_params(("parallel", "parallel", "arbitrary")),
    )(a.arr, b.arr, *[e.arr for e in extras], *deps)
    return res


def _out(rows, cols, dtype, kind="plain", lead=(), shape=None):
    return Mat(None, rows, cols, kind, lead, shape=shape if shape is not None else (rows, cols), dtype=dtype)


def _rt(arr, tr, width=None, cb=0):
    width = arr.shape[1] if width is None else width
    return arr, pl.BlockSpec((tr, width), lambda i: (i, cb))


def _whole(arr):
    nd = arr.ndim
    return arr, pl.BlockSpec(arr.shape, lambda i: (0,) * nd)


def _rowwise(name, fn, n_steps, ins, outs, accs=(), deps=()):
    n_in, n_out, n_acc, n_dep = len(ins), len(outs), len(accs), len(deps)

    def body(*refs):
        vals = fn(*[r[...] for r in refs[:n_in]])
        if not isinstance(vals, (tuple, list)):
            vals = (vals,)
        for ref, v in zip(refs[n_in + n_dep:n_in + n_dep + n_out], vals[:n_out]):
            ref[...] = v.astype(ref.dtype)
        if n_acc:
            acc_refs = refs[n_in + n_dep + n_out:]

            @pl.when(pl.program_id(0) == 0)
            def _():
                for ref in acc_refs:
                    ref[...] = jnp.zeros_like(ref)

            for ref, v in zip(acc_refs, vals[n_out:]):
                ref[...] += v

    acc_specs = [pl.BlockSpec(s.shape, lambda i, nd=len(s.shape): (0,) * nd) for s in accs]
    res = pl.pallas_call(
        body, name=name, grid=(n_steps,),
        in_specs=[s for _, s in ins] + [pl.BlockSpec(memory_space=pl.ANY) for _ in deps],
        out_specs=[s for _, s in outs] + acc_specs,
        out_shape=[o for o, _ in outs] + list(accs),
        compiler_params=_params(("arbitrary",) if n_acc else ("parallel",)),
    )(*[a for a, _ in ins], *deps)
    return res


def _rt_out(t, width, dtype, tr):
    return jax.ShapeDtypeStruct((t, width), dtype), pl.BlockSpec((tr, width), lambda i: (i, 0))


def _rms(x, g):
    r = lax.rsqrt(jnp.mean(x * x, axis=-1, keepdims=True) + EPS)
    return x * r * g


def _rms_bwd(dy, x, g):
    r = lax.rsqrt(jnp.mean(x * x, axis=-1, keepdims=True) + EPS)
    xh = x * r
    dxh = dy * g
    dx = r * (dxh - xh * jnp.mean(dxh * xh, axis=-1, keepdims=True))
    dg = jnp.sum(dy * xh, axis=0, keepdims=True)
    return dx, dg


def _gelu(x):
    k = 0.7978845608028654
    th = jnp.tanh(k * (x + 0.044715 * (x * x * x)))
    return x * (0.5 * (1.0 + th))


def _gelu_grad(x):
    k = 0.7978845608028654
    x2 = x * x
    th = jnp.tanh(k * (x + 0.044715 * (x2 * x)))
    return 0.5 * (1.0 + th) + 0.5 * x * (1.0 - th * th) * (k * (1.0 + 3.0 * 0.044715 * x2))


def _norm_fwd(name, x, g, tr):
    t, d = x.shape
    return _rowwise(name, lambda xv, gv: _rms(xv, gv), t // tr, [_rt(x, tr), _whole(g)], [_rt_out(t, d, BF16, tr)])[0]


def _norm_bwd(name, dh, x, g, dres, tr):
    t, d = x.shape

    def fn(dhv, xv, gv, drv):
        dx, dg = _rms_bwd(dhv, xv, gv)
        dx = dx + drv
        return dx, dx, dg

    return _rowwise(name, fn, t // tr, [_rt(dh, tr), _rt(x, tr), _whole(g), _rt(dres, tr)],
                    [_rt_out(t, d, F32, tr), _rt_out(t, d, BF16, tr)], [jax.ShapeDtypeStruct((1, d), F32)])


def _rope_tables(posf, invf, cmask, smask, tr):
    t = posf.shape[0]

    def fn(p, f, cm, sm):
        ang = p * f
        return jnp.cos(ang) * cm, jnp.sin(ang) * sm

    return _rowwise("rope_tables", fn, t // tr, [_rt(posf, tr), _whole(invf), _whole(cmask), _whole(smask)],
                    [_rt_out(t, LANES, F32, tr), _rt_out(t, LANES, F32, tr)])


def _rot(v, c, s):
    return v * c + pltpu.roll(v, ROPE, axis=1) * s


def _rot_bwd(dv, c, s):
    return dv * c + pltpu.roll(dv * s, ROPE, axis=1)


def _rope_fwd(qfull, proj, kr_cb, ctab, stab, heads, tr):
    t = qfull.shape[0]
    hw = heads * LANES

    def fn(q, kr, c, s):
        parts = [q[:, :hw]] + [_rot(q[:, hw + h * LANES: hw + (h + 1) * LANES], c, s) for h in range(heads)]
        return jnp.concatenate(parts, axis=1), _rot(kr, c, s)

    return _rowwise("rope_fwd", fn, t // tr, [_rt(qfull, tr), _rt(proj, tr, LANES, kr_cb), _rt(ctab, tr), _rt(stab, tr)],
                    [_rt_out(t, 2 * hw, BF16, tr), _rt_out(t, LANES, BF16, tr)])


def _rope_bwd(dq1, dq2, dkr_h, ctab, stab, heads, tr):
    t = dq1.shape[0]
    hw = heads * LANES

    def fn(a, b, dk, c, s):
        parts = [a] + [_rot_bwd(b[:, h * LANES:(h + 1) * LANES], c, s) for h in range(heads)]
        dks = dk[0]
        for h in range(1, heads):
            dks = dks + dk[h]
        return jnp.concatenate(parts, axis=1), _rot_bwd(dks, c, s)

    dk_spec = pl.BlockSpec((heads, tr, LANES), lambda i: (0, i, 0))
    return _rowwise("rope_bwd", fn, t // tr, [_rt(dq1, tr), _rt(dq2, tr), (dkr_h, dk_spec), _rt(ctab, tr), _rt(stab, tr)],
                    [_rt_out(t, 2 * hw, BF16, tr), _rt_out(t, LANES, BF16, tr)])


def _dot_nt(a, b):
    return lax.dot_general(a, b, (((1,), (1,)), ((), ())), preferred_element_type=F32)


def _dot_tn(a, b):
    return lax.dot_general(a, b, (((0,), (0,)), ((), ())), preferred_element_type=F32)


def _dot(a, b):
    return jnp.dot(a, b, preferred_element_type=F32)


def _ranges(n_blocks):
    n_var = min(4, n_blocks)
    assert n_blocks % n_var == 0
    return n_var, n_blocks // n_var


def _row_of(col):
    return col.T[:8, :]


def _attn_fwd(qall, kvall, kr, heads, scale, tq):
    t = qall.shape[0]
    nq = t // tq
    n_var, per = _ranges(nq)

    def body(qn_ref, qr_ref, kn_ref, v_ref, kr_ref, o_ref, lser_ref):
        i = pl.program_id(1)
        for var in range(n_var):
            kv = (var + 1) * per * tq

            @pl.when(jnp.logical_and(i >= var * per, i < (var + 1) * per))
            def _(kv=kv):
                s = (_dot_nt(qn_ref[...], kn_ref[:kv, :]) + _dot_nt(qr_ref[...], kr_ref[:kv, :])) * scale
                rows = i * tq + lax.broadcasted_iota(jnp.int32, (tq, kv), 0)
                cols = lax.broadcasted_iota(jnp.int32, (tq, kv), 1)
                s = jnp.where(cols <= rows, s, NEG)
                m = jnp.max(s, axis=-1, keepdims=True)
                p = jnp.exp(s - m)
                l = jnp.sum(p, axis=-1, keepdims=True)
                o_ref[...] = _dot(p.astype(BF16), v_ref[:kv, :]) / l
                lser_ref[...] = _row_of(jnp.broadcast_to(m + jnp.log(l), (tq, LANES)))

    return pl.pallas_call(
        body, name="attn_fwd", grid=(heads, nq),
        in_specs=[pl.BlockSpec((tq, LANES), lambda h, i: (i, h)),
                  pl.BlockSpec((tq, LANES), lambda h, i: (i, heads + h)),
                  pl.BlockSpec((t, LANES), lambda h, i: (0, h)),
                  pl.BlockSpec((t, LANES), lambda h, i: (0, heads + h)),
                  pl.BlockSpec((t, LANES), lambda h, i: (0, 0))],
        out_specs=[pl.BlockSpec((tq, LANES), lambda h, i: (i, h)),
                   pl.BlockSpec((None, 8, tq), lambda h, i: (h, 0, i))],
        out_shape=[jax.ShapeDtypeStruct((t, heads * LANES), F32), jax.ShapeDtypeStruct((heads, 8, t), F32)],
        compiler_params=_params(("parallel", "parallel")),
    )(qall, qall, kvall, kvall, kr)


def _attn_bwd(qall, kvall, kr, do, lse_row, delta_row, heads, scale, tk):
    t = qall.shape[0]
    nk = t // tk
    n_var, per = _ranges(nk)

    def body(qn_ref, qr_ref, kn_ref, v_ref, kr_ref, do_ref, lse_ref, dl_ref, dq1_ref, dq2_ref, dk_ref, dv_ref, dkr_ref):
        j = pl.program_id(1)

        @pl.when(j == 0)
        def _():
            dq1_ref[...] = jnp.zeros_like(dq1_ref)
            dq2_ref[...] = jnp.zeros_like(dq2_ref)

        for var in range(n_var):
            q0 = var * per * tk
            nq = t - q0

            @pl.when(jnp.logical_and(j >= var * per, j < (var + 1) * per))
            def _(q0=q0, nq=nq):
                qn, qr, do_v = qn_ref[q0:, :], qr_ref[q0:, :], do_ref[q0:, :]
                k1, k2 = kn_ref[...], kr_ref[...]
                st = (_dot_nt(k1, qn) + _dot_nt(k2, qr)) * scale
                keys = j * tk + lax.broadcasted_iota(jnp.int32, (tk, nq), 0)
                queries = q0 + lax.broadcasted_iota(jnp.int32, (tk, nq), 1)
                pt = jnp.where(keys <= queries, jnp.exp(st - lse_ref[0:1, q0:]), 0.0)
                dpt = _dot_nt(v_ref[...], do_v)
                dst = (pt * (dpt - dl_ref[0:1, q0:]) * scale).astype(BF16)
                dv_ref[...] = _dot(pt.astype(BF16), do_v).astype(dv_ref.dtype)
                dk_ref[...] = _dot(dst, qn).astype(dk_ref.dtype)
                dkr_ref[...] = _dot(dst, qr)
                dq1_ref[q0:, :] += _dot_tn(dst, k1)
                dq2_ref[q0:, :] += _dot_tn(dst, k2)

    kblk = lambda off: pl.BlockSpec((tk, LANES), lambda h, j: (j, off + h))
    full = lambda off: pl.BlockSpec((t, LANES), lambda h, j: (0, off + h))
    stat = pl.BlockSpec((None, 8, t), lambda h, j: (h, 0, 0))
    return pl.pallas_call(
        body, name="attn_bwd", grid=(heads, nk),
        in_specs=[full(0), full(heads), kblk(0), kblk(heads), pl.BlockSpec((tk, LANES), lambda h, j: (j, 0)),
                  full(0), stat, stat],
        out_specs=[full(0), full(0), kblk(0), kblk(0), pl.BlockSpec((None, tk, LANES), lambda h, j: (h, j, 0))],
        out_shape=[jax.ShapeDtypeStruct((t, heads * LANES), F32)] * 2 + [jax.ShapeDtypeStruct((t, heads * LANES), BF16)] * 2
        + [jax.ShapeDtypeStruct((heads, t, LANES), F32)],
        compiler_params=_params(("parallel", "arbitrary")),
    )(qall, qall, kvall, kvall, kr, do, lse_row, delta_row)


def _tril():
    return lax.broadcasted_iota(jnp.int32, (LANES, LANES), 0) >= lax.broadcasted_iota(jnp.int32, (LANES, LANES), 1)


def _group_norm(vg):
    mu = jnp.mean(vg, axis=-1, keepdims=True)
    vc = vg - mu
    rs = lax.rsqrt(jnp.mean(vc * vc, axis=-1, keepdims=True) + EPS)
    return vc * rs, rs


def _sgu_fwd(proj, gain, w, bias, groups, rb):
    t = proj.shape[0]
    gw = groups * LANES
    cpb = rb // LANES

    def body(u_ref, v_ref, gain_ref, w_ref, b_ref, s_ref):
        tril = _tril()
        for g in range(groups):
            wt = jnp.where(tril, w_ref[g], 0.0).astype(BF16)
            cols = slice(g * LANES, (g + 1) * LANES)
            for ci in range(cpb):
                rows = slice(ci * LANES, (ci + 1) * LANES)
                ug = _gelu(u_ref[rows, cols])
                vh, _ = _group_norm(_gelu(v_ref[rows, cols]))
                vn = vh * gain_ref[:, cols]
                y = _dot(wt, vn.astype(BF16)) + b_ref[g]
                s_ref[rows, cols] = ug * y

    return pl.pallas_call(
        body, name="sgu_fwd", grid=(t // rb,),
        in_specs=[pl.BlockSpec((rb, gw), lambda i: (i, 0)), pl.BlockSpec((rb, gw), lambda i: (i, 1)),
                  pl.BlockSpec((1, gw), lambda i: (0, 0)),
                  pl.BlockSpec((groups, LANES, LANES), lambda i: (0, 0, 0)),
                  pl.BlockSpec((groups, LANES, LANES), lambda i: (0, 0, 0))],
        out_specs=pl.BlockSpec((rb, gw), lambda i: (i, 0)),
        out_shape=jax.ShapeDtypeStruct((t, gw), F32),
        compiler_params=_params(("parallel",)),
    )(proj, proj, gain, w, bias)


def _sgu_bwd(proj, ds, gain, w, bias, groups, rb):
    t = proj.shape[0]
    gw = groups * LANES
    cpb = rb // LANES
    n_steps = t // rb

    def body(u_ref, v_ref, ds_ref, gain_ref, w_ref, b_ref, du_ref, dv_ref, dw_ref, db_ref, dg_ref, dy_acc):
        step = pl.program_id(0)

        @pl.when(step == 0)
        def _():
            dw_ref[...] = jnp.zeros_like(dw_ref)
            dy_acc[...] = jnp.zeros_like(dy_acc)
            dg_ref[...] = jnp.zeros_like(dg_ref)

        tril = _tril()
        for g in range(groups):
            wt = jnp.where(tril, w_ref[g], 0.0).astype(BF16)
            cols = slice(g * LANES, (g + 1) * LANES)
            gain_g = gain_ref[:, cols]
            for ci in range(cpb):
                rows = slice(ci * LANES, (ci + 1) * LANES)
                u_raw, v_raw, ds_v = u_ref[rows, cols], v_ref[rows, cols], ds_ref[rows, cols]
                ug = _gelu(u_raw)
                vh, rs = _group_norm(_gelu(v_raw))
                vn = (vh * gain_g).astype(BF16)
                y = _dot(wt, vn) + b_ref[g]
                dy = ds_v * ug
                dyb = dy.astype(BF16)
                du_ref[rows, cols] = (ds_v * y * _gelu_grad(u_raw)).astype(du_ref.dtype)
                dy_acc[g] += dy
                dw_ref[g] += _dot_nt(dyb, vn)
                dvn = _dot_tn(wt, dyb)
                dg_ref[:, cols] += jnp.sum(dvn * vh, axis=0, keepdims=True)
                dvh = dvn * gain_g
                dvg = rs * (dvh - jnp.mean(dvh, axis=-1, keepdims=True)
                            - vh * jnp.mean(dvh * vh, axis=-1, keepdims=True))
                dv_ref[rows, cols] = (dvg * _gelu_grad(v_raw)).astype(dv_ref.dtype)

        @pl.when(step == n_steps - 1)
        def _():
            ones = jnp.ones((8, LANES), F32)
            for g in range(groups):
                dw_ref[g] = jnp.where(tril, dw_ref[g], 0.0)
                db_ref[g] = lax.dot_general(ones, dy_acc[g], (((1,), (1,)), ((), ())),
                                            precision=lax.Precision.HIGHEST, preferred_element_type=F32)

    blk = lambda cb: pl.BlockSpec((rb, gw), lambda i: (i, cb))
    whole3 = pl.BlockSpec((groups, LANES, LANES), lambda i: (0, 0, 0))
    return pl.pallas_call(
        body, name="sgu_bwd", grid=(n_steps,),
        in_specs=[blk(0), blk(1), blk(0), pl.BlockSpec((1, gw), lambda i: (0, 0)), whole3, whole3],
        out_specs=[blk(0), blk(0), whole3, pl.BlockSpec((groups, 8, LANES), lambda i: (0, 0, 0)),
                   pl.BlockSpec((1, gw), lambda i: (0, 0))],
        out_shape=[jax.ShapeDtypeStruct((t, gw), BF16), jax.ShapeDtypeStruct((t, gw), BF16),
                   jax.ShapeDtypeStruct((groups, LANES, LANES), F32), jax.ShapeDtypeStruct((groups, 8, LANES), F32),
                   jax.ShapeDtypeStruct((1, gw), F32)],
        scratch_shapes=[pltpu.VMEM((groups, LANES, LANES), F32)],
        compiler_params=_params(("arbitrary",)),
    )(proj, proj, ds, gain, w, bias)


def _shift_down(z, s):
    rows = lax.broadcasted_iota(jnp.int32, z.shape, 0)
    return jnp.where(rows >= s, pltpu.roll(z, s, axis=0), 0.0)


def _shift_up(z, s):
    n = z.shape[0]
    rows = lax.broadcasted_iota(jnp.int32, z.shape, 0)
    return jnp.where(rows < n - s, pltpu.roll(z, n - s, axis=0), 0.0)


def _conv_fwd(proj3, cw, tc):
    _, t, cd = proj3.shape

    def body(p_ref, w_ref, o_ref):
        z = p_ref[1] * p_ref[2]
        w = w_ref[...]
        zc = w[2:3] * z + w[1:2] * _shift_down(z, 1) + w[0:1] * _shift_down(z, 2)
        o_ref[...] = (p_ref[0] * zc).astype(o_ref.dtype)

    return pl.pallas_call(
        body, name="conv_fwd", grid=(cd // tc,),
        in_specs=[pl.BlockSpec((3, t, tc), lambda j: (0, 0, j)), pl.BlockSpec((8, tc), lambda j: (0, j))],
        out_specs=pl.BlockSpec((t, tc), lambda j: (0, j)),
        out_shape=jax.ShapeDtypeStruct((t, cd), BF16),
        compiler_params=_params(("parallel",)),
    )(proj3, cw)


def _conv_bwd(proj3, cw, dbz, tc):
    _, t, cd = proj3.shape

    def body(p_ref, w_ref, d_ref, o_ref, dw_ref):
        b, c, xin = p_ref[0], p_ref[1], p_ref[2]
        w = w_ref[...]
        z = c * xin
        z1, z2 = _shift_down(z, 1), _shift_down(z, 2)
        zc = w[2:3] * z + w[1:2] * z1 + w[0:1] * z2
        d = d_ref[...]
        dzc = d * b
        dz = w[2:3] * dzc + w[1:2] * _shift_up(dzc, 1) + w[0:1] * _shift_up(dzc, 2)
        o_ref[0] = (d * zc).astype(o_ref.dtype)
        o_ref[1] = (dz * xin).astype(o_ref.dtype)
        o_ref[2] = (dz * c).astype(o_ref.dtype)
        row = lax.broadcasted_iota(jnp.int32, (8, tc), 0)
        dw0 = jnp.sum(dzc * z2, axis=0, keepdims=True)
        dw1 = jnp.sum(dzc * z1, axis=0, keepdims=True)
        dw2 = jnp.sum(dzc * z, axis=0, keepdims=True)
        dw_ref[...] = jnp.where(row == 0, dw0, 0.0) + jnp.where(row == 1, dw1, 0.0) + jnp.where(row == 2, dw2, 0.0)

    return pl.pallas_call(
        body, name="conv_bwd", grid=(cd // tc,),
        in_specs=[pl.BlockSpec((3, t, tc), lambda j: (0, 0, j)), pl.BlockSpec((8, tc), lambda j: (0, j)),
                  pl.BlockSpec((t, tc), lambda j: (0, j))],
        out_specs=[pl.BlockSpec((3, t, tc), lambda j: (0, 0, j)), pl.BlockSpec((8, tc), lambda j: (0, j))],
        out_shape=[jax.ShapeDtypeStruct((3, t, cd), BF16), jax.ShapeDtypeStruct((8, cd), F32)],
        compiler_params=_params(("parallel",)),
    )(proj3, cw, dbz)


def _place():
    x, y, c = lax.axis_index("x"), lax.axis_index("y"), lax.axis_index("c")
    chips = [(1 - x, y), (x, 1 - y), (1 - x, 1 - y)]
    return x, y, c, chips


def _any_specs(n):
    return [pl.BlockSpec(memory_space=pl.ANY) for _ in range(n)]


HBM_SPEC = pl.BlockSpec(memory_space=pltpu.HBM)
SEM_SPEC = pl.BlockSpec(memory_space=pltpu.SEMAPHORE)
ORDERED_EFFECT = pltpu.SideEffectType.DATAFLOW_SIDE_EFFECTING


def _in_hbm(a):
    return pltpu.with_memory_space_constraint(a, pltpu.HBM)


def _token():
    return jax.ShapeDtypeStruct((8, LANES), F32), pl.BlockSpec(memory_space=pltpu.VMEM)


def _gather_start(name, groups):
    sizes = [len(g) for g in groups]
    flat = [b for g in groups for b in g]
    n, ng = len(flat), len(groups)

    def body(*refs):
        ins, sems, token = refs[:n], refs[n:n + 2 * ng], refs[-1]
        x, y, c, chips = _place()
        me = 2 * x + y
        i = 0
        for gi, size in enumerate(sizes):
            for j in range(size):
                blk = ins[i].at[me, c]
                for k, chip in enumerate(chips):
                    pltpu.make_async_remote_copy(src_ref=blk, dst_ref=blk, send_sem=sems[2 * gi].at[3 * j + k],
                                                 recv_sem=sems[2 * gi + 1].at[3 * j + k],
                                                 device_id=(*chip, c), device_id_type=MESH).start()
                i += 1
        token[...] = jnp.zeros_like(token)

    tok_shape, tok_spec = _token()
    res = pl.pallas_call(
        body, name=name,
        in_specs=[HBM_SPEC] * n,
        out_specs=[SEM_SPEC] * (2 * ng) + [HBM_SPEC] * n + [tok_spec],
        out_shape=[pltpu.SemaphoreType.DMA((3 * size,)) for size in sizes for _ in (0, 1)]
        + [pltpu.HBM(b.shape, b.dtype) for b in flat] + [tok_shape],
        input_output_aliases={i: 2 * ng + i for i in range(n)},
        compiler_params=pltpu.CompilerParams(has_side_effects=ORDERED_EFFECT),
    )(*[_in_hbm(b) for b in flat])
    out, i = [], 2 * ng
    for gi, size in enumerate(sizes):
        out.append((res[2 * gi], res[2 * gi + 1], list(res[i:i + size])))
        i += size
    return out, res[-1]


def _gather_wait(tag, send, recv, bufs, after):
    n = len(bufs)
    after = tuple(after) if isinstance(after, (tuple, list)) else (after,)

    def body(*refs):
        ins, send_ref, recv_ref = refs[:n], refs[n], refs[n + 1]
        x, y, c, chips = _place()
        me = 2 * x + y
        for j in range(n):
            for k, (px, py) in enumerate(chips):
                cp = pltpu.make_async_remote_copy(src_ref=ins[j].at[me, c], dst_ref=ins[j].at[2 * px + py, c],
                                                  send_sem=send_ref.at[3 * j + k], recv_sem=recv_ref.at[3 * j + k],
                                                  device_id=(px, py, c), device_id_type=MESH)
                cp.wait_send()
                cp.wait_recv()

    return pl.pallas_call(
        body, name="gather_wait_" + tag,
        in_specs=[HBM_SPEC] * n + [SEM_SPEC, SEM_SPEC] + _any_specs(len(after)),
        out_specs=[HBM_SPEC] * n,
        out_shape=[pltpu.HBM(b.shape, b.dtype) for b in bufs],
        input_output_aliases={i: i for i in range(n)},
        compiler_params=pltpu.CompilerParams(has_side_effects=ORDERED_EFFECT),
    )(*bufs, send, recv, *after)


def _gather_forward(tag, bufs):
    n = len(bufs)

    def body(*refs):
        ins, outs = refs[:n], refs[n:2 * n]
        send, recv = refs[2 * n:]
        x, y, c, chips = _place()
        sib = (x, y, 1 - c)

        def cp(i, k, slot, half):
            return pltpu.make_async_remote_copy(src_ref=ins[i].at[slot, half], dst_ref=outs[i].at[slot, half],
                                                send_sem=send.at[3 * i + k], recv_sem=recv.at[3 * i + k],
                                                device_id=sib, device_id_type=MESH)

        cps = [cp(i, k, 2 * px + py, c) for i in range(n) for k, (px, py) in enumerate(chips)]
        for d in cps:
            d.start()
        for i in range(n):
            for k, (px, py) in enumerate(chips):
                cp(i, k, 2 * px + py, 1 - c).wait_recv()
        for d in cps:
            d.wait_send()

    return pl.pallas_call(
        body, name="gather_forward_" + tag,
        in_specs=_any_specs(n), out_specs=_any_specs(n),
        out_shape=[jax.ShapeDtypeStruct(b.shape, b.dtype) for b in bufs],
        scratch_shapes=[pltpu.SemaphoreType.DMA((3 * n,))] * 2,
        input_output_aliases={i: i for i in range(n)},
        compiler_params=pltpu.CompilerParams(has_side_effects=True),
    )(*bufs)


def _pair_route(srcs, zones):
    x, y, c, _ = _place()
    return [(srcs[i].at[j, 1 - c], zones[i].at[j], (x, y, 1 - c)) for i in range(len(srcs)) for j in range(N_CHIPS)]


def _chip_route(srcs, zones):
    x, y, c, chips = _place()
    return [(srcs[i].at[2 * px + py], zones[i].at[k], (px, py, c)) for i in range(len(srcs)) for k, (px, py) in enumerate(chips)]


def _all_route(srcs, zones):
    x, y, c, _ = _place()
    flips = [(fx, fy, fc) for fx in (0, 1) for fy in (0, 1) for fc in (0, 1)][1:]
    return [(srcs[0], zones[0].at[4 * x + 2 * y + c], (x + fx - 2 * x * fx, y + fy - 2 * y * fy, c + fc - 2 * c * fc))
            for fx, fy, fc in flips]


def _share_route(srcs, zones):
    x, y, c, _ = _place()
    return [(s.at[c], s.at[c], (x, y, 1 - c)) for s in srcs]


def _exchange_start(name, route, n_copies, srcs, zones):
    n, nz = len(srcs), len(zones)
    lands = [lax.empty(z, a.dtype) if isinstance(z, tuple) else z for z, a in zip(zones, srcs)]

    def body(*refs):
        ins, zone_refs, send, recv, token = refs[:n], refs[n:n + nz], refs[n + nz], refs[n + nz + 1], refs[-1]
        for k, (src, dst, dev) in enumerate(route(ins, zone_refs)):
            pltpu.make_async_remote_copy(src_ref=src, dst_ref=dst, send_sem=send.at[k], recv_sem=recv.at[k],
                                         device_id=dev, device_id_type=MESH).start()
        token[...] = jnp.zeros_like(token)

    tok_shape, tok_spec = _token()
    res = pl.pallas_call(
        body, name=name,
        in_specs=[HBM_SPEC] * (n + nz),
        out_specs=[SEM_SPEC, SEM_SPEC] + [HBM_SPEC] * (n + nz) + [tok_spec],
        out_shape=[pltpu.SemaphoreType.DMA((n_copies,))] * 2 + [pltpu.HBM(a.shape, a.dtype) for a in srcs + lands]
        + [tok_shape],
        input_output_aliases={i: 2 + i for i in range(n + nz)},
        compiler_params=pltpu.CompilerParams(has_side_effects=ORDERED_EFFECT),
    )(*[_in_hbm(a) for a in srcs + lands])
    return (res[0], res[1], list(res[2:2 + n]), list(res[2 + n:2 + n + nz])), res[-1]


def _exchange_wait(name, route, started, after):
    send, recv, srcs, lands = started
    n, nz = len(srcs), len(lands)
    after = tuple(after) if isinstance(after, (tuple, list)) else (after,)

    def body(*refs):
        ins, zone_refs, send_ref, recv_ref = refs[:n], refs[n:n + nz], refs[n + nz], refs[n + nz + 1]
        for k, (src, dst, dev) in enumerate(route(ins, zone_refs)):
            cp = pltpu.make_async_remote_copy(src_ref=src, dst_ref=dst, send_sem=send_ref.at[k], recv_sem=recv_ref.at[k],
                                              device_id=dev, device_id_type=MESH)
            cp.wait_send()
            cp.wait_recv()

    res = pl.pallas_call(
        body, name=name,
        in_specs=[HBM_SPEC] * (n + nz) + [SEM_SPEC, SEM_SPEC] + _any_specs(len(after)),
        out_specs=[HBM_SPEC] * (n + nz),
        out_shape=[pltpu.HBM(a.shape, a.dtype) for a in srcs + lands],
        input_output_aliases={i: i for i in range(n + nz)},
        compiler_params=pltpu.CompilerParams(has_side_effects=ORDERED_EFFECT),
    )(*srcs, *lands, send, recv, *after)
    return list(res[:n]), list(res[n:])


def _spread(v):
    rows, cols = v.shape
    tr = _row_tile(rows, cols, budget=256 * 1024)

    def body(v_ref, o_ref):
        o_ref[...] = jnp.broadcast_to(v_ref[...][None], o_ref.shape)

    return pl.pallas_call(body, name="spread_small_grads", grid=(rows // tr,),
                          in_specs=[pl.BlockSpec((tr, cols), lambda r: (r, 0))],
                          out_specs=pl.BlockSpec((8, tr, cols), lambda r: (0, r, 0)),
                          out_shape=jax.ShapeDtypeStruct((8, rows, cols), v.dtype),
                          compiler_params=_params(("parallel",)))(v)


def _row_tile(rows, cols, itemsize=4, budget=2 * 1024 * 1024):
    best = None
    for t in range(8, rows + 1, 8):
        if rows % t == 0 and t * cols * itemsize <= budget:
            best = t
    return best if best is not None else rows


def _my_chip():
    return 2 * lax.axis_index("x") + lax.axis_index("y")


def _pair_sum(g5, gsib):
    _, _, rh, cols = g5.shape
    tr = _row_tile(rh, cols)

    def body(a_ref, b_ref, o_ref):
        o_ref[...] = (a_ref[...].astype(F32) + b_ref[...].astype(F32)).astype(o_ref.dtype)

    return pl.pallas_call(body, name="grad_pair_sum", grid=(N_CHIPS, rh // tr),
                          in_specs=[pl.BlockSpec((None, None, tr, cols), lambda j, r: (j, lax.axis_index("c"), r, 0)),
                                    pl.BlockSpec((None, tr, cols), lambda j, r: (j, r, 0))],
                          out_specs=pl.BlockSpec((None, tr, cols), lambda j, r: (j, r, 0)),
                          out_shape=jax.ShapeDtypeStruct((N_CHIPS, rh, cols), BF16),
                          compiler_params=_params(("parallel", "parallel")))(g5, gsib)


def _chip_sum(part, recv):
    _, rh, cols = part.shape
    tr = _row_tile(rh, cols)

    def body(a_ref, b_ref, o_ref):
        acc = a_ref[...].astype(F32)
        for k in range(3):
            acc = acc + b_ref[k].astype(F32)
        o_ref[...] = acc

    return pl.pallas_call(body, name="grad_chip_sum", grid=(rh // tr,),
                          in_specs=[pl.BlockSpec((None, tr, cols), lambda r: (_my_chip(), r, 0)),
                                    pl.BlockSpec((3, tr, cols), lambda r: (0, r, 0))],
                          out_specs=pl.BlockSpec((None, tr, cols), lambda r: (lax.axis_index("c"), r, 0)),
                          out_shape=jax.ShapeDtypeStruct((2, rh, cols), F32),
                          compiler_params=_params(("parallel",)))(part, recv)


def _sum_devices(g):
    _, rows, cols = g.shape
    tr = _row_tile(rows, cols, budget=256 * 1024)

    def body(g_ref, o_ref):
        acc = g_ref[0]
        for d in range(1, 8):
            acc = acc + g_ref[d]
        o_ref[...] = acc

    return pl.pallas_call(body, name="sum_small_grads", grid=(rows // tr,),
                          in_specs=[pl.BlockSpec((8, tr, cols), lambda r: (0, r, 0))],
                          out_specs=pl.BlockSpec((tr, cols), lambda r: (r, 0)),
                          out_shape=jax.ShapeDtypeStruct((rows, cols), F32),
                          compiler_params=_params(("parallel",)))(g)


def _place_shard(w, layer, dtype, deps=()):
    _, rows, cols = w.shape
    tr = _row_tile(rows, cols)

    def body(i_ref, *rest):
        o_ref = rest[-1]
        o_ref[...] = i_ref[...].astype(o_ref.dtype)

    out = pl.pallas_call(body, name="place_shard", grid=(rows // tr,),
                         in_specs=[pl.BlockSpec((None, tr, cols), lambda r: (layer, r, 0))] + _any_specs(len(deps)),
                         out_specs=pl.BlockSpec((None, tr, cols), lambda r: (_my_chip(), r, 0)),
                         out_shape=jax.ShapeDtypeStruct((N_CHIPS, rows, cols), dtype),
                         compiler_params=_params(("parallel",)))(w, *deps)
    return out.reshape(N_CHIPS, 2, rows // 2, cols)


def _adamw(w, gs, m, v):
    n_layers, rows, cols = w.shape
    tr = _row_tile(rows, cols, budget=1024 * 1024)

    def body(w_ref, m_ref, v_ref, *rest):
        g_refs = rest[:n_layers]
        go_ref, d_ref, mo_ref, vo_ref = rest[n_layers:]
        gv = g_refs[0][...]
        for layer in range(1, n_layers):
            gv = jnp.where(pl.program_id(0) == layer, g_refs[layer][...], gv)
        mn = ADAM_B1 * m_ref[...] + (1.0 - ADAM_B1) * gv
        vn = ADAM_B2 * v_ref[...] + (1.0 - ADAM_B2) * jnp.square(gv)
        m_hat = mn / (1.0 - ADAM_B1 ** ADAM_STEP)
        v_hat = vn / (1.0 - ADAM_B2 ** ADAM_STEP)
        d_ref[...] = -ADAM_LR * (m_hat / (jnp.sqrt(v_hat) + ADAM_EPS) + ADAM_WD * w_ref[...])
        go_ref[...] = gv
        mo_ref[...] = mn
        vo_ref[...] = vn

    spec = pl.BlockSpec((None, tr, cols), lambda layer, r: (layer, r, 0))
    g_specs = [pl.BlockSpec((tr, cols), lambda layer, r, own=own: (jnp.where(layer == own, r, 0), 0))
               for own in range(n_layers)]
    return pl.pallas_call(body, name="adamw", grid=(n_layers, rows // tr), in_specs=[spec] * 3 + g_specs,
                          out_specs=[spec] * 4, out_shape=[jax.ShapeDtypeStruct((n_layers, rows, cols), F32)] * 4,
                          compiler_params=_params(("parallel", "parallel")))(w, m, v, *gs)


def _pad_rope(w):
    z = jnp.zeros(w.shape[:-1] + (ROPE_HALF,), w.dtype)
    return jnp.concatenate([w[..., :ROPE_HALF], z, w[..., ROPE_HALF:], z], axis=-1)


def _unpad_rope(g):
    return jnp.concatenate([g[..., :ROPE_HALF], g[..., ROPE:ROPE + ROPE_HALF]], axis=-1)


def _unstack_cols(s):
    n, r, cs = s.shape
    return jnp.transpose(s, (1, 0, 2)).reshape(r, n * cs)


def _stack_cols(f):
    r, cfull = f.shape
    return jnp.transpose(f.reshape(r, N_CHIPS, cfull // N_CHIPS), (1, 0, 2))


def _small_shard(norm, conv):
    return jnp.concatenate([jnp.pad(norm, ((0, 15), (0, 0))), jnp.pad(conv, ((0, 13), (0, 0)))], axis=0)


def _flat_rows(a):
    return a.reshape(-1, LANES)


def _pack_small(arrs):
    return jnp.concatenate([_flat_rows(a.astype(F32)) for a in arrs], axis=0)


def _unpack_small(flat, like):
    out, r = [], 0
    for a in like:
        n = a.size // LANES
        out.append(flat[r:r + n].reshape(a.shape))
        r += n
    return out


def kernel(x, positions, e_norm_mix, e_w_in, e_q_norm, e_w_uq, e_kv_norm, e_w_ukv, e_v_norm, e_sgu_w, e_sgu_b, e_mla_out_norm, e_sgu_out_norm, e_w_out, o_norm_mix, o_w_in, o_conv_w, o_w_out, mlp_norm, mlp_w1, mlp_w2, final_norm, loss_target, m_e_norm_mix, m_e_w_in, m_e_q_norm, m_e_w_uq, m_e_kv_norm, m_e_w_ukv, m_e_v_norm, m_e_sgu_w, m_e_sgu_b, m_e_mla_out_norm, m_e_sgu_out_norm, m_e_w_out, m_o_norm_mix, m_o_w_in, m_o_conv_w, m_o_w_out, m_mlp_norm, m_mlp_w1, m_mlp_w2, m_final_norm, v_e_norm_mix, v_e_w_in, v_e_q_norm, v_e_w_uq, v_e_kv_norm, v_e_w_ukv, v_e_v_norm, v_e_sgu_w, v_e_sgu_b, v_e_mla_out_norm, v_e_sgu_out_norm, v_e_w_out, v_o_norm_mix, v_o_w_in, v_o_conv_w, v_o_w_out, v_mlp_norm, v_mlp_w1, v_mlp_w2, v_final_norm):
    t, d = x.shape[1], x.shape[2]
    ql, kvl = e_q_norm.shape[1], e_kv_norm.shape[1]
    groups = e_v_norm.shape[1]
    gw = groups * LANES
    heads = N_CHIPS * e_w_uq.shape[2] // (LANES + ROPE)
    hw = heads * LANES
    mix = hw + gw
    ei = N_CHIPS * e_w_in.shape[2]
    cd = N_CHIPS * o_conv_w.shape[2]
    ff = N_CHIPS * mlp_w1.shape[2]
    ffs = ff // N_CHIPS
    pi = 2 * gw + ql + kvl + LANES
    assert e_norm_mix.shape[0] == 1 and o_norm_mix.shape[0] == 1 and mlp_norm.shape[0] == 2
    assert ei == ql + kvl + ROPE + 2 * gw and cd == d and e_sgu_w.shape[2] == LANES
    assert (2 * gw) % ql == 0 and (2 * gw + ql) % kvl == 0 and t % LANES == 0
    scale = (LANES + ROPE) ** -0.5

    tr = min(256, t)
    tm = _pick(t, 1024, 8)
    kt, kd = _pick(t, 2048, 8), _pick(d, 2048)
    xs = x.reshape(t, d)
    tgt = loss_target.reshape(t, d)

    small_shard = _small_shard(o_norm_mix, o_conv_w[0])
    first, tok = _gather_start("gather_start_e", [
        [_place_shard(e_w_in, 0, BF16)],
        [_place_shard(e_w_uq, 0, BF16), _place_shard(e_w_ukv, 0, BF16), _place_shard(e_w_out, 0, BF16),
         _place_shard(small_shard[None], 0, F32)]])
    rest, tok = _gather_start("gather_start_rest", [
        [_place_shard(mlp_w1, 0, BF16, (tok,))], [_place_shard(mlp_w2, 0, BF16, (tok,))],
        [_place_shard(o_w_in, 0, BF16, (tok,)), _place_shard(o_w_out, 0, BF16, (tok,))],
        [_place_shard(mlp_w1, 1, BF16, (tok,))], [_place_shard(mlp_w2, 1, BF16, (tok,))]])
    started = first + rest

    def gathered(gi, tag, after):
        send, recv, bufs = started[gi]
        bufs = _gather_forward(tag, _gather_wait(tag, send, recv, bufs, after))
        return [b.reshape(N_CHIPS, 2 * b.shape[2], b.shape[3]) for b in bufs]

    g_e = e_norm_mix
    h0 = _norm_fwd("e_norm", xs, g_e, tr)
    inv_freq = ROPE_BASE ** (-jnp.arange(0, ROPE, 2, dtype=F32) / ROPE)
    zeros32 = jnp.zeros((ROPE_HALF,), F32)
    ones32 = jnp.ones((ROPE_HALF,), F32)
    invf = jnp.concatenate([inv_freq, zeros32, inv_freq, zeros32]).reshape(1, LANES)
    cmask = jnp.concatenate([ones32, zeros32, ones32, zeros32]).reshape(1, LANES)
    smask = jnp.concatenate([-ones32, zeros32, ones32, zeros32]).reshape(1, LANES)
    ctab, stab = _rope_tables(positions.reshape(t, 1).astype(F32), invf, cmask, smask, tr)

    w_in_g, = gathered(0, "e_in", (h0, ctab, tok))
    full = _unstack_cols(w_in_g)
    c2, c3 = ql + kvl, ql + kvl + ROPE
    w_in_all = jnp.concatenate([full[:, c3:], full[:, :c2], _pad_rope(full[:, c2:c3])], axis=1)
    proj, = _matmul("e_proj", Mat(h0, t, d), Mat(w_in_all, d, pi), "nn", [_out(t, pi, F32)], tm, _pick(pi, 1024), kd)

    w_uq_g, w_ukv_g, w_eout_g, small_g = gathered(1, "e", proj)
    full = _unstack_cols(w_uq_g).reshape(ql, heads, LANES + ROPE)
    w_q_all = jnp.concatenate([full[:, :, :LANES].reshape(ql, hw), _pad_rope(full[:, :, LANES:]).reshape(ql, hw)], axis=1)
    full = _unstack_cols(w_ukv_g).reshape(kvl, heads, 2 * LANES)
    w_kv_all = jnp.concatenate([full[:, :, :LANES].reshape(kvl, hw), full[:, :, LANES:].reshape(kvl, hw)], axis=1)
    w_eout = w_eout_g.reshape(mix, d)
    g_o = small_g[:, 0].reshape(1, d)
    conv_w = jnp.pad(jnp.transpose(small_g[:, 16:19], (1, 0, 2)).reshape(3, cd), ((0, 5), (0, 0)))

    g_q, g_kv = e_q_norm, e_kv_norm
    g_vn = e_v_norm.reshape(1, gw)
    sgu_w = e_sgu_w[0]
    sgu_b = jnp.broadcast_to(e_sgu_b[0][:, :, None], (groups, LANES, LANES))
    g_mla, g_sgu = e_mla_out_norm, e_sgu_out_norm
    g_m0, g_m1 = mlp_norm[0:1], mlp_norm[1:2]
    g_f = final_norm.reshape(1, d)

    def mlp_fwd(tag, xin, g, gi):
        hm = _norm_fwd("mlp_norm_" + tag, xin, g, tr)
        tn = _pick(ffs, 1024)
        w1 = Mat(gathered(gi, "w1_" + tag, hm)[0], d, ff, "colstack")
        a, act = _matmul("mlp_up_" + tag, Mat(hm, t, d), w1, "nn",
                         [_out(t, ff, BF16), _out(t, ff, BF16)], tm, tn, kd,
                         epilogue=lambda z: (jnp.maximum(z, 0.0), jnp.square(jnp.maximum(z, 0.0))))
        w2 = Mat(gathered(gi + 1, "w2_" + tag, act)[0].reshape(ff, d), ff, d)
        xo, = _matmul("mlp_down_" + tag, Mat(act, t, ff), w2, "nn",
                      [_out(t, d, F32)], tm, _pick(d, 1024), _pick(ffs, 2048),
                      epilogue=lambda z, r: (z + r,), extras=[Mat(xin, t, d)])
        return xo, hm, a, act, w1, w2

    def chip_start(tag, part):
        return _exchange_start("scatter_start_" + tag, _chip_route, 3 * len(part), part, [(3,) + p.shape[1:] for p in part])

    def pair_start(tag, stacked):
        g5 = [g.reshape(N_CHIPS, 2, g.shape[1] // 2, g.shape[2]) for g in stacked]
        return _exchange_start("pair_start_" + tag, _pair_route, N_CHIPS * len(g5), g5,
                               [(N_CHIPS,) + g.shape[2:] for g in g5])

    def pair_finish(tag, started, after):
        g5, from_sib = _exchange_wait("pair_wait_" + tag, _pair_route, started, after)
        return chip_start(tag, [_pair_sum(a, b) for a, b in zip(g5, from_sib)])

    def mlp_bwd(tag, dx, dxb, xin, g, w1, w2, hm, a, act, deps, extra_grads=()):
        tn = _pick(ffs, 1024)
        dz, = _matmul("mlp_dact_" + tag, Mat(dxb, t, d), w2, "nt",
                      [_out(t, ff, BF16)], tm, tn, kd,
                      epilogue=lambda z, av: (z * (2.0 * av.astype(F32)),), extras=[Mat(a, t, ff)], deps=deps)
        dw2, = _matmul("mlp_dw2_" + tag, Mat(act, t, ff), Mat(dxb, t, d), "tn",
                       [_out(ff, d, BF16)], tn, _pick(d, 1024), kt)
        dw1, = _matmul("mlp_dw1_" + tag, Mat(hm, t, d), Mat(dz, t, ff), "tn",
                       [_out(d, ff, BF16, "colstack", (), (N_CHIPS, d, ffs))], _pick(d, 1024), tn, kt)
        started, tok = pair_start("m" + tag, [dw1, dw2.reshape(N_CHIPS, ffs, d), *extra_grads])
        dhm, = _matmul("mlp_dh_" + tag, Mat(dz, t, ff), w1, "nt",
                       [_out(t, d, F32)], tm, _pick(d, 1024), _pick(ffs, 2048), deps=(tok,))
        dxo, dxob, dg = _norm_bwd("mlp_norm_bwd_" + tag, dhm, xin, g, dx, tr)
        sc, tok = pair_finish("m" + tag, started, dxo)
        return dxo, dxob, dg, sc, tok

    cq_cb, ckv_cb, kr_cb = 2 * gw // ql, (2 * gw + ql) // kvl, (2 * gw + ql + kvl) // LANES
    qn, kvn = _rowwise("qkv_norm", lambda a, b, ga, gb: (_rms(a, ga), _rms(b, gb)), t // tr,
                       [_rt(proj, tr, ql, cq_cb), _rt(proj, tr, kvl, ckv_cb), _whole(g_q), _whole(g_kv)],
                       [_rt_out(t, ql, BF16, tr), _rt_out(t, kvl, BF16, tr)])
    qfull, = _matmul("q_up", Mat(qn, t, ql), Mat(w_q_all, ql, 2 * hw), "nn", [_out(t, 2 * hw, F32)], tm, _pick(2 * hw, 1024), ql)
    kvall, = _matmul("kv_up", Mat(kvn, t, kvl), Mat(w_kv_all, kvl, 2 * hw), "nn", [_out(t, 2 * hw, BF16)], tm, _pick(2 * hw, 1024), kvl)
    qall, kr = _rope_fwd(qfull, proj, kr_cb, ctab, stab, heads, tr)
    att, lse_row = _attn_fwd(qall, kvall, kr, heads, scale, tr)
    rb = min(2 * LANES, t)
    sgu = _sgu_fwd(proj, g_vn, sgu_w, sgu_b, groups, rb)
    mixed = _rowwise("mix_norm", lambda a, s, ga, gs: jnp.concatenate([_rms(a, ga), _rms(s, gs)], axis=1), t // tr,
                     [_rt(att, tr), _rt(sgu, tr), _whole(g_mla), _whole(g_sgu)], [_rt_out(t, mix, BF16, tr)])[0]
    x1, = _matmul("e_out", Mat(mixed, t, mix), Mat(w_eout, mix, d), "nn", [_out(t, d, F32)], tm, _pick(d, 1024), _pick(mix, 2048),
                  epilogue=lambda z, r: (z + r,), extras=[Mat(xs, t, d)])
    x2, hm0, a0, act0, w1_0, w2_0 = mlp_fwd("0", x1, g_m0, 2)

    w_oin_g, w_oout_g = gathered(4, "o", x2)
    w_oout = w_oout_g.reshape(cd, d)
    h1 = _norm_fwd("o_norm", x2, g_o, tr)
    oin = Mat(_unstack_cols(w_oin_g), d, 3 * cd)
    tn_o = _pick(_gcd(3 * cd // N_CHIPS, cd), 512)
    proj3, = _matmul("o_proj", Mat(h1, t, d), oin, "nn", [_out(t, 3 * cd, F32, "colstack", (), (3, t, cd))],
                     tm, _pick(cd, 1024), kd)
    tc = _pick(cd, 256)
    bz = _conv_fwd(proj3, conv_w, tc)
    x3, = _matmul("o_out", Mat(bz, t, cd), Mat(w_oout, cd, d), "nn", [_out(t, d, F32)], tm, _pick(d, 1024), _pick(cd, 2048),
                  epilogue=lambda z, r: (z + r,), extras=[Mat(x2, t, d)])
    x4, hm1, a1, act1, w1_1, w2_1 = mlp_fwd("1", x3, g_m1, 5)

    def final_fn(xv, gv, tv):
        r = lax.rsqrt(jnp.mean(xv * xv, axis=-1, keepdims=True) + EPS)
        xh = xv * r
        err = xh * gv - tv
        dy = err * (1.0 / d)
        dxh = dy * gv
        dx = r * (dxh - xh * jnp.mean(dxh * xh, axis=-1, keepdims=True))
        sq = jnp.sum(err * err, axis=0, keepdims=True)
        part = sq[:, :LANES]
        for k in range(1, d // LANES):
            part = part + sq[:, k * LANES:(k + 1) * LANES]
        return dx, dx, part, jnp.sum(dy * xh, axis=0, keepdims=True)

    dx4, dx4b, loss_vec, dg_f = _rowwise("loss_final_norm", final_fn, t // tr, [_rt(x4, tr), _whole(g_f), _rt(tgt, tr)],
                                         [_rt_out(t, d, F32, tr), _rt_out(t, d, BF16, tr)],
                                         [jax.ShapeDtypeStruct((1, LANES), F32), jax.ShapeDtypeStruct((1, d), F32)])
    loss = lax.psum(0.5 * jnp.sum(loss_vec) / d, ("x", "y", "c"))

    dx3, dx3b, dg_m1, sc_m1, tok = mlp_bwd("1", dx4, dx4b, x3, g_m1, w1_1, w2_1, hm1, a1, act1, ())

    dbz, = _matmul("o_out_dx", Mat(dx3b, t, d), Mat(w_oout, cd, d), "nt", [_out(t, cd, F32)], tm, _pick(cd, 1024), kd,
                   deps=(tok,))
    dw_oout, = _matmul("o_out_dw", Mat(bz, t, cd), Mat(dx3b, t, d), "tn", [_out(cd, d, BF16)], _pick(cd, 1024), _pick(d, 1024), kt)
    dproj3, dconv = _conv_bwd(proj3, conv_w, dbz, tc)
    dp3 = Mat(dproj3, t, 3 * cd, "colstack")
    dw_oin, = _matmul("o_proj_dw", Mat(h1, t, d), dp3, "tn", [_out(d, 3 * cd, BF16, "colstack", (), (N_CHIPS, d, 3 * cd // N_CHIPS))],
                      _pick(d, 1024), tn_o, kt)
    started_o, tok = pair_start("o", [dw_oin, dw_oout.reshape(N_CHIPS, cd // N_CHIPS, d)])
    dh1, = _matmul("o_proj_dx", dp3, oin, "nt", [_out(t, d, F32)], tm, _pick(d, 1024), _pick(cd, 2048), deps=(tok,))
    dx2, dx2b, dg_o = _norm_bwd("o_norm_bwd", dh1, x2, g_o, dx3, tr)
    sc_o, tok = pair_finish("o", started_o, dx2)

    dconv_s = jnp.transpose(dconv[:3].reshape(3, N_CHIPS, cd // N_CHIPS), (1, 0, 2))
    gsmall = jnp.concatenate([jnp.pad(dg_o.reshape(N_CHIPS, 1, d // N_CHIPS), ((0, 0), (0, 15), (0, 0))),
                              jnp.pad(dconv_s, ((0, 0), (0, 13), (0, 0)))], axis=1)
    dx1, dx1b, dg_m0, sc_m0, tok = mlp_bwd("0", dx2, dx2b, x1, g_m0, w1_0, w2_0, hm0, a0, act0, (tok,), (gsmall,))

    dmixed, = _matmul("e_out_dx", Mat(dx1b, t, d), Mat(w_eout, mix, d), "nt", [_out(t, mix, F32)], tm, _pick(mix, 1024), kd,
                      deps=(tok,))
    dw_eout, = _matmul("e_out_dw", Mat(mixed, t, mix), Mat(dx1b, t, d), "tn", [_out(mix, d, BF16)], _pick(mix, 1024), _pick(d, 1024), kt)

    def mixb_fn(dm, a, s, ga, gs):
        da, dga = _rms_bwd(dm[:, :hw], a, ga)
        dsg, dgs = _rms_bwd(dm[:, hw:], s, gs)
        prod = da * a
        cols = [jnp.broadcast_to(jnp.sum(prod[:, h * LANES:(h + 1) * LANES], axis=-1, keepdims=True), (tr, LANES))
                for h in range(heads)]
        return da, dsg, jnp.stack([_row_of(c) for c in cols], axis=0), dga, dgs

    da_b, dsgu, delta_row, dg_mla, dg_sgu = _rowwise(
        "mix_norm_bwd", mixb_fn, t // tr, [_rt(dmixed, tr), _rt(att, tr), _rt(sgu, tr), _whole(g_mla), _whole(g_sgu)],
        [_rt_out(t, hw, BF16, tr), _rt_out(t, gw, F32, tr),
         (jax.ShapeDtypeStruct((heads, 8, t), F32), pl.BlockSpec((heads, 8, tr), lambda i: (0, 0, i)))],
        [jax.ShapeDtypeStruct((1, hw), F32), jax.ShapeDtypeStruct((1, gw), F32)])

    du, dv, dsgu_w, dsgu_b8, dg_vn = _sgu_bwd(proj, dsgu, g_vn, sgu_w, sgu_b, groups, rb)
    dq1, dq2, dk1, dvv, dkr_h = _attn_bwd(qall, kvall, kr, da_b, lse_row, delta_row, heads, scale, tr)
    dqfull, dkr = _rope_bwd(dq1, dq2, dkr_h, ctab, stab, heads, tr)
    dkvall = jnp.concatenate([dk1, dvv], axis=1)
    dw_q, = _matmul("q_up_dw", Mat(qn, t, ql), Mat(dqfull, t, 2 * hw), "tn", [_out(ql, 2 * hw, BF16)], ql, _pick(2 * hw, 1024), kt)
    dqn, = _matmul("q_up_dx", Mat(dqfull, t, 2 * hw), Mat(w_q_all, ql, 2 * hw), "nt", [_out(t, ql, F32)], tm, ql, _pick(2 * hw, 2048))
    dw_kv, = _matmul("kv_up_dw", Mat(kvn, t, kvl), Mat(dkvall, t, 2 * hw), "tn", [_out(kvl, 2 * hw, BF16)], kvl, _pick(2 * hw, 1024), kt)
    dkvn, = _matmul("kv_up_dx", Mat(dkvall, t, 2 * hw), Mat(w_kv_all, kvl, 2 * hw), "nt", [_out(t, kvl, F32)], tm, kvl, _pick(2 * hw, 2048))

    def qkvb_fn(da, db, a, b, ga, gb):
        dxa, dga = _rms_bwd(da, a, ga)
        dxb, dgb = _rms_bwd(db, b, gb)
        return dxa, dxb, dga, dgb

    dcq, dckv, dg_q, dg_kv = _rowwise(
        "qkv_norm_bwd", qkvb_fn, t // tr,
        [_rt(dqn, tr), _rt(dkvn, tr), _rt(proj, tr, ql, cq_cb), _rt(proj, tr, kvl, ckv_cb), _whole(g_q), _whole(g_kv)],
        [_rt_out(t, ql, BF16, tr), _rt_out(t, kvl, BF16, tr)],
        [jax.ShapeDtypeStruct((1, ql), F32), jax.ShapeDtypeStruct((1, kvl), F32)])
    dproj = jnp.concatenate([du, dv, dcq, dckv, dkr], axis=1)
    dw_in, = _matmul("e_proj_dw", Mat(h0, t, d), Mat(dproj, t, pi), "tn", [_out(d, pi, BF16)], _pick(d, 1024), _pick(pi, 1024), kt)
    dh0, = _matmul("e_proj_dx", Mat(dproj, t, pi), Mat(w_in_all, d, pi), "nt", [_out(t, d, F32)], tm, _pick(d, 1024), _pick(pi, 4096))
    dx0, _, dg_e = _norm_bwd("e_norm_bwd", dh0, xs, g_e, dx1, tr)

    gfull = jnp.concatenate([dw_in[:, 2 * gw:2 * gw + c2], _unpad_rope(dw_in[:, 2 * gw + c2:]), dw_in[:, :2 * gw]], axis=1)
    gw_in = _stack_cols(gfull)
    gq = jnp.concatenate([dw_q[:, :hw].reshape(ql, heads, LANES), _unpad_rope(dw_q[:, hw:].reshape(ql, heads, LANES))], axis=-1)
    gw_uq = _stack_cols(gq.reshape(ql, heads * (LANES + ROPE)))
    gkv = jnp.concatenate([dw_kv[:, :hw].reshape(kvl, heads, LANES), dw_kv[:, hw:].reshape(kvl, heads, LANES)], axis=-1)
    gw_ukv = _stack_cols(gkv.reshape(kvl, heads * 2 * LANES))
    started_e, tok_pair = pair_start("e", [gw_in, gw_uq, gw_ukv, dw_eout.reshape(N_CHIPS, mix // N_CHIPS, d)])

    small_like = [e_norm_mix, e_q_norm, e_kv_norm, e_v_norm, e_sgu_w, e_sgu_b, e_mla_out_norm, e_sgu_out_norm, mlp_norm, final_norm]
    small_grads = [dg_e, dg_q, dg_kv, dg_vn, dsgu_w, dsgu_b8[:, 0, :], dg_mla, dg_sgu, jnp.concatenate([dg_m0, dg_m1], axis=0), dg_f]
    sflat = _pack_small(small_grads)
    pad = (-sflat.shape[0]) % 8
    sflat = jnp.pad(sflat, ((0, pad), (0, 0)))
    small_started, tok_small = _exchange_start("small_start", _all_route, 7, [sflat], [_spread(sflat)])

    def summed(tag, sc, after):
        part, lands = _exchange_wait("scatter_wait_" + tag, _chip_route, sc, after)
        half = [_chip_sum(p, r) for p, r in zip(part, lands)]
        return _exchange_start("share_start_" + tag, _share_route, len(half), half, [])

    def shared(tag, started, after):
        bufs, _ = _exchange_wait("share_wait_" + tag, _share_route, started, after)
        return [r.reshape(2 * r.shape[1], r.shape[2]) for r in bufs]

    sh_m1, tok = summed("m1", sc_m1, (tok_pair, tok_small))
    sc_e, tok = pair_finish("e", started_e, tok)
    sh_o, tok = summed("o", sc_o, tok)
    sh_m0, tok = summed("m0", sc_m0, tok)
    r_oin, r_oout = shared("o", sh_o, tok)
    late = {"o_w_in": _adamw(o_w_in, [r_oin], m_o_w_in, v_o_w_in),
            "o_w_out": _adamw(o_w_out, [r_oout], m_o_w_out, v_o_w_out)}
    r_w1_1, r_w2_1 = shared("m1", sh_m1, late["o_w_in"][1])
    r_w1_0, r_w2_0, r_small = shared("m0", sh_m0, r_w2_1)
    late["mlp_w1"] = _adamw(mlp_w1, [r_w1_0, r_w1_1], m_mlp_w1, v_mlp_w1)
    late["mlp_w2"] = _adamw(mlp_w2, [r_w2_0, r_w2_1], m_mlp_w2, v_mlp_w2)

    _, (all_small,) = _exchange_wait("small_wait", _all_route, small_started, late["mlp_w2"][1])
    g_small = _sum_devices(all_small)

    def padded(arrs):
        return jnp.pad(_pack_small(arrs), ((0, pad), (0, 0)))

    s_m = [m_e_norm_mix, m_e_q_norm, m_e_kv_norm, m_e_v_norm, m_e_sgu_w, m_e_sgu_b, m_e_mla_out_norm, m_e_sgu_out_norm, m_mlp_norm, m_final_norm]
    s_v = [v_e_norm_mix, v_e_q_norm, v_e_kv_norm, v_e_v_norm, v_e_sgu_w, v_e_sgu_b, v_e_mla_out_norm, v_e_sgu_out_norm, v_mlp_norm, v_final_norm]
    s_out = [_unpack_small(o[0], small_like)
             for o in _adamw(padded(small_like)[None], [g_small], padded(s_m)[None], padded(s_v)[None])]

    sm = [o[0] for o in _adamw(small_shard[None], [r_small], _small_shard(m_o_norm_mix, m_o_conv_w[0])[None],
                               _small_shard(v_o_norm_mix, v_o_conv_w[0])[None])]

    sh_e, tok = summed("e", sc_e, late["mlp_w2"][1])
    r_in, r_uq, r_ukv, r_eout = shared("e", sh_e, tok)
    big = dict(late)
    flip = lambda a: jnp.swapaxes(a, 1, 2)
    big.update({
        "e_w_in": [flip(o) for o in _adamw(flip(e_w_in), [r_in.T], flip(m_e_w_in), flip(v_e_w_in))],
        "e_w_uq": _adamw(e_w_uq, [r_uq], m_e_w_uq, v_e_w_uq),
        "e_w_ukv": _adamw(e_w_ukv, [r_ukv], m_e_w_ukv, v_e_w_ukv),
        "e_w_out": _adamw(e_w_out, [r_eout], m_e_w_out, v_e_w_out),
    })

    names = ["e_norm_mix", "e_w_in", "e_q_norm", "e_w_uq", "e_kv_norm", "e_w_ukv", "e_v_norm", "e_sgu_w", "e_sgu_b",
             "e_mla_out_norm", "e_sgu_out_norm", "e_w_out", "o_norm_mix", "o_w_in", "o_conv_w", "o_w_out",
             "mlp_norm", "mlp_w1", "mlp_w2", "final_norm"]
    shapes = {"e_w_in": e_w_in.shape, "e_w_uq": e_w_uq.shape, "e_w_ukv": e_w_ukv.shape, "e_w_out": e_w_out.shape,
              "o_w_in": o_w_in.shape, "o_w_out": o_w_out.shape, "mlp_w1": mlp_w1.shape, "mlp_w2": mlp_w2.shape}
    small_names = ["e_norm_mix", "e_q_norm", "e_kv_norm", "e_v_norm", "e_sgu_w", "e_sgu_b", "e_mla_out_norm",
                   "e_sgu_out_norm", "mlp_norm", "final_norm"]

    def leaf(kind, name):
        if name in big:
            return big[name][kind].reshape(shapes[name])
        if name == "o_norm_mix":
            return sm[kind][0:1]
        if name == "o_conv_w":
            return sm[kind][16:19].reshape(o_conv_w.shape)
        return s_out[kind][small_names.index(name)]

    outs = [loss, dx0.reshape(x.shape)]
    for kind in range(4):
        outs += [leaf(kind, nm) for nm in names]
    return tuple(outs)


def _gcd(a, b):
    while b:
        a, b = b, a % b
    return a
```

```python
import functools

import jax
import jax.numpy as jnp
from jax import lax
from jax.experimental import pallas as pl
from jax.experimental.pallas import tpu as pltpu

F32 = jnp.float32
BF16 = jnp.bfloat16
MESH = pl.DeviceIdType.MESH

LANES = 128
ROPE = 64
ROPE_HALF = ROPE // 2
ROPE_BASE = 10000.0
EPS = 1e-6
N_CHIPS = 4
VMEM_LIMIT = 48 * 1024 * 1024
NEG = -1e30

ADAM_LR = 0.001
ADAM_B1 = 0.9
ADAM_B2 = 0.999
ADAM_EPS = 1e-08
ADAM_WD = 0.01
ADAM_STEP = 10


def _pick(n, target, step=LANES):
    best = None
    for t in range(step, min(n, target) + 1, step):
        if n % t == 0:
            best = t
    return best if best is not None else n


def _params(sem, vmem=VMEM_LIMIT):
    return pltpu.CompilerParams(dimension_semantics=sem, vmem_limit_bytes=vmem)


class Mat:
    def __init__(self, arr, rows, cols, kind="plain", lead=(), col_off=0, shape=None, dtype=None):
        self.arr, self.rows, self.cols, self.kind, self.lead, self.col_off = arr, rows, cols, kind, tuple(lead), col_off
        self.shape = tuple(arr.shape) if arr is not None else tuple(shape)
        self.dtype = arr.dtype if arr is not None else dtype

    def sds(self):
        return jax.ShapeDtypeStruct(self.shape, self.dtype)

    def spec(self, br, bc, gridmap):
        lead, nl = self.lead, len(self.lead)
        if self.kind == "plain":
            assert self.col_off % bc == 0 and self.rows % br == 0 and self.cols % bc == 0, (self.shape, br, bc)
            off = self.col_off // bc
            block = (None,) * nl + (br, bc)

            def phys(rb, cb):
                return lead + (rb, cb + off)
        elif self.kind == "colstack":
            cs = self.shape[-1]
            assert cs % bc == 0 and self.rows % br == 0, (self.shape, br, bc)
            q = cs // bc
            block = (None,) * (nl + 1) + (br, bc)

            def phys(rb, cb):
                return (cb // q,) + lead + (rb, cb % q)
        else:
            rs = self.shape[-2]
            assert rs % br == 0 and self.cols % bc == 0, (self.shape, br, bc)
            q = rs // br
            block = (None,) * (nl + 1) + (br, bc)

            def phys(rb, cb):
                return (rb // q,) + lead + (rb % q, cb)

        return pl.BlockSpec(block, lambda *g: phys(*gridmap(*g)))


def _matmul(name, a, b, mode, outs, tm, tn, tk, epilogue=None, extras=(), deps=()):
    if mode == "nn":
        m, k, n = a.rows, a.cols, b.cols
        a_spec = a.spec(tm, tk, lambda i, j, kk: (i, kk))
        b_spec = b.spec(tk, tn, lambda i, j, kk: (kk, j))
        dims = (((1,), (0,)), ((), ()))
    elif mode == "nt":
        m, k, n = a.rows, a.cols, b.rows
        a_spec = a.spec(tm, tk, lambda i, j, kk: (i, kk))
        b_spec = b.spec(tn, tk, lambda i, j, kk: (j, kk))
        dims = (((1,), (1,)), ((), ()))
    else:
        k, m, n = a.rows, a.cols, b.cols
        a_spec = a.spec(tk, tm, lambda i, j, kk: (kk, i))
        b_spec = b.spec(tk, tn, lambda i, j, kk: (kk, j))
        dims = (((0,), (0,)), ((), ()))
    assert m % tm == 0 and n % tn == 0 and k % tk == 0, (name, m, n, k, tm, tn, tk)
    grid = (m // tm, n // tn, k // tk)
    nk = grid[2]
    n_ex, n_out, n_dep = len(extras), len(outs), len(deps)
    tile = lambda i, j, kk: (i, j)

    def finish(z, ex, out_refs):
        vals = epilogue(z, *[e[...] for e in ex]) if epilogue is not None else (z,)
        for o, v in zip(out_refs, vals):
            o[...] = v.astype(o.dtype)

    def body_single(a_ref, b_ref, *rest):
        finish(lax.dot_general(a_ref[...], b_ref[...], dims, preferred_element_type=F32),
               rest[:n_ex], rest[n_ex + n_dep:n_ex + n_dep + n_out])

    def body_acc(a_ref, b_ref, *rest):
        acc = rest[-1]
        kk = pl.program_id(2)

        @pl.when(kk == 0)
        def _():
            acc[...] = jnp.zeros_like(acc)

        acc[...] += lax.dot_general(a_ref[...], b_ref[...], dims, preferred_element_type=F32)

        @pl.when(kk == nk - 1)
        def _():
            finish(acc[...], rest[:n_ex], rest[n_ex + n_dep:n_ex + n_dep + n_out])

    res = pl.pallas_call(
        body_single if nk == 1 else body_acc, name=name, grid=grid,
        in_specs=[a_spec, b_spec] + [e.spec(tm, tn, tile) for e in extras]
        + [pl.BlockSpec(memory_space=pl.ANY) for _ in deps],
        out_specs=[o.spec(tm, tn, tile) for o in outs],
        out_shape=[o.sds() for o in outs],
        scratch_shapes=[] if nk == 1 else [pltpu.VMEM((tm, tn), F32)],
        compiler_params=_params(("parallel", "parallel", "arbitrary")),
    )(a.arr, b.arr, *[e.arr for e in extras], *deps)
    return res


def _out(rows, cols, dtype, kind="plain", lead=(), shape=None):
    return Mat(None, rows, cols, kind, lead, shape=shape if shape is not None else (rows, cols), dtype=dtype)


def _rt(arr, tr, width=None, cb=0):
    width = arr.shape[1] if width is None else width
    return arr, pl.BlockSpec((tr, width), lambda i: (i, cb))


def _whole(arr):
    nd = arr.ndim
    return arr, pl.BlockSpec(arr.shape, lambda i: (0,) * nd)


def _rowwise(name, fn, n_steps, ins, outs, accs=(), deps=(), fill=None):
    n_in, n_out, n_acc, n_dep = len(ins), len(outs), len(accs), len(deps)

    def body(*refs):
        vals = fn(*[r[...] for r in refs[:n_in]])
        if not isinstance(vals, (tuple, list)):
            vals = (vals,)
        for ref, v in zip(refs[n_in + n_dep:n_in + n_dep + n_out], vals[:n_out]):
            ref[...] = v.astype(ref.dtype)
        if n_acc:
            acc_refs = refs[n_in + n_dep + n_out:]

            @pl.when(pl.program_id(0) == 0)
            def _():
                for ref in acc_refs:
                    ref[...] = jnp.zeros_like(ref)

            for ref, v in zip(acc_refs, vals[n_out:]):
                ref[...] += v

    acc_specs = [pl.BlockSpec(s.shape, lambda i, nd=len(s.shape): (0,) * nd) for s in accs]
    res = pl.pallas_call(
        body, name=name, grid=(n_steps,),
        in_specs=[s for _, s in ins] + [pl.BlockSpec(memory_space=pl.ANY) for _ in deps],
        out_specs=[s for _, s in outs] + acc_specs,
        out_shape=[o for o, _ in outs] + list(accs),
        input_output_aliases={} if fill is None else {n_in + fill[0]: fill[1]},
        compiler_params=_params(("arbitrary",) if n_acc else ("parallel",)),
    )(*[a for a, _ in ins], *deps)
    return res


def _rt_out(t, width, dtype, tr):
    return jax.ShapeDtypeStruct((t, width), dtype), pl.BlockSpec((tr, width), lambda i: (i, 0))


def _rms(x, g):
    r = lax.rsqrt(jnp.mean(x * x, axis=-1, keepdims=True) + EPS)
    return x * r * g


def _rms_bwd(dy, x, g):
    r = lax.rsqrt(jnp.mean(x * x, axis=-1, keepdims=True) + EPS)
    xh = x * r
    dxh = dy * g
    dx = r * (dxh - xh * jnp.mean(dxh * xh, axis=-1, keepdims=True))
    dg = jnp.sum(dy * xh, axis=0, keepdims=True)
    return dx, dg


def _gelu(x):
    k = 0.7978845608028654
    th = jnp.tanh(k * (x + 0.044715 * (x * x * x)))
    return x * (0.5 * (1.0 + th))


def _gelu_grad(x):
    k = 0.7978845608028654
    x2 = x * x
    th = jnp.tanh(k * (x + 0.044715 * (x2 * x)))
    return 0.5 * (1.0 + th) + 0.5 * x * (1.0 - th * th) * (k * (1.0 + 3.0 * 0.044715 * x2))


def _norm_fwd(name, x, g, tr):
    t, d = x.shape
    return _rowwise(name, lambda xv, gv: _rms(xv, gv), t // tr, [_rt(x, tr), _whole(g)], [_rt_out(t, d, BF16, tr)])[0]


def _norm_bwd(name, dh, x, g, dres, tr):
    t, d = x.shape

    def fn(dhv, xv, gv, drv):
        dx, dg = _rms_bwd(dhv, xv, gv)
        dx = dx + drv
        return dx, dx, dg

    return _rowwise(name, fn, t // tr, [_rt(dh, tr), _rt(x, tr), _whole(g), _rt(dres, tr)],
                    [_rt_out(t, d, F32, tr), _rt_out(t, d, BF16, tr)], [jax.ShapeDtypeStruct((1, d), F32)])


def _rope_tables(posf, invf, cmask, smask, tr):
    t = posf.shape[0]

    def fn(p, f, cm, sm):
        ang = p * f
        return jnp.cos(ang) * cm, jnp.sin(ang) * sm

    return _rowwise("rope_tables", fn, t // tr, [_rt(posf, tr), _whole(invf), _whole(cmask), _whole(smask)],
                    [_rt_out(t, LANES, F32, tr), _rt_out(t, LANES, F32, tr)])


def _rot(v, c, s):
    return v * c + pltpu.roll(v, ROPE, axis=1) * s


def _rot_bwd(dv, c, s):
    return dv * c + pltpu.roll(dv * s, ROPE, axis=1)


def _rope_fwd(qfull, proj, kr_cb, ctab, stab, heads, tr):
    t = qfull.shape[0]
    hw = heads * LANES

    def fn(q, kr, c, s):
        parts = [q[:, :hw]] + [_rot(q[:, hw + h * LANES: hw + (h + 1) * LANES], c, s) for h in range(heads)]
        return jnp.concatenate(parts, axis=1), _rot(kr, c, s)

    return _rowwise("rope_fwd", fn, t // tr, [_rt(qfull, tr), _rt(proj, tr, LANES, kr_cb), _rt(ctab, tr), _rt(stab, tr)],
                    [_rt_out(t, 2 * hw, BF16, tr), _rt_out(t, LANES, BF16, tr)])


def _rope_bwd(dq1, dq2, dkr_h, ctab, stab, heads, tr, dproj, kr_cb):
    t = dq1.shape[0]
    hw = heads * LANES

    def fn(a, b, dk, c, s):
        parts = [a] + [_rot_bwd(b[:, h * LANES:(h + 1) * LANES], c, s) for h in range(heads)]
        dks = dk[0]
        for h in range(1, heads):
            dks = dks + dk[h]
        return jnp.concatenate(parts, axis=1), _rot_bwd(dks, c, s)

    dk_spec = pl.BlockSpec((heads, tr, LANES), lambda i: (0, i, 0))
    into = (jax.ShapeDtypeStruct(dproj.shape, dproj.dtype), pl.BlockSpec((tr, LANES), lambda i: (i, kr_cb)))
    return _rowwise("rope_bwd", fn, t // tr, [_rt(dq1, tr), _rt(dq2, tr), (dkr_h, dk_spec), _rt(ctab, tr), _rt(stab, tr)],
                    [_rt_out(t, 2 * hw, BF16, tr), into], deps=(dproj,), fill=(0, 1))


def _dot_nt(a, b):
    return lax.dot_general(a, b, (((1,), (1,)), ((), ())), preferred_element_type=F32)


def _dot_tn(a, b):
    return lax.dot_general(a, b, (((0,), (0,)), ((), ())), preferred_element_type=F32)


def _dot(a, b):
    return jnp.dot(a, b, preferred_element_type=F32)


def _ranges(n_blocks):
    n_var = min(4, n_blocks)
    assert n_blocks % n_var == 0
    return n_var, n_blocks // n_var


def _row_of(col):
    return col.T[:8, :]


def _attn_fwd(qall, kvall, kr, heads, scale, tq):
    t = qall.shape[0]
    nq = t // tq
    n_var, per = _ranges(nq)

    def body(qn_ref, qr_ref, kn_ref, v_ref, kr_ref, o_ref, lser_ref):
        i = pl.program_id(1)
        for var in range(n_var):
            kv = (var + 1) * per * tq

            @pl.when(jnp.logical_and(i >= var * per, i < (var + 1) * per))
            def _(kv=kv):
                s = (_dot_nt(qn_ref[...], kn_ref[:kv, :]) + _dot_nt(qr_ref[...], kr_ref[:kv, :])) * scale
                rows = i * tq + lax.broadcasted_iota(jnp.int32, (tq, kv), 0)
                cols = lax.broadcasted_iota(jnp.int32, (tq, kv), 1)
                s = jnp.where(cols <= rows, s, NEG)
                m = jnp.max(s, axis=-1, keepdims=True)
                p = jnp.exp(s - m)
                l = jnp.sum(p, axis=-1, keepdims=True)
                o_ref[...] = _dot(p.astype(BF16), v_ref[:kv, :]) / l
                lser_ref[...] = _row_of(jnp.broadcast_to(m + jnp.log(l), (tq, LANES)))

    return pl.pallas_call(
        body, name="attn_fwd", grid=(heads, nq),
        in_specs=[pl.BlockSpec((tq, LANES), lambda h, i: (i, h)),
                  pl.BlockSpec((tq, LANES), lambda h, i: (i, heads + h)),
                  pl.BlockSpec((t, LANES), lambda h, i: (0, h)),
                  pl.BlockSpec((t, LANES), lambda h, i: (0, heads + h)),
                  pl.BlockSpec((t, LANES), lambda h, i: (0, 0))],
        out_specs=[pl.BlockSpec((tq, LANES), lambda h, i: (i, h)),
                   pl.BlockSpec((None, 8, tq), lambda h, i: (h, 0, i))],
        out_shape=[jax.ShapeDtypeStruct((t, heads * LANES), F32), jax.ShapeDtypeStruct((heads, 8, t), F32)],
        compiler_params=_params(("parallel", "parallel")),
    )(qall, qall, kvall, kvall, kr)


def _attn_bwd(qall, kvall, kr, do, lse_row, delta_row, heads, scale, tk):
    t = qall.shape[0]
    nk = t // tk
    n_var, per = _ranges(nk)

    def body(qn_ref, qr_ref, kn_ref, v_ref, kr_ref, do_ref, lse_ref, dl_ref, dq1_ref, dq2_ref, dk_ref, dv_ref, dkr_ref):
        j = pl.program_id(1)

        @pl.when(j == 0)
        def _():
            dq1_ref[...] = jnp.zeros_like(dq1_ref)
            dq2_ref[...] = jnp.zeros_like(dq2_ref)

        for var in range(n_var):
            q0 = var * per * tk
            nq = t - q0

            @pl.when(jnp.logical_and(j >= var * per, j < (var + 1) * per))
            def _(q0=q0, nq=nq):
                qn, qr, do_v = qn_ref[q0:, :], qr_ref[q0:, :], do_ref[q0:, :]
                k1, k2 = kn_ref[...], kr_ref[...]
                st = (_dot_nt(k1, qn) + _dot_nt(k2, qr)) * scale
                keys = j * tk + lax.broadcasted_iota(jnp.int32, (tk, nq), 0)
                queries = q0 + lax.broadcasted_iota(jnp.int32, (tk, nq), 1)
                pt = jnp.where(keys <= queries, jnp.exp(st - lse_ref[0:1, q0:]), 0.0)
                dpt = _dot_nt(v_ref[...], do_v)
                dst = (pt * (dpt - dl_ref[0:1, q0:]) * scale).astype(BF16)
                dv_ref[...] = _dot(pt.astype(BF16), do_v).astype(dv_ref.dtype)
                dk_ref[...] = _dot(dst, qn).astype(dk_ref.dtype)
                dkr_ref[...] = _dot(dst, qr)
                dq1_ref[q0:, :] += _dot_tn(dst, k1)
                dq2_ref[q0:, :] += _dot_tn(dst, k2)

    kblk = lambda off: pl.BlockSpec((tk, LANES), lambda h, j: (j, off + h))
    full = lambda off: pl.BlockSpec((t, LANES), lambda h, j: (0, off + h))
    stat = pl.BlockSpec((None, 8, t), lambda h, j: (h, 0, 0))
    return pl.pallas_call(
        body, name="attn_bwd", grid=(heads, nk),
        in_specs=[full(0), full(heads), kblk(0), kblk(heads), pl.BlockSpec((tk, LANES), lambda h, j: (j, 0)),
                  full(0), stat, stat],
        out_specs=[full(0), full(0), kblk(0), kblk(0), pl.BlockSpec((None, tk, LANES), lambda h, j: (h, j, 0))],
        out_shape=[jax.ShapeDtypeStruct((t, heads * LANES), F32)] * 2 + [jax.ShapeDtypeStruct((t, heads * LANES), BF16)] * 2
        + [jax.ShapeDtypeStruct((heads, t, LANES), F32)],
        compiler_params=_params(("parallel", "arbitrary")),
    )(qall, qall, kvall, kvall, kr, do, lse_row, delta_row)


def _tril():
    return lax.broadcasted_iota(jnp.int32, (LANES, LANES), 0) >= lax.broadcasted_iota(jnp.int32, (LANES, LANES), 1)


def _group_norm(vg):
    mu = jnp.mean(vg, axis=-1, keepdims=True)
    vc = vg - mu
    rs = lax.rsqrt(jnp.mean(vc * vc, axis=-1, keepdims=True) + EPS)
    return vc * rs, rs


def _sgu_fwd(proj, gain, w, bias, groups, rb):
    t = proj.shape[0]
    gw = groups * LANES
    cpb = rb // LANES

    def body(u_ref, v_ref, gain_ref, w_ref, b_ref, s_ref):
        tril = _tril()
        for g in range(groups):
            wt = jnp.where(tril, w_ref[g], 0.0).astype(BF16)
            cols = slice(g * LANES, (g + 1) * LANES)
            for ci in range(cpb):
                rows = slice(ci * LANES, (ci + 1) * LANES)
                ug = _gelu(u_ref[rows, cols])
                vh, _ = _group_norm(_gelu(v_ref[rows, cols]))
                vn = vh * gain_ref[:, cols]
                y = _dot(wt, vn.astype(BF16)) + b_ref[g]
                s_ref[rows, cols] = ug * y

    return pl.pallas_call(
        body, name="sgu_fwd", grid=(t // rb,),
        in_specs=[pl.BlockSpec((rb, gw), lambda i: (i, 0)), pl.BlockSpec((rb, gw), lambda i: (i, 1)),
                  pl.BlockSpec((1, gw), lambda i: (0, 0)),
                  pl.BlockSpec((groups, LANES, LANES), lambda i: (0, 0, 0)),
                  pl.BlockSpec((groups, LANES, LANES), lambda i: (0, 0, 0))],
        out_specs=pl.BlockSpec((rb, gw), lambda i: (i, 0)),
        out_shape=jax.ShapeDtypeStruct((t, gw), F32),
        compiler_params=_params(("parallel",)),
    )(proj, proj, gain, w, bias)


def _sgu_bwd(proj, ds, gain, w, bias, groups, rb):
    t, width = proj.shape
    gw = groups * LANES
    cpb = rb // LANES
    n_steps = t // rb

    def body(u_ref, v_ref, ds_ref, gain_ref, w_ref, b_ref, dp_ref, dw_ref, db_ref, dg_ref, dy_acc):
        du_ref, dv_ref = dp_ref.at[:, :gw], dp_ref.at[:, gw:]
        step = pl.program_id(0)

        @pl.when(step == 0)
        def _():
            dw_ref[...] = jnp.zeros_like(dw_ref)
            dy_acc[...] = jnp.zeros_like(dy_acc)
            dg_ref[...] = jnp.zeros_like(dg_ref)

        tril = _tril()
        for g in range(groups):
            wt = jnp.where(tril, w_ref[g], 0.0).astype(BF16)
            cols = slice(g * LANES, (g + 1) * LANES)
            gain_g = gain_ref[:, cols]
            for ci in range(cpb):
                rows = slice(ci * LANES, (ci + 1) * LANES)
                u_raw, v_raw, ds_v = u_ref[rows, cols], v_ref[rows, cols], ds_ref[rows, cols]
                ug = _gelu(u_raw)
                vh, rs = _group_norm(_gelu(v_raw))
                vn = (vh * gain_g).astype(BF16)
                y = _dot(wt, vn) + b_ref[g]
                dy = ds_v * ug
                dyb = dy.astype(BF16)
                du_ref[rows, cols] = (ds_v * y * _gelu_grad(u_raw)).astype(du_ref.dtype)
                dy_acc[g] += dy
                dw_ref[g] += _dot_nt(dyb, vn)
                dvn = _dot_tn(wt, dyb)
                dg_ref[:, cols] += jnp.sum(dvn * vh, axis=0, keepdims=True)
                dvh = dvn * gain_g
                dvg = rs * (dvh - jnp.mean(dvh, axis=-1, keepdims=True)
                            - vh * jnp.mean(dvh * vh, axis=-1, keepdims=True))
                dv_ref[rows, cols] = (dvg * _gelu_grad(v_raw)).astype(dv_ref.dtype)

        @pl.when(step == n_steps - 1)
        def _():
            ones = jnp.ones((8, LANES), F32)
            for g in range(groups):
                dw_ref[g] = jnp.where(tril, dw_ref[g], 0.0)
                db_ref[g] = lax.dot_general(ones, dy_acc[g], (((1,), (1,)), ((), ())),
                                            precision=lax.Precision.HIGHEST, preferred_element_type=F32)

    blk = lambda cb: pl.BlockSpec((rb, gw), lambda i: (i, cb))
    whole3 = pl.BlockSpec((groups, LANES, LANES), lambda i: (0, 0, 0))
    return pl.pallas_call(
        body, name="sgu_bwd", grid=(n_steps,),
        in_specs=[blk(0), blk(1), blk(0), pl.BlockSpec((1, gw), lambda i: (0, 0)), whole3, whole3],
        out_specs=[pl.BlockSpec((rb, 2 * gw), lambda i: (i, 0)), whole3,
                   pl.BlockSpec((groups, 8, LANES), lambda i: (0, 0, 0)), pl.BlockSpec((1, gw), lambda i: (0, 0))],
        out_shape=[jax.ShapeDtypeStruct((t, width), BF16),
                   jax.ShapeDtypeStruct((groups, LANES, LANES), F32), jax.ShapeDtypeStruct((groups, 8, LANES), F32),
                   jax.ShapeDtypeStruct((1, gw), F32)],
        scratch_shapes=[pltpu.VMEM((groups, LANES, LANES), F32)],
        compiler_params=_params(("arbitrary",)),
    )(proj, proj, ds, gain, w, bias)


def _shift_down(z, s):
    rows = lax.broadcasted_iota(jnp.int32, z.shape, 0)
    return jnp.where(rows >= s, pltpu.roll(z, s, axis=0), 0.0)


def _shift_up(z, s):
    n = z.shape[0]
    rows = lax.broadcasted_iota(jnp.int32, z.shape, 0)
    return jnp.where(rows < n - s, pltpu.roll(z, n - s, axis=0), 0.0)


def _conv_fwd(proj3, cw, tc):
    _, t, cd = proj3.shape

    def body(p_ref, w_ref, o_ref):
        z = p_ref[1] * p_ref[2]
        w = w_ref[...]
        zc = w[2:3] * z + w[1:2] * _shift_down(z, 1) + w[0:1] * _shift_down(z, 2)
        o_ref[...] = (p_ref[0] * zc).astype(o_ref.dtype)

    return pl.pallas_call(
        body, name="conv_fwd", grid=(cd // tc,),
        in_specs=[pl.BlockSpec((3, t, tc), lambda j: (0, 0, j)), pl.BlockSpec((8, tc), lambda j: (0, j))],
        out_specs=pl.BlockSpec((t, tc), lambda j: (0, j)),
        out_shape=jax.ShapeDtypeStruct((t, cd), BF16),
        compiler_params=_params(("parallel",)),
    )(proj3, cw)


def _conv_bwd(proj3, cw, dbz, tc):
    _, t, cd = proj3.shape

    def body(p_ref, w_ref, d_ref, o_ref, dw_ref):
        b, c, xin = p_ref[0], p_ref[1], p_ref[2]
        w = w_ref[...]
        z = c * xin
        z1, z2 = _shift_down(z, 1), _shift_down(z, 2)
        zc = w[2:3] * z + w[1:2] * z1 + w[0:1] * z2
        d = d_ref[...]
        dzc = d * b
        dz = w[2:3] * dzc + w[1:2] * _shift_up(dzc, 1) + w[0:1] * _shift_up(dzc, 2)
        o_ref[0] = (d * zc).astype(o_ref.dtype)
        o_ref[1] = (dz * xin).astype(o_ref.dtype)
        o_ref[2] = (dz * c).astype(o_ref.dtype)
        row = lax.broadcasted_iota(jnp.int32, (8, tc), 0)
        dw0 = jnp.sum(dzc * z2, axis=0, keepdims=True)
        dw1 = jnp.sum(dzc * z1, axis=0, keepdims=True)
        dw2 = jnp.sum(dzc * z, axis=0, keepdims=True)
        dw_ref[...] = jnp.where(row == 0, dw0, 0.0) + jnp.where(row == 1, dw1, 0.0) + jnp.where(row == 2, dw2, 0.0)

    return pl.pallas_call(
        body, name="conv_bwd", grid=(cd // tc,),
        in_specs=[pl.BlockSpec((3, t, tc), lambda j: (0, 0, j)), pl.BlockSpec((8, tc), lambda j: (0, j)),
                  pl.BlockSpec((t, tc), lambda j: (0, j))],
        out_specs=[pl.BlockSpec((3, t, tc), lambda j: (0, 0, j)), pl.BlockSpec((8, tc), lambda j: (0, j))],
        out_shape=[jax.ShapeDtypeStruct((3, t, cd), BF16), jax.ShapeDtypeStruct((8, cd), F32)],
        compiler_params=_params(("parallel",)),
    )(proj3, cw, dbz)


def _place():
    x, y, c = lax.axis_index("x"), lax.axis_index("y"), lax.axis_index("c")
    chips = [(1 - x, y), (x, 1 - y), (1 - x, 1 - y)]
    return x, y, c, chips


def _any_specs(n):
    return [pl.BlockSpec(memory_space=pl.ANY) for _ in range(n)]


HBM_SPEC = pl.BlockSpec(memory_space=pltpu.HBM)
SEM_SPEC = pl.BlockSpec(memory_space=pltpu.SEMAPHORE)
ORDERED_EFFECT = pltpu.SideEffectType.DATAFLOW_SIDE_EFFECTING


def _in_hbm(a):
    return pltpu.with_memory_space_constraint(a, pltpu.HBM)


def _token():
    return jax.ShapeDtypeStruct((8, LANES), F32), pl.BlockSpec(memory_space=pltpu.VMEM)


def _gather_start(name, groups):
    sizes = [len(g) for g in groups]
    flat = [b for g in groups for b in g]
    n, ng = len(flat), len(groups)

    def body(*refs):
        ins, sems, token = refs[:n], refs[n:n + 2 * ng], refs[-1]
        x, y, c, chips = _place()
        me = 2 * x + y
        i = 0
        for gi, size in enumerate(sizes):
            for j in range(size):
                blk = ins[i].at[me, c]
                for k, chip in enumerate(chips):
                    pltpu.make_async_remote_copy(src_ref=blk, dst_ref=blk, send_sem=sems[2 * gi].at[3 * j + k],
                                                 recv_sem=sems[2 * gi + 1].at[3 * j + k],
                                                 device_id=(*chip, c), device_id_type=MESH).start()
                i += 1
        token[...] = jnp.zeros_like(token)

    tok_shape, tok_spec = _token()
    res = pl.pallas_call(
        body, name=name,
        in_specs=[HBM_SPEC] * n,
        out_specs=[SEM_SPEC] * (2 * ng) + [HBM_SPEC] * n + [tok_spec],
        out_shape=[pltpu.SemaphoreType.DMA((3 * size,)) for size in sizes for _ in (0, 1)]
        + [pltpu.HBM(b.shape, b.dtype) for b in flat] + [tok_shape],
        input_output_aliases={i: 2 * ng + i for i in range(n)},
        compiler_params=pltpu.CompilerParams(has_side_effects=ORDERED_EFFECT),
    )(*[_in_hbm(b) for b in flat])
    out, i = [], 2 * ng
    for gi, size in enumerate(sizes):
        out.append((res[2 * gi], res[2 * gi + 1], list(res[i:i + size])))
        i += size
    return out, res[-1]


def _gather_wait(tag, send, recv, bufs, after):
    n = len(bufs)
    after = tuple(after) if isinstance(after, (tuple, list)) else (after,)

    def body(*refs):
        ins, send_ref, recv_ref = refs[:n], refs[n], refs[n + 1]
        x, y, c, chips = _place()
        me = 2 * x + y
        for j in range(n):
            for k, (px, py) in enumerate(chips):
                cp = pltpu.make_async_remote_copy(src_ref=ins[j].at[me, c], dst_ref=ins[j].at[2 * px + py, c],
                                                  send_sem=send_ref.at[3 * j + k], recv_sem=recv_ref.at[3 * j + k],
                                                  device_id=(px, py, c), device_id_type=MESH)
                cp.wait_send()
                cp.wait_recv()

    return pl.pallas_call(
        body, name="gather_wait_" + tag,
        in_specs=[HBM_SPEC] * n + [SEM_SPEC, SEM_SPEC] + _any_specs(len(after)),
        out_specs=[HBM_SPEC] * n,
        out_shape=[pltpu.HBM(b.shape, b.dtype) for b in bufs],
        input_output_aliases={i: i for i in range(n)},
        compiler_params=pltpu.CompilerParams(has_side_effects=ORDERED_EFFECT),
    )(*bufs, send, recv, *after)


def _gather_forward(tag, bufs):
    n = len(bufs)

    def body(*refs):
        ins, outs = refs[:n], refs[n:2 * n]
        send, recv = refs[2 * n:]
        x, y, c, chips = _place()
        sib = (x, y, 1 - c)

        def cp(i, k, slot, half):
            return pltpu.make_async_remote_copy(src_ref=ins[i].at[slot, half], dst_ref=outs[i].at[slot, half],
                                                send_sem=send.at[3 * i + k], recv_sem=recv.at[3 * i + k],
                                                device_id=sib, device_id_type=MESH)

        cps = [cp(i, k, 2 * px + py, c) for i in range(n) for k, (px, py) in enumerate(chips)]
        for d in cps:
            d.start()
        for i in range(n):
            for k, (px, py) in enumerate(chips):
                cp(i, k, 2 * px + py, 1 - c).wait_recv()
        for d in cps:
            d.wait_send()

    return pl.pallas_call(
        body, name="gather_forward_" + tag,
        in_specs=_any_specs(n), out_specs=_any_specs(n),
        out_shape=[jax.ShapeDtypeStruct(b.shape, b.dtype) for b in bufs],
        scratch_shapes=[pltpu.SemaphoreType.DMA((3 * n,))] * 2,
        input_output_aliases={i: i for i in range(n)},
        compiler_params=pltpu.CompilerParams(has_side_effects=True),
    )(*bufs)


def _pair_route(srcs, zones):
    x, y, c, _ = _place()
    return [(srcs[i].at[j, 1 - c], zones[i].at[j], (x, y, 1 - c)) for i in range(len(srcs)) for j in range(N_CHIPS)]


def _chip_route(srcs, zones):
    x, y, c, chips = _place()
    return [(srcs[i].at[2 * px + py], zones[i].at[k], (px, py, c)) for i in range(len(srcs)) for k, (px, py) in enumerate(chips)]


def _all_route(srcs, zones):
    x, y, c, _ = _place()
    flips = [(fx, fy, fc) for fx in (0, 1) for fy in (0, 1) for fc in (0, 1)][1:]
    return [(srcs[0], zones[0].at[4 * x + 2 * y + c], (x + fx - 2 * x * fx, y + fy - 2 * y * fy, c + fc - 2 * c * fc))
            for fx, fy, fc in flips]


def _share_route(srcs, zones):
    x, y, c, _ = _place()
    return [(s.at[c], s.at[c], (x, y, 1 - c)) for s in srcs]


def _exchange_start(name, route, n_copies, srcs, zones):
    n, nz = len(srcs), len(zones)
    lands = [lax.empty(z, a.dtype) if isinstance(z, tuple) else z for z, a in zip(zones, srcs)]

    def body(*refs):
        ins, zone_refs, send, recv, token = refs[:n], refs[n:n + nz], refs[n + nz], refs[n + nz + 1], refs[-1]
        for k, (src, dst, dev) in enumerate(route(ins, zone_refs)):
            pltpu.make_async_remote_copy(src_ref=src, dst_ref=dst, send_sem=send.at[k], recv_sem=recv.at[k],
                                         device_id=dev, device_id_type=MESH).start()
        token[...] = jnp.zeros_like(token)

    tok_shape, tok_spec = _token()
    res = pl.pallas_call(
        body, name=name,
        in_specs=[HBM_SPEC] * (n + nz),
        out_specs=[SEM_SPEC, SEM_SPEC] + [HBM_SPEC] * (n + nz) + [tok_spec],
        out_shape=[pltpu.SemaphoreType.DMA((n_copies,))] * 2 + [pltpu.HBM(a.shape, a.dtype) for a in srcs + lands]
        + [tok_shape],
        input_output_aliases={i: 2 + i for i in range(n + nz)},
        compiler_params=pltpu.CompilerParams(has_side_effects=ORDERED_EFFECT),
    )(*[_in_hbm(a) for a in srcs + lands])
    return (res[0], res[1], list(res[2:2 + n]), list(res[2 + n:2 + n + nz])), res[-1]


def _exchange_wait(name, route, started, after):
    send, recv, srcs, lands = started
    n, nz = len(srcs), len(lands)
    after = tuple(after) if isinstance(after, (tuple, list)) else (after,)

    def body(*refs):
        ins, zone_refs, send_ref, recv_ref = refs[:n], refs[n:n + nz], refs[n + nz], refs[n + nz + 1]
        for k, (src, dst, dev) in enumerate(route(ins, zone_refs)):
            cp = pltpu.make_async_remote_copy(src_ref=src, dst_ref=dst, send_sem=send_ref.at[k], recv_sem=recv_ref.at[k],
                                              device_id=dev, device_id_type=MESH)
            cp.wait_send()
            cp.wait_recv()

    res = pl.pallas_call(
        body, name=name,
        in_specs=[HBM_SPEC] * (n + nz) + [SEM_SPEC, SEM_SPEC] + _any_specs(len(after)),
        out_specs=[HBM_SPEC] * (n + nz),
        out_shape=[pltpu.HBM(a.shape, a.dtype) for a in srcs + lands],
        input_output_aliases={i: i for i in range(n + nz)},
        compiler_params=pltpu.CompilerParams(has_side_effects=ORDERED_EFFECT),
    )(*srcs, *lands, send, recv, *after)
    return list(res[:n]), list(res[n:])


def _spread(v):
    rows, cols = v.shape
    tr = _row_tile(rows, cols, budget=256 * 1024)

    def body(v_ref, o_ref):
        o_ref[...] = jnp.broadcast_to(v_ref[...][None], o_ref.shape)

    return pl.pallas_call(body, name="spread_small_grads", grid=(rows // tr,),
                          in_specs=[pl.BlockSpec((tr, cols), lambda r: (r, 0))],
                          out_specs=pl.BlockSpec((8, tr, cols), lambda r: (0, r, 0)),
                          out_shape=jax.ShapeDtypeStruct((8, rows, cols), v.dtype),
                          compiler_params=_params(("parallel",)))(v)


def _row_tile(rows, cols, itemsize=4, budget=2 * 1024 * 1024):
    best = None
    for t in range(8, rows + 1, 8):
        if rows % t == 0 and t * cols * itemsize <= budget:
            best = t
    return best if best is not None else rows


def _my_chip():
    return 2 * lax.axis_index("x") + lax.axis_index("y")


def _pair_sum(g5, gsib):
    _, _, rh, cols = g5.shape
    tr = _row_tile(rh, cols)

    def body(a_ref, b_ref, o_ref):
        o_ref[...] = (a_ref[...].astype(F32) + b_ref[...].astype(F32)).astype(o_ref.dtype)

    return pl.pallas_call(body, name="grad_pair_sum", grid=(N_CHIPS, rh // tr),
                          in_specs=[pl.BlockSpec((None, None, tr, cols), lambda j, r: (j, lax.axis_index("c"), r, 0)),
                                    pl.BlockSpec((None, tr, cols), lambda j, r: (j, r, 0))],
                          out_specs=pl.BlockSpec((None, tr, cols), lambda j, r: (j, r, 0)),
                          out_shape=jax.ShapeDtypeStruct((N_CHIPS, rh, cols), BF16),
                          compiler_params=_params(("parallel", "parallel")))(g5, gsib)


def _chip_sum(part, recv):
    _, rh, cols = part.shape
    tr = _row_tile(rh, cols)

    def body(a_ref, b_ref, o_ref):
        acc = a_ref[...].astype(F32)
        for k in range(3):
            acc = acc + b_ref[k].astype(F32)
        o_ref[...] = acc

    return pl.pallas_call(body, name="grad_chip_sum", grid=(rh // tr,),
                          in_specs=[pl.BlockSpec((None, tr, cols), lambda r: (_my_chip(), r, 0)),
                                    pl.BlockSpec((3, tr, cols), lambda r: (0, r, 0))],
                          out_specs=pl.BlockSpec((None, tr, cols), lambda r: (lax.axis_index("c"), r, 0)),
                          out_shape=jax.ShapeDtypeStruct((2, rh, cols), F32),
                          compiler_params=_params(("parallel",)))(part, recv)


def _sum_devices(g):
    _, rows, cols = g.shape
    tr = _row_tile(rows, cols, budget=256 * 1024)

    def body(g_ref, o_ref):
        acc = g_ref[0]
        for d in range(1, 8):
            acc = acc + g_ref[d]
        o_ref[...] = acc

    return pl.pallas_call(body, name="sum_small_grads", grid=(rows // tr,),
                          in_specs=[pl.BlockSpec((8, tr, cols), lambda r: (0, r, 0))],
                          out_specs=pl.BlockSpec((tr, cols), lambda r: (r, 0)),
                          out_shape=jax.ShapeDtypeStruct((rows, cols), F32),
                          compiler_params=_params(("parallel",)))(g)


def _place_shard(w, layer, dtype, deps=()):
    _, rows, cols = w.shape
    tr = _row_tile(rows, cols)

    def body(i_ref, *rest):
        o_ref = rest[-1]
        o_ref[...] = i_ref[...].astype(o_ref.dtype)

    out = pl.pallas_call(body, name="place_shard", grid=(rows // tr,),
                         in_specs=[pl.BlockSpec((None, tr, cols), lambda r: (layer, r, 0))] + _any_specs(len(deps)),
                         out_specs=pl.BlockSpec((None, tr, cols), lambda r: (_my_chip(), r, 0)),
                         out_shape=jax.ShapeDtypeStruct((N_CHIPS, rows, cols), dtype),
                         compiler_params=_params(("parallel",)))(w, *deps)
    return out.reshape(N_CHIPS, 2, rows // 2, cols)


def _adamw(w, gs, m, v):
    n_layers, rows, cols = w.shape
    tr = _row_tile(rows, cols)

    def body(w_ref, m_ref, v_ref, *rest):
        g_refs = rest[:n_layers]
        go_ref, d_ref, mo_ref, vo_ref = rest[n_layers:]
        gv = g_refs[0][...]
        for layer in range(1, n_layers):
            gv = jnp.where(pl.program_id(0) == layer, g_refs[layer][...], gv)
        mn = ADAM_B1 * m_ref[...] + (1.0 - ADAM_B1) * gv
        vn = ADAM_B2 * v_ref[...] + (1.0 - ADAM_B2) * jnp.square(gv)
        m_hat = mn / (1.0 - ADAM_B1 ** ADAM_STEP)
        v_hat = vn / (1.0 - ADAM_B2 ** ADAM_STEP)
        d_ref[...] = -ADAM_LR * (m_hat / (jnp.sqrt(v_hat) + ADAM_EPS) + ADAM_WD * w_ref[...])
        go_ref[...] = gv
        mo_ref[...] = mn
        vo_ref[...] = vn

    spec = pl.BlockSpec((None, tr, cols), lambda layer, r: (layer, r, 0))
    g_specs = [pl.BlockSpec((tr, cols), lambda layer, r, own=own: (jnp.where(layer == own, r, 0), 0))
               for own in range(n_layers)]
    return pl.pallas_call(body, name="adamw", grid=(n_layers, rows // tr), in_specs=[spec] * 3 + g_specs,
                          out_specs=[spec] * 4, out_shape=[jax.ShapeDtypeStruct((n_layers, rows, cols), F32)] * 4,
                          compiler_params=_params(("parallel", "parallel")))(w, m, v, *gs)


def _pad_rope(w):
    z = jnp.zeros(w.shape[:-1] + (ROPE_HALF,), w.dtype)
    return jnp.concatenate([w[..., :ROPE_HALF], z, w[..., ROPE_HALF:], z], axis=-1)


def _unpad_rope(g):
    return jnp.concatenate([g[..., :ROPE_HALF], g[..., ROPE:ROPE + ROPE_HALF]], axis=-1)


def _unstack_cols(s):
    n, r, cs = s.shape
    return jnp.transpose(s, (1, 0, 2)).reshape(r, n * cs)


def _stack_cols(f):
    r, cfull = f.shape
    return jnp.transpose(f.reshape(r, N_CHIPS, cfull // N_CHIPS), (1, 0, 2))


def _small_shard(norm, conv):
    return jnp.concatenate([jnp.pad(norm, ((0, 15), (0, 0))), jnp.pad(conv, ((0, 13), (0, 0)))], axis=0)


def _flat_rows(a):
    return a.reshape(-1, LANES)


def _pack_small(arrs):
    return jnp.concatenate([_flat_rows(a.astype(F32)) for a in arrs], axis=0)


def _unpack_small(flat, like):
    out, r = [], 0
    for a in like:
        n = a.size // LANES
        out.append(flat[r:r + n].reshape(a.shape))
        r += n
    return out


def kernel(x, positions, e_norm_mix, e_w_in, e_q_norm, e_w_uq, e_kv_norm, e_w_ukv, e_v_norm, e_sgu_w, e_sgu_b, e_mla_out_norm, e_sgu_out_norm, e_w_out, o_norm_mix, o_w_in, o_conv_w, o_w_out, mlp_norm, mlp_w1, mlp_w2, final_norm, loss_target, m_e_norm_mix, m_e_w_in, m_e_q_norm, m_e_w_uq, m_e_kv_norm, m_e_w_ukv, m_e_v_norm, m_e_sgu_w, m_e_sgu_b, m_e_mla_out_norm, m_e_sgu_out_norm, m_e_w_out, m_o_norm_mix, m_o_w_in, m_o_conv_w, m_o_w_out, m_mlp_norm, m_mlp_w1, m_mlp_w2, m_final_norm, v_e_norm_mix, v_e_w_in, v_e_q_norm, v_e_w_uq, v_e_kv_norm, v_e_w_ukv, v_e_v_norm, v_e_sgu_w, v_e_sgu_b, v_e_mla_out_norm, v_e_sgu_out_norm, v_e_w_out, v_o_norm_mix, v_o_w_in, v_o_conv_w, v_o_w_out, v_mlp_norm, v_mlp_w1, v_mlp_w2, v_final_norm):
    t, d = x.shape[1], x.shape[2]
    ql, kvl = e_q_norm.shape[1], e_kv_norm.shape[1]
    groups = e_v_norm.shape[1]
    gw = groups * LANES
    heads = N_CHIPS * e_w_uq.shape[2] // (LANES + ROPE)
    hw = heads * LANES
    mix = hw + gw
    ei = N_CHIPS * e_w_in.shape[2]
    cd = N_CHIPS * o_conv_w.shape[2]
    ff = N_CHIPS * mlp_w1.shape[2]
    ffs = ff // N_CHIPS
    pi = 2 * gw + ql + kvl + LANES
    assert e_norm_mix.shape[0] == 1 and o_norm_mix.shape[0] == 1 and mlp_norm.shape[0] == 2
    assert ei == ql + kvl + ROPE + 2 * gw and cd == d and e_sgu_w.shape[2] == LANES
    assert (2 * gw) % ql == 0 and (2 * gw + ql) % kvl == 0 and t % LANES == 0
    scale = (LANES + ROPE) ** -0.5

    tr = min(256, t)
    tm = _pick(t, 1024, 8)
    kt, kd = _pick(t, 2048, 8), _pick(d, 2048)
    xs = x.reshape(t, d)
    tgt = loss_target.reshape(t, d)

    small_shard = _small_shard(o_norm_mix, o_conv_w[0])
    first, tok = _gather_start("gather_start_e", [
        [_place_shard(e_w_in, 0, BF16)],
        [_place_shard(e_w_uq, 0, BF16), _place_shard(e_w_ukv, 0, BF16), _place_shard(e_w_out, 0, BF16),
         _place_shard(small_shard[None], 0, F32)]])
    rest, tok = _gather_start("gather_start_rest", [
        [_place_shard(mlp_w1, 0, BF16, (tok,))], [_place_shard(mlp_w2, 0, BF16, (tok,))],
        [_place_shard(o_w_in, 0, BF16, (tok,)), _place_shard(o_w_out, 0, BF16, (tok,))],
        [_place_shard(mlp_w1, 1, BF16, (tok,))], [_place_shard(mlp_w2, 1, BF16, (tok,))]])
    started = first + rest

    def gathered(gi, tag, after):
        send, recv, bufs = started[gi]
        bufs = _gather_forward(tag, _gather_wait(tag, send, recv, bufs, after))
        return [b.reshape(N_CHIPS, 2 * b.shape[2], b.shape[3]) for b in bufs]

    g_e = e_norm_mix
    h0 = _norm_fwd("e_norm", xs, g_e, tr)
    inv_freq = ROPE_BASE ** (-jnp.arange(0, ROPE, 2, dtype=F32) / ROPE)
    zeros32 = jnp.zeros((ROPE_HALF,), F32)
    ones32 = jnp.ones((ROPE_HALF,), F32)
    invf = jnp.concatenate([inv_freq, zeros32, inv_freq, zeros32]).reshape(1, LANES)
    cmask = jnp.concatenate([ones32, zeros32, ones32, zeros32]).reshape(1, LANES)
    smask = jnp.concatenate([-ones32, zeros32, ones32, zeros32]).reshape(1, LANES)
    ctab, stab = _rope_tables(positions.reshape(t, 1).astype(F32), invf, cmask, smask, tr)

    w_in_g, = gathered(0, "e_in", (h0, ctab, tok))
    full = _unstack_cols(w_in_g)
    c2, c3 = ql + kvl, ql + kvl + ROPE
    w_in_all = jnp.concatenate([full[:, c3:], full[:, :c2], _pad_rope(full[:, c2:c3])], axis=1)
    proj, = _matmul("e_proj", Mat(h0, t, d), Mat(w_in_all, d, pi), "nn", [_out(t, pi, F32)], tm, _pick(pi, 1024), kd)

    w_uq_g, w_ukv_g, w_eout_g, small_g = gathered(1, "e", proj)
    full = _unstack_cols(w_uq_g).reshape(ql, heads, LANES + ROPE)
    w_q_all = jnp.concatenate([full[:, :, :LANES].reshape(ql, hw), _pad_rope(full[:, :, LANES:]).reshape(ql, hw)], axis=1)
    full = _unstack_cols(w_ukv_g).reshape(kvl, heads, 2 * LANES)
    w_kv_all = jnp.concatenate([full[:, :, :LANES].reshape(kvl, hw), full[:, :, LANES:].reshape(kvl, hw)], axis=1)
    w_eout = w_eout_g.reshape(mix, d)
    g_o = small_g[:, 0].reshape(1, d)
    conv_w = jnp.pad(jnp.transpose(small_g[:, 16:19], (1, 0, 2)).reshape(3, cd), ((0, 5), (0, 0)))

    g_q, g_kv = e_q_norm, e_kv_norm
    g_vn = e_v_norm.reshape(1, gw)
    sgu_w = e_sgu_w[0]
    sgu_b = jnp.broadcast_to(e_sgu_b[0][:, :, None], (groups, LANES, LANES))
    g_mla, g_sgu = e_mla_out_norm, e_sgu_out_norm
    g_m0, g_m1 = mlp_norm[0:1], mlp_norm[1:2]
    g_f = final_norm.reshape(1, d)

    def mlp_fwd(tag, xin, g, gi):
        hm = _norm_fwd("mlp_norm_" + tag, xin, g, tr)
        tn = _pick(ffs, 1024)
        w1 = Mat(gathered(gi, "w1_" + tag, hm)[0], d, ff, "colstack")
        a, act = _matmul("mlp_up_" + tag, Mat(hm, t, d), w1, "nn",
                         [_out(t, ff, BF16), _out(t, ff, BF16)], tm, tn, kd,
                         epilogue=lambda z: (jnp.maximum(z, 0.0), jnp.square(jnp.maximum(z, 0.0))))
        w2 = Mat(gathered(gi + 1, "w2_" + tag, act)[0].reshape(ff, d), ff, d)
        xo, = _matmul("mlp_down_" + tag, Mat(act, t, ff), w2, "nn",
                      [_out(t, d, F32)], tm, _pick(d, 1024), _pick(ffs, 2048),
                      epilogue=lambda z, r: (z + r,), extras=[Mat(xin, t, d)])
        return xo, hm, a, act, w1, w2

    def chip_start(tag, part):
        return _exchange_start("scatter_start_" + tag, _chip_route, 3 * len(part), part, [(3,) + p.shape[1:] for p in part])

    def pair_start(tag, stacked):
        g5 = [g.reshape(N_CHIPS, 2, g.shape[1] // 2, g.shape[2]) for g in stacked]
        return _exchange_start("pair_start_" + tag, _pair_route, N_CHIPS * len(g5), g5,
                               [(N_CHIPS,) + g.shape[2:] for g in g5])

    def pair_finish(tag, started, after):
        g5, from_sib = _exchange_wait("pair_wait_" + tag, _pair_route, started, after)
        return chip_start(tag, [_pair_sum(a, b) for a, b in zip(g5, from_sib)])

    def mlp_bwd(tag, dx, dxb, xin, g, w1, w2, hm, a, act, deps, extra_grads=()):
        tn = _pick(ffs, 1024)
        dz, = _matmul("mlp_dact_" + tag, Mat(dxb, t, d), w2, "nt",
                      [_out(t, ff, BF16)], tm, tn, kd,
                      epilogue=lambda z, av: (z * (2.0 * av.astype(F32)),), extras=[Mat(a, t, ff)], deps=deps)
        dw2, = _matmul("mlp_dw2_" + tag, Mat(act, t, ff), Mat(dxb, t, d), "tn",
                       [_out(ff, d, BF16)], tn, _pick(d, 1024), kt)
        dw1, = _matmul("mlp_dw1_" + tag, Mat(hm, t, d), Mat(dz, t, ff), "tn",
                       [_out(d, ff, BF16, "colstack", (), (N_CHIPS, d, ffs))], _pick(d, 1024), tn, kt)
        started, tok = pair_start("m" + tag, [dw1, dw2.reshape(N_CHIPS, ffs, d), *extra_grads])
        dhm, = _matmul("mlp_dh_" + tag, Mat(dz, t, ff), w1, "nt",
                       [_out(t, d, F32)], tm, _pick(d, 1024), _pick(ffs, 2048), deps=(tok,))
        dxo, dxob, dg = _norm_bwd("mlp_norm_bwd_" + tag, dhm, xin, g, dx, tr)
        sc, tok = pair_finish("m" + tag, started, dxo)
        return dxo, dxob, dg, sc, tok

    cq_cb, ckv_cb, kr_cb = 2 * gw // ql, (2 * gw + ql) // kvl, (2 * gw + ql + kvl) // LANES
    qn, kvn = _rowwise("qkv_norm", lambda a, b, ga, gb: (_rms(a, ga), _rms(b, gb)), t // tr,
                       [_rt(proj, tr, ql, cq_cb), _rt(proj, tr, kvl, ckv_cb), _whole(g_q), _whole(g_kv)],
                       [_rt_out(t, ql, BF16, tr), _rt_out(t, kvl, BF16, tr)])
    qfull, = _matmul("q_up", Mat(qn, t, ql), Mat(w_q_all, ql, 2 * hw), "nn", [_out(t, 2 * hw, F32)], tm, _pick(2 * hw, 1024), ql)
    kvall, = _matmul("kv_up", Mat(kvn, t, kvl), Mat(w_kv_all, kvl, 2 * hw), "nn", [_out(t, 2 * hw, BF16)], tm, _pick(2 * hw, 1024), kvl)
    qall, kr = _rope_fwd(qfull, proj, kr_cb, ctab, stab, heads, tr)
    att, lse_row = _attn_fwd(qall, kvall, kr, heads, scale, tr)
    rb = min(2 * LANES, t)
    sgu = _sgu_fwd(proj, g_vn, sgu_w, sgu_b, groups, rb)
    mixed = _rowwise("mix_norm", lambda a, s, ga, gs: jnp.concatenate([_rms(a, ga), _rms(s, gs)], axis=1), t // tr,
                     [_rt(att, tr), _rt(sgu, tr), _whole(g_mla), _whole(g_sgu)], [_rt_out(t, mix, BF16, tr)])[0]
    x1, = _matmul("e_out", Mat(mixed, t, mix), Mat(w_eout, mix, d), "nn", [_out(t, d, F32)], tm, _pick(d, 1024), _pick(mix, 2048),
                  epilogue=lambda z, r: (z + r,), extras=[Mat(xs, t, d)])
    x2, hm0, a0, act0, w1_0, w2_0 = mlp_fwd("0", x1, g_m0, 2)

    w_oin_g, w_oout_g = gathered(4, "o", x2)
    w_oout = w_oout_g.reshape(cd, d)
    h1 = _norm_fwd("o_norm", x2, g_o, tr)
    oin = Mat(_unstack_cols(w_oin_g), d, 3 * cd)
    tn_o = _pick(_gcd(3 * cd // N_CHIPS, cd), 512)
    proj3, = _matmul("o_proj", Mat(h1, t, d), oin, "nn", [_out(t, 3 * cd, F32, "colstack", (), (3, t, cd))],
                     tm, _pick(cd, 1024), kd)
    tc = _pick(cd, 256)
    bz = _conv_fwd(proj3, conv_w, tc)
    x3, = _matmul("o_out", Mat(bz, t, cd), Mat(w_oout, cd, d), "nn", [_out(t, d, F32)], tm, _pick(d, 1024), _pick(cd, 2048),
                  epilogue=lambda z, r: (z + r,), extras=[Mat(x2, t, d)])
    x4, hm1, a1, act1, w1_1, w2_1 = mlp_fwd("1", x3, g_m1, 5)

    def final_fn(xv, gv, tv):
        r = lax.rsqrt(jnp.mean(xv * xv, axis=-1, keepdims=True) + EPS)
        xh = xv * r
        err = xh * gv - tv
        dy = err * (1.0 / d)
        dxh = dy * gv
        dx = r * (dxh - xh * jnp.mean(dxh * xh, axis=-1, keepdims=True))
        sq = jnp.sum(err * err, axis=0, keepdims=True)
        part = sq[:, :LANES]
        for k in range(1, d // LANES):
            part = part + sq[:, k * LANES:(k + 1) * LANES]
        return dx, dx, part, jnp.sum(dy * xh, axis=0, keepdims=True)

    dx4, dx4b, loss_vec, dg_f = _rowwise("loss_final_norm", final_fn, t // tr, [_rt(x4, tr), _whole(g_f), _rt(tgt, tr)],
                                         [_rt_out(t, d, F32, tr), _rt_out(t, d, BF16, tr)],
                                         [jax.ShapeDtypeStruct((1, LANES), F32), jax.ShapeDtypeStruct((1, d), F32)])
    loss = lax.psum(0.5 * jnp.sum(loss_vec) / d, ("x", "y", "c"))

    dx3, dx3b, dg_m1, sc_m1, tok = mlp_bwd("1", dx4, dx4b, x3, g_m1, w1_1, w2_1, hm1, a1, act1, ())

    dbz, = _matmul("o_out_dx", Mat(dx3b, t, d), Mat(w_oout, cd, d), "nt", [_out(t, cd, F32)], tm, _pick(cd, 1024), kd,
                   deps=(tok,))
    dw_oout, = _matmul("o_out_dw", Mat(bz, t, cd), Mat(dx3b, t, d), "tn", [_out(cd, d, BF16)], _pick(cd, 1024), _pick(d, 1024), kt)
    dproj3, dconv = _conv_bwd(proj3, conv_w, dbz, tc)
    dp3 = Mat(dproj3, t, 3 * cd, "colstack")
    dw_oin, = _matmul("o_proj_dw", Mat(h1, t, d), dp3, "tn", [_out(d, 3 * cd, BF16, "colstack", (), (N_CHIPS, d, 3 * cd // N_CHIPS))],
                      _pick(d, 1024), tn_o, kt)
    started_o, tok = pair_start("o", [dw_oin, dw_oout.reshape(N_CHIPS, cd // N_CHIPS, d)])
    dh1, = _matmul("o_proj_dx", dp3, oin, "nt", [_out(t, d, F32)], tm, _pick(d, 1024), _pick(cd, 2048), deps=(tok,))
    dx2, dx2b, dg_o = _norm_bwd("o_norm_bwd", dh1, x2, g_o, dx3, tr)
    sc_o, tok = pair_finish("o", started_o, dx2)

    dconv_s = jnp.transpose(dconv[:3].reshape(3, N_CHIPS, cd // N_CHIPS), (1, 0, 2))
    gsmall = jnp.concatenate([jnp.pad(dg_o.reshape(N_CHIPS, 1, d // N_CHIPS), ((0, 0), (0, 15), (0, 0))),
                              jnp.pad(dconv_s, ((0, 0), (0, 13), (0, 0)))], axis=1)
    dx1, dx1b, dg_m0, sc_m0, tok = mlp_bwd("0", dx2, dx2b, x1, g_m0, w1_0, w2_0, hm0, a0, act0, (tok,), (gsmall,))

    dmixed, = _matmul("e_out_dx", Mat(dx1b, t, d), Mat(w_eout, mix, d), "nt", [_out(t, mix, F32)], tm, _pick(mix, 1024), kd,
                      deps=(tok,))
    dw_eout, = _matmul("e_out_dw", Mat(mixed, t, mix), Mat(dx1b, t, d), "tn", [_out(mix, d, BF16)], _pick(mix, 1024), _pick(d, 1024), kt)

    def mixb_fn(dm, a, s, ga, gs):
        da, dga = _rms_bwd(dm[:, :hw], a, ga)
        dsg, dgs = _rms_bwd(dm[:, hw:], s, gs)
        prod = da * a
        cols = [jnp.broadcast_to(jnp.sum(prod[:, h * LANES:(h + 1) * LANES], axis=-1, keepdims=True), (tr, LANES))
                for h in range(heads)]
        return da, dsg, jnp.stack([_row_of(c) for c in cols], axis=0), dga, dgs

    da_b, dsgu, delta_row, dg_mla, dg_sgu = _rowwise(
        "mix_norm_bwd", mixb_fn, t // tr, [_rt(dmixed, tr), _rt(att, tr), _rt(sgu, tr), _whole(g_mla), _whole(g_sgu)],
        [_rt_out(t, hw, BF16, tr), _rt_out(t, gw, F32, tr),
         (jax.ShapeDtypeStruct((heads, 8, t), F32), pl.BlockSpec((heads, 8, tr), lambda i: (0, 0, i)))],
        [jax.ShapeDtypeStruct((1, hw), F32), jax.ShapeDtypeStruct((1, gw), F32)])

    dproj, dsgu_w, dsgu_b8, dg_vn = _sgu_bwd(proj, dsgu, g_vn, sgu_w, sgu_b, groups, rb)
    dq1, dq2, dk1, dvv, dkr_h = _attn_bwd(qall, kvall, kr, da_b, lse_row, delta_row, heads, scale, tr)
    dqfull, dproj = _rope_bwd(dq1, dq2, dkr_h, ctab, stab, heads, tr, dproj, kr_cb)
    dkvall = jnp.concatenate([dk1, dvv], axis=1)
    dw_q, = _matmul("q_up_dw", Mat(qn, t, ql), Mat(dqfull, t, 2 * hw), "tn", [_out(ql, 2 * hw, BF16)], ql, _pick(2 * hw, 1024), kt)
    dqn, = _matmul("q_up_dx", Mat(dqfull, t, 2 * hw), Mat(w_q_all, ql, 2 * hw), "nt", [_out(t, ql, F32)], tm, ql, _pick(2 * hw, 2048))
    dw_kv, = _matmul("kv_up_dw", Mat(kvn, t, kvl), Mat(dkvall, t, 2 * hw), "tn", [_out(kvl, 2 * hw, BF16)], kvl, _pick(2 * hw, 1024), kt)
    dkvn, = _matmul("kv_up_dx", Mat(dkvall, t, 2 * hw), Mat(w_kv_all, kvl, 2 * hw), "nt", [_out(t, kvl, F32)], tm, kvl, _pick(2 * hw, 2048))

    def qkvb_fn(da, db, a, b, ga, gb):
        dxa, dga = _rms_bwd(da, a, ga)
        dxb, dgb = _rms_bwd(db, b, gb)
        return jnp.concatenate([dxa, dxb], axis=1), dga, dgb

    assert (2 * gw) % (ql + kvl) == 0
    into = (jax.ShapeDtypeStruct(dproj.shape, dproj.dtype),
            pl.BlockSpec((tr, ql + kvl), lambda i: (i, 2 * gw // (ql + kvl))))
    dproj, dg_q, dg_kv = _rowwise(
        "qkv_norm_bwd", qkvb_fn, t // tr,
        [_rt(dqn, tr), _rt(dkvn, tr), _rt(proj, tr, ql, cq_cb), _rt(proj, tr, kvl, ckv_cb), _whole(g_q), _whole(g_kv)],
        [into], [jax.ShapeDtypeStruct((1, ql), F32), jax.ShapeDtypeStruct((1, kvl), F32)], deps=(dproj,), fill=(0, 0))
    dw_in, = _matmul("e_proj_dw", Mat(h0, t, d), Mat(dproj, t, pi), "tn", [_out(d, pi, BF16)], _pick(d, 1024), _pick(pi, 1024), kt)
    dh0, = _matmul("e_proj_dx", Mat(dproj, t, pi), Mat(w_in_all, d, pi), "nt", [_out(t, d, F32)], tm, _pick(d, 1024), _pick(pi, 4096))
    dx0, _, dg_e = _norm_bwd("e_norm_bwd", dh0, xs, g_e, dx1, tr)

    gfull = jnp.concatenate([dw_in[:, 2 * gw:2 * gw + c2], _unpad_rope(dw_in[:, 2 * gw + c2:]), dw_in[:, :2 * gw]], axis=1)
    gw_in = _stack_cols(gfull)
    gq = jnp.concatenate([dw_q[:, :hw].reshape(ql, heads, LANES), _unpad_rope(dw_q[:, hw:].reshape(ql, heads, LANES))], axis=-1)
    gw_uq = _stack_cols(gq.reshape(ql, heads * (LANES + ROPE)))
    gkv = jnp.concatenate([dw_kv[:, :hw].reshape(kvl, heads, LANES), dw_kv[:, hw:].reshape(kvl, heads, LANES)], axis=-1)
    gw_ukv = _stack_cols(gkv.reshape(kvl, heads * 2 * LANES))
    started_e, tok_pair = pair_start("e", [gw_in, gw_uq, gw_ukv, dw_eout.reshape(N_CHIPS, mix // N_CHIPS, d)])

    small_like = [e_norm_mix, e_q_norm, e_kv_norm, e_v_norm, e_sgu_w, e_sgu_b, e_mla_out_norm, e_sgu_out_norm, mlp_norm, final_norm]
    small_grads = [dg_e, dg_q, dg_kv, dg_vn, dsgu_w, dsgu_b8[:, 0, :], dg_mla, dg_sgu, jnp.concatenate([dg_m0, dg_m1], axis=0), dg_f]
    sflat = _pack_small(small_grads)
    pad = (-sflat.shape[0]) % 8
    sflat = jnp.pad(sflat, ((0, pad), (0, 0)))
    small_started, tok_small = _exchange_start("small_start", _all_route, 7, [sflat], [_spread(sflat)])

    def summed(tag, sc, after):
        part, lands = _exchange_wait("scatter_wait_" + tag, _chip_route, sc, after)
        half = [_chip_sum(p, r) for p, r in zip(part, lands)]
        return _exchange_start("share_start_" + tag, _share_route, len(half), half, [])

    def shared(tag, started, after):
        bufs, _ = _exchange_wait("share_wait_" + tag, _share_route, started, after)
        return [r.reshape(2 * r.shape[1], r.shape[2]) for r in bufs]

    sh_m1, tok = summed("m1", sc_m1, (tok_pair, tok_small))
    sc_e, tok = pair_finish("e", started_e, tok)
    sh_o, tok = summed("o", sc_o, tok)
    sh_m0, tok = summed("m0", sc_m0, tok)
    r_oin, r_oout = shared("o", sh_o, tok)
    late = {"o_w_in": _adamw(o_w_in, [r_oin], m_o_w_in, v_o_w_in),
            "o_w_out": _adamw(o_w_out, [r_oout], m_o_w_out, v_o_w_out)}
    r_w1_1, r_w2_1 = shared("m1", sh_m1, late["o_w_in"][1])
    r_w1_0, r_w2_0, r_small = shared("m0", sh_m0, r_w2_1)
    late["mlp_w1"] = _adamw(mlp_w1, [r_w1_0, r_w1_1], m_mlp_w1, v_mlp_w1)
    late["mlp_w2"] = _adamw(mlp_w2, [r_w2_0, r_w2_1], m_mlp_w2, v_mlp_w2)

    _, (all_small,) = _exchange_wait("small_wait", _all_route, small_started, late["mlp_w2"][1])
    g_small = _sum_devices(all_small)

    def padded(arrs):
        return jnp.pad(_pack_small(arrs), ((0, pad), (0, 0)))

    s_m = [m_e_norm_mix, m_e_q_norm, m_e_kv_norm, m_e_v_norm, m_e_sgu_w, m_e_sgu_b, m_e_mla_out_norm, m_e_sgu_out_norm, m_mlp_norm, m_final_norm]
    s_v = [v_e_norm_mix, v_e_q_norm, v_e_kv_norm, v_e_v_norm, v_e_sgu_w, v_e_sgu_b, v_e_mla_out_norm, v_e_sgu_out_norm, v_mlp_norm, v_final_norm]
    s_out = [_unpack_small(o[0], small_like)
             for o in _adamw(padded(small_like)[None], [g_small], padded(s_m)[None], padded(s_v)[None])]

    sm = [o[0] for o in _adamw(small_shard[None], [r_small], _small_shard(m_o_norm_mix, m_o_conv_w[0])[None],
                               _small_shard(v_o_norm_mix, v_o_conv_w[0])[None])]

    sh_e, tok = summed("e", sc_e, late["mlp_w2"][1])
    r_in, r_uq, r_ukv, r_eout = shared("e", sh_e, tok)
    big = dict(late)
    flip = lambda a: jnp.swapaxes(a, 1, 2)
    big.update({
        "e_w_in": [flip(o) for o in _adamw(flip(e_w_in), [r_in.T], flip(m_e_w_in), flip(v_e_w_in))],
        "e_w_uq": _adamw(e_w_uq, [r_uq], m_e_w_uq, v_e_w_uq),
        "e_w_ukv": _adamw(e_w_ukv, [r_ukv], m_e_w_ukv, v_e_w_ukv),
        "e_w_out": _adamw(e_w_out, [r_eout], m_e_w_out, v_e_w_out),
    })

    names = ["e_norm_mix", "e_w_in", "e_q_norm", "e_w_uq", "e_kv_norm", "e_w_ukv", "e_v_norm", "e_sgu_w", "e_sgu_b",
             "e_mla_out_norm", "e_sgu_out_norm", "e_w_out", "o_norm_mix", "o_w_in", "o_conv_w", "o_w_out",
             "mlp_norm", "mlp_w1", "mlp_w2", "final_norm"]
    shapes = {"e_w_in": e_w_in.shape, "e_w_uq": e_w_uq.shape, "e_w_ukv": e_w_ukv.shape, "e_w_out": e_w_out.shape,
              "o_w_in": o_w_in.shape, "o_w_out": o_w_out.shape, "mlp_w1": mlp_w1.shape, "mlp_w2": mlp_w2.shape}
    small_names = ["e_norm_mix", "e_q_norm", "e_kv_norm", "e_v_norm", "e_sgu_w", "e_sgu_b", "e_mla_out_norm",
                   "e_sgu_out_norm", "mlp_norm", "final_norm"]

    def leaf(kind, name):
        if name in big:
            return big[name][kind].reshape(shapes[name])
        if name == "o_norm_mix":
            return sm[kind][0:1]
        if name == "o_conv_w":
            return sm[kind][16:19].reshape(o_conv_w.shape)
        return s_out[kind][small_names.index(name)]

    outs = [loss, dx0.reshape(x.shape)]
    for kind in range(4):
        outs += [leaf(kind, nm) for nm in names]
    return tuple(outs)


def _gcd(a, b):
    while b:
        a, b = b, a % b
    return a
```

```python
import functools

import jax
import jax.numpy as jnp
from jax import lax
from jax.experimental import pallas as pl
from jax.experimental.pallas import tpu as pltpu

F32 = jnp.float32
BF16 = jnp.bfloat16
MESH = pl.DeviceIdType.MESH

LANES = 128
ROPE = 64
ROPE_HALF = ROPE // 2
ROPE_BASE = 10000.0
EPS = 1e-6
N_CHIPS = 4
VMEM_LIMIT = 48 * 1024 * 1024
NEG = -1e30

ADAM_LR = 0.001
ADAM_B1 = 0.9
ADAM_B2 = 0.999
ADAM_EPS = 1e-08
ADAM_WD = 0.01
ADAM_STEP = 10


def _pick(n, target, step=LANES):
    best = None
    for t in range(step, min(n, target) + 1, step):
        if n % t == 0:
            best = t
    return best if best is not None else n


def _params(sem, vmem=VMEM_LIMIT):
    return pltpu.CompilerParams(dimension_semantics=sem, vmem_limit_bytes=vmem)


class Mat:
    def __init__(self, arr, rows, cols, kind="plain", lead=(), col_off=0, shape=None, dtype=None):
        self.arr, self.rows, self.cols, self.kind, self.lead, self.col_off = arr, rows, cols, kind, tuple(lead), col_off
        self.shape = tuple(arr.shape) if arr is not None else tuple(shape)
        self.dtype = arr.dtype if arr is not None else dtype

    def sds(self):
        return jax.ShapeDtypeStruct(self.shape, self.dtype)

    def spec(self, br, bc, gridmap):
        lead, nl = self.lead, len(self.lead)
        if self.kind == "plain":
            assert self.col_off % bc == 0 and self.rows % br == 0 and self.cols % bc == 0, (self.shape, br, bc)
            off = self.col_off // bc
            block = (None,) * nl + (br, bc)

            def phys(rb, cb):
                return lead + (rb, cb + off)
        elif self.kind == "colstack":
            cs = self.shape[-1]
            assert cs % bc == 0 and self.rows % br == 0, (self.shape, br, bc)
            q = cs // bc
            block = (None,) * (nl + 1) + (br, bc)

            def phys(rb, cb):
                return (cb // q,) + lead + (rb, cb % q)
        else:
            rs = self.shape[-2]
            assert rs % br == 0 and self.cols % bc == 0, (self.shape, br, bc)
            q = rs // br
            block = (None,) * (nl + 1) + (br, bc)

            def phys(rb, cb):
                return (rb // q,) + lead + (rb % q, cb)

        return pl.BlockSpec(block, lambda *g: phys(*gridmap(*g)))


def _adamw_math(w, g, m, v):
    mn = ADAM_B1 * m + (1.0 - ADAM_B1) * g
    vn = ADAM_B2 * v + (1.0 - ADAM_B2) * jnp.square(g)
    m_hat = mn / (1.0 - ADAM_B1 ** ADAM_STEP)
    v_hat = vn / (1.0 - ADAM_B2 ** ADAM_STEP)
    return -ADAM_LR * (m_hat / (jnp.sqrt(v_hat) + ADAM_EPS) + ADAM_WD * w), mn, vn


def _matmul(name, a, b, mode, outs, tm, tn, tk, epilogue=None, extras=(), deps=(), side=None):
    if mode == "nn":
        m, k, n = a.rows, a.cols, b.cols
        a_spec = a.spec(tm, tk, lambda i, j, kk: (i, kk))
        b_spec = b.spec(tk, tn, lambda i, j, kk: (kk, j))
        dims = (((1,), (0,)), ((), ()))
    elif mode == "nt":
        m, k, n = a.rows, a.cols, b.rows
        a_spec = a.spec(tm, tk, lambda i, j, kk: (i, kk))
        b_spec = b.spec(tn, tk, lambda i, j, kk: (j, kk))
        dims = (((1,), (1,)), ((), ()))
    else:
        k, m, n = a.rows, a.cols, b.cols
        a_spec = a.spec(tk, tm, lambda i, j, kk: (kk, i))
        b_spec = b.spec(tk, tn, lambda i, j, kk: (kk, j))
        dims = (((0,), (0,)), ((), ()))
    assert m % tm == 0 and n % tn == 0 and k % tk == 0, (name, m, n, k, tm, tn, tk)
    grid = (m // tm, n // tn, k // tk)
    nk = grid[2]
    n_ex, n_out, n_dep = len(extras), len(outs), len(deps)
    tile = lambda i, j, kk: (i, j)

    def finish(z, ex, out_refs):
        vals = epilogue(z, *[e[...] for e in ex]) if epilogue is not None else (z,)
        for o, v in zip(out_refs, vals):
            o[...] = v.astype(o.dtype)

    n_side = 0 if side is None else 4
    side_in, side_out, side_shapes, side_args = [], [], [], []
    if side is not None:
        sw, sg, sm, sv, layer = side
        n_steps = grid[0] * grid[1]
        _, srows, scols = sw.shape
        assert nk == 1 and srows % n_steps == 0 and (srows // n_steps) % 8 == 0, (name, sw.shape, grid)
        st = srows // n_steps
        step = lambda i, j, kk: i * grid[1] + j
        spec3 = pl.BlockSpec((None, st, scols), lambda i, j, kk: (layer, step(i, j, kk), 0))
        side_in = [spec3, spec3, spec3, pl.BlockSpec((st, scols), lambda i, j, kk: (step(i, j, kk), 0))]
        side_out = [spec3] * 4
        side_shapes = [jax.ShapeDtypeStruct(sw.shape, F32)] * 4
        side_args = [sw, sm, sv, sg]

    def body_single(a_ref, b_ref, *rest):
        finish(lax.dot_general(a_ref[...], b_ref[...], dims, preferred_element_type=F32),
               rest[:n_ex], rest[n_ex + n_dep + n_side:n_ex + n_dep + n_side + n_out])
        if side is not None:
            w_ref, m_ref, v_ref, g_ref = rest[n_ex + n_dep:n_ex + n_dep + 4]
            go_ref, d_ref, mo_ref, vo_ref = rest[n_ex + n_dep + 4 + n_out:]
            gv = g_ref[...]
            d_ref[...], mo_ref[...], vo_ref[...] = _adamw_math(w_ref[...], gv, m_ref[...], v_ref[...])
            go_ref[...] = gv

    def body_acc(a_ref, b_ref, *rest):
        acc = rest[-1]
        kk = pl.program_id(2)

        @pl.when(kk == 0)
        def _():
            acc[...] = jnp.zeros_like(acc)

        acc[...] += lax.dot_general(a_ref[...], b_ref[...], dims, preferred_element_type=F32)

        @pl.when(kk == nk - 1)
        def _():
            finish(acc[...], rest[:n_ex], rest[n_ex + n_dep:n_ex + n_dep + n_out])

    res = pl.pallas_call(
        body_single if nk == 1 else body_acc, name=name, grid=grid,
        in_specs=[a_spec, b_spec] + [e.spec(tm, tn, tile) for e in extras]
        + [pl.BlockSpec(memory_space=pl.ANY) for _ in deps] + side_in,
        out_specs=[o.spec(tm, tn, tile) for o in outs] + side_out,
        out_shape=[o.sds() for o in outs] + side_shapes,
        scratch_shapes=[] if nk == 1 else [pltpu.VMEM((tm, tn), F32)],
        compiler_params=_params(("parallel", "parallel", "arbitrary")),
    )(a.arr, b.arr, *[e.arr for e in extras], *deps, *side_args)
    return res


def _out(rows, cols, dtype, kind="plain", lead=(), shape=None):
    return Mat(None, rows, cols, kind, lead, shape=shape if shape is not None else (rows, cols), dtype=dtype)


def _rt(arr, tr, width=None, cb=0):
    width = arr.shape[1] if width is None else width
    return arr, pl.BlockSpec((tr, width), lambda i: (i, cb))


def _whole(arr):
    nd = arr.ndim
    return arr, pl.BlockSpec(arr.shape, lambda i: (0,) * nd)


def _rowwise(name, fn, n_steps, ins, outs, accs=(), deps=(), fill=None):
    n_in, n_out, n_acc, n_dep = len(ins), len(outs), len(accs), len(deps)

    def body(*refs):
        vals = fn(*[r[...] for r in refs[:n_in]])
        if not isinstance(vals, (tuple, list)):
            vals = (vals,)
        for ref, v in zip(refs[n_in + n_dep:n_in + n_dep + n_out], vals[:n_out]):
            ref[...] = v.astype(ref.dtype)
        if n_acc:
            acc_refs = refs[n_in + n_dep + n_out:]

            @pl.when(pl.program_id(0) == 0)
            def _():
                for ref in acc_refs:
                    ref[...] = jnp.zeros_like(ref)

            for ref, v in zip(acc_refs, vals[n_out:]):
                ref[...] += v

    acc_specs = [pl.BlockSpec(s.shape, lambda i, nd=len(s.shape): (0,) * nd) for s in accs]
    res = pl.pallas_call(
        body, name=name, grid=(n_steps,),
        in_specs=[s for _, s in ins] + [pl.BlockSpec(memory_space=pl.ANY) for _ in deps],
        out_specs=[s for _, s in outs] + acc_specs,
        out_shape=[o for o, _ in outs] + list(accs),
        input_output_aliases={} if fill is None else {n_in + fill[0]: fill[1]},
        compiler_params=_params(("arbitrary",) if n_acc else ("parallel",)),
    )(*[a for a, _ in ins], *deps)
    return res


def _rt_out(t, width, dtype, tr):
    return jax.ShapeDtypeStruct((t, width), dtype), pl.BlockSpec((tr, width), lambda i: (i, 0))


def _rms(x, g):
    r = lax.rsqrt(jnp.mean(x * x, axis=-1, keepdims=True) + EPS)
    return x * r * g


def _rms_bwd(dy, x, g):
    r = lax.rsqrt(jnp.mean(x * x, axis=-1, keepdims=True) + EPS)
    xh = x * r
    dxh = dy * g
    dx = r * (dxh - xh * jnp.mean(dxh * xh, axis=-1, keepdims=True))
    dg = jnp.sum(dy * xh, axis=0, keepdims=True)
    return dx, dg


def _gelu(x):
    k = 0.7978845608028654
    th = jnp.tanh(k * (x + 0.044715 * (x * x * x)))
    return x * (0.5 * (1.0 + th))


def _gelu_grad(x):
    k = 0.7978845608028654
    x2 = x * x
    th = jnp.tanh(k * (x + 0.044715 * (x2 * x)))
    return 0.5 * (1.0 + th) + 0.5 * x * (1.0 - th * th) * (k * (1.0 + 3.0 * 0.044715 * x2))


def _norm_fwd(name, x, g, tr):
    t, d = x.shape
    return _rowwise(name, lambda xv, gv: _rms(xv, gv), t // tr, [_rt(x, tr), _whole(g)], [_rt_out(t, d, BF16, tr)])[0]


def _norm_bwd(name, dh, x, g, dres, tr):
    t, d = x.shape

    def fn(dhv, xv, gv, drv):
        dx, dg = _rms_bwd(dhv, xv, gv)
        dx = dx + drv
        return dx, dx, dg

    return _rowwise(name, fn, t // tr, [_rt(dh, tr), _rt(x, tr), _whole(g), _rt(dres, tr)],
                    [_rt_out(t, d, F32, tr), _rt_out(t, d, BF16, tr)], [jax.ShapeDtypeStruct((1, d), F32)])


def _rope_tables(posf, invf, cmask, smask, tr):
    t = posf.shape[0]

    def fn(p, f, cm, sm):
        ang = p * f
        return jnp.cos(ang) * cm, jnp.sin(ang) * sm

    return _rowwise("rope_tables", fn, t // tr, [_rt(posf, tr), _whole(invf), _whole(cmask), _whole(smask)],
                    [_rt_out(t, LANES, F32, tr), _rt_out(t, LANES, F32, tr)])


def _rot(v, c, s):
    return v * c + pltpu.roll(v, ROPE, axis=1) * s


def _rot_bwd(dv, c, s):
    return dv * c + pltpu.roll(dv * s, ROPE, axis=1)


def _rope_fwd(qfull, proj, kr_cb, ctab, stab, heads, tr):
    t = qfull.shape[0]
    hw = heads * LANES

    def fn(q, kr, c, s):
        parts = [q[:, :hw]] + [_rot(q[:, hw + h * LANES: hw + (h + 1) * LANES], c, s) for h in range(heads)]
        return jnp.concatenate(parts, axis=1), _rot(kr, c, s)

    return _rowwise("rope_fwd", fn, t // tr, [_rt(qfull, tr), _rt(proj, tr, LANES, kr_cb), _rt(ctab, tr), _rt(stab, tr)],
                    [_rt_out(t, 2 * hw, BF16, tr), _rt_out(t, LANES, BF16, tr)])


def _rope_bwd(dq1, dq2, dkr_h, ctab, stab, heads, tr, dproj, kr_cb):
    t = dq1.shape[0]
    hw = heads * LANES

    def fn(a, b, dk, c, s):
        parts = [a] + [_rot_bwd(b[:, h * LANES:(h + 1) * LANES], c, s) for h in range(heads)]
        dks = dk[0]
        for h in range(1, heads):
            dks = dks + dk[h]
        return jnp.concatenate(parts, axis=1), _rot_bwd(dks, c, s)

    dk_spec = pl.BlockSpec((heads, tr, LANES), lambda i: (0, i, 0))
    into = (jax.ShapeDtypeStruct(dproj.shape, dproj.dtype), pl.BlockSpec((tr, LANES), lambda i: (i, kr_cb)))
    return _rowwise("rope_bwd", fn, t // tr, [_rt(dq1, tr), _rt(dq2, tr), (dkr_h, dk_spec), _rt(ctab, tr), _rt(stab, tr)],
                    [_rt_out(t, 2 * hw, BF16, tr), into], deps=(dproj,), fill=(0, 1))


def _dot_nt(a, b):
    return lax.dot_general(a, b, (((1,), (1,)), ((), ())), preferred_element_type=F32)


def _dot_tn(a, b):
    return lax.dot_general(a, b, (((0,), (0,)), ((), ())), preferred_element_type=F32)


def _dot(a, b):
    return jnp.dot(a, b, preferred_element_type=F32)


def _ranges(n_blocks):
    n_var = min(4, n_blocks)
    assert n_blocks % n_var == 0
    return n_var, n_blocks // n_var


def _row_of(col):
    return col.T[:8, :]


def _attn_fwd(qall, kvall, kr, heads, scale, tq):
    t = qall.shape[0]
    nq = t // tq
    n_var, per = _ranges(nq)

    def body(qn_ref, qr_ref, kn_ref, v_ref, kr_ref, o_ref, lser_ref):
        i = pl.program_id(1)
        for var in range(n_var):
            kv = (var + 1) * per * tq

            @pl.when(jnp.logical_and(i >= var * per, i < (var + 1) * per))
            def _(kv=kv):
                s = (_dot_nt(qn_ref[...], kn_ref[:kv, :]) + _dot_nt(qr_ref[...], kr_ref[:kv, :])) * scale
                rows = i * tq + lax.broadcasted_iota(jnp.int32, (tq, kv), 0)
                cols = lax.broadcasted_iota(jnp.int32, (tq, kv), 1)
                s = jnp.where(cols <= rows, s, NEG)
                m = jnp.max(s, axis=-1, keepdims=True)
                p = jnp.exp(s - m)
                l = jnp.sum(p, axis=-1, keepdims=True)
                o_ref[...] = _dot(p.astype(BF16), v_ref[:kv, :]) / l
                lser_ref[...] = _row_of(jnp.broadcast_to(m + jnp.log(l), (tq, LANES)))

    return pl.pallas_call(
        body, name="attn_fwd", grid=(heads, nq),
        in_specs=[pl.BlockSpec((tq, LANES), lambda h, i: (i, h)),
                  pl.BlockSpec((tq, LANES), lambda h, i: (i, heads + h)),
                  pl.BlockSpec((t, LANES), lambda h, i: (0, h)),
                  pl.BlockSpec((t, LANES), lambda h, i: (0, heads + h)),
                  pl.BlockSpec((t, LANES), lambda h, i: (0, 0))],
        out_specs=[pl.BlockSpec((tq, LANES), lambda h, i: (i, h)),
                   pl.BlockSpec((None, 8, tq), lambda h, i: (h, 0, i))],
        out_shape=[jax.ShapeDtypeStruct((t, heads * LANES), F32), jax.ShapeDtypeStruct((heads, 8, t), F32)],
        compiler_params=_params(("parallel", "parallel")),
    )(qall, qall, kvall, kvall, kr)


def _attn_bwd(qall, kvall, kr, do, lse_row, delta_row, heads, scale, tk):
    t = qall.shape[0]
    nk = t // tk
    n_var, per = _ranges(nk)

    def body(qn_ref, qr_ref, kn_ref, v_ref, kr_ref, do_ref, lse_ref, dl_ref, dq1_ref, dq2_ref, dk_ref, dv_ref, dkr_ref):
        j = pl.program_id(1)

        @pl.when(j == 0)
        def _():
            dq1_ref[...] = jnp.zeros_like(dq1_ref)
            dq2_ref[...] = jnp.zeros_like(dq2_ref)

        for var in range(n_var):
            q0 = var * per * tk
            nq = t - q0

            @pl.when(jnp.logical_and(j >= var * per, j < (var + 1) * per))
            def _(q0=q0, nq=nq):
                qn, qr, do_v = qn_ref[q0:, :], qr_ref[q0:, :], do_ref[q0:, :]
                k1, k2 = kn_ref[...], kr_ref[...]
                st = (_dot_nt(k1, qn) + _dot_nt(k2, qr)) * scale
                keys = j * tk + lax.broadcasted_iota(jnp.int32, (tk, nq), 0)
                queries = q0 + lax.broadcasted_iota(jnp.int32, (tk, nq), 1)
                pt = jnp.where(keys <= queries, jnp.exp(st - lse_ref[0:1, q0:]), 0.0)
                dpt = _dot_nt(v_ref[...], do_v)
                dst = (pt * (dpt - dl_ref[0:1, q0:]) * scale).astype(BF16)
                dv_ref[...] = _dot(pt.astype(BF16), do_v).astype(dv_ref.dtype)
                dk_ref[...] = _dot(dst, qn).astype(dk_ref.dtype)
                dkr_ref[...] = _dot(dst, qr)
                dq1_ref[q0:, :] += _dot_tn(dst, k1)
                dq2_ref[q0:, :] += _dot_tn(dst, k2)

    kblk = lambda off: pl.BlockSpec((tk, LANES), lambda h, j: (j, off + h))
    full = lambda off: pl.BlockSpec((t, LANES), lambda h, j: (0, off + h))
    stat = pl.BlockSpec((None, 8, t), lambda h, j: (h, 0, 0))
    return pl.pallas_call(
        body, name="attn_bwd", grid=(heads, nk),
        in_specs=[full(0), full(heads), kblk(0), kblk(heads), pl.BlockSpec((tk, LANES), lambda h, j: (j, 0)),
                  full(0), stat, stat],
        out_specs=[full(0), full(0), kblk(0), kblk(0), pl.BlockSpec((None, tk, LANES), lambda h, j: (h, j, 0))],
        out_shape=[jax.ShapeDtypeStruct((t, heads * LANES), F32)] * 2 + [jax.ShapeDtypeStruct((t, heads * LANES), BF16)] * 2
        + [jax.ShapeDtypeStruct((heads, t, LANES), F32)],
        compiler_params=_params(("parallel", "arbitrary")),
    )(qall, qall, kvall, kvall, kr, do, lse_row, delta_row)


def _tril():
    return lax.broadcasted_iota(jnp.int32, (LANES, LANES), 0) >= lax.broadcasted_iota(jnp.int32, (LANES, LANES), 1)


def _group_norm(vg):
    mu = jnp.mean(vg, axis=-1, keepdims=True)
    vc = vg - mu
    rs = lax.rsqrt(jnp.mean(vc * vc, axis=-1, keepdims=True) + EPS)
    return vc * rs, rs


def _sgu_fwd(proj, gain, w, bias, groups, rb):
    t = proj.shape[0]
    gw = groups * LANES
    cpb = rb // LANES

    def body(u_ref, v_ref, gain_ref, w_ref, b_ref, s_ref):
        tril = _tril()
        for g in range(groups):
            wt = jnp.where(tril, w_ref[g], 0.0).astype(BF16)
            cols = slice(g * LANES, (g + 1) * LANES)
            for ci in range(cpb):
                rows = slice(ci * LANES, (ci + 1) * LANES)
                ug = _gelu(u_ref[rows, cols])
                vh, _ = _group_norm(_gelu(v_ref[rows, cols]))
                vn = vh * gain_ref[:, cols]
                y = _dot(wt, vn.astype(BF16)) + b_ref[g]
                s_ref[rows, cols] = ug * y

    return pl.pallas_call(
        body, name="sgu_fwd", grid=(t // rb,),
        in_specs=[pl.BlockSpec((rb, gw), lambda i: (i, 0)), pl.BlockSpec((rb, gw), lambda i: (i, 1)),
                  pl.BlockSpec((1, gw), lambda i: (0, 0)),
                  pl.BlockSpec((groups, LANES, LANES), lambda i: (0, 0, 0)),
                  pl.BlockSpec((groups, LANES, LANES), lambda i: (0, 0, 0))],
        out_specs=pl.BlockSpec((rb, gw), lambda i: (i, 0)),
        out_shape=jax.ShapeDtypeStruct((t, gw), F32),
        compiler_params=_params(("parallel",)),
    )(proj, proj, gain, w, bias)


def _sgu_bwd(proj, ds, gain, w, bias, groups, rb):
    t, width = proj.shape
    gw = groups * LANES
    cpb = rb // LANES
    n_steps = t // rb

    def body(u_ref, v_ref, ds_ref, gain_ref, w_ref, b_ref, dp_ref, dw_ref, db_ref, dg_ref, dy_acc):
        du_ref, dv_ref = dp_ref.at[:, :gw], dp_ref.at[:, gw:]
        step = pl.program_id(0)

        @pl.when(step == 0)
        def _():
            dw_ref[...] = jnp.zeros_like(dw_ref)
            dy_acc[...] = jnp.zeros_like(dy_acc)
            dg_ref[...] = jnp.zeros_like(dg_ref)

        tril = _tril()
        for g in range(groups):
            wt = jnp.where(tril, w_ref[g], 0.0).astype(BF16)
            cols = slice(g * LANES, (g + 1) * LANES)
            gain_g = gain_ref[:, cols]
            for ci in range(cpb):
                rows = slice(ci * LANES, (ci + 1) * LANES)
                u_raw, v_raw, ds_v = u_ref[rows, cols], v_ref[rows, cols], ds_ref[rows, cols]
                ug = _gelu(u_raw)
                vh, rs = _group_norm(_gelu(v_raw))
                vn = (vh * gain_g).astype(BF16)
                y = _dot(wt, vn) + b_ref[g]
                dy = ds_v * ug
                dyb = dy.astype(BF16)
                du_ref[rows, cols] = (ds_v * y * _gelu_grad(u_raw)).astype(du_ref.dtype)
                dy_acc[g] += dy
                dw_ref[g] += _dot_nt(dyb, vn)
                dvn = _dot_tn(wt, dyb)
                dg_ref[:, cols] += jnp.sum(dvn * vh, axis=0, keepdims=True)
                dvh = dvn * gain_g
                dvg = rs * (dvh - jnp.mean(dvh, axis=-1, keepdims=True)
                            - vh * jnp.mean(dvh * vh, axis=-1, keepdims=True))
                dv_ref[rows, cols] = (dvg * _gelu_grad(v_raw)).astype(dv_ref.dtype)

        @pl.when(step == n_steps - 1)
        def _():
            ones = jnp.ones((8, LANES), F32)
            for g in range(groups):
                dw_ref[g] = jnp.where(tril, dw_ref[g], 0.0)
                db_ref[g] = lax.dot_general(ones, dy_acc[g], (((1,), (1,)), ((), ())),
                                            precision=lax.Precision.HIGHEST, preferred_element_type=F32)

    blk = lambda cb: pl.BlockSpec((rb, gw), lambda i: (i, cb))
    whole3 = pl.BlockSpec((groups, LANES, LANES), lambda i: (0, 0, 0))
    return pl.pallas_call(
        body, name="sgu_bwd", grid=(n_steps,),
        in_specs=[blk(0), blk(1), blk(0), pl.BlockSpec((1, gw), lambda i: (0, 0)), whole3, whole3],
        out_specs=[pl.BlockSpec((rb, 2 * gw), lambda i: (i, 0)), whole3,
                   pl.BlockSpec((groups, 8, LANES), lambda i: (0, 0, 0)), pl.BlockSpec((1, gw), lambda i: (0, 0))],
        out_shape=[jax.ShapeDtypeStruct((t, width), BF16),
                   jax.ShapeDtypeStruct((groups, LANES, LANES), F32), jax.ShapeDtypeStruct((groups, 8, LANES), F32),
                   jax.ShapeDtypeStruct((1, gw), F32)],
        scratch_shapes=[pltpu.VMEM((groups, LANES, LANES), F32)],
        compiler_params=_params(("arbitrary",)),
    )(proj, proj, ds, gain, w, bias)


def _shift_down(z, s):
    rows = lax.broadcasted_iota(jnp.int32, z.shape, 0)
    return jnp.where(rows >= s, pltpu.roll(z, s, axis=0), 0.0)


def _shift_up(z, s):
    n = z.shape[0]
    rows = lax.broadcasted_iota(jnp.int32, z.shape, 0)
    return jnp.where(rows < n - s, pltpu.roll(z, n - s, axis=0), 0.0)


def _conv_fwd(proj3, cw, tc):
    _, t, cd = proj3.shape

    def body(p_ref, w_ref, o_ref):
        z = p_ref[1] * p_ref[2]
        w = w_ref[...]
        zc = w[2:3] * z + w[1:2] * _shift_down(z, 1) + w[0:1] * _shift_down(z, 2)
        o_ref[...] = (p_ref[0] * zc).astype(o_ref.dtype)

    return pl.pallas_call(
        body, name="conv_fwd", grid=(cd // tc,),
        in_specs=[pl.BlockSpec((3, t, tc), lambda j: (0, 0, j)), pl.BlockSpec((8, tc), lambda j: (0, j))],
        out_specs=pl.BlockSpec((t, tc), lambda j: (0, j)),
        out_shape=jax.ShapeDtypeStruct((t, cd), BF16),
        compiler_params=_params(("parallel",)),
    )(proj3, cw)


def _conv_bwd(proj3, cw, dbz, tc):
    _, t, cd = proj3.shape

    def body(p_ref, w_ref, d_ref, o_ref, dw_ref):
        b, c, xin = p_ref[0], p_ref[1], p_ref[2]
        w = w_ref[...]
        z = c * xin
        z1, z2 = _shift_down(z, 1), _shift_down(z, 2)
        zc = w[2:3] * z + w[1:2] * z1 + w[0:1] * z2
        d = d_ref[...]
        dzc = d * b
        dz = w[2:3] * dzc + w[1:2] * _shift_up(dzc, 1) + w[0:1] * _shift_up(dzc, 2)
        o_ref[0] = (d * zc).astype(o_ref.dtype)
        o_ref[1] = (dz * xin).astype(o_ref.dtype)
        o_ref[2] = (dz * c).astype(o_ref.dtype)
        row = lax.broadcasted_iota(jnp.int32, (8, tc), 0)
        dw0 = jnp.sum(dzc * z2, axis=0, keepdims=True)
        dw1 = jnp.sum(dzc * z1, axis=0, keepdims=True)
        dw2 = jnp.sum(dzc * z, axis=0, keepdims=True)
        dw_ref[...] = jnp.where(row == 0, dw0, 0.0) + jnp.where(row == 1, dw1, 0.0) + jnp.where(row == 2, dw2, 0.0)

    return pl.pallas_call(
        body, name="conv_bwd", grid=(cd // tc,),
        in_specs=[pl.BlockSpec((3, t, tc), lambda j: (0, 0, j)), pl.BlockSpec((8, tc), lambda j: (0, j)),
                  pl.BlockSpec((t, tc), lambda j: (0, j))],
        out_specs=[pl.BlockSpec((3, t, tc), lambda j: (0, 0, j)), pl.BlockSpec((8, tc), lambda j: (0, j))],
        out_shape=[jax.ShapeDtypeStruct((3, t, cd), BF16), jax.ShapeDtypeStruct((8, cd), F32)],
        compiler_params=_params(("parallel",)),
    )(proj3, cw, dbz)


def _place():
    x, y, c = lax.axis_index("x"), lax.axis_index("y"), lax.axis_index("c")
    chips = [(1 - x, y), (x, 1 - y), (1 - x, 1 - y)]
    return x, y, c, chips


def _any_specs(n):
    return [pl.BlockSpec(memory_space=pl.ANY) for _ in range(n)]


HBM_SPEC = pl.BlockSpec(memory_space=pltpu.HBM)
SEM_SPEC = pl.BlockSpec(memory_space=pltpu.SEMAPHORE)
ORDERED_EFFECT = pltpu.SideEffectType.DATAFLOW_SIDE_EFFECTING


def _in_hbm(a):
    return pltpu.with_memory_space_constraint(a, pltpu.HBM)


def _token():
    return jax.ShapeDtypeStruct((8, LANES), F32), pl.BlockSpec(memory_space=pltpu.VMEM)


def _gather_start(name, groups):
    sizes = [len(g) for g in groups]
    flat = [b for g in groups for b in g]
    n, ng = len(flat), len(groups)

    def body(*refs):
        ins, sems, token = refs[:n], refs[n:n + 2 * ng], refs[-1]
        x, y, c, chips = _place()
        me = 2 * x + y
        i = 0
        for gi, size in enumerate(sizes):
            for j in range(size):
                blk = ins[i].at[me, c]
                for k, chip in enumerate(chips):
                    pltpu.make_async_remote_copy(src_ref=blk, dst_ref=blk, send_sem=sems[2 * gi].at[3 * j + k],
                                                 recv_sem=sems[2 * gi + 1].at[3 * j + k],
                                                 device_id=(*chip, c), device_id_type=MESH).start()
                i += 1
        token[...] = jnp.zeros_like(token)

    tok_shape, tok_spec = _token()
    res = pl.pallas_call(
        body, name=name,
        in_specs=[HBM_SPEC] * n,
        out_specs=[SEM_SPEC] * (2 * ng) + [HBM_SPEC] * n + [tok_spec],
        out_shape=[pltpu.SemaphoreType.DMA((3 * size,)) for size in sizes for _ in (0, 1)]
        + [pltpu.HBM(b.shape, b.dtype) for b in flat] + [tok_shape],
        input_output_aliases={i: 2 * ng + i for i in range(n)},
        compiler_params=pltpu.CompilerParams(has_side_effects=ORDERED_EFFECT),
    )(*[_in_hbm(b) for b in flat])
    out, i = [], 2 * ng
    for gi, size in enumerate(sizes):
        out.append((res[2 * gi], res[2 * gi + 1], list(res[i:i + size])))
        i += size
    return out, res[-1]


def _gather_wait(tag, send, recv, bufs, after):
    n = len(bufs)
    after = tuple(after) if isinstance(after, (tuple, list)) else (after,)

    def body(*refs):
        ins, send_ref, recv_ref = refs[:n], refs[n], refs[n + 1]
        x, y, c, chips = _place()
        me = 2 * x + y
        for j in range(n):
            for k, (px, py) in enumerate(chips):
                cp = pltpu.make_async_remote_copy(src_ref=ins[j].at[me, c], dst_ref=ins[j].at[2 * px + py, c],
                                                  send_sem=send_ref.at[3 * j + k], recv_sem=recv_ref.at[3 * j + k],
                                                  device_id=(px, py, c), device_id_type=MESH)
                cp.wait_send()
                cp.wait_recv()

    return pl.pallas_call(
        body, name="gather_wait_" + tag,
        in_specs=[HBM_SPEC] * n + [SEM_SPEC, SEM_SPEC] + _any_specs(len(after)),
        out_specs=[HBM_SPEC] * n,
        out_shape=[pltpu.HBM(b.shape, b.dtype) for b in bufs],
        input_output_aliases={i: i for i in range(n)},
        compiler_params=pltpu.CompilerParams(has_side_effects=ORDERED_EFFECT),
    )(*bufs, send, recv, *after)


def _gather_forward(tag, bufs):
    n = len(bufs)

    def body(*refs):
        ins, outs = refs[:n], refs[n:2 * n]
        send, recv = refs[2 * n:]
        x, y, c, chips = _place()
        sib = (x, y, 1 - c)

        def cp(i, k, slot, half):
            return pltpu.make_async_remote_copy(src_ref=ins[i].at[slot, half], dst_ref=outs[i].at[slot, half],
                                                send_sem=send.at[3 * i + k], recv_sem=recv.at[3 * i + k],
                                                device_id=sib, device_id_type=MESH)

        cps = [cp(i, k, 2 * px + py, c) for i in range(n) for k, (px, py) in enumerate(chips)]
        for d in cps:
            d.start()
        for i in range(n):
            for k, (px, py) in enumerate(chips):
                cp(i, k, 2 * px + py, 1 - c).wait_recv()
        for d in cps:
            d.wait_send()

    return pl.pallas_call(
        body, name="gather_forward_" + tag,
        in_specs=_any_specs(n), out_specs=_any_specs(n),
        out_shape=[jax.ShapeDtypeStruct(b.shape, b.dtype) for b in bufs],
        scratch_shapes=[pltpu.SemaphoreType.DMA((3 * n,))] * 2,
        input_output_aliases={i: i for i in range(n)},
        compiler_params=pltpu.CompilerParams(has_side_effects=True),
    )(*bufs)


def _pair_route(srcs, zones):
    x, y, c, _ = _place()
    return [(srcs[i].at[j, 1 - c], zones[i].at[j], (x, y, 1 - c)) for i in range(len(srcs)) for j in range(N_CHIPS)]


def _chip_route(srcs, zones):
    x, y, c, chips = _place()
    return [(srcs[i].at[2 * px + py], zones[i].at[k], (px, py, c)) for i in range(len(srcs)) for k, (px, py) in enumerate(chips)]


def _all_route(srcs, zones):
    x, y, c, _ = _place()
    flips = [(fx, fy, fc) for fx in (0, 1) for fy in (0, 1) for fc in (0, 1)][1:]
    return [(srcs[0], zones[0].at[4 * x + 2 * y + c], (x + fx - 2 * x * fx, y + fy - 2 * y * fy, c + fc - 2 * c * fc))
            for fx, fy, fc in flips]


def _share_route(srcs, zones):
    x, y, c, _ = _place()
    return [(s.at[c], s.at[c], (x, y, 1 - c)) for s in srcs]


def _exchange_start(name, route, n_copies, srcs, zones):
    n, nz = len(srcs), len(zones)
    lands = [lax.empty(z, a.dtype) if isinstance(z, tuple) else z for z, a in zip(zones, srcs)]

    def body(*refs):
        ins, zone_refs, send, recv, token = refs[:n], refs[n:n + nz], refs[n + nz], refs[n + nz + 1], refs[-1]
        for k, (src, dst, dev) in enumerate(route(ins, zone_refs)):
            pltpu.make_async_remote_copy(src_ref=src, dst_ref=dst, send_sem=send.at[k], recv_sem=recv.at[k],
                                         device_id=dev, device_id_type=MESH).start()
        token[...] = jnp.zeros_like(token)

    tok_shape, tok_spec = _token()
    res = pl.pallas_call(
        body, name=name,
        in_specs=[HBM_SPEC] * (n + nz),
        out_specs=[SEM_SPEC, SEM_SPEC] + [HBM_SPEC] * (n + nz) + [tok_spec],
        out_shape=[pltpu.SemaphoreType.DMA((n_copies,))] * 2 + [pltpu.HBM(a.shape, a.dtype) for a in srcs + lands]
        + [tok_shape],
        input_output_aliases={i: 2 + i for i in range(n + nz)},
        compiler_params=pltpu.CompilerParams(has_side_effects=ORDERED_EFFECT),
    )(*[_in_hbm(a) for a in srcs + lands])
    return (res[0], res[1], list(res[2:2 + n]), list(res[2 + n:2 + n + nz])), res[-1]


def _exchange_wait(name, route, started, after):
    send, recv, srcs, lands = started
    n, nz = len(srcs), len(lands)
    after = tuple(after) if isinstance(after, (tuple, list)) else (after,)

    def body(*refs):
        ins, zone_refs, send_ref, recv_ref = refs[:n], refs[n:n + nz], refs[n + nz], refs[n + nz + 1]
        for k, (src, dst, dev) in enumerate(route(ins, zone_refs)):
            cp = pltpu.make_async_remote_copy(src_ref=src, dst_ref=dst, send_sem=send_ref.at[k], recv_sem=recv_ref.at[k],
                                              device_id=dev, device_id_type=MESH)
            cp.wait_send()
            cp.wait_recv()

    res = pl.pallas_call(
        body, name=name,
        in_specs=[HBM_SPEC] * (n + nz) + [SEM_SPEC, SEM_SPEC] + _any_specs(len(after)),
        out_specs=[HBM_SPEC] * (n + nz),
        out_shape=[pltpu.HBM(a.shape, a.dtype) for a in srcs + lands],
        input_output_aliases={i: i for i in range(n + nz)},
        compiler_params=pltpu.CompilerParams(has_side_effects=ORDERED_EFFECT),
    )(*srcs, *lands, send, recv, *after)
    return list(res[:n]), list(res[n:])


def _spread(v):
    rows, cols = v.shape
    tr = _row_tile(rows, cols, budget=256 * 1024)

    def body(v_ref, o_ref):
        o_ref[...] = jnp.broadcast_to(v_ref[...][None], o_ref.shape)

    return pl.pallas_call(body, name="spread_small_grads", grid=(rows // tr,),
                          in_specs=[pl.BlockSpec((tr, cols), lambda r: (r, 0))],
                          out_specs=pl.BlockSpec((8, tr, cols), lambda r: (0, r, 0)),
                          out_shape=jax.ShapeDtypeStruct((8, rows, cols), v.dtype),
                          compiler_params=_params(("parallel",)))(v)


def _row_tile(rows, cols, itemsize=4, budget=2 * 1024 * 1024):
    best = None
    for t in range(8, rows + 1, 8):
        if rows % t == 0 and t * cols * itemsize <= budget:
            best = t
    return best if best is not None else rows


def _my_chip():
    return 2 * lax.axis_index("x") + lax.axis_index("y")


def _pair_sum(g5, gsib):
    _, _, rh, cols = g5.shape
    tr = _row_tile(rh, cols)

    def body(a_ref, b_ref, o_ref):
        o_ref[...] = (a_ref[...].astype(F32) + b_ref[...].astype(F32)).astype(o_ref.dtype)

    return pl.pallas_call(body, name="grad_pair_sum", grid=(N_CHIPS, rh // tr),
                          in_specs=[pl.BlockSpec((None, None, tr, cols), lambda j, r: (j, lax.axis_index("c"), r, 0)),
                                    pl.BlockSpec((None, tr, cols), lambda j, r: (j, r, 0))],
                          out_specs=pl.BlockSpec((None, tr, cols), lambda j, r: (j, r, 0)),
                          out_shape=jax.ShapeDtypeStruct((N_CHIPS, rh, cols), BF16),
                          compiler_params=_params(("parallel", "parallel")))(g5, gsib)


def _chip_sum(part, recv):
    _, rh, cols = part.shape
    tr = _row_tile(rh, cols)

    def body(a_ref, b_ref, o_ref):
        acc = a_ref[...].astype(F32)
        for k in range(3):
            acc = acc + b_ref[k].astype(F32)
        o_ref[...] = acc

    return pl.pallas_call(body, name="grad_chip_sum", grid=(rh // tr,),
                          in_specs=[pl.BlockSpec((None, tr, cols), lambda r: (_my_chip(), r, 0)),
                                    pl.BlockSpec((3, tr, cols), lambda r: (0, r, 0))],
                          out_specs=pl.BlockSpec((None, tr, cols), lambda r: (lax.axis_index("c"), r, 0)),
                          out_shape=jax.ShapeDtypeStruct((2, rh, cols), F32),
                          compiler_params=_params(("parallel",)))(part, recv)


def _sum_devices(g):
    _, rows, cols = g.shape
    tr = _row_tile(rows, cols, budget=256 * 1024)

    def body(g_ref, o_ref):
        acc = g_ref[0]
        for d in range(1, 8):
            acc = acc + g_ref[d]
        o_ref[...] = acc

    return pl.pallas_call(body, name="sum_small_grads", grid=(rows // tr,),
                          in_specs=[pl.BlockSpec((8, tr, cols), lambda r: (0, r, 0))],
                          out_specs=pl.BlockSpec((tr, cols), lambda r: (r, 0)),
                          out_shape=jax.ShapeDtypeStruct((rows, cols), F32),
                          compiler_params=_params(("parallel",)))(g)


def _place_shard(w, layer, dtype, deps=()):
    _, rows, cols = w.shape
    tr = _row_tile(rows, cols)

    def body(i_ref, *rest):
        o_ref = rest[-1]
        o_ref[...] = i_ref[...].astype(o_ref.dtype)

    out = pl.pallas_call(body, name="place_shard", grid=(rows // tr,),
                         in_specs=[pl.BlockSpec((None, tr, cols), lambda r: (layer, r, 0))] + _any_specs(len(deps)),
                         out_specs=pl.BlockSpec((None, tr, cols), lambda r: (_my_chip(), r, 0)),
                         out_shape=jax.ShapeDtypeStruct((N_CHIPS, rows, cols), dtype),
                         compiler_params=_params(("parallel",)))(w, *deps)
    return out.reshape(N_CHIPS, 2, rows // 2, cols)


def _adamw(w, gs, m, v, first=0, into=None):
    n_all, rows, cols = w.shape
    n_layers = len(gs)
    tr = _row_tile(rows, cols)
    n_into = 0 if into is None else 4

    def body(w_ref, m_ref, v_ref, *rest):
        g_refs = rest[:n_layers]
        go_ref, d_ref, mo_ref, vo_ref = rest[n_layers + n_into:]
        gv = g_refs[0][...]
        for layer in range(1, n_layers):
            gv = jnp.where(pl.program_id(0) == layer, g_refs[layer][...], gv)
        d_ref[...], mo_ref[...], vo_ref[...] = _adamw_math(w_ref[...], gv, m_ref[...], v_ref[...])
        go_ref[...] = gv

    spec = pl.BlockSpec((None, tr, cols), lambda layer, r: (first + layer, r, 0))
    g_specs = [pl.BlockSpec((tr, cols), lambda layer, r, own=own: (jnp.where(layer == own, r, 0), 0))
               for own in range(n_layers)]
    return pl.pallas_call(body, name="adamw", grid=(n_layers, rows // tr),
                          in_specs=[spec] * 3 + g_specs + _any_specs(n_into),
                          out_specs=[spec] * 4, out_shape=[jax.ShapeDtypeStruct((n_all, rows, cols), F32)] * 4,
                          input_output_aliases={3 + n_layers + k: k for k in range(n_into)},
                          compiler_params=_params(("parallel", "parallel")))(w, m, v, *gs, *(into or ()))


def _pad_rope(w):
    z = jnp.zeros(w.shape[:-1] + (ROPE_HALF,), w.dtype)
    return jnp.concatenate([w[..., :ROPE_HALF], z, w[..., ROPE_HALF:], z], axis=-1)


def _unpad_rope(g):
    return jnp.concatenate([g[..., :ROPE_HALF], g[..., ROPE:ROPE + ROPE_HALF]], axis=-1)


def _unstack_cols(s):
    n, r, cs = s.shape
    return jnp.transpose(s, (1, 0, 2)).reshape(r, n * cs)


def _stack_cols(f):
    r, cfull = f.shape
    return jnp.transpose(f.reshape(r, N_CHIPS, cfull // N_CHIPS), (1, 0, 2))


def _small_shard(norm, conv):
    return jnp.concatenate([jnp.pad(norm, ((0, 15), (0, 0))), jnp.pad(conv, ((0, 13), (0, 0)))], axis=0)


def _flat_rows(a):
    return a.reshape(-1, LANES)


def _pack_small(arrs):
    return jnp.concatenate([_flat_rows(a.astype(F32)) for a in arrs], axis=0)


def _unpack_small(flat, like):
    out, r = [], 0
    for a in like:
        n = a.size // LANES
        out.append(flat[r:r + n].reshape(a.shape))
        r += n
    return out


def kernel(x, positions, e_norm_mix, e_w_in, e_q_norm, e_w_uq, e_kv_norm, e_w_ukv, e_v_norm, e_sgu_w, e_sgu_b, e_mla_out_norm, e_sgu_out_norm, e_w_out, o_norm_mix, o_w_in, o_conv_w, o_w_out, mlp_norm, mlp_w1, mlp_w2, final_norm, loss_target, m_e_norm_mix, m_e_w_in, m_e_q_norm, m_e_w_uq, m_e_kv_norm, m_e_w_ukv, m_e_v_norm, m_e_sgu_w, m_e_sgu_b, m_e_mla_out_norm, m_e_sgu_out_norm, m_e_w_out, m_o_norm_mix, m_o_w_in, m_o_conv_w, m_o_w_out, m_mlp_norm, m_mlp_w1, m_mlp_w2, m_final_norm, v_e_norm_mix, v_e_w_in, v_e_q_norm, v_e_w_uq, v_e_kv_norm, v_e_w_ukv, v_e_v_norm, v_e_sgu_w, v_e_sgu_b, v_e_mla_out_norm, v_e_sgu_out_norm, v_e_w_out, v_o_norm_mix, v_o_w_in, v_o_conv_w, v_o_w_out, v_mlp_norm, v_mlp_w1, v_mlp_w2, v_final_norm):
    t, d = x.shape[1], x.shape[2]
    ql, kvl = e_q_norm.shape[1], e_kv_norm.shape[1]
    groups = e_v_norm.shape[1]
    gw = groups * LANES
    heads = N_CHIPS * e_w_uq.shape[2] // (LANES + ROPE)
    hw = heads * LANES
    mix = hw + gw
    ei = N_CHIPS * e_w_in.shape[2]
    cd = N_CHIPS * o_conv_w.shape[2]
    ff = N_CHIPS * mlp_w1.shape[2]
    ffs = ff // N_CHIPS
    pi = 2 * gw + ql + kvl + LANES
    assert e_norm_mix.shape[0] == 1 and o_norm_mix.shape[0] == 1 and mlp_norm.shape[0] == 2
    assert ei == ql + kvl + ROPE + 2 * gw and cd == d and e_sgu_w.shape[2] == LANES
    assert (2 * gw) % ql == 0 and (2 * gw + ql) % kvl == 0 and t % LANES == 0
    scale = (LANES + ROPE) ** -0.5

    tr = min(256, t)
    tm = _pick(t, 1024, 8)
    kt, kd = _pick(t, 2048, 8), _pick(d, 2048)
    xs = x.reshape(t, d)
    tgt = loss_target.reshape(t, d)

    small_shard = _small_shard(o_norm_mix, o_conv_w[0])
    first, tok = _gather_start("gather_start_e", [
        [_place_shard(e_w_in, 0, BF16)],
        [_place_shard(e_w_uq, 0, BF16), _place_shard(e_w_ukv, 0, BF16), _place_shard(e_w_out, 0, BF16),
         _place_shard(small_shard[None], 0, F32)]])
    rest, tok = _gather_start("gather_start_rest", [
        [_place_shard(mlp_w1, 0, BF16, (tok,))], [_place_shard(mlp_w2, 0, BF16, (tok,))],
        [_place_shard(o_w_in, 0, BF16, (tok,)), _place_shard(o_w_out, 0, BF16, (tok,))],
        [_place_shard(mlp_w1, 1, BF16, (tok,))], [_place_shard(mlp_w2, 1, BF16, (tok,))]])
    started = first + rest

    def gathered(gi, tag, after):
        send, recv, bufs = started[gi]
        bufs = _gather_forward(tag, _gather_wait(tag, send, recv, bufs, after))
        return [b.reshape(N_CHIPS, 2 * b.shape[2], b.shape[3]) for b in bufs]

    g_e = e_norm_mix
    h0 = _norm_fwd("e_norm", xs, g_e, tr)
    inv_freq = ROPE_BASE ** (-jnp.arange(0, ROPE, 2, dtype=F32) / ROPE)
    zeros32 = jnp.zeros((ROPE_HALF,), F32)
    ones32 = jnp.ones((ROPE_HALF,), F32)
    invf = jnp.concatenate([inv_freq, zeros32, inv_freq, zeros32]).reshape(1, LANES)
    cmask = jnp.concatenate([ones32, zeros32, ones32, zeros32]).reshape(1, LANES)
    smask = jnp.concatenate([-ones32, zeros32, ones32, zeros32]).reshape(1, LANES)
    ctab, stab = _rope_tables(positions.reshape(t, 1).astype(F32), invf, cmask, smask, tr)

    w_in_g, = gathered(0, "e_in", (h0, ctab, tok))
    full = _unstack_cols(w_in_g)
    c2, c3 = ql + kvl, ql + kvl + ROPE
    w_in_all = jnp.concatenate([full[:, c3:], full[:, :c2], _pad_rope(full[:, c2:c3])], axis=1)
    proj, = _matmul("e_proj", Mat(h0, t, d), Mat(w_in_all, d, pi), "nn", [_out(t, pi, F32)], tm, _pick(pi, 1024), kd)

    w_uq_g, w_ukv_g, w_eout_g, small_g = gathered(1, "e", proj)
    full = _unstack_cols(w_uq_g).reshape(ql, heads, LANES + ROPE)
    w_q_all = jnp.concatenate([full[:, :, :LANES].reshape(ql, hw), _pad_rope(full[:, :, LANES:]).reshape(ql, hw)], axis=1)
    full = _unstack_cols(w_ukv_g).reshape(kvl, heads, 2 * LANES)
    w_kv_all = jnp.concatenate([full[:, :, :LANES].reshape(kvl, hw), full[:, :, LANES:].reshape(kvl, hw)], axis=1)
    w_eout = w_eout_g.reshape(mix, d)
    g_o = small_g[:, 0].reshape(1, d)
    conv_w = jnp.pad(jnp.transpose(small_g[:, 16:19], (1, 0, 2)).reshape(3, cd), ((0, 5), (0, 0)))

    g_q, g_kv = e_q_norm, e_kv_norm
    g_vn = e_v_norm.reshape(1, gw)
    sgu_w = e_sgu_w[0]
    sgu_b = jnp.broadcast_to(e_sgu_b[0][:, :, None], (groups, LANES, LANES))
    g_mla, g_sgu = e_mla_out_norm, e_sgu_out_norm
    g_m0, g_m1 = mlp_norm[0:1], mlp_norm[1:2]
    g_f = final_norm.reshape(1, d)

    def mlp_fwd(tag, xin, g, gi):
        hm = _norm_fwd("mlp_norm_" + tag, xin, g, tr)
        tn = _pick(ffs, 1024)
        w1 = Mat(gathered(gi, "w1_" + tag, hm)[0], d, ff, "colstack")
        a, act = _matmul("mlp_up_" + tag, Mat(hm, t, d), w1, "nn",
                         [_out(t, ff, BF16), _out(t, ff, BF16)], tm, tn, kd,
                         epilogue=lambda z: (jnp.maximum(z, 0.0), jnp.square(jnp.maximum(z, 0.0))))
        w2 = Mat(gathered(gi + 1, "w2_" + tag, act)[0].reshape(ff, d), ff, d)
        xo, = _matmul("mlp_down_" + tag, Mat(act, t, ff), w2, "nn",
                      [_out(t, d, F32)], tm, _pick(d, 1024), _pick(ffs, 2048),
                      epilogue=lambda z, r: (z + r,), extras=[Mat(xin, t, d)])
        return xo, hm, a, act, w1, w2

    def chip_start(tag, part):
        return _exchange_start("scatter_start_" + tag, _chip_route, 3 * len(part), part, [(3,) + p.shape[1:] for p in part])

    def pair_start(tag, stacked):
        g5 = [g.reshape(N_CHIPS, 2, g.shape[1] // 2, g.shape[2]) for g in stacked]
        return _exchange_start("pair_start_" + tag, _pair_route, N_CHIPS * len(g5), g5,
                               [(N_CHIPS,) + g.shape[2:] for g in g5])

    def pair_finish(tag, started, after):
        g5, from_sib = _exchange_wait("pair_wait_" + tag, _pair_route, started, after)
        return chip_start(tag, [_pair_sum(a, b) for a, b in zip(g5, from_sib)])

    def summed(tag, sc, after):
        part, lands = _exchange_wait("scatter_wait_" + tag, _chip_route, sc, after)
        half = [_chip_sum(p, r) for p, r in zip(part, lands)]
        return _exchange_start("share_start_" + tag, _share_route, len(half), half, [])

    def shared(tag, started, after):
        bufs, _ = _exchange_wait("share_wait_" + tag, _share_route, started, after)
        return [r.reshape(2 * r.shape[1], r.shape[2]) for r in bufs]

    def mlp_bwd(tag, dx, dxb, xin, g, w1, w2, hm, a, act, deps, extra_grads=(), ready=None):
        tn = _pick(ffs, 1024)
        if ready is not None:
            sh, tok = summed(ready[0], ready[1], deps)
            deps = (tok,)
        dz, = _matmul("mlp_dact_" + tag, Mat(dxb, t, d), w2, "nt",
                      [_out(t, ff, BF16)], tm, tn, kd,
                      epilogue=lambda z, av: (z * (2.0 * av.astype(F32)),), extras=[Mat(a, t, ff)], deps=deps)
        side1 = side2 = None
        if ready is not None:
            r_w1, r_w2 = shared(ready[0], sh, dz)
            side1, side2 = (mlp_w1, r_w1, m_mlp_w1, v_mlp_w1, 1), (mlp_w2, r_w2, m_mlp_w2, v_mlp_w2, 1)
        dw2, *upd2 = _matmul("mlp_dw2_" + tag, Mat(act, t, ff), Mat(dxb, t, d), "tn",
                             [_out(ff, d, BF16)], tn, _pick(d, 1024), kt, side=side2)
        dw1, *upd1 = _matmul("mlp_dw1_" + tag, Mat(hm, t, d), Mat(dz, t, ff), "tn",
                             [_out(d, ff, BF16, "colstack", (), (N_CHIPS, d, ffs))], _pick(d, 1024), tn, kt, side=side1)
        started, tok = pair_start("m" + tag, [dw1, dw2.reshape(N_CHIPS, ffs, d), *extra_grads])
        dhm, = _matmul("mlp_dh_" + tag, Mat(dz, t, ff), w1, "nt",
                       [_out(t, d, F32)], tm, _pick(d, 1024), _pick(ffs, 2048), deps=(tok,))
        dxo, dxob, dg = _norm_bwd("mlp_norm_bwd_" + tag, dhm, xin, g, dx, tr)
        sc, tok = pair_finish("m" + tag, started, dxo)
        return dxo, dxob, dg, sc, tok, upd1, upd2

    cq_cb, ckv_cb, kr_cb = 2 * gw // ql, (2 * gw + ql) // kvl, (2 * gw + ql + kvl) // LANES
    qn, kvn = _rowwise("qkv_norm", lambda a, b, ga, gb: (_rms(a, ga), _rms(b, gb)), t // tr,
                       [_rt(proj, tr, ql, cq_cb), _rt(proj, tr, kvl, ckv_cb), _whole(g_q), _whole(g_kv)],
                       [_rt_out(t, ql, BF16, tr), _rt_out(t, kvl, BF16, tr)])
    qfull, = _matmul("q_up", Mat(qn, t, ql), Mat(w_q_all, ql, 2 * hw), "nn", [_out(t, 2 * hw, F32)], tm, _pick(2 * hw, 1024), ql)
    kvall, = _matmul("kv_up", Mat(kvn, t, kvl), Mat(w_kv_all, kvl, 2 * hw), "nn", [_out(t, 2 * hw, BF16)], tm, _pick(2 * hw, 1024), kvl)
    qall, kr = _rope_fwd(qfull, proj, kr_cb, ctab, stab, heads, tr)
    att, lse_row = _attn_fwd(qall, kvall, kr, heads, scale, tr)
    rb = min(2 * LANES, t)
    sgu = _sgu_fwd(proj, g_vn, sgu_w, sgu_b, groups, rb)
    mixed = _rowwise("mix_norm", lambda a, s, ga, gs: jnp.concatenate([_rms(a, ga), _rms(s, gs)], axis=1), t // tr,
                     [_rt(att, tr), _rt(sgu, tr), _whole(g_mla), _whole(g_sgu)], [_rt_out(t, mix, BF16, tr)])[0]
    x1, = _matmul("e_out", Mat(mixed, t, mix), Mat(w_eout, mix, d), "nn", [_out(t, d, F32)], tm, _pick(d, 1024), _pick(mix, 2048),
                  epilogue=lambda z, r: (z + r,), extras=[Mat(xs, t, d)])
    x2, hm0, a0, act0, w1_0, w2_0 = mlp_fwd("0", x1, g_m0, 2)

    w_oin_g, w_oout_g = gathered(4, "o", x2)
    w_oout = w_oout_g.reshape(cd, d)
    h1 = _norm_fwd("o_norm", x2, g_o, tr)
    oin = Mat(_unstack_cols(w_oin_g), d, 3 * cd)
    tn_o = _pick(_gcd(3 * cd // N_CHIPS, cd), 512)
    proj3, = _matmul("o_proj", Mat(h1, t, d), oin, "nn", [_out(t, 3 * cd, F32, "colstack", (), (3, t, cd))],
                     tm, _pick(cd, 1024), kd)
    tc = _pick(cd, 256)
    bz = _conv_fwd(proj3, conv_w, tc)
    x3, = _matmul("o_out", Mat(bz, t, cd), Mat(w_oout, cd, d), "nn", [_out(t, d, F32)], tm, _pick(d, 1024), _pick(cd, 2048),
                  epilogue=lambda z, r: (z + r,), extras=[Mat(x2, t, d)])
    x4, hm1, a1, act1, w1_1, w2_1 = mlp_fwd("1", x3, g_m1, 5)

    def final_fn(xv, gv, tv):
        r = lax.rsqrt(jnp.mean(xv * xv, axis=-1, keepdims=True) + EPS)
        xh = xv * r
        err = xh * gv - tv
        dy = err * (1.0 / d)
        dxh = dy * gv
        dx = r * (dxh - xh * jnp.mean(dxh * xh, axis=-1, keepdims=True))
        sq = jnp.sum(err * err, axis=0, keepdims=True)
        part = sq[:, :LANES]
        for k in range(1, d // LANES):
            part = part + sq[:, k * LANES:(k + 1) * LANES]
        return dx, dx, part, jnp.sum(dy * xh, axis=0, keepdims=True)

    dx4, dx4b, loss_vec, dg_f = _rowwise("loss_final_norm", final_fn, t // tr, [_rt(x4, tr), _whole(g_f), _rt(tgt, tr)],
                                         [_rt_out(t, d, F32, tr), _rt_out(t, d, BF16, tr)],
                                         [jax.ShapeDtypeStruct((1, LANES), F32), jax.ShapeDtypeStruct((1, d), F32)])
    loss = lax.psum(0.5 * jnp.sum(loss_vec) / d, ("x", "y", "c"))

    dx3, dx3b, dg_m1, sc_m1, tok, _, _ = mlp_bwd("1", dx4, dx4b, x3, g_m1, w1_1, w2_1, hm1, a1, act1, ())

    dbz, = _matmul("o_out_dx", Mat(dx3b, t, d), Mat(w_oout, cd, d), "nt", [_out(t, cd, F32)], tm, _pick(cd, 1024), kd,
                   deps=(tok,))
    dw_oout, = _matmul("o_out_dw", Mat(bz, t, cd), Mat(dx3b, t, d), "tn", [_out(cd, d, BF16)], _pick(cd, 1024), _pick(d, 1024), kt)
    dproj3, dconv = _conv_bwd(proj3, conv_w, dbz, tc)
    dp3 = Mat(dproj3, t, 3 * cd, "colstack")
    dw_oin, = _matmul("o_proj_dw", Mat(h1, t, d), dp3, "tn", [_out(d, 3 * cd, BF16, "colstack", (), (N_CHIPS, d, 3 * cd // N_CHIPS))],
                      _pick(d, 1024), tn_o, kt)
    started_o, tok = pair_start("o", [dw_oin, dw_oout.reshape(N_CHIPS, cd // N_CHIPS, d)])
    dh1, = _matmul("o_proj_dx", dp3, oin, "nt", [_out(t, d, F32)], tm, _pick(d, 1024), _pick(cd, 2048), deps=(tok,))
    dx2, dx2b, dg_o = _norm_bwd("o_norm_bwd", dh1, x2, g_o, dx3, tr)
    sc_o, tok = pair_finish("o", started_o, dx2)

    dconv_s = jnp.transpose(dconv[:3].reshape(3, N_CHIPS, cd // N_CHIPS), (1, 0, 2))
    gsmall = jnp.concatenate([jnp.pad(dg_o.reshape(N_CHIPS, 1, d // N_CHIPS), ((0, 0), (0, 15), (0, 0))),
                              jnp.pad(dconv_s, ((0, 0), (0, 13), (0, 0)))], axis=1)
    dx1, dx1b, dg_m0, sc_m0, tok, upd_w1, upd_w2 = mlp_bwd("0", dx2, dx2b, x1, g_m0, w1_0, w2_0, hm0, a0, act0, (tok,),
                                                           (gsmall,), ("m1", sc_m1))

    dmixed, = _matmul("e_out_dx", Mat(dx1b, t, d), Mat(w_eout, mix, d), "nt", [_out(t, mix, F32)], tm, _pick(mix, 1024), kd,
                      deps=(tok,))
    dw_eout, = _matmul("e_out_dw", Mat(mixed, t, mix), Mat(dx1b, t, d), "tn", [_out(mix, d, BF16)], _pick(mix, 1024), _pick(d, 1024), kt)

    def mixb_fn(dm, a, s, ga, gs):
        da, dga = _rms_bwd(dm[:, :hw], a, ga)
        dsg, dgs = _rms_bwd(dm[:, hw:], s, gs)
        prod = da * a
        cols = [jnp.broadcast_to(jnp.sum(prod[:, h * LANES:(h + 1) * LANES], axis=-1, keepdims=True), (tr, LANES))
                for h in range(heads)]
        return da, dsg, jnp.stack([_row_of(c) for c in cols], axis=0), dga, dgs

    da_b, dsgu, delta_row, dg_mla, dg_sgu = _rowwise(
        "mix_norm_bwd", mixb_fn, t // tr, [_rt(dmixed, tr), _rt(att, tr), _rt(sgu, tr), _whole(g_mla), _whole(g_sgu)],
        [_rt_out(t, hw, BF16, tr), _rt_out(t, gw, F32, tr),
         (jax.ShapeDtypeStruct((heads, 8, t), F32), pl.BlockSpec((heads, 8, tr), lambda i: (0, 0, i)))],
        [jax.ShapeDtypeStruct((1, hw), F32), jax.ShapeDtypeStruct((1, gw), F32)])

    dproj, dsgu_w, dsgu_b8, dg_vn = _sgu_bwd(proj, dsgu, g_vn, sgu_w, sgu_b, groups, rb)
    dq1, dq2, dk1, dvv, dkr_h = _attn_bwd(qall, kvall, kr, da_b, lse_row, delta_row, heads, scale, tr)
    dqfull, dproj = _rope_bwd(dq1, dq2, dkr_h, ctab, stab, heads, tr, dproj, kr_cb)
    dkvall = jnp.concatenate([dk1, dvv], axis=1)
    dw_q, = _matmul("q_up_dw", Mat(qn, t, ql), Mat(dqfull, t, 2 * hw), "tn", [_out(ql, 2 * hw, BF16)], ql, _pick(2 * hw, 1024), kt)
    dqn, = _matmul("q_up_dx", Mat(dqfull, t, 2 * hw), Mat(w_q_all, ql, 2 * hw), "nt", [_out(t, ql, F32)], tm, ql, _pick(2 * hw, 2048))
    dw_kv, = _matmul("kv_up_dw", Mat(kvn, t, kvl), Mat(dkvall, t, 2 * hw), "tn", [_out(kvl, 2 * hw, BF16)], kvl, _pick(2 * hw, 1024), kt)
    dkvn, = _matmul("kv_up_dx", Mat(dkvall, t, 2 * hw), Mat(w_kv_all, kvl, 2 * hw), "nt", [_out(t, kvl, F32)], tm, kvl, _pick(2 * hw, 2048))

    def qkvb_fn(da, db, a, b, ga, gb):
        dxa, dga = _rms_bwd(da, a, ga)
        dxb, dgb = _rms_bwd(db, b, gb)
        return jnp.concatenate([dxa, dxb], axis=1), dga, dgb

    assert (2 * gw) % (ql + kvl) == 0
    into = (jax.ShapeDtypeStruct(dproj.shape, dproj.dtype),
            pl.BlockSpec((tr, ql + kvl), lambda i: (i, 2 * gw // (ql + kvl))))
    dproj, dg_q, dg_kv = _rowwise(
        "qkv_norm_bwd", qkvb_fn, t // tr,
        [_rt(dqn, tr), _rt(dkvn, tr), _rt(proj, tr, ql, cq_cb), _rt(proj, tr, kvl, ckv_cb), _whole(g_q), _whole(g_kv)],
        [into], [jax.ShapeDtypeStruct((1, ql), F32), jax.ShapeDtypeStruct((1, kvl), F32)], deps=(dproj,), fill=(0, 0))
    dw_in, = _matmul("e_proj_dw", Mat(h0, t, d), Mat(dproj, t, pi), "tn", [_out(d, pi, BF16)], _pick(d, 1024), _pick(pi, 1024), kt)
    dh0, = _matmul("e_proj_dx", Mat(dproj, t, pi), Mat(w_in_all, d, pi), "nt", [_out(t, d, F32)], tm, _pick(d, 1024), _pick(pi, 4096))
    dx0, _, dg_e = _norm_bwd("e_norm_bwd", dh0, xs, g_e, dx1, tr)

    gfull = jnp.concatenate([dw_in[:, 2 * gw:2 * gw + c2], _unpad_rope(dw_in[:, 2 * gw + c2:]), dw_in[:, :2 * gw]], axis=1)
    gw_in = _stack_cols(gfull)
    gq = jnp.concatenate([dw_q[:, :hw].reshape(ql, heads, LANES), _unpad_rope(dw_q[:, hw:].reshape(ql, heads, LANES))], axis=-1)
    gw_uq = _stack_cols(gq.reshape(ql, heads * (LANES + ROPE)))
    gkv = jnp.concatenate([dw_kv[:, :hw].reshape(kvl, heads, LANES), dw_kv[:, hw:].reshape(kvl, heads, LANES)], axis=-1)
    gw_ukv = _stack_cols(gkv.reshape(kvl, heads * 2 * LANES))
    started_e, tok_pair = pair_start("e", [gw_in, gw_uq, gw_ukv, dw_eout.reshape(N_CHIPS, mix // N_CHIPS, d)])

    small_like = [e_norm_mix, e_q_norm, e_kv_norm, e_v_norm, e_sgu_w, e_sgu_b, e_mla_out_norm, e_sgu_out_norm, mlp_norm, final_norm]
    small_grads = [dg_e, dg_q, dg_kv, dg_vn, dsgu_w, dsgu_b8[:, 0, :], dg_mla, dg_sgu, jnp.concatenate([dg_m0, dg_m1], axis=0), dg_f]
    sflat = _pack_small(small_grads)
    pad = (-sflat.shape[0]) % 8
    sflat = jnp.pad(sflat, ((0, pad), (0, 0)))
    small_started, tok_small = _exchange_start("small_start", _all_route, 7, [sflat], [_spread(sflat)])

    sh_o, tok = summed("o", sc_o, (tok_pair, tok_small))
    sc_e, tok = pair_finish("e", started_e, tok)
    sh_m0, tok = summed("m0", sc_m0, tok)
    r_oin, r_oout = shared("o", sh_o, tok)
    late = {"o_w_in": _adamw(o_w_in, [r_oin], m_o_w_in, v_o_w_in),
            "o_w_out": _adamw(o_w_out, [r_oout], m_o_w_out, v_o_w_out)}
    r_w1_0, r_w2_0, r_small = shared("m0", sh_m0, late["o_w_in"][1])
    late["mlp_w1"] = _adamw(mlp_w1, [r_w1_0], m_mlp_w1, v_mlp_w1, into=upd_w1)
    late["mlp_w2"] = _adamw(mlp_w2, [r_w2_0], m_mlp_w2, v_mlp_w2, into=upd_w2)

    _, (all_small,) = _exchange_wait("small_wait", _all_route, small_started, late["mlp_w2"][1])
    g_small = _sum_devices(all_small)

    def padded(arrs):
        return jnp.pad(_pack_small(arrs), ((0, pad), (0, 0)))

    s_m = [m_e_norm_mix, m_e_q_norm, m_e_kv_norm, m_e_v_norm, m_e_sgu_w, m_e_sgu_b, m_e_mla_out_norm, m_e_sgu_out_norm, m_mlp_norm, m_final_norm]
    s_v = [v_e_norm_mix, v_e_q_norm, v_e_kv_norm, v_e_v_norm, v_e_sgu_w, v_e_sgu_b, v_e_mla_out_norm, v_e_sgu_out_norm, v_mlp_norm, v_final_norm]
    s_out = [_unpack_small(o[0], small_like)
             for o in _adamw(padded(small_like)[None], [g_small], padded(s_m)[None], padded(s_v)[None])]

    sm = [o[0] for o in _adamw(small_shard[None], [r_small], _small_shard(m_o_norm_mix, m_o_conv_w[0])[None],
                               _small_shard(v_o_norm_mix, v_o_conv_w[0])[None])]

    sh_e, tok = summed("e", sc_e, late["mlp_w2"][1])
    r_in, r_uq, r_ukv, r_eout = shared("e", sh_e, tok)
    big = dict(late)
    flip = lambda a: jnp.swapaxes(a, 1, 2)
    big.update({
        "e_w_in": [flip(o) for o in _adamw(flip(e_w_in), [r_in.T], flip(m_e_w_in), flip(v_e_w_in))],
        "e_w_uq": _adamw(e_w_uq, [r_uq], m_e_w_uq, v_e_w_uq),
        "e_w_ukv": _adamw(e_w_ukv, [r_ukv], m_e_w_ukv, v_e_w_ukv),
        "e_w_out": _adamw(e_w_out, [r_eout], m_e_w_out, v_e_w_out),
    })

    names = ["e_norm_mix", "e_w_in", "e_q_norm", "e_w_uq", "e_kv_norm", "e_w_ukv", "e_v_norm", "e_sgu_w", "e_sgu_b",
             "e_mla_out_norm", "e_sgu_out_norm", "e_w_out", "o_norm_mix", "o_w_in", "o_conv_w", "o_w_out",
             "mlp_norm", "mlp_w1", "mlp_w2", "final_norm"]
    shapes = {"e_w_in": e_w_in.shape, "e_w_uq": e_w_uq.shape, "e_w_ukv": e_w_ukv.shape, "e_w_out": e_w_out.shape,
              "o_w_in": o_w_in.shape, "o_w_out": o_w_out.shape, "mlp_w1": mlp_w1.shape, "mlp_w2": mlp_w2.shape}
    small_names = ["e_norm_mix", "e_q_norm", "e_kv_norm", "e_v_norm", "e_sgu_w", "e_sgu_b", "e_mla_out_norm",
                   "e_sgu_out_norm", "mlp_norm", "final_norm"]

    def leaf(kind, name):
        if name in big:
            return big[name][kind].reshape(shapes[name])
        if name == "o_norm_mix":
            return sm[kind][0:1]
        if name == "o_conv_w":
            return sm[kind][16:19].reshape(o_conv_w.shape)
        return s_out[kind][small_names.index(name)]

    outs = [loss, dx0.reshape(x.shape)]
    for kind in range(4):
        outs += [leaf(kind, nm) for nm in names]
    return tuple(outs)


def _gcd(a, b):
    while b:
        a, b = b, a % b
    return a
```

```python
import functools

import jax
import jax.numpy as jnp
from jax import lax
from jax.experimental import pallas as pl
from jax.experimental.pallas import tpu as pltpu

F32 = jnp.float32
BF16 = jnp.bfloat16
MESH = pl.DeviceIdType.MESH

LANES = 128
ROPE = 64
ROPE_HALF = ROPE // 2
ROPE_BASE = 10000.0
EPS = 1e-6
N_CHIPS = 4
VMEM_LIMIT = 48 * 1024 * 1024
NEG = -1e30

ADAM_LR = 0.001
ADAM_B1 = 0.9
ADAM_B2 = 0.999
ADAM_EPS = 1e-08
ADAM_WD = 0.01
ADAM_STEP = 10


def _pick(n, target, step=LANES):
    best = None
    for t in range(step, min(n, target) + 1, step):
        if n % t == 0:
            best = t
    return best if best is not None else n


def _params(sem, vmem=VMEM_LIMIT):
    return pltpu.CompilerParams(dimension_semantics=sem, vmem_limit_bytes=vmem)


class Mat:
    def __init__(self, arr, rows, cols, kind="plain", lead=(), col_off=0, shape=None, dtype=None):
        self.arr, self.rows, self.cols, self.kind, self.lead, self.col_off = arr, rows, cols, kind, tuple(lead), col_off
        self.shape = tuple(arr.shape) if arr is not None else tuple(shape)
        self.dtype = arr.dtype if arr is not None else dtype

    def sds(self):
        return jax.ShapeDtypeStruct(self.shape, self.dtype)

    def spec(self, br, bc, gridmap):
        lead, nl = self.lead, len(self.lead)
        if self.kind == "plain":
            assert self.col_off % bc == 0 and self.rows % br == 0 and self.cols % bc == 0, (self.shape, br, bc)
            off = self.col_off // bc
            block = (None,) * nl + (br, bc)

            def phys(rb, cb):
                return lead + (rb, cb + off)
        elif self.kind == "colstack":
            cs = self.shape[-1]
            assert cs % bc == 0 and self.rows % br == 0, (self.shape, br, bc)
            q = cs // bc
            block = (None,) * (nl + 1) + (br, bc)

            def phys(rb, cb):
                return (cb // q,) + lead + (rb, cb % q)
        else:
            rs = self.shape[-2]
            assert rs % br == 0 and self.cols % bc == 0, (self.shape, br, bc)
            q = rs // br
            block = (None,) * (nl + 1) + (br, bc)

            def phys(rb, cb):
                return (rb // q,) + lead + (rb % q, cb)

        return pl.BlockSpec(block, lambda *g: phys(*gridmap(*g)))


def _adamw_math(w, g, m, v):
    mn = ADAM_B1 * m + (1.0 - ADAM_B1) * g
    vn = ADAM_B2 * v + (1.0 - ADAM_B2) * jnp.square(g)
    m_hat = mn / (1.0 - ADAM_B1 ** ADAM_STEP)
    v_hat = vn / (1.0 - ADAM_B2 ** ADAM_STEP)
    return -ADAM_LR * (m_hat / (jnp.sqrt(v_hat) + ADAM_EPS) + ADAM_WD * w), mn, vn


def _matmul(name, a, b, mode, outs, tm, tn, tk, epilogue=None, extras=(), deps=()):
    if mode == "nn":
        m, k, n = a.rows, a.cols, b.cols
        a_spec = a.spec(tm, tk, lambda i, j, kk: (i, kk))
        b_spec = b.spec(tk, tn, lambda i, j, kk: (kk, j))
        dims = (((1,), (0,)), ((), ()))
    elif mode == "nt":
        m, k, n = a.rows, a.cols, b.rows
        a_spec = a.spec(tm, tk, lambda i, j, kk: (i, kk))
        b_spec = b.spec(tn, tk, lambda i, j, kk: (j, kk))
        dims = (((1,), (1,)), ((), ()))
    else:
        k, m, n = a.rows, a.cols, b.cols
        a_spec = a.spec(tk, tm, lambda i, j, kk: (kk, i))
        b_spec = b.spec(tk, tn, lambda i, j, kk: (kk, j))
        dims = (((0,), (0,)), ((), ()))
    assert m % tm == 0 and n % tn == 0 and k % tk == 0, (name, m, n, k, tm, tn, tk)
    grid = (m // tm, n // tn, k // tk)
    nk = grid[2]
    n_ex, n_out, n_dep = len(extras), len(outs), len(deps)
    tile = lambda i, j, kk: (i, j)

    def finish(z, ex, out_refs):
        vals = epilogue(z, *[e[...] for e in ex]) if epilogue is not None else (z,)
        for o, v in zip(out_refs, vals):
            o[...] = v.astype(o.dtype)

    def body_single(a_ref, b_ref, *rest):
        finish(lax.dot_general(a_ref[...], b_ref[...], dims, preferred_element_type=F32),
               rest[:n_ex], rest[n_ex + n_dep:n_ex + n_dep + n_out])

    def body_acc(a_ref, b_ref, *rest):
        acc = rest[-1]
        kk = pl.program_id(2)

        @pl.when(kk == 0)
        def _():
            acc[...] = jnp.zeros_like(acc)

        acc[...] += lax.dot_general(a_ref[...], b_ref[...], dims, preferred_element_type=F32)

        @pl.when(kk == nk - 1)
        def _():
            finish(acc[...], rest[:n_ex], rest[n_ex + n_dep:n_ex + n_dep + n_out])

    res = pl.pallas_call(
        body_single if nk == 1 else body_acc, name=name, grid=grid,
        in_specs=[a_spec, b_spec] + [e.spec(tm, tn, tile) for e in extras]
        + [pl.BlockSpec(memory_space=pl.ANY) for _ in deps],
        out_specs=[o.spec(tm, tn, tile) for o in outs],
        out_shape=[o.sds() for o in outs],
        scratch_shapes=[] if nk == 1 else [pltpu.VMEM((tm, tn), F32)],
        compiler_params=_params(("parallel", "parallel", "arbitrary")),
    )(a.arr, b.arr, *[e.arr for e in extras], *deps)
    return res


def _out(rows, cols, dtype, kind="plain", lead=(), shape=None):
    return Mat(None, rows, cols, kind, lead, shape=shape if shape is not None else (rows, cols), dtype=dtype)


def _rt(arr, tr, width=None, cb=0):
    width = arr.shape[1] if width is None else width
    return arr, pl.BlockSpec((tr, width), lambda i: (i, cb))


def _whole(arr):
    nd = arr.ndim
    return arr, pl.BlockSpec(arr.shape, lambda i: (0,) * nd)


def _rowwise(name, fn, n_steps, ins, outs, accs=(), deps=(), fill=None):
    n_in, n_out, n_acc, n_dep = len(ins), len(outs), len(accs), len(deps)

    def body(*refs):
        vals = fn(*[r[...] for r in refs[:n_in]])
        if not isinstance(vals, (tuple, list)):
            vals = (vals,)
        for ref, v in zip(refs[n_in + n_dep:n_in + n_dep + n_out], vals[:n_out]):
            ref[...] = v.astype(ref.dtype)
        if n_acc:
            acc_refs = refs[n_in + n_dep + n_out:]

            @pl.when(pl.program_id(0) == 0)
            def _():
                for ref in acc_refs:
                    ref[...] = jnp.zeros_like(ref)

            for ref, v in zip(acc_refs, vals[n_out:]):
                ref[...] += v

    acc_specs = [pl.BlockSpec(s.shape, lambda i, nd=len(s.shape): (0,) * nd) for s in accs]
    res = pl.pallas_call(
        body, name=name, grid=(n_steps,),
        in_specs=[s for _, s in ins] + [pl.BlockSpec(memory_space=pl.ANY) for _ in deps],
        out_specs=[s for _, s in outs] + acc_specs,
        out_shape=[o for o, _ in outs] + list(accs),
        input_output_aliases={} if fill is None else {n_in + fill[0]: fill[1]},
        compiler_params=_params(("arbitrary",) if n_acc else ("parallel",)),
    )(*[a for a, _ in ins], *deps)
    return res


def _rt_out(t, width, dtype, tr):
    return jax.ShapeDtypeStruct((t, width), dtype), pl.BlockSpec((tr, width), lambda i: (i, 0))


def _rms(x, g):
    r = lax.rsqrt(jnp.mean(x * x, axis=-1, keepdims=True) + EPS)
    return x * r * g


def _rms_bwd(dy, x, g):
    r = lax.rsqrt(jnp.mean(x * x, axis=-1, keepdims=True) + EPS)
    xh = x * r
    dxh = dy * g
    dx = r * (dxh - xh * jnp.mean(dxh * xh, axis=-1, keepdims=True))
    dg = jnp.sum(dy * xh, axis=0, keepdims=True)
    return dx, dg


def _gelu(x):
    k = 0.7978845608028654
    th = jnp.tanh(k * (x + 0.044715 * (x * x * x)))
    return x * (0.5 * (1.0 + th))


def _gelu_grad(x):
    k = 0.7978845608028654
    x2 = x * x
    th = jnp.tanh(k * (x + 0.044715 * (x2 * x)))
    return 0.5 * (1.0 + th) + 0.5 * x * (1.0 - th * th) * (k * (1.0 + 3.0 * 0.044715 * x2))


def _norm_fwd(name, x, g, tr):
    t, d = x.shape
    return _rowwise(name, lambda xv, gv: _rms(xv, gv), t // tr, [_rt(x, tr), _whole(g)], [_rt_out(t, d, BF16, tr)])[0]


def _norm_bwd(name, dh, x, g, dres, tr):
    t, d = x.shape

    def fn(dhv, xv, gv, drv):
        dx, dg = _rms_bwd(dhv, xv, gv)
        dx = dx + drv
        return dx, dx, dg

    return _rowwise(name, fn, t // tr, [_rt(dh, tr), _rt(x, tr), _whole(g), _rt(dres, tr)],
                    [_rt_out(t, d, F32, tr), _rt_out(t, d, BF16, tr)], [jax.ShapeDtypeStruct((1, d), F32)])


def _rope_tables(posf, invf, cmask, smask, tr):
    t = posf.shape[0]

    def fn(p, f, cm, sm):
        ang = p * f
        return jnp.cos(ang) * cm, jnp.sin(ang) * sm

    return _rowwise("rope_tables", fn, t // tr, [_rt(posf, tr), _whole(invf), _whole(cmask), _whole(smask)],
                    [_rt_out(t, LANES, F32, tr), _rt_out(t, LANES, F32, tr)])


def _rot(v, c, s):
    return v * c + pltpu.roll(v, ROPE, axis=1) * s


def _rot_bwd(dv, c, s):
    return dv * c + pltpu.roll(dv * s, ROPE, axis=1)


def _rope_fwd(qfull, proj, kr_cb, ctab, stab, heads, tr):
    t = qfull.shape[0]
    hw = heads * LANES

    def fn(q, kr, c, s):
        parts = [q[:, :hw]] + [_rot(q[:, hw + h * LANES: hw + (h + 1) * LANES], c, s) for h in range(heads)]
        return jnp.concatenate(parts, axis=1), _rot(kr, c, s)

    return _rowwise("rope_fwd", fn, t // tr, [_rt(qfull, tr), _rt(proj, tr, LANES, kr_cb), _rt(ctab, tr), _rt(stab, tr)],
                    [_rt_out(t, 2 * hw, BF16, tr), _rt_out(t, LANES, BF16, tr)])


def _rope_bwd(dq1, dq2, dkr_h, ctab, stab, heads, tr, dproj, kr_cb):
    t = dq1.shape[0]
    hw = heads * LANES

    def fn(a, b, dk, c, s):
        parts = [a] + [_rot_bwd(b[:, h * LANES:(h + 1) * LANES], c, s) for h in range(heads)]
        dks = dk[0]
        for h in range(1, heads):
            dks = dks + dk[h]
        return jnp.concatenate(parts, axis=1), _rot_bwd(dks, c, s)

    dk_spec = pl.BlockSpec((heads, tr, LANES), lambda i: (0, i, 0))
    into = (jax.ShapeDtypeStruct(dproj.shape, dproj.dtype), pl.BlockSpec((tr, LANES), lambda i: (i, kr_cb)))
    return _rowwise("rope_bwd", fn, t // tr, [_rt(dq1, tr), _rt(dq2, tr), (dkr_h, dk_spec), _rt(ctab, tr), _rt(stab, tr)],
                    [_rt_out(t, 2 * hw, BF16, tr), into], deps=(dproj,), fill=(0, 1))


def _dot_nt(a, b):
    return lax.dot_general(a, b, (((1,), (1,)), ((), ())), preferred_element_type=F32)


def _dot_tn(a, b):
    return lax.dot_general(a, b, (((0,), (0,)), ((), ())), preferred_element_type=F32)


def _dot(a, b):
    return jnp.dot(a, b, preferred_element_type=F32)


def _ranges(n_blocks):
    n_var = min(4, n_blocks)
    assert n_blocks % n_var == 0
    return n_var, n_blocks // n_var


def _row_of(col):
    return col.T[:8, :]


def _attn_fwd(qall, kvall, kr, heads, scale, tq):
    t = qall.shape[0]
    nq = t // tq
    n_var, per = _ranges(nq)

    def body(qn_ref, qr_ref, kn_ref, v_ref, kr_ref, o_ref, lser_ref):
        i = pl.program_id(1)
        for var in range(n_var):
            kv = (var + 1) * per * tq

            @pl.when(jnp.logical_and(i >= var * per, i < (var + 1) * per))
            def _(kv=kv):
                s = (_dot_nt(qn_ref[...], kn_ref[:kv, :]) + _dot_nt(qr_ref[...], kr_ref[:kv, :])) * scale
                rows = i * tq + lax.broadcasted_iota(jnp.int32, (tq, kv), 0)
                cols = lax.broadcasted_iota(jnp.int32, (tq, kv), 1)
                s = jnp.where(cols <= rows, s, NEG)
                m = jnp.max(s, axis=-1, keepdims=True)
                p = jnp.exp(s - m)
                l = jnp.sum(p, axis=-1, keepdims=True)
                o_ref[...] = _dot(p.astype(BF16), v_ref[:kv, :]) / l
                lser_ref[...] = _row_of(jnp.broadcast_to(m + jnp.log(l), (tq, LANES)))

    return pl.pallas_call(
        body, name="attn_fwd", grid=(heads, nq),
        in_specs=[pl.BlockSpec((tq, LANES), lambda h, i: (i, h)),
                  pl.BlockSpec((tq, LANES), lambda h, i: (i, heads + h)),
                  pl.BlockSpec((t, LANES), lambda h, i: (0, h)),
                  pl.BlockSpec((t, LANES), lambda h, i: (0, heads + h)),
                  pl.BlockSpec((t, LANES), lambda h, i: (0, 0))],
        out_specs=[pl.BlockSpec((tq, LANES), lambda h, i: (i, h)),
                   pl.BlockSpec((None, 8, tq), lambda h, i: (h, 0, i))],
        out_shape=[jax.ShapeDtypeStruct((t, heads * LANES), F32), jax.ShapeDtypeStruct((heads, 8, t), F32)],
        compiler_params=_params(("parallel", "parallel")),
    )(qall, qall, kvall, kvall, kr)


def _attn_bwd(qall, kvall, kr, do, lse_row, delta_row, heads, scale, tk):
    t = qall.shape[0]
    nk = t // tk
    n_var, per = _ranges(nk)

    def body(qn_ref, qr_ref, kn_ref, v_ref, kr_ref, do_ref, lse_ref, dl_ref, dq1_ref, dq2_ref, dk_ref, dv_ref, dkr_ref):
        j = pl.program_id(1)

        @pl.when(j == 0)
        def _():
            dq1_ref[...] = jnp.zeros_like(dq1_ref)
            dq2_ref[...] = jnp.zeros_like(dq2_ref)

        for var in range(n_var):
            q0 = var * per * tk
            nq = t - q0

            @pl.when(jnp.logical_and(j >= var * per, j < (var + 1) * per))
            def _(q0=q0, nq=nq):
                qn, qr, do_v = qn_ref[q0:, :], qr_ref[q0:, :], do_ref[q0:, :]
                k1, k2 = kn_ref[...], kr_ref[...]
                st = (_dot_nt(k1, qn) + _dot_nt(k2, qr)) * scale
                keys = j * tk + lax.broadcasted_iota(jnp.int32, (tk, nq), 0)
                queries = q0 + lax.broadcasted_iota(jnp.int32, (tk, nq), 1)
                pt = jnp.where(keys <= queries, jnp.exp(st - lse_ref[0:1, q0:]), 0.0)
                dpt = _dot_nt(v_ref[...], do_v)
                dst = (pt * (dpt - dl_ref[0:1, q0:]) * scale).astype(BF16)
                dv_ref[...] = _dot(pt.astype(BF16), do_v).astype(dv_ref.dtype)
                dk_ref[...] = _dot(dst, qn).astype(dk_ref.dtype)
                dkr_ref[...] = _dot(dst, qr)
                dq1_ref[q0:, :] += _dot_tn(dst, k1)
                dq2_ref[q0:, :] += _dot_tn(dst, k2)

    kblk = lambda off: pl.BlockSpec((tk, LANES), lambda h, j: (j, off + h))
    full = lambda off: pl.BlockSpec((t, LANES), lambda h, j: (0, off + h))
    stat = pl.BlockSpec((None, 8, t), lambda h, j: (h, 0, 0))
    return pl.pallas_call(
        body, name="attn_bwd", grid=(heads, nk),
        in_specs=[full(0), full(heads), kblk(0), kblk(heads), pl.BlockSpec((tk, LANES), lambda h, j: (j, 0)),
                  full(0), stat, stat],
        out_specs=[full(0), full(0), kblk(0), kblk(0), pl.BlockSpec((None, tk, LANES), lambda h, j: (h, j, 0))],
        out_shape=[jax.ShapeDtypeStruct((t, heads * LANES), F32)] * 2 + [jax.ShapeDtypeStruct((t, heads * LANES), BF16)] * 2
        + [jax.ShapeDtypeStruct((heads, t, LANES), F32)],
        compiler_params=_params(("parallel", "arbitrary")),
    )(qall, qall, kvall, kvall, kr, do, lse_row, delta_row)


def _tril():
    return lax.broadcasted_iota(jnp.int32, (LANES, LANES), 0) >= lax.broadcasted_iota(jnp.int32, (LANES, LANES), 1)


def _group_norm(vg):
    mu = jnp.mean(vg, axis=-1, keepdims=True)
    vc = vg - mu
    rs = lax.rsqrt(jnp.mean(vc * vc, axis=-1, keepdims=True) + EPS)
    return vc * rs, rs


def _sgu_fwd(proj, gain, w, bias, groups, rb):
    t = proj.shape[0]
    gw = groups * LANES
    cpb = rb // LANES

    def body(u_ref, v_ref, gain_ref, w_ref, b_ref, s_ref):
        tril = _tril()
        for g in range(groups):
            wt = jnp.where(tril, w_ref[g], 0.0).astype(BF16)
            cols = slice(g * LANES, (g + 1) * LANES)
            for ci in range(cpb):
                rows = slice(ci * LANES, (ci + 1) * LANES)
                ug = _gelu(u_ref[rows, cols])
                vh, _ = _group_norm(_gelu(v_ref[rows, cols]))
                vn = vh * gain_ref[:, cols]
                y = _dot(wt, vn.astype(BF16)) + b_ref[g]
                s_ref[rows, cols] = ug * y

    return pl.pallas_call(
        body, name="sgu_fwd", grid=(t // rb,),
        in_specs=[pl.BlockSpec((rb, gw), lambda i: (i, 0)), pl.BlockSpec((rb, gw), lambda i: (i, 1)),
                  pl.BlockSpec((1, gw), lambda i: (0, 0)),
                  pl.BlockSpec((groups, LANES, LANES), lambda i: (0, 0, 0)),
                  pl.BlockSpec((groups, LANES, LANES), lambda i: (0, 0, 0))],
        out_specs=pl.BlockSpec((rb, gw), lambda i: (i, 0)),
        out_shape=jax.ShapeDtypeStruct((t, gw), F32),
        compiler_params=_params(("parallel",)),
    )(proj, proj, gain, w, bias)


def _sgu_bwd(proj, ds, gain, w, bias, groups, rb):
    t, width = proj.shape
    gw = groups * LANES
    cpb = rb // LANES
    n_steps = t // rb

    def body(u_ref, v_ref, ds_ref, gain_ref, w_ref, b_ref, dp_ref, dw_ref, db_ref, dg_ref, dy_acc):
        du_ref, dv_ref = dp_ref.at[:, :gw], dp_ref.at[:, gw:]
        step = pl.program_id(0)

        @pl.when(step == 0)
        def _():
            dw_ref[...] = jnp.zeros_like(dw_ref)
            dy_acc[...] = jnp.zeros_like(dy_acc)
            dg_ref[...] = jnp.zeros_like(dg_ref)

        tril = _tril()
        for g in range(groups):
            wt = jnp.where(tril, w_ref[g], 0.0).astype(BF16)
            cols = slice(g * LANES, (g + 1) * LANES)
            gain_g = gain_ref[:, cols]
            for ci in range(cpb):
                rows = slice(ci * LANES, (ci + 1) * LANES)
                u_raw, v_raw, ds_v = u_ref[rows, cols], v_ref[rows, cols], ds_ref[rows, cols]
                ug = _gelu(u_raw)
                vh, rs = _group_norm(_gelu(v_raw))
                vn = (vh * gain_g).astype(BF16)
                y = _dot(wt, vn) + b_ref[g]
                dy = ds_v * ug
                dyb = dy.astype(BF16)
                du_ref[rows, cols] = (ds_v * y * _gelu_grad(u_raw)).astype(du_ref.dtype)
                dy_acc[g] += dy
                dw_ref[g] += _dot_nt(dyb, vn)
                dvn = _dot_tn(wt, dyb)
                dg_ref[:, cols] += jnp.sum(dvn * vh, axis=0, keepdims=True)
                dvh = dvn * gain_g
                dvg = rs * (dvh - jnp.mean(dvh, axis=-1, keepdims=True)
                            - vh * jnp.mean(dvh * vh, axis=-1, keepdims=True))
                dv_ref[rows, cols] = (dvg * _gelu_grad(v_raw)).astype(dv_ref.dtype)

        @pl.when(step == n_steps - 1)
        def _():
            ones = jnp.ones((8, LANES), F32)
            for g in range(groups):
                dw_ref[g] = jnp.where(tril, dw_ref[g], 0.0)
                db_ref[g] = lax.dot_general(ones, dy_acc[g], (((1,), (1,)), ((), ())),
                                            precision=lax.Precision.HIGHEST, preferred_element_type=F32)

    blk = lambda cb: pl.BlockSpec((rb, gw), lambda i: (i, cb))
    whole3 = pl.BlockSpec((groups, LANES, LANES), lambda i: (0, 0, 0))
    return pl.pallas_call(
        body, name="sgu_bwd", grid=(n_steps,),
        in_specs=[blk(0), blk(1), blk(0), pl.BlockSpec((1, gw), lambda i: (0, 0)), whole3, whole3],
        out_specs=[pl.BlockSpec((rb, 2 * gw), lambda i: (i, 0)), whole3,
                   pl.BlockSpec((groups, 8, LANES), lambda i: (0, 0, 0)), pl.BlockSpec((1, gw), lambda i: (0, 0))],
        out_shape=[jax.ShapeDtypeStruct((t, width), BF16),
                   jax.ShapeDtypeStruct((groups, LANES, LANES), F32), jax.ShapeDtypeStruct((groups, 8, LANES), F32),
                   jax.ShapeDtypeStruct((1, gw), F32)],
        scratch_shapes=[pltpu.VMEM((groups, LANES, LANES), F32)],
        compiler_params=_params(("arbitrary",)),
    )(proj, proj, ds, gain, w, bias)


def _shift_down(z, s):
    rows = lax.broadcasted_iota(jnp.int32, z.shape, 0)
    return jnp.where(rows >= s, pltpu.roll(z, s, axis=0), 0.0)


def _shift_up(z, s):
    n = z.shape[0]
    rows = lax.broadcasted_iota(jnp.int32, z.shape, 0)
    return jnp.where(rows < n - s, pltpu.roll(z, n - s, axis=0), 0.0)


def _conv_fwd(proj3, cw, tc):
    _, t, cd = proj3.shape

    def body(p_ref, w_ref, o_ref):
        z = p_ref[1] * p_ref[2]
        w = w_ref[...]
        zc = w[2:3] * z + w[1:2] * _shift_down(z, 1) + w[0:1] * _shift_down(z, 2)
        o_ref[...] = (p_ref[0] * zc).astype(o_ref.dtype)

    return pl.pallas_call(
        body, name="conv_fwd", grid=(cd // tc,),
        in_specs=[pl.BlockSpec((3, t, tc), lambda j: (0, 0, j)), pl.BlockSpec((8, tc), lambda j: (0, j))],
        out_specs=pl.BlockSpec((t, tc), lambda j: (0, j)),
        out_shape=jax.ShapeDtypeStruct((t, cd), BF16),
        compiler_params=_params(("parallel",)),
    )(proj3, cw)


def _conv_bwd(proj3, cw, dbz, tc):
    _, t, cd = proj3.shape

    def body(p_ref, w_ref, d_ref, o_ref, dw_ref):
        b, c, xin = p_ref[0], p_ref[1], p_ref[2]
        w = w_ref[...]
        z = c * xin
        z1, z2 = _shift_down(z, 1), _shift_down(z, 2)
        zc = w[2:3] * z + w[1:2] * z1 + w[0:1] * z2
        d = d_ref[...]
        dzc = d * b
        dz = w[2:3] * dzc + w[1:2] * _shift_up(dzc, 1) + w[0:1] * _shift_up(dzc, 2)
        o_ref[0] = (d * zc).astype(o_ref.dtype)
        o_ref[1] = (dz * xin).astype(o_ref.dtype)
        o_ref[2] = (dz * c).astype(o_ref.dtype)
        row = lax.broadcasted_iota(jnp.int32, (8, tc), 0)
        dw0 = jnp.sum(dzc * z2, axis=0, keepdims=True)
        dw1 = jnp.sum(dzc * z1, axis=0, keepdims=True)
        dw2 = jnp.sum(dzc * z, axis=0, keepdims=True)
        dw_ref[...] = jnp.where(row == 0, dw0, 0.0) + jnp.where(row == 1, dw1, 0.0) + jnp.where(row == 2, dw2, 0.0)

    return pl.pallas_call(
        body, name="conv_bwd", grid=(cd // tc,),
        in_specs=[pl.BlockSpec((3, t, tc), lambda j: (0, 0, j)), pl.BlockSpec((8, tc), lambda j: (0, j)),
                  pl.BlockSpec((t, tc), lambda j: (0, j))],
        out_specs=[pl.BlockSpec((3, t, tc), lambda j: (0, 0, j)), pl.BlockSpec((8, tc), lambda j: (0, j))],
        out_shape=[jax.ShapeDtypeStruct((3, t, cd), BF16), jax.ShapeDtypeStruct((8, cd), F32)],
        compiler_params=_params(("parallel",)),
    )(proj3, cw, dbz)


def _place():
    x, y, c = lax.axis_index("x"), lax.axis_index("y"), lax.axis_index("c")
    chips = [(1 - x, y), (x, 1 - y), (1 - x, 1 - y)]
    return x, y, c, chips


def _any_specs(n):
    return [pl.BlockSpec(memory_space=pl.ANY) for _ in range(n)]


HBM_SPEC = pl.BlockSpec(memory_space=pltpu.HBM)
SEM_SPEC = pl.BlockSpec(memory_space=pltpu.SEMAPHORE)
ORDERED_EFFECT = pltpu.SideEffectType.DATAFLOW_SIDE_EFFECTING


def _in_hbm(a):
    return pltpu.with_memory_space_constraint(a, pltpu.HBM)


def _token():
    return jax.ShapeDtypeStruct((8, LANES), F32), pl.BlockSpec(memory_space=pltpu.VMEM)


def _gather_start(name, groups):
    sizes = [len(g) for g in groups]
    flat = [b for g in groups for b in g]
    n, ng = len(flat), len(groups)

    def body(*refs):
        ins, sems, token = refs[:n], refs[n:n + 2 * ng], refs[-1]
        x, y, c, chips = _place()
        me = 2 * x + y
        i = 0
        for gi, size in enumerate(sizes):
            for j in range(size):
                blk = ins[i].at[me, c]
                for k, chip in enumerate(chips):
                    pltpu.make_async_remote_copy(src_ref=blk, dst_ref=blk, send_sem=sems[2 * gi].at[3 * j + k],
                                                 recv_sem=sems[2 * gi + 1].at[3 * j + k],
                                                 device_id=(*chip, c), device_id_type=MESH).start()
                i += 1
        token[...] = jnp.zeros_like(token)

    tok_shape, tok_spec = _token()
    res = pl.pallas_call(
        body, name=name,
        in_specs=[HBM_SPEC] * n,
        out_specs=[SEM_SPEC] * (2 * ng) + [HBM_SPEC] * n + [tok_spec],
        out_shape=[pltpu.SemaphoreType.DMA((3 * size,)) for size in sizes for _ in (0, 1)]
        + [pltpu.HBM(b.shape, b.dtype) for b in flat] + [tok_shape],
        input_output_aliases={i: 2 * ng + i for i in range(n)},
        compiler_params=pltpu.CompilerParams(has_side_effects=ORDERED_EFFECT),
    )(*[_in_hbm(b) for b in flat])
    out, i = [], 2 * ng
    for gi, size in enumerate(sizes):
        out.append((res[2 * gi], res[2 * gi + 1], list(res[i:i + size])))
        i += size
    return out, res[-1]


def _gather_wait(tag, send, recv, bufs, after):
    n = len(bufs)
    after = tuple(after) if isinstance(after, (tuple, list)) else (after,)

    def body(*refs):
        ins, send_ref, recv_ref = refs[:n], refs[n], refs[n + 1]
        x, y, c, chips = _place()
        me = 2 * x + y
        for j in range(n):
            for k, (px, py) in enumerate(chips):
                cp = pltpu.make_async_remote_copy(src_ref=ins[j].at[me, c], dst_ref=ins[j].at[2 * px + py, c],
                                                  send_sem=send_ref.at[3 * j + k], recv_sem=recv_ref.at[3 * j + k],
                                                  device_id=(px, py, c), device_id_type=MESH)
                cp.wait_send()
                cp.wait_recv()

    return pl.pallas_call(
        body, name="gather_wait_" + tag,
        in_specs=[HBM_SPEC] * n + [SEM_SPEC, SEM_SPEC] + _any_specs(len(after)),
        out_specs=[HBM_SPEC] * n,
        out_shape=[pltpu.HBM(b.shape, b.dtype) for b in bufs],
        input_output_aliases={i: i for i in range(n)},
        compiler_params=pltpu.CompilerParams(has_side_effects=ORDERED_EFFECT),
    )(*bufs, send, recv, *after)


def _gather_forward(tag, bufs):
    n = len(bufs)

    def body(*refs):
        ins, outs = refs[:n], refs[n:2 * n]
        send, recv = refs[2 * n:]
        x, y, c, chips = _place()
        sib = (x, y, 1 - c)

        def cp(i, k, slot, half):
            return pltpu.make_async_remote_copy(src_ref=ins[i].at[slot, half], dst_ref=outs[i].at[slot, half],
                                                send_sem=send.at[3 * i + k], recv_sem=recv.at[3 * i + k],
                                                device_id=sib, device_id_type=MESH)

        cps = [cp(i, k, 2 * px + py, c) for i in range(n) for k, (px, py) in enumerate(chips)]
        for d in cps:
            d.start()
        for i in range(n):
            for k, (px, py) in enumerate(chips):
                cp(i, k, 2 * px + py, 1 - c).wait_recv()
        for d in cps:
            d.wait_send()

    return pl.pallas_call(
        body, name="gather_forward_" + tag,
        in_specs=_any_specs(n), out_specs=_any_specs(n),
        out_shape=[jax.ShapeDtypeStruct(b.shape, b.dtype) for b in bufs],
        scratch_shapes=[pltpu.SemaphoreType.DMA((3 * n,))] * 2,
        input_output_aliases={i: i for i in range(n)},
        compiler_params=pltpu.CompilerParams(has_side_effects=True),
    )(*bufs)


def _pair_route(srcs, zones):
    x, y, c, _ = _place()
    return [(srcs[i].at[j, 1 - c], zones[i].at[j], (x, y, 1 - c)) for i in range(len(srcs)) for j in range(N_CHIPS)]


def _chip_route(srcs, zones):
    x, y, c, chips = _place()
    return [(srcs[i].at[2 * px + py], zones[i].at[k], (px, py, c)) for i in range(len(srcs)) for k, (px, py) in enumerate(chips)]


def _all_route(srcs, zones):
    x, y, c, _ = _place()
    flips = [(fx, fy, fc) for fx in (0, 1) for fy in (0, 1) for fc in (0, 1)][1:]
    return [(srcs[0], zones[0].at[4 * x + 2 * y + c], (x + fx - 2 * x * fx, y + fy - 2 * y * fy, c + fc - 2 * c * fc))
            for fx, fy, fc in flips]


def _share_route(srcs, zones):
    x, y, c, _ = _place()
    return [(s.at[c], s.at[c], (x, y, 1 - c)) for s in srcs]


def _exchange_start(name, route, n_copies, srcs, zones):
    n, nz = len(srcs), len(zones)
    lands = [lax.empty(z, a.dtype) if isinstance(z, tuple) else z for z, a in zip(zones, srcs)]

    def body(*refs):
        ins, zone_refs, send, recv, token = refs[:n], refs[n:n + nz], refs[n + nz], refs[n + nz + 1], refs[-1]
        for k, (src, dst, dev) in enumerate(route(ins, zone_refs)):
            pltpu.make_async_remote_copy(src_ref=src, dst_ref=dst, send_sem=send.at[k], recv_sem=recv.at[k],
                                         device_id=dev, device_id_type=MESH).start()
        token[...] = jnp.zeros_like(token)

    tok_shape, tok_spec = _token()
    res = pl.pallas_call(
        body, name=name,
        in_specs=[HBM_SPEC] * (n + nz),
        out_specs=[SEM_SPEC, SEM_SPEC] + [HBM_SPEC] * (n + nz) + [tok_spec],
        out_shape=[pltpu.SemaphoreType.DMA((n_copies,))] * 2 + [pltpu.HBM(a.shape, a.dtype) for a in srcs + lands]
        + [tok_shape],
        input_output_aliases={i: 2 + i for i in range(n + nz)},
        compiler_params=pltpu.CompilerParams(has_side_effects=ORDERED_EFFECT),
    )(*[_in_hbm(a) for a in srcs + lands])
    return (res[0], res[1], list(res[2:2 + n]), list(res[2 + n:2 + n + nz])), res[-1]


def _exchange_wait(name, route, started, after):
    send, recv, srcs, lands = started
    n, nz = len(srcs), len(lands)
    after = tuple(after) if isinstance(after, (tuple, list)) else (after,)

    def body(*refs):
        ins, zone_refs, send_ref, recv_ref = refs[:n], refs[n:n + nz], refs[n + nz], refs[n + nz + 1]
        for k, (src, dst, dev) in enumerate(route(ins, zone_refs)):
            cp = pltpu.make_async_remote_copy(src_ref=src, dst_ref=dst, send_sem=send_ref.at[k], recv_sem=recv_ref.at[k],
                                              device_id=dev, device_id_type=MESH)
            cp.wait_send()
            cp.wait_recv()

    res = pl.pallas_call(
        body, name=name,
        in_specs=[HBM_SPEC] * (n + nz) + [SEM_SPEC, SEM_SPEC] + _any_specs(len(after)),
        out_specs=[HBM_SPEC] * (n + nz),
        out_shape=[pltpu.HBM(a.shape, a.dtype) for a in srcs + lands],
        input_output_aliases={i: i for i in range(n + nz)},
        compiler_params=pltpu.CompilerParams(has_side_effects=ORDERED_EFFECT),
    )(*srcs, *lands, send, recv, *after)
    return list(res[:n]), list(res[n:])


def _spread(v):
    rows, cols = v.shape
    tr = _row_tile(rows, cols, budget=256 * 1024)

    def body(v_ref, o_ref):
        o_ref[...] = jnp.broadcast_to(v_ref[...][None], o_ref.shape)

    return pl.pallas_call(body, name="spread_small_grads", grid=(rows // tr,),
                          in_specs=[pl.BlockSpec((tr, cols), lambda r: (r, 0))],
                          out_specs=pl.BlockSpec((8, tr, cols), lambda r: (0, r, 0)),
                          out_shape=jax.ShapeDtypeStruct((8, rows, cols), v.dtype),
                          compiler_params=_params(("parallel",)))(v)


def _row_tile(rows, cols, itemsize=4, budget=2 * 1024 * 1024, step=8):
    best = None
    for t in range(step, rows + 1, step):
        if rows % t == 0 and t * cols * itemsize <= budget:
            best = t
    return best if best is not None else rows


def _my_chip():
    return 2 * lax.axis_index("x") + lax.axis_index("y")


def _pair_sum(g5, gsib):
    _, _, rh, cols = g5.shape
    tr = _row_tile(rh, cols, step=16)

    def body(a_ref, b_ref, o_ref):
        o_ref[...] = (a_ref[...].astype(F32) + b_ref[...].astype(F32)).astype(o_ref.dtype)

    return pl.pallas_call(body, name="grad_pair_sum", grid=(N_CHIPS, rh // tr),
                          in_specs=[pl.BlockSpec((None, None, tr, cols), lambda j, r: (j, lax.axis_index("c"), r, 0)),
                                    pl.BlockSpec((None, tr, cols), lambda j, r: (j, r, 0))],
                          out_specs=pl.BlockSpec((None, tr, cols), lambda j, r: (j, r, 0)),
                          out_shape=jax.ShapeDtypeStruct((N_CHIPS, rh, cols), BF16),
                          compiler_params=_params(("parallel", "parallel")))(g5, gsib)


def _chip_sum(part, recv):
    _, rh, cols = part.shape
    tr = _row_tile(rh, cols, step=16)

    def body(a_ref, b_ref, o_ref):
        acc = a_ref[...].astype(F32)
        for k in range(3):
            acc = acc + b_ref[k].astype(F32)
        o_ref[...] = acc

    return pl.pallas_call(body, name="grad_chip_sum", grid=(rh // tr,),
                          in_specs=[pl.BlockSpec((None, tr, cols), lambda r: (_my_chip(), r, 0)),
                                    pl.BlockSpec((3, tr, cols), lambda r: (0, r, 0))],
                          out_specs=pl.BlockSpec((None, tr, cols), lambda r: (lax.axis_index("c"), r, 0)),
                          out_shape=jax.ShapeDtypeStruct((2, rh, cols), F32),
                          compiler_params=_params(("parallel",)))(part, recv)


def _sum_devices(g):
    _, rows, cols = g.shape
    tr = _row_tile(rows, cols, budget=256 * 1024)

    def body(g_ref, o_ref):
        acc = g_ref[0]
        for d in range(1, 8):
            acc = acc + g_ref[d]
        o_ref[...] = acc

    return pl.pallas_call(body, name="sum_small_grads", grid=(rows // tr,),
                          in_specs=[pl.BlockSpec((8, tr, cols), lambda r: (0, r, 0))],
                          out_specs=pl.BlockSpec((tr, cols), lambda r: (r, 0)),
                          out_shape=jax.ShapeDtypeStruct((rows, cols), F32),
                          compiler_params=_params(("parallel",)))(g)


def _place_shard(w, layer, dtype, deps=()):
    _, rows, cols = w.shape
    tr = _row_tile(rows, cols)

    def body(i_ref, *rest):
        o_ref = rest[-1]
        o_ref[...] = i_ref[...].astype(o_ref.dtype)

    out = pl.pallas_call(body, name="place_shard", grid=(rows // tr,),
                         in_specs=[pl.BlockSpec((None, tr, cols), lambda r: (layer, r, 0))] + _any_specs(len(deps)),
                         out_specs=pl.BlockSpec((None, tr, cols), lambda r: (_my_chip(), r, 0)),
                         out_shape=jax.ShapeDtypeStruct((N_CHIPS, rows, cols), dtype),
                         compiler_params=_params(("parallel",)))(w, *deps)
    return out.reshape(N_CHIPS, 2, rows // 2, cols)


def _adamw(w, gs, m, v):
    n_layers, rows, cols = w.shape
    tr = _row_tile(rows, cols)

    def body(w_ref, m_ref, v_ref, *rest):
        g_refs = rest[:n_layers]
        go_ref, d_ref, mo_ref, vo_ref = rest[n_layers:]
        gv = g_refs[0][...]
        for layer in range(1, n_layers):
            gv = jnp.where(pl.program_id(0) == layer, g_refs[layer][...], gv)
        d_ref[...], mo_ref[...], vo_ref[...] = _adamw_math(w_ref[...], gv, m_ref[...], v_ref[...])
        go_ref[...] = gv

    spec = pl.BlockSpec((None, tr, cols), lambda layer, r: (layer, r, 0))
    g_specs = [pl.BlockSpec((tr, cols), lambda layer, r, own=own: (jnp.where(layer == own, r, 0), 0))
               for own in range(n_layers)]
    return pl.pallas_call(body, name="adamw", grid=(n_layers, rows // tr), in_specs=[spec] * 3 + g_specs,
                          out_specs=[spec] * 4, out_shape=[jax.ShapeDtypeStruct((n_layers, rows, cols), F32)] * 4,
                          compiler_params=_params(("parallel", "parallel")))(w, m, v, *gs)


def _pad_rope(w):
    z = jnp.zeros(w.shape[:-1] + (ROPE_HALF,), w.dtype)
    return jnp.concatenate([w[..., :ROPE_HALF], z, w[..., ROPE_HALF:], z], axis=-1)


def _unpad_rope(g):
    return jnp.concatenate([g[..., :ROPE_HALF], g[..., ROPE:ROPE + ROPE_HALF]], axis=-1)


def _unstack_cols(s):
    n, r, cs = s.shape
    return jnp.transpose(s, (1, 0, 2)).reshape(r, n * cs)


def _stack_cols(f):
    r, cfull = f.shape
    return jnp.transpose(f.reshape(r, N_CHIPS, cfull // N_CHIPS), (1, 0, 2))


def _small_shard(norm, conv):
    return jnp.concatenate([jnp.pad(norm, ((0, 15), (0, 0))), jnp.pad(conv, ((0, 13), (0, 0)))], axis=0)


def _flat_rows(a):
    return a.reshape(-1, LANES)


def _pack_small(arrs):
    return jnp.concatenate([_flat_rows(a.astype(F32)) for a in arrs], axis=0)


def _unpack_small(flat, like):
    out, r = [], 0
    for a in like:
        n = a.size // LANES
        out.append(flat[r:r + n].reshape(a.shape))
        r += n
    return out


def kernel(x, positions, e_norm_mix, e_w_in, e_q_norm, e_w_uq, e_kv_norm, e_w_ukv, e_v_norm, e_sgu_w, e_sgu_b, e_mla_out_norm, e_sgu_out_norm, e_w_out, o_norm_mix, o_w_in, o_conv_w, o_w_out, mlp_norm, mlp_w1, mlp_w2, final_norm, loss_target, m_e_norm_mix, m_e_w_in, m_e_q_norm, m_e_w_uq, m_e_kv_norm, m_e_w_ukv, m_e_v_norm, m_e_sgu_w, m_e_sgu_b, m_e_mla_out_norm, m_e_sgu_out_norm, m_e_w_out, m_o_norm_mix, m_o_w_in, m_o_conv_w, m_o_w_out, m_mlp_norm, m_mlp_w1, m_mlp_w2, m_final_norm, v_e_norm_mix, v_e_w_in, v_e_q_norm, v_e_w_uq, v_e_kv_norm, v_e_w_ukv, v_e_v_norm, v_e_sgu_w, v_e_sgu_b, v_e_mla_out_norm, v_e_sgu_out_norm, v_e_w_out, v_o_norm_mix, v_o_w_in, v_o_conv_w, v_o_w_out, v_mlp_norm, v_mlp_w1, v_mlp_w2, v_final_norm):
    t, d = x.shape[1], x.shape[2]
    ql, kvl = e_q_norm.shape[1], e_kv_norm.shape[1]
    groups = e_v_norm.shape[1]
    gw = groups * LANES
    heads = N_CHIPS * e_w_uq.shape[2] // (LANES + ROPE)
    hw = heads * LANES
    mix = hw + gw
    ei = N_CHIPS * e_w_in.shape[2]
    cd = N_CHIPS * o_conv_w.shape[2]
    ff = N_CHIPS * mlp_w1.shape[2]
    ffs = ff // N_CHIPS
    pi = 2 * gw + ql + kvl + LANES
    assert e_norm_mix.shape[0] == 1 and o_norm_mix.shape[0] == 1 and mlp_norm.shape[0] == 2
    assert ei == ql + kvl + ROPE + 2 * gw and cd == d and e_sgu_w.shape[2] == LANES
    assert (2 * gw) % ql == 0 and (2 * gw + ql) % kvl == 0 and t % LANES == 0
    scale = (LANES + ROPE) ** -0.5

    tr = min(256, t)
    tm = _pick(t, 1024, 8)
    kt, kd = _pick(t, 2048, 8), _pick(d, 2048)
    xs = x.reshape(t, d)
    tgt = loss_target.reshape(t, d)

    small_shard = _small_shard(o_norm_mix, o_conv_w[0])
    first, tok = _gather_start("gather_start_e", [
        [_place_shard(e_w_in, 0, BF16)],
        [_place_shard(e_w_uq, 0, BF16), _place_shard(e_w_ukv, 0, BF16), _place_shard(e_w_out, 0, BF16),
         _place_shard(small_shard[None], 0, F32)]])
    rest, tok = _gather_start("gather_start_rest", [
        [_place_shard(mlp_w1, 0, BF16, (tok,))], [_place_shard(mlp_w2, 0, BF16, (tok,))],
        [_place_shard(o_w_in, 0, BF16, (tok,)), _place_shard(o_w_out, 0, BF16, (tok,))],
        [_place_shard(mlp_w1, 1, BF16, (tok,))], [_place_shard(mlp_w2, 1, BF16, (tok,))]])
    started = first + rest

    def gathered(gi, tag, after):
        send, recv, bufs = started[gi]
        bufs = _gather_forward(tag, _gather_wait(tag, send, recv, bufs, after))
        return [b.reshape(N_CHIPS, 2 * b.shape[2], b.shape[3]) for b in bufs]

    g_e = e_norm_mix
    h0 = _norm_fwd("e_norm", xs, g_e, tr)
    inv_freq = ROPE_BASE ** (-jnp.arange(0, ROPE, 2, dtype=F32) / ROPE)
    zeros32 = jnp.zeros((ROPE_HALF,), F32)
    ones32 = jnp.ones((ROPE_HALF,), F32)
    invf = jnp.concatenate([inv_freq, zeros32, inv_freq, zeros32]).reshape(1, LANES)
    cmask = jnp.concatenate([ones32, zeros32, ones32, zeros32]).reshape(1, LANES)
    smask = jnp.concatenate([-ones32, zeros32, ones32, zeros32]).reshape(1, LANES)
    ctab, stab = _rope_tables(positions.reshape(t, 1).astype(F32), invf, cmask, smask, tr)

    w_in_g, = gathered(0, "e_in", (h0, ctab, tok))
    full = _unstack_cols(w_in_g)
    c2, c3 = ql + kvl, ql + kvl + ROPE
    w_in_all = jnp.concatenate([full[:, c3:], full[:, :c2], _pad_rope(full[:, c2:c3])], axis=1)
    proj, = _matmul("e_proj", Mat(h0, t, d), Mat(w_in_all, d, pi), "nn", [_out(t, pi, F32)], tm, _pick(pi, 1024), kd)

    w_uq_g, w_ukv_g, w_eout_g, small_g = gathered(1, "e", proj)
    full = _unstack_cols(w_uq_g).reshape(ql, heads, LANES + ROPE)
    w_q_all = jnp.concatenate([full[:, :, :LANES].reshape(ql, hw), _pad_rope(full[:, :, LANES:]).reshape(ql, hw)], axis=1)
    full = _unstack_cols(w_ukv_g).reshape(kvl, heads, 2 * LANES)
    w_kv_all = jnp.concatenate([full[:, :, :LANES].reshape(kvl, hw), full[:, :, LANES:].reshape(kvl, hw)], axis=1)
    w_eout = w_eout_g.reshape(mix, d)
    g_o = small_g[:, 0].reshape(1, d)
    conv_w = jnp.pad(jnp.transpose(small_g[:, 16:19], (1, 0, 2)).reshape(3, cd), ((0, 5), (0, 0)))

    g_q, g_kv = e_q_norm, e_kv_norm
    g_vn = e_v_norm.reshape(1, gw)
    sgu_w = e_sgu_w[0]
    sgu_b = jnp.broadcast_to(e_sgu_b[0][:, :, None], (groups, LANES, LANES))
    g_mla, g_sgu = e_mla_out_norm, e_sgu_out_norm
    g_m0, g_m1 = mlp_norm[0:1], mlp_norm[1:2]
    g_f = final_norm.reshape(1, d)

    def mlp_fwd(tag, xin, g, gi):
        hm = _norm_fwd("mlp_norm_" + tag, xin, g, tr)
        tn = _pick(ffs, 1024)
        w1 = Mat(gathered(gi, "w1_" + tag, hm)[0], d, ff, "colstack")
        a, act = _matmul("mlp_up_" + tag, Mat(hm, t, d), w1, "nn",
                         [_out(t, ff, BF16), _out(t, ff, BF16)], tm, tn, kd,
                         epilogue=lambda z: (jnp.maximum(z, 0.0), jnp.square(jnp.maximum(z, 0.0))))
        w2 = Mat(gathered(gi + 1, "w2_" + tag, act)[0].reshape(ff, d), ff, d)
        xo, = _matmul("mlp_down_" + tag, Mat(act, t, ff), w2, "nn",
                      [_out(t, d, F32)], tm, _pick(d, 1024), _pick(ffs, 2048),
                      epilogue=lambda z, r: (z + r,), extras=[Mat(xin, t, d)])
        return xo, hm, a, act, w1, w2

    def chip_start(tag, part):
        return _exchange_start("scatter_start_" + tag, _chip_route, 3 * len(part), part, [(3,) + p.shape[1:] for p in part])

    def pair_start(tag, stacked):
        g5 = [g.reshape(N_CHIPS, 2, g.shape[1] // 2, g.shape[2]) for g in stacked]
        return _exchange_start("pair_start_" + tag, _pair_route, N_CHIPS * len(g5), g5,
                               [(N_CHIPS,) + g.shape[2:] for g in g5])

    def pair_finish(tag, started, after):
        g5, from_sib = _exchange_wait("pair_wait_" + tag, _pair_route, started, after)
        return chip_start(tag, [_pair_sum(a, b) for a, b in zip(g5, from_sib)])

    def summed(tag, sc, after):
        part, lands = _exchange_wait("scatter_wait_" + tag, _chip_route, sc, after)
        half = [_chip_sum(p, r) for p, r in zip(part, lands)]
        return _exchange_start("share_start_" + tag, _share_route, len(half), half, [])

    def shared(tag, started, after):
        bufs, _ = _exchange_wait("share_wait_" + tag, _share_route, started, after)
        return [r.reshape(2 * r.shape[1], r.shape[2]) for r in bufs]

    def mlp_bwd(tag, dx, dxb, xin, g, w1, w2, hm, a, act, deps, extra_grads=()):
        tn = _pick(ffs, 1024)
        dz, = _matmul("mlp_dact_" + tag, Mat(dxb, t, d), w2, "nt",
                      [_out(t, ff, BF16)], tm, tn, kd,
                      epilogue=lambda z, av: (z * (2.0 * av.astype(F32)),), extras=[Mat(a, t, ff)], deps=deps)
        dw2, = _matmul("mlp_dw2_" + tag, Mat(act, t, ff), Mat(dxb, t, d), "tn",
                       [_out(ff, d, BF16)], tn, _pick(d, 1024), kt)
        dw1, = _matmul("mlp_dw1_" + tag, Mat(hm, t, d), Mat(dz, t, ff), "tn",
                       [_out(d, ff, BF16, "colstack", (), (N_CHIPS, d, ffs))], _pick(d, 1024), tn, kt)
        started, tok = pair_start("m" + tag, [dw1, dw2.reshape(N_CHIPS, ffs, d), *extra_grads])
        dhm, = _matmul("mlp_dh_" + tag, Mat(dz, t, ff), w1, "nt",
                       [_out(t, d, F32)], tm, _pick(d, 1024), _pick(ffs, 2048), deps=(tok,))
        dxo, dxob, dg = _norm_bwd("mlp_norm_bwd_" + tag, dhm, xin, g, dx, tr)
        sc, tok = pair_finish("m" + tag, started, dxo)
        return dxo, dxob, dg, sc, tok

    cq_cb, ckv_cb, kr_cb = 2 * gw // ql, (2 * gw + ql) // kvl, (2 * gw + ql + kvl) // LANES
    qn, kvn = _rowwise("qkv_norm", lambda a, b, ga, gb: (_rms(a, ga), _rms(b, gb)), t // tr,
                       [_rt(proj, tr, ql, cq_cb), _rt(proj, tr, kvl, ckv_cb), _whole(g_q), _whole(g_kv)],
                       [_rt_out(t, ql, BF16, tr), _rt_out(t, kvl, BF16, tr)])
    qfull, = _matmul("q_up", Mat(qn, t, ql), Mat(w_q_all, ql, 2 * hw), "nn", [_out(t, 2 * hw, F32)], tm, _pick(2 * hw, 1024), ql)
    kvall, = _matmul("kv_up", Mat(kvn, t, kvl), Mat(w_kv_all, kvl, 2 * hw), "nn", [_out(t, 2 * hw, BF16)], tm, _pick(2 * hw, 1024), kvl)
    qall, kr = _rope_fwd(qfull, proj, kr_cb, ctab, stab, heads, tr)
    att, lse_row = _attn_fwd(qall, kvall, kr, heads, scale, tr)
    rb = min(2 * LANES, t)
    sgu = _sgu_fwd(proj, g_vn, sgu_w, sgu_b, groups, rb)
    mixed = _rowwise("mix_norm", lambda a, s, ga, gs: jnp.concatenate([_rms(a, ga), _rms(s, gs)], axis=1), t // tr,
                     [_rt(att, tr), _rt(sgu, tr), _whole(g_mla), _whole(g_sgu)], [_rt_out(t, mix, BF16, tr)])[0]
    x1, = _matmul("e_out", Mat(mixed, t, mix), Mat(w_eout, mix, d), "nn", [_out(t, d, F32)], tm, _pick(d, 1024), _pick(mix, 2048),
                  epilogue=lambda z, r: (z + r,), extras=[Mat(xs, t, d)])
    x2, hm0, a0, act0, w1_0, w2_0 = mlp_fwd("0", x1, g_m0, 2)

    w_oin_g, w_oout_g = gathered(4, "o", x2)
    w_oout = w_oout_g.reshape(cd, d)
    h1 = _norm_fwd("o_norm", x2, g_o, tr)
    oin = Mat(_unstack_cols(w_oin_g), d, 3 * cd)
    tn_o = _pick(_gcd(3 * cd // N_CHIPS, cd), 512)
    proj3, = _matmul("o_proj", Mat(h1, t, d), oin, "nn", [_out(t, 3 * cd, F32, "colstack", (), (3, t, cd))],
                     tm, _pick(cd, 1024), kd)
    tc = _pick(cd, 256)
    bz = _conv_fwd(proj3, conv_w, tc)
    x3, = _matmul("o_out", Mat(bz, t, cd), Mat(w_oout, cd, d), "nn", [_out(t, d, F32)], tm, _pick(d, 1024), _pick(cd, 2048),
                  epilogue=lambda z, r: (z + r,), extras=[Mat(x2, t, d)])
    x4, hm1, a1, act1, w1_1, w2_1 = mlp_fwd("1", x3, g_m1, 5)

    def final_fn(xv, gv, tv):
        r = lax.rsqrt(jnp.mean(xv * xv, axis=-1, keepdims=True) + EPS)
        xh = xv * r
        err = xh * gv - tv
        dy = err * (1.0 / d)
        dxh = dy * gv
        dx = r * (dxh - xh * jnp.mean(dxh * xh, axis=-1, keepdims=True))
        sq = jnp.sum(err * err, axis=0, keepdims=True)
        part = sq[:, :LANES]
        for k in range(1, d // LANES):
            part = part + sq[:, k * LANES:(k + 1) * LANES]
        return dx, dx, part, jnp.sum(dy * xh, axis=0, keepdims=True)

    dx4, dx4b, loss_vec, dg_f = _rowwise("loss_final_norm", final_fn, t // tr, [_rt(x4, tr), _whole(g_f), _rt(tgt, tr)],
                                         [_rt_out(t, d, F32, tr), _rt_out(t, d, BF16, tr)],
                                         [jax.ShapeDtypeStruct((1, LANES), F32), jax.ShapeDtypeStruct((1, d), F32)])

    dx3, dx3b, dg_m1, sc_m1, tok = mlp_bwd("1", dx4, dx4b, x3, g_m1, w1_1, w2_1, hm1, a1, act1, ())

    dbz, = _matmul("o_out_dx", Mat(dx3b, t, d), Mat(w_oout, cd, d), "nt", [_out(t, cd, F32)], tm, _pick(cd, 1024), kd,
                   deps=(tok,))
    dw_oout, = _matmul("o_out_dw", Mat(bz, t, cd), Mat(dx3b, t, d), "tn", [_out(cd, d, BF16)], _pick(cd, 1024), _pick(d, 1024), kt)
    dproj3, dconv = _conv_bwd(proj3, conv_w, dbz, tc)
    dp3 = Mat(dproj3, t, 3 * cd, "colstack")
    dw_oin, = _matmul("o_proj_dw", Mat(h1, t, d), dp3, "tn", [_out(d, 3 * cd, BF16, "colstack", (), (N_CHIPS, d, 3 * cd // N_CHIPS))],
                      _pick(d, 1024), tn_o, kt)
    started_o, tok = pair_start("o", [dw_oin, dw_oout.reshape(N_CHIPS, cd // N_CHIPS, d)])
    dh1, = _matmul("o_proj_dx", dp3, oin, "nt", [_out(t, d, F32)], tm, _pick(d, 1024), _pick(cd, 2048), deps=(tok,))
    dx2, dx2b, dg_o = _norm_bwd("o_norm_bwd", dh1, x2, g_o, dx3, tr)
    sc_o, tok = pair_finish("o", started_o, dx2)

    dconv_s = jnp.transpose(dconv[:3].reshape(3, N_CHIPS, cd // N_CHIPS), (1, 0, 2))
    gsmall = jnp.concatenate([jnp.pad(dg_o.reshape(N_CHIPS, 1, d // N_CHIPS), ((0, 0), (0, 15), (0, 0))),
                              jnp.pad(dconv_s, ((0, 0), (0, 13), (0, 0)))], axis=1)
    dx1, dx1b, dg_m0, sc_m0, tok = mlp_bwd("0", dx2, dx2b, x1, g_m0, w1_0, w2_0, hm0, a0, act0, (tok,), (gsmall,))

    dmixed, = _matmul("e_out_dx", Mat(dx1b, t, d), Mat(w_eout, mix, d), "nt", [_out(t, mix, F32)], tm, _pick(mix, 1024), kd,
                      deps=(tok,))
    dw_eout, = _matmul("e_out_dw", Mat(mixed, t, mix), Mat(dx1b, t, d), "tn", [_out(mix, d, BF16)], _pick(mix, 1024), _pick(d, 1024), kt)

    def mixb_fn(dm, a, s, ga, gs):
        da, dga = _rms_bwd(dm[:, :hw], a, ga)
        dsg, dgs = _rms_bwd(dm[:, hw:], s, gs)
        prod = da * a
        cols = [jnp.broadcast_to(jnp.sum(prod[:, h * LANES:(h + 1) * LANES], axis=-1, keepdims=True), (tr, LANES))
                for h in range(heads)]
        return da, dsg, jnp.stack([_row_of(c) for c in cols], axis=0), dga, dgs

    da_b, dsgu, delta_row, dg_mla, dg_sgu = _rowwise(
        "mix_norm_bwd", mixb_fn, t // tr, [_rt(dmixed, tr), _rt(att, tr), _rt(sgu, tr), _whole(g_mla), _whole(g_sgu)],
        [_rt_out(t, hw, BF16, tr), _rt_out(t, gw, F32, tr),
         (jax.ShapeDtypeStruct((heads, 8, t), F32), pl.BlockSpec((heads, 8, tr), lambda i: (0, 0, i)))],
        [jax.ShapeDtypeStruct((1, hw), F32), jax.ShapeDtypeStruct((1, gw), F32)])

    dproj, dsgu_w, dsgu_b8, dg_vn = _sgu_bwd(proj, dsgu, g_vn, sgu_w, sgu_b, groups, rb)
    dq1, dq2, dk1, dvv, dkr_h = _attn_bwd(qall, kvall, kr, da_b, lse_row, delta_row, heads, scale, tr)
    dqfull, dproj = _rope_bwd(dq1, dq2, dkr_h, ctab, stab, heads, tr, dproj, kr_cb)
    dkvall = jnp.concatenate([dk1, dvv], axis=1)
    dw_q, = _matmul("q_up_dw", Mat(qn, t, ql), Mat(dqfull, t, 2 * hw), "tn", [_out(ql, 2 * hw, BF16)], ql, _pick(2 * hw, 1024), kt)
    dqn, = _matmul("q_up_dx", Mat(dqfull, t, 2 * hw), Mat(w_q_all, ql, 2 * hw), "nt", [_out(t, ql, F32)], tm, ql, _pick(2 * hw, 2048))
    dw_kv, = _matmul("kv_up_dw", Mat(kvn, t, kvl), Mat(dkvall, t, 2 * hw), "tn", [_out(kvl, 2 * hw, BF16)], kvl, _pick(2 * hw, 1024), kt)
    dkvn, = _matmul("kv_up_dx", Mat(dkvall, t, 2 * hw), Mat(w_kv_all, kvl, 2 * hw), "nt", [_out(t, kvl, F32)], tm, kvl, _pick(2 * hw, 2048))

    def qkvb_fn(da, db, a, b, ga, gb):
        dxa, dga = _rms_bwd(da, a, ga)
        dxb, dgb = _rms_bwd(db, b, gb)
        return jnp.concatenate([dxa, dxb], axis=1), dga, dgb

    assert (2 * gw) % (ql + kvl) == 0
    into = (jax.ShapeDtypeStruct(dproj.shape, dproj.dtype),
            pl.BlockSpec((tr, ql + kvl), lambda i: (i, 2 * gw // (ql + kvl))))
    dproj, dg_q, dg_kv = _rowwise(
        "qkv_norm_bwd", qkvb_fn, t // tr,
        [_rt(dqn, tr), _rt(dkvn, tr), _rt(proj, tr, ql, cq_cb), _rt(proj, tr, kvl, ckv_cb), _whole(g_q), _whole(g_kv)],
        [into], [jax.ShapeDtypeStruct((1, ql), F32), jax.ShapeDtypeStruct((1, kvl), F32)], deps=(dproj,), fill=(0, 0))
    dw_in, = _matmul("e_proj_dw", Mat(dproj, t, pi), Mat(h0, t, d), "tn", [_out(pi, d, F32)], _pick(pi, 1024), _pick(d, 1024), kt)
    dh0, = _matmul("e_proj_dx", Mat(dproj, t, pi), Mat(w_in_all, d, pi), "nt", [_out(t, d, F32)], tm, _pick(d, 1024), _pick(pi, 4096))
    dx0, _, dg_e = _norm_bwd("e_norm_bwd", dh0, xs, g_e, dx1, tr)

    kr0 = 2 * gw + c2
    gw_in = jnp.concatenate([dw_in[2 * gw:kr0], dw_in[kr0:kr0 + ROPE_HALF], dw_in[kr0 + ROPE:kr0 + ROPE + ROPE_HALF],
                             dw_in[:2 * gw]], axis=0).reshape(N_CHIPS, ei // N_CHIPS, d)
    gq = jnp.concatenate([dw_q[:, :hw].reshape(ql, heads, LANES), _unpad_rope(dw_q[:, hw:].reshape(ql, heads, LANES))], axis=-1)
    gw_uq = _stack_cols(gq.reshape(ql, heads * (LANES + ROPE)))
    gkv = jnp.concatenate([dw_kv[:, :hw].reshape(kvl, heads, LANES), dw_kv[:, hw:].reshape(kvl, heads, LANES)], axis=-1)
    gw_ukv = _stack_cols(gkv.reshape(kvl, heads * 2 * LANES))
    started_e, tok_pair = pair_start("e", [gw_in, gw_uq, gw_ukv, dw_eout.reshape(N_CHIPS, mix // N_CHIPS, d)])

    small_like = [e_norm_mix, e_q_norm, e_kv_norm, e_v_norm, e_sgu_w, e_sgu_b, e_mla_out_norm, e_sgu_out_norm, mlp_norm, final_norm]
    small_grads = [dg_e, dg_q, dg_kv, dg_vn, dsgu_w, dsgu_b8[:, 0, :], dg_mla, dg_sgu, jnp.concatenate([dg_m0, dg_m1], axis=0), dg_f]
    packed = _pack_small(small_grads)
    n_small = packed.shape[0] + (-packed.shape[0]) % 8
    pad = n_small - packed.shape[0] + 8
    sflat = jnp.concatenate([jnp.pad(packed, ((0, pad - 8), (0, 0))), jnp.pad(loss_vec, ((0, 7), (0, 0)))], axis=0)
    small_started, tok_small = _exchange_start("small_start", _all_route, 7, [sflat], [_spread(sflat)])

    sh_m1, tok = summed("m1", sc_m1, (tok_pair, tok_small))
    sc_e, tok = pair_finish("e", started_e, tok)
    sh_o, tok = summed("o", sc_o, tok)
    sh_m0, tok = summed("m0", sc_m0, tok)
    r_oin, r_oout = shared("o", sh_o, tok)
    late = {"o_w_in": _adamw(o_w_in, [r_oin], m_o_w_in, v_o_w_in),
            "o_w_out": _adamw(o_w_out, [r_oout], m_o_w_out, v_o_w_out)}
    r_w1_1, r_w2_1 = shared("m1", sh_m1, late["o_w_in"][1])
    r_w1_0, r_w2_0, r_small = shared("m0", sh_m0, r_w2_1)
    late["mlp_w1"] = _adamw(mlp_w1, [r_w1_0, r_w1_1], m_mlp_w1, v_mlp_w1)
    late["mlp_w2"] = _adamw(mlp_w2, [r_w2_0, r_w2_1], m_mlp_w2, v_mlp_w2)

    _, (all_small,) = _exchange_wait("small_wait", _all_route, small_started, late["mlp_w2"][1])
    g_small = _sum_devices(all_small)
    loss = 0.5 * jnp.sum(g_small[n_small]) / d

    def padded(arrs):
        return jnp.pad(_pack_small(arrs), ((0, pad), (0, 0)))

    s_m = [m_e_norm_mix, m_e_q_norm, m_e_kv_norm, m_e_v_norm, m_e_sgu_w, m_e_sgu_b, m_e_mla_out_norm, m_e_sgu_out_norm, m_mlp_norm, m_final_norm]
    s_v = [v_e_norm_mix, v_e_q_norm, v_e_kv_norm, v_e_v_norm, v_e_sgu_w, v_e_sgu_b, v_e_mla_out_norm, v_e_sgu_out_norm, v_mlp_norm, v_final_norm]
    s_out = [_unpack_small(o[0], small_like)
             for o in _adamw(padded(small_like)[None], [g_small], padded(s_m)[None], padded(s_v)[None])]

    sm = [o[0] for o in _adamw(small_shard[None], [r_small], _small_shard(m_o_norm_mix, m_o_conv_w[0])[None],
                               _small_shard(v_o_norm_mix, v_o_conv_w[0])[None])]

    sh_e, tok = summed("e", sc_e, late["mlp_w2"][1])
    r_in, r_uq, r_ukv, r_eout = shared("e", sh_e, tok)
    big = dict(late)
    flip = lambda a: jnp.swapaxes(a, 1, 2)
    big.update({
        "e_w_in": [flip(o) for o in _adamw(flip(e_w_in), [r_in], flip(m_e_w_in), flip(v_e_w_in))],
        "e_w_uq": _adamw(e_w_uq, [r_uq], m_e_w_uq, v_e_w_uq),
        "e_w_ukv": _adamw(e_w_ukv, [r_ukv], m_e_w_ukv, v_e_w_ukv),
        "e_w_out": _adamw(e_w_out, [r_eout], m_e_w_out, v_e_w_out),
    })

    names = ["e_norm_mix", "e_w_in", "e_q_norm", "e_w_uq", "e_kv_norm", "e_w_ukv", "e_v_norm", "e_sgu_w", "e_sgu_b",
             "e_mla_out_norm", "e_sgu_out_norm", "e_w_out", "o_norm_mix", "o_w_in", "o_conv_w", "o_w_out",
             "mlp_norm", "mlp_w1", "mlp_w2", "final_norm"]
    shapes = {"e_w_in": e_w_in.shape, "e_w_uq": e_w_uq.shape, "e_w_ukv": e_w_ukv.shape, "e_w_out": e_w_out.shape,
              "o_w_in": o_w_in.shape, "o_w_out": o_w_out.shape, "mlp_w1": mlp_w1.shape, "mlp_w2": mlp_w2.shape}
    small_names = ["e_norm_mix", "e_q_norm", "e_kv_norm", "e_v_norm", "e_sgu_w", "e_sgu_b", "e_mla_out_norm",
                   "e_sgu_out_norm", "mlp_norm", "final_norm"]

    def leaf(kind, name):
        if name in big:
            return big[name][kind].reshape(shapes[name])
        if name == "o_norm_mix":
            return sm[kind][0:1]
        if name == "o_conv_w":
            return sm[kind][16:19].reshape(o_conv_w.shape)
        return s_out[kind][small_names.index(name)]

    outs = [loss, dx0.reshape(x.shape)]
    for kind in range(4):
        outs += [leaf(kind, nm) for nm in names]
    return tuple(outs)


def _gcd(a, b):
    while b:
        a, b = b, a % b
    return a
```

```python
import functools

import jax
import jax.numpy as jnp
from jax import lax
from jax.experimental import pallas as pl
from jax.experimental.pallas import tpu as pltpu

F32 = jnp.float32
BF16 = jnp.bfloat16
MESH = pl.DeviceIdType.MESH

LANES = 128
ROPE = 64
ROPE_HALF = ROPE // 2
ROPE_BASE = 10000.0
EPS = 1e-6
N_CHIPS = 4
VMEM_LIMIT = 48 * 1024 * 1024
NEG = -1e30

ADAM_LR = 0.001
ADAM_B1 = 0.9
ADAM_B2 = 0.999
ADAM_EPS = 1e-08
ADAM_WD = 0.01
ADAM_STEP = 10


def _pick(n, target, step=LANES):
    best = None
    for t in range(step, min(n, target) + 1, step):
        if n % t == 0:
            best = t
    return best if best is not None else n


def _params(sem, vmem=VMEM_LIMIT):
    return pltpu.CompilerParams(dimension_semantics=sem, vmem_limit_bytes=vmem)


class Mat:
    def __init__(self, arr, rows, cols, kind="plain", lead=(), col_off=0, shape=None, dtype=None):
        self.arr, self.rows, self.cols, self.kind, self.lead, self.col_off = arr, rows, cols, kind, tuple(lead), col_off
        self.shape = tuple(arr.shape) if arr is not None else tuple(shape)
        self.dtype = arr.dtype if arr is not None else dtype

    def sds(self):
        return jax.ShapeDtypeStruct(self.shape, self.dtype)

    def spec(self, br, bc, gridmap):
        lead, nl = self.lead, len(self.lead)
        if self.kind == "plain":
            assert self.col_off % bc == 0 and self.rows % br == 0 and self.cols % bc == 0, (self.shape, br, bc)
            off = self.col_off // bc
            block = (None,) * nl + (br, bc)

            def phys(rb, cb):
                return lead + (rb, cb + off)
        elif self.kind == "colstack":
            cs = self.shape[-1]
            assert cs % bc == 0 and self.rows % br == 0, (self.shape, br, bc)
            q = cs // bc
            block = (None,) * (nl + 1) + (br, bc)

            def phys(rb, cb):
                return (cb // q,) + lead + (rb, cb % q)
        else:
            rs = self.shape[-2]
            assert rs % br == 0 and self.cols % bc == 0, (self.shape, br, bc)
            q = rs // br
            block = (None,) * (nl + 1) + (br, bc)

            def phys(rb, cb):
                return (rb // q,) + lead + (rb % q, cb)

        return pl.BlockSpec(block, lambda *g: phys(*gridmap(*g)))


def _adamw_math(w, g, m, v):
    mn = ADAM_B1 * m + (1.0 - ADAM_B1) * g
    vn = ADAM_B2 * v + (1.0 - ADAM_B2) * jnp.square(g)
    m_hat = mn / (1.0 - ADAM_B1 ** ADAM_STEP)
    v_hat = vn / (1.0 - ADAM_B2 ** ADAM_STEP)
    return -ADAM_LR * (m_hat / (jnp.sqrt(v_hat) + ADAM_EPS) + ADAM_WD * w), mn, vn


def _matmul(name, a, b, mode, outs, tm, tn, tk, epilogue=None, extras=(), deps=()):
    if mode == "nn":
        m, k, n = a.rows, a.cols, b.cols
        a_spec = a.spec(tm, tk, lambda i, j, kk: (i, kk))
        b_spec = b.spec(tk, tn, lambda i, j, kk: (kk, j))
        dims = (((1,), (0,)), ((), ()))
    elif mode == "nt":
        m, k, n = a.rows, a.cols, b.rows
        a_spec = a.spec(tm, tk, lambda i, j, kk: (i, kk))
        b_spec = b.spec(tn, tk, lambda i, j, kk: (j, kk))
        dims = (((1,), (1,)), ((), ()))
    else:
        k, m, n = a.rows, a.cols, b.cols
        a_spec = a.spec(tk, tm, lambda i, j, kk: (kk, i))
        b_spec = b.spec(tk, tn, lambda i, j, kk: (kk, j))
        dims = (((0,), (0,)), ((), ()))
    assert m % tm == 0 and n % tn == 0 and k % tk == 0, (name, m, n, k, tm, tn, tk)
    grid = (m // tm, n // tn, k // tk)
    nk = grid[2]
    n_ex, n_out, n_dep = len(extras), len(outs), len(deps)
    tile = lambda i, j, kk: (i, j)

    def finish(z, ex, out_refs):
        vals = epilogue(z, *[e[...] for e in ex]) if epilogue is not None else (z,)
        for o, v in zip(out_refs, vals):
            o[...] = v.astype(o.dtype)

    def body_single(a_ref, b_ref, *rest):
        finish(lax.dot_general(a_ref[...], b_ref[...], dims, preferred_element_type=F32),
               rest[:n_ex], rest[n_ex + n_dep:n_ex + n_dep + n_out])

    def body_acc(a_ref, b_ref, *rest):
        acc = rest[-1]
        kk = pl.program_id(2)

        @pl.when(kk == 0)
        def _():
            acc[...] = jnp.zeros_like(acc)

        acc[...] += lax.dot_general(a_ref[...], b_ref[...], dims, preferred_element_type=F32)

        @pl.when(kk == nk - 1)
        def _():
            finish(acc[...], rest[:n_ex], rest[n_ex + n_dep:n_ex + n_dep + n_out])

    res = pl.pallas_call(
        body_single if nk == 1 else body_acc, name=name, grid=grid,
        in_specs=[a_spec, b_spec] + [e.spec(tm, tn, tile) for e in extras]
        + [pl.BlockSpec(memory_space=pl.ANY) for _ in deps],
        out_specs=[o.spec(tm, tn, tile) for o in outs],
        out_shape=[o.sds() for o in outs],
        scratch_shapes=[] if nk == 1 else [pltpu.VMEM((tm, tn), F32)],
        compiler_params=_params(("parallel", "parallel", "arbitrary")),
    )(a.arr, b.arr, *[e.arr for e in extras], *deps)
    return res


def _out(rows, cols, dtype, kind="plain", lead=(), shape=None):
    return Mat(None, rows, cols, kind, lead, shape=shape if shape is not None else (rows, cols), dtype=dtype)


def _rt(arr, tr, width=None, cb=0):
    width = arr.shape[1] if width is None else width
    return arr, pl.BlockSpec((tr, width), lambda i: (i, cb))


def _whole(arr):
    nd = arr.ndim
    return arr, pl.BlockSpec(arr.shape, lambda i: (0,) * nd)


def _rowwise(name, fn, n_steps, ins, outs, accs=(), deps=(), fill=None):
    n_in, n_out, n_acc, n_dep = len(ins), len(outs), len(accs), len(deps)

    def body(*refs):
        vals = fn(*[r[...] for r in refs[:n_in]])
        if not isinstance(vals, (tuple, list)):
            vals = (vals,)
        for ref, v in zip(refs[n_in + n_dep:n_in + n_dep + n_out], vals[:n_out]):
            ref[...] = v.astype(ref.dtype)
        if n_acc:
            acc_refs = refs[n_in + n_dep + n_out:]

            @pl.when(pl.program_id(0) == 0)
            def _():
                for ref in acc_refs:
                    ref[...] = jnp.zeros_like(ref)

            for ref, v in zip(acc_refs, vals[n_out:]):
                ref[...] += v

    acc_specs = [pl.BlockSpec(s.shape, lambda i, nd=len(s.shape): (0,) * nd) for s in accs]
    res = pl.pallas_call(
        body, name=name, grid=(n_steps,),
        in_specs=[s for _, s in ins] + [pl.BlockSpec(memory_space=pl.ANY) for _ in deps],
        out_specs=[s for _, s in outs] + acc_specs,
        out_shape=[o for o, _ in outs] + list(accs),
        input_output_aliases={} if fill is None else {n_in + fill[0]: fill[1]},
        compiler_params=_params(("arbitrary",) if n_acc else ("parallel",)),
    )(*[a for a, _ in ins], *deps)
    return res


def _rt_out(t, width, dtype, tr):
    return jax.ShapeDtypeStruct((t, width), dtype), pl.BlockSpec((tr, width), lambda i: (i, 0))


def _rms(x, g):
    r = lax.rsqrt(jnp.mean(x * x, axis=-1, keepdims=True) + EPS)
    return x * r * g


def _rms_bwd(dy, x, g):
    r = lax.rsqrt(jnp.mean(x * x, axis=-1, keepdims=True) + EPS)
    xh = x * r
    dxh = dy * g
    dx = r * (dxh - xh * jnp.mean(dxh * xh, axis=-1, keepdims=True))
    dg = jnp.sum(dy * xh, axis=0, keepdims=True)
    return dx, dg


def _gelu(x):
    k = 0.7978845608028654
    th = jnp.tanh(k * (x + 0.044715 * (x * x * x)))
    return x * (0.5 * (1.0 + th))


def _gelu_grad(x):
    k = 0.7978845608028654
    x2 = x * x
    th = jnp.tanh(k * (x + 0.044715 * (x2 * x)))
    return 0.5 * (1.0 + th) + 0.5 * x * (1.0 - th * th) * (k * (1.0 + 3.0 * 0.044715 * x2))


def _norm_fwd(name, x, g, tr):
    t, d = x.shape
    return _rowwise(name, lambda xv, gv: _rms(xv, gv), t // tr, [_rt(x, tr), _whole(g)], [_rt_out(t, d, BF16, tr)])[0]


def _norm_bwd(name, dh, x, g, dres, tr):
    t, d = x.shape

    def fn(dhv, xv, gv, drv):
        dx, dg = _rms_bwd(dhv, xv, gv)
        dx = dx + drv
        return dx, dx, dg

    return _rowwise(name, fn, t // tr, [_rt(dh, tr), _rt(x, tr), _whole(g), _rt(dres, tr)],
                    [_rt_out(t, d, F32, tr), _rt_out(t, d, BF16, tr)], [jax.ShapeDtypeStruct((1, d), F32)])


def _rope_tables(posf, invf, cmask, smask, tr):
    t = posf.shape[0]

    def fn(p, f, cm, sm):
        ang = p * f
        return jnp.cos(ang) * cm, jnp.sin(ang) * sm

    return _rowwise("rope_tables", fn, t // tr, [_rt(posf, tr), _whole(invf), _whole(cmask), _whole(smask)],
                    [_rt_out(t, LANES, F32, tr), _rt_out(t, LANES, F32, tr)])


def _rot(v, c, s):
    return v * c + pltpu.roll(v, ROPE, axis=1) * s


def _rot_bwd(dv, c, s):
    return dv * c + pltpu.roll(dv * s, ROPE, axis=1)


def _rope_fwd(qfull, proj, kr_cb, ctab, stab, heads, tr):
    t = qfull.shape[0]
    hw = heads * LANES

    def fn(q, kr, c, s):
        parts = [q[:, :hw]] + [_rot(q[:, hw + h * LANES: hw + (h + 1) * LANES], c, s) for h in range(heads)]
        return jnp.concatenate(parts, axis=1), _rot(kr, c, s)

    return _rowwise("rope_fwd", fn, t // tr, [_rt(qfull, tr), _rt(proj, tr, LANES, kr_cb), _rt(ctab, tr), _rt(stab, tr)],
                    [_rt_out(t, 2 * hw, BF16, tr), _rt_out(t, LANES, BF16, tr)])


def _rope_bwd(dq1, dq2, dkr_h, ctab, stab, heads, tr, dproj, kr_cb):
    t = dq1.shape[0]
    hw = heads * LANES

    def fn(a, b, dk, c, s):
        parts = [a] + [_rot_bwd(b[:, h * LANES:(h + 1) * LANES], c, s) for h in range(heads)]
        dks = dk[0]
        for h in range(1, heads):
            dks = dks + dk[h]
        return jnp.concatenate(parts, axis=1), _rot_bwd(dks, c, s)

    dk_spec = pl.BlockSpec((heads, tr, LANES), lambda i: (0, i, 0))
    into = (jax.ShapeDtypeStruct(dproj.shape, dproj.dtype), pl.BlockSpec((tr, LANES), lambda i: (i, kr_cb)))
    return _rowwise("rope_bwd", fn, t // tr, [_rt(dq1, tr), _rt(dq2, tr), (dkr_h, dk_spec), _rt(ctab, tr), _rt(stab, tr)],
                    [_rt_out(t, 2 * hw, BF16, tr), into], deps=(dproj,), fill=(0, 1))


def _dot_nt(a, b):
    return lax.dot_general(a, b, (((1,), (1,)), ((), ())), preferred_element_type=F32)


def _dot_tn(a, b):
    return lax.dot_general(a, b, (((0,), (0,)), ((), ())), preferred_element_type=F32)


def _dot(a, b):
    return jnp.dot(a, b, preferred_element_type=F32)


def _ranges(n_blocks):
    n_var = min(4, n_blocks)
    assert n_blocks % n_var == 0
    return n_var, n_blocks // n_var


def _row_of(col):
    return col.T[:8, :]


def _attn_fwd(qall, kvall, kr, heads, scale, tq):
    t = qall.shape[0]
    nq = t // tq
    n_var, per = _ranges(nq)

    def body(qn_ref, qr_ref, kn_ref, v_ref, kr_ref, o_ref, lser_ref):
        i = pl.program_id(1)
        for var in range(n_var):
            kv = (var + 1) * per * tq

            @pl.when(jnp.logical_and(i >= var * per, i < (var + 1) * per))
            def _(kv=kv):
                s = _dot_nt(jnp.concatenate([qn_ref[...], qr_ref[...]], axis=1),
                            jnp.concatenate([kn_ref[:kv, :], kr_ref[:kv, :]], axis=1)) * scale
                rows = i * tq + lax.broadcasted_iota(jnp.int32, (tq, kv), 0)
                cols = lax.broadcasted_iota(jnp.int32, (tq, kv), 1)
                s = jnp.where(cols <= rows, s, NEG)
                m = jnp.max(s, axis=-1, keepdims=True)
                p = jnp.exp(s - m)
                l = jnp.sum(p, axis=-1, keepdims=True)
                o_ref[...] = _dot(p.astype(BF16), v_ref[:kv, :]) / l
                lser_ref[...] = _row_of(jnp.broadcast_to(m + jnp.log(l), (tq, LANES)))

    return pl.pallas_call(
        body, name="attn_fwd", grid=(heads, nq),
        in_specs=[pl.BlockSpec((tq, LANES), lambda h, i: (i, h)),
                  pl.BlockSpec((tq, LANES), lambda h, i: (i, heads + h)),
                  pl.BlockSpec((t, LANES), lambda h, i: (0, h)),
                  pl.BlockSpec((t, LANES), lambda h, i: (0, heads + h)),
                  pl.BlockSpec((t, LANES), lambda h, i: (0, 0))],
        out_specs=[pl.BlockSpec((tq, LANES), lambda h, i: (i, h)),
                   pl.BlockSpec((None, 8, tq), lambda h, i: (h, 0, i))],
        out_shape=[jax.ShapeDtypeStruct((t, heads * LANES), F32), jax.ShapeDtypeStruct((heads, 8, t), F32)],
        compiler_params=_params(("parallel", "parallel")),
    )(qall, qall, kvall, kvall, kr)


def _attn_bwd(qall, kvall, kr, do, lse_row, delta_row, heads, scale, tk):
    t = qall.shape[0]
    nk = t // tk
    n_var, per = _ranges(nk)

    def body(qn_ref, qr_ref, kn_ref, v_ref, kr_ref, do_ref, lse_ref, dl_ref, dq1_ref, dq2_ref, dk_ref, dv_ref, dkr_ref):
        j = pl.program_id(1)

        @pl.when(j == 0)
        def _():
            dq1_ref[...] = jnp.zeros_like(dq1_ref)
            dq2_ref[...] = jnp.zeros_like(dq2_ref)

        for var in range(n_var):
            q0 = var * per * tk
            nq = t - q0

            @pl.when(jnp.logical_and(j >= var * per, j < (var + 1) * per))
            def _(q0=q0, nq=nq):
                qn, qr, do_v = qn_ref[q0:, :], qr_ref[q0:, :], do_ref[q0:, :]
                k1, k2 = kn_ref[...], kr_ref[...]
                qcat, kcat = jnp.concatenate([qn, qr], axis=1), jnp.concatenate([k1, k2], axis=1)
                st = _dot_nt(kcat, qcat) * scale
                keys = j * tk + lax.broadcasted_iota(jnp.int32, (tk, nq), 0)
                queries = q0 + lax.broadcasted_iota(jnp.int32, (tk, nq), 1)
                pt = jnp.where(keys <= queries, jnp.exp(st - lse_ref[0:1, q0:]), 0.0)
                dpt = _dot_nt(v_ref[...], do_v)
                dst = (pt * (dpt - dl_ref[0:1, q0:]) * scale).astype(BF16)
                dv_ref[...] = _dot(pt.astype(BF16), do_v).astype(dv_ref.dtype)
                dkc = _dot(dst, qcat)
                dk_ref[...] = dkc[:, :LANES].astype(dk_ref.dtype)
                dkr_ref[...] = dkc[:, LANES:]
                dqc = _dot_tn(dst, kcat)
                dq1_ref[q0:, :] += dqc[:, :LANES]
                dq2_ref[q0:, :] += dqc[:, LANES:]

    kblk = lambda off: pl.BlockSpec((tk, LANES), lambda h, j: (j, off + h))
    full = lambda off: pl.BlockSpec((t, LANES), lambda h, j: (0, off + h))
    stat = pl.BlockSpec((None, 8, t), lambda h, j: (h, 0, 0))
    return pl.pallas_call(
        body, name="attn_bwd", grid=(heads, nk),
        in_specs=[full(0), full(heads), kblk(0), kblk(heads), pl.BlockSpec((tk, LANES), lambda h, j: (j, 0)),
                  full(0), stat, stat],
        out_specs=[full(0), full(0), kblk(0), kblk(0), pl.BlockSpec((None, tk, LANES), lambda h, j: (h, j, 0))],
        out_shape=[jax.ShapeDtypeStruct((t, heads * LANES), F32)] * 2 + [jax.ShapeDtypeStruct((t, heads * LANES), BF16)] * 2
        + [jax.ShapeDtypeStruct((heads, t, LANES), F32)],
        compiler_params=_params(("parallel", "arbitrary")),
    )(qall, qall, kvall, kvall, kr, do, lse_row, delta_row)


def _tril():
    return lax.broadcasted_iota(jnp.int32, (LANES, LANES), 0) >= lax.broadcasted_iota(jnp.int32, (LANES, LANES), 1)


def _group_norm(vg):
    mu = jnp.mean(vg, axis=-1, keepdims=True)
    vc = vg - mu
    rs = lax.rsqrt(jnp.mean(vc * vc, axis=-1, keepdims=True) + EPS)
    return vc * rs, rs


def _sgu_fwd(proj, gain, w, bias, groups, rb):
    t = proj.shape[0]
    gw = groups * LANES
    cpb = rb // LANES

    def body(u_ref, v_ref, gain_ref, w_ref, b_ref, s_ref):
        tril = _tril()
        for g in range(groups):
            wt = jnp.where(tril, w_ref[g], 0.0).astype(BF16)
            cols = slice(g * LANES, (g + 1) * LANES)
            for ci in range(cpb):
                rows = slice(ci * LANES, (ci + 1) * LANES)
                ug = _gelu(u_ref[rows, cols])
                vh, _ = _group_norm(_gelu(v_ref[rows, cols]))
                vn = vh * gain_ref[:, cols]
                y = _dot(wt, vn.astype(BF16)) + b_ref[g]
                s_ref[rows, cols] = ug * y

    return pl.pallas_call(
        body, name="sgu_fwd", grid=(t // rb,),
        in_specs=[pl.BlockSpec((rb, gw), lambda i: (i, 0)), pl.BlockSpec((rb, gw), lambda i: (i, 1)),
                  pl.BlockSpec((1, gw), lambda i: (0, 0)),
                  pl.BlockSpec((groups, LANES, LANES), lambda i: (0, 0, 0)),
                  pl.BlockSpec((groups, LANES, LANES), lambda i: (0, 0, 0))],
        out_specs=pl.BlockSpec((rb, gw), lambda i: (i, 0)),
        out_shape=jax.ShapeDtypeStruct((t, gw), F32),
        compiler_params=_params(("parallel",)),
    )(proj, proj, gain, w, bias)


def _sgu_bwd(proj, ds, gain, w, bias, groups, rb):
    t, width = proj.shape
    gw = groups * LANES
    cpb = rb // LANES
    n_steps = t // rb

    def body(u_ref, v_ref, ds_ref, gain_ref, w_ref, b_ref, dp_ref, dw_ref, db_ref, dg_ref, dy_acc):
        du_ref, dv_ref = dp_ref.at[:, :gw], dp_ref.at[:, gw:]
        step = pl.program_id(0)

        @pl.when(step == 0)
        def _():
            dw_ref[...] = jnp.zeros_like(dw_ref)
            dy_acc[...] = jnp.zeros_like(dy_acc)
            dg_ref[...] = jnp.zeros_like(dg_ref)

        tril = _tril()
        for g in range(groups):
            wt = jnp.where(tril, w_ref[g], 0.0).astype(BF16)
            cols = slice(g * LANES, (g + 1) * LANES)
            gain_g = gain_ref[:, cols]
            for ci in range(cpb):
                rows = slice(ci * LANES, (ci + 1) * LANES)
                u_raw, v_raw, ds_v = u_ref[rows, cols], v_ref[rows, cols], ds_ref[rows, cols]
                ug = _gelu(u_raw)
                vh, rs = _group_norm(_gelu(v_raw))
                vn = (vh * gain_g).astype(BF16)
                y = _dot(wt, vn) + b_ref[g]
                dy = ds_v * ug
                dyb = dy.astype(BF16)
                du_ref[rows, cols] = (ds_v * y * _gelu_grad(u_raw)).astype(du_ref.dtype)
                dy_acc[g] += dy
                dw_ref[g] += _dot_nt(dyb, vn)
                dvn = _dot_tn(wt, dyb)
                dg_ref[:, cols] += jnp.sum(dvn * vh, axis=0, keepdims=True)
                dvh = dvn * gain_g
                dvg = rs * (dvh - jnp.mean(dvh, axis=-1, keepdims=True)
                            - vh * jnp.mean(dvh * vh, axis=-1, keepdims=True))
                dv_ref[rows, cols] = (dvg * _gelu_grad(v_raw)).astype(dv_ref.dtype)

        @pl.when(step == n_steps - 1)
        def _():
            ones = jnp.ones((8, LANES), F32)
            for g in range(groups):
                dw_ref[g] = jnp.where(tril, dw_ref[g], 0.0)
                db_ref[g] = lax.dot_general(ones, dy_acc[g], (((1,), (1,)), ((), ())),
                                            precision=lax.Precision.HIGHEST, preferred_element_type=F32)

    blk = lambda cb: pl.BlockSpec((rb, gw), lambda i: (i, cb))
    whole3 = pl.BlockSpec((groups, LANES, LANES), lambda i: (0, 0, 0))
    return pl.pallas_call(
        body, name="sgu_bwd", grid=(n_steps,),
        in_specs=[blk(0), blk(1), blk(0), pl.BlockSpec((1, gw), lambda i: (0, 0)), whole3, whole3],
        out_specs=[pl.BlockSpec((rb, 2 * gw), lambda i: (i, 0)), whole3,
                   pl.BlockSpec((groups, 8, LANES), lambda i: (0, 0, 0)), pl.BlockSpec((1, gw), lambda i: (0, 0))],
        out_shape=[jax.ShapeDtypeStruct((t, width), BF16),
                   jax.ShapeDtypeStruct((groups, LANES, LANES), F32), jax.ShapeDtypeStruct((groups, 8, LANES), F32),
                   jax.ShapeDtypeStruct((1, gw), F32)],
        scratch_shapes=[pltpu.VMEM((groups, LANES, LANES), F32)],
        compiler_params=_params(("arbitrary",)),
    )(proj, proj, ds, gain, w, bias)


def _shift_down(z, s):
    rows = lax.broadcasted_iota(jnp.int32, z.shape, 0)
    return jnp.where(rows >= s, pltpu.roll(z, s, axis=0), 0.0)


def _shift_up(z, s):
    n = z.shape[0]
    rows = lax.broadcasted_iota(jnp.int32, z.shape, 0)
    return jnp.where(rows < n - s, pltpu.roll(z, n - s, axis=0), 0.0)


def _conv_fwd(proj3, cw, tc):
    _, t, cd = proj3.shape

    def body(p_ref, w_ref, o_ref):
        z = p_ref[1] * p_ref[2]
        w = w_ref[...]
        zc = w[2:3] * z + w[1:2] * _shift_down(z, 1) + w[0:1] * _shift_down(z, 2)
        o_ref[...] = (p_ref[0] * zc).astype(o_ref.dtype)

    return pl.pallas_call(
        body, name="conv_fwd", grid=(cd // tc,),
        in_specs=[pl.BlockSpec((3, t, tc), lambda j: (0, 0, j)), pl.BlockSpec((8, tc), lambda j: (0, j))],
        out_specs=pl.BlockSpec((t, tc), lambda j: (0, j)),
        out_shape=jax.ShapeDtypeStruct((t, cd), BF16),
        compiler_params=_params(("parallel",)),
    )(proj3, cw)


def _conv_bwd(proj3, cw, dbz, tc):
    _, t, cd = proj3.shape

    def body(p_ref, w_ref, d_ref, o_ref, dw_ref):
        b, c, xin = p_ref[0], p_ref[1], p_ref[2]
        w = w_ref[...]
        z = c * xin
        z1, z2 = _shift_down(z, 1), _shift_down(z, 2)
        zc = w[2:3] * z + w[1:2] * z1 + w[0:1] * z2
        d = d_ref[...]
        dzc = d * b
        dz = w[2:3] * dzc + w[1:2] * _shift_up(dzc, 1) + w[0:1] * _shift_up(dzc, 2)
        o_ref[0] = (d * zc).astype(o_ref.dtype)
        o_ref[1] = (dz * xin).astype(o_ref.dtype)
        o_ref[2] = (dz * c).astype(o_ref.dtype)
        row = lax.broadcasted_iota(jnp.int32, (8, tc), 0)
        dw0 = jnp.sum(dzc * z2, axis=0, keepdims=True)
        dw1 = jnp.sum(dzc * z1, axis=0, keepdims=True)
        dw2 = jnp.sum(dzc * z, axis=0, keepdims=True)
        dw_ref[...] = jnp.where(row == 0, dw0, 0.0) + jnp.where(row == 1, dw1, 0.0) + jnp.where(row == 2, dw2, 0.0)

    return pl.pallas_call(
        body, name="conv_bwd", grid=(cd // tc,),
        in_specs=[pl.BlockSpec((3, t, tc), lambda j: (0, 0, j)), pl.BlockSpec((8, tc), lambda j: (0, j)),
                  pl.BlockSpec((t, tc), lambda j: (0, j))],
        out_specs=[pl.BlockSpec((3, t, tc), lambda j: (0, 0, j)), pl.BlockSpec((8, tc), lambda j: (0, j))],
        out_shape=[jax.ShapeDtypeStruct((3, t, cd), BF16), jax.ShapeDtypeStruct((8, cd), F32)],
        compiler_params=_params(("parallel",)),
    )(proj3, cw, dbz)


def _place():
    x, y, c = lax.axis_index("x"), lax.axis_index("y"), lax.axis_index("c")
    chips = [(1 - x, y), (x, 1 - y), (1 - x, 1 - y)]
    return x, y, c, chips


def _any_specs(n):
    return [pl.BlockSpec(memory_space=pl.ANY) for _ in range(n)]


HBM_SPEC = pl.BlockSpec(memory_space=pltpu.HBM)
SEM_SPEC = pl.BlockSpec(memory_space=pltpu.SEMAPHORE)
ORDERED_EFFECT = pltpu.SideEffectType.DATAFLOW_SIDE_EFFECTING


def _in_hbm(a):
    return pltpu.with_memory_space_constraint(a, pltpu.HBM)


def _token():
    return jax.ShapeDtypeStruct((8, LANES), F32), pl.BlockSpec(memory_space=pltpu.VMEM)


def _gather_start(name, groups):
    sizes = [len(g) for g in groups]
    flat = [b for g in groups for b in g]
    n, ng = len(flat), len(groups)

    def body(*refs):
        ins, sems, token = refs[:n], refs[n:n + 2 * ng], refs[-1]
        x, y, c, chips = _place()
        me = 2 * x + y
        i = 0
        for gi, size in enumerate(sizes):
            for j in range(size):
                blk = ins[i].at[me, c]
                for k, chip in enumerate(chips):
                    pltpu.make_async_remote_copy(src_ref=blk, dst_ref=blk, send_sem=sems[2 * gi].at[3 * j + k],
                                                 recv_sem=sems[2 * gi + 1].at[3 * j + k],
                                                 device_id=(*chip, c), device_id_type=MESH).start()
                i += 1
        token[...] = jnp.zeros_like(token)

    tok_shape, tok_spec = _token()
    res = pl.pallas_call(
        body, name=name,
        in_specs=[HBM_SPEC] * n,
        out_specs=[SEM_SPEC] * (2 * ng) + [HBM_SPEC] * n + [tok_spec],
        out_shape=[pltpu.SemaphoreType.DMA((3 * size,)) for size in sizes for _ in (0, 1)]
        + [pltpu.HBM(b.shape, b.dtype) for b in flat] + [tok_shape],
        input_output_aliases={i: 2 * ng + i for i in range(n)},
        compiler_params=pltpu.CompilerParams(has_side_effects=ORDERED_EFFECT),
    )(*[_in_hbm(b) for b in flat])
    out, i = [], 2 * ng
    for gi, size in enumerate(sizes):
        out.append((res[2 * gi], res[2 * gi + 1], list(res[i:i + size])))
        i += size
    return out, res[-1]


def _gather_wait(tag, send, recv, bufs, after):
    n = len(bufs)
    after = tuple(after) if isinstance(after, (tuple, list)) else (after,)

    def body(*refs):
        ins, send_ref, recv_ref = refs[:n], refs[n], refs[n + 1]
        x, y, c, chips = _place()
        me = 2 * x + y
        for j in range(n):
            for k, (px, py) in enumerate(chips):
                cp = pltpu.make_async_remote_copy(src_ref=ins[j].at[me, c], dst_ref=ins[j].at[2 * px + py, c],
                                                  send_sem=send_ref.at[3 * j + k], recv_sem=recv_ref.at[3 * j + k],
                                                  device_id=(px, py, c), device_id_type=MESH)
                cp.wait_send()
                cp.wait_recv()

    return pl.pallas_call(
        body, name="gather_wait_" + tag,
        in_specs=[HBM_SPEC] * n + [SEM_SPEC, SEM_SPEC] + _any_specs(len(after)),
        out_specs=[HBM_SPEC] * n,
        out_shape=[pltpu.HBM(b.shape, b.dtype) for b in bufs],
        input_output_aliases={i: i for i in range(n)},
        compiler_params=pltpu.CompilerParams(has_side_effects=ORDERED_EFFECT),
    )(*bufs, send, recv, *after)


def _gather_forward(tag, bufs):
    n = len(bufs)

    def body(*refs):
        ins, outs = refs[:n], refs[n:2 * n]
        send, recv = refs[2 * n:]
        x, y, c, chips = _place()
        sib = (x, y, 1 - c)

        def cp(i, k, slot, half):
            return pltpu.make_async_remote_copy(src_ref=ins[i].at[slot, half], dst_ref=outs[i].at[slot, half],
                                                send_sem=send.at[3 * i + k], recv_sem=recv.at[3 * i + k],
                                                device_id=sib, device_id_type=MESH)

        cps = [cp(i, k, 2 * px + py, c) for i in range(n) for k, (px, py) in enumerate(chips)]
        for d in cps:
            d.start()
        for i in range(n):
            for k, (px, py) in enumerate(chips):
                cp(i, k, 2 * px + py, 1 - c).wait_recv()
        for d in cps:
            d.wait_send()

    return pl.pallas_call(
        body, name="gather_forward_" + tag,
        in_specs=_any_specs(n), out_specs=_any_specs(n),
        out_shape=[jax.ShapeDtypeStruct(b.shape, b.dtype) for b in bufs],
        scratch_shapes=[pltpu.SemaphoreType.DMA((3 * n,))] * 2,
        input_output_aliases={i: i for i in range(n)},
        compiler_params=pltpu.CompilerParams(has_side_effects=True),
    )(*bufs)


def _pair_route(srcs, zones):
    x, y, c, _ = _place()
    return [(srcs[i].at[j, 1 - c], zones[i].at[j], (x, y, 1 - c)) for i in range(len(srcs)) for j in range(N_CHIPS)]


def _chip_route(srcs, zones):
    x, y, c, chips = _place()
    return [(srcs[i].at[2 * px + py], zones[i].at[k], (px, py, c)) for i in range(len(srcs)) for k, (px, py) in enumerate(chips)]


def _all_route(srcs, zones):
    x, y, c, _ = _place()
    flips = [(fx, fy, fc) for fx in (0, 1) for fy in (0, 1) for fc in (0, 1)][1:]
    return [(srcs[0], zones[0].at[4 * x + 2 * y + c], (x + fx - 2 * x * fx, y + fy - 2 * y * fy, c + fc - 2 * c * fc))
            for fx, fy, fc in flips]


def _share_route(srcs, zones):
    x, y, c, _ = _place()
    return [(s.at[c], s.at[c], (x, y, 1 - c)) for s in srcs]


def _exchange_start(name, route, n_copies, srcs, zones):
    n, nz = len(srcs), len(zones)
    lands = [lax.empty(z, a.dtype) if isinstance(z, tuple) else z for z, a in zip(zones, srcs)]

    def body(*refs):
        ins, zone_refs, send, recv, token = refs[:n], refs[n:n + nz], refs[n + nz], refs[n + nz + 1], refs[-1]
        for k, (src, dst, dev) in enumerate(route(ins, zone_refs)):
            pltpu.make_async_remote_copy(src_ref=src, dst_ref=dst, send_sem=send.at[k], recv_sem=recv.at[k],
                                         device_id=dev, device_id_type=MESH).start()
        token[...] = jnp.zeros_like(token)

    tok_shape, tok_spec = _token()
    res = pl.pallas_call(
        body, name=name,
        in_specs=[HBM_SPEC] * (n + nz),
        out_specs=[SEM_SPEC, SEM_SPEC] + [HBM_SPEC] * (n + nz) + [tok_spec],
        out_shape=[pltpu.SemaphoreType.DMA((n_copies,))] * 2 + [pltpu.HBM(a.shape, a.dtype) for a in srcs + lands]
        + [tok_shape],
        input_output_aliases={i: 2 + i for i in range(n + nz)},
        compiler_params=pltpu.CompilerParams(has_side_effects=ORDERED_EFFECT),
    )(*[_in_hbm(a) for a in srcs + lands])
    return (res[0], res[1], list(res[2:2 + n]), list(res[2 + n:2 + n + nz])), res[-1]


def _exchange_wait(name, route, started, after):
    send, recv, srcs, lands = started
    n, nz = len(srcs), len(lands)
    after = tuple(after) if isinstance(after, (tuple, list)) else (after,)

    def body(*refs):
        ins, zone_refs, send_ref, recv_ref = refs[:n], refs[n:n + nz], refs[n + nz], refs[n + nz + 1]
        for k, (src, dst, dev) in enumerate(route(ins, zone_refs)):
            cp = pltpu.make_async_remote_copy(src_ref=src, dst_ref=dst, send_sem=send_ref.at[k], recv_sem=recv_ref.at[k],
                                              device_id=dev, device_id_type=MESH)
            cp.wait_send()
            cp.wait_recv()

    res = pl.pallas_call(
        body, name=name,
        in_specs=[HBM_SPEC] * (n + nz) + [SEM_SPEC, SEM_SPEC] + _any_specs(len(after)),
        out_specs=[HBM_SPEC] * (n + nz),
        out_shape=[pltpu.HBM(a.shape, a.dtype) for a in srcs + lands],
        input_output_aliases={i: i for i in range(n + nz)},
        compiler_params=pltpu.CompilerParams(has_side_effects=ORDERED_EFFECT),
    )(*srcs, *lands, send, recv, *after)
    return list(res[:n]), list(res[n:])


def _spread(v):
    rows, cols = v.shape
    tr = _row_tile(rows, cols, budget=256 * 1024)

    def body(v_ref, o_ref):
        o_ref[...] = jnp.broadcast_to(v_ref[...][None], o_ref.shape)

    return pl.pallas_call(body, name="spread_small_grads", grid=(rows // tr,),
                          in_specs=[pl.BlockSpec((tr, cols), lambda r: (r, 0))],
                          out_specs=pl.BlockSpec((8, tr, cols), lambda r: (0, r, 0)),
                          out_shape=jax.ShapeDtypeStruct((8, rows, cols), v.dtype),
                          compiler_params=_params(("parallel",)))(v)


def _row_tile(rows, cols, itemsize=4, budget=2 * 1024 * 1024, step=8):
    best = None
    for t in range(step, rows + 1, step):
        if rows % t == 0 and t * cols * itemsize <= budget:
            best = t
    return best if best is not None else rows


def _my_chip():
    return 2 * lax.axis_index("x") + lax.axis_index("y")


def _pair_sum(g5, gsib):
    _, _, rh, cols = g5.shape
    tr = _row_tile(rh, cols, step=16)

    def body(a_ref, b_ref, o_ref):
        o_ref[...] = (a_ref[...].astype(F32) + b_ref[...].astype(F32)).astype(o_ref.dtype)

    return pl.pallas_call(body, name="grad_pair_sum", grid=(N_CHIPS, rh // tr),
                          in_specs=[pl.BlockSpec((None, None, tr, cols), lambda j, r: (j, lax.axis_index("c"), r, 0)),
                                    pl.BlockSpec((None, tr, cols), lambda j, r: (j, r, 0))],
                          out_specs=pl.BlockSpec((None, tr, cols), lambda j, r: (j, r, 0)),
                          out_shape=jax.ShapeDtypeStruct((N_CHIPS, rh, cols), BF16),
                          compiler_params=_params(("parallel", "parallel")))(g5, gsib)


def _chip_sum(part, recv):
    _, rh, cols = part.shape
    tr = _row_tile(rh, cols, step=16)

    def body(a_ref, b_ref, o_ref):
        acc = a_ref[...].astype(F32)
        for k in range(3):
            acc = acc + b_ref[k].astype(F32)
        o_ref[...] = acc

    return pl.pallas_call(body, name="grad_chip_sum", grid=(rh // tr,),
                          in_specs=[pl.BlockSpec((None, tr, cols), lambda r: (_my_chip(), r, 0)),
                                    pl.BlockSpec((3, tr, cols), lambda r: (0, r, 0))],
                          out_specs=pl.BlockSpec((None, tr, cols), lambda r: (lax.axis_index("c"), r, 0)),
                          out_shape=jax.ShapeDtypeStruct((2, rh, cols), F32),
                          compiler_params=_params(("parallel",)))(part, recv)


def _sum_devices(g):
    _, rows, cols = g.shape
    tr = _row_tile(rows, cols, budget=256 * 1024)

    def body(g_ref, o_ref):
        acc = g_ref[0]
        for d in range(1, 8):
            acc = acc + g_ref[d]
        o_ref[...] = acc

    return pl.pallas_call(body, name="sum_small_grads", grid=(rows // tr,),
                          in_specs=[pl.BlockSpec((8, tr, cols), lambda r: (0, r, 0))],
                          out_specs=pl.BlockSpec((tr, cols), lambda r: (r, 0)),
                          out_shape=jax.ShapeDtypeStruct((rows, cols), F32),
                          compiler_params=_params(("parallel",)))(g)


def _place_shard(w, layer, dtype, deps=()):
    _, rows, cols = w.shape
    tr = _row_tile(rows, cols)

    def body(i_ref, *rest):
        o_ref = rest[-1]
        o_ref[...] = i_ref[...].astype(o_ref.dtype)

    out = pl.pallas_call(body, name="place_shard", grid=(rows // tr,),
                         in_specs=[pl.BlockSpec((None, tr, cols), lambda r: (layer, r, 0))] + _any_specs(len(deps)),
                         out_specs=pl.BlockSpec((None, tr, cols), lambda r: (_my_chip(), r, 0)),
                         out_shape=jax.ShapeDtypeStruct((N_CHIPS, rows, cols), dtype),
                         compiler_params=_params(("parallel",)))(w, *deps)
    return out.reshape(N_CHIPS, 2, rows // 2, cols)


def _adamw(w, gs, m, v):
    n_layers, rows, cols = w.shape
    tr = _row_tile(rows, cols)

    def body(w_ref, m_ref, v_ref, *rest):
        g_refs = rest[:n_layers]
        go_ref, d_ref, mo_ref, vo_ref = rest[n_layers:]
        gv = g_refs[0][...]
        for layer in range(1, n_layers):
            gv = jnp.where(pl.program_id(0) == layer, g_refs[layer][...], gv)
        d_ref[...], mo_ref[...], vo_ref[...] = _adamw_math(w_ref[...], gv, m_ref[...], v_ref[...])
        go_ref[...] = gv

    spec = pl.BlockSpec((None, tr, cols), lambda layer, r: (layer, r, 0))
    g_specs = [pl.BlockSpec((tr, cols), lambda layer, r, own=own: (jnp.where(layer == own, r, 0), 0))
               for own in range(n_layers)]
    return pl.pallas_call(body, name="adamw", grid=(n_layers, rows // tr), in_specs=[spec] * 3 + g_specs,
                          out_specs=[spec] * 4, out_shape=[jax.ShapeDtypeStruct((n_layers, rows, cols), F32)] * 4,
                          compiler_params=_params(("parallel", "parallel")))(w, m, v, *gs)


def _pad_rope(w):
    z = jnp.zeros(w.shape[:-1] + (ROPE_HALF,), w.dtype)
    return jnp.concatenate([w[..., :ROPE_HALF], z, w[..., ROPE_HALF:], z], axis=-1)


def _unpad_rope(g):
    return jnp.concatenate([g[..., :ROPE_HALF], g[..., ROPE:ROPE + ROPE_HALF]], axis=-1)


def _unstack_cols(s):
    n, r, cs = s.shape
    return jnp.transpose(s, (1, 0, 2)).reshape(r, n * cs)


def _stack_cols(f):
    r, cfull = f.shape
    return jnp.transpose(f.reshape(r, N_CHIPS, cfull // N_CHIPS), (1, 0, 2))


def _small_shard(norm, conv):
    return jnp.concatenate([jnp.pad(norm, ((0, 15), (0, 0))), jnp.pad(conv, ((0, 13), (0, 0)))], axis=0)


def _flat_rows(a):
    return a.reshape(-1, LANES)


def _pack_small(arrs):
    return jnp.concatenate([_flat_rows(a.astype(F32)) for a in arrs], axis=0)


def _unpack_small(flat, like):
    out, r = [], 0
    for a in like:
        n = a.size // LANES
        out.append(flat[r:r + n].reshape(a.shape))
        r += n
    return out


def kernel(x, positions, e_norm_mix, e_w_in, e_q_norm, e_w_uq, e_kv_norm, e_w_ukv, e_v_norm, e_sgu_w, e_sgu_b, e_mla_out_norm, e_sgu_out_norm, e_w_out, o_norm_mix, o_w_in, o_conv_w, o_w_out, mlp_norm, mlp_w1, mlp_w2, final_norm, loss_target, m_e_norm_mix, m_e_w_in, m_e_q_norm, m_e_w_uq, m_e_kv_norm, m_e_w_ukv, m_e_v_norm, m_e_sgu_w, m_e_sgu_b, m_e_mla_out_norm, m_e_sgu_out_norm, m_e_w_out, m_o_norm_mix, m_o_w_in, m_o_conv_w, m_o_w_out, m_mlp_norm, m_mlp_w1, m_mlp_w2, m_final_norm, v_e_norm_mix, v_e_w_in, v_e_q_norm, v_e_w_uq, v_e_kv_norm, v_e_w_ukv, v_e_v_norm, v_e_sgu_w, v_e_sgu_b, v_e_mla_out_norm, v_e_sgu_out_norm, v_e_w_out, v_o_norm_mix, v_o_w_in, v_o_conv_w, v_o_w_out, v_mlp_norm, v_mlp_w1, v_mlp_w2, v_final_norm):
    t, d = x.shape[1], x.shape[2]
    ql, kvl = e_q_norm.shape[1], e_kv_norm.shape[1]
    groups = e_v_norm.shape[1]
    gw = groups * LANES
    heads = N_CHIPS * e_w_uq.shape[2] // (LANES + ROPE)
    hw = heads * LANES
    mix = hw + gw
    ei = N_CHIPS * e_w_in.shape[2]
    cd = N_CHIPS * o_conv_w.shape[2]
    ff = N_CHIPS * mlp_w1.shape[2]
    ffs = ff // N_CHIPS
    pi = 2 * gw + ql + kvl + LANES
    assert e_norm_mix.shape[0] == 1 and o_norm_mix.shape[0] == 1 and mlp_norm.shape[0] == 2
    assert ei == ql + kvl + ROPE + 2 * gw and cd == d and e_sgu_w.shape[2] == LANES
    assert (2 * gw) % ql == 0 and (2 * gw + ql) % kvl == 0 and t % LANES == 0
    scale = (LANES + ROPE) ** -0.5

    tr = min(256, t)
    tm = _pick(t, 1024, 8)
    kt, kd = _pick(t, 2048, 8), _pick(d, 2048)
    xs = x.reshape(t, d)
    tgt = loss_target.reshape(t, d)

    small_shard = _small_shard(o_norm_mix, o_conv_w[0])
    first, tok = _gather_start("gather_start_e", [
        [_place_shard(e_w_in, 0, BF16)],
        [_place_shard(e_w_uq, 0, BF16), _place_shard(e_w_ukv, 0, BF16), _place_shard(e_w_out, 0, BF16),
         _place_shard(small_shard[None], 0, F32)]])
    rest, tok = _gather_start("gather_start_rest", [
        [_place_shard(mlp_w1, 0, BF16, (tok,))], [_place_shard(mlp_w2, 0, BF16, (tok,))],
        [_place_shard(o_w_in, 0, BF16, (tok,)), _place_shard(o_w_out, 0, BF16, (tok,))],
        [_place_shard(mlp_w1, 1, BF16, (tok,))], [_place_shard(mlp_w2, 1, BF16, (tok,))]])
    started = first + rest

    def gathered(gi, tag, after):
        send, recv, bufs = started[gi]
        bufs = _gather_forward(tag, _gather_wait(tag, send, recv, bufs, after))
        return [b.reshape(N_CHIPS, 2 * b.shape[2], b.shape[3]) for b in bufs]

    g_e = e_norm_mix
    h0 = _norm_fwd("e_norm", xs, g_e, tr)
    inv_freq = ROPE_BASE ** (-jnp.arange(0, ROPE, 2, dtype=F32) / ROPE)
    zeros32 = jnp.zeros((ROPE_HALF,), F32)
    ones32 = jnp.ones((ROPE_HALF,), F32)
    invf = jnp.concatenate([inv_freq, zeros32, inv_freq, zeros32]).reshape(1, LANES)
    cmask = jnp.concatenate([ones32, zeros32, ones32, zeros32]).reshape(1, LANES)
    smask = jnp.concatenate([-ones32, zeros32, ones32, zeros32]).reshape(1, LANES)
    ctab, stab = _rope_tables(positions.reshape(t, 1).astype(F32), invf, cmask, smask, tr)

    w_in_g, = gathered(0, "e_in", (h0, ctab, tok))
    full = _unstack_cols(w_in_g)
    c2, c3 = ql + kvl, ql + kvl + ROPE
    w_in_all = jnp.concatenate([full[:, c3:], full[:, :c2], _pad_rope(full[:, c2:c3])], axis=1)
    proj, = _matmul("e_proj", Mat(h0, t, d), Mat(w_in_all, d, pi), "nn", [_out(t, pi, F32)], tm, _pick(pi, 1024), kd)

    w_uq_g, w_ukv_g, w_eout_g, small_g = gathered(1, "e", proj)
    full = _unstack_cols(w_uq_g).reshape(ql, heads, LANES + ROPE)
    w_q_all = jnp.concatenate([full[:, :, :LANES].reshape(ql, hw), _pad_rope(full[:, :, LANES:]).reshape(ql, hw)], axis=1)
    full = _unstack_cols(w_ukv_g).reshape(kvl, heads, 2 * LANES)
    w_kv_all = jnp.concatenate([full[:, :, :LANES].reshape(kvl, hw), full[:, :, LANES:].reshape(kvl, hw)], axis=1)
    w_eout = w_eout_g.reshape(mix, d)
    g_o = small_g[:, 0].reshape(1, d)
    conv_w = jnp.pad(jnp.transpose(small_g[:, 16:19], (1, 0, 2)).reshape(3, cd), ((0, 5), (0, 0)))

    g_q, g_kv = e_q_norm, e_kv_norm
    g_vn = e_v_norm.reshape(1, gw)
    sgu_w = e_sgu_w[0]
    sgu_b = jnp.broadcast_to(e_sgu_b[0][:, :, None], (groups, LANES, LANES))
    g_mla, g_sgu = e_mla_out_norm, e_sgu_out_norm
    g_m0, g_m1 = mlp_norm[0:1], mlp_norm[1:2]
    g_f = final_norm.reshape(1, d)

    def mlp_fwd(tag, xin, g, gi):
        hm = _norm_fwd("mlp_norm_" + tag, xin, g, tr)
        tn = _pick(ffs, 1024)
        w1 = Mat(gathered(gi, "w1_" + tag, hm)[0], d, ff, "colstack")
        a, act = _matmul("mlp_up_" + tag, Mat(hm, t, d), w1, "nn",
                         [_out(t, ff, BF16), _out(t, ff, BF16)], tm, tn, kd,
                         epilogue=lambda z: (jnp.maximum(z, 0.0), jnp.square(jnp.maximum(z, 0.0))))
        w2 = Mat(gathered(gi + 1, "w2_" + tag, act)[0].reshape(ff, d), ff, d)
        xo, = _matmul("mlp_down_" + tag, Mat(act, t, ff), w2, "nn",
                      [_out(t, d, F32)], tm, _pick(d, 1024), _pick(ffs, 2048),
                      epilogue=lambda z, r: (z + r,), extras=[Mat(xin, t, d)])
        return xo, hm, a, act, w1, w2

    def chip_start(tag, part):
        return _exchange_start("scatter_start_" + tag, _chip_route, 3 * len(part), part, [(3,) + p.shape[1:] for p in part])

    def pair_start(tag, stacked):
        g5 = [g.reshape(N_CHIPS, 2, g.shape[1] // 2, g.shape[2]) for g in stacked]
        return _exchange_start("pair_start_" + tag, _pair_route, N_CHIPS * len(g5), g5,
                               [(N_CHIPS,) + g.shape[2:] for g in g5])

    def pair_finish(tag, started, after):
        g5, from_sib = _exchange_wait("pair_wait_" + tag, _pair_route, started, after)
        return chip_start(tag, [_pair_sum(a, b) for a, b in zip(g5, from_sib)])

    def summed(tag, sc, after):
        part, lands = _exchange_wait("scatter_wait_" + tag, _chip_route, sc, after)
        half = [_chip_sum(p, r) for p, r in zip(part, lands)]
        return _exchange_start("share_start_" + tag, _share_route, len(half), half, [])

    def shared(tag, started, after):
        bufs, _ = _exchange_wait("share_wait_" + tag, _share_route, started, after)
        return [r.reshape(2 * r.shape[1], r.shape[2]) for r in bufs]

    def mlp_bwd(tag, dx, dxb, xin, g, w1, w2, hm, a, act, deps, extra_grads=()):
        tn = _pick(ffs, 1024)
        dz, = _matmul("mlp_dact_" + tag, Mat(dxb, t, d), w2, "nt",
                      [_out(t, ff, BF16)], tm, tn, kd,
                      epilogue=lambda z, av: (z * (2.0 * av.astype(F32)),), extras=[Mat(a, t, ff)], deps=deps)
        dw2, = _matmul("mlp_dw2_" + tag, Mat(act, t, ff), Mat(dxb, t, d), "tn",
                       [_out(ff, d, BF16)], tn, _pick(d, 2048), kt)
        dw1, = _matmul("mlp_dw1_" + tag, Mat(hm, t, d), Mat(dz, t, ff), "tn",
                       [_out(d, ff, BF16, "colstack", (), (N_CHIPS, d, ffs))], _pick(d, 2048), tn, kt)
        started, tok = pair_start("m" + tag, [dw1, dw2.reshape(N_CHIPS, ffs, d), *extra_grads])
        dhm, = _matmul("mlp_dh_" + tag, Mat(dz, t, ff), w1, "nt",
                       [_out(t, d, F32)], tm, _pick(d, 1024), _pick(ffs, 2048), deps=(tok,))
        dxo, dxob, dg = _norm_bwd("mlp_norm_bwd_" + tag, dhm, xin, g, dx, tr)
        sc, tok = pair_finish("m" + tag, started, dxo)
        return dxo, dxob, dg, sc, tok

    cq_cb, ckv_cb, kr_cb = 2 * gw // ql, (2 * gw + ql) // kvl, (2 * gw + ql + kvl) // LANES
    qn, kvn = _rowwise("qkv_norm", lambda a, b, ga, gb: (_rms(a, ga), _rms(b, gb)), t // tr,
                       [_rt(proj, tr, ql, cq_cb), _rt(proj, tr, kvl, ckv_cb), _whole(g_q), _whole(g_kv)],
                       [_rt_out(t, ql, BF16, tr), _rt_out(t, kvl, BF16, tr)])
    qfull, = _matmul("q_up", Mat(qn, t, ql), Mat(w_q_all, ql, 2 * hw), "nn", [_out(t, 2 * hw, F32)], tm, _pick(2 * hw, 1024), ql)
    kvall, = _matmul("kv_up", Mat(kvn, t, kvl), Mat(w_kv_all, kvl, 2 * hw), "nn", [_out(t, 2 * hw, BF16)], tm, _pick(2 * hw, 1024), kvl)
    qall, kr = _rope_fwd(qfull, proj, kr_cb, ctab, stab, heads, tr)
    att, lse_row = _attn_fwd(qall, kvall, kr, heads, scale, tr)
    rb = min(2 * LANES, t)
    sgu = _sgu_fwd(proj, g_vn, sgu_w, sgu_b, groups, rb)
    mixed = _rowwise("mix_norm", lambda a, s, ga, gs: jnp.concatenate([_rms(a, ga), _rms(s, gs)], axis=1), t // tr,
                     [_rt(att, tr), _rt(sgu, tr), _whole(g_mla), _whole(g_sgu)], [_rt_out(t, mix, BF16, tr)])[0]
    x1, = _matmul("e_out", Mat(mixed, t, mix), Mat(w_eout, mix, d), "nn", [_out(t, d, F32)], tm, _pick(d, 1024), _pick(mix, 2048),
                  epilogue=lambda z, r: (z + r,), extras=[Mat(xs, t, d)])
    x2, hm0, a0, act0, w1_0, w2_0 = mlp_fwd("0", x1, g_m0, 2)

    w_oin_g, w_oout_g = gathered(4, "o", x2)
    w_oout = w_oout_g.reshape(cd, d)
    h1 = _norm_fwd("o_norm", x2, g_o, tr)
    oin = Mat(_unstack_cols(w_oin_g), d, 3 * cd)
    tn_o = _pick(_gcd(3 * cd // N_CHIPS, cd), 512)
    proj3, = _matmul("o_proj", Mat(h1, t, d), oin, "nn", [_out(t, 3 * cd, F32, "colstack", (), (3, t, cd))],
                     tm, _pick(cd, 1024), kd)
    tc = _pick(cd, 256)
    bz = _conv_fwd(proj3, conv_w, tc)
    x3, = _matmul("o_out", Mat(bz, t, cd), Mat(w_oout, cd, d), "nn", [_out(t, d, F32)], tm, _pick(d, 1024), _pick(cd, 2048),
                  epilogue=lambda z, r: (z + r,), extras=[Mat(x2, t, d)])
    x4, hm1, a1, act1, w1_1, w2_1 = mlp_fwd("1", x3, g_m1, 5)

    def final_fn(xv, gv, tv):
        r = lax.rsqrt(jnp.mean(xv * xv, axis=-1, keepdims=True) + EPS)
        xh = xv * r
        err = xh * gv - tv
        dy = err * (1.0 / d)
        dxh = dy * gv
        dx = r * (dxh - xh * jnp.mean(dxh * xh, axis=-1, keepdims=True))
        sq = jnp.sum(err * err, axis=0, keepdims=True)
        part = sq[:, :LANES]
        for k in range(1, d // LANES):
            part = part + sq[:, k * LANES:(k + 1) * LANES]
        return dx, dx, part, jnp.sum(dy * xh, axis=0, keepdims=True)

    dx4, dx4b, loss_vec, dg_f = _rowwise("loss_final_norm", final_fn, t // tr, [_rt(x4, tr), _whole(g_f), _rt(tgt, tr)],
                                         [_rt_out(t, d, F32, tr), _rt_out(t, d, BF16, tr)],
                                         [jax.ShapeDtypeStruct((1, LANES), F32), jax.ShapeDtypeStruct((1, d), F32)])

    dx3, dx3b, dg_m1, sc_m1, tok = mlp_bwd("1", dx4, dx4b, x3, g_m1, w1_1, w2_1, hm1, a1, act1, ())

    dbz, = _matmul("o_out_dx", Mat(dx3b, t, d), Mat(w_oout, cd, d), "nt", [_out(t, cd, F32)], tm, _pick(cd, 1024), kd,
                   deps=(tok,))
    dw_oout, = _matmul("o_out_dw", Mat(bz, t, cd), Mat(dx3b, t, d), "tn", [_out(cd, d, BF16)], _pick(cd, 1024), _pick(d, 1024), kt)
    dproj3, dconv = _conv_bwd(proj3, conv_w, dbz, tc)
    dp3 = Mat(dproj3, t, 3 * cd, "colstack")
    dw_oin, = _matmul("o_proj_dw", Mat(h1, t, d), dp3, "tn", [_out(d, 3 * cd, BF16, "colstack", (), (N_CHIPS, d, 3 * cd // N_CHIPS))],
                      _pick(d, 1024), tn_o, kt)
    started_o, tok = pair_start("o", [dw_oin, dw_oout.reshape(N_CHIPS, cd // N_CHIPS, d)])
    dh1, = _matmul("o_proj_dx", dp3, oin, "nt", [_out(t, d, F32)], tm, _pick(d, 1024), _pick(cd, 2048), deps=(tok,))
    dx2, dx2b, dg_o = _norm_bwd("o_norm_bwd", dh1, x2, g_o, dx3, tr)
    sc_o, tok = pair_finish("o", started_o, dx2)

    dconv_s = jnp.transpose(dconv[:3].reshape(3, N_CHIPS, cd // N_CHIPS), (1, 0, 2))
    gsmall = jnp.concatenate([jnp.pad(dg_o.reshape(N_CHIPS, 1, d // N_CHIPS), ((0, 0), (0, 15), (0, 0))),
                              jnp.pad(dconv_s, ((0, 0), (0, 13), (0, 0)))], axis=1)
    dx1, dx1b, dg_m0, sc_m0, tok = mlp_bwd("0", dx2, dx2b, x1, g_m0, w1_0, w2_0, hm0, a0, act0, (tok,), (gsmall,))

    dmixed, = _matmul("e_out_dx", Mat(dx1b, t, d), Mat(w_eout, mix, d), "nt", [_out(t, mix, F32)], tm, _pick(mix, 1024), kd,
                      deps=(tok,))
    dw_eout, = _matmul("e_out_dw", Mat(mixed, t, mix), Mat(dx1b, t, d), "tn", [_out(mix, d, BF16)], _pick(mix, 1024), _pick(d, 1024), kt)

    def mixb_fn(dm, a, s, ga, gs):
        da, dga = _rms_bwd(dm[:, :hw], a, ga)
        dsg, dgs = _rms_bwd(dm[:, hw:], s, gs)
        prod = da * a
        cols = [jnp.broadcast_to(jnp.sum(prod[:, h * LANES:(h + 1) * LANES], axis=-1, keepdims=True), (tr, LANES))
                for h in range(heads)]
        return da, dsg, jnp.stack([_row_of(c) for c in cols], axis=0), dga, dgs

    da_b, dsgu, delta_row, dg_mla, dg_sgu = _rowwise(
        "mix_norm_bwd", mixb_fn, t // tr, [_rt(dmixed, tr), _rt(att, tr), _rt(sgu, tr), _whole(g_mla), _whole(g_sgu)],
        [_rt_out(t, hw, BF16, tr), _rt_out(t, gw, F32, tr),
         (jax.ShapeDtypeStruct((heads, 8, t), F32), pl.BlockSpec((heads, 8, tr), lambda i: (0, 0, i)))],
        [jax.ShapeDtypeStruct((1, hw), F32), jax.ShapeDtypeStruct((1, gw), F32)])

    dproj, dsgu_w, dsgu_b8, dg_vn = _sgu_bwd(proj, dsgu, g_vn, sgu_w, sgu_b, groups, rb)
    dq1, dq2, dk1, dvv, dkr_h = _attn_bwd(qall, kvall, kr, da_b, lse_row, delta_row, heads, scale, tr)
    dqfull, dproj = _rope_bwd(dq1, dq2, dkr_h, ctab, stab, heads, tr, dproj, kr_cb)
    dkvall = jnp.concatenate([dk1, dvv], axis=1)
    dw_q, = _matmul("q_up_dw", Mat(qn, t, ql), Mat(dqfull, t, 2 * hw), "tn", [_out(ql, 2 * hw, BF16)], ql, _pick(2 * hw, 1024), kt)
    dqn, = _matmul("q_up_dx", Mat(dqfull, t, 2 * hw), Mat(w_q_all, ql, 2 * hw), "nt", [_out(t, ql, F32)], tm, ql, _pick(2 * hw, 2048))
    dw_kv, = _matmul("kv_up_dw", Mat(kvn, t, kvl), Mat(dkvall, t, 2 * hw), "tn", [_out(kvl, 2 * hw, BF16)], kvl, _pick(2 * hw, 1024), kt)
    dkvn, = _matmul("kv_up_dx", Mat(dkvall, t, 2 * hw), Mat(w_kv_all, kvl, 2 * hw), "nt", [_out(t, kvl, F32)], tm, kvl, _pick(2 * hw, 2048))

    def qkvb_fn(da, db, a, b, ga, gb):
        dxa, dga = _rms_bwd(da, a, ga)
        dxb, dgb = _rms_bwd(db, b, gb)
        return jnp.concatenate([dxa, dxb], axis=1), dga, dgb

    assert (2 * gw) % (ql + kvl) == 0
    into = (jax.ShapeDtypeStruct(dproj.shape, dproj.dtype),
            pl.BlockSpec((tr, ql + kvl), lambda i: (i, 2 * gw // (ql + kvl))))
    dproj, dg_q, dg_kv = _rowwise(
        "qkv_norm_bwd", qkvb_fn, t // tr,
        [_rt(dqn, tr), _rt(dkvn, tr), _rt(proj, tr, ql, cq_cb), _rt(proj, tr, kvl, ckv_cb), _whole(g_q), _whole(g_kv)],
        [into], [jax.ShapeDtypeStruct((1, ql), F32), jax.ShapeDtypeStruct((1, kvl), F32)], deps=(dproj,), fill=(0, 0))
    dw_in, = _matmul("e_proj_dw", Mat(dproj, t, pi), Mat(h0, t, d), "tn", [_out(pi, d, F32)], _pick(pi, 1024), _pick(d, 1024), kt)
    dh0, = _matmul("e_proj_dx", Mat(dproj, t, pi), Mat(w_in_all, d, pi), "nt", [_out(t, d, F32)], tm, _pick(d, 1024), _pick(pi, 4096))
    dx0, _, dg_e = _norm_bwd("e_norm_bwd", dh0, xs, g_e, dx1, tr)

    kr0 = 2 * gw + c2
    gw_in = jnp.concatenate([dw_in[2 * gw:kr0], dw_in[kr0:kr0 + ROPE_HALF], dw_in[kr0 + ROPE:kr0 + ROPE + ROPE_HALF],
                             dw_in[:2 * gw]], axis=0).reshape(N_CHIPS, ei // N_CHIPS, d)
    gq = jnp.concatenate([dw_q[:, :hw].reshape(ql, heads, LANES), _unpad_rope(dw_q[:, hw:].reshape(ql, heads, LANES))], axis=-1)
    gw_uq = _stack_cols(gq.reshape(ql, heads * (LANES + ROPE)))
    gkv = jnp.concatenate([dw_kv[:, :hw].reshape(kvl, heads, LANES), dw_kv[:, hw:].reshape(kvl, heads, LANES)], axis=-1)
    gw_ukv = _stack_cols(gkv.reshape(kvl, heads * 2 * LANES))
    started_e, tok_pair = pair_start("e", [gw_in, gw_uq, gw_ukv, dw_eout.reshape(N_CHIPS, mix // N_CHIPS, d)])

    small_like = [e_norm_mix, e_q_norm, e_kv_norm, e_v_norm, e_sgu_w, e_sgu_b, e_mla_out_norm, e_sgu_out_norm, mlp_norm, final_norm]
    small_grads = [dg_e, dg_q, dg_kv, dg_vn, dsgu_w, dsgu_b8[:, 0, :], dg_mla, dg_sgu, jnp.concatenate([dg_m0, dg_m1], axis=0), dg_f]
    packed = _pack_small(small_grads)
    n_small = packed.shape[0] + (-packed.shape[0]) % 8
    pad = n_small - packed.shape[0] + 8
    sflat = jnp.concatenate([jnp.pad(packed, ((0, pad - 8), (0, 0))), jnp.pad(loss_vec, ((0, 7), (0, 0)))], axis=0)
    small_started, tok_small = _exchange_start("small_start", _all_route, 7, [sflat], [_spread(sflat)])

    sh_m1, tok = summed("m1", sc_m1, (tok_pair, tok_small))
    sc_e, tok = pair_finish("e", started_e, tok)
    sh_o, tok = summed("o", sc_o, tok)
    sh_m0, tok = summed("m0", sc_m0, tok)
    r_oin, r_oout = shared("o", sh_o, tok)
    late = {"o_w_in": _adamw(o_w_in, [r_oin], m_o_w_in, v_o_w_in),
            "o_w_out": _adamw(o_w_out, [r_oout], m_o_w_out, v_o_w_out)}
    r_w1_1, r_w2_1 = shared("m1", sh_m1, late["o_w_in"][1])
    r_w1_0, r_w2_0, r_small = shared("m0", sh_m0, r_w2_1)
    late["mlp_w1"] = _adamw(mlp_w1, [r_w1_0, r_w1_1], m_mlp_w1, v_mlp_w1)
    late["mlp_w2"] = _adamw(mlp_w2, [r_w2_0, r_w2_1], m_mlp_w2, v_mlp_w2)

    _, (all_small,) = _exchange_wait("small_wait", _all_route, small_started, late["mlp_w2"][1])
    g_small = _sum_devices(all_small)
    loss = 0.5 * jnp.sum(g_small[n_small]) / d

    def padded(arrs):
        return jnp.pad(_pack_small(arrs), ((0, pad), (0, 0)))

    s_m = [m_e_norm_mix, m_e_q_norm, m_e_kv_norm, m_e_v_norm, m_e_sgu_w, m_e_sgu_b, m_e_mla_out_norm, m_e_sgu_out_norm, m_mlp_norm, m_final_norm]
    s_v = [v_e_norm_mix, v_e_q_norm, v_e_kv_norm, v_e_v_norm, v_e_sgu_w, v_e_sgu_b, v_e_mla_out_norm, v_e_sgu_out_norm, v_mlp_norm, v_final_norm]
    s_out = [_unpack_small(o[0], small_like)
             for o in _adamw(padded(small_like)[None], [g_small], padded(s_m)[None], padded(s_v)[None])]

    sm = [o[0] for o in _adamw(small_shard[None], [r_small], _small_shard(m_o_norm_mix, m_o_conv_w[0])[None],
                               _small_shard(v_o_norm_mix, v_o_conv_w[0])[None])]

    sh_e, tok = summed("e", sc_e, late["mlp_w2"][1])
    r_in, r_uq, r_ukv, r_eout = shared("e", sh_e, tok)
    big = dict(late)
    flip = lambda a: jnp.swapaxes(a, 1, 2)
    big.update({
        "e_w_in": [flip(o) for o in _adamw(flip(e_w_in), [r_in], flip(m_e_w_in), flip(v_e_w_in))],
        "e_w_uq": _adamw(e_w_uq, [r_uq], m_e_w_uq, v_e_w_uq),
        "e_w_ukv": _adamw(e_w_ukv, [r_ukv], m_e_w_ukv, v_e_w_ukv),
        "e_w_out": _adamw(e_w_out, [r_eout], m_e_w_out, v_e_w_out),
    })

    names = ["e_norm_mix", "e_w_in", "e_q_norm", "e_w_uq", "e_kv_norm", "e_w_ukv", "e_v_norm", "e_sgu_w", "e_sgu_b",
             "e_mla_out_norm", "e_sgu_out_norm", "e_w_out", "o_norm_mix", "o_w_in", "o_conv_w", "o_w_out",
             "mlp_norm", "mlp_w1", "mlp_w2", "final_norm"]
    shapes = {"e_w_in": e_w_in.shape, "e_w_uq": e_w_uq.shape, "e_w_ukv": e_w_ukv.shape, "e_w_out": e_w_out.shape,
              "o_w_in": o_w_in.shape, "o_w_out": o_w_out.shape, "mlp_w1": mlp_w1.shape, "mlp_w2": mlp_w2.shape}
    small_names = ["e_norm_mix", "e_q_norm", "e_kv_norm", "e_v_norm", "e_sgu_w", "e_sgu_b", "e_mla_out_norm",
                   "e_sgu_out_norm", "mlp_norm", "final_norm"]

    def leaf(kind, name):
        if name in big:
            return big[name][kind].reshape(shapes[name])
        if name == "o_norm_mix":
            return sm[kind][0:1]
        if name == "o_conv_w":
            return sm[kind][16:19].reshape(o_conv_w.shape)
        return s_out[kind][small_names.index(name)]

    outs = [loss, dx0.reshape(x.shape)]
    for kind in range(4):
        outs += [leaf(kind, nm) for nm in names]
    return tuple(outs)


def _gcd(a, b):
    while b:
        a, b = b, a % b
    return a
```

```python
import functools

import jax
import jax.numpy as jnp
from jax import lax
from jax.experimental import pallas as pl
from jax.experimental.pallas import tpu as pltpu

F32 = jnp.float32
BF16 = jnp.bfloat16
MESH = pl.DeviceIdType.MESH

LANES = 128
ROPE = 64
ROPE_HALF = ROPE // 2
ROPE_BASE = 10000.0
EPS = 1e-6
N_CHIPS = 4
VMEM_LIMIT = 48 * 1024 * 1024
NEG = -1e30

ADAM_LR = 0.001
ADAM_B1 = 0.9
ADAM_B2 = 0.999
ADAM_EPS = 1e-08
ADAM_WD = 0.01
ADAM_STEP = 10


def _pick(n, target, step=LANES):
    best = None
    for t in range(step, min(n, target) + 1, step):
        if n % t == 0:
            best = t
    return best if best is not None else n


def _params(sem, vmem=VMEM_LIMIT):
    return pltpu.CompilerParams(dimension_semantics=sem, vmem_limit_bytes=vmem)


class Mat:
    def __init__(self, arr, rows, cols, kind="plain", lead=(), cmap=None, shape=None, dtype=None):
        self.arr, self.rows, self.cols, self.kind, self.lead, self.cmap = arr, rows, cols, kind, tuple(lead), cmap
        self.shape = tuple(arr.shape) if arr is not None else tuple(shape)
        self.dtype = arr.dtype if arr is not None else dtype

    def sds(self):
        return jax.ShapeDtypeStruct(self.shape, self.dtype)

    def spec(self, br, bc, gridmap):
        lead, nl = self.lead, len(self.lead)
        if self.kind == "plain":
            assert self.rows % br == 0 and self.cols % bc == 0, (self.shape, br, bc)
            cmap = self.cmap if self.cmap is not None else (lambda cb, _: cb)
            block = (None,) * nl + (br, bc)

            def phys(rb, cb):
                return lead + (rb, cmap(cb, bc))
        elif self.kind == "colstack":
            cs = self.shape[-1]
            assert cs % bc == 0 and self.rows % br == 0, (self.shape, br, bc)
            q = cs // bc
            block = (None,) * (nl + 1) + (br, bc)

            def phys(rb, cb):
                return (cb // q,) + lead + (rb, cb % q)
        else:
            rs = self.shape[-2]
            assert rs % br == 0 and self.cols % bc == 0, (self.shape, br, bc)
            q = rs // br
            block = (None,) * (nl + 1) + (br, bc)

            def phys(rb, cb):
                return (rb // q,) + lead + (rb % q, cb)

        return pl.BlockSpec(block, lambda *g: phys(*gridmap(*g)))


def _adamw_math(w, g, m, v):
    mn = ADAM_B1 * m + (1.0 - ADAM_B1) * g
    vn = ADAM_B2 * v + (1.0 - ADAM_B2) * jnp.square(g)
    m_hat = mn / (1.0 - ADAM_B1 ** ADAM_STEP)
    v_hat = vn / (1.0 - ADAM_B2 ** ADAM_STEP)
    return -ADAM_LR * (m_hat / (jnp.sqrt(v_hat) + ADAM_EPS) + ADAM_WD * w), mn, vn


def _matmul(name, a, b, mode, outs, tm, tn, tk, epilogue=None, extras=(), deps=()):
    if mode == "nn":
        m, k, n = a.rows, a.cols, b.cols
        a_spec = a.spec(tm, tk, lambda i, j, kk: (i, kk))
        b_spec = b.spec(tk, tn, lambda i, j, kk: (kk, j))
        dims = (((1,), (0,)), ((), ()))
    elif mode == "nt":
        m, k, n = a.rows, a.cols, b.rows
        a_spec = a.spec(tm, tk, lambda i, j, kk: (i, kk))
        b_spec = b.spec(tn, tk, lambda i, j, kk: (j, kk))
        dims = (((1,), (1,)), ((), ()))
    else:
        k, m, n = a.rows, a.cols, b.cols
        a_spec = a.spec(tk, tm, lambda i, j, kk: (kk, i))
        b_spec = b.spec(tk, tn, lambda i, j, kk: (kk, j))
        dims = (((0,), (0,)), ((), ()))
    assert m % tm == 0 and n % tn == 0 and k % tk == 0, (name, m, n, k, tm, tn, tk)
    grid = (m // tm, n // tn, k // tk)
    nk = grid[2]
    n_ex, n_out, n_dep = len(extras), len(outs), len(deps)
    tile = lambda i, j, kk: (i, j)

    def finish(z, ex, out_refs):
        vals = epilogue(z, *[e[...] for e in ex]) if epilogue is not None else (z,)
        for o, v in zip(out_refs, vals):
            o[...] = v.astype(o.dtype)

    def body_single(a_ref, b_ref, *rest):
        finish(lax.dot_general(a_ref[...], b_ref[...], dims, preferred_element_type=F32),
               rest[:n_ex], rest[n_ex + n_dep:n_ex + n_dep + n_out])

    def body_acc(a_ref, b_ref, *rest):
        acc = rest[-1]
        kk = pl.program_id(2)

        @pl.when(kk == 0)
        def _():
            acc[...] = jnp.zeros_like(acc)

        acc[...] += lax.dot_general(a_ref[...], b_ref[...], dims, preferred_element_type=F32)

        @pl.when(kk == nk - 1)
        def _():
            finish(acc[...], rest[:n_ex], rest[n_ex + n_dep:n_ex + n_dep + n_out])

    res = pl.pallas_call(
        body_single if nk == 1 else body_acc, name=name, grid=grid,
        in_specs=[a_spec, b_spec] + [e.spec(tm, tn, tile) for e in extras]
        + [pl.BlockSpec(memory_space=pl.ANY) for _ in deps],
        out_specs=[o.spec(tm, tn, tile) for o in outs],
        out_shape=[o.sds() for o in outs],
        scratch_shapes=[] if nk == 1 else [pltpu.VMEM((tm, tn), F32)],
        compiler_params=_params(("parallel", "parallel", "arbitrary")),
    )(a.arr, b.arr, *[e.arr for e in extras], *deps)
    return res


def _out(rows, cols, dtype, kind="plain", lead=(), shape=None):
    return Mat(None, rows, cols, kind, lead, shape=shape if shape is not None else (rows, cols), dtype=dtype)


def _rt(arr, tr, width=None, cb=0):
    width = arr.shape[1] if width is None else width
    return arr, pl.BlockSpec((tr, width), lambda i: (i, cb))


def _whole(arr):
    nd = arr.ndim
    return arr, pl.BlockSpec(arr.shape, lambda i: (0,) * nd)


def _rowwise(name, fn, n_steps, ins, outs, accs=(), deps=(), fill=None):
    n_in, n_out, n_acc, n_dep = len(ins), len(outs), len(accs), len(deps)

    def body(*refs):
        vals = fn(*[r[...] for r in refs[:n_in]])
        if not isinstance(vals, (tuple, list)):
            vals = (vals,)
        for ref, v in zip(refs[n_in + n_dep:n_in + n_dep + n_out], vals[:n_out]):
            ref[...] = v.astype(ref.dtype)
        if n_acc:
            acc_refs = refs[n_in + n_dep + n_out:]

            @pl.when(pl.program_id(0) == 0)
            def _():
                for ref in acc_refs:
                    ref[...] = jnp.zeros_like(ref)

            for ref, v in zip(acc_refs, vals[n_out:]):
                ref[...] += v

    acc_specs = [pl.BlockSpec(s.shape, lambda i, nd=len(s.shape): (0,) * nd) for s in accs]
    res = pl.pallas_call(
        body, name=name, grid=(n_steps,),
        in_specs=[s for _, s in ins] + [pl.BlockSpec(memory_space=pl.ANY) for _ in deps],
        out_specs=[s for _, s in outs] + acc_specs,
        out_shape=[o for o, _ in outs] + list(accs),
        input_output_aliases={} if fill is None else {n_in + fill[0]: fill[1]},
        compiler_params=_params(("arbitrary",) if n_acc else ("parallel",)),
    )(*[a for a, _ in ins], *deps)
    return res


def _rt_out(t, width, dtype, tr):
    return jax.ShapeDtypeStruct((t, width), dtype), pl.BlockSpec((tr, width), lambda i: (i, 0))


def _rms(x, g):
    r = lax.rsqrt(jnp.mean(x * x, axis=-1, keepdims=True) + EPS)
    return x * r * g


def _rms_bwd(dy, x, g):
    r = lax.rsqrt(jnp.mean(x * x, axis=-1, keepdims=True) + EPS)
    xh = x * r
    dxh = dy * g
    dx = r * (dxh - xh * jnp.mean(dxh * xh, axis=-1, keepdims=True))
    dg = jnp.sum(dy * xh, axis=0, keepdims=True)
    return dx, dg


def _gelu(x):
    k = 0.7978845608028654
    th = jnp.tanh(k * (x + 0.044715 * (x * x * x)))
    return x * (0.5 * (1.0 + th))


def _gelu_grad(x):
    k = 0.7978845608028654
    x2 = x * x
    th = jnp.tanh(k * (x + 0.044715 * (x2 * x)))
    return 0.5 * (1.0 + th) + 0.5 * x * (1.0 - th * th) * (k * (1.0 + 3.0 * 0.044715 * x2))


def _norm_fwd(name, x, g, tr):
    t, d = x.shape
    return _rowwise(name, lambda xv, gv: _rms(xv, gv), t // tr, [_rt(x, tr), _whole(g)], [_rt_out(t, d, BF16, tr)])[0]


def _norm_bwd(name, dh, x, g, dres, tr):
    t, d = x.shape

    def fn(dhv, xv, gv, drv):
        dx, dg = _rms_bwd(dhv, xv, gv)
        dx = dx + drv
        return dx, dx, dg

    return _rowwise(name, fn, t // tr, [_rt(dh, tr), _rt(x, tr), _whole(g), _rt(dres, tr)],
                    [_rt_out(t, d, F32, tr), _rt_out(t, d, BF16, tr)], [jax.ShapeDtypeStruct((1, d), F32)])


def _rope_tables(posf, invf, cmask, smask, tr):
    t = posf.shape[0]

    def fn(p, f, cm, sm):
        ang = p * f
        return jnp.cos(ang) * cm, jnp.sin(ang) * sm

    return _rowwise("rope_tables", fn, t // tr, [_rt(posf, tr), _whole(invf), _whole(cmask), _whole(smask)],
                    [_rt_out(t, LANES, F32, tr), _rt_out(t, LANES, F32, tr)])


def _rot(v, c, s):
    return v * c + pltpu.roll(v, ROPE, axis=1) * s


def _rot_bwd(dv, c, s):
    return dv * c + pltpu.roll(dv * s, ROPE, axis=1)


def _rope_fwd(qfull, proj, kr_cb, ctab, stab, heads, tr):
    t = qfull.shape[0]
    hw = heads * LANES

    def fn(q, kr, c, s):
        parts = [q[:, :hw]] + [_rot(q[:, hw + h * LANES: hw + (h + 1) * LANES], c, s) for h in range(heads)]
        return jnp.concatenate(parts, axis=1), _rot(kr, c, s)

    return _rowwise("rope_fwd", fn, t // tr, [_rt(qfull, tr), _rt(proj, tr, LANES, kr_cb), _rt(ctab, tr), _rt(stab, tr)],
                    [_rt_out(t, 2 * hw, BF16, tr), _rt_out(t, LANES, BF16, tr)])


def _rope_bwd(dq1, dq2, dkr_h, ctab, stab, heads, tr, dproj, kr_cb):
    t = dq1.shape[0]
    hw = heads * LANES

    def fn(a, b, dk, c, s):
        parts = [a] + [_rot_bwd(b[:, h * LANES:(h + 1) * LANES], c, s) for h in range(heads)]
        dks = dk[0]
        for h in range(1, heads):
            dks = dks + dk[h]
        return jnp.concatenate(parts, axis=1), _rot_bwd(dks, c, s)

    dk_spec = pl.BlockSpec((heads, tr, LANES), lambda i: (0, i, 0))
    into = (jax.ShapeDtypeStruct(dproj.shape, dproj.dtype), pl.BlockSpec((tr, LANES), lambda i: (i, kr_cb)))
    return _rowwise("rope_bwd", fn, t // tr, [_rt(dq1, tr), _rt(dq2, tr), (dkr_h, dk_spec), _rt(ctab, tr), _rt(stab, tr)],
                    [_rt_out(t, 2 * hw, BF16, tr), into], deps=(dproj,), fill=(0, 1))


def _dot_nt(a, b):
    return lax.dot_general(a, b, (((1,), (1,)), ((), ())), preferred_element_type=F32)


def _dot_tn(a, b):
    return lax.dot_general(a, b, (((0,), (0,)), ((), ())), preferred_element_type=F32)


def _dot(a, b):
    return jnp.dot(a, b, preferred_element_type=F32)


def _ranges(n_blocks):
    n_var = min(4, n_blocks)
    assert n_blocks % n_var == 0
    return n_var, n_blocks // n_var


def _row_of(col):
    return col.T[:8, :]


def _attn_fwd(qall, kvall, kr, heads, scale, tq):
    t = qall.shape[0]
    nq = t // tq
    n_var, per = _ranges(nq)

    def body(qn_ref, qr_ref, kn_ref, v_ref, kr_ref, o_ref, lser_ref):
        i = pl.program_id(1)
        for var in range(n_var):
            kv = (var + 1) * per * tq

            @pl.when(jnp.logical_and(i >= var * per, i < (var + 1) * per))
            def _(kv=kv):
                s = _dot_nt(jnp.concatenate([qn_ref[...], qr_ref[...]], axis=1),
                            jnp.concatenate([kn_ref[:kv, :], kr_ref[:kv, :]], axis=1)) * scale
                rows = i * tq + lax.broadcasted_iota(jnp.int32, (tq, kv), 0)
                cols = lax.broadcasted_iota(jnp.int32, (tq, kv), 1)
                s = jnp.where(cols <= rows, s, NEG)
                m = jnp.max(s, axis=-1, keepdims=True)
                p = jnp.exp(s - m)
                l = jnp.sum(p, axis=-1, keepdims=True)
                o_ref[...] = _dot(p.astype(BF16), v_ref[:kv, :]) / l
                lser_ref[...] = _row_of(jnp.broadcast_to(m + jnp.log(l), (tq, LANES)))

    return pl.pallas_call(
        body, name="attn_fwd", grid=(heads, nq),
        in_specs=[pl.BlockSpec((tq, LANES), lambda h, i: (i, h)),
                  pl.BlockSpec((tq, LANES), lambda h, i: (i, heads + h)),
                  pl.BlockSpec((t, LANES), lambda h, i: (0, h)),
                  pl.BlockSpec((t, LANES), lambda h, i: (0, heads + h)),
                  pl.BlockSpec((t, LANES), lambda h, i: (0, 0))],
        out_specs=[pl.BlockSpec((tq, LANES), lambda h, i: (i, h)),
                   pl.BlockSpec((None, 8, tq), lambda h, i: (h, 0, i))],
        out_shape=[jax.ShapeDtypeStruct((t, heads * LANES), F32), jax.ShapeDtypeStruct((heads, 8, t), F32)],
        compiler_params=_params(("parallel", "parallel")),
    )(qall, qall, kvall, kvall, kr)


def _attn_bwd(qall, kvall, kr, do, lse_row, delta_row, heads, scale, tk):
    t = qall.shape[0]
    nk = t // tk
    n_var, per = _ranges(nk)

    def body(qn_ref, qr_ref, kn_ref, v_ref, kr_ref, do_ref, lse_ref, dl_ref, dq1_ref, dq2_ref, dk_ref, dv_ref, dkr_ref):
        j = pl.program_id(1)

        @pl.when(j == 0)
        def _():
            dq1_ref[...] = jnp.zeros_like(dq1_ref)
            dq2_ref[...] = jnp.zeros_like(dq2_ref)

        for var in range(n_var):
            q0 = var * per * tk
            nq = t - q0

            @pl.when(jnp.logical_and(j >= var * per, j < (var + 1) * per))
            def _(q0=q0, nq=nq):
                qn, qr, do_v = qn_ref[q0:, :], qr_ref[q0:, :], do_ref[q0:, :]
                k1, k2 = kn_ref[...], kr_ref[...]
                qcat, kcat = jnp.concatenate([qn, qr], axis=1), jnp.concatenate([k1, k2], axis=1)
                st = _dot_nt(kcat, qcat) * scale
                keys = j * tk + lax.broadcasted_iota(jnp.int32, (tk, nq), 0)
                queries = q0 + lax.broadcasted_iota(jnp.int32, (tk, nq), 1)
                pt = jnp.where(keys <= queries, jnp.exp(st - lse_ref[0:1, q0:]), 0.0)
                dpt = _dot_nt(v_ref[...], do_v)
                dst = (pt * (dpt - dl_ref[0:1, q0:]) * scale).astype(BF16)
                dv_ref[...] = _dot(pt.astype(BF16), do_v).astype(dv_ref.dtype)
                dkc = _dot(dst, qcat)
                dk_ref[...] = dkc[:, :LANES].astype(dk_ref.dtype)
                dkr_ref[...] = dkc[:, LANES:]
                dqc = _dot_tn(dst, kcat)
                dq1_ref[q0:, :] += dqc[:, :LANES]
                dq2_ref[q0:, :] += dqc[:, LANES:]

    kblk = lambda off: pl.BlockSpec((tk, LANES), lambda h, j: (j, off + h))
    full = lambda off: pl.BlockSpec((t, LANES), lambda h, j: (0, off + h))
    stat = pl.BlockSpec((None, 8, t), lambda h, j: (h, 0, 0))
    return pl.pallas_call(
        body, name="attn_bwd", grid=(heads, nk),
        in_specs=[full(0), full(heads), kblk(0), kblk(heads), pl.BlockSpec((tk, LANES), lambda h, j: (j, 0)),
                  full(0), stat, stat],
        out_specs=[full(0), full(0), kblk(0), kblk(0), pl.BlockSpec((None, tk, LANES), lambda h, j: (h, j, 0))],
        out_shape=[jax.ShapeDtypeStruct((t, heads * LANES), F32)] * 2 + [jax.ShapeDtypeStruct((t, heads * LANES), BF16)] * 2
        + [jax.ShapeDtypeStruct((heads, t, LANES), F32)],
        compiler_params=_params(("parallel", "arbitrary")),
    )(qall, qall, kvall, kvall, kr, do, lse_row, delta_row)


def _tril():
    return lax.broadcasted_iota(jnp.int32, (LANES, LANES), 0) >= lax.broadcasted_iota(jnp.int32, (LANES, LANES), 1)


def _group_norm(vg):
    mu = jnp.mean(vg, axis=-1, keepdims=True)
    vc = vg - mu
    rs = lax.rsqrt(jnp.mean(vc * vc, axis=-1, keepdims=True) + EPS)
    return vc * rs, rs


def _sgu_fwd(proj, gain, w, bias, groups, rb):
    t = proj.shape[0]
    gw = groups * LANES
    cpb = rb // LANES

    def body(u_ref, v_ref, gain_ref, w_ref, b_ref, s_ref):
        tril = _tril()
        for g in range(groups):
            wt = jnp.where(tril, w_ref[g], 0.0).astype(BF16)
            cols = slice(g * LANES, (g + 1) * LANES)
            for ci in range(cpb):
                rows = slice(ci * LANES, (ci + 1) * LANES)
                ug = _gelu(u_ref[rows, cols])
                vh, _ = _group_norm(_gelu(v_ref[rows, cols]))
                vn = vh * gain_ref[:, cols]
                y = _dot(wt, vn.astype(BF16)) + b_ref[g]
                s_ref[rows, cols] = ug * y

    return pl.pallas_call(
        body, name="sgu_fwd", grid=(t // rb,),
        in_specs=[pl.BlockSpec((rb, gw), lambda i: (i, 0)), pl.BlockSpec((rb, gw), lambda i: (i, 1)),
                  pl.BlockSpec((1, gw), lambda i: (0, 0)),
                  pl.BlockSpec((groups, LANES, LANES), lambda i: (0, 0, 0)),
                  pl.BlockSpec((groups, LANES, LANES), lambda i: (0, 0, 0))],
        out_specs=pl.BlockSpec((rb, gw), lambda i: (i, 0)),
        out_shape=jax.ShapeDtypeStruct((t, gw), F32),
        compiler_params=_params(("parallel",)),
    )(proj, proj, gain, w, bias)


def _sgu_bwd(proj, ds, gain, w, bias, groups, rb):
    t, width = proj.shape
    gw = groups * LANES
    cpb = rb // LANES
    n_steps = t // rb

    def body(u_ref, v_ref, ds_ref, gain_ref, w_ref, b_ref, dp_ref, dw_ref, db_ref, dg_ref, dy_acc):
        du_ref, dv_ref = dp_ref.at[:, :gw], dp_ref.at[:, gw:]
        step = pl.program_id(0)

        @pl.when(step == 0)
        def _():
            dw_ref[...] = jnp.zeros_like(dw_ref)
            dy_acc[...] = jnp.zeros_like(dy_acc)
            dg_ref[...] = jnp.zeros_like(dg_ref)

        tril = _tril()
        for g in range(groups):
            wt = jnp.where(tril, w_ref[g], 0.0).astype(BF16)
            cols = slice(g * LANES, (g + 1) * LANES)
            gain_g = gain_ref[:, cols]
            for ci in range(cpb):
                rows = slice(ci * LANES, (ci + 1) * LANES)
                u_raw, v_raw, ds_v = u_ref[rows, cols], v_ref[rows, cols], ds_ref[rows, cols]
                ug = _gelu(u_raw)
                vh, rs = _group_norm(_gelu(v_raw))
                vn = (vh * gain_g).astype(BF16)
                y = _dot(wt, vn) + b_ref[g]
                dy = ds_v * ug
                dyb = dy.astype(BF16)
                du_ref[rows, cols] = (ds_v * y * _gelu_grad(u_raw)).astype(du_ref.dtype)
                dy_acc[g] += dy
                dw_ref[g] += _dot_nt(dyb, vn)
                dvn = _dot_tn(wt, dyb)
                dg_ref[:, cols] += jnp.sum(dvn * vh, axis=0, keepdims=True)
                dvh = dvn * gain_g
                dvg = rs * (dvh - jnp.mean(dvh, axis=-1, keepdims=True)
                            - vh * jnp.mean(dvh * vh, axis=-1, keepdims=True))
                dv_ref[rows, cols] = (dvg * _gelu_grad(v_raw)).astype(dv_ref.dtype)

        @pl.when(step == n_steps - 1)
        def _():
            ones = jnp.ones((8, LANES), F32)
            for g in range(groups):
                dw_ref[g] = jnp.where(tril, dw_ref[g], 0.0)
                db_ref[g] = lax.dot_general(ones, dy_acc[g], (((1,), (1,)), ((), ())),
                                            precision=lax.Precision.HIGHEST, preferred_element_type=F32)

    blk = lambda cb: pl.BlockSpec((rb, gw), lambda i: (i, cb))
    whole3 = pl.BlockSpec((groups, LANES, LANES), lambda i: (0, 0, 0))
    return pl.pallas_call(
        body, name="sgu_bwd", grid=(n_steps,),
        in_specs=[blk(0), blk(1), blk(0), pl.BlockSpec((1, gw), lambda i: (0, 0)), whole3, whole3],
        out_specs=[pl.BlockSpec((rb, 2 * gw), lambda i: (i, 0)), whole3,
                   pl.BlockSpec((groups, 8, LANES), lambda i: (0, 0, 0)), pl.BlockSpec((1, gw), lambda i: (0, 0))],
        out_shape=[jax.ShapeDtypeStruct((t, width), BF16),
                   jax.ShapeDtypeStruct((groups, LANES, LANES), F32), jax.ShapeDtypeStruct((groups, 8, LANES), F32),
                   jax.ShapeDtypeStruct((1, gw), F32)],
        scratch_shapes=[pltpu.VMEM((groups, LANES, LANES), F32)],
        compiler_params=_params(("arbitrary",)),
    )(proj, proj, ds, gain, w, bias)


def _shift_down(z, s):
    rows = lax.broadcasted_iota(jnp.int32, z.shape, 0)
    return jnp.where(rows >= s, pltpu.roll(z, s, axis=0), 0.0)


def _shift_up(z, s):
    n = z.shape[0]
    rows = lax.broadcasted_iota(jnp.int32, z.shape, 0)
    return jnp.where(rows < n - s, pltpu.roll(z, n - s, axis=0), 0.0)


def _conv_fwd(proj3, cw, tc):
    _, t, cd = proj3.shape

    def body(p_ref, w_ref, o_ref):
        z = p_ref[1] * p_ref[2]
        w = w_ref[...]
        zc = w[2:3] * z + w[1:2] * _shift_down(z, 1) + w[0:1] * _shift_down(z, 2)
        o_ref[...] = (p_ref[0] * zc).astype(o_ref.dtype)

    return pl.pallas_call(
        body, name="conv_fwd", grid=(cd // tc,),
        in_specs=[pl.BlockSpec((3, t, tc), lambda j: (0, 0, j)), pl.BlockSpec((8, tc), lambda j: (0, j))],
        out_specs=pl.BlockSpec((t, tc), lambda j: (0, j)),
        out_shape=jax.ShapeDtypeStruct((t, cd), BF16),
        compiler_params=_params(("parallel",)),
    )(proj3, cw)


def _conv_bwd(proj3, cw, dbz, tc):
    _, t, cd = proj3.shape

    def body(p_ref, w_ref, d_ref, o_ref, dw_ref):
        b, c, xin = p_ref[0], p_ref[1], p_ref[2]
        w = w_ref[...]
        z = c * xin
        z1, z2 = _shift_down(z, 1), _shift_down(z, 2)
        zc = w[2:3] * z + w[1:2] * z1 + w[0:1] * z2
        d = d_ref[...]
        dzc = d * b
        dz = w[2:3] * dzc + w[1:2] * _shift_up(dzc, 1) + w[0:1] * _shift_up(dzc, 2)
        o_ref[0] = (d * zc).astype(o_ref.dtype)
        o_ref[1] = (dz * xin).astype(o_ref.dtype)
        o_ref[2] = (dz * c).astype(o_ref.dtype)
        row = lax.broadcasted_iota(jnp.int32, (8, tc), 0)
        dw0 = jnp.sum(dzc * z2, axis=0, keepdims=True)
        dw1 = jnp.sum(dzc * z1, axis=0, keepdims=True)
        dw2 = jnp.sum(dzc * z, axis=0, keepdims=True)
        dw_ref[...] = jnp.where(row == 0, dw0, 0.0) + jnp.where(row == 1, dw1, 0.0) + jnp.where(row == 2, dw2, 0.0)

    return pl.pallas_call(
        body, name="conv_bwd", grid=(cd // tc,),
        in_specs=[pl.BlockSpec((3, t, tc), lambda j: (0, 0, j)), pl.BlockSpec((8, tc), lambda j: (0, j)),
                  pl.BlockSpec((t, tc), lambda j: (0, j))],
        out_specs=[pl.BlockSpec((3, t, tc), lambda j: (0, 0, j)), pl.BlockSpec((8, tc), lambda j: (0, j))],
        out_shape=[jax.ShapeDtypeStruct((3, t, cd), BF16), jax.ShapeDtypeStruct((8, cd), F32)],
        compiler_params=_params(("parallel",)),
    )(proj3, cw, dbz)


def _place():
    x, y, c = lax.axis_index("x"), lax.axis_index("y"), lax.axis_index("c")
    chips = [(1 - x, y), (x, 1 - y), (1 - x, 1 - y)]
    return x, y, c, chips


def _any_specs(n):
    return [pl.BlockSpec(memory_space=pl.ANY) for _ in range(n)]


HBM_SPEC = pl.BlockSpec(memory_space=pltpu.HBM)
SEM_SPEC = pl.BlockSpec(memory_space=pltpu.SEMAPHORE)
ORDERED_EFFECT = pltpu.SideEffectType.DATAFLOW_SIDE_EFFECTING


def _in_hbm(a):
    return pltpu.with_memory_space_constraint(a, pltpu.HBM)


def _token():
    return jax.ShapeDtypeStruct((8, LANES), F32), pl.BlockSpec(memory_space=pltpu.VMEM)


def _gather_start(name, groups):
    sizes = [len(g) for g in groups]
    flat = [b for g in groups for b in g]
    n, ng = len(flat), len(groups)

    def body(*refs):
        ins, sems, token = refs[:n], refs[n:n + 2 * ng], refs[-1]
        x, y, c, chips = _place()
        me = 2 * x + y
        i = 0
        for gi, size in enumerate(sizes):
            for j in range(size):
                blk = ins[i].at[me, c]
                for k, chip in enumerate(chips):
                    pltpu.make_async_remote_copy(src_ref=blk, dst_ref=blk, send_sem=sems[2 * gi].at[3 * j + k],
                                                 recv_sem=sems[2 * gi + 1].at[3 * j + k],
                                                 device_id=(*chip, c), device_id_type=MESH).start()
                i += 1
        token[...] = jnp.zeros_like(token)

    tok_shape, tok_spec = _token()
    res = pl.pallas_call(
        body, name=name,
        in_specs=[HBM_SPEC] * n,
        out_specs=[SEM_SPEC] * (2 * ng) + [HBM_SPEC] * n + [tok_spec],
        out_shape=[pltpu.SemaphoreType.DMA((3 * size,)) for size in sizes for _ in (0, 1)]
        + [pltpu.HBM(b.shape, b.dtype) for b in flat] + [tok_shape],
        input_output_aliases={i: 2 * ng + i for i in range(n)},
        compiler_params=pltpu.CompilerParams(has_side_effects=ORDERED_EFFECT),
    )(*[_in_hbm(b) for b in flat])
    out, i = [], 2 * ng
    for gi, size in enumerate(sizes):
        out.append((res[2 * gi], res[2 * gi + 1], list(res[i:i + size])))
        i += size
    return out, res[-1]


def _gather_wait(tag, send, recv, bufs, after):
    n = len(bufs)
    after = tuple(after) if isinstance(after, (tuple, list)) else (after,)

    def body(*refs):
        ins, send_ref, recv_ref = refs[:n], refs[n], refs[n + 1]
        x, y, c, chips = _place()
        me = 2 * x + y
        for j in range(n):
            for k, (px, py) in enumerate(chips):
                cp = pltpu.make_async_remote_copy(src_ref=ins[j].at[me, c], dst_ref=ins[j].at[2 * px + py, c],
                                                  send_sem=send_ref.at[3 * j + k], recv_sem=recv_ref.at[3 * j + k],
                                                  device_id=(px, py, c), device_id_type=MESH)
                cp.wait_send()
                cp.wait_recv()

    return pl.pallas_call(
        body, name="gather_wait_" + tag,
        in_specs=[HBM_SPEC] * n + [SEM_SPEC, SEM_SPEC] + _any_specs(len(after)),
        out_specs=[HBM_SPEC] * n,
        out_shape=[pltpu.HBM(b.shape, b.dtype) for b in bufs],
        input_output_aliases={i: i for i in range(n)},
        compiler_params=pltpu.CompilerParams(has_side_effects=ORDERED_EFFECT),
    )(*bufs, send, recv, *after)


def _gather_forward(tag, bufs):
    n = len(bufs)

    def body(*refs):
        ins, outs = refs[:n], refs[n:2 * n]
        send, recv = refs[2 * n:]
        x, y, c, chips = _place()
        sib = (x, y, 1 - c)

        def cp(i, k, slot, half):
            return pltpu.make_async_remote_copy(src_ref=ins[i].at[slot, half], dst_ref=outs[i].at[slot, half],
                                                send_sem=send.at[3 * i + k], recv_sem=recv.at[3 * i + k],
                                                device_id=sib, device_id_type=MESH)

        cps = [cp(i, k, 2 * px + py, c) for i in range(n) for k, (px, py) in enumerate(chips)]
        for d in cps:
            d.start()
        for i in range(n):
            for k, (px, py) in enumerate(chips):
                cp(i, k, 2 * px + py, 1 - c).wait_recv()
        for d in cps:
            d.wait_send()

    return pl.pallas_call(
        body, name="gather_forward_" + tag,
        in_specs=_any_specs(n), out_specs=_any_specs(n),
        out_shape=[jax.ShapeDtypeStruct(b.shape, b.dtype) for b in bufs],
        scratch_shapes=[pltpu.SemaphoreType.DMA((3 * n,))] * 2,
        input_output_aliases={i: i for i in range(n)},
        compiler_params=pltpu.CompilerParams(has_side_effects=True),
    )(*bufs)


def _pair_route(srcs, zones):
    x, y, c, _ = _place()
    return [(srcs[i].at[j, 1 - c], zones[i].at[j], (x, y, 1 - c)) for i in range(len(srcs)) for j in range(N_CHIPS)]


def _slab_route(srcs, zones):
    x, y, c, _ = _place()
    return [(srcs[i].at[j], zones[i].at[j], (x, y, 1 - c)) for i in range(len(srcs)) for j in range(N_CHIPS)]


def _chip_route(srcs, zones):
    x, y, c, chips = _place()
    return [(srcs[i].at[2 * px + py], zones[i].at[k], (px, py, c)) for i in range(len(srcs)) for k, (px, py) in enumerate(chips)]


def _all_route(srcs, zones):
    x, y, c, _ = _place()
    flips = [(fx, fy, fc) for fx in (0, 1) for fy in (0, 1) for fc in (0, 1)][1:]
    return [(srcs[0], zones[0].at[4 * x + 2 * y + c], (x + fx - 2 * x * fx, y + fy - 2 * y * fy, c + fc - 2 * c * fc))
            for fx, fy, fc in flips]


def _share_route(srcs, zones):
    x, y, c, _ = _place()
    return [(s.at[c], s.at[c], (x, y, 1 - c)) for s in srcs]


def _exchange_start(name, route, n_copies, srcs, zones):
    n, nz = len(srcs), len(zones)
    lands = [lax.empty(z, a.dtype) if isinstance(z, tuple) else z for z, a in zip(zones, srcs)]

    def body(*refs):
        ins, zone_refs, send, recv, token = refs[:n], refs[n:n + nz], refs[n + nz], refs[n + nz + 1], refs[-1]
        for k, (src, dst, dev) in enumerate(route(ins, zone_refs)):
            pltpu.make_async_remote_copy(src_ref=src, dst_ref=dst, send_sem=send.at[k], recv_sem=recv.at[k],
                                         device_id=dev, device_id_type=MESH).start()
        token[...] = jnp.zeros_like(token)

    tok_shape, tok_spec = _token()
    res = pl.pallas_call(
        body, name=name,
        in_specs=[HBM_SPEC] * (n + nz),
        out_specs=[SEM_SPEC, SEM_SPEC] + [HBM_SPEC] * (n + nz) + [tok_spec],
        out_shape=[pltpu.SemaphoreType.DMA((n_copies,))] * 2 + [pltpu.HBM(a.shape, a.dtype) for a in srcs + lands]
        + [tok_shape],
        input_output_aliases={i: 2 + i for i in range(n + nz)},
        compiler_params=pltpu.CompilerParams(has_side_effects=ORDERED_EFFECT),
    )(*[_in_hbm(a) for a in srcs + lands])
    return (res[0], res[1], list(res[2:2 + n]), list(res[2 + n:2 + n + nz])), res[-1]


def _exchange_wait(name, route, started, after):
    send, recv, srcs, lands = started
    n, nz = len(srcs), len(lands)
    after = tuple(after) if isinstance(after, (tuple, list)) else (after,)

    def body(*refs):
        ins, zone_refs, send_ref, recv_ref = refs[:n], refs[n:n + nz], refs[n + nz], refs[n + nz + 1]
        for k, (src, dst, dev) in enumerate(route(ins, zone_refs)):
            cp = pltpu.make_async_remote_copy(src_ref=src, dst_ref=dst, send_sem=send_ref.at[k], recv_sem=recv_ref.at[k],
                                              device_id=dev, device_id_type=MESH)
            cp.wait_send()
            cp.wait_recv()

    res = pl.pallas_call(
        body, name=name,
        in_specs=[HBM_SPEC] * (n + nz) + [SEM_SPEC, SEM_SPEC] + _any_specs(len(after)),
        out_specs=[HBM_SPEC] * (n + nz),
        out_shape=[pltpu.HBM(a.shape, a.dtype) for a in srcs + lands],
        input_output_aliases={i: i for i in range(n + nz)},
        compiler_params=pltpu.CompilerParams(has_side_effects=ORDERED_EFFECT),
    )(*srcs, *lands, send, recv, *after)
    return list(res[:n]), list(res[n:])


def _spread(v):
    rows, cols = v.shape
    tr = _row_tile(rows, cols, budget=256 * 1024)

    def body(v_ref, o_ref):
        o_ref[...] = jnp.broadcast_to(v_ref[...][None], o_ref.shape)

    return pl.pallas_call(body, name="spread_small_grads", grid=(rows // tr,),
                          in_specs=[pl.BlockSpec((tr, cols), lambda r: (r, 0))],
                          out_specs=pl.BlockSpec((8, tr, cols), lambda r: (0, r, 0)),
                          out_shape=jax.ShapeDtypeStruct((8, rows, cols), v.dtype),
                          compiler_params=_params(("parallel",)))(v)


def _row_tile(rows, cols, itemsize=4, budget=2 * 1024 * 1024, step=8):
    best = None
    for t in range(step, rows + 1, step):
        if rows % t == 0 and t * cols * itemsize <= budget:
            best = t
    return best if best is not None else rows


def _my_chip():
    return 2 * lax.axis_index("x") + lax.axis_index("y")


def _pair_sum(g5, gsib):
    _, _, rh, cols = g5.shape
    tr = _row_tile(rh, cols, step=16)

    def body(a_ref, b_ref, o_ref):
        o_ref[...] = (a_ref[...].astype(F32) + b_ref[...].astype(F32)).astype(o_ref.dtype)

    return pl.pallas_call(body, name="grad_pair_sum", grid=(N_CHIPS, rh // tr),
                          in_specs=[pl.BlockSpec((None, None, tr, cols), lambda j, r: (j, lax.axis_index("c"), r, 0)),
                                    pl.BlockSpec((None, tr, cols), lambda j, r: (j, r, 0))],
                          out_specs=pl.BlockSpec((None, tr, cols), lambda j, r: (j, r, 0)),
                          out_shape=jax.ShapeDtypeStruct((N_CHIPS, rh, cols), BF16),
                          compiler_params=_params(("parallel", "parallel")))(g5, gsib)


def _chip_sum(part, recv):
    _, rh, cols = part.shape
    tr = _row_tile(rh, cols, step=16)

    def body(a_ref, b_ref, o_ref):
        acc = a_ref[...].astype(F32)
        for k in range(3):
            acc = acc + b_ref[k].astype(F32)
        o_ref[...] = acc

    return pl.pallas_call(body, name="grad_chip_sum", grid=(rh // tr,),
                          in_specs=[pl.BlockSpec((None, tr, cols), lambda r: (_my_chip(), r, 0)),
                                    pl.BlockSpec((3, tr, cols), lambda r: (0, r, 0))],
                          out_specs=pl.BlockSpec((None, tr, cols), lambda r: (lax.axis_index("c"), r, 0)),
                          out_shape=jax.ShapeDtypeStruct((2, rh, cols), F32),
                          compiler_params=_params(("parallel",)))(part, recv)


def _sum_devices(g):
    _, rows, cols = g.shape
    tr = _row_tile(rows, cols, budget=256 * 1024)

    def body(g_ref, o_ref):
        acc = g_ref[0]
        for d in range(1, 8):
            acc = acc + g_ref[d]
        o_ref[...] = acc

    return pl.pallas_call(body, name="sum_small_grads", grid=(rows // tr,),
                          in_specs=[pl.BlockSpec((8, tr, cols), lambda r: (0, r, 0))],
                          out_specs=pl.BlockSpec((tr, cols), lambda r: (r, 0)),
                          out_shape=jax.ShapeDtypeStruct((rows, cols), F32),
                          compiler_params=_params(("parallel",)))(g)


def _place_shard(w, layer, dtype, deps=()):
    _, rows, cols = w.shape
    tr = _row_tile(rows, cols)

    def body(i_ref, *rest):
        o_ref = rest[-1]
        o_ref[...] = i_ref[...].astype(o_ref.dtype)

    out = pl.pallas_call(body, name="place_shard", grid=(rows // tr,),
                         in_specs=[pl.BlockSpec((None, tr, cols), lambda r: (layer, r, 0))] + _any_specs(len(deps)),
                         out_specs=pl.BlockSpec((None, tr, cols), lambda r: (_my_chip(), r, 0)),
                         out_shape=jax.ShapeDtypeStruct((N_CHIPS, rows, cols), dtype),
                         compiler_params=_params(("parallel",)))(w, *deps)
    return out.reshape(N_CHIPS, 2, rows // 2, cols)


def _adamw(w, gs, m, v):
    n_layers, rows, cols = w.shape
    tr = _row_tile(rows, cols)

    def body(w_ref, m_ref, v_ref, *rest):
        g_refs = rest[:n_layers]
        go_ref, d_ref, mo_ref, vo_ref = rest[n_layers:]
        gv = g_refs[0][...]
        for layer in range(1, n_layers):
            gv = jnp.where(pl.program_id(0) == layer, g_refs[layer][...], gv)
        d_ref[...], mo_ref[...], vo_ref[...] = _adamw_math(w_ref[...], gv, m_ref[...], v_ref[...])
        go_ref[...] = gv

    spec = pl.BlockSpec((None, tr, cols), lambda layer, r: (layer, r, 0))
    g_specs = [pl.BlockSpec((tr, cols), lambda layer, r, own=own: (jnp.where(layer == own, r, 0), 0))
               for own in range(n_layers)]
    return pl.pallas_call(body, name="adamw", grid=(n_layers, rows // tr), in_specs=[spec] * 3 + g_specs,
                          out_specs=[spec] * 4, out_shape=[jax.ShapeDtypeStruct((n_layers, rows, cols), F32)] * 4,
                          compiler_params=_params(("parallel", "parallel")))(w, m, v, *gs)


def _pad_rope(w):
    z = jnp.zeros(w.shape[:-1] + (ROPE_HALF,), w.dtype)
    return jnp.concatenate([w[..., :ROPE_HALF], z, w[..., ROPE_HALF:], z], axis=-1)


def _unpad_rope(g):
    return jnp.concatenate([g[..., :ROPE_HALF], g[..., ROPE:ROPE + ROPE_HALF]], axis=-1)


def _unstack_cols(s):
    n, r, cs = s.shape
    return jnp.transpose(s, (1, 0, 2)).reshape(r, n * cs)


def _stack_cols(f):
    r, cfull = f.shape
    return jnp.transpose(f.reshape(r, N_CHIPS, cfull // N_CHIPS), (1, 0, 2))


def _small_shard(norm, conv):
    return jnp.concatenate([jnp.pad(norm, ((0, 15), (0, 0))), jnp.pad(conv, ((0, 13), (0, 0)))], axis=0)


def _flat_rows(a):
    return a.reshape(-1, LANES)


def _pack_small(arrs):
    return jnp.concatenate([_flat_rows(a.astype(F32)) for a in arrs], axis=0)


def _unpack_small(flat, like):
    out, r = [], 0
    for a in like:
        n = a.size // LANES
        out.append(flat[r:r + n].reshape(a.shape))
        r += n
    return out


def kernel(x, positions, e_norm_mix, e_w_in, e_q_norm, e_w_uq, e_kv_norm, e_w_ukv, e_v_norm, e_sgu_w, e_sgu_b, e_mla_out_norm, e_sgu_out_norm, e_w_out, o_norm_mix, o_w_in, o_conv_w, o_w_out, mlp_norm, mlp_w1, mlp_w2, final_norm, loss_target, m_e_norm_mix, m_e_w_in, m_e_q_norm, m_e_w_uq, m_e_kv_norm, m_e_w_ukv, m_e_v_norm, m_e_sgu_w, m_e_sgu_b, m_e_mla_out_norm, m_e_sgu_out_norm, m_e_w_out, m_o_norm_mix, m_o_w_in, m_o_conv_w, m_o_w_out, m_mlp_norm, m_mlp_w1, m_mlp_w2, m_final_norm, v_e_norm_mix, v_e_w_in, v_e_q_norm, v_e_w_uq, v_e_kv_norm, v_e_w_ukv, v_e_v_norm, v_e_sgu_w, v_e_sgu_b, v_e_mla_out_norm, v_e_sgu_out_norm, v_e_w_out, v_o_norm_mix, v_o_w_in, v_o_conv_w, v_o_w_out, v_mlp_norm, v_mlp_w1, v_mlp_w2, v_final_norm):
    t, d = x.shape[1], x.shape[2]
    ql, kvl = e_q_norm.shape[1], e_kv_norm.shape[1]
    groups = e_v_norm.shape[1]
    gw = groups * LANES
    heads = N_CHIPS * e_w_uq.shape[2] // (LANES + ROPE)
    hw = heads * LANES
    mix = hw + gw
    ei = N_CHIPS * e_w_in.shape[2]
    cd = N_CHIPS * o_conv_w.shape[2]
    ff = N_CHIPS * mlp_w1.shape[2]
    ffs = ff // N_CHIPS
    pi = 2 * gw + ql + kvl + LANES
    assert e_norm_mix.shape[0] == 1 and o_norm_mix.shape[0] == 1 and mlp_norm.shape[0] == 2
    assert ei == ql + kvl + ROPE + 2 * gw and cd == d and e_sgu_w.shape[2] == LANES
    assert (2 * gw) % ql == 0 and (2 * gw + ql) % kvl == 0 and t % LANES == 0
    scale = (LANES + ROPE) ** -0.5

    tr = min(256, t)
    tm = _pick(t, 1024, 8)
    kt, kd = _pick(t, 2048, 8), _pick(d, 2048)
    xs = x.reshape(t, d)
    tgt = loss_target.reshape(t, d)

    small_shard = _small_shard(o_norm_mix, o_conv_w[0])
    first, tok = _gather_start("gather_start_e", [
        [_place_shard(e_w_in, 0, BF16)],
        [_place_shard(e_w_uq, 0, BF16), _place_shard(e_w_ukv, 0, BF16), _place_shard(e_w_out, 0, BF16),
         _place_shard(small_shard[None], 0, F32)]])
    rest, tok = _gather_start("gather_start_rest", [
        [_place_shard(mlp_w1, 0, BF16, (tok,))], [_place_shard(mlp_w2, 0, BF16, (tok,))],
        [_place_shard(o_w_in, 0, BF16, (tok,)), _place_shard(o_w_out, 0, BF16, (tok,))],
        [_place_shard(mlp_w1, 1, BF16, (tok,))], [_place_shard(mlp_w2, 1, BF16, (tok,))]])
    started = first + rest

    def gathered(gi, tag, after):
        send, recv, bufs = started[gi]
        bufs = _gather_forward(tag, _gather_wait(tag, send, recv, bufs, after))
        return [b.reshape(N_CHIPS, 2 * b.shape[2], b.shape[3]) for b in bufs]

    g_e = e_norm_mix
    h0 = _norm_fwd("e_norm", xs, g_e, tr)
    inv_freq = ROPE_BASE ** (-jnp.arange(0, ROPE, 2, dtype=F32) / ROPE)
    zeros32 = jnp.zeros((ROPE_HALF,), F32)
    ones32 = jnp.ones((ROPE_HALF,), F32)
    invf = jnp.concatenate([inv_freq, zeros32, inv_freq, zeros32]).reshape(1, LANES)
    cmask = jnp.concatenate([ones32, zeros32, ones32, zeros32]).reshape(1, LANES)
    smask = jnp.concatenate([-ones32, zeros32, ones32, zeros32]).reshape(1, LANES)
    ctab, stab = _rope_tables(positions.reshape(t, 1).astype(F32), invf, cmask, smask, tr)

    w_in_g, = gathered(0, "e_in", (h0, ctab, tok))
    full = _unstack_cols(w_in_g)
    c2, c3 = ql + kvl, ql + kvl + ROPE
    w_in_all = jnp.concatenate([full[:, c3:], full[:, :c2], _pad_rope(full[:, c2:c3])], axis=1)
    proj, = _matmul("e_proj", Mat(h0, t, d), Mat(w_in_all, d, pi), "nn", [_out(t, pi, F32)], tm, _pick(pi, 1024), kd)

    w_uq_g, w_ukv_g, w_eout_g, small_g = gathered(1, "e", proj)
    full = _unstack_cols(w_uq_g).reshape(ql, heads, LANES + ROPE)
    w_q_all = jnp.concatenate([full[:, :, :LANES].reshape(ql, hw), _pad_rope(full[:, :, LANES:]).reshape(ql, hw)], axis=1)
    full = _unstack_cols(w_ukv_g).reshape(kvl, heads, 2 * LANES)
    w_kv_all = jnp.concatenate([full[:, :, :LANES].reshape(kvl, hw), full[:, :, LANES:].reshape(kvl, hw)], axis=1)
    w_eout = w_eout_g.reshape(mix, d)
    g_o = small_g[:, 0].reshape(1, d)
    conv_w = jnp.pad(jnp.transpose(small_g[:, 16:19], (1, 0, 2)).reshape(3, cd), ((0, 5), (0, 0)))

    g_q, g_kv = e_q_norm, e_kv_norm
    g_vn = e_v_norm.reshape(1, gw)
    sgu_w = e_sgu_w[0]
    sgu_b = jnp.broadcast_to(e_sgu_b[0][:, :, None], (groups, LANES, LANES))
    g_mla, g_sgu = e_mla_out_norm, e_sgu_out_norm
    g_m0, g_m1 = mlp_norm[0:1], mlp_norm[1:2]
    g_f = final_norm.reshape(1, d)

    def mlp_fwd(tag, xin, g, gi):
        hm = _norm_fwd("mlp_norm_" + tag, xin, g, tr)
        tn = _pick(ffs, 1024)
        w1 = Mat(gathered(gi, "w1_" + tag, hm)[0], d, ff, "colstack")
        a, act = _matmul("mlp_up_" + tag, Mat(hm, t, d), w1, "nn",
                         [_out(t, ff, BF16), _out(t, ff, BF16)], tm, tn, kd,
                         epilogue=lambda z: (jnp.maximum(z, 0.0), jnp.square(jnp.maximum(z, 0.0))))
        w2 = Mat(gathered(gi + 1, "w2_" + tag, act)[0].reshape(ff, d), ff, d)
        xo, = _matmul("mlp_down_" + tag, Mat(act, t, ff), w2, "nn",
                      [_out(t, d, F32)], tm, _pick(d, 1024), _pick(ffs, 2048),
                      epilogue=lambda z, r: (z + r,), extras=[Mat(xin, t, d)])
        return xo, hm, a, act, w1, w2

    def chip_start(tag, part):
        return _exchange_start("scatter_start_" + tag, _chip_route, 3 * len(part), part, [(3,) + p.shape[1:] for p in part])

    def pair_start(tag, stacked):
        g5 = [g.reshape(N_CHIPS, 2, g.shape[1] // 2, g.shape[2]) for g in stacked]
        return _exchange_start("pair_start_" + tag, _pair_route, N_CHIPS * len(g5), g5,
                               [(N_CHIPS,) + g.shape[2:] for g in g5])

    def pair_finish(tag, started, after):
        g5, from_sib = _exchange_wait("pair_wait_" + tag, _pair_route, started, after)
        return chip_start(tag, [_pair_sum(a, b) for a, b in zip(g5, from_sib)])

    def summed(tag, sc, after):
        part, lands = _exchange_wait("scatter_wait_" + tag, _chip_route, sc, after)
        half = [_chip_sum(p, r) for p, r in zip(part, lands)]
        return _exchange_start("share_start_" + tag, _share_route, len(half), half, [])

    def shared(tag, started, after):
        bufs, _ = _exchange_wait("share_wait_" + tag, _share_route, started, after)
        return [r.reshape(2 * r.shape[1], r.shape[2]) for r in bufs]

    def mlp_bwd(tag, dx, dxb, xin, g, w1, w2, hm, a, act, deps):
        tn = _pick(ffs, 1024)
        hr, hd = ffs // 2, d // 2
        dz, = _matmul("mlp_dact_" + tag, Mat(dxb, t, d), w2, "nt",
                      [_out(t, ff, BF16)], tm, tn, kd,
                      epilogue=lambda z, av: (z * (2.0 * av.astype(F32)),), extras=[Mat(a, t, ff)], deps=deps)

        def half(own):
            c = lax.axis_index("c")
            return c if own else 1 - c

        def act_half(own):
            return Mat(act, t, ff // 2, cmap=lambda cb, bc: (cb // (hr // bc)) * (ffs // bc) + half(own) * (hr // bc)
                       + cb % (hr // bc))

        def hm_half(own):
            return Mat(hm, t, hd, cmap=lambda cb, bc: cb + half(own) * (hd // bc))

        w1_out = lambda: _out(hd, ff, BF16, "colstack", (), (N_CHIPS, hd, ffs))
        theirs2, = _matmul("mlp_dw2_theirs_" + tag, act_half(False), Mat(dxb, t, d), "tn",
                           [_out(ff // 2, d, BF16)], _pick(hr, 1024), _pick(d, 2048), kt)
        theirs1, = _matmul("mlp_dw1_theirs_" + tag, hm_half(False), Mat(dz, t, ff), "tn",
                           [w1_out()], _pick(hd, 2048), tn, kt)
        sent = [theirs1, theirs2.reshape(N_CHIPS, hr, d)]
        started, tok = _exchange_start("pair_start_m" + tag, _slab_route, N_CHIPS * 2, sent, [s.shape for s in sent])
        dhm, = _matmul("mlp_dh_" + tag, Mat(dz, t, ff), w1, "nt",
                       [_out(t, d, F32)], tm, _pick(d, 1024), _pick(ffs, 2048), deps=(tok,))
        dxo, dxob, dg = _norm_bwd("mlp_norm_bwd_" + tag, dhm, xin, g, dx, tr)
        _, (sib1, sib2) = _exchange_wait("pair_wait_m" + tag, _slab_route, started, dxo)
        add = lambda z, s: (z + s.astype(F32),)
        part2, = _matmul("mlp_dw2_mine_" + tag, act_half(True), Mat(dxb, t, d), "tn",
                         [_out(ff // 2, d, BF16)], _pick(hr, 1024), _pick(d, 2048), kt,
                         epilogue=add, extras=[Mat(sib2.reshape(ff // 2, d), ff // 2, d)])
        part1, = _matmul("mlp_dw1_mine_" + tag, hm_half(True), Mat(dz, t, ff), "tn",
                         [w1_out()], _pick(hd, 2048), tn, kt, epilogue=add, extras=[Mat(sib1, hd, ff, "colstack")])
        sc, tok = chip_start("m" + tag, [part1, part2.reshape(N_CHIPS, hr, d)])
        return dxo, dxob, dg, sc, tok

    cq_cb, ckv_cb, kr_cb = 2 * gw // ql, (2 * gw + ql) // kvl, (2 * gw + ql + kvl) // LANES
    qn, kvn = _rowwise("qkv_norm", lambda a, b, ga, gb: (_rms(a, ga), _rms(b, gb)), t // tr,
                       [_rt(proj, tr, ql, cq_cb), _rt(proj, tr, kvl, ckv_cb), _whole(g_q), _whole(g_kv)],
                       [_rt_out(t, ql, BF16, tr), _rt_out(t, kvl, BF16, tr)])
    qfull, = _matmul("q_up", Mat(qn, t, ql), Mat(w_q_all, ql, 2 * hw), "nn", [_out(t, 2 * hw, F32)], tm, _pick(2 * hw, 1024), ql)
    kvall, = _matmul("kv_up", Mat(kvn, t, kvl), Mat(w_kv_all, kvl, 2 * hw), "nn", [_out(t, 2 * hw, BF16)], tm, _pick(2 * hw, 1024), kvl)
    qall, kr = _rope_fwd(qfull, proj, kr_cb, ctab, stab, heads, tr)
    att, lse_row = _attn_fwd(qall, kvall, kr, heads, scale, tr)
    rb = min(2 * LANES, t)
    sgu = _sgu_fwd(proj, g_vn, sgu_w, sgu_b, groups, rb)
    mixed = _rowwise("mix_norm", lambda a, s, ga, gs: jnp.concatenate([_rms(a, ga), _rms(s, gs)], axis=1), t // tr,
                     [_rt(att, tr), _rt(sgu, tr), _whole(g_mla), _whole(g_sgu)], [_rt_out(t, mix, BF16, tr)])[0]
    x1, = _matmul("e_out", Mat(mixed, t, mix), Mat(w_eout, mix, d), "nn", [_out(t, d, F32)], tm, _pick(d, 1024), _pick(mix, 2048),
                  epilogue=lambda z, r: (z + r,), extras=[Mat(xs, t, d)])
    x2, hm0, a0, act0, w1_0, w2_0 = mlp_fwd("0", x1, g_m0, 2)

    w_oin_g, w_oout_g = gathered(4, "o", x2)
    w_oout = w_oout_g.reshape(cd, d)
    h1 = _norm_fwd("o_norm", x2, g_o, tr)
    oin = Mat(_unstack_cols(w_oin_g), d, 3 * cd)
    tn_o = _pick(_gcd(3 * cd // N_CHIPS, cd), 512)
    proj3, = _matmul("o_proj", Mat(h1, t, d), oin, "nn", [_out(t, 3 * cd, F32, "colstack", (), (3, t, cd))],
                     tm, _pick(cd, 1024), kd)
    tc = _pick(cd, 256)
    bz = _conv_fwd(proj3, conv_w, tc)
    x3, = _matmul("o_out", Mat(bz, t, cd), Mat(w_oout, cd, d), "nn", [_out(t, d, F32)], tm, _pick(d, 1024), _pick(cd, 2048),
                  epilogue=lambda z, r: (z + r,), extras=[Mat(x2, t, d)])
    x4, hm1, a1, act1, w1_1, w2_1 = mlp_fwd("1", x3, g_m1, 5)

    def final_fn(xv, gv, tv):
        r = lax.rsqrt(jnp.mean(xv * xv, axis=-1, keepdims=True) + EPS)
        xh = xv * r
        err = xh * gv - tv
        dy = err * (1.0 / d)
        dxh = dy * gv
        dx = r * (dxh - xh * jnp.mean(dxh * xh, axis=-1, keepdims=True))
        sq = jnp.sum(err * err, axis=0, keepdims=True)
        part = sq[:, :LANES]
        for k in range(1, d // LANES):
            part = part + sq[:, k * LANES:(k + 1) * LANES]
        return dx, dx, part, jnp.sum(dy * xh, axis=0, keepdims=True)

    dx4, dx4b, loss_vec, dg_f = _rowwise("loss_final_norm", final_fn, t // tr, [_rt(x4, tr), _whole(g_f), _rt(tgt, tr)],
                                         [_rt_out(t, d, F32, tr), _rt_out(t, d, BF16, tr)],
                                         [jax.ShapeDtypeStruct((1, LANES), F32), jax.ShapeDtypeStruct((1, d), F32)])

    dx3, dx3b, dg_m1, sc_m1, tok = mlp_bwd("1", dx4, dx4b, x3, g_m1, w1_1, w2_1, hm1, a1, act1, ())

    dbz, = _matmul("o_out_dx", Mat(dx3b, t, d), Mat(w_oout, cd, d), "nt", [_out(t, cd, F32)], tm, _pick(cd, 1024), kd,
                   deps=(tok,))
    dw_oout, = _matmul("o_out_dw", Mat(bz, t, cd), Mat(dx3b, t, d), "tn", [_out(cd, d, BF16)], _pick(cd, 1024), _pick(d, 1024), kt)
    dproj3, dconv = _conv_bwd(proj3, conv_w, dbz, tc)
    dp3 = Mat(dproj3, t, 3 * cd, "colstack")
    dw_oin, = _matmul("o_proj_dw", Mat(h1, t, d), dp3, "tn", [_out(d, 3 * cd, BF16, "colstack", (), (N_CHIPS, d, 3 * cd // N_CHIPS))],
                      _pick(d, 1024), tn_o, kt)
    started_o, tok = pair_start("o", [dw_oin, dw_oout.reshape(N_CHIPS, cd // N_CHIPS, d)])
    dh1, = _matmul("o_proj_dx", dp3, oin, "nt", [_out(t, d, F32)], tm, _pick(d, 1024), _pick(cd, 2048), deps=(tok,))
    dx2, dx2b, dg_o = _norm_bwd("o_norm_bwd", dh1, x2, g_o, dx3, tr)
    sc_o, tok = pair_finish("o", started_o, dx2)

    dconv_s = jnp.transpose(dconv[:3].reshape(3, N_CHIPS, cd // N_CHIPS), (1, 0, 2))
    gsmall = jnp.concatenate([jnp.pad(dg_o.reshape(N_CHIPS, 1, d // N_CHIPS), ((0, 0), (0, 15), (0, 0))),
                              jnp.pad(dconv_s, ((0, 0), (0, 13), (0, 0)))], axis=1)
    dx1, dx1b, dg_m0, sc_m0, tok = mlp_bwd("0", dx2, dx2b, x1, g_m0, w1_0, w2_0, hm0, a0, act0, (tok,))

    dmixed, = _matmul("e_out_dx", Mat(dx1b, t, d), Mat(w_eout, mix, d), "nt", [_out(t, mix, F32)], tm, _pick(mix, 1024), kd,
                      deps=(tok,))
    dw_eout, = _matmul("e_out_dw", Mat(mixed, t, mix), Mat(dx1b, t, d), "tn", [_out(mix, d, BF16)], _pick(mix, 1024), _pick(d, 1024), kt)

    def mixb_fn(dm, a, s, ga, gs):
        da, dga = _rms_bwd(dm[:, :hw], a, ga)
        dsg, dgs = _rms_bwd(dm[:, hw:], s, gs)
        prod = da * a
        cols = [jnp.broadcast_to(jnp.sum(prod[:, h * LANES:(h + 1) * LANES], axis=-1, keepdims=True), (tr, LANES))
                for h in range(heads)]
        return da, dsg, jnp.stack([_row_of(c) for c in cols], axis=0), dga, dgs

    da_b, dsgu, delta_row, dg_mla, dg_sgu = _rowwise(
        "mix_norm_bwd", mixb_fn, t // tr, [_rt(dmixed, tr), _rt(att, tr), _rt(sgu, tr), _whole(g_mla), _whole(g_sgu)],
        [_rt_out(t, hw, BF16, tr), _rt_out(t, gw, F32, tr),
         (jax.ShapeDtypeStruct((heads, 8, t), F32), pl.BlockSpec((heads, 8, tr), lambda i: (0, 0, i)))],
        [jax.ShapeDtypeStruct((1, hw), F32), jax.ShapeDtypeStruct((1, gw), F32)])

    dproj, dsgu_w, dsgu_b8, dg_vn = _sgu_bwd(proj, dsgu, g_vn, sgu_w, sgu_b, groups, rb)
    dq1, dq2, dk1, dvv, dkr_h = _attn_bwd(qall, kvall, kr, da_b, lse_row, delta_row, heads, scale, tr)
    dqfull, dproj = _rope_bwd(dq1, dq2, dkr_h, ctab, stab, heads, tr, dproj, kr_cb)
    dkvall = jnp.concatenate([dk1, dvv], axis=1)
    dw_q, = _matmul("q_up_dw", Mat(qn, t, ql), Mat(dqfull, t, 2 * hw), "tn", [_out(ql, 2 * hw, BF16)], ql, _pick(2 * hw, 1024), kt)
    dqn, = _matmul("q_up_dx", Mat(dqfull, t, 2 * hw), Mat(w_q_all, ql, 2 * hw), "nt", [_out(t, ql, F32)], tm, ql, _pick(2 * hw, 2048))
    dw_kv, = _matmul("kv_up_dw", Mat(kvn, t, kvl), Mat(dkvall, t, 2 * hw), "tn", [_out(kvl, 2 * hw, BF16)], kvl, _pick(2 * hw, 1024), kt)
    dkvn, = _matmul("kv_up_dx", Mat(dkvall, t, 2 * hw), Mat(w_kv_all, kvl, 2 * hw), "nt", [_out(t, kvl, F32)], tm, kvl, _pick(2 * hw, 2048))

    def qkvb_fn(da, db, a, b, ga, gb):
        dxa, dga = _rms_bwd(da, a, ga)
        dxb, dgb = _rms_bwd(db, b, gb)
        return jnp.concatenate([dxa, dxb], axis=1), dga, dgb

    assert (2 * gw) % (ql + kvl) == 0
    into = (jax.ShapeDtypeStruct(dproj.shape, dproj.dtype),
            pl.BlockSpec((tr, ql + kvl), lambda i: (i, 2 * gw // (ql + kvl))))
    dproj, dg_q, dg_kv = _rowwise(
        "qkv_norm_bwd", qkvb_fn, t // tr,
        [_rt(dqn, tr), _rt(dkvn, tr), _rt(proj, tr, ql, cq_cb), _rt(proj, tr, kvl, ckv_cb), _whole(g_q), _whole(g_kv)],
        [into], [jax.ShapeDtypeStruct((1, ql), F32), jax.ShapeDtypeStruct((1, kvl), F32)], deps=(dproj,), fill=(0, 0))
    dw_in, = _matmul("e_proj_dw", Mat(dproj, t, pi), Mat(h0, t, d), "tn", [_out(pi, d, F32)], _pick(pi, 1024), _pick(d, 1024), kt)
    dh0, = _matmul("e_proj_dx", Mat(dproj, t, pi), Mat(w_in_all, d, pi), "nt", [_out(t, d, F32)], tm, _pick(d, 1024), _pick(pi, 4096))
    dx0, _, dg_e = _norm_bwd("e_norm_bwd", dh0, xs, g_e, dx1, tr)

    kr0 = 2 * gw + c2
    gw_in = jnp.concatenate([dw_in[2 * gw:kr0], dw_in[kr0:kr0 + ROPE_HALF], dw_in[kr0 + ROPE:kr0 + ROPE + ROPE_HALF],
                             dw_in[:2 * gw]], axis=0).reshape(N_CHIPS, ei // N_CHIPS, d)
    gq = jnp.concatenate([dw_q[:, :hw].reshape(ql, heads, LANES), _unpad_rope(dw_q[:, hw:].reshape(ql, heads, LANES))], axis=-1)
    gw_uq = _stack_cols(gq.reshape(ql, heads * (LANES + ROPE)))
    gkv = jnp.concatenate([dw_kv[:, :hw].reshape(kvl, heads, LANES), dw_kv[:, hw:].reshape(kvl, heads, LANES)], axis=-1)
    gw_ukv = _stack_cols(gkv.reshape(kvl, heads * 2 * LANES))
    started_e, tok_pair = pair_start("e", [gw_in, gw_uq, gw_ukv, dw_eout.reshape(N_CHIPS, mix // N_CHIPS, d), gsmall])

    small_like = [e_norm_mix, e_q_norm, e_kv_norm, e_v_norm, e_sgu_w, e_sgu_b, e_mla_out_norm, e_sgu_out_norm, mlp_norm, final_norm]
    small_grads = [dg_e, dg_q, dg_kv, dg_vn, dsgu_w, dsgu_b8[:, 0, :], dg_mla, dg_sgu, jnp.concatenate([dg_m0, dg_m1], axis=0), dg_f]
    packed = _pack_small(small_grads)
    n_small = packed.shape[0] + (-packed.shape[0]) % 8
    pad = n_small - packed.shape[0] + 8
    sflat = jnp.concatenate([jnp.pad(packed, ((0, pad - 8), (0, 0))), jnp.pad(loss_vec, ((0, 7), (0, 0)))], axis=0)
    small_started, tok_small = _exchange_start("small_start", _all_route, 7, [sflat], [_spread(sflat)])

    sh_m1, tok = summed("m1", sc_m1, (tok_pair, tok_small))
    sc_e, tok = pair_finish("e", started_e, tok)
    sh_o, tok = summed("o", sc_o, tok)
    sh_m0, tok = summed("m0", sc_m0, tok)
    r_oin, r_oout = shared("o", sh_o, tok)
    late = {"o_w_in": _adamw(o_w_in, [r_oin], m_o_w_in, v_o_w_in),
            "o_w_out": _adamw(o_w_out, [r_oout], m_o_w_out, v_o_w_out)}
    r_w1_1, r_w2_1 = shared("m1", sh_m1, late["o_w_in"][1])
    r_w1_0, r_w2_0 = shared("m0", sh_m0, r_w2_1)
    late["mlp_w1"] = _adamw(mlp_w1, [r_w1_0, r_w1_1], m_mlp_w1, v_mlp_w1)
    late["mlp_w2"] = _adamw(mlp_w2, [r_w2_0, r_w2_1], m_mlp_w2, v_mlp_w2)

    _, (all_small,) = _exchange_wait("small_wait", _all_route, small_started, late["mlp_w2"][1])
    g_small = _sum_devices(all_small)
    loss = 0.5 * jnp.sum(g_small[n_small]) / d

    def padded(arrs):
        return jnp.pad(_pack_small(arrs), ((0, pad), (0, 0)))

    s_m = [m_e_norm_mix, m_e_q_norm, m_e_kv_norm, m_e_v_norm, m_e_sgu_w, m_e_sgu_b, m_e_mla_out_norm, m_e_sgu_out_norm, m_mlp_norm, m_final_norm]
    s_v = [v_e_norm_mix, v_e_q_norm, v_e_kv_norm, v_e_v_norm, v_e_sgu_w, v_e_sgu_b, v_e_mla_out_norm, v_e_sgu_out_norm, v_mlp_norm, v_final_norm]
    s_out = [_unpack_small(o[0], small_like)
             for o in _adamw(padded(small_like)[None], [g_small], padded(s_m)[None], padded(s_v)[None])]

    sh_e, tok = summed("e", sc_e, late["mlp_w2"][1])
    r_in, r_uq, r_ukv, r_eout, r_small = shared("e", sh_e, tok)
    sm = [o[0] for o in _adamw(small_shard[None], [r_small], _small_shard(m_o_norm_mix, m_o_conv_w[0])[None],
                               _small_shard(v_o_norm_mix, v_o_conv_w[0])[None])]
    big = dict(late)
    flip = lambda a: jnp.swapaxes(a, 1, 2)
    big.update({
        "e_w_in": [flip(o) for o in _adamw(flip(e_w_in), [r_in], flip(m_e_w_in), flip(v_e_w_in))],
        "e_w_uq": _adamw(e_w_uq, [r_uq], m_e_w_uq, v_e_w_uq),
        "e_w_ukv": _adamw(e_w_ukv, [r_ukv], m_e_w_ukv, v_e_w_ukv),
        "e_w_out": _adamw(e_w_out, [r_eout], m_e_w_out, v_e_w_out),
    })

    names = ["e_norm_mix", "e_w_in", "e_q_norm", "e_w_uq", "e_kv_norm", "e_w_ukv", "e_v_norm", "e_sgu_w", "e_sgu_b",
             "e_mla_out_norm", "e_sgu_out_norm", "e_w_out", "o_norm_mix", "o_w_in", "o_conv_w", "o_w_out",
             "mlp_norm", "mlp_w1", "mlp_w2", "final_norm"]
    shapes = {"e_w_in": e_w_in.shape, "e_w_uq": e_w_uq.shape, "e_w_ukv": e_w_ukv.shape, "e_w_out": e_w_out.shape,
              "o_w_in": o_w_in.shape, "o_w_out": o_w_out.shape, "mlp_w1": mlp_w1.shape, "mlp_w2": mlp_w2.shape}
    small_names = ["e_norm_mix", "e_q_norm", "e_kv_norm", "e_v_norm", "e_sgu_w", "e_sgu_b", "e_mla_out_norm",
                   "e_sgu_out_norm", "mlp_norm", "final_norm"]

    def leaf(kind, name):
        if name in big:
            return big[name][kind].reshape(shapes[name])
        if name == "o_norm_mix":
            return sm[kind][0:1]
        if name == "o_conv_w":
            return sm[kind][16:19].reshape(o_conv_w.shape)
        return s_out[kind][small_names.index(name)]

    outs = [loss, dx0.reshape(x.shape)]
    for kind in range(4):
        outs += [leaf(kind, nm) for nm in names]
    return tuple(outs)


def _gcd(a, b):
    while b:
        a, b = b, a % b
    return a
```

```python
import jax
import jax.numpy as jnp
from jax import lax
from jax.experimental import pallas as pl
from jax.experimental.pallas import tpu as pltpu

F32 = jnp.float32
BF16 = jnp.bfloat16
MESH = pl.DeviceIdType.MESH

LANES = 128
ROPE = 64
ROPE_HALF = ROPE // 2
ROPE_BASE = 10000.0
EPS = 1e-6
N_CHIPS = 4
VMEM_LIMIT = 48 * 1024 * 1024
NEG = -1e30

ADAM_LR = 0.001
ADAM_B1 = 0.9
ADAM_B2 = 0.999
ADAM_EPS = 1e-08
ADAM_WD = 0.01
ADAM_STEP = 10


def _pick(n, target, step=LANES):
    best = None
    for t in range(step, min(n, target) + 1, step):
        if n % t == 0:
            best = t
    return best if best is not None else n


def _params(sem, vmem=VMEM_LIMIT):
    return pltpu.CompilerParams(dimension_semantics=sem, vmem_limit_bytes=vmem)


class Mat:
    def __init__(self, arr, rows, cols, kind="plain", lead=(), cmap=None, shape=None, dtype=None):
        self.arr, self.rows, self.cols, self.kind, self.lead, self.cmap = arr, rows, cols, kind, tuple(lead), cmap
        self.shape = tuple(arr.shape) if arr is not None else tuple(shape)
        self.dtype = arr.dtype if arr is not None else dtype

    def sds(self):
        return jax.ShapeDtypeStruct(self.shape, self.dtype)

    def spec(self, br, bc, gridmap):
        lead, nl = self.lead, len(self.lead)
        if self.kind == "plain":
            assert self.rows % br == 0 and self.cols % bc == 0, (self.shape, br, bc)
            cmap = self.cmap if self.cmap is not None else (lambda cb, _: cb)
            block = (None,) * nl + (br, bc)

            def phys(rb, cb):
                return lead + (rb, cmap(cb, bc))
        elif self.kind == "colstack":
            cs = self.shape[-1]
            assert cs % bc == 0 and self.rows % br == 0, (self.shape, br, bc)
            q = cs // bc
            block = (None,) * (nl + 1) + (br, bc)

            def phys(rb, cb):
                return (cb // q,) + lead + (rb, cb % q)
        else:
            rs = self.shape[-2]
            assert rs % br == 0 and self.cols % bc == 0, (self.shape, br, bc)
            q = rs // br
            block = (None,) * (nl + 1) + (br, bc)

            def phys(rb, cb):
                return (rb // q,) + lead + (rb % q, cb)

        return pl.BlockSpec(block, lambda *g: phys(*gridmap(*g)))


def _adamw_math(w, g, m, v):
    mn = ADAM_B1 * m + (1.0 - ADAM_B1) * g
    vn = ADAM_B2 * v + (1.0 - ADAM_B2) * jnp.square(g)
    m_hat = mn / (1.0 - ADAM_B1 ** ADAM_STEP)
    v_hat = vn / (1.0 - ADAM_B2 ** ADAM_STEP)
    return -ADAM_LR * (m_hat / (jnp.sqrt(v_hat) + ADAM_EPS) + ADAM_WD * w), mn, vn


def _matmul(name, a, b, mode, outs, tm, tn, tk, epilogue=None, extras=(), deps=()):
    if mode == "nn":
        m, k, n = a.rows, a.cols, b.cols
        a_spec = a.spec(tm, tk, lambda i, j, kk: (i, kk))
        b_spec = b.spec(tk, tn, lambda i, j, kk: (kk, j))
        dims = (((1,), (0,)), ((), ()))
    elif mode == "nt":
        m, k, n = a.rows, a.cols, b.rows
        a_spec = a.spec(tm, tk, lambda i, j, kk: (i, kk))
        b_spec = b.spec(tn, tk, lambda i, j, kk: (j, kk))
        dims = (((1,), (1,)), ((), ()))
    else:
        k, m, n = a.rows, a.cols, b.cols
        a_spec = a.spec(tk, tm, lambda i, j, kk: (kk, i))
        b_spec = b.spec(tk, tn, lambda i, j, kk: (kk, j))
        dims = (((0,), (0,)), ((), ()))
    assert m % tm == 0 and n % tn == 0 and k % tk == 0, (name, m, n, k, tm, tn, tk)
    grid = (m // tm, n // tn, k // tk)
    nk = grid[2]
    n_ex, n_out, n_dep = len(extras), len(outs), len(deps)
    tile = lambda i, j, kk: (i, j)

    def finish(z, ex, out_refs):
        vals = epilogue(z, *[e[...] for e in ex]) if epilogue is not None else (z,)
        for o, v in zip(out_refs, vals):
            o[...] = v.astype(o.dtype)

    def body_single(a_ref, b_ref, *rest):
        finish(lax.dot_general(a_ref[...], b_ref[...], dims, preferred_element_type=F32),
               rest[:n_ex], rest[n_ex + n_dep:n_ex + n_dep + n_out])

    def body_acc(a_ref, b_ref, *rest):
        acc = rest[-1]
        kk = pl.program_id(2)

        @pl.when(kk == 0)
        def _():
            acc[...] = jnp.zeros_like(acc)

        acc[...] += lax.dot_general(a_ref[...], b_ref[...], dims, preferred_element_type=F32)

        @pl.when(kk == nk - 1)
        def _():
            finish(acc[...], rest[:n_ex], rest[n_ex + n_dep:n_ex + n_dep + n_out])

    res = pl.pallas_call(
        body_single if nk == 1 else body_acc, name=name, grid=grid,
        in_specs=[a_spec, b_spec] + [e.spec(tm, tn, tile) for e in extras]
        + [pl.BlockSpec(memory_space=pl.ANY) for _ in deps],
        out_specs=[o.spec(tm, tn, tile) for o in outs],
        out_shape=[o.sds() for o in outs],
        scratch_shapes=[] if nk == 1 else [pltpu.VMEM((tm, tn), F32)],
        compiler_params=_params(("parallel", "parallel", "arbitrary")),
    )(a.arr, b.arr, *[e.arr for e in extras], *deps)
    return res


def _out(rows, cols, dtype, kind="plain", lead=(), shape=None):
    return Mat(None, rows, cols, kind, lead, shape=shape if shape is not None else (rows, cols), dtype=dtype)


def _rt(arr, tr, width=None, cb=0):
    width = arr.shape[1] if width is None else width
    return arr, pl.BlockSpec((tr, width), lambda i: (i, cb))


def _whole(arr):
    nd = arr.ndim
    return arr, pl.BlockSpec(arr.shape, lambda i: (0,) * nd)


def _rowwise(name, fn, n_steps, ins, outs, accs=(), deps=(), fill=None):
    n_in, n_out, n_acc, n_dep = len(ins), len(outs), len(accs), len(deps)

    def body(*refs):
        vals = fn(*[r[...] for r in refs[:n_in]])
        if not isinstance(vals, (tuple, list)):
            vals = (vals,)
        for ref, v in zip(refs[n_in + n_dep:n_in + n_dep + n_out], vals[:n_out]):
            ref[...] = v.astype(ref.dtype)
        if n_acc:
            acc_refs = refs[n_in + n_dep + n_out:]

            @pl.when(pl.program_id(0) == 0)
            def _():
                for ref in acc_refs:
                    ref[...] = jnp.zeros_like(ref)

            for ref, v in zip(acc_refs, vals[n_out:]):
                ref[...] += v

    acc_specs = [pl.BlockSpec(s.shape, lambda i, nd=len(s.shape): (0,) * nd) for s in accs]
    res = pl.pallas_call(
        body, name=name, grid=(n_steps,),
        in_specs=[s for _, s in ins] + [pl.BlockSpec(memory_space=pl.ANY) for _ in deps],
        out_specs=[s for _, s in outs] + acc_specs,
        out_shape=[o for o, _ in outs] + list(accs),
        input_output_aliases={} if fill is None else {n_in + fill[0]: fill[1]},
        compiler_params=_params(("arbitrary",) if n_acc else ("parallel",)),
    )(*[a for a, _ in ins], *deps)
    return res


def _rt_out(t, width, dtype, tr):
    return jax.ShapeDtypeStruct((t, width), dtype), pl.BlockSpec((tr, width), lambda i: (i, 0))


def _rms(x, g):
    r = lax.rsqrt(jnp.mean(x * x, axis=-1, keepdims=True) + EPS)
    return x * r * g


def _rms_bwd(dy, x, g):
    r = lax.rsqrt(jnp.mean(x * x, axis=-1, keepdims=True) + EPS)
    xh = x * r
    dxh = dy * g
    dx = r * (dxh - xh * jnp.mean(dxh * xh, axis=-1, keepdims=True))
    dg = jnp.sum(dy * xh, axis=0, keepdims=True)
    return dx, dg


def _gelu(x):
    k = 0.7978845608028654
    th = jnp.tanh(k * (x + 0.044715 * (x * x * x)))
    return x * (0.5 * (1.0 + th))


def _gelu_grad(x):
    k = 0.7978845608028654
    x2 = x * x
    th = jnp.tanh(k * (x + 0.044715 * (x2 * x)))
    return 0.5 * (1.0 + th) + 0.5 * x * (1.0 - th * th) * (k * (1.0 + 3.0 * 0.044715 * x2))


def _norm_fwd(name, x, g, tr):
    t, d = x.shape
    return _rowwise(name, lambda xv, gv: _rms(xv, gv), t // tr, [_rt(x, tr), _whole(g)], [_rt_out(t, d, BF16, tr)])[0]


def _norm_bwd(name, dh, x, g, dres, tr):
    t, d = x.shape

    def fn(dhv, xv, gv, drv):
        dx, dg = _rms_bwd(dhv, xv, gv)
        dx = dx + drv
        return dx, dx, dg

    return _rowwise(name, fn, t // tr, [_rt(dh, tr), _rt(x, tr), _whole(g), _rt(dres, tr)],
                    [_rt_out(t, d, F32, tr), _rt_out(t, d, BF16, tr)], [jax.ShapeDtypeStruct((1, d), F32)])


def _rope_tables(posf, invf, cmask, smask, tr):
    t = posf.shape[0]

    def fn(p, f, cm, sm):
        ang = p * f
        return jnp.cos(ang) * cm, jnp.sin(ang) * sm

    return _rowwise("rope_tables", fn, t // tr, [_rt(posf, tr), _whole(invf), _whole(cmask), _whole(smask)],
                    [_rt_out(t, LANES, F32, tr), _rt_out(t, LANES, F32, tr)])


def _rot(v, c, s):
    return v * c + pltpu.roll(v, ROPE, axis=1) * s


def _rot_bwd(dv, c, s):
    return dv * c + pltpu.roll(dv * s, ROPE, axis=1)


def _rope_fwd(qfull, proj, kr_cb, ctab, stab, heads, tr):
    t = qfull.shape[0]
    hw = heads * LANES

    def fn(q, kr, c, s):
        parts = [q[:, :hw]] + [_rot(q[:, hw + h * LANES: hw + (h + 1) * LANES], c, s) for h in range(heads)]
        return jnp.concatenate(parts, axis=1), _rot(kr, c, s)

    return _rowwise("rope_fwd", fn, t // tr, [_rt(qfull, tr), _rt(proj, tr, LANES, kr_cb), _rt(ctab, tr), _rt(stab, tr)],
                    [_rt_out(t, 2 * hw, BF16, tr), _rt_out(t, LANES, BF16, tr)])


def _rope_bwd(dq1, dq2, dkr_h, ctab, stab, heads, tr, dproj, kr_cb):
    t = dq1.shape[0]
    hw = heads * LANES

    def fn(a, b, dk, c, s):
        parts = [a] + [_rot_bwd(b[:, h * LANES:(h + 1) * LANES], c, s) for h in range(heads)]
        dks = dk[0]
        for h in range(1, heads):
            dks = dks + dk[h]
        return jnp.concatenate(parts, axis=1), _rot_bwd(dks, c, s)

    dk_spec = pl.BlockSpec((heads, tr, LANES), lambda i: (0, i, 0))
    into = (jax.ShapeDtypeStruct(dproj.shape, dproj.dtype), pl.BlockSpec((tr, LANES), lambda i: (i, kr_cb)))
    return _rowwise("rope_bwd", fn, t // tr, [_rt(dq1, tr), _rt(dq2, tr), (dkr_h, dk_spec), _rt(ctab, tr), _rt(stab, tr)],
                    [_rt_out(t, 2 * hw, BF16, tr), into], deps=(dproj,), fill=(0, 1))


def _dot_nt(a, b):
    return lax.dot_general(a, b, (((1,), (1,)), ((), ())), preferred_element_type=F32)


def _dot_tn(a, b):
    return lax.dot_general(a, b, (((0,), (0,)), ((), ())), preferred_element_type=F32)


def _dot(a, b):
    return jnp.dot(a, b, preferred_element_type=F32)


def _ranges(n_blocks):
    n_var = min(4, n_blocks)
    assert n_blocks % n_var == 0
    return n_var, n_blocks // n_var


def _row_of(col):
    return col.T[:8, :]


def _attn_fwd(qall, kvall, kr, heads, scale, tq):
    t = qall.shape[0]
    nq = t // tq
    n_var, per = _ranges(nq)

    def body(qn_ref, qr_ref, kn_ref, v_ref, kr_ref, o_ref, lser_ref):
        i = pl.program_id(1)
        for var in range(n_var):
            kv = (var + 1) * per * tq

            @pl.when(jnp.logical_and(i >= var * per, i < (var + 1) * per))
            def _(kv=kv):
                s = _dot_nt(jnp.concatenate([qn_ref[...], qr_ref[...]], axis=1),
                            jnp.concatenate([kn_ref[:kv, :], kr_ref[:kv, :]], axis=1)) * scale
                rows = i * tq + lax.broadcasted_iota(jnp.int32, (tq, kv), 0)
                cols = lax.broadcasted_iota(jnp.int32, (tq, kv), 1)
                s = jnp.where(cols <= rows, s, NEG)
                m = jnp.max(s, axis=-1, keepdims=True)
                p = jnp.exp(s - m)
                l = jnp.sum(p, axis=-1, keepdims=True)
                o_ref[...] = _dot(p.astype(BF16), v_ref[:kv, :]) / l
                lser_ref[...] = _row_of(jnp.broadcast_to(m + jnp.log(l), (tq, LANES)))

    return pl.pallas_call(
        body, name="attn_fwd", grid=(heads, nq),
        in_specs=[pl.BlockSpec((tq, LANES), lambda h, i: (i, h)),
                  pl.BlockSpec((tq, LANES), lambda h, i: (i, heads + h)),
                  pl.BlockSpec((t, LANES), lambda h, i: (0, h)),
                  pl.BlockSpec((t, LANES), lambda h, i: (0, heads + h)),
                  pl.BlockSpec((t, LANES), lambda h, i: (0, 0))],
        out_specs=[pl.BlockSpec((tq, LANES), lambda h, i: (i, h)),
                   pl.BlockSpec((None, 8, tq), lambda h, i: (h, 0, i))],
        out_shape=[jax.ShapeDtypeStruct((t, heads * LANES), F32), jax.ShapeDtypeStruct((heads, 8, t), F32)],
        compiler_params=_params(("parallel", "parallel")),
    )(qall, qall, kvall, kvall, kr)


def _attn_bwd(qall, kvall, kr, do, lse_row, delta_row, heads, scale, tk):
    t = qall.shape[0]
    nk = t // tk
    n_var, per = _ranges(nk)

    def body(qn_ref, qr_ref, kn_ref, v_ref, kr_ref, do_ref, lse_ref, dl_ref, dq1_ref, dq2_ref, dk_ref, dv_ref, dkr_ref):
        j = pl.program_id(1)

        @pl.when(j == 0)
        def _():
            dq1_ref[...] = jnp.zeros_like(dq1_ref)
            dq2_ref[...] = jnp.zeros_like(dq2_ref)

        for var in range(n_var):
            q0 = var * per * tk
            nq = t - q0

            @pl.when(jnp.logical_and(j >= var * per, j < (var + 1) * per))
            def _(q0=q0, nq=nq):
                qn, qr, do_v = qn_ref[q0:, :], qr_ref[q0:, :], do_ref[q0:, :]
                k1, k2 = kn_ref[...], kr_ref[...]
                qcat, kcat = jnp.concatenate([qn, qr], axis=1), jnp.concatenate([k1, k2], axis=1)
                st = _dot_nt(kcat, qcat) * scale
                keys = j * tk + lax.broadcasted_iota(jnp.int32, (tk, nq), 0)
                queries = q0 + lax.broadcasted_iota(jnp.int32, (tk, nq), 1)
                pt = jnp.where(keys <= queries, jnp.exp(st - lse_ref[0:1, q0:]), 0.0)
                dpt = _dot_nt(v_ref[...], do_v)
                dst = (pt * (dpt - dl_ref[0:1, q0:]) * scale).astype(BF16)
                dv_ref[...] = _dot(pt.astype(BF16), do_v).astype(dv_ref.dtype)
                dkc = _dot(dst, qcat)
                dk_ref[...] = dkc[:, :LANES].astype(dk_ref.dtype)
                dkr_ref[...] = dkc[:, LANES:]
                dqc = _dot_tn(dst, kcat)
                dq1_ref[q0:, :] += dqc[:, :LANES]
                dq2_ref[q0:, :] += dqc[:, LANES:]

    kblk = lambda off: pl.BlockSpec((tk, LANES), lambda h, j: (j, off + h))
    full = lambda off: pl.BlockSpec((t, LANES), lambda h, j: (0, off + h))
    stat = pl.BlockSpec((None, 8, t), lambda h, j: (h, 0, 0))
    return pl.pallas_call(
        body, name="attn_bwd", grid=(heads, nk),
        in_specs=[full(0), full(heads), kblk(0), kblk(heads), pl.BlockSpec((tk, LANES), lambda h, j: (j, 0)),
                  full(0), stat, stat],
        out_specs=[full(0), full(0), kblk(0), kblk(0), pl.BlockSpec((None, tk, LANES), lambda h, j: (h, j, 0))],
        out_shape=[jax.ShapeDtypeStruct((t, heads * LANES), F32)] * 2 + [jax.ShapeDtypeStruct((t, heads * LANES), BF16)] * 2
        + [jax.ShapeDtypeStruct((heads, t, LANES), F32)],
        compiler_params=_params(("parallel", "arbitrary")),
    )(qall, qall, kvall, kvall, kr, do, lse_row, delta_row)


def _tril():
    return lax.broadcasted_iota(jnp.int32, (LANES, LANES), 0) >= lax.broadcasted_iota(jnp.int32, (LANES, LANES), 1)


def _group_norm(vg):
    mu = jnp.mean(vg, axis=-1, keepdims=True)
    vc = vg - mu
    rs = lax.rsqrt(jnp.mean(vc * vc, axis=-1, keepdims=True) + EPS)
    return vc * rs, rs


def _sgu_fwd(proj, gain, w, bias, groups, rb):
    t = proj.shape[0]
    gw = groups * LANES
    cpb = rb // LANES

    def body(u_ref, v_ref, gain_ref, w_ref, b_ref, s_ref):
        tril = _tril()
        for g in range(groups):
            wt = jnp.where(tril, w_ref[g], 0.0).astype(BF16)
            cols = slice(g * LANES, (g + 1) * LANES)
            for ci in range(cpb):
                rows = slice(ci * LANES, (ci + 1) * LANES)
                ug = _gelu(u_ref[rows, cols])
                vh, _ = _group_norm(_gelu(v_ref[rows, cols]))
                vn = vh * gain_ref[:, cols]
                y = _dot(wt, vn.astype(BF16)) + b_ref[g]
                s_ref[rows, cols] = ug * y

    return pl.pallas_call(
        body, name="sgu_fwd", grid=(t // rb,),
        in_specs=[pl.BlockSpec((rb, gw), lambda i: (i, 0)), pl.BlockSpec((rb, gw), lambda i: (i, 1)),
                  pl.BlockSpec((1, gw), lambda i: (0, 0)),
                  pl.BlockSpec((groups, LANES, LANES), lambda i: (0, 0, 0)),
                  pl.BlockSpec((groups, LANES, LANES), lambda i: (0, 0, 0))],
        out_specs=pl.BlockSpec((rb, gw), lambda i: (i, 0)),
        out_shape=jax.ShapeDtypeStruct((t, gw), F32),
        compiler_params=_params(("parallel",)),
    )(proj, proj, gain, w, bias)


def _sgu_bwd(proj, ds, gain, w, bias, groups, rb):
    t, width = proj.shape
    gw = groups * LANES
    cpb = rb // LANES
    n_steps = t // rb

    def body(u_ref, v_ref, ds_ref, gain_ref, w_ref, b_ref, dp_ref, dw_ref, db_ref, dg_ref, dy_acc):
        du_ref, dv_ref = dp_ref.at[:, :gw], dp_ref.at[:, gw:]
        step = pl.program_id(0)

        @pl.when(step == 0)
        def _():
            dw_ref[...] = jnp.zeros_like(dw_ref)
            dy_acc[...] = jnp.zeros_like(dy_acc)
            dg_ref[...] = jnp.zeros_like(dg_ref)

        tril = _tril()
        for g in range(groups):
            wt = jnp.where(tril, w_ref[g], 0.0).astype(BF16)
            cols = slice(g * LANES, (g + 1) * LANES)
            gain_g = gain_ref[:, cols]
            for ci in range(cpb):
                rows = slice(ci * LANES, (ci + 1) * LANES)
                u_raw, v_raw, ds_v = u_ref[rows, cols], v_ref[rows, cols], ds_ref[rows, cols]
                ug = _gelu(u_raw)
                vh, rs = _group_norm(_gelu(v_raw))
                vn = (vh * gain_g).astype(BF16)
                y = _dot(wt, vn) + b_ref[g]
                dy = ds_v * ug
                dyb = dy.astype(BF16)
                du_ref[rows, cols] = (ds_v * y * _gelu_grad(u_raw)).astype(du_ref.dtype)
                dy_acc[g] += dy
                dw_ref[g] += _dot_nt(dyb, vn)
                dvn = _dot_tn(wt, dyb)
                dg_ref[:, cols] += jnp.sum(dvn * vh, axis=0, keepdims=True)
                dvh = dvn * gain_g
                dvg = rs * (dvh - jnp.mean(dvh, axis=-1, keepdims=True)
                            - vh * jnp.mean(dvh * vh, axis=-1, keepdims=True))
                dv_ref[rows, cols] = (dvg * _gelu_grad(v_raw)).astype(dv_ref.dtype)

        @pl.when(step == n_steps - 1)
        def _():
            ones = jnp.ones((8, LANES), F32)
            for g in range(groups):
                dw_ref[g] = jnp.where(tril, dw_ref[g], 0.0)
                db_ref[g] = lax.dot_general(ones, dy_acc[g], (((1,), (1,)), ((), ())),
                                            precision=lax.Precision.HIGHEST, preferred_element_type=F32)

    blk = lambda cb: pl.BlockSpec((rb, gw), lambda i: (i, cb))
    whole3 = pl.BlockSpec((groups, LANES, LANES), lambda i: (0, 0, 0))
    return pl.pallas_call(
        body, name="sgu_bwd", grid=(n_steps,),
        in_specs=[blk(0), blk(1), blk(0), pl.BlockSpec((1, gw), lambda i: (0, 0)), whole3, whole3],
        out_specs=[pl.BlockSpec((rb, 2 * gw), lambda i: (i, 0)), whole3,
                   pl.BlockSpec((groups, 8, LANES), lambda i: (0, 0, 0)), pl.BlockSpec((1, gw), lambda i: (0, 0))],
        out_shape=[jax.ShapeDtypeStruct((t, width), BF16),
                   jax.ShapeDtypeStruct((groups, LANES, LANES), F32), jax.ShapeDtypeStruct((groups, 8, LANES), F32),
                   jax.ShapeDtypeStruct((1, gw), F32)],
        scratch_shapes=[pltpu.VMEM((groups, LANES, LANES), F32)],
        compiler_params=_params(("arbitrary",)),
    )(proj, proj, ds, gain, w, bias)


def _shift_down(z, s):
    rows = lax.broadcasted_iota(jnp.int32, z.shape, 0)
    return jnp.where(rows >= s, pltpu.roll(z, s, axis=0), 0.0)


def _shift_up(z, s):
    n = z.shape[0]
    rows = lax.broadcasted_iota(jnp.int32, z.shape, 0)
    return jnp.where(rows < n - s, pltpu.roll(z, n - s, axis=0), 0.0)


def _conv_fwd(proj3, cw, tc):
    _, t, cd = proj3.shape

    def body(p_ref, w_ref, o_ref):
        z = p_ref[1] * p_ref[2]
        w = w_ref[...]
        zc = w[2:3] * z + w[1:2] * _shift_down(z, 1) + w[0:1] * _shift_down(z, 2)
        o_ref[...] = (p_ref[0] * zc).astype(o_ref.dtype)

    return pl.pallas_call(
        body, name="conv_fwd", grid=(cd // tc,),
        in_specs=[pl.BlockSpec((3, t, tc), lambda j: (0, 0, j)), pl.BlockSpec((8, tc), lambda j: (0, j))],
        out_specs=pl.BlockSpec((t, tc), lambda j: (0, j)),
        out_shape=jax.ShapeDtypeStruct((t, cd), BF16),
        compiler_params=_params(("parallel",)),
    )(proj3, cw)


def _conv_bwd(proj3, cw, dbz, tc):
    _, t, cd = proj3.shape

    def body(p_ref, w_ref, d_ref, o_ref, dw_ref):
        b, c, xin = p_ref[0], p_ref[1], p_ref[2]
        w = w_ref[...]
        z = c * xin
        z1, z2 = _shift_down(z, 1), _shift_down(z, 2)
        zc = w[2:3] * z + w[1:2] * z1 + w[0:1] * z2
        d = d_ref[...]
        dzc = d * b
        dz = w[2:3] * dzc + w[1:2] * _shift_up(dzc, 1) + w[0:1] * _shift_up(dzc, 2)
        o_ref[0] = (d * zc).astype(o_ref.dtype)
        o_ref[1] = (dz * xin).astype(o_ref.dtype)
        o_ref[2] = (dz * c).astype(o_ref.dtype)
        row = lax.broadcasted_iota(jnp.int32, (8, tc), 0)
        dw0 = jnp.sum(dzc * z2, axis=0, keepdims=True)
        dw1 = jnp.sum(dzc * z1, axis=0, keepdims=True)
        dw2 = jnp.sum(dzc * z, axis=0, keepdims=True)
        dw_ref[...] = jnp.where(row == 0, dw0, 0.0) + jnp.where(row == 1, dw1, 0.0) + jnp.where(row == 2, dw2, 0.0)

    return pl.pallas_call(
        body, name="conv_bwd", grid=(cd // tc,),
        in_specs=[pl.BlockSpec((3, t, tc), lambda j: (0, 0, j)), pl.BlockSpec((8, tc), lambda j: (0, j)),
                  pl.BlockSpec((t, tc), lambda j: (0, j))],
        out_specs=[pl.BlockSpec((3, t, tc), lambda j: (0, 0, j)), pl.BlockSpec((8, tc), lambda j: (0, j))],
        out_shape=[jax.ShapeDtypeStruct((3, t, cd), BF16), jax.ShapeDtypeStruct((8, cd), F32)],
        compiler_params=_params(("parallel",)),
    )(proj3, cw, dbz)


def _place():
    x, y, c = lax.axis_index("x"), lax.axis_index("y"), lax.axis_index("c")
    chips = [(1 - x, y), (x, 1 - y), (1 - x, 1 - y)]
    return x, y, c, chips


def _any_specs(n):
    return [pl.BlockSpec(memory_space=pl.ANY) for _ in range(n)]


HBM_SPEC = pl.BlockSpec(memory_space=pltpu.HBM)
SEM_SPEC = pl.BlockSpec(memory_space=pltpu.SEMAPHORE)
ORDERED_EFFECT = pltpu.SideEffectType.DATAFLOW_SIDE_EFFECTING


def _in_hbm(a):
    return pltpu.with_memory_space_constraint(a, pltpu.HBM)


def _token():
    return jax.ShapeDtypeStruct((8, LANES), F32), pl.BlockSpec(memory_space=pltpu.VMEM)


def _gather_start(name, groups):
    sizes = [len(g) for g in groups]
    flat = [b for g in groups for b in g]
    n, ng = len(flat), len(groups)

    def body(*refs):
        ins, sems, token = refs[:n], refs[n:n + 2 * ng], refs[-1]
        x, y, c, chips = _place()
        me = 2 * x + y
        i = 0
        for gi, size in enumerate(sizes):
            for j in range(size):
                blk = ins[i].at[me, c]
                for k, chip in enumerate(chips):
                    pltpu.make_async_remote_copy(src_ref=blk, dst_ref=blk, send_sem=sems[2 * gi].at[3 * j + k],
                                                 recv_sem=sems[2 * gi + 1].at[3 * j + k],
                                                 device_id=(*chip, c), device_id_type=MESH).start()
                i += 1
        token[...] = jnp.zeros_like(token)

    tok_shape, tok_spec = _token()
    res = pl.pallas_call(
        body, name=name,
        in_specs=[HBM_SPEC] * n,
        out_specs=[SEM_SPEC] * (2 * ng) + [HBM_SPEC] * n + [tok_spec],
        out_shape=[pltpu.SemaphoreType.DMA((3 * size,)) for size in sizes for _ in (0, 1)]
        + [pltpu.HBM(b.shape, b.dtype) for b in flat] + [tok_shape],
        input_output_aliases={i: 2 * ng + i for i in range(n)},
        compiler_params=pltpu.CompilerParams(has_side_effects=ORDERED_EFFECT),
    )(*[_in_hbm(b) for b in flat])
    out, i = [], 2 * ng
    for gi, size in enumerate(sizes):
        out.append((res[2 * gi], res[2 * gi + 1], list(res[i:i + size])))
        i += size
    return out, res[-1]


def _gather_wait(tag, send, recv, bufs, after):
    n = len(bufs)
    after = tuple(after) if isinstance(after, (tuple, list)) else (after,)

    def body(*refs):
        ins, send_ref, recv_ref = refs[:n], refs[n], refs[n + 1]
        x, y, c, chips = _place()
        me = 2 * x + y
        for j in range(n):
            for k, (px, py) in enumerate(chips):
                cp = pltpu.make_async_remote_copy(src_ref=ins[j].at[me, c], dst_ref=ins[j].at[2 * px + py, c],
                                                  send_sem=send_ref.at[3 * j + k], recv_sem=recv_ref.at[3 * j + k],
                                                  device_id=(px, py, c), device_id_type=MESH)
                cp.wait_send()
                cp.wait_recv()

    return pl.pallas_call(
        body, name="gather_wait_" + tag,
        in_specs=[HBM_SPEC] * n + [SEM_SPEC, SEM_SPEC] + _any_specs(len(after)),
        out_specs=[HBM_SPEC] * n,
        out_shape=[pltpu.HBM(b.shape, b.dtype) for b in bufs],
        input_output_aliases={i: i for i in range(n)},
        compiler_params=pltpu.CompilerParams(has_side_effects=ORDERED_EFFECT),
    )(*bufs, send, recv, *after)


def _gather_forward(tag, bufs):
    n = len(bufs)

    def body(*refs):
        ins, outs = refs[:n], refs[n:2 * n]
        send, recv = refs[2 * n:]
        x, y, c, chips = _place()
        sib = (x, y, 1 - c)

        def cp(i, k, slot, half):
            return pltpu.make_async_remote_copy(src_ref=ins[i].at[slot, half], dst_ref=outs[i].at[slot, half],
                                                send_sem=send.at[3 * i + k], recv_sem=recv.at[3 * i + k],
                                                device_id=sib, device_id_type=MESH)

        cps = [cp(i, k, 2 * px + py, c) for i in range(n) for k, (px, py) in enumerate(chips)]
        for d in cps:
            d.start()
        for i in range(n):
            for k, (px, py) in enumerate(chips):
                cp(i, k, 2 * px + py, 1 - c).wait_recv()
        for d in cps:
            d.wait_send()

    return pl.pallas_call(
        body, name="gather_forward_" + tag,
        in_specs=_any_specs(n), out_specs=_any_specs(n),
        out_shape=[jax.ShapeDtypeStruct(b.shape, b.dtype) for b in bufs],
        scratch_shapes=[pltpu.SemaphoreType.DMA((3 * n,))] * 2,
        input_output_aliases={i: i for i in range(n)},
        compiler_params=pltpu.CompilerParams(has_side_effects=True),
    )(*bufs)


def _pair_route(srcs, zones):
    x, y, c, _ = _place()
    return [(srcs[i].at[j, 1 - c], zones[i].at[j], (x, y, 1 - c)) for i in range(len(srcs)) for j in range(N_CHIPS)]


def _slab_route(srcs, zones):
    x, y, c, _ = _place()
    return [(srcs[i].at[j], zones[i].at[j], (x, y, 1 - c)) for i in range(len(srcs)) for j in range(N_CHIPS)]


def _chip_route(srcs, zones):
    x, y, c, chips = _place()
    return [(srcs[i].at[2 * px + py], zones[i].at[k], (px, py, c)) for i in range(len(srcs)) for k, (px, py) in enumerate(chips)]


def _all_route(srcs, zones):
    x, y, c, _ = _place()
    flips = [(fx, fy, fc) for fx in (0, 1) for fy in (0, 1) for fc in (0, 1)][1:]
    return [(srcs[0], zones[0].at[4 * x + 2 * y + c], (x + fx - 2 * x * fx, y + fy - 2 * y * fy, c + fc - 2 * c * fc))
            for fx, fy, fc in flips]


def _share_route(srcs, zones):
    x, y, c, _ = _place()
    return [(s.at[c], s.at[c], (x, y, 1 - c)) for s in srcs]


def _exchange_start(name, route, n_copies, srcs, zones):
    n, nz = len(srcs), len(zones)
    lands = [lax.empty(z, a.dtype) if isinstance(z, tuple) else z for z, a in zip(zones, srcs)]

    def body(*refs):
        ins, zone_refs, send, recv, token = refs[:n], refs[n:n + nz], refs[n + nz], refs[n + nz + 1], refs[-1]
        for k, (src, dst, dev) in enumerate(route(ins, zone_refs)):
            pltpu.make_async_remote_copy(src_ref=src, dst_ref=dst, send_sem=send.at[k], recv_sem=recv.at[k],
                                         device_id=dev, device_id_type=MESH).start()
        token[...] = jnp.zeros_like(token)

    tok_shape, tok_spec = _token()
    res = pl.pallas_call(
        body, name=name,
        in_specs=[HBM_SPEC] * (n + nz),
        out_specs=[SEM_SPEC, SEM_SPEC] + [HBM_SPEC] * (n + nz) + [tok_spec],
        out_shape=[pltpu.SemaphoreType.DMA((n_copies,))] * 2 + [pltpu.HBM(a.shape, a.dtype) for a in srcs + lands]
        + [tok_shape],
        input_output_aliases={i: 2 + i for i in range(n + nz)},
        compiler_params=pltpu.CompilerParams(has_side_effects=ORDERED_EFFECT),
    )(*[_in_hbm(a) for a in srcs + lands])
    return (res[0], res[1], list(res[2:2 + n]), list(res[2 + n:2 + n + nz])), res[-1]


def _exchange_wait(name, route, started, after):
    send, recv, srcs, lands = started
    n, nz = len(srcs), len(lands)
    after = tuple(after) if isinstance(after, (tuple, list)) else (after,)

    def body(*refs):
        ins, zone_refs, send_ref, recv_ref = refs[:n], refs[n:n + nz], refs[n + nz], refs[n + nz + 1]
        for k, (src, dst, dev) in enumerate(route(ins, zone_refs)):
            cp = pltpu.make_async_remote_copy(src_ref=src, dst_ref=dst, send_sem=send_ref.at[k], recv_sem=recv_ref.at[k],
                                              device_id=dev, device_id_type=MESH)
            cp.wait_send()
            cp.wait_recv()

    res = pl.pallas_call(
        body, name=name,
        in_specs=[HBM_SPEC] * (n + nz) + [SEM_SPEC, SEM_SPEC] + _any_specs(len(after)),
        out_specs=[HBM_SPEC] * (n + nz),
        out_shape=[pltpu.HBM(a.shape, a.dtype) for a in srcs + lands],
        input_output_aliases={i: i for i in range(n + nz)},
        compiler_params=pltpu.CompilerParams(has_side_effects=ORDERED_EFFECT),
    )(*srcs, *lands, send, recv, *after)
    return list(res[:n]), list(res[n:])


def _spread(v):
    rows, cols = v.shape
    tr = _row_tile(rows, cols, budget=256 * 1024)

    def body(v_ref, o_ref):
        o_ref[...] = jnp.broadcast_to(v_ref[...][None], o_ref.shape)

    return pl.pallas_call(body, name="spread_small_grads", grid=(rows // tr,),
                          in_specs=[pl.BlockSpec((tr, cols), lambda r: (r, 0))],
                          out_specs=pl.BlockSpec((8, tr, cols), lambda r: (0, r, 0)),
                          out_shape=jax.ShapeDtypeStruct((8, rows, cols), v.dtype),
                          compiler_params=_params(("parallel",)))(v)


def _row_tile(rows, cols, itemsize=4, budget=2 * 1024 * 1024, step=8):
    best = None
    for t in range(step, rows + 1, step):
        if rows % t == 0 and t * cols * itemsize <= budget:
            best = t
    return best if best is not None else rows


def _my_chip():
    return 2 * lax.axis_index("x") + lax.axis_index("y")


def _pair_sum(g5, gsib):
    _, _, rh, cols = g5.shape
    tr = _row_tile(rh, cols, step=16)

    def body(a_ref, b_ref, o_ref):
        o_ref[...] = (a_ref[...].astype(F32) + b_ref[...].astype(F32)).astype(o_ref.dtype)

    return pl.pallas_call(body, name="grad_pair_sum", grid=(N_CHIPS, rh // tr),
                          in_specs=[pl.BlockSpec((None, None, tr, cols), lambda j, r: (j, lax.axis_index("c"), r, 0)),
                                    pl.BlockSpec((None, tr, cols), lambda j, r: (j, r, 0))],
                          out_specs=pl.BlockSpec((None, tr, cols), lambda j, r: (j, r, 0)),
                          out_shape=jax.ShapeDtypeStruct((N_CHIPS, rh, cols), BF16),
                          compiler_params=_params(("parallel", "parallel")))(g5, gsib)


def _chip_sum(part, recv):
    _, rh, cols = part.shape
    tr = _row_tile(rh, cols, step=16)

    def body(a_ref, b_ref, o_ref):
        acc = a_ref[...].astype(F32)
        for k in range(3):
            acc = acc + b_ref[k].astype(F32)
        o_ref[...] = acc

    return pl.pallas_call(body, name="grad_chip_sum", grid=(rh // tr,),
                          in_specs=[pl.BlockSpec((None, tr, cols), lambda r: (_my_chip(), r, 0)),
                                    pl.BlockSpec((3, tr, cols), lambda r: (0, r, 0))],
                          out_specs=pl.BlockSpec((None, tr, cols), lambda r: (lax.axis_index("c"), r, 0)),
                          out_shape=jax.ShapeDtypeStruct((2, rh, cols), F32),
                          compiler_params=_params(("parallel",)))(part, recv)


def _sum_devices(g):
    _, rows, cols = g.shape
    tr = _row_tile(rows, cols, budget=256 * 1024)

    def body(g_ref, o_ref):
        acc = g_ref[0]
        for d in range(1, 8):
            acc = acc + g_ref[d]
        o_ref[...] = acc

    return pl.pallas_call(body, name="sum_small_grads", grid=(rows // tr,),
                          in_specs=[pl.BlockSpec((8, tr, cols), lambda r: (0, r, 0))],
                          out_specs=pl.BlockSpec((tr, cols), lambda r: (r, 0)),
                          out_shape=jax.ShapeDtypeStruct((rows, cols), F32),
                          compiler_params=_params(("parallel",)))(g)


def _place_shard(w, layer, dtype, deps=()):
    _, rows, cols = w.shape
    tr = _row_tile(rows, cols)

    def body(i_ref, *rest):
        o_ref = rest[-1]
        o_ref[...] = i_ref[...].astype(o_ref.dtype)

    out = pl.pallas_call(body, name="place_shard", grid=(rows // tr,),
                         in_specs=[pl.BlockSpec((None, tr, cols), lambda r: (layer, r, 0))] + _any_specs(len(deps)),
                         out_specs=pl.BlockSpec((None, tr, cols), lambda r: (_my_chip(), r, 0)),
                         out_shape=jax.ShapeDtypeStruct((N_CHIPS, rows, cols), dtype),
                         compiler_params=_params(("parallel",)))(w, *deps)
    return out.reshape(N_CHIPS, 2, rows // 2, cols)


def _adamw(w, gs, m, v):
    n_layers, rows, cols = w.shape
    tr = _row_tile(rows, cols)

    def body(w_ref, m_ref, v_ref, *rest):
        g_refs = rest[:n_layers]
        go_ref, d_ref, mo_ref, vo_ref = rest[n_layers:]
        gv = g_refs[0][...]
        for layer in range(1, n_layers):
            gv = jnp.where(pl.program_id(0) == layer, g_refs[layer][...], gv)
        d_ref[...], mo_ref[...], vo_ref[...] = _adamw_math(w_ref[...], gv, m_ref[...], v_ref[...])
        go_ref[...] = gv

    spec = pl.BlockSpec((None, tr, cols), lambda layer, r: (layer, r, 0))
    g_specs = [pl.BlockSpec((tr, cols), lambda layer, r, own=own: (jnp.where(layer == own, r, 0), 0))
               for own in range(n_layers)]
    return pl.pallas_call(body, name="adamw", grid=(n_layers, rows // tr), in_specs=[spec] * 3 + g_specs,
                          out_specs=[spec] * 4, out_shape=[jax.ShapeDtypeStruct((n_layers, rows, cols), F32)] * 4,
                          compiler_params=_params(("parallel", "parallel")))(w, m, v, *gs)


def _pad_rope(w):
    z = jnp.zeros(w.shape[:-1] + (ROPE_HALF,), w.dtype)
    return jnp.concatenate([w[..., :ROPE_HALF], z, w[..., ROPE_HALF:], z], axis=-1)


def _unpad_rope(g):
    return jnp.concatenate([g[..., :ROPE_HALF], g[..., ROPE:ROPE + ROPE_HALF]], axis=-1)


def _unstack_cols(s):
    n, r, cs = s.shape
    return jnp.transpose(s, (1, 0, 2)).reshape(r, n * cs)


def _stack_cols(f):
    r, cfull = f.shape
    return jnp.transpose(f.reshape(r, N_CHIPS, cfull // N_CHIPS), (1, 0, 2))


def _small_shard(norm, conv):
    return jnp.concatenate([jnp.pad(norm, ((0, 15), (0, 0))), jnp.pad(conv, ((0, 13), (0, 0)))], axis=0)


def _flat_rows(a):
    return a.reshape(-1, LANES)


def _pack_small(arrs):
    return jnp.concatenate([_flat_rows(a.astype(F32)) for a in arrs], axis=0)


def _unpack_small(flat, like):
    out, r = [], 0
    for a in like:
        n = a.size // LANES
        out.append(flat[r:r + n].reshape(a.shape))
        r += n
    return out


def kernel(x, positions, e_norm_mix, e_w_in, e_q_norm, e_w_uq, e_kv_norm, e_w_ukv, e_v_norm, e_sgu_w, e_sgu_b, e_mla_out_norm, e_sgu_out_norm, e_w_out, o_norm_mix, o_w_in, o_conv_w, o_w_out, mlp_norm, mlp_w1, mlp_w2, final_norm, loss_target, m_e_norm_mix, m_e_w_in, m_e_q_norm, m_e_w_uq, m_e_kv_norm, m_e_w_ukv, m_e_v_norm, m_e_sgu_w, m_e_sgu_b, m_e_mla_out_norm, m_e_sgu_out_norm, m_e_w_out, m_o_norm_mix, m_o_w_in, m_o_conv_w, m_o_w_out, m_mlp_norm, m_mlp_w1, m_mlp_w2, m_final_norm, v_e_norm_mix, v_e_w_in, v_e_q_norm, v_e_w_uq, v_e_kv_norm, v_e_w_ukv, v_e_v_norm, v_e_sgu_w, v_e_sgu_b, v_e_mla_out_norm, v_e_sgu_out_norm, v_e_w_out, v_o_norm_mix, v_o_w_in, v_o_conv_w, v_o_w_out, v_mlp_norm, v_mlp_w1, v_mlp_w2, v_final_norm):
    t, d = x.shape[1], x.shape[2]
    ql, kvl = e_q_norm.shape[1], e_kv_norm.shape[1]
    groups = e_v_norm.shape[1]
    gw = groups * LANES
    heads = N_CHIPS * e_w_uq.shape[2] // (LANES + ROPE)
    hw = heads * LANES
    mix = hw + gw
    ei = N_CHIPS * e_w_in.shape[2]
    cd = N_CHIPS * o_conv_w.shape[2]
    ff = N_CHIPS * mlp_w1.shape[2]
    ffs = ff // N_CHIPS
    pi = 2 * gw + ql + kvl + LANES
    assert e_norm_mix.shape[0] == 1 and o_norm_mix.shape[0] == 1 and mlp_norm.shape[0] == 2
    assert ei == ql + kvl + ROPE + 2 * gw and cd == d and e_sgu_w.shape[2] == LANES
    assert (2 * gw) % ql == 0 and (2 * gw + ql) % kvl == 0 and t % LANES == 0
    scale = (LANES + ROPE) ** -0.5

    tr = min(256, t)
    tm = _pick(t, 1024, 8)
    kt, kd = _pick(t, 2048, 8), _pick(d, 2048)
    xs = x.reshape(t, d)
    tgt = loss_target.reshape(t, d)

    small_shard = _small_shard(o_norm_mix, o_conv_w[0])
    first, tok = _gather_start("gather_start_e", [
        [_place_shard(e_w_in, 0, BF16)],
        [_place_shard(e_w_uq, 0, BF16), _place_shard(e_w_ukv, 0, BF16), _place_shard(e_w_out, 0, BF16),
         _place_shard(small_shard[None], 0, F32)]])
    rest, tok = _gather_start("gather_start_rest", [
        [_place_shard(mlp_w1, 0, BF16, (tok,))], [_place_shard(mlp_w2, 0, BF16, (tok,))],
        [_place_shard(o_w_in, 0, BF16, (tok,)), _place_shard(o_w_out, 0, BF16, (tok,))],
        [_place_shard(mlp_w1, 1, BF16, (tok,))], [_place_shard(mlp_w2, 1, BF16, (tok,))]])
    started = first + rest

    def gathered(gi, tag, after):
        send, recv, bufs = started[gi]
        bufs = _gather_forward(tag, _gather_wait(tag, send, recv, bufs, after))
        return [b.reshape(N_CHIPS, 2 * b.shape[2], b.shape[3]) for b in bufs]

    g_e = e_norm_mix
    h0 = _norm_fwd("e_norm", xs, g_e, tr)
    inv_freq = ROPE_BASE ** (-jnp.arange(0, ROPE, 2, dtype=F32) / ROPE)
    zeros32 = jnp.zeros((ROPE_HALF,), F32)
    ones32 = jnp.ones((ROPE_HALF,), F32)
    invf = jnp.concatenate([inv_freq, zeros32, inv_freq, zeros32]).reshape(1, LANES)
    cmask = jnp.concatenate([ones32, zeros32, ones32, zeros32]).reshape(1, LANES)
    smask = jnp.concatenate([-ones32, zeros32, ones32, zeros32]).reshape(1, LANES)
    ctab, stab = _rope_tables(positions.reshape(t, 1).astype(F32), invf, cmask, smask, tr)

    w_in_g, = gathered(0, "e_in", (h0, ctab, tok))
    full = _unstack_cols(w_in_g)
    c2, c3 = ql + kvl, ql + kvl + ROPE
    w_in_all = jnp.concatenate([full[:, c3:], full[:, :c2], _pad_rope(full[:, c2:c3])], axis=1)
    proj, = _matmul("e_proj", Mat(h0, t, d), Mat(w_in_all, d, pi), "nn", [_out(t, pi, F32)], tm, _pick(pi, 1024), kd)

    w_uq_g, w_ukv_g, w_eout_g, small_g = gathered(1, "e", proj)
    full = _unstack_cols(w_uq_g).reshape(ql, heads, LANES + ROPE)
    w_q_all = jnp.concatenate([full[:, :, :LANES].reshape(ql, hw), _pad_rope(full[:, :, LANES:]).reshape(ql, hw)], axis=1)
    full = _unstack_cols(w_ukv_g).reshape(kvl, heads, 2 * LANES)
    w_kv_all = jnp.concatenate([full[:, :, :LANES].reshape(kvl, hw), full[:, :, LANES:].reshape(kvl, hw)], axis=1)
    w_eout = w_eout_g.reshape(mix, d)
    g_o = small_g[:, 0].reshape(1, d)
    conv_w = jnp.pad(jnp.transpose(small_g[:, 16:19], (1, 0, 2)).reshape(3, cd), ((0, 5), (0, 0)))

    g_q, g_kv = e_q_norm, e_kv_norm
    g_vn = e_v_norm.reshape(1, gw)
    sgu_w = e_sgu_w[0]
    sgu_b = jnp.broadcast_to(e_sgu_b[0][:, :, None], (groups, LANES, LANES))
    g_mla, g_sgu = e_mla_out_norm, e_sgu_out_norm
    g_m0, g_m1 = mlp_norm[0:1], mlp_norm[1:2]
    g_f = final_norm.reshape(1, d)

    def mlp_fwd(tag, xin, g, gi):
        hm = _norm_fwd("mlp_norm_" + tag, xin, g, tr)
        tn = _pick(ffs, 1024)
        w1 = Mat(_unstack_cols(gathered(gi, "w1_" + tag, hm)[0]), d, ff)
        a, act = _matmul("mlp_up_" + tag, Mat(hm, t, d), w1, "nn",
                         [_out(t, ff, BF16), _out(t, ff, BF16)], tm, tn, kd,
                         epilogue=lambda z: (jnp.maximum(z, 0.0), jnp.square(jnp.maximum(z, 0.0))))
        w2 = Mat(gathered(gi + 1, "w2_" + tag, act)[0].reshape(ff, d), ff, d)
        xo, = _matmul("mlp_down_" + tag, Mat(act, t, ff), w2, "nn",
                      [_out(t, d, F32)], tm, _pick(d, 1024), _pick(ffs, 2048),
                      epilogue=lambda z, r: (z + r,), extras=[Mat(xin, t, d)])
        return xo, hm, a, act, w1, w2

    def chip_start(tag, part):
        return _exchange_start("scatter_start_" + tag, _chip_route, 3 * len(part), part, [(3,) + p.shape[1:] for p in part])

    def pair_start(tag, stacked):
        g5 = [g.reshape(N_CHIPS, 2, g.shape[1] // 2, g.shape[2]) for g in stacked]
        return _exchange_start("pair_start_" + tag, _pair_route, N_CHIPS * len(g5), g5,
                               [(N_CHIPS,) + g.shape[2:] for g in g5])

    def pair_finish(tag, started, after):
        g5, from_sib = _exchange_wait("pair_wait_" + tag, _pair_route, started, after)
        return chip_start(tag, [_pair_sum(a, b) for a, b in zip(g5, from_sib)])

    def summed(tag, sc, after):
        part, lands = _exchange_wait("scatter_wait_" + tag, _chip_route, sc, after)
        half = [_chip_sum(p, r) for p, r in zip(part, lands)]
        return _exchange_start("share_start_" + tag, _share_route, len(half), half, [])

    def shared(tag, started, after):
        bufs, _ = _exchange_wait("share_wait_" + tag, _share_route, started, after)
        return [r.reshape(2 * r.shape[1], r.shape[2]) for r in bufs]

    def mlp_bwd(tag, dx, dxb, xin, g, w1, w2, hm, a, act, deps):
        tn = _pick(ffs, 1024)
        hr, hd = ffs // 2, d // 2
        dz, = _matmul("mlp_dact_" + tag, Mat(dxb, t, d), w2, "nt",
                      [_out(t, ff, BF16)], tm, tn, kd,
                      epilogue=lambda z, av: (z * (2.0 * av.astype(F32)),), extras=[Mat(a, t, ff)], deps=deps)

        def half(own):
            c = lax.axis_index("c")
            return c if own else 1 - c

        def act_half(own):
            return Mat(act, t, ff // 2, cmap=lambda cb, bc: (cb // (hr // bc)) * (ffs // bc) + half(own) * (hr // bc)
                       + cb % (hr // bc))

        def hm_half(own):
            return Mat(hm, t, hd, cmap=lambda cb, bc: cb + half(own) * (hd // bc))

        w1_out = lambda: _out(hd, ff, BF16, "colstack", (), (N_CHIPS, hd, ffs))
        theirs2, = _matmul("mlp_dw2_theirs_" + tag, act_half(False), Mat(dxb, t, d), "tn",
                           [_out(ff // 2, d, BF16)], _pick(hr, 1024), _pick(d, 2048), kt)
        theirs1, = _matmul("mlp_dw1_theirs_" + tag, hm_half(False), Mat(dz, t, ff), "tn",
                           [w1_out()], _pick(hd, 2048), tn, kt)
        sent = [theirs1, theirs2.reshape(N_CHIPS, hr, d)]
        started, tok = _exchange_start("pair_start_m" + tag, _slab_route, N_CHIPS * 2, sent, [s.shape for s in sent])
        dhm, = _matmul("mlp_dh_" + tag, Mat(dz, t, ff), w1, "nt",
                       [_out(t, d, F32)], _pick(t, 512, 8), _pick(d, 512), ff, deps=(tok,))
        dxo, dxob, dg = _norm_bwd("mlp_norm_bwd_" + tag, dhm, xin, g, dx, tr)
        _, (sib1, sib2) = _exchange_wait("pair_wait_m" + tag, _slab_route, started, dxo)
        add = lambda z, s: (z + s.astype(F32),)
        part2, = _matmul("mlp_dw2_mine_" + tag, act_half(True), Mat(dxb, t, d), "tn",
                         [_out(ff // 2, d, BF16)], _pick(hr, 1024), _pick(d, 2048), kt,
                         epilogue=add, extras=[Mat(sib2.reshape(ff // 2, d), ff // 2, d)])
        part1, = _matmul("mlp_dw1_mine_" + tag, hm_half(True), Mat(dz, t, ff), "tn",
                         [w1_out()], _pick(hd, 2048), tn, kt, epilogue=add, extras=[Mat(sib1, hd, ff, "colstack")])
        sc, tok = chip_start("m" + tag, [part1, part2.reshape(N_CHIPS, hr, d)])
        return dxo, dxob, dg, sc, tok

    cq_cb, ckv_cb, kr_cb = 2 * gw // ql, (2 * gw + ql) // kvl, (2 * gw + ql + kvl) // LANES
    qn, kvn = _rowwise("qkv_norm", lambda a, b, ga, gb: (_rms(a, ga), _rms(b, gb)), t // tr,
                       [_rt(proj, tr, ql, cq_cb), _rt(proj, tr, kvl, ckv_cb), _whole(g_q), _whole(g_kv)],
                       [_rt_out(t, ql, BF16, tr), _rt_out(t, kvl, BF16, tr)])
    qfull, = _matmul("q_up", Mat(qn, t, ql), Mat(w_q_all, ql, 2 * hw), "nn", [_out(t, 2 * hw, F32)], tm, _pick(2 * hw, 1024), ql)
    kvall, = _matmul("kv_up", Mat(kvn, t, kvl), Mat(w_kv_all, kvl, 2 * hw), "nn", [_out(t, 2 * hw, BF16)], tm, _pick(2 * hw, 1024), kvl)
    qall, kr = _rope_fwd(qfull, proj, kr_cb, ctab, stab, heads, tr)
    att, lse_row = _attn_fwd(qall, kvall, kr, heads, scale, tr)
    rb = min(2 * LANES, t)
    sgu = _sgu_fwd(proj, g_vn, sgu_w, sgu_b, groups, rb)
    mixed = _rowwise("mix_norm", lambda a, s, ga, gs: jnp.concatenate([_rms(a, ga), _rms(s, gs)], axis=1), t // tr,
                     [_rt(att, tr), _rt(sgu, tr), _whole(g_mla), _whole(g_sgu)], [_rt_out(t, mix, BF16, tr)])[0]
    x1, = _matmul("e_out", Mat(mixed, t, mix), Mat(w_eout, mix, d), "nn", [_out(t, d, F32)], tm, _pick(d, 1024), _pick(mix, 2048),
                  epilogue=lambda z, r: (z + r,), extras=[Mat(xs, t, d)])
    x2, hm0, a0, act0, w1_0, w2_0 = mlp_fwd("0", x1, g_m0, 2)

    w_oin_g, w_oout_g = gathered(4, "o", x2)
    w_oout = w_oout_g.reshape(cd, d)
    h1 = _norm_fwd("o_norm", x2, g_o, tr)
    oin = Mat(_unstack_cols(w_oin_g), d, 3 * cd)
    tn_o = _pick(_gcd(3 * cd // N_CHIPS, cd), 512)
    proj3, = _matmul("o_proj", Mat(h1, t, d), oin, "nn", [_out(t, 3 * cd, F32, "colstack", (), (3, t, cd))],
                     tm, _pick(cd, 1024), kd)
    tc = _pick(cd, 256)
    bz = _conv_fwd(proj3, conv_w, tc)
    x3, = _matmul("o_out", Mat(bz, t, cd), Mat(w_oout, cd, d), "nn", [_out(t, d, F32)], tm, _pick(d, 1024), _pick(cd, 2048),
                  epilogue=lambda z, r: (z + r,), extras=[Mat(x2, t, d)])
    x4, hm1, a1, act1, w1_1, w2_1 = mlp_fwd("1", x3, g_m1, 5)

    def final_fn(xv, gv, tv):
        r = lax.rsqrt(jnp.mean(xv * xv, axis=-1, keepdims=True) + EPS)
        xh = xv * r
        err = xh * gv - tv
        dy = err * (1.0 / d)
        dxh = dy * gv
        dx = r * (dxh - xh * jnp.mean(dxh * xh, axis=-1, keepdims=True))
        sq = jnp.sum(err * err, axis=0, keepdims=True)
        part = sq[:, :LANES]
        for k in range(1, d // LANES):
            part = part + sq[:, k * LANES:(k + 1) * LANES]
        return dx, dx, part, jnp.sum(dy * xh, axis=0, keepdims=True)

    dx4, dx4b, loss_vec, dg_f = _rowwise("loss_final_norm", final_fn, t // tr, [_rt(x4, tr), _whole(g_f), _rt(tgt, tr)],
                                         [_rt_out(t, d, F32, tr), _rt_out(t, d, BF16, tr)],
                                         [jax.ShapeDtypeStruct((1, LANES), F32), jax.ShapeDtypeStruct((1, d), F32)])

    dx3, dx3b, dg_m1, sc_m1, tok = mlp_bwd("1", dx4, dx4b, x3, g_m1, w1_1, w2_1, hm1, a1, act1, ())

    dbz, = _matmul("o_out_dx", Mat(dx3b, t, d), Mat(w_oout, cd, d), "nt", [_out(t, cd, F32)], tm, _pick(cd, 1024), kd,
                   deps=(tok,))
    dw_oout, = _matmul("o_out_dw", Mat(bz, t, cd), Mat(dx3b, t, d), "tn", [_out(cd, d, BF16)], _pick(cd, 1024), _pick(d, 1024), kt)
    dproj3, dconv = _conv_bwd(proj3, conv_w, dbz, tc)
    dp3 = Mat(dproj3, t, 3 * cd, "colstack")
    dw_oin, = _matmul("o_proj_dw", Mat(h1, t, d), dp3, "tn", [_out(d, 3 * cd, BF16, "colstack", (), (N_CHIPS, d, 3 * cd // N_CHIPS))],
                      _pick(d, 2048), tn_o, kt)
    started_o, tok = pair_start("o", [dw_oin, dw_oout.reshape(N_CHIPS, cd // N_CHIPS, d)])
    dh1, = _matmul("o_proj_dx", dp3, oin, "nt", [_out(t, d, F32)], tm, _pick(d, 1024), _pick(cd, 2048), deps=(tok,))
    dx2, dx2b, dg_o = _norm_bwd("o_norm_bwd", dh1, x2, g_o, dx3, tr)
    sc_o, tok = pair_finish("o", started_o, dx2)

    dconv_s = jnp.transpose(dconv[:3].reshape(3, N_CHIPS, cd // N_CHIPS), (1, 0, 2))
    gsmall = jnp.concatenate([jnp.pad(dg_o.reshape(N_CHIPS, 1, d // N_CHIPS), ((0, 0), (0, 15), (0, 0))),
                              jnp.pad(dconv_s, ((0, 0), (0, 13), (0, 0)))], axis=1)
    dx1, dx1b, dg_m0, sc_m0, tok = mlp_bwd("0", dx2, dx2b, x1, g_m0, w1_0, w2_0, hm0, a0, act0, (tok,))

    dmixed, = _matmul("e_out_dx", Mat(dx1b, t, d), Mat(w_eout, mix, d), "nt", [_out(t, mix, F32)], tm, _pick(mix, 1024), kd,
                      deps=(tok,))
    dw_eout, = _matmul("e_out_dw", Mat(mixed, t, mix), Mat(dx1b, t, d), "tn", [_out(mix, d, BF16)], _pick(mix, 1024), _pick(d, 1024), kt)

    def mixb_fn(dm, a, s, ga, gs):
        da, dga = _rms_bwd(dm[:, :hw], a, ga)
        dsg, dgs = _rms_bwd(dm[:, hw:], s, gs)
        prod = da * a
        cols = [jnp.broadcast_to(jnp.sum(prod[:, h * LANES:(h + 1) * LANES], axis=-1, keepdims=True), (tr, LANES))
                for h in range(heads)]
        return da, dsg, jnp.stack([_row_of(c) for c in cols], axis=0), dga, dgs

    da_b, dsgu, delta_row, dg_mla, dg_sgu = _rowwise(
        "mix_norm_bwd", mixb_fn, t // tr, [_rt(dmixed, tr), _rt(att, tr), _rt(sgu, tr), _whole(g_mla), _whole(g_sgu)],
        [_rt_out(t, hw, BF16, tr), _rt_out(t, gw, F32, tr),
         (jax.ShapeDtypeStruct((heads, 8, t), F32), pl.BlockSpec((heads, 8, tr), lambda i: (0, 0, i)))],
        [jax.ShapeDtypeStruct((1, hw), F32), jax.ShapeDtypeStruct((1, gw), F32)])

    dproj, dsgu_w, dsgu_b8, dg_vn = _sgu_bwd(proj, dsgu, g_vn, sgu_w, sgu_b, groups, rb)
    dq1, dq2, dk1, dvv, dkr_h = _attn_bwd(qall, kvall, kr, da_b, lse_row, delta_row, heads, scale, tr)
    dqfull, dproj = _rope_bwd(dq1, dq2, dkr_h, ctab, stab, heads, tr, dproj, kr_cb)
    dkvall = jnp.concatenate([dk1, dvv], axis=1)
    dw_q, = _matmul("q_up_dw", Mat(qn, t, ql), Mat(dqfull, t, 2 * hw), "tn", [_out(ql, 2 * hw, BF16)], ql, _pick(2 * hw, 1024), kt)
    dqn, = _matmul("q_up_dx", Mat(dqfull, t, 2 * hw), Mat(w_q_all, ql, 2 * hw), "nt", [_out(t, ql, F32)], tm, ql, _pick(2 * hw, 2048))
    dw_kv, = _matmul("kv_up_dw", Mat(kvn, t, kvl), Mat(dkvall, t, 2 * hw), "tn", [_out(kvl, 2 * hw, BF16)], kvl, _pick(2 * hw, 1024), kt)
    dkvn, = _matmul("kv_up_dx", Mat(dkvall, t, 2 * hw), Mat(w_kv_all, kvl, 2 * hw), "nt", [_out(t, kvl, F32)], tm, kvl, _pick(2 * hw, 2048))

    def qkvb_fn(da, db, a, b, ga, gb):
        dxa, dga = _rms_bwd(da, a, ga)
        dxb, dgb = _rms_bwd(db, b, gb)
        return jnp.concatenate([dxa, dxb], axis=1), dga, dgb

    assert (2 * gw) % (ql + kvl) == 0
    into = (jax.ShapeDtypeStruct(dproj.shape, dproj.dtype),
            pl.BlockSpec((tr, ql + kvl), lambda i: (i, 2 * gw // (ql + kvl))))
    dproj, dg_q, dg_kv = _rowwise(
        "qkv_norm_bwd", qkvb_fn, t // tr,
        [_rt(dqn, tr), _rt(dkvn, tr), _rt(proj, tr, ql, cq_cb), _rt(proj, tr, kvl, ckv_cb), _whole(g_q), _whole(g_kv)],
        [into], [jax.ShapeDtypeStruct((1, ql), F32), jax.ShapeDtypeStruct((1, kvl), F32)], deps=(dproj,), fill=(0, 0))
    dw_in, = _matmul("e_proj_dw", Mat(dproj, t, pi), Mat(h0, t, d), "tn", [_out(pi, d, F32)], _pick(pi, 1024), _pick(d, 2048), kt)
    dh0, = _matmul("e_proj_dx", Mat(dproj, t, pi), Mat(w_in_all, d, pi), "nt", [_out(t, d, F32)], tm, _pick(d, 1024), _pick(pi, 4096))
    dx0, _, dg_e = _norm_bwd("e_norm_bwd", dh0, xs, g_e, dx1, tr)

    kr0 = 2 * gw + c2
    gw_in = jnp.concatenate([dw_in[2 * gw:kr0], dw_in[kr0:kr0 + ROPE_HALF], dw_in[kr0 + ROPE:kr0 + ROPE + ROPE_HALF],
                             dw_in[:2 * gw]], axis=0).reshape(N_CHIPS, ei // N_CHIPS, d)
    gq = jnp.concatenate([dw_q[:, :hw].reshape(ql, heads, LANES), _unpad_rope(dw_q[:, hw:].reshape(ql, heads, LANES))], axis=-1)
    gw_uq = _stack_cols(gq.reshape(ql, heads * (LANES + ROPE)))
    gkv = jnp.concatenate([dw_kv[:, :hw].reshape(kvl, heads, LANES), dw_kv[:, hw:].reshape(kvl, heads, LANES)], axis=-1)
    gw_ukv = _stack_cols(gkv.reshape(kvl, heads * 2 * LANES))
    started_e, tok_pair = pair_start("e", [gw_in, gw_uq, gw_ukv, dw_eout.reshape(N_CHIPS, mix // N_CHIPS, d), gsmall])

    small_like = [e_norm_mix, e_q_norm, e_kv_norm, e_v_norm, e_sgu_w, e_sgu_b, e_mla_out_norm, e_sgu_out_norm, mlp_norm, final_norm]
    small_grads = [dg_e, dg_q, dg_kv, dg_vn, dsgu_w, dsgu_b8[:, 0, :], dg_mla, dg_sgu, jnp.concatenate([dg_m0, dg_m1], axis=0), dg_f]
    packed = _pack_small(small_grads)
    n_small = packed.shape[0] + (-packed.shape[0]) % 8
    pad = n_small - packed.shape[0] + 8
    sflat = jnp.concatenate([jnp.pad(packed, ((0, pad - 8), (0, 0))), jnp.pad(loss_vec, ((0, 7), (0, 0)))], axis=0)
    small_started, tok_small = _exchange_start("small_start", _all_route, 7, [sflat], [_spread(sflat)])

    sh_m1, tok = summed("m1", sc_m1, (tok_pair, tok_small))
    sc_e, tok = pair_finish("e", started_e, tok)
    sh_o, tok = summed("o", sc_o, tok)
    sh_m0, tok = summed("m0", sc_m0, tok)
    r_oin, r_oout = shared("o", sh_o, tok)
    late = {"o_w_in": _adamw(o_w_in, [r_oin], m_o_w_in, v_o_w_in),
            "o_w_out": _adamw(o_w_out, [r_oout], m_o_w_out, v_o_w_out)}
    r_w1_1, r_w2_1 = shared("m1", sh_m1, late["o_w_in"][1])
    r_w1_0, r_w2_0 = shared("m0", sh_m0, r_w2_1)
    late["mlp_w1"] = _adamw(mlp_w1, [r_w1_0, r_w1_1], m_mlp_w1, v_mlp_w1)
    late["mlp_w2"] = _adamw(mlp_w2, [r_w2_0, r_w2_1], m_mlp_w2, v_mlp_w2)

    _, (all_small,) = _exchange_wait("small_wait", _all_route, small_started, late["mlp_w2"][1])
    g_small = _sum_devices(all_small)
    loss = 0.5 * jnp.sum(g_small[n_small]) / d

    def padded(arrs):
        return jnp.pad(_pack_small(arrs), ((0, pad), (0, 0)))

    s_m = [m_e_norm_mix, m_e_q_norm, m_e_kv_norm, m_e_v_norm, m_e_sgu_w, m_e_sgu_b, m_e_mla_out_norm, m_e_sgu_out_norm, m_mlp_norm, m_final_norm]
    s_v = [v_e_norm_mix, v_e_q_norm, v_e_kv_norm, v_e_v_norm, v_e_sgu_w, v_e_sgu_b, v_e_mla_out_norm, v_e_sgu_out_norm, v_mlp_norm, v_final_norm]
    s_out = [_unpack_small(o[0], small_like)
             for o in _adamw(padded(small_like)[None], [g_small], padded(s_m)[None], padded(s_v)[None])]

    sh_e, tok = summed("e", sc_e, late["mlp_w2"][1])
    r_in, r_uq, r_ukv, r_eout, r_small = shared("e", sh_e, tok)
    sm = [o[0] for o in _adamw(small_shard[None], [r_small], _small_shard(m_o_norm_mix, m_o_conv_w[0])[None],
                               _small_shard(v_o_norm_mix, v_o_conv_w[0])[None])]
    big = dict(late)
    flip = lambda a: jnp.swapaxes(a, 1, 2)
    big.update({
        "e_w_in": [flip(o) for o in _adamw(flip(e_w_in), [r_in], flip(m_e_w_in), flip(v_e_w_in))],
        "e_w_uq": _adamw(e_w_uq, [r_uq], m_e_w_uq, v_e_w_uq),
        "e_w_ukv": _adamw(e_w_ukv, [r_ukv], m_e_w_ukv, v_e_w_ukv),
        "e_w_out": _adamw(e_w_out, [r_eout], m_e_w_out, v_e_w_out),
    })

    names = ["e_norm_mix", "e_w_in", "e_q_norm", "e_w_uq", "e_kv_norm", "e_w_ukv", "e_v_norm", "e_sgu_w", "e_sgu_b",
             "e_mla_out_norm", "e_sgu_out_norm", "e_w_out", "o_norm_mix", "o_w_in", "o_conv_w", "o_w_out",
             "mlp_norm", "mlp_w1", "mlp_w2", "final_norm"]
    shapes = {"e_w_in": e_w_in.shape, "e_w_uq": e_w_uq.shape, "e_w_ukv": e_w_ukv.shape, "e_w_out": e_w_out.shape,
              "o_w_in": o_w_in.shape, "o_w_out": o_w_out.shape, "mlp_w1": mlp_w1.shape, "mlp_w2": mlp_w2.shape}
    small_names = ["e_norm_mix", "e_q_norm", "e_kv_norm", "e_v_norm", "e_sgu_w", "e_sgu_b", "e_mla_out_norm",
                   "e_sgu_out_norm", "mlp_norm", "final_norm"]

    def leaf(kind, name):
        if name in big:
            return big[name][kind].reshape(shapes[name])
        if name == "o_norm_mix":
            return sm[kind][0:1]
        if name == "o_conv_w":
            return sm[kind][16:19].reshape(o_conv_w.shape)
        return s_out[kind][small_names.index(name)]

    outs = [loss, dx0.reshape(x.shape)]
    for kind in range(4):
        outs += [leaf(kind, nm) for nm in names]
    return tuple(outs)


def _gcd(a, b):
    while b:
        a, b = b, a % b
    return a
```

```python
import jax
import jax.numpy as jnp
from jax import lax
from jax.experimental import pallas as pl
from jax.experimental.pallas import tpu as pltpu

F32 = jnp.float32
BF16 = jnp.bfloat16
MESH = pl.DeviceIdType.MESH

LANES = 128
ROPE = 64
ROPE_HALF = ROPE // 2
ROPE_BASE = 10000.0
EPS = 1e-6
N_CHIPS = 4
VMEM_LIMIT = 48 * 1024 * 1024
NEG = -1e30

ADAM_LR = 0.001
ADAM_B1 = 0.9
ADAM_B2 = 0.999
ADAM_EPS = 1e-08
ADAM_WD = 0.01
ADAM_STEP = 10


def _pick(n, target, step=LANES):
    best = None
    for t in range(step, min(n, target) + 1, step):
        if n % t == 0:
            best = t
    return best if best is not None else n


def _params(sem, vmem=VMEM_LIMIT):
    return pltpu.CompilerParams(dimension_semantics=sem, vmem_limit_bytes=vmem)


class Mat:
    def __init__(self, arr, rows, cols, kind="plain", lead=(), cmap=None, shape=None, dtype=None):
        self.arr, self.rows, self.cols, self.kind, self.lead, self.cmap = arr, rows, cols, kind, tuple(lead), cmap
        self.shape = tuple(arr.shape) if arr is not None else tuple(shape)
        self.dtype = arr.dtype if arr is not None else dtype

    def sds(self):
        return jax.ShapeDtypeStruct(self.shape, self.dtype)

    def spec(self, br, bc, gridmap):
        lead, nl = self.lead, len(self.lead)
        if self.kind == "plain":
            assert self.rows % br == 0 and self.cols % bc == 0, (self.shape, br, bc)
            cmap = self.cmap if self.cmap is not None else (lambda cb, _: cb)
            block = (None,) * nl + (br, bc)

            def phys(rb, cb):
                return lead + (rb, cmap(cb, bc))
        elif self.kind == "colstack":
            cs = self.shape[-1]
            assert cs % bc == 0 and self.rows % br == 0, (self.shape, br, bc)
            q = cs // bc
            block = (None,) * (nl + 1) + (br, bc)

            def phys(rb, cb):
                return (cb // q,) + lead + (rb, cb % q)
        else:
            rs = self.shape[-2]
            assert rs % br == 0 and self.cols % bc == 0, (self.shape, br, bc)
            q = rs // br
            block = (None,) * (nl + 1) + (br, bc)

            def phys(rb, cb):
                return (rb // q,) + lead + (rb % q, cb)

        return pl.BlockSpec(block, lambda *g: phys(*gridmap(*g)))


def _adamw_math(w, g, m, v):
    mn = ADAM_B1 * m + (1.0 - ADAM_B1) * g
    vn = ADAM_B2 * v + (1.0 - ADAM_B2) * jnp.square(g)
    m_hat = mn / (1.0 - ADAM_B1 ** ADAM_STEP)
    v_hat = vn / (1.0 - ADAM_B2 ** ADAM_STEP)
    return -ADAM_LR * (m_hat / (jnp.sqrt(v_hat) + ADAM_EPS) + ADAM_WD * w), mn, vn


def _matmul(name, a, b, mode, outs, tm, tn, tk, epilogue=None, extras=(), deps=()):
    if mode == "nn":
        m, k, n = a.rows, a.cols, b.cols
        a_spec = a.spec(tm, tk, lambda i, j, kk: (i, kk))
        b_spec = b.spec(tk, tn, lambda i, j, kk: (kk, j))
        dims = (((1,), (0,)), ((), ()))
    elif mode == "nt":
        m, k, n = a.rows, a.cols, b.rows
        a_spec = a.spec(tm, tk, lambda i, j, kk: (i, kk))
        b_spec = b.spec(tn, tk, lambda i, j, kk: (j, kk))
        dims = (((1,), (1,)), ((), ()))
    else:
        k, m, n = a.rows, a.cols, b.cols
        a_spec = a.spec(tk, tm, lambda i, j, kk: (kk, i))
        b_spec = b.spec(tk, tn, lambda i, j, kk: (kk, j))
        dims = (((0,), (0,)), ((), ()))
    assert m % tm == 0 and n % tn == 0 and k % tk == 0, (name, m, n, k, tm, tn, tk)
    grid = (m // tm, n // tn, k // tk)
    nk = grid[2]
    n_ex, n_out, n_dep = len(extras), len(outs), len(deps)
    tile = lambda i, j, kk: (i, j)

    def finish(z, ex, out_refs):
        vals = epilogue(z, *[e[...] for e in ex]) if epilogue is not None else (z,)
        for o, v in zip(out_refs, vals):
            o[...] = v.astype(o.dtype)

    def body_single(a_ref, b_ref, *rest):
        finish(lax.dot_general(a_ref[...], b_ref[...], dims, preferred_element_type=F32),
               rest[:n_ex], rest[n_ex + n_dep:n_ex + n_dep + n_out])

    def body_acc(a_ref, b_ref, *rest):
        acc = rest[-1]
        kk = pl.program_id(2)

        @pl.when(kk == 0)
        def _():
            acc[...] = jnp.zeros_like(acc)

        acc[...] += lax.dot_general(a_ref[...], b_ref[...], dims, preferred_element_type=F32)

        @pl.when(kk == nk - 1)
        def _():
            finish(acc[...], rest[:n_ex], rest[n_ex + n_dep:n_ex + n_dep + n_out])

    res = pl.pallas_call(
        body_single if nk == 1 else body_acc, name=name, grid=grid,
        in_specs=[a_spec, b_spec] + [e.spec(tm, tn, tile) for e in extras]
        + [pl.BlockSpec(memory_space=pl.ANY) for _ in deps],
        out_specs=[o.spec(tm, tn, tile) for o in outs],
        out_shape=[o.sds() for o in outs],
        scratch_shapes=[] if nk == 1 else [pltpu.VMEM((tm, tn), F32)],
        compiler_params=_params(("parallel", "parallel", "arbitrary")),
    )(a.arr, b.arr, *[e.arr for e in extras], *deps)
    return res


def _out(rows, cols, dtype, kind="plain", lead=(), shape=None):
    return Mat(None, rows, cols, kind, lead, shape=shape if shape is not None else (rows, cols), dtype=dtype)


def _rt(arr, tr, width=None, cb=0):
    width = arr.shape[1] if width is None else width
    return arr, pl.BlockSpec((tr, width), lambda i: (i, cb))


def _whole(arr):
    nd = arr.ndim
    return arr, pl.BlockSpec(arr.shape, lambda i: (0,) * nd)


def _rowwise(name, fn, n_steps, ins, outs, accs=(), deps=(), fill=None):
    n_in, n_out, n_acc, n_dep = len(ins), len(outs), len(accs), len(deps)

    def body(*refs):
        vals = fn(*[r[...] for r in refs[:n_in]])
        if not isinstance(vals, (tuple, list)):
            vals = (vals,)
        for ref, v in zip(refs[n_in + n_dep:n_in + n_dep + n_out], vals[:n_out]):
            ref[...] = v.astype(ref.dtype)
        if n_acc:
            acc_refs = refs[n_in + n_dep + n_out:]

            @pl.when(pl.program_id(0) == 0)
            def _():
                for ref in acc_refs:
                    ref[...] = jnp.zeros_like(ref)

            for ref, v in zip(acc_refs, vals[n_out:]):
                ref[...] += v

    acc_specs = [pl.BlockSpec(s.shape, lambda i, nd=len(s.shape): (0,) * nd) for s in accs]
    res = pl.pallas_call(
        body, name=name, grid=(n_steps,),
        in_specs=[s for _, s in ins] + [pl.BlockSpec(memory_space=pl.ANY) for _ in deps],
        out_specs=[s for _, s in outs] + acc_specs,
        out_shape=[o for o, _ in outs] + list(accs),
        input_output_aliases={} if fill is None else {n_in + fill[0]: fill[1]},
        compiler_params=_params(("arbitrary",) if n_acc else ("parallel",)),
    )(*[a for a, _ in ins], *deps)
    return res


def _rt_out(t, width, dtype, tr):
    return jax.ShapeDtypeStruct((t, width), dtype), pl.BlockSpec((tr, width), lambda i: (i, 0))


def _rms(x, g):
    r = lax.rsqrt(jnp.mean(x * x, axis=-1, keepdims=True) + EPS)
    return x * r * g


def _rms_bwd(dy, x, g):
    r = lax.rsqrt(jnp.mean(x * x, axis=-1, keepdims=True) + EPS)
    xh = x * r
    dxh = dy * g
    dx = r * (dxh - xh * jnp.mean(dxh * xh, axis=-1, keepdims=True))
    dg = jnp.sum(dy * xh, axis=0, keepdims=True)
    return dx, dg


def _gelu(x):
    k = 0.7978845608028654
    th = jnp.tanh(k * (x + 0.044715 * (x * x * x)))
    return x * (0.5 * (1.0 + th))


def _gelu_grad(x):
    k = 0.7978845608028654
    x2 = x * x
    th = jnp.tanh(k * (x + 0.044715 * (x2 * x)))
    return 0.5 * (1.0 + th) + 0.5 * x * (1.0 - th * th) * (k * (1.0 + 3.0 * 0.044715 * x2))


def _norm_fwd(name, x, g, tr):
    t, d = x.shape
    return _rowwise(name, lambda xv, gv: _rms(xv, gv), t // tr, [_rt(x, tr), _whole(g)], [_rt_out(t, d, BF16, tr)])[0]


def _norm_bwd(name, dh, x, g, dres, tr):
    t, d = x.shape

    def fn(dhv, xv, gv, drv):
        dx, dg = _rms_bwd(dhv, xv, gv)
        dx = dx + drv
        return dx, dx, dg

    return _rowwise(name, fn, t // tr, [_rt(dh, tr), _rt(x, tr), _whole(g), _rt(dres, tr)],
                    [_rt_out(t, d, F32, tr), _rt_out(t, d, BF16, tr)], [jax.ShapeDtypeStruct((1, d), F32)])


def _rope_tables(posf, invf, cmask, smask, tr):
    t = posf.shape[0]

    def fn(p, f, cm, sm):
        ang = p * f
        return jnp.cos(ang) * cm, jnp.sin(ang) * sm

    return _rowwise("rope_tables", fn, t // tr, [_rt(posf, tr), _whole(invf), _whole(cmask), _whole(smask)],
                    [_rt_out(t, LANES, F32, tr), _rt_out(t, LANES, F32, tr)])


def _rot(v, c, s):
    return v * c + pltpu.roll(v, ROPE, axis=1) * s


def _rot_bwd(dv, c, s):
    return dv * c + pltpu.roll(dv * s, ROPE, axis=1)


def _rope_fwd(qfull, proj, kr_cb, ctab, stab, heads, tr):
    t = qfull.shape[0]
    hw = heads * LANES

    def fn(q, kr, c, s):
        parts = [q[:, :hw]] + [_rot(q[:, hw + h * LANES: hw + (h + 1) * LANES], c, s) for h in range(heads)]
        return jnp.concatenate(parts, axis=1), _rot(kr, c, s)

    return _rowwise("rope_fwd", fn, t // tr, [_rt(qfull, tr), _rt(proj, tr, LANES, kr_cb), _rt(ctab, tr), _rt(stab, tr)],
                    [_rt_out(t, 2 * hw, BF16, tr), _rt_out(t, LANES, BF16, tr)])


def _rope_bwd(dq1, dq2, dkr_h, ctab, stab, heads, tr, dproj, kr_cb):
    t = dq1.shape[0]
    hw = heads * LANES

    def fn(a, b, dk, c, s):
        parts = [a] + [_rot_bwd(b[:, h * LANES:(h + 1) * LANES], c, s) for h in range(heads)]
        dks = dk[0]
        for h in range(1, heads):
            dks = dks + dk[h]
        return jnp.concatenate(parts, axis=1), _rot_bwd(dks, c, s)

    dk_spec = pl.BlockSpec((heads, tr, LANES), lambda i: (0, i, 0))
    into = (jax.ShapeDtypeStruct(dproj.shape, dproj.dtype), pl.BlockSpec((tr, LANES), lambda i: (i, kr_cb)))
    return _rowwise("rope_bwd", fn, t // tr, [_rt(dq1, tr), _rt(dq2, tr), (dkr_h, dk_spec), _rt(ctab, tr), _rt(stab, tr)],
                    [_rt_out(t, 2 * hw, BF16, tr), into], deps=(dproj,), fill=(0, 1))


def _dot_nt(a, b):
    return lax.dot_general(a, b, (((1,), (1,)), ((), ())), preferred_element_type=F32)


def _dot_tn(a, b):
    return lax.dot_general(a, b, (((0,), (0,)), ((), ())), preferred_element_type=F32)


def _dot(a, b):
    return jnp.dot(a, b, preferred_element_type=F32)


def _ranges(n_blocks):
    n_var = min(4, n_blocks)
    assert n_blocks % n_var == 0
    return n_var, n_blocks // n_var


def _row_of(col):
    return col.T[:8, :]


def _attn_fwd(qall, kvall, kr, heads, scale, tq):
    t = qall.shape[0]
    nq = t // tq
    n_var, per = _ranges(nq)

    def body(qn_ref, qr_ref, kn_ref, v_ref, kr_ref, o_ref, lser_ref):
        i = pl.program_id(1)
        for var in range(n_var):
            kv = (var + 1) * per * tq

            @pl.when(jnp.logical_and(i >= var * per, i < (var + 1) * per))
            def _(kv=kv):
                s = _dot_nt(jnp.concatenate([qn_ref[...], qr_ref[...]], axis=1),
                            jnp.concatenate([kn_ref[:kv, :], kr_ref[:kv, :]], axis=1)) * scale
                rows = i * tq + lax.broadcasted_iota(jnp.int32, (tq, kv), 0)
                cols = lax.broadcasted_iota(jnp.int32, (tq, kv), 1)
                s = jnp.where(cols <= rows, s, NEG)
                m = jnp.max(s, axis=-1, keepdims=True)
                p = jnp.exp(s - m)
                l = jnp.sum(p, axis=-1, keepdims=True)
                o_ref[...] = _dot(p.astype(BF16), v_ref[:kv, :]) / l
                lser_ref[...] = _row_of(jnp.broadcast_to(m + jnp.log(l), (tq, LANES)))

    return pl.pallas_call(
        body, name="attn_fwd", grid=(heads, nq),
        in_specs=[pl.BlockSpec((tq, LANES), lambda h, i: (i, h)),
                  pl.BlockSpec((tq, LANES), lambda h, i: (i, heads + h)),
                  pl.BlockSpec((t, LANES), lambda h, i: (0, h)),
                  pl.BlockSpec((t, LANES), lambda h, i: (0, heads + h)),
                  pl.BlockSpec((t, LANES), lambda h, i: (0, 0))],
        out_specs=[pl.BlockSpec((tq, LANES), lambda h, i: (i, h)),
                   pl.BlockSpec((None, 8, tq), lambda h, i: (h, 0, i))],
        out_shape=[jax.ShapeDtypeStruct((t, heads * LANES), F32), jax.ShapeDtypeStruct((heads, 8, t), F32)],
        compiler_params=_params(("parallel", "parallel")),
    )(qall, qall, kvall, kvall, kr)


def _attn_bwd(qall, kvall, kr, do, lse_row, delta_row, heads, scale, tk):
    t = qall.shape[0]
    nk = t // tk
    n_var, per = _ranges(nk)

    def body(qn_ref, qr_ref, kn_ref, v_ref, kr_ref, do_ref, lse_ref, dl_ref, dq1_ref, dq2_ref, dk_ref, dv_ref, dkr_ref):
        j = pl.program_id(1)

        @pl.when(j == 0)
        def _():
            dq1_ref[...] = jnp.zeros_like(dq1_ref)
            dq2_ref[...] = jnp.zeros_like(dq2_ref)

        for var in range(n_var):
            q0 = var * per * tk
            nq = t - q0

            @pl.when(jnp.logical_and(j >= var * per, j < (var + 1) * per))
            def _(q0=q0, nq=nq):
                qn, qr, do_v = qn_ref[q0:, :], qr_ref[q0:, :], do_ref[q0:, :]
                k1, k2 = kn_ref[...], kr_ref[...]
                qcat, kcat = jnp.concatenate([qn, qr], axis=1), jnp.concatenate([k1, k2], axis=1)
                st = _dot_nt(kcat, qcat) * scale
                keys = j * tk + lax.broadcasted_iota(jnp.int32, (tk, nq), 0)
                queries = q0 + lax.broadcasted_iota(jnp.int32, (tk, nq), 1)
                pt = jnp.where(keys <= queries, jnp.exp(st - lse_ref[0:1, q0:]), 0.0)
                dpt = _dot_nt(v_ref[...], do_v)
                dst = (pt * (dpt - dl_ref[0:1, q0:]) * scale).astype(BF16)
                dv_ref[...] = _dot(pt.astype(BF16), do_v).astype(dv_ref.dtype)
                dkc = _dot(dst, qcat)
                dk_ref[...] = dkc[:, :LANES].astype(dk_ref.dtype)
                dkr_ref[...] = dkc[:, LANES:]
                dqc = _dot_tn(dst, kcat)
                dq1_ref[q0:, :] += dqc[:, :LANES]
                dq2_ref[q0:, :] += dqc[:, LANES:]

    kblk = lambda off: pl.BlockSpec((tk, LANES), lambda h, j: (j, off + h))
    full = lambda off: pl.BlockSpec((t, LANES), lambda h, j: (0, off + h))
    stat = pl.BlockSpec((None, 8, t), lambda h, j: (h, 0, 0))
    return pl.pallas_call(
        body, name="attn_bwd", grid=(heads, nk),
        in_specs=[full(0), full(heads), kblk(0), kblk(heads), pl.BlockSpec((tk, LANES), lambda h, j: (j, 0)),
                  full(0), stat, stat],
        out_specs=[full(0), full(0), kblk(0), kblk(0), pl.BlockSpec((None, tk, LANES), lambda h, j: (h, j, 0))],
        out_shape=[jax.ShapeDtypeStruct((t, heads * LANES), F32)] * 2 + [jax.ShapeDtypeStruct((t, heads * LANES), BF16)] * 2
        + [jax.ShapeDtypeStruct((heads, t, LANES), F32)],
        compiler_params=_params(("parallel", "arbitrary")),
    )(qall, qall, kvall, kvall, kr, do, lse_row, delta_row)


def _tril():
    return lax.broadcasted_iota(jnp.int32, (LANES, LANES), 0) >= lax.broadcasted_iota(jnp.int32, (LANES, LANES), 1)


def _group_norm(vg):
    mu = jnp.mean(vg, axis=-1, keepdims=True)
    vc = vg - mu
    rs = lax.rsqrt(jnp.mean(vc * vc, axis=-1, keepdims=True) + EPS)
    return vc * rs, rs


def _sgu_fwd(proj, gain, w, bias, groups, rb):
    t = proj.shape[0]
    gw = groups * LANES
    cpb = rb // LANES

    def body(u_ref, v_ref, gain_ref, w_ref, b_ref, s_ref):
        tril = _tril()
        for g in range(groups):
            wt = jnp.where(tril, w_ref[g], 0.0).astype(BF16)
            cols = slice(g * LANES, (g + 1) * LANES)
            for ci in range(cpb):
                rows = slice(ci * LANES, (ci + 1) * LANES)
                ug = _gelu(u_ref[rows, cols])
                vh, _ = _group_norm(_gelu(v_ref[rows, cols]))
                vn = vh * gain_ref[:, cols]
                y = _dot(wt, vn.astype(BF16)) + b_ref[g]
                s_ref[rows, cols] = ug * y

    return pl.pallas_call(
        body, name="sgu_fwd", grid=(t // rb,),
        in_specs=[pl.BlockSpec((rb, gw), lambda i: (i, 0)), pl.BlockSpec((rb, gw), lambda i: (i, 1)),
                  pl.BlockSpec((1, gw), lambda i: (0, 0)),
                  pl.BlockSpec((groups, LANES, LANES), lambda i: (0, 0, 0)),
                  pl.BlockSpec((groups, LANES, LANES), lambda i: (0, 0, 0))],
        out_specs=pl.BlockSpec((rb, gw), lambda i: (i, 0)),
        out_shape=jax.ShapeDtypeStruct((t, gw), F32),
        compiler_params=_params(("parallel",)),
    )(proj, proj, gain, w, bias)


def _sgu_bwd(proj, ds, gain, w, bias, groups, rb):
    t, width = proj.shape
    gw = groups * LANES
    cpb = rb // LANES
    n_steps = t // rb

    def body(u_ref, v_ref, ds_ref, gain_ref, w_ref, b_ref, dp_ref, dw_ref, db_ref, dg_ref, dy_acc):
        du_ref, dv_ref = dp_ref.at[:, :gw], dp_ref.at[:, gw:]
        step = pl.program_id(0)

        @pl.when(step == 0)
        def _():
            dw_ref[...] = jnp.zeros_like(dw_ref)
            dy_acc[...] = jnp.zeros_like(dy_acc)
            dg_ref[...] = jnp.zeros_like(dg_ref)

        tril = _tril()
        for g in range(groups):
            wt = jnp.where(tril, w_ref[g], 0.0).astype(BF16)
            cols = slice(g * LANES, (g + 1) * LANES)
            gain_g = gain_ref[:, cols]
            for ci in range(cpb):
                rows = slice(ci * LANES, (ci + 1) * LANES)
                u_raw, v_raw, ds_v = u_ref[rows, cols], v_ref[rows, cols], ds_ref[rows, cols]
                ug = _gelu(u_raw)
                vh, rs = _group_norm(_gelu(v_raw))
                vn = (vh * gain_g).astype(BF16)
                y = _dot(wt, vn) + b_ref[g]
                dy = ds_v * ug
                dyb = dy.astype(BF16)
                du_ref[rows, cols] = (ds_v * y * _gelu_grad(u_raw)).astype(du_ref.dtype)
                dy_acc[g] += dy
                dw_ref[g] += _dot_nt(dyb, vn)
                dvn = _dot_tn(wt, dyb)
                dg_ref[:, cols] += jnp.sum(dvn * vh, axis=0, keepdims=True)
                dvh = dvn * gain_g
                dvg = rs * (dvh - jnp.mean(dvh, axis=-1, keepdims=True)
                            - vh * jnp.mean(dvh * vh, axis=-1, keepdims=True))
                dv_ref[rows, cols] = (dvg * _gelu_grad(v_raw)).astype(dv_ref.dtype)

        @pl.when(step == n_steps - 1)
        def _():
            ones = jnp.ones((8, LANES), F32)
            for g in range(groups):
                dw_ref[g] = jnp.where(tril, dw_ref[g], 0.0)
                db_ref[g] = lax.dot_general(ones, dy_acc[g], (((1,), (1,)), ((), ())),
                                            precision=lax.Precision.HIGHEST, preferred_element_type=F32)

    blk = lambda cb: pl.BlockSpec((rb, gw), lambda i: (i, cb))
    whole3 = pl.BlockSpec((groups, LANES, LANES), lambda i: (0, 0, 0))
    return pl.pallas_call(
        body, name="sgu_bwd", grid=(n_steps,),
        in_specs=[blk(0), blk(1), blk(0), pl.BlockSpec((1, gw), lambda i: (0, 0)), whole3, whole3],
        out_specs=[pl.BlockSpec((rb, 2 * gw), lambda i: (i, 0)), whole3,
                   pl.BlockSpec((groups, 8, LANES), lambda i: (0, 0, 0)), pl.BlockSpec((1, gw), lambda i: (0, 0))],
        out_shape=[jax.ShapeDtypeStruct((t, width), BF16),
                   jax.ShapeDtypeStruct((groups, LANES, LANES), F32), jax.ShapeDtypeStruct((groups, 8, LANES), F32),
                   jax.ShapeDtypeStruct((1, gw), F32)],
        scratch_shapes=[pltpu.VMEM((groups, LANES, LANES), F32)],
        compiler_params=_params(("arbitrary",)),
    )(proj, proj, ds, gain, w, bias)


def _shift_down(z, s):
    rows = lax.broadcasted_iota(jnp.int32, z.shape, 0)
    return jnp.where(rows >= s, pltpu.roll(z, s, axis=0), 0.0)


def _shift_up(z, s):
    n = z.shape[0]
    rows = lax.broadcasted_iota(jnp.int32, z.shape, 0)
    return jnp.where(rows < n - s, pltpu.roll(z, n - s, axis=0), 0.0)


def _conv_fwd(proj3, cw, tc):
    _, t, cd = proj3.shape

    def body(p_ref, w_ref, o_ref):
        z = p_ref[1] * p_ref[2]
        w = w_ref[...]
        zc = w[2:3] * z + w[1:2] * _shift_down(z, 1) + w[0:1] * _shift_down(z, 2)
        o_ref[...] = (p_ref[0] * zc).astype(o_ref.dtype)

    return pl.pallas_call(
        body, name="conv_fwd", grid=(cd // tc,),
        in_specs=[pl.BlockSpec((3, t, tc), lambda j: (0, 0, j)), pl.BlockSpec((8, tc), lambda j: (0, j))],
        out_specs=pl.BlockSpec((t, tc), lambda j: (0, j)),
        out_shape=jax.ShapeDtypeStruct((t, cd), BF16),
        compiler_params=_params(("parallel",)),
    )(proj3, cw)


def _conv_bwd(proj3, cw, dbz, tc):
    _, t, cd = proj3.shape

    def body(p_ref, w_ref, d_ref, o_ref, dw_ref):
        b, c, xin = p_ref[0], p_ref[1], p_ref[2]
        w = w_ref[...]
        z = c * xin
        z1, z2 = _shift_down(z, 1), _shift_down(z, 2)
        zc = w[2:3] * z + w[1:2] * z1 + w[0:1] * z2
        d = d_ref[...]
        dzc = d * b
        dz = w[2:3] * dzc + w[1:2] * _shift_up(dzc, 1) + w[0:1] * _shift_up(dzc, 2)
        o_ref[0] = (d * zc).astype(o_ref.dtype)
        o_ref[1] = (dz * xin).astype(o_ref.dtype)
        o_ref[2] = (dz * c).astype(o_ref.dtype)
        row = lax.broadcasted_iota(jnp.int32, (8, tc), 0)
        dw0 = jnp.sum(dzc * z2, axis=0, keepdims=True)
        dw1 = jnp.sum(dzc * z1, axis=0, keepdims=True)
        dw2 = jnp.sum(dzc * z, axis=0, keepdims=True)
        dw_ref[...] = jnp.where(row == 0, dw0, 0.0) + jnp.where(row == 1, dw1, 0.0) + jnp.where(row == 2, dw2, 0.0)

    return pl.pallas_call(
        body, name="conv_bwd", grid=(cd // tc,),
        in_specs=[pl.BlockSpec((3, t, tc), lambda j: (0, 0, j)), pl.BlockSpec((8, tc), lambda j: (0, j)),
                  pl.BlockSpec((t, tc), lambda j: (0, j))],
        out_specs=[pl.BlockSpec((3, t, tc), lambda j: (0, 0, j)), pl.BlockSpec((8, tc), lambda j: (0, j))],
        out_shape=[jax.ShapeDtypeStruct((3, t, cd), BF16), jax.ShapeDtypeStruct((8, cd), F32)],
        compiler_params=_params(("parallel",)),
    )(proj3, cw, dbz)


def _place():
    x, y, c = lax.axis_index("x"), lax.axis_index("y"), lax.axis_index("c")
    chips = [(1 - x, y), (x, 1 - y), (1 - x, 1 - y)]
    return x, y, c, chips


def _any_specs(n):
    return [pl.BlockSpec(memory_space=pl.ANY) for _ in range(n)]


HBM_SPEC = pl.BlockSpec(memory_space=pltpu.HBM)
SEM_SPEC = pl.BlockSpec(memory_space=pltpu.SEMAPHORE)
ORDERED_EFFECT = pltpu.SideEffectType.DATAFLOW_SIDE_EFFECTING


def _in_hbm(a):
    return pltpu.with_memory_space_constraint(a, pltpu.HBM)


def _token():
    return jax.ShapeDtypeStruct((8, LANES), F32), pl.BlockSpec(memory_space=pltpu.VMEM)


def _gather_start(name, groups):
    sizes = [len(g) for g in groups]
    flat = [b for g in groups for b in g]
    n, ng = len(flat), len(groups)

    def body(*refs):
        ins, sems, token = refs[:n], refs[n:n + 2 * ng], refs[-1]
        x, y, c, chips = _place()
        me = 2 * x + y
        i = 0
        for gi, size in enumerate(sizes):
            for j in range(size):
                blk = ins[i].at[me, c]
                for k, chip in enumerate(chips):
                    pltpu.make_async_remote_copy(src_ref=blk, dst_ref=blk, send_sem=sems[2 * gi].at[3 * j + k],
                                                 recv_sem=sems[2 * gi + 1].at[3 * j + k],
                                                 device_id=(*chip, c), device_id_type=MESH).start()
                i += 1
        token[...] = jnp.zeros_like(token)

    tok_shape, tok_spec = _token()
    res = pl.pallas_call(
        body, name=name,
        in_specs=[HBM_SPEC] * n,
        out_specs=[SEM_SPEC] * (2 * ng) + [HBM_SPEC] * n + [tok_spec],
        out_shape=[pltpu.SemaphoreType.DMA((3 * size,)) for size in sizes for _ in (0, 1)]
        + [pltpu.HBM(b.shape, b.dtype) for b in flat] + [tok_shape],
        input_output_aliases={i: 2 * ng + i for i in range(n)},
        compiler_params=pltpu.CompilerParams(has_side_effects=ORDERED_EFFECT),
    )(*[_in_hbm(b) for b in flat])
    out, i = [], 2 * ng
    for gi, size in enumerate(sizes):
        out.append((res[2 * gi], res[2 * gi + 1], list(res[i:i + size])))
        i += size
    return out, res[-1]


def _gather_wait(tag, send, recv, bufs, after):
    n = len(bufs)
    after = tuple(after) if isinstance(after, (tuple, list)) else (after,)

    def body(*refs):
        ins, send_ref, recv_ref = refs[:n], refs[n], refs[n + 1]
        x, y, c, chips = _place()
        me = 2 * x + y
        for j in range(n):
            for k, (px, py) in enumerate(chips):
                cp = pltpu.make_async_remote_copy(src_ref=ins[j].at[me, c], dst_ref=ins[j].at[2 * px + py, c],
                                                  send_sem=send_ref.at[3 * j + k], recv_sem=recv_ref.at[3 * j + k],
                                                  device_id=(px, py, c), device_id_type=MESH)
                cp.wait_send()
                cp.wait_recv()

    return pl.pallas_call(
        body, name="gather_wait_" + tag,
        in_specs=[HBM_SPEC] * n + [SEM_SPEC, SEM_SPEC] + _any_specs(len(after)),
        out_specs=[HBM_SPEC] * n,
        out_shape=[pltpu.HBM(b.shape, b.dtype) for b in bufs],
        input_output_aliases={i: i for i in range(n)},
        compiler_params=pltpu.CompilerParams(has_side_effects=ORDERED_EFFECT),
    )(*bufs, send, recv, *after)


def _gather_forward(tag, bufs):
    n = len(bufs)

    def body(*refs):
        ins, outs = refs[:n], refs[n:2 * n]
        send, recv = refs[2 * n:]
        x, y, c, chips = _place()
        sib = (x, y, 1 - c)

        def cp(i, k, slot, half):
            return pltpu.make_async_remote_copy(src_ref=ins[i].at[slot, half], dst_ref=outs[i].at[slot, half],
                                                send_sem=send.at[3 * i + k], recv_sem=recv.at[3 * i + k],
                                                device_id=sib, device_id_type=MESH)

        cps = [cp(i, k, 2 * px + py, c) for i in range(n) for k, (px, py) in enumerate(chips)]
        for d in cps:
            d.start()
        for i in range(n):
            for k, (px, py) in enumerate(chips):
                cp(i, k, 2 * px + py, 1 - c).wait_recv()
        for d in cps:
            d.wait_send()

    return pl.pallas_call(
        body, name="gather_forward_" + tag,
        in_specs=_any_specs(n), out_specs=_any_specs(n),
        out_shape=[jax.ShapeDtypeStruct(b.shape, b.dtype) for b in bufs],
        scratch_shapes=[pltpu.SemaphoreType.DMA((3 * n,))] * 2,
        input_output_aliases={i: i for i in range(n)},
        compiler_params=pltpu.CompilerParams(has_side_effects=True),
    )(*bufs)


def _pair_route(srcs, zones):
    x, y, c, _ = _place()
    return [(srcs[i].at[j, 1 - c], zones[i].at[j], (x, y, 1 - c)) for i in range(len(srcs)) for j in range(N_CHIPS)]


def _slab_route(srcs, zones):
    x, y, c, _ = _place()
    return [(srcs[i].at[j], zones[i].at[j], (x, y, 1 - c)) for i in range(len(srcs)) for j in range(N_CHIPS)]


def _chip_route(srcs, zones):
    x, y, c, chips = _place()
    return [(srcs[i].at[2 * px + py], zones[i].at[k], (px, py, c)) for i in range(len(srcs)) for k, (px, py) in enumerate(chips)]


def _all_route(srcs, zones):
    x, y, c, _ = _place()
    flips = [(fx, fy, fc) for fx in (0, 1) for fy in (0, 1) for fc in (0, 1)][1:]
    return [(srcs[0], zones[0].at[4 * x + 2 * y + c], (x + fx - 2 * x * fx, y + fy - 2 * y * fy, c + fc - 2 * c * fc))
            for fx, fy, fc in flips]


def _share_route(srcs, zones):
    x, y, c, _ = _place()
    return [(s.at[c], s.at[c], (x, y, 1 - c)) for s in srcs]


def _exchange_start(name, route, n_copies, srcs, zones):
    n, nz = len(srcs), len(zones)
    lands = [lax.empty(z, a.dtype) if isinstance(z, tuple) else z for z, a in zip(zones, srcs)]

    def body(*refs):
        ins, zone_refs, send, recv, token = refs[:n], refs[n:n + nz], refs[n + nz], refs[n + nz + 1], refs[-1]
        for k, (src, dst, dev) in enumerate(route(ins, zone_refs)):
            pltpu.make_async_remote_copy(src_ref=src, dst_ref=dst, send_sem=send.at[k], recv_sem=recv.at[k],
                                         device_id=dev, device_id_type=MESH).start()
        token[...] = jnp.zeros_like(token)

    tok_shape, tok_spec = _token()
    res = pl.pallas_call(
        body, name=name,
        in_specs=[HBM_SPEC] * (n + nz),
        out_specs=[SEM_SPEC, SEM_SPEC] + [HBM_SPEC] * (n + nz) + [tok_spec],
        out_shape=[pltpu.SemaphoreType.DMA((n_copies,))] * 2 + [pltpu.HBM(a.shape, a.dtype) for a in srcs + lands]
        + [tok_shape],
        input_output_aliases={i: 2 + i for i in range(n + nz)},
        compiler_params=pltpu.CompilerParams(has_side_effects=ORDERED_EFFECT),
    )(*[_in_hbm(a) for a in srcs + lands])
    return (res[0], res[1], list(res[2:2 + n]), list(res[2 + n:2 + n + nz])), res[-1]


def _exchange_wait(name, route, started, after):
    send, recv, srcs, lands = started
    n, nz = len(srcs), len(lands)
    after = tuple(after) if isinstance(after, (tuple, list)) else (after,)

    def body(*refs):
        ins, zone_refs, send_ref, recv_ref = refs[:n], refs[n:n + nz], refs[n + nz], refs[n + nz + 1]
        for k, (src, dst, dev) in enumerate(route(ins, zone_refs)):
            cp = pltpu.make_async_remote_copy(src_ref=src, dst_ref=dst, send_sem=send_ref.at[k], recv_sem=recv_ref.at[k],
                                              device_id=dev, device_id_type=MESH)
            cp.wait_send()
            cp.wait_recv()

    res = pl.pallas_call(
        body, name=name,
        in_specs=[HBM_SPEC] * (n + nz) + [SEM_SPEC, SEM_SPEC] + _any_specs(len(after)),
        out_specs=[HBM_SPEC] * (n + nz),
        out_shape=[pltpu.HBM(a.shape, a.dtype) for a in srcs + lands],
        input_output_aliases={i: i for i in range(n + nz)},
        compiler_params=pltpu.CompilerParams(has_side_effects=ORDERED_EFFECT),
    )(*srcs, *lands, send, recv, *after)
    return list(res[:n]), list(res[n:])


def _spread(v):
    rows, cols = v.shape
    tr = _row_tile(rows, cols, budget=256 * 1024)

    def body(v_ref, o_ref):
        o_ref[...] = jnp.broadcast_to(v_ref[...][None], o_ref.shape)

    return pl.pallas_call(body, name="spread_small_grads", grid=(rows // tr,),
                          in_specs=[pl.BlockSpec((tr, cols), lambda r: (r, 0))],
                          out_specs=pl.BlockSpec((8, tr, cols), lambda r: (0, r, 0)),
                          out_shape=jax.ShapeDtypeStruct((8, rows, cols), v.dtype),
                          compiler_params=_params(("parallel",)))(v)


def _row_tile(rows, cols, itemsize=4, budget=2 * 1024 * 1024, step=8):
    best = None
    for t in range(step, rows + 1, step):
        if rows % t == 0 and t * cols * itemsize <= budget:
            best = t
    return best if best is not None else rows


def _my_chip():
    return 2 * lax.axis_index("x") + lax.axis_index("y")


def _pair_sum(g5, gsib):
    _, _, rh, cols = g5.shape
    tr = _row_tile(rh, cols, step=16)

    def body(a_ref, b_ref, o_ref):
        o_ref[...] = (a_ref[...].astype(F32) + b_ref[...].astype(F32)).astype(o_ref.dtype)

    return pl.pallas_call(body, name="grad_pair_sum", grid=(N_CHIPS, rh // tr),
                          in_specs=[pl.BlockSpec((None, None, tr, cols), lambda j, r: (j, lax.axis_index("c"), r, 0)),
                                    pl.BlockSpec((None, tr, cols), lambda j, r: (j, r, 0))],
                          out_specs=pl.BlockSpec((None, tr, cols), lambda j, r: (j, r, 0)),
                          out_shape=jax.ShapeDtypeStruct((N_CHIPS, rh, cols), BF16),
                          compiler_params=_params(("parallel", "parallel")))(g5, gsib)


def _chip_sum(part, recv):
    _, rh, cols = part.shape
    tr = _row_tile(rh, cols, step=16)

    def body(a_ref, b_ref, o_ref):
        acc = a_ref[...].astype(F32)
        for k in range(3):
            acc = acc + b_ref[k].astype(F32)
        o_ref[...] = acc

    return pl.pallas_call(body, name="grad_chip_sum", grid=(rh // tr,),
                          in_specs=[pl.BlockSpec((None, tr, cols), lambda r: (_my_chip(), r, 0)),
                                    pl.BlockSpec((3, tr, cols), lambda r: (0, r, 0))],
                          out_specs=pl.BlockSpec((None, tr, cols), lambda r: (lax.axis_index("c"), r, 0)),
                          out_shape=jax.ShapeDtypeStruct((2, rh, cols), F32),
                          compiler_params=_params(("parallel",)))(part, recv)


def _sum_devices(g):
    _, rows, cols = g.shape
    tr = _row_tile(rows, cols, budget=256 * 1024)

    def body(g_ref, o_ref):
        acc = g_ref[0]
        for d in range(1, 8):
            acc = acc + g_ref[d]
        o_ref[...] = acc

    return pl.pallas_call(body, name="sum_small_grads", grid=(rows // tr,),
                          in_specs=[pl.BlockSpec((8, tr, cols), lambda r: (0, r, 0))],
                          out_specs=pl.BlockSpec((tr, cols), lambda r: (r, 0)),
                          out_shape=jax.ShapeDtypeStruct((rows, cols), F32),
                          compiler_params=_params(("parallel",)))(g)


def _place_shard(w, layer, dtype, deps=()):
    _, rows, cols = w.shape
    tr = _row_tile(rows, cols)

    def body(i_ref, *rest):
        o_ref = rest[-1]
        o_ref[...] = i_ref[...].astype(o_ref.dtype)

    out = pl.pallas_call(body, name="place_shard", grid=(rows // tr,),
                         in_specs=[pl.BlockSpec((None, tr, cols), lambda r: (layer, r, 0))] + _any_specs(len(deps)),
                         out_specs=pl.BlockSpec((None, tr, cols), lambda r: (_my_chip(), r, 0)),
                         out_shape=jax.ShapeDtypeStruct((N_CHIPS, rows, cols), dtype),
                         compiler_params=_params(("parallel",)))(w, *deps)
    return out.reshape(N_CHIPS, 2, rows // 2, cols)


def _adamw(w, gs, m, v):
    n_layers, rows, cols = w.shape
    tr = _row_tile(rows, cols)

    def body(w_ref, m_ref, v_ref, *rest):
        g_refs = rest[:n_layers]
        go_ref, d_ref, mo_ref, vo_ref = rest[n_layers:]
        gv = g_refs[0][...]
        for layer in range(1, n_layers):
            gv = jnp.where(pl.program_id(0) == layer, g_refs[layer][...], gv)
        d_ref[...], mo_ref[...], vo_ref[...] = _adamw_math(w_ref[...], gv, m_ref[...], v_ref[...])
        go_ref[...] = gv

    spec = pl.BlockSpec((None, tr, cols), lambda layer, r: (layer, r, 0))
    g_specs = [pl.BlockSpec((tr, cols), lambda layer, r, own=own: (jnp.where(layer == own, r, 0), 0))
               for own in range(n_layers)]
    return pl.pallas_call(body, name="adamw", grid=(n_layers, rows // tr), in_specs=[spec] * 3 + g_specs,
                          out_specs=[spec] * 4, out_shape=[jax.ShapeDtypeStruct((n_layers, rows, cols), F32)] * 4,
                          compiler_params=_params(("parallel", "parallel")))(w, m, v, *gs)


def _pad_rope(w):
    z = jnp.zeros(w.shape[:-1] + (ROPE_HALF,), w.dtype)
    return jnp.concatenate([w[..., :ROPE_HALF], z, w[..., ROPE_HALF:], z], axis=-1)


def _unpad_rope(g):
    return jnp.concatenate([g[..., :ROPE_HALF], g[..., ROPE:ROPE + ROPE_HALF]], axis=-1)


def _unstack_cols(s):
    n, r, cs = s.shape
    return jnp.transpose(s, (1, 0, 2)).reshape(r, n * cs)


def _stack_cols(f):
    r, cfull = f.shape
    return jnp.transpose(f.reshape(r, N_CHIPS, cfull // N_CHIPS), (1, 0, 2))


def _small_shard(norm, conv):
    return jnp.concatenate([jnp.pad(norm, ((0, 15), (0, 0))), jnp.pad(conv, ((0, 13), (0, 0)))], axis=0)


def _flat_rows(a):
    return a.reshape(-1, LANES)


def _pack_small(arrs):
    return jnp.concatenate([_flat_rows(a.astype(F32)) for a in arrs], axis=0)


def _unpack_small(flat, like):
    out, r = [], 0
    for a in like:
        n = a.size // LANES
        out.append(flat[r:r + n].reshape(a.shape))
        r += n
    return out


def kernel(x, positions, e_norm_mix, e_w_in, e_q_norm, e_w_uq, e_kv_norm, e_w_ukv, e_v_norm, e_sgu_w, e_sgu_b, e_mla_out_norm, e_sgu_out_norm, e_w_out, o_norm_mix, o_w_in, o_conv_w, o_w_out, mlp_norm, mlp_w1, mlp_w2, final_norm, loss_target, m_e_norm_mix, m_e_w_in, m_e_q_norm, m_e_w_uq, m_e_kv_norm, m_e_w_ukv, m_e_v_norm, m_e_sgu_w, m_e_sgu_b, m_e_mla_out_norm, m_e_sgu_out_norm, m_e_w_out, m_o_norm_mix, m_o_w_in, m_o_conv_w, m_o_w_out, m_mlp_norm, m_mlp_w1, m_mlp_w2, m_final_norm, v_e_norm_mix, v_e_w_in, v_e_q_norm, v_e_w_uq, v_e_kv_norm, v_e_w_ukv, v_e_v_norm, v_e_sgu_w, v_e_sgu_b, v_e_mla_out_norm, v_e_sgu_out_norm, v_e_w_out, v_o_norm_mix, v_o_w_in, v_o_conv_w, v_o_w_out, v_mlp_norm, v_mlp_w1, v_mlp_w2, v_final_norm):
    t, d = x.shape[1], x.shape[2]
    ql, kvl = e_q_norm.shape[1], e_kv_norm.shape[1]
    groups = e_v_norm.shape[1]
    gw = groups * LANES
    heads = N_CHIPS * e_w_uq.shape[2] // (LANES + ROPE)
    hw = heads * LANES
    mix = hw + gw
    ei = N_CHIPS * e_w_in.shape[2]
    cd = N_CHIPS * o_conv_w.shape[2]
    ff = N_CHIPS * mlp_w1.shape[2]
    ffs = ff // N_CHIPS
    pi = 2 * gw + ql + kvl + LANES
    assert e_norm_mix.shape[0] == 1 and o_norm_mix.shape[0] == 1 and mlp_norm.shape[0] == 2
    assert ei == ql + kvl + ROPE + 2 * gw and cd == d and e_sgu_w.shape[2] == LANES
    assert (2 * gw) % ql == 0 and (2 * gw + ql) % kvl == 0 and t % LANES == 0
    scale = (LANES + ROPE) ** -0.5

    tr = min(256, t)
    tm = _pick(t, 1024, 8)
    kt, kd = _pick(t, 2048, 8), _pick(d, 2048)
    xs = x.reshape(t, d)
    tgt = loss_target.reshape(t, d)

    small_shard = _small_shard(o_norm_mix, o_conv_w[0])
    first, tok = _gather_start("gather_start_e", [
        [_place_shard(e_w_in, 0, BF16)],
        [_place_shard(e_w_uq, 0, BF16), _place_shard(e_w_ukv, 0, BF16), _place_shard(e_w_out, 0, BF16),
         _place_shard(small_shard[None], 0, F32)]])
    rest, tok = _gather_start("gather_start_rest", [
        [_place_shard(mlp_w1, 0, BF16, (tok,))], [_place_shard(mlp_w2, 0, BF16, (tok,))],
        [_place_shard(o_w_in, 0, BF16, (tok,)), _place_shard(o_w_out, 0, BF16, (tok,))],
        [_place_shard(mlp_w1, 1, BF16, (tok,))], [_place_shard(mlp_w2, 1, BF16, (tok,))]])
    started = first + rest

    def gathered(gi, tag, after):
        send, recv, bufs = started[gi]
        bufs = _gather_forward(tag, _gather_wait(tag, send, recv, bufs, after))
        return [b.reshape(N_CHIPS, 2 * b.shape[2], b.shape[3]) for b in bufs]

    g_e = e_norm_mix
    h0 = _norm_fwd("e_norm", xs, g_e, tr)
    inv_freq = ROPE_BASE ** (-jnp.arange(0, ROPE, 2, dtype=F32) / ROPE)
    zeros32 = jnp.zeros((ROPE_HALF,), F32)
    ones32 = jnp.ones((ROPE_HALF,), F32)
    invf = jnp.concatenate([inv_freq, zeros32, inv_freq, zeros32]).reshape(1, LANES)
    cmask = jnp.concatenate([ones32, zeros32, ones32, zeros32]).reshape(1, LANES)
    smask = jnp.concatenate([-ones32, zeros32, ones32, zeros32]).reshape(1, LANES)
    ctab, stab = _rope_tables(positions.reshape(t, 1).astype(F32), invf, cmask, smask, tr)

    w_in_g, = gathered(0, "e_in", (h0, ctab, tok))
    full = _unstack_cols(w_in_g)
    c2, c3 = ql + kvl, ql + kvl + ROPE
    w_in_all = jnp.concatenate([full[:, c3:], full[:, :c2], _pad_rope(full[:, c2:c3])], axis=1)
    proj, = _matmul("e_proj", Mat(h0, t, d), Mat(w_in_all, d, pi), "nn", [_out(t, pi, F32)], tm, _pick(pi, 1024), kd)

    w_uq_g, w_ukv_g, w_eout_g, small_g = gathered(1, "e", proj)
    full = _unstack_cols(w_uq_g).reshape(ql, heads, LANES + ROPE)
    w_q_all = jnp.concatenate([full[:, :, :LANES].reshape(ql, hw), _pad_rope(full[:, :, LANES:]).reshape(ql, hw)], axis=1)
    full = _unstack_cols(w_ukv_g).reshape(kvl, heads, 2 * LANES)
    w_kv_all = jnp.concatenate([full[:, :, :LANES].reshape(kvl, hw), full[:, :, LANES:].reshape(kvl, hw)], axis=1)
    w_eout = w_eout_g.reshape(mix, d)
    g_o = small_g[:, 0].reshape(1, d)
    conv_w = jnp.pad(jnp.transpose(small_g[:, 16:19], (1, 0, 2)).reshape(3, cd), ((0, 5), (0, 0)))

    g_q, g_kv = e_q_norm, e_kv_norm
    g_vn = e_v_norm.reshape(1, gw)
    sgu_w = e_sgu_w[0]
    sgu_b = jnp.broadcast_to(e_sgu_b[0][:, :, None], (groups, LANES, LANES))
    g_mla, g_sgu = e_mla_out_norm, e_sgu_out_norm
    g_m0, g_m1 = mlp_norm[0:1], mlp_norm[1:2]
    g_f = final_norm.reshape(1, d)

    def mlp_fwd(tag, xin, g, gi):
        hm = _norm_fwd("mlp_norm_" + tag, xin, g, tr)
        tn = _pick(ffs, 1024)
        w1 = Mat(gathered(gi, "w1_" + tag, hm)[0], d, ff, "colstack")
        a, act = _matmul("mlp_up_" + tag, Mat(hm, t, d), w1, "nn",
                         [_out(t, ff, BF16), _out(t, ff, BF16)], tm, tn, kd,
                         epilogue=lambda z: (jnp.maximum(z, 0.0), jnp.square(jnp.maximum(z, 0.0))))
        w2 = Mat(gathered(gi + 1, "w2_" + tag, act)[0].reshape(ff, d), ff, d)
        xo, = _matmul("mlp_down_" + tag, Mat(act, t, ff), w2, "nn",
                      [_out(t, d, F32)], tm, _pick(d, 1024), _pick(ffs, 2048),
                      epilogue=lambda z, r: (z + r,), extras=[Mat(xin, t, d)])
        return xo, hm, a, act, w1, w2

    def chip_start(tag, part):
        return _exchange_start("scatter_start_" + tag, _chip_route, 3 * len(part), part, [(3,) + p.shape[1:] for p in part])

    def pair_start(tag, stacked):
        g5 = [g.reshape(N_CHIPS, 2, g.shape[1] // 2, g.shape[2]) for g in stacked]
        return _exchange_start("pair_start_" + tag, _pair_route, N_CHIPS * len(g5), g5,
                               [(N_CHIPS,) + g.shape[2:] for g in g5])

    def pair_finish(tag, started, after):
        g5, from_sib = _exchange_wait("pair_wait_" + tag, _pair_route, started, after)
        return chip_start(tag, [_pair_sum(a, b) for a, b in zip(g5, from_sib)])

    def summed(tag, sc, after):
        part, lands = _exchange_wait("scatter_wait_" + tag, _chip_route, sc, after)
        half = [_chip_sum(p, r) for p, r in zip(part, lands)]
        return _exchange_start("share_start_" + tag, _share_route, len(half), half, [])

    def shared(tag, started, after):
        bufs, _ = _exchange_wait("share_wait_" + tag, _share_route, started, after)
        return [r.reshape(2 * r.shape[1], r.shape[2]) for r in bufs]

    def mlp_bwd(tag, dx, dxb, xin, g, w1, w2, hm, a, act, deps):
        tn = _pick(ffs, 1024)
        hr, hd = ffs // 2, d // 2
        dz, = _matmul("mlp_dact_" + tag, Mat(dxb, t, d), w2, "nt",
                      [_out(t, ff, BF16)], tm, tn, kd,
                      epilogue=lambda z, av: (z * (2.0 * av.astype(F32)),), extras=[Mat(a, t, ff)], deps=deps)

        def half(own):
            c = lax.axis_index("c")
            return c if own else 1 - c

        def act_half(own):
            return Mat(act, t, ff // 2, cmap=lambda cb, bc: (cb // (hr // bc)) * (ffs // bc) + half(own) * (hr // bc)
                       + cb % (hr // bc))

        def hm_half(own):
            return Mat(hm, t, hd, cmap=lambda cb, bc: cb + half(own) * (hd // bc))

        w1_out = lambda: _out(hd, ff, BF16, "colstack", (), (N_CHIPS, hd, ffs))
        theirs2, = _matmul("mlp_dw2_theirs_" + tag, act_half(False), Mat(dxb, t, d), "tn",
                           [_out(ff // 2, d, BF16)], _pick(hr, 1024), _pick(d, 2048), kt)
        theirs1, = _matmul("mlp_dw1_theirs_" + tag, hm_half(False), Mat(dz, t, ff), "tn",
                           [w1_out()], _pick(hd, 2048), tn, kt)
        sent = [theirs1, theirs2.reshape(N_CHIPS, hr, d)]
        started, tok = _exchange_start("pair_start_m" + tag, _slab_route, N_CHIPS * 2, sent, [s.shape for s in sent])
        dhm, = _matmul("mlp_dh_" + tag, Mat(dz, t, ff), w1, "nt",
                       [_out(t, d, F32)], tm, _pick(d, 1024), _pick(ffs, 2048), deps=(tok,))
        dxo, dxob, dg = _norm_bwd("mlp_norm_bwd_" + tag, dhm, xin, g, dx, tr)
        _, (sib1, sib2) = _exchange_wait("pair_wait_m" + tag, _slab_route, started, dxo)
        add = lambda z, s: (z + s.astype(F32),)
        part2, = _matmul("mlp_dw2_mine_" + tag, act_half(True), Mat(dxb, t, d), "tn",
                         [_out(ff // 2, d, BF16)], _pick(hr, 1024), _pick(d, 2048), kt,
                         epilogue=add, extras=[Mat(sib2.reshape(ff // 2, d), ff // 2, d)])
        part1, = _matmul("mlp_dw1_mine_" + tag, hm_half(True), Mat(dz, t, ff), "tn",
                         [w1_out()], _pick(hd, 2048), tn, kt, epilogue=add, extras=[Mat(sib1, hd, ff, "colstack")])
        sc, tok = chip_start("m" + tag, [part1, part2.reshape(N_CHIPS, hr, d)])
        return dxo, dxob, dg, sc, tok

    cq_cb, ckv_cb, kr_cb = 2 * gw // ql, (2 * gw + ql) // kvl, (2 * gw + ql + kvl) // LANES
    qn, kvn = _rowwise("qkv_norm", lambda a, b, ga, gb: (_rms(a, ga), _rms(b, gb)), t // tr,
                       [_rt(proj, tr, ql, cq_cb), _rt(proj, tr, kvl, ckv_cb), _whole(g_q), _whole(g_kv)],
                       [_rt_out(t, ql, BF16, tr), _rt_out(t, kvl, BF16, tr)])
    qfull, = _matmul("q_up", Mat(qn, t, ql), Mat(w_q_all, ql, 2 * hw), "nn", [_out(t, 2 * hw, F32)], tm, _pick(2 * hw, 1024), ql)
    kvall, = _matmul("kv_up", Mat(kvn, t, kvl), Mat(w_kv_all, kvl, 2 * hw), "nn", [_out(t, 2 * hw, BF16)], tm, _pick(2 * hw, 1024), kvl)
    qall, kr = _rope_fwd(qfull, proj, kr_cb, ctab, stab, heads, tr)
    att, lse_row = _attn_fwd(qall, kvall, kr, heads, scale, tr)
    rb = min(2 * LANES, t)
    sgu = _sgu_fwd(proj, g_vn, sgu_w, sgu_b, groups, rb)
    mixed = _rowwise("mix_norm", lambda a, s, ga, gs: jnp.concatenate([_rms(a, ga), _rms(s, gs)], axis=1), t // tr,
                     [_rt(att, tr), _rt(sgu, tr), _whole(g_mla), _whole(g_sgu)], [_rt_out(t, mix, BF16, tr)])[0]
    x1, = _matmul("e_out", Mat(mixed, t, mix), Mat(w_eout, mix, d), "nn", [_out(t, d, F32)], tm, _pick(d, 1024), _pick(mix, 2048),
                  epilogue=lambda z, r: (z + r,), extras=[Mat(xs, t, d)])
    x2, hm0, a0, act0, w1_0, w2_0 = mlp_fwd("0", x1, g_m0, 2)

    w_oin_g, w_oout_g = gathered(4, "o", x2)
    w_oout = w_oout_g.reshape(cd, d)
    h1 = _norm_fwd("o_norm", x2, g_o, tr)
    oin = Mat(_unstack_cols(w_oin_g), d, 3 * cd)
    tn_o = _pick(_gcd(3 * cd // N_CHIPS, cd), 512)
    proj3, = _matmul("o_proj", Mat(h1, t, d), oin, "nn", [_out(t, 3 * cd, F32, "colstack", (), (3, t, cd))],
                     tm, _pick(cd, 1024), kd)
    tc = _pick(cd, 256)
    bz = _conv_fwd(proj3, conv_w, tc)
    x3, = _matmul("o_out", Mat(bz, t, cd), Mat(w_oout, cd, d), "nn", [_out(t, d, F32)], tm, _pick(d, 1024), _pick(cd, 2048),
                  epilogue=lambda z, r: (z + r,), extras=[Mat(x2, t, d)])
    x4, hm1, a1, act1, w1_1, w2_1 = mlp_fwd("1", x3, g_m1, 5)

    def final_fn(xv, gv, tv):
        r = lax.rsqrt(jnp.mean(xv * xv, axis=-1, keepdims=True) + EPS)
        xh = xv * r
        err = xh * gv - tv
        dy = err * (1.0 / d)
        dxh = dy * gv
        dx = r * (dxh - xh * jnp.mean(dxh * xh, axis=-1, keepdims=True))
        sq = jnp.sum(err * err, axis=0, keepdims=True)
        part = sq[:, :LANES]
        for k in range(1, d // LANES):
            part = part + sq[:, k * LANES:(k + 1) * LANES]
        return dx, dx, part, jnp.sum(dy * xh, axis=0, keepdims=True)

    dx4, dx4b, loss_vec, dg_f = _rowwise("loss_final_norm", final_fn, t // tr, [_rt(x4, tr), _whole(g_f), _rt(tgt, tr)],
                                         [_rt_out(t, d, F32, tr), _rt_out(t, d, BF16, tr)],
                                         [jax.ShapeDtypeStruct((1, LANES), F32), jax.ShapeDtypeStruct((1, d), F32)])

    dx3, dx3b, dg_m1, sc_m1, tok = mlp_bwd("1", dx4, dx4b, x3, g_m1, w1_1, w2_1, hm1, a1, act1, ())

    dbz, = _matmul("o_out_dx", Mat(dx3b, t, d), Mat(w_oout, cd, d), "nt", [_out(t, cd, F32)], tm, _pick(cd, 1024), kd,
                   deps=(tok,))
    dw_oout, = _matmul("o_out_dw", Mat(bz, t, cd), Mat(dx3b, t, d), "tn", [_out(cd, d, BF16)], _pick(cd, 1024), _pick(d, 1024), kt)
    dproj3, dconv = _conv_bwd(proj3, conv_w, dbz, tc)
    dp3 = Mat(dproj3, t, 3 * cd, "colstack")
    dw_oin, = _matmul("o_proj_dw", Mat(h1, t, d), dp3, "tn", [_out(d, 3 * cd, BF16, "colstack", (), (N_CHIPS, d, 3 * cd // N_CHIPS))],
                      _pick(d, 2048), tn_o, kt)
    started_o, tok = pair_start("o", [dw_oin, dw_oout.reshape(N_CHIPS, cd // N_CHIPS, d)])
    dh1, = _matmul("o_proj_dx", dp3, oin, "nt", [_out(t, d, F32)], tm, _pick(d, 1024), _pick(cd, 2048), deps=(tok,))
    dx2, dx2b, dg_o = _norm_bwd("o_norm_bwd", dh1, x2, g_o, dx3, tr)
    sc_o, tok = pair_finish("o", started_o, dx2)

    dconv_s = jnp.transpose(dconv[:3].reshape(3, N_CHIPS, cd // N_CHIPS), (1, 0, 2))
    gsmall = jnp.concatenate([jnp.pad(dg_o.reshape(N_CHIPS, 1, d // N_CHIPS), ((0, 0), (0, 15), (0, 0))),
                              jnp.pad(dconv_s, ((0, 0), (0, 13), (0, 0)))], axis=1)
    dx1, dx1b, dg_m0, sc_m0, tok = mlp_bwd("0", dx2, dx2b, x1, g_m0, w1_0, w2_0, hm0, a0, act0, (tok,))

    dmixed, = _matmul("e_out_dx", Mat(dx1b, t, d), Mat(w_eout, mix, d), "nt", [_out(t, mix, F32)], tm, _pick(mix, 1024), kd,
                      deps=(tok,))
    dw_eout, = _matmul("e_out_dw", Mat(mixed, t, mix), Mat(dx1b, t, d), "tn", [_out(mix, d, BF16)], _pick(mix, 1024), _pick(d, 1024), kt)

    def mixb_fn(dm, a, s, ga, gs):
        da, dga = _rms_bwd(dm[:, :hw], a, ga)
        dsg, dgs = _rms_bwd(dm[:, hw:], s, gs)
        prod = da * a
        cols = [jnp.broadcast_to(jnp.sum(prod[:, h * LANES:(h + 1) * LANES], axis=-1, keepdims=True), (tr, LANES))
                for h in range(heads)]
        return da, dsg, jnp.stack([_row_of(c) for c in cols], axis=0), dga, dgs

    da_b, dsgu, delta_row, dg_mla, dg_sgu = _rowwise(
        "mix_norm_bwd", mixb_fn, t // tr, [_rt(dmixed, tr), _rt(att, tr), _rt(sgu, tr), _whole(g_mla), _whole(g_sgu)],
        [_rt_out(t, hw, BF16, tr), _rt_out(t, gw, F32, tr),
         (jax.ShapeDtypeStruct((heads, 8, t), F32), pl.BlockSpec((heads, 8, tr), lambda i: (0, 0, i)))],
        [jax.ShapeDtypeStruct((1, hw), F32), jax.ShapeDtypeStruct((1, gw), F32)])

    dproj, dsgu_w, dsgu_b8, dg_vn = _sgu_bwd(proj, dsgu, g_vn, sgu_w, sgu_b, groups, rb)
    dq1, dq2, dk1, dvv, dkr_h = _attn_bwd(qall, kvall, kr, da_b, lse_row, delta_row, heads, scale, tr)
    dqfull, dproj = _rope_bwd(dq1, dq2, dkr_h, ctab, stab, heads, tr, dproj, kr_cb)
    dkvall = jnp.concatenate([dk1, dvv], axis=1)
    dw_q, = _matmul("q_up_dw", Mat(qn, t, ql), Mat(dqfull, t, 2 * hw), "tn", [_out(ql, 2 * hw, BF16)], ql, _pick(2 * hw, 1024), kt)
    dqn, = _matmul("q_up_dx", Mat(dqfull, t, 2 * hw), Mat(w_q_all, ql, 2 * hw), "nt", [_out(t, ql, F32)], tm, ql, _pick(2 * hw, 2048))
    dw_kv, = _matmul("kv_up_dw", Mat(kvn, t, kvl), Mat(dkvall, t, 2 * hw), "tn", [_out(kvl, 2 * hw, BF16)], kvl, _pick(2 * hw, 1024), kt)
    dkvn, = _matmul("kv_up_dx", Mat(dkvall, t, 2 * hw), Mat(w_kv_all, kvl, 2 * hw), "nt", [_out(t, kvl, F32)], tm, kvl, _pick(2 * hw, 2048))

    def qkvb_fn(da, db, a, b, ga, gb):
        dxa, dga = _rms_bwd(da, a, ga)
        dxb, dgb = _rms_bwd(db, b, gb)
        return jnp.concatenate([dxa, dxb], axis=1), dga, dgb

    assert (2 * gw) % (ql + kvl) == 0
    into = (jax.ShapeDtypeStruct(dproj.shape, dproj.dtype),
            pl.BlockSpec((tr, ql + kvl), lambda i: (i, 2 * gw // (ql + kvl))))
    dproj, dg_q, dg_kv = _rowwise(
        "qkv_norm_bwd", qkvb_fn, t // tr,
        [_rt(dqn, tr), _rt(dkvn, tr), _rt(proj, tr, ql, cq_cb), _rt(proj, tr, kvl, ckv_cb), _whole(g_q), _whole(g_kv)],
        [into], [jax.ShapeDtypeStruct((1, ql), F32), jax.ShapeDtypeStruct((1, kvl), F32)], deps=(dproj,), fill=(0, 0))
    dw_in, = _matmul("e_proj_dw", Mat(dproj, t, pi), Mat(h0, t, d), "tn", [_out(pi, d, F32)], _pick(pi, 1024), _pick(d, 2048), kt)
    dh0, = _matmul("e_proj_dx", Mat(dproj, t, pi), Mat(w_in_all, d, pi), "nt", [_out(t, d, F32)], tm, _pick(d, 1024), _pick(pi, 4096))
    dx0, _, dg_e = _norm_bwd("e_norm_bwd", dh0, xs, g_e, dx1, tr)

    kr0 = 2 * gw + c2
    gw_in = jnp.concatenate([dw_in[2 * gw:kr0], dw_in[kr0:kr0 + ROPE_HALF], dw_in[kr0 + ROPE:kr0 + ROPE + ROPE_HALF],
                             dw_in[:2 * gw]], axis=0).reshape(N_CHIPS, ei // N_CHIPS, d)
    gq = jnp.concatenate([dw_q[:, :hw].reshape(ql, heads, LANES), _unpad_rope(dw_q[:, hw:].reshape(ql, heads, LANES))], axis=-1)
    gw_uq = _stack_cols(gq.reshape(ql, heads * (LANES + ROPE)))
    gkv = jnp.concatenate([dw_kv[:, :hw].reshape(kvl, heads, LANES), dw_kv[:, hw:].reshape(kvl, heads, LANES)], axis=-1)
    gw_ukv = _stack_cols(gkv.reshape(kvl, heads * 2 * LANES))
    started_e, tok_pair = pair_start("e", [gw_in, gw_uq, gw_ukv, dw_eout.reshape(N_CHIPS, mix // N_CHIPS, d), gsmall])

    small_like = [e_norm_mix, e_q_norm, e_kv_norm, e_v_norm, e_sgu_w, e_sgu_b, e_mla_out_norm, e_sgu_out_norm, mlp_norm, final_norm]
    small_grads = [dg_e, dg_q, dg_kv, dg_vn, dsgu_w, dsgu_b8[:, 0, :], dg_mla, dg_sgu, jnp.concatenate([dg_m0, dg_m1], axis=0), dg_f]
    packed = _pack_small(small_grads)
    n_small = packed.shape[0] + (-packed.shape[0]) % 8
    pad = n_small - packed.shape[0] + 8
    sflat = jnp.concatenate([jnp.pad(packed, ((0, pad - 8), (0, 0))), jnp.pad(loss_vec, ((0, 7), (0, 0)))], axis=0)
    small_started, tok_small = _exchange_start("small_start", _all_route, 7, [sflat], [_spread(sflat)])

    sh_m1, tok = summed("m1", sc_m1, (tok_pair, tok_small))
    sc_e, tok = pair_finish("e", started_e, tok)
    sh_o, tok = summed("o", sc_o, tok)
    sh_m0, tok = summed("m0", sc_m0, tok)
    r_oin, r_oout = shared("o", sh_o, tok)
    late = {"o_w_in": _adamw(o_w_in, [r_oin], m_o_w_in, v_o_w_in),
            "o_w_out": _adamw(o_w_out, [r_oout], m_o_w_out, v_o_w_out)}
    r_w1_1, r_w2_1 = shared("m1", sh_m1, late["o_w_in"][1])
    r_w1_0, r_w2_0 = shared("m0", sh_m0, r_w2_1)
    late["mlp_w1"] = _adamw(mlp_w1, [r_w1_0, r_w1_1], m_mlp_w1, v_mlp_w1)
    late["mlp_w2"] = _adamw(mlp_w2, [r_w2_0, r_w2_1], m_mlp_w2, v_mlp_w2)

    _, (all_small,) = _exchange_wait("small_wait", _all_route, small_started, late["mlp_w2"][1])
    g_small = _sum_devices(all_small)
    loss = 0.5 * jnp.sum(g_small[n_small]) / d

    def padded(arrs):
        return jnp.pad(_pack_small(arrs), ((0, pad), (0, 0)))

    s_m = [m_e_norm_mix, m_e_q_norm, m_e_kv_norm, m_e_v_norm, m_e_sgu_w, m_e_sgu_b, m_e_mla_out_norm, m_e_sgu_out_norm, m_mlp_norm, m_final_norm]
    s_v = [v_e_norm_mix, v_e_q_norm, v_e_kv_norm, v_e_v_norm, v_e_sgu_w, v_e_sgu_b, v_e_mla_out_norm, v_e_sgu_out_norm, v_mlp_norm, v_final_norm]
    s_out = [_unpack_small(o[0], small_like)
             for o in _adamw(padded(small_like)[None], [g_small], padded(s_m)[None], padded(s_v)[None])]

    sh_e, tok = summed("e", sc_e, late["mlp_w2"][1])
    r_in, r_uq, r_ukv, r_eout, r_small = shared("e", sh_e, tok)
    sm = [o[0] for o in _adamw(small_shard[None], [r_small], _small_shard(m_o_norm_mix, m_o_conv_w[0])[None],
                               _small_shard(v_o_norm_mix, v_o_conv_w[0])[None])]
    big = dict(late)
    flip = lambda a: jnp.swapaxes(a, 1, 2)
    big.update({
        "e_w_in": [flip(o) for o in _adamw(flip(e_w_in), [r_in], flip(m_e_w_in), flip(v_e_w_in))],
        "e_w_uq": _adamw(e_w_uq, [r_uq], m_e_w_uq, v_e_w_uq),
        "e_w_ukv": _adamw(e_w_ukv, [r_ukv], m_e_w_ukv, v_e_w_ukv),
        "e_w_out": _adamw(e_w_out, [r_eout], m_e_w_out, v_e_w_out),
    })

    names = ["e_norm_mix", "e_w_in", "e_q_norm", "e_w_uq", "e_kv_norm", "e_w_ukv", "e_v_norm", "e_sgu_w", "e_sgu_b",
             "e_mla_out_norm", "e_sgu_out_norm", "e_w_out", "o_norm_mix", "o_w_in", "o_conv_w", "o_w_out",
             "mlp_norm", "mlp_w1", "mlp_w2", "final_norm"]
    shapes = {"e_w_in": e_w_in.shape, "e_w_uq": e_w_uq.shape, "e_w_ukv": e_w_ukv.shape, "e_w_out": e_w_out.shape,
              "o_w_in": o_w_in.shape, "o_w_out": o_w_out.shape, "mlp_w1": mlp_w1.shape, "mlp_w2": mlp_w2.shape}
    small_names = ["e_norm_mix", "e_q_norm", "e_kv_norm", "e_v_norm", "e_sgu_w", "e_sgu_b", "e_mla_out_norm",
                   "e_sgu_out_norm", "mlp_norm", "final_norm"]

    def leaf(kind, name):
        if name in big:
            return big[name][kind].reshape(shapes[name])
        if name == "o_norm_mix":
            return sm[kind][0:1]
        if name == "o_conv_w":
            return sm[kind][16:19].reshape(o_conv_w.shape)
        return s_out[kind][small_names.index(name)]

    outs = [loss, dx0.reshape(x.shape)]
    for kind in range(4):
        outs += [leaf(kind, nm) for nm in names]
    return tuple(outs)


def _gcd(a, b):
    while b:
        a, b = b, a % b
    return a
```

```python
import jax
import jax.numpy as jnp
from jax import lax
from jax.experimental import pallas as pl
from jax.experimental.pallas import tpu as pltpu

F32 = jnp.float32
BF16 = jnp.bfloat16
MESH = pl.DeviceIdType.MESH

LANES = 128
ROPE = 64
ROPE_HALF = ROPE // 2
ROPE_BASE = 10000.0
EPS = 1e-6
N_CHIPS = 4
VMEM_LIMIT = 48 * 1024 * 1024
NEG = -1e30

ADAM_LR = 0.001
ADAM_B1 = 0.9
ADAM_B2 = 0.999
ADAM_EPS = 1e-08
ADAM_WD = 0.01
ADAM_STEP = 10


def _pick(n, target, step=LANES):
    best = None
    for t in range(step, min(n, target) + 1, step):
        if n % t == 0:
            best = t
    return best if best is not None else n


def _params(sem, vmem=VMEM_LIMIT):
    return pltpu.CompilerParams(dimension_semantics=sem, vmem_limit_bytes=vmem)


class Mat:
    def __init__(self, arr, rows, cols, kind="plain", lead=(), cmap=None, shape=None, dtype=None):
        self.arr, self.rows, self.cols, self.kind, self.lead, self.cmap = arr, rows, cols, kind, tuple(lead), cmap
        self.shape = tuple(arr.shape) if arr is not None else tuple(shape)
        self.dtype = arr.dtype if arr is not None else dtype

    def sds(self):
        return jax.ShapeDtypeStruct(self.shape, self.dtype)

    def spec(self, br, bc, gridmap):
        lead, nl = self.lead, len(self.lead)
        if self.kind == "plain":
            assert self.rows % br == 0 and self.cols % bc == 0, (self.shape, br, bc)
            cmap = self.cmap if self.cmap is not None else (lambda cb, _: cb)
            block = (None,) * nl + (br, bc)

            def phys(rb, cb):
                return lead + (rb, cmap(cb, bc))
        elif self.kind == "colstack":
            cs = self.shape[-1]
            assert cs % bc == 0 and self.rows % br == 0, (self.shape, br, bc)
            q = cs // bc
            block = (None,) * (nl + 1) + (br, bc)

            def phys(rb, cb):
                return (cb // q,) + lead + (rb, cb % q)
        else:
            rs = self.shape[-2]
            assert rs % br == 0 and self.cols % bc == 0, (self.shape, br, bc)
            q = rs // br
            block = (None,) * (nl + 1) + (br, bc)

            def phys(rb, cb):
                return (rb // q,) + lead + (rb % q, cb)

        return pl.BlockSpec(block, lambda *g: phys(*gridmap(*g)))


def _adamw_math(w, g, m, v):
    mn = ADAM_B1 * m + (1.0 - ADAM_B1) * g
    vn = ADAM_B2 * v + (1.0 - ADAM_B2) * jnp.square(g)
    m_hat = mn / (1.0 - ADAM_B1 ** ADAM_STEP)
    v_hat = vn / (1.0 - ADAM_B2 ** ADAM_STEP)
    return -ADAM_LR * (m_hat / (jnp.sqrt(v_hat) + ADAM_EPS) + ADAM_WD * w), mn, vn


def _matmul(name, a, b, mode, outs, tm, tn, tk, epilogue=None, extras=(), deps=()):
    if mode == "nn":
        m, k, n = a.rows, a.cols, b.cols
        a_spec = a.spec(tm, tk, lambda i, j, kk: (i, kk))
        b_spec = b.spec(tk, tn, lambda i, j, kk: (kk, j))
        dims = (((1,), (0,)), ((), ()))
    elif mode == "nt":
        m, k, n = a.rows, a.cols, b.rows
        a_spec = a.spec(tm, tk, lambda i, j, kk: (i, kk))
        b_spec = b.spec(tn, tk, lambda i, j, kk: (j, kk))
        dims = (((1,), (1,)), ((), ()))
    else:
        k, m, n = a.rows, a.cols, b.cols
        a_spec = a.spec(tk, tm, lambda i, j, kk: (kk, i))
        b_spec = b.spec(tk, tn, lambda i, j, kk: (kk, j))
        dims = (((0,), (0,)), ((), ()))
    assert m % tm == 0 and n % tn == 0 and k % tk == 0, (name, m, n, k, tm, tn, tk)
    grid = (m // tm, n // tn, k // tk)
    nk = grid[2]
    n_ex, n_out, n_dep = len(extras), len(outs), len(deps)
    tile = lambda i, j, kk: (i, j)

    def finish(z, ex, out_refs):
        vals = epilogue(z, *[e[...] for e in ex]) if epilogue is not None else (z,)
        for o, v in zip(out_refs, vals):
            o[...] = v.astype(o.dtype)

    def body_single(a_ref, b_ref, *rest):
        finish(lax.dot_general(a_ref[...], b_ref[...], dims, preferred_element_type=F32),
               rest[:n_ex], rest[n_ex + n_dep:n_ex + n_dep + n_out])

    def body_acc(a_ref, b_ref, *rest):
        acc = rest[-1]
        kk = pl.program_id(2)

        @pl.when(kk == 0)
        def _():
            acc[...] = jnp.zeros_like(acc)

        acc[...] += lax.dot_general(a_ref[...], b_ref[...], dims, preferred_element_type=F32)

        @pl.when(kk == nk - 1)
        def _():
            finish(acc[...], rest[:n_ex], rest[n_ex + n_dep:n_ex + n_dep + n_out])

    res = pl.pallas_call(
        body_single if nk == 1 else body_acc, name=name, grid=grid,
        in_specs=[a_spec, b_spec] + [e.spec(tm, tn, tile) for e in extras]
        + [pl.BlockSpec(memory_space=pl.ANY) for _ in deps],
        out_specs=[o.spec(tm, tn, tile) for o in outs],
        out_shape=[o.sds() for o in outs],
        scratch_shapes=[] if nk == 1 else [pltpu.VMEM((tm, tn), F32)],
        compiler_params=_params(("parallel", "parallel", "arbitrary")),
    )(a.arr, b.arr, *[e.arr for e in extras], *deps)
    return res


def _out(rows, cols, dtype, kind="plain", lead=(), shape=None):
    return Mat(None, rows, cols, kind, lead, shape=shape if shape is not None else (rows, cols), dtype=dtype)


def _rt(arr, tr, width=None, cb=0):
    width = arr.shape[1] if width is None else width
    return arr, pl.BlockSpec((tr, width), lambda i: (i, cb))


def _whole(arr):
    nd = arr.ndim
    return arr, pl.BlockSpec(arr.shape, lambda i: (0,) * nd)


def _rowwise(name, fn, n_steps, ins, outs, accs=(), deps=(), fill=None):
    n_in, n_out, n_acc, n_dep = len(ins), len(outs), len(accs), len(deps)

    def body(*refs):
        vals = fn(*[r[...] for r in refs[:n_in]])
        if not isinstance(vals, (tuple, list)):
            vals = (vals,)
        for ref, v in zip(refs[n_in + n_dep:n_in + n_dep + n_out], vals[:n_out]):
            ref[...] = v.astype(ref.dtype)
        if n_acc:
            acc_refs = refs[n_in + n_dep + n_out:]

            @pl.when(pl.program_id(0) == 0)
            def _():
                for ref in acc_refs:
                    ref[...] = jnp.zeros_like(ref)

            for ref, v in zip(acc_refs, vals[n_out:]):
                ref[...] += v

    acc_specs = [pl.BlockSpec(s.shape, lambda i, nd=len(s.shape): (0,) * nd) for s in accs]
    res = pl.pallas_call(
        body, name=name, grid=(n_steps,),
        in_specs=[s for _, s in ins] + [pl.BlockSpec(memory_space=pl.ANY) for _ in deps],
        out_specs=[s for _, s in outs] + acc_specs,
        out_shape=[o for o, _ in outs] + list(accs),
        input_output_aliases={} if fill is None else {n_in + fill[0]: fill[1]},
        compiler_params=_params(("arbitrary",) if n_acc else ("parallel",)),
    )(*[a for a, _ in ins], *deps)
    return res


def _rt_out(t, width, dtype, tr):
    return jax.ShapeDtypeStruct((t, width), dtype), pl.BlockSpec((tr, width), lambda i: (i, 0))


def _rms(x, g):
    r = lax.rsqrt(jnp.mean(x * x, axis=-1, keepdims=True) + EPS)
    return x * r * g


def _rms_bwd(dy, x, g):
    r = lax.rsqrt(jnp.mean(x * x, axis=-1, keepdims=True) + EPS)
    xh = x * r
    dxh = dy * g
    dx = r * (dxh - xh * jnp.mean(dxh * xh, axis=-1, keepdims=True))
    dg = jnp.sum(dy * xh, axis=0, keepdims=True)
    return dx, dg


def _gelu(x):
    k = 0.7978845608028654
    th = jnp.tanh(k * (x + 0.044715 * (x * x * x)))
    return x * (0.5 * (1.0 + th))


def _gelu_grad(x):
    k = 0.7978845608028654
    x2 = x * x
    th = jnp.tanh(k * (x + 0.044715 * (x2 * x)))
    return 0.5 * (1.0 + th) + 0.5 * x * (1.0 - th * th) * (k * (1.0 + 3.0 * 0.044715 * x2))


def _norm_fwd(name, x, g, tr):
    t, d = x.shape
    return _rowwise(name, lambda xv, gv: _rms(xv, gv), t // tr, [_rt(x, tr), _whole(g)], [_rt_out(t, d, BF16, tr)])[0]


def _norm_bwd(name, dh, x, g, dres, tr):
    t, d = x.shape

    def fn(dhv, xv, gv, drv):
        dx, dg = _rms_bwd(dhv, xv, gv)
        dx = dx + drv
        return dx, dx, dg

    return _rowwise(name, fn, t // tr, [_rt(dh, tr), _rt(x, tr), _whole(g), _rt(dres, tr)],
                    [_rt_out(t, d, F32, tr), _rt_out(t, d, BF16, tr)], [jax.ShapeDtypeStruct((1, d), F32)])


def _rope_tables(posf, invf, cmask, smask, tr):
    t = posf.shape[0]

    def fn(p, f, cm, sm):
        ang = p * f
        return jnp.cos(ang) * cm, jnp.sin(ang) * sm

    return _rowwise("rope_tables", fn, t // tr, [_rt(posf, tr), _whole(invf), _whole(cmask), _whole(smask)],
                    [_rt_out(t, LANES, F32, tr), _rt_out(t, LANES, F32, tr)])


def _rot(v, c, s):
    return v * c + pltpu.roll(v, ROPE, axis=1) * s


def _rot_bwd(dv, c, s):
    return dv * c + pltpu.roll(dv * s, ROPE, axis=1)


def _rope_fwd(qfull, proj, kr_cb, ctab, stab, heads, tr):
    t = qfull.shape[0]
    hw = heads * LANES

    def fn(q, kr, c, s):
        parts = [q[:, :hw]] + [_rot(q[:, hw + h * LANES: hw + (h + 1) * LANES], c, s) for h in range(heads)]
        return jnp.concatenate(parts, axis=1), _rot(kr, c, s)

    return _rowwise("rope_fwd", fn, t // tr, [_rt(qfull, tr), _rt(proj, tr, LANES, kr_cb), _rt(ctab, tr), _rt(stab, tr)],
                    [_rt_out(t, 2 * hw, BF16, tr), _rt_out(t, LANES, BF16, tr)])


def _rope_bwd(dq1, dq2, dkr_h, ctab, stab, heads, tr, dproj, kr_cb):
    t = dq1.shape[0]
    hw = heads * LANES

    def fn(a, b, dk, c, s):
        parts = [a] + [_rot_bwd(b[:, h * LANES:(h + 1) * LANES], c, s) for h in range(heads)]
        dks = dk[0]
        for h in range(1, heads):
            dks = dks + dk[h]
        return jnp.concatenate(parts, axis=1), _rot_bwd(dks, c, s)

    dk_spec = pl.BlockSpec((heads, tr, LANES), lambda i: (0, i, 0))
    into = (jax.ShapeDtypeStruct(dproj.shape, dproj.dtype), pl.BlockSpec((tr, LANES), lambda i: (i, kr_cb)))
    return _rowwise("rope_bwd", fn, t // tr, [_rt(dq1, tr), _rt(dq2, tr), (dkr_h, dk_spec), _rt(ctab, tr), _rt(stab, tr)],
                    [_rt_out(t, 2 * hw, BF16, tr), into], deps=(dproj,), fill=(0, 1))


def _dot_nt(a, b):
    return lax.dot_general(a, b, (((1,), (1,)), ((), ())), preferred_element_type=F32)


def _dot_tn(a, b):
    return lax.dot_general(a, b, (((0,), (0,)), ((), ())), preferred_element_type=F32)


def _dot(a, b):
    return jnp.dot(a, b, preferred_element_type=F32)


def _ranges(n_blocks):
    n_var = min(4, n_blocks)
    assert n_blocks % n_var == 0
    return n_var, n_blocks // n_var


def _row_of(col):
    return col.T[:8, :]


def _attn_fwd(qall, kvall, kr, heads, scale, tq):
    t = qall.shape[0]
    nq = t // tq
    n_var, per = _ranges(nq)

    def body(qn_ref, qr_ref, kn_ref, v_ref, kr_ref, o_ref, lser_ref):
        i = pl.program_id(1)
        for var in range(n_var):
            kv = (var + 1) * per * tq

            @pl.when(jnp.logical_and(i >= var * per, i < (var + 1) * per))
            def _(kv=kv):
                s = _dot_nt(jnp.concatenate([qn_ref[...], qr_ref[...]], axis=1),
                            jnp.concatenate([kn_ref[:kv, :], kr_ref[:kv, :]], axis=1)) * scale
                rows = i * tq + lax.broadcasted_iota(jnp.int32, (tq, kv), 0)
                cols = lax.broadcasted_iota(jnp.int32, (tq, kv), 1)
                s = jnp.where(cols <= rows, s, NEG)
                m = jnp.max(s, axis=-1, keepdims=True)
                p = jnp.exp(s - m)
                l = jnp.sum(p, axis=-1, keepdims=True)
                o_ref[...] = _dot(p.astype(BF16), v_ref[:kv, :]) / l
                lser_ref[...] = _row_of(jnp.broadcast_to(m + jnp.log(l), (tq, LANES)))

    return pl.pallas_call(
        body, name="attn_fwd", grid=(heads, nq),
        in_specs=[pl.BlockSpec((tq, LANES), lambda h, i: (i, h)),
                  pl.BlockSpec((tq, LANES), lambda h, i: (i, heads + h)),
                  pl.BlockSpec((t, LANES), lambda h, i: (0, h)),
                  pl.BlockSpec((t, LANES), lambda h, i: (0, heads + h)),
                  pl.BlockSpec((t, LANES), lambda h, i: (0, 0))],
        out_specs=[pl.BlockSpec((tq, LANES), lambda h, i: (i, h)),
                   pl.BlockSpec((None, 8, tq), lambda h, i: (h, 0, i))],
        out_shape=[jax.ShapeDtypeStruct((t, heads * LANES), F32), jax.ShapeDtypeStruct((heads, 8, t), F32)],
        compiler_params=_params(("parallel", "parallel")),
    )(qall, qall, kvall, kvall, kr)


def _attn_bwd(qall, kvall, kr, do, lse_row, delta_row, heads, scale, tk):
    t = qall.shape[0]
    nk = t // tk
    n_var, per = _ranges(nk)

    def body(qn_ref, qr_ref, kn_ref, v_ref, kr_ref, do_ref, lse_ref, dl_ref, dq1_ref, dq2_ref, dk_ref, dv_ref, dkr_ref):
        j = pl.program_id(1)

        @pl.when(j == 0)
        def _():
            dq1_ref[...] = jnp.zeros_like(dq1_ref)
            dq2_ref[...] = jnp.zeros_like(dq2_ref)

        for var in range(n_var):
            q0 = var * per * tk
            nq = t - q0

            @pl.when(jnp.logical_and(j >= var * per, j < (var + 1) * per))
            def _(q0=q0, nq=nq):
                qn, qr, do_v = qn_ref[q0:, :], qr_ref[q0:, :], do_ref[q0:, :]
                k1, k2 = kn_ref[...], kr_ref[...]
                qcat, kcat = jnp.concatenate([qn, qr], axis=1), jnp.concatenate([k1, k2], axis=1)
                st = _dot_nt(kcat, qcat) * scale
                keys = j * tk + lax.broadcasted_iota(jnp.int32, (tk, nq), 0)
                queries = q0 + lax.broadcasted_iota(jnp.int32, (tk, nq), 1)
                pt = jnp.where(keys <= queries, jnp.exp(st - lse_ref[0:1, q0:]), 0.0)
                dpt = _dot_nt(v_ref[...], do_v)
                dst = (pt * (dpt - dl_ref[0:1, q0:]) * scale).astype(BF16)
                dv_ref[...] = _dot(pt.astype(BF16), do_v).astype(dv_ref.dtype)
                dkc = _dot(dst, qcat)
                dk_ref[...] = dkc[:, :LANES].astype(dk_ref.dtype)
                dkr_ref[...] = dkc[:, LANES:]
                dqc = _dot_tn(dst, kcat)
                dq1_ref[q0:, :] += dqc[:, :LANES]
                dq2_ref[q0:, :] += dqc[:, LANES:]

    kblk = lambda off: pl.BlockSpec((tk, LANES), lambda h, j: (j, off + h))
    full = lambda off: pl.BlockSpec((t, LANES), lambda h, j: (0, off + h))
    stat = pl.BlockSpec((None, 8, t), lambda h, j: (h, 0, 0))
    return pl.pallas_call(
        body, name="attn_bwd", grid=(heads, nk),
        in_specs=[full(0), full(heads), kblk(0), kblk(heads), pl.BlockSpec((tk, LANES), lambda h, j: (j, 0)),
                  full(0), stat, stat],
        out_specs=[full(0), full(0), kblk(0), kblk(0), pl.BlockSpec((None, tk, LANES), lambda h, j: (h, j, 0))],
        out_shape=[jax.ShapeDtypeStruct((t, heads * LANES), F32)] * 2 + [jax.ShapeDtypeStruct((t, heads * LANES), BF16)] * 2
        + [jax.ShapeDtypeStruct((heads, t, LANES), F32)],
        compiler_params=_params(("parallel", "arbitrary")),
    )(qall, qall, kvall, kvall, kr, do, lse_row, delta_row)


def _tril():
    return lax.broadcasted_iota(jnp.int32, (LANES, LANES), 0) >= lax.broadcasted_iota(jnp.int32, (LANES, LANES), 1)


def _group_norm(vg):
    mu = jnp.mean(vg, axis=-1, keepdims=True)
    vc = vg - mu
    rs = lax.rsqrt(jnp.mean(vc * vc, axis=-1, keepdims=True) + EPS)
    return vc * rs, rs


def _sgu_fwd(proj, gain, w, bias, groups, rb):
    t = proj.shape[0]
    gw = groups * LANES
    cpb = rb // LANES

    def body(u_ref, v_ref, gain_ref, w_ref, b_ref, s_ref):
        tril = _tril()
        for g in range(groups):
            wt = jnp.where(tril, w_ref[g], 0.0).astype(BF16)
            cols = slice(g * LANES, (g + 1) * LANES)
            for ci in range(cpb):
                rows = slice(ci * LANES, (ci + 1) * LANES)
                ug = _gelu(u_ref[rows, cols])
                vh, _ = _group_norm(_gelu(v_ref[rows, cols]))
                vn = vh * gain_ref[:, cols]
                y = _dot(wt, vn.astype(BF16)) + b_ref[g]
                s_ref[rows, cols] = ug * y

    return pl.pallas_call(
        body, name="sgu_fwd", grid=(t // rb,),
        in_specs=[pl.BlockSpec((rb, gw), lambda i: (i, 0)), pl.BlockSpec((rb, gw), lambda i: (i, 1)),
                  pl.BlockSpec((1, gw), lambda i: (0, 0)),
                  pl.BlockSpec((groups, LANES, LANES), lambda i: (0, 0, 0)),
                  pl.BlockSpec((groups, LANES, LANES), lambda i: (0, 0, 0))],
        out_specs=pl.BlockSpec((rb, gw), lambda i: (i, 0)),
        out_shape=jax.ShapeDtypeStruct((t, gw), F32),
        compiler_params=_params(("parallel",)),
    )(proj, proj, gain, w, bias)


def _sgu_bwd(proj, ds, gain, w, bias, groups, rb):
    t, width = proj.shape
    gw = groups * LANES
    cpb = rb // LANES
    n_steps = t // rb

    def body(u_ref, v_ref, ds_ref, gain_ref, w_ref, b_ref, dp_ref, dw_ref, db_ref, dg_ref, dy_acc):
        du_ref, dv_ref = dp_ref.at[:, :gw], dp_ref.at[:, gw:]
        step = pl.program_id(0)

        @pl.when(step == 0)
        def _():
            dw_ref[...] = jnp.zeros_like(dw_ref)
            dy_acc[...] = jnp.zeros_like(dy_acc)
            dg_ref[...] = jnp.zeros_like(dg_ref)

        tril = _tril()
        for g in range(groups):
            wt = jnp.where(tril, w_ref[g], 0.0).astype(BF16)
            cols = slice(g * LANES, (g + 1) * LANES)
            gain_g = gain_ref[:, cols]
            for ci in range(cpb):
                rows = slice(ci * LANES, (ci + 1) * LANES)
                u_raw, v_raw, ds_v = u_ref[rows, cols], v_ref[rows, cols], ds_ref[rows, cols]
                ug = _gelu(u_raw)
                vh, rs = _group_norm(_gelu(v_raw))
                vn = (vh * gain_g).astype(BF16)
                y = _dot(wt, vn) + b_ref[g]
                dy = ds_v * ug
                dyb = dy.astype(BF16)
                du_ref[rows, cols] = (ds_v * y * _gelu_grad(u_raw)).astype(du_ref.dtype)
                dy_acc[g] += dy
                dw_ref[g] += _dot_nt(dyb, vn)
                dvn = _dot_tn(wt, dyb)
                dg_ref[:, cols] += jnp.sum(dvn * vh, axis=0, keepdims=True)
                dvh = dvn * gain_g
                dvg = rs * (dvh - jnp.mean(dvh, axis=-1, keepdims=True)
                            - vh * jnp.mean(dvh * vh, axis=-1, keepdims=True))
                dv_ref[rows, cols] = (dvg * _gelu_grad(v_raw)).astype(dv_ref.dtype)

        @pl.when(step == n_steps - 1)
        def _():
            ones = jnp.ones((8, LANES), F32)
            for g in range(groups):
                dw_ref[g] = jnp.where(tril, dw_ref[g], 0.0)
                db_ref[g] = lax.dot_general(ones, dy_acc[g], (((1,), (1,)), ((), ())),
                                            precision=lax.Precision.HIGHEST, preferred_element_type=F32)

    blk = lambda cb: pl.BlockSpec((rb, gw), lambda i: (i, cb))
    whole3 = pl.BlockSpec((groups, LANES, LANES), lambda i: (0, 0, 0))
    return pl.pallas_call(
        body, name="sgu_bwd", grid=(n_steps,),
        in_specs=[blk(0), blk(1), blk(0), pl.BlockSpec((1, gw), lambda i: (0, 0)), whole3, whole3],
        out_specs=[pl.BlockSpec((rb, 2 * gw), lambda i: (i, 0)), whole3,
                   pl.BlockSpec((groups, 8, LANES), lambda i: (0, 0, 0)), pl.BlockSpec((1, gw), lambda i: (0, 0))],
        out_shape=[jax.ShapeDtypeStruct((t, width), BF16),
                   jax.ShapeDtypeStruct((groups, LANES, LANES), F32), jax.ShapeDtypeStruct((groups, 8, LANES), F32),
                   jax.ShapeDtypeStruct((1, gw), F32)],
        scratch_shapes=[pltpu.VMEM((groups, LANES, LANES), F32)],
        compiler_params=_params(("arbitrary",)),
    )(proj, proj, ds, gain, w, bias)


def _shift_down(z, s):
    rows = lax.broadcasted_iota(jnp.int32, z.shape, 0)
    return jnp.where(rows >= s, pltpu.roll(z, s, axis=0), 0.0)


def _shift_up(z, s):
    n = z.shape[0]
    rows = lax.broadcasted_iota(jnp.int32, z.shape, 0)
    return jnp.where(rows < n - s, pltpu.roll(z, n - s, axis=0), 0.0)


def _conv_fwd(proj3, cw, tc):
    _, t, cd = proj3.shape

    def body(p_ref, w_ref, o_ref):
        z = p_ref[1] * p_ref[2]
        w = w_ref[...]
        zc = w[2:3] * z + w[1:2] * _shift_down(z, 1) + w[0:1] * _shift_down(z, 2)
        o_ref[...] = (p_ref[0] * zc).astype(o_ref.dtype)

    return pl.pallas_call(
        body, name="conv_fwd", grid=(cd // tc,),
        in_specs=[pl.BlockSpec((3, t, tc), lambda j: (0, 0, j)), pl.BlockSpec((8, tc), lambda j: (0, j))],
        out_specs=pl.BlockSpec((t, tc), lambda j: (0, j)),
        out_shape=jax.ShapeDtypeStruct((t, cd), BF16),
        compiler_params=_params(("parallel",)),
    )(proj3, cw)


def _conv_bwd(proj3, cw, dbz, tc):
    _, t, cd = proj3.shape

    def body(p_ref, w_ref, d_ref, o_ref, dw_ref):
        b, c, xin = p_ref[0], p_ref[1], p_ref[2]
        w = w_ref[...]
        z = c * xin
        z1, z2 = _shift_down(z, 1), _shift_down(z, 2)
        zc = w[2:3] * z + w[1:2] * z1 + w[0:1] * z2
        d = d_ref[...]
        dzc = d * b
        dz = w[2:3] * dzc + w[1:2] * _shift_up(dzc, 1) + w[0:1] * _shift_up(dzc, 2)
        o_ref[0] = (d * zc).astype(o_ref.dtype)
        o_ref[1] = (dz * xin).astype(o_ref.dtype)
        o_ref[2] = (dz * c).astype(o_ref.dtype)
        row = lax.broadcasted_iota(jnp.int32, (8, tc), 0)
        dw0 = jnp.sum(dzc * z2, axis=0, keepdims=True)
        dw1 = jnp.sum(dzc * z1, axis=0, keepdims=True)
        dw2 = jnp.sum(dzc * z, axis=0, keepdims=True)
        dw_ref[...] = jnp.where(row == 0, dw0, 0.0) + jnp.where(row == 1, dw1, 0.0) + jnp.where(row == 2, dw2, 0.0)

    return pl.pallas_call(
        body, name="conv_bwd", grid=(cd // tc,),
        in_specs=[pl.BlockSpec((3, t, tc), lambda j: (0, 0, j)), pl.BlockSpec((8, tc), lambda j: (0, j)),
                  pl.BlockSpec((t, tc), lambda j: (0, j))],
        out_specs=[pl.BlockSpec((3, t, tc), lambda j: (0, 0, j)), pl.BlockSpec((8, tc), lambda j: (0, j))],
        out_shape=[jax.ShapeDtypeStruct((3, t, cd), BF16), jax.ShapeDtypeStruct((8, cd), F32)],
        compiler_params=_params(("parallel",)),
    )(proj3, cw, dbz)


def _place():
    x, y, c = lax.axis_index("x"), lax.axis_index("y"), lax.axis_index("c")
    chips = [(1 - x, y), (x, 1 - y), (1 - x, 1 - y)]
    return x, y, c, chips


def _any_specs(n):
    return [pl.BlockSpec(memory_space=pl.ANY) for _ in range(n)]


HBM_SPEC = pl.BlockSpec(memory_space=pltpu.HBM)
SEM_SPEC = pl.BlockSpec(memory_space=pltpu.SEMAPHORE)
ORDERED_EFFECT = pltpu.SideEffectType.DATAFLOW_SIDE_EFFECTING


def _in_hbm(a):
    return pltpu.with_memory_space_constraint(a, pltpu.HBM)


def _token():
    return jax.ShapeDtypeStruct((8, LANES), F32), pl.BlockSpec(memory_space=pltpu.VMEM)


def _gather_start(name, groups):
    sizes = [len(g) for g in groups]
    flat = [b for g in groups for b in g]
    n, ng = len(flat), len(groups)

    def body(*refs):
        ins, sems, token = refs[:n], refs[n:n + 2 * ng], refs[-1]
        x, y, c, chips = _place()
        me = 2 * x + y
        i = 0
        for gi, size in enumerate(sizes):
            for j in range(size):
                blk = ins[i].at[me, c]
                for k, chip in enumerate(chips):
                    pltpu.make_async_remote_copy(src_ref=blk, dst_ref=blk, send_sem=sems[2 * gi].at[3 * j + k],
                                                 recv_sem=sems[2 * gi + 1].at[3 * j + k],
                                                 device_id=(*chip, c), device_id_type=MESH).start()
                i += 1
        token[...] = jnp.zeros_like(token)

    tok_shape, tok_spec = _token()
    res = pl.pallas_call(
        body, name=name,
        in_specs=[HBM_SPEC] * n,
        out_specs=[SEM_SPEC] * (2 * ng) + [HBM_SPEC] * n + [tok_spec],
        out_shape=[pltpu.SemaphoreType.DMA((3 * size,)) for size in sizes for _ in (0, 1)]
        + [pltpu.HBM(b.shape, b.dtype) for b in flat] + [tok_shape],
        input_output_aliases={i: 2 * ng + i for i in range(n)},
        compiler_params=pltpu.CompilerParams(has_side_effects=ORDERED_EFFECT),
    )(*[_in_hbm(b) for b in flat])
    out, i = [], 2 * ng
    for gi, size in enumerate(sizes):
        out.append((res[2 * gi], res[2 * gi + 1], list(res[i:i + size])))
        i += size
    return out, res[-1]


def _gather_wait(tag, send, recv, bufs, after):
    n = len(bufs)
    after = tuple(after) if isinstance(after, (tuple, list)) else (after,)

    def body(*refs):
        ins, send_ref, recv_ref = refs[:n], refs[n], refs[n + 1]
        x, y, c, chips = _place()
        me = 2 * x + y
        for j in range(n):
            for k, (px, py) in enumerate(chips):
                cp = pltpu.make_async_remote_copy(src_ref=ins[j].at[me, c], dst_ref=ins[j].at[2 * px + py, c],
                                                  send_sem=send_ref.at[3 * j + k], recv_sem=recv_ref.at[3 * j + k],
                                                  device_id=(px, py, c), device_id_type=MESH)
                cp.wait_send()
                cp.wait_recv()

    return pl.pallas_call(
        body, name="gather_wait_" + tag,
        in_specs=[HBM_SPEC] * n + [SEM_SPEC, SEM_SPEC] + _any_specs(len(after)),
        out_specs=[HBM_SPEC] * n,
        out_shape=[pltpu.HBM(b.shape, b.dtype) for b in bufs],
        input_output_aliases={i: i for i in range(n)},
        compiler_params=pltpu.CompilerParams(has_side_effects=ORDERED_EFFECT),
    )(*bufs, send, recv, *after)


def _gather_forward(tag, bufs):
    n = len(bufs)

    def body(*refs):
        ins, outs = refs[:n], refs[n:2 * n]
        send, recv = refs[2 * n:]
        x, y, c, chips = _place()
        sib = (x, y, 1 - c)

        def cp(i, k, slot, half):
            return pltpu.make_async_remote_copy(src_ref=ins[i].at[slot, half], dst_ref=outs[i].at[slot, half],
                                                send_sem=send.at[3 * i + k], recv_sem=recv.at[3 * i + k],
                                                device_id=sib, device_id_type=MESH)

        cps = [cp(i, k, 2 * px + py, c) for i in range(n) for k, (px, py) in enumerate(chips)]
        for d in cps:
            d.start()
        for i in range(n):
            for k, (px, py) in enumerate(chips):
                cp(i, k, 2 * px + py, 1 - c).wait_recv()
        for d in cps:
            d.wait_send()

    return pl.pallas_call(
        body, name="gather_forward_" + tag,
        in_specs=_any_specs(n), out_specs=_any_specs(n),
        out_shape=[jax.ShapeDtypeStruct(b.shape, b.dtype) for b in bufs],
        scratch_shapes=[pltpu.SemaphoreType.DMA((3 * n,))] * 2,
        input_output_aliases={i: i for i in range(n)},
        compiler_params=pltpu.CompilerParams(has_side_effects=True),
    )(*bufs)


def _pair_route(srcs, zones):
    x, y, c, _ = _place()
    return [(srcs[i].at[j, 1 - c], zones[i].at[j], (x, y, 1 - c)) for i in range(len(srcs)) for j in range(N_CHIPS)]


def _slab_route(srcs, zones):
    x, y, c, _ = _place()
    return [(srcs[i].at[j], zones[i].at[j], (x, y, 1 - c)) for i in range(len(srcs)) for j in range(N_CHIPS)]


def _chip_route(srcs, zones):
    x, y, c, chips = _place()
    return [(srcs[i].at[2 * px + py], zones[i].at[k], (px, py, c)) for i in range(len(srcs)) for k, (px, py) in enumerate(chips)]


def _all_route(srcs, zones):
    x, y, c, _ = _place()
    flips = [(fx, fy, fc) for fx in (0, 1) for fy in (0, 1) for fc in (0, 1)][1:]
    return [(srcs[0], zones[0].at[4 * x + 2 * y + c], (x + fx - 2 * x * fx, y + fy - 2 * y * fy, c + fc - 2 * c * fc))
            for fx, fy, fc in flips]


def _share_route(srcs, zones):
    x, y, c, _ = _place()
    return [(s.at[c], s.at[c], (x, y, 1 - c)) for s in srcs]


def _exchange_start(name, route, n_copies, srcs, zones):
    n, nz = len(srcs), len(zones)
    lands = [lax.empty(z, a.dtype) if isinstance(z, tuple) else z for z, a in zip(zones, srcs)]

    def body(*refs):
        ins, zone_refs, send, recv, token = refs[:n], refs[n:n + nz], refs[n + nz], refs[n + nz + 1], refs[-1]
        for k, (src, dst, dev) in enumerate(route(ins, zone_refs)):
            pltpu.make_async_remote_copy(src_ref=src, dst_ref=dst, send_sem=send.at[k], recv_sem=recv.at[k],
                                         device_id=dev, device_id_type=MESH).start()
        token[...] = jnp.zeros_like(token)

    tok_shape, tok_spec = _token()
    res = pl.pallas_call(
        body, name=name,
        in_specs=[HBM_SPEC] * (n + nz),
        out_specs=[SEM_SPEC, SEM_SPEC] + [HBM_SPEC] * (n + nz) + [tok_spec],
        out_shape=[pltpu.SemaphoreType.DMA((n_copies,))] * 2 + [pltpu.HBM(a.shape, a.dtype) for a in srcs + lands]
        + [tok_shape],
        input_output_aliases={i: 2 + i for i in range(n + nz)},
        compiler_params=pltpu.CompilerParams(has_side_effects=ORDERED_EFFECT),
    )(*[_in_hbm(a) for a in srcs + lands])
    return (res[0], res[1], list(res[2:2 + n]), list(res[2 + n:2 + n + nz])), res[-1]


def _exchange_wait(name, route, started, after):
    send, recv, srcs, lands = started
    n, nz = len(srcs), len(lands)
    after = tuple(after) if isinstance(after, (tuple, list)) else (after,)

    def body(*refs):
        ins, zone_refs, send_ref, recv_ref = refs[:n], refs[n:n + nz], refs[n + nz], refs[n + nz + 1]
        for k, (src, dst, dev) in enumerate(route(ins, zone_refs)):
            cp = pltpu.make_async_remote_copy(src_ref=src, dst_ref=dst, send_sem=send_ref.at[k], recv_sem=recv_ref.at[k],
                                              device_id=dev, device_id_type=MESH)
            cp.wait_send()
            cp.wait_recv()

    res = pl.pallas_call(
        body, name=name,
        in_specs=[HBM_SPEC] * (n + nz) + [SEM_SPEC, SEM_SPEC] + _any_specs(len(after)),
        out_specs=[HBM_SPEC] * (n + nz),
        out_shape=[pltpu.HBM(a.shape, a.dtype) for a in srcs + lands],
        input_output_aliases={i: i for i in range(n + nz)},
        compiler_params=pltpu.CompilerParams(has_side_effects=ORDERED_EFFECT),
    )(*srcs, *lands, send, recv, *after)
    return list(res[:n]), list(res[n:])


def _spread(v):
    rows, cols = v.shape
    tr = _row_tile(rows, cols, budget=256 * 1024)

    def body(v_ref, o_ref):
        o_ref[...] = jnp.broadcast_to(v_ref[...][None], o_ref.shape)

    return pl.pallas_call(body, name="spread_small_grads", grid=(rows // tr,),
                          in_specs=[pl.BlockSpec((tr, cols), lambda r: (r, 0))],
                          out_specs=pl.BlockSpec((8, tr, cols), lambda r: (0, r, 0)),
                          out_shape=jax.ShapeDtypeStruct((8, rows, cols), v.dtype),
                          compiler_params=_params(("parallel",)))(v)


def _row_tile(rows, cols, itemsize=4, budget=2 * 1024 * 1024, step=8):
    best = None
    for t in range(step, rows + 1, step):
        if rows % t == 0 and t * cols * itemsize <= budget:
            best = t
    return best if best is not None else rows


def _my_chip():
    return 2 * lax.axis_index("x") + lax.axis_index("y")


def _pair_sum(g5, gsib):
    _, _, rh, cols = g5.shape
    tr = _row_tile(rh, cols, step=16)

    def body(a_ref, b_ref, o_ref):
        o_ref[...] = (a_ref[...].astype(F32) + b_ref[...].astype(F32)).astype(o_ref.dtype)

    return pl.pallas_call(body, name="grad_pair_sum", grid=(N_CHIPS, rh // tr),
                          in_specs=[pl.BlockSpec((None, None, tr, cols), lambda j, r: (j, lax.axis_index("c"), r, 0)),
                                    pl.BlockSpec((None, tr, cols), lambda j, r: (j, r, 0))],
                          out_specs=pl.BlockSpec((None, tr, cols), lambda j, r: (j, r, 0)),
                          out_shape=jax.ShapeDtypeStruct((N_CHIPS, rh, cols), BF16),
                          compiler_params=_params(("parallel", "parallel")))(g5, gsib)


def _chip_sum(part, recv):
    _, rh, cols = part.shape
    tr = _row_tile(rh, cols, step=16)

    def body(a_ref, b_ref, o_ref):
        acc = a_ref[...].astype(F32)
        for k in range(3):
            acc = acc + b_ref[k].astype(F32)
        o_ref[...] = acc

    return pl.pallas_call(body, name="grad_chip_sum", grid=(rh // tr,),
                          in_specs=[pl.BlockSpec((None, tr, cols), lambda r: (_my_chip(), r, 0)),
                                    pl.BlockSpec((3, tr, cols), lambda r: (0, r, 0))],
                          out_specs=pl.BlockSpec((None, tr, cols), lambda r: (lax.axis_index("c"), r, 0)),
                          out_shape=jax.ShapeDtypeStruct((2, rh, cols), F32),
                          compiler_params=_params(("parallel",)))(part, recv)


def _sum_devices(g):
    _, rows, cols = g.shape
    tr = _row_tile(rows, cols, budget=256 * 1024)

    def body(g_ref, o_ref):
        acc = g_ref[0]
        for d in range(1, 8):
            acc = acc + g_ref[d]
        o_ref[...] = acc

    return pl.pallas_call(body, name="sum_small_grads", grid=(rows // tr,),
                          in_specs=[pl.BlockSpec((8, tr, cols), lambda r: (0, r, 0))],
                          out_specs=pl.BlockSpec((tr, cols), lambda r: (r, 0)),
                          out_shape=jax.ShapeDtypeStruct((rows, cols), F32),
                          compiler_params=_params(("parallel",)))(g)


def _place_shard(w, layer, dtype, deps=()):
    _, rows, cols = w.shape
    tr = _row_tile(rows, cols)

    def body(i_ref, *rest):
        o_ref = rest[-1]
        o_ref[...] = i_ref[...].astype(o_ref.dtype)

    out = pl.pallas_call(body, name="place_shard", grid=(rows // tr,),
                         in_specs=[pl.BlockSpec((None, tr, cols), lambda r: (layer, r, 0))] + _any_specs(len(deps)),
                         out_specs=pl.BlockSpec((None, tr, cols), lambda r: (_my_chip(), r, 0)),
                         out_shape=jax.ShapeDtypeStruct((N_CHIPS, rows, cols), dtype),
                         compiler_params=_params(("parallel",)))(w, *deps)
    return out.reshape(N_CHIPS, 2, rows // 2, cols)


def _adamw(w, gs, m, v):
    n_layers, rows, cols = w.shape
    tr = _row_tile(rows, cols)

    def body(w_ref, m_ref, v_ref, *rest):
        g_refs = rest[:n_layers]
        go_ref, d_ref, mo_ref, vo_ref = rest[n_layers:]
        gv = g_refs[0][...]
        for layer in range(1, n_layers):
            gv = jnp.where(pl.program_id(0) == layer, g_refs[layer][...], gv)
        d_ref[...], mo_ref[...], vo_ref[...] = _adamw_math(w_ref[...], gv, m_ref[...], v_ref[...])
        go_ref[...] = gv

    spec = pl.BlockSpec((None, tr, cols), lambda layer, r: (layer, r, 0))
    g_specs = [pl.BlockSpec((tr, cols), lambda layer, r, own=own: (jnp.where(layer == own, r, 0), 0))
               for own in range(n_layers)]
    return pl.pallas_call(body, name="adamw", grid=(n_layers, rows // tr), in_specs=[spec] * 3 + g_specs,
                          out_specs=[spec] * 4, out_shape=[jax.ShapeDtypeStruct((n_layers, rows, cols), F32)] * 4,
                          compiler_params=_params(("parallel", "parallel")))(w, m, v, *gs)


def _pad_rope(w):
    z = jnp.zeros(w.shape[:-1] + (ROPE_HALF,), w.dtype)
    return jnp.concatenate([w[..., :ROPE_HALF], z, w[..., ROPE_HALF:], z], axis=-1)


def _unpad_rope(g):
    return jnp.concatenate([g[..., :ROPE_HALF], g[..., ROPE:ROPE + ROPE_HALF]], axis=-1)


def _unstack_cols(s):
    n, r, cs = s.shape
    return jnp.transpose(s, (1, 0, 2)).reshape(r, n * cs)


def _stack_cols(f):
    r, cfull = f.shape
    return jnp.transpose(f.reshape(r, N_CHIPS, cfull // N_CHIPS), (1, 0, 2))


def _small_shard(norm, conv):
    return jnp.concatenate([jnp.pad(norm, ((0, 15), (0, 0))), jnp.pad(conv, ((0, 13), (0, 0)))], axis=0)


def _flat_rows(a):
    return a.reshape(-1, LANES)


def _pack_small(arrs):
    return jnp.concatenate([_flat_rows(a.astype(F32)) for a in arrs], axis=0)


def _unpack_small(flat, like):
    out, r = [], 0
    for a in like:
        n = a.size // LANES
        out.append(flat[r:r + n].reshape(a.shape))
        r += n
    return out


def kernel(x, positions, e_norm_mix, e_w_in, e_q_norm, e_w_uq, e_kv_norm, e_w_ukv, e_v_norm, e_sgu_w, e_sgu_b, e_mla_out_norm, e_sgu_out_norm, e_w_out, o_norm_mix, o_w_in, o_conv_w, o_w_out, mlp_norm, mlp_w1, mlp_w2, final_norm, loss_target, m_e_norm_mix, m_e_w_in, m_e_q_norm, m_e_w_uq, m_e_kv_norm, m_e_w_ukv, m_e_v_norm, m_e_sgu_w, m_e_sgu_b, m_e_mla_out_norm, m_e_sgu_out_norm, m_e_w_out, m_o_norm_mix, m_o_w_in, m_o_conv_w, m_o_w_out, m_mlp_norm, m_mlp_w1, m_mlp_w2, m_final_norm, v_e_norm_mix, v_e_w_in, v_e_q_norm, v_e_w_uq, v_e_kv_norm, v_e_w_ukv, v_e_v_norm, v_e_sgu_w, v_e_sgu_b, v_e_mla_out_norm, v_e_sgu_out_norm, v_e_w_out, v_o_norm_mix, v_o_w_in, v_o_conv_w, v_o_w_out, v_mlp_norm, v_mlp_w1, v_mlp_w2, v_final_norm):
    t, d = x.shape[1], x.shape[2]
    ql, kvl = e_q_norm.shape[1], e_kv_norm.shape[1]
    groups = e_v_norm.shape[1]
    gw = groups * LANES
    heads = N_CHIPS * e_w_uq.shape[2] // (LANES + ROPE)
    hw = heads * LANES
    mix = hw + gw
    ei = N_CHIPS * e_w_in.shape[2]
    cd = N_CHIPS * o_conv_w.shape[2]
    ff = N_CHIPS * mlp_w1.shape[2]
    ffs = ff // N_CHIPS
    pi = 2 * gw + ql + kvl + LANES
    assert e_norm_mix.shape[0] == 1 and o_norm_mix.shape[0] == 1 and mlp_norm.shape[0] == 2
    assert ei == ql + kvl + ROPE + 2 * gw and cd == d and e_sgu_w.shape[2] == LANES
    assert (2 * gw) % ql == 0 and (2 * gw + ql) % kvl == 0 and t % LANES == 0
    scale = (LANES + ROPE) ** -0.5

    tr = min(256, t)
    tm = _pick(t, 1024, 8)
    kt, kd = _pick(t, 2048, 8), _pick(d, 2048)
    xs = x.reshape(t, d)
    tgt = loss_target.reshape(t, d)

    small_shard = _small_shard(o_norm_mix, o_conv_w[0])
    first, tok = _gather_start("gather_start_e", [
        [_place_shard(e_w_in, 0, BF16)],
        [_place_shard(e_w_uq, 0, BF16), _place_shard(e_w_ukv, 0, BF16), _place_shard(e_w_out, 0, BF16),
         _place_shard(small_shard[None], 0, F32)]])
    rest, tok = _gather_start("gather_start_rest", [
        [_place_shard(mlp_w1, 0, BF16, (tok,))], [_place_shard(mlp_w2, 0, BF16, (tok,))],
        [_place_shard(o_w_in, 0, BF16, (tok,)), _place_shard(o_w_out, 0, BF16, (tok,))],
        [_place_shard(mlp_w1, 1, BF16, (tok,))], [_place_shard(mlp_w2, 1, BF16, (tok,))]])
    started = first + rest

    def gathered(gi, tag, after):
        send, recv, bufs = started[gi]
        bufs = _gather_forward(tag, _gather_wait(tag, send, recv, bufs, after))
        return [b.reshape(N_CHIPS, 2 * b.shape[2], b.shape[3]) for b in bufs]

    g_e = e_norm_mix
    h0 = _norm_fwd("e_norm", xs, g_e, tr)
    inv_freq = ROPE_BASE ** (-jnp.arange(0, ROPE, 2, dtype=F32) / ROPE)
    zeros32 = jnp.zeros((ROPE_HALF,), F32)
    ones32 = jnp.ones((ROPE_HALF,), F32)
    invf = jnp.concatenate([inv_freq, zeros32, inv_freq, zeros32]).reshape(1, LANES)
    cmask = jnp.concatenate([ones32, zeros32, ones32, zeros32]).reshape(1, LANES)
    smask = jnp.concatenate([-ones32, zeros32, ones32, zeros32]).reshape(1, LANES)
    ctab, stab = _rope_tables(positions.reshape(t, 1).astype(F32), invf, cmask, smask, tr)

    w_in_g, = gathered(0, "e_in", (h0, ctab, tok))
    full = _unstack_cols(w_in_g)
    c2, c3 = ql + kvl, ql + kvl + ROPE
    w_in_all = jnp.concatenate([full[:, c3:], full[:, :c2], _pad_rope(full[:, c2:c3])], axis=1)
    proj, = _matmul("e_proj", Mat(h0, t, d), Mat(w_in_all, d, pi), "nn", [_out(t, pi, F32)], tm, _pick(pi, 1024), kd)

    w_uq_g, w_ukv_g, w_eout_g, small_g = gathered(1, "e", proj)
    full = _unstack_cols(w_uq_g).reshape(ql, heads, LANES + ROPE)
    w_q_all = jnp.concatenate([full[:, :, :LANES].reshape(ql, hw), _pad_rope(full[:, :, LANES:]).reshape(ql, hw)], axis=1)
    full = _unstack_cols(w_ukv_g).reshape(kvl, heads, 2 * LANES)
    w_kv_all = jnp.concatenate([full[:, :, :LANES].reshape(kvl, hw), full[:, :, LANES:].reshape(kvl, hw)], axis=1)
    w_eout = w_eout_g.reshape(mix, d)
    g_o = small_g[:, 0].reshape(1, d)
    conv_w = jnp.pad(jnp.transpose(small_g[:, 16:19], (1, 0, 2)).reshape(3, cd), ((0, 5), (0, 0)))

    g_q, g_kv = e_q_norm, e_kv_norm
    g_vn = e_v_norm.reshape(1, gw)
    sgu_w = e_sgu_w[0]
    sgu_b = jnp.broadcast_to(e_sgu_b[0][:, :, None], (groups, LANES, LANES))
    g_mla, g_sgu = e_mla_out_norm, e_sgu_out_norm
    g_m0, g_m1 = mlp_norm[0:1], mlp_norm[1:2]
    g_f = final_norm.reshape(1, d)

    def mlp_fwd(tag, xin, g, gi):
        hm = _norm_fwd("mlp_norm_" + tag, xin, g, tr)
        tn = _pick(ffs, 1024)
        w1 = Mat(gathered(gi, "w1_" + tag, hm)[0], d, ff, "colstack")
        a, act = _matmul("mlp_up_" + tag, Mat(hm, t, d), w1, "nn",
                         [_out(t, ff, BF16), _out(t, ff, BF16)], tm, tn, kd,
                         epilogue=lambda z: (jnp.maximum(z, 0.0), jnp.square(jnp.maximum(z, 0.0))))
        w2 = Mat(gathered(gi + 1, "w2_" + tag, act)[0].reshape(ff, d), ff, d)
        xo, = _matmul("mlp_down_" + tag, Mat(act, t, ff), w2, "nn",
                      [_out(t, d, F32)], tm, _pick(d, 1024), _pick(ffs, 2048),
                      epilogue=lambda z, r: (z + r,), extras=[Mat(xin, t, d)])
        return xo, hm, a, act, w1, w2

    def chip_start(tag, part):
        return _exchange_start("scatter_start_" + tag, _chip_route, 3 * len(part), part, [(3,) + p.shape[1:] for p in part])

    def pair_start(tag, stacked):
        g5 = [g.reshape(N_CHIPS, 2, g.shape[1] // 2, g.shape[2]) for g in stacked]
        return _exchange_start("pair_start_" + tag, _pair_route, N_CHIPS * len(g5), g5,
                               [(N_CHIPS,) + g.shape[2:] for g in g5])

    def pair_finish(tag, started, after):
        g5, from_sib = _exchange_wait("pair_wait_" + tag, _pair_route, started, after)
        return chip_start(tag, [_pair_sum(a, b) for a, b in zip(g5, from_sib)])

    def summed(tag, sc, after):
        part, lands = _exchange_wait("scatter_wait_" + tag, _chip_route, sc, after)
        half = [_chip_sum(p, r) for p, r in zip(part, lands)]
        return _exchange_start("share_start_" + tag, _share_route, len(half), half, [])

    def shared(tag, started, after):
        bufs, _ = _exchange_wait("share_wait_" + tag, _share_route, started, after)
        return [r.reshape(2 * r.shape[1], r.shape[2]) for r in bufs]

    def mlp_bwd(tag, dx, dxb, xin, g, w1, w2, hm, a, act, deps):
        tn = _pick(ffs, 1024)
        hr, hd = ffs // 2, d // 2
        dz, = _matmul("mlp_dact_" + tag, Mat(dxb, t, d), w2, "nt",
                      [_out(t, ff, BF16)], tm, tn, kd,
                      epilogue=lambda z, av: (z * (2.0 * av.astype(F32)),), extras=[Mat(a, t, ff)], deps=deps)

        def half(own):
            c = lax.axis_index("c")
            return c if own else 1 - c

        def act_half(own):
            return Mat(act, t, ff // 2, cmap=lambda cb, bc: (cb // (hr // bc)) * (ffs // bc) + half(own) * (hr // bc)
                       + cb % (hr // bc))

        def hm_half(own):
            return Mat(hm, t, hd, cmap=lambda cb, bc: cb + half(own) * (hd // bc))

        w1_out = lambda: _out(hd, ff, BF16, "colstack", (), (N_CHIPS, hd, ffs))
        theirs2, = _matmul("mlp_dw2_theirs_" + tag, act_half(False), Mat(dxb, t, d), "tn",
                           [_out(ff // 2, d, BF16)], _pick(hr, 1024), _pick(d, 2048), kt)
        theirs1, = _matmul("mlp_dw1_theirs_" + tag, hm_half(False), Mat(dz, t, ff), "tn",
                           [w1_out()], _pick(hd, 2048), tn, kt)
        sent = [theirs1, theirs2.reshape(N_CHIPS, hr, d)]
        started, tok = _exchange_start("pair_start_m" + tag, _slab_route, N_CHIPS * 2, sent, [s.shape for s in sent])
        dhm, = _matmul("mlp_dh_" + tag, Mat(dz, t, ff), w1, "nt",
                       [_out(t, d, F32)], tm, _pick(d, 1024), _pick(ffs, 2048), deps=(tok,))
        dxo, dxob, dg = _norm_bwd("mlp_norm_bwd_" + tag, dhm, xin, g, dx, tr)
        _, (sib1, sib2) = _exchange_wait("pair_wait_m" + tag, _slab_route, started, dxo)
        add = lambda z, s: (z + s.astype(F32),)
        part2, = _matmul("mlp_dw2_mine_" + tag, act_half(True), Mat(dxb, t, d), "tn",
                         [_out(ff // 2, d, BF16)], _pick(hr, 1024), _pick(d, 2048), kt,
                         epilogue=add, extras=[Mat(sib2.reshape(ff // 2, d), ff // 2, d)])
        part1, = _matmul("mlp_dw1_mine_" + tag, hm_half(True), Mat(dz, t, ff), "tn",
                         [w1_out()], _pick(hd, 2048), tn, kt, epilogue=add, extras=[Mat(sib1, hd, ff, "colstack")])
        sc, tok = chip_start("m" + tag, [part1, part2.reshape(N_CHIPS, hr, d)])
        return dxo, dxob, dg, sc, tok

    cq_cb, ckv_cb, kr_cb = 2 * gw // ql, (2 * gw + ql) // kvl, (2 * gw + ql + kvl) // LANES
    qn, kvn = _rowwise("qkv_norm", lambda a, b, ga, gb: (_rms(a, ga), _rms(b, gb)), t // tr,
                       [_rt(proj, tr, ql, cq_cb), _rt(proj, tr, kvl, ckv_cb), _whole(g_q), _whole(g_kv)],
                       [_rt_out(t, ql, BF16, tr), _rt_out(t, kvl, BF16, tr)])
    qfull, = _matmul("q_up", Mat(qn, t, ql), Mat(w_q_all, ql, 2 * hw), "nn", [_out(t, 2 * hw, F32)], tm, _pick(2 * hw, 1024), ql)
    kvall, = _matmul("kv_up", Mat(kvn, t, kvl), Mat(w_kv_all, kvl, 2 * hw), "nn", [_out(t, 2 * hw, BF16)], tm, _pick(2 * hw, 1024), kvl)
    qall, kr = _rope_fwd(qfull, proj, kr_cb, ctab, stab, heads, tr)
    att, lse_row = _attn_fwd(qall, kvall, kr, heads, scale, tr)
    rb = min(2 * LANES, t)
    sgu = _sgu_fwd(proj, g_vn, sgu_w, sgu_b, groups, rb)
    mixed = _rowwise("mix_norm", lambda a, s, ga, gs: jnp.concatenate([_rms(a, ga), _rms(s, gs)], axis=1), t // tr,
                     [_rt(att, tr), _rt(sgu, tr), _whole(g_mla), _whole(g_sgu)], [_rt_out(t, mix, BF16, tr)])[0]
    x1, = _matmul("e_out", Mat(mixed, t, mix), Mat(w_eout, mix, d), "nn", [_out(t, d, F32)], tm, _pick(d, 1024), _pick(mix, 2048),
                  epilogue=lambda z, r: (z + r,), extras=[Mat(xs, t, d)])
    x2, hm0, a0, act0, w1_0, w2_0 = mlp_fwd("0", x1, g_m0, 2)

    w_oin_g, w_oout_g = gathered(4, "o", x2)
    w_oout = w_oout_g.reshape(cd, d)
    h1 = _norm_fwd("o_norm", x2, g_o, tr)
    oin = Mat(_unstack_cols(w_oin_g), d, 3 * cd)
    tn_o = _pick(_gcd(3 * cd // N_CHIPS, cd), 512)
    proj3, = _matmul("o_proj", Mat(h1, t, d), oin, "nn", [_out(t, 3 * cd, F32, "colstack", (), (3, t, cd))],
                     tm, _pick(cd, 1024), kd)
    tc = _pick(cd, 256)
    bz = _conv_fwd(proj3, conv_w, tc)
    x3, = _matmul("o_out", Mat(bz, t, cd), Mat(w_oout, cd, d), "nn", [_out(t, d, F32)], tm, _pick(d, 1024), _pick(cd, 2048),
                  epilogue=lambda z, r: (z + r,), extras=[Mat(x2, t, d)])
    x4, hm1, a1, act1, w1_1, w2_1 = mlp_fwd("1", x3, g_m1, 5)

    def final_fn(xv, gv, tv):
        r = lax.rsqrt(jnp.mean(xv * xv, axis=-1, keepdims=True) + EPS)
        xh = xv * r
        err = xh * gv - tv
        dy = err * (1.0 / d)
        dxh = dy * gv
        dx = r * (dxh - xh * jnp.mean(dxh * xh, axis=-1, keepdims=True))
        sq = jnp.sum(err * err, axis=0, keepdims=True)
        part = sq[:, :LANES]
        for k in range(1, d // LANES):
            part = part + sq[:, k * LANES:(k + 1) * LANES]
        return dx, dx, part, jnp.sum(dy * xh, axis=0, keepdims=True)

    dx4, dx4b, loss_vec, dg_f = _rowwise("loss_final_norm", final_fn, t // tr, [_rt(x4, tr), _whole(g_f), _rt(tgt, tr)],
                                         [_rt_out(t, d, F32, tr), _rt_out(t, d, BF16, tr)],
                                         [jax.ShapeDtypeStruct((1, LANES), F32), jax.ShapeDtypeStruct((1, d), F32)])

    dx3, dx3b, dg_m1, sc_m1, tok = mlp_bwd("1", dx4, dx4b, x3, g_m1, w1_1, w2_1, hm1, a1, act1, ())

    dbz, = _matmul("o_out_dx", Mat(dx3b, t, d), Mat(w_oout, cd, d), "nt", [_out(t, cd, F32)], tm, _pick(cd, 1024), kd,
                   deps=(tok,))
    dw_oout, = _matmul("o_out_dw", Mat(bz, t, cd), Mat(dx3b, t, d), "tn", [_out(cd, d, BF16)], _pick(cd, 1024), _pick(d, 1024), kt)
    dproj3, dconv = _conv_bwd(proj3, conv_w, dbz, tc)
    dp3 = Mat(dproj3, t, 3 * cd, "colstack")
    dw_oin, = _matmul("o_proj_dw", Mat(h1, t, d), dp3, "tn", [_out(d, 3 * cd, BF16, "colstack", (), (N_CHIPS, d, 3 * cd // N_CHIPS))],
                      _pick(d, 2048), tn_o, kt)
    started_o, tok = pair_start("o", [dw_oin, dw_oout.reshape(N_CHIPS, cd // N_CHIPS, d)])
    dh1, = _matmul("o_proj_dx", dp3, oin, "nt", [_out(t, d, F32)], tm, _pick(d, 1024), _pick(cd, 2048), deps=(tok,))
    dx2, dx2b, dg_o = _norm_bwd("o_norm_bwd", dh1, x2, g_o, dx3, tr)
    sc_o, tok = pair_finish("o", started_o, dx2)

    dconv_s = jnp.transpose(dconv[:3].reshape(3, N_CHIPS, cd // N_CHIPS), (1, 0, 2))
    gsmall = jnp.concatenate([jnp.pad(dg_o.reshape(N_CHIPS, 1, d // N_CHIPS), ((0, 0), (0, 15), (0, 0))),
                              jnp.pad(dconv_s, ((0, 0), (0, 13), (0, 0)))], axis=1)
    dx1, dx1b, dg_m0, sc_m0, tok = mlp_bwd("0", dx2, dx2b, x1, g_m0, w1_0, w2_0, hm0, a0, act0, (tok,))

    dmixed, = _matmul("e_out_dx", Mat(dx1b, t, d), Mat(w_eout, mix, d), "nt", [_out(t, mix, F32)], tm, _pick(mix, 1024), kd,
                      deps=(tok,))
    dw_eout, = _matmul("e_out_dw", Mat(mixed, t, mix), Mat(dx1b, t, d), "tn", [_out(mix, d, BF16)], _pick(mix, 1024), _pick(d, 1024), kt)

    def mixb_fn(dm, a, s, ga, gs):
        da, dga = _rms_bwd(dm[:, :hw], a, ga)
        dsg, dgs = _rms_bwd(dm[:, hw:], s, gs)
        prod = da * a
        cols = [jnp.broadcast_to(jnp.sum(prod[:, h * LANES:(h + 1) * LANES], axis=-1, keepdims=True), (tr, LANES))
                for h in range(heads)]
        return da, dsg, jnp.stack([_row_of(c) for c in cols], axis=0), dga, dgs

    da_b, dsgu, delta_row, dg_mla, dg_sgu = _rowwise(
        "mix_norm_bwd", mixb_fn, t // tr, [_rt(dmixed, tr), _rt(att, tr), _rt(sgu, tr), _whole(g_mla), _whole(g_sgu)],
        [_rt_out(t, hw, BF16, tr), _rt_out(t, gw, F32, tr),
         (jax.ShapeDtypeStruct((heads, 8, t), F32), pl.BlockSpec((heads, 8, tr), lambda i: (0, 0, i)))],
        [jax.ShapeDtypeStruct((1, hw), F32), jax.ShapeDtypeStruct((1, gw), F32)])

    dproj, dsgu_w, dsgu_b8, dg_vn = _sgu_bwd(proj, dsgu, g_vn, sgu_w, sgu_b, groups, rb)
    dq1, dq2, dk1, dvv, dkr_h = _attn_bwd(qall, kvall, kr, da_b, lse_row, delta_row, heads, scale, min(2 * tr, t))
    dqfull, dproj = _rope_bwd(dq1, dq2, dkr_h, ctab, stab, heads, tr, dproj, kr_cb)
    dkvall = jnp.concatenate([dk1, dvv], axis=1)
    dw_q, = _matmul("q_up_dw", Mat(qn, t, ql), Mat(dqfull, t, 2 * hw), "tn", [_out(ql, 2 * hw, BF16)], ql, _pick(2 * hw, 1024), kt)
    dqn, = _matmul("q_up_dx", Mat(dqfull, t, 2 * hw), Mat(w_q_all, ql, 2 * hw), "nt", [_out(t, ql, F32)], tm, ql, _pick(2 * hw, 2048))
    dw_kv, = _matmul("kv_up_dw", Mat(kvn, t, kvl), Mat(dkvall, t, 2 * hw), "tn", [_out(kvl, 2 * hw, BF16)], kvl, _pick(2 * hw, 1024), kt)
    dkvn, = _matmul("kv_up_dx", Mat(dkvall, t, 2 * hw), Mat(w_kv_all, kvl, 2 * hw), "nt", [_out(t, kvl, F32)], tm, kvl, _pick(2 * hw, 2048))

    def qkvb_fn(da, db, a, b, ga, gb):
        dxa, dga = _rms_bwd(da, a, ga)
        dxb, dgb = _rms_bwd(db, b, gb)
        return jnp.concatenate([dxa, dxb], axis=1), dga, dgb

    assert (2 * gw) % (ql + kvl) == 0
    into = (jax.ShapeDtypeStruct(dproj.shape, dproj.dtype),
            pl.BlockSpec((tr, ql + kvl), lambda i: (i, 2 * gw // (ql + kvl))))
    dproj, dg_q, dg_kv = _rowwise(
        "qkv_norm_bwd", qkvb_fn, t // tr,
        [_rt(dqn, tr), _rt(dkvn, tr), _rt(proj, tr, ql, cq_cb), _rt(proj, tr, kvl, ckv_cb), _whole(g_q), _whole(g_kv)],
        [into], [jax.ShapeDtypeStruct((1, ql), F32), jax.ShapeDtypeStruct((1, kvl), F32)], deps=(dproj,), fill=(0, 0))
    dw_in, = _matmul("e_proj_dw", Mat(dproj, t, pi), Mat(h0, t, d), "tn", [_out(pi, d, F32)], _pick(pi, 1024), _pick(d, 2048), kt)
    dh0, = _matmul("e_proj_dx", Mat(dproj, t, pi), Mat(w_in_all, d, pi), "nt", [_out(t, d, F32)], tm, _pick(d, 1024), _pick(pi, 4096))
    dx0, _, dg_e = _norm_bwd("e_norm_bwd", dh0, xs, g_e, dx1, tr)

    kr0 = 2 * gw + c2
    gw_in = jnp.concatenate([dw_in[2 * gw:kr0], dw_in[kr0:kr0 + ROPE_HALF], dw_in[kr0 + ROPE:kr0 + ROPE + ROPE_HALF],
                             dw_in[:2 * gw]], axis=0).reshape(N_CHIPS, ei // N_CHIPS, d)
    gq = jnp.concatenate([dw_q[:, :hw].reshape(ql, heads, LANES), _unpad_rope(dw_q[:, hw:].reshape(ql, heads, LANES))], axis=-1)
    gw_uq = _stack_cols(gq.reshape(ql, heads * (LANES + ROPE)))
    gkv = jnp.concatenate([dw_kv[:, :hw].reshape(kvl, heads, LANES), dw_kv[:, hw:].reshape(kvl, heads, LANES)], axis=-1)
    gw_ukv = _stack_cols(gkv.reshape(kvl, heads * 2 * LANES))
    started_e, tok_pair = pair_start("e", [gw_in, gw_uq, gw_ukv, dw_eout.reshape(N_CHIPS, mix // N_CHIPS, d), gsmall])

    small_like = [e_norm_mix, e_q_norm, e_kv_norm, e_v_norm, e_sgu_w, e_sgu_b, e_mla_out_norm, e_sgu_out_norm, mlp_norm, final_norm]
    small_grads = [dg_e, dg_q, dg_kv, dg_vn, dsgu_w, dsgu_b8[:, 0, :], dg_mla, dg_sgu, jnp.concatenate([dg_m0, dg_m1], axis=0), dg_f]
    packed = _pack_small(small_grads)
    n_small = packed.shape[0] + (-packed.shape[0]) % 8
    pad = n_small - packed.shape[0] + 8
    sflat = jnp.concatenate([jnp.pad(packed, ((0, pad - 8), (0, 0))), jnp.pad(loss_vec, ((0, 7), (0, 0)))], axis=0)
    small_started, tok_small = _exchange_start("small_start", _all_route, 7, [sflat], [_spread(sflat)])

    sh_m1, tok = summed("m1", sc_m1, (tok_pair, tok_small))
    sc_e, tok = pair_finish("e", started_e, tok)
    sh_o, tok = summed("o", sc_o, tok)
    sh_m0, tok = summed("m0", sc_m0, tok)
    r_oin, r_oout = shared("o", sh_o, tok)
    late = {"o_w_in": _adamw(o_w_in, [r_oin], m_o_w_in, v_o_w_in),
            "o_w_out": _adamw(o_w_out, [r_oout], m_o_w_out, v_o_w_out)}
    r_w1_1, r_w2_1 = shared("m1", sh_m1, late["o_w_in"][1])
    r_w1_0, r_w2_0 = shared("m0", sh_m0, r_w2_1)
    late["mlp_w1"] = _adamw(mlp_w1, [r_w1_0, r_w1_1], m_mlp_w1, v_mlp_w1)
    sh_e, tok = summed("e", sc_e, late["mlp_w1"][1])
    late["mlp_w2"] = _adamw(mlp_w2, [r_w2_0, r_w2_1], m_mlp_w2, v_mlp_w2)

    _, (all_small,) = _exchange_wait("small_wait", _all_route, small_started, late["mlp_w2"][1])
    g_small = _sum_devices(all_small)
    loss = 0.5 * jnp.sum(g_small[n_small]) / d

    def padded(arrs):
        return jnp.pad(_pack_small(arrs), ((0, pad), (0, 0)))

    s_m = [m_e_norm_mix, m_e_q_norm, m_e_kv_norm, m_e_v_norm, m_e_sgu_w, m_e_sgu_b, m_e_mla_out_norm, m_e_sgu_out_norm, m_mlp_norm, m_final_norm]
    s_v = [v_e_norm_mix, v_e_q_norm, v_e_kv_norm, v_e_v_norm, v_e_sgu_w, v_e_sgu_b, v_e_mla_out_norm, v_e_sgu_out_norm, v_mlp_norm, v_final_norm]
    s_out = [_unpack_small(o[0], small_like)
             for o in _adamw(padded(small_like)[None], [g_small], padded(s_m)[None], padded(s_v)[None])]

    r_in, r_uq, r_ukv, r_eout, r_small = shared("e", sh_e, (tok, late["mlp_w2"][1]))
    sm = [o[0] for o in _adamw(small_shard[None], [r_small], _small_shard(m_o_norm_mix, m_o_conv_w[0])[None],
                               _small_shard(v_o_norm_mix, v_o_conv_w[0])[None])]
    big = dict(late)
    flip = lambda a: jnp.swapaxes(a, 1, 2)
    big.update({
        "e_w_in": [flip(o) for o in _adamw(flip(e_w_in), [r_in], flip(m_e_w_in), flip(v_e_w_in))],
        "e_w_uq": _adamw(e_w_uq, [r_uq], m_e_w_uq, v_e_w_uq),
        "e_w_ukv": _adamw(e_w_ukv, [r_ukv], m_e_w_ukv, v_e_w_ukv),
        "e_w_out": _adamw(e_w_out, [r_eout], m_e_w_out, v_e_w_out),
    })

    names = ["e_norm_mix", "e_w_in", "e_q_norm", "e_w_uq", "e_kv_norm", "e_w_ukv", "e_v_norm", "e_sgu_w", "e_sgu_b",
             "e_mla_out_norm", "e_sgu_out_norm", "e_w_out", "o_norm_mix", "o_w_in", "o_conv_w", "o_w_out",
             "mlp_norm", "mlp_w1", "mlp_w2", "final_norm"]
    shapes = {"e_w_in": e_w_in.shape, "e_w_uq": e_w_uq.shape, "e_w_ukv": e_w_ukv.shape, "e_w_out": e_w_out.shape,
              "o_w_in": o_w_in.shape, "o_w_out": o_w_out.shape, "mlp_w1": mlp_w1.shape, "mlp_w2": mlp_w2.shape}
    small_names = ["e_norm_mix", "e_q_norm", "e_kv_norm", "e_v_norm", "e_sgu_w", "e_sgu_b", "e_mla_out_norm",
                   "e_sgu_out_norm", "mlp_norm", "final_norm"]

    def leaf(kind, name):
        if name in big:
            return big[name][kind].reshape(shapes[name])
        if name == "o_norm_mix":
            return sm[kind][0:1]
        if name == "o_conv_w":
            return sm[kind][16:19].reshape(o_conv_w.shape)
        return s_out[kind][small_names.index(name)]

    outs = [loss, dx0.reshape(x.shape)]
    for kind in range(4):
        outs += [leaf(kind, nm) for nm in names]
    return tuple(outs)


def _gcd(a, b):
    while b:
        a, b = b, a % b
    return a
```

```python
import jax
import jax.numpy as jnp
from jax import lax
from jax.experimental import pallas as pl
from jax.experimental.pallas import tpu as pltpu

F32 = jnp.float32
BF16 = jnp.bfloat16
MESH = pl.DeviceIdType.MESH

LANES = 128
ROPE = 64
ROPE_HALF = ROPE // 2
ROPE_BASE = 10000.0
EPS = 1e-6
N_CHIPS = 4
VMEM_LIMIT = 48 * 1024 * 1024
NEG = -1e30

ADAM_LR = 0.001
ADAM_B1 = 0.9
ADAM_B2 = 0.999
ADAM_EPS = 1e-08
ADAM_WD = 0.01
ADAM_STEP = 10


def _pick(n, target, step=LANES):
    best = None
    for t in range(step, min(n, target) + 1, step):
        if n % t == 0:
            best = t
    return best if best is not None else n


def _params(sem, vmem=VMEM_LIMIT):
    return pltpu.CompilerParams(dimension_semantics=sem, vmem_limit_bytes=vmem)


class Mat:
    def __init__(self, arr, rows, cols, kind="plain", lead=(), cmap=None, shape=None, dtype=None):
        self.arr, self.rows, self.cols, self.kind, self.lead, self.cmap = arr, rows, cols, kind, tuple(lead), cmap
        self.shape = tuple(arr.shape) if arr is not None else tuple(shape)
        self.dtype = arr.dtype if arr is not None else dtype

    def sds(self):
        return jax.ShapeDtypeStruct(self.shape, self.dtype)

    def spec(self, br, bc, gridmap):
        lead, nl = self.lead, len(self.lead)
        if self.kind == "plain":
            assert self.rows % br == 0 and self.cols % bc == 0, (self.shape, br, bc)
            cmap = self.cmap if self.cmap is not None else (lambda cb, _: cb)
            block = (None,) * nl + (br, bc)

            def phys(rb, cb):
                return lead + (rb, cmap(cb, bc))
        elif self.kind == "colstack":
            cs = self.shape[-1]
            assert cs % bc == 0 and self.rows % br == 0, (self.shape, br, bc)
            q = cs // bc
            block = (None,) * (nl + 1) + (br, bc)

            def phys(rb, cb):
                return (cb // q,) + lead + (rb, cb % q)
        else:
            rs = self.shape[-2]
            assert rs % br == 0 and self.cols % bc == 0, (self.shape, br, bc)
            q = rs // br
            block = (None,) * (nl + 1) + (br, bc)

            def phys(rb, cb):
                return (rb // q,) + lead + (rb % q, cb)

        return pl.BlockSpec(block, lambda *g: phys(*gridmap(*g)))


def _adamw_math(w, g, m, v):
    mn = ADAM_B1 * m + (1.0 - ADAM_B1) * g
    vn = ADAM_B2 * v + (1.0 - ADAM_B2) * jnp.square(g)
    m_hat = mn / (1.0 - ADAM_B1 ** ADAM_STEP)
    v_hat = vn / (1.0 - ADAM_B2 ** ADAM_STEP)
    return -ADAM_LR * (m_hat / (jnp.sqrt(v_hat) + ADAM_EPS) + ADAM_WD * w), mn, vn


def _matmul(name, a, b, mode, outs, tm, tn, tk, epilogue=None, extras=(), deps=()):
    if mode == "nn":
        m, k, n = a.rows, a.cols, b.cols
        a_spec = a.spec(tm, tk, lambda i, j, kk: (i, kk))
        b_spec = b.spec(tk, tn, lambda i, j, kk: (kk, j))
        dims = (((1,), (0,)), ((), ()))
    elif mode == "nt":
        m, k, n = a.rows, a.cols, b.rows
        a_spec = a.spec(tm, tk, lambda i, j, kk: (i, kk))
        b_spec = b.spec(tn, tk, lambda i, j, kk: (j, kk))
        dims = (((1,), (1,)), ((), ()))
    else:
        k, m, n = a.rows, a.cols, b.cols
        a_spec = a.spec(tk, tm, lambda i, j, kk: (kk, i))
        b_spec = b.spec(tk, tn, lambda i, j, kk: (kk, j))
        dims = (((0,), (0,)), ((), ()))
    assert m % tm == 0 and n % tn == 0 and k % tk == 0, (name, m, n, k, tm, tn, tk)
    grid = (m // tm, n // tn, k // tk)
    nk = grid[2]
    n_ex, n_out, n_dep = len(extras), len(outs), len(deps)
    tile = lambda i, j, kk: (i, j)

    def finish(z, ex, out_refs):
        vals = epilogue(z, *[e[...] for e in ex]) if epilogue is not None else (z,)
        for o, v in zip(out_refs, vals):
            o[...] = v.astype(o.dtype)

    def body_single(a_ref, b_ref, *rest):
        finish(lax.dot_general(a_ref[...], b_ref[...], dims, preferred_element_type=F32),
               rest[:n_ex], rest[n_ex + n_dep:n_ex + n_dep + n_out])

    def body_acc(a_ref, b_ref, *rest):
        acc = rest[-1]
        kk = pl.program_id(2)

        @pl.when(kk == 0)
        def _():
            acc[...] = jnp.zeros_like(acc)

        acc[...] += lax.dot_general(a_ref[...], b_ref[...], dims, preferred_element_type=F32)

        @pl.when(kk == nk - 1)
        def _():
            finish(acc[...], rest[:n_ex], rest[n_ex + n_dep:n_ex + n_dep + n_out])

    res = pl.pallas_call(
        body_single if nk == 1 else body_acc, name=name, grid=grid,
        in_specs=[a_spec, b_spec] + [e.spec(tm, tn, tile) for e in extras]
        + [pl.BlockSpec(memory_space=pl.ANY) for _ in deps],
        out_specs=[o.spec(tm, tn, tile) for o in outs],
        out_shape=[o.sds() for o in outs],
        scratch_shapes=[] if nk == 1 else [pltpu.VMEM((tm, tn), F32)],
        compiler_params=_params(("parallel", "parallel", "arbitrary")),
    )(a.arr, b.arr, *[e.arr for e in extras], *deps)
    return res


def _out(rows, cols, dtype, kind="plain", lead=(), shape=None):
    return Mat(None, rows, cols, kind, lead, shape=shape if shape is not None else (rows, cols), dtype=dtype)


def _rt(arr, tr, width=None, cb=0):
    width = arr.shape[1] if width is None else width
    return arr, pl.BlockSpec((tr, width), lambda i: (i, cb))


def _whole(arr):
    nd = arr.ndim
    return arr, pl.BlockSpec(arr.shape, lambda i: (0,) * nd)


def _rowwise(name, fn, n_steps, ins, outs, accs=(), deps=(), fill=None):
    n_in, n_out, n_acc, n_dep = len(ins), len(outs), len(accs), len(deps)

    def body(*refs):
        vals = fn(*[r[...] for r in refs[:n_in]])
        if not isinstance(vals, (tuple, list)):
            vals = (vals,)
        for ref, v in zip(refs[n_in + n_dep:n_in + n_dep + n_out], vals[:n_out]):
            ref[...] = v.astype(ref.dtype)
        if n_acc:
            acc_refs = refs[n_in + n_dep + n_out:]

            @pl.when(pl.program_id(0) == 0)
            def _():
                for ref in acc_refs:
                    ref[...] = jnp.zeros_like(ref)

            for ref, v in zip(acc_refs, vals[n_out:]):
                ref[...] += v

    acc_specs = [pl.BlockSpec(s.shape, lambda i, nd=len(s.shape): (0,) * nd) for s in accs]
    res = pl.pallas_call(
        body, name=name, grid=(n_steps,),
        in_specs=[s for _, s in ins] + [pl.BlockSpec(memory_space=pl.ANY) for _ in deps],
        out_specs=[s for _, s in outs] + acc_specs,
        out_shape=[o for o, _ in outs] + list(accs),
        input_output_aliases={} if fill is None else {n_in + fill[0]: fill[1]},
        compiler_params=_params(("arbitrary",) if n_acc else ("parallel",)),
    )(*[a for a, _ in ins], *deps)
    return res


def _rt_out(t, width, dtype, tr):
    return jax.ShapeDtypeStruct((t, width), dtype), pl.BlockSpec((tr, width), lambda i: (i, 0))


def _rms(x, g):
    r = lax.rsqrt(jnp.mean(x * x, axis=-1, keepdims=True) + EPS)
    return x * r * g


def _rms_bwd(dy, x, g):
    r = lax.rsqrt(jnp.mean(x * x, axis=-1, keepdims=True) + EPS)
    xh = x * r
    dxh = dy * g
    dx = r * (dxh - xh * jnp.mean(dxh * xh, axis=-1, keepdims=True))
    dg = jnp.sum(dy * xh, axis=0, keepdims=True)
    return dx, dg


def _gelu_and_grad(x):
    k = 0.7978845608028654
    x2 = x * x
    th = jnp.tanh(k * (x + 0.044715 * (x2 * x)))
    half = 0.5 * (1.0 + th)
    return x * half, half + 0.5 * x * (1.0 - th * th) * (k * (1.0 + 3.0 * 0.044715 * x2))


def _gelu(x):
    return _gelu_and_grad(x)[0]


def _gelu_grad(x):
    return _gelu_and_grad(x)[1]


def _norm_fwd(name, x, g, tr):
    t, d = x.shape
    return _rowwise(name, lambda xv, gv: _rms(xv, gv), t // tr, [_rt(x, tr), _whole(g)], [_rt_out(t, d, BF16, tr)])[0]


def _norm_bwd(name, dh, x, g, dres, tr):
    t, d = x.shape

    def fn(dhv, xv, gv, drv):
        dx, dg = _rms_bwd(dhv, xv, gv)
        dx = dx + drv
        return dx, dx, dg

    return _rowwise(name, fn, t // tr, [_rt(dh, tr), _rt(x, tr), _whole(g), _rt(dres, tr)],
                    [_rt_out(t, d, F32, tr), _rt_out(t, d, BF16, tr)], [jax.ShapeDtypeStruct((1, d), F32)])


def _rope_tables(posf, invf, cmask, smask, tr):
    t = posf.shape[0]

    def fn(p, f, cm, sm):
        ang = p * f
        return jnp.cos(ang) * cm, jnp.sin(ang) * sm

    return _rowwise("rope_tables", fn, t // tr, [_rt(posf, tr), _whole(invf), _whole(cmask), _whole(smask)],
                    [_rt_out(t, LANES, F32, tr), _rt_out(t, LANES, F32, tr)])


def _rot(v, c, s):
    return v * c + pltpu.roll(v, ROPE, axis=1) * s


def _rot_bwd(dv, c, s):
    return dv * c + pltpu.roll(dv * s, ROPE, axis=1)


def _rope_fwd(qfull, proj, kr_cb, ctab, stab, heads, tr):
    t = qfull.shape[0]
    hw = heads * LANES

    def fn(q, kr, c, s):
        parts = [q[:, :hw]] + [_rot(q[:, hw + h * LANES: hw + (h + 1) * LANES], c, s) for h in range(heads)]
        return jnp.concatenate(parts, axis=1), _rot(kr, c, s)

    return _rowwise("rope_fwd", fn, t // tr, [_rt(qfull, tr), _rt(proj, tr, LANES, kr_cb), _rt(ctab, tr), _rt(stab, tr)],
                    [_rt_out(t, 2 * hw, BF16, tr), _rt_out(t, LANES, BF16, tr)])


def _rope_bwd(dq1, dq2, dkr_h, ctab, stab, heads, tr, dproj, kr_cb):
    t = dq1.shape[0]
    hw = heads * LANES

    def fn(a, b, dk, c, s):
        parts = [a] + [_rot_bwd(b[:, h * LANES:(h + 1) * LANES], c, s) for h in range(heads)]
        dks = dk[0]
        for h in range(1, heads):
            dks = dks + dk[h]
        return jnp.concatenate(parts, axis=1), _rot_bwd(dks, c, s)

    dk_spec = pl.BlockSpec((heads, tr, LANES), lambda i: (0, i, 0))
    into = (jax.ShapeDtypeStruct(dproj.shape, dproj.dtype), pl.BlockSpec((tr, LANES), lambda i: (i, kr_cb)))
    return _rowwise("rope_bwd", fn, t // tr, [_rt(dq1, tr), _rt(dq2, tr), (dkr_h, dk_spec), _rt(ctab, tr), _rt(stab, tr)],
                    [_rt_out(t, 2 * hw, BF16, tr), into], deps=(dproj,), fill=(0, 1))


def _dot_nt(a, b):
    return lax.dot_general(a, b, (((1,), (1,)), ((), ())), preferred_element_type=F32)


def _dot_tn(a, b):
    return lax.dot_general(a, b, (((0,), (0,)), ((), ())), preferred_element_type=F32)


def _dot(a, b):
    return jnp.dot(a, b, preferred_element_type=F32)


def _ranges(n_blocks):
    n_var = min(4, n_blocks)
    assert n_blocks % n_var == 0
    return n_var, n_blocks // n_var


def _row_of(col):
    return col.T[:8, :]


def _attn_fwd(qall, kvall, kr, heads, scale, tq):
    t = qall.shape[0]
    nq = t // tq
    n_var, per = _ranges(nq)

    def body(qn_ref, qr_ref, kn_ref, v_ref, kr_ref, o_ref, lser_ref):
        i = pl.program_id(1)
        for var in range(n_var):
            kv = (var + 1) * per * tq

            @pl.when(jnp.logical_and(i >= var * per, i < (var + 1) * per))
            def _(kv=kv):
                s = _dot_nt(jnp.concatenate([qn_ref[...], qr_ref[...]], axis=1),
                            jnp.concatenate([kn_ref[:kv, :], kr_ref[:kv, :]], axis=1)) * scale
                rows = i * tq + lax.broadcasted_iota(jnp.int32, (tq, kv), 0)
                cols = lax.broadcasted_iota(jnp.int32, (tq, kv), 1)
                s = jnp.where(cols <= rows, s, NEG)
                m = jnp.max(s, axis=-1, keepdims=True)
                p = jnp.exp(s - m)
                l = jnp.sum(p, axis=-1, keepdims=True)
                o_ref[...] = _dot(p.astype(BF16), v_ref[:kv, :]) / l
                lser_ref[...] = _row_of(jnp.broadcast_to(m + jnp.log(l), (tq, LANES)))

    return pl.pallas_call(
        body, name="attn_fwd", grid=(heads, nq),
        in_specs=[pl.BlockSpec((tq, LANES), lambda h, i: (i, h)),
                  pl.BlockSpec((tq, LANES), lambda h, i: (i, heads + h)),
                  pl.BlockSpec((t, LANES), lambda h, i: (0, h)),
                  pl.BlockSpec((t, LANES), lambda h, i: (0, heads + h)),
                  pl.BlockSpec((t, LANES), lambda h, i: (0, 0))],
        out_specs=[pl.BlockSpec((tq, LANES), lambda h, i: (i, h)),
                   pl.BlockSpec((None, 8, tq), lambda h, i: (h, 0, i))],
        out_shape=[jax.ShapeDtypeStruct((t, heads * LANES), F32), jax.ShapeDtypeStruct((heads, 8, t), F32)],
        compiler_params=_params(("parallel", "parallel")),
    )(qall, qall, kvall, kvall, kr)


def _attn_bwd(qall, kvall, kr, do, lse_row, delta_row, heads, scale, tk):
    t = qall.shape[0]
    nk = t // tk
    n_var, per = _ranges(nk)

    def body(qn_ref, qr_ref, kn_ref, v_ref, kr_ref, do_ref, lse_ref, dl_ref, dq1_ref, dq2_ref, dk_ref, dv_ref, dkr_ref):
        j = pl.program_id(1)

        @pl.when(j == 0)
        def _():
            dq1_ref[...] = jnp.zeros_like(dq1_ref)
            dq2_ref[...] = jnp.zeros_like(dq2_ref)

        for var in range(n_var):
            q0 = var * per * tk
            nq = t - q0

            @pl.when(jnp.logical_and(j >= var * per, j < (var + 1) * per))
            def _(q0=q0, nq=nq):
                qn, qr, do_v = qn_ref[q0:, :], qr_ref[q0:, :], do_ref[q0:, :]
                k1, k2 = kn_ref[...], kr_ref[...]
                qcat, kcat = jnp.concatenate([qn, qr], axis=1), jnp.concatenate([k1, k2], axis=1)
                st = _dot_nt(kcat, qcat) * scale
                keys = j * tk + lax.broadcasted_iota(jnp.int32, (tk, nq), 0)
                queries = q0 + lax.broadcasted_iota(jnp.int32, (tk, nq), 1)
                pt = jnp.where(keys <= queries, jnp.exp(st - lse_ref[0:1, q0:]), 0.0)
                dpt = _dot_nt(v_ref[...], do_v)
                dst = (pt * (dpt - dl_ref[0:1, q0:]) * scale).astype(BF16)
                dv_ref[...] = _dot(pt.astype(BF16), do_v).astype(dv_ref.dtype)
                dkc = _dot(dst, qcat)
                dk_ref[...] = dkc[:, :LANES].astype(dk_ref.dtype)
                dkr_ref[...] = dkc[:, LANES:]
                dqc = _dot_tn(dst, kcat)
                dq1_ref[q0:, :] += dqc[:, :LANES]
                dq2_ref[q0:, :] += dqc[:, LANES:]

    kblk = lambda off: pl.BlockSpec((tk, LANES), lambda h, j: (j, off + h))
    full = lambda off: pl.BlockSpec((t, LANES), lambda h, j: (0, off + h))
    stat = pl.BlockSpec((None, 8, t), lambda h, j: (h, 0, 0))
    return pl.pallas_call(
        body, name="attn_bwd", grid=(heads, nk),
        in_specs=[full(0), full(heads), kblk(0), kblk(heads), pl.BlockSpec((tk, LANES), lambda h, j: (j, 0)),
                  full(0), stat, stat],
        out_specs=[full(0), full(0), kblk(0), kblk(0), pl.BlockSpec((None, tk, LANES), lambda h, j: (h, j, 0))],
        out_shape=[jax.ShapeDtypeStruct((t, heads * LANES), F32)] * 2 + [jax.ShapeDtypeStruct((t, heads * LANES), BF16)] * 2
        + [jax.ShapeDtypeStruct((heads, t, LANES), F32)],
        compiler_params=_params(("parallel", "arbitrary")),
    )(qall, qall, kvall, kvall, kr, do, lse_row, delta_row)


def _tril():
    return lax.broadcasted_iota(jnp.int32, (LANES, LANES), 0) >= lax.broadcasted_iota(jnp.int32, (LANES, LANES), 1)


def _group_norm(vg):
    mu = jnp.mean(vg, axis=-1, keepdims=True)
    vc = vg - mu
    rs = lax.rsqrt(jnp.mean(vc * vc, axis=-1, keepdims=True) + EPS)
    return vc * rs, rs


def _sgu_fwd(proj, gain, w, bias, groups, rb):
    t = proj.shape[0]
    gw = groups * LANES
    cpb = rb // LANES

    def body(u_ref, v_ref, gain_ref, w_ref, b_ref, s_ref):
        tril = _tril()
        for g in range(groups):
            wt = jnp.where(tril, w_ref[g], 0.0).astype(BF16)
            cols = slice(g * LANES, (g + 1) * LANES)
            for ci in range(cpb):
                rows = slice(ci * LANES, (ci + 1) * LANES)
                ug = _gelu(u_ref[rows, cols])
                vh, _ = _group_norm(_gelu(v_ref[rows, cols]))
                vn = vh * gain_ref[:, cols]
                y = _dot(wt, vn.astype(BF16)) + b_ref[g]
                s_ref[rows, cols] = ug * y

    return pl.pallas_call(
        body, name="sgu_fwd", grid=(t // rb,),
        in_specs=[pl.BlockSpec((rb, gw), lambda i: (i, 0)), pl.BlockSpec((rb, gw), lambda i: (i, 1)),
                  pl.BlockSpec((1, gw), lambda i: (0, 0)),
                  pl.BlockSpec((groups, LANES, LANES), lambda i: (0, 0, 0)),
                  pl.BlockSpec((groups, LANES, LANES), lambda i: (0, 0, 0))],
        out_specs=pl.BlockSpec((rb, gw), lambda i: (i, 0)),
        out_shape=jax.ShapeDtypeStruct((t, gw), F32),
        compiler_params=_params(("parallel",)),
    )(proj, proj, gain, w, bias)


def _sgu_bwd(proj, ds, gain, w, bias, groups, rb):
    t, width = proj.shape
    gw = groups * LANES
    cpb = rb // LANES
    n_steps = t // rb

    def body(u_ref, v_ref, ds_ref, gain_ref, w_ref, b_ref, dp_ref, dw_ref, db_ref, dg_ref, dy_acc):
        du_ref, dv_ref = dp_ref.at[:, :gw], dp_ref.at[:, gw:]
        step = pl.program_id(0)

        @pl.when(step == 0)
        def _():
            dw_ref[...] = jnp.zeros_like(dw_ref)
            dy_acc[...] = jnp.zeros_like(dy_acc)
            dg_ref[...] = jnp.zeros_like(dg_ref)

        tril = _tril()
        for g in range(groups):
            wt = jnp.where(tril, w_ref[g], 0.0).astype(BF16)
            cols = slice(g * LANES, (g + 1) * LANES)
            gain_g = gain_ref[:, cols]
            for ci in range(cpb):
                rows = slice(ci * LANES, (ci + 1) * LANES)
                u_raw, v_raw, ds_v = u_ref[rows, cols], v_ref[rows, cols], ds_ref[rows, cols]
                ug, ug_grad = _gelu_and_grad(u_raw)
                vg, vg_grad = _gelu_and_grad(v_raw)
                vh, rs = _group_norm(vg)
                vn = (vh * gain_g).astype(BF16)
                y = _dot(wt, vn) + b_ref[g]
                dy = ds_v * ug
                dyb = dy.astype(BF16)
                du_ref[rows, cols] = (ds_v * y * ug_grad).astype(du_ref.dtype)
                dy_acc[g] += dy
                dw_ref[g] += _dot_nt(dyb, vn)
                dvn = _dot_tn(wt, dyb)
                dg_ref[:, cols] += jnp.sum(dvn * vh, axis=0, keepdims=True)
                dvh = dvn * gain_g
                dvg = rs * (dvh - jnp.mean(dvh, axis=-1, keepdims=True)
                            - vh * jnp.mean(dvh * vh, axis=-1, keepdims=True))
                dv_ref[rows, cols] = (dvg * vg_grad).astype(dv_ref.dtype)

        @pl.when(step == n_steps - 1)
        def _():
            ones = jnp.ones((8, LANES), F32)
            for g in range(groups):
                dw_ref[g] = jnp.where(tril, dw_ref[g], 0.0)
                db_ref[g] = lax.dot_general(ones, dy_acc[g], (((1,), (1,)), ((), ())),
                                            precision=lax.Precision.HIGHEST, preferred_element_type=F32)

    blk = lambda cb: pl.BlockSpec((rb, gw), lambda i: (i, cb))
    whole3 = pl.BlockSpec((groups, LANES, LANES), lambda i: (0, 0, 0))
    return pl.pallas_call(
        body, name="sgu_bwd", grid=(n_steps,),
        in_specs=[blk(0), blk(1), blk(0), pl.BlockSpec((1, gw), lambda i: (0, 0)), whole3, whole3],
        out_specs=[pl.BlockSpec((rb, 2 * gw), lambda i: (i, 0)), whole3,
                   pl.BlockSpec((groups, 8, LANES), lambda i: (0, 0, 0)), pl.BlockSpec((1, gw), lambda i: (0, 0))],
        out_shape=[jax.ShapeDtypeStruct((t, width), BF16),
                   jax.ShapeDtypeStruct((groups, LANES, LANES), F32), jax.ShapeDtypeStruct((groups, 8, LANES), F32),
                   jax.ShapeDtypeStruct((1, gw), F32)],
        scratch_shapes=[pltpu.VMEM((groups, LANES, LANES), F32)],
        compiler_params=_params(("arbitrary",)),
    )(proj, proj, ds, gain, w, bias)


def _shift_down(z, s):
    rows = lax.broadcasted_iota(jnp.int32, z.shape, 0)
    return jnp.where(rows >= s, pltpu.roll(z, s, axis=0), 0.0)


def _shift_up(z, s):
    n = z.shape[0]
    rows = lax.broadcasted_iota(jnp.int32, z.shape, 0)
    return jnp.where(rows < n - s, pltpu.roll(z, n - s, axis=0), 0.0)


def _conv_fwd(proj3, cw, tc):
    _, t, cd = proj3.shape

    def body(p_ref, w_ref, o_ref):
        z = p_ref[1] * p_ref[2]
        w = w_ref[...]
        zc = w[2:3] * z + w[1:2] * _shift_down(z, 1) + w[0:1] * _shift_down(z, 2)
        o_ref[...] = (p_ref[0] * zc).astype(o_ref.dtype)

    return pl.pallas_call(
        body, name="conv_fwd", grid=(cd // tc,),
        in_specs=[pl.BlockSpec((3, t, tc), lambda j: (0, 0, j)), pl.BlockSpec((8, tc), lambda j: (0, j))],
        out_specs=pl.BlockSpec((t, tc), lambda j: (0, j)),
        out_shape=jax.ShapeDtypeStruct((t, cd), BF16),
        compiler_params=_params(("parallel",)),
    )(proj3, cw)


def _conv_bwd(proj3, cw, dbz, tc):
    _, t, cd = proj3.shape

    def body(p_ref, w_ref, d_ref, o_ref, dw_ref):
        b, c, xin = p_ref[0], p_ref[1], p_ref[2]
        w = w_ref[...]
        z = c * xin
        z1, z2 = _shift_down(z, 1), _shift_down(z, 2)
        zc = w[2:3] * z + w[1:2] * z1 + w[0:1] * z2
        d = d_ref[...]
        dzc = d * b
        dz = w[2:3] * dzc + w[1:2] * _shift_up(dzc, 1) + w[0:1] * _shift_up(dzc, 2)
        o_ref[0] = (d * zc).astype(o_ref.dtype)
        o_ref[1] = (dz * xin).astype(o_ref.dtype)
        o_ref[2] = (dz * c).astype(o_ref.dtype)
        row = lax.broadcasted_iota(jnp.int32, (8, tc), 0)
        dw0 = jnp.sum(dzc * z2, axis=0, keepdims=True)
        dw1 = jnp.sum(dzc * z1, axis=0, keepdims=True)
        dw2 = jnp.sum(dzc * z, axis=0, keepdims=True)
        dw_ref[...] = jnp.where(row == 0, dw0, 0.0) + jnp.where(row == 1, dw1, 0.0) + jnp.where(row == 2, dw2, 0.0)

    return pl.pallas_call(
        body, name="conv_bwd", grid=(cd // tc,),
        in_specs=[pl.BlockSpec((3, t, tc), lambda j: (0, 0, j)), pl.BlockSpec((8, tc), lambda j: (0, j)),
                  pl.BlockSpec((t, tc), lambda j: (0, j))],
        out_specs=[pl.BlockSpec((3, t, tc), lambda j: (0, 0, j)), pl.BlockSpec((8, tc), lambda j: (0, j))],
        out_shape=[jax.ShapeDtypeStruct((3, t, cd), BF16), jax.ShapeDtypeStruct((8, cd), F32)],
        compiler_params=_params(("parallel",)),
    )(proj3, cw, dbz)


def _place():
    x, y, c = lax.axis_index("x"), lax.axis_index("y"), lax.axis_index("c")
    chips = [(1 - x, y), (x, 1 - y), (1 - x, 1 - y)]
    return x, y, c, chips


def _any_specs(n):
    return [pl.BlockSpec(memory_space=pl.ANY) for _ in range(n)]


HBM_SPEC = pl.BlockSpec(memory_space=pltpu.HBM)
SEM_SPEC = pl.BlockSpec(memory_space=pltpu.SEMAPHORE)
ORDERED_EFFECT = pltpu.SideEffectType.DATAFLOW_SIDE_EFFECTING


def _in_hbm(a):
    return pltpu.with_memory_space_constraint(a, pltpu.HBM)


def _token():
    return jax.ShapeDtypeStruct((8, LANES), F32), pl.BlockSpec(memory_space=pltpu.VMEM)


def _gather_start(name, groups):
    sizes = [len(g) for g in groups]
    flat = [b for g in groups for b in g]
    n, ng = len(flat), len(groups)

    def body(*refs):
        ins, sems, token = refs[:n], refs[n:n + 2 * ng], refs[-1]
        x, y, c, chips = _place()
        me = 2 * x + y
        i = 0
        for gi, size in enumerate(sizes):
            for j in range(size):
                blk = ins[i].at[me, c]
                for k, chip in enumerate(chips):
                    pltpu.make_async_remote_copy(src_ref=blk, dst_ref=blk, send_sem=sems[2 * gi].at[3 * j + k],
                                                 recv_sem=sems[2 * gi + 1].at[3 * j + k],
                                                 device_id=(*chip, c), device_id_type=MESH).start()
                i += 1
        token[...] = jnp.zeros_like(token)

    tok_shape, tok_spec = _token()
    res = pl.pallas_call(
        body, name=name,
        in_specs=[HBM_SPEC] * n,
        out_specs=[SEM_SPEC] * (2 * ng) + [HBM_SPEC] * n + [tok_spec],
        out_shape=[pltpu.SemaphoreType.DMA((3 * size,)) for size in sizes for _ in (0, 1)]
        + [pltpu.HBM(b.shape, b.dtype) for b in flat] + [tok_shape],
        input_output_aliases={i: 2 * ng + i for i in range(n)},
        compiler_params=pltpu.CompilerParams(has_side_effects=ORDERED_EFFECT),
    )(*[_in_hbm(b) for b in flat])
    out, i = [], 2 * ng
    for gi, size in enumerate(sizes):
        out.append((res[2 * gi], res[2 * gi + 1], list(res[i:i + size])))
        i += size
    return out, res[-1]


def _gather_wait(tag, send, recv, bufs, after):
    n = len(bufs)
    after = tuple(after) if isinstance(after, (tuple, list)) else (after,)

    def body(*refs):
        ins, send_ref, recv_ref = refs[:n], refs[n], refs[n + 1]
        x, y, c, chips = _place()
        me = 2 * x + y
        for j in range(n):
            for k, (px, py) in enumerate(chips):
                cp = pltpu.make_async_remote_copy(src_ref=ins[j].at[me, c], dst_ref=ins[j].at[2 * px + py, c],
                                                  send_sem=send_ref.at[3 * j + k], recv_sem=recv_ref.at[3 * j + k],
                                                  device_id=(px, py, c), device_id_type=MESH)
                cp.wait_send()
                cp.wait_recv()

    return pl.pallas_call(
        body, name="gather_wait_" + tag,
        in_specs=[HBM_SPEC] * n + [SEM_SPEC, SEM_SPEC] + _any_specs(len(after)),
        out_specs=[HBM_SPEC] * n,
        out_shape=[pltpu.HBM(b.shape, b.dtype) for b in bufs],
        input_output_aliases={i: i for i in range(n)},
        compiler_params=pltpu.CompilerParams(has_side_effects=ORDERED_EFFECT),
    )(*bufs, send, recv, *after)


def _gather_forward(tag, bufs):
    n = len(bufs)

    def body(*refs):
        ins, outs = refs[:n], refs[n:2 * n]
        send, recv = refs[2 * n:]
        x, y, c, chips = _place()
        sib = (x, y, 1 - c)

        def cp(i, k, slot, half):
            return pltpu.make_async_remote_copy(src_ref=ins[i].at[slot, half], dst_ref=outs[i].at[slot, half],
                                                send_sem=send.at[3 * i + k], recv_sem=recv.at[3 * i + k],
                                                device_id=sib, device_id_type=MESH)

        cps = [cp(i, k, 2 * px + py, c) for i in range(n) for k, (px, py) in enumerate(chips)]
        for d in cps:
            d.start()
        for i in range(n):
            for k, (px, py) in enumerate(chips):
                cp(i, k, 2 * px + py, 1 - c).wait_recv()
        for d in cps:
            d.wait_send()

    return pl.pallas_call(
        body, name="gather_forward_" + tag,
        in_specs=_any_specs(n), out_specs=_any_specs(n),
        out_shape=[jax.ShapeDtypeStruct(b.shape, b.dtype) for b in bufs],
        scratch_shapes=[pltpu.SemaphoreType.DMA((3 * n,))] * 2,
        input_output_aliases={i: i for i in range(n)},
        compiler_params=pltpu.CompilerParams(has_side_effects=True),
    )(*bufs)


def _pair_route(srcs, zones):
    x, y, c, _ = _place()
    return [(srcs[i].at[j, 1 - c], zones[i].at[j], (x, y, 1 - c)) for i in range(len(srcs)) for j in range(N_CHIPS)]


def _slab_route(srcs, zones):
    x, y, c, _ = _place()
    return [(srcs[i].at[j], zones[i].at[j], (x, y, 1 - c)) for i in range(len(srcs)) for j in range(N_CHIPS)]


def _chip_route(srcs, zones):
    x, y, c, chips = _place()
    return [(srcs[i].at[2 * px + py], zones[i].at[k], (px, py, c)) for i in range(len(srcs)) for k, (px, py) in enumerate(chips)]


def _all_route(srcs, zones):
    x, y, c, _ = _place()
    flips = [(fx, fy, fc) for fx in (0, 1) for fy in (0, 1) for fc in (0, 1)][1:]
    return [(srcs[0], zones[0].at[4 * x + 2 * y + c], (x + fx - 2 * x * fx, y + fy - 2 * y * fy, c + fc - 2 * c * fc))
            for fx, fy, fc in flips]


def _share_route(srcs, zones):
    x, y, c, _ = _place()
    return [(s.at[c], s.at[c], (x, y, 1 - c)) for s in srcs]


def _exchange_start(name, route, n_copies, srcs, zones):
    n, nz = len(srcs), len(zones)
    lands = [lax.empty(z, a.dtype) if isinstance(z, tuple) else z for z, a in zip(zones, srcs)]

    def body(*refs):
        ins, zone_refs, send, recv, token = refs[:n], refs[n:n + nz], refs[n + nz], refs[n + nz + 1], refs[-1]
        for k, (src, dst, dev) in enumerate(route(ins, zone_refs)):
            pltpu.make_async_remote_copy(src_ref=src, dst_ref=dst, send_sem=send.at[k], recv_sem=recv.at[k],
                                         device_id=dev, device_id_type=MESH).start()
        token[...] = jnp.zeros_like(token)

    tok_shape, tok_spec = _token()
    res = pl.pallas_call(
        body, name=name,
        in_specs=[HBM_SPEC] * (n + nz),
        out_specs=[SEM_SPEC, SEM_SPEC] + [HBM_SPEC] * (n + nz) + [tok_spec],
        out_shape=[pltpu.SemaphoreType.DMA((n_copies,))] * 2 + [pltpu.HBM(a.shape, a.dtype) for a in srcs + lands]
        + [tok_shape],
        input_output_aliases={i: 2 + i for i in range(n + nz)},
        compiler_params=pltpu.CompilerParams(has_side_effects=ORDERED_EFFECT),
    )(*[_in_hbm(a) for a in srcs + lands])
    return (res[0], res[1], list(res[2:2 + n]), list(res[2 + n:2 + n + nz])), res[-1]


def _exchange_wait(name, route, started, after):
    send, recv, srcs, lands = started
    n, nz = len(srcs), len(lands)
    after = tuple(after) if isinstance(after, (tuple, list)) else (after,)

    def body(*refs):
        ins, zone_refs, send_ref, recv_ref = refs[:n], refs[n:n + nz], refs[n + nz], refs[n + nz + 1]
        for k, (src, dst, dev) in enumerate(route(ins, zone_refs)):
            cp = pltpu.make_async_remote_copy(src_ref=src, dst_ref=dst, send_sem=send_ref.at[k], recv_sem=recv_ref.at[k],
                                              device_id=dev, device_id_type=MESH)
            cp.wait_send()
            cp.wait_recv()

    res = pl.pallas_call(
        body, name=name,
        in_specs=[HBM_SPEC] * (n + nz) + [SEM_SPEC, SEM_SPEC] + _any_specs(len(after)),
        out_specs=[HBM_SPEC] * (n + nz),
        out_shape=[pltpu.HBM(a.shape, a.dtype) for a in srcs + lands],
        input_output_aliases={i: i for i in range(n + nz)},
        compiler_params=pltpu.CompilerParams(has_side_effects=ORDERED_EFFECT),
    )(*srcs, *lands, send, recv, *after)
    return list(res[:n]), list(res[n:])


def _spread(v):
    rows, cols = v.shape
    tr = _row_tile(rows, cols, budget=256 * 1024)

    def body(v_ref, o_ref):
        o_ref[...] = jnp.broadcast_to(v_ref[...][None], o_ref.shape)

    return pl.pallas_call(body, name="spread_small_grads", grid=(rows // tr,),
                          in_specs=[pl.BlockSpec((tr, cols), lambda r: (r, 0))],
                          out_specs=pl.BlockSpec((8, tr, cols), lambda r: (0, r, 0)),
                          out_shape=jax.ShapeDtypeStruct((8, rows, cols), v.dtype),
                          compiler_params=_params(("parallel",)))(v)


def _row_tile(rows, cols, itemsize=4, budget=2 * 1024 * 1024, step=8):
    best = None
    for t in range(step, rows + 1, step):
        if rows % t == 0 and t * cols * itemsize <= budget:
            best = t
    return best if best is not None else rows


def _my_chip():
    return 2 * lax.axis_index("x") + lax.axis_index("y")


def _pair_sum(g5, gsib):
    _, _, rh, cols = g5.shape
    tr = _row_tile(rh, cols, step=16)

    def body(a_ref, b_ref, o_ref):
        o_ref[...] = (a_ref[...].astype(F32) + b_ref[...].astype(F32)).astype(o_ref.dtype)

    return pl.pallas_call(body, name="grad_pair_sum", grid=(N_CHIPS, rh // tr),
                          in_specs=[pl.BlockSpec((None, None, tr, cols), lambda j, r: (j, lax.axis_index("c"), r, 0)),
                                    pl.BlockSpec((None, tr, cols), lambda j, r: (j, r, 0))],
                          out_specs=pl.BlockSpec((None, tr, cols), lambda j, r: (j, r, 0)),
                          out_shape=jax.ShapeDtypeStruct((N_CHIPS, rh, cols), BF16),
                          compiler_params=_params(("parallel", "parallel")))(g5, gsib)


def _chip_sum(part, recv):
    _, rh, cols = part.shape
    tr = _row_tile(rh, cols, step=16)

    def body(a_ref, b_ref, o_ref):
        acc = a_ref[...].astype(F32)
        for k in range(3):
            acc = acc + b_ref[k].astype(F32)
        o_ref[...] = acc

    return pl.pallas_call(body, name="grad_chip_sum", grid=(rh // tr,),
                          in_specs=[pl.BlockSpec((None, tr, cols), lambda r: (_my_chip(), r, 0)),
                                    pl.BlockSpec((3, tr, cols), lambda r: (0, r, 0))],
                          out_specs=pl.BlockSpec((None, tr, cols), lambda r: (lax.axis_index("c"), r, 0)),
                          out_shape=jax.ShapeDtypeStruct((2, rh, cols), F32),
                          compiler_params=_params(("parallel",)))(part, recv)


def _sum_devices(g):
    _, rows, cols = g.shape
    tr = _row_tile(rows, cols, budget=256 * 1024)

    def body(g_ref, o_ref):
        acc = g_ref[0]
        for d in range(1, 8):
            acc = acc + g_ref[d]
        o_ref[...] = acc

    return pl.pallas_call(body, name="sum_small_grads", grid=(rows // tr,),
                          in_specs=[pl.BlockSpec((8, tr, cols), lambda r: (0, r, 0))],
                          out_specs=pl.BlockSpec((tr, cols), lambda r: (r, 0)),
                          out_shape=jax.ShapeDtypeStruct((rows, cols), F32),
                          compiler_params=_params(("parallel",)))(g)


def _place_shard(w, layer, dtype, deps=()):
    _, rows, cols = w.shape
    tr = _row_tile(rows, cols)

    def body(i_ref, *rest):
        o_ref = rest[-1]
        o_ref[...] = i_ref[...].astype(o_ref.dtype)

    out = pl.pallas_call(body, name="place_shard", grid=(rows // tr,),
                         in_specs=[pl.BlockSpec((None, tr, cols), lambda r: (layer, r, 0))] + _any_specs(len(deps)),
                         out_specs=pl.BlockSpec((None, tr, cols), lambda r: (_my_chip(), r, 0)),
                         out_shape=jax.ShapeDtypeStruct((N_CHIPS, rows, cols), dtype),
                         compiler_params=_params(("parallel",)))(w, *deps)
    return out.reshape(N_CHIPS, 2, rows // 2, cols)


def _adamw(w, gs, m, v, deps=()):
    n_layers, rows, cols = w.shape
    tr = _row_tile(rows, cols)

    def body(w_ref, m_ref, v_ref, *rest):
        g_refs = rest[:n_layers]
        go_ref, d_ref, mo_ref, vo_ref = rest[-4:]
        gv = g_refs[0][...]
        for layer in range(1, n_layers):
            gv = jnp.where(pl.program_id(0) == layer, g_refs[layer][...], gv)
        d_ref[...], mo_ref[...], vo_ref[...] = _adamw_math(w_ref[...], gv, m_ref[...], v_ref[...])
        go_ref[...] = gv

    spec = pl.BlockSpec((None, tr, cols), lambda layer, r: (layer, r, 0))
    g_specs = [pl.BlockSpec((tr, cols), lambda layer, r, own=own: (jnp.where(layer == own, r, 0), 0))
               for own in range(n_layers)]
    return pl.pallas_call(body, name="adamw", grid=(n_layers, rows // tr),
                          in_specs=[spec] * 3 + g_specs + _any_specs(len(deps)),
                          out_specs=[spec] * 4, out_shape=[jax.ShapeDtypeStruct((n_layers, rows, cols), F32)] * 4,
                          compiler_params=_params(("parallel", "parallel")))(w, m, v, *gs, *deps)


def _pad_rope(w):
    z = jnp.zeros(w.shape[:-1] + (ROPE_HALF,), w.dtype)
    return jnp.concatenate([w[..., :ROPE_HALF], z, w[..., ROPE_HALF:], z], axis=-1)


def _unpad_rope(g):
    return jnp.concatenate([g[..., :ROPE_HALF], g[..., ROPE:ROPE + ROPE_HALF]], axis=-1)


def _unstack_cols(s):
    n, r, cs = s.shape
    return jnp.transpose(s, (1, 0, 2)).reshape(r, n * cs)


def _stack_cols(f):
    r, cfull = f.shape
    return jnp.transpose(f.reshape(r, N_CHIPS, cfull // N_CHIPS), (1, 0, 2))


def _small_shard(norm, conv):
    return jnp.concatenate([jnp.pad(norm, ((0, 15), (0, 0))), jnp.pad(conv, ((0, 13), (0, 0)))], axis=0)


def _flat_rows(a):
    return a.reshape(-1, LANES)


def _pack_small(arrs):
    return jnp.concatenate([_flat_rows(a.astype(F32)) for a in arrs], axis=0)


def _unpack_small(flat, like):
    out, r = [], 0
    for a in like:
        n = a.size // LANES
        out.append(flat[r:r + n].reshape(a.shape))
        r += n
    return out


def kernel(x, positions, e_norm_mix, e_w_in, e_q_norm, e_w_uq, e_kv_norm, e_w_ukv, e_v_norm, e_sgu_w, e_sgu_b, e_mla_out_norm, e_sgu_out_norm, e_w_out, o_norm_mix, o_w_in, o_conv_w, o_w_out, mlp_norm, mlp_w1, mlp_w2, final_norm, loss_target, m_e_norm_mix, m_e_w_in, m_e_q_norm, m_e_w_uq, m_e_kv_norm, m_e_w_ukv, m_e_v_norm, m_e_sgu_w, m_e_sgu_b, m_e_mla_out_norm, m_e_sgu_out_norm, m_e_w_out, m_o_norm_mix, m_o_w_in, m_o_conv_w, m_o_w_out, m_mlp_norm, m_mlp_w1, m_mlp_w2, m_final_norm, v_e_norm_mix, v_e_w_in, v_e_q_norm, v_e_w_uq, v_e_kv_norm, v_e_w_ukv, v_e_v_norm, v_e_sgu_w, v_e_sgu_b, v_e_mla_out_norm, v_e_sgu_out_norm, v_e_w_out, v_o_norm_mix, v_o_w_in, v_o_conv_w, v_o_w_out, v_mlp_norm, v_mlp_w1, v_mlp_w2, v_final_norm):
    t, d = x.shape[1], x.shape[2]
    ql, kvl = e_q_norm.shape[1], e_kv_norm.shape[1]
    groups = e_v_norm.shape[1]
    gw = groups * LANES
    heads = N_CHIPS * e_w_uq.shape[2] // (LANES + ROPE)
    hw = heads * LANES
    mix = hw + gw
    ei = N_CHIPS * e_w_in.shape[2]
    cd = N_CHIPS * o_conv_w.shape[2]
    ff = N_CHIPS * mlp_w1.shape[2]
    ffs = ff // N_CHIPS
    pi = 2 * gw + ql + kvl + LANES
    assert e_norm_mix.shape[0] == 1 and o_norm_mix.shape[0] == 1 and mlp_norm.shape[0] == 2
    assert ei == ql + kvl + ROPE + 2 * gw and cd == d and e_sgu_w.shape[2] == LANES
    assert (2 * gw) % ql == 0 and (2 * gw + ql) % kvl == 0 and t % LANES == 0
    scale = (LANES + ROPE) ** -0.5

    tr = min(256, t)
    tm = _pick(t, 1024, 8)
    kt, kd = _pick(t, 2048, 8), _pick(d, 2048)
    xs = x.reshape(t, d)
    tgt = loss_target.reshape(t, d)

    small_shard = _small_shard(o_norm_mix, o_conv_w[0])
    first, tok = _gather_start("gather_start_e", [
        [_place_shard(e_w_in, 0, BF16)],
        [_place_shard(e_w_uq, 0, BF16), _place_shard(e_w_ukv, 0, BF16), _place_shard(e_w_out, 0, BF16),
         _place_shard(small_shard[None], 0, F32)]])
    rest, tok = _gather_start("gather_start_rest", [
        [_place_shard(mlp_w1, 0, BF16, (tok,))], [_place_shard(mlp_w2, 0, BF16, (tok,))],
        [_place_shard(o_w_in, 0, BF16, (tok,)), _place_shard(o_w_out, 0, BF16, (tok,))],
        [_place_shard(mlp_w1, 1, BF16, (tok,))], [_place_shard(mlp_w2, 1, BF16, (tok,))]])
    started = first + rest

    def gathered(gi, tag, after):
        send, recv, bufs = started[gi]
        bufs = _gather_forward(tag, _gather_wait(tag, send, recv, bufs, after))
        return [b.reshape(N_CHIPS, 2 * b.shape[2], b.shape[3]) for b in bufs]

    g_e = e_norm_mix
    h0 = _norm_fwd("e_norm", xs, g_e, tr)
    inv_freq = ROPE_BASE ** (-jnp.arange(0, ROPE, 2, dtype=F32) / ROPE)
    zeros32 = jnp.zeros((ROPE_HALF,), F32)
    ones32 = jnp.ones((ROPE_HALF,), F32)
    invf = jnp.concatenate([inv_freq, zeros32, inv_freq, zeros32]).reshape(1, LANES)
    cmask = jnp.concatenate([ones32, zeros32, ones32, zeros32]).reshape(1, LANES)
    smask = jnp.concatenate([-ones32, zeros32, ones32, zeros32]).reshape(1, LANES)
    ctab, stab = _rope_tables(positions.reshape(t, 1).astype(F32), invf, cmask, smask, tr)

    w_in_g, = gathered(0, "e_in", (h0, ctab, tok))
    full = _unstack_cols(w_in_g)
    c2, c3 = ql + kvl, ql + kvl + ROPE
    w_in_all = jnp.concatenate([full[:, c3:], full[:, :c2], _pad_rope(full[:, c2:c3])], axis=1)
    proj, = _matmul("e_proj", Mat(h0, t, d), Mat(w_in_all, d, pi), "nn", [_out(t, pi, F32)], tm, _pick(pi, 1024), kd)

    w_uq_g, w_ukv_g, w_eout_g, small_g = gathered(1, "e", proj)
    full = _unstack_cols(w_uq_g).reshape(ql, heads, LANES + ROPE)
    w_q_all = jnp.concatenate([full[:, :, :LANES].reshape(ql, hw), _pad_rope(full[:, :, LANES:]).reshape(ql, hw)], axis=1)
    full = _unstack_cols(w_ukv_g).reshape(kvl, heads, 2 * LANES)
    w_kv_all = jnp.concatenate([full[:, :, :LANES].reshape(kvl, hw), full[:, :, LANES:].reshape(kvl, hw)], axis=1)
    w_eout = w_eout_g.reshape(mix, d)
    g_o = small_g[:, 0].reshape(1, d)
    conv_w = jnp.pad(jnp.transpose(small_g[:, 16:19], (1, 0, 2)).reshape(3, cd), ((0, 5), (0, 0)))

    g_q, g_kv = e_q_norm, e_kv_norm
    g_vn = e_v_norm.reshape(1, gw)
    sgu_w = e_sgu_w[0]
    sgu_b = jnp.broadcast_to(e_sgu_b[0][:, :, None], (groups, LANES, LANES))
    g_mla, g_sgu = e_mla_out_norm, e_sgu_out_norm
    g_m0, g_m1 = mlp_norm[0:1], mlp_norm[1:2]
    g_f = final_norm.reshape(1, d)

    def mlp_fwd(tag, xin, g, gi):
        hm = _norm_fwd("mlp_norm_" + tag, xin, g, tr)
        tn = _pick(ffs, 1024)
        w1 = Mat(gathered(gi, "w1_" + tag, hm)[0], d, ff, "colstack")
        a, act = _matmul("mlp_up_" + tag, Mat(hm, t, d), w1, "nn",
                         [_out(t, ff, BF16), _out(t, ff, BF16)], tm, tn, kd,
                         epilogue=lambda z: (jnp.maximum(z, 0.0), jnp.square(jnp.maximum(z, 0.0))))
        w2 = Mat(gathered(gi + 1, "w2_" + tag, act)[0].reshape(ff, d), ff, d)
        xo, = _matmul("mlp_down_" + tag, Mat(act, t, ff), w2, "nn",
                      [_out(t, d, F32)], tm, _pick(d, 1024), _pick(ffs, 2048),
                      epilogue=lambda z, r: (z + r,), extras=[Mat(xin, t, d)])
        return xo, hm, a, act, w1, w2

    def chip_start(tag, part):
        return _exchange_start("scatter_start_" + tag, _chip_route, 3 * len(part), part, [(3,) + p.shape[1:] for p in part])

    def pair_start(tag, stacked):
        g5 = [g.reshape(N_CHIPS, 2, g.shape[1] // 2, g.shape[2]) for g in stacked]
        return _exchange_start("pair_start_" + tag, _pair_route, N_CHIPS * len(g5), g5,
                               [(N_CHIPS,) + g.shape[2:] for g in g5])

    def pair_finish(tag, started, after):
        g5, from_sib = _exchange_wait("pair_wait_" + tag, _pair_route, started, after)
        return chip_start(tag, [_pair_sum(a, b) for a, b in zip(g5, from_sib)])

    def summed(tag, sc, after):
        part, lands = _exchange_wait("scatter_wait_" + tag, _chip_route, sc, after)
        half = [_chip_sum(p, r) for p, r in zip(part, lands)]
        return _exchange_start("share_start_" + tag, _share_route, len(half), half, [])

    def shared(tag, started, after):
        bufs, _ = _exchange_wait("share_wait_" + tag, _share_route, started, after)
        return [r.reshape(2 * r.shape[1], r.shape[2]) for r in bufs]

    def mlp_bwd(tag, dx, dxb, xin, g, w1, w2, hm, a, act, deps):
        tn = _pick(ffs, 1024)
        hr, hd = ffs // 2, d // 2
        dz, = _matmul("mlp_dact_" + tag, Mat(dxb, t, d), w2, "nt",
                      [_out(t, ff, BF16)], tm, tn, kd,
                      epilogue=lambda z, av: (z * (2.0 * av.astype(F32)),), extras=[Mat(a, t, ff)], deps=deps)

        def half(own):
            c = lax.axis_index("c")
            return c if own else 1 - c

        def act_half(own):
            return Mat(act, t, ff // 2, cmap=lambda cb, bc: (cb // (hr // bc)) * (ffs // bc) + half(own) * (hr // bc)
                       + cb % (hr // bc))

        def hm_half(own):
            return Mat(hm, t, hd, cmap=lambda cb, bc: cb + half(own) * (hd // bc))

        w1_out = lambda: _out(hd, ff, BF16, "colstack", (), (N_CHIPS, hd, ffs))
        theirs2, = _matmul("mlp_dw2_theirs_" + tag, act_half(False), Mat(dxb, t, d), "tn",
                           [_out(ff // 2, d, BF16)], _pick(hr, 1024), _pick(d, 2048), kt)
        theirs1, = _matmul("mlp_dw1_theirs_" + tag, hm_half(False), Mat(dz, t, ff), "tn",
                           [w1_out()], _pick(hd, 2048), tn, kt)
        sent = [theirs1, theirs2.reshape(N_CHIPS, hr, d)]
        started, tok = _exchange_start("pair_start_m" + tag, _slab_route, N_CHIPS * 2, sent, [s.shape for s in sent])
        dhm, = _matmul("mlp_dh_" + tag, Mat(dz, t, ff), w1, "nt",
                       [_out(t, d, F32)], tm, _pick(d, 1024), _pick(ffs, 2048), deps=(tok,))
        dxo, dxob, dg = _norm_bwd("mlp_norm_bwd_" + tag, dhm, xin, g, dx, tr)
        _, (sib1, sib2) = _exchange_wait("pair_wait_m" + tag, _slab_route, started, dxo)
        add = lambda z, s: (z + s.astype(F32),)
        part2, = _matmul("mlp_dw2_mine_" + tag, act_half(True), Mat(dxb, t, d), "tn",
                         [_out(ff // 2, d, BF16)], _pick(hr, 1024), _pick(d, 2048), kt,
                         epilogue=add, extras=[Mat(sib2.reshape(ff // 2, d), ff // 2, d)])
        part1, = _matmul("mlp_dw1_mine_" + tag, hm_half(True), Mat(dz, t, ff), "tn",
                         [w1_out()], _pick(hd, 2048), tn, kt, epilogue=add, extras=[Mat(sib1, hd, ff, "colstack")])
        sc, tok = chip_start("m" + tag, [part1, part2.reshape(N_CHIPS, hr, d)])
        return dxo, dxob, dg, sc, tok

    cq_cb, ckv_cb, kr_cb = 2 * gw // ql, (2 * gw + ql) // kvl, (2 * gw + ql + kvl) // LANES
    qn, kvn = _rowwise("qkv_norm", lambda a, b, ga, gb: (_rms(a, ga), _rms(b, gb)), t // tr,
                       [_rt(proj, tr, ql, cq_cb), _rt(proj, tr, kvl, ckv_cb), _whole(g_q), _whole(g_kv)],
                       [_rt_out(t, ql, BF16, tr), _rt_out(t, kvl, BF16, tr)])
    qfull, = _matmul("q_up", Mat(qn, t, ql), Mat(w_q_all, ql, 2 * hw), "nn", [_out(t, 2 * hw, F32)], tm, _pick(2 * hw, 1024), ql)
    kvall, = _matmul("kv_up", Mat(kvn, t, kvl), Mat(w_kv_all, kvl, 2 * hw), "nn", [_out(t, 2 * hw, BF16)], tm, _pick(2 * hw, 1024), kvl)
    qall, kr = _rope_fwd(qfull, proj, kr_cb, ctab, stab, heads, tr)
    att, lse_row = _attn_fwd(qall, kvall, kr, heads, scale, tr)
    rb = min(2 * LANES, t)
    sgu = _sgu_fwd(proj, g_vn, sgu_w, sgu_b, groups, rb)
    mixed = _rowwise("mix_norm", lambda a, s, ga, gs: jnp.concatenate([_rms(a, ga), _rms(s, gs)], axis=1), t // tr,
                     [_rt(att, tr), _rt(sgu, tr), _whole(g_mla), _whole(g_sgu)], [_rt_out(t, mix, BF16, tr)])[0]
    x1, = _matmul("e_out", Mat(mixed, t, mix), Mat(w_eout, mix, d), "nn", [_out(t, d, F32)], tm, _pick(d, 1024), _pick(mix, 2048),
                  epilogue=lambda z, r: (z + r,), extras=[Mat(xs, t, d)])
    x2, hm0, a0, act0, w1_0, w2_0 = mlp_fwd("0", x1, g_m0, 2)

    w_oin_g, w_oout_g = gathered(4, "o", x2)
    w_oout = w_oout_g.reshape(cd, d)
    h1 = _norm_fwd("o_norm", x2, g_o, tr)
    oin = Mat(_unstack_cols(w_oin_g), d, 3 * cd)
    tn_o = _pick(_gcd(3 * cd // N_CHIPS, cd), 512)
    proj3, = _matmul("o_proj", Mat(h1, t, d), oin, "nn", [_out(t, 3 * cd, F32, "colstack", (), (3, t, cd))],
                     tm, _pick(cd, 1024), kd)
    tc = _pick(cd, 256)
    bz = _conv_fwd(proj3, conv_w, tc)
    x3, = _matmul("o_out", Mat(bz, t, cd), Mat(w_oout, cd, d), "nn", [_out(t, d, F32)], tm, _pick(d, 1024), _pick(cd, 2048),
                  epilogue=lambda z, r: (z + r,), extras=[Mat(x2, t, d)])
    x4, hm1, a1, act1, w1_1, w2_1 = mlp_fwd("1", x3, g_m1, 5)

    def final_fn(xv, gv, tv):
        r = lax.rsqrt(jnp.mean(xv * xv, axis=-1, keepdims=True) + EPS)
        xh = xv * r
        err = xh * gv - tv
        dy = err * (1.0 / d)
        dxh = dy * gv
        dx = r * (dxh - xh * jnp.mean(dxh * xh, axis=-1, keepdims=True))
        sq = jnp.sum(err * err, axis=0, keepdims=True)
        part = sq[:, :LANES]
        for k in range(1, d // LANES):
            part = part + sq[:, k * LANES:(k + 1) * LANES]
        return dx, dx, part, jnp.sum(dy * xh, axis=0, keepdims=True)

    dx4, dx4b, loss_vec, dg_f = _rowwise("loss_final_norm", final_fn, t // tr, [_rt(x4, tr), _whole(g_f), _rt(tgt, tr)],
                                         [_rt_out(t, d, F32, tr), _rt_out(t, d, BF16, tr)],
                                         [jax.ShapeDtypeStruct((1, LANES), F32), jax.ShapeDtypeStruct((1, d), F32)])

    dx3, dx3b, dg_m1, sc_m1, tok = mlp_bwd("1", dx4, dx4b, x3, g_m1, w1_1, w2_1, hm1, a1, act1, ())

    dbz, = _matmul("o_out_dx", Mat(dx3b, t, d), Mat(w_oout, cd, d), "nt", [_out(t, cd, F32)], tm, _pick(cd, 1024), kd,
                   deps=(tok,))
    dw_oout, = _matmul("o_out_dw", Mat(bz, t, cd), Mat(dx3b, t, d), "tn", [_out(cd, d, BF16)], _pick(cd, 1024), _pick(d, 1024), kt)
    dproj3, dconv = _conv_bwd(proj3, conv_w, dbz, tc)
    dp3 = Mat(dproj3, t, 3 * cd, "colstack")
    dw_oin, = _matmul("o_proj_dw", Mat(h1, t, d), dp3, "tn", [_out(d, 3 * cd, BF16, "colstack", (), (N_CHIPS, d, 3 * cd // N_CHIPS))],
                      _pick(d, 2048), tn_o, kt)
    started_o, tok = pair_start("o", [dw_oin, dw_oout.reshape(N_CHIPS, cd // N_CHIPS, d)])
    dh1, = _matmul("o_proj_dx", dp3, oin, "nt", [_out(t, d, F32)], tm, _pick(d, 1024), _pick(cd, 2048), deps=(tok,))
    dx2, dx2b, dg_o = _norm_bwd("o_norm_bwd", dh1, x2, g_o, dx3, tr)
    sc_o, tok = pair_finish("o", started_o, dx2)

    dconv_s = jnp.transpose(dconv[:3].reshape(3, N_CHIPS, cd // N_CHIPS), (1, 0, 2))
    gsmall = jnp.concatenate([jnp.pad(dg_o.reshape(N_CHIPS, 1, d // N_CHIPS), ((0, 0), (0, 15), (0, 0))),
                              jnp.pad(dconv_s, ((0, 0), (0, 13), (0, 0)))], axis=1)
    dx1, dx1b, dg_m0, sc_m0, tok = mlp_bwd("0", dx2, dx2b, x1, g_m0, w1_0, w2_0, hm0, a0, act0, (tok,))

    dmixed, = _matmul("e_out_dx", Mat(dx1b, t, d), Mat(w_eout, mix, d), "nt", [_out(t, mix, F32)], tm, _pick(mix, 1024), kd,
                      deps=(tok,))
    dw_eout, = _matmul("e_out_dw", Mat(mixed, t, mix), Mat(dx1b, t, d), "tn", [_out(mix, d, BF16)], _pick(mix, 1024), _pick(d, 1024), kt)

    def mixb_fn(dm, a, s, ga, gs):
        da, dga = _rms_bwd(dm[:, :hw], a, ga)
        dsg, dgs = _rms_bwd(dm[:, hw:], s, gs)
        prod = da * a
        cols = [jnp.broadcast_to(jnp.sum(prod[:, h * LANES:(h + 1) * LANES], axis=-1, keepdims=True), (tr, LANES))
                for h in range(heads)]
        return da, dsg, jnp.stack([_row_of(c) for c in cols], axis=0), dga, dgs

    da_b, dsgu, delta_row, dg_mla, dg_sgu = _rowwise(
        "mix_norm_bwd", mixb_fn, t // tr, [_rt(dmixed, tr), _rt(att, tr), _rt(sgu, tr), _whole(g_mla), _whole(g_sgu)],
        [_rt_out(t, hw, BF16, tr), _rt_out(t, gw, F32, tr),
         (jax.ShapeDtypeStruct((heads, 8, t), F32), pl.BlockSpec((heads, 8, tr), lambda i: (0, 0, i)))],
        [jax.ShapeDtypeStruct((1, hw), F32), jax.ShapeDtypeStruct((1, gw), F32)])

    dproj, dsgu_w, dsgu_b8, dg_vn = _sgu_bwd(proj, dsgu, g_vn, sgu_w, sgu_b, groups, rb)
    dq1, dq2, dk1, dvv, dkr_h = _attn_bwd(qall, kvall, kr, da_b, lse_row, delta_row, heads, scale, min(2 * tr, t))
    dqfull, dproj = _rope_bwd(dq1, dq2, dkr_h, ctab, stab, heads, tr, dproj, kr_cb)
    dkvall = jnp.concatenate([dk1, dvv], axis=1)
    dw_q, = _matmul("q_up_dw", Mat(qn, t, ql), Mat(dqfull, t, 2 * hw), "tn", [_out(ql, 2 * hw, BF16)], ql, _pick(2 * hw, 1024), kt)
    dqn, = _matmul("q_up_dx", Mat(dqfull, t, 2 * hw), Mat(w_q_all, ql, 2 * hw), "nt", [_out(t, ql, F32)], tm, ql, _pick(2 * hw, 2048))
    dw_kv, = _matmul("kv_up_dw", Mat(kvn, t, kvl), Mat(dkvall, t, 2 * hw), "tn", [_out(kvl, 2 * hw, BF16)], kvl, _pick(2 * hw, 1024), kt)
    dkvn, = _matmul("kv_up_dx", Mat(dkvall, t, 2 * hw), Mat(w_kv_all, kvl, 2 * hw), "nt", [_out(t, kvl, F32)], tm, kvl, _pick(2 * hw, 2048))

    def qkvb_fn(da, db, a, b, ga, gb):
        dxa, dga = _rms_bwd(da, a, ga)
        dxb, dgb = _rms_bwd(db, b, gb)
        return jnp.concatenate([dxa, dxb], axis=1), dga, dgb

    assert (2 * gw) % (ql + kvl) == 0
    into = (jax.ShapeDtypeStruct(dproj.shape, dproj.dtype),
            pl.BlockSpec((tr, ql + kvl), lambda i: (i, 2 * gw // (ql + kvl))))
    dproj, dg_q, dg_kv = _rowwise(
        "qkv_norm_bwd", qkvb_fn, t // tr,
        [_rt(dqn, tr), _rt(dkvn, tr), _rt(proj, tr, ql, cq_cb), _rt(proj, tr, kvl, ckv_cb), _whole(g_q), _whole(g_kv)],
        [into], [jax.ShapeDtypeStruct((1, ql), F32), jax.ShapeDtypeStruct((1, kvl), F32)], deps=(dproj,), fill=(0, 0))
    dw_in, = _matmul("e_proj_dw", Mat(dproj, t, pi), Mat(h0, t, d), "tn", [_out(pi, d, F32)], _pick(pi, 1024), _pick(d, 2048), kt)
    dh0, = _matmul("e_proj_dx", Mat(dproj, t, pi), Mat(w_in_all, d, pi), "nt", [_out(t, d, F32)], tm, _pick(d, 1024), _pick(pi, 4096))
    dx0, _, dg_e = _norm_bwd("e_norm_bwd", dh0, xs, g_e, dx1, tr)

    kr0 = 2 * gw + c2
    gw_in = jnp.concatenate([dw_in[2 * gw:kr0], dw_in[kr0:kr0 + ROPE_HALF], dw_in[kr0 + ROPE:kr0 + ROPE + ROPE_HALF],
                             dw_in[:2 * gw]], axis=0).reshape(N_CHIPS, ei // N_CHIPS, d)
    gq = jnp.concatenate([dw_q[:, :hw].reshape(ql, heads, LANES), _unpad_rope(dw_q[:, hw:].reshape(ql, heads, LANES))], axis=-1)
    gw_uq = _stack_cols(gq.reshape(ql, heads * (LANES + ROPE)))
    gkv = jnp.concatenate([dw_kv[:, :hw].reshape(kvl, heads, LANES), dw_kv[:, hw:].reshape(kvl, heads, LANES)], axis=-1)
    gw_ukv = _stack_cols(gkv.reshape(kvl, heads * 2 * LANES))
    started_e, tok_pair = pair_start("e", [gw_in, gw_uq, gw_ukv, dw_eout.reshape(N_CHIPS, mix // N_CHIPS, d), gsmall])

    small_like = [e_norm_mix, e_q_norm, e_kv_norm, e_v_norm, e_sgu_w, e_sgu_b, e_mla_out_norm, e_sgu_out_norm, mlp_norm, final_norm]
    small_grads = [dg_e, dg_q, dg_kv, dg_vn, dsgu_w, dsgu_b8[:, 0, :], dg_mla, dg_sgu, jnp.concatenate([dg_m0, dg_m1], axis=0), dg_f]
    packed = _pack_small(small_grads)
    n_small = packed.shape[0] + (-packed.shape[0]) % 8
    pad = n_small - packed.shape[0] + 8
    sflat = jnp.concatenate([jnp.pad(packed, ((0, pad - 8), (0, 0))), jnp.pad(loss_vec, ((0, 7), (0, 0)))], axis=0)
    small_started, tok_small = _exchange_start("small_start", _all_route, 7, [sflat], [_spread(sflat)])

    sh_m1, tok = summed("m1", sc_m1, (tok_pair, tok_small))
    sc_e, tok = pair_finish("e", started_e, tok)
    sh_o, tok = summed("o", sc_o, tok)
    sh_m0, tok = summed("m0", sc_m0, tok)
    r_oin, r_oout = shared("o", sh_o, tok)
    late = {"o_w_in": _adamw(o_w_in, [r_oin], m_o_w_in, v_o_w_in),
            "o_w_out": _adamw(o_w_out, [r_oout], m_o_w_out, v_o_w_out)}
    r_w1_1, r_w2_1 = shared("m1", sh_m1, late["o_w_in"][1])
    r_w1_0, r_w2_0 = shared("m0", sh_m0, r_w2_1)
    late["mlp_w1"] = _adamw(mlp_w1, [r_w1_0, r_w1_1], m_mlp_w1, v_mlp_w1)
    sh_e, tok = summed("e", sc_e, late["mlp_w1"][1])
    late["mlp_w2"] = _adamw(mlp_w2, [r_w2_0, r_w2_1], m_mlp_w2, v_mlp_w2, deps=[tok])

    _, (all_small,) = _exchange_wait("small_wait", _all_route, small_started, late["mlp_w2"][1])
    g_small = _sum_devices(all_small)
    loss = 0.5 * jnp.sum(g_small[n_small]) / d

    def padded(arrs):
        return jnp.pad(_pack_small(arrs), ((0, pad), (0, 0)))

    s_m = [m_e_norm_mix, m_e_q_norm, m_e_kv_norm, m_e_v_norm, m_e_sgu_w, m_e_sgu_b, m_e_mla_out_norm, m_e_sgu_out_norm, m_mlp_norm, m_final_norm]
    s_v = [v_e_norm_mix, v_e_q_norm, v_e_kv_norm, v_e_v_norm, v_e_sgu_w, v_e_sgu_b, v_e_mla_out_norm, v_e_sgu_out_norm, v_mlp_norm, v_final_norm]
    s_out = [_unpack_small(o[0], small_like)
             for o in _adamw(padded(small_like)[None], [g_small], padded(s_m)[None], padded(s_v)[None])]

    r_in, r_uq, r_ukv, r_eout, r_small = shared("e", sh_e, (tok, late["mlp_w2"][1]))
    sm = [o[0] for o in _adamw(small_shard[None], [r_small], _small_shard(m_o_norm_mix, m_o_conv_w[0])[None],
                               _small_shard(v_o_norm_mix, v_o_conv_w[0])[None])]
    big = dict(late)
    flip = lambda a: jnp.swapaxes(a, 1, 2)
    big.update({
        "e_w_in": [flip(o) for o in _adamw(flip(e_w_in), [r_in], flip(m_e_w_in), flip(v_e_w_in))],
        "e_w_uq": _adamw(e_w_uq, [r_uq], m_e_w_uq, v_e_w_uq),
        "e_w_ukv": _adamw(e_w_ukv, [r_ukv], m_e_w_ukv, v_e_w_ukv),
        "e_w_out": _adamw(e_w_out, [r_eout], m_e_w_out, v_e_w_out),
    })

    names = ["e_norm_mix", "e_w_in", "e_q_norm", "e_w_uq", "e_kv_norm", "e_w_ukv", "e_v_norm", "e_sgu_w", "e_sgu_b",
             "e_mla_out_norm", "e_sgu_out_norm", "e_w_out", "o_norm_mix", "o_w_in", "o_conv_w", "o_w_out",
             "mlp_norm", "mlp_w1", "mlp_w2", "final_norm"]
    shapes = {"e_w_in": e_w_in.shape, "e_w_uq": e_w_uq.shape, "e_w_ukv": e_w_ukv.shape, "e_w_out": e_w_out.shape,
              "o_w_in": o_w_in.shape, "o_w_out": o_w_out.shape, "mlp_w1": mlp_w1.shape, "mlp_w2": mlp_w2.shape}
    small_names = ["e_norm_mix", "e_q_norm", "e_kv_norm", "e_v_norm", "e_sgu_w", "e_sgu_b", "e_mla_out_norm",
                   "e_sgu_out_norm", "mlp_norm", "final_norm"]

    def leaf(kind, name):
        if name in big:
            return big[name][kind].reshape(shapes[name])
        if name == "o_norm_mix":
            return sm[kind][0:1]
        if name == "o_conv_w":
            return sm[kind][16:19].reshape(o_conv_w.shape)
        return s_out[kind][small_names.index(name)]

    outs = [loss, dx0.reshape(x.shape)]
    for kind in range(4):
        outs += [leaf(kind, nm) for nm in names]
    return tuple(outs)


def _gcd(a, b):
    while b:
        a, b = b, a % b
    return a
```

```python
import jax
import jax.numpy as jnp
from jax import lax
from jax.experimental import pallas as pl
from jax.experimental.pallas import tpu as pltpu

F32 = jnp.float32
BF16 = jnp.bfloat16
MESH = pl.DeviceIdType.MESH

LANES = 128
ROPE = 64
ROPE_HALF = ROPE // 2
ROPE_BASE = 10000.0
EPS = 1e-6
N_CHIPS = 4
VMEM_LIMIT = 48 * 1024 * 1024
NEG = -1e30

ADAM_LR = 0.001
ADAM_B1 = 0.9
ADAM_B2 = 0.999
ADAM_EPS = 1e-08
ADAM_WD = 0.01
ADAM_STEP = 10


def _pick(n, target, step=LANES):
    best = None
    for t in range(step, min(n, target) + 1, step):
        if n % t == 0:
            best = t
    return best if best is not None else n


def _params(sem, vmem=VMEM_LIMIT):
    return pltpu.CompilerParams(dimension_semantics=sem, vmem_limit_bytes=vmem)


class Mat:
    def __init__(self, arr, rows, cols, kind="plain", lead=(), cmap=None, shape=None, dtype=None):
        self.arr, self.rows, self.cols, self.kind, self.lead, self.cmap = arr, rows, cols, kind, tuple(lead), cmap
        self.shape = tuple(arr.shape) if arr is not None else tuple(shape)
        self.dtype = arr.dtype if arr is not None else dtype

    def sds(self):
        return jax.ShapeDtypeStruct(self.shape, self.dtype)

    def spec(self, br, bc, gridmap):
        lead, nl = self.lead, len(self.lead)
        if self.kind == "plain":
            assert self.rows % br == 0 and self.cols % bc == 0, (self.shape, br, bc)
            cmap = self.cmap if self.cmap is not None else (lambda cb, _: cb)
            block = (None,) * nl + (br, bc)

            def phys(rb, cb):
                return lead + (rb, cmap(cb, bc))
        elif self.kind == "colstack":
            cs = self.shape[-1]
            assert cs % bc == 0 and self.rows % br == 0, (self.shape, br, bc)
            q = cs // bc
            block = (None,) * (nl + 1) + (br, bc)

            def phys(rb, cb):
                return (cb // q,) + lead + (rb, cb % q)
        else:
            rs = self.shape[-2]
            assert rs % br == 0 and self.cols % bc == 0, (self.shape, br, bc)
            q = rs // br
            block = (None,) * (nl + 1) + (br, bc)

            def phys(rb, cb):
                return (rb // q,) + lead + (rb % q, cb)

        return pl.BlockSpec(block, lambda *g: phys(*gridmap(*g)))


def _adamw_math(w, g, m, v):
    mn = ADAM_B1 * m + (1.0 - ADAM_B1) * g
    vn = ADAM_B2 * v + (1.0 - ADAM_B2) * jnp.square(g)
    m_hat = mn / (1.0 - ADAM_B1 ** ADAM_STEP)
    v_hat = vn / (1.0 - ADAM_B2 ** ADAM_STEP)
    return -ADAM_LR * (m_hat / (jnp.sqrt(v_hat) + ADAM_EPS) + ADAM_WD * w), mn, vn


def _matmul(name, a, b, mode, outs, tm, tn, tk, epilogue=None, extras=(), deps=()):
    if mode == "nn":
        m, k, n = a.rows, a.cols, b.cols
        a_spec = a.spec(tm, tk, lambda i, j, kk: (i, kk))
        b_spec = b.spec(tk, tn, lambda i, j, kk: (kk, j))
        dims = (((1,), (0,)), ((), ()))
    elif mode == "nt":
        m, k, n = a.rows, a.cols, b.rows
        a_spec = a.spec(tm, tk, lambda i, j, kk: (i, kk))
        b_spec = b.spec(tn, tk, lambda i, j, kk: (j, kk))
        dims = (((1,), (1,)), ((), ()))
    else:
        k, m, n = a.rows, a.cols, b.cols
        a_spec = a.spec(tk, tm, lambda i, j, kk: (kk, i))
        b_spec = b.spec(tk, tn, lambda i, j, kk: (kk, j))
        dims = (((0,), (0,)), ((), ()))
    assert m % tm == 0 and n % tn == 0 and k % tk == 0, (name, m, n, k, tm, tn, tk)
    grid = (m // tm, n // tn, k // tk)
    nk = grid[2]
    n_ex, n_out, n_dep = len(extras), len(outs), len(deps)
    tile = lambda i, j, kk: (i, j)

    def finish(z, ex, out_refs):
        vals = epilogue(z, *[e[...] for e in ex]) if epilogue is not None else (z,)
        for o, v in zip(out_refs, vals):
            o[...] = v.astype(o.dtype)

    def body_single(a_ref, b_ref, *rest):
        finish(lax.dot_general(a_ref[...], b_ref[...], dims, preferred_element_type=F32),
               rest[:n_ex], rest[n_ex + n_dep:n_ex + n_dep + n_out])

    def body_acc(a_ref, b_ref, *rest):
        acc = rest[-1]
        kk = pl.program_id(2)

        @pl.when(kk == 0)
        def _():
            acc[...] = jnp.zeros_like(acc)

        acc[...] += lax.dot_general(a_ref[...], b_ref[...], dims, preferred_element_type=F32)

        @pl.when(kk == nk - 1)
        def _():
            finish(acc[...], rest[:n_ex], rest[n_ex + n_dep:n_ex + n_dep + n_out])

    res = pl.pallas_call(
        body_single if nk == 1 else body_acc, name=name, grid=grid,
        in_specs=[a_spec, b_spec] + [e.spec(tm, tn, tile) for e in extras]
        + [pl.BlockSpec(memory_space=pl.ANY) for _ in deps],
        out_specs=[o.spec(tm, tn, tile) for o in outs],
        out_shape=[o.sds() for o in outs],
        scratch_shapes=[] if nk == 1 else [pltpu.VMEM((tm, tn), F32)],
        compiler_params=_params(("parallel", "parallel", "arbitrary")),
    )(a.arr, b.arr, *[e.arr for e in extras], *deps)
    return res


def _out(rows, cols, dtype, kind="plain", lead=(), shape=None):
    return Mat(None, rows, cols, kind, lead, shape=shape if shape is not None else (rows, cols), dtype=dtype)


def _rt(arr, tr, width=None, cb=0):
    width = arr.shape[1] if width is None else width
    return arr, pl.BlockSpec((tr, width), lambda i: (i, cb))


def _whole(arr):
    nd = arr.ndim
    return arr, pl.BlockSpec(arr.shape, lambda i: (0,) * nd)


def _rowwise(name, fn, n_steps, ins, outs, accs=(), deps=(), fill=None):
    n_in, n_out, n_acc, n_dep = len(ins), len(outs), len(accs), len(deps)

    def body(*refs):
        vals = fn(*[r[...] for r in refs[:n_in]])
        if not isinstance(vals, (tuple, list)):
            vals = (vals,)
        for ref, v in zip(refs[n_in + n_dep:n_in + n_dep + n_out], vals[:n_out]):
            ref[...] = v.astype(ref.dtype)
        if n_acc:
            acc_refs = refs[n_in + n_dep + n_out:]

            @pl.when(pl.program_id(0) == 0)
            def _():
                for ref in acc_refs:
                    ref[...] = jnp.zeros_like(ref)

            for ref, v in zip(acc_refs, vals[n_out:]):
                ref[...] += v

    acc_specs = [pl.BlockSpec(s.shape, lambda i, nd=len(s.shape): (0,) * nd) for s in accs]
    res = pl.pallas_call(
        body, name=name, grid=(n_steps,),
        in_specs=[s for _, s in ins] + [pl.BlockSpec(memory_space=pl.ANY) for _ in deps],
        out_specs=[s for _, s in outs] + acc_specs,
        out_shape=[o for o, _ in outs] + list(accs),
        input_output_aliases={} if fill is None else {n_in + fill[0]: fill[1]},
        compiler_params=_params(("arbitrary",) if n_acc else ("parallel",)),
    )(*[a for a, _ in ins], *deps)
    return res


def _rt_out(t, width, dtype, tr):
    return jax.ShapeDtypeStruct((t, width), dtype), pl.BlockSpec((tr, width), lambda i: (i, 0))


def _rms(x, g):
    r = lax.rsqrt(jnp.mean(x * x, axis=-1, keepdims=True) + EPS)
    return x * r * g


def _rms_bwd(dy, x, g):
    r = lax.rsqrt(jnp.mean(x * x, axis=-1, keepdims=True) + EPS)
    xh = x * r
    dxh = dy * g
    dx = r * (dxh - xh * jnp.mean(dxh * xh, axis=-1, keepdims=True))
    dg = jnp.sum(dy * xh, axis=0, keepdims=True)
    return dx, dg


def _gelu_and_grad(x):
    k = 0.7978845608028654
    x2 = x * x
    th = jnp.tanh(k * (x + 0.044715 * (x2 * x)))
    half = 0.5 * (1.0 + th)
    return x * half, half + 0.5 * x * (1.0 - th * th) * (k * (1.0 + 3.0 * 0.044715 * x2))


def _gelu(x):
    return _gelu_and_grad(x)[0]


def _gelu_grad(x):
    return _gelu_and_grad(x)[1]


def _norm_fwd(name, x, g, tr):
    t, d = x.shape
    return _rowwise(name, lambda xv, gv: _rms(xv, gv), t // tr, [_rt(x, tr), _whole(g)], [_rt_out(t, d, BF16, tr)])[0]


def _norm_bwd(name, dh, x, g, dres, tr):
    t, d = x.shape

    def fn(dhv, xv, gv, drv):
        dx, dg = _rms_bwd(dhv, xv, gv)
        dx = dx + drv
        return dx, dx, dg

    return _rowwise(name, fn, t // tr, [_rt(dh, tr), _rt(x, tr), _whole(g), _rt(dres, tr)],
                    [_rt_out(t, d, F32, tr), _rt_out(t, d, BF16, tr)], [jax.ShapeDtypeStruct((1, d), F32)])


def _rope_tables(posf, invf, cmask, smask, tr):
    t = posf.shape[0]

    def fn(p, f, cm, sm):
        ang = p * f
        return jnp.cos(ang) * cm, jnp.sin(ang) * sm

    return _rowwise("rope_tables", fn, t // tr, [_rt(posf, tr), _whole(invf), _whole(cmask), _whole(smask)],
                    [_rt_out(t, LANES, F32, tr), _rt_out(t, LANES, F32, tr)])


def _rot(v, c, s):
    return v * c + pltpu.roll(v, ROPE, axis=1) * s


def _rot_bwd(dv, c, s):
    return dv * c + pltpu.roll(dv * s, ROPE, axis=1)


def _rope_fwd(qfull, proj, kr_cb, ctab, stab, heads, tr):
    t = qfull.shape[0]
    hw = heads * LANES

    def fn(q, kr, c, s):
        parts = [q[:, :hw]] + [_rot(q[:, hw + h * LANES: hw + (h + 1) * LANES], c, s) for h in range(heads)]
        return jnp.concatenate(parts, axis=1), _rot(kr, c, s)

    return _rowwise("rope_fwd", fn, t // tr, [_rt(qfull, tr), _rt(proj, tr, LANES, kr_cb), _rt(ctab, tr), _rt(stab, tr)],
                    [_rt_out(t, 2 * hw, BF16, tr), _rt_out(t, LANES, BF16, tr)])


def _rope_bwd(dq1, dq2, dkr_h, ctab, stab, heads, tr, dproj, kr_cb):
    t = dq1.shape[0]
    hw = heads * LANES

    def fn(a, b, dk, c, s):
        parts = [a] + [_rot_bwd(b[:, h * LANES:(h + 1) * LANES], c, s) for h in range(heads)]
        dks = dk[0]
        for h in range(1, heads):
            dks = dks + dk[h]
        return jnp.concatenate(parts, axis=1), _rot_bwd(dks, c, s)

    dk_spec = pl.BlockSpec((heads, tr, LANES), lambda i: (0, i, 0))
    into = (jax.ShapeDtypeStruct(dproj.shape, dproj.dtype), pl.BlockSpec((tr, LANES), lambda i: (i, kr_cb)))
    return _rowwise("rope_bwd", fn, t // tr, [_rt(dq1, tr), _rt(dq2, tr), (dkr_h, dk_spec), _rt(ctab, tr), _rt(stab, tr)],
                    [_rt_out(t, 2 * hw, BF16, tr), into], deps=(dproj,), fill=(0, 1))


def _dot_nt(a, b):
    return lax.dot_general(a, b, (((1,), (1,)), ((), ())), preferred_element_type=F32)


def _dot_tn(a, b):
    return lax.dot_general(a, b, (((0,), (0,)), ((), ())), preferred_element_type=F32)


def _dot(a, b):
    return jnp.dot(a, b, preferred_element_type=F32)


def _ranges(n_blocks):
    n_var = min(4, n_blocks)
    assert n_blocks % n_var == 0
    return n_var, n_blocks // n_var


def _row_of(col):
    return col.T[:8, :]


def _attn_fwd(qall, kvall, kr, heads, scale, tq):
    t = qall.shape[0]
    nq = t // tq
    n_var, per = _ranges(nq)

    def body(qn_ref, qr_ref, kn_ref, v_ref, kr_ref, o_ref, lser_ref):
        i = pl.program_id(1)
        for var in range(n_var):
            kv = (var + 1) * per * tq

            @pl.when(jnp.logical_and(i >= var * per, i < (var + 1) * per))
            def _(kv=kv):
                s = _dot_nt(jnp.concatenate([qn_ref[...], qr_ref[...]], axis=1),
                            jnp.concatenate([kn_ref[:kv, :], kr_ref[:kv, :]], axis=1)) * scale
                rows = i * tq + lax.broadcasted_iota(jnp.int32, (tq, kv), 0)
                cols = lax.broadcasted_iota(jnp.int32, (tq, kv), 1)
                s = jnp.where(cols <= rows, s, NEG)
                m = jnp.max(s, axis=-1, keepdims=True)
                p = jnp.exp(s - m)
                l = jnp.sum(p, axis=-1, keepdims=True)
                o_ref[...] = _dot(p.astype(BF16), v_ref[:kv, :]) / l
                lser_ref[...] = _row_of(jnp.broadcast_to(m + jnp.log(l), (tq, LANES)))

    return pl.pallas_call(
        body, name="attn_fwd", grid=(heads, nq),
        in_specs=[pl.BlockSpec((tq, LANES), lambda h, i: (i, h)),
                  pl.BlockSpec((tq, LANES), lambda h, i: (i, heads + h)),
                  pl.BlockSpec((t, LANES), lambda h, i: (0, h)),
                  pl.BlockSpec((t, LANES), lambda h, i: (0, heads + h)),
                  pl.BlockSpec((t, LANES), lambda h, i: (0, 0))],
        out_specs=[pl.BlockSpec((tq, LANES), lambda h, i: (i, h)),
                   pl.BlockSpec((None, 8, tq), lambda h, i: (h, 0, i))],
        out_shape=[jax.ShapeDtypeStruct((t, heads * LANES), F32), jax.ShapeDtypeStruct((heads, 8, t), F32)],
        compiler_params=_params(("parallel", "parallel")),
    )(qall, qall, kvall, kvall, kr)


def _attn_bwd(qall, kvall, kr, do, lse_row, delta_row, heads, scale, tk):
    t = qall.shape[0]
    nk = t // tk
    n_var, per = _ranges(nk)

    def body(qn_ref, qr_ref, kn_ref, v_ref, kr_ref, do_ref, lse_ref, dl_ref, dq1_ref, dq2_ref, dk_ref, dv_ref, dkr_ref):
        j = pl.program_id(1)

        @pl.when(j == 0)
        def _():
            dq1_ref[...] = jnp.zeros_like(dq1_ref)
            dq2_ref[...] = jnp.zeros_like(dq2_ref)

        for var in range(n_var):
            q0 = var * per * tk
            nq = t - q0

            @pl.when(jnp.logical_and(j >= var * per, j < (var + 1) * per))
            def _(q0=q0, nq=nq):
                qn, qr, do_v = qn_ref[q0:, :], qr_ref[q0:, :], do_ref[q0:, :]
                k1, k2 = kn_ref[...], kr_ref[...]
                qcat, kcat = jnp.concatenate([qn, qr], axis=1), jnp.concatenate([k1, k2], axis=1)
                st = _dot_nt(kcat, qcat) * scale
                keys = j * tk + lax.broadcasted_iota(jnp.int32, (tk, nq), 0)
                queries = q0 + lax.broadcasted_iota(jnp.int32, (tk, nq), 1)
                pt = jnp.where(keys <= queries, jnp.exp(st - lse_ref[0:1, q0:]), 0.0)
                dpt = _dot_nt(v_ref[...], do_v)
                dst = (pt * (dpt - dl_ref[0:1, q0:]) * scale).astype(BF16)
                dv_ref[...] = _dot(pt.astype(BF16), do_v).astype(dv_ref.dtype)
                dkc = _dot(dst, qcat)
                dk_ref[...] = dkc[:, :LANES].astype(dk_ref.dtype)
                dkr_ref[...] = dkc[:, LANES:]
                dqc = _dot_tn(dst, kcat)
                dq1_ref[q0:, :] += dqc[:, :LANES]
                dq2_ref[q0:, :] += dqc[:, LANES:]

    kblk = lambda off: pl.BlockSpec((tk, LANES), lambda h, j: (j, off + h))
    full = lambda off: pl.BlockSpec((t, LANES), lambda h, j: (0, off + h))
    stat = pl.BlockSpec((None, 8, t), lambda h, j: (h, 0, 0))
    return pl.pallas_call(
        body, name="attn_bwd", grid=(heads, nk),
        in_specs=[full(0), full(heads), kblk(0), kblk(heads), pl.BlockSpec((tk, LANES), lambda h, j: (j, 0)),
                  full(0), stat, stat],
        out_specs=[full(0), full(0), kblk(0), kblk(0), pl.BlockSpec((None, tk, LANES), lambda h, j: (h, j, 0))],
        out_shape=[jax.ShapeDtypeStruct((t, heads * LANES), F32)] * 2 + [jax.ShapeDtypeStruct((t, heads * LANES), BF16)] * 2
        + [jax.ShapeDtypeStruct((heads, t, LANES), F32)],
        compiler_params=_params(("parallel", "arbitrary")),
    )(qall, qall, kvall, kvall, kr, do, lse_row, delta_row)


def _tril():
    return lax.broadcasted_iota(jnp.int32, (LANES, LANES), 0) >= lax.broadcasted_iota(jnp.int32, (LANES, LANES), 1)


def _group_norm(vg):
    mu = jnp.mean(vg, axis=-1, keepdims=True)
    vc = vg - mu
    rs = lax.rsqrt(jnp.mean(vc * vc, axis=-1, keepdims=True) + EPS)
    return vc * rs, rs


def _sgu_fwd(proj, gain, w, bias, groups, rb):
    t = proj.shape[0]
    gw = groups * LANES
    cpb = rb // LANES

    def body(u_ref, v_ref, gain_ref, w_ref, b_ref, s_ref):
        tril = _tril()
        for g in range(groups):
            wt = jnp.where(tril, w_ref[g], 0.0).astype(BF16)
            cols = slice(g * LANES, (g + 1) * LANES)
            for ci in range(cpb):
                rows = slice(ci * LANES, (ci + 1) * LANES)
                ug = _gelu(u_ref[rows, cols])
                vh, _ = _group_norm(_gelu(v_ref[rows, cols]))
                vn = vh * gain_ref[:, cols]
                y = _dot(wt, vn.astype(BF16)) + b_ref[g]
                s_ref[rows, cols] = ug * y

    return pl.pallas_call(
        body, name="sgu_fwd", grid=(t // rb,),
        in_specs=[pl.BlockSpec((rb, gw), lambda i: (i, 0)), pl.BlockSpec((rb, gw), lambda i: (i, 1)),
                  pl.BlockSpec((1, gw), lambda i: (0, 0)),
                  pl.BlockSpec((groups, LANES, LANES), lambda i: (0, 0, 0)),
                  pl.BlockSpec((groups, LANES, LANES), lambda i: (0, 0, 0))],
        out_specs=pl.BlockSpec((rb, gw), lambda i: (i, 0)),
        out_shape=jax.ShapeDtypeStruct((t, gw), F32),
        compiler_params=_params(("parallel",)),
    )(proj, proj, gain, w, bias)


def _sgu_bwd(proj, ds, gain, w, bias, groups, rb):
    t, width = proj.shape
    gw = groups * LANES
    cpb = rb // LANES
    n_steps = t // rb

    def body(u_ref, v_ref, ds_ref, gain_ref, w_ref, b_ref, dp_ref, dw_ref, db_ref, dg_ref, dy_acc):
        du_ref, dv_ref = dp_ref.at[:, :gw], dp_ref.at[:, gw:]
        step = pl.program_id(0)

        @pl.when(step == 0)
        def _():
            dw_ref[...] = jnp.zeros_like(dw_ref)
            dy_acc[...] = jnp.zeros_like(dy_acc)
            dg_ref[...] = jnp.zeros_like(dg_ref)

        tril = _tril()
        for g in range(groups):
            wt = jnp.where(tril, w_ref[g], 0.0).astype(BF16)
            cols = slice(g * LANES, (g + 1) * LANES)
            gain_g = gain_ref[:, cols]
            for ci in range(cpb):
                rows = slice(ci * LANES, (ci + 1) * LANES)
                u_raw, v_raw, ds_v = u_ref[rows, cols], v_ref[rows, cols], ds_ref[rows, cols]
                ug, ug_grad = _gelu_and_grad(u_raw)
                vg, vg_grad = _gelu_and_grad(v_raw)
                vh, rs = _group_norm(vg)
                vn = (vh * gain_g).astype(BF16)
                y = _dot(wt, vn) + b_ref[g]
                dy = ds_v * ug
                dyb = dy.astype(BF16)
                du_ref[rows, cols] = (ds_v * y * ug_grad).astype(du_ref.dtype)
                dy_acc[g] += dy
                dw_ref[g] += _dot_nt(dyb, vn)
                dvn = _dot_tn(wt, dyb)
                dg_ref[:, cols] += jnp.sum(dvn * vh, axis=0, keepdims=True)
                dvh = dvn * gain_g
                dvg = rs * (dvh - jnp.mean(dvh, axis=-1, keepdims=True)
                            - vh * jnp.mean(dvh * vh, axis=-1, keepdims=True))
                dv_ref[rows, cols] = (dvg * vg_grad).astype(dv_ref.dtype)

        @pl.when(step == n_steps - 1)
        def _():
            ones = jnp.ones((8, LANES), F32)
            for g in range(groups):
                dw_ref[g] = jnp.where(tril, dw_ref[g], 0.0)
                db_ref[g] = lax.dot_general(ones, dy_acc[g], (((1,), (1,)), ((), ())),
                                            precision=lax.Precision.HIGHEST, preferred_element_type=F32)

    blk = lambda cb: pl.BlockSpec((rb, gw), lambda i: (i, cb))
    whole3 = pl.BlockSpec((groups, LANES, LANES), lambda i: (0, 0, 0))
    return pl.pallas_call(
        body, name="sgu_bwd", grid=(n_steps,),
        in_specs=[blk(0), blk(1), blk(0), pl.BlockSpec((1, gw), lambda i: (0, 0)), whole3, whole3],
        out_specs=[pl.BlockSpec((rb, 2 * gw), lambda i: (i, 0)), whole3,
                   pl.BlockSpec((groups, 8, LANES), lambda i: (0, 0, 0)), pl.BlockSpec((1, gw), lambda i: (0, 0))],
        out_shape=[jax.ShapeDtypeStruct((t, width), BF16),
                   jax.ShapeDtypeStruct((groups, LANES, LANES), F32), jax.ShapeDtypeStruct((groups, 8, LANES), F32),
                   jax.ShapeDtypeStruct((1, gw), F32)],
        scratch_shapes=[pltpu.VMEM((groups, LANES, LANES), F32)],
        compiler_params=_params(("arbitrary",)),
    )(proj, proj, ds, gain, w, bias)


def _shift_down(z, s):
    rows = lax.broadcasted_iota(jnp.int32, z.shape, 0)
    return jnp.where(rows >= s, pltpu.roll(z, s, axis=0), 0.0)


def _shift_up(z, s):
    n = z.shape[0]
    rows = lax.broadcasted_iota(jnp.int32, z.shape, 0)
    return jnp.where(rows < n - s, pltpu.roll(z, n - s, axis=0), 0.0)


def _conv_fwd(proj3, cw, tc):
    _, t, cd = proj3.shape

    def body(p_ref, w_ref, o_ref):
        z = p_ref[1] * p_ref[2]
        w = w_ref[...]
        zc = w[2:3] * z + w[1:2] * _shift_down(z, 1) + w[0:1] * _shift_down(z, 2)
        o_ref[...] = (p_ref[0] * zc).astype(o_ref.dtype)

    return pl.pallas_call(
        body, name="conv_fwd", grid=(cd // tc,),
        in_specs=[pl.BlockSpec((3, t, tc), lambda j: (0, 0, j)), pl.BlockSpec((8, tc), lambda j: (0, j))],
        out_specs=pl.BlockSpec((t, tc), lambda j: (0, j)),
        out_shape=jax.ShapeDtypeStruct((t, cd), BF16),
        compiler_params=_params(("parallel",)),
    )(proj3, cw)


def _conv_bwd(proj3, cw, dbz, tc):
    _, t, cd = proj3.shape

    def body(p_ref, w_ref, d_ref, o_ref, dw_ref):
        b, c, xin = p_ref[0], p_ref[1], p_ref[2]
        w = w_ref[...]
        z = c * xin
        z1, z2 = _shift_down(z, 1), _shift_down(z, 2)
        zc = w[2:3] * z + w[1:2] * z1 + w[0:1] * z2
        d = d_ref[...]
        dzc = d * b
        dz = w[2:3] * dzc + w[1:2] * _shift_up(dzc, 1) + w[0:1] * _shift_up(dzc, 2)
        o_ref[0] = (d * zc).astype(o_ref.dtype)
        o_ref[1] = (dz * xin).astype(o_ref.dtype)
        o_ref[2] = (dz * c).astype(o_ref.dtype)
        row = lax.broadcasted_iota(jnp.int32, (8, tc), 0)
        dw0 = jnp.sum(dzc * z2, axis=0, keepdims=True)
        dw1 = jnp.sum(dzc * z1, axis=0, keepdims=True)
        dw2 = jnp.sum(dzc * z, axis=0, keepdims=True)
        dw_ref[...] = jnp.where(row == 0, dw0, 0.0) + jnp.where(row == 1, dw1, 0.0) + jnp.where(row == 2, dw2, 0.0)

    return pl.pallas_call(
        body, name="conv_bwd", grid=(cd // tc,),
        in_specs=[pl.BlockSpec((3, t, tc), lambda j: (0, 0, j)), pl.BlockSpec((8, tc), lambda j: (0, j)),
                  pl.BlockSpec((t, tc), lambda j: (0, j))],
        out_specs=[pl.BlockSpec((3, t, tc), lambda j: (0, 0, j)), pl.BlockSpec((8, tc), lambda j: (0, j))],
        out_shape=[jax.ShapeDtypeStruct((3, t, cd), BF16), jax.ShapeDtypeStruct((8, cd), F32)],
        compiler_params=_params(("parallel",)),
    )(proj3, cw, dbz)


def _place():
    x, y, c = lax.axis_index("x"), lax.axis_index("y"), lax.axis_index("c")
    chips = [(1 - x, y), (x, 1 - y), (1 - x, 1 - y)]
    return x, y, c, chips


def _any_specs(n):
    return [pl.BlockSpec(memory_space=pl.ANY) for _ in range(n)]


HBM_SPEC = pl.BlockSpec(memory_space=pltpu.HBM)
SEM_SPEC = pl.BlockSpec(memory_space=pltpu.SEMAPHORE)
ORDERED_EFFECT = pltpu.SideEffectType.DATAFLOW_SIDE_EFFECTING


def _in_hbm(a):
    return pltpu.with_memory_space_constraint(a, pltpu.HBM)


def _token():
    return jax.ShapeDtypeStruct((8, LANES), F32), pl.BlockSpec(memory_space=pltpu.VMEM)


def _gather_start(name, groups):
    sizes = [len(g) for g in groups]
    flat = [b for g in groups for b in g]
    n, ng = len(flat), len(groups)

    def body(*refs):
        ins, sems, token = refs[:n], refs[n:n + 2 * ng], refs[-1]
        x, y, c, chips = _place()
        me = 2 * x + y
        i = 0
        for gi, size in enumerate(sizes):
            for j in range(size):
                blk = ins[i].at[me, c]
                for k, chip in enumerate(chips):
                    pltpu.make_async_remote_copy(src_ref=blk, dst_ref=blk, send_sem=sems[2 * gi].at[3 * j + k],
                                                 recv_sem=sems[2 * gi + 1].at[3 * j + k],
                                                 device_id=(*chip, c), device_id_type=MESH).start()
                i += 1
        token[...] = jnp.zeros_like(token)

    tok_shape, tok_spec = _token()
    res = pl.pallas_call(
        body, name=name,
        in_specs=[HBM_SPEC] * n,
        out_specs=[SEM_SPEC] * (2 * ng) + [HBM_SPEC] * n + [tok_spec],
        out_shape=[pltpu.SemaphoreType.DMA((3 * size,)) for size in sizes for _ in (0, 1)]
        + [pltpu.HBM(b.shape, b.dtype) for b in flat] + [tok_shape],
        input_output_aliases={i: 2 * ng + i for i in range(n)},
        compiler_params=pltpu.CompilerParams(has_side_effects=ORDERED_EFFECT),
    )(*[_in_hbm(b) for b in flat])
    out, i = [], 2 * ng
    for gi, size in enumerate(sizes):
        out.append((res[2 * gi], res[2 * gi + 1], list(res[i:i + size])))
        i += size
    return out, res[-1]


def _gather_wait(tag, send, recv, bufs, after):
    n = len(bufs)
    after = tuple(after) if isinstance(after, (tuple, list)) else (after,)

    def body(*refs):
        ins, send_ref, recv_ref = refs[:n], refs[n], refs[n + 1]
        x, y, c, chips = _place()
        me = 2 * x + y
        for j in range(n):
            for k, (px, py) in enumerate(chips):
                cp = pltpu.make_async_remote_copy(src_ref=ins[j].at[me, c], dst_ref=ins[j].at[2 * px + py, c],
                                                  send_sem=send_ref.at[3 * j + k], recv_sem=recv_ref.at[3 * j + k],
                                                  device_id=(px, py, c), device_id_type=MESH)
                cp.wait_send()
                cp.wait_recv()

    return pl.pallas_call(
        body, name="gather_wait_" + tag,
        in_specs=[HBM_SPEC] * n + [SEM_SPEC, SEM_SPEC] + _any_specs(len(after)),
        out_specs=[HBM_SPEC] * n,
        out_shape=[pltpu.HBM(b.shape, b.dtype) for b in bufs],
        input_output_aliases={i: i for i in range(n)},
        compiler_params=pltpu.CompilerParams(has_side_effects=ORDERED_EFFECT),
    )(*bufs, send, recv, *after)


def _gather_forward(tag, bufs):
    n = len(bufs)

    def body(*refs):
        ins, outs = refs[:n], refs[n:2 * n]
        send, recv = refs[2 * n:]
        x, y, c, chips = _place()
        sib = (x, y, 1 - c)

        def cp(i, k, slot, half):
            return pltpu.make_async_remote_copy(src_ref=ins[i].at[slot, half], dst_ref=outs[i].at[slot, half],
                                                send_sem=send.at[3 * i + k], recv_sem=recv.at[3 * i + k],
                                                device_id=sib, device_id_type=MESH)

        cps = [cp(i, k, 2 * px + py, c) for i in range(n) for k, (px, py) in enumerate(chips)]
        for d in cps:
            d.start()
        for i in range(n):
            for k, (px, py) in enumerate(chips):
                cp(i, k, 2 * px + py, 1 - c).wait_recv()
        for d in cps:
            d.wait_send()

    return pl.pallas_call(
        body, name="gather_forward_" + tag,
        in_specs=_any_specs(n), out_specs=_any_specs(n),
        out_shape=[jax.ShapeDtypeStruct(b.shape, b.dtype) for b in bufs],
        scratch_shapes=[pltpu.SemaphoreType.DMA((3 * n,))] * 2,
        input_output_aliases={i: i for i in range(n)},
        compiler_params=pltpu.CompilerParams(has_side_effects=True),
    )(*bufs)


def _pair_route(srcs, zones):
    x, y, c, _ = _place()
    return [(srcs[i].at[j, 1 - c], zones[i].at[j], (x, y, 1 - c)) for i in range(len(srcs)) for j in range(N_CHIPS)]


def _slab_route(srcs, zones):
    x, y, c, _ = _place()
    return [(srcs[i].at[j], zones[i].at[j], (x, y, 1 - c)) for i in range(len(srcs)) for j in range(N_CHIPS)]


def _chip_route(srcs, zones):
    x, y, c, chips = _place()
    return [(srcs[i].at[2 * px + py], zones[i].at[k], (px, py, c)) for i in range(len(srcs)) for k, (px, py) in enumerate(chips)]


def _all_route(srcs, zones):
    x, y, c, _ = _place()
    flips = [(fx, fy, fc) for fx in (0, 1) for fy in (0, 1) for fc in (0, 1)][1:]
    return [(srcs[0], zones[0].at[4 * x + 2 * y + c], (x + fx - 2 * x * fx, y + fy - 2 * y * fy, c + fc - 2 * c * fc))
            for fx, fy, fc in flips]


def _share_route(srcs, zones):
    x, y, c, _ = _place()
    return [(s.at[c], s.at[c], (x, y, 1 - c)) for s in srcs]


def _exchange_start(name, route, n_copies, srcs, zones):
    n, nz = len(srcs), len(zones)
    lands = [lax.empty(z, a.dtype) if isinstance(z, tuple) else z for z, a in zip(zones, srcs)]

    def body(*refs):
        ins, zone_refs, send, recv, token = refs[:n], refs[n:n + nz], refs[n + nz], refs[n + nz + 1], refs[-1]
        for k, (src, dst, dev) in enumerate(route(ins, zone_refs)):
            pltpu.make_async_remote_copy(src_ref=src, dst_ref=dst, send_sem=send.at[k], recv_sem=recv.at[k],
                                         device_id=dev, device_id_type=MESH).start()
        token[...] = jnp.zeros_like(token)

    tok_shape, tok_spec = _token()
    res = pl.pallas_call(
        body, name=name,
        in_specs=[HBM_SPEC] * (n + nz),
        out_specs=[SEM_SPEC, SEM_SPEC] + [HBM_SPEC] * (n + nz) + [tok_spec],
        out_shape=[pltpu.SemaphoreType.DMA((n_copies,))] * 2 + [pltpu.HBM(a.shape, a.dtype) for a in srcs + lands]
        + [tok_shape],
        input_output_aliases={i: 2 + i for i in range(n + nz)},
        compiler_params=pltpu.CompilerParams(has_side_effects=ORDERED_EFFECT),
    )(*[_in_hbm(a) for a in srcs + lands])
    return (res[0], res[1], list(res[2:2 + n]), list(res[2 + n:2 + n + nz])), res[-1]


def _exchange_wait(name, route, started, after):
    send, recv, srcs, lands = started
    n, nz = len(srcs), len(lands)
    after = tuple(after) if isinstance(after, (tuple, list)) else (after,)

    def body(*refs):
        ins, zone_refs, send_ref, recv_ref = refs[:n], refs[n:n + nz], refs[n + nz], refs[n + nz + 1]
        for k, (src, dst, dev) in enumerate(route(ins, zone_refs)):
            cp = pltpu.make_async_remote_copy(src_ref=src, dst_ref=dst, send_sem=send_ref.at[k], recv_sem=recv_ref.at[k],
                                              device_id=dev, device_id_type=MESH)
            cp.wait_send()
            cp.wait_recv()

    res = pl.pallas_call(
        body, name=name,
        in_specs=[HBM_SPEC] * (n + nz) + [SEM_SPEC, SEM_SPEC] + _any_specs(len(after)),
        out_specs=[HBM_SPEC] * (n + nz),
        out_shape=[pltpu.HBM(a.shape, a.dtype) for a in srcs + lands],
        input_output_aliases={i: i for i in range(n + nz)},
        compiler_params=pltpu.CompilerParams(has_side_effects=ORDERED_EFFECT),
    )(*srcs, *lands, send, recv, *after)
    return list(res[:n]), list(res[n:])


def _spread(v):
    rows, cols = v.shape
    tr = _row_tile(rows, cols, budget=256 * 1024)

    def body(v_ref, o_ref):
        o_ref[...] = jnp.broadcast_to(v_ref[...][None], o_ref.shape)

    return pl.pallas_call(body, name="spread_small_grads", grid=(rows // tr,),
                          in_specs=[pl.BlockSpec((tr, cols), lambda r: (r, 0))],
                          out_specs=pl.BlockSpec((8, tr, cols), lambda r: (0, r, 0)),
                          out_shape=jax.ShapeDtypeStruct((8, rows, cols), v.dtype),
                          compiler_params=_params(("parallel",)))(v)


def _row_tile(rows, cols, itemsize=4, budget=2 * 1024 * 1024, step=8):
    best = None
    for t in range(step, rows + 1, step):
        if rows % t == 0 and t * cols * itemsize <= budget:
            best = t
    return best if best is not None else rows


def _my_chip():
    return 2 * lax.axis_index("x") + lax.axis_index("y")


def _for_pieces(rows, cols, fn):
    pr, pc = 16, min(cols, 1024)
    if rows % pr or cols % pc:
        fn(slice(None), slice(None))
        return

    def step(i, carry):
        r = pl.ds(pl.multiple_of(i * pr, pr), pr)
        for c0 in range(0, cols, pc):
            fn(r, slice(c0, c0 + pc))
        return carry

    lax.fori_loop(0, rows // pr, step, 0)


def _pair_sum(g5, gsib):
    _, _, rh, cols = g5.shape
    tr = _row_tile(rh, cols, step=16)

    def body(a_ref, b_ref, o_ref):
        def piece(r, c):
            o_ref[r, c] = (a_ref[r, c].astype(F32) + b_ref[r, c].astype(F32)).astype(o_ref.dtype)
        _for_pieces(tr, cols, piece)

    return pl.pallas_call(body, name="grad_pair_sum", grid=(N_CHIPS, rh // tr),
                          in_specs=[pl.BlockSpec((None, None, tr, cols), lambda j, r: (j, lax.axis_index("c"), r, 0)),
                                    pl.BlockSpec((None, tr, cols), lambda j, r: (j, r, 0))],
                          out_specs=pl.BlockSpec((None, tr, cols), lambda j, r: (j, r, 0)),
                          out_shape=jax.ShapeDtypeStruct((N_CHIPS, rh, cols), BF16),
                          compiler_params=_params(("parallel", "parallel")))(g5, gsib)


def _chip_sum(part, recv):
    _, rh, cols = part.shape
    tr = _row_tile(rh, cols, budget=1024 * 1024, step=16)

    def body(a_ref, b_ref, o_ref):
        def piece(r, c):
            acc = a_ref[r, c].astype(F32)
            for k in range(3):
                acc = acc + b_ref[k, r, c].astype(F32)
            o_ref[r, c] = acc
        _for_pieces(tr, cols, piece)

    return pl.pallas_call(body, name="grad_chip_sum", grid=(rh // tr,),
                          in_specs=[pl.BlockSpec((None, tr, cols), lambda r: (_my_chip(), r, 0)),
                                    pl.BlockSpec((3, tr, cols), lambda r: (0, r, 0))],
                          out_specs=pl.BlockSpec((None, tr, cols), lambda r: (lax.axis_index("c"), r, 0)),
                          out_shape=jax.ShapeDtypeStruct((2, rh, cols), F32),
                          compiler_params=_params(("parallel",)))(part, recv)


def _sum_devices(g):
    _, rows, cols = g.shape
    tr = _row_tile(rows, cols, budget=256 * 1024)

    def body(g_ref, o_ref):
        acc = g_ref[0]
        for d in range(1, 8):
            acc = acc + g_ref[d]
        o_ref[...] = acc

    return pl.pallas_call(body, name="sum_small_grads", grid=(rows // tr,),
                          in_specs=[pl.BlockSpec((8, tr, cols), lambda r: (0, r, 0))],
                          out_specs=pl.BlockSpec((tr, cols), lambda r: (r, 0)),
                          out_shape=jax.ShapeDtypeStruct((rows, cols), F32),
                          compiler_params=_params(("parallel",)))(g)


def _place_shard(w, layer, dtype, deps=()):
    _, rows, cols = w.shape
    tr = _row_tile(rows, cols)

    def body(i_ref, *rest):
        o_ref = rest[-1]
        o_ref[...] = i_ref[...].astype(o_ref.dtype)

    out = pl.pallas_call(body, name="place_shard", grid=(rows // tr,),
                         in_specs=[pl.BlockSpec((None, tr, cols), lambda r: (layer, r, 0))] + _any_specs(len(deps)),
                         out_specs=pl.BlockSpec((None, tr, cols), lambda r: (_my_chip(), r, 0)),
                         out_shape=jax.ShapeDtypeStruct((N_CHIPS, rows, cols), dtype),
                         compiler_params=_params(("parallel",)))(w, *deps)
    return out.reshape(N_CHIPS, 2, rows // 2, cols)


def _adamw(w, gs, m, v, deps=()):
    n_layers, rows, cols = w.shape
    tr = _row_tile(rows, cols)

    def body(w_ref, m_ref, v_ref, *rest):
        g_refs = rest[:n_layers]
        go_ref, d_ref, mo_ref, vo_ref = rest[-4:]
        gv = g_refs[0][...]
        for layer in range(1, n_layers):
            gv = jnp.where(pl.program_id(0) == layer, g_refs[layer][...], gv)
        d_ref[...], mo_ref[...], vo_ref[...] = _adamw_math(w_ref[...], gv, m_ref[...], v_ref[...])
        go_ref[...] = gv

    spec = pl.BlockSpec((None, tr, cols), lambda layer, r: (layer, r, 0))
    g_specs = [pl.BlockSpec((tr, cols), lambda layer, r, own=own: (jnp.where(layer == own, r, 0), 0))
               for own in range(n_layers)]
    return pl.pallas_call(body, name="adamw", grid=(n_layers, rows // tr),
                          in_specs=[spec] * 3 + g_specs + _any_specs(len(deps)),
                          out_specs=[spec] * 4, out_shape=[jax.ShapeDtypeStruct((n_layers, rows, cols), F32)] * 4,
                          compiler_params=_params(("parallel", "parallel")))(w, m, v, *gs, *deps)


def _pad_rope(w):
    z = jnp.zeros(w.shape[:-1] + (ROPE_HALF,), w.dtype)
    return jnp.concatenate([w[..., :ROPE_HALF], z, w[..., ROPE_HALF:], z], axis=-1)


def _unpad_rope(g):
    return jnp.concatenate([g[..., :ROPE_HALF], g[..., ROPE:ROPE + ROPE_HALF]], axis=-1)


def _unstack_cols(s):
    n, r, cs = s.shape
    return jnp.transpose(s, (1, 0, 2)).reshape(r, n * cs)


def _stack_cols(f):
    r, cfull = f.shape
    return jnp.transpose(f.reshape(r, N_CHIPS, cfull // N_CHIPS), (1, 0, 2))


def _small_shard(norm, conv):
    return jnp.concatenate([jnp.pad(norm, ((0, 15), (0, 0))), jnp.pad(conv, ((0, 13), (0, 0)))], axis=0)


def _flat_rows(a):
    return a.reshape(-1, LANES)


def _pack_small(arrs):
    return jnp.concatenate([_flat_rows(a.astype(F32)) for a in arrs], axis=0)


def _unpack_small(flat, like):
    out, r = [], 0
    for a in like:
        n = a.size // LANES
        out.append(flat[r:r + n].reshape(a.shape))
        r += n
    return out


def kernel(x, positions, e_norm_mix, e_w_in, e_q_norm, e_w_uq, e_kv_norm, e_w_ukv, e_v_norm, e_sgu_w, e_sgu_b, e_mla_out_norm, e_sgu_out_norm, e_w_out, o_norm_mix, o_w_in, o_conv_w, o_w_out, mlp_norm, mlp_w1, mlp_w2, final_norm, loss_target, m_e_norm_mix, m_e_w_in, m_e_q_norm, m_e_w_uq, m_e_kv_norm, m_e_w_ukv, m_e_v_norm, m_e_sgu_w, m_e_sgu_b, m_e_mla_out_norm, m_e_sgu_out_norm, m_e_w_out, m_o_norm_mix, m_o_w_in, m_o_conv_w, m_o_w_out, m_mlp_norm, m_mlp_w1, m_mlp_w2, m_final_norm, v_e_norm_mix, v_e_w_in, v_e_q_norm, v_e_w_uq, v_e_kv_norm, v_e_w_ukv, v_e_v_norm, v_e_sgu_w, v_e_sgu_b, v_e_mla_out_norm, v_e_sgu_out_norm, v_e_w_out, v_o_norm_mix, v_o_w_in, v_o_conv_w, v_o_w_out, v_mlp_norm, v_mlp_w1, v_mlp_w2, v_final_norm):
    t, d = x.shape[1], x.shape[2]
    ql, kvl = e_q_norm.shape[1], e_kv_norm.shape[1]
    groups = e_v_norm.shape[1]
    gw = groups * LANES
    heads = N_CHIPS * e_w_uq.shape[2] // (LANES + ROPE)
    hw = heads * LANES
    mix = hw + gw
    ei = N_CHIPS * e_w_in.shape[2]
    cd = N_CHIPS * o_conv_w.shape[2]
    ff = N_CHIPS * mlp_w1.shape[2]
    ffs = ff // N_CHIPS
    pi = 2 * gw + ql + kvl + LANES
    assert e_norm_mix.shape[0] == 1 and o_norm_mix.shape[0] == 1 and mlp_norm.shape[0] == 2
    assert ei == ql + kvl + ROPE + 2 * gw and cd == d and e_sgu_w.shape[2] == LANES
    assert (2 * gw) % ql == 0 and (2 * gw + ql) % kvl == 0 and t % LANES == 0
    scale = (LANES + ROPE) ** -0.5

    tr = min(256, t)
    tm = _pick(t, 1024, 8)
    kt, kd = _pick(t, 2048, 8), _pick(d, 2048)
    xs = x.reshape(t, d)
    tgt = loss_target.reshape(t, d)

    small_shard = _small_shard(o_norm_mix, o_conv_w[0])
    first, tok = _gather_start("gather_start_e", [
        [_place_shard(e_w_in, 0, BF16)],
        [_place_shard(e_w_uq, 0, BF16), _place_shard(e_w_ukv, 0, BF16), _place_shard(e_w_out, 0, BF16),
         _place_shard(small_shard[None], 0, F32)]])
    rest, tok = _gather_start("gather_start_rest", [
        [_place_shard(mlp_w1, 0, BF16, (tok,))], [_place_shard(mlp_w2, 0, BF16, (tok,))],
        [_place_shard(o_w_in, 0, BF16, (tok,)), _place_shard(o_w_out, 0, BF16, (tok,))],
        [_place_shard(mlp_w1, 1, BF16, (tok,))], [_place_shard(mlp_w2, 1, BF16, (tok,))]])
    started = first + rest

    def gathered(gi, tag, after):
        send, recv, bufs = started[gi]
        bufs = _gather_forward(tag, _gather_wait(tag, send, recv, bufs, after))
        return [b.reshape(N_CHIPS, 2 * b.shape[2], b.shape[3]) for b in bufs]

    g_e = e_norm_mix
    h0 = _norm_fwd("e_norm", xs, g_e, tr)
    inv_freq = ROPE_BASE ** (-jnp.arange(0, ROPE, 2, dtype=F32) / ROPE)
    zeros32 = jnp.zeros((ROPE_HALF,), F32)
    ones32 = jnp.ones((ROPE_HALF,), F32)
    invf = jnp.concatenate([inv_freq, zeros32, inv_freq, zeros32]).reshape(1, LANES)
    cmask = jnp.concatenate([ones32, zeros32, ones32, zeros32]).reshape(1, LANES)
    smask = jnp.concatenate([-ones32, zeros32, ones32, zeros32]).reshape(1, LANES)
    ctab, stab = _rope_tables(positions.reshape(t, 1).astype(F32), invf, cmask, smask, tr)

    w_in_g, = gathered(0, "e_in", (h0, ctab, tok))
    full = _unstack_cols(w_in_g)
    c2, c3 = ql + kvl, ql + kvl + ROPE
    w_in_all = jnp.concatenate([full[:, c3:], full[:, :c2], _pad_rope(full[:, c2:c3])], axis=1)
    proj, = _matmul("e_proj", Mat(h0, t, d), Mat(w_in_all, d, pi), "nn", [_out(t, pi, F32)], tm, _pick(pi, 1024), kd)

    w_uq_g, w_ukv_g, w_eout_g, small_g = gathered(1, "e", proj)
    full = _unstack_cols(w_uq_g).reshape(ql, heads, LANES + ROPE)
    w_q_all = jnp.concatenate([full[:, :, :LANES].reshape(ql, hw), _pad_rope(full[:, :, LANES:]).reshape(ql, hw)], axis=1)
    full = _unstack_cols(w_ukv_g).reshape(kvl, heads, 2 * LANES)
    w_kv_all = jnp.concatenate([full[:, :, :LANES].reshape(kvl, hw), full[:, :, LANES:].reshape(kvl, hw)], axis=1)
    w_eout = w_eout_g.reshape(mix, d)
    g_o = small_g[:, 0].reshape(1, d)
    conv_w = jnp.pad(jnp.transpose(small_g[:, 16:19], (1, 0, 2)).reshape(3, cd), ((0, 5), (0, 0)))

    g_q, g_kv = e_q_norm, e_kv_norm
    g_vn = e_v_norm.reshape(1, gw)
    sgu_w = e_sgu_w[0]
    sgu_b = jnp.broadcast_to(e_sgu_b[0][:, :, None], (groups, LANES, LANES))
    g_mla, g_sgu = e_mla_out_norm, e_sgu_out_norm
    g_m0, g_m1 = mlp_norm[0:1], mlp_norm[1:2]
    g_f = final_norm.reshape(1, d)

    def mlp_fwd(tag, xin, g, gi):
        hm = _norm_fwd("mlp_norm_" + tag, xin, g, tr)
        tn = _pick(ffs, 1024)
        w1 = Mat(gathered(gi, "w1_" + tag, hm)[0], d, ff, "colstack")
        a, act = _matmul("mlp_up_" + tag, Mat(hm, t, d), w1, "nn",
                         [_out(t, ff, BF16), _out(t, ff, BF16)], tm, tn, kd,
                         epilogue=lambda z: (jnp.maximum(z, 0.0), jnp.square(jnp.maximum(z, 0.0))))
        w2 = Mat(gathered(gi + 1, "w2_" + tag, act)[0].reshape(ff, d), ff, d)
        xo, = _matmul("mlp_down_" + tag, Mat(act, t, ff), w2, "nn",
                      [_out(t, d, F32)], tm, _pick(d, 1024), _pick(ffs, 2048),
                      epilogue=lambda z, r: (z + r,), extras=[Mat(xin, t, d)])
        return xo, hm, a, act, w1, w2

    def chip_start(tag, part):
        return _exchange_start("scatter_start_" + tag, _chip_route, 3 * len(part), part, [(3,) + p.shape[1:] for p in part])

    def pair_start(tag, stacked):
        g5 = [g.reshape(N_CHIPS, 2, g.shape[1] // 2, g.shape[2]) for g in stacked]
        return _exchange_start("pair_start_" + tag, _pair_route, N_CHIPS * len(g5), g5,
                               [(N_CHIPS,) + g.shape[2:] for g in g5])

    def pair_finish(tag, started, after):
        g5, from_sib = _exchange_wait("pair_wait_" + tag, _pair_route, started, after)
        return chip_start(tag, [_pair_sum(a, b) for a, b in zip(g5, from_sib)])

    def summed(tag, sc, after):
        part, lands = _exchange_wait("scatter_wait_" + tag, _chip_route, sc, after)
        half = [_chip_sum(p, r) for p, r in zip(part, lands)]
        return _exchange_start("share_start_" + tag, _share_route, len(half), half, [])

    def shared(tag, started, after):
        bufs, _ = _exchange_wait("share_wait_" + tag, _share_route, started, after)
        return [r.reshape(2 * r.shape[1], r.shape[2]) for r in bufs]

    def mlp_bwd(tag, dx, dxb, xin, g, w1, w2, hm, a, act, deps):
        tn = _pick(ffs, 1024)
        hr, hd = ffs // 2, d // 2
        dz, = _matmul("mlp_dact_" + tag, Mat(dxb, t, d), w2, "nt",
                      [_out(t, ff, BF16)], tm, tn, kd,
                      epilogue=lambda z, av: (z * (2.0 * av.astype(F32)),), extras=[Mat(a, t, ff)], deps=deps)

        def half(own):
            c = lax.axis_index("c")
            return c if own else 1 - c

        def act_half(own):
            return Mat(act, t, ff // 2, cmap=lambda cb, bc: (cb // (hr // bc)) * (ffs // bc) + half(own) * (hr // bc)
                       + cb % (hr // bc))

        def hm_half(own):
            return Mat(hm, t, hd, cmap=lambda cb, bc: cb + half(own) * (hd // bc))

        w1_out = lambda: _out(hd, ff, BF16, "colstack", (), (N_CHIPS, hd, ffs))
        theirs2, = _matmul("mlp_dw2_theirs_" + tag, act_half(False), Mat(dxb, t, d), "tn",
                           [_out(ff // 2, d, BF16)], _pick(hr, 1024), _pick(d, 2048), kt)
        theirs1, = _matmul("mlp_dw1_theirs_" + tag, hm_half(False), Mat(dz, t, ff), "tn",
                           [w1_out()], _pick(hd, 2048), tn, kt)
        sent = [theirs1, theirs2.reshape(N_CHIPS, hr, d)]
        started, tok = _exchange_start("pair_start_m" + tag, _slab_route, N_CHIPS * 2, sent, [s.shape for s in sent])
        dhm, = _matmul("mlp_dh_" + tag, Mat(dz, t, ff), w1, "nt",
                       [_out(t, d, F32)], tm, _pick(d, 1024), _pick(ffs, 2048), deps=(tok,))
        dxo, dxob, dg = _norm_bwd("mlp_norm_bwd_" + tag, dhm, xin, g, dx, tr)
        _, (sib1, sib2) = _exchange_wait("pair_wait_m" + tag, _slab_route, started, dxo)
        add = lambda z, s: (z + s.astype(F32),)
        part2, = _matmul("mlp_dw2_mine_" + tag, act_half(True), Mat(dxb, t, d), "tn",
                         [_out(ff // 2, d, BF16)], _pick(hr, 1024), _pick(d, 2048), kt,
                         epilogue=add, extras=[Mat(sib2.reshape(ff // 2, d), ff // 2, d)])
        part1, = _matmul("mlp_dw1_mine_" + tag, hm_half(True), Mat(dz, t, ff), "tn",
                         [w1_out()], _pick(hd, 2048), tn, kt, epilogue=add, extras=[Mat(sib1, hd, ff, "colstack")])
        sc, tok = chip_start("m" + tag, [part1, part2.reshape(N_CHIPS, hr, d)])
        return dxo, dxob, dg, sc, tok

    cq_cb, ckv_cb, kr_cb = 2 * gw // ql, (2 * gw + ql) // kvl, (2 * gw + ql + kvl) // LANES
    qn, kvn = _rowwise("qkv_norm", lambda a, b, ga, gb: (_rms(a, ga), _rms(b, gb)), t // tr,
                       [_rt(proj, tr, ql, cq_cb), _rt(proj, tr, kvl, ckv_cb), _whole(g_q), _whole(g_kv)],
                       [_rt_out(t, ql, BF16, tr), _rt_out(t, kvl, BF16, tr)])
    qfull, = _matmul("q_up", Mat(qn, t, ql), Mat(w_q_all, ql, 2 * hw), "nn", [_out(t, 2 * hw, F32)], tm, _pick(2 * hw, 1024), ql)
    kvall, = _matmul("kv_up", Mat(kvn, t, kvl), Mat(w_kv_all, kvl, 2 * hw), "nn", [_out(t, 2 * hw, BF16)], tm, _pick(2 * hw, 1024), kvl)
    qall, kr = _rope_fwd(qfull, proj, kr_cb, ctab, stab, heads, tr)
    att, lse_row = _attn_fwd(qall, kvall, kr, heads, scale, tr)
    rb = min(2 * LANES, t)
    sgu = _sgu_fwd(proj, g_vn, sgu_w, sgu_b, groups, rb)
    mixed = _rowwise("mix_norm", lambda a, s, ga, gs: jnp.concatenate([_rms(a, ga), _rms(s, gs)], axis=1), t // tr,
                     [_rt(att, tr), _rt(sgu, tr), _whole(g_mla), _whole(g_sgu)], [_rt_out(t, mix, BF16, tr)])[0]
    x1, = _matmul("e_out", Mat(mixed, t, mix), Mat(w_eout, mix, d), "nn", [_out(t, d, F32)], tm, _pick(d, 1024), _pick(mix, 2048),
                  epilogue=lambda z, r: (z + r,), extras=[Mat(xs, t, d)])
    x2, hm0, a0, act0, w1_0, w2_0 = mlp_fwd("0", x1, g_m0, 2)

    w_oin_g, w_oout_g = gathered(4, "o", x2)
    w_oout = w_oout_g.reshape(cd, d)
    h1 = _norm_fwd("o_norm", x2, g_o, tr)
    oin = Mat(_unstack_cols(w_oin_g), d, 3 * cd)
    tn_o = _pick(_gcd(3 * cd // N_CHIPS, cd), 512)
    proj3, = _matmul("o_proj", Mat(h1, t, d), oin, "nn", [_out(t, 3 * cd, F32, "colstack", (), (3, t, cd))],
                     tm, _pick(cd, 1024), kd)
    tc = _pick(cd, 256)
    bz = _conv_fwd(proj3, conv_w, tc)
    x3, = _matmul("o_out", Mat(bz, t, cd), Mat(w_oout, cd, d), "nn", [_out(t, d, F32)], tm, _pick(d, 1024), _pick(cd, 2048),
                  epilogue=lambda z, r: (z + r,), extras=[Mat(x2, t, d)])
    x4, hm1, a1, act1, w1_1, w2_1 = mlp_fwd("1", x3, g_m1, 5)

    def final_fn(xv, gv, tv):
        r = lax.rsqrt(jnp.mean(xv * xv, axis=-1, keepdims=True) + EPS)
        xh = xv * r
        err = xh * gv - tv
        dy = err * (1.0 / d)
        dxh = dy * gv
        dx = r * (dxh - xh * jnp.mean(dxh * xh, axis=-1, keepdims=True))
        sq = jnp.sum(err * err, axis=0, keepdims=True)
        part = sq[:, :LANES]
        for k in range(1, d // LANES):
            part = part + sq[:, k * LANES:(k + 1) * LANES]
        return dx, dx, part, jnp.sum(dy * xh, axis=0, keepdims=True)

    dx4, dx4b, loss_vec, dg_f = _rowwise("loss_final_norm", final_fn, t // tr, [_rt(x4, tr), _whole(g_f), _rt(tgt, tr)],
                                         [_rt_out(t, d, F32, tr), _rt_out(t, d, BF16, tr)],
                                         [jax.ShapeDtypeStruct((1, LANES), F32), jax.ShapeDtypeStruct((1, d), F32)])

    dx3, dx3b, dg_m1, sc_m1, tok = mlp_bwd("1", dx4, dx4b, x3, g_m1, w1_1, w2_1, hm1, a1, act1, ())

    dbz, = _matmul("o_out_dx", Mat(dx3b, t, d), Mat(w_oout, cd, d), "nt", [_out(t, cd, F32)], tm, _pick(cd, 1024), kd,
                   deps=(tok,))
    dw_oout, = _matmul("o_out_dw", Mat(bz, t, cd), Mat(dx3b, t, d), "tn", [_out(cd, d, BF16)], _pick(cd, 1024), _pick(d, 1024), kt)
    dproj3, dconv = _conv_bwd(proj3, conv_w, dbz, tc)
    dp3 = Mat(dproj3, t, 3 * cd, "colstack")
    dw_oin, = _matmul("o_proj_dw", Mat(h1, t, d), dp3, "tn", [_out(d, 3 * cd, BF16, "colstack", (), (N_CHIPS, d, 3 * cd // N_CHIPS))],
                      _pick(d, 2048), tn_o, kt)
    started_o, tok = pair_start("o", [dw_oin, dw_oout.reshape(N_CHIPS, cd // N_CHIPS, d)])
    dh1, = _matmul("o_proj_dx", dp3, oin, "nt", [_out(t, d, F32)], tm, _pick(d, 1024), _pick(cd, 2048), deps=(tok,))
    dx2, dx2b, dg_o = _norm_bwd("o_norm_bwd", dh1, x2, g_o, dx3, tr)
    sc_o, tok = pair_finish("o", started_o, dx2)

    dconv_s = jnp.transpose(dconv[:3].reshape(3, N_CHIPS, cd // N_CHIPS), (1, 0, 2))
    gsmall = jnp.concatenate([jnp.pad(dg_o.reshape(N_CHIPS, 1, d // N_CHIPS), ((0, 0), (0, 15), (0, 0))),
                              jnp.pad(dconv_s, ((0, 0), (0, 13), (0, 0)))], axis=1)
    dx1, dx1b, dg_m0, sc_m0, tok = mlp_bwd("0", dx2, dx2b, x1, g_m0, w1_0, w2_0, hm0, a0, act0, (tok,))

    dmixed, = _matmul("e_out_dx", Mat(dx1b, t, d), Mat(w_eout, mix, d), "nt", [_out(t, mix, F32)], tm, _pick(mix, 1024), kd,
                      deps=(tok,))
    dw_eout, = _matmul("e_out_dw", Mat(mixed, t, mix), Mat(dx1b, t, d), "tn", [_out(mix, d, BF16)], _pick(mix, 1024), _pick(d, 1024), kt)

    def mixb_fn(dm, a, s, ga, gs):
        da, dga = _rms_bwd(dm[:, :hw], a, ga)
        dsg, dgs = _rms_bwd(dm[:, hw:], s, gs)
        prod = da * a
        cols = [jnp.broadcast_to(jnp.sum(prod[:, h * LANES:(h + 1) * LANES], axis=-1, keepdims=True), (tr, LANES))
                for h in range(heads)]
        return da, dsg, jnp.stack([_row_of(c) for c in cols], axis=0), dga, dgs

    da_b, dsgu, delta_row, dg_mla, dg_sgu = _rowwise(
        "mix_norm_bwd", mixb_fn, t // tr, [_rt(dmixed, tr), _rt(att, tr), _rt(sgu, tr), _whole(g_mla), _whole(g_sgu)],
        [_rt_out(t, hw, BF16, tr), _rt_out(t, gw, F32, tr),
         (jax.ShapeDtypeStruct((heads, 8, t), F32), pl.BlockSpec((heads, 8, tr), lambda i: (0, 0, i)))],
        [jax.ShapeDtypeStruct((1, hw), F32), jax.ShapeDtypeStruct((1, gw), F32)])

    dproj, dsgu_w, dsgu_b8, dg_vn = _sgu_bwd(proj, dsgu, g_vn, sgu_w, sgu_b, groups, rb)
    dq1, dq2, dk1, dvv, dkr_h = _attn_bwd(qall, kvall, kr, da_b, lse_row, delta_row, heads, scale, min(2 * tr, t))
    dqfull, dproj = _rope_bwd(dq1, dq2, dkr_h, ctab, stab, heads, tr, dproj, kr_cb)
    dkvall = jnp.concatenate([dk1, dvv], axis=1)
    dw_q, = _matmul("q_up_dw", Mat(qn, t, ql), Mat(dqfull, t, 2 * hw), "tn", [_out(ql, 2 * hw, BF16)], ql, _pick(2 * hw, 1024), kt)
    dqn, = _matmul("q_up_dx", Mat(dqfull, t, 2 * hw), Mat(w_q_all, ql, 2 * hw), "nt", [_out(t, ql, F32)], tm, ql, _pick(2 * hw, 2048))
    dw_kv, = _matmul("kv_up_dw", Mat(kvn, t, kvl), Mat(dkvall, t, 2 * hw), "tn", [_out(kvl, 2 * hw, BF16)], kvl, _pick(2 * hw, 1024), kt)
    dkvn, = _matmul("kv_up_dx", Mat(dkvall, t, 2 * hw), Mat(w_kv_all, kvl, 2 * hw), "nt", [_out(t, kvl, F32)], tm, kvl, _pick(2 * hw, 2048))

    def qkvb_fn(da, db, a, b, ga, gb):
        dxa, dga = _rms_bwd(da, a, ga)
        dxb, dgb = _rms_bwd(db, b, gb)
        return jnp.concatenate([dxa, dxb], axis=1), dga, dgb

    assert (2 * gw) % (ql + kvl) == 0
    into = (jax.ShapeDtypeStruct(dproj.shape, dproj.dtype),
            pl.BlockSpec((tr, ql + kvl), lambda i: (i, 2 * gw // (ql + kvl))))
    dproj, dg_q, dg_kv = _rowwise(
        "qkv_norm_bwd", qkvb_fn, t // tr,
        [_rt(dqn, tr), _rt(dkvn, tr), _rt(proj, tr, ql, cq_cb), _rt(proj, tr, kvl, ckv_cb), _whole(g_q), _whole(g_kv)],
        [into], [jax.ShapeDtypeStruct((1, ql), F32), jax.ShapeDtypeStruct((1, kvl), F32)], deps=(dproj,), fill=(0, 0))
    dw_in, = _matmul("e_proj_dw", Mat(dproj, t, pi), Mat(h0, t, d), "tn", [_out(pi, d, F32)], _pick(pi, 1024), _pick(d, 2048), kt)
    dh0, = _matmul("e_proj_dx", Mat(dproj, t, pi), Mat(w_in_all, d, pi), "nt", [_out(t, d, F32)], tm, _pick(d, 1024), _pick(pi, 4096))
    dx0, _, dg_e = _norm_bwd("e_norm_bwd", dh0, xs, g_e, dx1, tr)

    kr0 = 2 * gw + c2
    gw_in = jnp.concatenate([dw_in[2 * gw:kr0], dw_in[kr0:kr0 + ROPE_HALF], dw_in[kr0 + ROPE:kr0 + ROPE + ROPE_HALF],
                             dw_in[:2 * gw]], axis=0).reshape(N_CHIPS, ei // N_CHIPS, d)
    gq = jnp.concatenate([dw_q[:, :hw].reshape(ql, heads, LANES), _unpad_rope(dw_q[:, hw:].reshape(ql, heads, LANES))], axis=-1)
    gw_uq = _stack_cols(gq.reshape(ql, heads * (LANES + ROPE)))
    gkv = jnp.concatenate([dw_kv[:, :hw].reshape(kvl, heads, LANES), dw_kv[:, hw:].reshape(kvl, heads, LANES)], axis=-1)
    gw_ukv = _stack_cols(gkv.reshape(kvl, heads * 2 * LANES))
    started_e, tok_pair = pair_start("e", [gw_in, gw_uq, gw_ukv, dw_eout.reshape(N_CHIPS, mix // N_CHIPS, d), gsmall])

    small_like = [e_norm_mix, e_q_norm, e_kv_norm, e_v_norm, e_sgu_w, e_sgu_b, e_mla_out_norm, e_sgu_out_norm, mlp_norm, final_norm]
    small_grads = [dg_e, dg_q, dg_kv, dg_vn, dsgu_w, dsgu_b8[:, 0, :], dg_mla, dg_sgu, jnp.concatenate([dg_m0, dg_m1], axis=0), dg_f]
    packed = _pack_small(small_grads)
    n_small = packed.shape[0] + (-packed.shape[0]) % 8
    pad = n_small - packed.shape[0] + 8
    sflat = jnp.concatenate([jnp.pad(packed, ((0, pad - 8), (0, 0))), jnp.pad(loss_vec, ((0, 7), (0, 0)))], axis=0)
    small_started, tok_small = _exchange_start("small_start", _all_route, 7, [sflat], [_spread(sflat)])

    sh_m1, tok = summed("m1", sc_m1, (tok_pair, tok_small))
    sc_e, tok = pair_finish("e", started_e, tok)
    sh_o, tok = summed("o", sc_o, tok)
    sh_m0, tok = summed("m0", sc_m0, tok)
    r_oin, r_oout = shared("o", sh_o, tok)
    late = {"o_w_in": _adamw(o_w_in, [r_oin], m_o_w_in, v_o_w_in),
            "o_w_out": _adamw(o_w_out, [r_oout], m_o_w_out, v_o_w_out)}
    r_w1_1, r_w2_1 = shared("m1", sh_m1, late["o_w_in"][1])
    r_w1_0, r_w2_0 = shared("m0", sh_m0, r_w2_1)
    late["mlp_w1"] = _adamw(mlp_w1, [r_w1_0, r_w1_1], m_mlp_w1, v_mlp_w1)
    sh_e, tok = summed("e", sc_e, late["mlp_w1"][1])
    late["mlp_w2"] = _adamw(mlp_w2, [r_w2_0, r_w2_1], m_mlp_w2, v_mlp_w2, deps=[tok])

    _, (all_small,) = _exchange_wait("small_wait", _all_route, small_started, late["mlp_w2"][1])
    g_small = _sum_devices(all_small)
    loss = 0.5 * jnp.sum(g_small[n_small]) / d

    def padded(arrs):
        return jnp.pad(_pack_small(arrs), ((0, pad), (0, 0)))

    s_m = [m_e_norm_mix, m_e_q_norm, m_e_kv_norm, m_e_v_norm, m_e_sgu_w, m_e_sgu_b, m_e_mla_out_norm, m_e_sgu_out_norm, m_mlp_norm, m_final_norm]
    s_v = [v_e_norm_mix, v_e_q_norm, v_e_kv_norm, v_e_v_norm, v_e_sgu_w, v_e_sgu_b, v_e_mla_out_norm, v_e_sgu_out_norm, v_mlp_norm, v_final_norm]
    s_out = [_unpack_small(o[0], small_like)
             for o in _adamw(padded(small_like)[None], [g_small], padded(s_m)[None], padded(s_v)[None])]

    r_in, r_uq, r_ukv, r_eout, r_small = shared("e", sh_e, (tok, late["mlp_w2"][1]))
    sm = [o[0] for o in _adamw(small_shard[None], [r_small], _small_shard(m_o_norm_mix, m_o_conv_w[0])[None],
                               _small_shard(v_o_norm_mix, v_o_conv_w[0])[None])]
    big = dict(late)
    flip = lambda a: jnp.swapaxes(a, 1, 2)
    big.update({
        "e_w_in": [flip(o) for o in _adamw(flip(e_w_in), [r_in], flip(m_e_w_in), flip(v_e_w_in))],
        "e_w_uq": _adamw(e_w_uq, [r_uq], m_e_w_uq, v_e_w_uq),
        "e_w_ukv": _adamw(e_w_ukv, [r_ukv], m_e_w_ukv, v_e_w_ukv),
        "e_w_out": _adamw(e_w_out, [r_eout], m_e_w_out, v_e_w_out),
    })

    names = ["e_norm_mix", "e_w_in", "e_q_norm", "e_w_uq", "e_kv_norm", "e_w_ukv", "e_v_norm", "e_sgu_w", "e_sgu_b",
             "e_mla_out_norm", "e_sgu_out_norm", "e_w_out", "o_norm_mix", "o_w_in", "o_conv_w", "o_w_out",
             "mlp_norm", "mlp_w1", "mlp_w2", "final_norm"]
    shapes = {"e_w_in": e_w_in.shape, "e_w_uq": e_w_uq.shape, "e_w_ukv": e_w_ukv.shape, "e_w_out": e_w_out.shape,
              "o_w_in": o_w_in.shape, "o_w_out": o_w_out.shape, "mlp_w1": mlp_w1.shape, "mlp_w2": mlp_w2.shape}
    small_names = ["e_norm_mix", "e_q_norm", "e_kv_norm", "e_v_norm", "e_sgu_w", "e_sgu_b", "e_mla_out_norm",
                   "e_sgu_out_norm", "mlp_norm", "final_norm"]

    def leaf(kind, name):
        if name in big:
            return big[name][kind].reshape(shapes[name])
        if name == "o_norm_mix":
            return sm[kind][0:1]
        if name == "o_conv_w":
            return sm[kind][16:19].reshape(o_conv_w.shape)
        return s_out[kind][small_names.index(name)]

    outs = [loss, dx0.reshape(x.shape)]
    for kind in range(4):
        outs += [leaf(kind, nm) for nm in names]
    return tuple(outs)


def _gcd(a, b):
    while b:
        a, b = b, a % b
    return a
```

```python
import jax
import jax.numpy as jnp
from jax import lax
from jax.experimental import pallas as pl
from jax.experimental.pallas import tpu as pltpu

F32 = jnp.float32
BF16 = jnp.bfloat16
MESH = pl.DeviceIdType.MESH

LANES = 128
ROPE = 64
ROPE_HALF = ROPE // 2
ROPE_BASE = 10000.0
EPS = 1e-6
N_CHIPS = 4
VMEM_LIMIT = 48 * 1024 * 1024
NEG = -1e30

ADAM_LR = 0.001
ADAM_B1 = 0.9
ADAM_B2 = 0.999
ADAM_EPS = 1e-08
ADAM_WD = 0.01
ADAM_STEP = 10


def _pick(n, target, step=LANES):
    best = None
    for t in range(step, min(n, target) + 1, step):
        if n % t == 0:
            best = t
    return best if best is not None else n


def _params(sem, vmem=VMEM_LIMIT):
    return pltpu.CompilerParams(dimension_semantics=sem, vmem_limit_bytes=vmem)


class Mat:
    def __init__(self, arr, rows, cols, kind="plain", lead=(), cmap=None, shape=None, dtype=None):
        self.arr, self.rows, self.cols, self.kind, self.lead, self.cmap = arr, rows, cols, kind, tuple(lead), cmap
        self.shape = tuple(arr.shape) if arr is not None else tuple(shape)
        self.dtype = arr.dtype if arr is not None else dtype

    def sds(self):
        return jax.ShapeDtypeStruct(self.shape, self.dtype)

    def spec(self, br, bc, gridmap):
        lead, nl = self.lead, len(self.lead)
        if self.kind == "plain":
            assert self.rows % br == 0 and self.cols % bc == 0, (self.shape, br, bc)
            cmap = self.cmap if self.cmap is not None else (lambda cb, _: cb)
            block = (None,) * nl + (br, bc)

            def phys(rb, cb):
                return lead + (rb, cmap(cb, bc))
        elif self.kind == "colstack":
            cs = self.shape[-1]
            assert cs % bc == 0 and self.rows % br == 0, (self.shape, br, bc)
            q = cs // bc
            block = (None,) * (nl + 1) + (br, bc)

            def phys(rb, cb):
                return (cb // q,) + lead + (rb, cb % q)
        else:
            rs = self.shape[-2]
            assert rs % br == 0 and self.cols % bc == 0, (self.shape, br, bc)
            q = rs // br
            block = (None,) * (nl + 1) + (br, bc)

            def phys(rb, cb):
                return (rb // q,) + lead + (rb % q, cb)

        return pl.BlockSpec(block, lambda *g: phys(*gridmap(*g)))


def _adamw_math(w, g, m, v):
    mn = ADAM_B1 * m + (1.0 - ADAM_B1) * g
    vn = ADAM_B2 * v + (1.0 - ADAM_B2) * jnp.square(g)
    m_hat = mn / (1.0 - ADAM_B1 ** ADAM_STEP)
    v_hat = vn / (1.0 - ADAM_B2 ** ADAM_STEP)
    return -ADAM_LR * (m_hat / (jnp.sqrt(v_hat) + ADAM_EPS) + ADAM_WD * w), mn, vn


def _matmul(name, a, b, mode, outs, tm, tn, tk, epilogue=None, extras=(), deps=()):
    if mode == "nn":
        m, k, n = a.rows, a.cols, b.cols
        a_spec = a.spec(tm, tk, lambda i, j, kk: (i, kk))
        b_spec = b.spec(tk, tn, lambda i, j, kk: (kk, j))
        dims = (((1,), (0,)), ((), ()))
    elif mode == "nt":
        m, k, n = a.rows, a.cols, b.rows
        a_spec = a.spec(tm, tk, lambda i, j, kk: (i, kk))
        b_spec = b.spec(tn, tk, lambda i, j, kk: (j, kk))
        dims = (((1,), (1,)), ((), ()))
    else:
        k, m, n = a.rows, a.cols, b.cols
        a_spec = a.spec(tk, tm, lambda i, j, kk: (kk, i))
        b_spec = b.spec(tk, tn, lambda i, j, kk: (kk, j))
        dims = (((0,), (0,)), ((), ()))
    assert m % tm == 0 and n % tn == 0 and k % tk == 0, (name, m, n, k, tm, tn, tk)
    grid = (m // tm, n // tn, k // tk)
    nk = grid[2]
    n_ex, n_out, n_dep = len(extras), len(outs), len(deps)
    tile = lambda i, j, kk: (i, j)

    def finish(z, ex, out_refs):
        vals = epilogue(z, *[e[...] for e in ex]) if epilogue is not None else (z,)
        for o, v in zip(out_refs, vals):
            o[...] = v.astype(o.dtype)

    def body_single(a_ref, b_ref, *rest):
        finish(lax.dot_general(a_ref[...], b_ref[...], dims, preferred_element_type=F32),
               rest[:n_ex], rest[n_ex + n_dep:n_ex + n_dep + n_out])

    def body_acc(a_ref, b_ref, *rest):
        acc = rest[-1]
        kk = pl.program_id(2)

        @pl.when(kk == 0)
        def _():
            acc[...] = jnp.zeros_like(acc)

        acc[...] += lax.dot_general(a_ref[...], b_ref[...], dims, preferred_element_type=F32)

        @pl.when(kk == nk - 1)
        def _():
            finish(acc[...], rest[:n_ex], rest[n_ex + n_dep:n_ex + n_dep + n_out])

    res = pl.pallas_call(
        body_single if nk == 1 else body_acc, name=name, grid=grid,
        in_specs=[a_spec, b_spec] + [e.spec(tm, tn, tile) for e in extras]
        + [pl.BlockSpec(memory_space=pl.ANY) for _ in deps],
        out_specs=[o.spec(tm, tn, tile) for o in outs],
        out_shape=[o.sds() for o in outs],
        scratch_shapes=[] if nk == 1 else [pltpu.VMEM((tm, tn), F32)],
        compiler_params=_params(("parallel", "parallel", "arbitrary")),
    )(a.arr, b.arr, *[e.arr for e in extras], *deps)
    return res


def _out(rows, cols, dtype, kind="plain", lead=(), shape=None):
    return Mat(None, rows, cols, kind, lead, shape=shape if shape is not None else (rows, cols), dtype=dtype)


def _rt(arr, tr, width=None, cb=0):
    width = arr.shape[1] if width is None else width
    return arr, pl.BlockSpec((tr, width), lambda i: (i, cb))


def _whole(arr):
    nd = arr.ndim
    return arr, pl.BlockSpec(arr.shape, lambda i: (0,) * nd)


def _rowwise(name, fn, n_steps, ins, outs, accs=(), deps=(), fill=None):
    n_in, n_out, n_acc, n_dep = len(ins), len(outs), len(accs), len(deps)

    def body(*refs):
        vals = fn(*[r[...] for r in refs[:n_in]])
        if not isinstance(vals, (tuple, list)):
            vals = (vals,)
        for ref, v in zip(refs[n_in + n_dep:n_in + n_dep + n_out], vals[:n_out]):
            ref[...] = v.astype(ref.dtype)
        if n_acc:
            acc_refs = refs[n_in + n_dep + n_out:]

            @pl.when(pl.program_id(0) == 0)
            def _():
                for ref in acc_refs:
                    ref[...] = jnp.zeros_like(ref)

            for ref, v in zip(acc_refs, vals[n_out:]):
                ref[...] += v

    acc_specs = [pl.BlockSpec(s.shape, lambda i, nd=len(s.shape): (0,) * nd) for s in accs]
    res = pl.pallas_call(
        body, name=name, grid=(n_steps,),
        in_specs=[s for _, s in ins] + [pl.BlockSpec(memory_space=pl.ANY) for _ in deps],
        out_specs=[s for _, s in outs] + acc_specs,
        out_shape=[o for o, _ in outs] + list(accs),
        input_output_aliases={} if fill is None else {n_in + fill[0]: fill[1]},
        compiler_params=_params(("arbitrary",) if n_acc else ("parallel",)),
    )(*[a for a, _ in ins], *deps)
    return res


def _rt_out(t, width, dtype, tr):
    return jax.ShapeDtypeStruct((t, width), dtype), pl.BlockSpec((tr, width), lambda i: (i, 0))


def _rms(x, g):
    r = lax.rsqrt(jnp.mean(x * x, axis=-1, keepdims=True) + EPS)
    return x * r * g


def _rms_bwd(dy, x, g):
    r = lax.rsqrt(jnp.mean(x * x, axis=-1, keepdims=True) + EPS)
    xh = x * r
    dxh = dy * g
    dx = r * (dxh - xh * jnp.mean(dxh * xh, axis=-1, keepdims=True))
    dg = jnp.sum(dy * xh, axis=0, keepdims=True)
    return dx, dg


def _gelu_and_grad(x):
    k = 0.7978845608028654
    x2 = x * x
    th = jnp.tanh(k * (x + 0.044715 * (x2 * x)))
    half = 0.5 * (1.0 + th)
    return x * half, half + 0.5 * x * (1.0 - th * th) * (k * (1.0 + 3.0 * 0.044715 * x2))


def _gelu(x):
    return _gelu_and_grad(x)[0]


def _gelu_grad(x):
    return _gelu_and_grad(x)[1]


def _norm_fwd(name, x, g, tr):
    t, d = x.shape
    return _rowwise(name, lambda xv, gv: _rms(xv, gv), t // tr, [_rt(x, tr), _whole(g)], [_rt_out(t, d, BF16, tr)])[0]


def _norm_bwd(name, dh, x, g, dres, tr):
    t, d = x.shape

    def fn(dhv, xv, gv, drv):
        dx, dg = _rms_bwd(dhv, xv, gv)
        dx = dx + drv
        return dx, dx, dg

    return _rowwise(name, fn, t // tr, [_rt(dh, tr), _rt(x, tr), _whole(g), _rt(dres, tr)],
                    [_rt_out(t, d, F32, tr), _rt_out(t, d, BF16, tr)], [jax.ShapeDtypeStruct((1, d), F32)])


def _rope_tables(posf, invf, cmask, smask, tr):
    t = posf.shape[0]

    def fn(p, f, cm, sm):
        ang = p * f
        return jnp.cos(ang) * cm, jnp.sin(ang) * sm

    return _rowwise("rope_tables", fn, t // tr, [_rt(posf, tr), _whole(invf), _whole(cmask), _whole(smask)],
                    [_rt_out(t, LANES, F32, tr), _rt_out(t, LANES, F32, tr)])


def _rot(v, c, s):
    return v * c + pltpu.roll(v, ROPE, axis=1) * s


def _rot_bwd(dv, c, s):
    return dv * c + pltpu.roll(dv * s, ROPE, axis=1)


def _rope_fwd(qfull, proj, kr_cb, ctab, stab, heads, tr):
    t = qfull.shape[0]
    hw = heads * LANES

    def fn(q, kr, c, s):
        parts = [q[:, :hw]] + [_rot(q[:, hw + h * LANES: hw + (h + 1) * LANES], c, s) for h in range(heads)]
        return jnp.concatenate(parts, axis=1), _rot(kr, c, s)

    return _rowwise("rope_fwd", fn, t // tr, [_rt(qfull, tr), _rt(proj, tr, LANES, kr_cb), _rt(ctab, tr), _rt(stab, tr)],
                    [_rt_out(t, 2 * hw, BF16, tr), _rt_out(t, LANES, BF16, tr)])


def _rope_bwd(dq1, dq2, dkr_h, ctab, stab, heads, tr, dproj, kr_cb):
    t = dq1.shape[0]
    hw = heads * LANES

    def fn(a, b, dk, c, s):
        parts = [a] + [_rot_bwd(b[:, h * LANES:(h + 1) * LANES], c, s) for h in range(heads)]
        dks = dk[0]
        for h in range(1, heads):
            dks = dks + dk[h]
        return jnp.concatenate(parts, axis=1), _rot_bwd(dks, c, s)

    dk_spec = pl.BlockSpec((heads, tr, LANES), lambda i: (0, i, 0))
    into = (jax.ShapeDtypeStruct(dproj.shape, dproj.dtype), pl.BlockSpec((tr, LANES), lambda i: (i, kr_cb)))
    return _rowwise("rope_bwd", fn, t // tr, [_rt(dq1, tr), _rt(dq2, tr), (dkr_h, dk_spec), _rt(ctab, tr), _rt(stab, tr)],
                    [_rt_out(t, 2 * hw, BF16, tr), into], deps=(dproj,), fill=(0, 1))


def _dot_nt(a, b):
    return lax.dot_general(a, b, (((1,), (1,)), ((), ())), preferred_element_type=F32)


def _dot_tn(a, b):
    return lax.dot_general(a, b, (((0,), (0,)), ((), ())), preferred_element_type=F32)


def _dot(a, b):
    return jnp.dot(a, b, preferred_element_type=F32)


def _ranges(n_blocks):
    n_var = min(4, n_blocks)
    assert n_blocks % n_var == 0
    return n_var, n_blocks // n_var


def _row_of(col):
    return col.T[:8, :]


def _attn_fwd(qall, kvall, kr, heads, scale, tq):
    t = qall.shape[0]
    nq = t // tq
    n_var, per = _ranges(nq)

    def body(qn_ref, qr_ref, kn_ref, v_ref, kr_ref, o_ref, lser_ref):
        i = pl.program_id(1)
        for var in range(n_var):
            kv = (var + 1) * per * tq

            @pl.when(jnp.logical_and(i >= var * per, i < (var + 1) * per))
            def _(kv=kv):
                s = _dot_nt(jnp.concatenate([qn_ref[...], qr_ref[...]], axis=1),
                            jnp.concatenate([kn_ref[:kv, :], kr_ref[:kv, :]], axis=1)) * scale
                rows = i * tq + lax.broadcasted_iota(jnp.int32, (tq, kv), 0)
                cols = lax.broadcasted_iota(jnp.int32, (tq, kv), 1)
                s = jnp.where(cols <= rows, s, NEG)
                m = jnp.max(s, axis=-1, keepdims=True)
                p = jnp.exp(s - m)
                l = jnp.sum(p, axis=-1, keepdims=True)
                o_ref[...] = _dot(p.astype(BF16), v_ref[:kv, :]) / l
                lser_ref[...] = _row_of(jnp.broadcast_to(m + jnp.log(l), (tq, LANES)))

    return pl.pallas_call(
        body, name="attn_fwd", grid=(heads, nq),
        in_specs=[pl.BlockSpec((tq, LANES), lambda h, i: (i, h)),
                  pl.BlockSpec((tq, LANES), lambda h, i: (i, heads + h)),
                  pl.BlockSpec((t, LANES), lambda h, i: (0, h)),
                  pl.BlockSpec((t, LANES), lambda h, i: (0, heads + h)),
                  pl.BlockSpec((t, LANES), lambda h, i: (0, 0))],
        out_specs=[pl.BlockSpec((tq, LANES), lambda h, i: (i, h)),
                   pl.BlockSpec((None, 8, tq), lambda h, i: (h, 0, i))],
        out_shape=[jax.ShapeDtypeStruct((t, heads * LANES), F32), jax.ShapeDtypeStruct((heads, 8, t), F32)],
        compiler_params=_params(("parallel", "parallel")),
    )(qall, qall, kvall, kvall, kr)


def _attn_bwd(qall, kvall, kr, do, lse_row, delta_row, heads, scale, tk):
    t = qall.shape[0]
    nk = t // tk
    n_var, per = _ranges(nk)

    def body(qn_ref, qr_ref, kn_ref, v_ref, kr_ref, do_ref, lse_ref, dl_ref, dq1_ref, dq2_ref, dk_ref, dv_ref, dkr_ref):
        j = pl.program_id(1)

        @pl.when(j == 0)
        def _():
            dq1_ref[...] = jnp.zeros_like(dq1_ref)
            dq2_ref[...] = jnp.zeros_like(dq2_ref)

        for var in range(n_var):
            q0 = var * per * tk
            nq = t - q0

            @pl.when(jnp.logical_and(j >= var * per, j < (var + 1) * per))
            def _(q0=q0, nq=nq):
                qn, qr, do_v = qn_ref[q0:, :], qr_ref[q0:, :], do_ref[q0:, :]
                k1, k2 = kn_ref[...], kr_ref[...]
                qcat, kcat = jnp.concatenate([qn, qr], axis=1), jnp.concatenate([k1, k2], axis=1)
                st = _dot_nt(kcat, qcat) * scale
                keys = j * tk + lax.broadcasted_iota(jnp.int32, (tk, nq), 0)
                queries = q0 + lax.broadcasted_iota(jnp.int32, (tk, nq), 1)
                pt = jnp.where(keys <= queries, jnp.exp(st - lse_ref[0:1, q0:]), 0.0)
                dpt = _dot_nt(v_ref[...], do_v)
                dst = (pt * (dpt - dl_ref[0:1, q0:]) * scale).astype(BF16)
                dv_ref[...] = _dot(pt.astype(BF16), do_v).astype(dv_ref.dtype)
                dkc = _dot(dst, qcat)
                dk_ref[...] = dkc[:, :LANES].astype(dk_ref.dtype)
                dkr_ref[...] = dkc[:, LANES:]
                dqc = _dot_tn(dst, kcat)
                dq1_ref[q0:, :] += dqc[:, :LANES]
                dq2_ref[q0:, :] += dqc[:, LANES:]

    kblk = lambda off: pl.BlockSpec((tk, LANES), lambda h, j: (j, off + h))
    full = lambda off: pl.BlockSpec((t, LANES), lambda h, j: (0, off + h))
    stat = pl.BlockSpec((None, 8, t), lambda h, j: (h, 0, 0))
    return pl.pallas_call(
        body, name="attn_bwd", grid=(heads, nk),
        in_specs=[full(0), full(heads), kblk(0), kblk(heads), pl.BlockSpec((tk, LANES), lambda h, j: (j, 0)),
                  full(0), stat, stat],
        out_specs=[full(0), full(0), kblk(0), kblk(0), pl.BlockSpec((None, tk, LANES), lambda h, j: (h, j, 0))],
        out_shape=[jax.ShapeDtypeStruct((t, heads * LANES), F32)] * 2 + [jax.ShapeDtypeStruct((t, heads * LANES), BF16)] * 2
        + [jax.ShapeDtypeStruct((heads, t, LANES), F32)],
        compiler_params=_params(("parallel", "arbitrary")),
    )(qall, qall, kvall, kvall, kr, do, lse_row, delta_row)


def _tril():
    return lax.broadcasted_iota(jnp.int32, (LANES, LANES), 0) >= lax.broadcasted_iota(jnp.int32, (LANES, LANES), 1)


def _group_norm(vg):
    mu = jnp.mean(vg, axis=-1, keepdims=True)
    vc = vg - mu
    rs = lax.rsqrt(jnp.mean(vc * vc, axis=-1, keepdims=True) + EPS)
    return vc * rs, rs


def _sgu_fwd(proj, gain, w, bias, groups, rb):
    t = proj.shape[0]
    gw = groups * LANES
    cpb = rb // LANES

    def body(u_ref, v_ref, gain_ref, w_ref, b_ref, s_ref):
        tril = _tril()
        for g in range(groups):
            wt = jnp.where(tril, w_ref[g], 0.0).astype(BF16)
            cols = slice(g * LANES, (g + 1) * LANES)
            for ci in range(cpb):
                rows = slice(ci * LANES, (ci + 1) * LANES)
                ug = _gelu(u_ref[rows, cols])
                vh, _ = _group_norm(_gelu(v_ref[rows, cols]))
                vn = vh * gain_ref[:, cols]
                y = _dot(wt, vn.astype(BF16)) + b_ref[g]
                s_ref[rows, cols] = ug * y

    return pl.pallas_call(
        body, name="sgu_fwd", grid=(t // rb,),
        in_specs=[pl.BlockSpec((rb, gw), lambda i: (i, 0)), pl.BlockSpec((rb, gw), lambda i: (i, 1)),
                  pl.BlockSpec((1, gw), lambda i: (0, 0)),
                  pl.BlockSpec((groups, LANES, LANES), lambda i: (0, 0, 0)),
                  pl.BlockSpec((groups, LANES, LANES), lambda i: (0, 0, 0))],
        out_specs=pl.BlockSpec((rb, gw), lambda i: (i, 0)),
        out_shape=jax.ShapeDtypeStruct((t, gw), F32),
        compiler_params=_params(("parallel",)),
    )(proj, proj, gain, w, bias)


def _sgu_bwd(proj, ds, gain, w, bias, groups, rb):
    t, width = proj.shape
    gw = groups * LANES
    cpb = rb // LANES
    n_steps = t // rb

    def body(u_ref, v_ref, ds_ref, gain_ref, w_ref, b_ref, dp_ref, dw_ref, db_ref, dg_ref, dy_acc):
        du_ref, dv_ref = dp_ref.at[:, :gw], dp_ref.at[:, gw:]
        step = pl.program_id(0)

        @pl.when(step == 0)
        def _():
            dw_ref[...] = jnp.zeros_like(dw_ref)
            dy_acc[...] = jnp.zeros_like(dy_acc)
            dg_ref[...] = jnp.zeros_like(dg_ref)

        tril = _tril()
        for g in range(groups):
            wt = jnp.where(tril, w_ref[g], 0.0).astype(BF16)
            cols = slice(g * LANES, (g + 1) * LANES)
            gain_g = gain_ref[:, cols]
            for ci in range(cpb):
                rows = slice(ci * LANES, (ci + 1) * LANES)
                u_raw, v_raw, ds_v = u_ref[rows, cols], v_ref[rows, cols], ds_ref[rows, cols]
                ug, ug_grad = _gelu_and_grad(u_raw)
                vg, vg_grad = _gelu_and_grad(v_raw)
                vh, rs = _group_norm(vg)
                vn = (vh * gain_g).astype(BF16)
                y = _dot(wt, vn) + b_ref[g]
                dy = ds_v * ug
                dyb = dy.astype(BF16)
                du_ref[rows, cols] = (ds_v * y * ug_grad).astype(du_ref.dtype)
                dy_acc[g] += dy
                dw_ref[g] += _dot_nt(dyb, vn)
                dvn = _dot_tn(wt, dyb)
                dg_ref[:, cols] += jnp.sum(dvn * vh, axis=0, keepdims=True)
                dvh = dvn * gain_g
                dvg = rs * (dvh - jnp.mean(dvh, axis=-1, keepdims=True)
                            - vh * jnp.mean(dvh * vh, axis=-1, keepdims=True))
                dv_ref[rows, cols] = (dvg * vg_grad).astype(dv_ref.dtype)

        @pl.when(step == n_steps - 1)
        def _():
            ones = jnp.ones((8, LANES), F32)
            for g in range(groups):
                dw_ref[g] = jnp.where(tril, dw_ref[g], 0.0)
                db_ref[g] = lax.dot_general(ones, dy_acc[g], (((1,), (1,)), ((), ())),
                                            precision=lax.Precision.HIGHEST, preferred_element_type=F32)

    blk = lambda cb: pl.BlockSpec((rb, gw), lambda i: (i, cb))
    whole3 = pl.BlockSpec((groups, LANES, LANES), lambda i: (0, 0, 0))
    return pl.pallas_call(
        body, name="sgu_bwd", grid=(n_steps,),
        in_specs=[blk(0), blk(1), blk(0), pl.BlockSpec((1, gw), lambda i: (0, 0)), whole3, whole3],
        out_specs=[pl.BlockSpec((rb, 2 * gw), lambda i: (i, 0)), whole3,
                   pl.BlockSpec((groups, 8, LANES), lambda i: (0, 0, 0)), pl.BlockSpec((1, gw), lambda i: (0, 0))],
        out_shape=[jax.ShapeDtypeStruct((t, width), BF16),
                   jax.ShapeDtypeStruct((groups, LANES, LANES), F32), jax.ShapeDtypeStruct((groups, 8, LANES), F32),
                   jax.ShapeDtypeStruct((1, gw), F32)],
        scratch_shapes=[pltpu.VMEM((groups, LANES, LANES), F32)],
        compiler_params=_params(("arbitrary",)),
    )(proj, proj, ds, gain, w, bias)


def _shift_down(z, s):
    rows = lax.broadcasted_iota(jnp.int32, z.shape, 0)
    return jnp.where(rows >= s, pltpu.roll(z, s, axis=0), 0.0)


def _shift_up(z, s):
    n = z.shape[0]
    rows = lax.broadcasted_iota(jnp.int32, z.shape, 0)
    return jnp.where(rows < n - s, pltpu.roll(z, n - s, axis=0), 0.0)


def _conv_fwd(proj3, cw, tc):
    _, t, cd = proj3.shape

    def body(p_ref, w_ref, o_ref):
        z = p_ref[1] * p_ref[2]
        w = w_ref[...]
        zc = w[2:3] * z + w[1:2] * _shift_down(z, 1) + w[0:1] * _shift_down(z, 2)
        o_ref[...] = (p_ref[0] * zc).astype(o_ref.dtype)

    return pl.pallas_call(
        body, name="conv_fwd", grid=(cd // tc,),
        in_specs=[pl.BlockSpec((3, t, tc), lambda j: (0, 0, j)), pl.BlockSpec((8, tc), lambda j: (0, j))],
        out_specs=pl.BlockSpec((t, tc), lambda j: (0, j)),
        out_shape=jax.ShapeDtypeStruct((t, cd), BF16),
        compiler_params=_params(("parallel",)),
    )(proj3, cw)


def _conv_bwd(proj3, cw, dbz, tc):
    _, t, cd = proj3.shape

    def body(p_ref, w_ref, d_ref, o_ref, dw_ref):
        b, c, xin = p_ref[0], p_ref[1], p_ref[2]
        w = w_ref[...]
        z = c * xin
        z1, z2 = _shift_down(z, 1), _shift_down(z, 2)
        zc = w[2:3] * z + w[1:2] * z1 + w[0:1] * z2
        d = d_ref[...]
        dzc = d * b
        dz = w[2:3] * dzc + w[1:2] * _shift_up(dzc, 1) + w[0:1] * _shift_up(dzc, 2)
        o_ref[0] = (d * zc).astype(o_ref.dtype)
        o_ref[1] = (dz * xin).astype(o_ref.dtype)
        o_ref[2] = (dz * c).astype(o_ref.dtype)
        row = lax.broadcasted_iota(jnp.int32, (8, tc), 0)
        dw0 = jnp.sum(dzc * z2, axis=0, keepdims=True)
        dw1 = jnp.sum(dzc * z1, axis=0, keepdims=True)
        dw2 = jnp.sum(dzc * z, axis=0, keepdims=True)
        dw_ref[...] = jnp.where(row == 0, dw0, 0.0) + jnp.where(row == 1, dw1, 0.0) + jnp.where(row == 2, dw2, 0.0)

    return pl.pallas_call(
        body, name="conv_bwd", grid=(cd // tc,),
        in_specs=[pl.BlockSpec((3, t, tc), lambda j: (0, 0, j)), pl.BlockSpec((8, tc), lambda j: (0, j)),
                  pl.BlockSpec((t, tc), lambda j: (0, j))],
        out_specs=[pl.BlockSpec((3, t, tc), lambda j: (0, 0, j)), pl.BlockSpec((8, tc), lambda j: (0, j))],
        out_shape=[jax.ShapeDtypeStruct((3, t, cd), BF16), jax.ShapeDtypeStruct((8, cd), F32)],
        compiler_params=_params(("parallel",)),
    )(proj3, cw, dbz)


def _place():
    x, y, c = lax.axis_index("x"), lax.axis_index("y"), lax.axis_index("c")
    chips = [(1 - x, y), (x, 1 - y), (1 - x, 1 - y)]
    return x, y, c, chips


def _any_specs(n):
    return [pl.BlockSpec(memory_space=pl.ANY) for _ in range(n)]


HBM_SPEC = pl.BlockSpec(memory_space=pltpu.HBM)
SEM_SPEC = pl.BlockSpec(memory_space=pltpu.SEMAPHORE)
ORDERED_EFFECT = pltpu.SideEffectType.DATAFLOW_SIDE_EFFECTING


def _in_hbm(a):
    return pltpu.with_memory_space_constraint(a, pltpu.HBM)


def _token():
    return jax.ShapeDtypeStruct((8, LANES), F32), pl.BlockSpec(memory_space=pltpu.VMEM)


def _gather_start(name, groups):
    sizes = [len(g) for g in groups]
    flat = [b for g in groups for b in g]
    n, ng = len(flat), len(groups)

    def body(*refs):
        ins, sems, token = refs[:n], refs[n:n + 2 * ng], refs[-1]
        x, y, c, chips = _place()
        me = 2 * x + y
        i = 0
        for gi, size in enumerate(sizes):
            for j in range(size):
                blk = ins[i].at[me, c]
                for k, chip in enumerate(chips):
                    pltpu.make_async_remote_copy(src_ref=blk, dst_ref=blk, send_sem=sems[2 * gi].at[3 * j + k],
                                                 recv_sem=sems[2 * gi + 1].at[3 * j + k],
                                                 device_id=(*chip, c), device_id_type=MESH).start()
                i += 1
        token[...] = jnp.zeros_like(token)

    tok_shape, tok_spec = _token()
    res = pl.pallas_call(
        body, name=name,
        in_specs=[HBM_SPEC] * n,
        out_specs=[SEM_SPEC] * (2 * ng) + [HBM_SPEC] * n + [tok_spec],
        out_shape=[pltpu.SemaphoreType.DMA((3 * size,)) for size in sizes for _ in (0, 1)]
        + [pltpu.HBM(b.shape, b.dtype) for b in flat] + [tok_shape],
        input_output_aliases={i: 2 * ng + i for i in range(n)},
        compiler_params=pltpu.CompilerParams(has_side_effects=ORDERED_EFFECT),
    )(*[_in_hbm(b) for b in flat])
    out, i = [], 2 * ng
    for gi, size in enumerate(sizes):
        out.append((res[2 * gi], res[2 * gi + 1], list(res[i:i + size])))
        i += size
    return out, res[-1]


def _gather_wait(tag, send, recv, bufs, after):
    n = len(bufs)
    after = tuple(after) if isinstance(after, (tuple, list)) else (after,)

    def body(*refs):
        ins, send_ref, recv_ref = refs[:n], refs[n], refs[n + 1]
        x, y, c, chips = _place()
        me = 2 * x + y
        for j in range(n):
            for k, (px, py) in enumerate(chips):
                cp = pltpu.make_async_remote_copy(src_ref=ins[j].at[me, c], dst_ref=ins[j].at[2 * px + py, c],
                                                  send_sem=send_ref.at[3 * j + k], recv_sem=recv_ref.at[3 * j + k],
                                                  device_id=(px, py, c), device_id_type=MESH)
                cp.wait_send()
                cp.wait_recv()

    return pl.pallas_call(
        body, name="gather_wait_" + tag,
        in_specs=[HBM_SPEC] * n + [SEM_SPEC, SEM_SPEC] + _any_specs(len(after)),
        out_specs=[HBM_SPEC] * n,
        out_shape=[pltpu.HBM(b.shape, b.dtype) for b in bufs],
        input_output_aliases={i: i for i in range(n)},
        compiler_params=pltpu.CompilerParams(has_side_effects=ORDERED_EFFECT),
    )(*bufs, send, recv, *after)


def _gather_forward(tag, bufs):
    n = len(bufs)

    def body(*refs):
        ins, outs = refs[:n], refs[n:2 * n]
        send, recv = refs[2 * n:]
        x, y, c, chips = _place()
        sib = (x, y, 1 - c)

        def cp(i, k, slot, half):
            return pltpu.make_async_remote_copy(src_ref=ins[i].at[slot, half], dst_ref=outs[i].at[slot, half],
                                                send_sem=send.at[3 * i + k], recv_sem=recv.at[3 * i + k],
                                                device_id=sib, device_id_type=MESH)

        cps = [cp(i, k, 2 * px + py, c) for i in range(n) for k, (px, py) in enumerate(chips)]
        for d in cps:
            d.start()
        for i in range(n):
            for k, (px, py) in enumerate(chips):
                cp(i, k, 2 * px + py, 1 - c).wait_recv()
        for d in cps:
            d.wait_send()

    return pl.pallas_call(
        body, name="gather_forward_" + tag,
        in_specs=_any_specs(n), out_specs=_any_specs(n),
        out_shape=[jax.ShapeDtypeStruct(b.shape, b.dtype) for b in bufs],
        scratch_shapes=[pltpu.SemaphoreType.DMA((3 * n,))] * 2,
        input_output_aliases={i: i for i in range(n)},
        compiler_params=pltpu.CompilerParams(has_side_effects=True),
    )(*bufs)


def _pair_route(srcs, zones):
    x, y, c, _ = _place()
    return [(srcs[i].at[j, 1 - c], zones[i].at[j], (x, y, 1 - c)) for i in range(len(srcs)) for j in range(N_CHIPS)]


def _slab_route(srcs, zones):
    x, y, c, _ = _place()
    return [(srcs[i].at[j], zones[i].at[j], (x, y, 1 - c)) for i in range(len(srcs)) for j in range(N_CHIPS)]


def _chip_route(srcs, zones):
    x, y, c, chips = _place()
    return [(srcs[i].at[2 * px + py], zones[i].at[k], (px, py, c)) for i in range(len(srcs)) for k, (px, py) in enumerate(chips)]


def _all_route(srcs, zones):
    x, y, c, _ = _place()
    flips = [(fx, fy, fc) for fx in (0, 1) for fy in (0, 1) for fc in (0, 1)][1:]
    return [(srcs[0], zones[0].at[4 * x + 2 * y + c], (x + fx - 2 * x * fx, y + fy - 2 * y * fy, c + fc - 2 * c * fc))
            for fx, fy, fc in flips]


def _share_route(srcs, zones):
    x, y, c, _ = _place()
    return [(s.at[c], s.at[c], (x, y, 1 - c)) for s in srcs]


def _exchange_start(name, route, n_copies, srcs, zones):
    n, nz = len(srcs), len(zones)
    lands = [lax.empty(z, a.dtype) if isinstance(z, tuple) else z for z, a in zip(zones, srcs)]

    def body(*refs):
        ins, zone_refs, send, recv, token = refs[:n], refs[n:n + nz], refs[n + nz], refs[n + nz + 1], refs[-1]
        for k, (src, dst, dev) in enumerate(route(ins, zone_refs)):
            pltpu.make_async_remote_copy(src_ref=src, dst_ref=dst, send_sem=send.at[k], recv_sem=recv.at[k],
                                         device_id=dev, device_id_type=MESH).start()
        token[...] = jnp.zeros_like(token)

    tok_shape, tok_spec = _token()
    res = pl.pallas_call(
        body, name=name,
        in_specs=[HBM_SPEC] * (n + nz),
        out_specs=[SEM_SPEC, SEM_SPEC] + [HBM_SPEC] * (n + nz) + [tok_spec],
        out_shape=[pltpu.SemaphoreType.DMA((n_copies,))] * 2 + [pltpu.HBM(a.shape, a.dtype) for a in srcs + lands]
        + [tok_shape],
        input_output_aliases={i: 2 + i for i in range(n + nz)},
        compiler_params=pltpu.CompilerParams(has_side_effects=ORDERED_EFFECT),
    )(*[_in_hbm(a) for a in srcs + lands])
    return (res[0], res[1], list(res[2:2 + n]), list(res[2 + n:2 + n + nz])), res[-1]


def _exchange_wait(name, route, started, after):
    send, recv, srcs, lands = started
    n, nz = len(srcs), len(lands)
    after = tuple(after) if isinstance(after, (tuple, list)) else (after,)

    def body(*refs):
        ins, zone_refs, send_ref, recv_ref = refs[:n], refs[n:n + nz], refs[n + nz], refs[n + nz + 1]
        for k, (src, dst, dev) in enumerate(route(ins, zone_refs)):
            cp = pltpu.make_async_remote_copy(src_ref=src, dst_ref=dst, send_sem=send_ref.at[k], recv_sem=recv_ref.at[k],
                                              device_id=dev, device_id_type=MESH)
            cp.wait_send()
            cp.wait_recv()

    res = pl.pallas_call(
        body, name=name,
        in_specs=[HBM_SPEC] * (n + nz) + [SEM_SPEC, SEM_SPEC] + _any_specs(len(after)),
        out_specs=[HBM_SPEC] * (n + nz),
        out_shape=[pltpu.HBM(a.shape, a.dtype) for a in srcs + lands],
        input_output_aliases={i: i for i in range(n + nz)},
        compiler_params=pltpu.CompilerParams(has_side_effects=ORDERED_EFFECT),
    )(*srcs, *lands, send, recv, *after)
    return list(res[:n]), list(res[n:])


def _spread(v):
    rows, cols = v.shape
    tr = _row_tile(rows, cols, budget=256 * 1024)

    def body(v_ref, o_ref):
        o_ref[...] = jnp.broadcast_to(v_ref[...][None], o_ref.shape)

    return pl.pallas_call(body, name="spread_small_grads", grid=(rows // tr,),
                          in_specs=[pl.BlockSpec((tr, cols), lambda r: (r, 0))],
                          out_specs=pl.BlockSpec((8, tr, cols), lambda r: (0, r, 0)),
                          out_shape=jax.ShapeDtypeStruct((8, rows, cols), v.dtype),
                          compiler_params=_params(("parallel",)))(v)


def _row_tile(rows, cols, itemsize=4, budget=2 * 1024 * 1024, step=8):
    best = None
    for t in range(step, rows + 1, step):
        if rows % t == 0 and t * cols * itemsize <= budget:
            best = t
    return best if best is not None else rows


def _my_chip():
    return 2 * lax.axis_index("x") + lax.axis_index("y")


def _col_parts(cols):
    return 2 if cols % (2 * LANES) == 0 else 1


def _pair_sum(g5, gsib):
    _, _, rh, cols = g5.shape
    tr = _row_tile(rh, cols, step=16)
    nc = _col_parts(cols)
    cw = cols // nc

    def body(*refs):
        o_ref = refs[-1]
        for h in range(nc):
            o_ref[:, h * cw:(h + 1) * cw] = (refs[h][...].astype(F32) + refs[nc + h][...].astype(F32)).astype(o_ref.dtype)

    return pl.pallas_call(body, name="grad_pair_sum", grid=(N_CHIPS, rh // tr),
                          in_specs=[pl.BlockSpec((None, None, tr, cw), lambda j, r, h=h: (j, lax.axis_index("c"), r, h))
                                    for h in range(nc)]
                          + [pl.BlockSpec((None, tr, cw), lambda j, r, h=h: (j, r, h)) for h in range(nc)],
                          out_specs=pl.BlockSpec((None, tr, cols), lambda j, r: (j, r, 0)),
                          out_shape=jax.ShapeDtypeStruct((N_CHIPS, rh, cols), BF16),
                          compiler_params=_params(("parallel", "parallel")))(*[g5] * nc, *[gsib] * nc)


def _chip_sum(part, recv):
    _, rh, cols = part.shape
    tr = _row_tile(rh, cols, step=16)
    nc = _col_parts(cols)
    cw = cols // nc

    def body(*refs):
        o_ref = refs[-1]
        for h in range(nc):
            acc = refs[h][...].astype(F32)
            for k in range(3):
                acc = acc + refs[nc + 3 * h + k][...].astype(F32)
            o_ref[:, h * cw:(h + 1) * cw] = acc

    return pl.pallas_call(body, name="grad_chip_sum", grid=(rh // tr,),
                          in_specs=[pl.BlockSpec((None, tr, cw), lambda r, h=h: (_my_chip(), r, h)) for h in range(nc)]
                          + [pl.BlockSpec((None, tr, cw), lambda r, h=h, k=k: (k, r, h)) for h in range(nc) for k in range(3)],
                          out_specs=pl.BlockSpec((None, tr, cols), lambda r: (lax.axis_index("c"), r, 0)),
                          out_shape=jax.ShapeDtypeStruct((2, rh, cols), F32),
                          compiler_params=_params(("parallel",)))(*[part] * nc, *[recv] * (3 * nc))


def _sum_devices(g):
    _, rows, cols = g.shape
    tr = _row_tile(rows, cols, budget=256 * 1024)

    def body(g_ref, o_ref):
        acc = g_ref[0]
        for d in range(1, 8):
            acc = acc + g_ref[d]
        o_ref[...] = acc

    return pl.pallas_call(body, name="sum_small_grads", grid=(rows // tr,),
                          in_specs=[pl.BlockSpec((8, tr, cols), lambda r: (0, r, 0))],
                          out_specs=pl.BlockSpec((tr, cols), lambda r: (r, 0)),
                          out_shape=jax.ShapeDtypeStruct((rows, cols), F32),
                          compiler_params=_params(("parallel",)))(g)


def _place_shard(w, layer, dtype, deps=()):
    _, rows, cols = w.shape
    tr = _row_tile(rows, cols)

    def body(i_ref, *rest):
        o_ref = rest[-1]
        o_ref[...] = i_ref[...].astype(o_ref.dtype)

    out = pl.pallas_call(body, name="place_shard", grid=(rows // tr,),
                         in_specs=[pl.BlockSpec((None, tr, cols), lambda r: (layer, r, 0))] + _any_specs(len(deps)),
                         out_specs=pl.BlockSpec((None, tr, cols), lambda r: (_my_chip(), r, 0)),
                         out_shape=jax.ShapeDtypeStruct((N_CHIPS, rows, cols), dtype),
                         compiler_params=_params(("parallel",)))(w, *deps)
    return out.reshape(N_CHIPS, 2, rows // 2, cols)


def _adamw(w, gs, m, v, deps=()):
    n_layers, rows, cols = w.shape
    tr = _row_tile(rows, cols)

    def body(w_ref, m_ref, v_ref, *rest):
        g_refs = rest[:n_layers]
        go_ref, d_ref, mo_ref, vo_ref = rest[-4:]
        gv = g_refs[0][...]
        for layer in range(1, n_layers):
            gv = jnp.where(pl.program_id(0) == layer, g_refs[layer][...], gv)
        d_ref[...], mo_ref[...], vo_ref[...] = _adamw_math(w_ref[...], gv, m_ref[...], v_ref[...])
        go_ref[...] = gv

    spec = pl.BlockSpec((None, tr, cols), lambda layer, r: (layer, r, 0))
    g_specs = [pl.BlockSpec((tr, cols), lambda layer, r, own=own: (jnp.where(layer == own, r, 0), 0))
               for own in range(n_layers)]
    return pl.pallas_call(body, name="adamw", grid=(n_layers, rows // tr),
                          in_specs=[spec] * 3 + g_specs + _any_specs(len(deps)),
                          out_specs=[spec] * 4, out_shape=[jax.ShapeDtypeStruct((n_layers, rows, cols), F32)] * 4,
                          compiler_params=_params(("parallel", "parallel")))(w, m, v, *gs, *deps)


def _pad_rope(w):
    z = jnp.zeros(w.shape[:-1] + (ROPE_HALF,), w.dtype)
    return jnp.concatenate([w[..., :ROPE_HALF], z, w[..., ROPE_HALF:], z], axis=-1)


def _unpad_rope(g):
    return jnp.concatenate([g[..., :ROPE_HALF], g[..., ROPE:ROPE + ROPE_HALF]], axis=-1)


def _unstack_cols(s):
    n, r, cs = s.shape
    return jnp.transpose(s, (1, 0, 2)).reshape(r, n * cs)


def _stack_cols(f):
    r, cfull = f.shape
    return jnp.transpose(f.reshape(r, N_CHIPS, cfull // N_CHIPS), (1, 0, 2))


def _small_shard(norm, conv):
    return jnp.concatenate([jnp.pad(norm, ((0, 15), (0, 0))), jnp.pad(conv, ((0, 13), (0, 0)))], axis=0)


def _flat_rows(a):
    return a.reshape(-1, LANES)


def _pack_small(arrs):
    return jnp.concatenate([_flat_rows(a.astype(F32)) for a in arrs], axis=0)


def _unpack_small(flat, like):
    out, r = [], 0
    for a in like:
        n = a.size // LANES
        out.append(flat[r:r + n].reshape(a.shape))
        r += n
    return out


def kernel(x, positions, e_norm_mix, e_w_in, e_q_norm, e_w_uq, e_kv_norm, e_w_ukv, e_v_norm, e_sgu_w, e_sgu_b, e_mla_out_norm, e_sgu_out_norm, e_w_out, o_norm_mix, o_w_in, o_conv_w, o_w_out, mlp_norm, mlp_w1, mlp_w2, final_norm, loss_target, m_e_norm_mix, m_e_w_in, m_e_q_norm, m_e_w_uq, m_e_kv_norm, m_e_w_ukv, m_e_v_norm, m_e_sgu_w, m_e_sgu_b, m_e_mla_out_norm, m_e_sgu_out_norm, m_e_w_out, m_o_norm_mix, m_o_w_in, m_o_conv_w, m_o_w_out, m_mlp_norm, m_mlp_w1, m_mlp_w2, m_final_norm, v_e_norm_mix, v_e_w_in, v_e_q_norm, v_e_w_uq, v_e_kv_norm, v_e_w_ukv, v_e_v_norm, v_e_sgu_w, v_e_sgu_b, v_e_mla_out_norm, v_e_sgu_out_norm, v_e_w_out, v_o_norm_mix, v_o_w_in, v_o_conv_w, v_o_w_out, v_mlp_norm, v_mlp_w1, v_mlp_w2, v_final_norm):
    t, d = x.shape[1], x.shape[2]
    ql, kvl = e_q_norm.shape[1], e_kv_norm.shape[1]
    groups = e_v_norm.shape[1]
    gw = groups * LANES
    heads = N_CHIPS * e_w_uq.shape[2] // (LANES + ROPE)
    hw = heads * LANES
    mix = hw + gw
    ei = N_CHIPS * e_w_in.shape[2]
    cd = N_CHIPS * o_conv_w.shape[2]
    ff = N_CHIPS * mlp_w1.shape[2]
    ffs = ff // N_CHIPS
    pi = 2 * gw + ql + kvl + LANES
    assert e_norm_mix.shape[0] == 1 and o_norm_mix.shape[0] == 1 and mlp_norm.shape[0] == 2
    assert ei == ql + kvl + ROPE + 2 * gw and cd == d and e_sgu_w.shape[2] == LANES
    assert (2 * gw) % ql == 0 and (2 * gw + ql) % kvl == 0 and t % LANES == 0
    scale = (LANES + ROPE) ** -0.5

    tr = min(256, t)
    tm = _pick(t, 1024, 8)
    kt, kd = _pick(t, 2048, 8), _pick(d, 2048)
    xs = x.reshape(t, d)
    tgt = loss_target.reshape(t, d)

    small_shard = _small_shard(o_norm_mix, o_conv_w[0])
    first, tok = _gather_start("gather_start_e", [
        [_place_shard(e_w_in, 0, BF16)],
        [_place_shard(e_w_uq, 0, BF16), _place_shard(e_w_ukv, 0, BF16), _place_shard(e_w_out, 0, BF16),
         _place_shard(small_shard[None], 0, F32)]])
    rest, tok = _gather_start("gather_start_rest", [
        [_place_shard(mlp_w1, 0, BF16, (tok,))], [_place_shard(mlp_w2, 0, BF16, (tok,))],
        [_place_shard(o_w_in, 0, BF16, (tok,)), _place_shard(o_w_out, 0, BF16, (tok,))],
        [_place_shard(mlp_w1, 1, BF16, (tok,))], [_place_shard(mlp_w2, 1, BF16, (tok,))]])
    started = first + rest

    def gathered(gi, tag, after):
        send, recv, bufs = started[gi]
        bufs = _gather_forward(tag, _gather_wait(tag, send, recv, bufs, after))
        return [b.reshape(N_CHIPS, 2 * b.shape[2], b.shape[3]) for b in bufs]

    g_e = e_norm_mix
    h0 = _norm_fwd("e_norm", xs, g_e, tr)
    inv_freq = ROPE_BASE ** (-jnp.arange(0, ROPE, 2, dtype=F32) / ROPE)
    zeros32 = jnp.zeros((ROPE_HALF,), F32)
    ones32 = jnp.ones((ROPE_HALF,), F32)
    invf = jnp.concatenate([inv_freq, zeros32, inv_freq, zeros32]).reshape(1, LANES)
    cmask = jnp.concatenate([ones32, zeros32, ones32, zeros32]).reshape(1, LANES)
    smask = jnp.concatenate([-ones32, zeros32, ones32, zeros32]).reshape(1, LANES)
    ctab, stab = _rope_tables(positions.reshape(t, 1).astype(F32), invf, cmask, smask, tr)

    w_in_g, = gathered(0, "e_in", (h0, ctab, tok))
    full = _unstack_cols(w_in_g)
    c2, c3 = ql + kvl, ql + kvl + ROPE
    w_in_all = jnp.concatenate([full[:, c3:], full[:, :c2], _pad_rope(full[:, c2:c3])], axis=1)
    proj, = _matmul("e_proj", Mat(h0, t, d), Mat(w_in_all, d, pi), "nn", [_out(t, pi, F32)], tm, _pick(pi, 1024), kd)

    w_uq_g, w_ukv_g, w_eout_g, small_g = gathered(1, "e", proj)
    full = _unstack_cols(w_uq_g).reshape(ql, heads, LANES + ROPE)
    w_q_all = jnp.concatenate([full[:, :, :LANES].reshape(ql, hw), _pad_rope(full[:, :, LANES:]).reshape(ql, hw)], axis=1)
    full = _unstack_cols(w_ukv_g).reshape(kvl, heads, 2 * LANES)
    w_kv_all = jnp.concatenate([full[:, :, :LANES].reshape(kvl, hw), full[:, :, LANES:].reshape(kvl, hw)], axis=1)
    w_eout = w_eout_g.reshape(mix, d)
    g_o = small_g[:, 0].reshape(1, d)
    conv_w = jnp.pad(jnp.transpose(small_g[:, 16:19], (1, 0, 2)).reshape(3, cd), ((0, 5), (0, 0)))

    g_q, g_kv = e_q_norm, e_kv_norm
    g_vn = e_v_norm.reshape(1, gw)
    sgu_w = e_sgu_w[0]
    sgu_b = jnp.broadcast_to(e_sgu_b[0][:, :, None], (groups, LANES, LANES))
    g_mla, g_sgu = e_mla_out_norm, e_sgu_out_norm
    g_m0, g_m1 = mlp_norm[0:1], mlp_norm[1:2]
    g_f = final_norm.reshape(1, d)

    def mlp_fwd(tag, xin, g, gi):
        hm = _norm_fwd("mlp_norm_" + tag, xin, g, tr)
        tn = _pick(ffs, 1024)
        w1 = Mat(gathered(gi, "w1_" + tag, hm)[0], d, ff, "colstack")
        a, act = _matmul("mlp_up_" + tag, Mat(hm, t, d), w1, "nn",
                         [_out(t, ff, BF16), _out(t, ff, BF16)], tm, tn, kd,
                         epilogue=lambda z: (jnp.maximum(z, 0.0), jnp.square(jnp.maximum(z, 0.0))))
        w2 = Mat(gathered(gi + 1, "w2_" + tag, act)[0].reshape(ff, d), ff, d)
        xo, = _matmul("mlp_down_" + tag, Mat(act, t, ff), w2, "nn",
                      [_out(t, d, F32)], tm, _pick(d, 1024), _pick(ffs, 2048),
                      epilogue=lambda z, r: (z + r,), extras=[Mat(xin, t, d)])
        return xo, hm, a, act, w1, w2

    def chip_start(tag, part):
        return _exchange_start("scatter_start_" + tag, _chip_route, 3 * len(part), part, [(3,) + p.shape[1:] for p in part])

    def pair_start(tag, stacked):
        g5 = [g.reshape(N_CHIPS, 2, g.shape[1] // 2, g.shape[2]) for g in stacked]
        return _exchange_start("pair_start_" + tag, _pair_route, N_CHIPS * len(g5), g5,
                               [(N_CHIPS,) + g.shape[2:] for g in g5])

    def pair_finish(tag, started, after):
        g5, from_sib = _exchange_wait("pair_wait_" + tag, _pair_route, started, after)
        return chip_start(tag, [_pair_sum(a, b) for a, b in zip(g5, from_sib)])

    def summed(tag, sc, after):
        part, lands = _exchange_wait("scatter_wait_" + tag, _chip_route, sc, after)
        half = [_chip_sum(p, r) for p, r in zip(part, lands)]
        return _exchange_start("share_start_" + tag, _share_route, len(half), half, [])

    def shared(tag, started, after):
        bufs, _ = _exchange_wait("share_wait_" + tag, _share_route, started, after)
        return [r.reshape(2 * r.shape[1], r.shape[2]) for r in bufs]

    def mlp_bwd(tag, dx, dxb, xin, g, w1, w2, hm, a, act, deps):
        tn = _pick(ffs, 1024)
        hr, hd = ffs // 2, d // 2
        dz, = _matmul("mlp_dact_" + tag, Mat(dxb, t, d), w2, "nt",
                      [_out(t, ff, BF16)], tm, tn, kd,
                      epilogue=lambda z, av: (z * (2.0 * av.astype(F32)),), extras=[Mat(a, t, ff)], deps=deps)

        def half(own):
            c = lax.axis_index("c")
            return c if own else 1 - c

        def act_half(own):
            return Mat(act, t, ff // 2, cmap=lambda cb, bc: (cb // (hr // bc)) * (ffs // bc) + half(own) * (hr // bc)
                       + cb % (hr // bc))

        def hm_half(own):
            return Mat(hm, t, hd, cmap=lambda cb, bc: cb + half(own) * (hd // bc))

        w1_out = lambda: _out(hd, ff, BF16, "colstack", (), (N_CHIPS, hd, ffs))
        theirs2, = _matmul("mlp_dw2_theirs_" + tag, act_half(False), Mat(dxb, t, d), "tn",
                           [_out(ff // 2, d, BF16)], _pick(hr, 1024), _pick(d, 2048), kt)
        theirs1, = _matmul("mlp_dw1_theirs_" + tag, hm_half(False), Mat(dz, t, ff), "tn",
                           [w1_out()], _pick(hd, 2048), tn, kt)
        sent = [theirs1, theirs2.reshape(N_CHIPS, hr, d)]
        started, tok = _exchange_start("pair_start_m" + tag, _slab_route, N_CHIPS * 2, sent, [s.shape for s in sent])
        dhm, = _matmul("mlp_dh_" + tag, Mat(dz, t, ff), w1, "nt",
                       [_out(t, d, F32)], tm, _pick(d, 1024), _pick(ffs, 2048), deps=(tok,))
        dxo, dxob, dg = _norm_bwd("mlp_norm_bwd_" + tag, dhm, xin, g, dx, tr)
        _, (sib1, sib2) = _exchange_wait("pair_wait_m" + tag, _slab_route, started, dxo)
        add = lambda z, s: (z + s.astype(F32),)
        part2, = _matmul("mlp_dw2_mine_" + tag, act_half(True), Mat(dxb, t, d), "tn",
                         [_out(ff // 2, d, BF16)], _pick(hr, 1024), _pick(d, 2048), kt,
                         epilogue=add, extras=[Mat(sib2.reshape(ff // 2, d), ff // 2, d)])
        part1, = _matmul("mlp_dw1_mine_" + tag, hm_half(True), Mat(dz, t, ff), "tn",
                         [w1_out()], _pick(hd, 2048), tn, kt, epilogue=add, extras=[Mat(sib1, hd, ff, "colstack")])
        sc, tok = chip_start("m" + tag, [part1, part2.reshape(N_CHIPS, hr, d)])
        return dxo, dxob, dg, sc, tok

    cq_cb, ckv_cb, kr_cb = 2 * gw // ql, (2 * gw + ql) // kvl, (2 * gw + ql + kvl) // LANES
    qn, kvn = _rowwise("qkv_norm", lambda a, b, ga, gb: (_rms(a, ga), _rms(b, gb)), t // tr,
                       [_rt(proj, tr, ql, cq_cb), _rt(proj, tr, kvl, ckv_cb), _whole(g_q), _whole(g_kv)],
                       [_rt_out(t, ql, BF16, tr), _rt_out(t, kvl, BF16, tr)])
    qfull, = _matmul("q_up", Mat(qn, t, ql), Mat(w_q_all, ql, 2 * hw), "nn", [_out(t, 2 * hw, F32)], tm, _pick(2 * hw, 1024), ql)
    kvall, = _matmul("kv_up", Mat(kvn, t, kvl), Mat(w_kv_all, kvl, 2 * hw), "nn", [_out(t, 2 * hw, BF16)], tm, _pick(2 * hw, 1024), kvl)
    qall, kr = _rope_fwd(qfull, proj, kr_cb, ctab, stab, heads, tr)
    att, lse_row = _attn_fwd(qall, kvall, kr, heads, scale, tr)
    rb = min(2 * LANES, t)
    sgu = _sgu_fwd(proj, g_vn, sgu_w, sgu_b, groups, rb)
    mixed = _rowwise("mix_norm", lambda a, s, ga, gs: jnp.concatenate([_rms(a, ga), _rms(s, gs)], axis=1), t // tr,
                     [_rt(att, tr), _rt(sgu, tr), _whole(g_mla), _whole(g_sgu)], [_rt_out(t, mix, BF16, tr)])[0]
    x1, = _matmul("e_out", Mat(mixed, t, mix), Mat(w_eout, mix, d), "nn", [_out(t, d, F32)], tm, _pick(d, 1024), _pick(mix, 2048),
                  epilogue=lambda z, r: (z + r,), extras=[Mat(xs, t, d)])
    x2, hm0, a0, act0, w1_0, w2_0 = mlp_fwd("0", x1, g_m0, 2)

    w_oin_g, w_oout_g = gathered(4, "o", x2)
    w_oout = w_oout_g.reshape(cd, d)
    h1 = _norm_fwd("o_norm", x2, g_o, tr)
    oin = Mat(_unstack_cols(w_oin_g), d, 3 * cd)
    tn_o = _pick(_gcd(3 * cd // N_CHIPS, cd), 512)
    proj3, = _matmul("o_proj", Mat(h1, t, d), oin, "nn", [_out(t, 3 * cd, F32, "colstack", (), (3, t, cd))],
                     tm, _pick(cd, 1024), kd)
    tc = _pick(cd, 256)
    bz = _conv_fwd(proj3, conv_w, tc)
    x3, = _matmul("o_out", Mat(bz, t, cd), Mat(w_oout, cd, d), "nn", [_out(t, d, F32)], tm, _pick(d, 1024), _pick(cd, 2048),
                  epilogue=lambda z, r: (z + r,), extras=[Mat(x2, t, d)])
    x4, hm1, a1, act1, w1_1, w2_1 = mlp_fwd("1", x3, g_m1, 5)

    def final_fn(xv, gv, tv):
        r = lax.rsqrt(jnp.mean(xv * xv, axis=-1, keepdims=True) + EPS)
        xh = xv * r
        err = xh * gv - tv
        dy = err * (1.0 / d)
        dxh = dy * gv
        dx = r * (dxh - xh * jnp.mean(dxh * xh, axis=-1, keepdims=True))
        sq = jnp.sum(err * err, axis=0, keepdims=True)
        part = sq[:, :LANES]
        for k in range(1, d // LANES):
            part = part + sq[:, k * LANES:(k + 1) * LANES]
        return dx, dx, part, jnp.sum(dy * xh, axis=0, keepdims=True)

    dx4, dx4b, loss_vec, dg_f = _rowwise("loss_final_norm", final_fn, t // tr, [_rt(x4, tr), _whole(g_f), _rt(tgt, tr)],
                                         [_rt_out(t, d, F32, tr), _rt_out(t, d, BF16, tr)],
                                         [jax.ShapeDtypeStruct((1, LANES), F32), jax.ShapeDtypeStruct((1, d), F32)])

    dx3, dx3b, dg_m1, sc_m1, tok = mlp_bwd("1", dx4, dx4b, x3, g_m1, w1_1, w2_1, hm1, a1, act1, ())

    dbz, = _matmul("o_out_dx", Mat(dx3b, t, d), Mat(w_oout, cd, d), "nt", [_out(t, cd, F32)], tm, _pick(cd, 1024), kd,
                   deps=(tok,))
    dw_oout, = _matmul("o_out_dw", Mat(bz, t, cd), Mat(dx3b, t, d), "tn", [_out(cd, d, BF16)], _pick(cd, 1024), _pick(d, 1024), kt)
    dproj3, dconv = _conv_bwd(proj3, conv_w, dbz, tc)
    dp3 = Mat(dproj3, t, 3 * cd, "colstack")
    dw_oin, = _matmul("o_proj_dw", Mat(h1, t, d), dp3, "tn", [_out(d, 3 * cd, BF16, "colstack", (), (N_CHIPS, d, 3 * cd // N_CHIPS))],
                      _pick(d, 2048), tn_o, kt)
    started_o, tok = pair_start("o", [dw_oin, dw_oout.reshape(N_CHIPS, cd // N_CHIPS, d)])
    dh1, = _matmul("o_proj_dx", dp3, oin, "nt", [_out(t, d, F32)], tm, _pick(d, 1024), _pick(cd, 2048), deps=(tok,))
    dx2, dx2b, dg_o = _norm_bwd("o_norm_bwd", dh1, x2, g_o, dx3, tr)
    sc_o, tok = pair_finish("o", started_o, dx2)

    dconv_s = jnp.transpose(dconv[:3].reshape(3, N_CHIPS, cd // N_CHIPS), (1, 0, 2))
    gsmall = jnp.concatenate([jnp.pad(dg_o.reshape(N_CHIPS, 1, d // N_CHIPS), ((0, 0), (0, 15), (0, 0))),
                              jnp.pad(dconv_s, ((0, 0), (0, 13), (0, 0)))], axis=1)
    dx1, dx1b, dg_m0, sc_m0, tok = mlp_bwd("0", dx2, dx2b, x1, g_m0, w1_0, w2_0, hm0, a0, act0, (tok,))

    dmixed, = _matmul("e_out_dx", Mat(dx1b, t, d), Mat(w_eout, mix, d), "nt", [_out(t, mix, F32)], tm, _pick(mix, 1024), kd,
                      deps=(tok,))
    dw_eout, = _matmul("e_out_dw", Mat(mixed, t, mix), Mat(dx1b, t, d), "tn", [_out(mix, d, BF16)], _pick(mix, 1024), _pick(d, 1024), kt)

    def mixb_fn(dm, a, s, ga, gs):
        da, dga = _rms_bwd(dm[:, :hw], a, ga)
        dsg, dgs = _rms_bwd(dm[:, hw:], s, gs)
        prod = da * a
        cols = [jnp.broadcast_to(jnp.sum(prod[:, h * LANES:(h + 1) * LANES], axis=-1, keepdims=True), (tr, LANES))
                for h in range(heads)]
        return da, dsg, jnp.stack([_row_of(c) for c in cols], axis=0), dga, dgs

    da_b, dsgu, delta_row, dg_mla, dg_sgu = _rowwise(
        "mix_norm_bwd", mixb_fn, t // tr, [_rt(dmixed, tr), _rt(att, tr), _rt(sgu, tr), _whole(g_mla), _whole(g_sgu)],
        [_rt_out(t, hw, BF16, tr), _rt_out(t, gw, F32, tr),
         (jax.ShapeDtypeStruct((heads, 8, t), F32), pl.BlockSpec((heads, 8, tr), lambda i: (0, 0, i)))],
        [jax.ShapeDtypeStruct((1, hw), F32), jax.ShapeDtypeStruct((1, gw), F32)])

    dproj, dsgu_w, dsgu_b8, dg_vn = _sgu_bwd(proj, dsgu, g_vn, sgu_w, sgu_b, groups, rb)
    dq1, dq2, dk1, dvv, dkr_h = _attn_bwd(qall, kvall, kr, da_b, lse_row, delta_row, heads, scale, min(2 * tr, t))
    dqfull, dproj = _rope_bwd(dq1, dq2, dkr_h, ctab, stab, heads, tr, dproj, kr_cb)
    dkvall = jnp.concatenate([dk1, dvv], axis=1)
    dw_q, = _matmul("q_up_dw", Mat(qn, t, ql), Mat(dqfull, t, 2 * hw), "tn", [_out(ql, 2 * hw, BF16)], ql, _pick(2 * hw, 1024), kt)
    dqn, = _matmul("q_up_dx", Mat(dqfull, t, 2 * hw), Mat(w_q_all, ql, 2 * hw), "nt", [_out(t, ql, F32)], tm, ql, _pick(2 * hw, 2048))
    dw_kv, = _matmul("kv_up_dw", Mat(kvn, t, kvl), Mat(dkvall, t, 2 * hw), "tn", [_out(kvl, 2 * hw, BF16)], kvl, _pick(2 * hw, 1024), kt)
    dkvn, = _matmul("kv_up_dx", Mat(dkvall, t, 2 * hw), Mat(w_kv_all, kvl, 2 * hw), "nt", [_out(t, kvl, F32)], tm, kvl, _pick(2 * hw, 2048))

    def qkvb_fn(da, db, a, b, ga, gb):
        dxa, dga = _rms_bwd(da, a, ga)
        dxb, dgb = _rms_bwd(db, b, gb)
        return jnp.concatenate([dxa, dxb], axis=1), dga, dgb

    assert (2 * gw) % (ql + kvl) == 0
    into = (jax.ShapeDtypeStruct(dproj.shape, dproj.dtype),
            pl.BlockSpec((tr, ql + kvl), lambda i: (i, 2 * gw // (ql + kvl))))
    dproj, dg_q, dg_kv = _rowwise(
        "qkv_norm_bwd", qkvb_fn, t // tr,
        [_rt(dqn, tr), _rt(dkvn, tr), _rt(proj, tr, ql, cq_cb), _rt(proj, tr, kvl, ckv_cb), _whole(g_q), _whole(g_kv)],
        [into], [jax.ShapeDtypeStruct((1, ql), F32), jax.ShapeDtypeStruct((1, kvl), F32)], deps=(dproj,), fill=(0, 0))
    dw_in, = _matmul("e_proj_dw", Mat(dproj, t, pi), Mat(h0, t, d), "tn", [_out(pi, d, F32)], _pick(pi, 1024), _pick(d, 2048), kt)
    dh0, = _matmul("e_proj_dx", Mat(dproj, t, pi), Mat(w_in_all, d, pi), "nt", [_out(t, d, F32)], tm, _pick(d, 1024), _pick(pi, 4096))
    dx0, _, dg_e = _norm_bwd("e_norm_bwd", dh0, xs, g_e, dx1, tr)

    kr0 = 2 * gw + c2
    gw_in = jnp.concatenate([dw_in[2 * gw:kr0], dw_in[kr0:kr0 + ROPE_HALF], dw_in[kr0 + ROPE:kr0 + ROPE + ROPE_HALF],
                             dw_in[:2 * gw]], axis=0).reshape(N_CHIPS, ei // N_CHIPS, d)
    gq = jnp.concatenate([dw_q[:, :hw].reshape(ql, heads, LANES), _unpad_rope(dw_q[:, hw:].reshape(ql, heads, LANES))], axis=-1)
    gw_uq = _stack_cols(gq.reshape(ql, heads * (LANES + ROPE)))
    gkv = jnp.concatenate([dw_kv[:, :hw].reshape(kvl, heads, LANES), dw_kv[:, hw:].reshape(kvl, heads, LANES)], axis=-1)
    gw_ukv = _stack_cols(gkv.reshape(kvl, heads * 2 * LANES))
    started_e, tok_pair = pair_start("e", [gw_in, gw_uq, gw_ukv, dw_eout.reshape(N_CHIPS, mix // N_CHIPS, d), gsmall])

    small_like = [e_norm_mix, e_q_norm, e_kv_norm, e_v_norm, e_sgu_w, e_sgu_b, e_mla_out_norm, e_sgu_out_norm, mlp_norm, final_norm]
    small_grads = [dg_e, dg_q, dg_kv, dg_vn, dsgu_w, dsgu_b8[:, 0, :], dg_mla, dg_sgu, jnp.concatenate([dg_m0, dg_m1], axis=0), dg_f]
    packed = _pack_small(small_grads)
    n_small = packed.shape[0] + (-packed.shape[0]) % 8
    pad = n_small - packed.shape[0] + 8
    sflat = jnp.concatenate([jnp.pad(packed, ((0, pad - 8), (0, 0))), jnp.pad(loss_vec, ((0, 7), (0, 0)))], axis=0)
    small_started, tok_small = _exchange_start("small_start", _all_route, 7, [sflat], [_spread(sflat)])

    sh_m1, tok = summed("m1", sc_m1, (tok_pair, tok_small))
    sc_e, tok = pair_finish("e", started_e, tok)
    sh_o, tok = summed("o", sc_o, tok)
    sh_m0, tok = summed("m0", sc_m0, tok)
    r_oin, r_oout = shared("o", sh_o, tok)
    late = {"o_w_in": _adamw(o_w_in, [r_oin], m_o_w_in, v_o_w_in),
            "o_w_out": _adamw(o_w_out, [r_oout], m_o_w_out, v_o_w_out)}
    r_w1_1, r_w2_1 = shared("m1", sh_m1, late["o_w_in"][1])
    r_w1_0, r_w2_0 = shared("m0", sh_m0, r_w2_1)
    late["mlp_w1"] = _adamw(mlp_w1, [r_w1_0, r_w1_1], m_mlp_w1, v_mlp_w1)
    sh_e, tok = summed("e", sc_e, late["mlp_w1"][1])
    late["mlp_w2"] = _adamw(mlp_w2, [r_w2_0, r_w2_1], m_mlp_w2, v_mlp_w2, deps=[tok])

    _, (all_small,) = _exchange_wait("small_wait", _all_route, small_started, late["mlp_w2"][1])
    g_small = _sum_devices(all_small)
    loss = 0.5 * jnp.sum(g_small[n_small]) / d

    def padded(arrs):
        return jnp.pad(_pack_small(arrs), ((0, pad), (0, 0)))

    s_m = [m_e_norm_mix, m_e_q_norm, m_e_kv_norm, m_e_v_norm, m_e_sgu_w, m_e_sgu_b, m_e_mla_out_norm, m_e_sgu_out_norm, m_mlp_norm, m_final_norm]
    s_v = [v_e_norm_mix, v_e_q_norm, v_e_kv_norm, v_e_v_norm, v_e_sgu_w, v_e_sgu_b, v_e_mla_out_norm, v_e_sgu_out_norm, v_mlp_norm, v_final_norm]
    s_out = [_unpack_small(o[0], small_like)
             for o in _adamw(padded(small_like)[None], [g_small], padded(s_m)[None], padded(s_v)[None])]

    r_in, r_uq, r_ukv, r_eout, r_small = shared("e", sh_e, (tok, late["mlp_w2"][1]))
    sm = [o[0] for o in _adamw(small_shard[None], [r_small], _small_shard(m_o_norm_mix, m_o_conv_w[0])[None],
                               _small_shard(v_o_norm_mix, v_o_conv_w[0])[None])]
    big = dict(late)
    flip = lambda a: jnp.swapaxes(a, 1, 2)
    big.update({
        "e_w_in": [flip(o) for o in _adamw(flip(e_w_in), [r_in], flip(m_e_w_in), flip(v_e_w_in))],
        "e_w_uq": _adamw(e_w_uq, [r_uq], m_e_w_uq, v_e_w_uq),
        "e_w_ukv": _adamw(e_w_ukv, [r_ukv], m_e_w_ukv, v_e_w_ukv),
        "e_w_out": _adamw(e_w_out, [r_eout], m_e_w_out, v_e_w_out),
    })

    names = ["e_norm_mix", "e_w_in", "e_q_norm", "e_w_uq", "e_kv_norm", "e_w_ukv", "e_v_norm", "e_sgu_w", "e_sgu_b",
             "e_mla_out_norm", "e_sgu_out_norm", "e_w_out", "o_norm_mix", "o_w_in", "o_conv_w", "o_w_out",
             "mlp_norm", "mlp_w1", "mlp_w2", "final_norm"]
    shapes = {"e_w_in": e_w_in.shape, "e_w_uq": e_w_uq.shape, "e_w_ukv": e_w_ukv.shape, "e_w_out": e_w_out.shape,
              "o_w_in": o_w_in.shape, "o_w_out": o_w_out.shape, "mlp_w1": mlp_w1.shape, "mlp_w2": mlp_w2.shape}
    small_names = ["e_norm_mix", "e_q_norm", "e_kv_norm", "e_v_norm", "e_sgu_w", "e_sgu_b", "e_mla_out_norm",
                   "e_sgu_out_norm", "mlp_norm", "final_norm"]

    def leaf(kind, name):
        if name in big:
            return big[name][kind].reshape(shapes[name])
        if name == "o_norm_mix":
            return sm[kind][0:1]
        if name == "o_conv_w":
            return sm[kind][16:19].reshape(o_conv_w.shape)
        return s_out[kind][small_names.index(name)]

    outs = [loss, dx0.reshape(x.shape)]
    for kind in range(4):
        outs += [leaf(kind, nm) for nm in names]
    return tuple(outs)


def _gcd(a, b):
    while b:
        a, b = b, a % b
    return a
```

```python
import jax
import jax.numpy as jnp
from jax import lax
from jax.experimental import pallas as pl
from jax.experimental.pallas import tpu as pltpu

F32 = jnp.float32
BF16 = jnp.bfloat16
MESH = pl.DeviceIdType.MESH

LANES = 128
ROPE = 64
ROPE_HALF = ROPE // 2
ROPE_BASE = 10000.0
EPS = 1e-6
N_CHIPS = 4
VMEM_LIMIT = 48 * 1024 * 1024
NEG = -1e30

ADAM_LR = 0.001
ADAM_B1 = 0.9
ADAM_B2 = 0.999
ADAM_EPS = 1e-08
ADAM_WD = 0.01
ADAM_STEP = 10


def _pick(n, target, step=LANES):
    best = None
    for t in range(step, min(n, target) + 1, step):
        if n % t == 0:
            best = t
    return best if best is not None else n


def _params(sem, vmem=VMEM_LIMIT):
    return pltpu.CompilerParams(dimension_semantics=sem, vmem_limit_bytes=vmem)


class Mat:
    def __init__(self, arr, rows, cols, kind="plain", lead=(), cmap=None, shape=None, dtype=None):
        self.arr, self.rows, self.cols, self.kind, self.lead, self.cmap = arr, rows, cols, kind, tuple(lead), cmap
        self.shape = tuple(arr.shape) if arr is not None else tuple(shape)
        self.dtype = arr.dtype if arr is not None else dtype

    def sds(self):
        return jax.ShapeDtypeStruct(self.shape, self.dtype)

    def spec(self, br, bc, gridmap):
        lead, nl = self.lead, len(self.lead)
        if self.kind == "plain":
            assert self.rows % br == 0 and self.cols % bc == 0, (self.shape, br, bc)
            cmap = self.cmap if self.cmap is not None else (lambda cb, _: cb)
            block = (None,) * nl + (br, bc)

            def phys(rb, cb):
                return lead + (rb, cmap(cb, bc))
        elif self.kind == "colstack":
            cs = self.shape[-1]
            assert cs % bc == 0 and self.rows % br == 0, (self.shape, br, bc)
            q = cs // bc
            block = (None,) * (nl + 1) + (br, bc)

            def phys(rb, cb):
                return (cb // q,) + lead + (rb, cb % q)
        else:
            rs = self.shape[-2]
            assert rs % br == 0 and self.cols % bc == 0, (self.shape, br, bc)
            q = rs // br
            block = (None,) * (nl + 1) + (br, bc)

            def phys(rb, cb):
                return (rb // q,) + lead + (rb % q, cb)

        return pl.BlockSpec(block, lambda *g: phys(*gridmap(*g)))


def _adamw_math(w, g, m, v):
    mn = ADAM_B1 * m + (1.0 - ADAM_B1) * g
    vn = ADAM_B2 * v + (1.0 - ADAM_B2) * jnp.square(g)
    m_hat = mn / (1.0 - ADAM_B1 ** ADAM_STEP)
    v_hat = vn / (1.0 - ADAM_B2 ** ADAM_STEP)
    return -ADAM_LR * (m_hat / (jnp.sqrt(v_hat) + ADAM_EPS) + ADAM_WD * w), mn, vn


def _matmul(name, a, b, mode, outs, tm, tn, tk, epilogue=None, extras=(), deps=()):
    if mode == "nn":
        m, k, n = a.rows, a.cols, b.cols
        a_spec = a.spec(tm, tk, lambda i, j, kk: (i, kk))
        b_spec = b.spec(tk, tn, lambda i, j, kk: (kk, j))
        dims = (((1,), (0,)), ((), ()))
    elif mode == "nt":
        m, k, n = a.rows, a.cols, b.rows
        a_spec = a.spec(tm, tk, lambda i, j, kk: (i, kk))
        b_spec = b.spec(tn, tk, lambda i, j, kk: (j, kk))
        dims = (((1,), (1,)), ((), ()))
    else:
        k, m, n = a.rows, a.cols, b.cols
        a_spec = a.spec(tk, tm, lambda i, j, kk: (kk, i))
        b_spec = b.spec(tk, tn, lambda i, j, kk: (kk, j))
        dims = (((0,), (0,)), ((), ()))
    assert m % tm == 0 and n % tn == 0 and k % tk == 0, (name, m, n, k, tm, tn, tk)
    grid = (m // tm, n // tn, k // tk)
    nk = grid[2]
    n_ex, n_out, n_dep = len(extras), len(outs), len(deps)
    tile = lambda i, j, kk: (i, j)

    def finish(z, ex, out_refs):
        vals = epilogue(z, *[e[...] for e in ex]) if epilogue is not None else (z,)
        for o, v in zip(out_refs, vals):
            o[...] = v.astype(o.dtype)

    def body_single(a_ref, b_ref, *rest):
        finish(lax.dot_general(a_ref[...], b_ref[...], dims, preferred_element_type=F32),
               rest[:n_ex], rest[n_ex + n_dep:n_ex + n_dep + n_out])

    def body_acc(a_ref, b_ref, *rest):
        acc = rest[-1]
        kk = pl.program_id(2)

        @pl.when(kk == 0)
        def _():
            acc[...] = jnp.zeros_like(acc)

        acc[...] += lax.dot_general(a_ref[...], b_ref[...], dims, preferred_element_type=F32)

        @pl.when(kk == nk - 1)
        def _():
            finish(acc[...], rest[:n_ex], rest[n_ex + n_dep:n_ex + n_dep + n_out])

    res = pl.pallas_call(
        body_single if nk == 1 else body_acc, name=name, grid=grid,
        in_specs=[a_spec, b_spec] + [e.spec(tm, tn, tile) for e in extras]
        + [pl.BlockSpec(memory_space=pl.ANY) for _ in deps],
        out_specs=[o.spec(tm, tn, tile) for o in outs],
        out_shape=[o.sds() for o in outs],
        scratch_shapes=[] if nk == 1 else [pltpu.VMEM((tm, tn), F32)],
        compiler_params=_params(("parallel", "parallel", "arbitrary")),
    )(a.arr, b.arr, *[e.arr for e in extras], *deps)
    return res


def _out(rows, cols, dtype, kind="plain", lead=(), shape=None):
    return Mat(None, rows, cols, kind, lead, shape=shape if shape is not None else (rows, cols), dtype=dtype)


def _rt(arr, tr, width=None, cb=0):
    width = arr.shape[1] if width is None else width
    return arr, pl.BlockSpec((tr, width), lambda i: (i, cb))


def _whole(arr):
    nd = arr.ndim
    return arr, pl.BlockSpec(arr.shape, lambda i: (0,) * nd)


def _rowwise(name, fn, n_steps, ins, outs, accs=(), deps=(), fill=None):
    n_in, n_out, n_acc, n_dep = len(ins), len(outs), len(accs), len(deps)

    def body(*refs):
        vals = fn(*[r[...] for r in refs[:n_in]])
        if not isinstance(vals, (tuple, list)):
            vals = (vals,)
        for ref, v in zip(refs[n_in + n_dep:n_in + n_dep + n_out], vals[:n_out]):
            ref[...] = v.astype(ref.dtype)
        if n_acc:
            acc_refs = refs[n_in + n_dep + n_out:]

            @pl.when(pl.program_id(0) == 0)
            def _():
                for ref in acc_refs:
                    ref[...] = jnp.zeros_like(ref)

            for ref, v in zip(acc_refs, vals[n_out:]):
                ref[...] += v

    acc_specs = [pl.BlockSpec(s.shape, lambda i, nd=len(s.shape): (0,) * nd) for s in accs]
    res = pl.pallas_call(
        body, name=name, grid=(n_steps,),
        in_specs=[s for _, s in ins] + [pl.BlockSpec(memory_space=pl.ANY) for _ in deps],
        out_specs=[s for _, s in outs] + acc_specs,
        out_shape=[o for o, _ in outs] + list(accs),
        input_output_aliases={} if fill is None else {n_in + fill[0]: fill[1]},
        compiler_params=_params(("arbitrary",) if n_acc else ("parallel",)),
    )(*[a for a, _ in ins], *deps)
    return res


def _rt_out(t, width, dtype, tr):
    return jax.ShapeDtypeStruct((t, width), dtype), pl.BlockSpec((tr, width), lambda i: (i, 0))


def _rms(x, g):
    r = lax.rsqrt(jnp.mean(x * x, axis=-1, keepdims=True) + EPS)
    return x * r * g


def _rms_bwd(dy, x, g):
    r = lax.rsqrt(jnp.mean(x * x, axis=-1, keepdims=True) + EPS)
    xh = x * r
    dxh = dy * g
    dx = r * (dxh - xh * jnp.mean(dxh * xh, axis=-1, keepdims=True))
    dg = jnp.sum(dy * xh, axis=0, keepdims=True)
    return dx, dg


def _gelu_and_grad(x):
    k = 0.7978845608028654
    x2 = x * x
    th = jnp.tanh(k * (x + 0.044715 * (x2 * x)))
    half = 0.5 * (1.0 + th)
    return x * half, half + 0.5 * x * (1.0 - th * th) * (k * (1.0 + 3.0 * 0.044715 * x2))


def _gelu(x):
    return _gelu_and_grad(x)[0]


def _gelu_grad(x):
    return _gelu_and_grad(x)[1]


def _norm_fwd(name, x, g, tr):
    t, d = x.shape
    return _rowwise(name, lambda xv, gv: _rms(xv, gv), t // tr, [_rt(x, tr), _whole(g)], [_rt_out(t, d, BF16, tr)])[0]


def _norm_bwd(name, dh, x, g, dres, tr):
    t, d = x.shape

    def fn(dhv, xv, gv, drv):
        dx, dg = _rms_bwd(dhv, xv, gv)
        dx = dx + drv
        return dx, dx, dg

    return _rowwise(name, fn, t // tr, [_rt(dh, tr), _rt(x, tr), _whole(g), _rt(dres, tr)],
                    [_rt_out(t, d, F32, tr), _rt_out(t, d, BF16, tr)], [jax.ShapeDtypeStruct((1, d), F32)])


def _rope_tables(posf, invf, cmask, smask, tr):
    t = posf.shape[0]

    def fn(p, f, cm, sm):
        ang = p * f
        return jnp.cos(ang) * cm, jnp.sin(ang) * sm

    return _rowwise("rope_tables", fn, t // tr, [_rt(posf, tr), _whole(invf), _whole(cmask), _whole(smask)],
                    [_rt_out(t, LANES, F32, tr), _rt_out(t, LANES, F32, tr)])


def _rot(v, c, s):
    return v * c + pltpu.roll(v, ROPE, axis=1) * s


def _rot_bwd(dv, c, s):
    return dv * c + pltpu.roll(dv * s, ROPE, axis=1)


def _rope_fwd(qfull, proj, kr_cb, ctab, stab, heads, tr):
    t = qfull.shape[0]
    hw = heads * LANES

    def fn(q, kr, c, s):
        parts = [q[:, :hw]] + [_rot(q[:, hw + h * LANES: hw + (h + 1) * LANES], c, s) for h in range(heads)]
        return jnp.concatenate(parts, axis=1), _rot(kr, c, s)

    return _rowwise("rope_fwd", fn, t // tr, [_rt(qfull, tr), _rt(proj, tr, LANES, kr_cb), _rt(ctab, tr), _rt(stab, tr)],
                    [_rt_out(t, 2 * hw, BF16, tr), _rt_out(t, LANES, BF16, tr)])


def _rope_bwd(dq1, dq2, dkr_h, ctab, stab, heads, tr, dproj, kr_cb):
    t = dq1.shape[0]
    hw = heads * LANES

    def fn(a, b, dk, c, s):
        parts = [a] + [_rot_bwd(b[:, h * LANES:(h + 1) * LANES], c, s) for h in range(heads)]
        dks = dk[0]
        for h in range(1, heads):
            dks = dks + dk[h]
        return jnp.concatenate(parts, axis=1), _rot_bwd(dks, c, s)

    dk_spec = pl.BlockSpec((heads, tr, LANES), lambda i: (0, i, 0))
    into = (jax.ShapeDtypeStruct(dproj.shape, dproj.dtype), pl.BlockSpec((tr, LANES), lambda i: (i, kr_cb)))
    return _rowwise("rope_bwd", fn, t // tr, [_rt(dq1, tr), _rt(dq2, tr), (dkr_h, dk_spec), _rt(ctab, tr), _rt(stab, tr)],
                    [_rt_out(t, 2 * hw, BF16, tr), into], deps=(dproj,), fill=(0, 1))


def _dot_nt(a, b):
    return lax.dot_general(a, b, (((1,), (1,)), ((), ())), preferred_element_type=F32)


def _dot_tn(a, b):
    return lax.dot_general(a, b, (((0,), (0,)), ((), ())), preferred_element_type=F32)


def _dot(a, b):
    return jnp.dot(a, b, preferred_element_type=F32)


def _ranges(n_blocks):
    n_var = min(4, n_blocks)
    assert n_blocks % n_var == 0
    return n_var, n_blocks // n_var


def _row_of(col):
    return col.T[:8, :]


def _attn_fwd(qall, kvall, kr, heads, scale, tq):
    t = qall.shape[0]
    nq = t // tq
    n_var, per = _ranges(nq)

    def body(qn_ref, qr_ref, kn_ref, v_ref, kr_ref, o_ref, lser_ref):
        i = pl.program_id(1)
        for var in range(n_var):
            kv = (var + 1) * per * tq

            @pl.when(jnp.logical_and(i >= var * per, i < (var + 1) * per))
            def _(kv=kv):
                s = _dot_nt(jnp.concatenate([qn_ref[...], qr_ref[...]], axis=1),
                            jnp.concatenate([kn_ref[:kv, :], kr_ref[:kv, :]], axis=1)) * scale
                rows = i * tq + lax.broadcasted_iota(jnp.int32, (tq, kv), 0)
                cols = lax.broadcasted_iota(jnp.int32, (tq, kv), 1)
                s = jnp.where(cols <= rows, s, NEG)
                m = jnp.max(s, axis=-1, keepdims=True)
                p = jnp.exp(s - m)
                l = jnp.sum(p, axis=-1, keepdims=True)
                o_ref[...] = _dot(p.astype(BF16), v_ref[:kv, :]) / l
                lser_ref[...] = _row_of(jnp.broadcast_to(m + jnp.log(l), (tq, LANES)))

    return pl.pallas_call(
        body, name="attn_fwd", grid=(heads, nq),
        in_specs=[pl.BlockSpec((tq, LANES), lambda h, i: (i, h)),
                  pl.BlockSpec((tq, LANES), lambda h, i: (i, heads + h)),
                  pl.BlockSpec((t, LANES), lambda h, i: (0, h)),
                  pl.BlockSpec((t, LANES), lambda h, i: (0, heads + h)),
                  pl.BlockSpec((t, LANES), lambda h, i: (0, 0))],
        out_specs=[pl.BlockSpec((tq, LANES), lambda h, i: (i, h)),
                   pl.BlockSpec((None, 8, tq), lambda h, i: (h, 0, i))],
        out_shape=[jax.ShapeDtypeStruct((t, heads * LANES), F32), jax.ShapeDtypeStruct((heads, 8, t), F32)],
        compiler_params=_params(("parallel", "parallel")),
    )(qall, qall, kvall, kvall, kr)


def _attn_bwd(qall, kvall, kr, do, lse_row, delta_row, heads, scale, tk):
    t = qall.shape[0]
    nk = t // tk
    n_var, per = _ranges(nk)

    def body(qn_ref, qr_ref, kn_ref, v_ref, kr_ref, do_ref, lse_ref, dl_ref, dq1_ref, dq2_ref, dk_ref, dv_ref, dkr_ref):
        j = pl.program_id(1)

        @pl.when(j == 0)
        def _():
            dq1_ref[...] = jnp.zeros_like(dq1_ref)
            dq2_ref[...] = jnp.zeros_like(dq2_ref)

        for var in range(n_var):
            q0 = var * per * tk
            nq = t - q0

            @pl.when(jnp.logical_and(j >= var * per, j < (var + 1) * per))
            def _(q0=q0, nq=nq):
                qn, qr, do_v = qn_ref[q0:, :], qr_ref[q0:, :], do_ref[q0:, :]
                k1, k2 = kn_ref[...], kr_ref[...]
                qcat, kcat = jnp.concatenate([qn, qr], axis=1), jnp.concatenate([k1, k2], axis=1)
                st = _dot_nt(kcat, qcat) * scale
                keys = j * tk + lax.broadcasted_iota(jnp.int32, (tk, nq), 0)
                queries = q0 + lax.broadcasted_iota(jnp.int32, (tk, nq), 1)
                pt = jnp.where(keys <= queries, jnp.exp(st - lse_ref[0:1, q0:]), 0.0)
                dpt = _dot_nt(v_ref[...], do_v)
                dst = (pt * (dpt - dl_ref[0:1, q0:]) * scale).astype(BF16)
                dv_ref[...] = _dot(pt.astype(BF16), do_v).astype(dv_ref.dtype)
                dkc = _dot(dst, qcat)
                dk_ref[...] = dkc[:, :LANES].astype(dk_ref.dtype)
                dkr_ref[...] = dkc[:, LANES:]
                dqc = _dot_tn(dst, kcat)
                dq1_ref[q0:, :] += dqc[:, :LANES]
                dq2_ref[q0:, :] += dqc[:, LANES:]

    kblk = lambda off: pl.BlockSpec((tk, LANES), lambda h, j: (j, off + h))
    full = lambda off: pl.BlockSpec((t, LANES), lambda h, j: (0, off + h))
    stat = pl.BlockSpec((None, 8, t), lambda h, j: (h, 0, 0))
    return pl.pallas_call(
        body, name="attn_bwd", grid=(heads, nk),
        in_specs=[full(0), full(heads), kblk(0), kblk(heads), pl.BlockSpec((tk, LANES), lambda h, j: (j, 0)),
                  full(0), stat, stat],
        out_specs=[full(0), full(0), kblk(0), kblk(0), pl.BlockSpec((None, tk, LANES), lambda h, j: (h, j, 0))],
        out_shape=[jax.ShapeDtypeStruct((t, heads * LANES), F32)] * 2 + [jax.ShapeDtypeStruct((t, heads * LANES), BF16)] * 2
        + [jax.ShapeDtypeStruct((heads, t, LANES), F32)],
        compiler_params=_params(("parallel", "arbitrary")),
    )(qall, qall, kvall, kvall, kr, do, lse_row, delta_row)


def _tril():
    return lax.broadcasted_iota(jnp.int32, (LANES, LANES), 0) >= lax.broadcasted_iota(jnp.int32, (LANES, LANES), 1)


def _group_norm(vg):
    mu = jnp.mean(vg, axis=-1, keepdims=True)
    vc = vg - mu
    rs = lax.rsqrt(jnp.mean(vc * vc, axis=-1, keepdims=True) + EPS)
    return vc * rs, rs


def _sgu_fwd(proj, gain, w, bias, groups, rb):
    t = proj.shape[0]
    gw = groups * LANES
    cpb = rb // LANES

    def body(u_ref, v_ref, gain_ref, w_ref, b_ref, s_ref):
        tril = _tril()
        for g in range(groups):
            wt = jnp.where(tril, w_ref[g], 0.0).astype(BF16)
            cols = slice(g * LANES, (g + 1) * LANES)
            for ci in range(cpb):
                rows = slice(ci * LANES, (ci + 1) * LANES)
                ug = _gelu(u_ref[rows, cols])
                vh, _ = _group_norm(_gelu(v_ref[rows, cols]))
                vn = vh * gain_ref[:, cols]
                y = _dot(wt, vn.astype(BF16)) + b_ref[g]
                s_ref[rows, cols] = ug * y

    return pl.pallas_call(
        body, name="sgu_fwd", grid=(t // rb,),
        in_specs=[pl.BlockSpec((rb, gw), lambda i: (i, 0)), pl.BlockSpec((rb, gw), lambda i: (i, 1)),
                  pl.BlockSpec((1, gw), lambda i: (0, 0)),
                  pl.BlockSpec((groups, LANES, LANES), lambda i: (0, 0, 0)),
                  pl.BlockSpec((groups, LANES, LANES), lambda i: (0, 0, 0))],
        out_specs=pl.BlockSpec((rb, gw), lambda i: (i, 0)),
        out_shape=jax.ShapeDtypeStruct((t, gw), F32),
        compiler_params=_params(("parallel",)),
    )(proj, proj, gain, w, bias)


def _sgu_bwd(proj, ds, gain, w, bias, groups, rb):
    t, width = proj.shape
    gw = groups * LANES
    cpb = rb // LANES
    n_steps = t // rb

    def body(u_ref, v_ref, ds_ref, gain_ref, w_ref, b_ref, dp_ref, dw_ref, db_ref, dg_ref, dy_acc):
        du_ref, dv_ref = dp_ref.at[:, :gw], dp_ref.at[:, gw:]
        step = pl.program_id(0)

        @pl.when(step == 0)
        def _():
            dw_ref[...] = jnp.zeros_like(dw_ref)
            dy_acc[...] = jnp.zeros_like(dy_acc)
            dg_ref[...] = jnp.zeros_like(dg_ref)

        tril = _tril()
        for g in range(groups):
            wt = jnp.where(tril, w_ref[g], 0.0).astype(BF16)
            cols = slice(g * LANES, (g + 1) * LANES)
            gain_g = gain_ref[:, cols]
            for ci in range(cpb):
                rows = slice(ci * LANES, (ci + 1) * LANES)
                u_raw, v_raw, ds_v = u_ref[rows, cols], v_ref[rows, cols], ds_ref[rows, cols]
                ug, ug_grad = _gelu_and_grad(u_raw)
                vg, vg_grad = _gelu_and_grad(v_raw)
                vh, rs = _group_norm(vg)
                vn = (vh * gain_g).astype(BF16)
                y = _dot(wt, vn) + b_ref[g]
                dy = ds_v * ug
                dyb = dy.astype(BF16)
                du_ref[rows, cols] = (ds_v * y * ug_grad).astype(du_ref.dtype)
                dy_acc[g] += dy
                dw_ref[g] += _dot_nt(dyb, vn)
                dvn = _dot_tn(wt, dyb)
                dg_ref[:, cols] += jnp.sum(dvn * vh, axis=0, keepdims=True)
                dvh = dvn * gain_g
                dvg = rs * (dvh - jnp.mean(dvh, axis=-1, keepdims=True)
                            - vh * jnp.mean(dvh * vh, axis=-1, keepdims=True))
                dv_ref[rows, cols] = (dvg * vg_grad).astype(dv_ref.dtype)

        @pl.when(step == n_steps - 1)
        def _():
            ones = jnp.ones((8, LANES), F32)
            for g in range(groups):
                dw_ref[g] = jnp.where(tril, dw_ref[g], 0.0)
                db_ref[g] = lax.dot_general(ones, dy_acc[g], (((1,), (1,)), ((), ())),
                                            precision=lax.Precision.HIGHEST, preferred_element_type=F32)

    blk = lambda cb: pl.BlockSpec((rb, gw), lambda i: (i, cb))
    whole3 = pl.BlockSpec((groups, LANES, LANES), lambda i: (0, 0, 0))
    return pl.pallas_call(
        body, name="sgu_bwd", grid=(n_steps,),
        in_specs=[blk(0), blk(1), blk(0), pl.BlockSpec((1, gw), lambda i: (0, 0)), whole3, whole3],
        out_specs=[pl.BlockSpec((rb, 2 * gw), lambda i: (i, 0)), whole3,
                   pl.BlockSpec((groups, 8, LANES), lambda i: (0, 0, 0)), pl.BlockSpec((1, gw), lambda i: (0, 0))],
        out_shape=[jax.ShapeDtypeStruct((t, width), BF16),
                   jax.ShapeDtypeStruct((groups, LANES, LANES), F32), jax.ShapeDtypeStruct((groups, 8, LANES), F32),
                   jax.ShapeDtypeStruct((1, gw), F32)],
        scratch_shapes=[pltpu.VMEM((groups, LANES, LANES), F32)],
        compiler_params=_params(("arbitrary",)),
    )(proj, proj, ds, gain, w, bias)


def _shift_down(z, s):
    rows = lax.broadcasted_iota(jnp.int32, z.shape, 0)
    return jnp.where(rows >= s, pltpu.roll(z, s, axis=0), 0.0)


def _shift_up(z, s):
    n = z.shape[0]
    rows = lax.broadcasted_iota(jnp.int32, z.shape, 0)
    return jnp.where(rows < n - s, pltpu.roll(z, n - s, axis=0), 0.0)


def _conv_fwd(proj3, cw, tc):
    _, t, cd = proj3.shape

    def body(p_ref, w_ref, o_ref):
        z = p_ref[1] * p_ref[2]
        w = w_ref[...]
        zc = w[2:3] * z + w[1:2] * _shift_down(z, 1) + w[0:1] * _shift_down(z, 2)
        o_ref[...] = (p_ref[0] * zc).astype(o_ref.dtype)

    return pl.pallas_call(
        body, name="conv_fwd", grid=(cd // tc,),
        in_specs=[pl.BlockSpec((3, t, tc), lambda j: (0, 0, j)), pl.BlockSpec((8, tc), lambda j: (0, j))],
        out_specs=pl.BlockSpec((t, tc), lambda j: (0, j)),
        out_shape=jax.ShapeDtypeStruct((t, cd), BF16),
        compiler_params=_params(("parallel",)),
    )(proj3, cw)


def _conv_bwd(proj3, cw, dbz, tc):
    _, t, cd = proj3.shape

    def body(p_ref, w_ref, d_ref, o_ref, dw_ref):
        b, c, xin = p_ref[0], p_ref[1], p_ref[2]
        w = w_ref[...]
        z = c * xin
        z1, z2 = _shift_down(z, 1), _shift_down(z, 2)
        zc = w[2:3] * z + w[1:2] * z1 + w[0:1] * z2
        d = d_ref[...]
        dzc = d * b
        dz = w[2:3] * dzc + w[1:2] * _shift_up(dzc, 1) + w[0:1] * _shift_up(dzc, 2)
        o_ref[0] = (d * zc).astype(o_ref.dtype)
        o_ref[1] = (dz * xin).astype(o_ref.dtype)
        o_ref[2] = (dz * c).astype(o_ref.dtype)
        row = lax.broadcasted_iota(jnp.int32, (8, tc), 0)
        dw0 = jnp.sum(dzc * z2, axis=0, keepdims=True)
        dw1 = jnp.sum(dzc * z1, axis=0, keepdims=True)
        dw2 = jnp.sum(dzc * z, axis=0, keepdims=True)
        dw_ref[...] = jnp.where(row == 0, dw0, 0.0) + jnp.where(row == 1, dw1, 0.0) + jnp.where(row == 2, dw2, 0.0)

    return pl.pallas_call(
        body, name="conv_bwd", grid=(cd // tc,),
        in_specs=[pl.BlockSpec((3, t, tc), lambda j: (0, 0, j)), pl.BlockSpec((8, tc), lambda j: (0, j)),
                  pl.BlockSpec((t, tc), lambda j: (0, j))],
        out_specs=[pl.BlockSpec((3, t, tc), lambda j: (0, 0, j)), pl.BlockSpec((8, tc), lambda j: (0, j))],
        out_shape=[jax.ShapeDtypeStruct((3, t, cd), BF16), jax.ShapeDtypeStruct((8, cd), F32)],
        compiler_params=_params(("parallel",)),
    )(proj3, cw, dbz)


def _place():
    x, y, c = lax.axis_index("x"), lax.axis_index("y"), lax.axis_index("c")
    chips = [(1 - x, y), (x, 1 - y), (1 - x, 1 - y)]
    return x, y, c, chips


def _any_specs(n):
    return [pl.BlockSpec(memory_space=pl.ANY) for _ in range(n)]


HBM_SPEC = pl.BlockSpec(memory_space=pltpu.HBM)
SEM_SPEC = pl.BlockSpec(memory_space=pltpu.SEMAPHORE)
ORDERED_EFFECT = pltpu.SideEffectType.DATAFLOW_SIDE_EFFECTING


def _in_hbm(a):
    return pltpu.with_memory_space_constraint(a, pltpu.HBM)


def _token():
    return jax.ShapeDtypeStruct((8, LANES), F32), pl.BlockSpec(memory_space=pltpu.VMEM)


def _gather_start(name, groups):
    sizes = [len(g) for g in groups]
    flat = [b for g in groups for b in g]
    n, ng = len(flat), len(groups)

    def body(*refs):
        ins, sems, token = refs[:n], refs[n:n + 2 * ng], refs[-1]
        x, y, c, chips = _place()
        me = 2 * x + y
        i = 0
        for gi, size in enumerate(sizes):
            for j in range(size):
                blk = ins[i].at[me, c]
                for k, chip in enumerate(chips):
                    pltpu.make_async_remote_copy(src_ref=blk, dst_ref=blk, send_sem=sems[2 * gi].at[3 * j + k],
                                                 recv_sem=sems[2 * gi + 1].at[3 * j + k],
                                                 device_id=(*chip, c), device_id_type=MESH).start()
                i += 1
        token[...] = jnp.zeros_like(token)

    tok_shape, tok_spec = _token()
    res = pl.pallas_call(
        body, name=name,
        in_specs=[HBM_SPEC] * n,
        out_specs=[SEM_SPEC] * (2 * ng) + [HBM_SPEC] * n + [tok_spec],
        out_shape=[pltpu.SemaphoreType.DMA((3 * size,)) for size in sizes for _ in (0, 1)]
        + [pltpu.HBM(b.shape, b.dtype) for b in flat] + [tok_shape],
        input_output_aliases={i: 2 * ng + i for i in range(n)},
        compiler_params=pltpu.CompilerParams(has_side_effects=ORDERED_EFFECT),
    )(*[_in_hbm(b) for b in flat])
    out, i = [], 2 * ng
    for gi, size in enumerate(sizes):
        out.append((res[2 * gi], res[2 * gi + 1], list(res[i:i + size])))
        i += size
    return out, res[-1]


def _gather_wait(tag, send, recv, bufs, after):
    n = len(bufs)
    after = tuple(after) if isinstance(after, (tuple, list)) else (after,)

    def body(*refs):
        ins, send_ref, recv_ref = refs[:n], refs[n], refs[n + 1]
        x, y, c, chips = _place()
        me = 2 * x + y
        for j in range(n):
            for k, (px, py) in enumerate(chips):
                cp = pltpu.make_async_remote_copy(src_ref=ins[j].at[me, c], dst_ref=ins[j].at[2 * px + py, c],
                                                  send_sem=send_ref.at[3 * j + k], recv_sem=recv_ref.at[3 * j + k],
                                                  device_id=(px, py, c), device_id_type=MESH)
                cp.wait_send()
                cp.wait_recv()

    return pl.pallas_call(
        body, name="gather_wait_" + tag,
        in_specs=[HBM_SPEC] * n + [SEM_SPEC, SEM_SPEC] + _any_specs(len(after)),
        out_specs=[HBM_SPEC] * n,
        out_shape=[pltpu.HBM(b.shape, b.dtype) for b in bufs],
        input_output_aliases={i: i for i in range(n)},
        compiler_params=pltpu.CompilerParams(has_side_effects=ORDERED_EFFECT),
    )(*bufs, send, recv, *after)


def _gather_forward(tag, bufs):
    n = len(bufs)

    def body(*refs):
        ins, outs = refs[:n], refs[n:2 * n]
        send, recv = refs[2 * n:]
        x, y, c, chips = _place()
        sib = (x, y, 1 - c)

        def cp(i, k, slot, half):
            return pltpu.make_async_remote_copy(src_ref=ins[i].at[slot, half], dst_ref=outs[i].at[slot, half],
                                                send_sem=send.at[3 * i + k], recv_sem=recv.at[3 * i + k],
                                                device_id=sib, device_id_type=MESH)

        cps = [cp(i, k, 2 * px + py, c) for i in range(n) for k, (px, py) in enumerate(chips)]
        for d in cps:
            d.start()
        for i in range(n):
            for k, (px, py) in enumerate(chips):
                cp(i, k, 2 * px + py, 1 - c).wait_recv()
        for d in cps:
            d.wait_send()

    return pl.pallas_call(
        body, name="gather_forward_" + tag,
        in_specs=_any_specs(n), out_specs=_any_specs(n),
        out_shape=[jax.ShapeDtypeStruct(b.shape, b.dtype) for b in bufs],
        scratch_shapes=[pltpu.SemaphoreType.DMA((3 * n,))] * 2,
        input_output_aliases={i: i for i in range(n)},
        compiler_params=pltpu.CompilerParams(has_side_effects=True),
    )(*bufs)


def _pair_route(srcs, zones):
    x, y, c, _ = _place()
    return [(srcs[i].at[j, 1 - c], zones[i].at[j], (x, y, 1 - c)) for i in range(len(srcs)) for j in range(N_CHIPS)]


def _slab_route(srcs, zones):
    x, y, c, _ = _place()
    return [(srcs[i].at[j], zones[i].at[j], (x, y, 1 - c)) for i in range(len(srcs)) for j in range(N_CHIPS)]


def _chip_route(srcs, zones):
    x, y, c, chips = _place()
    return [(srcs[i].at[2 * px + py], zones[i].at[k], (px, py, c)) for i in range(len(srcs)) for k, (px, py) in enumerate(chips)]


def _all_route(srcs, zones):
    x, y, c, _ = _place()
    flips = [(fx, fy, fc) for fx in (0, 1) for fy in (0, 1) for fc in (0, 1)][1:]
    return [(srcs[0], zones[0].at[4 * x + 2 * y + c], (x + fx - 2 * x * fx, y + fy - 2 * y * fy, c + fc - 2 * c * fc))
            for fx, fy, fc in flips]


def _share_route(srcs, zones):
    x, y, c, _ = _place()
    return [(s.at[c], s.at[c], (x, y, 1 - c)) for s in srcs]


def _exchange_start(name, route, n_copies, srcs, zones):
    n, nz = len(srcs), len(zones)
    lands = [lax.empty(z, a.dtype) if isinstance(z, tuple) else z for z, a in zip(zones, srcs)]

    def body(*refs):
        ins, zone_refs, send, recv, token = refs[:n], refs[n:n + nz], refs[n + nz], refs[n + nz + 1], refs[-1]
        for k, (src, dst, dev) in enumerate(route(ins, zone_refs)):
            pltpu.make_async_remote_copy(src_ref=src, dst_ref=dst, send_sem=send.at[k], recv_sem=recv.at[k],
                                         device_id=dev, device_id_type=MESH).start()
        token[...] = jnp.zeros_like(token)

    tok_shape, tok_spec = _token()
    res = pl.pallas_call(
        body, name=name,
        in_specs=[HBM_SPEC] * (n + nz),
        out_specs=[SEM_SPEC, SEM_SPEC] + [HBM_SPEC] * (n + nz) + [tok_spec],
        out_shape=[pltpu.SemaphoreType.DMA((n_copies,))] * 2 + [pltpu.HBM(a.shape, a.dtype) for a in srcs + lands]
        + [tok_shape],
        input_output_aliases={i: 2 + i for i in range(n + nz)},
        compiler_params=pltpu.CompilerParams(has_side_effects=ORDERED_EFFECT),
    )(*[_in_hbm(a) for a in srcs + lands])
    return (res[0], res[1], list(res[2:2 + n]), list(res[2 + n:2 + n + nz])), res[-1]


def _exchange_wait(name, route, started, after):
    send, recv, srcs, lands = started
    n, nz = len(srcs), len(lands)
    after = tuple(after) if isinstance(after, (tuple, list)) else (after,)

    def body(*refs):
        ins, zone_refs, send_ref, recv_ref = refs[:n], refs[n:n + nz], refs[n + nz], refs[n + nz + 1]
        for k, (src, dst, dev) in enumerate(route(ins, zone_refs)):
            cp = pltpu.make_async_remote_copy(src_ref=src, dst_ref=dst, send_sem=send_ref.at[k], recv_sem=recv_ref.at[k],
                                              device_id=dev, device_id_type=MESH)
            cp.wait_send()
            cp.wait_recv()

    res = pl.pallas_call(
        body, name=name,
        in_specs=[HBM_SPEC] * (n + nz) + [SEM_SPEC, SEM_SPEC] + _any_specs(len(after)),
        out_specs=[HBM_SPEC] * (n + nz),
        out_shape=[pltpu.HBM(a.shape, a.dtype) for a in srcs + lands],
        input_output_aliases={i: i for i in range(n + nz)},
        compiler_params=pltpu.CompilerParams(has_side_effects=ORDERED_EFFECT),
    )(*srcs, *lands, send, recv, *after)
    return list(res[:n]), list(res[n:])


def _spread(v):
    rows, cols = v.shape
    tr = _row_tile(rows, cols, budget=256 * 1024)

    def body(v_ref, o_ref):
        o_ref[...] = jnp.broadcast_to(v_ref[...][None], o_ref.shape)

    return pl.pallas_call(body, name="spread_small_grads", grid=(rows // tr,),
                          in_specs=[pl.BlockSpec((tr, cols), lambda r: (r, 0))],
                          out_specs=pl.BlockSpec((8, tr, cols), lambda r: (0, r, 0)),
                          out_shape=jax.ShapeDtypeStruct((8, rows, cols), v.dtype),
                          compiler_params=_params(("parallel",)))(v)


def _row_tile(rows, cols, itemsize=4, budget=2 * 1024 * 1024, step=8):
    best = None
    for t in range(step, rows + 1, step):
        if rows % t == 0 and t * cols * itemsize <= budget:
            best = t
    return best if best is not None else rows


def _my_chip():
    return 2 * lax.axis_index("x") + lax.axis_index("y")


def _pair_sum(g5, gsib):
    _, _, rh, cols = g5.shape
    tr = _row_tile(rh, cols, step=16)

    def body(a_ref, b_ref, o_ref):
        o_ref[...] = (a_ref[...].astype(F32) + b_ref[...].astype(F32)).astype(o_ref.dtype)

    return pl.pallas_call(body, name="grad_pair_sum", grid=(N_CHIPS, rh // tr),
                          in_specs=[pl.BlockSpec((None, None, tr, cols), lambda j, r: (j, lax.axis_index("c"), r, 0)),
                                    pl.BlockSpec((None, tr, cols), lambda j, r: (j, r, 0))],
                          out_specs=pl.BlockSpec((None, tr, cols), lambda j, r: (j, r, 0)),
                          out_shape=jax.ShapeDtypeStruct((N_CHIPS, rh, cols), BF16),
                          compiler_params=_params(("parallel", "parallel")))(g5, gsib)


def _chip_sum(part, recv):
    _, rh, cols = part.shape
    tr = _row_tile(rh, cols, step=16)

    def body(a_ref, b_ref, o_ref):
        acc = a_ref[...].astype(F32)
        for k in range(3):
            acc = acc + b_ref[k].astype(F32)
        o_ref[...] = acc

    return pl.pallas_call(body, name="grad_chip_sum", grid=(rh // tr,),
                          in_specs=[pl.BlockSpec((None, tr, cols), lambda r: (_my_chip(), r, 0)),
                                    pl.BlockSpec((3, tr, cols), lambda r: (0, r, 0))],
                          out_specs=pl.BlockSpec((None, tr, cols), lambda r: (lax.axis_index("c"), r, 0)),
                          out_shape=jax.ShapeDtypeStruct((2, rh, cols), F32),
                          compiler_params=_params(("parallel",)))(part, recv)


def _sum_devices(g):
    _, rows, cols = g.shape
    tr = _row_tile(rows, cols, budget=256 * 1024)

    def body(g_ref, o_ref):
        acc = g_ref[0]
        for d in range(1, 8):
            acc = acc + g_ref[d]
        o_ref[...] = acc

    return pl.pallas_call(body, name="sum_small_grads", grid=(rows // tr,),
                          in_specs=[pl.BlockSpec((8, tr, cols), lambda r: (0, r, 0))],
                          out_specs=pl.BlockSpec((tr, cols), lambda r: (r, 0)),
                          out_shape=jax.ShapeDtypeStruct((rows, cols), F32),
                          compiler_params=_params(("parallel",)))(g)


def _place_shard(w, layer, dtype, deps=()):
    _, rows, cols = w.shape
    tr = _row_tile(rows, cols)

    def body(i_ref, *rest):
        o_ref = rest[-1]
        o_ref[...] = i_ref[...].astype(o_ref.dtype)

    out = pl.pallas_call(body, name="place_shard", grid=(rows // tr,),
                         in_specs=[pl.BlockSpec((None, tr, cols), lambda r: (layer, r, 0))] + _any_specs(len(deps)),
                         out_specs=pl.BlockSpec((None, tr, cols), lambda r: (_my_chip(), r, 0)),
                         out_shape=jax.ShapeDtypeStruct((N_CHIPS, rows, cols), dtype),
                         compiler_params=_params(("parallel",)))(w, *deps)
    return out.reshape(N_CHIPS, 2, rows // 2, cols)


def _adamw(w, gs, m, v, deps=()):
    n_layers, rows, cols = w.shape
    tr = _row_tile(rows, cols, budget=1024 * 1024)

    def body(w_ref, m_ref, v_ref, *rest):
        g_refs = rest[:n_layers]
        go_ref, d_ref, mo_ref, vo_ref = rest[-4:]
        gv = g_refs[0][...]
        for layer in range(1, n_layers):
            gv = jnp.where(pl.program_id(0) == layer, g_refs[layer][...], gv)
        d_ref[...], mo_ref[...], vo_ref[...] = _adamw_math(w_ref[...], gv, m_ref[...], v_ref[...])
        go_ref[...] = gv

    spec = pl.BlockSpec((None, tr, cols), lambda layer, r: (layer, r, 0))
    g_specs = [pl.BlockSpec((tr, cols), lambda layer, r, own=own: (jnp.where(layer == own, r, 0), 0))
               for own in range(n_layers)]
    return pl.pallas_call(body, name="adamw", grid=(n_layers, rows // tr),
                          in_specs=[spec] * 3 + g_specs + _any_specs(len(deps)),
                          out_specs=[spec] * 4, out_shape=[jax.ShapeDtypeStruct((n_layers, rows, cols), F32)] * 4,
                          compiler_params=_params(("parallel", "parallel")))(w, m, v, *gs, *deps)


def _pad_rope(w):
    z = jnp.zeros(w.shape[:-1] + (ROPE_HALF,), w.dtype)
    return jnp.concatenate([w[..., :ROPE_HALF], z, w[..., ROPE_HALF:], z], axis=-1)


def _unpad_rope(g):
    return jnp.concatenate([g[..., :ROPE_HALF], g[..., ROPE:ROPE + ROPE_HALF]], axis=-1)


def _unstack_cols(s):
    n, r, cs = s.shape
    return jnp.transpose(s, (1, 0, 2)).reshape(r, n * cs)


def _stack_cols(f):
    r, cfull = f.shape
    return jnp.transpose(f.reshape(r, N_CHIPS, cfull // N_CHIPS), (1, 0, 2))


def _small_shard(norm, conv):
    return jnp.concatenate([jnp.pad(norm, ((0, 15), (0, 0))), jnp.pad(conv, ((0, 13), (0, 0)))], axis=0)


def _flat_rows(a):
    return a.reshape(-1, LANES)


def _pack_small(arrs):
    return jnp.concatenate([_flat_rows(a.astype(F32)) for a in arrs], axis=0)


def _unpack_small(flat, like):
    out, r = [], 0
    for a in like:
        n = a.size // LANES
        out.append(flat[r:r + n].reshape(a.shape))
        r += n
    return out


def kernel(x, positions, e_norm_mix, e_w_in, e_q_norm, e_w_uq, e_kv_norm, e_w_ukv, e_v_norm, e_sgu_w, e_sgu_b, e_mla_out_norm, e_sgu_out_norm, e_w_out, o_norm_mix, o_w_in, o_conv_w, o_w_out, mlp_norm, mlp_w1, mlp_w2, final_norm, loss_target, m_e_norm_mix, m_e_w_in, m_e_q_norm, m_e_w_uq, m_e_kv_norm, m_e_w_ukv, m_e_v_norm, m_e_sgu_w, m_e_sgu_b, m_e_mla_out_norm, m_e_sgu_out_norm, m_e_w_out, m_o_norm_mix, m_o_w_in, m_o_conv_w, m_o_w_out, m_mlp_norm, m_mlp_w1, m_mlp_w2, m_final_norm, v_e_norm_mix, v_e_w_in, v_e_q_norm, v_e_w_uq, v_e_kv_norm, v_e_w_ukv, v_e_v_norm, v_e_sgu_w, v_e_sgu_b, v_e_mla_out_norm, v_e_sgu_out_norm, v_e_w_out, v_o_norm_mix, v_o_w_in, v_o_conv_w, v_o_w_out, v_mlp_norm, v_mlp_w1, v_mlp_w2, v_final_norm):
    t, d = x.shape[1], x.shape[2]
    ql, kvl = e_q_norm.shape[1], e_kv_norm.shape[1]
    groups = e_v_norm.shape[1]
    gw = groups * LANES
    heads = N_CHIPS * e_w_uq.shape[2] // (LANES + ROPE)
    hw = heads * LANES
    mix = hw + gw
    ei = N_CHIPS * e_w_in.shape[2]
    cd = N_CHIPS * o_conv_w.shape[2]
    ff = N_CHIPS * mlp_w1.shape[2]
    ffs = ff // N_CHIPS
    pi = 2 * gw + ql + kvl + LANES
    assert e_norm_mix.shape[0] == 1 and o_norm_mix.shape[0] == 1 and mlp_norm.shape[0] == 2
    assert ei == ql + kvl + ROPE + 2 * gw and cd == d and e_sgu_w.shape[2] == LANES
    assert (2 * gw) % ql == 0 and (2 * gw + ql) % kvl == 0 and t % LANES == 0
    scale = (LANES + ROPE) ** -0.5

    tr = min(256, t)
    tm = _pick(t, 1024, 8)
    kt, kd = _pick(t, 2048, 8), _pick(d, 2048)
    xs = x.reshape(t, d)
    tgt = loss_target.reshape(t, d)

    small_shard = _small_shard(o_norm_mix, o_conv_w[0])
    first, tok = _gather_start("gather_start_e", [
        [_place_shard(e_w_in, 0, BF16)],
        [_place_shard(e_w_uq, 0, BF16), _place_shard(e_w_ukv, 0, BF16), _place_shard(e_w_out, 0, BF16),
         _place_shard(small_shard[None], 0, F32)]])
    rest, tok = _gather_start("gather_start_rest", [
        [_place_shard(mlp_w1, 0, BF16, (tok,))], [_place_shard(mlp_w2, 0, BF16, (tok,))],
        [_place_shard(o_w_in, 0, BF16, (tok,)), _place_shard(o_w_out, 0, BF16, (tok,))],
        [_place_shard(mlp_w1, 1, BF16, (tok,))], [_place_shard(mlp_w2, 1, BF16, (tok,))]])
    started = first + rest

    def gathered(gi, tag, after):
        send, recv, bufs = started[gi]
        bufs = _gather_forward(tag, _gather_wait(tag, send, recv, bufs, after))
        return [b.reshape(N_CHIPS, 2 * b.shape[2], b.shape[3]) for b in bufs]

    g_e = e_norm_mix
    h0 = _norm_fwd("e_norm", xs, g_e, tr)
    inv_freq = ROPE_BASE ** (-jnp.arange(0, ROPE, 2, dtype=F32) / ROPE)
    zeros32 = jnp.zeros((ROPE_HALF,), F32)
    ones32 = jnp.ones((ROPE_HALF,), F32)
    invf = jnp.concatenate([inv_freq, zeros32, inv_freq, zeros32]).reshape(1, LANES)
    cmask = jnp.concatenate([ones32, zeros32, ones32, zeros32]).reshape(1, LANES)
    smask = jnp.concatenate([-ones32, zeros32, ones32, zeros32]).reshape(1, LANES)
    ctab, stab = _rope_tables(positions.reshape(t, 1).astype(F32), invf, cmask, smask, tr)

    w_in_g, = gathered(0, "e_in", (h0, ctab, tok))
    full = _unstack_cols(w_in_g)
    c2, c3 = ql + kvl, ql + kvl + ROPE
    w_in_all = jnp.concatenate([full[:, c3:], full[:, :c2], _pad_rope(full[:, c2:c3])], axis=1)
    proj, = _matmul("e_proj", Mat(h0, t, d), Mat(w_in_all, d, pi), "nn", [_out(t, pi, F32)], tm, _pick(pi, 1024), kd)

    w_uq_g, w_ukv_g, w_eout_g, small_g = gathered(1, "e", proj)
    full = _unstack_cols(w_uq_g).reshape(ql, heads, LANES + ROPE)
    w_q_all = jnp.concatenate([full[:, :, :LANES].reshape(ql, hw), _pad_rope(full[:, :, LANES:]).reshape(ql, hw)], axis=1)
    full = _unstack_cols(w_ukv_g).reshape(kvl, heads, 2 * LANES)
    w_kv_all = jnp.concatenate([full[:, :, :LANES].reshape(kvl, hw), full[:, :, LANES:].reshape(kvl, hw)], axis=1)
    w_eout = w_eout_g.reshape(mix, d)
    g_o = small_g[:, 0].reshape(1, d)
    conv_w = jnp.pad(jnp.transpose(small_g[:, 16:19], (1, 0, 2)).reshape(3, cd), ((0, 5), (0, 0)))

    g_q, g_kv = e_q_norm, e_kv_norm
    g_vn = e_v_norm.reshape(1, gw)
    sgu_w = e_sgu_w[0]
    sgu_b = jnp.broadcast_to(e_sgu_b[0][:, :, None], (groups, LANES, LANES))
    g_mla, g_sgu = e_mla_out_norm, e_sgu_out_norm
    g_m0, g_m1 = mlp_norm[0:1], mlp_norm[1:2]
    g_f = final_norm.reshape(1, d)

    def mlp_fwd(tag, xin, g, gi):
        hm = _norm_fwd("mlp_norm_" + tag, xin, g, tr)
        tn = _pick(ffs, 1024)
        w1 = Mat(gathered(gi, "w1_" + tag, hm)[0], d, ff, "colstack")
        a, act = _matmul("mlp_up_" + tag, Mat(hm, t, d), w1, "nn",
                         [_out(t, ff, BF16), _out(t, ff, BF16)], tm, tn, kd,
                         epilogue=lambda z: (jnp.maximum(z, 0.0), jnp.square(jnp.maximum(z, 0.0))))
        w2 = Mat(gathered(gi + 1, "w2_" + tag, act)[0].reshape(ff, d), ff, d)
        xo, = _matmul("mlp_down_" + tag, Mat(act, t, ff), w2, "nn",
                      [_out(t, d, F32)], tm, _pick(d, 1024), _pick(ffs, 2048),
                      epilogue=lambda z, r: (z + r,), extras=[Mat(xin, t, d)])
        return xo, hm, a, act, w1, w2

    def chip_start(tag, part):
        return _exchange_start("scatter_start_" + tag, _chip_route, 3 * len(part), part, [(3,) + p.shape[1:] for p in part])

    def pair_start(tag, stacked):
        g5 = [g.reshape(N_CHIPS, 2, g.shape[1] // 2, g.shape[2]) for g in stacked]
        return _exchange_start("pair_start_" + tag, _pair_route, N_CHIPS * len(g5), g5,
                               [(N_CHIPS,) + g.shape[2:] for g in g5])

    def pair_finish(tag, started, after):
        g5, from_sib = _exchange_wait("pair_wait_" + tag, _pair_route, started, after)
        return chip_start(tag, [_pair_sum(a, b) for a, b in zip(g5, from_sib)])

    def summed(tag, sc, after):
        part, lands = _exchange_wait("scatter_wait_" + tag, _chip_route, sc, after)
        half = [_chip_sum(p, r) for p, r in zip(part, lands)]
        return _exchange_start("share_start_" + tag, _share_route, len(half), half, [])

    def shared(tag, started, after):
        bufs, _ = _exchange_wait("share_wait_" + tag, _share_route, started, after)
        return [r.reshape(2 * r.shape[1], r.shape[2]) for r in bufs]

    def mlp_bwd(tag, dx, dxb, xin, g, w1, w2, hm, a, act, deps):
        tn = _pick(ffs, 1024)
        hr, hd = ffs // 2, d // 2
        dz, = _matmul("mlp_dact_" + tag, Mat(dxb, t, d), w2, "nt",
                      [_out(t, ff, BF16)], tm, tn, kd,
                      epilogue=lambda z, av: (z * (2.0 * av.astype(F32)),), extras=[Mat(a, t, ff)], deps=deps)

        def half(own):
            c = lax.axis_index("c")
            return c if own else 1 - c

        def act_half(own):
            return Mat(act, t, ff // 2, cmap=lambda cb, bc: (cb // (hr // bc)) * (ffs // bc) + half(own) * (hr // bc)
                       + cb % (hr // bc))

        def hm_half(own):
            return Mat(hm, t, hd, cmap=lambda cb, bc: cb + half(own) * (hd // bc))

        w1_out = lambda: _out(hd, ff, BF16, "colstack", (), (N_CHIPS, hd, ffs))
        theirs2, = _matmul("mlp_dw2_theirs_" + tag, act_half(False), Mat(dxb, t, d), "tn",
                           [_out(ff // 2, d, BF16)], _pick(hr, 1024), _pick(d, 2048), kt)
        theirs1, = _matmul("mlp_dw1_theirs_" + tag, hm_half(False), Mat(dz, t, ff), "tn",
                           [w1_out()], _pick(hd, 2048), tn, kt)
        sent = [theirs1, theirs2.reshape(N_CHIPS, hr, d)]
        started, tok = _exchange_start("pair_start_m" + tag, _slab_route, N_CHIPS * 2, sent, [s.shape for s in sent])
        dhm, = _matmul("mlp_dh_" + tag, Mat(dz, t, ff), w1, "nt",
                       [_out(t, d, F32)], tm, _pick(d, 1024), _pick(ffs, 2048), deps=(tok,))
        dxo, dxob, dg = _norm_bwd("mlp_norm_bwd_" + tag, dhm, xin, g, dx, tr)
        _, (sib1, sib2) = _exchange_wait("pair_wait_m" + tag, _slab_route, started, dxo)
        add = lambda z, s: (z + s.astype(F32),)
        part2, = _matmul("mlp_dw2_mine_" + tag, act_half(True), Mat(dxb, t, d), "tn",
                         [_out(ff // 2, d, BF16)], _pick(hr, 1024), _pick(d, 2048), kt,
                         epilogue=add, extras=[Mat(sib2.reshape(ff // 2, d), ff // 2, d)])
        part1, = _matmul("mlp_dw1_mine_" + tag, hm_half(True), Mat(dz, t, ff), "tn",
                         [w1_out()], _pick(hd, 2048), tn, kt, epilogue=add, extras=[Mat(sib1, hd, ff, "colstack")])
        sc, tok = chip_start("m" + tag, [part1, part2.reshape(N_CHIPS, hr, d)])
        return dxo, dxob, dg, sc, tok

    cq_cb, ckv_cb, kr_cb = 2 * gw // ql, (2 * gw + ql) // kvl, (2 * gw + ql + kvl) // LANES
    qn, kvn = _rowwise("qkv_norm", lambda a, b, ga, gb: (_rms(a, ga), _rms(b, gb)), t // tr,
                       [_rt(proj, tr, ql, cq_cb), _rt(proj, tr, kvl, ckv_cb), _whole(g_q), _whole(g_kv)],
                       [_rt_out(t, ql, BF16, tr), _rt_out(t, kvl, BF16, tr)])
    qfull, = _matmul("q_up", Mat(qn, t, ql), Mat(w_q_all, ql, 2 * hw), "nn", [_out(t, 2 * hw, F32)], tm, _pick(2 * hw, 1024), ql)
    kvall, = _matmul("kv_up", Mat(kvn, t, kvl), Mat(w_kv_all, kvl, 2 * hw), "nn", [_out(t, 2 * hw, BF16)], tm, _pick(2 * hw, 1024), kvl)
    qall, kr = _rope_fwd(qfull, proj, kr_cb, ctab, stab, heads, tr)
    att, lse_row = _attn_fwd(qall, kvall, kr, heads, scale, tr)
    rb = min(2 * LANES, t)
    sgu = _sgu_fwd(proj, g_vn, sgu_w, sgu_b, groups, rb)
    mixed = _rowwise("mix_norm", lambda a, s, ga, gs: jnp.concatenate([_rms(a, ga), _rms(s, gs)], axis=1), t // tr,
                     [_rt(att, tr), _rt(sgu, tr), _whole(g_mla), _whole(g_sgu)], [_rt_out(t, mix, BF16, tr)])[0]
    x1, = _matmul("e_out", Mat(mixed, t, mix), Mat(w_eout, mix, d), "nn", [_out(t, d, F32)], tm, _pick(d, 1024), _pick(mix, 2048),
                  epilogue=lambda z, r: (z + r,), extras=[Mat(xs, t, d)])
    x2, hm0, a0, act0, w1_0, w2_0 = mlp_fwd("0", x1, g_m0, 2)

    w_oin_g, w_oout_g = gathered(4, "o", x2)
    w_oout = w_oout_g.reshape(cd, d)
    h1 = _norm_fwd("o_norm", x2, g_o, tr)
    oin = Mat(_unstack_cols(w_oin_g), d, 3 * cd)
    tn_o = _pick(_gcd(3 * cd // N_CHIPS, cd), 512)
    proj3, = _matmul("o_proj", Mat(h1, t, d), oin, "nn", [_out(t, 3 * cd, F32, "colstack", (), (3, t, cd))],
                     tm, _pick(cd, 1024), kd)
    tc = _pick(cd, 256)
    bz = _conv_fwd(proj3, conv_w, tc)
    x3, = _matmul("o_out", Mat(bz, t, cd), Mat(w_oout, cd, d), "nn", [_out(t, d, F32)], tm, _pick(d, 1024), _pick(cd, 2048),
                  epilogue=lambda z, r: (z + r,), extras=[Mat(x2, t, d)])
    x4, hm1, a1, act1, w1_1, w2_1 = mlp_fwd("1", x3, g_m1, 5)

    def final_fn(xv, gv, tv):
        r = lax.rsqrt(jnp.mean(xv * xv, axis=-1, keepdims=True) + EPS)
        xh = xv * r
        err = xh * gv - tv
        dy = err * (1.0 / d)
        dxh = dy * gv
        dx = r * (dxh - xh * jnp.mean(dxh * xh, axis=-1, keepdims=True))
        sq = jnp.sum(err * err, axis=0, keepdims=True)
        part = sq[:, :LANES]
        for k in range(1, d // LANES):
            part = part + sq[:, k * LANES:(k + 1) * LANES]
        return dx, dx, part, jnp.sum(dy * xh, axis=0, keepdims=True)

    dx4, dx4b, loss_vec, dg_f = _rowwise("loss_final_norm", final_fn, t // tr, [_rt(x4, tr), _whole(g_f), _rt(tgt, tr)],
                                         [_rt_out(t, d, F32, tr), _rt_out(t, d, BF16, tr)],
                                         [jax.ShapeDtypeStruct((1, LANES), F32), jax.ShapeDtypeStruct((1, d), F32)])

    dx3, dx3b, dg_m1, sc_m1, tok = mlp_bwd("1", dx4, dx4b, x3, g_m1, w1_1, w2_1, hm1, a1, act1, ())

    dbz, = _matmul("o_out_dx", Mat(dx3b, t, d), Mat(w_oout, cd, d), "nt", [_out(t, cd, F32)], tm, _pick(cd, 1024), kd,
                   deps=(tok,))
    dw_oout, = _matmul("o_out_dw", Mat(bz, t, cd), Mat(dx3b, t, d), "tn", [_out(cd, d, BF16)], _pick(cd, 1024), _pick(d, 1024), kt)
    dproj3, dconv = _conv_bwd(proj3, conv_w, dbz, tc)
    dp3 = Mat(dproj3, t, 3 * cd, "colstack")
    dw_oin, = _matmul("o_proj_dw", Mat(h1, t, d), dp3, "tn", [_out(d, 3 * cd, BF16, "colstack", (), (N_CHIPS, d, 3 * cd // N_CHIPS))],
                      _pick(d, 2048), tn_o, kt)
    started_o, tok = pair_start("o", [dw_oin, dw_oout.reshape(N_CHIPS, cd // N_CHIPS, d)])
    dh1, = _matmul("o_proj_dx", dp3, oin, "nt", [_out(t, d, F32)], tm, _pick(d, 1024), _pick(cd, 2048), deps=(tok,))
    dx2, dx2b, dg_o = _norm_bwd("o_norm_bwd", dh1, x2, g_o, dx3, tr)
    sc_o, tok = pair_finish("o", started_o, dx2)

    dconv_s = jnp.transpose(dconv[:3].reshape(3, N_CHIPS, cd // N_CHIPS), (1, 0, 2))
    gsmall = jnp.concatenate([jnp.pad(dg_o.reshape(N_CHIPS, 1, d // N_CHIPS), ((0, 0), (0, 15), (0, 0))),
                              jnp.pad(dconv_s, ((0, 0), (0, 13), (0, 0)))], axis=1)
    dx1, dx1b, dg_m0, sc_m0, tok = mlp_bwd("0", dx2, dx2b, x1, g_m0, w1_0, w2_0, hm0, a0, act0, (tok,))

    dmixed, = _matmul("e_out_dx", Mat(dx1b, t, d), Mat(w_eout, mix, d), "nt", [_out(t, mix, F32)], tm, _pick(mix, 1024), kd,
                      deps=(tok,))
    dw_eout, = _matmul("e_out_dw", Mat(mixed, t, mix), Mat(dx1b, t, d), "tn", [_out(mix, d, BF16)], _pick(mix, 1024), _pick(d, 1024), kt)

    def mixb_fn(dm, a, s, ga, gs):
        da, dga = _rms_bwd(dm[:, :hw], a, ga)
        dsg, dgs = _rms_bwd(dm[:, hw:], s, gs)
        prod = da * a
        cols = [jnp.broadcast_to(jnp.sum(prod[:, h * LANES:(h + 1) * LANES], axis=-1, keepdims=True), (tr, LANES))
                for h in range(heads)]
        return da, dsg, jnp.stack([_row_of(c) for c in cols], axis=0), dga, dgs

    da_b, dsgu, delta_row, dg_mla, dg_sgu = _rowwise(
        "mix_norm_bwd", mixb_fn, t // tr, [_rt(dmixed, tr), _rt(att, tr), _rt(sgu, tr), _whole(g_mla), _whole(g_sgu)],
        [_rt_out(t, hw, BF16, tr), _rt_out(t, gw, F32, tr),
         (jax.ShapeDtypeStruct((heads, 8, t), F32), pl.BlockSpec((heads, 8, tr), lambda i: (0, 0, i)))],
        [jax.ShapeDtypeStruct((1, hw), F32), jax.ShapeDtypeStruct((1, gw), F32)])

    dproj, dsgu_w, dsgu_b8, dg_vn = _sgu_bwd(proj, dsgu, g_vn, sgu_w, sgu_b, groups, rb)
    dq1, dq2, dk1, dvv, dkr_h = _attn_bwd(qall, kvall, kr, da_b, lse_row, delta_row, heads, scale, min(2 * tr, t))
    dqfull, dproj = _rope_bwd(dq1, dq2, dkr_h, ctab, stab, heads, tr, dproj, kr_cb)
    dkvall = jnp.concatenate([dk1, dvv], axis=1)
    dw_q, = _matmul("q_up_dw", Mat(qn, t, ql), Mat(dqfull, t, 2 * hw), "tn", [_out(ql, 2 * hw, BF16)], ql, _pick(2 * hw, 1024), kt)
    dqn, = _matmul("q_up_dx", Mat(dqfull, t, 2 * hw), Mat(w_q_all, ql, 2 * hw), "nt", [_out(t, ql, F32)], tm, ql, _pick(2 * hw, 2048))
    dw_kv, = _matmul("kv_up_dw", Mat(kvn, t, kvl), Mat(dkvall, t, 2 * hw), "tn", [_out(kvl, 2 * hw, BF16)], kvl, _pick(2 * hw, 1024), kt)
    dkvn, = _matmul("kv_up_dx", Mat(dkvall, t, 2 * hw), Mat(w_kv_all, kvl, 2 * hw), "nt", [_out(t, kvl, F32)], tm, kvl, _pick(2 * hw, 2048))

    def qkvb_fn(da, db, a, b, ga, gb):
        dxa, dga = _rms_bwd(da, a, ga)
        dxb, dgb = _rms_bwd(db, b, gb)
        return jnp.concatenate([dxa, dxb], axis=1), dga, dgb

    assert (2 * gw) % (ql + kvl) == 0
    into = (jax.ShapeDtypeStruct(dproj.shape, dproj.dtype),
            pl.BlockSpec((tr, ql + kvl), lambda i: (i, 2 * gw // (ql + kvl))))
    dproj, dg_q, dg_kv = _rowwise(
        "qkv_norm_bwd", qkvb_fn, t // tr,
        [_rt(dqn, tr), _rt(dkvn, tr), _rt(proj, tr, ql, cq_cb), _rt(proj, tr, kvl, ckv_cb), _whole(g_q), _whole(g_kv)],
        [into], [jax.ShapeDtypeStruct((1, ql), F32), jax.ShapeDtypeStruct((1, kvl), F32)], deps=(dproj,), fill=(0, 0))
    dw_in, = _matmul("e_proj_dw", Mat(dproj, t, pi), Mat(h0, t, d), "tn", [_out(pi, d, F32)], _pick(pi, 1024), _pick(d, 2048), kt)
    dh0, = _matmul("e_proj_dx", Mat(dproj, t, pi), Mat(w_in_all, d, pi), "nt", [_out(t, d, F32)], tm, _pick(d, 1024), _pick(pi, 4096))
    dx0, _, dg_e = _norm_bwd("e_norm_bwd", dh0, xs, g_e, dx1, tr)

    kr0 = 2 * gw + c2
    gw_in = jnp.concatenate([dw_in[2 * gw:kr0], dw_in[kr0:kr0 + ROPE_HALF], dw_in[kr0 + ROPE:kr0 + ROPE + ROPE_HALF],
                             dw_in[:2 * gw]], axis=0).reshape(N_CHIPS, ei // N_CHIPS, d)
    gq = jnp.concatenate([dw_q[:, :hw].reshape(ql, heads, LANES), _unpad_rope(dw_q[:, hw:].reshape(ql, heads, LANES))], axis=-1)
    gw_uq = _stack_cols(gq.reshape(ql, heads * (LANES + ROPE)))
    gkv = jnp.concatenate([dw_kv[:, :hw].reshape(kvl, heads, LANES), dw_kv[:, hw:].reshape(kvl, heads, LANES)], axis=-1)
    gw_ukv = _stack_cols(gkv.reshape(kvl, heads * 2 * LANES))
    started_e, tok_pair = pair_start("e", [gw_in, gw_uq, gw_ukv, dw_eout.reshape(N_CHIPS, mix // N_CHIPS, d), gsmall])

    small_like = [e_norm_mix, e_q_norm, e_kv_norm, e_v_norm, e_sgu_w, e_sgu_b, e_mla_out_norm, e_sgu_out_norm, mlp_norm, final_norm]
    small_grads = [dg_e, dg_q, dg_kv, dg_vn, dsgu_w, dsgu_b8[:, 0, :], dg_mla, dg_sgu, jnp.concatenate([dg_m0, dg_m1], axis=0), dg_f]
    packed = _pack_small(small_grads)
    n_small = packed.shape[0] + (-packed.shape[0]) % 8
    pad = n_small - packed.shape[0] + 8
    sflat = jnp.concatenate([jnp.pad(packed, ((0, pad - 8), (0, 0))), jnp.pad(loss_vec, ((0, 7), (0, 0)))], axis=0)
    small_started, tok_small = _exchange_start("small_start", _all_route, 7, [sflat], [_spread(sflat)])

    sh_m1, tok = summed("m1", sc_m1, (tok_pair, tok_small))
    sc_e, tok = pair_finish("e", started_e, tok)
    sh_o, tok = summed("o", sc_o, tok)
    sh_m0, tok = summed("m0", sc_m0, tok)
    r_oin, r_oout = shared("o", sh_o, tok)
    late = {"o_w_in": _adamw(o_w_in, [r_oin], m_o_w_in, v_o_w_in),
            "o_w_out": _adamw(o_w_out, [r_oout], m_o_w_out, v_o_w_out)}
    r_w1_1, r_w2_1 = shared("m1", sh_m1, late["o_w_in"][1])
    r_w1_0, r_w2_0 = shared("m0", sh_m0, r_w2_1)
    late["mlp_w1"] = _adamw(mlp_w1, [r_w1_0, r_w1_1], m_mlp_w1, v_mlp_w1)
    sh_e, tok = summed("e", sc_e, late["mlp_w1"][1])
    late["mlp_w2"] = _adamw(mlp_w2, [r_w2_0, r_w2_1], m_mlp_w2, v_mlp_w2, deps=[tok])

    _, (all_small,) = _exchange_wait("small_wait", _all_route, small_started, late["mlp_w2"][1])
    g_small = _sum_devices(all_small)
    loss = 0.5 * jnp.sum(g_small[n_small]) / d

    def padded(arrs):
        return jnp.pad(_pack_small(arrs), ((0, pad), (0, 0)))

    s_m = [m_e_norm_mix, m_e_q_norm, m_e_kv_norm, m_e_v_norm, m_e_sgu_w, m_e_sgu_b, m_e_mla_out_norm, m_e_sgu_out_norm, m_mlp_norm, m_final_norm]
    s_v = [v_e_norm_mix, v_e_q_norm, v_e_kv_norm, v_e_v_norm, v_e_sgu_w, v_e_sgu_b, v_e_mla_out_norm, v_e_sgu_out_norm, v_mlp_norm, v_final_norm]
    s_out = [_unpack_small(o[0], small_like)
             for o in _adamw(padded(small_like)[None], [g_small], padded(s_m)[None], padded(s_v)[None])]

    r_in, r_uq, r_ukv, r_eout, r_small = shared("e", sh_e, (tok, late["mlp_w2"][1]))
    sm = [o[0] for o in _adamw(small_shard[None], [r_small], _small_shard(m_o_norm_mix, m_o_conv_w[0])[None],
                               _small_shard(v_o_norm_mix, v_o_conv_w[0])[None])]
    big = dict(late)
    flip = lambda a: jnp.swapaxes(a, 1, 2)
    big.update({
        "e_w_in": [flip(o) for o in _adamw(flip(e_w_in), [r_in], flip(m_e_w_in), flip(v_e_w_in))],
        "e_w_uq": _adamw(e_w_uq, [r_uq], m_e_w_uq, v_e_w_uq),
        "e_w_ukv": _adamw(e_w_ukv, [r_ukv], m_e_w_ukv, v_e_w_ukv),
        "e_w_out": _adamw(e_w_out, [r_eout], m_e_w_out, v_e_w_out),
    })

    names = ["e_norm_mix", "e_w_in", "e_q_norm", "e_w_uq", "e_kv_norm", "e_w_ukv", "e_v_norm", "e_sgu_w", "e_sgu_b",
             "e_mla_out_norm", "e_sgu_out_norm", "e_w_out", "o_norm_mix", "o_w_in", "o_conv_w", "o_w_out",
             "mlp_norm", "mlp_w1", "mlp_w2", "final_norm"]
    shapes = {"e_w_in": e_w_in.shape, "e_w_uq": e_w_uq.shape, "e_w_ukv": e_w_ukv.shape, "e_w_out": e_w_out.shape,
              "o_w_in": o_w_in.shape, "o_w_out": o_w_out.shape, "mlp_w1": mlp_w1.shape, "mlp_w2": mlp_w2.shape}
    small_names = ["e_norm_mix", "e_q_norm", "e_kv_norm", "e_v_norm", "e_sgu_w", "e_sgu_b", "e_mla_out_norm",
                   "e_sgu_out_norm", "mlp_norm", "final_norm"]

    def leaf(kind, name):
        if name in big:
            return big[name][kind].reshape(shapes[name])
        if name == "o_norm_mix":
            return sm[kind][0:1]
        if name == "o_conv_w":
            return sm[kind][16:19].reshape(o_conv_w.shape)
        return s_out[kind][small_names.index(name)]

    outs = [loss, dx0.reshape(x.shape)]
    for kind in range(4):
        outs += [leaf(kind, nm) for nm in names]
    return tuple(outs)


def _gcd(a, b):
    while b:
        a, b = b, a % b
    return a
```

```python
import jax
import jax.numpy as jnp
from jax import lax
from jax.experimental import pallas as pl
from jax.experimental.pallas import tpu as pltpu

F32 = jnp.float32
BF16 = jnp.bfloat16
MESH = pl.DeviceIdType.MESH

LANES = 128
ROPE = 64
ROPE_HALF = ROPE // 2
ROPE_BASE = 10000.0
EPS = 1e-6
N_CHIPS = 4
VMEM_LIMIT = 48 * 1024 * 1024
NEG = -1e30

ADAM_LR = 0.001
ADAM_B1 = 0.9
ADAM_B2 = 0.999
ADAM_EPS = 1e-08
ADAM_WD = 0.01
ADAM_STEP = 10


def _pick(n, target, step=LANES):
    best = None
    for t in range(step, min(n, target) + 1, step):
        if n % t == 0:
            best = t
    return best if best is not None else n


def _params(sem, vmem=VMEM_LIMIT):
    return pltpu.CompilerParams(dimension_semantics=sem, vmem_limit_bytes=vmem)


class Mat:
    def __init__(self, arr, rows, cols, kind="plain", lead=(), cmap=None, shape=None, dtype=None):
        self.arr, self.rows, self.cols, self.kind, self.lead, self.cmap = arr, rows, cols, kind, tuple(lead), cmap
        self.shape = tuple(arr.shape) if arr is not None else tuple(shape)
        self.dtype = arr.dtype if arr is not None else dtype

    def sds(self):
        return jax.ShapeDtypeStruct(self.shape, self.dtype)

    def spec(self, br, bc, gridmap):
        lead, nl = self.lead, len(self.lead)
        if self.kind == "plain":
            assert self.rows % br == 0 and self.cols % bc == 0, (self.shape, br, bc)
            cmap = self.cmap if self.cmap is not None else (lambda cb, _: cb)
            block = (None,) * nl + (br, bc)

            def phys(rb, cb):
                return lead + (rb, cmap(cb, bc))
        elif self.kind == "colstack":
            cs = self.shape[-1]
            assert cs % bc == 0 and self.rows % br == 0, (self.shape, br, bc)
            q = cs // bc
            block = (None,) * (nl + 1) + (br, bc)

            def phys(rb, cb):
                return (cb // q,) + lead + (rb, cb % q)
        else:
            rs = self.shape[-2]
            assert rs % br == 0 and self.cols % bc == 0, (self.shape, br, bc)
            q = rs // br
            block = (None,) * (nl + 1) + (br, bc)

            def phys(rb, cb):
                return (rb // q,) + lead + (rb % q, cb)

        return pl.BlockSpec(block, lambda *g: phys(*gridmap(*g)))


def _adamw_math(w, g, m, v):
    mn = ADAM_B1 * m + (1.0 - ADAM_B1) * g
    vn = ADAM_B2 * v + (1.0 - ADAM_B2) * jnp.square(g)
    m_hat = mn / (1.0 - ADAM_B1 ** ADAM_STEP)
    v_hat = vn / (1.0 - ADAM_B2 ** ADAM_STEP)
    return -ADAM_LR * (m_hat / (jnp.sqrt(v_hat) + ADAM_EPS) + ADAM_WD * w), mn, vn


def _matmul(name, a, b, mode, outs, tm, tn, tk, epilogue=None, extras=(), deps=()):
    if mode == "nn":
        m, k, n = a.rows, a.cols, b.cols
        a_spec = a.spec(tm, tk, lambda i, j, kk: (i, kk))
        b_spec = b.spec(tk, tn, lambda i, j, kk: (kk, j))
        dims = (((1,), (0,)), ((), ()))
    elif mode == "nt":
        m, k, n = a.rows, a.cols, b.rows
        a_spec = a.spec(tm, tk, lambda i, j, kk: (i, kk))
        b_spec = b.spec(tn, tk, lambda i, j, kk: (j, kk))
        dims = (((1,), (1,)), ((), ()))
    else:
        k, m, n = a.rows, a.cols, b.cols
        a_spec = a.spec(tk, tm, lambda i, j, kk: (kk, i))
        b_spec = b.spec(tk, tn, lambda i, j, kk: (kk, j))
        dims = (((0,), (0,)), ((), ()))
    assert m % tm == 0 and n % tn == 0 and k % tk == 0, (name, m, n, k, tm, tn, tk)
    grid = (m // tm, n // tn, k // tk)
    nk = grid[2]
    n_ex, n_out, n_dep = len(extras), len(outs), len(deps)
    tile = lambda i, j, kk: (i, j)

    def finish(z, ex, out_refs):
        vals = epilogue(z, *[e[...] for e in ex]) if epilogue is not None else (z,)
        for o, v in zip(out_refs, vals):
            o[...] = v.astype(o.dtype)

    def body_single(a_ref, b_ref, *rest):
        finish(lax.dot_general(a_ref[...], b_ref[...], dims, preferred_element_type=F32),
               rest[:n_ex], rest[n_ex + n_dep:n_ex + n_dep + n_out])

    def body_acc(a_ref, b_ref, *rest):
        acc = rest[-1]
        kk = pl.program_id(2)

        @pl.when(kk == 0)
        def _():
            acc[...] = jnp.zeros_like(acc)

        acc[...] += lax.dot_general(a_ref[...], b_ref[...], dims, preferred_element_type=F32)

        @pl.when(kk == nk - 1)
        def _():
            finish(acc[...], rest[:n_ex], rest[n_ex + n_dep:n_ex + n_dep + n_out])

    res = pl.pallas_call(
        body_single if nk == 1 else body_acc, name=name, grid=grid,
        in_specs=[a_spec, b_spec] + [e.spec(tm, tn, tile) for e in extras]
        + [pl.BlockSpec(memory_space=pl.ANY) for _ in deps],
        out_specs=[o.spec(tm, tn, tile) for o in outs],
        out_shape=[o.sds() for o in outs],
        scratch_shapes=[] if nk == 1 else [pltpu.VMEM((tm, tn), F32)],
        compiler_params=_params(("parallel", "parallel", "arbitrary")),
    )(a.arr, b.arr, *[e.arr for e in extras], *deps)
    return res


def _out(rows, cols, dtype, kind="plain", lead=(), shape=None):
    return Mat(None, rows, cols, kind, lead, shape=shape if shape is not None else (rows, cols), dtype=dtype)


def _rt(arr, tr, width=None, cb=0):
    width = arr.shape[1] if width is None else width
    return arr, pl.BlockSpec((tr, width), lambda i: (i, cb))


def _whole(arr):
    nd = arr.ndim
    return arr, pl.BlockSpec(arr.shape, lambda i: (0,) * nd)


def _rowwise(name, fn, n_steps, ins, outs, accs=(), deps=(), fill=None):
    n_in, n_out, n_acc, n_dep = len(ins), len(outs), len(accs), len(deps)

    def body(*refs):
        vals = fn(*[r[...] for r in refs[:n_in]])
        if not isinstance(vals, (tuple, list)):
            vals = (vals,)
        for ref, v in zip(refs[n_in + n_dep:n_in + n_dep + n_out], vals[:n_out]):
            ref[...] = v.astype(ref.dtype)
        if n_acc:
            acc_refs = refs[n_in + n_dep + n_out:]

            @pl.when(pl.program_id(0) == 0)
            def _():
                for ref in acc_refs:
                    ref[...] = jnp.zeros_like(ref)

            for ref, v in zip(acc_refs, vals[n_out:]):
                ref[...] += v

    acc_specs = [pl.BlockSpec(s.shape, lambda i, nd=len(s.shape): (0,) * nd) for s in accs]
    res = pl.pallas_call(
        body, name=name, grid=(n_steps,),
        in_specs=[s for _, s in ins] + [pl.BlockSpec(memory_space=pl.ANY) for _ in deps],
        out_specs=[s for _, s in outs] + acc_specs,
        out_shape=[o for o, _ in outs] + list(accs),
        input_output_aliases={} if fill is None else {n_in + fill[0]: fill[1]},
        compiler_params=_params(("arbitrary",) if n_acc else ("parallel",)),
    )(*[a for a, _ in ins], *deps)
    return res


def _rt_out(t, width, dtype, tr):
    return jax.ShapeDtypeStruct((t, width), dtype), pl.BlockSpec((tr, width), lambda i: (i, 0))


def _rms(x, g):
    r = lax.rsqrt(jnp.mean(x * x, axis=-1, keepdims=True) + EPS)
    return x * r * g


def _rms_bwd(dy, x, g):
    r = lax.rsqrt(jnp.mean(x * x, axis=-1, keepdims=True) + EPS)
    xh = x * r
    dxh = dy * g
    dx = r * (dxh - xh * jnp.mean(dxh * xh, axis=-1, keepdims=True))
    dg = jnp.sum(dy * xh, axis=0, keepdims=True)
    return dx, dg


def _gelu_and_grad(x):
    k = 0.7978845608028654
    x2 = x * x
    th = jnp.tanh(k * (x + 0.044715 * (x2 * x)))
    half = 0.5 * (1.0 + th)
    return x * half, half + 0.5 * x * (1.0 - th * th) * (k * (1.0 + 3.0 * 0.044715 * x2))


def _gelu(x):
    return _gelu_and_grad(x)[0]


def _gelu_grad(x):
    return _gelu_and_grad(x)[1]


def _norm_fwd(name, x, g, tr):
    t, d = x.shape
    return _rowwise(name, lambda xv, gv: _rms(xv, gv), t // tr, [_rt(x, tr), _whole(g)], [_rt_out(t, d, BF16, tr)])[0]


def _norm_bwd(name, dh, x, g, dres, tr):
    t, d = x.shape

    def fn(dhv, xv, gv, drv):
        dx, dg = _rms_bwd(dhv, xv, gv)
        dx = dx + drv
        return dx, dx, dg

    return _rowwise(name, fn, t // tr, [_rt(dh, tr), _rt(x, tr), _whole(g), _rt(dres, tr)],
                    [_rt_out(t, d, F32, tr), _rt_out(t, d, BF16, tr)], [jax.ShapeDtypeStruct((1, d), F32)])


def _rope_tables(posf, invf, cmask, smask, tr):
    t = posf.shape[0]

    def fn(p, f, cm, sm):
        ang = p * f
        return jnp.cos(ang) * cm, jnp.sin(ang) * sm

    return _rowwise("rope_tables", fn, t // tr, [_rt(posf, tr), _whole(invf), _whole(cmask), _whole(smask)],
                    [_rt_out(t, LANES, F32, tr), _rt_out(t, LANES, F32, tr)])


def _rot(v, c, s):
    return v * c + pltpu.roll(v, ROPE, axis=1) * s


def _rot_bwd(dv, c, s):
    return dv * c + pltpu.roll(dv * s, ROPE, axis=1)


def _rope_fwd(qfull, proj, kr_cb, ctab, stab, heads, tr):
    t = qfull.shape[0]
    hw = heads * LANES

    def fn(q, kr, c, s):
        parts = [q[:, :hw]] + [_rot(q[:, hw + h * LANES: hw + (h + 1) * LANES], c, s) for h in range(heads)]
        return jnp.concatenate(parts, axis=1), _rot(kr, c, s)

    return _rowwise("rope_fwd", fn, t // tr, [_rt(qfull, tr), _rt(proj, tr, LANES, kr_cb), _rt(ctab, tr), _rt(stab, tr)],
                    [_rt_out(t, 2 * hw, BF16, tr), _rt_out(t, LANES, BF16, tr)])


def _rope_bwd(dq1, dq2, dkr_h, ctab, stab, heads, tr, dproj, kr_cb):
    t = dq1.shape[0]
    hw = heads * LANES

    def fn(a, b, dk, c, s):
        parts = [a] + [_rot_bwd(b[:, h * LANES:(h + 1) * LANES], c, s) for h in range(heads)]
        dks = dk[0]
        for h in range(1, heads):
            dks = dks + dk[h]
        return jnp.concatenate(parts, axis=1), _rot_bwd(dks, c, s)

    dk_spec = pl.BlockSpec((heads, tr, LANES), lambda i: (0, i, 0))
    into = (jax.ShapeDtypeStruct(dproj.shape, dproj.dtype), pl.BlockSpec((tr, LANES), lambda i: (i, kr_cb)))
    return _rowwise("rope_bwd", fn, t // tr, [_rt(dq1, tr), _rt(dq2, tr), (dkr_h, dk_spec), _rt(ctab, tr), _rt(stab, tr)],
                    [_rt_out(t, 2 * hw, BF16, tr), into], deps=(dproj,), fill=(0, 1))


def _dot_nt(a, b):
    return lax.dot_general(a, b, (((1,), (1,)), ((), ())), preferred_element_type=F32)


def _dot_tn(a, b):
    return lax.dot_general(a, b, (((0,), (0,)), ((), ())), preferred_element_type=F32)


def _dot(a, b):
    return jnp.dot(a, b, preferred_element_type=F32)


def _ranges(n_blocks):
    n_var = min(4, n_blocks)
    assert n_blocks % n_var == 0
    return n_var, n_blocks // n_var


def _row_of(col):
    return col.T[:8, :]


def _attn_fwd(qall, kvall, kr, heads, scale, tq):
    t = qall.shape[0]
    nq = t // tq
    n_var, per = _ranges(nq)

    def body(qn_ref, qr_ref, kn_ref, v_ref, kr_ref, o_ref, lser_ref):
        i = pl.program_id(1)
        for var in range(n_var):
            kv = (var + 1) * per * tq

            @pl.when(jnp.logical_and(i >= var * per, i < (var + 1) * per))
            def _(kv=kv):
                s = _dot_nt(jnp.concatenate([qn_ref[...], qr_ref[...]], axis=1),
                            jnp.concatenate([kn_ref[:kv, :], kr_ref[:kv, :]], axis=1)) * scale
                rows = i * tq + lax.broadcasted_iota(jnp.int32, (tq, kv), 0)
                cols = lax.broadcasted_iota(jnp.int32, (tq, kv), 1)
                s = jnp.where(cols <= rows, s, NEG)
                m = jnp.max(s, axis=-1, keepdims=True)
                p = jnp.exp(s - m)
                l = jnp.sum(p, axis=-1, keepdims=True)
                o_ref[...] = _dot(p.astype(BF16), v_ref[:kv, :]) / l
                lser_ref[...] = _row_of(jnp.broadcast_to(m + jnp.log(l), (tq, LANES)))

    return pl.pallas_call(
        body, name="attn_fwd", grid=(heads, nq),
        in_specs=[pl.BlockSpec((tq, LANES), lambda h, i: (i, h)),
                  pl.BlockSpec((tq, LANES), lambda h, i: (i, heads + h)),
                  pl.BlockSpec((t, LANES), lambda h, i: (0, h)),
                  pl.BlockSpec((t, LANES), lambda h, i: (0, heads + h)),
                  pl.BlockSpec((t, LANES), lambda h, i: (0, 0))],
        out_specs=[pl.BlockSpec((tq, LANES), lambda h, i: (i, h)),
                   pl.BlockSpec((None, 8, tq), lambda h, i: (h, 0, i))],
        out_shape=[jax.ShapeDtypeStruct((t, heads * LANES), F32), jax.ShapeDtypeStruct((heads, 8, t), F32)],
        compiler_params=_params(("parallel", "parallel")),
    )(qall, qall, kvall, kvall, kr)


def _attn_bwd(qall, kvall, kr, do, lse_row, delta_row, heads, scale, tk):
    t = qall.shape[0]
    nk = t // tk
    n_var, per = _ranges(nk)

    def body(qn_ref, qr_ref, kn_ref, v_ref, kr_ref, do_ref, lse_ref, dl_ref, dq1_ref, dq2_ref, dk_ref, dv_ref, dkr_ref):
        j = pl.program_id(1)

        @pl.when(j == 0)
        def _():
            dq1_ref[...] = jnp.zeros_like(dq1_ref)
            dq2_ref[...] = jnp.zeros_like(dq2_ref)

        for var in range(n_var):
            q0 = var * per * tk
            nq = t - q0

            @pl.when(jnp.logical_and(j >= var * per, j < (var + 1) * per))
            def _(q0=q0, nq=nq):
                qn, qr, do_v = qn_ref[q0:, :], qr_ref[q0:, :], do_ref[q0:, :]
                k1, k2 = kn_ref[...], kr_ref[...]
                qcat, kcat = jnp.concatenate([qn, qr], axis=1), jnp.concatenate([k1, k2], axis=1)
                st = _dot_nt(kcat, qcat) * scale
                keys = j * tk + lax.broadcasted_iota(jnp.int32, (tk, nq), 0)
                queries = q0 + lax.broadcasted_iota(jnp.int32, (tk, nq), 1)
                pt = jnp.where(keys <= queries, jnp.exp(st - lse_ref[0:1, q0:]), 0.0)
                dpt = _dot_nt(v_ref[...], do_v)
                dst = (pt * (dpt - dl_ref[0:1, q0:]) * scale).astype(BF16)
                dv_ref[...] = _dot(pt.astype(BF16), do_v).astype(dv_ref.dtype)
                dkc = _dot(dst, qcat)
                dk_ref[...] = dkc[:, :LANES].astype(dk_ref.dtype)
                dkr_ref[...] = dkc[:, LANES:]
                dqc = _dot_tn(dst, kcat)
                dq1_ref[q0:, :] += dqc[:, :LANES]
                dq2_ref[q0:, :] += dqc[:, LANES:]

    kblk = lambda off: pl.BlockSpec((tk, LANES), lambda h, j: (j, off + h))
    full = lambda off: pl.BlockSpec((t, LANES), lambda h, j: (0, off + h))
    stat = pl.BlockSpec((None, 8, t), lambda h, j: (h, 0, 0))
    return pl.pallas_call(
        body, name="attn_bwd", grid=(heads, nk),
        in_specs=[full(0), full(heads), kblk(0), kblk(heads), pl.BlockSpec((tk, LANES), lambda h, j: (j, 0)),
                  full(0), stat, stat],
        out_specs=[full(0), full(0), kblk(0), kblk(0), pl.BlockSpec((None, tk, LANES), lambda h, j: (h, j, 0))],
        out_shape=[jax.ShapeDtypeStruct((t, heads * LANES), F32)] * 2 + [jax.ShapeDtypeStruct((t, heads * LANES), BF16)] * 2
        + [jax.ShapeDtypeStruct((heads, t, LANES), F32)],
        compiler_params=_params(("parallel", "arbitrary")),
    )(qall, qall, kvall, kvall, kr, do, lse_row, delta_row)


def _tril():
    return lax.broadcasted_iota(jnp.int32, (LANES, LANES), 0) >= lax.broadcasted_iota(jnp.int32, (LANES, LANES), 1)


def _group_norm(vg):
    mu = jnp.mean(vg, axis=-1, keepdims=True)
    vc = vg - mu
    rs = lax.rsqrt(jnp.mean(vc * vc, axis=-1, keepdims=True) + EPS)
    return vc * rs, rs


def _sgu_fwd(proj, gain, w, bias, groups, rb):
    t = proj.shape[0]
    gw = groups * LANES
    cpb = rb // LANES

    def body(u_ref, v_ref, gain_ref, w_ref, b_ref, s_ref):
        tril = _tril()
        for g in range(groups):
            wt = jnp.where(tril, w_ref[g], 0.0).astype(BF16)
            cols = slice(g * LANES, (g + 1) * LANES)
            for ci in range(cpb):
                rows = slice(ci * LANES, (ci + 1) * LANES)
                ug = _gelu(u_ref[rows, cols])
                vh, _ = _group_norm(_gelu(v_ref[rows, cols]))
                vn = vh * gain_ref[:, cols]
                y = _dot(wt, vn.astype(BF16)) + b_ref[g]
                s_ref[rows, cols] = ug * y

    return pl.pallas_call(
        body, name="sgu_fwd", grid=(t // rb,),
        in_specs=[pl.BlockSpec((rb, gw), lambda i: (i, 0)), pl.BlockSpec((rb, gw), lambda i: (i, 1)),
                  pl.BlockSpec((1, gw), lambda i: (0, 0)),
                  pl.BlockSpec((groups, LANES, LANES), lambda i: (0, 0, 0)),
                  pl.BlockSpec((groups, LANES, LANES), lambda i: (0, 0, 0))],
        out_specs=pl.BlockSpec((rb, gw), lambda i: (i, 0)),
        out_shape=jax.ShapeDtypeStruct((t, gw), F32),
        compiler_params=_params(("parallel",)),
    )(proj, proj, gain, w, bias)


def _sgu_bwd(proj, ds, gain, w, bias, groups, rb):
    t, width = proj.shape
    gw = groups * LANES
    cpb = rb // LANES
    n_steps = t // rb

    def body(u_ref, v_ref, ds_ref, gain_ref, w_ref, b_ref, dp_ref, dw_ref, db_ref, dg_ref, dy_acc):
        du_ref, dv_ref = dp_ref.at[:, :gw], dp_ref.at[:, gw:]
        step = pl.program_id(0)

        @pl.when(step == 0)
        def _():
            dw_ref[...] = jnp.zeros_like(dw_ref)
            dy_acc[...] = jnp.zeros_like(dy_acc)
            dg_ref[...] = jnp.zeros_like(dg_ref)

        tril = _tril()
        for g in range(groups):
            wt = jnp.where(tril, w_ref[g], 0.0).astype(BF16)
            cols = slice(g * LANES, (g + 1) * LANES)
            gain_g = gain_ref[:, cols]
            for ci in range(cpb):
                rows = slice(ci * LANES, (ci + 1) * LANES)
                u_raw, v_raw, ds_v = u_ref[rows, cols], v_ref[rows, cols], ds_ref[rows, cols]
                ug, ug_grad = _gelu_and_grad(u_raw)
                vg, vg_grad = _gelu_and_grad(v_raw)
                vh, rs = _group_norm(vg)
                vn = (vh * gain_g).astype(BF16)
                y = _dot(wt, vn) + b_ref[g]
                dy = ds_v * ug
                dyb = dy.astype(BF16)
                du_ref[rows, cols] = (ds_v * y * ug_grad).astype(du_ref.dtype)
                dy_acc[g] += dy
                dw_ref[g] += _dot_nt(dyb, vn)
                dvn = _dot_tn(wt, dyb)
                dg_ref[:, cols] += jnp.sum(dvn * vh, axis=0, keepdims=True)
                dvh = dvn * gain_g
                dvg = rs * (dvh - jnp.mean(dvh, axis=-1, keepdims=True)
                            - vh * jnp.mean(dvh * vh, axis=-1, keepdims=True))
                dv_ref[rows, cols] = (dvg * vg_grad).astype(dv_ref.dtype)

        @pl.when(step == n_steps - 1)
        def _():
            ones = jnp.ones((8, LANES), F32)
            for g in range(groups):
                dw_ref[g] = jnp.where(tril, dw_ref[g], 0.0)
                db_ref[g] = lax.dot_general(ones, dy_acc[g], (((1,), (1,)), ((), ())),
                                            precision=lax.Precision.HIGHEST, preferred_element_type=F32)

    blk = lambda cb: pl.BlockSpec((rb, gw), lambda i: (i, cb))
    whole3 = pl.BlockSpec((groups, LANES, LANES), lambda i: (0, 0, 0))
    return pl.pallas_call(
        body, name="sgu_bwd", grid=(n_steps,),
        in_specs=[blk(0), blk(1), blk(0), pl.BlockSpec((1, gw), lambda i: (0, 0)), whole3, whole3],
        out_specs=[pl.BlockSpec((rb, 2 * gw), lambda i: (i, 0)), whole3,
                   pl.BlockSpec((groups, 8, LANES), lambda i: (0, 0, 0)), pl.BlockSpec((1, gw), lambda i: (0, 0))],
        out_shape=[jax.ShapeDtypeStruct((t, width), BF16),
                   jax.ShapeDtypeStruct((groups, LANES, LANES), F32), jax.ShapeDtypeStruct((groups, 8, LANES), F32),
                   jax.ShapeDtypeStruct((1, gw), F32)],
        scratch_shapes=[pltpu.VMEM((groups, LANES, LANES), F32)],
        compiler_params=_params(("arbitrary",)),
    )(proj, proj, ds, gain, w, bias)


def _shift_down(z, s):
    rows = lax.broadcasted_iota(jnp.int32, z.shape, 0)
    return jnp.where(rows >= s, pltpu.roll(z, s, axis=0), 0.0)


def _shift_up(z, s):
    n = z.shape[0]
    rows = lax.broadcasted_iota(jnp.int32, z.shape, 0)
    return jnp.where(rows < n - s, pltpu.roll(z, n - s, axis=0), 0.0)


def _conv_fwd(proj3, cw, tc):
    _, t, cd = proj3.shape

    def body(p_ref, w_ref, o_ref):
        z = p_ref[1] * p_ref[2]
        w = w_ref[...]
        zc = w[2:3] * z + w[1:2] * _shift_down(z, 1) + w[0:1] * _shift_down(z, 2)
        o_ref[...] = (p_ref[0] * zc).astype(o_ref.dtype)

    return pl.pallas_call(
        body, name="conv_fwd", grid=(cd // tc,),
        in_specs=[pl.BlockSpec((3, t, tc), lambda j: (0, 0, j)), pl.BlockSpec((8, tc), lambda j: (0, j))],
        out_specs=pl.BlockSpec((t, tc), lambda j: (0, j)),
        out_shape=jax.ShapeDtypeStruct((t, cd), BF16),
        compiler_params=_params(("parallel",)),
    )(proj3, cw)


def _conv_bwd(proj3, cw, dbz, tc):
    _, t, cd = proj3.shape

    def body(p_ref, w_ref, d_ref, o_ref, dw_ref):
        b, c, xin = p_ref[0], p_ref[1], p_ref[2]
        w = w_ref[...]
        z = c * xin
        z1, z2 = _shift_down(z, 1), _shift_down(z, 2)
        zc = w[2:3] * z + w[1:2] * z1 + w[0:1] * z2
        d = d_ref[...]
        dzc = d * b
        dz = w[2:3] * dzc + w[1:2] * _shift_up(dzc, 1) + w[0:1] * _shift_up(dzc, 2)
        o_ref[0] = (d * zc).astype(o_ref.dtype)
        o_ref[1] = (dz * xin).astype(o_ref.dtype)
        o_ref[2] = (dz * c).astype(o_ref.dtype)
        row = lax.broadcasted_iota(jnp.int32, (8, tc), 0)
        dw0 = jnp.sum(dzc * z2, axis=0, keepdims=True)
        dw1 = jnp.sum(dzc * z1, axis=0, keepdims=True)
        dw2 = jnp.sum(dzc * z, axis=0, keepdims=True)
        dw_ref[...] = jnp.where(row == 0, dw0, 0.0) + jnp.where(row == 1, dw1, 0.0) + jnp.where(row == 2, dw2, 0.0)

    return pl.pallas_call(
        body, name="conv_bwd", grid=(cd // tc,),
        in_specs=[pl.BlockSpec((3, t, tc), lambda j: (0, 0, j)), pl.BlockSpec((8, tc), lambda j: (0, j)),
                  pl.BlockSpec((t, tc), lambda j: (0, j))],
        out_specs=[pl.BlockSpec((3, t, tc), lambda j: (0, 0, j)), pl.BlockSpec((8, tc), lambda j: (0, j))],
        out_shape=[jax.ShapeDtypeStruct((3, t, cd), BF16), jax.ShapeDtypeStruct((8, cd), F32)],
        compiler_params=_params(("parallel",)),
    )(proj3, cw, dbz)


def _place():
    x, y, c = lax.axis_index("x"), lax.axis_index("y"), lax.axis_index("c")
    chips = [(1 - x, y), (x, 1 - y), (1 - x, 1 - y)]
    return x, y, c, chips


def _any_specs(n):
    return [pl.BlockSpec(memory_space=pl.ANY) for _ in range(n)]


HBM_SPEC = pl.BlockSpec(memory_space=pltpu.HBM)
SEM_SPEC = pl.BlockSpec(memory_space=pltpu.SEMAPHORE)
ORDERED_EFFECT = pltpu.SideEffectType.DATAFLOW_SIDE_EFFECTING


def _in_hbm(a):
    return pltpu.with_memory_space_constraint(a, pltpu.HBM)


def _token():
    return jax.ShapeDtypeStruct((8, LANES), F32), pl.BlockSpec(memory_space=pltpu.VMEM)


def _gather_start(name, groups):
    sizes = [len(g) for g in groups]
    flat = [b for g in groups for b in g]
    n, ng = len(flat), len(groups)

    def body(*refs):
        ins, sems, token = refs[:n], refs[n:n + 2 * ng], refs[-1]
        x, y, c, chips = _place()
        me = 2 * x + y
        i = 0
        for gi, size in enumerate(sizes):
            for j in range(size):
                blk = ins[i].at[me, c]
                for k, chip in enumerate(chips):
                    pltpu.make_async_remote_copy(src_ref=blk, dst_ref=blk, send_sem=sems[2 * gi].at[3 * j + k],
                                                 recv_sem=sems[2 * gi + 1].at[3 * j + k],
                                                 device_id=(*chip, c), device_id_type=MESH).start()
                i += 1
        token[...] = jnp.zeros_like(token)

    tok_shape, tok_spec = _token()
    res = pl.pallas_call(
        body, name=name,
        in_specs=[HBM_SPEC] * n,
        out_specs=[SEM_SPEC] * (2 * ng) + [HBM_SPEC] * n + [tok_spec],
        out_shape=[pltpu.SemaphoreType.DMA((3 * size,)) for size in sizes for _ in (0, 1)]
        + [pltpu.HBM(b.shape, b.dtype) for b in flat] + [tok_shape],
        input_output_aliases={i: 2 * ng + i for i in range(n)},
        compiler_params=pltpu.CompilerParams(has_side_effects=ORDERED_EFFECT),
    )(*[_in_hbm(b) for b in flat])
    out, i = [], 2 * ng
    for gi, size in enumerate(sizes):
        out.append((res[2 * gi], res[2 * gi + 1], list(res[i:i + size])))
        i += size
    return out, res[-1]


def _gather_wait(tag, send, recv, bufs, after):
    n = len(bufs)
    after = tuple(after) if isinstance(after, (tuple, list)) else (after,)

    def body(*refs):
        ins, send_ref, recv_ref = refs[:n], refs[n], refs[n + 1]
        x, y, c, chips = _place()
        me = 2 * x + y
        for j in range(n):
            for k, (px, py) in enumerate(chips):
                cp = pltpu.make_async_remote_copy(src_ref=ins[j].at[me, c], dst_ref=ins[j].at[2 * px + py, c],
                                                  send_sem=send_ref.at[3 * j + k], recv_sem=recv_ref.at[3 * j + k],
                                                  device_id=(px, py, c), device_id_type=MESH)
                cp.wait_send()
                cp.wait_recv()

    return pl.pallas_call(
        body, name="gather_wait_" + tag,
        in_specs=[HBM_SPEC] * n + [SEM_SPEC, SEM_SPEC] + _any_specs(len(after)),
        out_specs=[HBM_SPEC] * n,
        out_shape=[pltpu.HBM(b.shape, b.dtype) for b in bufs],
        input_output_aliases={i: i for i in range(n)},
        compiler_params=pltpu.CompilerParams(has_side_effects=ORDERED_EFFECT),
    )(*bufs, send, recv, *after)


def _gather_forward(tag, bufs):
    n = len(bufs)

    def body(*refs):
        ins, outs = refs[:n], refs[n:2 * n]
        send, recv = refs[2 * n:]
        x, y, c, chips = _place()
        sib = (x, y, 1 - c)

        def cp(i, k, slot, half):
            return pltpu.make_async_remote_copy(src_ref=ins[i].at[slot, half], dst_ref=outs[i].at[slot, half],
                                                send_sem=send.at[3 * i + k], recv_sem=recv.at[3 * i + k],
                                                device_id=sib, device_id_type=MESH)

        cps = [cp(i, k, 2 * px + py, c) for i in range(n) for k, (px, py) in enumerate(chips)]
        for d in cps:
            d.start()
        for i in range(n):
            for k, (px, py) in enumerate(chips):
                cp(i, k, 2 * px + py, 1 - c).wait_recv()
        for d in cps:
            d.wait_send()

    return pl.pallas_call(
        body, name="gather_forward_" + tag,
        in_specs=_any_specs(n), out_specs=_any_specs(n),
        out_shape=[jax.ShapeDtypeStruct(b.shape, b.dtype) for b in bufs],
        scratch_shapes=[pltpu.SemaphoreType.DMA((3 * n,))] * 2,
        input_output_aliases={i: i for i in range(n)},
        compiler_params=pltpu.CompilerParams(has_side_effects=True),
    )(*bufs)


def _pair_route(srcs, zones):
    x, y, c, _ = _place()
    return [(srcs[i].at[j, 1 - c], zones[i].at[j], (x, y, 1 - c)) for i in range(len(srcs)) for j in range(N_CHIPS)]


def _slab_route(srcs, zones):
    x, y, c, _ = _place()
    return [(srcs[i].at[j], zones[i].at[j], (x, y, 1 - c)) for i in range(len(srcs)) for j in range(N_CHIPS)]


def _chip_route(srcs, zones):
    x, y, c, chips = _place()
    return [(srcs[i].at[2 * px + py], zones[i].at[k], (px, py, c)) for i in range(len(srcs)) for k, (px, py) in enumerate(chips)]


def _all_route(srcs, zones):
    x, y, c, _ = _place()
    flips = [(fx, fy, fc) for fx in (0, 1) for fy in (0, 1) for fc in (0, 1)][1:]
    return [(srcs[0], zones[0].at[4 * x + 2 * y + c], (x + fx - 2 * x * fx, y + fy - 2 * y * fy, c + fc - 2 * c * fc))
            for fx, fy, fc in flips]


def _share_route(srcs, zones):
    x, y, c, _ = _place()
    return [(s.at[c], s.at[c], (x, y, 1 - c)) for s in srcs]


def _exchange_start(name, route, n_copies, srcs, zones):
    n, nz = len(srcs), len(zones)
    lands = [lax.empty(z, a.dtype) if isinstance(z, tuple) else z for z, a in zip(zones, srcs)]

    def body(*refs):
        ins, zone_refs, send, recv, token = refs[:n], refs[n:n + nz], refs[n + nz], refs[n + nz + 1], refs[-1]
        for k, (src, dst, dev) in enumerate(route(ins, zone_refs)):
            pltpu.make_async_remote_copy(src_ref=src, dst_ref=dst, send_sem=send.at[k], recv_sem=recv.at[k],
                                         device_id=dev, device_id_type=MESH).start()
        token[...] = jnp.zeros_like(token)

    tok_shape, tok_spec = _token()
    res = pl.pallas_call(
        body, name=name,
        in_specs=[HBM_SPEC] * (n + nz),
        out_specs=[SEM_SPEC, SEM_SPEC] + [HBM_SPEC] * (n + nz) + [tok_spec],
        out_shape=[pltpu.SemaphoreType.DMA((n_copies,))] * 2 + [pltpu.HBM(a.shape, a.dtype) for a in srcs + lands]
        + [tok_shape],
        input_output_aliases={i: 2 + i for i in range(n + nz)},
        compiler_params=pltpu.CompilerParams(has_side_effects=ORDERED_EFFECT),
    )(*[_in_hbm(a) for a in srcs + lands])
    return (res[0], res[1], list(res[2:2 + n]), list(res[2 + n:2 + n + nz])), res[-1]


def _exchange_wait(name, route, started, after):
    send, recv, srcs, lands = started
    n, nz = len(srcs), len(lands)
    after = tuple(after) if isinstance(after, (tuple, list)) else (after,)

    def body(*refs):
        ins, zone_refs, send_ref, recv_ref = refs[:n], refs[n:n + nz], refs[n + nz], refs[n + nz + 1]
        for k, (src, dst, dev) in enumerate(route(ins, zone_refs)):
            cp = pltpu.make_async_remote_copy(src_ref=src, dst_ref=dst, send_sem=send_ref.at[k], recv_sem=recv_ref.at[k],
                                              device_id=dev, device_id_type=MESH)
            cp.wait_send()
            cp.wait_recv()

    res = pl.pallas_call(
        body, name=name,
        in_specs=[HBM_SPEC] * (n + nz) + [SEM_SPEC, SEM_SPEC] + _any_specs(len(after)),
        out_specs=[HBM_SPEC] * (n + nz),
        out_shape=[pltpu.HBM(a.shape, a.dtype) for a in srcs + lands],
        input_output_aliases={i: i for i in range(n + nz)},
        compiler_params=pltpu.CompilerParams(has_side_effects=ORDERED_EFFECT),
    )(*srcs, *lands, send, recv, *after)
    return list(res[:n]), list(res[n:])


def _spread(v):
    rows, cols = v.shape
    tr = _row_tile(rows, cols, budget=256 * 1024)

    def body(v_ref, o_ref):
        o_ref[...] = jnp.broadcast_to(v_ref[...][None], o_ref.shape)

    return pl.pallas_call(body, name="spread_small_grads", grid=(rows // tr,),
                          in_specs=[pl.BlockSpec((tr, cols), lambda r: (r, 0))],
                          out_specs=pl.BlockSpec((8, tr, cols), lambda r: (0, r, 0)),
                          out_shape=jax.ShapeDtypeStruct((8, rows, cols), v.dtype),
                          compiler_params=_params(("parallel",)))(v)


def _row_tile(rows, cols, itemsize=4, budget=2 * 1024 * 1024, step=8):
    best = None
    for t in range(step, rows + 1, step):
        if rows % t == 0 and t * cols * itemsize <= budget:
            best = t
    return best if best is not None else rows


def _my_chip():
    return 2 * lax.axis_index("x") + lax.axis_index("y")


def _pair_sum(g5, gsib):
    _, _, rh, cols = g5.shape
    tr = _row_tile(rh, cols, step=16)

    def body(a_ref, b_ref, o_ref):
        o_ref[...] = (a_ref[...].astype(F32) + b_ref[...].astype(F32)).astype(o_ref.dtype)

    return pl.pallas_call(body, name="grad_pair_sum", grid=(N_CHIPS, rh // tr),
                          in_specs=[pl.BlockSpec((None, None, tr, cols), lambda j, r: (j, lax.axis_index("c"), r, 0)),
                                    pl.BlockSpec((None, tr, cols), lambda j, r: (j, r, 0))],
                          out_specs=pl.BlockSpec((None, tr, cols), lambda j, r: (j, r, 0)),
                          out_shape=jax.ShapeDtypeStruct((N_CHIPS, rh, cols), BF16),
                          compiler_params=_params(("parallel", "parallel")))(g5, gsib)


def _chip_sum(part, recv):
    _, rh, cols = part.shape
    tr = _row_tile(rh, cols, budget=4 * 1024 * 1024, step=16)

    def body(a_ref, b_ref, o_ref):
        acc = a_ref[...].astype(F32)
        for k in range(3):
            acc = acc + b_ref[k].astype(F32)
        o_ref[...] = acc

    return pl.pallas_call(body, name="grad_chip_sum", grid=(rh // tr,),
                          in_specs=[pl.BlockSpec((None, tr, cols), lambda r: (_my_chip(), r, 0)),
                                    pl.BlockSpec((3, tr, cols), lambda r: (0, r, 0))],
                          out_specs=pl.BlockSpec((None, tr, cols), lambda r: (lax.axis_index("c"), r, 0)),
                          out_shape=jax.ShapeDtypeStruct((2, rh, cols), F32),
                          compiler_params=_params(("parallel",)))(part, recv)


def _sum_devices(g):
    _, rows, cols = g.shape
    tr = _row_tile(rows, cols, budget=256 * 1024)

    def body(g_ref, o_ref):
        acc = g_ref[0]
        for d in range(1, 8):
            acc = acc + g_ref[d]
        o_ref[...] = acc

    return pl.pallas_call(body, name="sum_small_grads", grid=(rows // tr,),
                          in_specs=[pl.BlockSpec((8, tr, cols), lambda r: (0, r, 0))],
                          out_specs=pl.BlockSpec((tr, cols), lambda r: (r, 0)),
                          out_shape=jax.ShapeDtypeStruct((rows, cols), F32),
                          compiler_params=_params(("parallel",)))(g)


def _place_shard(w, layer, dtype, deps=()):
    _, rows, cols = w.shape
    tr = _row_tile(rows, cols)

    def body(i_ref, *rest):
        o_ref = rest[-1]
        o_ref[...] = i_ref[...].astype(o_ref.dtype)

    out = pl.pallas_call(body, name="place_shard", grid=(rows // tr,),
                         in_specs=[pl.BlockSpec((None, tr, cols), lambda r: (layer, r, 0))] + _any_specs(len(deps)),
                         out_specs=pl.BlockSpec((None, tr, cols), lambda r: (_my_chip(), r, 0)),
                         out_shape=jax.ShapeDtypeStruct((N_CHIPS, rows, cols), dtype),
                         compiler_params=_params(("parallel",)))(w, *deps)
    return out.reshape(N_CHIPS, 2, rows // 2, cols)


def _adamw(w, gs, m, v, deps=()):
    n_layers, rows, cols = w.shape
    tr = _row_tile(rows, cols)

    def body(w_ref, m_ref, v_ref, *rest):
        g_refs = rest[:n_layers]
        go_ref, d_ref, mo_ref, vo_ref = rest[-4:]
        gv = g_refs[0][...]
        for layer in range(1, n_layers):
            gv = jnp.where(pl.program_id(0) == layer, g_refs[layer][...], gv)
        d_ref[...], mo_ref[...], vo_ref[...] = _adamw_math(w_ref[...], gv, m_ref[...], v_ref[...])
        go_ref[...] = gv

    spec = pl.BlockSpec((None, tr, cols), lambda layer, r: (layer, r, 0))
    g_specs = [pl.BlockSpec((tr, cols), lambda layer, r, own=own: (jnp.where(layer == own, r, 0), 0))
               for own in range(n_layers)]
    return pl.pallas_call(body, name="adamw", grid=(n_layers, rows // tr),
                          in_specs=[spec] * 3 + g_specs + _any_specs(len(deps)),
                          out_specs=[spec] * 4, out_shape=[jax.ShapeDtypeStruct((n_layers, rows, cols), F32)] * 4,
                          compiler_params=_params(("parallel", "parallel")))(w, m, v, *gs, *deps)


def _pad_rope(w):
    z = jnp.zeros(w.shape[:-1] + (ROPE_HALF,), w.dtype)
    return jnp.concatenate([w[..., :ROPE_HALF], z, w[..., ROPE_HALF:], z], axis=-1)


def _unpad_rope(g):
    return jnp.concatenate([g[..., :ROPE_HALF], g[..., ROPE:ROPE + ROPE_HALF]], axis=-1)


def _unstack_cols(s):
    n, r, cs = s.shape
    return jnp.transpose(s, (1, 0, 2)).reshape(r, n * cs)


def _stack_cols(f):
    r, cfull = f.shape
    return jnp.transpose(f.reshape(r, N_CHIPS, cfull // N_CHIPS), (1, 0, 2))


def _small_shard(norm, conv):
    return jnp.concatenate([jnp.pad(norm, ((0, 15), (0, 0))), jnp.pad(conv, ((0, 13), (0, 0)))], axis=0)


def _flat_rows(a):
    return a.reshape(-1, LANES)


def _pack_small(arrs):
    return jnp.concatenate([_flat_rows(a.astype(F32)) for a in arrs], axis=0)


def _unpack_small(flat, like):
    out, r = [], 0
    for a in like:
        n = a.size // LANES
        out.append(flat[r:r + n].reshape(a.shape))
        r += n
    return out


def kernel(x, positions, e_norm_mix, e_w_in, e_q_norm, e_w_uq, e_kv_norm, e_w_ukv, e_v_norm, e_sgu_w, e_sgu_b, e_mla_out_norm, e_sgu_out_norm, e_w_out, o_norm_mix, o_w_in, o_conv_w, o_w_out, mlp_norm, mlp_w1, mlp_w2, final_norm, loss_target, m_e_norm_mix, m_e_w_in, m_e_q_norm, m_e_w_uq, m_e_kv_norm, m_e_w_ukv, m_e_v_norm, m_e_sgu_w, m_e_sgu_b, m_e_mla_out_norm, m_e_sgu_out_norm, m_e_w_out, m_o_norm_mix, m_o_w_in, m_o_conv_w, m_o_w_out, m_mlp_norm, m_mlp_w1, m_mlp_w2, m_final_norm, v_e_norm_mix, v_e_w_in, v_e_q_norm, v_e_w_uq, v_e_kv_norm, v_e_w_ukv, v_e_v_norm, v_e_sgu_w, v_e_sgu_b, v_e_mla_out_norm, v_e_sgu_out_norm, v_e_w_out, v_o_norm_mix, v_o_w_in, v_o_conv_w, v_o_w_out, v_mlp_norm, v_mlp_w1, v_mlp_w2, v_final_norm):
    t, d = x.shape[1], x.shape[2]
    ql, kvl = e_q_norm.shape[1], e_kv_norm.shape[1]
    groups = e_v_norm.shape[1]
    gw = groups * LANES
    heads = N_CHIPS * e_w_uq.shape[2] // (LANES + ROPE)
    hw = heads * LANES
    mix = hw + gw
    ei = N_CHIPS * e_w_in.shape[2]
    cd = N_CHIPS * o_conv_w.shape[2]
    ff = N_CHIPS * mlp_w1.shape[2]
    ffs = ff // N_CHIPS
    pi = 2 * gw + ql + kvl + LANES
    assert e_norm_mix.shape[0] == 1 and o_norm_mix.shape[0] == 1 and mlp_norm.shape[0] == 2
    assert ei == ql + kvl + ROPE + 2 * gw and cd == d and e_sgu_w.shape[2] == LANES
    assert (2 * gw) % ql == 0 and (2 * gw + ql) % kvl == 0 and t % LANES == 0
    scale = (LANES + ROPE) ** -0.5

    tr = min(256, t)
    tm = _pick(t, 1024, 8)
    kt, kd = _pick(t, 2048, 8), _pick(d, 2048)
    xs = x.reshape(t, d)
    tgt = loss_target.reshape(t, d)

    small_shard = _small_shard(o_norm_mix, o_conv_w[0])
    first, tok = _gather_start("gather_start_e", [
        [_place_shard(e_w_in, 0, BF16)],
        [_place_shard(e_w_uq, 0, BF16), _place_shard(e_w_ukv, 0, BF16), _place_shard(e_w_out, 0, BF16),
         _place_shard(small_shard[None], 0, F32)]])
    rest, tok = _gather_start("gather_start_rest", [
        [_place_shard(mlp_w1, 0, BF16, (tok,))], [_place_shard(mlp_w2, 0, BF16, (tok,))],
        [_place_shard(o_w_in, 0, BF16, (tok,)), _place_shard(o_w_out, 0, BF16, (tok,))],
        [_place_shard(mlp_w1, 1, BF16, (tok,))], [_place_shard(mlp_w2, 1, BF16, (tok,))]])
    started = first + rest

    def gathered(gi, tag, after):
        send, recv, bufs = started[gi]
        bufs = _gather_forward(tag, _gather_wait(tag, send, recv, bufs, after))
        return [b.reshape(N_CHIPS, 2 * b.shape[2], b.shape[3]) for b in bufs]

    g_e = e_norm_mix
    h0 = _norm_fwd("e_norm", xs, g_e, tr)
    inv_freq = ROPE_BASE ** (-jnp.arange(0, ROPE, 2, dtype=F32) / ROPE)
    zeros32 = jnp.zeros((ROPE_HALF,), F32)
    ones32 = jnp.ones((ROPE_HALF,), F32)
    invf = jnp.concatenate([inv_freq, zeros32, inv_freq, zeros32]).reshape(1, LANES)
    cmask = jnp.concatenate([ones32, zeros32, ones32, zeros32]).reshape(1, LANES)
    smask = jnp.concatenate([-ones32, zeros32, ones32, zeros32]).reshape(1, LANES)
    ctab, stab = _rope_tables(positions.reshape(t, 1).astype(F32), invf, cmask, smask, tr)

    w_in_g, = gathered(0, "e_in", (h0, ctab, tok))
    full = _unstack_cols(w_in_g)
    c2, c3 = ql + kvl, ql + kvl + ROPE
    w_in_all = jnp.concatenate([full[:, c3:], full[:, :c2], _pad_rope(full[:, c2:c3])], axis=1)
    proj, = _matmul("e_proj", Mat(h0, t, d), Mat(w_in_all, d, pi), "nn", [_out(t, pi, F32)], tm, _pick(pi, 1024), kd)

    w_uq_g, w_ukv_g, w_eout_g, small_g = gathered(1, "e", proj)
    full = _unstack_cols(w_uq_g).reshape(ql, heads, LANES + ROPE)
    w_q_all = jnp.concatenate([full[:, :, :LANES].reshape(ql, hw), _pad_rope(full[:, :, LANES:]).reshape(ql, hw)], axis=1)
    full = _unstack_cols(w_ukv_g).reshape(kvl, heads, 2 * LANES)
    w_kv_all = jnp.concatenate([full[:, :, :LANES].reshape(kvl, hw), full[:, :, LANES:].reshape(kvl, hw)], axis=1)
    w_eout = w_eout_g.reshape(mix, d)
    g_o = small_g[:, 0].reshape(1, d)
    conv_w = jnp.pad(jnp.transpose(small_g[:, 16:19], (1, 0, 2)).reshape(3, cd), ((0, 5), (0, 0)))

    g_q, g_kv = e_q_norm, e_kv_norm
    g_vn = e_v_norm.reshape(1, gw)
    sgu_w = e_sgu_w[0]
    sgu_b = jnp.broadcast_to(e_sgu_b[0][:, :, None], (groups, LANES, LANES))
    g_mla, g_sgu = e_mla_out_norm, e_sgu_out_norm
    g_m0, g_m1 = mlp_norm[0:1], mlp_norm[1:2]
    g_f = final_norm.reshape(1, d)

    def mlp_fwd(tag, xin, g, gi):
        hm = _norm_fwd("mlp_norm_" + tag, xin, g, tr)
        tn = _pick(ffs, 1024)
        w1 = Mat(gathered(gi, "w1_" + tag, hm)[0], d, ff, "colstack")
        a, act = _matmul("mlp_up_" + tag, Mat(hm, t, d), w1, "nn",
                         [_out(t, ff, BF16), _out(t, ff, BF16)], tm, tn, kd,
                         epilogue=lambda z: (jnp.maximum(z, 0.0), jnp.square(jnp.maximum(z, 0.0))))
        w2 = Mat(gathered(gi + 1, "w2_" + tag, act)[0].reshape(ff, d), ff, d)
        xo, = _matmul("mlp_down_" + tag, Mat(act, t, ff), w2, "nn",
                      [_out(t, d, F32)], tm, _pick(d, 1024), _pick(ffs, 2048),
                      epilogue=lambda z, r: (z + r,), extras=[Mat(xin, t, d)])
        return xo, hm, a, act, w1, w2

    def chip_start(tag, part):
        return _exchange_start("scatter_start_" + tag, _chip_route, 3 * len(part), part, [(3,) + p.shape[1:] for p in part])

    def pair_start(tag, stacked):
        g5 = [g.reshape(N_CHIPS, 2, g.shape[1] // 2, g.shape[2]) for g in stacked]
        return _exchange_start("pair_start_" + tag, _pair_route, N_CHIPS * len(g5), g5,
                               [(N_CHIPS,) + g.shape[2:] for g in g5])

    def pair_finish(tag, started, after):
        g5, from_sib = _exchange_wait("pair_wait_" + tag, _pair_route, started, after)
        return chip_start(tag, [_pair_sum(a, b) for a, b in zip(g5, from_sib)])

    def summed(tag, sc, after):
        part, lands = _exchange_wait("scatter_wait_" + tag, _chip_route, sc, after)
        half = [_chip_sum(p, r) for p, r in zip(part, lands)]
        return _exchange_start("share_start_" + tag, _share_route, len(half), half, [])

    def shared(tag, started, after):
        bufs, _ = _exchange_wait("share_wait_" + tag, _share_route, started, after)
        return [r.reshape(2 * r.shape[1], r.shape[2]) for r in bufs]

    def mlp_bwd(tag, dx, dxb, xin, g, w1, w2, hm, a, act, deps):
        tn = _pick(ffs, 1024)
        hr, hd = ffs // 2, d // 2
        dz, = _matmul("mlp_dact_" + tag, Mat(dxb, t, d), w2, "nt",
                      [_out(t, ff, BF16)], tm, tn, kd,
                      epilogue=lambda z, av: (z * (2.0 * av.astype(F32)),), extras=[Mat(a, t, ff)], deps=deps)

        def half(own):
            c = lax.axis_index("c")
            return c if own else 1 - c

        def act_half(own):
            return Mat(act, t, ff // 2, cmap=lambda cb, bc: (cb // (hr // bc)) * (ffs // bc) + half(own) * (hr // bc)
                       + cb % (hr // bc))

        def hm_half(own):
            return Mat(hm, t, hd, cmap=lambda cb, bc: cb + half(own) * (hd // bc))

        w1_out = lambda: _out(hd, ff, BF16, "colstack", (), (N_CHIPS, hd, ffs))
        theirs2, = _matmul("mlp_dw2_theirs_" + tag, act_half(False), Mat(dxb, t, d), "tn",
                           [_out(ff // 2, d, BF16)], _pick(hr, 1024), _pick(d, 2048), kt)
        theirs1, = _matmul("mlp_dw1_theirs_" + tag, hm_half(False), Mat(dz, t, ff), "tn",
                           [w1_out()], _pick(hd, 2048), tn, kt)
        sent = [theirs1, theirs2.reshape(N_CHIPS, hr, d)]
        started, tok = _exchange_start("pair_start_m" + tag, _slab_route, N_CHIPS * 2, sent, [s.shape for s in sent])
        dhm, = _matmul("mlp_dh_" + tag, Mat(dz, t, ff), w1, "nt",
                       [_out(t, d, F32)], tm, _pick(d, 1024), _pick(ffs, 2048), deps=(tok,))
        dxo, dxob, dg = _norm_bwd("mlp_norm_bwd_" + tag, dhm, xin, g, dx, tr)
        _, (sib1, sib2) = _exchange_wait("pair_wait_m" + tag, _slab_route, started, dxo)
        add = lambda z, s: (z + s.astype(F32),)
        part2, = _matmul("mlp_dw2_mine_" + tag, act_half(True), Mat(dxb, t, d), "tn",
                         [_out(ff // 2, d, BF16)], _pick(hr, 1024), _pick(d, 2048), kt,
                         epilogue=add, extras=[Mat(sib2.reshape(ff // 2, d), ff // 2, d)])
        part1, = _matmul("mlp_dw1_mine_" + tag, hm_half(True), Mat(dz, t, ff), "tn",
                         [w1_out()], _pick(hd, 2048), tn, kt, epilogue=add, extras=[Mat(sib1, hd, ff, "colstack")])
        sc, tok = chip_start("m" + tag, [part1, part2.reshape(N_CHIPS, hr, d)])
        return dxo, dxob, dg, sc, tok

    cq_cb, ckv_cb, kr_cb = 2 * gw // ql, (2 * gw + ql) // kvl, (2 * gw + ql + kvl) // LANES
    qn, kvn = _rowwise("qkv_norm", lambda a, b, ga, gb: (_rms(a, ga), _rms(b, gb)), t // tr,
                       [_rt(proj, tr, ql, cq_cb), _rt(proj, tr, kvl, ckv_cb), _whole(g_q), _whole(g_kv)],
                       [_rt_out(t, ql, BF16, tr), _rt_out(t, kvl, BF16, tr)])
    qfull, = _matmul("q_up", Mat(qn, t, ql), Mat(w_q_all, ql, 2 * hw), "nn", [_out(t, 2 * hw, F32)], tm, _pick(2 * hw, 1024), ql)
    kvall, = _matmul("kv_up", Mat(kvn, t, kvl), Mat(w_kv_all, kvl, 2 * hw), "nn", [_out(t, 2 * hw, BF16)], tm, _pick(2 * hw, 1024), kvl)
    qall, kr = _rope_fwd(qfull, proj, kr_cb, ctab, stab, heads, tr)
    att, lse_row = _attn_fwd(qall, kvall, kr, heads, scale, tr)
    rb = min(2 * LANES, t)
    sgu = _sgu_fwd(proj, g_vn, sgu_w, sgu_b, groups, rb)
    mixed = _rowwise("mix_norm", lambda a, s, ga, gs: jnp.concatenate([_rms(a, ga), _rms(s, gs)], axis=1), t // tr,
                     [_rt(att, tr), _rt(sgu, tr), _whole(g_mla), _whole(g_sgu)], [_rt_out(t, mix, BF16, tr)])[0]
    x1, = _matmul("e_out", Mat(mixed, t, mix), Mat(w_eout, mix, d), "nn", [_out(t, d, F32)], tm, _pick(d, 1024), _pick(mix, 2048),
                  epilogue=lambda z, r: (z + r,), extras=[Mat(xs, t, d)])
    x2, hm0, a0, act0, w1_0, w2_0 = mlp_fwd("0", x1, g_m0, 2)

    w_oin_g, w_oout_g = gathered(4, "o", x2)
    w_oout = w_oout_g.reshape(cd, d)
    h1 = _norm_fwd("o_norm", x2, g_o, tr)
    oin = Mat(_unstack_cols(w_oin_g), d, 3 * cd)
    tn_o = _pick(_gcd(3 * cd // N_CHIPS, cd), 512)
    proj3, = _matmul("o_proj", Mat(h1, t, d), oin, "nn", [_out(t, 3 * cd, F32, "colstack", (), (3, t, cd))],
                     tm, _pick(cd, 1024), kd)
    tc = _pick(cd, 256)
    bz = _conv_fwd(proj3, conv_w, tc)
    x3, = _matmul("o_out", Mat(bz, t, cd), Mat(w_oout, cd, d), "nn", [_out(t, d, F32)], tm, _pick(d, 1024), _pick(cd, 2048),
                  epilogue=lambda z, r: (z + r,), extras=[Mat(x2, t, d)])
    x4, hm1, a1, act1, w1_1, w2_1 = mlp_fwd("1", x3, g_m1, 5)

    def final_fn(xv, gv, tv):
        r = lax.rsqrt(jnp.mean(xv * xv, axis=-1, keepdims=True) + EPS)
        xh = xv * r
        err = xh * gv - tv
        dy = err * (1.0 / d)
        dxh = dy * gv
        dx = r * (dxh - xh * jnp.mean(dxh * xh, axis=-1, keepdims=True))
        sq = jnp.sum(err * err, axis=0, keepdims=True)
        part = sq[:, :LANES]
        for k in range(1, d // LANES):
            part = part + sq[:, k * LANES:(k + 1) * LANES]
        return dx, dx, part, jnp.sum(dy * xh, axis=0, keepdims=True)

    dx4, dx4b, loss_vec, dg_f = _rowwise("loss_final_norm", final_fn, t // tr, [_rt(x4, tr), _whole(g_f), _rt(tgt, tr)],
                                         [_rt_out(t, d, F32, tr), _rt_out(t, d, BF16, tr)],
                                         [jax.ShapeDtypeStruct((1, LANES), F32), jax.ShapeDtypeStruct((1, d), F32)])

    dx3, dx3b, dg_m1, sc_m1, tok = mlp_bwd("1", dx4, dx4b, x3, g_m1, w1_1, w2_1, hm1, a1, act1, ())

    dbz, = _matmul("o_out_dx", Mat(dx3b, t, d), Mat(w_oout, cd, d), "nt", [_out(t, cd, F32)], tm, _pick(cd, 1024), kd,
                   deps=(tok,))
    dw_oout, = _matmul("o_out_dw", Mat(bz, t, cd), Mat(dx3b, t, d), "tn", [_out(cd, d, BF16)], _pick(cd, 1024), _pick(d, 1024), kt)
    dproj3, dconv = _conv_bwd(proj3, conv_w, dbz, tc)
    dp3 = Mat(dproj3, t, 3 * cd, "colstack")
    dw_oin, = _matmul("o_proj_dw", Mat(h1, t, d), dp3, "tn", [_out(d, 3 * cd, BF16, "colstack", (), (N_CHIPS, d, 3 * cd // N_CHIPS))],
                      _pick(d, 2048), tn_o, kt)
    started_o, tok = pair_start("o", [dw_oin, dw_oout.reshape(N_CHIPS, cd // N_CHIPS, d)])
    dh1, = _matmul("o_proj_dx", dp3, oin, "nt", [_out(t, d, F32)], tm, _pick(d, 1024), _pick(cd, 2048), deps=(tok,))
    dx2, dx2b, dg_o = _norm_bwd("o_norm_bwd", dh1, x2, g_o, dx3, tr)
    sc_o, tok = pair_finish("o", started_o, dx2)

    dconv_s = jnp.transpose(dconv[:3].reshape(3, N_CHIPS, cd // N_CHIPS), (1, 0, 2))
    gsmall = jnp.concatenate([jnp.pad(dg_o.reshape(N_CHIPS, 1, d // N_CHIPS), ((0, 0), (0, 15), (0, 0))),
                              jnp.pad(dconv_s, ((0, 0), (0, 13), (0, 0)))], axis=1)
    dx1, dx1b, dg_m0, sc_m0, tok = mlp_bwd("0", dx2, dx2b, x1, g_m0, w1_0, w2_0, hm0, a0, act0, (tok,))

    dmixed, = _matmul("e_out_dx", Mat(dx1b, t, d), Mat(w_eout, mix, d), "nt", [_out(t, mix, F32)], tm, _pick(mix, 1024), kd,
                      deps=(tok,))
    dw_eout, = _matmul("e_out_dw", Mat(mixed, t, mix), Mat(dx1b, t, d), "tn", [_out(mix, d, BF16)], _pick(mix, 1024), _pick(d, 1024), kt)

    def mixb_fn(dm, a, s, ga, gs):
        da, dga = _rms_bwd(dm[:, :hw], a, ga)
        dsg, dgs = _rms_bwd(dm[:, hw:], s, gs)
        prod = da * a
        cols = [jnp.broadcast_to(jnp.sum(prod[:, h * LANES:(h + 1) * LANES], axis=-1, keepdims=True), (tr, LANES))
                for h in range(heads)]
        return da, dsg, jnp.stack([_row_of(c) for c in cols], axis=0), dga, dgs

    da_b, dsgu, delta_row, dg_mla, dg_sgu = _rowwise(
        "mix_norm_bwd", mixb_fn, t // tr, [_rt(dmixed, tr), _rt(att, tr), _rt(sgu, tr), _whole(g_mla), _whole(g_sgu)],
        [_rt_out(t, hw, BF16, tr), _rt_out(t, gw, F32, tr),
         (jax.ShapeDtypeStruct((heads, 8, t), F32), pl.BlockSpec((heads, 8, tr), lambda i: (0, 0, i)))],
        [jax.ShapeDtypeStruct((1, hw), F32), jax.ShapeDtypeStruct((1, gw), F32)])

    dproj, dsgu_w, dsgu_b8, dg_vn = _sgu_bwd(proj, dsgu, g_vn, sgu_w, sgu_b, groups, rb)
    dq1, dq2, dk1, dvv, dkr_h = _attn_bwd(qall, kvall, kr, da_b, lse_row, delta_row, heads, scale, min(2 * tr, t))
    dqfull, dproj = _rope_bwd(dq1, dq2, dkr_h, ctab, stab, heads, tr, dproj, kr_cb)
    dkvall = jnp.concatenate([dk1, dvv], axis=1)
    dw_q, = _matmul("q_up_dw", Mat(qn, t, ql), Mat(dqfull, t, 2 * hw), "tn", [_out(ql, 2 * hw, BF16)], ql, _pick(2 * hw, 1024), kt)
    dqn, = _matmul("q_up_dx", Mat(dqfull, t, 2 * hw), Mat(w_q_all, ql, 2 * hw), "nt", [_out(t, ql, F32)], tm, ql, _pick(2 * hw, 2048))
    dw_kv, = _matmul("kv_up_dw", Mat(kvn, t, kvl), Mat(dkvall, t, 2 * hw), "tn", [_out(kvl, 2 * hw, BF16)], kvl, _pick(2 * hw, 1024), kt)
    dkvn, = _matmul("kv_up_dx", Mat(dkvall, t, 2 * hw), Mat(w_kv_all, kvl, 2 * hw), "nt", [_out(t, kvl, F32)], tm, kvl, _pick(2 * hw, 2048))

    def qkvb_fn(da, db, a, b, ga, gb):
        dxa, dga = _rms_bwd(da, a, ga)
        dxb, dgb = _rms_bwd(db, b, gb)
        return jnp.concatenate([dxa, dxb], axis=1), dga, dgb

    assert (2 * gw) % (ql + kvl) == 0
    into = (jax.ShapeDtypeStruct(dproj.shape, dproj.dtype),
            pl.BlockSpec((tr, ql + kvl), lambda i: (i, 2 * gw // (ql + kvl))))
    dproj, dg_q, dg_kv = _rowwise(
        "qkv_norm_bwd", qkvb_fn, t // tr,
        [_rt(dqn, tr), _rt(dkvn, tr), _rt(proj, tr, ql, cq_cb), _rt(proj, tr, kvl, ckv_cb), _whole(g_q), _whole(g_kv)],
        [into], [jax.ShapeDtypeStruct((1, ql), F32), jax.ShapeDtypeStruct((1, kvl), F32)], deps=(dproj,), fill=(0, 0))
    dw_in, = _matmul("e_proj_dw", Mat(dproj, t, pi), Mat(h0, t, d), "tn", [_out(pi, d, F32)], _pick(pi, 1024), _pick(d, 2048), kt)
    dh0, = _matmul("e_proj_dx", Mat(dproj, t, pi), Mat(w_in_all, d, pi), "nt", [_out(t, d, F32)], tm, _pick(d, 1024), _pick(pi, 4096))
    dx0, _, dg_e = _norm_bwd("e_norm_bwd", dh0, xs, g_e, dx1, tr)

    kr0 = 2 * gw + c2
    gw_in = jnp.concatenate([dw_in[2 * gw:kr0], dw_in[kr0:kr0 + ROPE_HALF], dw_in[kr0 + ROPE:kr0 + ROPE + ROPE_HALF],
                             dw_in[:2 * gw]], axis=0).reshape(N_CHIPS, ei // N_CHIPS, d)
    gq = jnp.concatenate([dw_q[:, :hw].reshape(ql, heads, LANES), _unpad_rope(dw_q[:, hw:].reshape(ql, heads, LANES))], axis=-1)
    gw_uq = _stack_cols(gq.reshape(ql, heads * (LANES + ROPE)))
    gkv = jnp.concatenate([dw_kv[:, :hw].reshape(kvl, heads, LANES), dw_kv[:, hw:].reshape(kvl, heads, LANES)], axis=-1)
    gw_ukv = _stack_cols(gkv.reshape(kvl, heads * 2 * LANES))
    started_e, tok_pair = pair_start("e", [gw_in, gw_uq, gw_ukv, dw_eout.reshape(N_CHIPS, mix // N_CHIPS, d), gsmall])

    small_like = [e_norm_mix, e_q_norm, e_kv_norm, e_v_norm, e_sgu_w, e_sgu_b, e_mla_out_norm, e_sgu_out_norm, mlp_norm, final_norm]
    small_grads = [dg_e, dg_q, dg_kv, dg_vn, dsgu_w, dsgu_b8[:, 0, :], dg_mla, dg_sgu, jnp.concatenate([dg_m0, dg_m1], axis=0), dg_f]
    packed = _pack_small(small_grads)
    n_small = packed.shape[0] + (-packed.shape[0]) % 8
    pad = n_small - packed.shape[0] + 8
    sflat = jnp.concatenate([jnp.pad(packed, ((0, pad - 8), (0, 0))), jnp.pad(loss_vec, ((0, 7), (0, 0)))], axis=0)
    small_started, tok_small = _exchange_start("small_start", _all_route, 7, [sflat], [_spread(sflat)])

    sh_m1, tok = summed("m1", sc_m1, (tok_pair, tok_small))
    sc_e, tok = pair_finish("e", started_e, tok)
    sh_o, tok = summed("o", sc_o, tok)
    sh_m0, tok = summed("m0", sc_m0, tok)
    r_oin, r_oout = shared("o", sh_o, tok)
    late = {"o_w_in": _adamw(o_w_in, [r_oin], m_o_w_in, v_o_w_in),
            "o_w_out": _adamw(o_w_out, [r_oout], m_o_w_out, v_o_w_out)}
    r_w1_1, r_w2_1 = shared("m1", sh_m1, late["o_w_in"][1])
    r_w1_0, r_w2_0 = shared("m0", sh_m0, r_w2_1)
    late["mlp_w1"] = _adamw(mlp_w1, [r_w1_0, r_w1_1], m_mlp_w1, v_mlp_w1)
    sh_e, tok = summed("e", sc_e, late["mlp_w1"][1])
    late["mlp_w2"] = _adamw(mlp_w2, [r_w2_0, r_w2_1], m_mlp_w2, v_mlp_w2, deps=[tok])

    _, (all_small,) = _exchange_wait("small_wait", _all_route, small_started, late["mlp_w2"][1])
    g_small = _sum_devices(all_small)
    loss = 0.5 * jnp.sum(g_small[n_small]) / d

    def padded(arrs):
        return jnp.pad(_pack_small(arrs), ((0, pad), (0, 0)))

    s_m = [m_e_norm_mix, m_e_q_norm, m_e_kv_norm, m_e_v_norm, m_e_sgu_w, m_e_sgu_b, m_e_mla_out_norm, m_e_sgu_out_norm, m_mlp_norm, m_final_norm]
    s_v = [v_e_norm_mix, v_e_q_norm, v_e_kv_norm, v_e_v_norm, v_e_sgu_w, v_e_sgu_b, v_e_mla_out_norm, v_e_sgu_out_norm, v_mlp_norm, v_final_norm]
    s_out = [_unpack_small(o[0], small_like)
             for o in _adamw(padded(small_like)[None], [g_small], padded(s_m)[None], padded(s_v)[None])]

    r_in, r_uq, r_ukv, r_eout, r_small = shared("e", sh_e, (tok, late["mlp_w2"][1]))
    sm = [o[0] for o in _adamw(small_shard[None], [r_small], _small_shard(m_o_norm_mix, m_o_conv_w[0])[None],
                               _small_shard(v_o_norm_mix, v_o_conv_w[0])[None])]
    big = dict(late)
    flip = lambda a: jnp.swapaxes(a, 1, 2)
    big.update({
        "e_w_in": [flip(o) for o in _adamw(flip(e_w_in), [r_in], flip(m_e_w_in), flip(v_e_w_in))],
        "e_w_uq": _adamw(e_w_uq, [r_uq], m_e_w_uq, v_e_w_uq),
        "e_w_ukv": _adamw(e_w_ukv, [r_ukv], m_e_w_ukv, v_e_w_ukv),
        "e_w_out": _adamw(e_w_out, [r_eout], m_e_w_out, v_e_w_out),
    })

    names = ["e_norm_mix", "e_w_in", "e_q_norm", "e_w_uq", "e_kv_norm", "e_w_ukv", "e_v_norm", "e_sgu_w", "e_sgu_b",
             "e_mla_out_norm", "e_sgu_out_norm", "e_w_out", "o_norm_mix", "o_w_in", "o_conv_w", "o_w_out",
             "mlp_norm", "mlp_w1", "mlp_w2", "final_norm"]
    shapes = {"e_w_in": e_w_in.shape, "e_w_uq": e_w_uq.shape, "e_w_ukv": e_w_ukv.shape, "e_w_out": e_w_out.shape,
              "o_w_in": o_w_in.shape, "o_w_out": o_w_out.shape, "mlp_w1": mlp_w1.shape, "mlp_w2": mlp_w2.shape}
    small_names = ["e_norm_mix", "e_q_norm", "e_kv_norm", "e_v_norm", "e_sgu_w", "e_sgu_b", "e_mla_out_norm",
                   "e_sgu_out_norm", "mlp_norm", "final_norm"]

    def leaf(kind, name):
        if name in big:
            return big[name][kind].reshape(shapes[name])
        if name == "o_norm_mix":
            return sm[kind][0:1]
        if name == "o_conv_w":
            return sm[kind][16:19].reshape(o_conv_w.shape)
        return s_out[kind][small_names.index(name)]

    outs = [loss, dx0.reshape(x.shape)]
    for kind in range(4):
        outs += [leaf(kind, nm) for nm in names]
    return tuple(outs)


def _gcd(a, b):
    while b:
        a, b = b, a % b
    return a
```

```python
import jax
import jax.numpy as jnp
from jax import lax
from jax.experimental import pallas as pl
from jax.experimental.pallas import tpu as pltpu

F32 = jnp.float32
BF16 = jnp.bfloat16
MESH = pl.DeviceIdType.MESH

LANES = 128
ROPE = 64
ROPE_HALF = ROPE // 2
ROPE_BASE = 10000.0
EPS = 1e-6
N_CHIPS = 4
VMEM_LIMIT = 48 * 1024 * 1024
NEG = -1e30

ADAM_LR = 0.001
ADAM_B1 = 0.9
ADAM_B2 = 0.999
ADAM_EPS = 1e-08
ADAM_WD = 0.01
ADAM_STEP = 10


def _pick(n, target, step=LANES):
    best = None
    for t in range(step, min(n, target) + 1, step):
        if n % t == 0:
            best = t
    return best if best is not None else n


def _params(sem, vmem=VMEM_LIMIT):
    return pltpu.CompilerParams(dimension_semantics=sem, vmem_limit_bytes=vmem)


class Mat:
    def __init__(self, arr, rows, cols, kind="plain", lead=(), cmap=None, shape=None, dtype=None):
        self.arr, self.rows, self.cols, self.kind, self.lead, self.cmap = arr, rows, cols, kind, tuple(lead), cmap
        self.shape = tuple(arr.shape) if arr is not None else tuple(shape)
        self.dtype = arr.dtype if arr is not None else dtype

    def sds(self):
        return jax.ShapeDtypeStruct(self.shape, self.dtype)

    def spec(self, br, bc, gridmap):
        lead, nl = self.lead, len(self.lead)
        if self.kind == "plain":
            assert self.rows % br == 0 and self.cols % bc == 0, (self.shape, br, bc)
            cmap = self.cmap if self.cmap is not None else (lambda cb, _: cb)
            block = (None,) * nl + (br, bc)

            def phys(rb, cb):
                return lead + (rb, cmap(cb, bc))
        elif self.kind == "colstack":
            cs = self.shape[-1]
            assert cs % bc == 0 and self.rows % br == 0, (self.shape, br, bc)
            q = cs // bc
            block = (None,) * (nl + 1) + (br, bc)

            def phys(rb, cb):
                return (cb // q,) + lead + (rb, cb % q)
        else:
            rs = self.shape[-2]
            assert rs % br == 0 and self.cols % bc == 0, (self.shape, br, bc)
            q = rs // br
            block = (None,) * (nl + 1) + (br, bc)

            def phys(rb, cb):
                return (rb // q,) + lead + (rb % q, cb)

        return pl.BlockSpec(block, lambda *g: phys(*gridmap(*g)))


def _adamw_math(w, g, m, v):
    mn = ADAM_B1 * m + (1.0 - ADAM_B1) * g
    vn = ADAM_B2 * v + (1.0 - ADAM_B2) * jnp.square(g)
    m_hat = mn / (1.0 - ADAM_B1 ** ADAM_STEP)
    v_hat = vn / (1.0 - ADAM_B2 ** ADAM_STEP)
    return -ADAM_LR * (m_hat / (jnp.sqrt(v_hat) + ADAM_EPS) + ADAM_WD * w), mn, vn


def _matmul(name, a, b, mode, outs, tm, tn, tk, epilogue=None, extras=(), deps=()):
    if mode == "nn":
        m, k, n = a.rows, a.cols, b.cols
        a_spec = a.spec(tm, tk, lambda i, j, kk: (i, kk))
        b_spec = b.spec(tk, tn, lambda i, j, kk: (kk, j))
        dims = (((1,), (0,)), ((), ()))
    elif mode == "nt":
        m, k, n = a.rows, a.cols, b.rows
        a_spec = a.spec(tm, tk, lambda i, j, kk: (i, kk))
        b_spec = b.spec(tn, tk, lambda i, j, kk: (j, kk))
        dims = (((1,), (1,)), ((), ()))
    else:
        k, m, n = a.rows, a.cols, b.cols
        a_spec = a.spec(tk, tm, lambda i, j, kk: (kk, i))
        b_spec = b.spec(tk, tn, lambda i, j, kk: (kk, j))
        dims = (((0,), (0,)), ((), ()))
    assert m % tm == 0 and n % tn == 0 and k % tk == 0, (name, m, n, k, tm, tn, tk)
    grid = (m // tm, n // tn, k // tk)
    nk = grid[2]
    n_ex, n_out, n_dep = len(extras), len(outs), len(deps)
    tile = lambda i, j, kk: (i, j)

    def finish(z, ex, out_refs):
        vals = epilogue(z, *[e[...] for e in ex]) if epilogue is not None else (z,)
        for o, v in zip(out_refs, vals):
            o[...] = v.astype(o.dtype)

    def body_single(a_ref, b_ref, *rest):
        finish(lax.dot_general(a_ref[...], b_ref[...], dims, preferred_element_type=F32),
               rest[:n_ex], rest[n_ex + n_dep:n_ex + n_dep + n_out])

    def body_acc(a_ref, b_ref, *rest):
        acc = rest[-1]
        kk = pl.program_id(2)

        @pl.when(kk == 0)
        def _():
            acc[...] = jnp.zeros_like(acc)

        acc[...] += lax.dot_general(a_ref[...], b_ref[...], dims, preferred_element_type=F32)

        @pl.when(kk == nk - 1)
        def _():
            finish(acc[...], rest[:n_ex], rest[n_ex + n_dep:n_ex + n_dep + n_out])

    res = pl.pallas_call(
        body_single if nk == 1 else body_acc, name=name, grid=grid,
        in_specs=[a_spec, b_spec] + [e.spec(tm, tn, tile) for e in extras]
        + [pl.BlockSpec(memory_space=pl.ANY) for _ in deps],
        out_specs=[o.spec(tm, tn, tile) for o in outs],
        out_shape=[o.sds() for o in outs],
        scratch_shapes=[] if nk == 1 else [pltpu.VMEM((tm, tn), F32)],
        compiler_params=_params(("parallel", "parallel", "arbitrary")),
    )(a.arr, b.arr, *[e.arr for e in extras], *deps)
    return res


def _out(rows, cols, dtype, kind="plain", lead=(), shape=None):
    return Mat(None, rows, cols, kind, lead, shape=shape if shape is not None else (rows, cols), dtype=dtype)


def _rt(arr, tr, width=None, cb=0):
    width = arr.shape[1] if width is None else width
    return arr, pl.BlockSpec((tr, width), lambda i: (i, cb))


def _whole(arr):
    nd = arr.ndim
    return arr, pl.BlockSpec(arr.shape, lambda i: (0,) * nd)


def _rowwise(name, fn, n_steps, ins, outs, accs=(), deps=(), fill=None):
    n_in, n_out, n_acc, n_dep = len(ins), len(outs), len(accs), len(deps)

    def body(*refs):
        vals = fn(*[r[...] for r in refs[:n_in]])
        if not isinstance(vals, (tuple, list)):
            vals = (vals,)
        for ref, v in zip(refs[n_in + n_dep:n_in + n_dep + n_out], vals[:n_out]):
            ref[...] = v.astype(ref.dtype)
        if n_acc:
            acc_refs = refs[n_in + n_dep + n_out:]

            @pl.when(pl.program_id(0) == 0)
            def _():
                for ref in acc_refs:
                    ref[...] = jnp.zeros_like(ref)

            for ref, v in zip(acc_refs, vals[n_out:]):
                ref[...] += v

    acc_specs = [pl.BlockSpec(s.shape, lambda i, nd=len(s.shape): (0,) * nd) for s in accs]
    res = pl.pallas_call(
        body, name=name, grid=(n_steps,),
        in_specs=[s for _, s in ins] + [pl.BlockSpec(memory_space=pl.ANY) for _ in deps],
        out_specs=[s for _, s in outs] + acc_specs,
        out_shape=[o for o, _ in outs] + list(accs),
        input_output_aliases={} if fill is None else {n_in + fill[0]: fill[1]},
        compiler_params=_params(("arbitrary",) if n_acc else ("parallel",)),
    )(*[a for a, _ in ins], *deps)
    return res


def _rt_out(t, width, dtype, tr):
    return jax.ShapeDtypeStruct((t, width), dtype), pl.BlockSpec((tr, width), lambda i: (i, 0))


def _rms(x, g):
    r = lax.rsqrt(jnp.mean(x * x, axis=-1, keepdims=True) + EPS)
    return x * r * g


def _rms_bwd(dy, x, g):
    r = lax.rsqrt(jnp.mean(x * x, axis=-1, keepdims=True) + EPS)
    xh = x * r
    dxh = dy * g
    dx = r * (dxh - xh * jnp.mean(dxh * xh, axis=-1, keepdims=True))
    dg = jnp.sum(dy * xh, axis=0, keepdims=True)
    return dx, dg


def _gelu_and_grad(x):
    k = 0.7978845608028654
    x2 = x * x
    th = jnp.tanh(k * (x + 0.044715 * (x2 * x)))
    half = 0.5 * (1.0 + th)
    return x * half, half + 0.5 * x * (1.0 - th * th) * (k * (1.0 + 3.0 * 0.044715 * x2))


def _gelu(x):
    return _gelu_and_grad(x)[0]


def _gelu_grad(x):
    return _gelu_and_grad(x)[1]


def _norm_fwd(name, x, g, tr):
    t, d = x.shape
    return _rowwise(name, lambda xv, gv: _rms(xv, gv), t // tr, [_rt(x, tr), _whole(g)], [_rt_out(t, d, BF16, tr)])[0]


def _norm_bwd(name, dh, x, g, dres, tr):
    t, d = x.shape

    def fn(dhv, xv, gv, drv):
        dx, dg = _rms_bwd(dhv, xv, gv)
        dx = dx + drv
        return dx, dx, dg

    return _rowwise(name, fn, t // tr, [_rt(dh, tr), _rt(x, tr), _whole(g), _rt(dres, tr)],
                    [_rt_out(t, d, F32, tr), _rt_out(t, d, BF16, tr)], [jax.ShapeDtypeStruct((1, d), F32)])


def _rope_tables(posf, invf, cmask, smask, tr):
    t = posf.shape[0]

    def fn(p, f, cm, sm):
        ang = p * f
        return jnp.cos(ang) * cm, jnp.sin(ang) * sm

    return _rowwise("rope_tables", fn, t // tr, [_rt(posf, tr), _whole(invf), _whole(cmask), _whole(smask)],
                    [_rt_out(t, LANES, F32, tr), _rt_out(t, LANES, F32, tr)])


def _rot(v, c, s):
    return v * c + pltpu.roll(v, ROPE, axis=1) * s


def _rot_bwd(dv, c, s):
    return dv * c + pltpu.roll(dv * s, ROPE, axis=1)


def _rope_fwd(qfull, proj, kr_cb, ctab, stab, heads, tr):
    t = qfull.shape[0]
    hw = heads * LANES

    def fn(q, kr, c, s):
        parts = [q[:, :hw]] + [_rot(q[:, hw + h * LANES: hw + (h + 1) * LANES], c, s) for h in range(heads)]
        return jnp.concatenate(parts, axis=1), _rot(kr, c, s)

    return _rowwise("rope_fwd", fn, t // tr, [_rt(qfull, tr), _rt(proj, tr, LANES, kr_cb), _rt(ctab, tr), _rt(stab, tr)],
                    [_rt_out(t, 2 * hw, BF16, tr), _rt_out(t, LANES, BF16, tr)])


def _rope_bwd(dq1, dq2, dkr_h, ctab, stab, heads, tr, dproj, kr_cb):
    t = dq1.shape[0]
    hw = heads * LANES

    def fn(a, b, dk, c, s):
        parts = [a] + [_rot_bwd(b[:, h * LANES:(h + 1) * LANES], c, s) for h in range(heads)]
        dks = dk[0]
        for h in range(1, heads):
            dks = dks + dk[h]
        return jnp.concatenate(parts, axis=1), _rot_bwd(dks, c, s)

    dk_spec = pl.BlockSpec((heads, tr, LANES), lambda i: (0, i, 0))
    into = (jax.ShapeDtypeStruct(dproj.shape, dproj.dtype), pl.BlockSpec((tr, LANES), lambda i: (i, kr_cb)))
    return _rowwise("rope_bwd", fn, t // tr, [_rt(dq1, tr), _rt(dq2, tr), (dkr_h, dk_spec), _rt(ctab, tr), _rt(stab, tr)],
                    [_rt_out(t, 2 * hw, BF16, tr), into], deps=(dproj,), fill=(0, 1))


def _dot_nt(a, b):
    return lax.dot_general(a, b, (((1,), (1,)), ((), ())), preferred_element_type=F32)


def _dot_tn(a, b):
    return lax.dot_general(a, b, (((0,), (0,)), ((), ())), preferred_element_type=F32)


def _dot(a, b):
    return jnp.dot(a, b, preferred_element_type=F32)


def _ranges(n_blocks):
    n_var = min(4, n_blocks)
    assert n_blocks % n_var == 0
    return n_var, n_blocks // n_var


def _row_of(col):
    return col.T[:8, :]


def _attn_fwd(qall, kvall, kr, heads, scale, tq):
    t = qall.shape[0]
    nq = t // tq
    n_var, per = _ranges(nq)

    def body(qn_ref, qr_ref, kn_ref, v_ref, kr_ref, o_ref, lser_ref):
        i = pl.program_id(1)
        for var in range(n_var):
            kv = (var + 1) * per * tq

            @pl.when(jnp.logical_and(i >= var * per, i < (var + 1) * per))
            def _(kv=kv):
                s = _dot_nt(jnp.concatenate([qn_ref[...], qr_ref[...]], axis=1),
                            jnp.concatenate([kn_ref[:kv, :], kr_ref[:kv, :]], axis=1)) * scale
                rows = i * tq + lax.broadcasted_iota(jnp.int32, (tq, kv), 0)
                cols = lax.broadcasted_iota(jnp.int32, (tq, kv), 1)
                s = jnp.where(cols <= rows, s, NEG)
                m = jnp.max(s, axis=-1, keepdims=True)
                p = jnp.exp(s - m)
                l = jnp.sum(p, axis=-1, keepdims=True)
                o_ref[...] = _dot(p.astype(BF16), v_ref[:kv, :]) / l
                lser_ref[...] = _row_of(jnp.broadcast_to(m + jnp.log(l), (tq, LANES)))

    return pl.pallas_call(
        body, name="attn_fwd", grid=(heads, nq),
        in_specs=[pl.BlockSpec((tq, LANES), lambda h, i: (i, h)),
                  pl.BlockSpec((tq, LANES), lambda h, i: (i, heads + h)),
                  pl.BlockSpec((t, LANES), lambda h, i: (0, h)),
                  pl.BlockSpec((t, LANES), lambda h, i: (0, heads + h)),
                  pl.BlockSpec((t, LANES), lambda h, i: (0, 0))],
        out_specs=[pl.BlockSpec((tq, LANES), lambda h, i: (i, h)),
                   pl.BlockSpec((None, 8, tq), lambda h, i: (h, 0, i))],
        out_shape=[jax.ShapeDtypeStruct((t, heads * LANES), F32), jax.ShapeDtypeStruct((heads, 8, t), F32)],
        compiler_params=_params(("parallel", "parallel")),
    )(qall, qall, kvall, kvall, kr)


def _attn_bwd(qall, kvall, kr, do, lse_row, delta_row, heads, scale, tk):
    t = qall.shape[0]
    nk = t // tk
    n_var, per = _ranges(nk)

    def body(qn_ref, qr_ref, kn_ref, v_ref, kr_ref, do_ref, lse_ref, dl_ref, dq1_ref, dq2_ref, dk_ref, dv_ref, dkr_ref):
        j = pl.program_id(1)

        @pl.when(j == 0)
        def _():
            dq1_ref[...] = jnp.zeros_like(dq1_ref)
            dq2_ref[...] = jnp.zeros_like(dq2_ref)

        for var in range(n_var):
            q0 = var * per * tk
            nq = t - q0

            @pl.when(jnp.logical_and(j >= var * per, j < (var + 1) * per))
            def _(q0=q0, nq=nq):
                qn, qr, do_v = qn_ref[q0:, :], qr_ref[q0:, :], do_ref[q0:, :]
                k1, k2 = kn_ref[...], kr_ref[...]
                qcat, kcat = jnp.concatenate([qn, qr], axis=1), jnp.concatenate([k1, k2], axis=1)
                st = _dot_nt(kcat, qcat) * scale
                keys = j * tk + lax.broadcasted_iota(jnp.int32, (tk, nq), 0)
                queries = q0 + lax.broadcasted_iota(jnp.int32, (tk, nq), 1)
                pt = jnp.where(keys <= queries, jnp.exp(st - lse_ref[0:1, q0:]), 0.0)
                dpt = _dot_nt(v_ref[...], do_v)
                dst = (pt * (dpt - dl_ref[0:1, q0:]) * scale).astype(BF16)
                dv_ref[...] = _dot(pt.astype(BF16), do_v).astype(dv_ref.dtype)
                dkc = _dot(dst, qcat)
                dk_ref[...] = dkc[:, :LANES].astype(dk_ref.dtype)
                dkr_ref[...] = dkc[:, LANES:]
                dqc = _dot_tn(dst, kcat)
                dq1_ref[q0:, :] += dqc[:, :LANES]
                dq2_ref[q0:, :] += dqc[:, LANES:]

    kblk = lambda off: pl.BlockSpec((tk, LANES), lambda h, j: (j, off + h))
    full = lambda off: pl.BlockSpec((t, LANES), lambda h, j: (0, off + h))
    stat = pl.BlockSpec((None, 8, t), lambda h, j: (h, 0, 0))
    return pl.pallas_call(
        body, name="attn_bwd", grid=(heads, nk),
        in_specs=[full(0), full(heads), kblk(0), kblk(heads), pl.BlockSpec((tk, LANES), lambda h, j: (j, 0)),
                  full(0), stat, stat],
        out_specs=[full(0), full(0), kblk(0), kblk(0), pl.BlockSpec((None, tk, LANES), lambda h, j: (h, j, 0))],
        out_shape=[jax.ShapeDtypeStruct((t, heads * LANES), F32)] * 2 + [jax.ShapeDtypeStruct((t, heads * LANES), BF16)] * 2
        + [jax.ShapeDtypeStruct((heads, t, LANES), F32)],
        compiler_params=_params(("parallel", "arbitrary")),
    )(qall, qall, kvall, kvall, kr, do, lse_row, delta_row)


def _tril():
    return lax.broadcasted_iota(jnp.int32, (LANES, LANES), 0) >= lax.broadcasted_iota(jnp.int32, (LANES, LANES), 1)


def _group_norm(vg):
    mu = jnp.mean(vg, axis=-1, keepdims=True)
    vc = vg - mu
    rs = lax.rsqrt(jnp.mean(vc * vc, axis=-1, keepdims=True) + EPS)
    return vc * rs, rs


def _sgu_fwd(proj, gain, w, bias, groups, rb):
    t = proj.shape[0]
    gw = groups * LANES
    cpb = rb // LANES

    def body(u_ref, v_ref, gain_ref, w_ref, b_ref, s_ref):
        tril = _tril()
        for g in range(groups):
            wt = jnp.where(tril, w_ref[g], 0.0).astype(BF16)
            cols = slice(g * LANES, (g + 1) * LANES)
            for ci in range(cpb):
                rows = slice(ci * LANES, (ci + 1) * LANES)
                ug = _gelu(u_ref[rows, cols])
                vh, _ = _group_norm(_gelu(v_ref[rows, cols]))
                vn = vh * gain_ref[:, cols]
                y = _dot(wt, vn.astype(BF16)) + b_ref[g]
                s_ref[rows, cols] = ug * y

    return pl.pallas_call(
        body, name="sgu_fwd", grid=(t // rb,),
        in_specs=[pl.BlockSpec((rb, gw), lambda i: (i, 0)), pl.BlockSpec((rb, gw), lambda i: (i, 1)),
                  pl.BlockSpec((1, gw), lambda i: (0, 0)),
                  pl.BlockSpec((groups, LANES, LANES), lambda i: (0, 0, 0)),
                  pl.BlockSpec((groups, LANES, LANES), lambda i: (0, 0, 0))],
        out_specs=pl.BlockSpec((rb, gw), lambda i: (i, 0)),
        out_shape=jax.ShapeDtypeStruct((t, gw), F32),
        compiler_params=_params(("parallel",)),
    )(proj, proj, gain, w, bias)


def _sgu_bwd(proj, ds, gain, w, bias, groups, rb):
    t, width = proj.shape
    gw = groups * LANES
    cpb = rb // LANES
    n_steps = t // rb

    def body(u_ref, v_ref, ds_ref, gain_ref, w_ref, b_ref, dp_ref, dw_ref, db_ref, dg_ref, dy_acc):
        du_ref, dv_ref = dp_ref.at[:, :gw], dp_ref.at[:, gw:]
        step = pl.program_id(0)

        @pl.when(step == 0)
        def _():
            dw_ref[...] = jnp.zeros_like(dw_ref)
            dy_acc[...] = jnp.zeros_like(dy_acc)
            dg_ref[...] = jnp.zeros_like(dg_ref)

        tril = _tril()
        for g in range(groups):
            wt = jnp.where(tril, w_ref[g], 0.0).astype(BF16)
            cols = slice(g * LANES, (g + 1) * LANES)
            gain_g = gain_ref[:, cols]
            for ci in range(cpb):
                rows = slice(ci * LANES, (ci + 1) * LANES)
                u_raw, v_raw, ds_v = u_ref[rows, cols], v_ref[rows, cols], ds_ref[rows, cols]
                ug, ug_grad = _gelu_and_grad(u_raw)
                vg, vg_grad = _gelu_and_grad(v_raw)
                vh, rs = _group_norm(vg)
                vn = (vh * gain_g).astype(BF16)
                y = _dot(wt, vn) + b_ref[g]
                dy = ds_v * ug
                dyb = dy.astype(BF16)
                du_ref[rows, cols] = (ds_v * y * ug_grad).astype(du_ref.dtype)
                dy_acc[g] += dy
                dw_ref[g] += _dot_nt(dyb, vn)
                dvn = _dot_tn(wt, dyb)
                dg_ref[:, cols] += jnp.sum(dvn * vh, axis=0, keepdims=True)
                dvh = dvn * gain_g
                dvg = rs * (dvh - jnp.mean(dvh, axis=-1, keepdims=True)
                            - vh * jnp.mean(dvh * vh, axis=-1, keepdims=True))
                dv_ref[rows, cols] = (dvg * vg_grad).astype(dv_ref.dtype)

        @pl.when(step == n_steps - 1)
        def _():
            ones = jnp.ones((8, LANES), F32)
            for g in range(groups):
                dw_ref[g] = jnp.where(tril, dw_ref[g], 0.0)
                db_ref[g] = lax.dot_general(ones, dy_acc[g], (((1,), (1,)), ((), ())),
                                            precision=lax.Precision.HIGHEST, preferred_element_type=F32)

    blk = lambda cb: pl.BlockSpec((rb, gw), lambda i: (i, cb))
    whole3 = pl.BlockSpec((groups, LANES, LANES), lambda i: (0, 0, 0))
    return pl.pallas_call(
        body, name="sgu_bwd", grid=(n_steps,),
        in_specs=[blk(0), blk(1), blk(0), pl.BlockSpec((1, gw), lambda i: (0, 0)), whole3, whole3],
        out_specs=[pl.BlockSpec((rb, 2 * gw), lambda i: (i, 0)), whole3,
                   pl.BlockSpec((groups, 8, LANES), lambda i: (0, 0, 0)), pl.BlockSpec((1, gw), lambda i: (0, 0))],
        out_shape=[jax.ShapeDtypeStruct((t, width), BF16),
                   jax.ShapeDtypeStruct((groups, LANES, LANES), F32), jax.ShapeDtypeStruct((groups, 8, LANES), F32),
                   jax.ShapeDtypeStruct((1, gw), F32)],
        scratch_shapes=[pltpu.VMEM((groups, LANES, LANES), F32)],
        compiler_params=_params(("arbitrary",)),
    )(proj, proj, ds, gain, w, bias)


def _shift_down(z, s):
    rows = lax.broadcasted_iota(jnp.int32, z.shape, 0)
    return jnp.where(rows >= s, pltpu.roll(z, s, axis=0), 0.0)


def _shift_up(z, s):
    n = z.shape[0]
    rows = lax.broadcasted_iota(jnp.int32, z.shape, 0)
    return jnp.where(rows < n - s, pltpu.roll(z, n - s, axis=0), 0.0)


def _conv_fwd(proj3, cw, tc):
    _, t, cd = proj3.shape

    def body(p_ref, w_ref, o_ref):
        z = p_ref[1] * p_ref[2]
        w = w_ref[...]
        zc = w[2:3] * z + w[1:2] * _shift_down(z, 1) + w[0:1] * _shift_down(z, 2)
        o_ref[...] = (p_ref[0] * zc).astype(o_ref.dtype)

    return pl.pallas_call(
        body, name="conv_fwd", grid=(cd // tc,),
        in_specs=[pl.BlockSpec((3, t, tc), lambda j: (0, 0, j)), pl.BlockSpec((8, tc), lambda j: (0, j))],
        out_specs=pl.BlockSpec((t, tc), lambda j: (0, j)),
        out_shape=jax.ShapeDtypeStruct((t, cd), BF16),
        compiler_params=_params(("parallel",)),
    )(proj3, cw)


def _conv_bwd(proj3, cw, dbz, tc):
    _, t, cd = proj3.shape

    def body(p_ref, w_ref, d_ref, o_ref, dw_ref):
        b, c, xin = p_ref[0], p_ref[1], p_ref[2]
        w = w_ref[...]
        z = c * xin
        z1, z2 = _shift_down(z, 1), _shift_down(z, 2)
        zc = w[2:3] * z + w[1:2] * z1 + w[0:1] * z2
        d = d_ref[...]
        dzc = d * b
        dz = w[2:3] * dzc + w[1:2] * _shift_up(dzc, 1) + w[0:1] * _shift_up(dzc, 2)
        o_ref[0] = (d * zc).astype(o_ref.dtype)
        o_ref[1] = (dz * xin).astype(o_ref.dtype)
        o_ref[2] = (dz * c).astype(o_ref.dtype)
        row = lax.broadcasted_iota(jnp.int32, (8, tc), 0)
        dw0 = jnp.sum(dzc * z2, axis=0, keepdims=True)
        dw1 = jnp.sum(dzc * z1, axis=0, keepdims=True)
        dw2 = jnp.sum(dzc * z, axis=0, keepdims=True)
        dw_ref[...] = jnp.where(row == 0, dw0, 0.0) + jnp.where(row == 1, dw1, 0.0) + jnp.where(row == 2, dw2, 0.0)

    return pl.pallas_call(
        body, name="conv_bwd", grid=(cd // tc,),
        in_specs=[pl.BlockSpec((3, t, tc), lambda j: (0, 0, j)), pl.BlockSpec((8, tc), lambda j: (0, j)),
                  pl.BlockSpec((t, tc), lambda j: (0, j))],
        out_specs=[pl.BlockSpec((3, t, tc), lambda j: (0, 0, j)), pl.BlockSpec((8, tc), lambda j: (0, j))],
        out_shape=[jax.ShapeDtypeStruct((3, t, cd), BF16), jax.ShapeDtypeStruct((8, cd), F32)],
        compiler_params=_params(("parallel",)),
    )(proj3, cw, dbz)


def _place():
    x, y, c = lax.axis_index("x"), lax.axis_index("y"), lax.axis_index("c")
    chips = [(1 - x, y), (x, 1 - y), (1 - x, 1 - y)]
    return x, y, c, chips


def _any_specs(n):
    return [pl.BlockSpec(memory_space=pl.ANY) for _ in range(n)]


HBM_SPEC = pl.BlockSpec(memory_space=pltpu.HBM)
SEM_SPEC = pl.BlockSpec(memory_space=pltpu.SEMAPHORE)
ORDERED_EFFECT = pltpu.SideEffectType.DATAFLOW_SIDE_EFFECTING


def _in_hbm(a):
    return pltpu.with_memory_space_constraint(a, pltpu.HBM)


def _token():
    return jax.ShapeDtypeStruct((8, LANES), F32), pl.BlockSpec(memory_space=pltpu.VMEM)


def _gather_start(name, groups):
    sizes = [len(g) for g in groups]
    flat = [b for g in groups for b in g]
    n, ng = len(flat), len(groups)

    def body(*refs):
        ins, sems, token = refs[:n], refs[n:n + 2 * ng], refs[-1]
        x, y, c, chips = _place()
        me = 2 * x + y
        i = 0
        for gi, size in enumerate(sizes):
            for j in range(size):
                blk = ins[i].at[me, c]
                for k, chip in enumerate(chips):
                    pltpu.make_async_remote_copy(src_ref=blk, dst_ref=blk, send_sem=sems[2 * gi].at[3 * j + k],
                                                 recv_sem=sems[2 * gi + 1].at[3 * j + k],
                                                 device_id=(*chip, c), device_id_type=MESH).start()
                i += 1
        token[...] = jnp.zeros_like(token)

    tok_shape, tok_spec = _token()
    res = pl.pallas_call(
        body, name=name,
        in_specs=[HBM_SPEC] * n,
        out_specs=[SEM_SPEC] * (2 * ng) + [HBM_SPEC] * n + [tok_spec],
        out_shape=[pltpu.SemaphoreType.DMA((3 * size,)) for size in sizes for _ in (0, 1)]
        + [pltpu.HBM(b.shape, b.dtype) for b in flat] + [tok_shape],
        input_output_aliases={i: 2 * ng + i for i in range(n)},
        compiler_params=pltpu.CompilerParams(has_side_effects=ORDERED_EFFECT),
    )(*[_in_hbm(b) for b in flat])
    out, i = [], 2 * ng
    for gi, size in enumerate(sizes):
        out.append((res[2 * gi], res[2 * gi + 1], list(res[i:i + size])))
        i += size
    return out, res[-1]


def _gather_wait(tag, send, recv, bufs, after):
    n = len(bufs)
    after = tuple(after) if isinstance(after, (tuple, list)) else (after,)

    def body(*refs):
        ins, send_ref, recv_ref = refs[:n], refs[n], refs[n + 1]
        x, y, c, chips = _place()
        me = 2 * x + y
        for j in range(n):
            for k, (px, py) in enumerate(chips):
                cp = pltpu.make_async_remote_copy(src_ref=ins[j].at[me, c], dst_ref=ins[j].at[2 * px + py, c],
                                                  send_sem=send_ref.at[3 * j + k], recv_sem=recv_ref.at[3 * j + k],
                                                  device_id=(px, py, c), device_id_type=MESH)
                cp.wait_send()
                cp.wait_recv()

    return pl.pallas_call(
        body, name="gather_wait_" + tag,
        in_specs=[HBM_SPEC] * n + [SEM_SPEC, SEM_SPEC] + _any_specs(len(after)),
        out_specs=[HBM_SPEC] * n,
        out_shape=[pltpu.HBM(b.shape, b.dtype) for b in bufs],
        input_output_aliases={i: i for i in range(n)},
        compiler_params=pltpu.CompilerParams(has_side_effects=ORDERED_EFFECT),
    )(*bufs, send, recv, *after)


def _gather_forward(tag, bufs):
    n = len(bufs)

    def body(*refs):
        ins, outs = refs[:n], refs[n:2 * n]
        send, recv = refs[2 * n:]
        x, y, c, chips = _place()
        sib = (x, y, 1 - c)

        def cp(i, k, slot, half):
            return pltpu.make_async_remote_copy(src_ref=ins[i].at[slot, half], dst_ref=outs[i].at[slot, half],
                                                send_sem=send.at[3 * i + k], recv_sem=recv.at[3 * i + k],
                                                device_id=sib, device_id_type=MESH)

        cps = [cp(i, k, 2 * px + py, c) for i in range(n) for k, (px, py) in enumerate(chips)]
        for d in cps:
            d.start()
        for i in range(n):
            for k, (px, py) in enumerate(chips):
                cp(i, k, 2 * px + py, 1 - c).wait_recv()
        for d in cps:
            d.wait_send()

    return pl.pallas_call(
        body, name="gather_forward_" + tag,
        in_specs=_any_specs(n), out_specs=_any_specs(n),
        out_shape=[jax.ShapeDtypeStruct(b.shape, b.dtype) for b in bufs],
        scratch_shapes=[pltpu.SemaphoreType.DMA((3 * n,))] * 2,
        input_output_aliases={i: i for i in range(n)},
        compiler_params=pltpu.CompilerParams(has_side_effects=True),
    )(*bufs)


def _pair_route(srcs, zones):
    x, y, c, _ = _place()
    return [(srcs[i].at[j, 1 - c], zones[i].at[j], (x, y, 1 - c)) for i in range(len(srcs)) for j in range(N_CHIPS)]


def _slab_route(srcs, zones):
    x, y, c, _ = _place()
    return [(srcs[i].at[j], zones[i].at[j], (x, y, 1 - c)) for i in range(len(srcs)) for j in range(N_CHIPS)]


def _chip_route(srcs, zones):
    x, y, c, chips = _place()
    return [(srcs[i].at[2 * px + py], zones[i].at[k], (px, py, c)) for i in range(len(srcs)) for k, (px, py) in enumerate(chips)]


def _all_route(srcs, zones):
    x, y, c, _ = _place()
    flips = [(fx, fy, fc) for fx in (0, 1) for fy in (0, 1) for fc in (0, 1)][1:]
    return [(srcs[0], zones[0].at[4 * x + 2 * y + c], (x + fx - 2 * x * fx, y + fy - 2 * y * fy, c + fc - 2 * c * fc))
            for fx, fy, fc in flips]


def _share_route(srcs, zones):
    x, y, c, _ = _place()
    return [(s.at[c], s.at[c], (x, y, 1 - c)) for s in srcs]


def _exchange_start(name, route, n_copies, srcs, zones):
    n, nz = len(srcs), len(zones)
    lands = [lax.empty(z, a.dtype) if isinstance(z, tuple) else z for z, a in zip(zones, srcs)]

    def body(*refs):
        ins, zone_refs, send, recv, token = refs[:n], refs[n:n + nz], refs[n + nz], refs[n + nz + 1], refs[-1]
        for k, (src, dst, dev) in enumerate(route(ins, zone_refs)):
            pltpu.make_async_remote_copy(src_ref=src, dst_ref=dst, send_sem=send.at[k], recv_sem=recv.at[k],
                                         device_id=dev, device_id_type=MESH).start()
        token[...] = jnp.zeros_like(token)

    tok_shape, tok_spec = _token()
    res = pl.pallas_call(
        body, name=name,
        in_specs=[HBM_SPEC] * (n + nz),
        out_specs=[SEM_SPEC, SEM_SPEC] + [HBM_SPEC] * (n + nz) + [tok_spec],
        out_shape=[pltpu.SemaphoreType.DMA((n_copies,))] * 2 + [pltpu.HBM(a.shape, a.dtype) for a in srcs + lands]
        + [tok_shape],
        input_output_aliases={i: 2 + i for i in range(n + nz)},
        compiler_params=pltpu.CompilerParams(has_side_effects=ORDERED_EFFECT),
    )(*[_in_hbm(a) for a in srcs + lands])
    return (res[0], res[1], list(res[2:2 + n]), list(res[2 + n:2 + n + nz])), res[-1]


def _exchange_wait(name, route, started, after):
    send, recv, srcs, lands = started
    n, nz = len(srcs), len(lands)
    after = tuple(after) if isinstance(after, (tuple, list)) else (after,)

    def body(*refs):
        ins, zone_refs, send_ref, recv_ref = refs[:n], refs[n:n + nz], refs[n + nz], refs[n + nz + 1]
        for k, (src, dst, dev) in enumerate(route(ins, zone_refs)):
            cp = pltpu.make_async_remote_copy(src_ref=src, dst_ref=dst, send_sem=send_ref.at[k], recv_sem=recv_ref.at[k],
                                              device_id=dev, device_id_type=MESH)
            cp.wait_send()
            cp.wait_recv()

    res = pl.pallas_call(
        body, name=name,
        in_specs=[HBM_SPEC] * (n + nz) + [SEM_SPEC, SEM_SPEC] + _any_specs(len(after)),
        out_specs=[HBM_SPEC] * (n + nz),
        out_shape=[pltpu.HBM(a.shape, a.dtype) for a in srcs + lands],
        input_output_aliases={i: i for i in range(n + nz)},
        compiler_params=pltpu.CompilerParams(has_side_effects=ORDERED_EFFECT),
    )(*srcs, *lands, send, recv, *after)
    return list(res[:n]), list(res[n:])


def _spread(v):
    rows, cols = v.shape
    tr = _row_tile(rows, cols, budget=256 * 1024)

    def body(v_ref, o_ref):
        o_ref[...] = jnp.broadcast_to(v_ref[...][None], o_ref.shape)

    return pl.pallas_call(body, name="spread_small_grads", grid=(rows // tr,),
                          in_specs=[pl.BlockSpec((tr, cols), lambda r: (r, 0))],
                          out_specs=pl.BlockSpec((8, tr, cols), lambda r: (0, r, 0)),
                          out_shape=jax.ShapeDtypeStruct((8, rows, cols), v.dtype),
                          compiler_params=_params(("parallel",)))(v)


def _row_tile(rows, cols, itemsize=4, budget=2 * 1024 * 1024, step=8):
    best = None
    for t in range(step, rows + 1, step):
        if rows % t == 0 and t * cols * itemsize <= budget:
            best = t
    return best if best is not None else rows


def _my_chip():
    return 2 * lax.axis_index("x") + lax.axis_index("y")


def _pair_sum(g5, gsib):
    _, _, rh, cols = g5.shape
    tr = _row_tile(rh, cols, step=16)

    def body(a_ref, b_ref, o_ref):
        o_ref[...] = (a_ref[...].astype(F32) + b_ref[...].astype(F32)).astype(o_ref.dtype)

    return pl.pallas_call(body, name="grad_pair_sum", grid=(N_CHIPS, rh // tr),
                          in_specs=[pl.BlockSpec((None, None, tr, cols), lambda j, r: (j, lax.axis_index("c"), r, 0)),
                                    pl.BlockSpec((None, tr, cols), lambda j, r: (j, r, 0))],
                          out_specs=pl.BlockSpec((None, tr, cols), lambda j, r: (j, r, 0)),
                          out_shape=jax.ShapeDtypeStruct((N_CHIPS, rh, cols), BF16),
                          compiler_params=_params(("parallel", "parallel")))(g5, gsib)


def _chip_sum(part, recv):
    _, rh, cols = part.shape
    tr = _row_tile(rh, cols, step=16)
    n, depth = rh // tr, 3

    def body(a_hbm, b_hbm, o_hbm, a_buf, b_buf, o_buf, in_sem, out_sem):
        chip, core = _my_chip(), lax.axis_index("c")

        def fetch(i):
            rows, slot = pl.ds(i * tr, tr), i % depth
            return (pltpu.make_async_copy(a_hbm.at[chip, rows], a_buf.at[slot], in_sem.at[0, slot]),
                    pltpu.make_async_copy(b_hbm.at[:, rows], b_buf.at[slot], in_sem.at[1, slot]))

        def put(i):
            return pltpu.make_async_copy(o_buf.at[i % 2], o_hbm.at[core, pl.ds(i * tr, tr)], out_sem.at[i % 2])

        for i in range(min(depth, n)):
            for cp in fetch(i):
                cp.start()
        for i in range(n):
            for cp in fetch(i):
                cp.wait()
            if i >= 2:
                put(i - 2).wait()
            acc = a_buf[i % depth].astype(F32)
            for k in range(3):
                acc = acc + b_buf[i % depth, k].astype(F32)
            o_buf[i % 2] = acc
            put(i).start()
            if i + depth < n:
                for cp in fetch(i + depth):
                    cp.start()
        for i in range(max(0, n - 2), n):
            put(i).wait()

    return pl.pallas_call(body, name="grad_chip_sum", in_specs=_any_specs(2), out_specs=_any_specs(1)[0],
                          out_shape=jax.ShapeDtypeStruct((2, rh, cols), F32),
                          scratch_shapes=[pltpu.VMEM((depth, tr, cols), part.dtype),
                                          pltpu.VMEM((depth, 3, tr, cols), recv.dtype),
                                          pltpu.VMEM((2, tr, cols), F32),
                                          pltpu.SemaphoreType.DMA((2, depth)), pltpu.SemaphoreType.DMA((2,))],
                          compiler_params=_params(()))(part, recv)


def _sum_devices(g):
    _, rows, cols = g.shape
    tr = _row_tile(rows, cols, budget=256 * 1024)

    def body(g_ref, o_ref):
        acc = g_ref[0]
        for d in range(1, 8):
            acc = acc + g_ref[d]
        o_ref[...] = acc

    return pl.pallas_call(body, name="sum_small_grads", grid=(rows // tr,),
                          in_specs=[pl.BlockSpec((8, tr, cols), lambda r: (0, r, 0))],
                          out_specs=pl.BlockSpec((tr, cols), lambda r: (r, 0)),
                          out_shape=jax.ShapeDtypeStruct((rows, cols), F32),
                          compiler_params=_params(("parallel",)))(g)


def _place_shard(w, layer, dtype, deps=()):
    _, rows, cols = w.shape
    tr = _row_tile(rows, cols)

    def body(i_ref, *rest):
        o_ref = rest[-1]
        o_ref[...] = i_ref[...].astype(o_ref.dtype)

    out = pl.pallas_call(body, name="place_shard", grid=(rows // tr,),
                         in_specs=[pl.BlockSpec((None, tr, cols), lambda r: (layer, r, 0))] + _any_specs(len(deps)),
                         out_specs=pl.BlockSpec((None, tr, cols), lambda r: (_my_chip(), r, 0)),
                         out_shape=jax.ShapeDtypeStruct((N_CHIPS, rows, cols), dtype),
                         compiler_params=_params(("parallel",)))(w, *deps)
    return out.reshape(N_CHIPS, 2, rows // 2, cols)


def _adamw(w, gs, m, v, deps=()):
    n_layers, rows, cols = w.shape
    tr = _row_tile(rows, cols)

    def body(w_ref, m_ref, v_ref, *rest):
        g_refs = rest[:n_layers]
        go_ref, d_ref, mo_ref, vo_ref = rest[-4:]
        gv = g_refs[0][...]
        for layer in range(1, n_layers):
            gv = jnp.where(pl.program_id(0) == layer, g_refs[layer][...], gv)
        d_ref[...], mo_ref[...], vo_ref[...] = _adamw_math(w_ref[...], gv, m_ref[...], v_ref[...])
        go_ref[...] = gv

    spec = pl.BlockSpec((None, tr, cols), lambda layer, r: (layer, r, 0))
    g_specs = [pl.BlockSpec((tr, cols), lambda layer, r, own=own: (jnp.where(layer == own, r, 0), 0))
               for own in range(n_layers)]
    return pl.pallas_call(body, name="adamw", grid=(n_layers, rows // tr),
                          in_specs=[spec] * 3 + g_specs + _any_specs(len(deps)),
                          out_specs=[spec] * 4, out_shape=[jax.ShapeDtypeStruct((n_layers, rows, cols), F32)] * 4,
                          compiler_params=_params(("parallel", "parallel")))(w, m, v, *gs, *deps)


def _pad_rope(w):
    z = jnp.zeros(w.shape[:-1] + (ROPE_HALF,), w.dtype)
    return jnp.concatenate([w[..., :ROPE_HALF], z, w[..., ROPE_HALF:], z], axis=-1)


def _unpad_rope(g):
    return jnp.concatenate([g[..., :ROPE_HALF], g[..., ROPE:ROPE + ROPE_HALF]], axis=-1)


def _unstack_cols(s):
    n, r, cs = s.shape
    return jnp.transpose(s, (1, 0, 2)).reshape(r, n * cs)


def _stack_cols(f):
    r, cfull = f.shape
    return jnp.transpose(f.reshape(r, N_CHIPS, cfull // N_CHIPS), (1, 0, 2))


def _small_shard(norm, conv):
    return jnp.concatenate([jnp.pad(norm, ((0, 15), (0, 0))), jnp.pad(conv, ((0, 13), (0, 0)))], axis=0)


def _flat_rows(a):
    return a.reshape(-1, LANES)


def _pack_small(arrs):
    return jnp.concatenate([_flat_rows(a.astype(F32)) for a in arrs], axis=0)


def _unpack_small(flat, like):
    out, r = [], 0
    for a in like:
        n = a.size // LANES
        out.append(flat[r:r + n].reshape(a.shape))
        r += n
    return out


def kernel(x, positions, e_norm_mix, e_w_in, e_q_norm, e_w_uq, e_kv_norm, e_w_ukv, e_v_norm, e_sgu_w, e_sgu_b, e_mla_out_norm, e_sgu_out_norm, e_w_out, o_norm_mix, o_w_in, o_conv_w, o_w_out, mlp_norm, mlp_w1, mlp_w2, final_norm, loss_target, m_e_norm_mix, m_e_w_in, m_e_q_norm, m_e_w_uq, m_e_kv_norm, m_e_w_ukv, m_e_v_norm, m_e_sgu_w, m_e_sgu_b, m_e_mla_out_norm, m_e_sgu_out_norm, m_e_w_out, m_o_norm_mix, m_o_w_in, m_o_conv_w, m_o_w_out, m_mlp_norm, m_mlp_w1, m_mlp_w2, m_final_norm, v_e_norm_mix, v_e_w_in, v_e_q_norm, v_e_w_uq, v_e_kv_norm, v_e_w_ukv, v_e_v_norm, v_e_sgu_w, v_e_sgu_b, v_e_mla_out_norm, v_e_sgu_out_norm, v_e_w_out, v_o_norm_mix, v_o_w_in, v_o_conv_w, v_o_w_out, v_mlp_norm, v_mlp_w1, v_mlp_w2, v_final_norm):
    t, d = x.shape[1], x.shape[2]
    ql, kvl = e_q_norm.shape[1], e_kv_norm.shape[1]
    groups = e_v_norm.shape[1]
    gw = groups * LANES
    heads = N_CHIPS * e_w_uq.shape[2] // (LANES + ROPE)
    hw = heads * LANES
    mix = hw + gw
    ei = N_CHIPS * e_w_in.shape[2]
    cd = N_CHIPS * o_conv_w.shape[2]
    ff = N_CHIPS * mlp_w1.shape[2]
    ffs = ff // N_CHIPS
    pi = 2 * gw + ql + kvl + LANES
    assert e_norm_mix.shape[0] == 1 and o_norm_mix.shape[0] == 1 and mlp_norm.shape[0] == 2
    assert ei == ql + kvl + ROPE + 2 * gw and cd == d and e_sgu_w.shape[2] == LANES
    assert (2 * gw) % ql == 0 and (2 * gw + ql) % kvl == 0 and t % LANES == 0
    scale = (LANES + ROPE) ** -0.5

    tr = min(256, t)
    tm = _pick(t, 1024, 8)
    kt, kd = _pick(t, 2048, 8), _pick(d, 2048)
    xs = x.reshape(t, d)
    tgt = loss_target.reshape(t, d)

    small_shard = _small_shard(o_norm_mix, o_conv_w[0])
    first, tok = _gather_start("gather_start_e", [
        [_place_shard(e_w_in, 0, BF16)],
        [_place_shard(e_w_uq, 0, BF16), _place_shard(e_w_ukv, 0, BF16), _place_shard(e_w_out, 0, BF16),
         _place_shard(small_shard[None], 0, F32)]])
    rest, tok = _gather_start("gather_start_rest", [
        [_place_shard(mlp_w1, 0, BF16, (tok,))], [_place_shard(mlp_w2, 0, BF16, (tok,))],
        [_place_shard(o_w_in, 0, BF16, (tok,)), _place_shard(o_w_out, 0, BF16, (tok,))],
        [_place_shard(mlp_w1, 1, BF16, (tok,))], [_place_shard(mlp_w2, 1, BF16, (tok,))]])
    started = first + rest

    def gathered(gi, tag, after):
        send, recv, bufs = started[gi]
        bufs = _gather_forward(tag, _gather_wait(tag, send, recv, bufs, after))
        return [b.reshape(N_CHIPS, 2 * b.shape[2], b.shape[3]) for b in bufs]

    g_e = e_norm_mix
    h0 = _norm_fwd("e_norm", xs, g_e, tr)
    inv_freq = ROPE_BASE ** (-jnp.arange(0, ROPE, 2, dtype=F32) / ROPE)
    zeros32 = jnp.zeros((ROPE_HALF,), F32)
    ones32 = jnp.ones((ROPE_HALF,), F32)
    invf = jnp.concatenate([inv_freq, zeros32, inv_freq, zeros32]).reshape(1, LANES)
    cmask = jnp.concatenate([ones32, zeros32, ones32, zeros32]).reshape(1, LANES)
    smask = jnp.concatenate([-ones32, zeros32, ones32, zeros32]).reshape(1, LANES)
    ctab, stab = _rope_tables(positions.reshape(t, 1).astype(F32), invf, cmask, smask, tr)

    w_in_g, = gathered(0, "e_in", (h0, ctab, tok))
    full = _unstack_cols(w_in_g)
    c2, c3 = ql + kvl, ql + kvl + ROPE
    w_in_all = jnp.concatenate([full[:, c3:], full[:, :c2], _pad_rope(full[:, c2:c3])], axis=1)
    proj, = _matmul("e_proj", Mat(h0, t, d), Mat(w_in_all, d, pi), "nn", [_out(t, pi, F32)], tm, _pick(pi, 1024), kd)

    w_uq_g, w_ukv_g, w_eout_g, small_g = gathered(1, "e", proj)
    full = _unstack_cols(w_uq_g).reshape(ql, heads, LANES + ROPE)
    w_q_all = jnp.concatenate([full[:, :, :LANES].reshape(ql, hw), _pad_rope(full[:, :, LANES:]).reshape(ql, hw)], axis=1)
    full = _unstack_cols(w_ukv_g).reshape(kvl, heads, 2 * LANES)
    w_kv_all = jnp.concatenate([full[:, :, :LANES].reshape(kvl, hw), full[:, :, LANES:].reshape(kvl, hw)], axis=1)
    w_eout = w_eout_g.reshape(mix, d)
    g_o = small_g[:, 0].reshape(1, d)
    conv_w = jnp.pad(jnp.transpose(small_g[:, 16:19], (1, 0, 2)).reshape(3, cd), ((0, 5), (0, 0)))

    g_q, g_kv = e_q_norm, e_kv_norm
    g_vn = e_v_norm.reshape(1, gw)
    sgu_w = e_sgu_w[0]
    sgu_b = jnp.broadcast_to(e_sgu_b[0][:, :, None], (groups, LANES, LANES))
    g_mla, g_sgu = e_mla_out_norm, e_sgu_out_norm
    g_m0, g_m1 = mlp_norm[0:1], mlp_norm[1:2]
    g_f = final_norm.reshape(1, d)

    def mlp_fwd(tag, xin, g, gi):
        hm = _norm_fwd("mlp_norm_" + tag, xin, g, tr)
        tn = _pick(ffs, 1024)
        w1 = Mat(gathered(gi, "w1_" + tag, hm)[0], d, ff, "colstack")
        a, act = _matmul("mlp_up_" + tag, Mat(hm, t, d), w1, "nn",
                         [_out(t, ff, BF16), _out(t, ff, BF16)], tm, tn, kd,
                         epilogue=lambda z: (jnp.maximum(z, 0.0), jnp.square(jnp.maximum(z, 0.0))))
        w2 = Mat(gathered(gi + 1, "w2_" + tag, act)[0].reshape(ff, d), ff, d)
        xo, = _matmul("mlp_down_" + tag, Mat(act, t, ff), w2, "nn",
                      [_out(t, d, F32)], tm, _pick(d, 1024), _pick(ffs, 2048),
                      epilogue=lambda z, r: (z + r,), extras=[Mat(xin, t, d)])
        return xo, hm, a, act, w1, w2

    def chip_start(tag, part):
        return _exchange_start("scatter_start_" + tag, _chip_route, 3 * len(part), part, [(3,) + p.shape[1:] for p in part])

    def pair_start(tag, stacked):
        g5 = [g.reshape(N_CHIPS, 2, g.shape[1] // 2, g.shape[2]) for g in stacked]
        return _exchange_start("pair_start_" + tag, _pair_route, N_CHIPS * len(g5), g5,
                               [(N_CHIPS,) + g.shape[2:] for g in g5])

    def pair_finish(tag, started, after):
        g5, from_sib = _exchange_wait("pair_wait_" + tag, _pair_route, started, after)
        return chip_start(tag, [_pair_sum(a, b) for a, b in zip(g5, from_sib)])

    def summed(tag, sc, after):
        part, lands = _exchange_wait("scatter_wait_" + tag, _chip_route, sc, after)
        half = [_chip_sum(p, r) for p, r in zip(part, lands)]
        return _exchange_start("share_start_" + tag, _share_route, len(half), half, [])

    def shared(tag, started, after):
        bufs, _ = _exchange_wait("share_wait_" + tag, _share_route, started, after)
        return [r.reshape(2 * r.shape[1], r.shape[2]) for r in bufs]

    def mlp_bwd(tag, dx, dxb, xin, g, w1, w2, hm, a, act, deps):
        tn = _pick(ffs, 1024)
        hr, hd = ffs // 2, d // 2
        dz, = _matmul("mlp_dact_" + tag, Mat(dxb, t, d), w2, "nt",
                      [_out(t, ff, BF16)], tm, tn, kd,
                      epilogue=lambda z, av: (z * (2.0 * av.astype(F32)),), extras=[Mat(a, t, ff)], deps=deps)

        def half(own):
            c = lax.axis_index("c")
            return c if own else 1 - c

        def act_half(own):
            return Mat(act, t, ff // 2, cmap=lambda cb, bc: (cb // (hr // bc)) * (ffs // bc) + half(own) * (hr // bc)
                       + cb % (hr // bc))

        def hm_half(own):
            return Mat(hm, t, hd, cmap=lambda cb, bc: cb + half(own) * (hd // bc))

        w1_out = lambda: _out(hd, ff, BF16, "colstack", (), (N_CHIPS, hd, ffs))
        theirs2, = _matmul("mlp_dw2_theirs_" + tag, act_half(False), Mat(dxb, t, d), "tn",
                           [_out(ff // 2, d, BF16)], _pick(hr, 1024), _pick(d, 2048), kt)
        theirs1, = _matmul("mlp_dw1_theirs_" + tag, hm_half(False), Mat(dz, t, ff), "tn",
                           [w1_out()], _pick(hd, 2048), tn, kt)
        sent = [theirs1, theirs2.reshape(N_CHIPS, hr, d)]
        started, tok = _exchange_start("pair_start_m" + tag, _slab_route, N_CHIPS * 2, sent, [s.shape for s in sent])
        dhm, = _matmul("mlp_dh_" + tag, Mat(dz, t, ff), w1, "nt",
                       [_out(t, d, F32)], tm, _pick(d, 1024), _pick(ffs, 2048), deps=(tok,))
        dxo, dxob, dg = _norm_bwd("mlp_norm_bwd_" + tag, dhm, xin, g, dx, tr)
        _, (sib1, sib2) = _exchange_wait("pair_wait_m" + tag, _slab_route, started, dxo)
        add = lambda z, s: (z + s.astype(F32),)
        part2, = _matmul("mlp_dw2_mine_" + tag, act_half(True), Mat(dxb, t, d), "tn",
                         [_out(ff // 2, d, BF16)], _pick(hr, 1024), _pick(d, 2048), kt,
                         epilogue=add, extras=[Mat(sib2.reshape(ff // 2, d), ff // 2, d)])
        part1, = _matmul("mlp_dw1_mine_" + tag, hm_half(True), Mat(dz, t, ff), "tn",
                         [w1_out()], _pick(hd, 2048), tn, kt, epilogue=add, extras=[Mat(sib1, hd, ff, "colstack")])
        sc, tok = chip_start("m" + tag, [part1, part2.reshape(N_CHIPS, hr, d)])
        return dxo, dxob, dg, sc, tok

    cq_cb, ckv_cb, kr_cb = 2 * gw // ql, (2 * gw + ql) // kvl, (2 * gw + ql + kvl) // LANES
    qn, kvn = _rowwise("qkv_norm", lambda a, b, ga, gb: (_rms(a, ga), _rms(b, gb)), t // tr,
                       [_rt(proj, tr, ql, cq_cb), _rt(proj, tr, kvl, ckv_cb), _whole(g_q), _whole(g_kv)],
                       [_rt_out(t, ql, BF16, tr), _rt_out(t, kvl, BF16, tr)])
    qfull, = _matmul("q_up", Mat(qn, t, ql), Mat(w_q_all, ql, 2 * hw), "nn", [_out(t, 2 * hw, F32)], tm, _pick(2 * hw, 1024), ql)
    kvall, = _matmul("kv_up", Mat(kvn, t, kvl), Mat(w_kv_all, kvl, 2 * hw), "nn", [_out(t, 2 * hw, BF16)], tm, _pick(2 * hw, 1024), kvl)
    qall, kr = _rope_fwd(qfull, proj, kr_cb, ctab, stab, heads, tr)
    att, lse_row = _attn_fwd(qall, kvall, kr, heads, scale, tr)
    rb = min(2 * LANES, t)
    sgu = _sgu_fwd(proj, g_vn, sgu_w, sgu_b, groups, rb)
    mixed = _rowwise("mix_norm", lambda a, s, ga, gs: jnp.concatenate([_rms(a, ga), _rms(s, gs)], axis=1), t // tr,
                     [_rt(att, tr), _rt(sgu, tr), _whole(g_mla), _whole(g_sgu)], [_rt_out(t, mix, BF16, tr)])[0]
    x1, = _matmul("e_out", Mat(mixed, t, mix), Mat(w_eout, mix, d), "nn", [_out(t, d, F32)], tm, _pick(d, 1024), _pick(mix, 2048),
                  epilogue=lambda z, r: (z + r,), extras=[Mat(xs, t, d)])
    x2, hm0, a0, act0, w1_0, w2_0 = mlp_fwd("0", x1, g_m0, 2)

    w_oin_g, w_oout_g = gathered(4, "o", x2)
    w_oout = w_oout_g.reshape(cd, d)
    h1 = _norm_fwd("o_norm", x2, g_o, tr)
    oin = Mat(_unstack_cols(w_oin_g), d, 3 * cd)
    tn_o = _pick(_gcd(3 * cd // N_CHIPS, cd), 512)
    proj3, = _matmul("o_proj", Mat(h1, t, d), oin, "nn", [_out(t, 3 * cd, F32, "colstack", (), (3, t, cd))],
                     tm, _pick(cd, 1024), kd)
    tc = _pick(cd, 256)
    bz = _conv_fwd(proj3, conv_w, tc)
    x3, = _matmul("o_out", Mat(bz, t, cd), Mat(w_oout, cd, d), "nn", [_out(t, d, F32)], tm, _pick(d, 1024), _pick(cd, 2048),
                  epilogue=lambda z, r: (z + r,), extras=[Mat(x2, t, d)])
    x4, hm1, a1, act1, w1_1, w2_1 = mlp_fwd("1", x3, g_m1, 5)

    def final_fn(xv, gv, tv):
        r = lax.rsqrt(jnp.mean(xv * xv, axis=-1, keepdims=True) + EPS)
        xh = xv * r
        err = xh * gv - tv
        dy = err * (1.0 / d)
        dxh = dy * gv
        dx = r * (dxh - xh * jnp.mean(dxh * xh, axis=-1, keepdims=True))
        sq = jnp.sum(err * err, axis=0, keepdims=True)
        part = sq[:, :LANES]
        for k in range(1, d // LANES):
            part = part + sq[:, k * LANES:(k + 1) * LANES]
        return dx, dx, part, jnp.sum(dy * xh, axis=0, keepdims=True)

    dx4, dx4b, loss_vec, dg_f = _rowwise("loss_final_norm", final_fn, t // tr, [_rt(x4, tr), _whole(g_f), _rt(tgt, tr)],
                                         [_rt_out(t, d, F32, tr), _rt_out(t, d, BF16, tr)],
                                         [jax.ShapeDtypeStruct((1, LANES), F32), jax.ShapeDtypeStruct((1, d), F32)])

    dx3, dx3b, dg_m1, sc_m1, tok = mlp_bwd("1", dx4, dx4b, x3, g_m1, w1_1, w2_1, hm1, a1, act1, ())

    dbz, = _matmul("o_out_dx", Mat(dx3b, t, d), Mat(w_oout, cd, d), "nt", [_out(t, cd, F32)], tm, _pick(cd, 1024), kd,
                   deps=(tok,))
    dw_oout, = _matmul("o_out_dw", Mat(bz, t, cd), Mat(dx3b, t, d), "tn", [_out(cd, d, BF16)], _pick(cd, 1024), _pick(d, 1024), kt)
    dproj3, dconv = _conv_bwd(proj3, conv_w, dbz, tc)
    dp3 = Mat(dproj3, t, 3 * cd, "colstack")
    dw_oin, = _matmul("o_proj_dw", Mat(h1, t, d), dp3, "tn", [_out(d, 3 * cd, BF16, "colstack", (), (N_CHIPS, d, 3 * cd // N_CHIPS))],
                      _pick(d, 2048), tn_o, kt)
    started_o, tok = pair_start("o", [dw_oin, dw_oout.reshape(N_CHIPS, cd // N_CHIPS, d)])
    dh1, = _matmul("o_proj_dx", dp3, oin, "nt", [_out(t, d, F32)], tm, _pick(d, 1024), _pick(cd, 2048), deps=(tok,))
    dx2, dx2b, dg_o = _norm_bwd("o_norm_bwd", dh1, x2, g_o, dx3, tr)
    sc_o, tok = pair_finish("o", started_o, dx2)

    dconv_s = jnp.transpose(dconv[:3].reshape(3, N_CHIPS, cd // N_CHIPS), (1, 0, 2))
    gsmall = jnp.concatenate([jnp.pad(dg_o.reshape(N_CHIPS, 1, d // N_CHIPS), ((0, 0), (0, 15), (0, 0))),
                              jnp.pad(dconv_s, ((0, 0), (0, 13), (0, 0)))], axis=1)
    dx1, dx1b, dg_m0, sc_m0, tok = mlp_bwd("0", dx2, dx2b, x1, g_m0, w1_0, w2_0, hm0, a0, act0, (tok,))

    dmixed, = _matmul("e_out_dx", Mat(dx1b, t, d), Mat(w_eout, mix, d), "nt", [_out(t, mix, F32)], tm, _pick(mix, 1024), kd,
                      deps=(tok,))
    dw_eout, = _matmul("e_out_dw", Mat(mixed, t, mix), Mat(dx1b, t, d), "tn", [_out(mix, d, BF16)], _pick(mix, 1024), _pick(d, 1024), kt)

    def mixb_fn(dm, a, s, ga, gs):
        da, dga = _rms_bwd(dm[:, :hw], a, ga)
        dsg, dgs = _rms_bwd(dm[:, hw:], s, gs)
        prod = da * a
        cols = [jnp.broadcast_to(jnp.sum(prod[:, h * LANES:(h + 1) * LANES], axis=-1, keepdims=True), (tr, LANES))
                for h in range(heads)]
        return da, dsg, jnp.stack([_row_of(c) for c in cols], axis=0), dga, dgs

    da_b, dsgu, delta_row, dg_mla, dg_sgu = _rowwise(
        "mix_norm_bwd", mixb_fn, t // tr, [_rt(dmixed, tr), _rt(att, tr), _rt(sgu, tr), _whole(g_mla), _whole(g_sgu)],
        [_rt_out(t, hw, BF16, tr), _rt_out(t, gw, F32, tr),
         (jax.ShapeDtypeStruct((heads, 8, t), F32), pl.BlockSpec((heads, 8, tr), lambda i: (0, 0, i)))],
        [jax.ShapeDtypeStruct((1, hw), F32), jax.ShapeDtypeStruct((1, gw), F32)])

    dproj, dsgu_w, dsgu_b8, dg_vn = _sgu_bwd(proj, dsgu, g_vn, sgu_w, sgu_b, groups, rb)
    dq1, dq2, dk1, dvv, dkr_h = _attn_bwd(qall, kvall, kr, da_b, lse_row, delta_row, heads, scale, min(2 * tr, t))
    dqfull, dproj = _rope_bwd(dq1, dq2, dkr_h, ctab, stab, heads, tr, dproj, kr_cb)
    dkvall = jnp.concatenate([dk1, dvv], axis=1)
    dw_q, = _matmul("q_up_dw", Mat(qn, t, ql), Mat(dqfull, t, 2 * hw), "tn", [_out(ql, 2 * hw, BF16)], ql, _pick(2 * hw, 1024), kt)
    dqn, = _matmul("q_up_dx", Mat(dqfull, t, 2 * hw), Mat(w_q_all, ql, 2 * hw), "nt", [_out(t, ql, F32)], tm, ql, _pick(2 * hw, 2048))
    dw_kv, = _matmul("kv_up_dw", Mat(kvn, t, kvl), Mat(dkvall, t, 2 * hw), "tn", [_out(kvl, 2 * hw, BF16)], kvl, _pick(2 * hw, 1024), kt)
    dkvn, = _matmul("kv_up_dx", Mat(dkvall, t, 2 * hw), Mat(w_kv_all, kvl, 2 * hw), "nt", [_out(t, kvl, F32)], tm, kvl, _pick(2 * hw, 2048))

    def qkvb_fn(da, db, a, b, ga, gb):
        dxa, dga = _rms_bwd(da, a, ga)
        dxb, dgb = _rms_bwd(db, b, gb)
        return jnp.concatenate([dxa, dxb], axis=1), dga, dgb

    assert (2 * gw) % (ql + kvl) == 0
    into = (jax.ShapeDtypeStruct(dproj.shape, dproj.dtype),
            pl.BlockSpec((tr, ql + kvl), lambda i: (i, 2 * gw // (ql + kvl))))
    dproj, dg_q, dg_kv = _rowwise(
        "qkv_norm_bwd", qkvb_fn, t // tr,
        [_rt(dqn, tr), _rt(dkvn, tr), _rt(proj, tr, ql, cq_cb), _rt(proj, tr, kvl, ckv_cb), _whole(g_q), _whole(g_kv)],
        [into], [jax.ShapeDtypeStruct((1, ql), F32), jax.ShapeDtypeStruct((1, kvl), F32)], deps=(dproj,), fill=(0, 0))
    dw_in, = _matmul("e_proj_dw", Mat(dproj, t, pi), Mat(h0, t, d), "tn", [_out(pi, d, F32)], _pick(pi, 1024), _pick(d, 2048), kt)
    dh0, = _matmul("e_proj_dx", Mat(dproj, t, pi), Mat(w_in_all, d, pi), "nt", [_out(t, d, F32)], tm, _pick(d, 1024), _pick(pi, 4096))
    dx0, _, dg_e = _norm_bwd("e_norm_bwd", dh0, xs, g_e, dx1, tr)

    kr0 = 2 * gw + c2
    gw_in = jnp.concatenate([dw_in[2 * gw:kr0], dw_in[kr0:kr0 + ROPE_HALF], dw_in[kr0 + ROPE:kr0 + ROPE + ROPE_HALF],
                             dw_in[:2 * gw]], axis=0).reshape(N_CHIPS, ei // N_CHIPS, d)
    gq = jnp.concatenate([dw_q[:, :hw].reshape(ql, heads, LANES), _unpad_rope(dw_q[:, hw:].reshape(ql, heads, LANES))], axis=-1)
    gw_uq = _stack_cols(gq.reshape(ql, heads * (LANES + ROPE)))
    gkv = jnp.concatenate([dw_kv[:, :hw].reshape(kvl, heads, LANES), dw_kv[:, hw:].reshape(kvl, heads, LANES)], axis=-1)
    gw_ukv = _stack_cols(gkv.reshape(kvl, heads * 2 * LANES))
    started_e, tok_pair = pair_start("e", [gw_in, gw_uq, gw_ukv, dw_eout.reshape(N_CHIPS, mix // N_CHIPS, d), gsmall])

    small_like = [e_norm_mix, e_q_norm, e_kv_norm, e_v_norm, e_sgu_w, e_sgu_b, e_mla_out_norm, e_sgu_out_norm, mlp_norm, final_norm]
    small_grads = [dg_e, dg_q, dg_kv, dg_vn, dsgu_w, dsgu_b8[:, 0, :], dg_mla, dg_sgu, jnp.concatenate([dg_m0, dg_m1], axis=0), dg_f]
    packed = _pack_small(small_grads)
    n_small = packed.shape[0] + (-packed.shape[0]) % 8
    pad = n_small - packed.shape[0] + 8
    sflat = jnp.concatenate([jnp.pad(packed, ((0, pad - 8), (0, 0))), jnp.pad(loss_vec, ((0, 7), (0, 0)))], axis=0)
    small_started, tok_small = _exchange_start("small_start", _all_route, 7, [sflat], [_spread(sflat)])

    sh_m1, tok = summed("m1", sc_m1, (tok_pair, tok_small))
    sc_e, tok = pair_finish("e", started_e, tok)
    sh_o, tok = summed("o", sc_o, tok)
    sh_m0, tok = summed("m0", sc_m0, tok)
    r_oin, r_oout = shared("o", sh_o, tok)
    late = {"o_w_in": _adamw(o_w_in, [r_oin], m_o_w_in, v_o_w_in),
            "o_w_out": _adamw(o_w_out, [r_oout], m_o_w_out, v_o_w_out)}
    r_w1_1, r_w2_1 = shared("m1", sh_m1, late["o_w_in"][1])
    r_w1_0, r_w2_0 = shared("m0", sh_m0, r_w2_1)
    late["mlp_w1"] = _adamw(mlp_w1, [r_w1_0, r_w1_1], m_mlp_w1, v_mlp_w1)
    sh_e, tok = summed("e", sc_e, late["mlp_w1"][1])
    late["mlp_w2"] = _adamw(mlp_w2, [r_w2_0, r_w2_1], m_mlp_w2, v_mlp_w2, deps=[tok])

    _, (all_small,) = _exchange_wait("small_wait", _all_route, small_started, late["mlp_w2"][1])
    g_small = _sum_devices(all_small)
    loss = 0.5 * jnp.sum(g_small[n_small]) / d

    def padded(arrs):
        return jnp.pad(_pack_small(arrs), ((0, pad), (0, 0)))

    s_m = [m_e_norm_mix, m_e_q_norm, m_e_kv_norm, m_e_v_norm, m_e_sgu_w, m_e_sgu_b, m_e_mla_out_norm, m_e_sgu_out_norm, m_mlp_norm, m_final_norm]
    s_v = [v_e_norm_mix, v_e_q_norm, v_e_kv_norm, v_e_v_norm, v_e_sgu_w, v_e_sgu_b, v_e_mla_out_norm, v_e_sgu_out_norm, v_mlp_norm, v_final_norm]
    s_out = [_unpack_small(o[0], small_like)
             for o in _adamw(padded(small_like)[None], [g_small], padded(s_m)[None], padded(s_v)[None])]

    r_in, r_uq, r_ukv, r_eout, r_small = shared("e", sh_e, (tok, late["mlp_w2"][1]))
    sm = [o[0] for o in _adamw(small_shard[None], [r_small], _small_shard(m_o_norm_mix, m_o_conv_w[0])[None],
                               _small_shard(v_o_norm_mix, v_o_conv_w[0])[None])]
    big = dict(late)
    flip = lambda a: jnp.swapaxes(a, 1, 2)
    big.update({
        "e_w_in": [flip(o) for o in _adamw(flip(e_w_in), [r_in], flip(m_e_w_in), flip(v_e_w_in))],
        "e_w_uq": _adamw(e_w_uq, [r_uq], m_e_w_uq, v_e_w_uq),
        "e_w_ukv": _adamw(e_w_ukv, [r_ukv], m_e_w_ukv, v_e_w_ukv),
        "e_w_out": _adamw(e_w_out, [r_eout], m_e_w_out, v_e_w_out),
    })

    names = ["e_norm_mix", "e_w_in", "e_q_norm", "e_w_uq", "e_kv_norm", "e_w_ukv", "e_v_norm", "e_sgu_w", "e_sgu_b",
             "e_mla_out_norm", "e_sgu_out_norm", "e_w_out", "o_norm_mix", "o_w_in", "o_conv_w", "o_w_out",
             "mlp_norm", "mlp_w1", "mlp_w2", "final_norm"]
    shapes = {"e_w_in": e_w_in.shape, "e_w_uq": e_w_uq.shape, "e_w_ukv": e_w_ukv.shape, "e_w_out": e_w_out.shape,
              "o_w_in": o_w_in.shape, "o_w_out": o_w_out.shape, "mlp_w1": mlp_w1.shape, "mlp_w2": mlp_w2.shape}
    small_names = ["e_norm_mix", "e_q_norm", "e_kv_norm", "e_v_norm", "e_sgu_w", "e_sgu_b", "e_mla_out_norm",
                   "e_sgu_out_norm", "mlp_norm", "final_norm"]

    def leaf(kind, name):
        if name in big:
            return big[name][kind].reshape(shapes[name])
        if name == "o_norm_mix":
            return sm[kind][0:1]
        if name == "o_conv_w":
            return sm[kind][16:19].reshape(o_conv_w.shape)
        return s_out[kind][small_names.index(name)]

    outs = [loss, dx0.reshape(x.shape)]
    for kind in range(4):
        outs += [leaf(kind, nm) for nm in names]
    return tuple(outs)


def _gcd(a, b):
    while b:
        a, b = b, a % b
    return a
```
